```python
import jax, jax.numpy as jnp
from jax import lax
import numpy as np

D_MODEL = 1024
BATCH = 8
SEQ = 2048
DEPTH = 1

GRID_W = 64
CTX_LEN = 256
N_HEADS = 8
QK_NOPE_DIM = 64
QK_ROPE_DIM = 32
V_HEAD_DIM = 64
Q_LORA_RANK = 384
KV_LORA_RANK = 256
ROPE_THETA = 10000.0
CONV_DIM = 512
CONV_WIDTH = 3
D_FF = 2816
Q_BLOCK = 128
EPS = 1e-6
ATTN_DIM = N_HEADS * V_HEAD_DIM
KV_END = KV_LORA_RANK
KR_END = KV_END + QK_ROPE_DIM
Q_END = KR_END + Q_LORA_RANK
CX_END = Q_END + CONV_DIM
CB_END = CX_END + CONV_DIM
CC_END = CB_END + CONV_DIM
GA_END = CC_END + D_MODEL
GC_END = GA_END + D_MODEL
IN_COLS = GC_END

kernel_name = 'hybrid_mla_shortconv_convffn_dit_block'


def rmsnorm(x, g):
    xf = x.astype(jnp.float32)
    y = xf * lax.rsqrt(jnp.mean(xf * xf, axis=-1, keepdims=True) + EPS)
    return (y * g.astype(jnp.float32)).astype(x.dtype)


def modulate(h, shift, scale):
    return h * (1 + scale[:, None, :]) + shift[:, None, :]


def dwconv(x, w, b):
    T = x.shape[1]
    pad = CONV_WIDTH // 2
    xp = jnp.pad(x, ((0, 0), (pad, CONV_WIDTH - 1 - pad), (0, 0)))
    y = b
    for k in range(CONV_WIDTH):
        y = y + xp[:, k:k + T] * w[k]
    return y


def axial_angles(rows):
    row = jnp.repeat(jnp.arange(rows), GRID_W).astype(jnp.float32)
    col = jnp.tile(jnp.arange(GRID_W), rows).astype(jnp.float32)
    axis_dim = QK_ROPE_DIM // 2
    inv = ROPE_THETA ** (-jnp.arange(0, axis_dim, 2, dtype=jnp.float32) / axis_dim)
    return jnp.concatenate([row[:, None] * inv, col[:, None] * inv], axis=-1)


def rope2d(x, ang):
    nf = QK_ROPE_DIM // 4
    xr = x.reshape(*x.shape[:-1], 2, 2, nf)
    x1, x2 = xr[..., 0, :], xr[..., 1, :]
    a = ang.reshape(ang.shape[0], 1, 2, nf)
    cos, sin = jnp.cos(a).astype(x.dtype), jnp.sin(a).astype(x.dtype)
    out = jnp.stack([x1 * cos - x2 * sin, x1 * sin + x2 * cos], axis=-2)
    return out.reshape(x.shape)


def mla_kv(p, kv_norm_g, w_ukv, ang):
    B, T = p.shape[:2]
    ckv = rmsnorm(p[..., :KV_END], kv_norm_g)
    kv = (ckv @ w_ukv).reshape(B, T, N_HEADS, QK_NOPE_DIM + V_HEAD_DIM)
    k_nope, v = kv[..., :QK_NOPE_DIM], kv[..., QK_NOPE_DIM:]
    k_rope = p[..., KV_END:KR_END][:, :, None, :]
    if ang is not None:
        k_rope = rope2d(k_rope, ang)
    return k_nope, k_rope[:, :, 0], v


def mla_query(p, q_norm_g, w_uq, ang):
    B, T = p.shape[:2]
    cq = rmsnorm(p[..., KR_END:Q_END], q_norm_g)
    q = (cq @ w_uq).reshape(B, T, N_HEADS, QK_NOPE_DIM + QK_ROPE_DIM)
    q_nope, q_rope = q[..., :QK_NOPE_DIM], q[..., QK_NOPE_DIM:]
    if ang is not None:
        q_rope = rope2d(q_rope, ang)
    return q_nope, q_rope


def mla_attend(q_nope, q_rope, k_nope, k_rope, v):
    scale = (QK_NOPE_DIM + QK_ROPE_DIM) ** -0.5
    s = (jnp.einsum('bqhd,bkhd->bhqk', q_nope, k_nope)
         + jnp.einsum('bqhr,bkr->bhqk', q_rope, k_rope))
    pr = jax.nn.softmax(s.astype(jnp.float32) * scale, axis=-1).astype(v.dtype)
    return jnp.einsum('bhqk,bkhd->bqhd', pr, v)


def mla_latent(q_nope, q_rope, k_nope, k_rope, v):
    B, T = q_nope.shape[:2]
    nb = T // Q_BLOCK

    def blocks(a):
        return a.reshape(B, nb, Q_BLOCK, *a.shape[2:]).swapaxes(0, 1)

    o = lax.map(lambda qs: mla_attend(qs[0], qs[1], k_nope, k_rope, v), (blocks(q_nope), blocks(q_rope)))
    return o.swapaxes(0, 1).reshape(B, T, ATTN_DIM)


def gated_merge(p, o_attn, conv_w, conv_b, w_attn_out, w_conv_out, w_o):
    x_in = p[..., Q_END:CX_END]
    b_gate = p[..., CX_END:CB_END]
    c_gate = p[..., CB_END:CC_END]
    y_conv = (b_gate * dwconv(c_gate * x_in, conv_w, conv_b)) @ w_conv_out
    y_attn = o_attn @ w_attn_out
    g_attn = jax.nn.sigmoid(p[..., CC_END:GA_END])
    g_conv = jax.nn.sigmoid(p[..., GA_END:GC_END])
    return (g_attn * y_attn + g_conv * y_conv) @ w_o


def conv_ffn(h, w_up, conv_w, conv_b, w_down):
    u = dwconv(h @ w_up, conv_w, conv_b)
    gate, val = jnp.split(u, 2, axis=-1)
    return (jax.nn.silu(gate) * val) @ w_down


def trunk_layer(x, ctx, mod_lat, mod_ctx, ang, update_ctx, norm1_g, w_in, q_norm_g, kv_norm_g,
                w_uq, w_ukv, conv_w, conv_b, w_attn_out, w_conv_out, w_o, norm2_g, w_up,
                ffn_conv_w, ffn_conv_b, w_down):
    sh1, sc1, g1, sh2, sc2, g2 = jnp.split(mod_lat, 6, axis=-1)
    csh1, csc1, cg1, csh2, csc2, cg2 = jnp.split(mod_ctx, 6, axis=-1)
    h = modulate(rmsnorm(x, norm1_g), sh1, sc1)
    hc = modulate(rmsnorm(ctx, norm1_g), csh1, csc1)
    p = h @ w_in
    pc = hc @ w_in if update_ctx else hc @ w_in[:, :KR_END]
    kn_c, kr_c, v_c = mla_kv(pc, kv_norm_g, w_ukv, None)
    kn_l, kr_l, v_l = mla_kv(p, kv_norm_g, w_ukv, ang)
    qn, qr = mla_query(p, q_norm_g, w_uq, ang)
    o = mla_latent(qn, qr, jnp.concatenate([kn_c, kn_l], axis=1),
                   jnp.concatenate([kr_c, kr_l], axis=1), jnp.concatenate([v_c, v_l], axis=1))
    x = x + g1[:, None, :] * gated_merge(p, o, conv_w, conv_b, w_attn_out, w_conv_out, w_o)
    x = x + g2[:, None, :] * conv_ffn(modulate(rmsnorm(x, norm2_g), sh2, sc2), w_up, ffn_conv_w, ffn_conv_b, w_down)
    if update_ctx:
        qn_c, qr_c = mla_query(pc, q_norm_g, w_uq, None)
        oc = mla_attend(qn_c, qr_c, kn_c, kr_c, v_c).reshape(ctx.shape[0], ctx.shape[1], ATTN_DIM)
        ctx = ctx + cg1[:, None, :] * gated_merge(pc, oc, conv_w, conv_b, w_attn_out, w_conv_out, w_o)
        ctx = ctx + cg2[:, None, :] * conv_ffn(modulate(rmsnorm(ctx, norm2_g), csh2, csc2), w_up, ffn_conv_w, ffn_conv_b, w_down)
    return x, ctx


def _fwd_setup_inputs(seed: int = 0) -> dict:
    key = jax.random.key(seed)
    ks = jax.random.split(key, 24)
    L, D = DEPTH, D_MODEL

    def nrm(k, shape, fan_in, gain=1.0):
        return jax.random.normal(k, shape, jnp.float32) * (gain * fan_in ** -0.5)

    def norm_gain(k, shape):
        return 1.0 + 0.05 * jax.random.normal(k, shape, jnp.float32)

    def bias(k, shape):
        return 0.02 * jax.random.normal(k, shape, jnp.float32)

    return {
        'x': jax.random.normal(ks[0], (BATCH, SEQ, D), jnp.float32),
        'c': jax.random.normal(ks[1], (BATCH, D), jnp.float32),
        'ctx': jax.random.normal(ks[2], (BATCH, CTX_LEN, D), jnp.float32),
        'c_ctx': jax.random.normal(ks[3], (D,), jnp.float32),
        'w_ada': nrm(ks[4], (L, D, 6 * D), D, 0.5),
        'b_ada': bias(ks[5], (L, 6 * D)),
        'norm1_g': norm_gain(ks[6], (L, D)),
        'w_in': nrm(ks[7], (L, D, IN_COLS), D),
        'q_norm_g': norm_gain(ks[8], (L, Q_LORA_RANK)),
        'kv_norm_g': norm_gain(ks[9], (L, KV_LORA_RANK)),
        'w_uq': nrm(ks[10], (L, Q_LORA_RANK, N_HEADS * (QK_NOPE_DIM + QK_ROPE_DIM)), Q_LORA_RANK),
        'w_ukv': nrm(ks[11], (L, KV_LORA_RANK, N_HEADS * (QK_NOPE_DIM + V_HEAD_DIM)), KV_LORA_RANK),
        'conv_w': nrm(ks[12], (L, CONV_WIDTH, CONV_DIM), CONV_WIDTH),
        'conv_b': bias(ks[13], (L, CONV_DIM)),
        'w_attn_out': nrm(ks[14], (L, ATTN_DIM, D), ATTN_DIM),
        'w_conv_out': nrm(ks[15], (L, CONV_DIM, D), CONV_DIM),
        'w_o': nrm(ks[16], (L, D, D), D),
        'norm2_g': norm_gain(ks[17], (L, D)),
        'w_up': nrm(ks[18], (L, D, 2 * D_FF), D),
        'ffn_conv_w': nrm(ks[19], (L, CONV_WIDTH, 2 * D_FF), CONV_WIDTH),
        'ffn_conv_b': bias(ks[20], (L, 2 * D_FF)),
        'w_down': nrm(ks[21], (L, D_FF, D), D_FF),
        'final_g': norm_gain(ks[22], (D,)),
    }


def _fwd_reference(x, c, ctx, c_ctx, w_ada, b_ada, norm1_g, w_in, q_norm_g, kv_norm_g, w_uq, w_ukv,
              conv_w, conv_b, w_attn_out, w_conv_out, w_o, norm2_g, w_up, ffn_conv_w, ffn_conv_b,
              w_down, final_g):
    rows = x.shape[1] // GRID_W
    ang = axial_angles(rows)
    sc = jax.nn.silu(c)
    sc_ctx = jax.nn.silu(c_ctx)[None, :]
    for i in range(DEPTH):
        mod_lat = sc @ w_ada[i] + b_ada[i]
        mod_ctx = sc_ctx @ w_ada[i] + b_ada[i]
        x, ctx = trunk_layer(x, ctx, mod_lat, mod_ctx, ang, i < DEPTH - 1, norm1_g[i], w_in[i],
                             q_norm_g[i], kv_norm_g[i], w_uq[i], w_ukv[i], conv_w[i], conv_b[i],
                             w_attn_out[i], w_conv_out[i], w_o[i], norm2_g[i], w_up[i],
                             ffn_conv_w[i], ffn_conv_b[i], w_down[i])
    return rmsnorm(x, final_g)


import jax as _jax
import jax.numpy as _jnp

TWIN_FORMAT = 'train_step'
FWD_PARAMS = ['x', 'c', 'ctx', 'c_ctx', 'w_ada', 'b_ada', 'norm1_g', 'w_in', 'q_norm_g', 'kv_norm_g', 'w_uq', 'w_ukv', 'conv_w', 'conv_b', 'w_attn_out', 'w_conv_out', 'w_o', 'norm2_g', 'w_up', 'ffn_conv_w', 'ffn_conv_b', 'w_down', 'final_g']
TWIN_WEIGHTS = ['c_ctx', 'w_ada', 'b_ada', 'norm1_g', 'w_in', 'q_norm_g', 'kv_norm_g', 'w_uq', 'w_ukv', 'conv_w', 'conv_b', 'w_attn_out', 'w_conv_out', 'w_o', 'norm2_g', 'w_up', 'ffn_conv_w', 'ffn_conv_b', 'w_down', 'final_g']
TWIN_DIFF_INPUT = 'x'
TWIN_INPUTS = ['x', 'c', 'ctx', 'c_ctx', 'w_ada', 'b_ada', 'norm1_g', 'w_in', 'q_norm_g', 'kv_norm_g', 'w_uq', 'w_ukv', 'conv_w', 'conv_b', 'w_attn_out', 'w_conv_out', 'w_o', 'norm2_g', 'w_up', 'ffn_conv_w', 'ffn_conv_b', 'w_down', 'final_g', 'loss_target', 'm_c_ctx', 'm_w_ada', 'm_b_ada', 'm_norm1_g', 'm_w_in', 'm_q_norm_g', 'm_kv_norm_g', 'm_w_uq', 'm_w_ukv', 'm_conv_w', 'm_conv_b', 'm_w_attn_out', 'm_w_conv_out', 'm_w_o', 'm_norm2_g', 'm_w_up', 'm_ffn_conv_w', 'm_ffn_conv_b', 'm_w_down', 'm_final_g', 'v_c_ctx', 'v_w_ada', 'v_b_ada', 'v_norm1_g', 'v_w_in', 'v_q_norm_g', 'v_kv_norm_g', 'v_w_uq', 'v_w_ukv', 'v_conv_w', 'v_conv_b', 'v_w_attn_out', 'v_w_conv_out', 'v_w_o', 'v_norm2_g', 'v_w_up', 'v_ffn_conv_w', 'v_ffn_conv_b', 'v_w_down', 'v_final_g']
TWIN_OUTPUTS = ['loss', 'grad_x', 'grad_c_ctx', 'grad_w_ada', 'grad_b_ada', 'grad_norm1_g', 'grad_w_in', 'grad_q_norm_g', 'grad_kv_norm_g', 'grad_w_uq', 'grad_w_ukv', 'grad_conv_w', 'grad_conv_b', 'grad_w_attn_out', 'grad_w_conv_out', 'grad_w_o', 'grad_norm2_g', 'grad_w_up', 'grad_ffn_conv_w', 'grad_ffn_conv_b', 'grad_w_down', 'grad_final_g', 'delta_c_ctx', 'delta_w_ada', 'delta_b_ada', 'delta_norm1_g', 'delta_w_in', 'delta_q_norm_g', 'delta_kv_norm_g', 'delta_w_uq', 'delta_w_ukv', 'delta_conv_w', 'delta_conv_b', 'delta_w_attn_out', 'delta_w_conv_out', 'delta_w_o', 'delta_norm2_g', 'delta_w_up', 'delta_ffn_conv_w', 'delta_ffn_conv_b', 'delta_w_down', 'delta_final_g', 'new_m_c_ctx', 'new_m_w_ada', 'new_m_b_ada', 'new_m_norm1_g', 'new_m_w_in', 'new_m_q_norm_g', 'new_m_kv_norm_g', 'new_m_w_uq', 'new_m_w_ukv', 'new_m_conv_w', 'new_m_conv_b', 'new_m_w_attn_out', 'new_m_w_conv_out', 'new_m_w_o', 'new_m_norm2_g', 'new_m_w_up', 'new_m_ffn_conv_w', 'new_m_ffn_conv_b', 'new_m_w_down', 'new_m_final_g', 'new_v_c_ctx', 'new_v_w_ada', 'new_v_b_ada', 'new_v_norm1_g', 'new_v_w_in', 'new_v_q_norm_g', 'new_v_kv_norm_g', 'new_v_w_uq', 'new_v_w_ukv', 'new_v_conv_w', 'new_v_conv_b', 'new_v_w_attn_out', 'new_v_w_conv_out', 'new_v_w_o', 'new_v_norm2_g', 'new_v_w_up', 'new_v_ffn_conv_w', 'new_v_ffn_conv_b', 'new_v_w_down', 'new_v_final_g']
TWIN_LEAF_KINDS = {'loss': 'loss', 'grad_x': 'grad_x', 'grad_c_ctx': 'grad_w', 'grad_w_ada': 'grad_w', 'grad_b_ada': 'grad_w', 'grad_norm1_g': 'grad_w', 'grad_w_in': 'grad_w', 'grad_q_norm_g': 'grad_w', 'grad_kv_norm_g': 'grad_w', 'grad_w_uq': 'grad_w', 'grad_w_ukv': 'grad_w', 'grad_conv_w': 'grad_w', 'grad_conv_b': 'grad_w', 'grad_w_attn_out': 'grad_w', 'grad_w_conv_out': 'grad_w', 'grad_w_o': 'grad_w', 'grad_norm2_g': 'grad_w', 'grad_w_up': 'grad_w', 'grad_ffn_conv_w': 'grad_w', 'grad_ffn_conv_b': 'grad_w', 'grad_w_down': 'grad_w', 'grad_final_g': 'grad_w', 'delta_c_ctx': 'delta_w', 'delta_w_ada': 'delta_w', 'delta_b_ada': 'delta_w', 'delta_norm1_g': 'delta_w', 'delta_w_in': 'delta_w', 'delta_q_norm_g': 'delta_w', 'delta_kv_norm_g': 'delta_w', 'delta_w_uq': 'delta_w', 'delta_w_ukv': 'delta_w', 'delta_conv_w': 'delta_w', 'delta_conv_b': 'delta_w', 'delta_w_attn_out': 'delta_w', 'delta_w_conv_out': 'delta_w', 'delta_w_o': 'delta_w', 'delta_norm2_g': 'delta_w', 'delta_w_up': 'delta_w', 'delta_ffn_conv_w': 'delta_w', 'delta_ffn_conv_b': 'delta_w', 'delta_w_down': 'delta_w', 'delta_final_g': 'delta_w', 'new_m_c_ctx': 'new_m', 'new_m_w_ada': 'new_m', 'new_m_b_ada': 'new_m', 'new_m_norm1_g': 'new_m', 'new_m_w_in': 'new_m', 'new_m_q_norm_g': 'new_m', 'new_m_kv_norm_g': 'new_m', 'new_m_w_uq': 'new_m', 'new_m_w_ukv': 'new_m', 'new_m_conv_w': 'new_m', 'new_m_conv_b': 'new_m', 'new_m_w_attn_out': 'new_m', 'new_m_w_conv_out': 'new_m', 'new_m_w_o': 'new_m', 'new_m_norm2_g': 'new_m', 'new_m_w_up': 'new_m', 'new_m_ffn_conv_w': 'new_m', 'new_m_ffn_conv_b': 'new_m', 'new_m_w_down': 'new_m', 'new_m_final_g': 'new_m', 'new_v_c_ctx': 'new_v', 'new_v_w_ada': 'new_v', 'new_v_b_ada': 'new_v', 'new_v_norm1_g': 'new_v', 'new_v_w_in': 'new_v', 'new_v_q_norm_g': 'new_v', 'new_v_kv_norm_g': 'new_v', 'new_v_w_uq': 'new_v', 'new_v_w_ukv': 'new_v', 'new_v_conv_w': 'new_v', 'new_v_conv_b': 'new_v', 'new_v_w_attn_out': 'new_v', 'new_v_w_conv_out': 'new_v', 'new_v_w_o': 'new_v', 'new_v_norm2_g': 'new_v', 'new_v_w_up': 'new_v', 'new_v_ffn_conv_w': 'new_v', 'new_v_ffn_conv_b': 'new_v', 'new_v_w_down': 'new_v', 'new_v_final_g': 'new_v'}


def _forward(args):
    return _fwd_reference(*[args[k] for k in FWD_PARAMS])


def _output_shape():
    out = _jax.eval_shape(lambda: _forward(_fwd_setup_inputs(0)))
    return out.shape, out.dtype

N_MICROBATCH = 1
ADAM_LR = 0.001
ADAM_B1 = 0.9
ADAM_B2 = 0.999
ADAM_EPS = 1e-08
ADAM_WD = 0.01
ADAM_STEP = 10
PER_EXAMPLE_BATCH_AXIS = {'x': 0, 'c': 0, 'ctx': 0, 'loss_target': 0}
SHARED_INPUTS = []
_WEIGHT_DTYPES = {'c_ctx': _jnp.float32, 'w_ada': _jnp.float32, 'b_ada': _jnp.float32, 'norm1_g': _jnp.float32, 'w_in': _jnp.float32, 'q_norm_g': _jnp.float32, 'kv_norm_g': _jnp.float32, 'w_uq': _jnp.float32, 'w_ukv': _jnp.float32, 'conv_w': _jnp.float32, 'conv_b': _jnp.float32, 'w_attn_out': _jnp.float32, 'w_conv_out': _jnp.float32, 'w_o': _jnp.float32, 'norm2_g': _jnp.float32, 'w_up': _jnp.float32, 'ffn_conv_w': _jnp.float32, 'ffn_conv_b': _jnp.float32, 'w_down': _jnp.float32, 'final_g': _jnp.float32}
MOMENT_SCALE = {'c_ctx': 2.842182e-03, 'w_ada': 4.921653e-02, 'b_ada': 8.636144e-02, 'norm1_g': 5.016824e-02, 'w_in': 2.396684e-02, 'q_norm_g': 3.486034e-03, 'kv_norm_g': 1.257115e-02, 'w_uq': 2.331117e-03, 'w_ukv': 5.808044e-03, 'conv_w': 4.131820e-02, 'conv_b': 3.564001e-02, 'w_attn_out': 5.773980e-03, 'w_conv_out': 2.746912e-02, 'w_o': 2.798304e-02, 'norm2_g': 3.910042e-02, 'w_up': 1.698447e-02, 'ffn_conv_w': 1.712331e-02, 'ffn_conv_b': 1.569818e-02, 'w_down': 2.802218e-02, 'final_g': 1.602417e+01}


def _to_microbatches(a, axis):
    t = _jnp.moveaxis(a, axis, 0)
    t = t.reshape((N_MICROBATCH, t.shape[0] // N_MICROBATCH) + t.shape[1:])
    return _jnp.moveaxis(t, 1, axis + 1)


def setup_inputs(seed: int = 0) -> dict:
    inp = _fwd_setup_inputs(seed)
    key = _jax.random.fold_in(_jax.random.key(seed), 7919)
    shape, _ = _output_shape()
    out = dict(inp)
    out["loss_target"] = _jax.random.normal(_jax.random.fold_in(key, 0), shape, _jnp.float32)
    for i, name in enumerate(TWIN_WEIGHTS):
        w = inp[name].astype(_jnp.float32)
        if MOMENT_SCALE is None:
            s = _jnp.sqrt(_jnp.mean(_jnp.square(w)) + 1e-30)
        else:
            s = MOMENT_SCALE[name]
        km, kv = _jax.random.split(_jax.random.fold_in(key, i + 1))
        out[name] = w
        out["m_" + name] = s * _jax.random.normal(km, w.shape, _jnp.float32)
        out["v_" + name] = (s * s) * _jax.random.uniform(kv, w.shape, _jnp.float32, 0.5, 1.5)
    if N_MICROBATCH > 1:
        for name, axis in PER_EXAMPLE_BATCH_AXIS.items():
            out[name] = _to_microbatches(out[name], axis)
    return {'x': out['x'], 'c': out['c'], 'ctx': out['ctx'], 'c_ctx': out['c_ctx'], 'w_ada': out['w_ada'], 'b_ada': out['b_ada'], 'norm1_g': out['norm1_g'], 'w_in': out['w_in'], 'q_norm_g': out['q_norm_g'], 'kv_norm_g': out['kv_norm_g'], 'w_uq': out['w_uq'], 'w_ukv': out['w_ukv'], 'conv_w': out['conv_w'], 'conv_b': out['conv_b'], 'w_attn_out': out['w_attn_out'], 'w_conv_out': out['w_conv_out'], 'w_o': out['w_o'], 'norm2_g': out['norm2_g'], 'w_up': out['w_up'], 'ffn_conv_w': out['ffn_conv_w'], 'ffn_conv_b': out['ffn_conv_b'], 'w_down': out['w_down'], 'final_g': out['final_g'], 'loss_target': out['loss_target'], 'm_c_ctx': out['m_c_ctx'], 'm_w_ada': out['m_w_ada'], 'm_b_ada': out['m_b_ada'], 'm_norm1_g': out['m_norm1_g'], 'm_w_in': out['m_w_in'], 'm_q_norm_g': out['m_q_norm_g'], 'm_kv_norm_g': out['m_kv_norm_g'], 'm_w_uq': out['m_w_uq'], 'm_w_ukv': out['m_w_ukv'], 'm_conv_w': out['m_conv_w'], 'm_conv_b': out['m_conv_b'], 'm_w_attn_out': out['m_w_attn_out'], 'm_w_conv_out': out['m_w_conv_out'], 'm_w_o': out['m_w_o'], 'm_norm2_g': out['m_norm2_g'], 'm_w_up': out['m_w_up'], 'm_ffn_conv_w': out['m_ffn_conv_w'], 'm_ffn_conv_b': out['m_ffn_conv_b'], 'm_w_down': out['m_w_down'], 'm_final_g': out['m_final_g'], 'v_c_ctx': out['v_c_ctx'], 'v_w_ada': out['v_w_ada'], 'v_b_ada': out['v_b_ada'], 'v_norm1_g': out['v_norm1_g'], 'v_w_in': out['v_w_in'], 'v_q_norm_g': out['v_q_norm_g'], 'v_kv_norm_g': out['v_kv_norm_g'], 'v_w_uq': out['v_w_uq'], 'v_w_ukv': out['v_w_ukv'], 'v_conv_w': out['v_conv_w'], 'v_conv_b': out['v_conv_b'], 'v_w_attn_out': out['v_w_attn_out'], 'v_w_conv_out': out['v_w_conv_out'], 'v_w_o': out['v_w_o'], 'v_norm2_g': out['v_norm2_g'], 'v_w_up': out['v_w_up'], 'v_ffn_conv_w': out['v_ffn_conv_w'], 'v_ffn_conv_b': out['v_ffn_conv_b'], 'v_w_down': out['v_w_down'], 'v_final_g': out['v_final_g']}


def _loss(weights, diff, rest, loss_target):
    with _jax.named_scope("forward"):
        args = {**rest, TWIN_DIFF_INPUT: diff, **{k: w.astype(_WEIGHT_DTYPES[k]) for k, w in weights.items()}}
        y = _forward(args)
    with _jax.named_scope("loss_head"):
        err = _jnp.square(y.astype(_jnp.float32) - loss_target)
        return 0.5 * _jnp.sum(_jnp.mean(err, axis=-1)) if err.ndim else 0.5 * err


def _adamw(w, g, m, v):
    m = ADAM_B1 * m + (1.0 - ADAM_B1) * g
    v = ADAM_B2 * v + (1.0 - ADAM_B2) * _jnp.square(g)
    m_hat = m / (1.0 - ADAM_B1 ** ADAM_STEP)
    v_hat = v / (1.0 - ADAM_B2 ** ADAM_STEP)
    delta = -ADAM_LR * (m_hat / (_jnp.sqrt(v_hat) + ADAM_EPS) + ADAM_WD * w)
    return delta, m, v


def reference(x, c, ctx, c_ctx, w_ada, b_ada, norm1_g, w_in, q_norm_g, kv_norm_g, w_uq, w_ukv, conv_w, conv_b, w_attn_out, w_conv_out, w_o, norm2_g, w_up, ffn_conv_w, ffn_conv_b, w_down, final_g, loss_target, m_c_ctx, m_w_ada, m_b_ada, m_norm1_g, m_w_in, m_q_norm_g, m_kv_norm_g, m_w_uq, m_w_ukv, m_conv_w, m_conv_b, m_w_attn_out, m_w_conv_out, m_w_o, m_norm2_g, m_w_up, m_ffn_conv_w, m_ffn_conv_b, m_w_down, m_final_g, v_c_ctx, v_w_ada, v_b_ada, v_norm1_g, v_w_in, v_q_norm_g, v_kv_norm_g, v_w_uq, v_w_ukv, v_conv_w, v_conv_b, v_w_attn_out, v_w_conv_out, v_w_o, v_norm2_g, v_w_up, v_ffn_conv_w, v_ffn_conv_b, v_w_down, v_final_g):
    given = dict(x=x, c=c, ctx=ctx, c_ctx=c_ctx, w_ada=w_ada, b_ada=b_ada, norm1_g=norm1_g, w_in=w_in, q_norm_g=q_norm_g, kv_norm_g=kv_norm_g, w_uq=w_uq, w_ukv=w_ukv, conv_w=conv_w, conv_b=conv_b, w_attn_out=w_attn_out, w_conv_out=w_conv_out, w_o=w_o, norm2_g=norm2_g, w_up=w_up, ffn_conv_w=ffn_conv_w, ffn_conv_b=ffn_conv_b, w_down=w_down, final_g=final_g, loss_target=loss_target, m_c_ctx=m_c_ctx, m_w_ada=m_w_ada, m_b_ada=m_b_ada, m_norm1_g=m_norm1_g, m_w_in=m_w_in, m_q_norm_g=m_q_norm_g, m_kv_norm_g=m_kv_norm_g, m_w_uq=m_w_uq, m_w_ukv=m_w_ukv, m_conv_w=m_conv_w, m_conv_b=m_conv_b, m_w_attn_out=m_w_attn_out, m_w_conv_out=m_w_conv_out, m_w_o=m_w_o, m_norm2_g=m_norm2_g, m_w_up=m_w_up, m_ffn_conv_w=m_ffn_conv_w, m_ffn_conv_b=m_ffn_conv_b, m_w_down=m_w_down, m_final_g=m_final_g, v_c_ctx=v_c_ctx, v_w_ada=v_w_ada, v_b_ada=v_b_ada, v_norm1_g=v_norm1_g, v_w_in=v_w_in, v_q_norm_g=v_q_norm_g, v_kv_norm_g=v_kv_norm_g, v_w_uq=v_w_uq, v_w_ukv=v_w_ukv, v_conv_w=v_conv_w, v_conv_b=v_conv_b, v_w_attn_out=v_w_attn_out, v_w_conv_out=v_w_conv_out, v_w_o=v_w_o, v_norm2_g=v_norm2_g, v_w_up=v_w_up, v_ffn_conv_w=v_ffn_conv_w, v_ffn_conv_b=v_ffn_conv_b, v_w_down=v_w_down, v_final_g=v_final_g)
    weights = {n: given[n] for n in TWIN_WEIGHTS}
    shared = {n: given[n] for n in SHARED_INPUTS}
    per_example = {n: given[n] for n in ['x', 'c', 'ctx']}
    grad_fn = _jax.value_and_grad(_loss, argnums=(0, 1))

    def one_microbatch(ex, loss_target):
        ex = dict(ex)
        diff = ex.pop(TWIN_DIFF_INPUT)
        return grad_fn(weights, diff, {**shared, **ex}, loss_target)

    if N_MICROBATCH == 1:
        loss, (grad_w, grad_x) = one_microbatch(per_example, given["loss_target"])
    else:
        def body(carry, xs):
            loss_sum, grad_sum = carry
            l_k, (gw_k, gx_k) = one_microbatch(xs[0], xs[1])
            with _jax.named_scope("update"):
                return (loss_sum + l_k, _jax.tree.map(_jnp.add, grad_sum, gw_k)), gx_k

        init = (_jnp.zeros((), _jnp.float32), _jax.tree.map(_jnp.zeros_like, weights))
        (loss, grad_w), grad_x = _jax.lax.scan(body, init, (per_example, given["loss_target"]))
    with _jax.named_scope("update"):
        delta_w, new_m, new_v = {}, {}, {}
        for n in TWIN_WEIGHTS:
            delta_w[n], new_m[n], new_v[n] = _adamw(weights[n], grad_w[n], given["m_" + n], given["v_" + n])
    return (loss, grad_x, *[grad_w[n] for n in TWIN_WEIGHTS], *[delta_w[n] for n in TWIN_WEIGHTS],
            *[new_m[n] for n in TWIN_WEIGHTS], *[new_v[n] for n in TWIN_WEIGHTS])
```

```python
import functools

import jax
import jax.numpy as jnp
from jax import lax
from jax.experimental import pallas as pl
from jax.experimental.pallas import tpu as pltpu

F32, BF16 = jnp.float32, jnp.bfloat16
MESH = pl.DeviceIdType.MESH

D_MODEL = 1024
N_HEADS = 8
HEAD_PAD = 128
QK_DIM = 96
Q_RANK, KV_RANK = 384, 256
CONV_DIM = 512
D_FF = 2816
GRID_W = 64
ROPE_THETA = 10000.0
EPS = 1e-6
GA0, GC0, CX0, CB0, CC0, KV0, Q0, KR0, P_COLS = 0, 1024, 2048, 2560, 3072, 3584, 3840, 4224, 4352
ROW_TILE = 256
VMEM_LIMIT_BYTES = 48 * 1024 * 1024

ADAM_LR, ADAM_B1, ADAM_B2, ADAM_EPS, ADAM_WD, ADAM_STEP = 0.001, 0.9, 0.999, 1e-08, 0.01, 10

BIG = (("w_in", (1024, 1064)), ("w_uq", (384, 192)), ("w_ukv", (256, 256)), ("w_attn_out", (512, 256)),
       ("w_conv_out", (512, 256)), ("w_o", (256, 1024)), ("w_up", (1024, 1408)), ("w_down", (704, 1024)))
BIG_ELEMS = sum(r * c for _, (r, c) in BIG)
PACK_ROWS = 1920
PACK_HALF = PACK_ROWS * 1024

NN = (((1,), (0,)), ((), ()))
NT = (((1,), (1,)), ((), ()))
TN = (((0,), (0,)), ((), ()))


def _cp(sem):
    return pltpu.CompilerParams(dimension_semantics=sem, vmem_limit_bytes=VMEM_LIMIT_BYTES)


def _pick(n, prefs):
    for p in prefs:
        if n % p == 0:
            return p
    return n


def _mm(a, b, mode, M, N, K, *, tm, tn, tk, name, out_dtype=F32, a_spec=None, b_spec=None, o_spec=None,
        out_shape=None):
    assert M % tm == 0 and N % tn == 0 and K % tk == 0, (name, M, N, K, tm, tn, tk)
    nk = K // tk
    dims = {"nn": NN, "nt": NT, "tn": TN}[mode]
    if a_spec is None:
        a_spec = (pl.BlockSpec((tk, tm), lambda i, j, k: (k, i)) if mode == "tn"
                  else pl.BlockSpec((tm, tk), lambda i, j, k: (i, k)))
    if b_spec is None:
        b_spec = (pl.BlockSpec((tn, tk), lambda i, j, k: (j, k)) if mode == "nt"
                  else pl.BlockSpec((tk, tn), lambda i, j, k: (k, j)))
    if o_spec is None:
        o_spec = pl.BlockSpec((tm, tn), lambda i, j, k: (i, j))
    if out_shape is None:
        out_shape = (M, N)

    def body(a_ref, b_ref, o_ref, acc_ref):
        k = pl.program_id(2)
        part = lax.dot_general(a_ref[...].astype(BF16), b_ref[...].astype(BF16), dims, preferred_element_type=F32)

        @pl.when(k == 0)
        def _():
            acc_ref[...] = part

        @pl.when(k > 0)
        def _():
            acc_ref[...] += part

        @pl.when(k == nk - 1)
        def _():
            o_ref[...] = acc_ref[...].astype(o_ref.dtype)

    return pl.pallas_call(
        body, grid=(M // tm, N // tn, nk), in_specs=[a_spec, b_spec], out_specs=o_spec,
        out_shape=jax.ShapeDtypeStruct(out_shape, out_dtype), scratch_shapes=[pltpu.VMEM((tm, tn), F32)],
        compiler_params=_cp(("parallel", "parallel", "arbitrary")), name=name)(a, b)


def _ew(fn, grid, ins, outs, name):
    n_in = len(ins)

    def store(ref, val, acc, ids):
        if isinstance(val, (list, tuple)):
            for h, v in enumerate(val):
                ref[h] = v.astype(ref.dtype)
            return
        if acc is None:
            ref[...] = val.astype(ref.dtype)
            return

        @pl.when(ids[acc] == 0)
        def _():
            ref[...] = val.astype(ref.dtype)

        @pl.when(ids[acc] > 0)
        def _():
            ref[...] += val.astype(ref.dtype)

    def body(*refs):
        ids = tuple(pl.program_id(a) for a in range(len(grid)))
        vals = fn(ids, *[r[...] for r in refs[:n_in]])
        for ref, val, (_, _, _, acc) in zip(refs[n_in:], vals, outs):
            store(ref, val, acc, ids)

    acc_axes = {o[3] for o in outs if o[3] is not None}
    sem = tuple("arbitrary" if a in acc_axes else "parallel" for a in range(len(grid)))
    res = pl.pallas_call(
        body, grid=grid, in_specs=[s for _, s in ins], out_specs=[o[2] for o in outs],
        out_shape=[jax.ShapeDtypeStruct(o[0], o[1]) for o in outs], compiler_params=_cp(sem), name=name,
    )(*[a for a, _ in ins])
    return res


def _rows(width, cblk=0, roff=0, tr=ROW_TILE):
    return pl.BlockSpec((tr, width), lambda i: (i + roff, cblk))


def _full(shape):
    nd = len(shape)
    return pl.BlockSpec(shape, lambda *_: (0,) * nd)


def _sigmoid(x):
    return 1.0 / (1.0 + jnp.exp(-x))


def _rms(x):
    return lax.rsqrt(jnp.mean(x * x, axis=-1, keepdims=True) + EPS)


def _rms_bwd(dn, xn, r):
    return r * (dn - xn * jnp.mean(dn * xn, axis=-1, keepdims=True))


def _colsum(x):
    return jnp.sum(x, axis=0, keepdims=True)


def _shift_prev(x):
    rows = lax.broadcasted_iota(jnp.int32, x.shape, 0)
    return jnp.where(rows == 0, 0.0, pltpu.roll(x, 1, 0))


def _shift_next(x):
    rows = lax.broadcasted_iota(jnp.int32, x.shape, 0)
    return jnp.where(rows == x.shape[0] - 1, 0.0, pltpu.roll(x, x.shape[0] - 1, 0))


def _conv(x, w, b):
    return b + _shift_prev(x) * w[0:1] + x * w[1:2] + _shift_next(x) * w[2:3]


def _conv_bwd_x(dy, w):
    return _shift_next(dy) * w[0:1] + dy * w[1:2] + _shift_prev(dy) * w[2:3]


def _conv_bwd_w(dy, x):
    return _colsum(dy * _shift_prev(x)), _colsum(dy * x), _colsum(dy * _shift_next(x))


def _rope(x, cos, sin_lo, sin_hi):
    return x * cos + pltpu.roll(x, HEAD_PAD - 8, 1) * sin_lo + pltpu.roll(x, 8, 1) * sin_hi


ATTN_SCALE = QK_DIM ** -0.5


def _attn_fwd(qr, kc, vp, T, TT):
    tq = ROW_TILE

    def body(q_ref, k_ref, v_ref, o_ref, l_ref):
        s = lax.dot_general(q_ref[...], k_ref[...], NT, preferred_element_type=F32) * ATTN_SCALE
        m = jnp.max(s, axis=-1, keepdims=True)
        p = jnp.exp(s - m)
        l = jnp.sum(p, axis=-1, keepdims=True)
        o = lax.dot_general(p.astype(BF16), v_ref[...], NN, preferred_element_type=F32)
        o_ref[...] = o / l
        l_ref[...] = m + jnp.log(l)

    qspec = pl.BlockSpec((None, tq, HEAD_PAD), lambda h, i: (h, i, 0))
    kspec = pl.BlockSpec((None, TT, HEAD_PAD), lambda h, i: (h, 0, 0))
    return pl.pallas_call(
        body, grid=(N_HEADS, T // tq), in_specs=[qspec, kspec, kspec],
        out_specs=[qspec, pl.BlockSpec((None, tq, 1), lambda h, i: (h, i, 0))],
        out_shape=[jax.ShapeDtypeStruct((N_HEADS, T, HEAD_PAD), F32), jax.ShapeDtypeStruct((N_HEADS, T, 1), F32)],
        compiler_params=_cp(("parallel", "parallel")), name="attn_fwd")(qr, kc, vp)


def _attn_bwd(qr, kc, vp, o, do, lse, T, TT):
    tq = ROW_TILE
    nq = T // tq

    def body(q_ref, k_ref, v_ref, o_ref, do_ref, l_ref, dq_ref, dk_ref, dv_ref):
        i = pl.program_id(1)

        @pl.when(i == 0)
        def _():
            dk_ref[...] = jnp.zeros_like(dk_ref)
            dv_ref[...] = jnp.zeros_like(dv_ref)

        @pl.when(i < nq)
        def _():
            q, k, v, d_o = q_ref[...], k_ref[...], v_ref[...], do_ref[...]
            s = lax.dot_general(q, k, NT, preferred_element_type=F32) * ATTN_SCALE
            p = jnp.exp(s - l_ref[...])
            dob = d_o.astype(BF16)
            dp = lax.dot_general(dob, v, NT, preferred_element_type=F32)
            dd = jnp.sum(d_o * o_ref[...], axis=-1, keepdims=True)
            ds = (p * (dp - dd) * ATTN_SCALE).astype(BF16)
            dq_ref[...] = lax.dot_general(ds, k, NN, preferred_element_type=F32)
            dk_ref[...] += lax.dot_general(ds, q, TN, preferred_element_type=F32)
            dv_ref[...] += lax.dot_general(p.astype(BF16), dob, TN, preferred_element_type=F32)

        @pl.when(i == nq)
        def _():
            dq_ref[...] = jnp.zeros_like(dq_ref)

    qspec = pl.BlockSpec((None, tq, HEAD_PAD), lambda h, i: (h, i, 0))
    lat = pl.BlockSpec((None, tq, HEAD_PAD), lambda h, i: (h, jnp.minimum(i, nq - 1), 0))
    lspec = pl.BlockSpec((None, tq, 1), lambda h, i: (h, jnp.minimum(i, nq - 1), 0))
    kspec = pl.BlockSpec((None, TT, HEAD_PAD), lambda h, i: (h, 0, 0))
    big = jax.ShapeDtypeStruct((N_HEADS, TT, HEAD_PAD), F32)
    return pl.pallas_call(
        body, grid=(N_HEADS, TT // tq), in_specs=[qspec, kspec, kspec, lat, lat, lspec],
        out_specs=[qspec, kspec, kspec], out_shape=[big, big, big],
        compiler_params=_cp(("parallel", "arbitrary")), name="attn_bwd")(qr, kc, vp, o, do, lse)


def _allgather8(x, name, in_vmem):
    m_per, n = x.shape

    def body(x_ref, out_ref, send_sems, recv_sems, local_sem):
        mx, my, mc = lax.axis_index("x"), lax.axis_index("y"), lax.axis_index("c")
        me, sibling = (mx, my, mc), (mx, my, 1 - mc)
        chips = [(1 - mx, my), (mx, 1 - my), (1 - mx, 1 - my)]

        def rows(px, py, pc):
            return out_ref.at[pl.ds((4 * px + 2 * py + pc) * m_per, m_per), :]

        def copy(k, block, to, src=None):
            return pltpu.make_async_remote_copy(
                src_ref=rows(*block) if src is None else src, dst_ref=rows(*block),
                send_sem=send_sems.at[k], recv_sem=recv_sems.at[k], device_id=to, device_id_type=MESH)

        mine = pltpu.make_async_copy(x_ref, rows(*me), local_sem)
        mine.start()
        first = [copy(0, me, sibling, src=x_ref)]
        first += [copy(1 + j, me, (*chip, mc), src=x_ref) for j, chip in enumerate(chips)]
        for cp in first:
            cp.start()
        passed = [copy(4 + j, (*chip, mc), sibling) for j, chip in enumerate(chips)]
        for j, chip in enumerate(chips):
            copy(1 + j, (*chip, mc), me).wait_recv()
            passed[j].start()
        copy(0, sibling, me).wait_recv()
        for j, chip in enumerate(chips):
            copy(4 + j, (*chip, 1 - mc), me).wait_recv()
        for cp in first + passed:
            cp.wait_send()
        mine.wait()

    space = pltpu.VMEM if in_vmem else pl.ANY
    return pl.pallas_call(
        body, out_shape=jax.ShapeDtypeStruct((8 * m_per, n), x.dtype),
        in_specs=[pl.BlockSpec(memory_space=space)], out_specs=pl.BlockSpec(memory_space=space),
        scratch_shapes=[pltpu.SemaphoreType.DMA((7,)), pltpu.SemaphoreType.DMA((7,)), pltpu.SemaphoreType.DMA],
        name=name)(x)


def _pair_send(send, name):
    def body(s_ref, l_ref, send_sem, recv_sem):
        mx, my, mc = lax.axis_index("x"), lax.axis_index("y"), lax.axis_index("c")
        cp = pltpu.make_async_remote_copy(src_ref=s_ref, dst_ref=l_ref, send_sem=send_sem, recv_sem=recv_sem,
                                          device_id=(mx, my, 1 - mc), device_id_type=MESH)
        cp.start()
        cp.wait()

    return pl.pallas_call(
        body, out_shape=jax.ShapeDtypeStruct(send.shape, send.dtype),
        in_specs=[pl.BlockSpec(memory_space=pl.ANY)], out_specs=pl.BlockSpec(memory_space=pl.ANY),
        scratch_shapes=[pltpu.SemaphoreType.DMA, pltpu.SemaphoreType.DMA], name=name)(send)


def _chip_scatter(parts, name):
    def body(s_ref, l_ref, send_sems, recv_sems, local_sem):
        mx, my, mc = lax.axis_index("x"), lax.axis_index("y"), lax.axis_index("c")
        j_me = 2 * mx + my
        local = pltpu.make_async_copy(s_ref.at[j_me], l_ref.at[j_me], local_sem)
        local.start()
        copies = []
        for k, (px, py) in enumerate([(1 - mx, my), (mx, 1 - my), (1 - mx, 1 - my)]):
            cp = pltpu.make_async_remote_copy(
                src_ref=s_ref.at[2 * px + py], dst_ref=l_ref.at[j_me], send_sem=send_sems.at[k],
                recv_sem=recv_sems.at[k], device_id=(px, py, mc), device_id_type=MESH)
            cp.start()
            copies.append(cp)
        for cp in copies:
            cp.wait()
        local.wait()

    return pl.pallas_call(
        body, out_shape=jax.ShapeDtypeStruct(parts.shape, parts.dtype),
        in_specs=[pl.BlockSpec(memory_space=pl.ANY)], out_specs=pl.BlockSpec(memory_space=pl.ANY),
        scratch_shapes=[pltpu.SemaphoreType.DMA((3,)), pltpu.SemaphoreType.DMA((3,)), pltpu.SemaphoreType.DMA],
        name=name)(parts)


def _pair_exchange(r, name):
    def body(r_ref, o_ref, send_sem, recv_sem, local_sem):
        mx, my, mc = lax.axis_index("x"), lax.axis_index("y"), lax.axis_index("c")
        local = pltpu.make_async_copy(r_ref, o_ref.at[mc], local_sem)
        local.start()
        cp = pltpu.make_async_remote_copy(src_ref=r_ref, dst_ref=o_ref.at[mc], send_sem=send_sem, recv_sem=recv_sem,
                                          device_id=(mx, my, 1 - mc), device_id_type=MESH)
        cp.start()
        cp.wait()
        local.wait()

    return pl.pallas_call(
        body, out_shape=jax.ShapeDtypeStruct((2,) + r.shape, r.dtype),
        in_specs=[pl.BlockSpec(memory_space=pl.ANY)], out_specs=pl.BlockSpec(memory_space=pl.ANY),
        scratch_shapes=[pltpu.SemaphoreType.DMA, pltpu.SemaphoreType.DMA, pltpu.SemaphoreType.DMA],
        name=name)(r)


def _w_in_to_p(w):
    z = lambda n: jnp.zeros((w.shape[0], n), w.dtype)
    return jnp.concatenate([w[:, 2208:3232], w[:, 3232:4256], w[:, 672:1184], w[:, 1184:1696], w[:, 1696:2208],
                            w[:, 0:256], w[:, 288:672], z(64), w[:, 256:288], z(32)], axis=1)


def _w_in_from_p(g):
    return jnp.concatenate([g[:, KV0:KV0 + 256], g[:, KR0 + 64:KR0 + 96], g[:, Q0:Q0 + 384], g[:, CX0:CX0 + 512],
                            g[:, CB0:CB0 + 512], g[:, CC0:CC0 + 512], g[:, GA0:GA0 + 1024], g[:, GC0:GC0 + 1024]],
                           axis=1)


def _cols_from_shards(s):
    return jnp.transpose(s, (1, 0, 2)).reshape(s.shape[1], -1)


def _cols_to_shards(g):
    k, n = g.shape
    return jnp.transpose(g.reshape(k, 4, n // 4), (1, 0, 2)).reshape(4, -1)


def _rope_tables(T, TT, inverse):
    rows = T // GRID_W
    row = jnp.repeat(jnp.arange(rows), GRID_W).astype(F32)
    col = jnp.tile(jnp.arange(GRID_W), rows).astype(F32)
    inv = ROPE_THETA ** (-jnp.arange(0, 16, 2, dtype=F32) / 16)
    ang = jnp.concatenate([row[:, None] * inv, col[:, None] * inv], axis=-1)
    cos, sin = jnp.cos(ang), jnp.sin(ang)
    lane = jnp.arange(32)
    src = (lane // 16) * 8 + lane % 8
    lo = ((lane % 16) // 8 == 0).astype(F32)
    sgn = -1.0 if inverse else 1.0
    cos32 = cos[:, src]
    sin_lo32 = -sgn * sin[:, src] * lo
    sin_hi32 = sgn * sin[:, src] * (1.0 - lo)

    def widen(t32, fill):
        t = jnp.concatenate([jnp.full((T, 64), fill, F32), t32, jnp.full((T, 32), fill, F32)], axis=1)
        return jnp.concatenate([t, jnp.full((TT - T, HEAD_PAD), fill, F32)], axis=0)

    return widen(cos32, 1.0), widen(sin_lo32, 0.0), widen(sin_hi32, 0.0)


def _local_step(xx, tgt, mod_lat, mod_ctx, W):
    TT = xx.shape[0]
    T = tgt.shape[0]
    n_lat, n_all = T // ROW_TILE, TT // ROW_TILE
    sh1, sc1, g1, sh2, sc2, g2 = [mod_lat[:, k * D_MODEL:(k + 1) * D_MODEL] for k in range(6)]
    csh1, csc1 = mod_ctx[:, :D_MODEL], mod_ctx[:, D_MODEL:2 * D_MODEL]
    vec = lambda n: _full((1, n))
    row_out = lambda n, dt, rows=T: ((rows, n), dt, _rows(n), None)
    acc_out = lambda n: ((1, n), F32, _full((1, n)), 0)

    def f_norm1(ids, x, g, a_sh, a_sc, b_sh, b_sc):
        ctx = ids[0] >= n_lat
        sh, sc = jnp.where(ctx, b_sh, a_sh), jnp.where(ctx, b_sc, a_sc)
        return ((x * _rms(x) * g) * (1.0 + sc) + sh,)

    (hh,) = _ew(f_norm1, (n_all,), [(xx, _rows(D_MODEL)), (W["norm1_g"], vec(D_MODEL)), (sh1, vec(D_MODEL)),
                                   (sc1, vec(D_MODEL)), (csh1, vec(D_MODEL)), (csc1, vec(D_MODEL))],
                [row_out(D_MODEL, BF16, TT)], "norm1_fwd")
    tm_all = _pick(TT, (768, 256))
    pp = _mm(hh, W["w_in"], "nn", TT, P_COLS, D_MODEL, tm=tm_all, tn=2176, tk=512, name="w_in_fwd")

    def f_lowrank(ids, ckv, cq, gkv, gq):
        return ckv * _rms(ckv) * gkv, cq * _rms(cq) * gq

    nkv, nq = _ew(f_lowrank, (n_all,), [(pp, _rows(KV_RANK, KV0 // KV_RANK)), (pp, _rows(Q_RANK, Q0 // Q_RANK)),
                                       (W["kv_norm_g"], vec(KV_RANK)), (W["q_norm_g"], vec(Q_RANK))],
                  [row_out(KV_RANK, BF16, TT), row_out(Q_RANK, BF16, TT)], "lowrank_norm_fwd")
    heads_out = pl.BlockSpec((None, tm_all, HEAD_PAD), lambda i, j, k: (j, i, 0))
    heads_shape = (N_HEADS, TT, HEAD_PAD)
    kv = _mm(nkv, W["w_ukv"], "nn", TT, 1024, KV_RANK, tm=tm_all, tn=HEAD_PAD, tk=KV_RANK, name="w_ukv_fwd",
             o_spec=heads_out, out_shape=heads_shape)
    q_raw = _mm(nq, W["w_uq"], "nn", TT, 1024, Q_RANK, tm=tm_all, tn=HEAD_PAD, tk=Q_RANK, name="w_uq_fwd",
                o_spec=heads_out, out_shape=heads_shape)

    cos_f, slo_f, shi_f = _rope_tables(T, TT, inverse=False)
    cos_b, slo_b, shi_b = _rope_tables(T, TT, inverse=True)
    hspec = pl.BlockSpec((None, ROW_TILE, HEAD_PAD), lambda h, i: (h, i, 0))
    tspec = pl.BlockSpec((ROW_TILE, HEAD_PAD), lambda h, i: (i, 0))

    def f_prep(ids, q, kvh, kr, cos, slo, shi):
        lane = lax.broadcasted_iota(jnp.int32, q.shape, 1)
        return (_rope(q, cos, slo, shi), jnp.where(lane < 64, kvh, _rope(kr, cos, slo, shi)),
                jnp.where(lane >= 64, kvh, 0.0))

    qr, kc, vp = _ew(f_prep, (N_HEADS, n_all),
                     [(q_raw, hspec), (kv, hspec), (pp, pl.BlockSpec((ROW_TILE, HEAD_PAD), lambda h, i: (i, KR0 // 128))),
                      (cos_f, tspec), (slo_f, tspec), (shi_f, tspec)],
                     [(heads_shape, BF16, hspec, None)] * 3, "attn_prep")
    o_pad, lse = _attn_fwd(qr, kc, vp, T, TT)
    tm_lat = _pick(T, (1024, 512, 256))
    kmajor_a = lambda tm: pl.BlockSpec((None, tm, HEAD_PAD), lambda i, j, k: (k, i, 0))
    ya = _mm(o_pad, W["w_attn_out"], "nn", T, D_MODEL, 1024, tm=tm_lat, tn=D_MODEL, tk=HEAD_PAD, name="w_attn_out_fwd",
             a_spec=kmajor_a(tm_lat))

    tc = 256
    colT = lambda blk0: pl.BlockSpec((T, tc), lambda j: (0, blk0 + j))

    def f_conv(ids, xin, cb, cc, w, b):
        return (cb * _conv(cc * xin, w, b),)

    (e,) = _ew(f_conv, (CONV_DIM // tc,),
               [(pp, colT(CX0 // tc)), (pp, colT(CB0 // tc)), (pp, colT(CC0 // tc)),
                (W["conv_w"], pl.BlockSpec((3, tc), lambda j: (0, j))), (W["conv_b"], pl.BlockSpec((1, tc), lambda j: (0, j)))],
               [((T, CONV_DIM), BF16, colT(0), None)], "conv_fwd")
    yc = _mm(e, W["w_conv_out"], "nn", T, D_MODEL, CONV_DIM, tm=tm_lat, tn=D_MODEL, tk=CONV_DIM, name="w_conv_out_fwd")

    def f_merge(ids, ga, gc, a, c):
        return (_sigmoid(ga) * a + _sigmoid(gc) * c,)

    (mrg,) = _ew(f_merge, (n_lat,), [(pp, _rows(D_MODEL, 0)), (pp, _rows(D_MODEL, 1)), (ya, _rows(D_MODEL)),
                                    (yc, _rows(D_MODEL))], [row_out(D_MODEL, BF16)], "merge_fwd")
    mo = _mm(mrg, W["w_o"], "nn", T, D_MODEL, D_MODEL, tm=tm_lat, tn=D_MODEL, tk=512, name="w_o_fwd")

    def f_norm2(ids, x, m, gate, g, sh, sc):
        x1 = x + gate * m
        return x1, (x1 * _rms(x1) * g) * (1.0 + sc) + sh

    x1, h2 = _ew(f_norm2, (n_lat,), [(xx, _rows(D_MODEL)), (mo, _rows(D_MODEL)), (g1, vec(D_MODEL)),
                                    (W["norm2_g"], vec(D_MODEL)), (sh2, vec(D_MODEL)), (sc2, vec(D_MODEL))],
                 [row_out(D_MODEL, F32), row_out(D_MODEL, BF16)], "norm2_fwd")
    up = _mm(h2, W["w_up"], "nn", T, 2 * D_FF, D_MODEL, tm=tm_lat, tn=1408, tk=512, name="w_up_fwd")

    n_ff = D_FF // tc
    ffw = lambda off, n=3: pl.BlockSpec((n, tc), lambda j: (0, j + off))

    def f_ffn(ids, ug, uv, wg, wv, bg, bv):
        gate, val = _conv(ug, wg, bg), _conv(uv, wv, bv)
        return (gate * _sigmoid(gate) * val,)

    (act,) = _ew(f_ffn, (n_ff,), [(up, colT(0)), (up, colT(n_ff)), (W["ffn_conv_w"], ffw(0)), (W["ffn_conv_w"], ffw(n_ff)),
                                 (W["ffn_conv_b"], ffw(0, 1)), (W["ffn_conv_b"], ffw(n_ff, 1))],
                 [((T, D_FF), BF16, colT(0), None)], "ffn_act_fwd")
    f = _mm(act, W["w_down"], "nn", T, D_MODEL, D_FF, tm=tm_lat, tn=D_MODEL, tk=1408, name="w_down_fwd")

    def f_head(ids, x1_, f_, gate, gf, t):
        x2 = x1_ + gate * f_
        r = _rms(x2)
        xn = x2 * r
        err = xn * gf - t
        loss = 0.5 * jnp.sum(jnp.mean(err * err, axis=-1, keepdims=True))
        dy = err * (1.0 / D_MODEL)
        dx2 = _rms_bwd(dy * gf, xn, r)
        return dx2, dx2 * gate, _colsum(dy * xn), _colsum(dx2 * f_), jnp.full((1, 128), loss, F32)

    dx2, df, dg_f, dg2, loss = _ew(
        f_head, (n_lat,), [(x1, _rows(D_MODEL)), (f, _rows(D_MODEL)), (g2, vec(D_MODEL)), (W["final_g"], vec(D_MODEL)),
                           (tgt, _rows(D_MODEL))],
        [row_out(D_MODEL, F32), row_out(D_MODEL, BF16), acc_out(D_MODEL), acc_out(D_MODEL), acc_out(128)], "loss_head")

    d_w_down = _mm(act, df, "tn", D_FF, D_MODEL, T, tm=1408, tn=D_MODEL, tk=_pick(T, (512, 256)), name="w_down_dw")
    da = _mm(df, W["w_down"], "nt", T, D_FF, D_MODEL, tm=tm_lat, tn=1408, tk=512, name="w_down_dx")

    tcb = 128
    n_fb = D_FF // tcb
    colb = lambda blk0: pl.BlockSpec((T, tcb), lambda j: (0, blk0 + j))
    ffwb = lambda off, n=3: pl.BlockSpec((n, tcb), lambda j: (0, j + off))
    cvec = ((1, D_FF), F32, pl.BlockSpec((1, tcb), lambda j: (0, j)), None)

    def f_ffn_bwd(ids, ug, uv, d_act, wg, wv, bg, bv):
        gate, val = _conv(ug, wg, bg), _conv(uv, wv, bv)
        s = _sigmoid(gate)
        d_gate = d_act * val * s * (1.0 + gate * (1.0 - s))
        d_val = d_act * gate * s
        wg0, wg1, wg2 = _conv_bwd_w(d_gate, ug)
        wv0, wv1, wv2 = _conv_bwd_w(d_val, uv)
        d_up = [_conv_bwd_x(d_gate, wg), _conv_bwd_x(d_val, wv)]
        return d_up, _colsum(d_gate), _colsum(d_val), wg0, wg1, wg2, wv0, wv1, wv2

    ffn_b = _ew(f_ffn_bwd, (n_fb,),
                [(up, colb(0)), (up, colb(n_fb)), (da, colb(0)), (W["ffn_conv_w"], ffwb(0)), (W["ffn_conv_w"], ffwb(n_fb)),
                 (W["ffn_conv_b"], ffwb(0, 1)), (W["ffn_conv_b"], ffwb(n_fb, 1))],
                [((2, T, D_FF), BF16, pl.BlockSpec((2, T, tcb), lambda j: (0, 0, j)), None)] + [cvec] * 8, "ffn_act_bwd")
    d_up3 = ffn_b[0]
    d_ffn_conv_b = jnp.concatenate([ffn_b[1], ffn_b[2]], axis=1)
    d_ffn_conv_w = jnp.concatenate([jnp.concatenate(ffn_b[3:6], axis=0), jnp.concatenate(ffn_b[6:9], axis=0)], axis=1)

    tk_t = _pick(T, (512, 256))
    d_w_up = _mm(h2, d_up3, "tn", D_MODEL, 2 * D_FF, T, tm=D_MODEL, tn=1408, tk=tk_t, name="w_up_dw",
                 b_spec=pl.BlockSpec((None, tk_t, 1408), lambda i, j, k: (j // 2, k, j % 2)))
    dh2 = _mm(d_up3, W["w_up"], "nt", T, D_MODEL, 2 * D_FF, tm=tm_lat, tn=D_MODEL, tk=1408, name="w_up_dx",
              a_spec=pl.BlockSpec((None, tm_lat, 1408), lambda i, j, k: (k // 2, i, k % 2)))

    def f_norm2_bwd(ids, dx2_, dh, x1_, m, g, sc, gate):
        r = _rms(x1_)
        xn = x1_ * r
        dx1 = dx2_ + _rms_bwd(dh * g * (1.0 + sc), xn, r)
        return dx1, dx1 * gate, _colsum(dh), _colsum(dh * xn * g), _colsum(dh * xn * (1.0 + sc)), _colsum(dx1 * m)

    dx1, dmo, dsh2, dsc2, dg_n2, dg1 = _ew(
        f_norm2_bwd, (n_lat,), [(dx2, _rows(D_MODEL)), (dh2, _rows(D_MODEL)), (x1, _rows(D_MODEL)), (mo, _rows(D_MODEL)),
                                (W["norm2_g"], vec(D_MODEL)), (sc2, vec(D_MODEL)), (g1, vec(D_MODEL))],
        [row_out(D_MODEL, F32), row_out(D_MODEL, BF16)] + [acc_out(D_MODEL)] * 4, "norm2_bwd")
    d_w_o = _mm(mrg, dmo, "tn", D_MODEL, D_MODEL, T, tm=D_MODEL, tn=D_MODEL, tk=tk_t, name="w_o_dw")
    dmrg = _mm(dmo, W["w_o"], "nt", T, D_MODEL, D_MODEL, tm=tm_lat, tn=D_MODEL, tk=512, name="w_o_dx")

    def f_merge_bwd(ids, dm, ga, gc, a, c):
        sa, sc_ = _sigmoid(ga), _sigmoid(gc)
        return dm * sa, dm * sc_, dm * a * sa * (1.0 - sa), dm * c * sc_ * (1.0 - sc_)

    dya, dyc, dp_ga, dp_gc = _ew(
        f_merge_bwd, (n_lat,), [(dmrg, _rows(D_MODEL)), (pp, _rows(D_MODEL, 0)), (pp, _rows(D_MODEL, 1)),
                                (ya, _rows(D_MODEL)), (yc, _rows(D_MODEL))], [row_out(D_MODEL, BF16)] * 4, "merge_bwd")

    d_w_ao_p = _mm(o_pad, dya, "tn", 1024, D_MODEL, T, tm=HEAD_PAD, tn=D_MODEL, tk=tk_t, name="w_attn_out_dw",
                   a_spec=pl.BlockSpec((None, tk_t, HEAD_PAD), lambda i, j, k: (i, k, 0)))
    do_pad = _mm(dya, W["w_attn_out"], "nt", T, 1024, D_MODEL, tm=tm_lat, tn=HEAD_PAD, tk=D_MODEL, name="w_attn_out_dx",
                 o_spec=pl.BlockSpec((None, tm_lat, HEAD_PAD), lambda i, j, k: (j, i, 0)), out_shape=(N_HEADS, T, HEAD_PAD))
    d_w_co = _mm(e, dyc, "tn", CONV_DIM, D_MODEL, T, tm=CONV_DIM, tn=D_MODEL, tk=tk_t, name="w_conv_out_dw")
    de = _mm(dyc, W["w_conv_out"], "nt", T, CONV_DIM, D_MODEL, tm=tm_lat, tn=CONV_DIM, tk=D_MODEL, name="w_conv_out_dx")

    def f_conv_bwd(ids, xin, cb, cc, d_e, w, b):
        z = cc * xin
        cz = _conv(z, w, b)
        dcz = d_e * cb
        w0, w1, w2 = _conv_bwd_w(dcz, z)
        dz = _conv_bwd_x(dcz, w)
        return dz * cc, d_e * cz, dz * xin, _colsum(dcz), w0, w1, w2

    cvec_c = ((1, CONV_DIM), F32, pl.BlockSpec((1, tc), lambda j: (0, j)), None)
    conv_b = _ew(f_conv_bwd, (CONV_DIM // tc,),
                 [(pp, colT(CX0 // tc)), (pp, colT(CB0 // tc)), (pp, colT(CC0 // tc)), (de, colT(0)),
                  (W["conv_w"], pl.BlockSpec((3, tc), lambda j: (0, j))), (W["conv_b"], pl.BlockSpec((1, tc), lambda j: (0, j)))],
                 [((T, CONV_DIM), BF16, colT(0), None)] * 3 + [cvec_c] * 4, "conv_bwd")
    dp_cx, dp_cb, dp_cc, d_conv_b = conv_b[:4]
    d_conv_w = jnp.concatenate(conv_b[4:7], axis=0)

    dqr, dkc, dvp = _attn_bwd(qr, kc, vp, o_pad, do_pad, lse, T, TT)

    h3 = pl.BlockSpec((N_HEADS, ROW_TILE, HEAD_PAD), lambda i: (0, i, 0))

    def f_post(ids, dq, dk, dv, cos, slo, shi):
        lane = lax.broadcasted_iota(jnp.int32, cos.shape, 1)
        rot = (lane >= 64) & (lane < 96)
        dq_raw = [_rope(dq[h], cos, slo, shi) for h in range(N_HEADS)]
        dkv_ = [jnp.where(lane < 64, dk[h], dv[h]) for h in range(N_HEADS)]
        kr = jnp.where(rot, dk[0], 0.0)
        for h in range(1, N_HEADS):
            kr = kr + jnp.where(rot, dk[h], 0.0)
        return dq_raw, dkv_, _rope(kr, cos, slo, shi)

    dq_raw, dkv, dp_kr = _ew(f_post, (n_all,), [(dqr, h3), (dkc, h3), (dvp, h3), (cos_b, _rows(HEAD_PAD)),
                                               (slo_b, _rows(HEAD_PAD)), (shi_b, _rows(HEAD_PAD))],
                             [(heads_shape, BF16, h3, None), (heads_shape, BF16, h3, None), row_out(HEAD_PAD, BF16, TT)],
                             "attn_post")

    tk_a = _pick(TT, (768, 256))
    heads_b = pl.BlockSpec((None, tk_a, HEAD_PAD), lambda i, j, k: (j, k, 0))
    d_w_uq_p = _mm(nq, dq_raw, "tn", Q_RANK, 1024, TT, tm=Q_RANK, tn=HEAD_PAD, tk=tk_a, name="w_uq_dw", b_spec=heads_b)
    dnq = _mm(dq_raw, W["w_uq"], "nt", TT, Q_RANK, 1024, tm=tm_all, tn=Q_RANK, tk=HEAD_PAD, name="w_uq_dx",
              a_spec=kmajor_a(tm_all))
    d_w_ukv = _mm(nkv, dkv, "tn", KV_RANK, 1024, TT, tm=KV_RANK, tn=HEAD_PAD, tk=tk_a, name="w_ukv_dw", b_spec=heads_b)
    dnkv = _mm(dkv, W["w_ukv"], "nt", TT, KV_RANK, 1024, tm=tm_all, tn=KV_RANK, tk=HEAD_PAD, name="w_ukv_dx",
               a_spec=kmajor_a(tm_all))

    def f_lowrank_bwd(ids, ckv, cq, dkv_, dq_, gkv, gq):
        rk, rq = _rms(ckv), _rms(cq)
        nk, nq_ = ckv * rk, cq * rq
        return (_rms_bwd(dkv_ * gkv, nk, rk), _rms_bwd(dq_ * gq, nq_, rq), _colsum(dkv_ * nk), _colsum(dq_ * nq_))

    dp_kv, dp_q, dg_kv, dg_q = _ew(
        f_lowrank_bwd, (n_all,), [(pp, _rows(KV_RANK, KV0 // KV_RANK)), (pp, _rows(Q_RANK, Q0 // Q_RANK)),
                                  (dnkv, _rows(KV_RANK)), (dnq, _rows(Q_RANK)), (W["kv_norm_g"], vec(KV_RANK)),
                                  (W["q_norm_g"], vec(Q_RANK))],
        [row_out(KV_RANK, BF16, TT), row_out(Q_RANK, BF16, TT), acc_out(KV_RANK), acc_out(Q_RANK)], "lowrank_norm_bwd")

    lat_cols = jnp.concatenate([dp_ga, dp_gc, dp_cx, dp_cb, dp_cc], axis=1)
    dpp = jnp.concatenate([jnp.pad(lat_cols, ((0, TT - T), (0, 0))), dp_kv, dp_q, dp_kr], axis=1)
    d_w_in_p = _mm(hh, dpp, "tn", D_MODEL, P_COLS, TT, tm=512, tn=2176, tk=_pick(TT, (256,)), name="w_in_dw")
    dhh = _mm(dpp, W["w_in"], "nt", TT, D_MODEL, P_COLS, tm=tm_all, tn=512, tk=2176, name="w_in_dx")

    def f_norm1_bwd(ids, x, dh, dres, g, sc):
        r = _rms(x)
        xn = x * r
        return (dres + _rms_bwd(dh * g * (1.0 + sc), xn, r), _colsum(dh), _colsum(dh * xn * g),
                _colsum(dh * xn * (1.0 + sc)))

    grad_x, dsh1, dsc1, dg_n1 = _ew(
        f_norm1_bwd, (n_lat,), [(xx, _rows(D_MODEL)), (dhh, _rows(D_MODEL)), (dx1, _rows(D_MODEL)),
                                (W["norm1_g"], vec(D_MODEL)), (sc1, vec(D_MODEL))],
        [row_out(D_MODEL, F32)] + [acc_out(D_MODEL)] * 3, "norm1_bwd")

    def f_norm1_ctx_bwd(ids, x, dh, g, sc):
        xn = x * _rms(x)
        return _colsum(dh), _colsum(dh * xn * g), _colsum(dh * xn * (1.0 + sc))

    n_ctx = n_all - n_lat
    dcsh1, dcsc1, dg_n1c = _ew(
        f_norm1_ctx_bwd, (n_ctx,), [(xx, _rows(D_MODEL, 0, n_lat)), (dhh, _rows(D_MODEL, 0, n_lat)),
                                    (W["norm1_g"], vec(D_MODEL)), (csc1, vec(D_MODEL))], [acc_out(D_MODEL)] * 3,
        "norm1_ctx_bwd")

    big = {
        "w_in": _w_in_from_p(d_w_in_p),
        "w_uq": d_w_uq_p.reshape(Q_RANK, N_HEADS, HEAD_PAD)[:, :, :QK_DIM].reshape(Q_RANK, N_HEADS * QK_DIM),
        "w_ukv": d_w_ukv,
        "w_attn_out": d_w_ao_p.reshape(N_HEADS, HEAD_PAD, D_MODEL)[:, 64:, :].reshape(N_HEADS * 64, D_MODEL),
        "w_conv_out": d_w_co, "w_o": d_w_o, "w_up": d_w_up, "w_down": d_w_down,
    }
    zero = jnp.zeros((1, 4 * D_MODEL), F32)
    small = {
        "dmod_lat": jnp.concatenate([dsh1, dsc1, dg1, dsh2, dsc2, dg2], axis=1),
        "dmod_ctx": jnp.concatenate([dcsh1, dcsc1, zero], axis=1),
        "norm1_g": dg_n1 + dg_n1c, "norm2_g": dg_n2, "final_g": dg_f, "q_norm_g": dg_q, "kv_norm_g": dg_kv,
        "conv_b": d_conv_b, "conv_w": d_conv_w.reshape(1, -1), "ffn_conv_b": d_ffn_conv_b,
        "ffn_conv_w": d_ffn_conv_w.reshape(1, -1),
    }
    return grad_x, loss, big, small


SMALL = (("dmod_lat", 6144), ("dmod_ctx", 6144), ("norm1_g", 1024), ("norm2_g", 1024), ("final_g", 1024),
         ("q_norm_g", 384), ("kv_norm_g", 256), ("conv_b", 512), ("conv_w", 1536), ("ffn_conv_b", 5632),
         ("ffn_conv_w", 16896))
SMALL_ROWS = 320


def _adamw(w, g, m, v, name):
    R, C = w.shape
    tr = 8 if R % 8 == 0 else R
    for t in range(8, R + 1, 8):
        if R % t == 0 and t * C * 4 <= (1 << 20):
            tr = t
    c1, c2 = 1.0 - ADAM_B1 ** ADAM_STEP, 1.0 - ADAM_B2 ** ADAM_STEP

    def fn(ids, w_, g_, m_, v_):
        m2 = ADAM_B1 * m_ + (1.0 - ADAM_B1) * g_
        v2 = ADAM_B2 * v_ + (1.0 - ADAM_B2) * (g_ * g_)
        delta = -ADAM_LR * ((m2 / c1) / (jnp.sqrt(v2 / c2) + ADAM_EPS) + ADAM_WD * w_)
        return delta, m2, v2

    spec = pl.BlockSpec((tr, C), lambda i: (i, 0))
    return _ew(fn, (R // tr,), [(w, spec), (g, spec), (m, spec), (v, spec)], [((R, C), F32, spec, None)] * 3, name)


def kernel(x, c, ctx, c_ctx, w_ada, b_ada, norm1_g, w_in, q_norm_g, kv_norm_g, w_uq, w_ukv, conv_w, conv_b, w_attn_out, w_conv_out, w_o, norm2_g, w_up, ffn_conv_w, ffn_conv_b, w_down, final_g, loss_target, m_c_ctx, m_w_ada, m_b_ada, m_norm1_g, m_w_in, m_q_norm_g, m_kv_norm_g, m_w_uq, m_w_ukv, m_conv_w, m_conv_b, m_w_attn_out, m_w_conv_out, m_w_o, m_norm2_g, m_w_up, m_ffn_conv_w, m_ffn_conv_b, m_w_down, m_final_g, v_c_ctx, v_w_ada, v_b_ada, v_norm1_g, v_w_in, v_q_norm_g, v_kv_norm_g, v_w_uq, v_w_ukv, v_conv_w, v_conv_b, v_w_attn_out, v_w_conv_out, v_w_o, v_norm2_g, v_w_up, v_ffn_conv_w, v_ffn_conv_b, v_w_down, v_final_g):
    mx, my, mc = lax.axis_index("x"), lax.axis_index("y"), lax.axis_index("c")
    chip = 2 * mx + my
    dev = 4 * mx + 2 * my + mc
    T, Tc = x.shape[1], ctx.shape[1]
    TT = T + Tc
    shards = {"w_in": w_in[0], "w_uq": w_uq[0], "w_ukv": w_ukv[0], "w_attn_out": w_attn_out[0],
              "w_conv_out": w_conv_out[0], "w_o": w_o[0], "w_up": w_up[0], "w_down": w_down[0]}

    conv_sh = jnp.concatenate([conv_w[0], ffn_conv_w[0]], axis=1)
    pay1 = jnp.concatenate([jnp.pad(c, ((0, 7), (0, 0))), jnp.pad(conv_sh, ((0, 5), (0, 0)))], axis=1)
    got1 = _allgather8(pay1, "gather_cond", in_vmem=True).reshape(8, 8, 2560)
    c_all = got1[:, 0, :D_MODEL]
    conv_all = got1[0::2, :3, D_MODEL:]
    conv_w_full = _cols_from_shards(conv_all[:, :, :128])
    ffn_conv_w_full = _cols_from_shards(conv_all[:, :, 128:])

    flat = jnp.concatenate([shards[n].reshape(-1) for n, _ in BIG])
    flat = jnp.pad(flat, (0, 2 * PACK_HALF - BIG_ELEMS)).reshape(2, PACK_ROWS, 1024)
    my_half = lax.dynamic_index_in_dim(flat, mc, 0, keepdims=False).astype(BF16)
    gathered = _allgather8(my_half, "gather_weights", in_vmem=False).reshape(4, 2 * PACK_HALF)
    full, off = {}, 0
    for n, (r, cdim) in BIG:
        full[n] = gathered[:, off:off + r * cdim].reshape(4, r, cdim)
        off += r * cdim
    wuq = _cols_from_shards(full["w_uq"]).reshape(Q_RANK, N_HEADS, QK_DIM)
    wao = _cols_from_shards(full["w_attn_out"]).reshape(N_HEADS, 64, D_MODEL)
    W = {
        "w_in": _w_in_to_p(_cols_from_shards(full["w_in"])),
        "w_uq": jnp.pad(wuq, ((0, 0), (0, 0), (0, HEAD_PAD - QK_DIM))).reshape(Q_RANK, N_HEADS * HEAD_PAD),
        "w_ukv": _cols_from_shards(full["w_ukv"]),
        "w_attn_out": jnp.pad(wao, ((0, 0), (64, 0), (0, 0))).reshape(N_HEADS * HEAD_PAD, D_MODEL),
        "w_conv_out": _cols_from_shards(full["w_conv_out"]),
        "w_o": full["w_o"].reshape(D_MODEL, D_MODEL),
        "w_up": _cols_from_shards(full["w_up"]),
        "w_down": full["w_down"].reshape(D_FF, D_MODEL),
        "norm1_g": norm1_g, "norm2_g": norm2_g, "final_g": final_g.reshape(1, D_MODEL), "q_norm_g": q_norm_g,
        "kv_norm_g": kv_norm_g, "conv_w": conv_w_full, "conv_b": conv_b, "ffn_conv_w": ffn_conv_w_full,
        "ffn_conv_b": ffn_conv_b,
    }

    cond = jnp.concatenate([c_all, c_ctx.reshape(1, D_MODEL), jnp.zeros((7, D_MODEL), F32)], axis=0)

    def f_silu(ids, v):
        return (v * _sigmoid(v),)

    (s16,) = _ew(f_silu, (1,), [(cond, _full((16, D_MODEL)))], [((16, D_MODEL), F32, _full((16, D_MODEL)), None)], "silu_cond")
    mod_sh = _mm(s16, w_ada[0], "nn", 16, 1536, D_MODEL, tm=16, tn=768, tk=D_MODEL, name="w_ada_fwd")
    got2 = _allgather8(mod_sh, "gather_mod", in_vmem=True).reshape(4, 2, 16, 1536)[:, 0]
    mod_all = _cols_from_shards(got2) + b_ada
    mod_lat = lax.dynamic_slice_in_dim(mod_all, dev, 1, axis=0)
    mod_ctx = mod_all[8:9]

    xx = jnp.concatenate([x[0], ctx[0]], axis=0)
    grad_x, loss_part, gbig, gsmall = _local_step(xx, loss_target[0], mod_lat, mod_ctx, W)
    loss = lax.psum(loss_part[0, 0], ("x", "y", "c"))

    pay3 = jnp.concatenate([gsmall[n].reshape(-1) for n, _ in SMALL])
    pay3 = jnp.pad(pay3, (0, SMALL_ROWS * 128 - pay3.shape[0])).reshape(SMALL_ROWS, 128)
    got3 = _allgather8(pay3, "gather_small", in_vmem=True)

    def f_sum8(ids, a):
        s = a[0:SMALL_ROWS]
        for d in range(1, 8):
            s = s + a[d * SMALL_ROWS:(d + 1) * SMALL_ROWS]
        return (s,)

    (vsum,) = _ew(f_sum8, (1,), [(got3, _full((8 * SMALL_ROWS, 128)))],
                  [((SMALL_ROWS, 128), F32, _full((SMALL_ROWS, 128)), None)], "sum_small")
    vflat = vsum.reshape(-1)
    gvec, off = {}, 0
    for n, size in SMALL:
        gvec[n] = vflat[off:off + size]
        off += size
    dmod_rows = got3.reshape(8, SMALL_ROWS * 128)[:, :6 * D_MODEL]
    dm16 = jnp.concatenate([dmod_rows, gvec["dmod_ctx"].reshape(1, -1), jnp.zeros((7, 6 * D_MODEL), F32)], axis=0)

    def f_colsum(ids, a):
        return (_colsum(a),)

    (g_b_ada,) = _ew(f_colsum, (1,), [(dm16, _full((16, 6 * D_MODEL)))],
                     [((1, 6 * D_MODEL), F32, _full((1, 6 * D_MODEL)), None)], "b_ada_grad")
    dm_sh = lax.dynamic_slice_in_dim(dm16, chip * 1536, 1536, axis=1)
    g_w_ada = _mm(s16, dm_sh, "tn", D_MODEL, 1536, 16, tm=512, tn=768, tk=16, name="w_ada_dw")
    dcond_part = _mm(dm_sh, w_ada[0], "nt", 16, D_MODEL, 1536, tm=16, tn=512, tk=1536, name="w_ada_dx")
    got4 = _allgather8(dcond_part[8:16], "gather_dcond", in_vmem=True).reshape(4, 2, 8, D_MODEL)[:, 0, 0]

    def f_c_ctx(ids, parts, cc):
        s = _sigmoid(cc)
        d = parts[0:1] + parts[1:2] + parts[2:3] + parts[3:4]
        return (d * s * (1.0 + cc * (1.0 - s)),)

    (g_c_ctx,) = _ew(f_c_ctx, (1,), [(got4, _full((4, D_MODEL))), (c_ctx.reshape(1, D_MODEL), _full((1, D_MODEL)))],
                     [((1, D_MODEL), F32, _full((1, D_MODEL)), None)], "c_ctx_grad")

    parts = []
    for n, (r, cdim) in BIG:
        g = gbig[n]
        parts.append(g.reshape(4, r * cdim) if n in ("w_o", "w_down") else _cols_to_shards(g))
    gflat = jnp.pad(jnp.concatenate(parts, axis=1), ((0, 0), (0, 2 * PACK_HALF - BIG_ELEMS)))
    gflat = gflat.astype(BF16).reshape(4, 2, PACK_ROWS, 1024)
    keep = lax.dynamic_index_in_dim(gflat, mc, 1, keepdims=False)
    give = lax.dynamic_index_in_dim(gflat, 1 - mc, 1, keepdims=False)
    from_sibling = _pair_send(give, "rs_pair")

    blk = pl.BlockSpec((None, 384, 1024), lambda j, i: (j, i, 0))

    def f_add2(ids, a, b):
        return (a.astype(F32) + b.astype(F32),)

    (pair_sum,) = _ew(f_add2, (4, PACK_ROWS // 384), [(keep, blk), (from_sibling, blk)],
                      [((4, PACK_ROWS, 1024), BF16, blk, None)], "rs_pair_add")
    land = _chip_scatter(pair_sum, "rs_chips")
    blk4 = pl.BlockSpec((4, 384, 1024), lambda i: (0, i, 0))

    def f_add4(ids, a):
        return (((a[0].astype(F32) + a[1].astype(F32)) + a[2].astype(F32)) + a[3].astype(F32),)

    (half_sum,) = _ew(f_add4, (PACK_ROWS // 384,), [(land, blk4)],
                      [((PACK_ROWS, 1024), F32, pl.BlockSpec((384, 1024), lambda i: (i, 0)), None)], "rs_chip_add")
    red = _pair_exchange(half_sum, "rs_pair_back").reshape(-1)
    gw, off = {}, 0
    for n, (r, cdim) in BIG:
        gw[n] = red[off:off + r * cdim].reshape(r, cdim)
        off += r * cdim
    gw["w_ada"] = g_w_ada

    moments = {"w_ada": (w_ada, m_w_ada, v_w_ada), "w_in": (w_in, m_w_in, v_w_in), "w_uq": (w_uq, m_w_uq, v_w_uq),
               "w_ukv": (w_ukv, m_w_ukv, v_w_ukv), "w_attn_out": (w_attn_out, m_w_attn_out, v_w_attn_out),
               "w_conv_out": (w_conv_out, m_w_conv_out, v_w_conv_out), "w_o": (w_o, m_w_o, v_w_o),
               "w_up": (w_up, m_w_up, v_w_up), "w_down": (w_down, m_w_down, v_w_down)}
    grads, deltas, new_m, new_v = {}, {}, {}, {}
    for n, (w_, m_, v_) in moments.items():
        d_, m2, v2 = _adamw(w_[0], gw[n], m_[0], v_[0], "adamw_" + n)
        grads[n], deltas[n], new_m[n], new_v[n] = gw[n][None], d_[None], m2[None], v2[None]

    conv_w_g = lax.dynamic_slice_in_dim(gvec["conv_w"].reshape(3, CONV_DIM), chip * 128, 128, axis=1)
    ffn_conv_w_g = lax.dynamic_slice_in_dim(gvec["ffn_conv_w"].reshape(3, 2 * D_FF), chip * 1408, 1408, axis=1)
    vec_params = (("c_ctx", c_ctx, m_c_ctx, v_c_ctx, g_c_ctx), ("b_ada", b_ada, m_b_ada, v_b_ada, g_b_ada),
                  ("norm1_g", norm1_g, m_norm1_g, v_norm1_g, gvec["norm1_g"]),
                  ("q_norm_g", q_norm_g, m_q_norm_g, v_q_norm_g, gvec["q_norm_g"]),
                  ("kv_norm_g", kv_norm_g, m_kv_norm_g, v_kv_norm_g, gvec["kv_norm_g"]),
                  ("conv_w", conv_w, m_conv_w, v_conv_w, conv_w_g), ("conv_b", conv_b, m_conv_b, v_conv_b, gvec["conv_b"]),
                  ("norm2_g", norm2_g, m_norm2_g, v_norm2_g, gvec["norm2_g"]),
                  ("ffn_conv_w", ffn_conv_w, m_ffn_conv_w, v_ffn_conv_w, ffn_conv_w_g),
                  ("ffn_conv_b", ffn_conv_b, m_ffn_conv_b, v_ffn_conv_b, gvec["ffn_conv_b"]),
                  ("final_g", final_g, m_final_g, v_final_g, gvec["final_g"]))
    total = sum(p[1].size for p in vec_params)
    rows_v = -(-total // 1024) * 8

    def packv(idx):
        flat_v = jnp.concatenate([p[idx].reshape(-1) for p in vec_params])
        return jnp.pad(flat_v, (0, rows_v * 128 - total)).reshape(rows_v, 128)

    vd, vm, vv = _adamw(packv(1), packv(4), packv(2), packv(3), "adamw_vectors")
    off = 0
    for p in vec_params:
        n, shape, size = p[0], p[1].shape, p[1].size
        grads[n] = p[4].reshape(shape)
        deltas[n] = vd.reshape(-1)[off:off + size].reshape(shape)
        new_m[n] = vm.reshape(-1)[off:off + size].reshape(shape)
        new_v[n] = vv.reshape(-1)[off:off + size].reshape(shape)
        off += size

    order = ("c_ctx", "w_ada", "b_ada", "norm1_g", "w_in", "q_norm_g", "kv_norm_g", "w_uq", "w_ukv", "conv_w", "conv_b",
             "w_attn_out", "w_conv_out", "w_o", "norm2_g", "w_up", "ffn_conv_w", "ffn_conv_b", "w_down", "final_g")
    return (loss, grad_x[None], *[grads[n] for n in order], *[deltas[n] for n in order],
            *[new_m[n] for n in order], *[new_v[n] for n in order])
```

```python
import functools

import jax
import jax.numpy as jnp
from jax import lax
from jax.experimental import pallas as pl
from jax.experimental.pallas import tpu as pltpu

F32, BF16 = jnp.float32, jnp.bfloat16
MESH = pl.DeviceIdType.MESH

D_MODEL = 1024
N_HEADS = 8
HEAD_PAD = 128
QK_DIM = 96
Q_RANK, KV_RANK = 384, 256
CONV_DIM = 512
D_FF = 2816
GRID_W = 64
ROPE_THETA = 10000.0
EPS = 1e-6
GA0, GC0, CX0, CB0, CC0, KV0, Q0, KR0, P_COLS = 0, 1024, 2048, 2560, 3072, 3584, 3840, 4224, 4352
ROW_TILE = 256
VMEM_LIMIT_BYTES = 48 * 1024 * 1024

ADAM_LR, ADAM_B1, ADAM_B2, ADAM_EPS, ADAM_WD, ADAM_STEP = 0.001, 0.9, 0.999, 1e-08, 0.01, 10

BIG = (("w_in", (1024, 1064)), ("w_uq", (384, 192)), ("w_ukv", (256, 256)), ("w_attn_out", (512, 256)),
       ("w_conv_out", (512, 256)), ("w_o", (256, 1024)), ("w_up", (1024, 1408)), ("w_down", (704, 1024)))

NN = (((1,), (0,)), ((), ()))
NT = (((1,), (1,)), ((), ()))
TN = (((0,), (0,)), ((), ()))


def _cp(sem):
    return pltpu.CompilerParams(dimension_semantics=sem, vmem_limit_bytes=VMEM_LIMIT_BYTES)


def _pick(n, prefs):
    for p in prefs:
        if n % p == 0:
            return p
    return n


def _mm(a, b, mode, M, N, K, *, tm, tn, tk, name, out_dtype=F32, a_spec=None, b_spec=None, o_spec=None,
        out_shape=None):
    assert M % tm == 0 and N % tn == 0 and K % tk == 0, (name, M, N, K, tm, tn, tk)
    nk = K // tk
    dims = {"nn": NN, "nt": NT, "tn": TN}[mode]
    if a_spec is None:
        a_spec = (pl.BlockSpec((tk, tm), lambda i, j, k: (k, i)) if mode == "tn"
                  else pl.BlockSpec((tm, tk), lambda i, j, k: (i, k)))
    if b_spec is None:
        b_spec = (pl.BlockSpec((tn, tk), lambda i, j, k: (j, k)) if mode == "nt"
                  else pl.BlockSpec((tk, tn), lambda i, j, k: (k, j)))
    if o_spec is None:
        o_spec = pl.BlockSpec((tm, tn), lambda i, j, k: (i, j))
    if out_shape is None:
        out_shape = (M, N)

    def body(a_ref, b_ref, o_ref, acc_ref):
        k = pl.program_id(2)
        part = lax.dot_general(a_ref[...].astype(BF16), b_ref[...].astype(BF16), dims, preferred_element_type=F32)

        @pl.when(k == 0)
        def _():
            acc_ref[...] = part

        @pl.when(k > 0)
        def _():
            acc_ref[...] += part

        @pl.when(k == nk - 1)
        def _():
            o_ref[...] = acc_ref[...].astype(o_ref.dtype)

    return pl.pallas_call(
        body, grid=(M // tm, N // tn, nk), in_specs=[a_spec, b_spec], out_specs=o_spec,
        out_shape=jax.ShapeDtypeStruct(out_shape, out_dtype), scratch_shapes=[pltpu.VMEM((tm, tn), F32)],
        compiler_params=_cp(("parallel", "parallel", "arbitrary")), name=name)(a, b)


def _ew(fn, grid, ins, outs, name):
    n_in = len(ins)

    def store(ref, val, acc, ids):
        if isinstance(val, (list, tuple)):
            for h, v in enumerate(val):
                ref[h] = v.astype(ref.dtype)
            return
        if acc is None:
            ref[...] = val.astype(ref.dtype)
            return

        @pl.when(ids[acc] == 0)
        def _():
            ref[...] = val.astype(ref.dtype)

        @pl.when(ids[acc] > 0)
        def _():
            ref[...] += val.astype(ref.dtype)

    def body(*refs):
        ids = tuple(pl.program_id(a) for a in range(len(grid)))
        vals = fn(ids, *[r[...] for r in refs[:n_in]])
        for ref, val, (_, _, _, acc) in zip(refs[n_in:], vals, outs):
            store(ref, val, acc, ids)

    acc_axes = {o[3] for o in outs if o[3] is not None}
    sem = tuple("arbitrary" if a in acc_axes else "parallel" for a in range(len(grid)))
    res = pl.pallas_call(
        body, grid=grid, in_specs=[s for _, s in ins], out_specs=[o[2] for o in outs],
        out_shape=[jax.ShapeDtypeStruct(o[0], o[1]) for o in outs], compiler_params=_cp(sem), name=name,
    )(*[a for a, _ in ins])
    return res


def _rows(width, cblk=0, roff=0, tr=ROW_TILE):
    return pl.BlockSpec((tr, width), lambda i: (i + roff, cblk))


def _full(shape):
    nd = len(shape)
    return pl.BlockSpec(shape, lambda *_: (0,) * nd)


def _sigmoid(x):
    return 1.0 / (1.0 + jnp.exp(-x))


def _rms(x):
    return lax.rsqrt(jnp.mean(x * x, axis=-1, keepdims=True) + EPS)


def _rms_bwd(dn, xn, r):
    return r * (dn - xn * jnp.mean(dn * xn, axis=-1, keepdims=True))


def _colsum(x):
    return jnp.sum(x, axis=0, keepdims=True)


def _shift_prev(x):
    rows = lax.broadcasted_iota(jnp.int32, x.shape, 0)
    return jnp.where(rows == 0, 0.0, pltpu.roll(x, 1, 0))


def _shift_next(x):
    rows = lax.broadcasted_iota(jnp.int32, x.shape, 0)
    return jnp.where(rows == x.shape[0] - 1, 0.0, pltpu.roll(x, x.shape[0] - 1, 0))


def _conv(x, w, b):
    return b + _shift_prev(x) * w[0:1] + x * w[1:2] + _shift_next(x) * w[2:3]


def _conv_bwd_x(dy, w):
    return _shift_next(dy) * w[0:1] + dy * w[1:2] + _shift_prev(dy) * w[2:3]


def _conv_bwd_w(dy, x):
    return _colsum(dy * _shift_prev(x)), _colsum(dy * x), _colsum(dy * _shift_next(x))


def _rope(x, cos, sin_lo, sin_hi):
    return x * cos + pltpu.roll(x, HEAD_PAD - 8, 1) * sin_lo + pltpu.roll(x, 8, 1) * sin_hi


ATTN_SCALE = QK_DIM ** -0.5


def _attn_fwd(qr, kc, vp, T, TT):
    tq = ROW_TILE

    def body(q_ref, k_ref, v_ref, o_ref, l_ref):
        s = lax.dot_general(q_ref[...], k_ref[...], NT, preferred_element_type=F32) * ATTN_SCALE
        m = jnp.max(s, axis=-1, keepdims=True)
        p = jnp.exp(s - m)
        l = jnp.sum(p, axis=-1, keepdims=True)
        o = lax.dot_general(p.astype(BF16), v_ref[...], NN, preferred_element_type=F32)
        o_ref[...] = o / l
        l_ref[...] = m + jnp.log(l)

    qspec = pl.BlockSpec((None, tq, HEAD_PAD), lambda h, i: (h, i, 0))
    kspec = pl.BlockSpec((None, TT, HEAD_PAD), lambda h, i: (h, 0, 0))
    return pl.pallas_call(
        body, grid=(N_HEADS, T // tq), in_specs=[qspec, kspec, kspec],
        out_specs=[qspec, pl.BlockSpec((None, tq, 1), lambda h, i: (h, i, 0))],
        out_shape=[jax.ShapeDtypeStruct((N_HEADS, T, HEAD_PAD), F32), jax.ShapeDtypeStruct((N_HEADS, T, 1), F32)],
        compiler_params=_cp(("parallel", "parallel")), name="attn_fwd")(qr, kc, vp)


def _attn_bwd(qr, kc, vp, o, do, lse, T, TT):
    tq = ROW_TILE
    nq = T // tq

    def body(q_ref, k_ref, v_ref, o_ref, do_ref, l_ref, dq_ref, dk_ref, dv_ref):
        i = pl.program_id(1)

        @pl.when(i == 0)
        def _():
            dk_ref[...] = jnp.zeros_like(dk_ref)
            dv_ref[...] = jnp.zeros_like(dv_ref)

        @pl.when(i < nq)
        def _():
            q, k, v, d_o = q_ref[...], k_ref[...], v_ref[...], do_ref[...]
            s = lax.dot_general(q, k, NT, preferred_element_type=F32) * ATTN_SCALE
            p = jnp.exp(s - l_ref[...])
            dob = d_o.astype(BF16)
            dp = lax.dot_general(dob, v, NT, preferred_element_type=F32)
            dd = jnp.sum(d_o * o_ref[...], axis=-1, keepdims=True)
            ds = (p * (dp - dd) * ATTN_SCALE).astype(BF16)
            dq_ref[...] = lax.dot_general(ds, k, NN, preferred_element_type=F32)
            dk_ref[...] += lax.dot_general(ds, q, TN, preferred_element_type=F32)
            dv_ref[...] += lax.dot_general(p.astype(BF16), dob, TN, preferred_element_type=F32)

        @pl.when(i == nq)
        def _():
            dq_ref[...] = jnp.zeros_like(dq_ref)

    qspec = pl.BlockSpec((None, tq, HEAD_PAD), lambda h, i: (h, i, 0))
    lat = pl.BlockSpec((None, tq, HEAD_PAD), lambda h, i: (h, jnp.minimum(i, nq - 1), 0))
    lspec = pl.BlockSpec((None, tq, 1), lambda h, i: (h, jnp.minimum(i, nq - 1), 0))
    kspec = pl.BlockSpec((None, TT, HEAD_PAD), lambda h, i: (h, 0, 0))
    big = jax.ShapeDtypeStruct((N_HEADS, TT, HEAD_PAD), F32)
    return pl.pallas_call(
        body, grid=(N_HEADS, TT // tq), in_specs=[qspec, kspec, kspec, lat, lat, lspec],
        out_specs=[qspec, kspec, kspec], out_shape=[big, big, big],
        compiler_params=_cp(("parallel", "arbitrary")), name="attn_bwd")(qr, kc, vp, o, do, lse)


def _allgather8(x, name, in_vmem):
    m_per, n = x.shape

    def body(x_ref, out_ref, send_sems, recv_sems, local_sem):
        mx, my, mc = lax.axis_index("x"), lax.axis_index("y"), lax.axis_index("c")
        me, sibling = (mx, my, mc), (mx, my, 1 - mc)
        chips = [(1 - mx, my), (mx, 1 - my), (1 - mx, 1 - my)]

        def rows(px, py, pc):
            return out_ref.at[pl.ds((4 * px + 2 * py + pc) * m_per, m_per), :]

        def copy(k, block, to, src=None):
            return pltpu.make_async_remote_copy(
                src_ref=rows(*block) if src is None else src, dst_ref=rows(*block),
                send_sem=send_sems.at[k], recv_sem=recv_sems.at[k], device_id=to, device_id_type=MESH)

        mine = pltpu.make_async_copy(x_ref, rows(*me), local_sem)
        mine.start()
        first = [copy(0, me, sibling, src=x_ref)]
        first += [copy(1 + j, me, (*chip, mc), src=x_ref) for j, chip in enumerate(chips)]
        for cp in first:
            cp.start()
        passed = [copy(4 + j, (*chip, mc), sibling) for j, chip in enumerate(chips)]
        for j, chip in enumerate(chips):
            copy(1 + j, (*chip, mc), me).wait_recv()
            passed[j].start()
        copy(0, sibling, me).wait_recv()
        for j, chip in enumerate(chips):
            copy(4 + j, (*chip, 1 - mc), me).wait_recv()
        for cp in first + passed:
            cp.wait_send()
        mine.wait()

    space = pltpu.VMEM if in_vmem else pl.ANY
    return pl.pallas_call(
        body, out_shape=jax.ShapeDtypeStruct((8 * m_per, n), x.dtype),
        in_specs=[pl.BlockSpec(memory_space=space)], out_specs=pl.BlockSpec(memory_space=space),
        scratch_shapes=[pltpu.SemaphoreType.DMA((7,)), pltpu.SemaphoreType.DMA((7,)), pltpu.SemaphoreType.DMA],
        name=name)(x)


def _hbm_specs(n):
    return [pl.BlockSpec(memory_space=pl.ANY)] * n


def _gather_weights(shards):
    n = len(shards)
    halves = [s.shape[0] // 2 for s in shards]

    def body(*refs):
        ins, outs = refs[:n], refs[n:2 * n]
        send_sems, recv_sems, local_sems = refs[2 * n:]
        mx, my, mc = lax.axis_index("x"), lax.axis_index("y"), lax.axis_index("c")
        j_me = 2 * mx + my
        chips = [(1 - mx, my), (mx, 1 - my), (1 - mx, 1 - my)]

        def half(w, chip_idx, hc):
            return outs[w].at[chip_idx, pl.ds(hc * halves[w], halves[w]), :]

        def copy(w, k, src, dst, to):
            return pltpu.make_async_remote_copy(src_ref=src, dst_ref=dst, send_sem=send_sems.at[w, k],
                                                recv_sem=recv_sems.at[w, k], device_id=to, device_id_type=MESH)

        pending = []
        for w in range(n):
            cp = pltpu.make_async_copy(ins[w], outs[w].at[j_me], local_sems.at[w])
            cp.start()
            pending.append(cp)
        sends = []
        for k, (px, py) in enumerate(chips):
            for w in range(n):
                cp = copy(w, k, ins[w].at[pl.ds(mc * halves[w], halves[w]), :], half(w, j_me, mc), (px, py, mc))
                cp.start()
                sends.append(cp)
        for k, (px, py) in enumerate(chips):
            for w in range(n):
                got = half(w, 2 * px + py, mc)
                copy(w, k, got, got, (px, py, mc)).wait_recv()
                cp = copy(w, 3 + k, got, got, (mx, my, 1 - mc))
                cp.start()
                sends.append(cp)
        for k, (px, py) in enumerate(chips):
            for w in range(n):
                got = half(w, 2 * px + py, 1 - mc)
                copy(w, 3 + k, got, got, (mx, my, 1 - mc)).wait_recv()
        for cp in sends:
            cp.wait_send()
        for cp in pending:
            cp.wait()

    return pl.pallas_call(
        body, out_shape=[jax.ShapeDtypeStruct((4,) + s.shape, s.dtype) for s in shards],
        in_specs=_hbm_specs(n), out_specs=_hbm_specs(n),
        scratch_shapes=[pltpu.SemaphoreType.DMA((n, 6)), pltpu.SemaphoreType.DMA((n, 6)), pltpu.SemaphoreType.DMA((n,))],
        name="gather_weights")(*shards)


def _rs_pair(gs):
    n = len(gs)
    halves = [g.shape[1] // 2 for g in gs]

    def body(*refs):
        ins, keeps, lands = refs[:n], refs[n:2 * n], refs[2 * n:3 * n]
        send_sems, recv_sems, local_sems = refs[3 * n:]
        mx, my, mc = lax.axis_index("x"), lax.axis_index("y"), lax.axis_index("c")
        copies = []
        for w in range(n):
            h = halves[w]
            loc = pltpu.make_async_copy(ins[w].at[:, pl.ds(mc * h, h), :], keeps[w], local_sems.at[w])
            loc.start()
            cp = pltpu.make_async_remote_copy(
                src_ref=ins[w].at[:, pl.ds((1 - mc) * h, h), :], dst_ref=lands[w], send_sem=send_sems.at[w],
                recv_sem=recv_sems.at[w], device_id=(mx, my, 1 - mc), device_id_type=MESH)
            cp.start()
            copies += [loc, cp]
        for cp in copies:
            cp.wait()

    half_shapes = [jax.ShapeDtypeStruct((4, h, g.shape[2]), g.dtype) for g, h in zip(gs, halves)]
    res = pl.pallas_call(
        body, out_shape=half_shapes + half_shapes, in_specs=_hbm_specs(n), out_specs=_hbm_specs(2 * n),
        scratch_shapes=[pltpu.SemaphoreType.DMA((n,)), pltpu.SemaphoreType.DMA((n,)), pltpu.SemaphoreType.DMA((n,))],
        name="rs_pair")(*gs)
    return res[:n], res[n:]


def _rs_chips(parts):
    n = len(parts)

    def body(*refs):
        ins, lands = refs[:n], refs[n:2 * n]
        send_sems, recv_sems, local_sems = refs[2 * n:]
        mx, my, mc = lax.axis_index("x"), lax.axis_index("y"), lax.axis_index("c")
        j_me = 2 * mx + my
        copies = []
        for w in range(n):
            loc = pltpu.make_async_copy(ins[w].at[j_me], lands[w].at[j_me], local_sems.at[w])
            loc.start()
            copies.append(loc)
        for k, (px, py) in enumerate([(1 - mx, my), (mx, 1 - my), (1 - mx, 1 - my)]):
            for w in range(n):
                cp = pltpu.make_async_remote_copy(
                    src_ref=ins[w].at[2 * px + py], dst_ref=lands[w].at[j_me], send_sem=send_sems.at[w, k],
                    recv_sem=recv_sems.at[w, k], device_id=(px, py, mc), device_id_type=MESH)
                cp.start()
                copies.append(cp)
        for cp in copies:
            cp.wait()

    return pl.pallas_call(
        body, out_shape=[jax.ShapeDtypeStruct(p.shape, p.dtype) for p in parts],
        in_specs=_hbm_specs(n), out_specs=_hbm_specs(n),
        scratch_shapes=[pltpu.SemaphoreType.DMA((n, 3)), pltpu.SemaphoreType.DMA((n, 3)), pltpu.SemaphoreType.DMA((n,))],
        name="rs_chips")(*parts)


def _rs_pair_back(rs):
    n = len(rs)

    def body(*refs):
        ins, outs = refs[:n], refs[n:2 * n]
        send_sems, recv_sems, local_sems = refs[2 * n:]
        mx, my, mc = lax.axis_index("x"), lax.axis_index("y"), lax.axis_index("c")
        copies = []
        for w in range(n):
            h = rs[w].shape[0]
            mine = outs[w].at[pl.ds(mc * h, h), :]
            loc = pltpu.make_async_copy(ins[w], mine, local_sems.at[w])
            loc.start()
            cp = pltpu.make_async_remote_copy(src_ref=ins[w], dst_ref=mine, send_sem=send_sems.at[w],
                                              recv_sem=recv_sems.at[w], device_id=(mx, my, 1 - mc), device_id_type=MESH)
            cp.start()
            copies += [loc, cp]
        for cp in copies:
            cp.wait()

    return pl.pallas_call(
        body, out_shape=[jax.ShapeDtypeStruct((2 * r.shape[0], r.shape[1]), r.dtype) for r in rs],
        in_specs=_hbm_specs(n), out_specs=_hbm_specs(n),
        scratch_shapes=[pltpu.SemaphoreType.DMA((n,)), pltpu.SemaphoreType.DMA((n,)), pltpu.SemaphoreType.DMA((n,))],
        name="rs_pair_back")(*rs)


def _tile_rows(h, c, itemsize, mult):
    best = h
    for t in range(mult, h + 1, mult):
        if h % t == 0 and t * c * itemsize <= (1 << 20):
            best = t
    return best


def _add_pair(a, b, name):
    _, h, c = a.shape
    t = _tile_rows(h, c, 2, 16)
    blk = pl.BlockSpec((None, t, c), lambda j, i: (j, i, 0))
    return _ew(lambda ids, u, v: (u.astype(F32) + v.astype(F32),), (4, h // t), [(a, blk), (b, blk)],
               [(a.shape, BF16, blk, None)], name)[0]


def _add_chips(land, name):
    _, h, c = land.shape
    t = _tile_rows(h, c, 4, 16)

    def fn(ids, a):
        return (((a[0].astype(F32) + a[1].astype(F32)) + a[2].astype(F32)) + a[3].astype(F32),)

    return _ew(fn, (h // t,), [(land, pl.BlockSpec((4, t, c), lambda i: (0, i, 0)))],
               [((h, c), F32, pl.BlockSpec((t, c), lambda i: (i, 0)), None)], name)[0]


W_IN_SEGMENTS = ((0, 256, KV0), (256, 288, KR0 + 64), (288, 672, Q0), (672, 1184, CX0), (1184, 1696, CB0),
                 (1696, 2208, CC0), (2208, 3232, GA0), (3232, 4256, GC0))
W_IN_SHARD = 1064


def _w_in_p_from_shards(s):
    pieces = []
    for o0, o1, p0 in sorted(W_IN_SEGMENTS, key=lambda t: t[2]):
        if p0 == KR0 + 64:
            pieces.append(jnp.zeros((s.shape[1], 64), s.dtype))
        for j in range(4):
            lo, hi = max(o0, j * W_IN_SHARD), min(o1, (j + 1) * W_IN_SHARD)
            if lo < hi:
                pieces.append(s[j][:, lo - j * W_IN_SHARD:hi - j * W_IN_SHARD])
    pieces.append(jnp.zeros((s.shape[1], 32), s.dtype))
    return jnp.concatenate(pieces, axis=1)


def _w_in_shards_from_p(g):
    shards = []
    for j in range(4):
        pieces = []
        for o0, o1, p0 in W_IN_SEGMENTS:
            lo, hi = max(o0, j * W_IN_SHARD), min(o1, (j + 1) * W_IN_SHARD)
            if lo < hi:
                pieces.append(g[:, p0 + lo - o0:p0 + hi - o0])
        shards.append(jnp.concatenate(pieces, axis=1))
    return jnp.stack(shards, axis=0)


def _cols_from_shards(s):
    return jnp.transpose(s, (1, 0, 2)).reshape(s.shape[1], -1)


def _rope_tables(T, TT, inverse):
    rows = T // GRID_W
    row = jnp.repeat(jnp.arange(rows), GRID_W).astype(F32)
    col = jnp.tile(jnp.arange(GRID_W), rows).astype(F32)
    inv = ROPE_THETA ** (-jnp.arange(0, 16, 2, dtype=F32) / 16)
    ang = jnp.concatenate([row[:, None] * inv, col[:, None] * inv], axis=-1)
    cos, sin = jnp.cos(ang), jnp.sin(ang)
    lane = jnp.arange(32)
    src = (lane // 16) * 8 + lane % 8
    lo = ((lane % 16) // 8 == 0).astype(F32)
    sgn = -1.0 if inverse else 1.0
    cos32 = cos[:, src]
    sin_lo32 = -sgn * sin[:, src] * lo
    sin_hi32 = sgn * sin[:, src] * (1.0 - lo)

    def widen(t32, fill):
        t = jnp.concatenate([jnp.full((T, 64), fill, F32), t32, jnp.full((T, 32), fill, F32)], axis=1)
        return jnp.concatenate([t, jnp.full((TT - T, HEAD_PAD), fill, F32)], axis=0)

    return widen(cos32, 1.0), widen(sin_lo32, 0.0), widen(sin_hi32, 0.0)


def _local_step(xx, tgt, mod_lat, mod_ctx, W):
    TT = xx.shape[0]
    T = tgt.shape[0]
    n_lat, n_all = T // ROW_TILE, TT // ROW_TILE
    sh1, sc1, g1, sh2, sc2, g2 = [mod_lat[:, k * D_MODEL:(k + 1) * D_MODEL] for k in range(6)]
    csh1, csc1 = mod_ctx[:, :D_MODEL], mod_ctx[:, D_MODEL:2 * D_MODEL]
    vec = lambda n: _full((1, n))
    row_out = lambda n, dt, rows=T: ((rows, n), dt, _rows(n), None)
    acc_out = lambda n: ((1, n), F32, _full((1, n)), 0)

    def f_norm1(ids, x, g, a_sh, a_sc, b_sh, b_sc):
        ctx = ids[0] >= n_lat
        sh, sc = jnp.where(ctx, b_sh, a_sh), jnp.where(ctx, b_sc, a_sc)
        return ((x * _rms(x) * g) * (1.0 + sc) + sh,)

    (hh,) = _ew(f_norm1, (n_all,), [(xx, _rows(D_MODEL)), (W["norm1_g"], vec(D_MODEL)), (sh1, vec(D_MODEL)),
                                   (sc1, vec(D_MODEL)), (csh1, vec(D_MODEL)), (csc1, vec(D_MODEL))],
                [row_out(D_MODEL, BF16, TT)], "norm1_fwd")
    tm_all = _pick(TT, (768, 256))
    pp = _mm(hh, W["w_in"], "nn", TT, P_COLS, D_MODEL, tm=tm_all, tn=2176, tk=512, name="w_in_fwd")

    def f_lowrank(ids, ckv, cq, gkv, gq):
        return ckv * _rms(ckv) * gkv, cq * _rms(cq) * gq

    nkv, nq = _ew(f_lowrank, (n_all,), [(pp, _rows(KV_RANK, KV0 // KV_RANK)), (pp, _rows(Q_RANK, Q0 // Q_RANK)),
                                       (W["kv_norm_g"], vec(KV_RANK)), (W["q_norm_g"], vec(Q_RANK))],
                  [row_out(KV_RANK, BF16, TT), row_out(Q_RANK, BF16, TT)], "lowrank_norm_fwd")
    heads_out = pl.BlockSpec((None, tm_all, HEAD_PAD), lambda i, j, k: (j, i, 0))
    heads_shape = (N_HEADS, TT, HEAD_PAD)
    kv = _mm(nkv, W["w_ukv"], "nn", TT, 1024, KV_RANK, tm=tm_all, tn=HEAD_PAD, tk=KV_RANK, name="w_ukv_fwd",
             b_spec=pl.BlockSpec((None, KV_RANK, HEAD_PAD), lambda i, j, k: (j // 2, k, j % 2)),
             o_spec=heads_out, out_shape=heads_shape)
    q_raw = _mm(nq, W["w_uq"], "nn", TT, 1024, Q_RANK, tm=tm_all, tn=HEAD_PAD, tk=Q_RANK, name="w_uq_fwd",
                o_spec=heads_out, out_shape=heads_shape)

    cos_f, slo_f, shi_f = _rope_tables(T, TT, inverse=False)
    cos_b, slo_b, shi_b = _rope_tables(T, TT, inverse=True)
    hspec = pl.BlockSpec((None, ROW_TILE, HEAD_PAD), lambda h, i: (h, i, 0))
    tspec = pl.BlockSpec((ROW_TILE, HEAD_PAD), lambda h, i: (i, 0))

    def f_prep(ids, q, kvh, kr, cos, slo, shi):
        lane = lax.broadcasted_iota(jnp.int32, q.shape, 1)
        return (_rope(q, cos, slo, shi), jnp.where(lane < 64, kvh, _rope(kr, cos, slo, shi)),
                jnp.where(lane >= 64, kvh, 0.0))

    qr, kc, vp = _ew(f_prep, (N_HEADS, n_all),
                     [(q_raw, hspec), (kv, hspec), (pp, pl.BlockSpec((ROW_TILE, HEAD_PAD), lambda h, i: (i, KR0 // 128))),
                      (cos_f, tspec), (slo_f, tspec), (shi_f, tspec)],
                     [(heads_shape, BF16, hspec, None)] * 3, "attn_prep")
    o_pad, lse = _attn_fwd(qr, kc, vp, T, TT)
    tm_lat = _pick(T, (1024, 512, 256))
    kmajor_a = lambda tm: pl.BlockSpec((None, tm, HEAD_PAD), lambda i, j, k: (k, i, 0))
    ya = _mm(o_pad, W["w_attn_out"], "nn", T, D_MODEL, 1024, tm=tm_lat, tn=D_MODEL, tk=HEAD_PAD, name="w_attn_out_fwd",
             a_spec=kmajor_a(tm_lat))

    tc = 256
    colT = lambda blk0: pl.BlockSpec((T, tc), lambda j: (0, blk0 + j))

    def f_conv(ids, xin, cb, cc, w, b):
        return (cb * _conv(cc * xin, w, b),)

    (e,) = _ew(f_conv, (CONV_DIM // tc,),
               [(pp, colT(CX0 // tc)), (pp, colT(CB0 // tc)), (pp, colT(CC0 // tc)),
                (W["conv_w"], pl.BlockSpec((3, tc), lambda j: (0, j))), (W["conv_b"], pl.BlockSpec((1, tc), lambda j: (0, j)))],
               [((T, CONV_DIM), BF16, colT(0), None)], "conv_fwd")
    yc = _mm(e, W["w_conv_out"], "nn", T, D_MODEL, CONV_DIM, tm=tm_lat, tn=256, tk=CONV_DIM, name="w_conv_out_fwd",
             b_spec=pl.BlockSpec((None, CONV_DIM, 256), lambda i, j, k: (j, k, 0)))

    def f_merge(ids, ga, gc, a, c):
        return (_sigmoid(ga) * a + _sigmoid(gc) * c,)

    (mrg,) = _ew(f_merge, (n_lat,), [(pp, _rows(D_MODEL, 0)), (pp, _rows(D_MODEL, 1)), (ya, _rows(D_MODEL)),
                                    (yc, _rows(D_MODEL))], [row_out(D_MODEL, BF16)], "merge_fwd")
    mo = _mm(mrg, W["w_o"], "nn", T, D_MODEL, D_MODEL, tm=tm_lat, tn=D_MODEL, tk=512, name="w_o_fwd")

    def f_norm2(ids, x, m, gate, g, sh, sc):
        x1 = x + gate * m
        return x1, (x1 * _rms(x1) * g) * (1.0 + sc) + sh

    x1, h2 = _ew(f_norm2, (n_lat,), [(xx, _rows(D_MODEL)), (mo, _rows(D_MODEL)), (g1, vec(D_MODEL)),
                                    (W["norm2_g"], vec(D_MODEL)), (sh2, vec(D_MODEL)), (sc2, vec(D_MODEL))],
                 [row_out(D_MODEL, F32), row_out(D_MODEL, BF16)], "norm2_fwd")
    up = _mm(h2, W["w_up"], "nn", T, 2 * D_FF, D_MODEL, tm=tm_lat, tn=1408, tk=512, name="w_up_fwd",
             b_spec=pl.BlockSpec((None, 512, 1408), lambda i, j, k: (j, k, 0)))

    n_ff = D_FF // tc
    ffw = lambda off, n=3: pl.BlockSpec((n, tc), lambda j: (0, j + off))

    def f_ffn(ids, ug, uv, wg, wv, bg, bv):
        gate, val = _conv(ug, wg, bg), _conv(uv, wv, bv)
        return (gate * _sigmoid(gate) * val,)

    (act,) = _ew(f_ffn, (n_ff,), [(up, colT(0)), (up, colT(n_ff)), (W["ffn_conv_w"], ffw(0)), (W["ffn_conv_w"], ffw(n_ff)),
                                 (W["ffn_conv_b"], ffw(0, 1)), (W["ffn_conv_b"], ffw(n_ff, 1))],
                 [((T, D_FF), BF16, colT(0), None)], "ffn_act_fwd")
    f = _mm(act, W["w_down"], "nn", T, D_MODEL, D_FF, tm=tm_lat, tn=D_MODEL, tk=1408, name="w_down_fwd")

    def f_head(ids, x1_, f_, gate, gf, t):
        x2 = x1_ + gate * f_
        r = _rms(x2)
        xn = x2 * r
        err = xn * gf - t
        loss = 0.5 * jnp.sum(jnp.mean(err * err, axis=-1, keepdims=True))
        dy = err * (1.0 / D_MODEL)
        dx2 = _rms_bwd(dy * gf, xn, r)
        return dx2, dx2 * gate, _colsum(dy * xn), _colsum(dx2 * f_), jnp.full((1, 128), loss, F32)

    dx2, df, dg_f, dg2, loss = _ew(
        f_head, (n_lat,), [(x1, _rows(D_MODEL)), (f, _rows(D_MODEL)), (g2, vec(D_MODEL)), (W["final_g"], vec(D_MODEL)),
                           (tgt, _rows(D_MODEL))],
        [row_out(D_MODEL, F32), row_out(D_MODEL, BF16), acc_out(D_MODEL), acc_out(D_MODEL), acc_out(128)], "loss_head")

    d_w_down = _mm(act, df, "tn", D_FF, D_MODEL, T, tm=1408, tn=D_MODEL, tk=_pick(T, (512, 256)), name="w_down_dw",
                   out_dtype=BF16).reshape(4, D_FF // 4, D_MODEL)
    da = _mm(df, W["w_down"], "nt", T, D_FF, D_MODEL, tm=tm_lat, tn=1408, tk=512, name="w_down_dx")

    tcb = 128
    n_fb = D_FF // tcb
    colb = lambda blk0: pl.BlockSpec((T, tcb), lambda j: (0, blk0 + j))
    ffwb = lambda off, n=3: pl.BlockSpec((n, tcb), lambda j: (0, j + off))
    cvec = ((1, D_FF), F32, pl.BlockSpec((1, tcb), lambda j: (0, j)), None)

    def f_ffn_bwd(ids, ug, uv, d_act, wg, wv, bg, bv):
        gate, val = _conv(ug, wg, bg), _conv(uv, wv, bv)
        s = _sigmoid(gate)
        d_gate = d_act * val * s * (1.0 + gate * (1.0 - s))
        d_val = d_act * gate * s
        wg0, wg1, wg2 = _conv_bwd_w(d_gate, ug)
        wv0, wv1, wv2 = _conv_bwd_w(d_val, uv)
        d_up = [_conv_bwd_x(d_gate, wg), _conv_bwd_x(d_val, wv)]
        return d_up, _colsum(d_gate), _colsum(d_val), wg0, wg1, wg2, wv0, wv1, wv2

    ffn_b = _ew(f_ffn_bwd, (n_fb,),
                [(up, colb(0)), (up, colb(n_fb)), (da, colb(0)), (W["ffn_conv_w"], ffwb(0)), (W["ffn_conv_w"], ffwb(n_fb)),
                 (W["ffn_conv_b"], ffwb(0, 1)), (W["ffn_conv_b"], ffwb(n_fb, 1))],
                [((2, T, D_FF), BF16, pl.BlockSpec((2, T, tcb), lambda j: (0, 0, j)), None)] + [cvec] * 8, "ffn_act_bwd")
    d_up3 = ffn_b[0]
    d_ffn_conv_b = jnp.concatenate([ffn_b[1], ffn_b[2]], axis=1)
    d_ffn_conv_w = jnp.concatenate([jnp.concatenate(ffn_b[3:6], axis=0), jnp.concatenate(ffn_b[6:9], axis=0)], axis=1)

    tk_t = _pick(T, (512, 256))
    d_w_up = _mm(h2, d_up3, "tn", D_MODEL, 2 * D_FF, T, tm=D_MODEL, tn=1408, tk=tk_t, name="w_up_dw", out_dtype=BF16,
                 b_spec=pl.BlockSpec((None, tk_t, 1408), lambda i, j, k: (j // 2, k, j % 2)),
                 o_spec=pl.BlockSpec((None, D_MODEL, 1408), lambda i, j, k: (j, i, 0)), out_shape=(4, D_MODEL, 1408))
    dh2 = _mm(d_up3, W["w_up"], "nt", T, D_MODEL, 2 * D_FF, tm=tm_lat, tn=D_MODEL, tk=1408, name="w_up_dx",
              a_spec=pl.BlockSpec((None, tm_lat, 1408), lambda i, j, k: (k // 2, i, k % 2)),
              b_spec=pl.BlockSpec((None, D_MODEL, 1408), lambda i, j, k: (k, j, 0)))

    def f_norm2_bwd(ids, dx2_, dh, x1_, m, g, sc, gate):
        r = _rms(x1_)
        xn = x1_ * r
        dx1 = dx2_ + _rms_bwd(dh * g * (1.0 + sc), xn, r)
        return dx1, dx1 * gate, _colsum(dh), _colsum(dh * xn * g), _colsum(dh * xn * (1.0 + sc)), _colsum(dx1 * m)

    dx1, dmo, dsh2, dsc2, dg_n2, dg1 = _ew(
        f_norm2_bwd, (n_lat,), [(dx2, _rows(D_MODEL)), (dh2, _rows(D_MODEL)), (x1, _rows(D_MODEL)), (mo, _rows(D_MODEL)),
                                (W["norm2_g"], vec(D_MODEL)), (sc2, vec(D_MODEL)), (g1, vec(D_MODEL))],
        [row_out(D_MODEL, F32), row_out(D_MODEL, BF16)] + [acc_out(D_MODEL)] * 4, "norm2_bwd")
    d_w_o = _mm(mrg, dmo, "tn", D_MODEL, D_MODEL, T, tm=D_MODEL, tn=D_MODEL, tk=tk_t, name="w_o_dw",
                out_dtype=BF16).reshape(4, D_MODEL // 4, D_MODEL)
    dmrg = _mm(dmo, W["w_o"], "nt", T, D_MODEL, D_MODEL, tm=tm_lat, tn=D_MODEL, tk=512, name="w_o_dx")

    def f_merge_bwd(ids, dm, ga, gc, a, c):
        sa, sc_ = _sigmoid(ga), _sigmoid(gc)
        return dm * sa, dm * sc_, dm * a * sa * (1.0 - sa), dm * c * sc_ * (1.0 - sc_)

    dya, dyc, dp_ga, dp_gc = _ew(
        f_merge_bwd, (n_lat,), [(dmrg, _rows(D_MODEL)), (pp, _rows(D_MODEL, 0)), (pp, _rows(D_MODEL, 1)),
                                (ya, _rows(D_MODEL)), (yc, _rows(D_MODEL))], [row_out(D_MODEL, BF16)] * 4, "merge_bwd")

    d_w_ao_p = _mm(o_pad, dya, "tn", 1024, D_MODEL, T, tm=HEAD_PAD, tn=256, tk=tk_t, name="w_attn_out_dw", out_dtype=BF16,
                   a_spec=pl.BlockSpec((None, tk_t, HEAD_PAD), lambda i, j, k: (i, k, 0)),
                   o_spec=pl.BlockSpec((None, HEAD_PAD, 256), lambda i, j, k: (j, i, 0)), out_shape=(4, 1024, 256))
    do_pad = _mm(dya, W["w_attn_out"], "nt", T, 1024, D_MODEL, tm=tm_lat, tn=HEAD_PAD, tk=D_MODEL, name="w_attn_out_dx",
                 o_spec=pl.BlockSpec((None, tm_lat, HEAD_PAD), lambda i, j, k: (j, i, 0)), out_shape=(N_HEADS, T, HEAD_PAD))
    d_w_co = _mm(e, dyc, "tn", CONV_DIM, D_MODEL, T, tm=CONV_DIM, tn=256, tk=tk_t, name="w_conv_out_dw", out_dtype=BF16,
                 o_spec=pl.BlockSpec((None, CONV_DIM, 256), lambda i, j, k: (j, i, 0)), out_shape=(4, CONV_DIM, 256))
    de = _mm(dyc, W["w_conv_out"], "nt", T, CONV_DIM, D_MODEL, tm=tm_lat, tn=CONV_DIM, tk=256, name="w_conv_out_dx",
             b_spec=pl.BlockSpec((None, CONV_DIM, 256), lambda i, j, k: (k, j, 0)))

    def f_conv_bwd(ids, xin, cb, cc, d_e, w, b):
        z = cc * xin
        cz = _conv(z, w, b)
        dcz = d_e * cb
        w0, w1, w2 = _conv_bwd_w(dcz, z)
        dz = _conv_bwd_x(dcz, w)
        return dz * cc, d_e * cz, dz * xin, _colsum(dcz), w0, w1, w2

    cvec_c = ((1, CONV_DIM), F32, pl.BlockSpec((1, tc), lambda j: (0, j)), None)
    conv_b = _ew(f_conv_bwd, (CONV_DIM // tc,),
                 [(pp, colT(CX0 // tc)), (pp, colT(CB0 // tc)), (pp, colT(CC0 // tc)), (de, colT(0)),
                  (W["conv_w"], pl.BlockSpec((3, tc), lambda j: (0, j))), (W["conv_b"], pl.BlockSpec((1, tc), lambda j: (0, j)))],
                 [((T, CONV_DIM), BF16, colT(0), None)] * 3 + [cvec_c] * 4, "conv_bwd")
    dp_cx, dp_cb, dp_cc, d_conv_b = conv_b[:4]
    d_conv_w = jnp.concatenate(conv_b[4:7], axis=0)

    dqr, dkc, dvp = _attn_bwd(qr, kc, vp, o_pad, do_pad, lse, T, TT)

    h3 = pl.BlockSpec((N_HEADS, ROW_TILE, HEAD_PAD), lambda i: (0, i, 0))

    def f_post(ids, dq, dk, dv, cos, slo, shi):
        lane = lax.broadcasted_iota(jnp.int32, cos.shape, 1)
        rot = (lane >= 64) & (lane < 96)
        dq_raw = [_rope(dq[h], cos, slo, shi) for h in range(N_HEADS)]
        dkv_ = [jnp.where(lane < 64, dk[h], dv[h]) for h in range(N_HEADS)]
        kr = jnp.where(rot, dk[0], 0.0)
        for h in range(1, N_HEADS):
            kr = kr + jnp.where(rot, dk[h], 0.0)
        return dq_raw, dkv_, _rope(kr, cos, slo, shi)

    dq_raw, dkv, dp_kr = _ew(f_post, (n_all,), [(dqr, h3), (dkc, h3), (dvp, h3), (cos_b, _rows(HEAD_PAD)),
                                               (slo_b, _rows(HEAD_PAD)), (shi_b, _rows(HEAD_PAD))],
                             [(heads_shape, BF16, h3, None), (heads_shape, BF16, h3, None), row_out(HEAD_PAD, BF16, TT)],
                             "attn_post")

    tk_a = _pick(TT, (768, 256))
    heads_b = pl.BlockSpec((None, tk_a, HEAD_PAD), lambda i, j, k: (j, k, 0))
    d_w_uq_p = _mm(nq, dq_raw, "tn", Q_RANK, 1024, TT, tm=Q_RANK, tn=HEAD_PAD, tk=tk_a, name="w_uq_dw", b_spec=heads_b)
    dnq = _mm(dq_raw, W["w_uq"], "nt", TT, Q_RANK, 1024, tm=tm_all, tn=Q_RANK, tk=HEAD_PAD, name="w_uq_dx",
              a_spec=kmajor_a(tm_all))
    d_w_ukv = _mm(nkv, dkv, "tn", KV_RANK, 1024, TT, tm=KV_RANK, tn=HEAD_PAD, tk=tk_a, name="w_ukv_dw", b_spec=heads_b,
                  out_dtype=BF16, o_spec=pl.BlockSpec((None, KV_RANK, HEAD_PAD), lambda i, j, k: (j // 2, i, j % 2)),
                  out_shape=(4, KV_RANK, 256))
    dnkv = _mm(dkv, W["w_ukv"], "nt", TT, KV_RANK, 1024, tm=tm_all, tn=KV_RANK, tk=HEAD_PAD, name="w_ukv_dx",
               a_spec=kmajor_a(tm_all),
               b_spec=pl.BlockSpec((None, KV_RANK, HEAD_PAD), lambda i, j, k: (k // 2, j, k % 2)))

    def f_lowrank_bwd(ids, ckv, cq, dkv_, dq_, gkv, gq):
        rk, rq = _rms(ckv), _rms(cq)
        nk, nq_ = ckv * rk, cq * rq
        return (_rms_bwd(dkv_ * gkv, nk, rk), _rms_bwd(dq_ * gq, nq_, rq), _colsum(dkv_ * nk), _colsum(dq_ * nq_))

    dp_kv, dp_q, dg_kv, dg_q = _ew(
        f_lowrank_bwd, (n_all,), [(pp, _rows(KV_RANK, KV0 // KV_RANK)), (pp, _rows(Q_RANK, Q0 // Q_RANK)),
                                  (dnkv, _rows(KV_RANK)), (dnq, _rows(Q_RANK)), (W["kv_norm_g"], vec(KV_RANK)),
                                  (W["q_norm_g"], vec(Q_RANK))],
        [row_out(KV_RANK, BF16, TT), row_out(Q_RANK, BF16, TT), acc_out(KV_RANK), acc_out(Q_RANK)], "lowrank_norm_bwd")

    lat_cols = jnp.concatenate([dp_ga, dp_gc, dp_cx, dp_cb, dp_cc], axis=1)
    dpp = jnp.concatenate([jnp.pad(lat_cols, ((0, TT - T), (0, 0))), dp_kv, dp_q, dp_kr], axis=1)
    d_w_in_p = _mm(hh, dpp, "tn", D_MODEL, P_COLS, TT, tm=512, tn=2176, tk=_pick(TT, (256,)), name="w_in_dw")
    dhh = _mm(dpp, W["w_in"], "nt", TT, D_MODEL, P_COLS, tm=tm_all, tn=512, tk=2176, name="w_in_dx")

    def f_norm1_bwd(ids, x, dh, dres, g, sc):
        r = _rms(x)
        xn = x * r
        return (dres + _rms_bwd(dh * g * (1.0 + sc), xn, r), _colsum(dh), _colsum(dh * xn * g),
                _colsum(dh * xn * (1.0 + sc)))

    grad_x, dsh1, dsc1, dg_n1 = _ew(
        f_norm1_bwd, (n_lat,), [(xx, _rows(D_MODEL)), (dhh, _rows(D_MODEL)), (dx1, _rows(D_MODEL)),
                                (W["norm1_g"], vec(D_MODEL)), (sc1, vec(D_MODEL))],
        [row_out(D_MODEL, F32)] + [acc_out(D_MODEL)] * 3, "norm1_bwd")

    def f_norm1_ctx_bwd(ids, x, dh, g, sc):
        xn = x * _rms(x)
        return _colsum(dh), _colsum(dh * xn * g), _colsum(dh * xn * (1.0 + sc))

    n_ctx = n_all - n_lat
    dcsh1, dcsc1, dg_n1c = _ew(
        f_norm1_ctx_bwd, (n_ctx,), [(xx, _rows(D_MODEL, 0, n_lat)), (dhh, _rows(D_MODEL, 0, n_lat)),
                                    (W["norm1_g"], vec(D_MODEL)), (csc1, vec(D_MODEL))], [acc_out(D_MODEL)] * 3,
        "norm1_ctx_bwd")

    d_w_uq = d_w_uq_p.reshape(Q_RANK, 4, 2, HEAD_PAD)[:, :, :, :QK_DIM].reshape(Q_RANK, 4, 2 * QK_DIM)
    big = {
        "w_in": _w_in_shards_from_p(d_w_in_p).astype(BF16),
        "w_uq": jnp.transpose(d_w_uq, (1, 0, 2)).astype(BF16),
        "w_ukv": d_w_ukv,
        "w_attn_out": d_w_ao_p.reshape(4, N_HEADS, HEAD_PAD, 256)[:, :, 64:, :].reshape(4, N_HEADS * 64, 256),
        "w_conv_out": d_w_co, "w_o": d_w_o, "w_up": d_w_up, "w_down": d_w_down,
    }
    zero = jnp.zeros((1, 4 * D_MODEL), F32)
    small = {
        "dmod_lat": jnp.concatenate([dsh1, dsc1, dg1, dsh2, dsc2, dg2], axis=1),
        "dmod_ctx": jnp.concatenate([dcsh1, dcsc1, zero], axis=1),
        "norm1_g": dg_n1 + dg_n1c, "norm2_g": dg_n2, "final_g": dg_f, "q_norm_g": dg_q, "kv_norm_g": dg_kv,
        "conv_b": d_conv_b, "conv_w": d_conv_w.reshape(1, -1), "ffn_conv_b": d_ffn_conv_b,
        "ffn_conv_w": d_ffn_conv_w.reshape(1, -1),
    }
    return grad_x, loss, big, small


SMALL = (("dmod_lat", 6144), ("dmod_ctx", 6144), ("norm1_g", 1024), ("norm2_g", 1024), ("final_g", 1024),
         ("q_norm_g", 384), ("kv_norm_g", 256), ("conv_b", 512), ("conv_w", 1536), ("ffn_conv_b", 5632),
         ("ffn_conv_w", 16896))
SMALL_ROWS = 320


def _adamw(w, g, m, v, name):
    R, C = w.shape
    tr = 8 if R % 8 == 0 else R
    for t in range(8, R + 1, 8):
        if R % t == 0 and t * C * 4 <= (1 << 20):
            tr = t
    c1, c2 = 1.0 - ADAM_B1 ** ADAM_STEP, 1.0 - ADAM_B2 ** ADAM_STEP

    def fn(ids, w_, g_, m_, v_):
        m2 = ADAM_B1 * m_ + (1.0 - ADAM_B1) * g_
        v2 = ADAM_B2 * v_ + (1.0 - ADAM_B2) * (g_ * g_)
        delta = -ADAM_LR * ((m2 / c1) / (jnp.sqrt(v2 / c2) + ADAM_EPS) + ADAM_WD * w_)
        return delta, m2, v2

    spec = pl.BlockSpec((tr, C), lambda i: (i, 0))
    return _ew(fn, (R // tr,), [(w, spec), (g, spec), (m, spec), (v, spec)], [((R, C), F32, spec, None)] * 3, name)


def kernel(x, c, ctx, c_ctx, w_ada, b_ada, norm1_g, w_in, q_norm_g, kv_norm_g, w_uq, w_ukv, conv_w, conv_b, w_attn_out, w_conv_out, w_o, norm2_g, w_up, ffn_conv_w, ffn_conv_b, w_down, final_g, loss_target, m_c_ctx, m_w_ada, m_b_ada, m_norm1_g, m_w_in, m_q_norm_g, m_kv_norm_g, m_w_uq, m_w_ukv, m_conv_w, m_conv_b, m_w_attn_out, m_w_conv_out, m_w_o, m_norm2_g, m_w_up, m_ffn_conv_w, m_ffn_conv_b, m_w_down, m_final_g, v_c_ctx, v_w_ada, v_b_ada, v_norm1_g, v_w_in, v_q_norm_g, v_kv_norm_g, v_w_uq, v_w_ukv, v_conv_w, v_conv_b, v_w_attn_out, v_w_conv_out, v_w_o, v_norm2_g, v_w_up, v_ffn_conv_w, v_ffn_conv_b, v_w_down, v_final_g):
    mx, my, mc = lax.axis_index("x"), lax.axis_index("y"), lax.axis_index("c")
    chip = 2 * mx + my
    dev = 4 * mx + 2 * my + mc
    T, Tc = x.shape[1], ctx.shape[1]
    TT = T + Tc
    shards = {"w_in": w_in[0], "w_uq": w_uq[0], "w_ukv": w_ukv[0], "w_attn_out": w_attn_out[0],
              "w_conv_out": w_conv_out[0], "w_o": w_o[0], "w_up": w_up[0], "w_down": w_down[0]}

    conv_sh = jnp.concatenate([conv_w[0], ffn_conv_w[0]], axis=1)
    pay1 = jnp.concatenate([jnp.pad(c, ((0, 7), (0, 0))), jnp.pad(conv_sh, ((0, 5), (0, 0)))], axis=1)
    got1 = _allgather8(pay1, "gather_cond", in_vmem=True).reshape(8, 8, 2560)
    c_all = got1[:, 0, :D_MODEL]
    conv_all = got1[0::2, :3, D_MODEL:]
    conv_w_full = _cols_from_shards(conv_all[:, :, :128])
    ffn_conv_w_full = _cols_from_shards(conv_all[:, :, 128:])

    names = [n for n, _ in BIG]
    full = dict(zip(names, _gather_weights([shards[n].astype(BF16) for n in names])))
    wuq = _cols_from_shards(full["w_uq"]).reshape(Q_RANK, N_HEADS, QK_DIM)
    wao = _cols_from_shards(full["w_attn_out"]).reshape(N_HEADS, 64, D_MODEL)
    W = {
        "w_in": _w_in_p_from_shards(full["w_in"]),
        "w_uq": jnp.pad(wuq, ((0, 0), (0, 0), (0, HEAD_PAD - QK_DIM))).reshape(Q_RANK, N_HEADS * HEAD_PAD),
        "w_ukv": full["w_ukv"],
        "w_attn_out": jnp.pad(wao, ((0, 0), (64, 0), (0, 0))).reshape(N_HEADS * HEAD_PAD, D_MODEL),
        "w_conv_out": full["w_conv_out"],
        "w_o": full["w_o"].reshape(D_MODEL, D_MODEL),
        "w_up": full["w_up"],
        "w_down": full["w_down"].reshape(D_FF, D_MODEL),
        "norm1_g": norm1_g, "norm2_g": norm2_g, "final_g": final_g.reshape(1, D_MODEL), "q_norm_g": q_norm_g,
        "kv_norm_g": kv_norm_g, "conv_w": conv_w_full, "conv_b": conv_b, "ffn_conv_w": ffn_conv_w_full,
        "ffn_conv_b": ffn_conv_b,
    }

    cond = jnp.concatenate([c_all, c_ctx.reshape(1, D_MODEL), jnp.zeros((7, D_MODEL), F32)], axis=0)

    def f_silu(ids, v):
        return (v * _sigmoid(v),)

    (s16,) = _ew(f_silu, (1,), [(cond, _full((16, D_MODEL)))], [((16, D_MODEL), F32, _full((16, D_MODEL)), None)], "silu_cond")
    mod_sh = _mm(s16, w_ada[0], "nn", 16, 1536, D_MODEL, tm=16, tn=768, tk=D_MODEL, name="w_ada_fwd")
    got2 = _allgather8(mod_sh, "gather_mod", in_vmem=True).reshape(4, 2, 16, 1536)[:, 0]
    mod_all = _cols_from_shards(got2) + b_ada
    mod_lat = lax.dynamic_slice_in_dim(mod_all, dev, 1, axis=0)
    mod_ctx = mod_all[8:9]

    xx = jnp.concatenate([x[0], ctx[0]], axis=0)
    grad_x, loss_part, gbig, gsmall = _local_step(xx, loss_target[0], mod_lat, mod_ctx, W)
    loss = lax.psum(loss_part[0, 0], ("x", "y", "c"))

    pay3 = jnp.concatenate([gsmall[n].reshape(-1) for n, _ in SMALL])
    pay3 = jnp.pad(pay3, (0, SMALL_ROWS * 128 - pay3.shape[0])).reshape(SMALL_ROWS, 128)
    got3 = _allgather8(pay3, "gather_small", in_vmem=True)

    def f_sum8(ids, a):
        s = a[0:SMALL_ROWS]
        for d in range(1, 8):
            s = s + a[d * SMALL_ROWS:(d + 1) * SMALL_ROWS]
        return (s,)

    (vsum,) = _ew(f_sum8, (1,), [(got3, _full((8 * SMALL_ROWS, 128)))],
                  [((SMALL_ROWS, 128), F32, _full((SMALL_ROWS, 128)), None)], "sum_small")
    vflat = vsum.reshape(-1)
    gvec, off = {}, 0
    for n, size in SMALL:
        gvec[n] = vflat[off:off + size]
        off += size
    dmod_rows = got3.reshape(8, SMALL_ROWS * 128)[:, :6 * D_MODEL]
    dm16 = jnp.concatenate([dmod_rows, gvec["dmod_ctx"].reshape(1, -1), jnp.zeros((7, 6 * D_MODEL), F32)], axis=0)

    def f_colsum(ids, a):
        return (_colsum(a),)

    (g_b_ada,) = _ew(f_colsum, (1,), [(dm16, _full((16, 6 * D_MODEL)))],
                     [((1, 6 * D_MODEL), F32, _full((1, 6 * D_MODEL)), None)], "b_ada_grad")
    dm_sh = lax.dynamic_slice_in_dim(dm16, chip * 1536, 1536, axis=1)
    g_w_ada = _mm(s16, dm_sh, "tn", D_MODEL, 1536, 16, tm=512, tn=768, tk=16, name="w_ada_dw")
    dcond_part = _mm(dm_sh, w_ada[0], "nt", 16, D_MODEL, 1536, tm=16, tn=512, tk=1536, name="w_ada_dx")
    got4 = _allgather8(dcond_part[8:16], "gather_dcond", in_vmem=True).reshape(4, 2, 8, D_MODEL)[:, 0, 0]

    def f_c_ctx(ids, parts, cc):
        s = _sigmoid(cc)
        d = parts[0:1] + parts[1:2] + parts[2:3] + parts[3:4]
        return (d * s * (1.0 + cc * (1.0 - s)),)

    (g_c_ctx,) = _ew(f_c_ctx, (1,), [(got4, _full((4, D_MODEL))), (c_ctx.reshape(1, D_MODEL), _full((1, D_MODEL)))],
                     [((1, D_MODEL), F32, _full((1, D_MODEL)), None)], "c_ctx_grad")

    keep, from_sibling = _rs_pair([gbig[n] for n in names])
    pair_sums = [_add_pair(keep[w], from_sibling[w], "rs_pair_add_" + n) for w, n in enumerate(names)]
    lands = _rs_chips(pair_sums)
    half_sums = [_add_chips(lands[w], "rs_chip_add_" + n) for w, n in enumerate(names)]
    gw = dict(zip(names, _rs_pair_back(half_sums)))
    gw["w_ada"] = g_w_ada

    moments = {"w_ada": (w_ada, m_w_ada, v_w_ada), "w_in": (w_in, m_w_in, v_w_in), "w_uq": (w_uq, m_w_uq, v_w_uq),
               "w_ukv": (w_ukv, m_w_ukv, v_w_ukv), "w_attn_out": (w_attn_out, m_w_attn_out, v_w_attn_out),
               "w_conv_out": (w_conv_out, m_w_conv_out, v_w_conv_out), "w_o": (w_o, m_w_o, v_w_o),
               "w_up": (w_up, m_w_up, v_w_up), "w_down": (w_down, m_w_down, v_w_down)}
    grads, deltas, new_m, new_v = {}, {}, {}, {}
    for n, (w_, m_, v_) in moments.items():
        d_, m2, v2 = _adamw(w_[0], gw[n], m_[0], v_[0], "adamw_" + n)
        grads[n], deltas[n], new_m[n], new_v[n] = gw[n][None], d_[None], m2[None], v2[None]

    conv_w_g = lax.dynamic_slice_in_dim(gvec["conv_w"].reshape(3, CONV_DIM), chip * 128, 128, axis=1)
    ffn_conv_w_g = lax.dynamic_slice_in_dim(gvec["ffn_conv_w"].reshape(3, 2 * D_FF), chip * 1408, 1408, axis=1)
    vec_params = (("c_ctx", c_ctx, m_c_ctx, v_c_ctx, g_c_ctx), ("b_ada", b_ada, m_b_ada, v_b_ada, g_b_ada),
                  ("norm1_g", norm1_g, m_norm1_g, v_norm1_g, gvec["norm1_g"]),
                  ("q_norm_g", q_norm_g, m_q_norm_g, v_q_norm_g, gvec["q_norm_g"]),
                  ("kv_norm_g", kv_norm_g, m_kv_norm_g, v_kv_norm_g, gvec["kv_norm_g"]),
                  ("conv_w", conv_w, m_conv_w, v_conv_w, conv_w_g), ("conv_b", conv_b, m_conv_b, v_conv_b, gvec["conv_b"]),
                  ("norm2_g", norm2_g, m_norm2_g, v_norm2_g, gvec["norm2_g"]),
                  ("ffn_conv_w", ffn_conv_w, m_ffn_conv_w, v_ffn_conv_w, ffn_conv_w_g),
                  ("ffn_conv_b", ffn_conv_b, m_ffn_conv_b, v_ffn_conv_b, gvec["ffn_conv_b"]),
                  ("final_g", final_g, m_final_g, v_final_g, gvec["final_g"]))
    total = sum(p[1].size for p in vec_params)
    rows_v = -(-total // 1024) * 8

    def packv(idx):
        flat_v = jnp.concatenate([p[idx].reshape(-1) for p in vec_params])
        return jnp.pad(flat_v, (0, rows_v * 128 - total)).reshape(rows_v, 128)

    vd, vm, vv = _adamw(packv(1), packv(4), packv(2), packv(3), "adamw_vectors")
    off = 0
    for p in vec_params:
        n, shape, size = p[0], p[1].shape, p[1].size
        grads[n] = p[4].reshape(shape)
        deltas[n] = vd.reshape(-1)[off:off + size].reshape(shape)
        new_m[n] = vm.reshape(-1)[off:off + size].reshape(shape)
        new_v[n] = vv.reshape(-1)[off:off + size].reshape(shape)
        off += size

    order = ("c_ctx", "w_ada", "b_ada", "norm1_g", "w_in", "q_norm_g", "kv_norm_g", "w_uq", "w_ukv", "conv_w", "conv_b",
             "w_attn_out", "w_conv_out", "w_o", "norm2_g", "w_up", "ffn_conv_w", "ffn_conv_b", "w_down", "final_g")
    return (loss, grad_x[None], *[grads[n] for n in order], *[deltas[n] for n in order],
            *[new_m[n] for n in order], *[new_v[n] for n in order])
```

```python
import functools

import jax
import jax.numpy as jnp
from jax import lax
from jax.experimental import pallas as pl
from jax.experimental.pallas import tpu as pltpu

F32, BF16 = jnp.float32, jnp.bfloat16
MESH = pl.DeviceIdType.MESH

D_MODEL = 1024
N_HEADS = 8
HEAD_PAD = 128
QK_DIM = 96
Q_RANK, KV_RANK = 384, 256
CONV_DIM = 512
D_FF = 2816
GRID_W = 64
ROPE_THETA = 10000.0
EPS = 1e-6
GA0, GC0, CX0, CB0, CC0, KV0, Q0, KR0, P_COLS = 0, 1024, 2048, 2560, 3072, 3584, 3840, 4224, 4352
ROW_TILE = 256
VMEM_LIMIT_BYTES = 48 * 1024 * 1024

ADAM_LR, ADAM_B1, ADAM_B2, ADAM_EPS, ADAM_WD, ADAM_STEP = 0.001, 0.9, 0.999, 1e-08, 0.01, 10

BIG = (("w_in", (1024, 1064)), ("w_uq", (384, 192)), ("w_ukv", (256, 256)), ("w_attn_out", (512, 256)),
       ("w_conv_out", (512, 256)), ("w_o", (256, 1024)), ("w_up", (1024, 1408)), ("w_down", (704, 1024)))

NN = (((1,), (0,)), ((), ()))
NT = (((1,), (1,)), ((), ()))
TN = (((0,), (0,)), ((), ()))


def _cp(sem):
    return pltpu.CompilerParams(dimension_semantics=sem, vmem_limit_bytes=VMEM_LIMIT_BYTES)


def _pick(n, prefs):
    for p in prefs:
        if n % p == 0:
            return p
    return n


def _mm(a, b, mode, M, N, K, *, tm, tn, tk, name, out_dtype=F32, a_spec=None, b_spec=None, o_spec=None,
        out_shape=None):
    assert M % tm == 0 and N % tn == 0 and K % tk == 0, (name, M, N, K, tm, tn, tk)
    nk = K // tk
    dims = {"nn": NN, "nt": NT, "tn": TN}[mode]
    if a_spec is None:
        a_spec = (pl.BlockSpec((tk, tm), lambda i, j, k: (k, i)) if mode == "tn"
                  else pl.BlockSpec((tm, tk), lambda i, j, k: (i, k)))
    if b_spec is None:
        b_spec = (pl.BlockSpec((tn, tk), lambda i, j, k: (j, k)) if mode == "nt"
                  else pl.BlockSpec((tk, tn), lambda i, j, k: (k, j)))
    if o_spec is None:
        o_spec = pl.BlockSpec((tm, tn), lambda i, j, k: (i, j))
    if out_shape is None:
        out_shape = (M, N)

    def body(a_ref, b_ref, o_ref, acc_ref):
        k = pl.program_id(2)
        part = lax.dot_general(a_ref[...].astype(BF16), b_ref[...].astype(BF16), dims, preferred_element_type=F32)

        @pl.when(k == 0)
        def _():
            acc_ref[...] = part

        @pl.when(k > 0)
        def _():
            acc_ref[...] += part

        @pl.when(k == nk - 1)
        def _():
            o_ref[...] = acc_ref[...].astype(o_ref.dtype)

    return pl.pallas_call(
        body, grid=(M // tm, N // tn, nk), in_specs=[a_spec, b_spec], out_specs=o_spec,
        out_shape=jax.ShapeDtypeStruct(out_shape, out_dtype), scratch_shapes=[pltpu.VMEM((tm, tn), F32)],
        compiler_params=_cp(("parallel", "parallel", "arbitrary")), name=name)(a, b)


def _ew(fn, grid, ins, outs, name, scalars=None):
    n_in = len(ins)
    n_sc = 0 if scalars is None else 1

    def store(ref, val, acc, ids):
        if isinstance(val, (list, tuple)):
            for h, v in enumerate(val):
                ref[h] = v.astype(ref.dtype)
            return
        if acc is None:
            ref[...] = val.astype(ref.dtype)
            return

        @pl.when(ids[acc] == 0)
        def _():
            ref[...] = val.astype(ref.dtype)

        @pl.when(ids[acc] > 0)
        def _():
            ref[...] += val.astype(ref.dtype)

    def body(*refs):
        refs = refs[n_sc:]
        ids = tuple(pl.program_id(a) for a in range(len(grid)))
        vals = fn(ids, *[r[...] for r in refs[:n_in]])
        for ref, val, (_, _, _, acc) in zip(refs[n_in:], vals, outs):
            store(ref, val, acc, ids)

    acc_axes = {o[3] for o in outs if o[3] is not None}
    sem = tuple("arbitrary" if a in acc_axes else "parallel" for a in range(len(grid)))
    in_specs, out_specs = [s for _, s in ins], [o[2] for o in outs]
    out_shape = [jax.ShapeDtypeStruct(o[0], o[1]) for o in outs]
    args = [a for a, _ in ins]
    if scalars is None:
        return pl.pallas_call(body, grid=grid, in_specs=in_specs, out_specs=out_specs, out_shape=out_shape,
                              compiler_params=_cp(sem), name=name)(*args)
    spec = pltpu.PrefetchScalarGridSpec(num_scalar_prefetch=1, grid=grid, in_specs=in_specs, out_specs=out_specs)
    return pl.pallas_call(body, grid_spec=spec, out_shape=out_shape, compiler_params=_cp(sem), name=name)(scalars, *args)


def _rows(width, cblk=0, roff=0, tr=ROW_TILE):
    return pl.BlockSpec((tr, width), lambda i: (i + roff, cblk))


def _full(shape):
    nd = len(shape)
    return pl.BlockSpec(shape, lambda *_: (0,) * nd)


def _sigmoid(x):
    return 1.0 / (1.0 + jnp.exp(-x))


def _rms(x):
    return lax.rsqrt(jnp.mean(x * x, axis=-1, keepdims=True) + EPS)


def _rms_bwd(dn, xn, r):
    return r * (dn - xn * jnp.mean(dn * xn, axis=-1, keepdims=True))


def _colsum(x):
    return jnp.sum(x, axis=0, keepdims=True)


def _shift_prev(x):
    rows = lax.broadcasted_iota(jnp.int32, x.shape, 0)
    return jnp.where(rows == 0, 0.0, pltpu.roll(x, 1, 0))


def _shift_next(x):
    rows = lax.broadcasted_iota(jnp.int32, x.shape, 0)
    return jnp.where(rows == x.shape[0] - 1, 0.0, pltpu.roll(x, x.shape[0] - 1, 0))


def _conv(x, w, b):
    return b + _shift_prev(x) * w[0:1] + x * w[1:2] + _shift_next(x) * w[2:3]


def _conv_bwd_x(dy, w):
    return _shift_next(dy) * w[0:1] + dy * w[1:2] + _shift_prev(dy) * w[2:3]


def _conv_bwd_w(dy, x):
    return _colsum(dy * _shift_prev(x)), _colsum(dy * x), _colsum(dy * _shift_next(x))


def _rope(x, cos, sin_lo, sin_hi):
    return x * cos + pltpu.roll(x, HEAD_PAD - 8, 1) * sin_lo + pltpu.roll(x, 8, 1) * sin_hi


ATTN_SCALE = QK_DIM ** -0.5


def _attn_fwd(qr, kc, vp, T, TT):
    tq = ROW_TILE

    def body(q_ref, k_ref, v_ref, o_ref, l_ref):
        s = lax.dot_general(q_ref[...], k_ref[...], NT, preferred_element_type=F32) * ATTN_SCALE
        m = jnp.max(s, axis=-1, keepdims=True)
        p = jnp.exp(s - m)
        l = jnp.sum(p, axis=-1, keepdims=True)
        o = lax.dot_general(p.astype(BF16), v_ref[...], NN, preferred_element_type=F32)
        o_ref[...] = o / l
        l_ref[...] = m + jnp.log(l)

    qspec = pl.BlockSpec((None, tq, HEAD_PAD), lambda h, i: (h, i, 0))
    kspec = pl.BlockSpec((None, TT, HEAD_PAD), lambda h, i: (h, 0, 0))
    return pl.pallas_call(
        body, grid=(N_HEADS, T // tq), in_specs=[qspec, kspec, kspec],
        out_specs=[qspec, pl.BlockSpec((None, tq, 1), lambda h, i: (h, i, 0))],
        out_shape=[jax.ShapeDtypeStruct((N_HEADS, T, HEAD_PAD), F32), jax.ShapeDtypeStruct((N_HEADS, T, 1), F32)],
        compiler_params=_cp(("parallel", "parallel")), name="attn_fwd")(qr, kc, vp)


def _attn_bwd(qr, kc, vp, o, do, lse, T, TT):
    tq = ROW_TILE
    nq = T // tq

    def body(q_ref, k_ref, v_ref, o_ref, do_ref, l_ref, dq_ref, dk_ref, dv_ref):
        i = pl.program_id(1)

        @pl.when(i == 0)
        def _():
            dk_ref[...] = jnp.zeros_like(dk_ref)
            dv_ref[...] = jnp.zeros_like(dv_ref)

        @pl.when(i < nq)
        def _():
            q, k, v, d_o = q_ref[...], k_ref[...], v_ref[...], do_ref[...]
            s = lax.dot_general(q, k, NT, preferred_element_type=F32) * ATTN_SCALE
            p = jnp.exp(s - l_ref[...])
            dob = d_o.astype(BF16)
            dp = lax.dot_general(dob, v, NT, preferred_element_type=F32)
            dd = jnp.sum(d_o * o_ref[...], axis=-1, keepdims=True)
            ds = (p * (dp - dd) * ATTN_SCALE).astype(BF16)
            dq_ref[...] = lax.dot_general(ds, k, NN, preferred_element_type=F32)
            dk_ref[...] += lax.dot_general(ds, q, TN, preferred_element_type=F32)
            dv_ref[...] += lax.dot_general(p.astype(BF16), dob, TN, preferred_element_type=F32)

        @pl.when(i == nq)
        def _():
            dq_ref[...] = jnp.zeros_like(dq_ref)

    qspec = pl.BlockSpec((None, tq, HEAD_PAD), lambda h, i: (h, i, 0))
    lat = pl.BlockSpec((None, tq, HEAD_PAD), lambda h, i: (h, jnp.minimum(i, nq - 1), 0))
    lspec = pl.BlockSpec((None, tq, 1), lambda h, i: (h, jnp.minimum(i, nq - 1), 0))
    kspec = pl.BlockSpec((None, TT, HEAD_PAD), lambda h, i: (h, 0, 0))
    big = jax.ShapeDtypeStruct((N_HEADS, TT, HEAD_PAD), F32)
    return pl.pallas_call(
        body, grid=(N_HEADS, TT // tq), in_specs=[qspec, kspec, kspec, lat, lat, lspec],
        out_specs=[qspec, kspec, kspec], out_shape=[big, big, big],
        compiler_params=_cp(("parallel", "arbitrary")), name="attn_bwd")(qr, kc, vp, o, do, lse)


def _allgather8(x, name, in_vmem):
    m_per, n = x.shape

    def body(x_ref, out_ref, send_sems, recv_sems, local_sem):
        mx, my, mc = lax.axis_index("x"), lax.axis_index("y"), lax.axis_index("c")
        me, sibling = (mx, my, mc), (mx, my, 1 - mc)
        chips = [(1 - mx, my), (mx, 1 - my), (1 - mx, 1 - my)]

        def rows(px, py, pc):
            return out_ref.at[pl.ds((4 * px + 2 * py + pc) * m_per, m_per), :]

        def copy(k, block, to, src=None):
            return pltpu.make_async_remote_copy(
                src_ref=rows(*block) if src is None else src, dst_ref=rows(*block),
                send_sem=send_sems.at[k], recv_sem=recv_sems.at[k], device_id=to, device_id_type=MESH)

        mine = pltpu.make_async_copy(x_ref, rows(*me), local_sem)
        mine.start()
        first = [copy(0, me, sibling, src=x_ref)]
        first += [copy(1 + j, me, (*chip, mc), src=x_ref) for j, chip in enumerate(chips)]
        for cp in first:
            cp.start()
        passed = [copy(4 + j, (*chip, mc), sibling) for j, chip in enumerate(chips)]
        for j, chip in enumerate(chips):
            copy(1 + j, (*chip, mc), me).wait_recv()
            passed[j].start()
        copy(0, sibling, me).wait_recv()
        for j, chip in enumerate(chips):
            copy(4 + j, (*chip, 1 - mc), me).wait_recv()
        for cp in first + passed:
            cp.wait_send()
        mine.wait()

    space = pltpu.VMEM if in_vmem else pl.ANY
    return pl.pallas_call(
        body, out_shape=jax.ShapeDtypeStruct((8 * m_per, n), x.dtype),
        in_specs=[pl.BlockSpec(memory_space=space)], out_specs=pl.BlockSpec(memory_space=space),
        scratch_shapes=[pltpu.SemaphoreType.DMA((7,)), pltpu.SemaphoreType.DMA((7,)), pltpu.SemaphoreType.DMA],
        name=name)(x)


def _hbm_specs(n):
    return [pl.BlockSpec(memory_space=pl.ANY)] * n


def _gather_weights(shards):
    n = len(shards)
    halves = [s.shape[0] // 2 for s in shards]

    def body(*refs):
        ins, outs = refs[:n], refs[n:2 * n]
        send_sems, recv_sems = refs[2 * n:]
        mx, my, mc = lax.axis_index("x"), lax.axis_index("y"), lax.axis_index("c")
        j_me = 2 * mx + my
        chips = [(1 - mx, my), (mx, 1 - my), (1 - mx, 1 - my)]

        def half(w, chip_idx, hc):
            return outs[w].at[chip_idx, pl.ds(hc * halves[w], halves[w]), :]

        def copy(w, k, src, dst, to):
            return pltpu.make_async_remote_copy(src_ref=src, dst_ref=dst, send_sem=send_sems.at[w, k],
                                                recv_sem=recv_sems.at[w, k], device_id=to, device_id_type=MESH)

        sends = []
        for w in range(n):
            cp = copy(w, 6, ins[w], outs[w].at[j_me], (mx, my, 1 - mc))
            cp.start()
            sends.append(cp)
        for k, (px, py) in enumerate(chips):
            for w in range(n):
                cp = copy(w, k, ins[w].at[pl.ds(mc * halves[w], halves[w]), :], half(w, j_me, mc), (px, py, mc))
                cp.start()
                sends.append(cp)
        for k, (px, py) in enumerate(chips):
            for w in range(n):
                got = half(w, 2 * px + py, mc)
                copy(w, k, got, got, (px, py, mc)).wait_recv()
                cp = copy(w, 3 + k, got, got, (mx, my, 1 - mc))
                cp.start()
                sends.append(cp)
        for k, (px, py) in enumerate(chips):
            for w in range(n):
                got = half(w, 2 * px + py, 1 - mc)
                copy(w, 3 + k, got, got, (mx, my, 1 - mc)).wait_recv()
        for w in range(n):
            own = outs[w].at[j_me]
            copy(w, 6, own, own, (mx, my, 1 - mc)).wait_recv()
        for cp in sends:
            cp.wait_send()

    return pl.pallas_call(
        body, out_shape=[jax.ShapeDtypeStruct((4,) + s.shape, s.dtype) for s in shards],
        in_specs=_hbm_specs(n), out_specs=_hbm_specs(n),
        scratch_shapes=[pltpu.SemaphoreType.DMA((n, 7)), pltpu.SemaphoreType.DMA((n, 7))],
        name="gather_weights")(*shards)


def _rs_pair(gs):
    n = len(gs)
    halves = [g.shape[1] // 2 for g in gs]

    def body(*refs):
        ins, lands = refs[:n], refs[n:2 * n]
        send_sems, recv_sems = refs[2 * n:]
        mx, my, mc = lax.axis_index("x"), lax.axis_index("y"), lax.axis_index("c")
        copies = []
        for w in range(n):
            h = halves[w]
            cp = pltpu.make_async_remote_copy(
                src_ref=ins[w].at[:, pl.ds((1 - mc) * h, h), :], dst_ref=lands[w], send_sem=send_sems.at[w],
                recv_sem=recv_sems.at[w], device_id=(mx, my, 1 - mc), device_id_type=MESH)
            cp.start()
            copies.append(cp)
        for cp in copies:
            cp.wait()

    return pl.pallas_call(
        body, out_shape=[jax.ShapeDtypeStruct((4, h, g.shape[2]), g.dtype) for g, h in zip(gs, halves)],
        in_specs=_hbm_specs(n), out_specs=_hbm_specs(n),
        scratch_shapes=[pltpu.SemaphoreType.DMA((n,)), pltpu.SemaphoreType.DMA((n,))], name="rs_pair")(*gs)


def _rs_chips(parts):
    n = len(parts)

    def body(*refs):
        ins, lands = refs[:n], refs[n:2 * n]
        send_sems, recv_sems = refs[2 * n:]
        mx, my, mc = lax.axis_index("x"), lax.axis_index("y"), lax.axis_index("c")
        copies = []
        for k, (px, py) in enumerate([(1 - mx, my), (mx, 1 - my), (1 - mx, 1 - my)]):
            for w in range(n):
                cp = pltpu.make_async_remote_copy(
                    src_ref=ins[w].at[2 * px + py], dst_ref=lands[w].at[k], send_sem=send_sems.at[w, k],
                    recv_sem=recv_sems.at[w, k], device_id=(px, py, mc), device_id_type=MESH)
                cp.start()
                copies.append(cp)
        for cp in copies:
            cp.wait()

    return pl.pallas_call(
        body, out_shape=[jax.ShapeDtypeStruct((3,) + p.shape[1:], p.dtype) for p in parts],
        in_specs=_hbm_specs(n), out_specs=_hbm_specs(n),
        scratch_shapes=[pltpu.SemaphoreType.DMA((n, 3)), pltpu.SemaphoreType.DMA((n, 3))], name="rs_chips")(*parts)


def _rs_pair_back(gs):
    n = len(gs)

    def body(*refs):
        outs = refs[n:2 * n]
        send_sems, recv_sems = refs[2 * n:]
        mx, my, mc = lax.axis_index("x"), lax.axis_index("y"), lax.axis_index("c")
        copies = []
        for w in range(n):
            h = gs[w].shape[0] // 2
            mine = outs[w].at[pl.ds(mc * h, h), :]
            cp = pltpu.make_async_remote_copy(src_ref=mine, dst_ref=mine, send_sem=send_sems.at[w],
                                              recv_sem=recv_sems.at[w], device_id=(mx, my, 1 - mc), device_id_type=MESH)
            cp.start()
            copies.append(cp)
        for cp in copies:
            cp.wait()

    return pl.pallas_call(
        body, out_shape=[jax.ShapeDtypeStruct(g.shape, g.dtype) for g in gs],
        in_specs=_hbm_specs(n), out_specs=_hbm_specs(n), input_output_aliases={w: w for w in range(n)},
        scratch_shapes=[pltpu.SemaphoreType.DMA((n,)), pltpu.SemaphoreType.DMA((n,))], name="rs_pair_back")(*gs)


def _tile_rows(h, c, itemsize, mult):
    best = h
    for t in range(mult, h + 1, mult):
        if h % t == 0 and t * c * itemsize <= (1 << 20):
            best = t
    return best


def _add_pair(g, land, place, name):
    _, h, c = land.shape
    t = _tile_rows(h, c, 2, 16)
    nb = h // t
    return _ew(lambda ids, u, v: (u.astype(F32) + v.astype(F32),), (4, nb),
               [(g, pl.BlockSpec((None, t, c), lambda j, i, s: (j, s[1] * nb + i, 0))),
                (land, pl.BlockSpec((None, t, c), lambda j, i, s: (j, i, 0)))],
               [(land.shape, BF16, pl.BlockSpec((None, t, c), lambda j, i, s: (j, i, 0)), None)], name, scalars=place)[0]


def _add_chips(own, land, place, name):
    _, h, c = land.shape
    t = _tile_rows(h, c, 4, 16)
    nb = h // t

    def fn(ids, a, b):
        return (((a.astype(F32) + b[0].astype(F32)) + b[1].astype(F32)) + b[2].astype(F32),)

    return _ew(fn, (nb,), [(own, pl.BlockSpec((None, t, c), lambda i, s: (s[0], i, 0))),
                           (land, pl.BlockSpec((3, t, c), lambda i, s: (0, i, 0)))],
               [((2 * h, c), F32, pl.BlockSpec((t, c), lambda i, s: (s[1] * nb + i, 0)), None)], name, scalars=place)[0]


W_IN_SEGMENTS = ((0, 256, KV0), (256, 288, KR0 + 64), (288, 672, Q0), (672, 1184, CX0), (1184, 1696, CB0),
                 (1696, 2208, CC0), (2208, 3232, GA0), (3232, 4256, GC0))
W_IN_SHARD = 1064


def _w_in_p_from_shards(s):
    pieces = []
    for o0, o1, p0 in sorted(W_IN_SEGMENTS, key=lambda t: t[2]):
        if p0 == KR0 + 64:
            pieces.append(jnp.zeros((s.shape[1], 64), s.dtype))
        for j in range(4):
            lo, hi = max(o0, j * W_IN_SHARD), min(o1, (j + 1) * W_IN_SHARD)
            if lo < hi:
                pieces.append(s[j][:, lo - j * W_IN_SHARD:hi - j * W_IN_SHARD])
    pieces.append(jnp.zeros((s.shape[1], 32), s.dtype))
    return jnp.concatenate(pieces, axis=1)


def _w_in_shards_from_p(g):
    shards = []
    for j in range(4):
        pieces = []
        for o0, o1, p0 in W_IN_SEGMENTS:
            lo, hi = max(o0, j * W_IN_SHARD), min(o1, (j + 1) * W_IN_SHARD)
            if lo < hi:
                pieces.append(g[:, p0 + lo - o0:p0 + hi - o0])
        shards.append(jnp.concatenate(pieces, axis=1))
    return jnp.stack(shards, axis=0)


def _cols_from_shards(s):
    return jnp.transpose(s, (1, 0, 2)).reshape(s.shape[1], -1)


def _rope_tables(T, TT, inverse):
    rows = T // GRID_W
    row = jnp.repeat(jnp.arange(rows), GRID_W).astype(F32)
    col = jnp.tile(jnp.arange(GRID_W), rows).astype(F32)
    inv = ROPE_THETA ** (-jnp.arange(0, 16, 2, dtype=F32) / 16)
    ang = jnp.concatenate([row[:, None] * inv, col[:, None] * inv], axis=-1)
    cos, sin = jnp.cos(ang), jnp.sin(ang)
    lane = jnp.arange(32)
    src = (lane // 16) * 8 + lane % 8
    lo = ((lane % 16) // 8 == 0).astype(F32)
    sgn = -1.0 if inverse else 1.0
    cos32 = cos[:, src]
    sin_lo32 = -sgn * sin[:, src] * lo
    sin_hi32 = sgn * sin[:, src] * (1.0 - lo)

    def widen(t32, fill):
        t = jnp.concatenate([jnp.full((T, 64), fill, F32), t32, jnp.full((T, 32), fill, F32)], axis=1)
        return jnp.concatenate([t, jnp.full((TT - T, HEAD_PAD), fill, F32)], axis=0)

    return widen(cos32, 1.0), widen(sin_lo32, 0.0), widen(sin_hi32, 0.0)


def _local_step(xx, tgt, mod_lat, mod_ctx, W):
    TT = xx.shape[0]
    T = tgt.shape[0]
    n_lat, n_all = T // ROW_TILE, TT // ROW_TILE
    sh1, sc1, g1, sh2, sc2, g2 = [mod_lat[:, k * D_MODEL:(k + 1) * D_MODEL] for k in range(6)]
    csh1, csc1 = mod_ctx[:, :D_MODEL], mod_ctx[:, D_MODEL:2 * D_MODEL]
    vec = lambda n: _full((1, n))
    row_out = lambda n, dt, rows=T: ((rows, n), dt, _rows(n), None)
    acc_out = lambda n: ((1, n), F32, _full((1, n)), 0)

    def f_norm1(ids, x, g, a_sh, a_sc, b_sh, b_sc):
        ctx = ids[0] >= n_lat
        sh, sc = jnp.where(ctx, b_sh, a_sh), jnp.where(ctx, b_sc, a_sc)
        return ((x * _rms(x) * g) * (1.0 + sc) + sh,)

    (hh,) = _ew(f_norm1, (n_all,), [(xx, _rows(D_MODEL)), (W["norm1_g"], vec(D_MODEL)), (sh1, vec(D_MODEL)),
                                   (sc1, vec(D_MODEL)), (csh1, vec(D_MODEL)), (csc1, vec(D_MODEL))],
                [row_out(D_MODEL, BF16, TT)], "norm1_fwd")
    tm_all = _pick(TT, (768, 256))
    pp = _mm(hh, W["w_in"], "nn", TT, P_COLS, D_MODEL, tm=tm_all, tn=2176, tk=512, name="w_in_fwd")

    def f_lowrank(ids, ckv, cq, gkv, gq):
        return ckv * _rms(ckv) * gkv, cq * _rms(cq) * gq

    nkv, nq = _ew(f_lowrank, (n_all,), [(pp, _rows(KV_RANK, KV0 // KV_RANK)), (pp, _rows(Q_RANK, Q0 // Q_RANK)),
                                       (W["kv_norm_g"], vec(KV_RANK)), (W["q_norm_g"], vec(Q_RANK))],
                  [row_out(KV_RANK, BF16, TT), row_out(Q_RANK, BF16, TT)], "lowrank_norm_fwd")
    heads_out = pl.BlockSpec((None, tm_all, HEAD_PAD), lambda i, j, k: (j, i, 0))
    heads_shape = (N_HEADS, TT, HEAD_PAD)
    kv = _mm(nkv, W["w_ukv"], "nn", TT, 1024, KV_RANK, tm=tm_all, tn=HEAD_PAD, tk=KV_RANK, name="w_ukv_fwd",
             b_spec=pl.BlockSpec((None, KV_RANK, HEAD_PAD), lambda i, j, k: (j // 2, k, j % 2)),
             o_spec=heads_out, out_shape=heads_shape)
    q_raw = _mm(nq, W["w_uq"], "nn", TT, 1024, Q_RANK, tm=tm_all, tn=HEAD_PAD, tk=Q_RANK, name="w_uq_fwd",
                o_spec=heads_out, out_shape=heads_shape)

    cos_f, slo_f, shi_f = _rope_tables(T, TT, inverse=False)
    cos_b, slo_b, shi_b = _rope_tables(T, TT, inverse=True)
    hspec = pl.BlockSpec((None, ROW_TILE, HEAD_PAD), lambda h, i: (h, i, 0))
    tspec = pl.BlockSpec((ROW_TILE, HEAD_PAD), lambda h, i: (i, 0))

    def f_prep(ids, q, kvh, kr, cos, slo, shi):
        lane = lax.broadcasted_iota(jnp.int32, q.shape, 1)
        return (_rope(q, cos, slo, shi), jnp.where(lane < 64, kvh, _rope(kr, cos, slo, shi)),
                jnp.where(lane >= 64, kvh, 0.0))

    qr, kc, vp = _ew(f_prep, (N_HEADS, n_all),
                     [(q_raw, hspec), (kv, hspec), (pp, pl.BlockSpec((ROW_TILE, HEAD_PAD), lambda h, i: (i, KR0 // 128))),
                      (cos_f, tspec), (slo_f, tspec), (shi_f, tspec)],
                     [(heads_shape, BF16, hspec, None)] * 3, "attn_prep")
    o_pad, lse = _attn_fwd(qr, kc, vp, T, TT)
    tm_lat = _pick(T, (1024, 512, 256))
    kmajor_a = lambda tm: pl.BlockSpec((None, tm, HEAD_PAD), lambda i, j, k: (k, i, 0))
    ya = _mm(o_pad, W["w_attn_out"], "nn", T, D_MODEL, 1024, tm=tm_lat, tn=D_MODEL, tk=HEAD_PAD, name="w_attn_out_fwd",
             a_spec=kmajor_a(tm_lat))

    tc = 256
    colT = lambda blk0: pl.BlockSpec((T, tc), lambda j: (0, blk0 + j))

    def f_conv(ids, xin, cb, cc, w, b):
        return (cb * _conv(cc * xin, w, b),)

    (e,) = _ew(f_conv, (CONV_DIM // tc,),
               [(pp, colT(CX0 // tc)), (pp, colT(CB0 // tc)), (pp, colT(CC0 // tc)),
                (W["conv_w"], pl.BlockSpec((3, tc), lambda j: (0, j))), (W["conv_b"], pl.BlockSpec((1, tc), lambda j: (0, j)))],
               [((T, CONV_DIM), BF16, colT(0), None)], "conv_fwd")
    yc = _mm(e, W["w_conv_out"], "nn", T, D_MODEL, CONV_DIM, tm=tm_lat, tn=256, tk=CONV_DIM, name="w_conv_out_fwd",
             b_spec=pl.BlockSpec((None, CONV_DIM, 256), lambda i, j, k: (j, k, 0)))

    def f_merge(ids, ga, gc, a, c):
        return (_sigmoid(ga) * a + _sigmoid(gc) * c,)

    (mrg,) = _ew(f_merge, (n_lat,), [(pp, _rows(D_MODEL, 0)), (pp, _rows(D_MODEL, 1)), (ya, _rows(D_MODEL)),
                                    (yc, _rows(D_MODEL))], [row_out(D_MODEL, BF16)], "merge_fwd")
    mo = _mm(mrg, W["w_o"], "nn", T, D_MODEL, D_MODEL, tm=tm_lat, tn=D_MODEL, tk=512, name="w_o_fwd")

    def f_norm2(ids, x, m, gate, g, sh, sc):
        x1 = x + gate * m
        return x1, (x1 * _rms(x1) * g) * (1.0 + sc) + sh

    x1, h2 = _ew(f_norm2, (n_lat,), [(xx, _rows(D_MODEL)), (mo, _rows(D_MODEL)), (g1, vec(D_MODEL)),
                                    (W["norm2_g"], vec(D_MODEL)), (sh2, vec(D_MODEL)), (sc2, vec(D_MODEL))],
                 [row_out(D_MODEL, F32), row_out(D_MODEL, BF16)], "norm2_fwd")
    up = _mm(h2, W["w_up"], "nn", T, 2 * D_FF, D_MODEL, tm=tm_lat, tn=1408, tk=512, name="w_up_fwd",
             b_spec=pl.BlockSpec((None, 512, 1408), lambda i, j, k: (j, k, 0)))

    n_ff = D_FF // tc
    ffw = lambda off, n=3: pl.BlockSpec((n, tc), lambda j: (0, j + off))

    def f_ffn(ids, ug, uv, wg, wv, bg, bv):
        gate, val = _conv(ug, wg, bg), _conv(uv, wv, bv)
        return (gate * _sigmoid(gate) * val,)

    (act,) = _ew(f_ffn, (n_ff,), [(up, colT(0)), (up, colT(n_ff)), (W["ffn_conv_w"], ffw(0)), (W["ffn_conv_w"], ffw(n_ff)),
                                 (W["ffn_conv_b"], ffw(0, 1)), (W["ffn_conv_b"], ffw(n_ff, 1))],
                 [((T, D_FF), BF16, colT(0), None)], "ffn_act_fwd")
    f = _mm(act, W["w_down"], "nn", T, D_MODEL, D_FF, tm=tm_lat, tn=D_MODEL, tk=1408, name="w_down_fwd")

    def f_head(ids, x1_, f_, gate, gf, t):
        x2 = x1_ + gate * f_
        r = _rms(x2)
        xn = x2 * r
        err = xn * gf - t
        loss = 0.5 * jnp.sum(jnp.mean(err * err, axis=-1, keepdims=True))
        dy = err * (1.0 / D_MODEL)
        dx2 = _rms_bwd(dy * gf, xn, r)
        return dx2, dx2 * gate, _colsum(dy * xn), _colsum(dx2 * f_), jnp.full((1, 128), loss, F32)

    dx2, df, dg_f, dg2, loss = _ew(
        f_head, (n_lat,), [(x1, _rows(D_MODEL)), (f, _rows(D_MODEL)), (g2, vec(D_MODEL)), (W["final_g"], vec(D_MODEL)),
                           (tgt, _rows(D_MODEL))],
        [row_out(D_MODEL, F32), row_out(D_MODEL, BF16), acc_out(D_MODEL), acc_out(D_MODEL), acc_out(128)], "loss_head")

    d_w_down = _mm(act, df, "tn", D_FF, D_MODEL, T, tm=1408, tn=D_MODEL, tk=_pick(T, (512, 256)), name="w_down_dw",
                   out_dtype=BF16).reshape(4, D_FF // 4, D_MODEL)
    da = _mm(df, W["w_down"], "nt", T, D_FF, D_MODEL, tm=tm_lat, tn=1408, tk=512, name="w_down_dx")

    tcb = 128
    n_fb = D_FF // tcb
    colb = lambda blk0: pl.BlockSpec((T, tcb), lambda j: (0, blk0 + j))
    ffwb = lambda off, n=3: pl.BlockSpec((n, tcb), lambda j: (0, j + off))
    cvec = ((1, D_FF), F32, pl.BlockSpec((1, tcb), lambda j: (0, j)), None)

    def f_ffn_bwd(ids, ug, uv, d_act, wg, wv, bg, bv):
        gate, val = _conv(ug, wg, bg), _conv(uv, wv, bv)
        s = _sigmoid(gate)
        d_gate = d_act * val * s * (1.0 + gate * (1.0 - s))
        d_val = d_act * gate * s
        wg0, wg1, wg2 = _conv_bwd_w(d_gate, ug)
        wv0, wv1, wv2 = _conv_bwd_w(d_val, uv)
        d_up = [_conv_bwd_x(d_gate, wg), _conv_bwd_x(d_val, wv)]
        return d_up, _colsum(d_gate), _colsum(d_val), wg0, wg1, wg2, wv0, wv1, wv2

    ffn_b = _ew(f_ffn_bwd, (n_fb,),
                [(up, colb(0)), (up, colb(n_fb)), (da, colb(0)), (W["ffn_conv_w"], ffwb(0)), (W["ffn_conv_w"], ffwb(n_fb)),
                 (W["ffn_conv_b"], ffwb(0, 1)), (W["ffn_conv_b"], ffwb(n_fb, 1))],
                [((2, T, D_FF), BF16, pl.BlockSpec((2, T, tcb), lambda j: (0, 0, j)), None)] + [cvec] * 8, "ffn_act_bwd")
    d_up3 = ffn_b[0]
    d_ffn_conv_b = jnp.concatenate([ffn_b[1], ffn_b[2]], axis=1)
    d_ffn_conv_w = jnp.concatenate([jnp.concatenate(ffn_b[3:6], axis=0), jnp.concatenate(ffn_b[6:9], axis=0)], axis=1)

    tk_t = _pick(T, (512, 256))
    d_w_up = _mm(h2, d_up3, "tn", D_MODEL, 2 * D_FF, T, tm=D_MODEL, tn=1408, tk=tk_t, name="w_up_dw", out_dtype=BF16,
                 b_spec=pl.BlockSpec((None, tk_t, 1408), lambda i, j, k: (j // 2, k, j % 2)),
                 o_spec=pl.BlockSpec((None, D_MODEL, 1408), lambda i, j, k: (j, i, 0)), out_shape=(4, D_MODEL, 1408))
    dh2 = _mm(d_up3, W["w_up"], "nt", T, D_MODEL, 2 * D_FF, tm=tm_lat, tn=D_MODEL, tk=1408, name="w_up_dx",
              a_spec=pl.BlockSpec((None, tm_lat, 1408), lambda i, j, k: (k // 2, i, k % 2)),
              b_spec=pl.BlockSpec((None, D_MODEL, 1408), lambda i, j, k: (k, j, 0)))

    def f_norm2_bwd(ids, dx2_, dh, x1_, m, g, sc, gate):
        r = _rms(x1_)
        xn = x1_ * r
        dx1 = dx2_ + _rms_bwd(dh * g * (1.0 + sc), xn, r)
        return dx1, dx1 * gate, _colsum(dh), _colsum(dh * xn * g), _colsum(dh * xn * (1.0 + sc)), _colsum(dx1 * m)

    dx1, dmo, dsh2, dsc2, dg_n2, dg1 = _ew(
        f_norm2_bwd, (n_lat,), [(dx2, _rows(D_MODEL)), (dh2, _rows(D_MODEL)), (x1, _rows(D_MODEL)), (mo, _rows(D_MODEL)),
                                (W["norm2_g"], vec(D_MODEL)), (sc2, vec(D_MODEL)), (g1, vec(D_MODEL))],
        [row_out(D_MODEL, F32), row_out(D_MODEL, BF16)] + [acc_out(D_MODEL)] * 4, "norm2_bwd")
    d_w_o = _mm(mrg, dmo, "tn", D_MODEL, D_MODEL, T, tm=D_MODEL, tn=D_MODEL, tk=tk_t, name="w_o_dw",
                out_dtype=BF16).reshape(4, D_MODEL // 4, D_MODEL)
    dmrg = _mm(dmo, W["w_o"], "nt", T, D_MODEL, D_MODEL, tm=tm_lat, tn=D_MODEL, tk=512, name="w_o_dx")

    def f_merge_bwd(ids, dm, ga, gc, a, c):
        sa, sc_ = _sigmoid(ga), _sigmoid(gc)
        return dm * sa, dm * sc_, dm * a * sa * (1.0 - sa), dm * c * sc_ * (1.0 - sc_)

    dya, dyc, dp_ga, dp_gc = _ew(
        f_merge_bwd, (n_lat,), [(dmrg, _rows(D_MODEL)), (pp, _rows(D_MODEL, 0)), (pp, _rows(D_MODEL, 1)),
                                (ya, _rows(D_MODEL)), (yc, _rows(D_MODEL))], [row_out(D_MODEL, BF16)] * 4, "merge_bwd")

    d_w_ao_p = _mm(o_pad, dya, "tn", 1024, D_MODEL, T, tm=HEAD_PAD, tn=D_MODEL, tk=tk_t, name="w_attn_out_dw",
                   out_dtype=BF16, a_spec=pl.BlockSpec((None, tk_t, HEAD_PAD), lambda i, j, k: (i, k, 0)))
    do_pad = _mm(dya, W["w_attn_out"], "nt", T, 1024, D_MODEL, tm=tm_lat, tn=HEAD_PAD, tk=D_MODEL, name="w_attn_out_dx",
                 o_spec=pl.BlockSpec((None, tm_lat, HEAD_PAD), lambda i, j, k: (j, i, 0)), out_shape=(N_HEADS, T, HEAD_PAD))
    d_w_co = _mm(e, dyc, "tn", CONV_DIM, D_MODEL, T, tm=CONV_DIM, tn=256, tk=tk_t, name="w_conv_out_dw", out_dtype=BF16,
                 o_spec=pl.BlockSpec((None, CONV_DIM, 256), lambda i, j, k: (j, i, 0)), out_shape=(4, CONV_DIM, 256))
    de = _mm(dyc, W["w_conv_out"], "nt", T, CONV_DIM, D_MODEL, tm=tm_lat, tn=CONV_DIM, tk=256, name="w_conv_out_dx",
             b_spec=pl.BlockSpec((None, CONV_DIM, 256), lambda i, j, k: (k, j, 0)))

    def f_conv_bwd(ids, xin, cb, cc, d_e, w, b):
        z = cc * xin
        cz = _conv(z, w, b)
        dcz = d_e * cb
        w0, w1, w2 = _conv_bwd_w(dcz, z)
        dz = _conv_bwd_x(dcz, w)
        return dz * cc, d_e * cz, dz * xin, _colsum(dcz), w0, w1, w2

    cvec_c = ((1, CONV_DIM), F32, pl.BlockSpec((1, tc), lambda j: (0, j)), None)
    conv_b = _ew(f_conv_bwd, (CONV_DIM // tc,),
                 [(pp, colT(CX0 // tc)), (pp, colT(CB0 // tc)), (pp, colT(CC0 // tc)), (de, colT(0)),
                  (W["conv_w"], pl.BlockSpec((3, tc), lambda j: (0, j))), (W["conv_b"], pl.BlockSpec((1, tc), lambda j: (0, j)))],
                 [((T, CONV_DIM), BF16, colT(0), None)] * 3 + [cvec_c] * 4, "conv_bwd")
    dp_cx, dp_cb, dp_cc, d_conv_b = conv_b[:4]
    d_conv_w = jnp.concatenate(conv_b[4:7], axis=0)

    dqr, dkc, dvp = _attn_bwd(qr, kc, vp, o_pad, do_pad, lse, T, TT)

    h3 = pl.BlockSpec((N_HEADS, ROW_TILE, HEAD_PAD), lambda i: (0, i, 0))

    def f_post(ids, dq, dk, dv, cos, slo, shi):
        lane = lax.broadcasted_iota(jnp.int32, cos.shape, 1)
        rot = (lane >= 64) & (lane < 96)
        dq_raw = [_rope(dq[h], cos, slo, shi) for h in range(N_HEADS)]
        dkv_ = [jnp.where(lane < 64, dk[h], dv[h]) for h in range(N_HEADS)]
        kr = jnp.where(rot, dk[0], 0.0)
        for h in range(1, N_HEADS):
            kr = kr + jnp.where(rot, dk[h], 0.0)
        return dq_raw, dkv_, _rope(kr, cos, slo, shi)

    dq_raw, dkv, dp_kr = _ew(f_post, (n_all,), [(dqr, h3), (dkc, h3), (dvp, h3), (cos_b, _rows(HEAD_PAD)),
                                               (slo_b, _rows(HEAD_PAD)), (shi_b, _rows(HEAD_PAD))],
                             [(heads_shape, BF16, h3, None), (heads_shape, BF16, h3, None), row_out(HEAD_PAD, BF16, TT)],
                             "attn_post")

    tk_a = _pick(TT, (768, 256))
    heads_b = pl.BlockSpec((None, tk_a, HEAD_PAD), lambda i, j, k: (j, k, 0))
    d_w_uq_p = _mm(nq, dq_raw, "tn", Q_RANK, 1024, TT, tm=Q_RANK, tn=HEAD_PAD, tk=tk_a, name="w_uq_dw", b_spec=heads_b)
    dnq = _mm(dq_raw, W["w_uq"], "nt", TT, Q_RANK, 1024, tm=tm_all, tn=Q_RANK, tk=HEAD_PAD, name="w_uq_dx",
              a_spec=kmajor_a(tm_all))
    d_w_ukv = _mm(nkv, dkv, "tn", KV_RANK, 1024, TT, tm=KV_RANK, tn=HEAD_PAD, tk=tk_a, name="w_ukv_dw", b_spec=heads_b,
                  out_dtype=BF16, o_spec=pl.BlockSpec((None, KV_RANK, HEAD_PAD), lambda i, j, k: (j // 2, i, j % 2)),
                  out_shape=(4, KV_RANK, 256))
    dnkv = _mm(dkv, W["w_ukv"], "nt", TT, KV_RANK, 1024, tm=tm_all, tn=KV_RANK, tk=HEAD_PAD, name="w_ukv_dx",
               a_spec=kmajor_a(tm_all),
               b_spec=pl.BlockSpec((None, KV_RANK, HEAD_PAD), lambda i, j, k: (k // 2, j, k % 2)))

    def f_lowrank_bwd(ids, ckv, cq, dkv_, dq_, gkv, gq):
        rk, rq = _rms(ckv), _rms(cq)
        nk, nq_ = ckv * rk, cq * rq
        return (_rms_bwd(dkv_ * gkv, nk, rk), _rms_bwd(dq_ * gq, nq_, rq), _colsum(dkv_ * nk), _colsum(dq_ * nq_))

    dp_kv, dp_q, dg_kv, dg_q = _ew(
        f_lowrank_bwd, (n_all,), [(pp, _rows(KV_RANK, KV0 // KV_RANK)), (pp, _rows(Q_RANK, Q0 // Q_RANK)),
                                  (dnkv, _rows(KV_RANK)), (dnq, _rows(Q_RANK)), (W["kv_norm_g"], vec(KV_RANK)),
                                  (W["q_norm_g"], vec(Q_RANK))],
        [row_out(KV_RANK, BF16, TT), row_out(Q_RANK, BF16, TT), acc_out(KV_RANK), acc_out(Q_RANK)], "lowrank_norm_bwd")

    lat_cols = jnp.concatenate([dp_ga, dp_gc, dp_cx, dp_cb, dp_cc], axis=1)
    dpp = jnp.concatenate([jnp.pad(lat_cols, ((0, TT - T), (0, 0))), dp_kv, dp_q, dp_kr], axis=1)
    d_w_in_p = _mm(hh, dpp, "tn", D_MODEL, P_COLS, TT, tm=512, tn=2176, tk=_pick(TT, (256,)), name="w_in_dw")
    dhh = _mm(dpp, W["w_in"], "nt", TT, D_MODEL, P_COLS, tm=tm_all, tn=512, tk=2176, name="w_in_dx")

    def f_norm1_bwd(ids, x, dh, dres, g, sc):
        r = _rms(x)
        xn = x * r
        return (dres + _rms_bwd(dh * g * (1.0 + sc), xn, r), _colsum(dh), _colsum(dh * xn * g),
                _colsum(dh * xn * (1.0 + sc)))

    grad_x, dsh1, dsc1, dg_n1 = _ew(
        f_norm1_bwd, (n_lat,), [(xx, _rows(D_MODEL)), (dhh, _rows(D_MODEL)), (dx1, _rows(D_MODEL)),
                                (W["norm1_g"], vec(D_MODEL)), (sc1, vec(D_MODEL))],
        [row_out(D_MODEL, F32)] + [acc_out(D_MODEL)] * 3, "norm1_bwd")

    def f_norm1_ctx_bwd(ids, x, dh, g, sc):
        xn = x * _rms(x)
        return _colsum(dh), _colsum(dh * xn * g), _colsum(dh * xn * (1.0 + sc))

    n_ctx = n_all - n_lat
    dcsh1, dcsc1, dg_n1c = _ew(
        f_norm1_ctx_bwd, (n_ctx,), [(xx, _rows(D_MODEL, 0, n_lat)), (dhh, _rows(D_MODEL, 0, n_lat)),
                                    (W["norm1_g"], vec(D_MODEL)), (csc1, vec(D_MODEL))], [acc_out(D_MODEL)] * 3,
        "norm1_ctx_bwd")

    d_w_uq = d_w_uq_p.reshape(Q_RANK, 4, 2, HEAD_PAD)[:, :, :, :QK_DIM].reshape(Q_RANK, 4, 2 * QK_DIM)
    big = {
        "w_in": _w_in_shards_from_p(d_w_in_p).astype(BF16),
        "w_uq": jnp.transpose(d_w_uq, (1, 0, 2)).astype(BF16),
        "w_ukv": d_w_ukv,
        "w_attn_out": jnp.transpose(d_w_ao_p.reshape(N_HEADS, HEAD_PAD, 4, 256)[:, 64:], (2, 0, 1, 3)).reshape(
            4, N_HEADS * 64, 256),
        "w_conv_out": d_w_co, "w_o": d_w_o, "w_up": d_w_up, "w_down": d_w_down,
    }
    zero = jnp.zeros((1, 4 * D_MODEL), F32)
    small = {
        "dmod_lat": jnp.concatenate([dsh1, dsc1, dg1, dsh2, dsc2, dg2], axis=1),
        "dmod_ctx": jnp.concatenate([dcsh1, dcsc1, zero], axis=1),
        "norm1_g": dg_n1 + dg_n1c, "norm2_g": dg_n2, "final_g": dg_f, "q_norm_g": dg_q, "kv_norm_g": dg_kv,
        "conv_b": d_conv_b, "conv_w": d_conv_w.reshape(1, -1), "ffn_conv_b": d_ffn_conv_b,
        "ffn_conv_w": d_ffn_conv_w.reshape(1, -1),
    }
    return grad_x, loss, big, small


SMALL = (("dmod_lat", 6144), ("dmod_ctx", 6144), ("norm1_g", 1024), ("norm2_g", 1024), ("final_g", 1024),
         ("q_norm_g", 384), ("kv_norm_g", 256), ("conv_b", 512), ("conv_w", 1536), ("ffn_conv_b", 5632),
         ("ffn_conv_w", 16896))
SMALL_ROWS = 320


def _adamw(w, g, m, v, name):
    R, C = w.shape
    tr = 8 if R % 8 == 0 else R
    for t in range(8, R + 1, 8):
        if R % t == 0 and t * C * 4 <= (1 << 20):
            tr = t
    c1, c2 = 1.0 - ADAM_B1 ** ADAM_STEP, 1.0 - ADAM_B2 ** ADAM_STEP

    def fn(ids, w_, g_, m_, v_):
        m2 = ADAM_B1 * m_ + (1.0 - ADAM_B1) * g_
        v2 = ADAM_B2 * v_ + (1.0 - ADAM_B2) * (g_ * g_)
        delta = -ADAM_LR * ((m2 / c1) / (jnp.sqrt(v2 / c2) + ADAM_EPS) + ADAM_WD * w_)
        return delta, m2, v2

    spec = pl.BlockSpec((tr, C), lambda i: (i, 0))
    return _ew(fn, (R // tr,), [(w, spec), (g, spec), (m, spec), (v, spec)], [((R, C), F32, spec, None)] * 3, name)


def kernel(x, c, ctx, c_ctx, w_ada, b_ada, norm1_g, w_in, q_norm_g, kv_norm_g, w_uq, w_ukv, conv_w, conv_b, w_attn_out, w_conv_out, w_o, norm2_g, w_up, ffn_conv_w, ffn_conv_b, w_down, final_g, loss_target, m_c_ctx, m_w_ada, m_b_ada, m_norm1_g, m_w_in, m_q_norm_g, m_kv_norm_g, m_w_uq, m_w_ukv, m_conv_w, m_conv_b, m_w_attn_out, m_w_conv_out, m_w_o, m_norm2_g, m_w_up, m_ffn_conv_w, m_ffn_conv_b, m_w_down, m_final_g, v_c_ctx, v_w_ada, v_b_ada, v_norm1_g, v_w_in, v_q_norm_g, v_kv_norm_g, v_w_uq, v_w_ukv, v_conv_w, v_conv_b, v_w_attn_out, v_w_conv_out, v_w_o, v_norm2_g, v_w_up, v_ffn_conv_w, v_ffn_conv_b, v_w_down, v_final_g):
    mx, my, mc = lax.axis_index("x"), lax.axis_index("y"), lax.axis_index("c")
    chip = 2 * mx + my
    dev = 4 * mx + 2 * my + mc
    T, Tc = x.shape[1], ctx.shape[1]
    TT = T + Tc
    shards = {"w_in": w_in[0], "w_uq": w_uq[0], "w_ukv": w_ukv[0], "w_attn_out": w_attn_out[0],
              "w_conv_out": w_conv_out[0], "w_o": w_o[0], "w_up": w_up[0], "w_down": w_down[0]}

    conv_sh = jnp.concatenate([conv_w[0], ffn_conv_w[0]], axis=1)
    pay1 = jnp.concatenate([jnp.pad(c, ((0, 7), (0, 0))), jnp.pad(conv_sh, ((0, 5), (0, 0)))], axis=1)
    got1 = _allgather8(pay1, "gather_cond", in_vmem=True).reshape(8, 8, 2560)
    c_all = got1[:, 0, :D_MODEL]
    conv_all = got1[0::2, :3, D_MODEL:]
    conv_w_full = _cols_from_shards(conv_all[:, :, :128])
    ffn_conv_w_full = _cols_from_shards(conv_all[:, :, 128:])

    names = [n for n, _ in BIG]
    full = dict(zip(names, _gather_weights([shards[n].astype(BF16) for n in names])))
    wuq = _cols_from_shards(full["w_uq"]).reshape(Q_RANK, N_HEADS, QK_DIM)
    wao = _cols_from_shards(full["w_attn_out"]).reshape(N_HEADS, 64, D_MODEL)
    W = {
        "w_in": _w_in_p_from_shards(full["w_in"]),
        "w_uq": jnp.pad(wuq, ((0, 0), (0, 0), (0, HEAD_PAD - QK_DIM))).reshape(Q_RANK, N_HEADS * HEAD_PAD),
        "w_ukv": full["w_ukv"],
        "w_attn_out": jnp.pad(wao, ((0, 0), (64, 0), (0, 0))).reshape(N_HEADS * HEAD_PAD, D_MODEL),
        "w_conv_out": full["w_conv_out"],
        "w_o": full["w_o"].reshape(D_MODEL, D_MODEL),
        "w_up": full["w_up"],
        "w_down": full["w_down"].reshape(D_FF, D_MODEL),
        "norm1_g": norm1_g, "norm2_g": norm2_g, "final_g": final_g.reshape(1, D_MODEL), "q_norm_g": q_norm_g,
        "kv_norm_g": kv_norm_g, "conv_w": conv_w_full, "conv_b": conv_b, "ffn_conv_w": ffn_conv_w_full,
        "ffn_conv_b": ffn_conv_b,
    }

    cond = jnp.concatenate([c_all, c_ctx.reshape(1, D_MODEL), jnp.zeros((7, D_MODEL), F32)], axis=0)

    def f_silu(ids, v):
        return (v * _sigmoid(v),)

    (s16,) = _ew(f_silu, (1,), [(cond, _full((16, D_MODEL)))], [((16, D_MODEL), F32, _full((16, D_MODEL)), None)], "silu_cond")
    mod_sh = _mm(s16, w_ada[0], "nn", 16, 1536, D_MODEL, tm=16, tn=768, tk=D_MODEL, name="w_ada_fwd")
    got2 = _allgather8(mod_sh, "gather_mod", in_vmem=True).reshape(4, 2, 16, 1536)[:, 0]
    mod_all = _cols_from_shards(got2) + b_ada
    mod_lat = lax.dynamic_slice_in_dim(mod_all, dev, 1, axis=0)
    mod_ctx = mod_all[8:9]

    xx = jnp.concatenate([x[0], ctx[0]], axis=0)
    grad_x, loss_part, gbig, gsmall = _local_step(xx, loss_target[0], mod_lat, mod_ctx, W)
    loss = lax.psum(loss_part[0, 0], ("x", "y", "c"))

    pay3 = jnp.concatenate([gsmall[n].reshape(-1) for n, _ in SMALL])
    pay3 = jnp.pad(pay3, (0, SMALL_ROWS * 128 - pay3.shape[0])).reshape(SMALL_ROWS, 128)
    got3 = _allgather8(pay3, "gather_small", in_vmem=True)

    def f_sum8(ids, a):
        s = a[0:SMALL_ROWS]
        for d in range(1, 8):
            s = s + a[d * SMALL_ROWS:(d + 1) * SMALL_ROWS]
        return (s,)

    (vsum,) = _ew(f_sum8, (1,), [(got3, _full((8 * SMALL_ROWS, 128)))],
                  [((SMALL_ROWS, 128), F32, _full((SMALL_ROWS, 128)), None)], "sum_small")
    vflat = vsum.reshape(-1)
    gvec, off = {}, 0
    for n, size in SMALL:
        gvec[n] = vflat[off:off + size]
        off += size
    dmod_rows = got3.reshape(8, SMALL_ROWS * 128)[:, :6 * D_MODEL]
    dm16 = jnp.concatenate([dmod_rows, gvec["dmod_ctx"].reshape(1, -1), jnp.zeros((7, 6 * D_MODEL), F32)], axis=0)

    def f_colsum(ids, a):
        return (_colsum(a),)

    (g_b_ada,) = _ew(f_colsum, (1,), [(dm16, _full((16, 6 * D_MODEL)))],
                     [((1, 6 * D_MODEL), F32, _full((1, 6 * D_MODEL)), None)], "b_ada_grad")
    dm_sh = lax.dynamic_slice_in_dim(dm16, chip * 1536, 1536, axis=1)
    g_w_ada = _mm(s16, dm_sh, "tn", D_MODEL, 1536, 16, tm=512, tn=768, tk=16, name="w_ada_dw")
    dcond_part = _mm(dm_sh, w_ada[0], "nt", 16, D_MODEL, 1536, tm=16, tn=512, tk=1536, name="w_ada_dx")
    got4 = _allgather8(dcond_part[8:16], "gather_dcond", in_vmem=True).reshape(4, 2, 8, D_MODEL)[:, 0, 0]

    def f_c_ctx(ids, parts, cc):
        s = _sigmoid(cc)
        d = parts[0:1] + parts[1:2] + parts[2:3] + parts[3:4]
        return (d * s * (1.0 + cc * (1.0 - s)),)

    (g_c_ctx,) = _ew(f_c_ctx, (1,), [(got4, _full((4, D_MODEL))), (c_ctx.reshape(1, D_MODEL), _full((1, D_MODEL)))],
                     [((1, D_MODEL), F32, _full((1, D_MODEL)), None)], "c_ctx_grad")

    place = jnp.stack([chip, mc]).astype(jnp.int32)
    from_sibling = _rs_pair([gbig[n] for n in names])
    pair_sums = [_add_pair(gbig[n], from_sibling[w], place, "rs_pair_add_" + n) for w, n in enumerate(names)]
    lands = _rs_chips(pair_sums)
    half_sums = [_add_chips(pair_sums[w], lands[w], place, "rs_chip_add_" + n) for w, n in enumerate(names)]
    gw = dict(zip(names, _rs_pair_back(half_sums)))
    gw["w_ada"] = g_w_ada

    moments = {"w_ada": (w_ada, m_w_ada, v_w_ada), "w_in": (w_in, m_w_in, v_w_in), "w_uq": (w_uq, m_w_uq, v_w_uq),
               "w_ukv": (w_ukv, m_w_ukv, v_w_ukv), "w_attn_out": (w_attn_out, m_w_attn_out, v_w_attn_out),
               "w_conv_out": (w_conv_out, m_w_conv_out, v_w_conv_out), "w_o": (w_o, m_w_o, v_w_o),
               "w_up": (w_up, m_w_up, v_w_up), "w_down": (w_down, m_w_down, v_w_down)}
    grads, deltas, new_m, new_v = {}, {}, {}, {}
    for n, (w_, m_, v_) in moments.items():
        d_, m2, v2 = _adamw(w_[0], gw[n], m_[0], v_[0], "adamw_" + n)
        grads[n], deltas[n], new_m[n], new_v[n] = gw[n][None], d_[None], m2[None], v2[None]

    conv_w_g = lax.dynamic_slice_in_dim(gvec["conv_w"].reshape(3, CONV_DIM), chip * 128, 128, axis=1)
    ffn_conv_w_g = lax.dynamic_slice_in_dim(gvec["ffn_conv_w"].reshape(3, 2 * D_FF), chip * 1408, 1408, axis=1)
    vec_params = (("c_ctx", c_ctx, m_c_ctx, v_c_ctx, g_c_ctx), ("b_ada", b_ada, m_b_ada, v_b_ada, g_b_ada),
                  ("norm1_g", norm1_g, m_norm1_g, v_norm1_g, gvec["norm1_g"]),
                  ("q_norm_g", q_norm_g, m_q_norm_g, v_q_norm_g, gvec["q_norm_g"]),
                  ("kv_norm_g", kv_norm_g, m_kv_norm_g, v_kv_norm_g, gvec["kv_norm_g"]),
                  ("conv_w", conv_w, m_conv_w, v_conv_w, conv_w_g), ("conv_b", conv_b, m_conv_b, v_conv_b, gvec["conv_b"]),
                  ("norm2_g", norm2_g, m_norm2_g, v_norm2_g, gvec["norm2_g"]),
                  ("ffn_conv_w", ffn_conv_w, m_ffn_conv_w, v_ffn_conv_w, ffn_conv_w_g),
                  ("ffn_conv_b", ffn_conv_b, m_ffn_conv_b, v_ffn_conv_b, gvec["ffn_conv_b"]),
                  ("final_g", final_g, m_final_g, v_final_g, gvec["final_g"]))
    total = sum(p[1].size for p in vec_params)
    rows_v = -(-total // 1024) * 8

    def packv(idx):
        flat_v = jnp.concatenate([p[idx].reshape(-1) for p in vec_params])
        return jnp.pad(flat_v, (0, rows_v * 128 - total)).reshape(rows_v, 128)

    vd, vm, vv = _adamw(packv(1), packv(4), packv(2), packv(3), "adamw_vectors")
    off = 0
    for p in vec_params:
        n, shape, size = p[0], p[1].shape, p[1].size
        grads[n] = p[4].reshape(shape)
        deltas[n] = vd.reshape(-1)[off:off + size].reshape(shape)
        new_m[n] = vm.reshape(-1)[off:off + size].reshape(shape)
        new_v[n] = vv.reshape(-1)[off:off + size].reshape(shape)
        off += size

    order = ("c_ctx", "w_ada", "b_ada", "norm1_g", "w_in", "q_norm_g", "kv_norm_g", "w_uq", "w_ukv", "conv_w", "conv_b",
             "w_attn_out", "w_conv_out", "w_o", "norm2_g", "w_up", "ffn_conv_w", "ffn_conv_b", "w_down", "final_g")
    return (loss, grad_x[None], *[grads[n] for n in order], *[deltas[n] for n in order],
            *[new_m[n] for n in order], *[new_v[n] for n in order])
```

```python
import functools

import jax
import jax.numpy as jnp
from jax import lax
from jax.experimental import pallas as pl
from jax.experimental.pallas import tpu as pltpu

F32, BF16 = jnp.float32, jnp.bfloat16
MESH = pl.DeviceIdType.MESH

D_MODEL = 1024
N_HEADS = 8
HEAD_PAD = 128
QK_DIM = 96
Q_RANK, KV_RANK = 384, 256
CONV_DIM = 512
D_FF = 2816
GRID_W = 64
ROPE_THETA = 10000.0
EPS = 1e-6
GA0, GC0, CX0, CB0, CC0, KV0, Q0, KR0, P_COLS = 0, 1024, 2048, 2560, 3072, 3584, 3840, 4224, 4352
ROW_TILE = 256
VMEM_LIMIT_BYTES = 48 * 1024 * 1024

ADAM_LR, ADAM_B1, ADAM_B2, ADAM_EPS, ADAM_WD, ADAM_STEP = 0.001, 0.9, 0.999, 1e-08, 0.01, 10

BIG = (("w_in", (1024, 1064)), ("w_uq", (384, 192)), ("w_ukv", (256, 256)), ("w_attn_out", (512, 256)),
       ("w_conv_out", (512, 256)), ("w_o", (256, 1024)), ("w_up", (1024, 1408)), ("w_down", (704, 1024)))

LATE = ("w_o", "w_up", "w_down")

NN = (((1,), (0,)), ((), ()))
NT = (((1,), (1,)), ((), ()))
TN = (((0,), (0,)), ((), ()))


def _cp(sem):
    return pltpu.CompilerParams(dimension_semantics=sem, vmem_limit_bytes=VMEM_LIMIT_BYTES)


def _pick(n, prefs):
    for p in prefs:
        if n % p == 0:
            return p
    return n


def _mm(a, b, mode, M, N, K, *, tm, tn, tk, name, out_dtype=F32, a_spec=None, b_spec=None, o_spec=None,
        out_shape=None):
    assert M % tm == 0 and N % tn == 0 and K % tk == 0, (name, M, N, K, tm, tn, tk)
    nk = K // tk
    dims = {"nn": NN, "nt": NT, "tn": TN}[mode]
    if a_spec is None:
        a_spec = (pl.BlockSpec((tk, tm), lambda i, j, k: (k, i)) if mode == "tn"
                  else pl.BlockSpec((tm, tk), lambda i, j, k: (i, k)))
    if b_spec is None:
        b_spec = (pl.BlockSpec((tn, tk), lambda i, j, k: (j, k)) if mode == "nt"
                  else pl.BlockSpec((tk, tn), lambda i, j, k: (k, j)))
    if o_spec is None:
        o_spec = pl.BlockSpec((tm, tn), lambda i, j, k: (i, j))
    if out_shape is None:
        out_shape = (M, N)

    def body(a_ref, b_ref, o_ref, acc_ref):
        k = pl.program_id(2)
        part = lax.dot_general(a_ref[...].astype(BF16), b_ref[...].astype(BF16), dims, preferred_element_type=F32)

        @pl.when(k == 0)
        def _():
            acc_ref[...] = part

        @pl.when(k > 0)
        def _():
            acc_ref[...] += part

        @pl.when(k == nk - 1)
        def _():
            o_ref[...] = acc_ref[...].astype(o_ref.dtype)

    return pl.pallas_call(
        body, grid=(M // tm, N // tn, nk), in_specs=[a_spec, b_spec], out_specs=o_spec,
        out_shape=jax.ShapeDtypeStruct(out_shape, out_dtype), scratch_shapes=[pltpu.VMEM((tm, tn), F32)],
        compiler_params=_cp(("parallel", "parallel", "arbitrary")), name=name)(a, b)


def _ew(fn, grid, ins, outs, name, scalars=None):
    n_in = len(ins)
    n_sc = 0 if scalars is None else 1

    def store(ref, val, acc, ids):
        if isinstance(val, (list, tuple)):
            for h, v in enumerate(val):
                ref[h] = v.astype(ref.dtype)
            return
        if acc is None:
            ref[...] = val.astype(ref.dtype)
            return

        @pl.when(ids[acc] == 0)
        def _():
            ref[...] = val.astype(ref.dtype)

        @pl.when(ids[acc] > 0)
        def _():
            ref[...] += val.astype(ref.dtype)

    def body(*refs):
        refs = refs[n_sc:]
        ids = tuple(pl.program_id(a) for a in range(len(grid)))
        vals = fn(ids, *[r[...] for r in refs[:n_in]])
        for ref, val, (_, _, _, acc) in zip(refs[n_in:], vals, outs):
            store(ref, val, acc, ids)

    acc_axes = {o[3] for o in outs if o[3] is not None}
    sem = tuple("arbitrary" if a in acc_axes else "parallel" for a in range(len(grid)))
    in_specs, out_specs = [s for _, s in ins], [o[2] for o in outs]
    out_shape = [jax.ShapeDtypeStruct(o[0], o[1]) for o in outs]
    args = [a for a, _ in ins]
    if scalars is None:
        return pl.pallas_call(body, grid=grid, in_specs=in_specs, out_specs=out_specs, out_shape=out_shape,
                              compiler_params=_cp(sem), name=name)(*args)
    spec = pltpu.PrefetchScalarGridSpec(num_scalar_prefetch=1, grid=grid, in_specs=in_specs, out_specs=out_specs)
    return pl.pallas_call(body, grid_spec=spec, out_shape=out_shape, compiler_params=_cp(sem), name=name)(scalars, *args)


def _rows(width, cblk=0, roff=0, tr=ROW_TILE):
    return pl.BlockSpec((tr, width), lambda i: (i + roff, cblk))


def _full(shape):
    nd = len(shape)
    return pl.BlockSpec(shape, lambda *_: (0,) * nd)


def _sigmoid(x):
    return 1.0 / (1.0 + jnp.exp(-x))


def _rms(x):
    return lax.rsqrt(jnp.mean(x * x, axis=-1, keepdims=True) + EPS)


def _rms_bwd(dn, xn, r):
    return r * (dn - xn * jnp.mean(dn * xn, axis=-1, keepdims=True))


def _colsum(x):
    return jnp.sum(x, axis=0, keepdims=True)


def _shift_prev(x):
    rows = lax.broadcasted_iota(jnp.int32, x.shape, 0)
    return jnp.where(rows == 0, 0.0, pltpu.roll(x, 1, 0))


def _shift_next(x):
    rows = lax.broadcasted_iota(jnp.int32, x.shape, 0)
    return jnp.where(rows == x.shape[0] - 1, 0.0, pltpu.roll(x, x.shape[0] - 1, 0))


def _conv(x, w, b):
    return b + _shift_prev(x) * w[0:1] + x * w[1:2] + _shift_next(x) * w[2:3]


def _conv_bwd_x(dy, w):
    return _shift_next(dy) * w[0:1] + dy * w[1:2] + _shift_prev(dy) * w[2:3]


def _conv_bwd_w(dy, x):
    return _colsum(dy * _shift_prev(x)), _colsum(dy * x), _colsum(dy * _shift_next(x))


def _rope(x, cos, sin_lo, sin_hi):
    return x * cos + pltpu.roll(x, HEAD_PAD - 8, 1) * sin_lo + pltpu.roll(x, 8, 1) * sin_hi


ATTN_SCALE = QK_DIM ** -0.5


def _attn_fwd(qr, kc, vp, T, TT):
    tq = ROW_TILE

    def body(q_ref, k_ref, v_ref, o_ref, l_ref):
        s = lax.dot_general(q_ref[...], k_ref[...], NT, preferred_element_type=F32) * ATTN_SCALE
        m = jnp.max(s, axis=-1, keepdims=True)
        p = jnp.exp(s - m)
        l = jnp.sum(p, axis=-1, keepdims=True)
        o = lax.dot_general(p.astype(BF16), v_ref[...], NN, preferred_element_type=F32)
        o_ref[...] = o / l
        l_ref[...] = m + jnp.log(l)

    qspec = pl.BlockSpec((None, tq, HEAD_PAD), lambda h, i: (h, i, 0))
    kspec = pl.BlockSpec((None, TT, HEAD_PAD), lambda h, i: (h, 0, 0))
    return pl.pallas_call(
        body, grid=(N_HEADS, T // tq), in_specs=[qspec, kspec, kspec],
        out_specs=[qspec, pl.BlockSpec((None, tq, 1), lambda h, i: (h, i, 0))],
        out_shape=[jax.ShapeDtypeStruct((N_HEADS, T, HEAD_PAD), F32), jax.ShapeDtypeStruct((N_HEADS, T, 1), F32)],
        compiler_params=_cp(("parallel", "parallel")), name="attn_fwd")(qr, kc, vp)


def _attn_bwd(qr, kc, vp, o, do, lse, T, TT):
    tq = ROW_TILE
    nq = T // tq

    def body(q_ref, k_ref, v_ref, o_ref, do_ref, l_ref, dq_ref, dk_ref, dv_ref):
        i = pl.program_id(1)

        @pl.when(i == 0)
        def _():
            dk_ref[...] = jnp.zeros_like(dk_ref)
            dv_ref[...] = jnp.zeros_like(dv_ref)

        @pl.when(i < nq)
        def _():
            q, k, v, d_o = q_ref[...], k_ref[...], v_ref[...], do_ref[...]
            s = lax.dot_general(q, k, NT, preferred_element_type=F32) * ATTN_SCALE
            p = jnp.exp(s - l_ref[...])
            dob = d_o.astype(BF16)
            dp = lax.dot_general(dob, v, NT, preferred_element_type=F32)
            dd = jnp.sum(d_o * o_ref[...], axis=-1, keepdims=True)
            ds = (p * (dp - dd) * ATTN_SCALE).astype(BF16)
            dq_ref[...] = lax.dot_general(ds, k, NN, preferred_element_type=F32)
            dk_ref[...] += lax.dot_general(ds, q, TN, preferred_element_type=F32)
            dv_ref[...] += lax.dot_general(p.astype(BF16), dob, TN, preferred_element_type=F32)

        @pl.when(i == nq)
        def _():
            dq_ref[...] = jnp.zeros_like(dq_ref)

    qspec = pl.BlockSpec((None, tq, HEAD_PAD), lambda h, i: (h, i, 0))
    lat = pl.BlockSpec((None, tq, HEAD_PAD), lambda h, i: (h, jnp.minimum(i, nq - 1), 0))
    lspec = pl.BlockSpec((None, tq, 1), lambda h, i: (h, jnp.minimum(i, nq - 1), 0))
    kspec = pl.BlockSpec((None, TT, HEAD_PAD), lambda h, i: (h, 0, 0))
    big = jax.ShapeDtypeStruct((N_HEADS, TT, HEAD_PAD), F32)
    return pl.pallas_call(
        body, grid=(N_HEADS, TT // tq), in_specs=[qspec, kspec, kspec, lat, lat, lspec],
        out_specs=[qspec, kspec, kspec], out_shape=[big, big, big],
        compiler_params=_cp(("parallel", "arbitrary")), name="attn_bwd")(qr, kc, vp, o, do, lse)


def _allgather8(x, name, in_vmem):
    m_per, n = x.shape

    def body(x_ref, out_ref, send_sems, recv_sems, local_sem):
        mx, my, mc = lax.axis_index("x"), lax.axis_index("y"), lax.axis_index("c")
        me, sibling = (mx, my, mc), (mx, my, 1 - mc)
        chips = [(1 - mx, my), (mx, 1 - my), (1 - mx, 1 - my)]

        def rows(px, py, pc):
            return out_ref.at[pl.ds((4 * px + 2 * py + pc) * m_per, m_per), :]

        def copy(k, block, to, src=None):
            return pltpu.make_async_remote_copy(
                src_ref=rows(*block) if src is None else src, dst_ref=rows(*block),
                send_sem=send_sems.at[k], recv_sem=recv_sems.at[k], device_id=to, device_id_type=MESH)

        mine = pltpu.make_async_copy(x_ref, rows(*me), local_sem)
        mine.start()
        first = [copy(0, me, sibling, src=x_ref)]
        first += [copy(1 + j, me, (*chip, mc), src=x_ref) for j, chip in enumerate(chips)]
        for cp in first:
            cp.start()
        passed = [copy(4 + j, (*chip, mc), sibling) for j, chip in enumerate(chips)]
        for j, chip in enumerate(chips):
            copy(1 + j, (*chip, mc), me).wait_recv()
            passed[j].start()
        copy(0, sibling, me).wait_recv()
        for j, chip in enumerate(chips):
            copy(4 + j, (*chip, 1 - mc), me).wait_recv()
        for cp in first + passed:
            cp.wait_send()
        mine.wait()

    space = pltpu.VMEM if in_vmem else pl.ANY
    return pl.pallas_call(
        body, out_shape=jax.ShapeDtypeStruct((8 * m_per, n), x.dtype),
        in_specs=[pl.BlockSpec(memory_space=space)], out_specs=pl.BlockSpec(memory_space=space),
        scratch_shapes=[pltpu.SemaphoreType.DMA((7,)), pltpu.SemaphoreType.DMA((7,)), pltpu.SemaphoreType.DMA],
        name=name)(x)


def _hbm_specs(n):
    return [pl.BlockSpec(memory_space=pl.ANY)] * n


def _gather_weights(shards):
    n = len(shards)
    halves = [s.shape[0] // 2 for s in shards]

    def body(*refs):
        ins, outs = refs[:n], refs[n:2 * n]
        token, send_sems, recv_sems = refs[2 * n:]
        token[...] = jnp.zeros_like(token)
        mx, my, mc = lax.axis_index("x"), lax.axis_index("y"), lax.axis_index("c")
        j_me = 2 * mx + my
        chips = [(1 - mx, my), (mx, 1 - my), (1 - mx, 1 - my)]

        def half(w, chip_idx, hc):
            return outs[w].at[chip_idx, pl.ds(hc * halves[w], halves[w]), :]

        def copy(w, k, src, dst, to):
            return pltpu.make_async_remote_copy(src_ref=src, dst_ref=dst, send_sem=send_sems.at[w, k],
                                                recv_sem=recv_sems.at[w, k], device_id=to, device_id_type=MESH)

        sends = []
        for w in range(n):
            cp = copy(w, 6, ins[w], outs[w].at[j_me], (mx, my, 1 - mc))
            cp.start()
            sends.append(cp)
        for k, (px, py) in enumerate(chips):
            for w in range(n):
                cp = copy(w, k, ins[w].at[pl.ds(mc * halves[w], halves[w]), :], half(w, j_me, mc), (px, py, mc))
                cp.start()
                sends.append(cp)
        for k, (px, py) in enumerate(chips):
            for w in range(n):
                got = half(w, 2 * px + py, mc)
                copy(w, k, got, got, (px, py, mc)).wait_recv()
                cp = copy(w, 3 + k, got, got, (mx, my, 1 - mc))
                cp.start()
                sends.append(cp)
        for k, (px, py) in enumerate(chips):
            for w in range(n):
                got = half(w, 2 * px + py, 1 - mc)
                copy(w, 3 + k, got, got, (mx, my, 1 - mc)).wait_recv()
        for w in range(n):
            own = outs[w].at[j_me]
            copy(w, 6, own, own, (mx, my, 1 - mc)).wait_recv()
        for cp in sends:
            cp.wait_send()

    res = pl.pallas_call(
        body, out_shape=[jax.ShapeDtypeStruct((4,) + s.shape, s.dtype) for s in shards]
        + [jax.ShapeDtypeStruct((8, 128), F32)],
        in_specs=_hbm_specs(n), out_specs=_hbm_specs(n) + [pl.BlockSpec(memory_space=pltpu.VMEM)],
        scratch_shapes=[pltpu.SemaphoreType.DMA((n, 7)), pltpu.SemaphoreType.DMA((n, 7))],
        name="gather_weights")(*shards)
    return list(res[:n]), res[n]


def _rs_pair(gs, name):
    n = len(gs)
    halves = [g.shape[1] // 2 for g in gs]

    def body(*refs):
        ins, lands = refs[:n], refs[n:2 * n]
        send_sems, recv_sems = refs[2 * n:]
        mx, my, mc = lax.axis_index("x"), lax.axis_index("y"), lax.axis_index("c")
        copies = []
        for w in range(n):
            h = halves[w]
            cp = pltpu.make_async_remote_copy(
                src_ref=ins[w].at[:, pl.ds((1 - mc) * h, h), :], dst_ref=lands[w], send_sem=send_sems.at[w],
                recv_sem=recv_sems.at[w], device_id=(mx, my, 1 - mc), device_id_type=MESH)
            cp.start()
            copies.append(cp)
        for cp in copies:
            cp.wait()

    return pl.pallas_call(
        body, out_shape=[jax.ShapeDtypeStruct((4, h, g.shape[2]), g.dtype) for g, h in zip(gs, halves)],
        in_specs=_hbm_specs(n), out_specs=_hbm_specs(n),
        scratch_shapes=[pltpu.SemaphoreType.DMA((n,)), pltpu.SemaphoreType.DMA((n,))], name=name)(*gs)


def _rs_chips(parts):
    n = len(parts)

    def body(*refs):
        ins, lands = refs[:n], refs[n:2 * n]
        send_sems, recv_sems = refs[2 * n:]
        mx, my, mc = lax.axis_index("x"), lax.axis_index("y"), lax.axis_index("c")
        copies = []
        for k, (px, py) in enumerate([(1 - mx, my), (mx, 1 - my), (1 - mx, 1 - my)]):
            for w in range(n):
                cp = pltpu.make_async_remote_copy(
                    src_ref=ins[w].at[2 * px + py], dst_ref=lands[w].at[k], send_sem=send_sems.at[w, k],
                    recv_sem=recv_sems.at[w, k], device_id=(px, py, mc), device_id_type=MESH)
                cp.start()
                copies.append(cp)
        for cp in copies:
            cp.wait()

    return list(pl.pallas_call(
        body, out_shape=[jax.ShapeDtypeStruct((3,) + p.shape[1:], p.dtype) for p in parts],
        in_specs=_hbm_specs(n), out_specs=_hbm_specs(n),
        scratch_shapes=[pltpu.SemaphoreType.DMA((n, 3)), pltpu.SemaphoreType.DMA((n, 3))], name="rs_chips")(*parts))


def _rs_pair_back(gs):
    n = len(gs)

    def body(*refs):
        outs = refs[n:2 * n]
        send_sems, recv_sems = refs[2 * n:]
        mx, my, mc = lax.axis_index("x"), lax.axis_index("y"), lax.axis_index("c")
        copies = []
        for w in range(n):
            h = gs[w].shape[0] // 2
            mine = outs[w].at[pl.ds(mc * h, h), :]
            cp = pltpu.make_async_remote_copy(src_ref=mine, dst_ref=mine, send_sem=send_sems.at[w],
                                              recv_sem=recv_sems.at[w], device_id=(mx, my, 1 - mc), device_id_type=MESH)
            cp.start()
            copies.append(cp)
        for cp in copies:
            cp.wait()

    return pl.pallas_call(
        body, out_shape=[jax.ShapeDtypeStruct(g.shape, g.dtype) for g in gs],
        in_specs=_hbm_specs(n), out_specs=_hbm_specs(n), input_output_aliases={w: w for w in range(n)},
        scratch_shapes=[pltpu.SemaphoreType.DMA((n,)), pltpu.SemaphoreType.DMA((n,))], name="rs_pair_back")(*gs)


_HBM = pl.BlockSpec(memory_space=pltpu.HBM)
_SEM = pl.BlockSpec(memory_space=pltpu.SEMAPHORE)
_EFFECT = pltpu.SideEffectType.DATAFLOW_SIDE_EFFECTING


def _ici_copies(kind, srcs, lands, send_sems, recv_sems):
    n = len(srcs)
    mx, my, mc = lax.axis_index("x"), lax.axis_index("y"), lax.axis_index("c")
    j_me = 2 * mx + my
    copies = []
    for k, (px, py) in enumerate([(1 - mx, my), (mx, 1 - my), (1 - mx, 1 - my)]):
        for w in range(n):
            if kind == "gather":
                h = srcs[w].shape[0] // 2
                src, dst = srcs[w].at[pl.ds(mc * h, h), :], lands[w].at[j_me, pl.ds(mc * h, h), :]
            else:
                src, dst = srcs[w].at[2 * px + py], lands[w].at[k]
            copies.append(pltpu.make_async_remote_copy(
                src_ref=src, dst_ref=dst, send_sem=send_sems.at[3 * w + k], recv_sem=recv_sems.at[3 * w + k],
                device_id=(px, py, mc), device_id_type=MESH))
    return copies


def _ici_start(kind, srcs, land_shapes, carry, name):
    n = len(srcs)

    def body(*refs):
        ins, lands = refs[:n], refs[n:2 * n]
        send_sems, recv_sems = refs[2 * n + 1], refs[2 * n + 2]
        for cp in _ici_copies(kind, ins, lands, send_sems, recv_sems):
            cp.start()

    hbm = lambda a: pltpu.with_memory_space_constraint(a, pltpu.HBM)
    lands = [lax.empty(s, srcs[0].dtype) for s in land_shapes]
    args = [hbm(a) for a in list(srcs) + lands + [carry]]
    out_shape = ([pltpu.SemaphoreType.DMA((3 * n,)), pltpu.SemaphoreType.DMA((3 * n,))]
                 + [pltpu.HBM(a.shape, a.dtype) for a in args])
    res = pl.pallas_call(
        body, name=name, out_shape=out_shape, in_specs=[_HBM] * len(args), out_specs=[_SEM, _SEM] + [_HBM] * len(args),
        input_output_aliases={i: 2 + i for i in range(len(args))},
        compiler_params=pltpu.CompilerParams(has_side_effects=_EFFECT))(*args)
    return res[0], res[1], list(res[2:2 + n]), list(res[2 + n:2 + 2 * n]), res[2 + 2 * n]


def _ici_wait(kind, send_sems, recv_sems, srcs, lands, after, name):
    n = len(srcs)

    def body(*refs):
        ins, zones = refs[:n], refs[n:2 * n]
        for cp in _ici_copies(kind, ins, zones, refs[2 * n], refs[2 * n + 1]):
            cp.wait_send()
            cp.wait_recv()

    args = list(srcs) + list(lands)
    res = pl.pallas_call(
        body, name=name, out_shape=[pltpu.HBM(a.shape, a.dtype) for a in args],
        in_specs=[_HBM] * len(args) + [_SEM, _SEM, pl.BlockSpec(memory_space=pl.ANY)], out_specs=[_HBM] * len(args),
        input_output_aliases={i: i for i in range(len(args))},
        compiler_params=pltpu.CompilerParams(has_side_effects=_EFFECT))(*args, send_sems, recv_sems, after)
    return list(res[:n]), list(res[n:])


def _gather_finish(shards, lands):
    n = len(shards)

    def body(*refs):
        own, outs = refs[:n], refs[2 * n:3 * n]
        send_sems, recv_sems = refs[3 * n:]
        mx, my, mc = lax.axis_index("x"), lax.axis_index("y"), lax.axis_index("c")
        j_me = 2 * mx + my
        sibling = (mx, my, 1 - mc)
        copies = []

        def push(w, k, src, dst):
            cp = pltpu.make_async_remote_copy(src_ref=src, dst_ref=dst, send_sem=send_sems.at[w, k],
                                              recv_sem=recv_sems.at[w, k], device_id=sibling, device_id_type=MESH)
            cp.start()
            copies.append(cp)

        for w in range(n):
            h = shards[w].shape[0] // 2
            push(w, 3, own[w], outs[w].at[j_me])
            for k, (px, py) in enumerate([(1 - mx, my), (mx, 1 - my), (1 - mx, 1 - my)]):
                got = outs[w].at[2 * px + py, pl.ds(mc * h, h), :]
                push(w, k, got, got)
        for cp in copies:
            cp.wait()

    return pl.pallas_call(
        body, out_shape=[jax.ShapeDtypeStruct(l.shape, l.dtype) for l in lands],
        in_specs=_hbm_specs(2 * n), out_specs=_hbm_specs(n), input_output_aliases={n + w: w for w in range(n)},
        scratch_shapes=[pltpu.SemaphoreType.DMA((n, 4)), pltpu.SemaphoreType.DMA((n, 4))], name="gather_finish",
    )(*shards, *lands)


def _tile_rows(h, c, itemsize, mult):
    best = h
    for t in range(mult, h + 1, mult):
        if h % t == 0 and t * c * itemsize <= (1 << 20):
            best = t
    return best


def _add_pair(g, land, place, name):
    _, h, c = land.shape
    t = _tile_rows(h, c, 2, 16)
    nb = h // t
    return _ew(lambda ids, u, v: (u.astype(F32) + v.astype(F32),), (4, nb),
               [(g, pl.BlockSpec((None, t, c), lambda j, i, s: (j, s[1] * nb + i, 0))),
                (land, pl.BlockSpec((None, t, c), lambda j, i, s: (j, i, 0)))],
               [(land.shape, BF16, pl.BlockSpec((None, t, c), lambda j, i, s: (j, i, 0)), None)], name, scalars=place)[0]


def _add_chips(own, land, place, name):
    _, h, c = land.shape
    t = _tile_rows(h, c, 4, 16)
    nb = h // t

    def fn(ids, a, b):
        return (((a.astype(F32) + b[0].astype(F32)) + b[1].astype(F32)) + b[2].astype(F32),)

    return _ew(fn, (nb,), [(own, pl.BlockSpec((None, t, c), lambda i, s: (s[0], i, 0))),
                           (land, pl.BlockSpec((3, t, c), lambda i, s: (0, i, 0)))],
               [((2 * h, c), F32, pl.BlockSpec((t, c), lambda i, s: (s[1] * nb + i, 0)), None)], name, scalars=place)[0]


W_IN_SEGMENTS = ((0, 256, KV0), (256, 288, KR0 + 64), (288, 672, Q0), (672, 1184, CX0), (1184, 1696, CB0),
                 (1696, 2208, CC0), (2208, 3232, GA0), (3232, 4256, GC0))
W_IN_SHARD = 1064


def _w_in_p_from_shards(s):
    pieces = []
    for o0, o1, p0 in sorted(W_IN_SEGMENTS, key=lambda t: t[2]):
        if p0 == KR0 + 64:
            pieces.append(jnp.zeros((s.shape[1], 64), s.dtype))
        for j in range(4):
            lo, hi = max(o0, j * W_IN_SHARD), min(o1, (j + 1) * W_IN_SHARD)
            if lo < hi:
                pieces.append(s[j][:, lo - j * W_IN_SHARD:hi - j * W_IN_SHARD])
    pieces.append(jnp.zeros((s.shape[1], 32), s.dtype))
    return jnp.concatenate(pieces, axis=1)


def _w_in_shards_from_p(g):
    shards = []
    for j in range(4):
        pieces = []
        for o0, o1, p0 in W_IN_SEGMENTS:
            lo, hi = max(o0, j * W_IN_SHARD), min(o1, (j + 1) * W_IN_SHARD)
            if lo < hi:
                pieces.append(g[:, p0 + lo - o0:p0 + hi - o0])
        shards.append(jnp.concatenate(pieces, axis=1))
    return jnp.stack(shards, axis=0)


def _cols_from_shards(s):
    return jnp.transpose(s, (1, 0, 2)).reshape(s.shape[1], -1)


def _rope_tables(T, TT, inverse):
    rows = T // GRID_W
    row = jnp.repeat(jnp.arange(rows), GRID_W).astype(F32)
    col = jnp.tile(jnp.arange(GRID_W), rows).astype(F32)
    inv = ROPE_THETA ** (-jnp.arange(0, 16, 2, dtype=F32) / 16)
    ang = jnp.concatenate([row[:, None] * inv, col[:, None] * inv], axis=-1)
    cos, sin = jnp.cos(ang), jnp.sin(ang)
    lane = jnp.arange(32)
    src = (lane // 16) * 8 + lane % 8
    lo = ((lane % 16) // 8 == 0).astype(F32)
    sgn = -1.0 if inverse else 1.0
    cos32 = cos[:, src]
    sin_lo32 = -sgn * sin[:, src] * lo
    sin_hi32 = sgn * sin[:, src] * (1.0 - lo)

    def widen(t32, fill):
        t = jnp.concatenate([jnp.full((T, 64), fill, F32), t32, jnp.full((T, 32), fill, F32)], axis=1)
        return jnp.concatenate([t, jnp.full((TT - T, HEAD_PAD), fill, F32)], axis=0)

    return widen(cos32, 1.0), widen(sin_lo32, 0.0), widen(sin_hi32, 0.0)


def _local_step(xx, tgt, mod_lat, mod_ctx, W, late_weights, early_grads):
    TT = xx.shape[0]
    T = tgt.shape[0]
    n_lat, n_all = T // ROW_TILE, TT // ROW_TILE
    sh1, sc1, g1, sh2, sc2, g2 = [mod_lat[:, k * D_MODEL:(k + 1) * D_MODEL] for k in range(6)]
    csh1, csc1 = mod_ctx[:, :D_MODEL], mod_ctx[:, D_MODEL:2 * D_MODEL]
    vec = lambda n: _full((1, n))
    row_out = lambda n, dt, rows=T: ((rows, n), dt, _rows(n), None)
    acc_out = lambda n: ((1, n), F32, _full((1, n)), 0)

    def f_norm1(ids, x, g, a_sh, a_sc, b_sh, b_sc):
        ctx = ids[0] >= n_lat
        sh, sc = jnp.where(ctx, b_sh, a_sh), jnp.where(ctx, b_sc, a_sc)
        return ((x * _rms(x) * g) * (1.0 + sc) + sh,)

    (hh,) = _ew(f_norm1, (n_all,), [(xx, _rows(D_MODEL)), (W["norm1_g"], vec(D_MODEL)), (sh1, vec(D_MODEL)),
                                   (sc1, vec(D_MODEL)), (csh1, vec(D_MODEL)), (csc1, vec(D_MODEL))],
                [row_out(D_MODEL, BF16, TT)], "norm1_fwd")
    tm_all = _pick(TT, (768, 256))
    pp = _mm(hh, W["w_in"], "nn", TT, P_COLS, D_MODEL, tm=tm_all, tn=2176, tk=512, name="w_in_fwd")

    def f_lowrank(ids, ckv, cq, gkv, gq):
        return ckv * _rms(ckv) * gkv, cq * _rms(cq) * gq

    nkv, nq = _ew(f_lowrank, (n_all,), [(pp, _rows(KV_RANK, KV0 // KV_RANK)), (pp, _rows(Q_RANK, Q0 // Q_RANK)),
                                       (W["kv_norm_g"], vec(KV_RANK)), (W["q_norm_g"], vec(Q_RANK))],
                  [row_out(KV_RANK, BF16, TT), row_out(Q_RANK, BF16, TT)], "lowrank_norm_fwd")
    heads_out = pl.BlockSpec((None, tm_all, HEAD_PAD), lambda i, j, k: (j, i, 0))
    heads_shape = (N_HEADS, TT, HEAD_PAD)
    kv = _mm(nkv, W["w_ukv"], "nn", TT, 1024, KV_RANK, tm=tm_all, tn=HEAD_PAD, tk=KV_RANK, name="w_ukv_fwd",
             b_spec=pl.BlockSpec((None, KV_RANK, HEAD_PAD), lambda i, j, k: (j // 2, k, j % 2)),
             o_spec=heads_out, out_shape=heads_shape)
    q_raw = _mm(nq, W["w_uq"], "nn", TT, 1024, Q_RANK, tm=tm_all, tn=HEAD_PAD, tk=Q_RANK, name="w_uq_fwd",
                o_spec=heads_out, out_shape=heads_shape)

    cos_f, slo_f, shi_f = _rope_tables(T, TT, inverse=False)
    cos_b, slo_b, shi_b = _rope_tables(T, TT, inverse=True)
    hspec = pl.BlockSpec((None, ROW_TILE, HEAD_PAD), lambda h, i: (h, i, 0))
    tspec = pl.BlockSpec((ROW_TILE, HEAD_PAD), lambda h, i: (i, 0))

    def f_prep(ids, q, kvh, kr, cos, slo, shi):
        lane = lax.broadcasted_iota(jnp.int32, q.shape, 1)
        return (_rope(q, cos, slo, shi), jnp.where(lane < 64, kvh, _rope(kr, cos, slo, shi)),
                jnp.where(lane >= 64, kvh, 0.0))

    qr, kc, vp = _ew(f_prep, (N_HEADS, n_all),
                     [(q_raw, hspec), (kv, hspec), (pp, pl.BlockSpec((ROW_TILE, HEAD_PAD), lambda h, i: (i, KR0 // 128))),
                      (cos_f, tspec), (slo_f, tspec), (shi_f, tspec)],
                     [(heads_shape, BF16, hspec, None)] * 3, "attn_prep")
    o_pad, lse = _attn_fwd(qr, kc, vp, T, TT)
    tm_lat = _pick(T, (1024, 512, 256))
    kmajor_a = lambda tm: pl.BlockSpec((None, tm, HEAD_PAD), lambda i, j, k: (k, i, 0))
    ya = _mm(o_pad, W["w_attn_out"], "nn", T, D_MODEL, 1024, tm=tm_lat, tn=D_MODEL, tk=HEAD_PAD, name="w_attn_out_fwd",
             a_spec=kmajor_a(tm_lat))

    tc = 256
    colT = lambda blk0: pl.BlockSpec((T, tc), lambda j: (0, blk0 + j))

    def f_conv(ids, xin, cb, cc, w, b):
        return (cb * _conv(cc * xin, w, b),)

    (e,) = _ew(f_conv, (CONV_DIM // tc,),
               [(pp, colT(CX0 // tc)), (pp, colT(CB0 // tc)), (pp, colT(CC0 // tc)),
                (W["conv_w"], pl.BlockSpec((3, tc), lambda j: (0, j))), (W["conv_b"], pl.BlockSpec((1, tc), lambda j: (0, j)))],
               [((T, CONV_DIM), BF16, colT(0), None)], "conv_fwd")
    yc = _mm(e, W["w_conv_out"], "nn", T, D_MODEL, CONV_DIM, tm=tm_lat, tn=256, tk=CONV_DIM, name="w_conv_out_fwd",
             b_spec=pl.BlockSpec((None, CONV_DIM, 256), lambda i, j, k: (j, k, 0)))

    def f_merge(ids, ga, gc, a, c):
        return (_sigmoid(ga) * a + _sigmoid(gc) * c,)

    (mrg,) = _ew(f_merge, (n_lat,), [(pp, _rows(D_MODEL, 0)), (pp, _rows(D_MODEL, 1)), (ya, _rows(D_MODEL)),
                                    (yc, _rows(D_MODEL))], [row_out(D_MODEL, BF16)], "merge_fwd")
    W = dict(W, **late_weights(mrg))
    mo = _mm(mrg, W["w_o"], "nn", T, D_MODEL, D_MODEL, tm=tm_lat, tn=D_MODEL, tk=512, name="w_o_fwd")

    def f_norm2(ids, x, m, gate, g, sh, sc):
        x1 = x + gate * m
        return x1, (x1 * _rms(x1) * g) * (1.0 + sc) + sh

    x1, h2 = _ew(f_norm2, (n_lat,), [(xx, _rows(D_MODEL)), (mo, _rows(D_MODEL)), (g1, vec(D_MODEL)),
                                    (W["norm2_g"], vec(D_MODEL)), (sh2, vec(D_MODEL)), (sc2, vec(D_MODEL))],
                 [row_out(D_MODEL, F32), row_out(D_MODEL, BF16)], "norm2_fwd")
    up = _mm(h2, W["w_up"], "nn", T, 2 * D_FF, D_MODEL, tm=tm_lat, tn=1408, tk=512, name="w_up_fwd",
             b_spec=pl.BlockSpec((None, 512, 1408), lambda i, j, k: (j, k, 0)))

    n_ff = D_FF // tc
    ffw = lambda off, n=3: pl.BlockSpec((n, tc), lambda j: (0, j + off))

    def f_ffn(ids, ug, uv, wg, wv, bg, bv):
        gate, val = _conv(ug, wg, bg), _conv(uv, wv, bv)
        return (gate * _sigmoid(gate) * val,)

    (act,) = _ew(f_ffn, (n_ff,), [(up, colT(0)), (up, colT(n_ff)), (W["ffn_conv_w"], ffw(0)), (W["ffn_conv_w"], ffw(n_ff)),
                                 (W["ffn_conv_b"], ffw(0, 1)), (W["ffn_conv_b"], ffw(n_ff, 1))],
                 [((T, D_FF), BF16, colT(0), None)], "ffn_act_fwd")
    f = _mm(act, W["w_down"], "nn", T, D_MODEL, D_FF, tm=tm_lat, tn=D_MODEL, tk=1408, name="w_down_fwd")

    def f_head(ids, x1_, f_, gate, gf, t):
        x2 = x1_ + gate * f_
        r = _rms(x2)
        xn = x2 * r
        err = xn * gf - t
        loss = 0.5 * jnp.sum(jnp.mean(err * err, axis=-1, keepdims=True))
        dy = err * (1.0 / D_MODEL)
        dx2 = _rms_bwd(dy * gf, xn, r)
        return dx2, dx2 * gate, _colsum(dy * xn), _colsum(dx2 * f_), jnp.full((1, 128), loss, F32)

    dx2, df, dg_f, dg2, loss = _ew(
        f_head, (n_lat,), [(x1, _rows(D_MODEL)), (f, _rows(D_MODEL)), (g2, vec(D_MODEL)), (W["final_g"], vec(D_MODEL)),
                           (tgt, _rows(D_MODEL))],
        [row_out(D_MODEL, F32), row_out(D_MODEL, BF16), acc_out(D_MODEL), acc_out(D_MODEL), acc_out(128)], "loss_head")

    d_w_down = _mm(act, df, "tn", D_FF, D_MODEL, T, tm=1408, tn=D_MODEL, tk=_pick(T, (512, 256)), name="w_down_dw",
                   out_dtype=BF16).reshape(4, D_FF // 4, D_MODEL)
    da = _mm(df, W["w_down"], "nt", T, D_FF, D_MODEL, tm=tm_lat, tn=1408, tk=512, name="w_down_dx")

    tcb = 128
    n_fb = D_FF // tcb
    colb = lambda blk0: pl.BlockSpec((T, tcb), lambda j: (0, blk0 + j))
    ffwb = lambda off, n=3: pl.BlockSpec((n, tcb), lambda j: (0, j + off))
    cvec = ((1, D_FF), F32, pl.BlockSpec((1, tcb), lambda j: (0, j)), None)

    def f_ffn_bwd(ids, ug, uv, d_act, wg, wv, bg, bv):
        gate, val = _conv(ug, wg, bg), _conv(uv, wv, bv)
        s = _sigmoid(gate)
        d_gate = d_act * val * s * (1.0 + gate * (1.0 - s))
        d_val = d_act * gate * s
        wg0, wg1, wg2 = _conv_bwd_w(d_gate, ug)
        wv0, wv1, wv2 = _conv_bwd_w(d_val, uv)
        d_up = [_conv_bwd_x(d_gate, wg), _conv_bwd_x(d_val, wv)]
        return d_up, _colsum(d_gate), _colsum(d_val), wg0, wg1, wg2, wv0, wv1, wv2

    ffn_b = _ew(f_ffn_bwd, (n_fb,),
                [(up, colb(0)), (up, colb(n_fb)), (da, colb(0)), (W["ffn_conv_w"], ffwb(0)), (W["ffn_conv_w"], ffwb(n_fb)),
                 (W["ffn_conv_b"], ffwb(0, 1)), (W["ffn_conv_b"], ffwb(n_fb, 1))],
                [((2, T, D_FF), BF16, pl.BlockSpec((2, T, tcb), lambda j: (0, 0, j)), None)] + [cvec] * 8, "ffn_act_bwd")
    d_up3 = ffn_b[0]
    d_ffn_conv_b = jnp.concatenate([ffn_b[1], ffn_b[2]], axis=1)
    d_ffn_conv_w = jnp.concatenate([jnp.concatenate(ffn_b[3:6], axis=0), jnp.concatenate(ffn_b[6:9], axis=0)], axis=1)

    tk_t = _pick(T, (512, 256))
    d_w_up = _mm(h2, d_up3, "tn", D_MODEL, 2 * D_FF, T, tm=D_MODEL, tn=1408, tk=tk_t, name="w_up_dw", out_dtype=BF16,
                 b_spec=pl.BlockSpec((None, tk_t, 1408), lambda i, j, k: (j // 2, k, j % 2)),
                 o_spec=pl.BlockSpec((None, D_MODEL, 1408), lambda i, j, k: (j, i, 0)), out_shape=(4, D_MODEL, 1408))
    dh2 = _mm(d_up3, W["w_up"], "nt", T, D_MODEL, 2 * D_FF, tm=tm_lat, tn=D_MODEL, tk=1408, name="w_up_dx",
              a_spec=pl.BlockSpec((None, tm_lat, 1408), lambda i, j, k: (k // 2, i, k % 2)),
              b_spec=pl.BlockSpec((None, D_MODEL, 1408), lambda i, j, k: (k, j, 0)))

    def f_norm2_bwd(ids, dx2_, dh, x1_, m, g, sc, gate):
        r = _rms(x1_)
        xn = x1_ * r
        dx1 = dx2_ + _rms_bwd(dh * g * (1.0 + sc), xn, r)
        return dx1, dx1 * gate, _colsum(dh), _colsum(dh * xn * g), _colsum(dh * xn * (1.0 + sc)), _colsum(dx1 * m)

    dx1, dmo, dsh2, dsc2, dg_n2, dg1 = _ew(
        f_norm2_bwd, (n_lat,), [(dx2, _rows(D_MODEL)), (dh2, _rows(D_MODEL)), (x1, _rows(D_MODEL)), (mo, _rows(D_MODEL)),
                                (W["norm2_g"], vec(D_MODEL)), (sc2, vec(D_MODEL)), (g1, vec(D_MODEL))],
        [row_out(D_MODEL, F32), row_out(D_MODEL, BF16)] + [acc_out(D_MODEL)] * 4, "norm2_bwd")
    d_w_o = _mm(mrg, dmo, "tn", D_MODEL, D_MODEL, T, tm=D_MODEL, tn=D_MODEL, tk=tk_t, name="w_o_dw",
                out_dtype=BF16).reshape(4, D_MODEL // 4, D_MODEL)
    dmrg = _mm(dmo, W["w_o"], "nt", T, D_MODEL, D_MODEL, tm=tm_lat, tn=D_MODEL, tk=512, name="w_o_dx")
    dmrg = early_grads({"w_o": d_w_o, "w_up": d_w_up, "w_down": d_w_down}, dmrg)

    def f_merge_bwd(ids, dm, ga, gc, a, c):
        sa, sc_ = _sigmoid(ga), _sigmoid(gc)
        return dm * sa, dm * sc_, dm * a * sa * (1.0 - sa), dm * c * sc_ * (1.0 - sc_)

    dya, dyc, dp_ga, dp_gc = _ew(
        f_merge_bwd, (n_lat,), [(dmrg, _rows(D_MODEL)), (pp, _rows(D_MODEL, 0)), (pp, _rows(D_MODEL, 1)),
                                (ya, _rows(D_MODEL)), (yc, _rows(D_MODEL))], [row_out(D_MODEL, BF16)] * 4, "merge_bwd")

    d_w_ao_p = _mm(o_pad, dya, "tn", 1024, D_MODEL, T, tm=HEAD_PAD, tn=D_MODEL, tk=tk_t, name="w_attn_out_dw",
                   out_dtype=BF16, a_spec=pl.BlockSpec((None, tk_t, HEAD_PAD), lambda i, j, k: (i, k, 0)))
    do_pad = _mm(dya, W["w_attn_out"], "nt", T, 1024, D_MODEL, tm=tm_lat, tn=HEAD_PAD, tk=D_MODEL, name="w_attn_out_dx",
                 o_spec=pl.BlockSpec((None, tm_lat, HEAD_PAD), lambda i, j, k: (j, i, 0)), out_shape=(N_HEADS, T, HEAD_PAD))
    d_w_co = _mm(e, dyc, "tn", CONV_DIM, D_MODEL, T, tm=CONV_DIM, tn=256, tk=tk_t, name="w_conv_out_dw", out_dtype=BF16,
                 o_spec=pl.BlockSpec((None, CONV_DIM, 256), lambda i, j, k: (j, i, 0)), out_shape=(4, CONV_DIM, 256))
    de = _mm(dyc, W["w_conv_out"], "nt", T, CONV_DIM, D_MODEL, tm=tm_lat, tn=CONV_DIM, tk=256, name="w_conv_out_dx",
             b_spec=pl.BlockSpec((None, CONV_DIM, 256), lambda i, j, k: (k, j, 0)))

    def f_conv_bwd(ids, xin, cb, cc, d_e, w, b):
        z = cc * xin
        cz = _conv(z, w, b)
        dcz = d_e * cb
        w0, w1, w2 = _conv_bwd_w(dcz, z)
        dz = _conv_bwd_x(dcz, w)
        return dz * cc, d_e * cz, dz * xin, _colsum(dcz), w0, w1, w2

    cvec_c = ((1, CONV_DIM), F32, pl.BlockSpec((1, tc), lambda j: (0, j)), None)
    conv_b = _ew(f_conv_bwd, (CONV_DIM // tc,),
                 [(pp, colT(CX0 // tc)), (pp, colT(CB0 // tc)), (pp, colT(CC0 // tc)), (de, colT(0)),
                  (W["conv_w"], pl.BlockSpec((3, tc), lambda j: (0, j))), (W["conv_b"], pl.BlockSpec((1, tc), lambda j: (0, j)))],
                 [((T, CONV_DIM), BF16, colT(0), None)] * 3 + [cvec_c] * 4, "conv_bwd")
    dp_cx, dp_cb, dp_cc, d_conv_b = conv_b[:4]
    d_conv_w = jnp.concatenate(conv_b[4:7], axis=0)

    dqr, dkc, dvp = _attn_bwd(qr, kc, vp, o_pad, do_pad, lse, T, TT)

    h3 = pl.BlockSpec((N_HEADS, ROW_TILE, HEAD_PAD), lambda i: (0, i, 0))

    def f_post(ids, dq, dk, dv, cos, slo, shi):
        lane = lax.broadcasted_iota(jnp.int32, cos.shape, 1)
        rot = (lane >= 64) & (lane < 96)
        dq_raw = [_rope(dq[h], cos, slo, shi) for h in range(N_HEADS)]
        dkv_ = [jnp.where(lane < 64, dk[h], dv[h]) for h in range(N_HEADS)]
        kr = jnp.where(rot, dk[0], 0.0)
        for h in range(1, N_HEADS):
            kr = kr + jnp.where(rot, dk[h], 0.0)
        return dq_raw, dkv_, _rope(kr, cos, slo, shi)

    dq_raw, dkv, dp_kr = _ew(f_post, (n_all,), [(dqr, h3), (dkc, h3), (dvp, h3), (cos_b, _rows(HEAD_PAD)),
                                               (slo_b, _rows(HEAD_PAD)), (shi_b, _rows(HEAD_PAD))],
                             [(heads_shape, BF16, h3, None), (heads_shape, BF16, h3, None), row_out(HEAD_PAD, BF16, TT)],
                             "attn_post")

    tk_a = _pick(TT, (768, 256))
    heads_b = pl.BlockSpec((None, tk_a, HEAD_PAD), lambda i, j, k: (j, k, 0))
    d_w_uq_p = _mm(nq, dq_raw, "tn", Q_RANK, 1024, TT, tm=Q_RANK, tn=HEAD_PAD, tk=tk_a, name="w_uq_dw", b_spec=heads_b)
    dnq = _mm(dq_raw, W["w_uq"], "nt", TT, Q_RANK, 1024, tm=tm_all, tn=Q_RANK, tk=HEAD_PAD, name="w_uq_dx",
              a_spec=kmajor_a(tm_all))
    d_w_ukv = _mm(nkv, dkv, "tn", KV_RANK, 1024, TT, tm=KV_RANK, tn=HEAD_PAD, tk=tk_a, name="w_ukv_dw", b_spec=heads_b,
                  out_dtype=BF16, o_spec=pl.BlockSpec((None, KV_RANK, HEAD_PAD), lambda i, j, k: (j // 2, i, j % 2)),
                  out_shape=(4, KV_RANK, 256))
    dnkv = _mm(dkv, W["w_ukv"], "nt", TT, KV_RANK, 1024, tm=tm_all, tn=KV_RANK, tk=HEAD_PAD, name="w_ukv_dx",
               a_spec=kmajor_a(tm_all),
               b_spec=pl.BlockSpec((None, KV_RANK, HEAD_PAD), lambda i, j, k: (k // 2, j, k % 2)))

    def f_lowrank_bwd(ids, ckv, cq, dkv_, dq_, gkv, gq):
        rk, rq = _rms(ckv), _rms(cq)
        nk, nq_ = ckv * rk, cq * rq
        return (_rms_bwd(dkv_ * gkv, nk, rk), _rms_bwd(dq_ * gq, nq_, rq), _colsum(dkv_ * nk), _colsum(dq_ * nq_))

    dp_kv, dp_q, dg_kv, dg_q = _ew(
        f_lowrank_bwd, (n_all,), [(pp, _rows(KV_RANK, KV0 // KV_RANK)), (pp, _rows(Q_RANK, Q0 // Q_RANK)),
                                  (dnkv, _rows(KV_RANK)), (dnq, _rows(Q_RANK)), (W["kv_norm_g"], vec(KV_RANK)),
                                  (W["q_norm_g"], vec(Q_RANK))],
        [row_out(KV_RANK, BF16, TT), row_out(Q_RANK, BF16, TT), acc_out(KV_RANK), acc_out(Q_RANK)], "lowrank_norm_bwd")

    lat_cols = jnp.concatenate([dp_ga, dp_gc, dp_cx, dp_cb, dp_cc], axis=1)
    dpp = jnp.concatenate([jnp.pad(lat_cols, ((0, TT - T), (0, 0))), dp_kv, dp_q, dp_kr], axis=1)
    d_w_in_p = _mm(hh, dpp, "tn", D_MODEL, P_COLS, TT, tm=512, tn=2176, tk=_pick(TT, (256,)), name="w_in_dw")
    dhh = _mm(dpp, W["w_in"], "nt", TT, D_MODEL, P_COLS, tm=tm_all, tn=512, tk=2176, name="w_in_dx")

    def f_norm1_bwd(ids, x, dh, dres, g, sc):
        r = _rms(x)
        xn = x * r
        return (dres + _rms_bwd(dh * g * (1.0 + sc), xn, r), _colsum(dh), _colsum(dh * xn * g),
                _colsum(dh * xn * (1.0 + sc)))

    grad_x, dsh1, dsc1, dg_n1 = _ew(
        f_norm1_bwd, (n_lat,), [(xx, _rows(D_MODEL)), (dhh, _rows(D_MODEL)), (dx1, _rows(D_MODEL)),
                                (W["norm1_g"], vec(D_MODEL)), (sc1, vec(D_MODEL))],
        [row_out(D_MODEL, F32)] + [acc_out(D_MODEL)] * 3, "norm1_bwd")

    def f_norm1_ctx_bwd(ids, x, dh, g, sc):
        xn = x * _rms(x)
        return _colsum(dh), _colsum(dh * xn * g), _colsum(dh * xn * (1.0 + sc))

    n_ctx = n_all - n_lat
    dcsh1, dcsc1, dg_n1c = _ew(
        f_norm1_ctx_bwd, (n_ctx,), [(xx, _rows(D_MODEL, 0, n_lat)), (dhh, _rows(D_MODEL, 0, n_lat)),
                                    (W["norm1_g"], vec(D_MODEL)), (csc1, vec(D_MODEL))], [acc_out(D_MODEL)] * 3,
        "norm1_ctx_bwd")

    d_w_uq = d_w_uq_p.reshape(Q_RANK, 4, 2, HEAD_PAD)[:, :, :, :QK_DIM].reshape(Q_RANK, 4, 2 * QK_DIM)
    big = {
        "w_in": _w_in_shards_from_p(d_w_in_p).astype(BF16),
        "w_uq": jnp.transpose(d_w_uq, (1, 0, 2)).astype(BF16),
        "w_ukv": d_w_ukv,
        "w_attn_out": jnp.transpose(d_w_ao_p.reshape(N_HEADS, HEAD_PAD, 4, 256)[:, 64:], (2, 0, 1, 3)).reshape(
            4, N_HEADS * 64, 256),
        "w_conv_out": d_w_co,
    }
    zero = jnp.zeros((1, 4 * D_MODEL), F32)
    small = {
        "dmod_lat": jnp.concatenate([dsh1, dsc1, dg1, dsh2, dsc2, dg2], axis=1),
        "dmod_ctx": jnp.concatenate([dcsh1, dcsc1, zero], axis=1),
        "norm1_g": dg_n1 + dg_n1c, "norm2_g": dg_n2, "final_g": dg_f, "q_norm_g": dg_q, "kv_norm_g": dg_kv,
        "conv_b": d_conv_b, "conv_w": d_conv_w.reshape(1, -1), "ffn_conv_b": d_ffn_conv_b,
        "ffn_conv_w": d_ffn_conv_w.reshape(1, -1),
    }
    return grad_x, loss, big, small


SMALL = (("dmod_lat", 6144), ("dmod_ctx", 6144), ("norm1_g", 1024), ("norm2_g", 1024), ("final_g", 1024),
         ("q_norm_g", 384), ("kv_norm_g", 256), ("conv_b", 512), ("conv_w", 1536), ("ffn_conv_b", 5632),
         ("ffn_conv_w", 16896))
SMALL_ROWS = 320


def _adamw(w, g, m, v, name):
    R, C = w.shape
    tr = 8 if R % 8 == 0 else R
    for t in range(8, R + 1, 8):
        if R % t == 0 and t * C * 4 <= (1 << 20):
            tr = t
    c1, c2 = 1.0 - ADAM_B1 ** ADAM_STEP, 1.0 - ADAM_B2 ** ADAM_STEP

    def fn(ids, w_, g_, m_, v_):
        m2 = ADAM_B1 * m_ + (1.0 - ADAM_B1) * g_
        v2 = ADAM_B2 * v_ + (1.0 - ADAM_B2) * (g_ * g_)
        delta = -ADAM_LR * ((m2 / c1) / (jnp.sqrt(v2 / c2) + ADAM_EPS) + ADAM_WD * w_)
        return delta, m2, v2

    spec = pl.BlockSpec((tr, C), lambda i: (i, 0))
    return _ew(fn, (R // tr,), [(w, spec), (g, spec), (m, spec), (v, spec)], [((R, C), F32, spec, None)] * 3, name)


def kernel(x, c, ctx, c_ctx, w_ada, b_ada, norm1_g, w_in, q_norm_g, kv_norm_g, w_uq, w_ukv, conv_w, conv_b, w_attn_out, w_conv_out, w_o, norm2_g, w_up, ffn_conv_w, ffn_conv_b, w_down, final_g, loss_target, m_c_ctx, m_w_ada, m_b_ada, m_norm1_g, m_w_in, m_q_norm_g, m_kv_norm_g, m_w_uq, m_w_ukv, m_conv_w, m_conv_b, m_w_attn_out, m_w_conv_out, m_w_o, m_norm2_g, m_w_up, m_ffn_conv_w, m_ffn_conv_b, m_w_down, m_final_g, v_c_ctx, v_w_ada, v_b_ada, v_norm1_g, v_w_in, v_q_norm_g, v_kv_norm_g, v_w_uq, v_w_ukv, v_conv_w, v_conv_b, v_w_attn_out, v_w_conv_out, v_w_o, v_norm2_g, v_w_up, v_ffn_conv_w, v_ffn_conv_b, v_w_down, v_final_g):
    mx, my, mc = lax.axis_index("x"), lax.axis_index("y"), lax.axis_index("c")
    chip = 2 * mx + my
    dev = 4 * mx + 2 * my + mc
    T, Tc = x.shape[1], ctx.shape[1]
    TT = T + Tc
    shards = {"w_in": w_in[0], "w_uq": w_uq[0], "w_ukv": w_ukv[0], "w_attn_out": w_attn_out[0],
              "w_conv_out": w_conv_out[0], "w_o": w_o[0], "w_up": w_up[0], "w_down": w_down[0]}

    conv_sh = jnp.concatenate([conv_w[0], ffn_conv_w[0]], axis=1)
    pay1 = jnp.concatenate([jnp.pad(c, ((0, 7), (0, 0))), jnp.pad(conv_sh, ((0, 5), (0, 0)))], axis=1)
    got1 = _allgather8(pay1, "gather_cond", in_vmem=True).reshape(8, 8, 2560)
    c_all = got1[:, 0, :D_MODEL]
    conv_all = got1[0::2, :3, D_MODEL:]
    conv_w_full = _cols_from_shards(conv_all[:, :, :128])
    ffn_conv_w_full = _cols_from_shards(conv_all[:, :, 128:])

    names = [n for n, _ in BIG]
    first = [n for n in names if n not in LATE]
    gathered, zero = _gather_weights([shards[n].astype(BF16) for n in first])
    full = dict(zip(first, gathered))
    xx = jnp.concatenate([x[0], ctx[0]], axis=0)
    late_bf = [(shards[n] + zero[0, 0]).astype(BF16) for n in LATE]
    g_send, g_recv, late_src, late_land, xx = _ici_start(
        "gather", late_bf, [(4,) + s.shape for s in late_bf], xx, "gather_late_start")

    def late_weights(after):
        src, land = _ici_wait("gather", g_send, g_recv, late_src, late_land, after, "gather_late_wait")
        got = dict(zip(LATE, _gather_finish(src, land)))
        return {"w_o": got["w_o"].reshape(D_MODEL, D_MODEL), "w_up": got["w_up"],
                "w_down": got["w_down"].reshape(D_FF, D_MODEL)}

    wuq = _cols_from_shards(full["w_uq"]).reshape(Q_RANK, N_HEADS, QK_DIM)
    wao = _cols_from_shards(full["w_attn_out"]).reshape(N_HEADS, 64, D_MODEL)
    W = {
        "w_in": _w_in_p_from_shards(full["w_in"]),
        "w_uq": jnp.pad(wuq, ((0, 0), (0, 0), (0, HEAD_PAD - QK_DIM))).reshape(Q_RANK, N_HEADS * HEAD_PAD),
        "w_ukv": full["w_ukv"],
        "w_attn_out": jnp.pad(wao, ((0, 0), (64, 0), (0, 0))).reshape(N_HEADS * HEAD_PAD, D_MODEL),
        "w_conv_out": full["w_conv_out"],
        "norm1_g": norm1_g, "norm2_g": norm2_g, "final_g": final_g.reshape(1, D_MODEL), "q_norm_g": q_norm_g,
        "kv_norm_g": kv_norm_g, "conv_w": conv_w_full, "conv_b": conv_b, "ffn_conv_w": ffn_conv_w_full,
        "ffn_conv_b": ffn_conv_b,
    }

    cond = jnp.concatenate([c_all, c_ctx.reshape(1, D_MODEL), jnp.zeros((7, D_MODEL), F32)], axis=0)

    def f_silu(ids, v):
        return (v * _sigmoid(v),)

    (s16,) = _ew(f_silu, (1,), [(cond, _full((16, D_MODEL)))], [((16, D_MODEL), F32, _full((16, D_MODEL)), None)], "silu_cond")
    mod_sh = _mm(s16, w_ada[0], "nn", 16, 1536, D_MODEL, tm=16, tn=768, tk=D_MODEL, name="w_ada_fwd")
    got2 = _allgather8(mod_sh, "gather_mod", in_vmem=True).reshape(4, 2, 16, 1536)[:, 0]
    mod_all = _cols_from_shards(got2) + b_ada
    mod_lat = lax.dynamic_slice_in_dim(mod_all, dev, 1, axis=0)
    mod_ctx = mod_all[8:9]

    place = jnp.stack([chip, mc]).astype(jnp.int32)
    early = {}

    def early_grads(g, carry):
        gs = [g[n] for n in LATE]
        from_sib = _rs_pair(gs, "rs_pair_late")
        sums = [_add_pair(gs[w], from_sib[w], place, "rs_pair_add_" + n) for w, n in enumerate(LATE)]
        early["send"], early["recv"], early["sums"], early["land"], carry = _ici_start(
            "scatter", sums, [(3,) + s.shape[1:] for s in sums], carry, "rs_chips_late_start")
        return carry

    grad_x, loss_part, gbig, gsmall = _local_step(xx, loss_target[0], mod_lat, mod_ctx, W, late_weights, early_grads)
    loss = lax.psum(loss_part[0, 0], ("x", "y", "c"))

    pay3 = jnp.concatenate([gsmall[n].reshape(-1) for n, _ in SMALL])
    pay3 = jnp.pad(pay3, (0, SMALL_ROWS * 128 - pay3.shape[0])).reshape(SMALL_ROWS, 128)
    got3 = _allgather8(pay3, "gather_small", in_vmem=True)

    def f_sum8(ids, a):
        s = a[0:SMALL_ROWS]
        for d in range(1, 8):
            s = s + a[d * SMALL_ROWS:(d + 1) * SMALL_ROWS]
        return (s,)

    (vsum,) = _ew(f_sum8, (1,), [(got3, _full((8 * SMALL_ROWS, 128)))],
                  [((SMALL_ROWS, 128), F32, _full((SMALL_ROWS, 128)), None)], "sum_small")
    vflat = vsum.reshape(-1)
    gvec, off = {}, 0
    for n, size in SMALL:
        gvec[n] = vflat[off:off + size]
        off += size
    dmod_rows = got3.reshape(8, SMALL_ROWS * 128)[:, :6 * D_MODEL]
    dm16 = jnp.concatenate([dmod_rows, gvec["dmod_ctx"].reshape(1, -1), jnp.zeros((7, 6 * D_MODEL), F32)], axis=0)

    def f_colsum(ids, a):
        return (_colsum(a),)

    (g_b_ada,) = _ew(f_colsum, (1,), [(dm16, _full((16, 6 * D_MODEL)))],
                     [((1, 6 * D_MODEL), F32, _full((1, 6 * D_MODEL)), None)], "b_ada_grad")
    dm_sh = lax.dynamic_slice_in_dim(dm16, chip * 1536, 1536, axis=1)
    g_w_ada = _mm(s16, dm_sh, "tn", D_MODEL, 1536, 16, tm=512, tn=768, tk=16, name="w_ada_dw")
    dcond_part = _mm(dm_sh, w_ada[0], "nt", 16, D_MODEL, 1536, tm=16, tn=512, tk=1536, name="w_ada_dx")
    got4 = _allgather8(dcond_part[8:16], "gather_dcond", in_vmem=True).reshape(4, 2, 8, D_MODEL)[:, 0, 0]

    def f_c_ctx(ids, parts, cc):
        s = _sigmoid(cc)
        d = parts[0:1] + parts[1:2] + parts[2:3] + parts[3:4]
        return (d * s * (1.0 + cc * (1.0 - s)),)

    (g_c_ctx,) = _ew(f_c_ctx, (1,), [(got4, _full((4, D_MODEL))), (c_ctx.reshape(1, D_MODEL), _full((1, D_MODEL)))],
                     [((1, D_MODEL), F32, _full((1, D_MODEL)), None)], "c_ctx_grad")

    from_sibling = _rs_pair([gbig[n] for n in first], "rs_pair")
    pair_sums = [_add_pair(gbig[n], from_sibling[w], place, "rs_pair_add_" + n) for w, n in enumerate(first)]
    lands = _rs_chips(pair_sums)
    late_sums, late_lands = _ici_wait("scatter", early["send"], early["recv"], early["sums"], early["land"], grad_x,
                                      "rs_chips_late_wait")
    half_sums = [_add_chips(a, b, place, "rs_chip_add_" + n)
                 for a, b, n in zip(pair_sums + late_sums, lands + late_lands, first + list(LATE))]
    gw = dict(zip(first + list(LATE), _rs_pair_back(half_sums)))
    gw["w_ada"] = g_w_ada

    moments = {"w_ada": (w_ada, m_w_ada, v_w_ada), "w_in": (w_in, m_w_in, v_w_in), "w_uq": (w_uq, m_w_uq, v_w_uq),
               "w_ukv": (w_ukv, m_w_ukv, v_w_ukv), "w_attn_out": (w_attn_out, m_w_attn_out, v_w_attn_out),
               "w_conv_out": (w_conv_out, m_w_conv_out, v_w_conv_out), "w_o": (w_o, m_w_o, v_w_o),
               "w_up": (w_up, m_w_up, v_w_up), "w_down": (w_down, m_w_down, v_w_down)}
    grads, deltas, new_m, new_v = {}, {}, {}, {}
    for n, (w_, m_, v_) in moments.items():
        d_, m2, v2 = _adamw(w_[0], gw[n], m_[0], v_[0], "adamw_" + n)
        grads[n], deltas[n], new_m[n], new_v[n] = gw[n][None], d_[None], m2[None], v2[None]

    conv_w_g = lax.dynamic_slice_in_dim(gvec["conv_w"].reshape(3, CONV_DIM), chip * 128, 128, axis=1)
    ffn_conv_w_g = lax.dynamic_slice_in_dim(gvec["ffn_conv_w"].reshape(3, 2 * D_FF), chip * 1408, 1408, axis=1)
    vec_params = (("c_ctx", c_ctx, m_c_ctx, v_c_ctx, g_c_ctx), ("b_ada", b_ada, m_b_ada, v_b_ada, g_b_ada),
                  ("norm1_g", norm1_g, m_norm1_g, v_norm1_g, gvec["norm1_g"]),
                  ("q_norm_g", q_norm_g, m_q_norm_g, v_q_norm_g, gvec["q_norm_g"]),
                  ("kv_norm_g", kv_norm_g, m_kv_norm_g, v_kv_norm_g, gvec["kv_norm_g"]),
                  ("conv_w", conv_w, m_conv_w, v_conv_w, conv_w_g), ("conv_b", conv_b, m_conv_b, v_conv_b, gvec["conv_b"]),
                  ("norm2_g", norm2_g, m_norm2_g, v_norm2_g, gvec["norm2_g"]),
                  ("ffn_conv_w", ffn_conv_w, m_ffn_conv_w, v_ffn_conv_w, ffn_conv_w_g),
                  ("ffn_conv_b", ffn_conv_b, m_ffn_conv_b, v_ffn_conv_b, gvec["ffn_conv_b"]),
                  ("final_g", final_g, m_final_g, v_final_g, gvec["final_g"]))
    total = sum(p[1].size for p in vec_params)
    rows_v = -(-total // 1024) * 8

    def packv(idx):
        flat_v = jnp.concatenate([p[idx].reshape(-1) for p in vec_params])
        return jnp.pad(flat_v, (0, rows_v * 128 - total)).reshape(rows_v, 128)

    vd, vm, vv = _adamw(packv(1), packv(4), packv(2), packv(3), "adamw_vectors")
    off = 0
    for p in vec_params:
        n, shape, size = p[0], p[1].shape, p[1].size
        grads[n] = p[4].reshape(shape)
        deltas[n] = vd.reshape(-1)[off:off + size].reshape(shape)
        new_m[n] = vm.reshape(-1)[off:off + size].reshape(shape)
        new_v[n] = vv.reshape(-1)[off:off + size].reshape(shape)
        off += size

    order = ("c_ctx", "w_ada", "b_ada", "norm1_g", "w_in", "q_norm_g", "kv_norm_g", "w_uq", "w_ukv", "conv_w", "conv_b",
             "w_attn_out", "w_conv_out", "w_o", "norm2_g", "w_up", "ffn_conv_w", "ffn_conv_b", "w_down", "final_g")
    return (loss, grad_x[None], *[grads[n] for n in order], *[deltas[n] for n in order],
            *[new_m[n] for n in order], *[new_v[n] for n in order])
```

```python
import functools

import jax
import jax.numpy as jnp
from jax import lax
from jax.experimental import pallas as pl
from jax.experimental.pallas import tpu as pltpu

F32, BF16 = jnp.float32, jnp.bfloat16
MESH = pl.DeviceIdType.MESH

D_MODEL = 1024
N_HEADS = 8
HEAD_PAD = 128
QK_DIM = 96
Q_RANK, KV_RANK = 384, 256
CONV_DIM = 512
D_FF = 2816
GRID_W = 64
ROPE_THETA = 10000.0
EPS = 1e-6
GA0, GC0, CX0, CB0, CC0, KV0, Q0, KR0, P_COLS = 0, 1024, 2048, 2560, 3072, 3584, 3840, 4224, 4352
ROW_TILE = 256
VMEM_LIMIT_BYTES = 48 * 1024 * 1024

ADAM_LR, ADAM_B1, ADAM_B2, ADAM_EPS, ADAM_WD, ADAM_STEP = 0.001, 0.9, 0.999, 1e-08, 0.01, 10

BIG = (("w_in", (1088, 1024)), ("w_uq", (192, 384)), ("w_ukv", (256, 256)), ("w_attn_out", (512, 256)),
       ("w_conv_out", (512, 256)), ("w_o", (256, 1024)), ("w_up", (1024, 1408)), ("w_down", (704, 1024)))

LATE = ("w_o", "w_up", "w_down")

NN = (((1,), (0,)), ((), ()))
NT = (((1,), (1,)), ((), ()))
TN = (((0,), (0,)), ((), ()))


def _cp(sem):
    return pltpu.CompilerParams(dimension_semantics=sem, vmem_limit_bytes=VMEM_LIMIT_BYTES)


def _pick(n, prefs):
    for p in prefs:
        if n % p == 0:
            return p
    return n


def _mm(a, b, mode, M, N, K, *, tm, tn, tk, name, out_dtype=F32, a_spec=None, b_spec=None, o_spec=None,
        out_shape=None, transpose_out=False):
    assert M % tm == 0 and N % tn == 0 and K % tk == 0, (name, M, N, K, tm, tn, tk)
    nk = K // tk
    dims = {"nn": NN, "nt": NT, "tn": TN}[mode]
    if a_spec is None:
        a_spec = (pl.BlockSpec((tk, tm), lambda i, j, k: (k, i)) if mode == "tn"
                  else pl.BlockSpec((tm, tk), lambda i, j, k: (i, k)))
    if b_spec is None:
        b_spec = (pl.BlockSpec((tn, tk), lambda i, j, k: (j, k)) if mode == "nt"
                  else pl.BlockSpec((tk, tn), lambda i, j, k: (k, j)))
    if o_spec is None:
        o_spec = (pl.BlockSpec((tn, tm), lambda i, j, k: (j, i)) if transpose_out
                  else pl.BlockSpec((tm, tn), lambda i, j, k: (i, j)))
    if out_shape is None:
        out_shape = (N, M) if transpose_out else (M, N)

    def emit(o_ref, val):
        o_ref[...] = (val.T if transpose_out else val).astype(o_ref.dtype)

    def body(a_ref, b_ref, o_ref, *scratch):
        part = lax.dot_general(a_ref[...].astype(BF16), b_ref[...].astype(BF16), dims, preferred_element_type=F32)
        if nk == 1:
            emit(o_ref, part)
            return
        acc_ref, = scratch
        k = pl.program_id(2)

        @pl.when(k == 0)
        def _():
            acc_ref[...] = part

        @pl.when((k > 0) & (k < nk - 1))
        def _():
            acc_ref[...] += part

        @pl.when(k == nk - 1)
        def _():
            emit(o_ref, acc_ref[...] + part)

    return pl.pallas_call(
        body, grid=(M // tm, N // tn, nk), in_specs=[a_spec, b_spec], out_specs=o_spec,
        out_shape=jax.ShapeDtypeStruct(out_shape, out_dtype),
        scratch_shapes=[pltpu.VMEM((tm, tn), F32)] if nk > 1 else [],
        compiler_params=_cp(("parallel", "parallel", "arbitrary")), name=name)(a, b)


def _ew(fn, grid, ins, outs, name, scalars=None):
    n_in = len(ins)
    n_sc = 0 if scalars is None else 1

    def store(ref, val, acc, ids):
        if isinstance(val, (list, tuple)):
            for h, v in enumerate(val):
                ref[h] = v.astype(ref.dtype)
            return
        if acc is None:
            ref[...] = val.astype(ref.dtype)
            return

        @pl.when(ids[acc] == 0)
        def _():
            ref[...] = val.astype(ref.dtype)

        @pl.when(ids[acc] > 0)
        def _():
            ref[...] += val.astype(ref.dtype)

    def body(*refs):
        refs = refs[n_sc:]
        ids = tuple(pl.program_id(a) for a in range(len(grid)))
        vals = fn(ids, *[r[...] for r in refs[:n_in]])
        for ref, val, (_, _, _, acc) in zip(refs[n_in:], vals, outs):
            store(ref, val, acc, ids)

    acc_axes = {o[3] for o in outs if o[3] is not None}
    sem = tuple("arbitrary" if a in acc_axes else "parallel" for a in range(len(grid)))
    in_specs, out_specs = [s for _, s in ins], [o[2] for o in outs]
    out_shape = [jax.ShapeDtypeStruct(o[0], o[1]) for o in outs]
    args = [a for a, _ in ins]
    if scalars is None:
        return pl.pallas_call(body, grid=grid, in_specs=in_specs, out_specs=out_specs, out_shape=out_shape,
                              compiler_params=_cp(sem), name=name)(*args)
    spec = pltpu.PrefetchScalarGridSpec(num_scalar_prefetch=1, grid=grid, in_specs=in_specs, out_specs=out_specs)
    return pl.pallas_call(body, grid_spec=spec, out_shape=out_shape, compiler_params=_cp(sem), name=name)(scalars, *args)


def _rows(width, cblk=0, roff=0, tr=ROW_TILE):
    return pl.BlockSpec((tr, width), lambda i: (i + roff, cblk))


def _full(shape):
    nd = len(shape)
    return pl.BlockSpec(shape, lambda *_: (0,) * nd)


def _sigmoid(x):
    return 1.0 / (1.0 + jnp.exp(-x))


def _rms(x):
    return lax.rsqrt(jnp.mean(x * x, axis=-1, keepdims=True) + EPS)


def _rms_bwd(dn, xn, r):
    return r * (dn - xn * jnp.mean(dn * xn, axis=-1, keepdims=True))


def _colsum(x):
    return jnp.sum(x, axis=0, keepdims=True)


def _shift_prev(x):
    rows = lax.broadcasted_iota(jnp.int32, x.shape, 0)
    return jnp.where(rows == 0, 0.0, pltpu.roll(x, 1, 0))


def _shift_next(x):
    rows = lax.broadcasted_iota(jnp.int32, x.shape, 0)
    return jnp.where(rows == x.shape[0] - 1, 0.0, pltpu.roll(x, x.shape[0] - 1, 0))


def _conv(x, w, b):
    return b + _shift_prev(x) * w[0:1] + x * w[1:2] + _shift_next(x) * w[2:3]


def _conv_bwd_x(dy, w):
    return _shift_next(dy) * w[0:1] + dy * w[1:2] + _shift_prev(dy) * w[2:3]


def _conv_bwd_w(dy, x):
    return _colsum(dy * _shift_prev(x)), _colsum(dy * x), _colsum(dy * _shift_next(x))


def _rope(x, cos, sin_lo, sin_hi):
    return x * cos + pltpu.roll(x, HEAD_PAD - 8, 1) * sin_lo + pltpu.roll(x, 8, 1) * sin_hi


ATTN_SCALE = QK_DIM ** -0.5


def _head_keys(kv_ref, kr_ref, cos_ref, slo_ref, shi_ref, kc_ref, vp_ref):
    kv = kv_ref[...]
    lane = lax.broadcasted_iota(jnp.int32, kv.shape, 1)
    kc_ref[...] = jnp.where(lane < 64, kv, _rope(kr_ref[...], cos_ref[...], slo_ref[...], shi_ref[...])).astype(BF16)
    vp_ref[...] = jnp.where(lane >= 64, kv, 0.0).astype(BF16)


def _attn_specs(tq, TT, clamp):
    row = (lambda i: jnp.minimum(i, clamp)) if clamp is not None else (lambda i: i)
    q = pl.BlockSpec((tq, HEAD_PAD), lambda h, i: (i, h))
    lat = pl.BlockSpec((tq, HEAD_PAD), lambda h, i: (row(i), h))
    keys = pl.BlockSpec((TT, HEAD_PAD), lambda h, i: (0, h))
    kr = pl.BlockSpec((TT, HEAD_PAD), lambda h, i: (0, KR0 // HEAD_PAD))
    tab_q = pl.BlockSpec((tq, HEAD_PAD), lambda h, i: (i, 0))
    tab_k = pl.BlockSpec((TT, HEAD_PAD), lambda h, i: (0, 0))
    lse = pl.BlockSpec((None, tq, 1), lambda h, i: (h, row(i), 0))
    return q, lat, keys, kr, tab_q, tab_k, lse


def _attn_fwd(q_raw, kv, pp, tabs, T, TT):
    tq = ROW_TILE
    cos, slo, shi = tabs

    def body(q_ref, kv_ref, kr_ref, cq, lq, hq, ck, lk, hk, o_ref, l_ref, kc, vp):
        @pl.when(pl.program_id(1) == 0)
        def _():
            _head_keys(kv_ref, kr_ref, ck, lk, hk, kc, vp)

        q = _rope(q_ref[...], cq[...], lq[...], hq[...]).astype(BF16)
        s = lax.dot_general(q, kc[...], NT, preferred_element_type=F32) * ATTN_SCALE
        m = jnp.max(s, axis=-1, keepdims=True)
        p = jnp.exp(s - m)
        l = jnp.sum(p, axis=-1, keepdims=True)
        o = lax.dot_general(p.astype(BF16), vp[...], NN, preferred_element_type=F32)
        o_ref[...] = o / l
        l_ref[...] = m + jnp.log(l)

    qs, _, keys, kr, tab_q, tab_k, lse = _attn_specs(tq, TT, None)
    return pl.pallas_call(
        body, grid=(N_HEADS, T // tq), in_specs=[qs, keys, kr, tab_q, tab_q, tab_q, tab_k, tab_k, tab_k],
        out_specs=[qs, lse],
        out_shape=[jax.ShapeDtypeStruct((T, N_HEADS * HEAD_PAD), F32), jax.ShapeDtypeStruct((N_HEADS, T, 1), F32)],
        scratch_shapes=[pltpu.VMEM((TT, HEAD_PAD), BF16), pltpu.VMEM((TT, HEAD_PAD), BF16)],
        compiler_params=_cp(("parallel", "arbitrary")), name="attn_fwd")(q_raw, kv, pp, cos, slo, shi, cos, slo, shi)


def _attn_bwd(q_raw, kv, pp, o, do, lse, tabs, tabs_inv, T, TT):
    tq = ROW_TILE
    nq = T // tq
    cos, slo, shi = tabs
    cos_i, slo_i, shi_i = tabs_inv

    def body(q_ref, kv_ref, kr_ref, cq, lq, hq, ck, lk, hk, iq, ilq, ihq, ik, ilk, ihk, o_ref, do_ref, l_ref,
             dq_ref, dkv_ref, dkr_ref, kc, vp, dk, dv):
        h, i = pl.program_id(0), pl.program_id(1)

        @pl.when(i == 0)
        def _():
            _head_keys(kv_ref, kr_ref, ck, lk, hk, kc, vp)
            dk[...] = jnp.zeros_like(dk)
            dv[...] = jnp.zeros_like(dv)

        @pl.when(i < nq)
        def _():
            q = _rope(q_ref[...], cq[...], lq[...], hq[...]).astype(BF16)
            k, v, d_o = kc[...], vp[...], do_ref[...]
            s = lax.dot_general(q, k, NT, preferred_element_type=F32) * ATTN_SCALE
            p = jnp.exp(s - l_ref[...])
            dob = d_o.astype(BF16)
            dp = lax.dot_general(dob, v, NT, preferred_element_type=F32)
            dd = jnp.sum(d_o * o_ref[...], axis=-1, keepdims=True)
            ds = (p * (dp - dd) * ATTN_SCALE).astype(BF16)
            dq = lax.dot_general(ds, k, NN, preferred_element_type=F32)
            dq_ref[...] = _rope(dq, iq[...], ilq[...], ihq[...]).astype(dq_ref.dtype)
            dk[...] += lax.dot_general(ds, q, TN, preferred_element_type=F32)
            dv[...] += lax.dot_general(p.astype(BF16), dob, TN, preferred_element_type=F32)

        @pl.when(i == nq)
        def _():
            dq_ref[...] = jnp.zeros_like(dq_ref)
            dkh = dk[...]
            lane = lax.broadcasted_iota(jnp.int32, dkh.shape, 1)
            dkv_ref[...] = jnp.where(lane < 64, dkh, dv[...]).astype(dkv_ref.dtype)
            rot = _rope(jnp.where((lane >= 64) & (lane < 96), dkh, 0.0), ik[...], ilk[...], ihk[...])

            @pl.when(h == 0)
            def _():
                dkr_ref[...] = rot

            @pl.when(h > 0)
            def _():
                dkr_ref[...] += rot

    qs, lat, keys, kr, tab_q, tab_k, lse_spec = _attn_specs(tq, TT, nq - 1)
    wide = jax.ShapeDtypeStruct((TT, N_HEADS * HEAD_PAD), BF16)
    return pl.pallas_call(
        body, grid=(N_HEADS, TT // tq),
        in_specs=[qs, keys, kr] + [tab_q] * 3 + [tab_k] * 3 + [tab_q] * 3 + [tab_k] * 3 + [lat, lat, lse_spec],
        out_specs=[qs, keys, pl.BlockSpec((TT, HEAD_PAD), lambda h, i: (0, 0))],
        out_shape=[wide, wide, jax.ShapeDtypeStruct((TT, HEAD_PAD), F32)],
        scratch_shapes=[pltpu.VMEM((TT, HEAD_PAD), BF16), pltpu.VMEM((TT, HEAD_PAD), BF16),
                        pltpu.VMEM((TT, HEAD_PAD), F32), pltpu.VMEM((TT, HEAD_PAD), F32)],
        compiler_params=_cp(("arbitrary", "arbitrary")), name="attn_bwd",
    )(q_raw, kv, pp, cos, slo, shi, cos, slo, shi, cos_i, slo_i, shi_i, cos_i, slo_i, shi_i, o, do, lse)


def _allgather8(x, name, in_vmem):
    m_per, n = x.shape

    def body(x_ref, out_ref, token, send_sems, recv_sems, local_sem):
        token[...] = jnp.zeros_like(token)
        mx, my, mc = lax.axis_index("x"), lax.axis_index("y"), lax.axis_index("c")
        me, sibling = (mx, my, mc), (mx, my, 1 - mc)
        chips = [(1 - mx, my), (mx, 1 - my), (1 - mx, 1 - my)]

        def rows(px, py, pc):
            return out_ref.at[pl.ds((4 * px + 2 * py + pc) * m_per, m_per), :]

        def copy(k, block, to, src=None):
            return pltpu.make_async_remote_copy(
                src_ref=rows(*block) if src is None else src, dst_ref=rows(*block),
                send_sem=send_sems.at[k], recv_sem=recv_sems.at[k], device_id=to, device_id_type=MESH)

        mine = pltpu.make_async_copy(x_ref, rows(*me), local_sem)
        mine.start()
        first = [copy(0, me, sibling, src=x_ref)]
        first += [copy(1 + j, me, (*chip, mc), src=x_ref) for j, chip in enumerate(chips)]
        for cp in first:
            cp.start()
        passed = [copy(4 + j, (*chip, mc), sibling) for j, chip in enumerate(chips)]
        for j, chip in enumerate(chips):
            copy(1 + j, (*chip, mc), me).wait_recv()
            passed[j].start()
        copy(0, sibling, me).wait_recv()
        for j, chip in enumerate(chips):
            copy(4 + j, (*chip, 1 - mc), me).wait_recv()
        for cp in first + passed:
            cp.wait_send()
        mine.wait()

    space = pltpu.VMEM if in_vmem else pl.ANY
    return pl.pallas_call(
        body, out_shape=[jax.ShapeDtypeStruct((8 * m_per, n), x.dtype), jax.ShapeDtypeStruct((8, 128), F32)],
        in_specs=[pl.BlockSpec(memory_space=space)],
        out_specs=[pl.BlockSpec(memory_space=space), pl.BlockSpec(memory_space=pltpu.VMEM)],
        scratch_shapes=[pltpu.SemaphoreType.DMA((7,)), pltpu.SemaphoreType.DMA((7,)), pltpu.SemaphoreType.DMA],
        name=name)(x)


def _hbm_specs(n):
    return [pl.BlockSpec(memory_space=pl.ANY)] * n


def _gather_weights(shards):
    n = len(shards)
    halves = [s.shape[0] // 2 for s in shards]

    def body(*refs):
        ins, outs = refs[:n], refs[n:2 * n]
        token, send_sems, recv_sems = refs[2 * n:]
        token[...] = jnp.zeros_like(token)
        mx, my, mc = lax.axis_index("x"), lax.axis_index("y"), lax.axis_index("c")
        j_me = 2 * mx + my
        chips = [(1 - mx, my), (mx, 1 - my), (1 - mx, 1 - my)]

        def half(w, chip_idx, hc):
            return outs[w].at[chip_idx, pl.ds(hc * halves[w], halves[w]), :]

        def copy(w, k, src, dst, to):
            return pltpu.make_async_remote_copy(src_ref=src, dst_ref=dst, send_sem=send_sems.at[w, k],
                                                recv_sem=recv_sems.at[w, k], device_id=to, device_id_type=MESH)

        sends = []
        for w in range(n):
            cp = copy(w, 6, ins[w], outs[w].at[j_me], (mx, my, 1 - mc))
            cp.start()
            sends.append(cp)
        for k, (px, py) in enumerate(chips):
            for w in range(n):
                cp = copy(w, k, ins[w].at[pl.ds(mc * halves[w], halves[w]), :], half(w, j_me, mc), (px, py, mc))
                cp.start()
                sends.append(cp)
        for k, (px, py) in enumerate(chips):
            for w in range(n):
                got = half(w, 2 * px + py, mc)
                copy(w, k, got, got, (px, py, mc)).wait_recv()
                cp = copy(w, 3 + k, got, got, (mx, my, 1 - mc))
                cp.start()
                sends.append(cp)
        for k, (px, py) in enumerate(chips):
            for w in range(n):
                got = half(w, 2 * px + py, 1 - mc)
                copy(w, 3 + k, got, got, (mx, my, 1 - mc)).wait_recv()
        for w in range(n):
            own = outs[w].at[j_me]
            copy(w, 6, own, own, (mx, my, 1 - mc)).wait_recv()
        for cp in sends:
            cp.wait_send()

    res = pl.pallas_call(
        body, out_shape=[jax.ShapeDtypeStruct((4,) + s.shape, s.dtype) for s in shards]
        + [jax.ShapeDtypeStruct((8, 128), F32)],
        in_specs=_hbm_specs(n), out_specs=_hbm_specs(n) + [pl.BlockSpec(memory_space=pltpu.VMEM)],
        scratch_shapes=[pltpu.SemaphoreType.DMA((n, 7)), pltpu.SemaphoreType.DMA((n, 7))],
        name="gather_weights")(*shards)
    return list(res[:n]), res[n]


def _rs_pair(gs, name):
    n = len(gs)
    halves = [g.shape[1] // 2 for g in gs]

    def body(*refs):
        ins, lands = refs[:n], refs[n:2 * n]
        send_sems, recv_sems = refs[2 * n:]
        mx, my, mc = lax.axis_index("x"), lax.axis_index("y"), lax.axis_index("c")
        copies = []
        for w in range(n):
            h = halves[w]
            cp = pltpu.make_async_remote_copy(
                src_ref=ins[w].at[:, pl.ds((1 - mc) * h, h), :], dst_ref=lands[w], send_sem=send_sems.at[w],
                recv_sem=recv_sems.at[w], device_id=(mx, my, 1 - mc), device_id_type=MESH)
            cp.start()
            copies.append(cp)
        for cp in copies:
            cp.wait()

    return pl.pallas_call(
        body, out_shape=[jax.ShapeDtypeStruct((4, h, g.shape[2]), g.dtype) for g, h in zip(gs, halves)],
        in_specs=_hbm_specs(n), out_specs=_hbm_specs(n),
        scratch_shapes=[pltpu.SemaphoreType.DMA((n,)), pltpu.SemaphoreType.DMA((n,))], name=name)(*gs)


def _rs_chips(parts):
    n = len(parts)

    def body(*refs):
        ins, lands = refs[:n], refs[n:2 * n]
        send_sems, recv_sems = refs[2 * n:]
        mx, my, mc = lax.axis_index("x"), lax.axis_index("y"), lax.axis_index("c")
        copies = []
        for k, (px, py) in enumerate([(1 - mx, my), (mx, 1 - my), (1 - mx, 1 - my)]):
            for w in range(n):
                cp = pltpu.make_async_remote_copy(
                    src_ref=ins[w].at[2 * px + py], dst_ref=lands[w].at[k], send_sem=send_sems.at[w, k],
                    recv_sem=recv_sems.at[w, k], device_id=(px, py, mc), device_id_type=MESH)
                cp.start()
                copies.append(cp)
        for cp in copies:
            cp.wait()

    return list(pl.pallas_call(
        body, out_shape=[jax.ShapeDtypeStruct((3,) + p.shape[1:], p.dtype) for p in parts],
        in_specs=_hbm_specs(n), out_specs=_hbm_specs(n),
        scratch_shapes=[pltpu.SemaphoreType.DMA((n, 3)), pltpu.SemaphoreType.DMA((n, 3))], name="rs_chips")(*parts))


def _rs_pair_back(gs):
    n = len(gs)

    def body(*refs):
        outs = refs[n:2 * n]
        send_sems, recv_sems = refs[2 * n:]
        mx, my, mc = lax.axis_index("x"), lax.axis_index("y"), lax.axis_index("c")
        copies = []
        for w in range(n):
            h = gs[w].shape[0] // 2
            mine = outs[w].at[pl.ds(mc * h, h), :]
            cp = pltpu.make_async_remote_copy(src_ref=mine, dst_ref=mine, send_sem=send_sems.at[w],
                                              recv_sem=recv_sems.at[w], device_id=(mx, my, 1 - mc), device_id_type=MESH)
            cp.start()
            copies.append(cp)
        for cp in copies:
            cp.wait()

    return pl.pallas_call(
        body, out_shape=[jax.ShapeDtypeStruct(g.shape, g.dtype) for g in gs],
        in_specs=_hbm_specs(n), out_specs=_hbm_specs(n), input_output_aliases={w: w for w in range(n)},
        scratch_shapes=[pltpu.SemaphoreType.DMA((n,)), pltpu.SemaphoreType.DMA((n,))], name="rs_pair_back")(*gs)


_HBM = pl.BlockSpec(memory_space=pltpu.HBM)
_SEM = pl.BlockSpec(memory_space=pltpu.SEMAPHORE)
_EFFECT = pltpu.SideEffectType.DATAFLOW_SIDE_EFFECTING


def _ici_copies(kind, srcs, lands, send_sems, recv_sems):
    n = len(srcs)
    mx, my, mc = lax.axis_index("x"), lax.axis_index("y"), lax.axis_index("c")
    j_me = 2 * mx + my
    copies = []
    for k, (px, py) in enumerate([(1 - mx, my), (mx, 1 - my), (1 - mx, 1 - my)]):
        for w in range(n):
            if kind == "gather":
                h = srcs[w].shape[0] // 2
                src, dst = srcs[w].at[pl.ds(mc * h, h), :], lands[w].at[j_me, pl.ds(mc * h, h), :]
            else:
                src, dst = srcs[w].at[2 * px + py], lands[w].at[k]
            copies.append(pltpu.make_async_remote_copy(
                src_ref=src, dst_ref=dst, send_sem=send_sems.at[3 * w + k], recv_sem=recv_sems.at[3 * w + k],
                device_id=(px, py, mc), device_id_type=MESH))
    return copies


def _ici_start(kind, srcs, land_shapes, carry, name):
    n = len(srcs)

    def body(*refs):
        ins, lands = refs[:n], refs[n:2 * n]
        send_sems, recv_sems = refs[2 * n + 1], refs[2 * n + 2]
        for cp in _ici_copies(kind, ins, lands, send_sems, recv_sems):
            cp.start()

    hbm = lambda a: pltpu.with_memory_space_constraint(a, pltpu.HBM)
    lands = [lax.empty(s, srcs[0].dtype) for s in land_shapes]
    args = [hbm(a) for a in list(srcs) + lands + [carry]]
    out_shape = ([pltpu.SemaphoreType.DMA((3 * n,)), pltpu.SemaphoreType.DMA((3 * n,))]
                 + [pltpu.HBM(a.shape, a.dtype) for a in args])
    res = pl.pallas_call(
        body, name=name, out_shape=out_shape, in_specs=[_HBM] * len(args), out_specs=[_SEM, _SEM] + [_HBM] * len(args),
        input_output_aliases={i: 2 + i for i in range(len(args))},
        compiler_params=pltpu.CompilerParams(has_side_effects=_EFFECT))(*args)
    return res[0], res[1], list(res[2:2 + n]), list(res[2 + n:2 + 2 * n]), res[2 + 2 * n]


def _ici_wait(kind, send_sems, recv_sems, srcs, lands, after, name):
    n = len(srcs)

    def body(*refs):
        ins, zones = refs[:n], refs[n:2 * n]
        for cp in _ici_copies(kind, ins, zones, refs[2 * n], refs[2 * n + 1]):
            cp.wait_send()
            cp.wait_recv()

    args = list(srcs) + list(lands)
    res = pl.pallas_call(
        body, name=name, out_shape=[pltpu.HBM(a.shape, a.dtype) for a in args],
        in_specs=[_HBM] * len(args) + [_SEM, _SEM, pl.BlockSpec(memory_space=pl.ANY)], out_specs=[_HBM] * len(args),
        input_output_aliases={i: i for i in range(len(args))},
        compiler_params=pltpu.CompilerParams(has_side_effects=_EFFECT))(*args, send_sems, recv_sems, after)
    return list(res[:n]), list(res[n:])


def _gather_finish(shards, lands):
    n = len(shards)

    def body(*refs):
        own, outs = refs[:n], refs[2 * n:3 * n]
        send_sems, recv_sems = refs[3 * n:]
        mx, my, mc = lax.axis_index("x"), lax.axis_index("y"), lax.axis_index("c")
        j_me = 2 * mx + my
        sibling = (mx, my, 1 - mc)
        copies = []

        def push(w, k, src, dst):
            cp = pltpu.make_async_remote_copy(src_ref=src, dst_ref=dst, send_sem=send_sems.at[w, k],
                                              recv_sem=recv_sems.at[w, k], device_id=sibling, device_id_type=MESH)
            cp.start()
            copies.append(cp)

        for w in range(n):
            h = shards[w].shape[0] // 2
            push(w, 3, own[w], outs[w].at[j_me])
            for k, (px, py) in enumerate([(1 - mx, my), (mx, 1 - my), (1 - mx, 1 - my)]):
                got = outs[w].at[2 * px + py, pl.ds(mc * h, h), :]
                push(w, k, got, got)
        for cp in copies:
            cp.wait()

    return pl.pallas_call(
        body, out_shape=[jax.ShapeDtypeStruct(l.shape, l.dtype) for l in lands],
        in_specs=_hbm_specs(2 * n), out_specs=_hbm_specs(n), input_output_aliases={n + w: w for w in range(n)},
        scratch_shapes=[pltpu.SemaphoreType.DMA((n, 4)), pltpu.SemaphoreType.DMA((n, 4))], name="gather_finish",
    )(*shards, *lands)


def _tile_rows(h, c, itemsize, mult):
    best = h
    for t in range(mult, h + 1, mult):
        if h % t == 0 and t * c * itemsize <= (1 << 21):
            best = t
    return best


def _add_pair(g, land, place, name):
    _, h, c = land.shape
    t = _tile_rows(h, c, 2, 16)
    nb = h // t
    return _ew(lambda ids, u, v: (u.astype(F32) + v.astype(F32),), (4, nb),
               [(g, pl.BlockSpec((None, t, c), lambda j, i, s: (j, s[1] * nb + i, 0))),
                (land, pl.BlockSpec((None, t, c), lambda j, i, s: (j, i, 0)))],
               [(land.shape, BF16, pl.BlockSpec((None, t, c), lambda j, i, s: (j, i, 0)), None)], name, scalars=place)[0]


def _add_chips(own, land, place, name):
    _, h, c = land.shape
    t = _tile_rows(h, c, 4, 16)
    nb = h // t

    def fn(ids, a, b):
        return (((a.astype(F32) + b[0].astype(F32)) + b[1].astype(F32)) + b[2].astype(F32),)

    return _ew(fn, (nb,), [(own, pl.BlockSpec((None, t, c), lambda i, s: (s[0], i, 0))),
                           (land, pl.BlockSpec((3, t, c), lambda i, s: (0, i, 0)))],
               [((2 * h, c), F32, pl.BlockSpec((t, c), lambda i, s: (s[1] * nb + i, 0)), None)], name, scalars=place)[0]


W_IN_SEGMENTS = ((0, 256, KV0), (256, 288, KR0 + 64), (288, 672, Q0), (672, 1184, CX0), (1184, 1696, CB0),
                 (1696, 2208, CC0), (2208, 3232, GA0), (3232, 4256, GC0))
W_IN_SHARD = 1064


W_IN_SHARD_PAD = 1088


def _w_in_t_p_from_shards(s):
    pieces = []
    for o0, o1, p0 in sorted(W_IN_SEGMENTS, key=lambda t: t[2]):
        if p0 == KR0 + 64:
            pieces.append(jnp.zeros((64, s.shape[2]), s.dtype))
        for j in range(4):
            lo, hi = max(o0, j * W_IN_SHARD), min(o1, (j + 1) * W_IN_SHARD)
            if lo < hi:
                pieces.append(s[j, lo - j * W_IN_SHARD:hi - j * W_IN_SHARD])
    pieces.append(jnp.zeros((32, s.shape[2]), s.dtype))
    return jnp.concatenate(pieces, axis=0)


def _w_in_t_shards_from_p(g):
    shards = []
    for j in range(4):
        pieces = []
        for o0, o1, p0 in W_IN_SEGMENTS:
            lo, hi = max(o0, j * W_IN_SHARD), min(o1, (j + 1) * W_IN_SHARD)
            if lo < hi:
                pieces.append(g[p0 + lo - o0:p0 + hi - o0])
        pieces.append(jnp.zeros((W_IN_SHARD_PAD - W_IN_SHARD, g.shape[1]), g.dtype))
        shards.append(jnp.concatenate(pieces, axis=0))
    return jnp.stack(shards, axis=0)


def _cols_from_shards(s):
    return jnp.transpose(s, (1, 0, 2)).reshape(s.shape[1], -1)


def _rope_tables(T, TT, inverse):
    rows = T // GRID_W
    row = jnp.repeat(jnp.arange(rows), GRID_W).astype(F32)
    col = jnp.tile(jnp.arange(GRID_W), rows).astype(F32)
    inv = ROPE_THETA ** (-jnp.arange(0, 16, 2, dtype=F32) / 16)
    ang = jnp.concatenate([row[:, None] * inv, col[:, None] * inv], axis=-1)
    cos, sin = jnp.cos(ang), jnp.sin(ang)
    lane = jnp.arange(32)
    src = (lane // 16) * 8 + lane % 8
    lo = ((lane % 16) // 8 == 0).astype(F32)
    sgn = -1.0 if inverse else 1.0
    cos32 = cos[:, src]
    sin_lo32 = -sgn * sin[:, src] * lo
    sin_hi32 = sgn * sin[:, src] * (1.0 - lo)

    def widen(t32, fill):
        t = jnp.concatenate([jnp.full((T, 64), fill, F32), t32, jnp.full((T, 32), fill, F32)], axis=1)
        return jnp.concatenate([t, jnp.full((TT - T, HEAD_PAD), fill, F32)], axis=0)

    return widen(cos32, 1.0), widen(sin_lo32, 0.0), widen(sin_hi32, 0.0)


def _local_step(xx, tgt, mod_lat, mod_ctx, W, late_weights, early_grads):
    TT = xx.shape[0]
    T = tgt.shape[0]
    n_lat, n_all = T // ROW_TILE, TT // ROW_TILE
    sh1, sc1, g1, sh2, sc2, g2 = [mod_lat[:, k * D_MODEL:(k + 1) * D_MODEL] for k in range(6)]
    csh1, csc1 = mod_ctx[:, :D_MODEL], mod_ctx[:, D_MODEL:2 * D_MODEL]
    vec = lambda n: _full((1, n))
    row_out = lambda n, dt, rows=T: ((rows, n), dt, _rows(n), None)
    acc_out = lambda n: ((1, n), F32, _full((1, n)), 0)

    def f_norm1(ids, x, g, a_sh, a_sc, b_sh, b_sc):
        ctx = ids[0] >= n_lat
        sh, sc = jnp.where(ctx, b_sh, a_sh), jnp.where(ctx, b_sc, a_sc)
        return ((x * _rms(x) * g) * (1.0 + sc) + sh,)

    (hh,) = _ew(f_norm1, (n_all,), [(xx, _rows(D_MODEL)), (W["norm1_g"], vec(D_MODEL)), (sh1, vec(D_MODEL)),
                                   (sc1, vec(D_MODEL)), (csh1, vec(D_MODEL)), (csc1, vec(D_MODEL))],
                [row_out(D_MODEL, BF16, TT)], "norm1_fwd")
    tm_all = _pick(TT, (768, 256))
    pp = _mm(hh, W["w_in_t"], "nt", TT, P_COLS, D_MODEL, tm=tm_all, tn=2176, tk=512, name="w_in_fwd")

    def f_lowrank(ids, ckv, cq, gkv, gq):
        return ckv * _rms(ckv) * gkv, cq * _rms(cq) * gq

    nkv, nq = _ew(f_lowrank, (n_all,), [(pp, _rows(KV_RANK, KV0 // KV_RANK)), (pp, _rows(Q_RANK, Q0 // Q_RANK)),
                                       (W["kv_norm_g"], vec(KV_RANK)), (W["q_norm_g"], vec(Q_RANK))],
                  [row_out(KV_RANK, BF16, TT), row_out(Q_RANK, BF16, TT)], "lowrank_norm_fwd")
    kv = _mm(nkv, W["w_ukv"], "nn", TT, 1024, KV_RANK, tm=tm_all, tn=256, tk=KV_RANK, name="w_ukv_fwd",
             b_spec=pl.BlockSpec((None, KV_RANK, 256), lambda i, j, k: (j, k, 0)))
    q_raw = _mm(nq, W["w_uq_t"], "nt", TT, 1024, Q_RANK, tm=tm_all, tn=1024, tk=Q_RANK, name="w_uq_fwd")

    tabs = _rope_tables(T, TT, inverse=False)
    tabs_inv = _rope_tables(T, TT, inverse=True)
    o_pad, lse = _attn_fwd(q_raw, kv, pp, tabs, T, TT)
    tm_lat = _pick(T, (1024, 512, 256))
    ya = _mm(o_pad, W["w_attn_out"], "nn", T, D_MODEL, 1024, tm=tm_lat, tn=D_MODEL, tk=512, name="w_attn_out_fwd")

    tc = 256
    colT = lambda blk0: pl.BlockSpec((T, tc), lambda j: (0, blk0 + j))

    def f_conv(ids, xin, cb, cc, w, b):
        return (cb * _conv(cc * xin, w, b),)

    (e,) = _ew(f_conv, (CONV_DIM // tc,),
               [(pp, colT(CX0 // tc)), (pp, colT(CB0 // tc)), (pp, colT(CC0 // tc)),
                (W["conv_w"], pl.BlockSpec((3, tc), lambda j: (0, j))), (W["conv_b"], pl.BlockSpec((1, tc), lambda j: (0, j)))],
               [((T, CONV_DIM), BF16, colT(0), None)], "conv_fwd")
    yc = _mm(e, W["w_conv_out"], "nn", T, D_MODEL, CONV_DIM, tm=tm_lat, tn=256, tk=CONV_DIM, name="w_conv_out_fwd",
             b_spec=pl.BlockSpec((None, CONV_DIM, 256), lambda i, j, k: (j, k, 0)))

    def f_merge(ids, ga, gc, a, c):
        return (_sigmoid(ga) * a + _sigmoid(gc) * c,)

    (mrg,) = _ew(f_merge, (n_lat,), [(pp, _rows(D_MODEL, 0)), (pp, _rows(D_MODEL, 1)), (ya, _rows(D_MODEL)),
                                    (yc, _rows(D_MODEL))], [row_out(D_MODEL, BF16)], "merge_fwd")
    W = dict(W, **late_weights(mrg))
    mo = _mm(mrg, W["w_o"], "nn", T, D_MODEL, D_MODEL, tm=tm_lat, tn=D_MODEL, tk=512, name="w_o_fwd")

    def f_norm2(ids, x, m, gate, g, sh, sc):
        x1 = x + gate * m
        return x1, (x1 * _rms(x1) * g) * (1.0 + sc) + sh

    x1, h2 = _ew(f_norm2, (n_lat,), [(xx, _rows(D_MODEL)), (mo, _rows(D_MODEL)), (g1, vec(D_MODEL)),
                                    (W["norm2_g"], vec(D_MODEL)), (sh2, vec(D_MODEL)), (sc2, vec(D_MODEL))],
                 [row_out(D_MODEL, F32), row_out(D_MODEL, BF16)], "norm2_fwd")
    up = _mm(h2, W["w_up"], "nn", T, 2 * D_FF, D_MODEL, tm=tm_lat, tn=1408, tk=512, name="w_up_fwd",
             b_spec=pl.BlockSpec((None, 512, 1408), lambda i, j, k: (j, k, 0)))

    n_ff = D_FF // tc
    ffw = lambda off, n=3: pl.BlockSpec((n, tc), lambda j: (0, j + off))

    def f_ffn(ids, ug, uv, wg, wv, bg, bv):
        gate, val = _conv(ug, wg, bg), _conv(uv, wv, bv)
        return (gate * _sigmoid(gate) * val,)

    (act,) = _ew(f_ffn, (n_ff,), [(up, colT(0)), (up, colT(n_ff)), (W["ffn_conv_w"], ffw(0)), (W["ffn_conv_w"], ffw(n_ff)),
                                 (W["ffn_conv_b"], ffw(0, 1)), (W["ffn_conv_b"], ffw(n_ff, 1))],
                 [((T, D_FF), BF16, colT(0), None)], "ffn_act_fwd")
    f = _mm(act, W["w_down"], "nn", T, D_MODEL, D_FF, tm=tm_lat, tn=D_MODEL, tk=1408, name="w_down_fwd")

    def f_head(ids, x1_, f_, gate, gf, t):
        x2 = x1_ + gate * f_
        r = _rms(x2)
        xn = x2 * r
        err = xn * gf - t
        loss = 0.5 * jnp.sum(jnp.mean(err * err, axis=-1, keepdims=True))
        dy = err * (1.0 / D_MODEL)
        dx2 = _rms_bwd(dy * gf, xn, r)
        return dx2, dx2 * gate, _colsum(dy * xn), _colsum(dx2 * f_), jnp.full((1, 128), loss, F32)

    dx2, df, dg_f, dg2, loss = _ew(
        f_head, (n_lat,), [(x1, _rows(D_MODEL)), (f, _rows(D_MODEL)), (g2, vec(D_MODEL)), (W["final_g"], vec(D_MODEL)),
                           (tgt, _rows(D_MODEL))],
        [row_out(D_MODEL, F32), row_out(D_MODEL, BF16), acc_out(D_MODEL), acc_out(D_MODEL), acc_out(128)], "loss_head")

    d_w_down = _mm(act, df, "tn", D_FF, D_MODEL, T, tm=1408, tn=D_MODEL, tk=_pick(T, (512, 256)), name="w_down_dw",
                   out_dtype=BF16).reshape(4, D_FF // 4, D_MODEL)
    da = _mm(df, W["w_down"], "nt", T, D_FF, D_MODEL, tm=tm_lat, tn=1408, tk=512, name="w_down_dx")

    tcb = 128
    n_fb = D_FF // tcb
    colb = lambda blk0: pl.BlockSpec((T, tcb), lambda j: (0, blk0 + j))
    ffwb = lambda off, n=3: pl.BlockSpec((n, tcb), lambda j: (0, j + off))
    cvec = ((1, D_FF), F32, pl.BlockSpec((1, tcb), lambda j: (0, j)), None)

    def f_ffn_bwd(ids, ug, uv, d_act, wg, wv, bg, bv):
        gate, val = _conv(ug, wg, bg), _conv(uv, wv, bv)
        s = _sigmoid(gate)
        d_gate = d_act * val * s * (1.0 + gate * (1.0 - s))
        d_val = d_act * gate * s
        wg0, wg1, wg2 = _conv_bwd_w(d_gate, ug)
        wv0, wv1, wv2 = _conv_bwd_w(d_val, uv)
        d_up = [_conv_bwd_x(d_gate, wg), _conv_bwd_x(d_val, wv)]
        return d_up, _colsum(d_gate), _colsum(d_val), wg0, wg1, wg2, wv0, wv1, wv2

    ffn_b = _ew(f_ffn_bwd, (n_fb,),
                [(up, colb(0)), (up, colb(n_fb)), (da, colb(0)), (W["ffn_conv_w"], ffwb(0)), (W["ffn_conv_w"], ffwb(n_fb)),
                 (W["ffn_conv_b"], ffwb(0, 1)), (W["ffn_conv_b"], ffwb(n_fb, 1))],
                [((2, T, D_FF), BF16, pl.BlockSpec((2, T, tcb), lambda j: (0, 0, j)), None)] + [cvec] * 8, "ffn_act_bwd")
    d_up3 = ffn_b[0]
    d_ffn_conv_b = jnp.concatenate([ffn_b[1], ffn_b[2]], axis=1)
    d_ffn_conv_w = jnp.concatenate([jnp.concatenate(ffn_b[3:6], axis=0), jnp.concatenate(ffn_b[6:9], axis=0)], axis=1)

    tk_t = _pick(T, (512, 256))
    d_w_up = _mm(h2, d_up3, "tn", D_MODEL, 2 * D_FF, T, tm=D_MODEL, tn=1408, tk=tk_t, name="w_up_dw", out_dtype=BF16,
                 b_spec=pl.BlockSpec((None, tk_t, 1408), lambda i, j, k: (j // 2, k, j % 2)),
                 o_spec=pl.BlockSpec((None, D_MODEL, 1408), lambda i, j, k: (j, i, 0)), out_shape=(4, D_MODEL, 1408))
    dh2 = _mm(d_up3, W["w_up"], "nt", T, D_MODEL, 2 * D_FF, tm=tm_lat, tn=D_MODEL, tk=1408, name="w_up_dx",
              a_spec=pl.BlockSpec((None, tm_lat, 1408), lambda i, j, k: (k // 2, i, k % 2)),
              b_spec=pl.BlockSpec((None, D_MODEL, 1408), lambda i, j, k: (k, j, 0)))

    def f_norm2_bwd(ids, dx2_, dh, x1_, m, g, sc, gate):
        r = _rms(x1_)
        xn = x1_ * r
        dx1 = dx2_ + _rms_bwd(dh * g * (1.0 + sc), xn, r)
        return dx1, dx1 * gate, _colsum(dh), _colsum(dh * xn * g), _colsum(dh * xn * (1.0 + sc)), _colsum(dx1 * m)

    dx1, dmo, dsh2, dsc2, dg_n2, dg1 = _ew(
        f_norm2_bwd, (n_lat,), [(dx2, _rows(D_MODEL)), (dh2, _rows(D_MODEL)), (x1, _rows(D_MODEL)), (mo, _rows(D_MODEL)),
                                (W["norm2_g"], vec(D_MODEL)), (sc2, vec(D_MODEL)), (g1, vec(D_MODEL))],
        [row_out(D_MODEL, F32), row_out(D_MODEL, BF16)] + [acc_out(D_MODEL)] * 4, "norm2_bwd")
    d_w_o = _mm(mrg, dmo, "tn", D_MODEL, D_MODEL, T, tm=D_MODEL, tn=D_MODEL, tk=tk_t, name="w_o_dw",
                out_dtype=BF16).reshape(4, D_MODEL // 4, D_MODEL)
    dmrg = _mm(dmo, W["w_o"], "nt", T, D_MODEL, D_MODEL, tm=tm_lat, tn=D_MODEL, tk=512, name="w_o_dx")
    dmrg = early_grads({"w_o": d_w_o, "w_up": d_w_up, "w_down": d_w_down}, dmrg)

    def f_merge_bwd(ids, dm, ga, gc, a, c):
        sa, sc_ = _sigmoid(ga), _sigmoid(gc)
        return dm * sa, dm * sc_, dm * a * sa * (1.0 - sa), dm * c * sc_ * (1.0 - sc_)

    dya, dyc, dp_ga, dp_gc = _ew(
        f_merge_bwd, (n_lat,), [(dmrg, _rows(D_MODEL)), (pp, _rows(D_MODEL, 0)), (pp, _rows(D_MODEL, 1)),
                                (ya, _rows(D_MODEL)), (yc, _rows(D_MODEL))], [row_out(D_MODEL, BF16)] * 4, "merge_bwd")

    d_w_ao_p = _mm(o_pad, dya, "tn", 1024, D_MODEL, T, tm=1024, tn=D_MODEL, tk=tk_t, name="w_attn_out_dw", out_dtype=BF16)
    do_pad = _mm(dya, W["w_attn_out"], "nt", T, 1024, D_MODEL, tm=tm_lat, tn=1024, tk=512, name="w_attn_out_dx")
    d_w_co = _mm(e, dyc, "tn", CONV_DIM, D_MODEL, T, tm=CONV_DIM, tn=256, tk=tk_t, name="w_conv_out_dw", out_dtype=BF16,
                 o_spec=pl.BlockSpec((None, CONV_DIM, 256), lambda i, j, k: (j, i, 0)), out_shape=(4, CONV_DIM, 256))
    de = _mm(dyc, W["w_conv_out"], "nt", T, CONV_DIM, D_MODEL, tm=tm_lat, tn=CONV_DIM, tk=256, name="w_conv_out_dx",
             b_spec=pl.BlockSpec((None, CONV_DIM, 256), lambda i, j, k: (k, j, 0)))

    def f_conv_bwd(ids, xin, cb, cc, d_e, w, b):
        z = cc * xin
        cz = _conv(z, w, b)
        dcz = d_e * cb
        w0, w1, w2 = _conv_bwd_w(dcz, z)
        dz = _conv_bwd_x(dcz, w)
        return dz * cc, d_e * cz, dz * xin, _colsum(dcz), w0, w1, w2

    cvec_c = ((1, CONV_DIM), F32, pl.BlockSpec((1, tc), lambda j: (0, j)), None)
    conv_b = _ew(f_conv_bwd, (CONV_DIM // tc,),
                 [(pp, colT(CX0 // tc)), (pp, colT(CB0 // tc)), (pp, colT(CC0 // tc)), (de, colT(0)),
                  (W["conv_w"], pl.BlockSpec((3, tc), lambda j: (0, j))), (W["conv_b"], pl.BlockSpec((1, tc), lambda j: (0, j)))],
                 [((T, CONV_DIM), BF16, colT(0), None)] * 3 + [cvec_c] * 4, "conv_bwd")
    dp_cx, dp_cb, dp_cc, d_conv_b = conv_b[:4]
    d_conv_w = jnp.concatenate(conv_b[4:7], axis=0)

    dq_raw, dkv, dp_kr = _attn_bwd(q_raw, kv, pp, o_pad, do_pad, lse, tabs, tabs_inv, T, TT)

    tk_a = _pick(TT, (768, 256))
    d_w_uq_t = _mm(nq, dq_raw, "tn", Q_RANK, 1024, TT, tm=Q_RANK, tn=1024, tk=tk_a, name="w_uq_dw", transpose_out=True)
    dnq = _mm(dq_raw, W["w_uq_t"], "nn", TT, Q_RANK, 1024, tm=tm_all, tn=Q_RANK, tk=512, name="w_uq_dx")
    d_w_ukv = _mm(nkv, dkv, "tn", KV_RANK, 1024, TT, tm=KV_RANK, tn=256, tk=tk_a, name="w_ukv_dw", out_dtype=BF16,
                  o_spec=pl.BlockSpec((None, KV_RANK, 256), lambda i, j, k: (j, i, 0)), out_shape=(4, KV_RANK, 256))
    dnkv = _mm(dkv, W["w_ukv"], "nt", TT, KV_RANK, 1024, tm=tm_all, tn=KV_RANK, tk=256, name="w_ukv_dx",
               b_spec=pl.BlockSpec((None, KV_RANK, 256), lambda i, j, k: (k, j, 0)))

    def f_lowrank_bwd(ids, ckv, cq, dkv_, dq_, gkv, gq):
        rk, rq = _rms(ckv), _rms(cq)
        nk, nq_ = ckv * rk, cq * rq
        return (_rms_bwd(dkv_ * gkv, nk, rk), _rms_bwd(dq_ * gq, nq_, rq), _colsum(dkv_ * nk), _colsum(dq_ * nq_))

    dp_kv, dp_q, dg_kv, dg_q = _ew(
        f_lowrank_bwd, (n_all,), [(pp, _rows(KV_RANK, KV0 // KV_RANK)), (pp, _rows(Q_RANK, Q0 // Q_RANK)),
                                  (dnkv, _rows(KV_RANK)), (dnq, _rows(Q_RANK)), (W["kv_norm_g"], vec(KV_RANK)),
                                  (W["q_norm_g"], vec(Q_RANK))],
        [row_out(KV_RANK, BF16, TT), row_out(Q_RANK, BF16, TT), acc_out(KV_RANK), acc_out(Q_RANK)], "lowrank_norm_bwd")

    lat_cols = jnp.concatenate([dp_ga, dp_gc, dp_cx, dp_cb, dp_cc], axis=1)
    dpp = jnp.concatenate([jnp.pad(lat_cols, ((0, TT - T), (0, 0))), dp_kv, dp_q, dp_kr.astype(BF16)], axis=1)
    d_w_in_t = _mm(hh, dpp, "tn", D_MODEL, P_COLS, TT, tm=512, tn=2176, tk=_pick(TT, (256,)), name="w_in_dw",
                   transpose_out=True)
    dhh = _mm(dpp, W["w_in_t"], "nn", TT, D_MODEL, P_COLS, tm=tm_all, tn=512, tk=2176, name="w_in_dx")

    def f_norm1_bwd(ids, x, dh, dres, g, sc):
        r = _rms(x)
        xn = x * r
        return (dres + _rms_bwd(dh * g * (1.0 + sc), xn, r), _colsum(dh), _colsum(dh * xn * g),
                _colsum(dh * xn * (1.0 + sc)))

    grad_x, dsh1, dsc1, dg_n1 = _ew(
        f_norm1_bwd, (n_lat,), [(xx, _rows(D_MODEL)), (dhh, _rows(D_MODEL)), (dx1, _rows(D_MODEL)),
                                (W["norm1_g"], vec(D_MODEL)), (sc1, vec(D_MODEL))],
        [row_out(D_MODEL, F32)] + [acc_out(D_MODEL)] * 3, "norm1_bwd")

    def f_norm1_ctx_bwd(ids, x, dh, g, sc):
        xn = x * _rms(x)
        return _colsum(dh), _colsum(dh * xn * g), _colsum(dh * xn * (1.0 + sc))

    n_ctx = n_all - n_lat
    dcsh1, dcsc1, dg_n1c = _ew(
        f_norm1_ctx_bwd, (n_ctx,), [(xx, _rows(D_MODEL, 0, n_lat)), (dhh, _rows(D_MODEL, 0, n_lat)),
                                    (W["norm1_g"], vec(D_MODEL)), (csc1, vec(D_MODEL))], [acc_out(D_MODEL)] * 3,
        "norm1_ctx_bwd")

    big = {
        "w_in": _w_in_t_shards_from_p(d_w_in_t).astype(BF16),
        "w_uq": d_w_uq_t.reshape(4, 2, HEAD_PAD, Q_RANK)[:, :, :QK_DIM].reshape(4, 2 * QK_DIM, Q_RANK).astype(BF16),
        "w_ukv": d_w_ukv,
        "w_attn_out": jnp.transpose(d_w_ao_p.reshape(N_HEADS, HEAD_PAD, 4, 256)[:, 64:], (2, 0, 1, 3)).reshape(
            4, N_HEADS * 64, 256),
        "w_conv_out": d_w_co,
    }
    zero = jnp.zeros((1, 4 * D_MODEL), F32)
    small = {
        "dmod_lat": jnp.concatenate([dsh1, dsc1, dg1, dsh2, dsc2, dg2], axis=1),
        "dmod_ctx": jnp.concatenate([dcsh1, dcsc1, zero], axis=1),
        "norm1_g": dg_n1 + dg_n1c, "norm2_g": dg_n2, "final_g": dg_f, "q_norm_g": dg_q, "kv_norm_g": dg_kv,
        "conv_b": d_conv_b, "conv_w": d_conv_w.reshape(1, -1), "ffn_conv_b": d_ffn_conv_b,
        "ffn_conv_w": d_ffn_conv_w.reshape(1, -1),
    }
    return grad_x, loss, big, small


SMALL = (("dmod_lat", 6144), ("dmod_ctx", 6144), ("norm1_g", 1024), ("norm2_g", 1024), ("final_g", 1024),
         ("q_norm_g", 384), ("kv_norm_g", 256), ("conv_b", 512), ("conv_w", 1536), ("ffn_conv_b", 5632),
         ("ffn_conv_w", 16896))
SMALL_ROWS = 320


def _adamw(w, g, m, v, name):
    R, C = w.shape
    tr = 8 if R % 8 == 0 else R
    for t in range(8, R + 1, 8):
        if R % t == 0 and t * C * 4 <= (1 << 20):
            tr = t
    c1, c2 = 1.0 - ADAM_B1 ** ADAM_STEP, 1.0 - ADAM_B2 ** ADAM_STEP

    def fn(ids, w_, g_, m_, v_):
        m2 = ADAM_B1 * m_ + (1.0 - ADAM_B1) * g_
        v2 = ADAM_B2 * v_ + (1.0 - ADAM_B2) * (g_ * g_)
        delta = -ADAM_LR * ((m2 / c1) / (jnp.sqrt(v2 / c2) + ADAM_EPS) + ADAM_WD * w_)
        return delta, m2, v2

    spec = pl.BlockSpec((tr, C), lambda i: (i, 0))
    return _ew(fn, (R // tr,), [(w, spec), (g, spec), (m, spec), (v, spec)], [((R, C), F32, spec, None)] * 3, name)


def kernel(x, c, ctx, c_ctx, w_ada, b_ada, norm1_g, w_in, q_norm_g, kv_norm_g, w_uq, w_ukv, conv_w, conv_b, w_attn_out, w_conv_out, w_o, norm2_g, w_up, ffn_conv_w, ffn_conv_b, w_down, final_g, loss_target, m_c_ctx, m_w_ada, m_b_ada, m_norm1_g, m_w_in, m_q_norm_g, m_kv_norm_g, m_w_uq, m_w_ukv, m_conv_w, m_conv_b, m_w_attn_out, m_w_conv_out, m_w_o, m_norm2_g, m_w_up, m_ffn_conv_w, m_ffn_conv_b, m_w_down, m_final_g, v_c_ctx, v_w_ada, v_b_ada, v_norm1_g, v_w_in, v_q_norm_g, v_kv_norm_g, v_w_uq, v_w_ukv, v_conv_w, v_conv_b, v_w_attn_out, v_w_conv_out, v_w_o, v_norm2_g, v_w_up, v_ffn_conv_w, v_ffn_conv_b, v_w_down, v_final_g):
    mx, my, mc = lax.axis_index("x"), lax.axis_index("y"), lax.axis_index("c")
    chip = 2 * mx + my
    dev = 4 * mx + 2 * my + mc
    T, Tc = x.shape[1], ctx.shape[1]
    TT = T + Tc
    w_in_t, m_w_in_t, v_w_in_t = (jnp.transpose(a[0]) for a in (w_in, m_w_in, v_w_in))
    w_uq_t, m_w_uq_t, v_w_uq_t = (jnp.transpose(a[0]) for a in (w_uq, m_w_uq, v_w_uq))
    shards = {"w_in": jnp.pad(w_in_t, ((0, W_IN_SHARD_PAD - W_IN_SHARD), (0, 0))), "w_uq": w_uq_t, "w_ukv": w_ukv[0],
              "w_attn_out": w_attn_out[0], "w_conv_out": w_conv_out[0], "w_o": w_o[0], "w_up": w_up[0],
              "w_down": w_down[0]}

    conv_sh = jnp.concatenate([conv_w[0], ffn_conv_w[0]], axis=1)
    pay1 = jnp.concatenate([jnp.pad(c, ((0, 7), (0, 0))), jnp.pad(conv_sh, ((0, 5), (0, 0)))], axis=1)
    got1 = _allgather8(pay1, "gather_cond", in_vmem=True)[0].reshape(8, 8, 2560)
    c_all = got1[:, 0, :D_MODEL]
    conv_all = got1[0::2, :3, D_MODEL:]
    conv_w_full = _cols_from_shards(conv_all[:, :, :128])
    ffn_conv_w_full = _cols_from_shards(conv_all[:, :, 128:])

    cond = jnp.concatenate([c_all, c_ctx.reshape(1, D_MODEL), jnp.zeros((7, D_MODEL), F32)], axis=0)

    def f_silu(ids, v):
        return (v * _sigmoid(v),)

    (s16,) = _ew(f_silu, (1,), [(cond, _full((16, D_MODEL)))], [((16, D_MODEL), F32, _full((16, D_MODEL)), None)], "silu_cond")
    mod_sh = _mm(s16, w_ada[0], "nn", 16, 1536, D_MODEL, tm=16, tn=768, tk=D_MODEL, name="w_ada_fwd")
    got2, after_mod = _allgather8(mod_sh, "gather_mod", in_vmem=True)
    mod_all = _cols_from_shards(got2.reshape(4, 2, 16, 1536)[:, 0]) + b_ada
    mod_lat = lax.dynamic_slice_in_dim(mod_all, dev, 1, axis=0)
    mod_ctx = mod_all[8:9]

    names = [n for n, _ in BIG]
    first = [n for n in names if n not in LATE]
    gathered, zero = _gather_weights([(shards[n] + after_mod[0, 0]).astype(BF16) for n in first])
    full = dict(zip(first, gathered))
    xx = jnp.concatenate([x[0], ctx[0]], axis=0)
    late_bf = [(shards[n] + zero[0, 0]).astype(BF16) for n in LATE]
    g_send, g_recv, late_src, late_land, xx = _ici_start(
        "gather", late_bf, [(4,) + s.shape for s in late_bf], xx, "gather_late_start")

    def late_weights(after):
        src, land = _ici_wait("gather", g_send, g_recv, late_src, late_land, after, "gather_late_wait")
        got = dict(zip(LATE, _gather_finish(src, land)))
        return {"w_o": got["w_o"].reshape(D_MODEL, D_MODEL), "w_up": got["w_up"],
                "w_down": got["w_down"].reshape(D_FF, D_MODEL)}

    wuq_t = full["w_uq"].reshape(N_HEADS, QK_DIM, Q_RANK)
    wao = _cols_from_shards(full["w_attn_out"]).reshape(N_HEADS, 64, D_MODEL)
    W = {
        "w_in_t": _w_in_t_p_from_shards(full["w_in"]),
        "w_uq_t": jnp.pad(wuq_t, ((0, 0), (0, HEAD_PAD - QK_DIM), (0, 0))).reshape(N_HEADS * HEAD_PAD, Q_RANK),
        "w_ukv": full["w_ukv"],
        "w_attn_out": jnp.pad(wao, ((0, 0), (64, 0), (0, 0))).reshape(N_HEADS * HEAD_PAD, D_MODEL),
        "w_conv_out": full["w_conv_out"],
        "norm1_g": norm1_g, "norm2_g": norm2_g, "final_g": final_g.reshape(1, D_MODEL), "q_norm_g": q_norm_g,
        "kv_norm_g": kv_norm_g, "conv_w": conv_w_full, "conv_b": conv_b, "ffn_conv_w": ffn_conv_w_full,
        "ffn_conv_b": ffn_conv_b,
    }

    place = jnp.stack([chip, mc]).astype(jnp.int32)
    early = {}

    def early_grads(g, carry):
        gs = [g[n] for n in LATE]
        from_sib = _rs_pair(gs, "rs_pair_late")
        sums = [_add_pair(gs[w], from_sib[w], place, "rs_pair_add_" + n) for w, n in enumerate(LATE)]
        early["send"], early["recv"], early["sums"], early["land"], carry = _ici_start(
            "scatter", sums, [(3,) + s.shape[1:] for s in sums], carry, "rs_chips_late_start")
        return carry

    grad_x, loss_part, gbig, gsmall = _local_step(xx, loss_target[0], mod_lat, mod_ctx, W, late_weights, early_grads)
    loss = lax.psum(loss_part[0, 0], ("x", "y", "c"))

    pay3 = jnp.concatenate([gsmall[n].reshape(-1) for n, _ in SMALL])
    pay3 = jnp.pad(pay3, (0, SMALL_ROWS * 128 - pay3.shape[0])).reshape(SMALL_ROWS, 128)
    got3 = _allgather8(pay3, "gather_small", in_vmem=True)[0]

    def f_sum8(ids, a):
        s = a[0:SMALL_ROWS]
        for d in range(1, 8):
            s = s + a[d * SMALL_ROWS:(d + 1) * SMALL_ROWS]
        return (s,)

    (vsum,) = _ew(f_sum8, (1,), [(got3, _full((8 * SMALL_ROWS, 128)))],
                  [((SMALL_ROWS, 128), F32, _full((SMALL_ROWS, 128)), None)], "sum_small")
    vflat = vsum.reshape(-1)
    gvec, off = {}, 0
    for n, size in SMALL:
        gvec[n] = vflat[off:off + size]
        off += size
    dmod_rows = got3.reshape(8, SMALL_ROWS * 128)[:, :6 * D_MODEL]
    dm16 = jnp.concatenate([dmod_rows, gvec["dmod_ctx"].reshape(1, -1), jnp.zeros((7, 6 * D_MODEL), F32)], axis=0)

    def f_colsum(ids, a):
        return (_colsum(a),)

    (g_b_ada,) = _ew(f_colsum, (1,), [(dm16, _full((16, 6 * D_MODEL)))],
                     [((1, 6 * D_MODEL), F32, _full((1, 6 * D_MODEL)), None)], "b_ada_grad")
    dm_sh = lax.dynamic_slice_in_dim(dm16, chip * 1536, 1536, axis=1)
    g_w_ada = _mm(s16, dm_sh, "tn", D_MODEL, 1536, 16, tm=512, tn=768, tk=16, name="w_ada_dw")
    dcond_part = _mm(dm_sh, w_ada[0], "nt", 16, D_MODEL, 1536, tm=16, tn=512, tk=1536, name="w_ada_dx")
    got4 = _allgather8(dcond_part[8:16], "gather_dcond", in_vmem=True)[0].reshape(4, 2, 8, D_MODEL)[:, 0, 0]

    def f_c_ctx(ids, parts, cc):
        s = _sigmoid(cc)
        d = parts[0:1] + parts[1:2] + parts[2:3] + parts[3:4]
        return (d * s * (1.0 + cc * (1.0 - s)),)

    (g_c_ctx,) = _ew(f_c_ctx, (1,), [(got4, _full((4, D_MODEL))), (c_ctx.reshape(1, D_MODEL), _full((1, D_MODEL)))],
                     [((1, D_MODEL), F32, _full((1, D_MODEL)), None)], "c_ctx_grad")

    from_sibling = _rs_pair([gbig[n] for n in first], "rs_pair")
    pair_sums = [_add_pair(gbig[n], from_sibling[w], place, "rs_pair_add_" + n) for w, n in enumerate(first)]
    lands = _rs_chips(pair_sums)
    late_sums, late_lands = _ici_wait("scatter", early["send"], early["recv"], early["sums"], early["land"], grad_x,
                                      "rs_chips_late_wait")
    half_sums = [_add_chips(a, b, place, "rs_chip_add_" + n)
                 for a, b, n in zip(pair_sums + late_sums, lands + late_lands, first + list(LATE))]
    gw = dict(zip(first + list(LATE), _rs_pair_back(half_sums)))
    gw["w_ada"] = g_w_ada

    moments = {"w_ada": (w_ada, m_w_ada, v_w_ada), "w_ukv": (w_ukv, m_w_ukv, v_w_ukv),
               "w_attn_out": (w_attn_out, m_w_attn_out, v_w_attn_out),
               "w_conv_out": (w_conv_out, m_w_conv_out, v_w_conv_out), "w_o": (w_o, m_w_o, v_w_o),
               "w_up": (w_up, m_w_up, v_w_up), "w_down": (w_down, m_w_down, v_w_down)}
    grads, deltas, new_m, new_v = {}, {}, {}, {}
    for n, (w_, m_, v_) in moments.items():
        d_, m2, v2 = _adamw(w_[0], gw[n], m_[0], v_[0], "adamw_" + n)
        grads[n], deltas[n], new_m[n], new_v[n] = gw[n][None], d_[None], m2[None], v2[None]
    for n, (w_, m_, v_) in {"w_in": (w_in_t, m_w_in_t, v_w_in_t), "w_uq": (w_uq_t, m_w_uq_t, v_w_uq_t)}.items():
        d_, m2, v2 = _adamw(w_, gw[n], m_, v_, "adamw_" + n)
        back = lambda a: jnp.transpose(a)[None]
        grads[n], deltas[n], new_m[n], new_v[n] = back(gw[n][:w_.shape[0]]), back(d_), back(m2), back(v2)

    conv_w_g = lax.dynamic_slice_in_dim(gvec["conv_w"].reshape(3, CONV_DIM), chip * 128, 128, axis=1)
    ffn_conv_w_g = lax.dynamic_slice_in_dim(gvec["ffn_conv_w"].reshape(3, 2 * D_FF), chip * 1408, 1408, axis=1)
    vec_params = (("c_ctx", c_ctx, m_c_ctx, v_c_ctx, g_c_ctx), ("b_ada", b_ada, m_b_ada, v_b_ada, g_b_ada),
                  ("norm1_g", norm1_g, m_norm1_g, v_norm1_g, gvec["norm1_g"]),
                  ("q_norm_g", q_norm_g, m_q_norm_g, v_q_norm_g, gvec["q_norm_g"]),
                  ("kv_norm_g", kv_norm_g, m_kv_norm_g, v_kv_norm_g, gvec["kv_norm_g"]),
                  ("conv_w", conv_w, m_conv_w, v_conv_w, conv_w_g), ("conv_b", conv_b, m_conv_b, v_conv_b, gvec["conv_b"]),
                  ("norm2_g", norm2_g, m_norm2_g, v_norm2_g, gvec["norm2_g"]),
                  ("ffn_conv_w", ffn_conv_w, m_ffn_conv_w, v_ffn_conv_w, ffn_conv_w_g),
                  ("ffn_conv_b", ffn_conv_b, m_ffn_conv_b, v_ffn_conv_b, gvec["ffn_conv_b"]),
                  ("final_g", final_g, m_final_g, v_final_g, gvec["final_g"]))
    total = sum(p[1].size for p in vec_params)
    rows_v = -(-total // 1024) * 8

    def packv(idx):
        flat_v = jnp.concatenate([p[idx].reshape(-1) for p in vec_params])
        return jnp.pad(flat_v, (0, rows_v * 128 - total)).reshape(rows_v, 128)

    vd, vm, vv = _adamw(packv(1), packv(4), packv(2), packv(3), "adamw_vectors")
    off = 0
    for p in vec_params:
        n, shape, size = p[0], p[1].shape, p[1].size
        grads[n] = p[4].reshape(shape)
        deltas[n] = vd.reshape(-1)[off:off + size].reshape(shape)
        new_m[n] = vm.reshape(-1)[off:off + size].reshape(shape)
        new_v[n] = vv.reshape(-1)[off:off + size].reshape(shape)
        off += size

    order = ("c_ctx", "w_ada", "b_ada", "norm1_g", "w_in", "q_norm_g", "kv_norm_g", "w_uq", "w_ukv", "conv_w", "conv_b",
             "w_attn_out", "w_conv_out", "w_o", "norm2_g", "w_up", "ffn_conv_w", "ffn_conv_b", "w_down", "final_g")
    return (loss, grad_x[None], *[grads[n] for n in order], *[deltas[n] for n in order],
            *[new_m[n] for n in order], *[new_v[n] for n in order])
```

```python
import functools

import jax
import jax.numpy as jnp
from jax import lax
from jax.experimental import pallas as pl
from jax.experimental.pallas import tpu as pltpu

F32, BF16 = jnp.float32, jnp.bfloat16
MESH = pl.DeviceIdType.MESH

D_MODEL = 1024
N_HEADS = 8
HEAD_PAD = 128
QK_DIM = 96
Q_RANK, KV_RANK = 384, 256
CONV_DIM = 512
D_FF = 2816
GRID_W = 64
ROPE_THETA = 10000.0
EPS = 1e-6
GA0, GC0, CX0, CB0, CC0, KV0, Q0, KR0, P_COLS = 0, 1024, 2048, 2560, 3072, 3584, 3840, 4224, 4352
ROW_TILE = 256
VMEM_LIMIT_BYTES = 48 * 1024 * 1024

ADAM_LR, ADAM_B1, ADAM_B2, ADAM_EPS, ADAM_WD, ADAM_STEP = 0.001, 0.9, 0.999, 1e-08, 0.01, 10

BIG = (("w_in", (1088, 1024)), ("w_uq", (192, 384)), ("w_ukv", (256, 256)), ("w_attn_out", (512, 256)),
       ("w_conv_out", (512, 256)), ("w_o", (256, 1024)), ("w_up", (1024, 1408)), ("w_down", (704, 1024)))

GATHER_LATE = ("w_attn_out", "w_conv_out", "w_o", "w_up", "w_down")

NN = (((1,), (0,)), ((), ()))
NT = (((1,), (1,)), ((), ()))
TN = (((0,), (0,)), ((), ()))


def _cp(sem):
    return pltpu.CompilerParams(dimension_semantics=sem, vmem_limit_bytes=VMEM_LIMIT_BYTES)


def _pick(n, prefs):
    for p in prefs:
        if n % p == 0:
            return p
    return n


def _mm(a, b, mode, M, N, K, *, tm, tn, tk, name, out_dtype=F32, a_spec=None, b_spec=None, o_spec=None,
        out_shape=None, transpose_out=False):
    assert M % tm == 0 and N % tn == 0 and K % tk == 0, (name, M, N, K, tm, tn, tk)
    nk = K // tk
    dims = {"nn": NN, "nt": NT, "tn": TN}[mode]
    if a_spec is None:
        a_spec = (pl.BlockSpec((tk, tm), lambda i, j, k: (k, i)) if mode == "tn"
                  else pl.BlockSpec((tm, tk), lambda i, j, k: (i, k)))
    if b_spec is None:
        b_spec = (pl.BlockSpec((tn, tk), lambda i, j, k: (j, k)) if mode == "nt"
                  else pl.BlockSpec((tk, tn), lambda i, j, k: (k, j)))
    if o_spec is None:
        o_spec = (pl.BlockSpec((tn, tm), lambda i, j, k: (j, i)) if transpose_out
                  else pl.BlockSpec((tm, tn), lambda i, j, k: (i, j)))
    if out_shape is None:
        out_shape = (N, M) if transpose_out else (M, N)

    def emit(o_ref, val):
        o_ref[...] = (val.T if transpose_out else val).astype(o_ref.dtype)

    def body(a_ref, b_ref, o_ref, *scratch):
        part = lax.dot_general(a_ref[...].astype(BF16), b_ref[...].astype(BF16), dims, preferred_element_type=F32)
        if nk == 1:
            emit(o_ref, part)
            return
        acc_ref, = scratch
        k = pl.program_id(2)

        @pl.when(k == 0)
        def _():
            acc_ref[...] = part

        @pl.when((k > 0) & (k < nk - 1))
        def _():
            acc_ref[...] += part

        @pl.when(k == nk - 1)
        def _():
            emit(o_ref, acc_ref[...] + part)

    return pl.pallas_call(
        body, grid=(M // tm, N // tn, nk), in_specs=[a_spec, b_spec], out_specs=o_spec,
        out_shape=jax.ShapeDtypeStruct(out_shape, out_dtype),
        scratch_shapes=[pltpu.VMEM((tm, tn), F32)] if nk > 1 else [],
        compiler_params=_cp(("parallel", "parallel", "arbitrary")), name=name)(a, b)


def _ew(fn, grid, ins, outs, name, scalars=None):
    n_in = len(ins)
    n_sc = 0 if scalars is None else 1

    def store(ref, val, acc, ids):
        if isinstance(val, (list, tuple)):
            for h, v in enumerate(val):
                ref[h] = v.astype(ref.dtype)
            return
        if acc is None:
            ref[...] = val.astype(ref.dtype)
            return

        @pl.when(ids[acc] == 0)
        def _():
            ref[...] = val.astype(ref.dtype)

        @pl.when(ids[acc] > 0)
        def _():
            ref[...] += val.astype(ref.dtype)

    def body(*refs):
        refs = refs[n_sc:]
        ids = tuple(pl.program_id(a) for a in range(len(grid)))
        vals = fn(ids, *[r[...] for r in refs[:n_in]])
        for ref, val, (_, _, _, acc) in zip(refs[n_in:], vals, outs):
            store(ref, val, acc, ids)

    acc_axes = {o[3] for o in outs if o[3] is not None}
    sem = tuple("arbitrary" if a in acc_axes else "parallel" for a in range(len(grid)))
    in_specs, out_specs = [s for _, s in ins], [o[2] for o in outs]
    out_shape = [jax.ShapeDtypeStruct(o[0], o[1]) for o in outs]
    args = [a for a, _ in ins]
    if scalars is None:
        return pl.pallas_call(body, grid=grid, in_specs=in_specs, out_specs=out_specs, out_shape=out_shape,
                              compiler_params=_cp(sem), name=name)(*args)
    spec = pltpu.PrefetchScalarGridSpec(num_scalar_prefetch=1, grid=grid, in_specs=in_specs, out_specs=out_specs)
    return pl.pallas_call(body, grid_spec=spec, out_shape=out_shape, compiler_params=_cp(sem), name=name)(scalars, *args)


def _rows(width, cblk=0, roff=0, tr=ROW_TILE):
    return pl.BlockSpec((tr, width), lambda i: (i + roff, cblk))


def _full(shape):
    nd = len(shape)
    return pl.BlockSpec(shape, lambda *_: (0,) * nd)


def _sigmoid(x):
    return 1.0 / (1.0 + jnp.exp(-x))


def _rms(x):
    return lax.rsqrt(jnp.mean(x * x, axis=-1, keepdims=True) + EPS)


def _rms_bwd(dn, xn, r):
    return r * (dn - xn * jnp.mean(dn * xn, axis=-1, keepdims=True))


def _colsum(x):
    return jnp.sum(x, axis=0, keepdims=True)


def _shift_prev(x):
    rows = lax.broadcasted_iota(jnp.int32, x.shape, 0)
    return jnp.where(rows == 0, 0.0, pltpu.roll(x, 1, 0))


def _shift_next(x):
    rows = lax.broadcasted_iota(jnp.int32, x.shape, 0)
    return jnp.where(rows == x.shape[0] - 1, 0.0, pltpu.roll(x, x.shape[0] - 1, 0))


def _conv(x, w, b):
    return b + _shift_prev(x) * w[0:1] + x * w[1:2] + _shift_next(x) * w[2:3]


def _conv_bwd_x(dy, w):
    return _shift_next(dy) * w[0:1] + dy * w[1:2] + _shift_prev(dy) * w[2:3]


def _conv_bwd_w(dy, x):
    return _colsum(dy * _shift_prev(x)), _colsum(dy * x), _colsum(dy * _shift_next(x))


def _rope(x, cos, sin_lo, sin_hi):
    return x * cos + pltpu.roll(x, HEAD_PAD - 8, 1) * sin_lo + pltpu.roll(x, 8, 1) * sin_hi


ATTN_SCALE = QK_DIM ** -0.5


def _head_keys(kv_ref, kr_ref, cos_ref, slo_ref, shi_ref, kc_ref, vp_ref):
    kv = kv_ref[...]
    lane = lax.broadcasted_iota(jnp.int32, kv.shape, 1)
    kc_ref[...] = jnp.where(lane < 64, kv, _rope(kr_ref[...], cos_ref[...], slo_ref[...], shi_ref[...])).astype(BF16)
    vp_ref[...] = jnp.where(lane >= 64, kv, 0.0).astype(BF16)


def _attn_specs(tq, TT, clamp):
    row = (lambda i: jnp.minimum(i, clamp)) if clamp is not None else (lambda i: i)
    q = pl.BlockSpec((tq, HEAD_PAD), lambda h, i: (i, h))
    lat = pl.BlockSpec((tq, HEAD_PAD), lambda h, i: (row(i), h))
    keys = pl.BlockSpec((TT, HEAD_PAD), lambda h, i: (0, h))
    kr = pl.BlockSpec((TT, HEAD_PAD), lambda h, i: (0, KR0 // HEAD_PAD))
    tab_q = pl.BlockSpec((tq, HEAD_PAD), lambda h, i: (i, 0))
    tab_k = pl.BlockSpec((TT, HEAD_PAD), lambda h, i: (0, 0))
    lse = pl.BlockSpec((None, tq, 1), lambda h, i: (h, row(i), 0))
    return q, lat, keys, kr, tab_q, tab_k, lse


def _attn_fwd(q_raw, kv, pp, tabs, T, TT):
    tq = ROW_TILE
    cos, slo, shi = tabs

    def body(q_ref, kv_ref, kr_ref, cq, lq, hq, ck, lk, hk, o_ref, l_ref, kc, vp):
        @pl.when(pl.program_id(1) == 0)
        def _():
            _head_keys(kv_ref, kr_ref, ck, lk, hk, kc, vp)

        q = _rope(q_ref[...], cq[...], lq[...], hq[...]).astype(BF16)
        s = lax.dot_general(q, kc[...], NT, preferred_element_type=F32) * ATTN_SCALE
        m = jnp.max(s, axis=-1, keepdims=True)
        p = jnp.exp(s - m)
        l = jnp.sum(p, axis=-1, keepdims=True)
        o = lax.dot_general(p.astype(BF16), vp[...], NN, preferred_element_type=F32)
        o_ref[...] = o / l
        l_ref[...] = m + jnp.log(l)

    qs, _, keys, kr, tab_q, tab_k, lse = _attn_specs(tq, TT, None)
    return pl.pallas_call(
        body, grid=(N_HEADS, T // tq), in_specs=[qs, keys, kr, tab_q, tab_q, tab_q, tab_k, tab_k, tab_k],
        out_specs=[qs, lse],
        out_shape=[jax.ShapeDtypeStruct((T, N_HEADS * HEAD_PAD), F32), jax.ShapeDtypeStruct((N_HEADS, T, 1), F32)],
        scratch_shapes=[pltpu.VMEM((TT, HEAD_PAD), BF16), pltpu.VMEM((TT, HEAD_PAD), BF16)],
        compiler_params=_cp(("parallel", "arbitrary")), name="attn_fwd")(q_raw, kv, pp, cos, slo, shi, cos, slo, shi)


def _attn_bwd(q_raw, kv, pp, o, do, lse, tabs, tabs_inv, T, TT):
    tq = ROW_TILE
    nq = T // tq
    cos, slo, shi = tabs
    cos_i, slo_i, shi_i = tabs_inv

    def body(q_ref, kv_ref, kr_ref, cq, lq, hq, ck, lk, hk, iq, ilq, ihq, ik, ilk, ihk, o_ref, do_ref, l_ref,
             dq_ref, dkv_ref, dkr_ref, kc, vp, dk, dv):
        h, i = pl.program_id(0), pl.program_id(1)

        @pl.when(i == 0)
        def _():
            _head_keys(kv_ref, kr_ref, ck, lk, hk, kc, vp)
            dk[...] = jnp.zeros_like(dk)
            dv[...] = jnp.zeros_like(dv)

        @pl.when(i < nq)
        def _():
            q = _rope(q_ref[...], cq[...], lq[...], hq[...]).astype(BF16)
            k, v, d_o = kc[...], vp[...], do_ref[...]
            s = lax.dot_general(q, k, NT, preferred_element_type=F32) * ATTN_SCALE
            p = jnp.exp(s - l_ref[...])
            dob = d_o.astype(BF16)
            dp = lax.dot_general(dob, v, NT, preferred_element_type=F32)
            dd = jnp.sum(d_o * o_ref[...], axis=-1, keepdims=True)
            ds = (p * (dp - dd) * ATTN_SCALE).astype(BF16)
            dq = lax.dot_general(ds, k, NN, preferred_element_type=F32)
            dq_ref[...] = _rope(dq, iq[...], ilq[...], ihq[...]).astype(dq_ref.dtype)
            dk[...] += lax.dot_general(ds, q, TN, preferred_element_type=F32)
            dv[...] += lax.dot_general(p.astype(BF16), dob, TN, preferred_element_type=F32)

        @pl.when(i == nq)
        def _():
            dq_ref[...] = jnp.zeros_like(dq_ref)
            dkh = dk[...]
            lane = lax.broadcasted_iota(jnp.int32, dkh.shape, 1)
            dkv_ref[...] = jnp.where(lane < 64, dkh, dv[...]).astype(dkv_ref.dtype)
            rot = _rope(jnp.where((lane >= 64) & (lane < 96), dkh, 0.0), ik[...], ilk[...], ihk[...])

            @pl.when(h == 0)
            def _():
                dkr_ref[...] = rot

            @pl.when(h > 0)
            def _():
                dkr_ref[...] += rot

    qs, lat, keys, kr, tab_q, tab_k, lse_spec = _attn_specs(tq, TT, nq - 1)
    wide = jax.ShapeDtypeStruct((TT, N_HEADS * HEAD_PAD), BF16)
    return pl.pallas_call(
        body, grid=(N_HEADS, TT // tq),
        in_specs=[qs, keys, kr] + [tab_q] * 3 + [tab_k] * 3 + [tab_q] * 3 + [tab_k] * 3 + [lat, lat, lse_spec],
        out_specs=[qs, keys, pl.BlockSpec((TT, HEAD_PAD), lambda h, i: (0, 0))],
        out_shape=[wide, wide, jax.ShapeDtypeStruct((TT, HEAD_PAD), F32)],
        scratch_shapes=[pltpu.VMEM((TT, HEAD_PAD), BF16), pltpu.VMEM((TT, HEAD_PAD), BF16),
                        pltpu.VMEM((TT, HEAD_PAD), F32), pltpu.VMEM((TT, HEAD_PAD), F32)],
        compiler_params=_cp(("arbitrary", "arbitrary")), name="attn_bwd",
    )(q_raw, kv, pp, cos, slo, shi, cos, slo, shi, cos_i, slo_i, shi_i, cos_i, slo_i, shi_i, o, do, lse)


def _allgather8(x, name, in_vmem):
    m_per, n = x.shape

    def body(x_ref, out_ref, token, send_sems, recv_sems, local_sem):
        token[...] = jnp.zeros_like(token)
        mx, my, mc = lax.axis_index("x"), lax.axis_index("y"), lax.axis_index("c")
        me, sibling = (mx, my, mc), (mx, my, 1 - mc)
        chips = [(1 - mx, my), (mx, 1 - my), (1 - mx, 1 - my)]

        def rows(px, py, pc):
            return out_ref.at[pl.ds((4 * px + 2 * py + pc) * m_per, m_per), :]

        def copy(k, block, to, src=None):
            return pltpu.make_async_remote_copy(
                src_ref=rows(*block) if src is None else src, dst_ref=rows(*block),
                send_sem=send_sems.at[k], recv_sem=recv_sems.at[k], device_id=to, device_id_type=MESH)

        mine = pltpu.make_async_copy(x_ref, rows(*me), local_sem)
        mine.start()
        first = [copy(0, me, sibling, src=x_ref)]
        first += [copy(1 + j, me, (*chip, mc), src=x_ref) for j, chip in enumerate(chips)]
        for cp in first:
            cp.start()
        passed = [copy(4 + j, (*chip, mc), sibling) for j, chip in enumerate(chips)]
        for j, chip in enumerate(chips):
            copy(1 + j, (*chip, mc), me).wait_recv()
            passed[j].start()
        copy(0, sibling, me).wait_recv()
        for j, chip in enumerate(chips):
            copy(4 + j, (*chip, 1 - mc), me).wait_recv()
        for cp in first + passed:
            cp.wait_send()
        mine.wait()

    space = pltpu.VMEM if in_vmem else pl.ANY
    return pl.pallas_call(
        body, out_shape=[jax.ShapeDtypeStruct((8 * m_per, n), x.dtype), jax.ShapeDtypeStruct((8, 128), F32)],
        in_specs=[pl.BlockSpec(memory_space=space)],
        out_specs=[pl.BlockSpec(memory_space=space), pl.BlockSpec(memory_space=pltpu.VMEM)],
        scratch_shapes=[pltpu.SemaphoreType.DMA((7,)), pltpu.SemaphoreType.DMA((7,)), pltpu.SemaphoreType.DMA],
        name=name)(x)


def _hbm_specs(n):
    return [pl.BlockSpec(memory_space=pl.ANY)] * n


def _gather_weights(shards):
    n = len(shards)
    halves = [s.shape[0] // 2 for s in shards]

    def body(*refs):
        ins, outs = refs[:n], refs[n:2 * n]
        token, send_sems, recv_sems = refs[2 * n:]
        token[...] = jnp.zeros_like(token)
        mx, my, mc = lax.axis_index("x"), lax.axis_index("y"), lax.axis_index("c")
        j_me = 2 * mx + my
        chips = [(1 - mx, my), (mx, 1 - my), (1 - mx, 1 - my)]

        def half(w, chip_idx, hc):
            return outs[w].at[chip_idx, pl.ds(hc * halves[w], halves[w]), :]

        def copy(w, k, src, dst, to):
            return pltpu.make_async_remote_copy(src_ref=src, dst_ref=dst, send_sem=send_sems.at[w, k],
                                                recv_sem=recv_sems.at[w, k], device_id=to, device_id_type=MESH)

        sends = []
        for w in range(n):
            cp = copy(w, 6, ins[w], outs[w].at[j_me], (mx, my, 1 - mc))
            cp.start()
            sends.append(cp)
        for k, (px, py) in enumerate(chips):
            for w in range(n):
                cp = copy(w, k, ins[w].at[pl.ds(mc * halves[w], halves[w]), :], half(w, j_me, mc), (px, py, mc))
                cp.start()
                sends.append(cp)
        for k, (px, py) in enumerate(chips):
            for w in range(n):
                got = half(w, 2 * px + py, mc)
                copy(w, k, got, got, (px, py, mc)).wait_recv()
                cp = copy(w, 3 + k, got, got, (mx, my, 1 - mc))
                cp.start()
                sends.append(cp)
        for k, (px, py) in enumerate(chips):
            for w in range(n):
                got = half(w, 2 * px + py, 1 - mc)
                copy(w, 3 + k, got, got, (mx, my, 1 - mc)).wait_recv()
        for w in range(n):
            own = outs[w].at[j_me]
            copy(w, 6, own, own, (mx, my, 1 - mc)).wait_recv()
        for cp in sends:
            cp.wait_send()

    res = pl.pallas_call(
        body, out_shape=[jax.ShapeDtypeStruct((4,) + s.shape, s.dtype) for s in shards]
        + [jax.ShapeDtypeStruct((8, 128), F32)],
        in_specs=_hbm_specs(n), out_specs=_hbm_specs(n) + [pl.BlockSpec(memory_space=pltpu.VMEM)],
        scratch_shapes=[pltpu.SemaphoreType.DMA((n, 7)), pltpu.SemaphoreType.DMA((n, 7))],
        name="gather_weights")(*shards)
    return list(res[:n]), res[n]


def _rs_pair(gs, name):
    n = len(gs)
    halves = [g.shape[1] // 2 for g in gs]

    def body(*refs):
        ins, lands = refs[:n], refs[n:2 * n]
        send_sems, recv_sems = refs[2 * n:]
        mx, my, mc = lax.axis_index("x"), lax.axis_index("y"), lax.axis_index("c")
        copies = []
        for w in range(n):
            h = halves[w]
            cp = pltpu.make_async_remote_copy(
                src_ref=ins[w].at[:, pl.ds((1 - mc) * h, h), :], dst_ref=lands[w], send_sem=send_sems.at[w],
                recv_sem=recv_sems.at[w], device_id=(mx, my, 1 - mc), device_id_type=MESH)
            cp.start()
            copies.append(cp)
        for cp in copies:
            cp.wait()

    return pl.pallas_call(
        body, out_shape=[jax.ShapeDtypeStruct((4, h, g.shape[2]), g.dtype) for g, h in zip(gs, halves)],
        in_specs=_hbm_specs(n), out_specs=_hbm_specs(n),
        scratch_shapes=[pltpu.SemaphoreType.DMA((n,)), pltpu.SemaphoreType.DMA((n,))], name=name)(*gs)


def _rs_chips(parts):
    n = len(parts)

    def body(*refs):
        ins, lands = refs[:n], refs[n:2 * n]
        send_sems, recv_sems = refs[2 * n:]
        mx, my, mc = lax.axis_index("x"), lax.axis_index("y"), lax.axis_index("c")
        copies = []
        for k, (px, py) in enumerate([(1 - mx, my), (mx, 1 - my), (1 - mx, 1 - my)]):
            for w in range(n):
                cp = pltpu.make_async_remote_copy(
                    src_ref=ins[w].at[2 * px + py], dst_ref=lands[w].at[k], send_sem=send_sems.at[w, k],
                    recv_sem=recv_sems.at[w, k], device_id=(px, py, mc), device_id_type=MESH)
                cp.start()
                copies.append(cp)
        for cp in copies:
            cp.wait()

    return list(pl.pallas_call(
        body, out_shape=[jax.ShapeDtypeStruct((3,) + p.shape[1:], p.dtype) for p in parts],
        in_specs=_hbm_specs(n), out_specs=_hbm_specs(n),
        scratch_shapes=[pltpu.SemaphoreType.DMA((n, 3)), pltpu.SemaphoreType.DMA((n, 3))], name="rs_chips")(*parts))


def _rs_pair_back(gs):
    n = len(gs)

    def body(*refs):
        outs = refs[n:2 * n]
        send_sems, recv_sems = refs[2 * n:]
        mx, my, mc = lax.axis_index("x"), lax.axis_index("y"), lax.axis_index("c")
        copies = []
        for w in range(n):
            h = gs[w].shape[0] // 2
            mine = outs[w].at[pl.ds(mc * h, h), :]
            cp = pltpu.make_async_remote_copy(src_ref=mine, dst_ref=mine, send_sem=send_sems.at[w],
                                              recv_sem=recv_sems.at[w], device_id=(mx, my, 1 - mc), device_id_type=MESH)
            cp.start()
            copies.append(cp)
        for cp in copies:
            cp.wait()

    return pl.pallas_call(
        body, out_shape=[jax.ShapeDtypeStruct(g.shape, g.dtype) for g in gs],
        in_specs=_hbm_specs(n), out_specs=_hbm_specs(n), input_output_aliases={w: w for w in range(n)},
        scratch_shapes=[pltpu.SemaphoreType.DMA((n,)), pltpu.SemaphoreType.DMA((n,))], name="rs_pair_back")(*gs)


_HBM = pl.BlockSpec(memory_space=pltpu.HBM)
_SEM = pl.BlockSpec(memory_space=pltpu.SEMAPHORE)
_EFFECT = pltpu.SideEffectType.DATAFLOW_SIDE_EFFECTING


def _ici_copies(kind, srcs, lands, send_sems, recv_sems):
    n = len(srcs)
    mx, my, mc = lax.axis_index("x"), lax.axis_index("y"), lax.axis_index("c")
    j_me = 2 * mx + my
    copies = []
    for k, (px, py) in enumerate([(1 - mx, my), (mx, 1 - my), (1 - mx, 1 - my)]):
        for w in range(n):
            if kind == "gather":
                h = srcs[w].shape[0] // 2
                src, dst = srcs[w].at[pl.ds(mc * h, h), :], lands[w].at[j_me, pl.ds(mc * h, h), :]
            else:
                src, dst = srcs[w].at[2 * px + py], lands[w].at[k]
            copies.append(pltpu.make_async_remote_copy(
                src_ref=src, dst_ref=dst, send_sem=send_sems.at[3 * w + k], recv_sem=recv_sems.at[3 * w + k],
                device_id=(px, py, mc), device_id_type=MESH))
    return copies


def _ici_start(kind, srcs, land_shapes, carry, name):
    n = len(srcs)

    def body(*refs):
        ins, lands = refs[:n], refs[n:2 * n]
        send_sems, recv_sems = refs[2 * n + 1], refs[2 * n + 2]
        for cp in _ici_copies(kind, ins, lands, send_sems, recv_sems):
            cp.start()

    hbm = lambda a: pltpu.with_memory_space_constraint(a, pltpu.HBM)
    lands = [lax.empty(s, srcs[0].dtype) for s in land_shapes]
    args = [hbm(a) for a in list(srcs) + lands + [carry]]
    out_shape = ([pltpu.SemaphoreType.DMA((3 * n,)), pltpu.SemaphoreType.DMA((3 * n,))]
                 + [pltpu.HBM(a.shape, a.dtype) for a in args])
    res = pl.pallas_call(
        body, name=name, out_shape=out_shape, in_specs=[_HBM] * len(args), out_specs=[_SEM, _SEM] + [_HBM] * len(args),
        input_output_aliases={i: 2 + i for i in range(len(args))},
        compiler_params=pltpu.CompilerParams(has_side_effects=_EFFECT))(*args)
    return res[0], res[1], list(res[2:2 + n]), list(res[2 + n:2 + 2 * n]), res[2 + 2 * n]


def _ici_wait(kind, send_sems, recv_sems, srcs, lands, after, name):
    n = len(srcs)

    def body(*refs):
        ins, zones = refs[:n], refs[n:2 * n]
        for cp in _ici_copies(kind, ins, zones, refs[2 * n], refs[2 * n + 1]):
            cp.wait_send()
            cp.wait_recv()

    args = list(srcs) + list(lands)
    res = pl.pallas_call(
        body, name=name, out_shape=[pltpu.HBM(a.shape, a.dtype) for a in args],
        in_specs=[_HBM] * len(args) + [_SEM, _SEM, pl.BlockSpec(memory_space=pl.ANY)], out_specs=[_HBM] * len(args),
        input_output_aliases={i: i for i in range(len(args))},
        compiler_params=pltpu.CompilerParams(has_side_effects=_EFFECT))(*args, send_sems, recv_sems, after)
    return list(res[:n]), list(res[n:])


def _gather_finish(shards, lands):
    n = len(shards)

    def body(*refs):
        own, outs = refs[:n], refs[2 * n:3 * n]
        send_sems, recv_sems = refs[3 * n:]
        mx, my, mc = lax.axis_index("x"), lax.axis_index("y"), lax.axis_index("c")
        j_me = 2 * mx + my
        sibling = (mx, my, 1 - mc)
        copies = []

        def push(w, k, src, dst):
            cp = pltpu.make_async_remote_copy(src_ref=src, dst_ref=dst, send_sem=send_sems.at[w, k],
                                              recv_sem=recv_sems.at[w, k], device_id=sibling, device_id_type=MESH)
            cp.start()
            copies.append(cp)

        for w in range(n):
            h = shards[w].shape[0] // 2
            push(w, 3, own[w], outs[w].at[j_me])
            for k, (px, py) in enumerate([(1 - mx, my), (mx, 1 - my), (1 - mx, 1 - my)]):
                got = outs[w].at[2 * px + py, pl.ds(mc * h, h), :]
                push(w, k, got, got)
        for cp in copies:
            cp.wait()

    return pl.pallas_call(
        body, out_shape=[jax.ShapeDtypeStruct(l.shape, l.dtype) for l in lands],
        in_specs=_hbm_specs(2 * n), out_specs=_hbm_specs(n), input_output_aliases={n + w: w for w in range(n)},
        scratch_shapes=[pltpu.SemaphoreType.DMA((n, 4)), pltpu.SemaphoreType.DMA((n, 4))], name="gather_finish",
    )(*shards, *lands)


def _tile_rows(h, c, itemsize, mult):
    best = h
    for t in range(mult, h + 1, mult):
        if h % t == 0 and t * c * itemsize <= (1 << 21):
            best = t
    return best


def _add_pair(g, land, place, name):
    _, h, c = land.shape
    t = _tile_rows(h, c, 2, 16)
    nb = h // t
    return _ew(lambda ids, u, v: (u.astype(F32) + v.astype(F32),), (4, nb),
               [(g, pl.BlockSpec((None, t, c), lambda j, i, s: (j, s[1] * nb + i, 0))),
                (land, pl.BlockSpec((None, t, c), lambda j, i, s: (j, i, 0)))],
               [(land.shape, BF16, pl.BlockSpec((None, t, c), lambda j, i, s: (j, i, 0)), None)], name, scalars=place)[0]


def _add_chips(own, land, place, name):
    _, h, c = land.shape
    t = _tile_rows(h, c, 4, 16)
    nb = h // t

    def fn(ids, a, b):
        return (((a.astype(F32) + b[0].astype(F32)) + b[1].astype(F32)) + b[2].astype(F32),)

    return _ew(fn, (nb,), [(own, pl.BlockSpec((None, t, c), lambda i, s: (s[0], i, 0))),
                           (land, pl.BlockSpec((3, t, c), lambda i, s: (0, i, 0)))],
               [((2 * h, c), F32, pl.BlockSpec((t, c), lambda i, s: (s[1] * nb + i, 0)), None)], name, scalars=place)[0]


W_IN_SEGMENTS = ((0, 256, KV0), (256, 288, KR0 + 64), (288, 672, Q0), (672, 1184, CX0), (1184, 1696, CB0),
                 (1696, 2208, CC0), (2208, 3232, GA0), (3232, 4256, GC0))
W_IN_SHARD = 1064


W_IN_SHARD_PAD = 1088


def _w_in_t_p_from_shards(s):
    pieces = []
    for o0, o1, p0 in sorted(W_IN_SEGMENTS, key=lambda t: t[2]):
        if p0 == KR0 + 64:
            pieces.append(jnp.zeros((64, s.shape[2]), s.dtype))
        for j in range(4):
            lo, hi = max(o0, j * W_IN_SHARD), min(o1, (j + 1) * W_IN_SHARD)
            if lo < hi:
                pieces.append(s[j, lo - j * W_IN_SHARD:hi - j * W_IN_SHARD])
    pieces.append(jnp.zeros((32, s.shape[2]), s.dtype))
    return jnp.concatenate(pieces, axis=0)


def _w_in_t_shards_from_p(g):
    shards = []
    for j in range(4):
        pieces = []
        for o0, o1, p0 in W_IN_SEGMENTS:
            lo, hi = max(o0, j * W_IN_SHARD), min(o1, (j + 1) * W_IN_SHARD)
            if lo < hi:
                pieces.append(g[p0 + lo - o0:p0 + hi - o0])
        pieces.append(jnp.zeros((W_IN_SHARD_PAD - W_IN_SHARD, g.shape[1]), g.dtype))
        shards.append(jnp.concatenate(pieces, axis=0))
    return jnp.stack(shards, axis=0)


def _cols_from_shards(s):
    return jnp.transpose(s, (1, 0, 2)).reshape(s.shape[1], -1)


def _rope_tables(T, TT, inverse):
    rows = T // GRID_W
    row = jnp.repeat(jnp.arange(rows), GRID_W).astype(F32)
    col = jnp.tile(jnp.arange(GRID_W), rows).astype(F32)
    inv = ROPE_THETA ** (-jnp.arange(0, 16, 2, dtype=F32) / 16)
    ang = jnp.concatenate([row[:, None] * inv, col[:, None] * inv], axis=-1)
    cos, sin = jnp.cos(ang), jnp.sin(ang)
    lane = jnp.arange(32)
    src = (lane // 16) * 8 + lane % 8
    lo = ((lane % 16) // 8 == 0).astype(F32)
    sgn = -1.0 if inverse else 1.0
    cos32 = cos[:, src]
    sin_lo32 = -sgn * sin[:, src] * lo
    sin_hi32 = sgn * sin[:, src] * (1.0 - lo)

    def widen(t32, fill):
        t = jnp.concatenate([jnp.full((T, 64), fill, F32), t32, jnp.full((T, 32), fill, F32)], axis=1)
        return jnp.concatenate([t, jnp.full((TT - T, HEAD_PAD), fill, F32)], axis=0)

    return widen(cos32, 1.0), widen(sin_lo32, 0.0), widen(sin_hi32, 0.0)


def _local_step(xx, tgt, mod_lat, mod_ctx, W, late_weights, early_grads):
    TT = xx.shape[0]
    T = tgt.shape[0]
    n_lat, n_all = T // ROW_TILE, TT // ROW_TILE
    sh1, sc1, g1, sh2, sc2, g2 = [mod_lat[:, k * D_MODEL:(k + 1) * D_MODEL] for k in range(6)]
    csh1, csc1 = mod_ctx[:, :D_MODEL], mod_ctx[:, D_MODEL:2 * D_MODEL]
    vec = lambda n: _full((1, n))
    row_out = lambda n, dt, rows=T: ((rows, n), dt, _rows(n), None)
    acc_out = lambda n: ((1, n), F32, _full((1, n)), 0)

    def f_norm1(ids, x, g, a_sh, a_sc, b_sh, b_sc):
        ctx = ids[0] >= n_lat
        sh, sc = jnp.where(ctx, b_sh, a_sh), jnp.where(ctx, b_sc, a_sc)
        return ((x * _rms(x) * g) * (1.0 + sc) + sh,)

    (hh,) = _ew(f_norm1, (n_all,), [(xx, _rows(D_MODEL)), (W["norm1_g"], vec(D_MODEL)), (sh1, vec(D_MODEL)),
                                   (sc1, vec(D_MODEL)), (csh1, vec(D_MODEL)), (csc1, vec(D_MODEL))],
                [row_out(D_MODEL, BF16, TT)], "norm1_fwd")
    tm_all = _pick(TT, (768, 256))
    pp = _mm(hh, W["w_in_t"], "nt", TT, P_COLS, D_MODEL, tm=tm_all, tn=2176, tk=D_MODEL, name="w_in_fwd")

    def f_lowrank(ids, ckv, cq, gkv, gq):
        return ckv * _rms(ckv) * gkv, cq * _rms(cq) * gq

    nkv, nq = _ew(f_lowrank, (n_all,), [(pp, _rows(KV_RANK, KV0 // KV_RANK)), (pp, _rows(Q_RANK, Q0 // Q_RANK)),
                                       (W["kv_norm_g"], vec(KV_RANK)), (W["q_norm_g"], vec(Q_RANK))],
                  [row_out(KV_RANK, BF16, TT), row_out(Q_RANK, BF16, TT)], "lowrank_norm_fwd")
    kv = _mm(nkv, W["w_ukv"], "nn", TT, 1024, KV_RANK, tm=tm_all, tn=256, tk=KV_RANK, name="w_ukv_fwd",
             b_spec=pl.BlockSpec((None, KV_RANK, 256), lambda i, j, k: (j, k, 0)))
    q_raw = _mm(nq, W["w_uq_t"], "nt", TT, 1024, Q_RANK, tm=tm_all, tn=1024, tk=Q_RANK, name="w_uq_fwd")

    tabs = _rope_tables(T, TT, inverse=False)
    tabs_inv = _rope_tables(T, TT, inverse=True)
    o_pad, lse = _attn_fwd(q_raw, kv, pp, tabs, T, TT)
    W = dict(W, **late_weights(o_pad))
    tm_lat = _pick(T, (1024, 512, 256))
    ya = _mm(o_pad, W["w_attn_out"], "nn", T, D_MODEL, 1024, tm=tm_lat, tn=D_MODEL, tk=1024, name="w_attn_out_fwd")

    tc = 256
    colT = lambda blk0: pl.BlockSpec((T, tc), lambda j: (0, blk0 + j))

    def f_conv(ids, xin, cb, cc, w, b):
        return (cb * _conv(cc * xin, w, b),)

    (e,) = _ew(f_conv, (CONV_DIM // tc,),
               [(pp, colT(CX0 // tc)), (pp, colT(CB0 // tc)), (pp, colT(CC0 // tc)),
                (W["conv_w"], pl.BlockSpec((3, tc), lambda j: (0, j))), (W["conv_b"], pl.BlockSpec((1, tc), lambda j: (0, j)))],
               [((T, CONV_DIM), BF16, colT(0), None)], "conv_fwd")
    yc = _mm(e, W["w_conv_out"], "nn", T, D_MODEL, CONV_DIM, tm=tm_lat, tn=256, tk=CONV_DIM, name="w_conv_out_fwd",
             b_spec=pl.BlockSpec((None, CONV_DIM, 256), lambda i, j, k: (j, k, 0)))

    def f_merge(ids, ga, gc, a, c):
        return (_sigmoid(ga) * a + _sigmoid(gc) * c,)

    (mrg,) = _ew(f_merge, (n_lat,), [(pp, _rows(D_MODEL, 0)), (pp, _rows(D_MODEL, 1)), (ya, _rows(D_MODEL)),
                                    (yc, _rows(D_MODEL))], [row_out(D_MODEL, BF16)], "merge_fwd")
    mo = _mm(mrg, W["w_o"], "nn", T, D_MODEL, D_MODEL, tm=tm_lat, tn=D_MODEL, tk=D_MODEL, name="w_o_fwd")

    def f_norm2(ids, x, m, gate, g, sh, sc):
        x1 = x + gate * m
        return x1, (x1 * _rms(x1) * g) * (1.0 + sc) + sh

    x1, h2 = _ew(f_norm2, (n_lat,), [(xx, _rows(D_MODEL)), (mo, _rows(D_MODEL)), (g1, vec(D_MODEL)),
                                    (W["norm2_g"], vec(D_MODEL)), (sh2, vec(D_MODEL)), (sc2, vec(D_MODEL))],
                 [row_out(D_MODEL, F32), row_out(D_MODEL, BF16)], "norm2_fwd")
    up = _mm(h2, W["w_up"], "nn", T, 2 * D_FF, D_MODEL, tm=tm_lat, tn=1408, tk=D_MODEL, name="w_up_fwd",
             b_spec=pl.BlockSpec((None, D_MODEL, 1408), lambda i, j, k: (j, k, 0)))

    n_ff = D_FF // tc
    ffw = lambda off, n=3: pl.BlockSpec((n, tc), lambda j: (0, j + off))

    def f_ffn(ids, ug, uv, wg, wv, bg, bv):
        gate, val = _conv(ug, wg, bg), _conv(uv, wv, bv)
        return (gate * _sigmoid(gate) * val,)

    (act,) = _ew(f_ffn, (n_ff,), [(up, colT(0)), (up, colT(n_ff)), (W["ffn_conv_w"], ffw(0)), (W["ffn_conv_w"], ffw(n_ff)),
                                 (W["ffn_conv_b"], ffw(0, 1)), (W["ffn_conv_b"], ffw(n_ff, 1))],
                 [((T, D_FF), BF16, colT(0), None)], "ffn_act_fwd")
    f = _mm(act, W["w_down"], "nn", T, D_MODEL, D_FF, tm=tm_lat, tn=D_MODEL, tk=D_FF, name="w_down_fwd")

    def f_head(ids, x1_, f_, gate, gf, t):
        x2 = x1_ + gate * f_
        r = _rms(x2)
        xn = x2 * r
        err = xn * gf - t
        loss = 0.5 * jnp.sum(jnp.mean(err * err, axis=-1, keepdims=True))
        dy = err * (1.0 / D_MODEL)
        dx2 = _rms_bwd(dy * gf, xn, r)
        return dx2, dx2 * gate, _colsum(dy * xn), _colsum(dx2 * f_), jnp.full((1, 128), loss, F32)

    dx2, df, dg_f, dg2, loss = _ew(
        f_head, (n_lat,), [(x1, _rows(D_MODEL)), (f, _rows(D_MODEL)), (g2, vec(D_MODEL)), (W["final_g"], vec(D_MODEL)),
                           (tgt, _rows(D_MODEL))],
        [row_out(D_MODEL, F32), row_out(D_MODEL, BF16), acc_out(D_MODEL), acc_out(D_MODEL), acc_out(128)], "loss_head")

    d_w_down = _mm(act, df, "tn", D_FF, D_MODEL, T, tm=1408, tn=D_MODEL, tk=T, name="w_down_dw",
                   out_dtype=BF16).reshape(4, D_FF // 4, D_MODEL)
    da = _mm(df, W["w_down"], "nt", T, D_FF, D_MODEL, tm=tm_lat, tn=1408, tk=D_MODEL, name="w_down_dx")

    tcb = 128
    n_fb = D_FF // tcb
    colb = lambda blk0: pl.BlockSpec((T, tcb), lambda j: (0, blk0 + j))
    ffwb = lambda off, n=3: pl.BlockSpec((n, tcb), lambda j: (0, j + off))
    cvec = ((1, D_FF), F32, pl.BlockSpec((1, tcb), lambda j: (0, j)), None)

    def f_ffn_bwd(ids, ug, uv, d_act, wg, wv, bg, bv):
        gate, val = _conv(ug, wg, bg), _conv(uv, wv, bv)
        s = _sigmoid(gate)
        d_gate = d_act * val * s * (1.0 + gate * (1.0 - s))
        d_val = d_act * gate * s
        wg0, wg1, wg2 = _conv_bwd_w(d_gate, ug)
        wv0, wv1, wv2 = _conv_bwd_w(d_val, uv)
        d_up = [_conv_bwd_x(d_gate, wg), _conv_bwd_x(d_val, wv)]
        return d_up, _colsum(d_gate), _colsum(d_val), wg0, wg1, wg2, wv0, wv1, wv2

    ffn_b = _ew(f_ffn_bwd, (n_fb,),
                [(up, colb(0)), (up, colb(n_fb)), (da, colb(0)), (W["ffn_conv_w"], ffwb(0)), (W["ffn_conv_w"], ffwb(n_fb)),
                 (W["ffn_conv_b"], ffwb(0, 1)), (W["ffn_conv_b"], ffwb(n_fb, 1))],
                [((2, T, D_FF), BF16, pl.BlockSpec((2, T, tcb), lambda j: (0, 0, j)), None)] + [cvec] * 8, "ffn_act_bwd")
    d_up3 = ffn_b[0]
    d_ffn_conv_b = jnp.concatenate([ffn_b[1], ffn_b[2]], axis=1)
    d_ffn_conv_w = jnp.concatenate([jnp.concatenate(ffn_b[3:6], axis=0), jnp.concatenate(ffn_b[6:9], axis=0)], axis=1)

    tk_t = T
    d_w_up = _mm(h2, d_up3, "tn", D_MODEL, 2 * D_FF, T, tm=D_MODEL, tn=1408, tk=tk_t, name="w_up_dw", out_dtype=BF16,
                 b_spec=pl.BlockSpec((None, tk_t, 1408), lambda i, j, k: (j // 2, k, j % 2)),
                 o_spec=pl.BlockSpec((None, D_MODEL, 1408), lambda i, j, k: (j, i, 0)), out_shape=(4, D_MODEL, 1408))
    dh2 = _mm(d_up3, W["w_up"], "nt", T, D_MODEL, 2 * D_FF, tm=tm_lat, tn=D_MODEL, tk=1408, name="w_up_dx",
              a_spec=pl.BlockSpec((None, tm_lat, 1408), lambda i, j, k: (k // 2, i, k % 2)),
              b_spec=pl.BlockSpec((None, D_MODEL, 1408), lambda i, j, k: (k, j, 0)))

    def f_norm2_bwd(ids, dx2_, dh, x1_, m, g, sc, gate):
        r = _rms(x1_)
        xn = x1_ * r
        dx1 = dx2_ + _rms_bwd(dh * g * (1.0 + sc), xn, r)
        return dx1, dx1 * gate, _colsum(dh), _colsum(dh * xn * g), _colsum(dh * xn * (1.0 + sc)), _colsum(dx1 * m)

    dx1, dmo, dsh2, dsc2, dg_n2, dg1 = _ew(
        f_norm2_bwd, (n_lat,), [(dx2, _rows(D_MODEL)), (dh2, _rows(D_MODEL)), (x1, _rows(D_MODEL)), (mo, _rows(D_MODEL)),
                                (W["norm2_g"], vec(D_MODEL)), (sc2, vec(D_MODEL)), (g1, vec(D_MODEL))],
        [row_out(D_MODEL, F32), row_out(D_MODEL, BF16)] + [acc_out(D_MODEL)] * 4, "norm2_bwd")
    d_w_o = _mm(mrg, dmo, "tn", D_MODEL, D_MODEL, T, tm=D_MODEL, tn=D_MODEL, tk=tk_t, name="w_o_dw",
                out_dtype=BF16).reshape(4, D_MODEL // 4, D_MODEL)
    dmrg = _mm(dmo, W["w_o"], "nt", T, D_MODEL, D_MODEL, tm=tm_lat, tn=D_MODEL, tk=D_MODEL, name="w_o_dx")
    dmrg = early_grads("late", {"w_o": d_w_o, "w_up": d_w_up, "w_down": d_w_down}, dmrg)

    def f_merge_bwd(ids, dm, ga, gc, a, c):
        sa, sc_ = _sigmoid(ga), _sigmoid(gc)
        return dm * sa, dm * sc_, dm * a * sa * (1.0 - sa), dm * c * sc_ * (1.0 - sc_)

    dya, dyc, dp_ga, dp_gc = _ew(
        f_merge_bwd, (n_lat,), [(dmrg, _rows(D_MODEL)), (pp, _rows(D_MODEL, 0)), (pp, _rows(D_MODEL, 1)),
                                (ya, _rows(D_MODEL)), (yc, _rows(D_MODEL))], [row_out(D_MODEL, BF16)] * 4, "merge_bwd")

    d_w_ao_p = _mm(o_pad, dya, "tn", 1024, D_MODEL, T, tm=1024, tn=D_MODEL, tk=tk_t, name="w_attn_out_dw", out_dtype=BF16)
    do_pad = _mm(dya, W["w_attn_out"], "nt", T, 1024, D_MODEL, tm=tm_lat, tn=1024, tk=D_MODEL, name="w_attn_out_dx")
    d_w_co = _mm(e, dyc, "tn", CONV_DIM, D_MODEL, T, tm=CONV_DIM, tn=256, tk=tk_t, name="w_conv_out_dw", out_dtype=BF16,
                 o_spec=pl.BlockSpec((None, CONV_DIM, 256), lambda i, j, k: (j, i, 0)), out_shape=(4, CONV_DIM, 256))
    de = _mm(dyc, W["w_conv_out"], "nt", T, CONV_DIM, D_MODEL, tm=tm_lat, tn=CONV_DIM, tk=256, name="w_conv_out_dx",
             b_spec=pl.BlockSpec((None, CONV_DIM, 256), lambda i, j, k: (k, j, 0)))

    def f_conv_bwd(ids, xin, cb, cc, d_e, w, b):
        z = cc * xin
        cz = _conv(z, w, b)
        dcz = d_e * cb
        w0, w1, w2 = _conv_bwd_w(dcz, z)
        dz = _conv_bwd_x(dcz, w)
        return dz * cc, d_e * cz, dz * xin, _colsum(dcz), w0, w1, w2

    cvec_c = ((1, CONV_DIM), F32, pl.BlockSpec((1, tc), lambda j: (0, j)), None)
    conv_b = _ew(f_conv_bwd, (CONV_DIM // tc,),
                 [(pp, colT(CX0 // tc)), (pp, colT(CB0 // tc)), (pp, colT(CC0 // tc)), (de, colT(0)),
                  (W["conv_w"], pl.BlockSpec((3, tc), lambda j: (0, j))), (W["conv_b"], pl.BlockSpec((1, tc), lambda j: (0, j)))],
                 [((T, CONV_DIM), BF16, colT(0), None)] * 3 + [cvec_c] * 4, "conv_bwd")
    dp_cx, dp_cb, dp_cc, d_conv_b = conv_b[:4]
    d_conv_w = jnp.concatenate(conv_b[4:7], axis=0)

    dq_raw, dkv, dp_kr = _attn_bwd(q_raw, kv, pp, o_pad, do_pad, lse, tabs, tabs_inv, T, TT)

    tk_a = TT
    d_w_uq_t = _mm(nq, dq_raw, "tn", Q_RANK, 1024, TT, tm=Q_RANK, tn=1024, tk=tk_a, name="w_uq_dw", transpose_out=True)
    dnq = _mm(dq_raw, W["w_uq_t"], "nn", TT, Q_RANK, 1024, tm=tm_all, tn=Q_RANK, tk=1024, name="w_uq_dx")
    d_w_ukv = _mm(nkv, dkv, "tn", KV_RANK, 1024, TT, tm=KV_RANK, tn=256, tk=tk_a, name="w_ukv_dw", out_dtype=BF16,
                  o_spec=pl.BlockSpec((None, KV_RANK, 256), lambda i, j, k: (j, i, 0)), out_shape=(4, KV_RANK, 256))
    dnkv = _mm(dkv, W["w_ukv"], "nt", TT, KV_RANK, 1024, tm=tm_all, tn=KV_RANK, tk=256, name="w_ukv_dx",
               b_spec=pl.BlockSpec((None, KV_RANK, 256), lambda i, j, k: (k, j, 0)))
    dnkv = early_grads("mid", {
        "w_attn_out": jnp.transpose(d_w_ao_p.reshape(N_HEADS, HEAD_PAD, 4, 256)[:, 64:], (2, 0, 1, 3)).reshape(
            4, N_HEADS * 64, 256),
        "w_conv_out": d_w_co,
        "w_uq": d_w_uq_t.reshape(4, 2, HEAD_PAD, Q_RANK)[:, :, :QK_DIM].reshape(4, 2 * QK_DIM, Q_RANK).astype(BF16),
        "w_ukv": d_w_ukv}, dnkv)

    def f_lowrank_bwd(ids, ckv, cq, dkv_, dq_, gkv, gq):
        rk, rq = _rms(ckv), _rms(cq)
        nk, nq_ = ckv * rk, cq * rq
        return (_rms_bwd(dkv_ * gkv, nk, rk), _rms_bwd(dq_ * gq, nq_, rq), _colsum(dkv_ * nk), _colsum(dq_ * nq_))

    dp_kv, dp_q, dg_kv, dg_q = _ew(
        f_lowrank_bwd, (n_all,), [(pp, _rows(KV_RANK, KV0 // KV_RANK)), (pp, _rows(Q_RANK, Q0 // Q_RANK)),
                                  (dnkv, _rows(KV_RANK)), (dnq, _rows(Q_RANK)), (W["kv_norm_g"], vec(KV_RANK)),
                                  (W["q_norm_g"], vec(Q_RANK))],
        [row_out(KV_RANK, BF16, TT), row_out(Q_RANK, BF16, TT), acc_out(KV_RANK), acc_out(Q_RANK)], "lowrank_norm_bwd")

    lat_cols = jnp.concatenate([dp_ga, dp_gc, dp_cx, dp_cb, dp_cc], axis=1)
    dpp = jnp.concatenate([jnp.pad(lat_cols, ((0, TT - T), (0, 0))), dp_kv, dp_q, dp_kr.astype(BF16)], axis=1)
    d_w_in_t = _mm(hh, dpp, "tn", D_MODEL, P_COLS, TT, tm=512, tn=2176, tk=TT, name="w_in_dw",
                   transpose_out=True)
    dhh = _mm(dpp, W["w_in_t"], "nn", TT, D_MODEL, P_COLS, tm=tm_all, tn=512, tk=2176, name="w_in_dx")

    def f_norm1_bwd(ids, x, dh, dres, g, sc):
        r = _rms(x)
        xn = x * r
        return (dres + _rms_bwd(dh * g * (1.0 + sc), xn, r), _colsum(dh), _colsum(dh * xn * g),
                _colsum(dh * xn * (1.0 + sc)))

    grad_x, dsh1, dsc1, dg_n1 = _ew(
        f_norm1_bwd, (n_lat,), [(xx, _rows(D_MODEL)), (dhh, _rows(D_MODEL)), (dx1, _rows(D_MODEL)),
                                (W["norm1_g"], vec(D_MODEL)), (sc1, vec(D_MODEL))],
        [row_out(D_MODEL, F32)] + [acc_out(D_MODEL)] * 3, "norm1_bwd")

    def f_norm1_ctx_bwd(ids, x, dh, g, sc):
        xn = x * _rms(x)
        return _colsum(dh), _colsum(dh * xn * g), _colsum(dh * xn * (1.0 + sc))

    n_ctx = n_all - n_lat
    dcsh1, dcsc1, dg_n1c = _ew(
        f_norm1_ctx_bwd, (n_ctx,), [(xx, _rows(D_MODEL, 0, n_lat)), (dhh, _rows(D_MODEL, 0, n_lat)),
                                    (W["norm1_g"], vec(D_MODEL)), (csc1, vec(D_MODEL))], [acc_out(D_MODEL)] * 3,
        "norm1_ctx_bwd")

    big = {"w_in": _w_in_t_shards_from_p(d_w_in_t).astype(BF16)}
    zero = jnp.zeros((1, 4 * D_MODEL), F32)
    small = {
        "dmod_lat": jnp.concatenate([dsh1, dsc1, dg1, dsh2, dsc2, dg2], axis=1),
        "dmod_ctx": jnp.concatenate([dcsh1, dcsc1, zero], axis=1),
        "norm1_g": dg_n1 + dg_n1c, "norm2_g": dg_n2, "final_g": dg_f, "q_norm_g": dg_q, "kv_norm_g": dg_kv,
        "conv_b": d_conv_b, "conv_w": d_conv_w.reshape(1, -1), "ffn_conv_b": d_ffn_conv_b,
        "ffn_conv_w": d_ffn_conv_w.reshape(1, -1),
    }
    return grad_x, loss, big, small


SMALL = (("dmod_lat", 6144), ("dmod_ctx", 6144), ("norm1_g", 1024), ("norm2_g", 1024), ("final_g", 1024),
         ("q_norm_g", 384), ("kv_norm_g", 256), ("conv_b", 512), ("conv_w", 1536), ("ffn_conv_b", 5632),
         ("ffn_conv_w", 16896))
SMALL_ROWS = 320


def _adamw(w, g, m, v, name):
    R, C = w.shape
    tr = 8 if R % 8 == 0 else R
    for t in range(8, R + 1, 8):
        if R % t == 0 and t * C * 4 <= (1 << 20):
            tr = t
    c1, c2 = 1.0 - ADAM_B1 ** ADAM_STEP, 1.0 - ADAM_B2 ** ADAM_STEP

    def fn(ids, w_, g_, m_, v_):
        m2 = ADAM_B1 * m_ + (1.0 - ADAM_B1) * g_
        v2 = ADAM_B2 * v_ + (1.0 - ADAM_B2) * (g_ * g_)
        delta = -ADAM_LR * ((m2 / c1) / (jnp.sqrt(v2 / c2) + ADAM_EPS) + ADAM_WD * w_)
        return delta, m2, v2

    spec = pl.BlockSpec((tr, C), lambda i: (i, 0))
    return _ew(fn, (R // tr,), [(w, spec), (g, spec), (m, spec), (v, spec)], [((R, C), F32, spec, None)] * 3, name)


def kernel(x, c, ctx, c_ctx, w_ada, b_ada, norm1_g, w_in, q_norm_g, kv_norm_g, w_uq, w_ukv, conv_w, conv_b, w_attn_out, w_conv_out, w_o, norm2_g, w_up, ffn_conv_w, ffn_conv_b, w_down, final_g, loss_target, m_c_ctx, m_w_ada, m_b_ada, m_norm1_g, m_w_in, m_q_norm_g, m_kv_norm_g, m_w_uq, m_w_ukv, m_conv_w, m_conv_b, m_w_attn_out, m_w_conv_out, m_w_o, m_norm2_g, m_w_up, m_ffn_conv_w, m_ffn_conv_b, m_w_down, m_final_g, v_c_ctx, v_w_ada, v_b_ada, v_norm1_g, v_w_in, v_q_norm_g, v_kv_norm_g, v_w_uq, v_w_ukv, v_conv_w, v_conv_b, v_w_attn_out, v_w_conv_out, v_w_o, v_norm2_g, v_w_up, v_ffn_conv_w, v_ffn_conv_b, v_w_down, v_final_g):
    mx, my, mc = lax.axis_index("x"), lax.axis_index("y"), lax.axis_index("c")
    chip = 2 * mx + my
    dev = 4 * mx + 2 * my + mc
    T, Tc = x.shape[1], ctx.shape[1]
    TT = T + Tc
    w_in_t, m_w_in_t, v_w_in_t = (jnp.transpose(a[0]) for a in (w_in, m_w_in, v_w_in))
    w_uq_t, m_w_uq_t, v_w_uq_t = (jnp.transpose(a[0]) for a in (w_uq, m_w_uq, v_w_uq))
    shards = {"w_in": jnp.pad(w_in_t, ((0, W_IN_SHARD_PAD - W_IN_SHARD), (0, 0))), "w_uq": w_uq_t, "w_ukv": w_ukv[0],
              "w_attn_out": w_attn_out[0], "w_conv_out": w_conv_out[0], "w_o": w_o[0], "w_up": w_up[0],
              "w_down": w_down[0]}

    conv_sh = jnp.concatenate([conv_w[0], ffn_conv_w[0]], axis=1)
    pay1 = jnp.concatenate([jnp.pad(c, ((0, 7), (0, 0))), jnp.pad(conv_sh, ((0, 5), (0, 0)))], axis=1)
    got1 = _allgather8(pay1, "gather_cond", in_vmem=True)[0].reshape(8, 8, 2560)
    c_all = got1[:, 0, :D_MODEL]
    conv_all = got1[0::2, :3, D_MODEL:]
    conv_w_full = _cols_from_shards(conv_all[:, :, :128])
    ffn_conv_w_full = _cols_from_shards(conv_all[:, :, 128:])

    cond = jnp.concatenate([c_all, c_ctx.reshape(1, D_MODEL), jnp.zeros((7, D_MODEL), F32)], axis=0)

    def f_silu(ids, v):
        return (v * _sigmoid(v),)

    (s16,) = _ew(f_silu, (1,), [(cond, _full((16, D_MODEL)))], [((16, D_MODEL), F32, _full((16, D_MODEL)), None)], "silu_cond")
    mod_sh = _mm(s16, w_ada[0], "nn", 16, 1536, D_MODEL, tm=16, tn=768, tk=D_MODEL, name="w_ada_fwd")
    got2, after_mod = _allgather8(mod_sh, "gather_mod", in_vmem=True)
    mod_all = _cols_from_shards(got2.reshape(4, 2, 16, 1536)[:, 0]) + b_ada
    mod_lat = lax.dynamic_slice_in_dim(mod_all, dev, 1, axis=0)
    mod_ctx = mod_all[8:9]

    names = [n for n, _ in BIG]
    first = [n for n in names if n not in GATHER_LATE]
    gathered, zero = _gather_weights([(shards[n] + after_mod[0, 0]).astype(BF16) for n in first])
    full = dict(zip(first, gathered))
    xx = jnp.concatenate([x[0], ctx[0]], axis=0)
    late_bf = [(shards[n] + zero[0, 0]).astype(BF16) for n in GATHER_LATE]
    g_send, g_recv, late_src, late_land, xx = _ici_start(
        "gather", late_bf, [(4,) + s.shape for s in late_bf], xx, "gather_late_start")

    def late_weights(after):
        src, land = _ici_wait("gather", g_send, g_recv, late_src, late_land, after, "gather_late_wait")
        got = dict(zip(GATHER_LATE, _gather_finish(src, land)))
        wao = _cols_from_shards(got["w_attn_out"]).reshape(N_HEADS, 64, D_MODEL)
        return {"w_attn_out": jnp.pad(wao, ((0, 0), (64, 0), (0, 0))).reshape(N_HEADS * HEAD_PAD, D_MODEL),
                "w_conv_out": got["w_conv_out"], "w_o": got["w_o"].reshape(D_MODEL, D_MODEL), "w_up": got["w_up"],
                "w_down": got["w_down"].reshape(D_FF, D_MODEL)}

    wuq_t = full["w_uq"].reshape(N_HEADS, QK_DIM, Q_RANK)
    W = {
        "w_in_t": _w_in_t_p_from_shards(full["w_in"]),
        "w_uq_t": jnp.pad(wuq_t, ((0, 0), (0, HEAD_PAD - QK_DIM), (0, 0))).reshape(N_HEADS * HEAD_PAD, Q_RANK),
        "w_ukv": full["w_ukv"],
        "norm1_g": norm1_g, "norm2_g": norm2_g, "final_g": final_g.reshape(1, D_MODEL), "q_norm_g": q_norm_g,
        "kv_norm_g": kv_norm_g, "conv_w": conv_w_full, "conv_b": conv_b, "ffn_conv_w": ffn_conv_w_full,
        "ffn_conv_b": ffn_conv_b,
    }

    place = jnp.stack([chip, mc]).astype(jnp.int32)
    early = {}

    def early_grads(tag, g, carry):
        gs = list(g.values())
        from_sib = _rs_pair(gs, "rs_pair_" + tag)
        sums = [_add_pair(gs[w], from_sib[w], place, "rs_pair_add_" + n) for w, n in enumerate(g)]
        send, recv, sums, land, carry = _ici_start(
            "scatter", sums, [(3,) + s.shape[1:] for s in sums], carry, "rs_chips_" + tag + "_start")
        early[tag] = (list(g), send, recv, sums, land)
        return carry

    grad_x, loss_part, gbig, gsmall = _local_step(xx, loss_target[0], mod_lat, mod_ctx, W, late_weights, early_grads)
    loss = lax.psum(loss_part[0, 0], ("x", "y", "c"))

    pay3 = jnp.concatenate([gsmall[n].reshape(-1) for n, _ in SMALL])
    pay3 = jnp.pad(pay3, (0, SMALL_ROWS * 128 - pay3.shape[0])).reshape(SMALL_ROWS, 128)
    got3 = _allgather8(pay3, "gather_small", in_vmem=True)[0]

    def f_sum8(ids, a):
        s = a[0:SMALL_ROWS]
        for d in range(1, 8):
            s = s + a[d * SMALL_ROWS:(d + 1) * SMALL_ROWS]
        return (s,)

    (vsum,) = _ew(f_sum8, (1,), [(got3, _full((8 * SMALL_ROWS, 128)))],
                  [((SMALL_ROWS, 128), F32, _full((SMALL_ROWS, 128)), None)], "sum_small")
    vflat = vsum.reshape(-1)
    gvec, off = {}, 0
    for n, size in SMALL:
        gvec[n] = vflat[off:off + size]
        off += size
    dmod_rows = got3.reshape(8, SMALL_ROWS * 128)[:, :6 * D_MODEL]
    dm16 = jnp.concatenate([dmod_rows, gvec["dmod_ctx"].reshape(1, -1), jnp.zeros((7, 6 * D_MODEL), F32)], axis=0)

    def f_colsum(ids, a):
        return (_colsum(a),)

    (g_b_ada,) = _ew(f_colsum, (1,), [(dm16, _full((16, 6 * D_MODEL)))],
                     [((1, 6 * D_MODEL), F32, _full((1, 6 * D_MODEL)), None)], "b_ada_grad")
    dm_sh = lax.dynamic_slice_in_dim(dm16, chip * 1536, 1536, axis=1)
    g_w_ada = _mm(s16, dm_sh, "tn", D_MODEL, 1536, 16, tm=512, tn=768, tk=16, name="w_ada_dw")
    dcond_part = _mm(dm_sh, w_ada[0], "nt", 16, D_MODEL, 1536, tm=16, tn=512, tk=1536, name="w_ada_dx")
    got4 = _allgather8(dcond_part[8:16], "gather_dcond", in_vmem=True)[0].reshape(4, 2, 8, D_MODEL)[:, 0, 0]

    def f_c_ctx(ids, parts, cc):
        s = _sigmoid(cc)
        d = parts[0:1] + parts[1:2] + parts[2:3] + parts[3:4]
        return (d * s * (1.0 + cc * (1.0 - s)),)

    (g_c_ctx,) = _ew(f_c_ctx, (1,), [(got4, _full((4, D_MODEL))), (c_ctx.reshape(1, D_MODEL), _full((1, D_MODEL)))],
                     [((1, D_MODEL), F32, _full((1, D_MODEL)), None)], "c_ctx_grad")

    last = list(gbig)
    from_sibling = _rs_pair([gbig[n] for n in last], "rs_pair")
    pair_sums = [_add_pair(gbig[n], from_sibling[w], place, "rs_pair_add_" + n) for w, n in enumerate(last)]
    lands = _rs_chips(pair_sums)
    done = last
    for tag, (tag_names, send, recv, sums, land) in early.items():
        sums, land = _ici_wait("scatter", send, recv, sums, land, grad_x, "rs_chips_" + tag + "_wait")
        done, pair_sums, lands = done + tag_names, pair_sums + sums, lands + land
    half_sums = [_add_chips(a, b, place, "rs_chip_add_" + n) for a, b, n in zip(pair_sums, lands, done)]
    gw = dict(zip(done, _rs_pair_back(half_sums)))
    gw["w_ada"] = g_w_ada

    moments = {"w_ada": (w_ada, m_w_ada, v_w_ada), "w_ukv": (w_ukv, m_w_ukv, v_w_ukv),
               "w_attn_out": (w_attn_out, m_w_attn_out, v_w_attn_out),
               "w_conv_out": (w_conv_out, m_w_conv_out, v_w_conv_out), "w_o": (w_o, m_w_o, v_w_o),
               "w_up": (w_up, m_w_up, v_w_up), "w_down": (w_down, m_w_down, v_w_down)}
    grads, deltas, new_m, new_v = {}, {}, {}, {}
    for n, (w_, m_, v_) in moments.items():
        d_, m2, v2 = _adamw(w_[0], gw[n], m_[0], v_[0], "adamw_" + n)
        grads[n], deltas[n], new_m[n], new_v[n] = gw[n][None], d_[None], m2[None], v2[None]
    for n, (w_, m_, v_) in {"w_in": (w_in_t, m_w_in_t, v_w_in_t), "w_uq": (w_uq_t, m_w_uq_t, v_w_uq_t)}.items():
        d_, m2, v2 = _adamw(w_, gw[n], m_, v_, "adamw_" + n)
        back = lambda a: jnp.transpose(a)[None]
        grads[n], deltas[n], new_m[n], new_v[n] = back(gw[n][:w_.shape[0]]), back(d_), back(m2), back(v2)

    conv_w_g = lax.dynamic_slice_in_dim(gvec["conv_w"].reshape(3, CONV_DIM), chip * 128, 128, axis=1)
    ffn_conv_w_g = lax.dynamic_slice_in_dim(gvec["ffn_conv_w"].reshape(3, 2 * D_FF), chip * 1408, 1408, axis=1)
    vec_params = (("c_ctx", c_ctx, m_c_ctx, v_c_ctx, g_c_ctx), ("b_ada", b_ada, m_b_ada, v_b_ada, g_b_ada),
                  ("norm1_g", norm1_g, m_norm1_g, v_norm1_g, gvec["norm1_g"]),
                  ("q_norm_g", q_norm_g, m_q_norm_g, v_q_norm_g, gvec["q_norm_g"]),
                  ("kv_norm_g", kv_norm_g, m_kv_norm_g, v_kv_norm_g, gvec["kv_norm_g"]),
                  ("conv_w", conv_w, m_conv_w, v_conv_w, conv_w_g), ("conv_b", conv_b, m_conv_b, v_conv_b, gvec["conv_b"]),
                  ("norm2_g", norm2_g, m_norm2_g, v_norm2_g, gvec["norm2_g"]),
                  ("ffn_conv_w", ffn_conv_w, m_ffn_conv_w, v_ffn_conv_w, ffn_conv_w_g),
                  ("ffn_conv_b", ffn_conv_b, m_ffn_conv_b, v_ffn_conv_b, gvec["ffn_conv_b"]),
                  ("final_g", final_g, m_final_g, v_final_g, gvec["final_g"]))
    total = sum(p[1].size for p in vec_params)
    rows_v = -(-total // 1024) * 8

    def packv(idx):
        flat_v = jnp.concatenate([p[idx].reshape(-1) for p in vec_params])
        return jnp.pad(flat_v, (0, rows_v * 128 - total)).reshape(rows_v, 128)

    vd, vm, vv = _adamw(packv(1), packv(4), packv(2), packv(3), "adamw_vectors")
    off = 0
    for p in vec_params:
        n, shape, size = p[0], p[1].shape, p[1].size
        grads[n] = p[4].reshape(shape)
        deltas[n] = vd.reshape(-1)[off:off + size].reshape(shape)
        new_m[n] = vm.reshape(-1)[off:off + size].reshape(shape)
        new_v[n] = vv.reshape(-1)[off:off + size].reshape(shape)
        off += size

    order = ("c_ctx", "w_ada", "b_ada", "norm1_g", "w_in", "q_norm_g", "kv_norm_g", "w_uq", "w_ukv", "conv_w", "conv_b",
             "w_attn_out", "w_conv_out", "w_o", "norm2_g", "w_up", "ffn_conv_w", "ffn_conv_b", "w_down", "final_g")
    return (loss, grad_x[None], *[grads[n] for n in order], *[deltas[n] for n in order],
            *[new_m[n] for n in order], *[new_v[n] for n in order])
```

```python
import functools

import jax
import jax.numpy as jnp
from jax import lax
from jax.experimental import pallas as pl
from jax.experimental.pallas import tpu as pltpu

F32, BF16 = jnp.float32, jnp.bfloat16
MESH = pl.DeviceIdType.MESH

D_MODEL = 1024
N_HEADS = 8
HEAD_PAD = 128
QK_DIM = 96
Q_RANK, KV_RANK = 384, 256
CONV_DIM = 512
D_FF = 2816
GRID_W = 64
ROPE_THETA = 10000.0
EPS = 1e-6
GA0, GC0, CX0, CB0, CC0, KV0, Q0, KR0, P_COLS = 0, 1024, 2048, 2560, 3072, 3584, 3840, 4224, 4352
ROW_TILE = 256
VMEM_LIMIT_BYTES = 48 * 1024 * 1024

ADAM_LR, ADAM_B1, ADAM_B2, ADAM_EPS, ADAM_WD, ADAM_STEP = 0.001, 0.9, 0.999, 1e-08, 0.01, 10

BIG = (("w_in", (1088, 1024)), ("w_uq", (192, 384)), ("w_ukv", (256, 256)), ("w_attn_out", (512, 256)),
       ("w_conv_out", (512, 256)), ("w_o", (256, 1024)), ("w_up", (1024, 1408)), ("w_down", (704, 1024)))

GATHER_LATE = ("w_attn_out", "w_conv_out", "w_o", "w_up", "w_down")

NN = (((1,), (0,)), ((), ()))
NT = (((1,), (1,)), ((), ()))
TN = (((0,), (0,)), ((), ()))


def _cp(sem):
    return pltpu.CompilerParams(dimension_semantics=sem, vmem_limit_bytes=VMEM_LIMIT_BYTES)


def _in_hbm(arrays):
    return [pltpu.with_memory_space_constraint(a, pltpu.HBM) for a in arrays]


def _pick(n, prefs):
    for p in prefs:
        if n % p == 0:
            return p
    return n


def _mm(a, b, mode, M, N, K, *, tm, tn, tk, name, out_dtype=F32, a_spec=None, b_spec=None, o_spec=None,
        out_shape=None, transpose_out=False):
    assert M % tm == 0 and N % tn == 0 and K % tk == 0, (name, M, N, K, tm, tn, tk)
    nk = K // tk
    dims = {"nn": NN, "nt": NT, "tn": TN}[mode]
    if a_spec is None:
        a_spec = (pl.BlockSpec((tk, tm), lambda i, j, k: (k, i)) if mode == "tn"
                  else pl.BlockSpec((tm, tk), lambda i, j, k: (i, k)))
    if b_spec is None:
        b_spec = (pl.BlockSpec((tn, tk), lambda i, j, k: (j, k)) if mode == "nt"
                  else pl.BlockSpec((tk, tn), lambda i, j, k: (k, j)))
    if o_spec is None:
        o_spec = (pl.BlockSpec((tn, tm), lambda i, j, k: (j, i)) if transpose_out
                  else pl.BlockSpec((tm, tn), lambda i, j, k: (i, j)))
    if out_shape is None:
        out_shape = (N, M) if transpose_out else (M, N)

    def emit(o_ref, val):
        o_ref[...] = (val.T if transpose_out else val).astype(o_ref.dtype)

    def body(a_ref, b_ref, o_ref, *scratch):
        part = lax.dot_general(a_ref[...].astype(BF16), b_ref[...].astype(BF16), dims, preferred_element_type=F32)
        if nk == 1:
            emit(o_ref, part)
            return
        acc_ref, = scratch
        k = pl.program_id(2)

        @pl.when(k == 0)
        def _():
            acc_ref[...] = part

        @pl.when((k > 0) & (k < nk - 1))
        def _():
            acc_ref[...] += part

        @pl.when(k == nk - 1)
        def _():
            emit(o_ref, acc_ref[...] + part)

    return pl.pallas_call(
        body, grid=(M // tm, N // tn, nk), in_specs=[a_spec, b_spec], out_specs=o_spec,
        out_shape=pltpu.HBM(out_shape, out_dtype),
        scratch_shapes=[pltpu.VMEM((tm, tn), F32)] if nk > 1 else [],
        compiler_params=_cp(("parallel", "parallel", "arbitrary")), name=name)(*_in_hbm([a, b]))


def _ew(fn, grid, ins, outs, name, scalars=None):
    n_in = len(ins)
    n_sc = 0 if scalars is None else 1

    def store(ref, val, acc, ids):
        if isinstance(val, (list, tuple)):
            for h, v in enumerate(val):
                ref[h] = v.astype(ref.dtype)
            return
        if acc is None:
            ref[...] = val.astype(ref.dtype)
            return

        @pl.when(ids[acc] == 0)
        def _():
            ref[...] = val.astype(ref.dtype)

        @pl.when(ids[acc] > 0)
        def _():
            ref[...] += val.astype(ref.dtype)

    def body(*refs):
        refs = refs[n_sc:]
        ids = tuple(pl.program_id(a) for a in range(len(grid)))
        vals = fn(ids, *[r[...] for r in refs[:n_in]])
        for ref, val, (_, _, _, acc) in zip(refs[n_in:], vals, outs):
            store(ref, val, acc, ids)

    acc_axes = {o[3] for o in outs if o[3] is not None}
    sem = tuple("arbitrary" if a in acc_axes else "parallel" for a in range(len(grid)))
    in_specs, out_specs = [s for _, s in ins], [o[2] for o in outs]
    out_shape = [pltpu.HBM(o[0], o[1]) for o in outs]
    args = _in_hbm([a for a, _ in ins])
    if scalars is None:
        return pl.pallas_call(body, grid=grid, in_specs=in_specs, out_specs=out_specs, out_shape=out_shape,
                              compiler_params=_cp(sem), name=name)(*args)
    spec = pltpu.PrefetchScalarGridSpec(num_scalar_prefetch=1, grid=grid, in_specs=in_specs, out_specs=out_specs)
    return pl.pallas_call(body, grid_spec=spec, out_shape=out_shape, compiler_params=_cp(sem), name=name)(scalars, *args)


def _rows(width, cblk=0, roff=0, tr=ROW_TILE):
    return pl.BlockSpec((tr, width), lambda i: (i + roff, cblk))


def _full(shape):
    nd = len(shape)
    return pl.BlockSpec(shape, lambda *_: (0,) * nd)


def _sigmoid(x):
    return 1.0 / (1.0 + jnp.exp(-x))


def _rms(x):
    return lax.rsqrt(jnp.mean(x * x, axis=-1, keepdims=True) + EPS)


def _rms_bwd(dn, xn, r):
    return r * (dn - xn * jnp.mean(dn * xn, axis=-1, keepdims=True))


def _colsum(x):
    return jnp.sum(x, axis=0, keepdims=True)


def _shift_prev(x):
    rows = lax.broadcasted_iota(jnp.int32, x.shape, 0)
    return jnp.where(rows == 0, 0.0, pltpu.roll(x, 1, 0))


def _shift_next(x):
    rows = lax.broadcasted_iota(jnp.int32, x.shape, 0)
    return jnp.where(rows == x.shape[0] - 1, 0.0, pltpu.roll(x, x.shape[0] - 1, 0))


def _conv(x, w, b):
    return b + _shift_prev(x) * w[0:1] + x * w[1:2] + _shift_next(x) * w[2:3]


def _conv_bwd_x(dy, w):
    return _shift_next(dy) * w[0:1] + dy * w[1:2] + _shift_prev(dy) * w[2:3]


def _conv_bwd_w(dy, x):
    return _colsum(dy * _shift_prev(x)), _colsum(dy * x), _colsum(dy * _shift_next(x))


def _rope(x, cos, sin_lo, sin_hi):
    return x * cos + pltpu.roll(x, HEAD_PAD - 8, 1) * sin_lo + pltpu.roll(x, 8, 1) * sin_hi


ATTN_SCALE = QK_DIM ** -0.5


def _head_keys(kv_ref, kr_ref, cos_ref, slo_ref, shi_ref, kc_ref, vp_ref):
    kv = kv_ref[...]
    lane = lax.broadcasted_iota(jnp.int32, kv.shape, 1)
    kc_ref[...] = jnp.where(lane < 64, kv, _rope(kr_ref[...], cos_ref[...], slo_ref[...], shi_ref[...])).astype(BF16)
    vp_ref[...] = jnp.where(lane >= 64, kv, 0.0).astype(BF16)


def _attn_specs(tq, TT, clamp):
    row = (lambda i: jnp.minimum(i, clamp)) if clamp is not None else (lambda i: i)
    q = pl.BlockSpec((tq, HEAD_PAD), lambda h, i: (i, h))
    lat = pl.BlockSpec((tq, HEAD_PAD), lambda h, i: (row(i), h))
    keys = pl.BlockSpec((TT, HEAD_PAD), lambda h, i: (0, h))
    kr = pl.BlockSpec((TT, HEAD_PAD), lambda h, i: (0, KR0 // HEAD_PAD))
    tab_q = pl.BlockSpec((tq, HEAD_PAD), lambda h, i: (i, 0))
    tab_k = pl.BlockSpec((TT, HEAD_PAD), lambda h, i: (0, 0))
    lse = pl.BlockSpec((None, tq, 1), lambda h, i: (h, row(i), 0))
    return q, lat, keys, kr, tab_q, tab_k, lse


def _attn_fwd(q_raw, kv, pp, tabs, T, TT):
    tq = ROW_TILE
    cos, slo, shi = tabs

    def body(q_ref, kv_ref, kr_ref, cq, lq, hq, ck, lk, hk, o_ref, l_ref, kc, vp):
        @pl.when(pl.program_id(1) == 0)
        def _():
            _head_keys(kv_ref, kr_ref, ck, lk, hk, kc, vp)

        q = _rope(q_ref[...], cq[...], lq[...], hq[...]).astype(BF16)
        s = lax.dot_general(q, kc[...], NT, preferred_element_type=F32) * ATTN_SCALE
        m = jnp.max(s, axis=-1, keepdims=True)
        p = jnp.exp(s - m)
        l = jnp.sum(p, axis=-1, keepdims=True)
        o = lax.dot_general(p.astype(BF16), vp[...], NN, preferred_element_type=F32)
        o_ref[...] = o / l
        l_ref[...] = m + jnp.log(l)

    qs, _, keys, kr, tab_q, tab_k, lse = _attn_specs(tq, TT, None)
    return pl.pallas_call(
        body, grid=(N_HEADS, T // tq), in_specs=[qs, keys, kr, tab_q, tab_q, tab_q, tab_k, tab_k, tab_k],
        out_specs=[qs, lse],
        out_shape=[jax.ShapeDtypeStruct((T, N_HEADS * HEAD_PAD), F32), jax.ShapeDtypeStruct((N_HEADS, T, 1), F32)],
        scratch_shapes=[pltpu.VMEM((TT, HEAD_PAD), BF16), pltpu.VMEM((TT, HEAD_PAD), BF16)],
        compiler_params=_cp(("parallel", "arbitrary")), name="attn_fwd",
    )(*_in_hbm([q_raw, kv, pp, cos, slo, shi, cos, slo, shi]))


def _attn_bwd(q_raw, kv, pp, o, do, lse, tabs, tabs_inv, T, TT):
    tq = ROW_TILE
    nq = T // tq
    cos, slo, shi = tabs
    cos_i, slo_i, shi_i = tabs_inv

    def body(q_ref, kv_ref, kr_ref, cq, lq, hq, ck, lk, hk, iq, ilq, ihq, ik, ilk, ihk, o_ref, do_ref, l_ref,
             dq_ref, dkv_ref, dkr_ref, kc, vp, dk, dv):
        h, i = pl.program_id(0), pl.program_id(1)

        @pl.when(i == 0)
        def _():
            _head_keys(kv_ref, kr_ref, ck, lk, hk, kc, vp)
            dk[...] = jnp.zeros_like(dk)
            dv[...] = jnp.zeros_like(dv)

        @pl.when(i < nq)
        def _():
            q = _rope(q_ref[...], cq[...], lq[...], hq[...]).astype(BF16)
            k, v, d_o = kc[...], vp[...], do_ref[...]
            s = lax.dot_general(q, k, NT, preferred_element_type=F32) * ATTN_SCALE
            p = jnp.exp(s - l_ref[...])
            dob = d_o.astype(BF16)
            dp = lax.dot_general(dob, v, NT, preferred_element_type=F32)
            dd = jnp.sum(d_o * o_ref[...], axis=-1, keepdims=True)
            ds = (p * (dp - dd) * ATTN_SCALE).astype(BF16)
            dq = lax.dot_general(ds, k, NN, preferred_element_type=F32)
            dq_ref[...] = _rope(dq, iq[...], ilq[...], ihq[...]).astype(dq_ref.dtype)
            dk[...] += lax.dot_general(ds, q, TN, preferred_element_type=F32)
            dv[...] += lax.dot_general(p.astype(BF16), dob, TN, preferred_element_type=F32)

        @pl.when(i == nq)
        def _():
            dq_ref[...] = jnp.zeros_like(dq_ref)
            dkh = dk[...]
            lane = lax.broadcasted_iota(jnp.int32, dkh.shape, 1)
            dkv_ref[...] = jnp.where(lane < 64, dkh, dv[...]).astype(dkv_ref.dtype)
            rot = _rope(jnp.where((lane >= 64) & (lane < 96), dkh, 0.0), ik[...], ilk[...], ihk[...])

            @pl.when(h == 0)
            def _():
                dkr_ref[...] = rot

            @pl.when(h > 0)
            def _():
                dkr_ref[...] += rot

    qs, lat, keys, kr, tab_q, tab_k, lse_spec = _attn_specs(tq, TT, nq - 1)
    wide = jax.ShapeDtypeStruct((TT, N_HEADS * HEAD_PAD), BF16)
    return pl.pallas_call(
        body, grid=(N_HEADS, TT // tq),
        in_specs=[qs, keys, kr] + [tab_q] * 3 + [tab_k] * 3 + [tab_q] * 3 + [tab_k] * 3 + [lat, lat, lse_spec],
        out_specs=[qs, keys, pl.BlockSpec((TT, HEAD_PAD), lambda h, i: (0, 0))],
        out_shape=[wide, wide, jax.ShapeDtypeStruct((TT, HEAD_PAD), F32)],
        scratch_shapes=[pltpu.VMEM((TT, HEAD_PAD), BF16), pltpu.VMEM((TT, HEAD_PAD), BF16),
                        pltpu.VMEM((TT, HEAD_PAD), F32), pltpu.VMEM((TT, HEAD_PAD), F32)],
        compiler_params=_cp(("arbitrary", "arbitrary")), name="attn_bwd",
    )(*_in_hbm([q_raw, kv, pp, cos, slo, shi, cos, slo, shi, cos_i, slo_i, shi_i, cos_i, slo_i, shi_i, o, do, lse]))


def _allgather8(x, name, in_vmem):
    m_per, n = x.shape

    def body(x_ref, out_ref, token, send_sems, recv_sems, local_sem):
        token[...] = jnp.zeros_like(token)
        mx, my, mc = lax.axis_index("x"), lax.axis_index("y"), lax.axis_index("c")
        me, sibling = (mx, my, mc), (mx, my, 1 - mc)
        chips = [(1 - mx, my), (mx, 1 - my), (1 - mx, 1 - my)]

        def rows(px, py, pc):
            return out_ref.at[pl.ds((4 * px + 2 * py + pc) * m_per, m_per), :]

        def copy(k, block, to, src=None):
            return pltpu.make_async_remote_copy(
                src_ref=rows(*block) if src is None else src, dst_ref=rows(*block),
                send_sem=send_sems.at[k], recv_sem=recv_sems.at[k], device_id=to, device_id_type=MESH)

        mine = pltpu.make_async_copy(x_ref, rows(*me), local_sem)
        mine.start()
        first = [copy(0, me, sibling, src=x_ref)]
        first += [copy(1 + j, me, (*chip, mc), src=x_ref) for j, chip in enumerate(chips)]
        for cp in first:
            cp.start()
        passed = [copy(4 + j, (*chip, mc), sibling) for j, chip in enumerate(chips)]
        for j, chip in enumerate(chips):
            copy(1 + j, (*chip, mc), me).wait_recv()
            passed[j].start()
        copy(0, sibling, me).wait_recv()
        for j, chip in enumerate(chips):
            copy(4 + j, (*chip, 1 - mc), me).wait_recv()
        for cp in first + passed:
            cp.wait_send()
        mine.wait()

    space = pltpu.VMEM if in_vmem else pl.ANY
    return pl.pallas_call(
        body, out_shape=[jax.ShapeDtypeStruct((8 * m_per, n), x.dtype), jax.ShapeDtypeStruct((8, 128), F32)],
        in_specs=[pl.BlockSpec(memory_space=space)],
        out_specs=[pl.BlockSpec(memory_space=space), pl.BlockSpec(memory_space=pltpu.VMEM)],
        scratch_shapes=[pltpu.SemaphoreType.DMA((7,)), pltpu.SemaphoreType.DMA((7,)), pltpu.SemaphoreType.DMA],
        name=name)(x)


def _hbm_specs(n):
    return [pl.BlockSpec(memory_space=pl.ANY)] * n


def _gather_weights(shards):
    n = len(shards)
    halves = [s.shape[0] // 2 for s in shards]

    def body(*refs):
        ins, outs = refs[:n], refs[n:2 * n]
        token, send_sems, recv_sems = refs[2 * n:]
        token[...] = jnp.zeros_like(token)
        mx, my, mc = lax.axis_index("x"), lax.axis_index("y"), lax.axis_index("c")
        j_me = 2 * mx + my
        chips = [(1 - mx, my), (mx, 1 - my), (1 - mx, 1 - my)]

        def half(w, chip_idx, hc):
            return outs[w].at[chip_idx, pl.ds(hc * halves[w], halves[w]), :]

        def copy(w, k, src, dst, to):
            return pltpu.make_async_remote_copy(src_ref=src, dst_ref=dst, send_sem=send_sems.at[w, k],
                                                recv_sem=recv_sems.at[w, k], device_id=to, device_id_type=MESH)

        sends = []
        for w in range(n):
            cp = copy(w, 6, ins[w], outs[w].at[j_me], (mx, my, 1 - mc))
            cp.start()
            sends.append(cp)
        for k, (px, py) in enumerate(chips):
            for w in range(n):
                cp = copy(w, k, ins[w].at[pl.ds(mc * halves[w], halves[w]), :], half(w, j_me, mc), (px, py, mc))
                cp.start()
                sends.append(cp)
        for k, (px, py) in enumerate(chips):
            for w in range(n):
                got = half(w, 2 * px + py, mc)
                copy(w, k, got, got, (px, py, mc)).wait_recv()
                cp = copy(w, 3 + k, got, got, (mx, my, 1 - mc))
                cp.start()
                sends.append(cp)
        for k, (px, py) in enumerate(chips):
            for w in range(n):
                got = half(w, 2 * px + py, 1 - mc)
                copy(w, 3 + k, got, got, (mx, my, 1 - mc)).wait_recv()
        for w in range(n):
            own = outs[w].at[j_me]
            copy(w, 6, own, own, (mx, my, 1 - mc)).wait_recv()
        for cp in sends:
            cp.wait_send()

    res = pl.pallas_call(
        body, out_shape=[jax.ShapeDtypeStruct((4,) + s.shape, s.dtype) for s in shards]
        + [jax.ShapeDtypeStruct((8, 128), F32)],
        in_specs=_hbm_specs(n), out_specs=_hbm_specs(n) + [pl.BlockSpec(memory_space=pltpu.VMEM)],
        scratch_shapes=[pltpu.SemaphoreType.DMA((n, 7)), pltpu.SemaphoreType.DMA((n, 7))],
        name="gather_weights")(*shards)
    return list(res[:n]), res[n]


def _rs_pair(gs, name):
    n = len(gs)
    halves = [g.shape[1] // 2 for g in gs]

    def body(*refs):
        ins, lands = refs[:n], refs[n:2 * n]
        send_sems, recv_sems = refs[2 * n:]
        mx, my, mc = lax.axis_index("x"), lax.axis_index("y"), lax.axis_index("c")
        copies = []
        for w in range(n):
            h = halves[w]
            cp = pltpu.make_async_remote_copy(
                src_ref=ins[w].at[:, pl.ds((1 - mc) * h, h), :], dst_ref=lands[w], send_sem=send_sems.at[w],
                recv_sem=recv_sems.at[w], device_id=(mx, my, 1 - mc), device_id_type=MESH)
            cp.start()
            copies.append(cp)
        for cp in copies:
            cp.wait()

    return pl.pallas_call(
        body, out_shape=[jax.ShapeDtypeStruct((4, h, g.shape[2]), g.dtype) for g, h in zip(gs, halves)],
        in_specs=_hbm_specs(n), out_specs=_hbm_specs(n),
        scratch_shapes=[pltpu.SemaphoreType.DMA((n,)), pltpu.SemaphoreType.DMA((n,))], name=name)(*gs)


def _rs_chips(parts):
    n = len(parts)

    def body(*refs):
        ins, lands = refs[:n], refs[n:2 * n]
        send_sems, recv_sems = refs[2 * n:]
        mx, my, mc = lax.axis_index("x"), lax.axis_index("y"), lax.axis_index("c")
        copies = []
        for k, (px, py) in enumerate([(1 - mx, my), (mx, 1 - my), (1 - mx, 1 - my)]):
            for w in range(n):
                cp = pltpu.make_async_remote_copy(
                    src_ref=ins[w].at[2 * px + py], dst_ref=lands[w].at[k], send_sem=send_sems.at[w, k],
                    recv_sem=recv_sems.at[w, k], device_id=(px, py, mc), device_id_type=MESH)
                cp.start()
                copies.append(cp)
        for cp in copies:
            cp.wait()

    return list(pl.pallas_call(
        body, out_shape=[jax.ShapeDtypeStruct((3,) + p.shape[1:], p.dtype) for p in parts],
        in_specs=_hbm_specs(n), out_specs=_hbm_specs(n),
        scratch_shapes=[pltpu.SemaphoreType.DMA((n, 3)), pltpu.SemaphoreType.DMA((n, 3))], name="rs_chips")(*parts))


def _rs_pair_back(gs):
    n = len(gs)

    def body(*refs):
        outs = refs[n:2 * n]
        send_sems, recv_sems = refs[2 * n:]
        mx, my, mc = lax.axis_index("x"), lax.axis_index("y"), lax.axis_index("c")
        copies = []
        for w in range(n):
            h = gs[w].shape[0] // 2
            mine = outs[w].at[pl.ds(mc * h, h), :]
            cp = pltpu.make_async_remote_copy(src_ref=mine, dst_ref=mine, send_sem=send_sems.at[w],
                                              recv_sem=recv_sems.at[w], device_id=(mx, my, 1 - mc), device_id_type=MESH)
            cp.start()
            copies.append(cp)
        for cp in copies:
            cp.wait()

    return pl.pallas_call(
        body, out_shape=[jax.ShapeDtypeStruct(g.shape, g.dtype) for g in gs],
        in_specs=_hbm_specs(n), out_specs=_hbm_specs(n), input_output_aliases={w: w for w in range(n)},
        scratch_shapes=[pltpu.SemaphoreType.DMA((n,)), pltpu.SemaphoreType.DMA((n,))], name="rs_pair_back")(*gs)


_HBM = pl.BlockSpec(memory_space=pltpu.HBM)
_SEM = pl.BlockSpec(memory_space=pltpu.SEMAPHORE)
_EFFECT = pltpu.SideEffectType.DATAFLOW_SIDE_EFFECTING


def _ici_copies(kind, srcs, lands, send_sems, recv_sems):
    n = len(srcs)
    mx, my, mc = lax.axis_index("x"), lax.axis_index("y"), lax.axis_index("c")
    j_me = 2 * mx + my
    copies = []
    for k, (px, py) in enumerate([(1 - mx, my), (mx, 1 - my), (1 - mx, 1 - my)]):
        for w in range(n):
            if kind == "gather":
                h = srcs[w].shape[0] // 2
                src, dst = srcs[w].at[pl.ds(mc * h, h), :], lands[w].at[j_me, pl.ds(mc * h, h), :]
            else:
                src, dst = srcs[w].at[2 * px + py], lands[w].at[k]
            copies.append(pltpu.make_async_remote_copy(
                src_ref=src, dst_ref=dst, send_sem=send_sems.at[3 * w + k], recv_sem=recv_sems.at[3 * w + k],
                device_id=(px, py, mc), device_id_type=MESH))
    return copies


def _ici_start(kind, srcs, land_shapes, carry, name):
    n = len(srcs)

    def body(*refs):
        ins, lands = refs[:n], refs[n:2 * n]
        send_sems, recv_sems = refs[2 * n + 1], refs[2 * n + 2]
        for cp in _ici_copies(kind, ins, lands, send_sems, recv_sems):
            cp.start()

    hbm = lambda a: pltpu.with_memory_space_constraint(a, pltpu.HBM)
    lands = [lax.empty(s, srcs[0].dtype) for s in land_shapes]
    args = [hbm(a) for a in list(srcs) + lands + [carry]]
    out_shape = ([pltpu.SemaphoreType.DMA((3 * n,)), pltpu.SemaphoreType.DMA((3 * n,))]
                 + [pltpu.HBM(a.shape, a.dtype) for a in args])
    res = pl.pallas_call(
        body, name=name, out_shape=out_shape, in_specs=[_HBM] * len(args), out_specs=[_SEM, _SEM] + [_HBM] * len(args),
        input_output_aliases={i: 2 + i for i in range(len(args))},
        compiler_params=pltpu.CompilerParams(has_side_effects=_EFFECT))(*args)
    return res[0], res[1], list(res[2:2 + n]), list(res[2 + n:2 + 2 * n]), res[2 + 2 * n]


def _ici_wait(kind, send_sems, recv_sems, srcs, lands, after, name):
    n = len(srcs)

    def body(*refs):
        ins, zones = refs[:n], refs[n:2 * n]
        for cp in _ici_copies(kind, ins, zones, refs[2 * n], refs[2 * n + 1]):
            cp.wait_send()
            cp.wait_recv()

    args = list(srcs) + list(lands)
    res = pl.pallas_call(
        body, name=name, out_shape=[pltpu.HBM(a.shape, a.dtype) for a in args],
        in_specs=[_HBM] * len(args) + [_SEM, _SEM, pl.BlockSpec(memory_space=pl.ANY)], out_specs=[_HBM] * len(args),
        input_output_aliases={i: i for i in range(len(args))},
        compiler_params=pltpu.CompilerParams(has_side_effects=_EFFECT))(*args, send_sems, recv_sems, after)
    return list(res[:n]), list(res[n:])


def _gather_finish(shards, lands):
    n = len(shards)

    def body(*refs):
        own, outs = refs[:n], refs[2 * n:3 * n]
        send_sems, recv_sems = refs[3 * n:]
        mx, my, mc = lax.axis_index("x"), lax.axis_index("y"), lax.axis_index("c")
        j_me = 2 * mx + my
        sibling = (mx, my, 1 - mc)
        copies = []

        def push(w, k, src, dst):
            cp = pltpu.make_async_remote_copy(src_ref=src, dst_ref=dst, send_sem=send_sems.at[w, k],
                                              recv_sem=recv_sems.at[w, k], device_id=sibling, device_id_type=MESH)
            cp.start()
            copies.append(cp)

        for w in range(n):
            h = shards[w].shape[0] // 2
            push(w, 3, own[w], outs[w].at[j_me])
            for k, (px, py) in enumerate([(1 - mx, my), (mx, 1 - my), (1 - mx, 1 - my)]):
                got = outs[w].at[2 * px + py, pl.ds(mc * h, h), :]
                push(w, k, got, got)
        for cp in copies:
            cp.wait()

    return pl.pallas_call(
        body, out_shape=[jax.ShapeDtypeStruct(l.shape, l.dtype) for l in lands],
        in_specs=_hbm_specs(2 * n), out_specs=_hbm_specs(n), input_output_aliases={n + w: w for w in range(n)},
        scratch_shapes=[pltpu.SemaphoreType.DMA((n, 4)), pltpu.SemaphoreType.DMA((n, 4))], name="gather_finish",
    )(*shards, *lands)


def _tile_rows(h, c, itemsize, mult):
    best = h
    for t in range(mult, h + 1, mult):
        if h % t == 0 and t * c * itemsize <= (1 << 21):
            best = t
    return best


def _add_pair(g, land, place, name):
    _, h, c = land.shape
    t = _tile_rows(h, c, 2, 16)
    nb = h // t
    return _ew(lambda ids, u, v: (u.astype(F32) + v.astype(F32),), (4, nb),
               [(g, pl.BlockSpec((None, t, c), lambda j, i, s: (j, s[1] * nb + i, 0))),
                (land, pl.BlockSpec((None, t, c), lambda j, i, s: (j, i, 0)))],
               [(land.shape, BF16, pl.BlockSpec((None, t, c), lambda j, i, s: (j, i, 0)), None)], name, scalars=place)[0]


def _add_chips(own, land, place, name):
    _, h, c = land.shape
    t = _tile_rows(h, c, 4, 16)
    nb = h // t

    def fn(ids, a, b):
        return (((a.astype(F32) + b[0].astype(F32)) + b[1].astype(F32)) + b[2].astype(F32),)

    return _ew(fn, (nb,), [(own, pl.BlockSpec((None, t, c), lambda i, s: (s[0], i, 0))),
                           (land, pl.BlockSpec((3, t, c), lambda i, s: (0, i, 0)))],
               [((2 * h, c), F32, pl.BlockSpec((t, c), lambda i, s: (s[1] * nb + i, 0)), None)], name, scalars=place)[0]


W_IN_SEGMENTS = ((0, 256, KV0), (256, 288, KR0 + 64), (288, 672, Q0), (672, 1184, CX0), (1184, 1696, CB0),
                 (1696, 2208, CC0), (2208, 3232, GA0), (3232, 4256, GC0))
W_IN_SHARD = 1064


W_IN_SHARD_PAD = 1088


def _w_in_t_p_from_shards(s):
    pieces = []
    for o0, o1, p0 in sorted(W_IN_SEGMENTS, key=lambda t: t[2]):
        if p0 == KR0 + 64:
            pieces.append(jnp.zeros((64, s.shape[2]), s.dtype))
        for j in range(4):
            lo, hi = max(o0, j * W_IN_SHARD), min(o1, (j + 1) * W_IN_SHARD)
            if lo < hi:
                pieces.append(s[j, lo - j * W_IN_SHARD:hi - j * W_IN_SHARD])
    pieces.append(jnp.zeros((32, s.shape[2]), s.dtype))
    return jnp.concatenate(pieces, axis=0)


def _w_in_t_shards_from_p(g):
    shards = []
    for j in range(4):
        pieces = []
        for o0, o1, p0 in W_IN_SEGMENTS:
            lo, hi = max(o0, j * W_IN_SHARD), min(o1, (j + 1) * W_IN_SHARD)
            if lo < hi:
                pieces.append(g[p0 + lo - o0:p0 + hi - o0])
        pieces.append(jnp.zeros((W_IN_SHARD_PAD - W_IN_SHARD, g.shape[1]), g.dtype))
        shards.append(jnp.concatenate(pieces, axis=0))
    return jnp.stack(shards, axis=0)


def _cols_from_shards(s):
    return jnp.transpose(s, (1, 0, 2)).reshape(s.shape[1], -1)


def _rope_tables(T, TT, inverse):
    rows = T // GRID_W
    row = jnp.repeat(jnp.arange(rows), GRID_W).astype(F32)
    col = jnp.tile(jnp.arange(GRID_W), rows).astype(F32)
    inv = ROPE_THETA ** (-jnp.arange(0, 16, 2, dtype=F32) / 16)
    ang = jnp.concatenate([row[:, None] * inv, col[:, None] * inv], axis=-1)
    cos, sin = jnp.cos(ang), jnp.sin(ang)
    lane = jnp.arange(32)
    src = (lane // 16) * 8 + lane % 8
    lo = ((lane % 16) // 8 == 0).astype(F32)
    sgn = -1.0 if inverse else 1.0
    cos32 = cos[:, src]
    sin_lo32 = -sgn * sin[:, src] * lo
    sin_hi32 = sgn * sin[:, src] * (1.0 - lo)

    def widen(t32, fill):
        t = jnp.concatenate([jnp.full((T, 64), fill, F32), t32, jnp.full((T, 32), fill, F32)], axis=1)
        return jnp.concatenate([t, jnp.full((TT - T, HEAD_PAD), fill, F32)], axis=0)

    return widen(cos32, 1.0), widen(sin_lo32, 0.0), widen(sin_hi32, 0.0)


def _local_step(xx, tgt, mod_lat, mod_ctx, W, late_weights, early_grads):
    TT = xx.shape[0]
    T = tgt.shape[0]
    n_lat, n_all = T // ROW_TILE, TT // ROW_TILE
    sh1, sc1, g1, sh2, sc2, g2 = [mod_lat[:, k * D_MODEL:(k + 1) * D_MODEL] for k in range(6)]
    csh1, csc1 = mod_ctx[:, :D_MODEL], mod_ctx[:, D_MODEL:2 * D_MODEL]
    vec = lambda n: _full((1, n))
    row_out = lambda n, dt, rows=T: ((rows, n), dt, _rows(n), None)
    acc_out = lambda n: ((1, n), F32, _full((1, n)), 0)

    def f_norm1(ids, x, g, a_sh, a_sc, b_sh, b_sc):
        ctx = ids[0] >= n_lat
        sh, sc = jnp.where(ctx, b_sh, a_sh), jnp.where(ctx, b_sc, a_sc)
        return ((x * _rms(x) * g) * (1.0 + sc) + sh,)

    (hh,) = _ew(f_norm1, (n_all,), [(xx, _rows(D_MODEL)), (W["norm1_g"], vec(D_MODEL)), (sh1, vec(D_MODEL)),
                                   (sc1, vec(D_MODEL)), (csh1, vec(D_MODEL)), (csc1, vec(D_MODEL))],
                [row_out(D_MODEL, BF16, TT)], "norm1_fwd")
    tm_all = _pick(TT, (768, 256))
    pp = _mm(hh, W["w_in_t"], "nt", TT, P_COLS, D_MODEL, tm=tm_all, tn=2176, tk=D_MODEL, name="w_in_fwd")

    def f_lowrank(ids, ckv, cq, gkv, gq):
        return ckv * _rms(ckv) * gkv, cq * _rms(cq) * gq

    nkv, nq = _ew(f_lowrank, (n_all,), [(pp, _rows(KV_RANK, KV0 // KV_RANK)), (pp, _rows(Q_RANK, Q0 // Q_RANK)),
                                       (W["kv_norm_g"], vec(KV_RANK)), (W["q_norm_g"], vec(Q_RANK))],
                  [row_out(KV_RANK, BF16, TT), row_out(Q_RANK, BF16, TT)], "lowrank_norm_fwd")
    kv = _mm(nkv, W["w_ukv"], "nn", TT, 1024, KV_RANK, tm=tm_all, tn=256, tk=KV_RANK, name="w_ukv_fwd",
             b_spec=pl.BlockSpec((None, KV_RANK, 256), lambda i, j, k: (j, k, 0)))
    q_raw = _mm(nq, W["w_uq_t"], "nt", TT, 1024, Q_RANK, tm=tm_all, tn=1024, tk=Q_RANK, name="w_uq_fwd")

    tabs = _rope_tables(T, TT, inverse=False)
    tabs_inv = _rope_tables(T, TT, inverse=True)
    o_pad, lse = _attn_fwd(q_raw, kv, pp, tabs, T, TT)
    W = dict(W, **late_weights(o_pad))
    tm_lat = _pick(T, (1024, 512, 256))
    ya = _mm(o_pad, W["w_attn_out"], "nn", T, D_MODEL, 1024, tm=tm_lat, tn=D_MODEL, tk=1024, name="w_attn_out_fwd")

    tc = 256
    colT = lambda blk0: pl.BlockSpec((T, tc), lambda j: (0, blk0 + j))

    def f_conv(ids, xin, cb, cc, w, b):
        return (cb * _conv(cc * xin, w, b),)

    (e,) = _ew(f_conv, (CONV_DIM // tc,),
               [(pp, colT(CX0 // tc)), (pp, colT(CB0 // tc)), (pp, colT(CC0 // tc)),
                (W["conv_w"], pl.BlockSpec((3, tc), lambda j: (0, j))), (W["conv_b"], pl.BlockSpec((1, tc), lambda j: (0, j)))],
               [((T, CONV_DIM), BF16, colT(0), None)], "conv_fwd")
    yc = _mm(e, W["w_conv_out"], "nn", T, D_MODEL, CONV_DIM, tm=tm_lat, tn=256, tk=CONV_DIM, name="w_conv_out_fwd",
             b_spec=pl.BlockSpec((None, CONV_DIM, 256), lambda i, j, k: (j, k, 0)))

    def f_merge(ids, ga, gc, a, c):
        return (_sigmoid(ga) * a + _sigmoid(gc) * c,)

    (mrg,) = _ew(f_merge, (n_lat,), [(pp, _rows(D_MODEL, 0)), (pp, _rows(D_MODEL, 1)), (ya, _rows(D_MODEL)),
                                    (yc, _rows(D_MODEL))], [row_out(D_MODEL, BF16)], "merge_fwd")
    mo = _mm(mrg, W["w_o"], "nn", T, D_MODEL, D_MODEL, tm=tm_lat, tn=D_MODEL, tk=D_MODEL, name="w_o_fwd")

    def f_norm2(ids, x, m, gate, g, sh, sc):
        x1 = x + gate * m
        return x1, (x1 * _rms(x1) * g) * (1.0 + sc) + sh

    x1, h2 = _ew(f_norm2, (n_lat,), [(xx, _rows(D_MODEL)), (mo, _rows(D_MODEL)), (g1, vec(D_MODEL)),
                                    (W["norm2_g"], vec(D_MODEL)), (sh2, vec(D_MODEL)), (sc2, vec(D_MODEL))],
                 [row_out(D_MODEL, F32), row_out(D_MODEL, BF16)], "norm2_fwd")
    up = _mm(h2, W["w_up"], "nn", T, 2 * D_FF, D_MODEL, tm=tm_lat, tn=1408, tk=D_MODEL, name="w_up_fwd",
             b_spec=pl.BlockSpec((None, D_MODEL, 1408), lambda i, j, k: (j, k, 0)))

    n_ff = D_FF // tc
    ffw = lambda off, n=3: pl.BlockSpec((n, tc), lambda j: (0, j + off))

    def f_ffn(ids, ug, uv, wg, wv, bg, bv):
        gate, val = _conv(ug, wg, bg), _conv(uv, wv, bv)
        return (gate * _sigmoid(gate) * val,)

    (act,) = _ew(f_ffn, (n_ff,), [(up, colT(0)), (up, colT(n_ff)), (W["ffn_conv_w"], ffw(0)), (W["ffn_conv_w"], ffw(n_ff)),
                                 (W["ffn_conv_b"], ffw(0, 1)), (W["ffn_conv_b"], ffw(n_ff, 1))],
                 [((T, D_FF), BF16, colT(0), None)], "ffn_act_fwd")
    f = _mm(act, W["w_down"], "nn", T, D_MODEL, D_FF, tm=tm_lat, tn=D_MODEL, tk=D_FF, name="w_down_fwd")

    def f_head(ids, x1_, f_, gate, gf, t):
        x2 = x1_ + gate * f_
        r = _rms(x2)
        xn = x2 * r
        err = xn * gf - t
        loss = 0.5 * jnp.sum(jnp.mean(err * err, axis=-1, keepdims=True))
        dy = err * (1.0 / D_MODEL)
        dx2 = _rms_bwd(dy * gf, xn, r)
        return dx2, dx2 * gate, _colsum(dy * xn), _colsum(dx2 * f_), jnp.full((1, 128), loss, F32)

    dx2, df, dg_f, dg2, loss = _ew(
        f_head, (n_lat,), [(x1, _rows(D_MODEL)), (f, _rows(D_MODEL)), (g2, vec(D_MODEL)), (W["final_g"], vec(D_MODEL)),
                           (tgt, _rows(D_MODEL))],
        [row_out(D_MODEL, F32), row_out(D_MODEL, BF16), acc_out(D_MODEL), acc_out(D_MODEL), acc_out(128)], "loss_head")

    d_w_down = _mm(act, df, "tn", D_FF, D_MODEL, T, tm=1408, tn=D_MODEL, tk=T, name="w_down_dw",
                   out_dtype=BF16).reshape(4, D_FF // 4, D_MODEL)
    da = _mm(df, W["w_down"], "nt", T, D_FF, D_MODEL, tm=tm_lat, tn=1408, tk=D_MODEL, name="w_down_dx")

    tcb = 128
    n_fb = D_FF // tcb
    colb = lambda blk0: pl.BlockSpec((T, tcb), lambda j: (0, blk0 + j))
    ffwb = lambda off, n=3: pl.BlockSpec((n, tcb), lambda j: (0, j + off))
    cvec = ((1, D_FF), F32, pl.BlockSpec((1, tcb), lambda j: (0, j)), None)

    def f_ffn_bwd(ids, ug, uv, d_act, wg, wv, bg, bv):
        gate, val = _conv(ug, wg, bg), _conv(uv, wv, bv)
        s = _sigmoid(gate)
        d_gate = d_act * val * s * (1.0 + gate * (1.0 - s))
        d_val = d_act * gate * s
        wg0, wg1, wg2 = _conv_bwd_w(d_gate, ug)
        wv0, wv1, wv2 = _conv_bwd_w(d_val, uv)
        d_up = [_conv_bwd_x(d_gate, wg), _conv_bwd_x(d_val, wv)]
        return d_up, _colsum(d_gate), _colsum(d_val), wg0, wg1, wg2, wv0, wv1, wv2

    ffn_b = _ew(f_ffn_bwd, (n_fb,),
                [(up, colb(0)), (up, colb(n_fb)), (da, colb(0)), (W["ffn_conv_w"], ffwb(0)), (W["ffn_conv_w"], ffwb(n_fb)),
                 (W["ffn_conv_b"], ffwb(0, 1)), (W["ffn_conv_b"], ffwb(n_fb, 1))],
                [((2, T, D_FF), BF16, pl.BlockSpec((2, T, tcb), lambda j: (0, 0, j)), None)] + [cvec] * 8, "ffn_act_bwd")
    d_up3 = ffn_b[0]
    d_ffn_conv_b = jnp.concatenate([ffn_b[1], ffn_b[2]], axis=1)
    d_ffn_conv_w = jnp.concatenate([jnp.concatenate(ffn_b[3:6], axis=0), jnp.concatenate(ffn_b[6:9], axis=0)], axis=1)

    tk_t = T
    d_w_up = _mm(h2, d_up3, "tn", D_MODEL, 2 * D_FF, T, tm=D_MODEL, tn=1408, tk=tk_t, name="w_up_dw", out_dtype=BF16,
                 b_spec=pl.BlockSpec((None, tk_t, 1408), lambda i, j, k: (j // 2, k, j % 2)),
                 o_spec=pl.BlockSpec((None, D_MODEL, 1408), lambda i, j, k: (j, i, 0)), out_shape=(4, D_MODEL, 1408))
    dh2 = _mm(d_up3, W["w_up"], "nt", T, D_MODEL, 2 * D_FF, tm=tm_lat, tn=D_MODEL, tk=1408, name="w_up_dx",
              a_spec=pl.BlockSpec((None, tm_lat, 1408), lambda i, j, k: (k // 2, i, k % 2)),
              b_spec=pl.BlockSpec((None, D_MODEL, 1408), lambda i, j, k: (k, j, 0)))

    def f_norm2_bwd(ids, dx2_, dh, x1_, m, g, sc, gate):
        r = _rms(x1_)
        xn = x1_ * r
        dx1 = dx2_ + _rms_bwd(dh * g * (1.0 + sc), xn, r)
        return dx1, dx1 * gate, _colsum(dh), _colsum(dh * xn * g), _colsum(dh * xn * (1.0 + sc)), _colsum(dx1 * m)

    dx1, dmo, dsh2, dsc2, dg_n2, dg1 = _ew(
        f_norm2_bwd, (n_lat,), [(dx2, _rows(D_MODEL)), (dh2, _rows(D_MODEL)), (x1, _rows(D_MODEL)), (mo, _rows(D_MODEL)),
                                (W["norm2_g"], vec(D_MODEL)), (sc2, vec(D_MODEL)), (g1, vec(D_MODEL))],
        [row_out(D_MODEL, F32), row_out(D_MODEL, BF16)] + [acc_out(D_MODEL)] * 4, "norm2_bwd")
    d_w_o = _mm(mrg, dmo, "tn", D_MODEL, D_MODEL, T, tm=D_MODEL, tn=D_MODEL, tk=tk_t, name="w_o_dw",
                out_dtype=BF16).reshape(4, D_MODEL // 4, D_MODEL)
    dmrg = _mm(dmo, W["w_o"], "nt", T, D_MODEL, D_MODEL, tm=tm_lat, tn=D_MODEL, tk=D_MODEL, name="w_o_dx")
    dmrg = early_grads("late", {"w_o": d_w_o, "w_up": d_w_up, "w_down": d_w_down}, dmrg)

    def f_merge_bwd(ids, dm, ga, gc, a, c):
        sa, sc_ = _sigmoid(ga), _sigmoid(gc)
        return dm * sa, dm * sc_, dm * a * sa * (1.0 - sa), dm * c * sc_ * (1.0 - sc_)

    dya, dyc, dp_ga, dp_gc = _ew(
        f_merge_bwd, (n_lat,), [(dmrg, _rows(D_MODEL)), (pp, _rows(D_MODEL, 0)), (pp, _rows(D_MODEL, 1)),
                                (ya, _rows(D_MODEL)), (yc, _rows(D_MODEL))], [row_out(D_MODEL, BF16)] * 4, "merge_bwd")

    d_w_ao_p = _mm(o_pad, dya, "tn", 1024, D_MODEL, T, tm=1024, tn=D_MODEL, tk=tk_t, name="w_attn_out_dw", out_dtype=BF16)
    do_pad = _mm(dya, W["w_attn_out"], "nt", T, 1024, D_MODEL, tm=tm_lat, tn=1024, tk=D_MODEL, name="w_attn_out_dx")
    d_w_co = _mm(e, dyc, "tn", CONV_DIM, D_MODEL, T, tm=CONV_DIM, tn=256, tk=tk_t, name="w_conv_out_dw", out_dtype=BF16,
                 o_spec=pl.BlockSpec((None, CONV_DIM, 256), lambda i, j, k: (j, i, 0)), out_shape=(4, CONV_DIM, 256))
    de = _mm(dyc, W["w_conv_out"], "nt", T, CONV_DIM, D_MODEL, tm=tm_lat, tn=CONV_DIM, tk=256, name="w_conv_out_dx",
             b_spec=pl.BlockSpec((None, CONV_DIM, 256), lambda i, j, k: (k, j, 0)))

    def f_conv_bwd(ids, xin, cb, cc, d_e, w, b):
        z = cc * xin
        cz = _conv(z, w, b)
        dcz = d_e * cb
        w0, w1, w2 = _conv_bwd_w(dcz, z)
        dz = _conv_bwd_x(dcz, w)
        return dz * cc, d_e * cz, dz * xin, _colsum(dcz), w0, w1, w2

    cvec_c = ((1, CONV_DIM), F32, pl.BlockSpec((1, tc), lambda j: (0, j)), None)
    conv_b = _ew(f_conv_bwd, (CONV_DIM // tc,),
                 [(pp, colT(CX0 // tc)), (pp, colT(CB0 // tc)), (pp, colT(CC0 // tc)), (de, colT(0)),
                  (W["conv_w"], pl.BlockSpec((3, tc), lambda j: (0, j))), (W["conv_b"], pl.BlockSpec((1, tc), lambda j: (0, j)))],
                 [((T, CONV_DIM), BF16, colT(0), None)] * 3 + [cvec_c] * 4, "conv_bwd")
    dp_cx, dp_cb, dp_cc, d_conv_b = conv_b[:4]
    d_conv_w = jnp.concatenate(conv_b[4:7], axis=0)

    dq_raw, dkv, dp_kr = _attn_bwd(q_raw, kv, pp, o_pad, do_pad, lse, tabs, tabs_inv, T, TT)

    tk_a = TT
    d_w_uq_t = _mm(nq, dq_raw, "tn", Q_RANK, 1024, TT, tm=Q_RANK, tn=1024, tk=tk_a, name="w_uq_dw", transpose_out=True)
    dnq = _mm(dq_raw, W["w_uq_t"], "nn", TT, Q_RANK, 1024, tm=tm_all, tn=Q_RANK, tk=1024, name="w_uq_dx")
    d_w_ukv = _mm(nkv, dkv, "tn", KV_RANK, 1024, TT, tm=KV_RANK, tn=256, tk=tk_a, name="w_ukv_dw", out_dtype=BF16,
                  o_spec=pl.BlockSpec((None, KV_RANK, 256), lambda i, j, k: (j, i, 0)), out_shape=(4, KV_RANK, 256))
    dnkv = _mm(dkv, W["w_ukv"], "nt", TT, KV_RANK, 1024, tm=tm_all, tn=KV_RANK, tk=256, name="w_ukv_dx",
               b_spec=pl.BlockSpec((None, KV_RANK, 256), lambda i, j, k: (k, j, 0)))
    dnkv = early_grads("mid", {
        "w_attn_out": jnp.transpose(d_w_ao_p.reshape(N_HEADS, HEAD_PAD, 4, 256)[:, 64:], (2, 0, 1, 3)).reshape(
            4, N_HEADS * 64, 256),
        "w_conv_out": d_w_co,
        "w_uq": d_w_uq_t.reshape(4, 2, HEAD_PAD, Q_RANK)[:, :, :QK_DIM].reshape(4, 2 * QK_DIM, Q_RANK).astype(BF16),
        "w_ukv": d_w_ukv}, dnkv)

    def f_lowrank_bwd(ids, ckv, cq, dkv_, dq_, gkv, gq):
        rk, rq = _rms(ckv), _rms(cq)
        nk, nq_ = ckv * rk, cq * rq
        return (_rms_bwd(dkv_ * gkv, nk, rk), _rms_bwd(dq_ * gq, nq_, rq), _colsum(dkv_ * nk), _colsum(dq_ * nq_))

    dp_kv, dp_q, dg_kv, dg_q = _ew(
        f_lowrank_bwd, (n_all,), [(pp, _rows(KV_RANK, KV0 // KV_RANK)), (pp, _rows(Q_RANK, Q0 // Q_RANK)),
                                  (dnkv, _rows(KV_RANK)), (dnq, _rows(Q_RANK)), (W["kv_norm_g"], vec(KV_RANK)),
                                  (W["q_norm_g"], vec(Q_RANK))],
        [row_out(KV_RANK, BF16, TT), row_out(Q_RANK, BF16, TT), acc_out(KV_RANK), acc_out(Q_RANK)], "lowrank_norm_bwd")

    lat_cols = jnp.concatenate([dp_ga, dp_gc, dp_cx, dp_cb, dp_cc], axis=1)
    dpp = jnp.concatenate([jnp.pad(lat_cols, ((0, TT - T), (0, 0))), dp_kv, dp_q, dp_kr.astype(BF16)], axis=1)
    d_w_in_t = _mm(hh, dpp, "tn", D_MODEL, P_COLS, TT, tm=512, tn=2176, tk=TT, name="w_in_dw",
                   transpose_out=True)
    dhh = _mm(dpp, W["w_in_t"], "nn", TT, D_MODEL, P_COLS, tm=tm_all, tn=512, tk=2176, name="w_in_dx")

    def f_norm1_bwd(ids, x, dh, dres, g, sc):
        r = _rms(x)
        xn = x * r
        return (dres + _rms_bwd(dh * g * (1.0 + sc), xn, r), _colsum(dh), _colsum(dh * xn * g),
                _colsum(dh * xn * (1.0 + sc)))

    grad_x, dsh1, dsc1, dg_n1 = _ew(
        f_norm1_bwd, (n_lat,), [(xx, _rows(D_MODEL)), (dhh, _rows(D_MODEL)), (dx1, _rows(D_MODEL)),
                                (W["norm1_g"], vec(D_MODEL)), (sc1, vec(D_MODEL))],
        [row_out(D_MODEL, F32)] + [acc_out(D_MODEL)] * 3, "norm1_bwd")

    def f_norm1_ctx_bwd(ids, x, dh, g, sc):
        xn = x * _rms(x)
        return _colsum(dh), _colsum(dh * xn * g), _colsum(dh * xn * (1.0 + sc))

    n_ctx = n_all - n_lat
    dcsh1, dcsc1, dg_n1c = _ew(
        f_norm1_ctx_bwd, (n_ctx,), [(xx, _rows(D_MODEL, 0, n_lat)), (dhh, _rows(D_MODEL, 0, n_lat)),
                                    (W["norm1_g"], vec(D_MODEL)), (csc1, vec(D_MODEL))], [acc_out(D_MODEL)] * 3,
        "norm1_ctx_bwd")

    big = {"w_in": _w_in_t_shards_from_p(d_w_in_t).astype(BF16)}
    zero = jnp.zeros((1, 4 * D_MODEL), F32)
    small = {
        "dmod_lat": jnp.concatenate([dsh1, dsc1, dg1, dsh2, dsc2, dg2], axis=1),
        "dmod_ctx": jnp.concatenate([dcsh1, dcsc1, zero], axis=1),
        "norm1_g": dg_n1 + dg_n1c, "norm2_g": dg_n2, "final_g": dg_f, "q_norm_g": dg_q, "kv_norm_g": dg_kv,
        "conv_b": d_conv_b, "conv_w": d_conv_w.reshape(1, -1), "ffn_conv_b": d_ffn_conv_b,
        "ffn_conv_w": d_ffn_conv_w.reshape(1, -1),
    }
    return grad_x, loss, big, small


SMALL = (("dmod_lat", 6144), ("dmod_ctx", 6144), ("norm1_g", 1024), ("norm2_g", 1024), ("final_g", 1024),
         ("q_norm_g", 384), ("kv_norm_g", 256), ("conv_b", 512), ("conv_w", 1536), ("ffn_conv_b", 5632),
         ("ffn_conv_w", 16896))
SMALL_ROWS = 320


def _adam_update(w, g, m, v):
    c1, c2 = 1.0 - ADAM_B1 ** ADAM_STEP, 1.0 - ADAM_B2 ** ADAM_STEP
    m2 = ADAM_B1 * m + (1.0 - ADAM_B1) * g
    v2 = ADAM_B2 * v + (1.0 - ADAM_B2) * (g * g)
    return [-ADAM_LR * ((m2 / c1) / (jnp.sqrt(v2 / c2) + ADAM_EPS) + ADAM_WD * w), m2, v2]


def _adamw(w, g, m, v, name):
    R, C = w.shape
    tr = 8 if R % 8 == 0 else R
    for t in range(8, R + 1, 8):
        if R % t == 0 and t * C * 4 <= (1 << 20):
            tr = t
    spec = pl.BlockSpec((tr, C), lambda i: (i, 0))
    return _ew(lambda ids, *vals: _adam_update(*vals), (R // tr,), [(w, spec), (g, spec), (m, spec), (v, spec)],
               [((R, C), F32, spec, None)] * 3, name)


def kernel(x, c, ctx, c_ctx, w_ada, b_ada, norm1_g, w_in, q_norm_g, kv_norm_g, w_uq, w_ukv, conv_w, conv_b, w_attn_out, w_conv_out, w_o, norm2_g, w_up, ffn_conv_w, ffn_conv_b, w_down, final_g, loss_target, m_c_ctx, m_w_ada, m_b_ada, m_norm1_g, m_w_in, m_q_norm_g, m_kv_norm_g, m_w_uq, m_w_ukv, m_conv_w, m_conv_b, m_w_attn_out, m_w_conv_out, m_w_o, m_norm2_g, m_w_up, m_ffn_conv_w, m_ffn_conv_b, m_w_down, m_final_g, v_c_ctx, v_w_ada, v_b_ada, v_norm1_g, v_w_in, v_q_norm_g, v_kv_norm_g, v_w_uq, v_w_ukv, v_conv_w, v_conv_b, v_w_attn_out, v_w_conv_out, v_w_o, v_norm2_g, v_w_up, v_ffn_conv_w, v_ffn_conv_b, v_w_down, v_final_g):
    mx, my, mc = lax.axis_index("x"), lax.axis_index("y"), lax.axis_index("c")
    chip = 2 * mx + my
    dev = 4 * mx + 2 * my + mc
    T, Tc = x.shape[1], ctx.shape[1]
    TT = T + Tc
    w_in_t, m_w_in_t, v_w_in_t = (jnp.transpose(a[0]) for a in (w_in, m_w_in, v_w_in))
    w_uq_t, m_w_uq_t, v_w_uq_t = (jnp.transpose(a[0]) for a in (w_uq, m_w_uq, v_w_uq))
    shards = {"w_in": jnp.pad(w_in_t, ((0, W_IN_SHARD_PAD - W_IN_SHARD), (0, 0))), "w_uq": w_uq_t, "w_ukv": w_ukv[0],
              "w_attn_out": w_attn_out[0], "w_conv_out": w_conv_out[0], "w_o": w_o[0], "w_up": w_up[0],
              "w_down": w_down[0]}

    conv_sh = jnp.concatenate([conv_w[0], ffn_conv_w[0]], axis=1)
    pay1 = jnp.concatenate([jnp.pad(c, ((0, 7), (0, 0))), jnp.pad(conv_sh, ((0, 5), (0, 0)))], axis=1)
    got1 = _allgather8(pay1, "gather_cond", in_vmem=True)[0].reshape(8, 8, 2560)
    c_all = got1[:, 0, :D_MODEL]
    conv_all = got1[0::2, :3, D_MODEL:]
    conv_w_full = _cols_from_shards(conv_all[:, :, :128])
    ffn_conv_w_full = _cols_from_shards(conv_all[:, :, 128:])

    cond = jnp.concatenate([c_all, c_ctx.reshape(1, D_MODEL), jnp.zeros((7, D_MODEL), F32)], axis=0)

    def f_silu(ids, v):
        return (v * _sigmoid(v),)

    (s16,) = _ew(f_silu, (1,), [(cond, _full((16, D_MODEL)))], [((16, D_MODEL), F32, _full((16, D_MODEL)), None)], "silu_cond")
    mod_sh = _mm(s16, w_ada[0], "nn", 16, 1536, D_MODEL, tm=16, tn=768, tk=D_MODEL, name="w_ada_fwd")
    got2, after_mod = _allgather8(mod_sh, "gather_mod", in_vmem=True)
    mod_all = _cols_from_shards(got2.reshape(4, 2, 16, 1536)[:, 0]) + b_ada
    mod_lat = lax.dynamic_slice_in_dim(mod_all, dev, 1, axis=0)
    mod_ctx = mod_all[8:9]

    names = [n for n, _ in BIG]
    first = [n for n in names if n not in GATHER_LATE]
    gathered, zero = _gather_weights([(shards[n] + after_mod[0, 0]).astype(BF16) for n in first])
    full = dict(zip(first, gathered))
    xx = jnp.concatenate([x[0], ctx[0]], axis=0)
    late_bf = [(shards[n] + zero[0, 0]).astype(BF16) for n in GATHER_LATE]
    g_send, g_recv, late_src, late_land, xx = _ici_start(
        "gather", late_bf, [(4,) + s.shape for s in late_bf], xx, "gather_late_start")

    def late_weights(after):
        src, land = _ici_wait("gather", g_send, g_recv, late_src, late_land, after, "gather_late_wait")
        got = dict(zip(GATHER_LATE, _gather_finish(src, land)))
        wao = _cols_from_shards(got["w_attn_out"]).reshape(N_HEADS, 64, D_MODEL)
        return {"w_attn_out": jnp.pad(wao, ((0, 0), (64, 0), (0, 0))).reshape(N_HEADS * HEAD_PAD, D_MODEL),
                "w_conv_out": got["w_conv_out"], "w_o": got["w_o"].reshape(D_MODEL, D_MODEL), "w_up": got["w_up"],
                "w_down": got["w_down"].reshape(D_FF, D_MODEL)}

    wuq_t = full["w_uq"].reshape(N_HEADS, QK_DIM, Q_RANK)
    W = {
        "w_in_t": _w_in_t_p_from_shards(full["w_in"]),
        "w_uq_t": jnp.pad(wuq_t, ((0, 0), (0, HEAD_PAD - QK_DIM), (0, 0))).reshape(N_HEADS * HEAD_PAD, Q_RANK),
        "w_ukv": full["w_ukv"],
        "norm1_g": norm1_g, "norm2_g": norm2_g, "final_g": final_g.reshape(1, D_MODEL), "q_norm_g": q_norm_g,
        "kv_norm_g": kv_norm_g, "conv_w": conv_w_full, "conv_b": conv_b, "ffn_conv_w": ffn_conv_w_full,
        "ffn_conv_b": ffn_conv_b,
    }

    place = jnp.stack([chip, mc]).astype(jnp.int32)
    early = {}

    def early_grads(tag, g, carry):
        gs = list(g.values())
        from_sib = _rs_pair(gs, "rs_pair_" + tag)
        sums = [_add_pair(gs[w], from_sib[w], place, "rs_pair_add_" + n) for w, n in enumerate(g)]
        send, recv, sums, land, carry = _ici_start(
            "scatter", sums, [(3,) + s.shape[1:] for s in sums], carry, "rs_chips_" + tag + "_start")
        early[tag] = (list(g), send, recv, sums, land)
        return carry

    grad_x, loss_part, gbig, gsmall = _local_step(xx, loss_target[0], mod_lat, mod_ctx, W, late_weights, early_grads)
    loss = lax.psum(loss_part[0, 0], ("x", "y", "c"))

    pay3 = jnp.concatenate([gsmall[n].reshape(-1) for n, _ in SMALL])
    pay3 = jnp.pad(pay3, (0, SMALL_ROWS * 128 - pay3.shape[0])).reshape(SMALL_ROWS, 128)
    got3 = _allgather8(pay3, "gather_small", in_vmem=True)[0]

    def f_sum8(ids, a):
        s = a[0:SMALL_ROWS]
        for d in range(1, 8):
            s = s + a[d * SMALL_ROWS:(d + 1) * SMALL_ROWS]
        return (s,)

    (vsum,) = _ew(f_sum8, (1,), [(got3, _full((8 * SMALL_ROWS, 128)))],
                  [((SMALL_ROWS, 128), F32, _full((SMALL_ROWS, 128)), None)], "sum_small")
    vflat = vsum.reshape(-1)
    gvec, off = {}, 0
    for n, size in SMALL:
        gvec[n] = vflat[off:off + size]
        off += size
    dmod_rows = got3.reshape(8, SMALL_ROWS * 128)[:, :6 * D_MODEL]
    dm16 = jnp.concatenate([dmod_rows, gvec["dmod_ctx"].reshape(1, -1), jnp.zeros((7, 6 * D_MODEL), F32)], axis=0)

    def f_colsum(ids, a):
        return (_colsum(a),)

    (g_b_ada,) = _ew(f_colsum, (1,), [(dm16, _full((16, 6 * D_MODEL)))],
                     [((1, 6 * D_MODEL), F32, _full((1, 6 * D_MODEL)), None)], "b_ada_grad")
    dm_sh = lax.dynamic_slice_in_dim(dm16, chip * 1536, 1536, axis=1)
    g_w_ada = _mm(s16, dm_sh, "tn", D_MODEL, 1536, 16, tm=512, tn=768, tk=16, name="w_ada_dw")
    dcond_part = _mm(dm_sh, w_ada[0], "nt", 16, D_MODEL, 1536, tm=16, tn=512, tk=1536, name="w_ada_dx")
    got4 = _allgather8(dcond_part[8:16], "gather_dcond", in_vmem=True)[0].reshape(4, 2, 8, D_MODEL)[:, 0, 0]

    def f_c_ctx(ids, parts, cc):
        s = _sigmoid(cc)
        d = parts[0:1] + parts[1:2] + parts[2:3] + parts[3:4]
        return (d * s * (1.0 + cc * (1.0 - s)),)

    (g_c_ctx,) = _ew(f_c_ctx, (1,), [(got4, _full((4, D_MODEL))), (c_ctx.reshape(1, D_MODEL), _full((1, D_MODEL)))],
                     [((1, D_MODEL), F32, _full((1, D_MODEL)), None)], "c_ctx_grad")

    last = list(gbig)
    from_sibling = _rs_pair([gbig[n] for n in last], "rs_pair")
    pair_sums = [_add_pair(gbig[n], from_sibling[w], place, "rs_pair_add_" + n) for w, n in enumerate(last)]
    lands = _rs_chips(pair_sums)
    done = last
    for tag, (tag_names, send, recv, sums, land) in early.items():
        sums, land = _ici_wait("scatter", send, recv, sums, land, grad_x, "rs_chips_" + tag + "_wait")
        done, pair_sums, lands = done + tag_names, pair_sums + sums, lands + land
    half_sums = [_add_chips(a, b, place, "rs_chip_add_" + n) for a, b, n in zip(pair_sums, lands, done)]
    gw = dict(zip(done, _rs_pair_back(half_sums)))
    gw["w_ada"] = g_w_ada

    moments = {"w_ada": (w_ada, m_w_ada, v_w_ada), "w_ukv": (w_ukv, m_w_ukv, v_w_ukv),
               "w_attn_out": (w_attn_out, m_w_attn_out, v_w_attn_out),
               "w_conv_out": (w_conv_out, m_w_conv_out, v_w_conv_out), "w_o": (w_o, m_w_o, v_w_o),
               "w_up": (w_up, m_w_up, v_w_up), "w_down": (w_down, m_w_down, v_w_down)}
    grads, deltas, new_m, new_v = {}, {}, {}, {}
    for n, (w_, m_, v_) in moments.items():
        d_, m2, v2 = _adamw(w_[0], gw[n], m_[0], v_[0], "adamw_" + n)
        grads[n], deltas[n], new_m[n], new_v[n] = gw[n][None], d_[None], m2[None], v2[None]
    for n, (w_, m_, v_) in {"w_in": (w_in_t, m_w_in_t, v_w_in_t), "w_uq": (w_uq_t, m_w_uq_t, v_w_uq_t)}.items():
        d_, m2, v2 = _adamw(w_, gw[n], m_, v_, "adamw_" + n)
        back = lambda a: jnp.transpose(a)[None]
        grads[n], deltas[n], new_m[n], new_v[n] = back(gw[n][:w_.shape[0]]), back(d_), back(m2), back(v2)

    conv_w_g = lax.dynamic_slice_in_dim(gvec["conv_w"].reshape(3, CONV_DIM), chip * 128, 128, axis=1)
    ffn_conv_w_g = lax.dynamic_slice_in_dim(gvec["ffn_conv_w"].reshape(3, 2 * D_FF), chip * 1408, 1408, axis=1)
    vec_params = (("c_ctx", c_ctx, m_c_ctx, v_c_ctx, g_c_ctx), ("b_ada", b_ada, m_b_ada, v_b_ada, g_b_ada),
                  ("norm1_g", norm1_g, m_norm1_g, v_norm1_g, gvec["norm1_g"]),
                  ("q_norm_g", q_norm_g, m_q_norm_g, v_q_norm_g, gvec["q_norm_g"]),
                  ("kv_norm_g", kv_norm_g, m_kv_norm_g, v_kv_norm_g, gvec["kv_norm_g"]),
                  ("conv_w", conv_w, m_conv_w, v_conv_w, conv_w_g), ("conv_b", conv_b, m_conv_b, v_conv_b, gvec["conv_b"]),
                  ("norm2_g", norm2_g, m_norm2_g, v_norm2_g, gvec["norm2_g"]),
                  ("ffn_conv_w", ffn_conv_w, m_ffn_conv_w, v_ffn_conv_w, ffn_conv_w_g),
                  ("ffn_conv_b", ffn_conv_b, m_ffn_conv_b, v_ffn_conv_b, gvec["ffn_conv_b"]),
                  ("final_g", final_g, m_final_g, v_final_g, gvec["final_g"]))
    two_d = lambda a: a.reshape((-1, a.shape[-1]))

    def f_adam_many(ids, *vals):
        out = []
        for k in range(len(vec_params)):
            out += _adam_update(*vals[4 * k:4 * k + 4])
        return out

    ins_v, outs_v = [], []
    for p in vec_params:
        shp = two_d(p[1]).shape
        ins_v += [(two_d(a), _full(shp)) for a in (p[1], p[4], p[2], p[3])]
        outs_v += [(shp, F32, _full(shp), None)] * 3
    res_v = _ew(f_adam_many, (1,), ins_v, outs_v, "adamw_vectors")
    for k, p in enumerate(vec_params):
        n, shape = p[0], p[1].shape
        grads[n] = p[4].reshape(shape)
        deltas[n], new_m[n], new_v[n] = (r.reshape(shape) for r in res_v[3 * k:3 * k + 3])

    order = ("c_ctx", "w_ada", "b_ada", "norm1_g", "w_in", "q_norm_g", "kv_norm_g", "w_uq", "w_ukv", "conv_w", "conv_b",
             "w_attn_out", "w_conv_out", "w_o", "norm2_g", "w_up", "ffn_conv_w", "ffn_conv_b", "w_down", "final_g")
    return (loss, grad_x[None], *[grads[n] for n in order], *[deltas[n] for n in order],
            *[new_m[n] for n in order], *[new_v[n] for n in order])
```

```python
import functools

import jax
import jax.numpy as jnp
from jax import lax
from jax.experimental import pallas as pl
from jax.experimental.pallas import tpu as pltpu

F32, BF16 = jnp.float32, jnp.bfloat16
MESH = pl.DeviceIdType.MESH

D_MODEL = 1024
N_HEADS = 8
HEAD_PAD = 128
QK_DIM = 96
Q_RANK, KV_RANK = 384, 256
CONV_DIM = 512
D_FF = 2816
GRID_W = 64
ROPE_THETA = 10000.0
EPS = 1e-6
GA0, GC0, CX0, CB0, CC0, KV0, Q0, KR0, P_COLS = 0, 1024, 2048, 2560, 3072, 3584, 3840, 4224, 4352
ROW_TILE = 256
VMEM_LIMIT_BYTES = 48 * 1024 * 1024

ADAM_LR, ADAM_B1, ADAM_B2, ADAM_EPS, ADAM_WD, ADAM_STEP = 0.001, 0.9, 0.999, 1e-08, 0.01, 10

BIG = (("w_in", (1088, 1024)), ("w_uq", (192, 384)), ("w_ukv", (256, 256)), ("w_attn_out", (512, 256)),
       ("w_conv_out", (512, 256)), ("w_o", (256, 1024)), ("w_up", (1024, 1408)), ("w_down", (704, 1024)))

GATHER_LATE = ("w_attn_out", "w_conv_out", "w_o", "w_up", "w_down")

NN = (((1,), (0,)), ((), ()))
NT = (((1,), (1,)), ((), ()))
TN = (((0,), (0,)), ((), ()))


def _cp(sem):
    return pltpu.CompilerParams(dimension_semantics=sem, vmem_limit_bytes=VMEM_LIMIT_BYTES)


PIN_BYTES = 1 << 19


def _in_hbm(arrays):
    return [pltpu.with_memory_space_constraint(a, pltpu.HBM) if a.size * a.dtype.itemsize >= PIN_BYTES else a
            for a in arrays]


def _out(shape, dtype):
    n = 1
    for d in shape:
        n *= d
    big = n * jnp.dtype(dtype).itemsize >= PIN_BYTES
    return pltpu.HBM(shape, dtype) if big else jax.ShapeDtypeStruct(shape, dtype)


def _pick(n, prefs):
    for p in prefs:
        if n % p == 0:
            return p
    return n


def _mm(a, b, mode, M, N, K, *, tm, tn, tk, name, out_dtype=F32, a_spec=None, b_spec=None, o_spec=None,
        out_shape=None, transpose_out=False):
    assert M % tm == 0 and N % tn == 0 and K % tk == 0, (name, M, N, K, tm, tn, tk)
    nk = K // tk
    dims = {"nn": NN, "nt": NT, "tn": TN}[mode]
    if a_spec is None:
        a_spec = (pl.BlockSpec((tk, tm), lambda i, j, k: (k, i)) if mode == "tn"
                  else pl.BlockSpec((tm, tk), lambda i, j, k: (i, k)))
    if b_spec is None:
        b_spec = (pl.BlockSpec((tn, tk), lambda i, j, k: (j, k)) if mode == "nt"
                  else pl.BlockSpec((tk, tn), lambda i, j, k: (k, j)))
    if o_spec is None:
        o_spec = (pl.BlockSpec((tn, tm), lambda i, j, k: (j, i)) if transpose_out
                  else pl.BlockSpec((tm, tn), lambda i, j, k: (i, j)))
    if out_shape is None:
        out_shape = (N, M) if transpose_out else (M, N)

    def emit(o_ref, val):
        o_ref[...] = (val.T if transpose_out else val).astype(o_ref.dtype)

    def body(a_ref, b_ref, o_ref, *scratch):
        part = lax.dot_general(a_ref[...].astype(BF16), b_ref[...].astype(BF16), dims, preferred_element_type=F32)
        if nk == 1:
            emit(o_ref, part)
            return
        acc_ref, = scratch
        k = pl.program_id(2)

        @pl.when(k == 0)
        def _():
            acc_ref[...] = part

        @pl.when((k > 0) & (k < nk - 1))
        def _():
            acc_ref[...] += part

        @pl.when(k == nk - 1)
        def _():
            emit(o_ref, acc_ref[...] + part)

    return pl.pallas_call(
        body, grid=(M // tm, N // tn, nk), in_specs=[a_spec, b_spec], out_specs=o_spec,
        out_shape=_out(out_shape, out_dtype),
        scratch_shapes=[pltpu.VMEM((tm, tn), F32)] if nk > 1 else [],
        compiler_params=_cp(("parallel", "parallel", "arbitrary")), name=name)(*_in_hbm([a, b]))


def _ew(fn, grid, ins, outs, name, scalars=None):
    n_in = len(ins)
    n_sc = 0 if scalars is None else 1

    def store(ref, val, acc, ids):
        if isinstance(val, (list, tuple)):
            for h, v in enumerate(val):
                ref[h] = v.astype(ref.dtype)
            return
        if acc is None:
            ref[...] = val.astype(ref.dtype)
            return

        @pl.when(ids[acc] == 0)
        def _():
            ref[...] = val.astype(ref.dtype)

        @pl.when(ids[acc] > 0)
        def _():
            ref[...] += val.astype(ref.dtype)

    def body(*refs):
        refs = refs[n_sc:]
        ids = tuple(pl.program_id(a) for a in range(len(grid)))
        vals = fn(ids, *[r[...] for r in refs[:n_in]])
        for ref, val, (_, _, _, acc) in zip(refs[n_in:], vals, outs):
            store(ref, val, acc, ids)

    acc_axes = {o[3] for o in outs if o[3] is not None}
    sem = tuple("arbitrary" if a in acc_axes else "parallel" for a in range(len(grid)))
    in_specs, out_specs = [s for _, s in ins], [o[2] for o in outs]
    out_shape = [_out(o[0], o[1]) for o in outs]
    args = _in_hbm([a for a, _ in ins])
    if scalars is None:
        return pl.pallas_call(body, grid=grid, in_specs=in_specs, out_specs=out_specs, out_shape=out_shape,
                              compiler_params=_cp(sem), name=name)(*args)
    spec = pltpu.PrefetchScalarGridSpec(num_scalar_prefetch=1, grid=grid, in_specs=in_specs, out_specs=out_specs)
    return pl.pallas_call(body, grid_spec=spec, out_shape=out_shape, compiler_params=_cp(sem), name=name)(scalars, *args)


def _rows(width, cblk=0, roff=0, tr=ROW_TILE):
    return pl.BlockSpec((tr, width), lambda i: (i + roff, cblk))


def _full(shape):
    nd = len(shape)
    return pl.BlockSpec(shape, lambda *_: (0,) * nd)


def _sigmoid(x):
    return 1.0 / (1.0 + jnp.exp(-x))


def _rms(x):
    return lax.rsqrt(jnp.mean(x * x, axis=-1, keepdims=True) + EPS)


def _rms_bwd(dn, xn, r):
    return r * (dn - xn * jnp.mean(dn * xn, axis=-1, keepdims=True))


def _colsum(x):
    return jnp.sum(x, axis=0, keepdims=True)


def _shift_prev(x):
    rows = lax.broadcasted_iota(jnp.int32, x.shape, 0)
    return jnp.where(rows == 0, 0.0, pltpu.roll(x, 1, 0))


def _shift_next(x):
    rows = lax.broadcasted_iota(jnp.int32, x.shape, 0)
    return jnp.where(rows == x.shape[0] - 1, 0.0, pltpu.roll(x, x.shape[0] - 1, 0))


def _conv(x, w, b):
    return b + _shift_prev(x) * w[0:1] + x * w[1:2] + _shift_next(x) * w[2:3]


def _conv_bwd_x(dy, w):
    return _shift_next(dy) * w[0:1] + dy * w[1:2] + _shift_prev(dy) * w[2:3]


def _conv_bwd_w(dy, x):
    return _colsum(dy * _shift_prev(x)), _colsum(dy * x), _colsum(dy * _shift_next(x))


def _rope(x, cos, sin_lo, sin_hi):
    return x * cos + pltpu.roll(x, HEAD_PAD - 8, 1) * sin_lo + pltpu.roll(x, 8, 1) * sin_hi


ATTN_SCALE = QK_DIM ** -0.5


def _head_keys(kv_ref, kr_ref, cos_ref, slo_ref, shi_ref, kc_ref, vp_ref):
    kv = kv_ref[...]
    lane = lax.broadcasted_iota(jnp.int32, kv.shape, 1)
    kc_ref[...] = jnp.where(lane < 64, kv, _rope(kr_ref[...], cos_ref[...], slo_ref[...], shi_ref[...])).astype(BF16)
    vp_ref[...] = jnp.where(lane >= 64, kv, 0.0).astype(BF16)


def _attn_specs(tq, TT, clamp):
    row = (lambda i: jnp.minimum(i, clamp)) if clamp is not None else (lambda i: i)
    q = pl.BlockSpec((tq, HEAD_PAD), lambda h, i: (i, h))
    lat = pl.BlockSpec((tq, HEAD_PAD), lambda h, i: (row(i), h))
    keys = pl.BlockSpec((TT, HEAD_PAD), lambda h, i: (0, h))
    kr = pl.BlockSpec((TT, HEAD_PAD), lambda h, i: (0, KR0 // HEAD_PAD))
    tab_q = pl.BlockSpec((tq, HEAD_PAD), lambda h, i: (i, 0))
    tab_k = pl.BlockSpec((TT, HEAD_PAD), lambda h, i: (0, 0))
    lse = pl.BlockSpec((None, tq, 1), lambda h, i: (h, row(i), 0))
    return q, lat, keys, kr, tab_q, tab_k, lse


def _attn_fwd(q_raw, kv, pp, tabs, T, TT):
    tq = ROW_TILE
    cos, slo, shi = tabs

    def body(q_ref, kv_ref, kr_ref, cq, lq, hq, ck, lk, hk, o_ref, l_ref, kc, vp):
        @pl.when(pl.program_id(1) == 0)
        def _():
            _head_keys(kv_ref, kr_ref, ck, lk, hk, kc, vp)

        q = _rope(q_ref[...], cq[...], lq[...], hq[...]).astype(BF16)
        s = lax.dot_general(q, kc[...], NT, preferred_element_type=F32) * ATTN_SCALE
        m = jnp.max(s, axis=-1, keepdims=True)
        p = jnp.exp(s - m)
        l = jnp.sum(p, axis=-1, keepdims=True)
        o = lax.dot_general(p.astype(BF16), vp[...], NN, preferred_element_type=F32)
        o_ref[...] = o / l
        l_ref[...] = m + jnp.log(l)

    qs, _, keys, kr, tab_q, tab_k, lse = _attn_specs(tq, TT, None)
    return pl.pallas_call(
        body, grid=(N_HEADS, T // tq), in_specs=[qs, keys, kr, tab_q, tab_q, tab_q, tab_k, tab_k, tab_k],
        out_specs=[qs, lse],
        out_shape=[jax.ShapeDtypeStruct((T, N_HEADS * HEAD_PAD), F32), jax.ShapeDtypeStruct((N_HEADS, T, 1), F32)],
        scratch_shapes=[pltpu.VMEM((TT, HEAD_PAD), BF16), pltpu.VMEM((TT, HEAD_PAD), BF16)],
        compiler_params=_cp(("parallel", "arbitrary")), name="attn_fwd",
    )(*_in_hbm([q_raw, kv, pp, cos, slo, shi, cos, slo, shi]))


def _attn_bwd(q_raw, kv, pp, o, do, lse, tabs, tabs_inv, T, TT):
    tq = ROW_TILE
    nq = T // tq
    cos, slo, shi = tabs
    cos_i, slo_i, shi_i = tabs_inv

    def body(q_ref, kv_ref, kr_ref, cq, lq, hq, ck, lk, hk, iq, ilq, ihq, ik, ilk, ihk, o_ref, do_ref, l_ref,
             dq_ref, dkv_ref, dkr_ref, kc, vp, dk, dv):
        h, i = pl.program_id(0), pl.program_id(1)

        @pl.when(i == 0)
        def _():
            _head_keys(kv_ref, kr_ref, ck, lk, hk, kc, vp)
            dk[...] = jnp.zeros_like(dk)
            dv[...] = jnp.zeros_like(dv)

        @pl.when(i < nq)
        def _():
            q = _rope(q_ref[...], cq[...], lq[...], hq[...]).astype(BF16)
            k, v, d_o = kc[...], vp[...], do_ref[...]
            s = lax.dot_general(q, k, NT, preferred_element_type=F32) * ATTN_SCALE
            p = jnp.exp(s - l_ref[...])
            dob = d_o.astype(BF16)
            dp = lax.dot_general(dob, v, NT, preferred_element_type=F32)
            dd = jnp.sum(d_o * o_ref[...], axis=-1, keepdims=True)
            ds = (p * (dp - dd) * ATTN_SCALE).astype(BF16)
            dq = lax.dot_general(ds, k, NN, preferred_element_type=F32)
            dq_ref[...] = _rope(dq, iq[...], ilq[...], ihq[...]).astype(dq_ref.dtype)
            dk[...] += lax.dot_general(ds, q, TN, preferred_element_type=F32)
            dv[...] += lax.dot_general(p.astype(BF16), dob, TN, preferred_element_type=F32)

        @pl.when(i == nq)
        def _():
            dq_ref[...] = jnp.zeros_like(dq_ref)
            dkh = dk[...]
            lane = lax.broadcasted_iota(jnp.int32, dkh.shape, 1)
            dkv_ref[...] = jnp.where(lane < 64, dkh, dv[...]).astype(dkv_ref.dtype)
            rot = _rope(jnp.where((lane >= 64) & (lane < 96), dkh, 0.0), ik[...], ilk[...], ihk[...])

            @pl.when(h == 0)
            def _():
                dkr_ref[...] = rot

            @pl.when(h > 0)
            def _():
                dkr_ref[...] += rot

    qs, lat, keys, kr, tab_q, tab_k, lse_spec = _attn_specs(tq, TT, nq - 1)
    wide = jax.ShapeDtypeStruct((TT, N_HEADS * HEAD_PAD), BF16)
    return pl.pallas_call(
        body, grid=(N_HEADS, TT // tq),
        in_specs=[qs, keys, kr] + [tab_q] * 3 + [tab_k] * 3 + [tab_q] * 3 + [tab_k] * 3 + [lat, lat, lse_spec],
        out_specs=[qs, keys, pl.BlockSpec((TT, HEAD_PAD), lambda h, i: (0, 0))],
        out_shape=[wide, wide, jax.ShapeDtypeStruct((TT, HEAD_PAD), F32)],
        scratch_shapes=[pltpu.VMEM((TT, HEAD_PAD), BF16), pltpu.VMEM((TT, HEAD_PAD), BF16),
                        pltpu.VMEM((TT, HEAD_PAD), F32), pltpu.VMEM((TT, HEAD_PAD), F32)],
        compiler_params=_cp(("arbitrary", "arbitrary")), name="attn_bwd",
    )(*_in_hbm([q_raw, kv, pp, cos, slo, shi, cos, slo, shi, cos_i, slo_i, shi_i, cos_i, slo_i, shi_i, o, do, lse]))


def _allgather8(x, name, in_vmem):
    m_per, n = x.shape

    def body(x_ref, out_ref, token, send_sems, recv_sems, local_sem):
        token[...] = jnp.zeros_like(token)
        mx, my, mc = lax.axis_index("x"), lax.axis_index("y"), lax.axis_index("c")
        me, sibling = (mx, my, mc), (mx, my, 1 - mc)
        chips = [(1 - mx, my), (mx, 1 - my), (1 - mx, 1 - my)]

        def rows(px, py, pc):
            return out_ref.at[pl.ds((4 * px + 2 * py + pc) * m_per, m_per), :]

        def copy(k, block, to, src=None):
            return pltpu.make_async_remote_copy(
                src_ref=rows(*block) if src is None else src, dst_ref=rows(*block),
                send_sem=send_sems.at[k], recv_sem=recv_sems.at[k], device_id=to, device_id_type=MESH)

        mine = pltpu.make_async_copy(x_ref, rows(*me), local_sem)
        mine.start()
        first = [copy(0, me, sibling, src=x_ref)]
        first += [copy(1 + j, me, (*chip, mc), src=x_ref) for j, chip in enumerate(chips)]
        for cp in first:
            cp.start()
        passed = [copy(4 + j, (*chip, mc), sibling) for j, chip in enumerate(chips)]
        for j, chip in enumerate(chips):
            copy(1 + j, (*chip, mc), me).wait_recv()
            passed[j].start()
        copy(0, sibling, me).wait_recv()
        for j, chip in enumerate(chips):
            copy(4 + j, (*chip, 1 - mc), me).wait_recv()
        for cp in first + passed:
            cp.wait_send()
        mine.wait()

    space = pltpu.VMEM if in_vmem else pl.ANY
    return pl.pallas_call(
        body, out_shape=[jax.ShapeDtypeStruct((8 * m_per, n), x.dtype), jax.ShapeDtypeStruct((8, 128), F32)],
        in_specs=[pl.BlockSpec(memory_space=space)],
        out_specs=[pl.BlockSpec(memory_space=space), pl.BlockSpec(memory_space=pltpu.VMEM)],
        scratch_shapes=[pltpu.SemaphoreType.DMA((7,)), pltpu.SemaphoreType.DMA((7,)), pltpu.SemaphoreType.DMA],
        name=name)(x)


def _hbm_specs(n):
    return [pl.BlockSpec(memory_space=pl.ANY)] * n


def _gather_weights(shards):
    n = len(shards)
    halves = [s.shape[0] // 2 for s in shards]

    def body(*refs):
        ins, outs = refs[:n], refs[n:2 * n]
        token, send_sems, recv_sems = refs[2 * n:]
        token[...] = jnp.zeros_like(token)
        mx, my, mc = lax.axis_index("x"), lax.axis_index("y"), lax.axis_index("c")
        j_me = 2 * mx + my
        chips = [(1 - mx, my), (mx, 1 - my), (1 - mx, 1 - my)]

        def half(w, chip_idx, hc):
            return outs[w].at[chip_idx, pl.ds(hc * halves[w], halves[w]), :]

        def copy(w, k, src, dst, to):
            return pltpu.make_async_remote_copy(src_ref=src, dst_ref=dst, send_sem=send_sems.at[w, k],
                                                recv_sem=recv_sems.at[w, k], device_id=to, device_id_type=MESH)

        sends = []
        for w in range(n):
            cp = copy(w, 6, ins[w], outs[w].at[j_me], (mx, my, 1 - mc))
            cp.start()
            sends.append(cp)
        for k, (px, py) in enumerate(chips):
            for w in range(n):
                cp = copy(w, k, ins[w].at[pl.ds(mc * halves[w], halves[w]), :], half(w, j_me, mc), (px, py, mc))
                cp.start()
                sends.append(cp)
        for k, (px, py) in enumerate(chips):
            for w in range(n):
                got = half(w, 2 * px + py, mc)
                copy(w, k, got, got, (px, py, mc)).wait_recv()
                cp = copy(w, 3 + k, got, got, (mx, my, 1 - mc))
                cp.start()
                sends.append(cp)
        for k, (px, py) in enumerate(chips):
            for w in range(n):
                got = half(w, 2 * px + py, 1 - mc)
                copy(w, 3 + k, got, got, (mx, my, 1 - mc)).wait_recv()
        for w in range(n):
            own = outs[w].at[j_me]
            copy(w, 6, own, own, (mx, my, 1 - mc)).wait_recv()
        for cp in sends:
            cp.wait_send()

    res = pl.pallas_call(
        body, out_shape=[jax.ShapeDtypeStruct((4,) + s.shape, s.dtype) for s in shards]
        + [jax.ShapeDtypeStruct((8, 128), F32)],
        in_specs=_hbm_specs(n), out_specs=_hbm_specs(n) + [pl.BlockSpec(memory_space=pltpu.VMEM)],
        scratch_shapes=[pltpu.SemaphoreType.DMA((n, 7)), pltpu.SemaphoreType.DMA((n, 7))],
        name="gather_weights")(*shards)
    return list(res[:n]), res[n]


def _rs_pair(gs, name):
    n = len(gs)
    halves = [g.shape[1] // 2 for g in gs]

    def body(*refs):
        ins, lands = refs[:n], refs[n:2 * n]
        send_sems, recv_sems = refs[2 * n:]
        mx, my, mc = lax.axis_index("x"), lax.axis_index("y"), lax.axis_index("c")
        copies = []
        for w in range(n):
            h = halves[w]
            cp = pltpu.make_async_remote_copy(
                src_ref=ins[w].at[:, pl.ds((1 - mc) * h, h), :], dst_ref=lands[w], send_sem=send_sems.at[w],
                recv_sem=recv_sems.at[w], device_id=(mx, my, 1 - mc), device_id_type=MESH)
            cp.start()
            copies.append(cp)
        for cp in copies:
            cp.wait()

    return pl.pallas_call(
        body, out_shape=[jax.ShapeDtypeStruct((4, h, g.shape[2]), g.dtype) for g, h in zip(gs, halves)],
        in_specs=_hbm_specs(n), out_specs=_hbm_specs(n),
        scratch_shapes=[pltpu.SemaphoreType.DMA((n,)), pltpu.SemaphoreType.DMA((n,))], name=name)(*gs)


def _rs_chips(parts):
    n = len(parts)

    def body(*refs):
        ins, lands = refs[:n], refs[n:2 * n]
        send_sems, recv_sems = refs[2 * n:]
        mx, my, mc = lax.axis_index("x"), lax.axis_index("y"), lax.axis_index("c")
        copies = []
        for k, (px, py) in enumerate([(1 - mx, my), (mx, 1 - my), (1 - mx, 1 - my)]):
            for w in range(n):
                cp = pltpu.make_async_remote_copy(
                    src_ref=ins[w].at[2 * px + py], dst_ref=lands[w].at[k], send_sem=send_sems.at[w, k],
                    recv_sem=recv_sems.at[w, k], device_id=(px, py, mc), device_id_type=MESH)
                cp.start()
                copies.append(cp)
        for cp in copies:
            cp.wait()

    return list(pl.pallas_call(
        body, out_shape=[jax.ShapeDtypeStruct((3,) + p.shape[1:], p.dtype) for p in parts],
        in_specs=_hbm_specs(n), out_specs=_hbm_specs(n),
        scratch_shapes=[pltpu.SemaphoreType.DMA((n, 3)), pltpu.SemaphoreType.DMA((n, 3))], name="rs_chips")(*parts))


def _rs_pair_back(gs):
    n = len(gs)

    def body(*refs):
        outs = refs[n:2 * n]
        send_sems, recv_sems = refs[2 * n:]
        mx, my, mc = lax.axis_index("x"), lax.axis_index("y"), lax.axis_index("c")
        copies = []
        for w in range(n):
            h = gs[w].shape[0] // 2
            mine = outs[w].at[pl.ds(mc * h, h), :]
            cp = pltpu.make_async_remote_copy(src_ref=mine, dst_ref=mine, send_sem=send_sems.at[w],
                                              recv_sem=recv_sems.at[w], device_id=(mx, my, 1 - mc), device_id_type=MESH)
            cp.start()
            copies.append(cp)
        for cp in copies:
            cp.wait()

    return pl.pallas_call(
        body, out_shape=[jax.ShapeDtypeStruct(g.shape, g.dtype) for g in gs],
        in_specs=_hbm_specs(n), out_specs=_hbm_specs(n), input_output_aliases={w: w for w in range(n)},
        scratch_shapes=[pltpu.SemaphoreType.DMA((n,)), pltpu.SemaphoreType.DMA((n,))], name="rs_pair_back")(*gs)


_HBM = pl.BlockSpec(memory_space=pltpu.HBM)
_SEM = pl.BlockSpec(memory_space=pltpu.SEMAPHORE)
_EFFECT = pltpu.SideEffectType.DATAFLOW_SIDE_EFFECTING


def _ici_copies(kind, srcs, lands, send_sems, recv_sems):
    n = len(srcs)
    mx, my, mc = lax.axis_index("x"), lax.axis_index("y"), lax.axis_index("c")
    j_me = 2 * mx + my
    copies = []
    for k, (px, py) in enumerate([(1 - mx, my), (mx, 1 - my), (1 - mx, 1 - my)]):
        for w in range(n):
            if kind == "gather":
                h = srcs[w].shape[0] // 2
                src, dst = srcs[w].at[pl.ds(mc * h, h), :], lands[w].at[j_me, pl.ds(mc * h, h), :]
            else:
                src, dst = srcs[w].at[2 * px + py], lands[w].at[k]
            copies.append(pltpu.make_async_remote_copy(
                src_ref=src, dst_ref=dst, send_sem=send_sems.at[3 * w + k], recv_sem=recv_sems.at[3 * w + k],
                device_id=(px, py, mc), device_id_type=MESH))
    return copies


def _ici_start(kind, srcs, land_shapes, carry, name):
    n = len(srcs)

    def body(*refs):
        ins, lands = refs[:n], refs[n:2 * n]
        send_sems, recv_sems = refs[2 * n + 1], refs[2 * n + 2]
        for cp in _ici_copies(kind, ins, lands, send_sems, recv_sems):
            cp.start()

    hbm = lambda a: pltpu.with_memory_space_constraint(a, pltpu.HBM)
    lands = [lax.empty(s, srcs[0].dtype) for s in land_shapes]
    args = [hbm(a) for a in list(srcs) + lands + [carry]]
    out_shape = ([pltpu.SemaphoreType.DMA((3 * n,)), pltpu.SemaphoreType.DMA((3 * n,))]
                 + [pltpu.HBM(a.shape, a.dtype) for a in args])
    res = pl.pallas_call(
        body, name=name, out_shape=out_shape, in_specs=[_HBM] * len(args), out_specs=[_SEM, _SEM] + [_HBM] * len(args),
        input_output_aliases={i: 2 + i for i in range(len(args))},
        compiler_params=pltpu.CompilerParams(has_side_effects=_EFFECT))(*args)
    return res[0], res[1], list(res[2:2 + n]), list(res[2 + n:2 + 2 * n]), res[2 + 2 * n]


def _ici_wait(kind, send_sems, recv_sems, srcs, lands, after, name):
    n = len(srcs)

    def body(*refs):
        ins, zones = refs[:n], refs[n:2 * n]
        for cp in _ici_copies(kind, ins, zones, refs[2 * n], refs[2 * n + 1]):
            cp.wait_send()
            cp.wait_recv()

    args = list(srcs) + list(lands)
    res = pl.pallas_call(
        body, name=name, out_shape=[pltpu.HBM(a.shape, a.dtype) for a in args],
        in_specs=[_HBM] * len(args) + [_SEM, _SEM, pl.BlockSpec(memory_space=pl.ANY)], out_specs=[_HBM] * len(args),
        input_output_aliases={i: i for i in range(len(args))},
        compiler_params=pltpu.CompilerParams(has_side_effects=_EFFECT))(*args, send_sems, recv_sems, after)
    return list(res[:n]), list(res[n:])


def _gather_finish(shards, lands):
    n = len(shards)

    def body(*refs):
        own, outs = refs[:n], refs[2 * n:3 * n]
        send_sems, recv_sems = refs[3 * n:]
        mx, my, mc = lax.axis_index("x"), lax.axis_index("y"), lax.axis_index("c")
        j_me = 2 * mx + my
        sibling = (mx, my, 1 - mc)
        copies = []

        def push(w, k, src, dst):
            cp = pltpu.make_async_remote_copy(src_ref=src, dst_ref=dst, send_sem=send_sems.at[w, k],
                                              recv_sem=recv_sems.at[w, k], device_id=sibling, device_id_type=MESH)
            cp.start()
            copies.append(cp)

        for w in range(n):
            h = shards[w].shape[0] // 2
            push(w, 3, own[w], outs[w].at[j_me])
            for k, (px, py) in enumerate([(1 - mx, my), (mx, 1 - my), (1 - mx, 1 - my)]):
                got = outs[w].at[2 * px + py, pl.ds(mc * h, h), :]
                push(w, k, got, got)
        for cp in copies:
            cp.wait()

    return pl.pallas_call(
        body, out_shape=[jax.ShapeDtypeStruct(l.shape, l.dtype) for l in lands],
        in_specs=_hbm_specs(2 * n), out_specs=_hbm_specs(n), input_output_aliases={n + w: w for w in range(n)},
        scratch_shapes=[pltpu.SemaphoreType.DMA((n, 4)), pltpu.SemaphoreType.DMA((n, 4))], name="gather_finish",
    )(*shards, *lands)


def _tile_rows(h, c, itemsize, mult):
    best = h
    for t in range(mult, h + 1, mult):
        if h % t == 0 and t * c * itemsize <= (1 << 21):
            best = t
    return best


def _add_pair(g, land, place, name):
    _, h, c = land.shape
    t = _tile_rows(h, c, 2, 16)
    nb = h // t
    return _ew(lambda ids, u, v: (u.astype(F32) + v.astype(F32),), (4, nb),
               [(g, pl.BlockSpec((None, t, c), lambda j, i, s: (j, s[1] * nb + i, 0))),
                (land, pl.BlockSpec((None, t, c), lambda j, i, s: (j, i, 0)))],
               [(land.shape, BF16, pl.BlockSpec((None, t, c), lambda j, i, s: (j, i, 0)), None)], name, scalars=place)[0]


def _add_chips(own, land, place, name):
    _, h, c = land.shape
    t = _tile_rows(h, c, 4, 16)
    nb = h // t

    def fn(ids, a, b):
        return (((a.astype(F32) + b[0].astype(F32)) + b[1].astype(F32)) + b[2].astype(F32),)

    return _ew(fn, (nb,), [(own, pl.BlockSpec((None, t, c), lambda i, s: (s[0], i, 0))),
                           (land, pl.BlockSpec((3, t, c), lambda i, s: (0, i, 0)))],
               [((2 * h, c), F32, pl.BlockSpec((t, c), lambda i, s: (s[1] * nb + i, 0)), None)], name, scalars=place)[0]


W_IN_SEGMENTS = ((0, 256, KV0), (256, 288, KR0 + 64), (288, 672, Q0), (672, 1184, CX0), (1184, 1696, CB0),
                 (1696, 2208, CC0), (2208, 3232, GA0), (3232, 4256, GC0))
W_IN_SHARD = 1064


W_IN_SHARD_PAD = 1088


def _w_in_t_p_from_shards(s):
    pieces = []
    for o0, o1, p0 in sorted(W_IN_SEGMENTS, key=lambda t: t[2]):
        if p0 == KR0 + 64:
            pieces.append(jnp.zeros((64, s.shape[2]), s.dtype))
        for j in range(4):
            lo, hi = max(o0, j * W_IN_SHARD), min(o1, (j + 1) * W_IN_SHARD)
            if lo < hi:
                pieces.append(s[j, lo - j * W_IN_SHARD:hi - j * W_IN_SHARD])
    pieces.append(jnp.zeros((32, s.shape[2]), s.dtype))
    return jnp.concatenate(pieces, axis=0)


def _w_in_t_shards_from_p(g):
    shards = []
    for j in range(4):
        pieces = []
        for o0, o1, p0 in W_IN_SEGMENTS:
            lo, hi = max(o0, j * W_IN_SHARD), min(o1, (j + 1) * W_IN_SHARD)
            if lo < hi:
                pieces.append(g[p0 + lo - o0:p0 + hi - o0])
        pieces.append(jnp.zeros((W_IN_SHARD_PAD - W_IN_SHARD, g.shape[1]), g.dtype))
        shards.append(jnp.concatenate(pieces, axis=0))
    return jnp.stack(shards, axis=0)


def _cols_from_shards(s):
    return jnp.transpose(s, (1, 0, 2)).reshape(s.shape[1], -1)


def _rope_tables(T, TT, inverse):
    rows = T // GRID_W
    row = jnp.repeat(jnp.arange(rows), GRID_W).astype(F32)
    col = jnp.tile(jnp.arange(GRID_W), rows).astype(F32)
    inv = ROPE_THETA ** (-jnp.arange(0, 16, 2, dtype=F32) / 16)
    ang = jnp.concatenate([row[:, None] * inv, col[:, None] * inv], axis=-1)
    cos, sin = jnp.cos(ang), jnp.sin(ang)
    lane = jnp.arange(32)
    src = (lane // 16) * 8 + lane % 8
    lo = ((lane % 16) // 8 == 0).astype(F32)
    sgn = -1.0 if inverse else 1.0
    cos32 = cos[:, src]
    sin_lo32 = -sgn * sin[:, src] * lo
    sin_hi32 = sgn * sin[:, src] * (1.0 - lo)

    def widen(t32, fill):
        t = jnp.concatenate([jnp.full((T, 64), fill, F32), t32, jnp.full((T, 32), fill, F32)], axis=1)
        return jnp.concatenate([t, jnp.full((TT - T, HEAD_PAD), fill, F32)], axis=0)

    return widen(cos32, 1.0), widen(sin_lo32, 0.0), widen(sin_hi32, 0.0)


def _local_step(xx, tgt, mod_lat, mod_ctx, W, late_weights, early_grads):
    TT = xx.shape[0]
    T = tgt.shape[0]
    n_lat, n_all = T // ROW_TILE, TT // ROW_TILE
    sh1, sc1, g1, sh2, sc2, g2 = [mod_lat[:, k * D_MODEL:(k + 1) * D_MODEL] for k in range(6)]
    csh1, csc1 = mod_ctx[:, :D_MODEL], mod_ctx[:, D_MODEL:2 * D_MODEL]
    vec = lambda n: _full((1, n))
    row_out = lambda n, dt, rows=T: ((rows, n), dt, _rows(n), None)
    acc_out = lambda n: ((1, n), F32, _full((1, n)), 0)

    def f_norm1(ids, x, g, a_sh, a_sc, b_sh, b_sc):
        ctx = ids[0] >= n_lat
        sh, sc = jnp.where(ctx, b_sh, a_sh), jnp.where(ctx, b_sc, a_sc)
        return ((x * _rms(x) * g) * (1.0 + sc) + sh,)

    (hh,) = _ew(f_norm1, (n_all,), [(xx, _rows(D_MODEL)), (W["norm1_g"], vec(D_MODEL)), (sh1, vec(D_MODEL)),
                                   (sc1, vec(D_MODEL)), (csh1, vec(D_MODEL)), (csc1, vec(D_MODEL))],
                [row_out(D_MODEL, BF16, TT)], "norm1_fwd")
    tm_all = _pick(TT, (768, 256))
    pp = _mm(hh, W["w_in_t"], "nt", TT, P_COLS, D_MODEL, tm=tm_all, tn=2176, tk=D_MODEL, name="w_in_fwd")

    def f_lowrank(ids, ckv, cq, gkv, gq):
        return ckv * _rms(ckv) * gkv, cq * _rms(cq) * gq

    nkv, nq = _ew(f_lowrank, (n_all,), [(pp, _rows(KV_RANK, KV0 // KV_RANK)), (pp, _rows(Q_RANK, Q0 // Q_RANK)),
                                       (W["kv_norm_g"], vec(KV_RANK)), (W["q_norm_g"], vec(Q_RANK))],
                  [row_out(KV_RANK, BF16, TT), row_out(Q_RANK, BF16, TT)], "lowrank_norm_fwd")
    kv = _mm(nkv, W["w_ukv"], "nn", TT, 1024, KV_RANK, tm=tm_all, tn=256, tk=KV_RANK, name="w_ukv_fwd",
             b_spec=pl.BlockSpec((None, KV_RANK, 256), lambda i, j, k: (j, k, 0)))
    q_raw = _mm(nq, W["w_uq_t"], "nt", TT, 1024, Q_RANK, tm=tm_all, tn=1024, tk=Q_RANK, name="w_uq_fwd")

    tabs = _rope_tables(T, TT, inverse=False)
    tabs_inv = _rope_tables(T, TT, inverse=True)
    o_pad, lse = _attn_fwd(q_raw, kv, pp, tabs, T, TT)
    W = dict(W, **late_weights(o_pad))
    tm_lat = _pick(T, (1024, 512, 256))
    ya = _mm(o_pad, W["w_attn_out"], "nn", T, D_MODEL, 1024, tm=tm_lat, tn=D_MODEL, tk=1024, name="w_attn_out_fwd")

    tc = 256
    colT = lambda blk0: pl.BlockSpec((T, tc), lambda j: (0, blk0 + j))

    def f_conv(ids, xin, cb, cc, w, b):
        return (cb * _conv(cc * xin, w, b),)

    (e,) = _ew(f_conv, (CONV_DIM // tc,),
               [(pp, colT(CX0 // tc)), (pp, colT(CB0 // tc)), (pp, colT(CC0 // tc)),
                (W["conv_w"], pl.BlockSpec((3, tc), lambda j: (0, j))), (W["conv_b"], pl.BlockSpec((1, tc), lambda j: (0, j)))],
               [((T, CONV_DIM), BF16, colT(0), None)], "conv_fwd")
    yc = _mm(e, W["w_conv_out"], "nn", T, D_MODEL, CONV_DIM, tm=tm_lat, tn=256, tk=CONV_DIM, name="w_conv_out_fwd",
             b_spec=pl.BlockSpec((None, CONV_DIM, 256), lambda i, j, k: (j, k, 0)))

    def f_merge(ids, ga, gc, a, c):
        return (_sigmoid(ga) * a + _sigmoid(gc) * c,)

    (mrg,) = _ew(f_merge, (n_lat,), [(pp, _rows(D_MODEL, 0)), (pp, _rows(D_MODEL, 1)), (ya, _rows(D_MODEL)),
                                    (yc, _rows(D_MODEL))], [row_out(D_MODEL, BF16)], "merge_fwd")
    mo = _mm(mrg, W["w_o"], "nn", T, D_MODEL, D_MODEL, tm=tm_lat, tn=D_MODEL, tk=D_MODEL, name="w_o_fwd")

    def f_norm2(ids, x, m, gate, g, sh, sc):
        x1 = x + gate * m
        return x1, (x1 * _rms(x1) * g) * (1.0 + sc) + sh

    x1, h2 = _ew(f_norm2, (n_lat,), [(xx, _rows(D_MODEL)), (mo, _rows(D_MODEL)), (g1, vec(D_MODEL)),
                                    (W["norm2_g"], vec(D_MODEL)), (sh2, vec(D_MODEL)), (sc2, vec(D_MODEL))],
                 [row_out(D_MODEL, F32), row_out(D_MODEL, BF16)], "norm2_fwd")
    up = _mm(h2, W["w_up"], "nn", T, 2 * D_FF, D_MODEL, tm=tm_lat, tn=1408, tk=D_MODEL, name="w_up_fwd",
             b_spec=pl.BlockSpec((None, D_MODEL, 1408), lambda i, j, k: (j, k, 0)))

    n_ff = D_FF // tc
    ffw = lambda off, n=3: pl.BlockSpec((n, tc), lambda j: (0, j + off))

    def f_ffn(ids, ug, uv, wg, wv, bg, bv):
        gate, val = _conv(ug, wg, bg), _conv(uv, wv, bv)
        return (gate * _sigmoid(gate) * val,)

    (act,) = _ew(f_ffn, (n_ff,), [(up, colT(0)), (up, colT(n_ff)), (W["ffn_conv_w"], ffw(0)), (W["ffn_conv_w"], ffw(n_ff)),
                                 (W["ffn_conv_b"], ffw(0, 1)), (W["ffn_conv_b"], ffw(n_ff, 1))],
                 [((T, D_FF), BF16, colT(0), None)], "ffn_act_fwd")
    f = _mm(act, W["w_down"], "nn", T, D_MODEL, D_FF, tm=tm_lat, tn=D_MODEL, tk=D_FF, name="w_down_fwd")

    def f_head(ids, x1_, f_, gate, gf, t):
        x2 = x1_ + gate * f_
        r = _rms(x2)
        xn = x2 * r
        err = xn * gf - t
        loss = 0.5 * jnp.sum(jnp.mean(err * err, axis=-1, keepdims=True))
        dy = err * (1.0 / D_MODEL)
        dx2 = _rms_bwd(dy * gf, xn, r)
        return dx2, dx2 * gate, _colsum(dy * xn), _colsum(dx2 * f_), jnp.full((1, 128), loss, F32)

    dx2, df, dg_f, dg2, loss = _ew(
        f_head, (n_lat,), [(x1, _rows(D_MODEL)), (f, _rows(D_MODEL)), (g2, vec(D_MODEL)), (W["final_g"], vec(D_MODEL)),
                           (tgt, _rows(D_MODEL))],
        [row_out(D_MODEL, F32), row_out(D_MODEL, BF16), acc_out(D_MODEL), acc_out(D_MODEL), acc_out(128)], "loss_head")

    d_w_down = _mm(act, df, "tn", D_FF, D_MODEL, T, tm=1408, tn=D_MODEL, tk=T, name="w_down_dw",
                   out_dtype=BF16).reshape(4, D_FF // 4, D_MODEL)
    da = _mm(df, W["w_down"], "nt", T, D_FF, D_MODEL, tm=tm_lat, tn=1408, tk=D_MODEL, name="w_down_dx")

    tcb = 128
    n_fb = D_FF // tcb
    colb = lambda blk0: pl.BlockSpec((T, tcb), lambda j: (0, blk0 + j))
    ffwb = lambda off, n=3: pl.BlockSpec((n, tcb), lambda j: (0, j + off))
    cvec = ((1, D_FF), F32, pl.BlockSpec((1, tcb), lambda j: (0, j)), None)

    def f_ffn_bwd(ids, ug, uv, d_act, wg, wv, bg, bv):
        gate, val = _conv(ug, wg, bg), _conv(uv, wv, bv)
        s = _sigmoid(gate)
        d_gate = d_act * val * s * (1.0 + gate * (1.0 - s))
        d_val = d_act * gate * s
        wg0, wg1, wg2 = _conv_bwd_w(d_gate, ug)
        wv0, wv1, wv2 = _conv_bwd_w(d_val, uv)
        d_up = [_conv_bwd_x(d_gate, wg), _conv_bwd_x(d_val, wv)]
        return d_up, [_colsum(d_gate), _colsum(d_val), wg0, wg1, wg2, wv0, wv1, wv2]

    d_up3, ffn_stats = _ew(
        f_ffn_bwd, (n_fb,),
        [(up, colb(0)), (up, colb(n_fb)), (da, colb(0)), (W["ffn_conv_w"], ffwb(0)), (W["ffn_conv_w"], ffwb(n_fb)),
         (W["ffn_conv_b"], ffwb(0, 1)), (W["ffn_conv_b"], ffwb(n_fb, 1))],
        [((2, T, D_FF), BF16, pl.BlockSpec((2, T, tcb), lambda j: (0, 0, j)), None),
         ((n_fb, 8, 1, tcb), F32, pl.BlockSpec((None, 8, 1, tcb), lambda j: (j, 0, 0, 0)), None)], "ffn_act_bwd")
    stat = lambda s: ffn_stats[:, s, 0, :].reshape(1, D_FF)
    d_ffn_conv_b = jnp.concatenate([stat(0), stat(1)], axis=1)
    d_ffn_conv_w = jnp.concatenate([jnp.concatenate([stat(2), stat(3), stat(4)], axis=0),
                                    jnp.concatenate([stat(5), stat(6), stat(7)], axis=0)], axis=1)

    tk_t = T
    d_w_up = _mm(h2, d_up3, "tn", D_MODEL, 2 * D_FF, T, tm=D_MODEL, tn=1408, tk=tk_t, name="w_up_dw", out_dtype=BF16,
                 b_spec=pl.BlockSpec((None, tk_t, 1408), lambda i, j, k: (j // 2, k, j % 2)),
                 o_spec=pl.BlockSpec((None, D_MODEL, 1408), lambda i, j, k: (j, i, 0)), out_shape=(4, D_MODEL, 1408))
    dh2 = _mm(d_up3, W["w_up"], "nt", T, D_MODEL, 2 * D_FF, tm=tm_lat, tn=D_MODEL, tk=1408, name="w_up_dx",
              a_spec=pl.BlockSpec((None, tm_lat, 1408), lambda i, j, k: (k // 2, i, k % 2)),
              b_spec=pl.BlockSpec((None, D_MODEL, 1408), lambda i, j, k: (k, j, 0)))

    def f_norm2_bwd(ids, dx2_, dh, x1_, m, g, sc, gate):
        r = _rms(x1_)
        xn = x1_ * r
        dx1 = dx2_ + _rms_bwd(dh * g * (1.0 + sc), xn, r)
        return dx1, dx1 * gate, _colsum(dh), _colsum(dh * xn * g), _colsum(dh * xn * (1.0 + sc)), _colsum(dx1 * m)

    dx1, dmo, dsh2, dsc2, dg_n2, dg1 = _ew(
        f_norm2_bwd, (n_lat,), [(dx2, _rows(D_MODEL)), (dh2, _rows(D_MODEL)), (x1, _rows(D_MODEL)), (mo, _rows(D_MODEL)),
                                (W["norm2_g"], vec(D_MODEL)), (sc2, vec(D_MODEL)), (g1, vec(D_MODEL))],
        [row_out(D_MODEL, F32), row_out(D_MODEL, BF16)] + [acc_out(D_MODEL)] * 4, "norm2_bwd")
    d_w_o = _mm(mrg, dmo, "tn", D_MODEL, D_MODEL, T, tm=D_MODEL, tn=D_MODEL, tk=tk_t, name="w_o_dw",
                out_dtype=BF16).reshape(4, D_MODEL // 4, D_MODEL)
    dmrg = _mm(dmo, W["w_o"], "nt", T, D_MODEL, D_MODEL, tm=tm_lat, tn=D_MODEL, tk=D_MODEL, name="w_o_dx")
    dmrg = early_grads("late", {"w_o": d_w_o, "w_up": d_w_up, "w_down": d_w_down}, dmrg)

    def f_merge_bwd(ids, dm, ga, gc, a, c):
        sa, sc_ = _sigmoid(ga), _sigmoid(gc)
        return dm * sa, dm * sc_, dm * a * sa * (1.0 - sa), dm * c * sc_ * (1.0 - sc_)

    dya, dyc, dp_ga, dp_gc = _ew(
        f_merge_bwd, (n_lat,), [(dmrg, _rows(D_MODEL)), (pp, _rows(D_MODEL, 0)), (pp, _rows(D_MODEL, 1)),
                                (ya, _rows(D_MODEL)), (yc, _rows(D_MODEL))], [row_out(D_MODEL, BF16)] * 4, "merge_bwd")

    d_w_ao_p = _mm(o_pad, dya, "tn", 1024, D_MODEL, T, tm=1024, tn=D_MODEL, tk=tk_t, name="w_attn_out_dw", out_dtype=BF16)
    do_pad = _mm(dya, W["w_attn_out"], "nt", T, 1024, D_MODEL, tm=tm_lat, tn=1024, tk=D_MODEL, name="w_attn_out_dx")
    d_w_co = _mm(e, dyc, "tn", CONV_DIM, D_MODEL, T, tm=CONV_DIM, tn=256, tk=tk_t, name="w_conv_out_dw", out_dtype=BF16,
                 o_spec=pl.BlockSpec((None, CONV_DIM, 256), lambda i, j, k: (j, i, 0)), out_shape=(4, CONV_DIM, 256))
    de = _mm(dyc, W["w_conv_out"], "nt", T, CONV_DIM, D_MODEL, tm=tm_lat, tn=CONV_DIM, tk=256, name="w_conv_out_dx",
             b_spec=pl.BlockSpec((None, CONV_DIM, 256), lambda i, j, k: (k, j, 0)))

    def f_conv_bwd(ids, xin, cb, cc, d_e, w, b):
        z = cc * xin
        cz = _conv(z, w, b)
        dcz = d_e * cb
        w0, w1, w2 = _conv_bwd_w(dcz, z)
        dz = _conv_bwd_x(dcz, w)
        return dz * cc, d_e * cz, dz * xin, _colsum(dcz), w0, w1, w2

    cvec_c = ((1, CONV_DIM), F32, pl.BlockSpec((1, tc), lambda j: (0, j)), None)
    conv_b = _ew(f_conv_bwd, (CONV_DIM // tc,),
                 [(pp, colT(CX0 // tc)), (pp, colT(CB0 // tc)), (pp, colT(CC0 // tc)), (de, colT(0)),
                  (W["conv_w"], pl.BlockSpec((3, tc), lambda j: (0, j))), (W["conv_b"], pl.BlockSpec((1, tc), lambda j: (0, j)))],
                 [((T, CONV_DIM), BF16, colT(0), None)] * 3 + [cvec_c] * 4, "conv_bwd")
    dp_cx, dp_cb, dp_cc, d_conv_b = conv_b[:4]
    d_conv_w = jnp.concatenate(conv_b[4:7], axis=0)

    dq_raw, dkv, dp_kr = _attn_bwd(q_raw, kv, pp, o_pad, do_pad, lse, tabs, tabs_inv, T, TT)

    tk_a = TT
    d_w_uq_t = _mm(nq, dq_raw, "tn", Q_RANK, 1024, TT, tm=Q_RANK, tn=1024, tk=tk_a, name="w_uq_dw", transpose_out=True)
    dnq = _mm(dq_raw, W["w_uq_t"], "nn", TT, Q_RANK, 1024, tm=tm_all, tn=Q_RANK, tk=1024, name="w_uq_dx")
    d_w_ukv = _mm(nkv, dkv, "tn", KV_RANK, 1024, TT, tm=KV_RANK, tn=256, tk=tk_a, name="w_ukv_dw", out_dtype=BF16,
                  o_spec=pl.BlockSpec((None, KV_RANK, 256), lambda i, j, k: (j, i, 0)), out_shape=(4, KV_RANK, 256))
    dnkv = _mm(dkv, W["w_ukv"], "nt", TT, KV_RANK, 1024, tm=tm_all, tn=KV_RANK, tk=256, name="w_ukv_dx",
               b_spec=pl.BlockSpec((None, KV_RANK, 256), lambda i, j, k: (k, j, 0)))
    dnkv = early_grads("mid", {
        "w_attn_out": jnp.transpose(d_w_ao_p.reshape(N_HEADS, HEAD_PAD, 4, 256)[:, 64:], (2, 0, 1, 3)).reshape(
            4, N_HEADS * 64, 256),
        "w_conv_out": d_w_co,
        "w_uq": d_w_uq_t.reshape(4, 2, HEAD_PAD, Q_RANK)[:, :, :QK_DIM].reshape(4, 2 * QK_DIM, Q_RANK).astype(BF16),
        "w_ukv": d_w_ukv}, dnkv)

    def f_lowrank_bwd(ids, ckv, cq, dkv_, dq_, gkv, gq):
        rk, rq = _rms(ckv), _rms(cq)
        nk, nq_ = ckv * rk, cq * rq
        return (_rms_bwd(dkv_ * gkv, nk, rk), _rms_bwd(dq_ * gq, nq_, rq), _colsum(dkv_ * nk), _colsum(dq_ * nq_))

    dp_kv, dp_q, dg_kv, dg_q = _ew(
        f_lowrank_bwd, (n_all,), [(pp, _rows(KV_RANK, KV0 // KV_RANK)), (pp, _rows(Q_RANK, Q0 // Q_RANK)),
                                  (dnkv, _rows(KV_RANK)), (dnq, _rows(Q_RANK)), (W["kv_norm_g"], vec(KV_RANK)),
                                  (W["q_norm_g"], vec(Q_RANK))],
        [row_out(KV_RANK, BF16, TT), row_out(Q_RANK, BF16, TT), acc_out(KV_RANK), acc_out(Q_RANK)], "lowrank_norm_bwd")

    lat_cols = jnp.concatenate([dp_ga, dp_gc, dp_cx, dp_cb, dp_cc], axis=1)
    dpp = jnp.concatenate([jnp.pad(lat_cols, ((0, TT - T), (0, 0))), dp_kv, dp_q, dp_kr.astype(BF16)], axis=1)
    d_w_in_t = _mm(hh, dpp, "tn", D_MODEL, P_COLS, TT, tm=512, tn=2176, tk=TT, name="w_in_dw",
                   transpose_out=True)
    dhh = _mm(dpp, W["w_in_t"], "nn", TT, D_MODEL, P_COLS, tm=tm_all, tn=512, tk=2176, name="w_in_dx")

    def f_norm1_bwd(ids, x, dh, dres, g, sc):
        r = _rms(x)
        xn = x * r
        return (dres + _rms_bwd(dh * g * (1.0 + sc), xn, r), _colsum(dh), _colsum(dh * xn * g),
                _colsum(dh * xn * (1.0 + sc)))

    grad_x, dsh1, dsc1, dg_n1 = _ew(
        f_norm1_bwd, (n_lat,), [(xx, _rows(D_MODEL)), (dhh, _rows(D_MODEL)), (dx1, _rows(D_MODEL)),
                                (W["norm1_g"], vec(D_MODEL)), (sc1, vec(D_MODEL))],
        [row_out(D_MODEL, F32)] + [acc_out(D_MODEL)] * 3, "norm1_bwd")

    def f_norm1_ctx_bwd(ids, x, dh, g, sc):
        xn = x * _rms(x)
        return _colsum(dh), _colsum(dh * xn * g), _colsum(dh * xn * (1.0 + sc))

    n_ctx = n_all - n_lat
    dcsh1, dcsc1, dg_n1c = _ew(
        f_norm1_ctx_bwd, (n_ctx,), [(xx, _rows(D_MODEL, 0, n_lat)), (dhh, _rows(D_MODEL, 0, n_lat)),
                                    (W["norm1_g"], vec(D_MODEL)), (csc1, vec(D_MODEL))], [acc_out(D_MODEL)] * 3,
        "norm1_ctx_bwd")

    big = {"w_in": _w_in_t_shards_from_p(d_w_in_t).astype(BF16)}
    zero = jnp.zeros((1, 4 * D_MODEL), F32)
    small = {
        "dmod_lat": jnp.concatenate([dsh1, dsc1, dg1, dsh2, dsc2, dg2], axis=1),
        "dmod_ctx": jnp.concatenate([dcsh1, dcsc1, zero], axis=1),
        "norm1_g": dg_n1 + dg_n1c, "norm2_g": dg_n2, "final_g": dg_f, "q_norm_g": dg_q, "kv_norm_g": dg_kv,
        "conv_b": d_conv_b, "conv_w": d_conv_w.reshape(1, -1), "ffn_conv_b": d_ffn_conv_b,
        "ffn_conv_w": d_ffn_conv_w.reshape(1, -1),
    }
    return grad_x, loss, big, small


SMALL = (("dmod_lat", 6144), ("dmod_ctx", 6144), ("norm1_g", 1024), ("norm2_g", 1024), ("final_g", 1024),
         ("q_norm_g", 384), ("kv_norm_g", 256), ("conv_b", 512), ("conv_w", 1536), ("ffn_conv_b", 5632),
         ("ffn_conv_w", 16896))
SMALL_ROWS = 320


def _adam_update(w, g, m, v):
    c1, c2 = 1.0 - ADAM_B1 ** ADAM_STEP, 1.0 - ADAM_B2 ** ADAM_STEP
    m2 = ADAM_B1 * m + (1.0 - ADAM_B1) * g
    v2 = ADAM_B2 * v + (1.0 - ADAM_B2) * (g * g)
    return [-ADAM_LR * ((m2 / c1) / (jnp.sqrt(v2 / c2) + ADAM_EPS) + ADAM_WD * w), m2, v2]


def _adamw(w, g, m, v, name):
    R, C = w.shape
    tr = 8 if R % 8 == 0 else R
    for t in range(8, R + 1, 8):
        if R % t == 0 and t * C * 4 <= (1 << 20):
            tr = t
    spec = pl.BlockSpec((tr, C), lambda i: (i, 0))
    return _ew(lambda ids, *vals: _adam_update(*vals), (R // tr,), [(w, spec), (g, spec), (m, spec), (v, spec)],
               [((R, C), F32, spec, None)] * 3, name)


def kernel(x, c, ctx, c_ctx, w_ada, b_ada, norm1_g, w_in, q_norm_g, kv_norm_g, w_uq, w_ukv, conv_w, conv_b, w_attn_out, w_conv_out, w_o, norm2_g, w_up, ffn_conv_w, ffn_conv_b, w_down, final_g, loss_target, m_c_ctx, m_w_ada, m_b_ada, m_norm1_g, m_w_in, m_q_norm_g, m_kv_norm_g, m_w_uq, m_w_ukv, m_conv_w, m_conv_b, m_w_attn_out, m_w_conv_out, m_w_o, m_norm2_g, m_w_up, m_ffn_conv_w, m_ffn_conv_b, m_w_down, m_final_g, v_c_ctx, v_w_ada, v_b_ada, v_norm1_g, v_w_in, v_q_norm_g, v_kv_norm_g, v_w_uq, v_w_ukv, v_conv_w, v_conv_b, v_w_attn_out, v_w_conv_out, v_w_o, v_norm2_g, v_w_up, v_ffn_conv_w, v_ffn_conv_b, v_w_down, v_final_g):
    mx, my, mc = lax.axis_index("x"), lax.axis_index("y"), lax.axis_index("c")
    chip = 2 * mx + my
    dev = 4 * mx + 2 * my + mc
    T, Tc = x.shape[1], ctx.shape[1]
    TT = T + Tc
    w_in_t, m_w_in_t, v_w_in_t = (jnp.transpose(a[0]) for a in (w_in, m_w_in, v_w_in))
    w_uq_t, m_w_uq_t, v_w_uq_t = (jnp.transpose(a[0]) for a in (w_uq, m_w_uq, v_w_uq))
    shards = {"w_in": jnp.pad(w_in_t, ((0, W_IN_SHARD_PAD - W_IN_SHARD), (0, 0))), "w_uq": w_uq_t, "w_ukv": w_ukv[0],
              "w_attn_out": w_attn_out[0], "w_conv_out": w_conv_out[0], "w_o": w_o[0], "w_up": w_up[0],
              "w_down": w_down[0]}

    conv_sh = jnp.concatenate([conv_w[0], ffn_conv_w[0]], axis=1)
    pay1 = jnp.concatenate([jnp.pad(c, ((0, 7), (0, 0))), jnp.pad(conv_sh, ((0, 5), (0, 0)))], axis=1)
    got1 = _allgather8(pay1, "gather_cond", in_vmem=True)[0].reshape(8, 8, 2560)
    c_all = got1[:, 0, :D_MODEL]
    conv_all = got1[0::2, :3, D_MODEL:]
    conv_w_full = _cols_from_shards(conv_all[:, :, :128])
    ffn_conv_w_full = _cols_from_shards(conv_all[:, :, 128:])

    cond = jnp.concatenate([c_all, c_ctx.reshape(1, D_MODEL), jnp.zeros((7, D_MODEL), F32)], axis=0)

    def f_silu(ids, v):
        return (v * _sigmoid(v),)

    (s16,) = _ew(f_silu, (1,), [(cond, _full((16, D_MODEL)))], [((16, D_MODEL), F32, _full((16, D_MODEL)), None)], "silu_cond")
    mod_sh = _mm(s16, w_ada[0], "nn", 16, 1536, D_MODEL, tm=16, tn=768, tk=D_MODEL, name="w_ada_fwd")
    got2, after_mod = _allgather8(mod_sh, "gather_mod", in_vmem=True)
    mod_all = _cols_from_shards(got2.reshape(4, 2, 16, 1536)[:, 0]) + b_ada
    mod_lat = lax.dynamic_slice_in_dim(mod_all, dev, 1, axis=0)
    mod_ctx = mod_all[8:9]

    names = [n for n, _ in BIG]
    first = [n for n in names if n not in GATHER_LATE]
    gathered, zero = _gather_weights([(shards[n] + after_mod[0, 0]).astype(BF16) for n in first])
    full = dict(zip(first, gathered))
    xx = jnp.concatenate([x[0], ctx[0]], axis=0)
    late_bf = [(shards[n] + zero[0, 0]).astype(BF16) for n in GATHER_LATE]
    g_send, g_recv, late_src, late_land, xx = _ici_start(
        "gather", late_bf, [(4,) + s.shape for s in late_bf], xx, "gather_late_start")

    def late_weights(after):
        src, land = _ici_wait("gather", g_send, g_recv, late_src, late_land, after, "gather_late_wait")
        got = dict(zip(GATHER_LATE, _gather_finish(src, land)))
        wao = _cols_from_shards(got["w_attn_out"]).reshape(N_HEADS, 64, D_MODEL)
        return {"w_attn_out": jnp.pad(wao, ((0, 0), (64, 0), (0, 0))).reshape(N_HEADS * HEAD_PAD, D_MODEL),
                "w_conv_out": got["w_conv_out"], "w_o": got["w_o"].reshape(D_MODEL, D_MODEL), "w_up": got["w_up"],
                "w_down": got["w_down"].reshape(D_FF, D_MODEL)}

    wuq_t = full["w_uq"].reshape(N_HEADS, QK_DIM, Q_RANK)
    W = {
        "w_in_t": _w_in_t_p_from_shards(full["w_in"]),
        "w_uq_t": jnp.pad(wuq_t, ((0, 0), (0, HEAD_PAD - QK_DIM), (0, 0))).reshape(N_HEADS * HEAD_PAD, Q_RANK),
        "w_ukv": full["w_ukv"],
        "norm1_g": norm1_g, "norm2_g": norm2_g, "final_g": final_g.reshape(1, D_MODEL), "q_norm_g": q_norm_g,
        "kv_norm_g": kv_norm_g, "conv_w": conv_w_full, "conv_b": conv_b, "ffn_conv_w": ffn_conv_w_full,
        "ffn_conv_b": ffn_conv_b,
    }

    place = jnp.stack([chip, mc]).astype(jnp.int32)
    early = {}

    def early_grads(tag, g, carry):
        gs = list(g.values())
        from_sib = _rs_pair(gs, "rs_pair_" + tag)
        sums = [_add_pair(gs[w], from_sib[w], place, "rs_pair_add_" + n) for w, n in enumerate(g)]
        send, recv, sums, land, carry = _ici_start(
            "scatter", sums, [(3,) + s.shape[1:] for s in sums], carry, "rs_chips_" + tag + "_start")
        early[tag] = (list(g), send, recv, sums, land)
        return carry

    grad_x, loss_part, gbig, gsmall = _local_step(xx, loss_target[0], mod_lat, mod_ctx, W, late_weights, early_grads)
    loss = lax.psum(loss_part[0, 0], ("x", "y", "c"))

    pay3 = jnp.concatenate([gsmall[n].reshape(-1) for n, _ in SMALL])
    pay3 = jnp.pad(pay3, (0, SMALL_ROWS * 128 - pay3.shape[0])).reshape(SMALL_ROWS, 128)
    got3 = _allgather8(pay3, "gather_small", in_vmem=True)[0]

    def f_sum8(ids, a):
        s = a[0:SMALL_ROWS]
        for d in range(1, 8):
            s = s + a[d * SMALL_ROWS:(d + 1) * SMALL_ROWS]
        return (s,)

    (vsum,) = _ew(f_sum8, (1,), [(got3, _full((8 * SMALL_ROWS, 128)))],
                  [((SMALL_ROWS, 128), F32, _full((SMALL_ROWS, 128)), None)], "sum_small")
    vflat = vsum.reshape(-1)
    gvec, off = {}, 0
    for n, size in SMALL:
        gvec[n] = vflat[off:off + size]
        off += size
    dmod_rows = got3.reshape(8, SMALL_ROWS * 128)[:, :6 * D_MODEL]
    dm16 = jnp.concatenate([dmod_rows, gvec["dmod_ctx"].reshape(1, -1), jnp.zeros((7, 6 * D_MODEL), F32)], axis=0)

    def f_colsum(ids, a):
        return (_colsum(a),)

    (g_b_ada,) = _ew(f_colsum, (1,), [(dm16, _full((16, 6 * D_MODEL)))],
                     [((1, 6 * D_MODEL), F32, _full((1, 6 * D_MODEL)), None)], "b_ada_grad")
    dm_sh = lax.dynamic_slice_in_dim(dm16, chip * 1536, 1536, axis=1)
    g_w_ada = _mm(s16, dm_sh, "tn", D_MODEL, 1536, 16, tm=512, tn=768, tk=16, name="w_ada_dw")
    dcond_part = _mm(dm_sh, w_ada[0], "nt", 16, D_MODEL, 1536, tm=16, tn=512, tk=1536, name="w_ada_dx")
    got4 = _allgather8(dcond_part[8:16], "gather_dcond", in_vmem=True)[0].reshape(4, 2, 8, D_MODEL)[:, 0, 0]

    def f_c_ctx(ids, parts, cc):
        s = _sigmoid(cc)
        d = parts[0:1] + parts[1:2] + parts[2:3] + parts[3:4]
        return (d * s * (1.0 + cc * (1.0 - s)),)

    (g_c_ctx,) = _ew(f_c_ctx, (1,), [(got4, _full((4, D_MODEL))), (c_ctx.reshape(1, D_MODEL), _full((1, D_MODEL)))],
                     [((1, D_MODEL), F32, _full((1, D_MODEL)), None)], "c_ctx_grad")

    last = list(gbig)
    from_sibling = _rs_pair([gbig[n] for n in last], "rs_pair")
    pair_sums = [_add_pair(gbig[n], from_sibling[w], place, "rs_pair_add_" + n) for w, n in enumerate(last)]
    lands = _rs_chips(pair_sums)
    done = last
    for tag, (tag_names, send, recv, sums, land) in early.items():
        sums, land = _ici_wait("scatter", send, recv, sums, land, grad_x, "rs_chips_" + tag + "_wait")
        done, pair_sums, lands = done + tag_names, pair_sums + sums, lands + land
    half_sums = [_add_chips(a, b, place, "rs_chip_add_" + n) for a, b, n in zip(pair_sums, lands, done)]
    gw = dict(zip(done, _rs_pair_back(half_sums)))
    gw["w_ada"] = g_w_ada

    moments = {"w_ada": (w_ada, m_w_ada, v_w_ada), "w_ukv": (w_ukv, m_w_ukv, v_w_ukv),
               "w_attn_out": (w_attn_out, m_w_attn_out, v_w_attn_out),
               "w_conv_out": (w_conv_out, m_w_conv_out, v_w_conv_out), "w_o": (w_o, m_w_o, v_w_o),
               "w_up": (w_up, m_w_up, v_w_up), "w_down": (w_down, m_w_down, v_w_down)}
    grads, deltas, new_m, new_v = {}, {}, {}, {}
    for n, (w_, m_, v_) in moments.items():
        d_, m2, v2 = _adamw(w_[0], gw[n], m_[0], v_[0], "adamw_" + n)
        grads[n], deltas[n], new_m[n], new_v[n] = gw[n][None], d_[None], m2[None], v2[None]
    for n, (w_, m_, v_) in {"w_in": (w_in_t, m_w_in_t, v_w_in_t), "w_uq": (w_uq_t, m_w_uq_t, v_w_uq_t)}.items():
        d_, m2, v2 = _adamw(w_, gw[n], m_, v_, "adamw_" + n)
        back = lambda a: jnp.transpose(a)[None]
        grads[n], deltas[n], new_m[n], new_v[n] = back(gw[n][:w_.shape[0]]), back(d_), back(m2), back(v2)

    conv_w_g = lax.dynamic_slice_in_dim(gvec["conv_w"].reshape(3, CONV_DIM), chip * 128, 128, axis=1)
    ffn_conv_w_g = lax.dynamic_slice_in_dim(gvec["ffn_conv_w"].reshape(3, 2 * D_FF), chip * 1408, 1408, axis=1)
    vec_params = (("c_ctx", c_ctx, m_c_ctx, v_c_ctx, g_c_ctx), ("b_ada", b_ada, m_b_ada, v_b_ada, g_b_ada),
                  ("norm1_g", norm1_g, m_norm1_g, v_norm1_g, gvec["norm1_g"]),
                  ("q_norm_g", q_norm_g, m_q_norm_g, v_q_norm_g, gvec["q_norm_g"]),
                  ("kv_norm_g", kv_norm_g, m_kv_norm_g, v_kv_norm_g, gvec["kv_norm_g"]),
                  ("conv_w", conv_w, m_conv_w, v_conv_w, conv_w_g), ("conv_b", conv_b, m_conv_b, v_conv_b, gvec["conv_b"]),
                  ("norm2_g", norm2_g, m_norm2_g, v_norm2_g, gvec["norm2_g"]),
                  ("ffn_conv_w", ffn_conv_w, m_ffn_conv_w, v_ffn_conv_w, ffn_conv_w_g),
                  ("ffn_conv_b", ffn_conv_b, m_ffn_conv_b, v_ffn_conv_b, gvec["ffn_conv_b"]),
                  ("final_g", final_g, m_final_g, v_final_g, gvec["final_g"]))
    two_d = lambda a: a.reshape((-1, a.shape[-1]))

    def f_adam_many(ids, *vals):
        out = []
        for k in range(len(vec_params)):
            out += _adam_update(*vals[4 * k:4 * k + 4])
        return out

    ins_v, outs_v = [], []
    for p in vec_params:
        shp = two_d(p[1]).shape
        ins_v += [(two_d(a), _full(shp)) for a in (p[1], p[4], p[2], p[3])]
        outs_v += [(shp, F32, _full(shp), None)] * 3
    res_v = _ew(f_adam_many, (1,), ins_v, outs_v, "adamw_vectors")
    for k, p in enumerate(vec_params):
        n, shape = p[0], p[1].shape
        grads[n] = p[4].reshape(shape)
        deltas[n], new_m[n], new_v[n] = (r.reshape(shape) for r in res_v[3 * k:3 * k + 3])

    order = ("c_ctx", "w_ada", "b_ada", "norm1_g", "w_in", "q_norm_g", "kv_norm_g", "w_uq", "w_ukv", "conv_w", "conv_b",
             "w_attn_out", "w_conv_out", "w_o", "norm2_g", "w_up", "ffn_conv_w", "ffn_conv_b", "w_down", "final_g")
    return (loss, grad_x[None], *[grads[n] for n in order], *[deltas[n] for n in order],
            *[new_m[n] for n in order], *[new_v[n] for n in order])
```

```python
import functools

import jax
import jax.numpy as jnp
from jax import lax
from jax.experimental import pallas as pl
from jax.experimental.pallas import tpu as pltpu

F32, BF16 = jnp.float32, jnp.bfloat16
MESH = pl.DeviceIdType.MESH

D_MODEL = 1024
N_HEADS = 8
HEAD_PAD = 128
QK_DIM = 96
Q_RANK, KV_RANK = 384, 256
CONV_DIM = 512
D_FF = 2816
GRID_W = 64
ROPE_THETA = 10000.0
EPS = 1e-6
GA0, GC0, CX0, CB0, CC0, KV0, Q0, KR0, P_COLS = 0, 1024, 2048, 2560, 3072, 3584, 3840, 4224, 4352
ROW_TILE = 256
VMEM_LIMIT_BYTES = 48 * 1024 * 1024

ADAM_LR, ADAM_B1, ADAM_B2, ADAM_EPS, ADAM_WD, ADAM_STEP = 0.001, 0.9, 0.999, 1e-08, 0.01, 10

BIG = (("w_in", (1088, 1024)), ("w_uq", (192, 384)), ("w_ukv", (256, 256)), ("w_attn_out", (512, 256)),
       ("w_conv_out", (512, 256)), ("w_o", (256, 1024)), ("w_up", (1024, 1408)), ("w_down", (704, 1024)))

GATHER_LATE = ("w_attn_out", "w_conv_out", "w_o", "w_up", "w_down")

NN = (((1,), (0,)), ((), ()))
NT = (((1,), (1,)), ((), ()))
TN = (((0,), (0,)), ((), ()))


def _cp(sem):
    return pltpu.CompilerParams(dimension_semantics=sem, vmem_limit_bytes=VMEM_LIMIT_BYTES)


PIN_BYTES = 1 << 19


def _in_hbm(arrays):
    return [pltpu.with_memory_space_constraint(a, pltpu.HBM) if a.size * a.dtype.itemsize >= PIN_BYTES else a
            for a in arrays]


def _out(shape, dtype):
    n = 1
    for d in shape:
        n *= d
    big = n * jnp.dtype(dtype).itemsize >= PIN_BYTES
    return pltpu.HBM(shape, dtype) if big else jax.ShapeDtypeStruct(shape, dtype)


def _pick(n, prefs):
    for p in prefs:
        if n % p == 0:
            return p
    return n


def _mm(a, b, mode, M, N, K, *, tm, tn, tk, name, out_dtype=F32, a_spec=None, b_spec=None, o_spec=None,
        out_shape=None, transpose_out=False):
    assert M % tm == 0 and N % tn == 0 and K % tk == 0, (name, M, N, K, tm, tn, tk)
    nk = K // tk
    dims = {"nn": NN, "nt": NT, "tn": TN}[mode]
    if a_spec is None:
        a_spec = (pl.BlockSpec((tk, tm), lambda i, j, k: (k, i)) if mode == "tn"
                  else pl.BlockSpec((tm, tk), lambda i, j, k: (i, k)))
    if b_spec is None:
        b_spec = (pl.BlockSpec((tn, tk), lambda i, j, k: (j, k)) if mode == "nt"
                  else pl.BlockSpec((tk, tn), lambda i, j, k: (k, j)))
    if o_spec is None:
        o_spec = (pl.BlockSpec((tn, tm), lambda i, j, k: (j, i)) if transpose_out
                  else pl.BlockSpec((tm, tn), lambda i, j, k: (i, j)))
    if out_shape is None:
        out_shape = (N, M) if transpose_out else (M, N)

    def emit(o_ref, val):
        o_ref[...] = (val.T if transpose_out else val).astype(o_ref.dtype)

    def body(a_ref, b_ref, o_ref, *scratch):
        part = lax.dot_general(a_ref[...].astype(BF16), b_ref[...].astype(BF16), dims, preferred_element_type=F32)
        if nk == 1:
            emit(o_ref, part)
            return
        acc_ref, = scratch
        k = pl.program_id(2)

        @pl.when(k == 0)
        def _():
            acc_ref[...] = part

        @pl.when((k > 0) & (k < nk - 1))
        def _():
            acc_ref[...] += part

        @pl.when(k == nk - 1)
        def _():
            emit(o_ref, acc_ref[...] + part)

    return pl.pallas_call(
        body, grid=(M // tm, N // tn, nk), in_specs=[a_spec, b_spec], out_specs=o_spec,
        out_shape=_out(out_shape, out_dtype),
        scratch_shapes=[pltpu.VMEM((tm, tn), F32)] if nk > 1 else [],
        compiler_params=_cp(("parallel", "parallel", "arbitrary")), name=name)(*_in_hbm([a, b]))


def _ew(fn, grid, ins, outs, name, scalars=None):
    n_in = len(ins)
    n_sc = 0 if scalars is None else 1

    def store(ref, val, acc, ids):
        if isinstance(val, (list, tuple)):
            for h, v in enumerate(val):
                ref[h] = v.astype(ref.dtype)
            return
        if acc is None:
            ref[...] = val.astype(ref.dtype)
            return

        @pl.when(ids[acc] == 0)
        def _():
            ref[...] = val.astype(ref.dtype)

        @pl.when(ids[acc] > 0)
        def _():
            ref[...] += val.astype(ref.dtype)

    def body(*refs):
        refs = refs[n_sc:]
        ids = tuple(pl.program_id(a) for a in range(len(grid)))
        vals = fn(ids, *[r[...] for r in refs[:n_in]])
        for ref, val, (_, _, _, acc) in zip(refs[n_in:], vals, outs):
            store(ref, val, acc, ids)

    acc_axes = {o[3] for o in outs if o[3] is not None}
    sem = tuple("arbitrary" if a in acc_axes else "parallel" for a in range(len(grid)))
    in_specs, out_specs = [s for _, s in ins], [o[2] for o in outs]
    out_shape = [_out(o[0], o[1]) for o in outs]
    args = _in_hbm([a for a, _ in ins])
    if scalars is None:
        return pl.pallas_call(body, grid=grid, in_specs=in_specs, out_specs=out_specs, out_shape=out_shape,
                              compiler_params=_cp(sem), name=name)(*args)
    spec = pltpu.PrefetchScalarGridSpec(num_scalar_prefetch=1, grid=grid, in_specs=in_specs, out_specs=out_specs)
    return pl.pallas_call(body, grid_spec=spec, out_shape=out_shape, compiler_params=_cp(sem), name=name)(scalars, *args)


def _rows(width, cblk=0, roff=0, tr=ROW_TILE):
    return pl.BlockSpec((tr, width), lambda i: (i + roff, cblk))


def _full(shape):
    nd = len(shape)
    return pl.BlockSpec(shape, lambda *_: (0,) * nd)


def _sigmoid(x):
    return 1.0 / (1.0 + jnp.exp(-x))


def _rms(x):
    return lax.rsqrt(jnp.mean(x * x, axis=-1, keepdims=True) + EPS)


def _rms_bwd(dn, xn, r):
    return r * (dn - xn * jnp.mean(dn * xn, axis=-1, keepdims=True))


def _colsum(x):
    return jnp.sum(x, axis=0, keepdims=True)


def _shift_prev(x):
    rows = lax.broadcasted_iota(jnp.int32, x.shape, 0)
    return jnp.where(rows == 0, 0.0, pltpu.roll(x, 1, 0))


def _shift_next(x):
    rows = lax.broadcasted_iota(jnp.int32, x.shape, 0)
    return jnp.where(rows == x.shape[0] - 1, 0.0, pltpu.roll(x, x.shape[0] - 1, 0))


def _conv(x, w, b):
    return b + _shift_prev(x) * w[0:1] + x * w[1:2] + _shift_next(x) * w[2:3]


def _conv_bwd_x(dy, w):
    return _shift_next(dy) * w[0:1] + dy * w[1:2] + _shift_prev(dy) * w[2:3]


def _conv_bwd_w(dy, x):
    return _colsum(dy * _shift_prev(x)), _colsum(dy * x), _colsum(dy * _shift_next(x))


def _rope(x, cos, sin_lo, sin_hi):
    return x * cos + pltpu.roll(x, HEAD_PAD - 8, 1) * sin_lo + pltpu.roll(x, 8, 1) * sin_hi


ATTN_SCALE = QK_DIM ** -0.5


def _head_keys(kv_ref, kr_ref, cos_ref, slo_ref, shi_ref, kc_ref, vp_ref):
    kv = kv_ref[...]
    lane = lax.broadcasted_iota(jnp.int32, kv.shape, 1)
    kc_ref[...] = jnp.where(lane < 64, kv, _rope(kr_ref[...], cos_ref[...], slo_ref[...], shi_ref[...])).astype(BF16)
    vp_ref[...] = jnp.where(lane >= 64, kv, 0.0).astype(BF16)


def _attn_specs(tq, TT, clamp):
    row = (lambda i: jnp.minimum(i, clamp)) if clamp is not None else (lambda i: i)
    q = pl.BlockSpec((tq, HEAD_PAD), lambda h, i: (i, h))
    lat = pl.BlockSpec((tq, HEAD_PAD), lambda h, i: (row(i), h))
    keys = pl.BlockSpec((TT, HEAD_PAD), lambda h, i: (0, h))
    kr = pl.BlockSpec((TT, HEAD_PAD), lambda h, i: (0, KR0 // HEAD_PAD))
    tab_q = pl.BlockSpec((tq, HEAD_PAD), lambda h, i: (i, 0))
    tab_k = pl.BlockSpec((TT, HEAD_PAD), lambda h, i: (0, 0))
    lse = pl.BlockSpec((None, tq, 1), lambda h, i: (h, row(i), 0))
    return q, lat, keys, kr, tab_q, tab_k, lse


def _attn_fwd(q_raw, kv, pp, tabs, T, TT):
    tq = ROW_TILE
    cos, slo, shi = tabs

    def body(q_ref, kv_ref, kr_ref, cq, lq, hq, ck, lk, hk, o_ref, l_ref, kc, vp):
        @pl.when(pl.program_id(1) == 0)
        def _():
            _head_keys(kv_ref, kr_ref, ck, lk, hk, kc, vp)

        q = _rope(q_ref[...], cq[...], lq[...], hq[...]).astype(BF16)
        s = lax.dot_general(q, kc[...], NT, preferred_element_type=F32) * ATTN_SCALE
        m = jnp.max(s, axis=-1, keepdims=True)
        p = jnp.exp(s - m)
        l = jnp.sum(p, axis=-1, keepdims=True)
        o = lax.dot_general(p.astype(BF16), vp[...], NN, preferred_element_type=F32)
        o_ref[...] = o / l
        l_ref[...] = m + jnp.log(l)

    qs, _, keys, kr, tab_q, tab_k, lse = _attn_specs(tq, TT, None)
    return pl.pallas_call(
        body, grid=(N_HEADS, T // tq), in_specs=[qs, keys, kr, tab_q, tab_q, tab_q, tab_k, tab_k, tab_k],
        out_specs=[qs, lse],
        out_shape=[jax.ShapeDtypeStruct((T, N_HEADS * HEAD_PAD), F32), jax.ShapeDtypeStruct((N_HEADS, T, 1), F32)],
        scratch_shapes=[pltpu.VMEM((TT, HEAD_PAD), BF16), pltpu.VMEM((TT, HEAD_PAD), BF16)],
        compiler_params=_cp(("parallel", "arbitrary")), name="attn_fwd",
    )(*_in_hbm([q_raw, kv, pp, cos, slo, shi, cos, slo, shi]))


def _attn_bwd(q_raw, kv, pp, o, do, lse, tabs, tabs_inv, T, TT):
    tq = ROW_TILE
    nq = T // tq
    cos, slo, shi = tabs
    cos_i, slo_i, shi_i = tabs_inv

    def body(q_ref, kv_ref, kr_ref, cq, lq, hq, ck, lk, hk, iq, ilq, ihq, ik, ilk, ihk, o_ref, do_ref, l_ref,
             dq_ref, dkv_ref, dkr_ref, kc, vp, dk, dv):
        h, i = pl.program_id(0), pl.program_id(1)

        @pl.when(i == 0)
        def _():
            _head_keys(kv_ref, kr_ref, ck, lk, hk, kc, vp)
            dk[...] = jnp.zeros_like(dk)
            dv[...] = jnp.zeros_like(dv)

        @pl.when(i < nq)
        def _():
            q = _rope(q_ref[...], cq[...], lq[...], hq[...]).astype(BF16)
            k, v, d_o = kc[...], vp[...], do_ref[...]
            s = lax.dot_general(q, k, NT, preferred_element_type=F32) * ATTN_SCALE
            p = jnp.exp(s - l_ref[...])
            dob = d_o.astype(BF16)
            dp = lax.dot_general(dob, v, NT, preferred_element_type=F32)
            dd = jnp.sum(d_o * o_ref[...], axis=-1, keepdims=True)
            ds = (p * (dp - dd) * ATTN_SCALE).astype(BF16)
            dq = lax.dot_general(ds, k, NN, preferred_element_type=F32)
            dq_ref[...] = _rope(dq, iq[...], ilq[...], ihq[...]).astype(dq_ref.dtype)
            dk[...] += lax.dot_general(ds, q, TN, preferred_element_type=F32)
            dv[...] += lax.dot_general(p.astype(BF16), dob, TN, preferred_element_type=F32)

        @pl.when(i == nq)
        def _():
            dq_ref[...] = jnp.zeros_like(dq_ref)
            dkh = dk[...]
            lane = lax.broadcasted_iota(jnp.int32, dkh.shape, 1)
            dkv_ref[...] = jnp.where(lane < 64, dkh, dv[...]).astype(dkv_ref.dtype)
            rot = _rope(jnp.where((lane >= 64) & (lane < 96), dkh, 0.0), ik[...], ilk[...], ihk[...])

            @pl.when(h == 0)
            def _():
                dkr_ref[...] = rot

            @pl.when(h > 0)
            def _():
                dkr_ref[...] += rot

    qs, lat, keys, kr, tab_q, tab_k, lse_spec = _attn_specs(tq, TT, nq - 1)
    wide = jax.ShapeDtypeStruct((TT, N_HEADS * HEAD_PAD), BF16)
    return pl.pallas_call(
        body, grid=(N_HEADS, TT // tq),
        in_specs=[qs, keys, kr] + [tab_q] * 3 + [tab_k] * 3 + [tab_q] * 3 + [tab_k] * 3 + [lat, lat, lse_spec],
        out_specs=[qs, keys, pl.BlockSpec((TT, HEAD_PAD), lambda h, i: (0, 0))],
        out_shape=[wide, wide, jax.ShapeDtypeStruct((TT, HEAD_PAD), F32)],
        scratch_shapes=[pltpu.VMEM((TT, HEAD_PAD), BF16), pltpu.VMEM((TT, HEAD_PAD), BF16),
                        pltpu.VMEM((TT, HEAD_PAD), F32), pltpu.VMEM((TT, HEAD_PAD), F32)],
        compiler_params=_cp(("arbitrary", "arbitrary")), name="attn_bwd",
    )(*_in_hbm([q_raw, kv, pp, cos, slo, shi, cos, slo, shi, cos_i, slo_i, shi_i, cos_i, slo_i, shi_i, o, do, lse]))


def _allgather8(x, name, in_vmem):
    m_per, n = x.shape

    def body(x_ref, out_ref, token, send_sems, recv_sems, local_sem):
        token[...] = jnp.zeros_like(token)
        mx, my, mc = lax.axis_index("x"), lax.axis_index("y"), lax.axis_index("c")
        me, sibling = (mx, my, mc), (mx, my, 1 - mc)
        chips = [(1 - mx, my), (mx, 1 - my), (1 - mx, 1 - my)]

        def rows(px, py, pc):
            return out_ref.at[pl.ds((4 * px + 2 * py + pc) * m_per, m_per), :]

        def copy(k, block, to, src=None):
            return pltpu.make_async_remote_copy(
                src_ref=rows(*block) if src is None else src, dst_ref=rows(*block),
                send_sem=send_sems.at[k], recv_sem=recv_sems.at[k], device_id=to, device_id_type=MESH)

        mine = pltpu.make_async_copy(x_ref, rows(*me), local_sem)
        mine.start()
        first = [copy(0, me, sibling, src=x_ref)]
        first += [copy(1 + j, me, (*chip, mc), src=x_ref) for j, chip in enumerate(chips)]
        for cp in first:
            cp.start()
        passed = [copy(4 + j, (*chip, mc), sibling) for j, chip in enumerate(chips)]
        for j, chip in enumerate(chips):
            copy(1 + j, (*chip, mc), me).wait_recv()
            passed[j].start()
        copy(0, sibling, me).wait_recv()
        for j, chip in enumerate(chips):
            copy(4 + j, (*chip, 1 - mc), me).wait_recv()
        for cp in first + passed:
            cp.wait_send()
        mine.wait()

    space = pltpu.VMEM if in_vmem else pl.ANY
    return pl.pallas_call(
        body, out_shape=[jax.ShapeDtypeStruct((8 * m_per, n), x.dtype), jax.ShapeDtypeStruct((8, 128), F32)],
        in_specs=[pl.BlockSpec(memory_space=space)],
        out_specs=[pl.BlockSpec(memory_space=space), pl.BlockSpec(memory_space=pltpu.VMEM)],
        scratch_shapes=[pltpu.SemaphoreType.DMA((7,)), pltpu.SemaphoreType.DMA((7,)), pltpu.SemaphoreType.DMA],
        name=name)(x)


def _hbm_specs(n):
    return [pl.BlockSpec(memory_space=pl.ANY)] * n


def _gather_weights(shards):
    n = len(shards)
    halves = [s.shape[0] // 2 for s in shards]

    def body(*refs):
        ins, outs = refs[:n], refs[n:2 * n]
        token, send_sems, recv_sems = refs[2 * n:]
        token[...] = jnp.zeros_like(token)
        mx, my, mc = lax.axis_index("x"), lax.axis_index("y"), lax.axis_index("c")
        j_me = 2 * mx + my
        chips = [(1 - mx, my), (mx, 1 - my), (1 - mx, 1 - my)]

        def half(w, chip_idx, hc):
            return outs[w].at[chip_idx, pl.ds(hc * halves[w], halves[w]), :]

        def copy(w, k, src, dst, to):
            return pltpu.make_async_remote_copy(src_ref=src, dst_ref=dst, send_sem=send_sems.at[w, k],
                                                recv_sem=recv_sems.at[w, k], device_id=to, device_id_type=MESH)

        sends = []
        for w in range(n):
            cp = copy(w, 6, ins[w], outs[w].at[j_me], (mx, my, 1 - mc))
            cp.start()
            sends.append(cp)
        for k, (px, py) in enumerate(chips):
            for w in range(n):
                cp = copy(w, k, ins[w].at[pl.ds(mc * halves[w], halves[w]), :], half(w, j_me, mc), (px, py, mc))
                cp.start()
                sends.append(cp)
        for k, (px, py) in enumerate(chips):
            for w in range(n):
                got = half(w, 2 * px + py, mc)
                copy(w, k, got, got, (px, py, mc)).wait_recv()
                cp = copy(w, 3 + k, got, got, (mx, my, 1 - mc))
                cp.start()
                sends.append(cp)
        for k, (px, py) in enumerate(chips):
            for w in range(n):
                got = half(w, 2 * px + py, 1 - mc)
                copy(w, 3 + k, got, got, (mx, my, 1 - mc)).wait_recv()
        for w in range(n):
            own = outs[w].at[j_me]
            copy(w, 6, own, own, (mx, my, 1 - mc)).wait_recv()
        for cp in sends:
            cp.wait_send()

    res = pl.pallas_call(
        body, out_shape=[jax.ShapeDtypeStruct((4,) + s.shape, s.dtype) for s in shards]
        + [jax.ShapeDtypeStruct((8, 128), F32)],
        in_specs=_hbm_specs(n), out_specs=_hbm_specs(n) + [pl.BlockSpec(memory_space=pltpu.VMEM)],
        scratch_shapes=[pltpu.SemaphoreType.DMA((n, 7)), pltpu.SemaphoreType.DMA((n, 7))],
        name="gather_weights")(*shards)
    return list(res[:n]), res[n]


def _rs_pair(gs, name):
    n = len(gs)
    halves = [g.shape[1] // 2 for g in gs]

    def body(*refs):
        ins, lands = refs[:n], refs[n:2 * n]
        send_sems, recv_sems = refs[2 * n:]
        mx, my, mc = lax.axis_index("x"), lax.axis_index("y"), lax.axis_index("c")
        copies = []
        for w in range(n):
            h = halves[w]
            cp = pltpu.make_async_remote_copy(
                src_ref=ins[w].at[:, pl.ds((1 - mc) * h, h), :], dst_ref=lands[w], send_sem=send_sems.at[w],
                recv_sem=recv_sems.at[w], device_id=(mx, my, 1 - mc), device_id_type=MESH)
            cp.start()
            copies.append(cp)
        for cp in copies:
            cp.wait()

    return pl.pallas_call(
        body, out_shape=[jax.ShapeDtypeStruct((4, h, g.shape[2]), g.dtype) for g, h in zip(gs, halves)],
        in_specs=_hbm_specs(n), out_specs=_hbm_specs(n),
        scratch_shapes=[pltpu.SemaphoreType.DMA((n,)), pltpu.SemaphoreType.DMA((n,))], name=name)(*gs)


def _rs_chips(parts):
    n = len(parts)

    def body(*refs):
        ins, lands = refs[:n], refs[n:2 * n]
        send_sems, recv_sems = refs[2 * n:]
        mx, my, mc = lax.axis_index("x"), lax.axis_index("y"), lax.axis_index("c")
        copies = []
        for k, (px, py) in enumerate([(1 - mx, my), (mx, 1 - my), (1 - mx, 1 - my)]):
            for w in range(n):
                cp = pltpu.make_async_remote_copy(
                    src_ref=ins[w].at[2 * px + py], dst_ref=lands[w].at[k], send_sem=send_sems.at[w, k],
                    recv_sem=recv_sems.at[w, k], device_id=(px, py, mc), device_id_type=MESH)
                cp.start()
                copies.append(cp)
        for cp in copies:
            cp.wait()

    return list(pl.pallas_call(
        body, out_shape=[jax.ShapeDtypeStruct((3,) + p.shape[1:], p.dtype) for p in parts],
        in_specs=_hbm_specs(n), out_specs=_hbm_specs(n),
        scratch_shapes=[pltpu.SemaphoreType.DMA((n, 3)), pltpu.SemaphoreType.DMA((n, 3))], name="rs_chips")(*parts))


def _rs_pair_back(gs):
    n = len(gs)

    def body(*refs):
        outs = refs[n:2 * n]
        send_sems, recv_sems = refs[2 * n:]
        mx, my, mc = lax.axis_index("x"), lax.axis_index("y"), lax.axis_index("c")
        copies = []
        for w in range(n):
            h = gs[w].shape[0] // 2
            mine = outs[w].at[pl.ds(mc * h, h), :]
            cp = pltpu.make_async_remote_copy(src_ref=mine, dst_ref=mine, send_sem=send_sems.at[w],
                                              recv_sem=recv_sems.at[w], device_id=(mx, my, 1 - mc), device_id_type=MESH)
            cp.start()
            copies.append(cp)
        for cp in copies:
            cp.wait()

    return pl.pallas_call(
        body, out_shape=[jax.ShapeDtypeStruct(g.shape, g.dtype) for g in gs],
        in_specs=_hbm_specs(n), out_specs=_hbm_specs(n), input_output_aliases={w: w for w in range(n)},
        scratch_shapes=[pltpu.SemaphoreType.DMA((n,)), pltpu.SemaphoreType.DMA((n,))], name="rs_pair_back")(*gs)


_HBM = pl.BlockSpec(memory_space=pltpu.HBM)
_SEM = pl.BlockSpec(memory_space=pltpu.SEMAPHORE)
_EFFECT = pltpu.SideEffectType.DATAFLOW_SIDE_EFFECTING


def _ici_copies(kind, srcs, lands, send_sems, recv_sems):
    n = len(srcs)
    mx, my, mc = lax.axis_index("x"), lax.axis_index("y"), lax.axis_index("c")
    j_me = 2 * mx + my
    copies = []
    if kind == "all":
        for k in range(7):
            a, b, c = (k + 1) >> 2 & 1, (k + 1) >> 1 & 1, (k + 1) & 1
            peer = (1 - mx if a else mx, 1 - my if b else my, 1 - mc if c else mc)
            for w in range(n):
                copies.append(pltpu.make_async_remote_copy(
                    src_ref=srcs[w], dst_ref=lands[w].at[4 * mx + 2 * my + mc], send_sem=send_sems.at[7 * w + k],
                    recv_sem=recv_sems.at[7 * w + k], device_id=peer, device_id_type=MESH))
        return copies
    if kind == "pair":
        for w in range(n):
            h = srcs[w].shape[1] // 2
            copies.append(pltpu.make_async_remote_copy(
                src_ref=srcs[w].at[:, pl.ds((1 - mc) * h, h), :], dst_ref=lands[w], send_sem=send_sems.at[w],
                recv_sem=recv_sems.at[w], device_id=(mx, my, 1 - mc), device_id_type=MESH))
        return copies
    for k, (px, py) in enumerate([(1 - mx, my), (mx, 1 - my), (1 - mx, 1 - my)]):
        for w in range(n):
            if kind == "gather":
                h = srcs[w].shape[0] // 2
                src, dst = srcs[w].at[pl.ds(mc * h, h), :], lands[w].at[j_me, pl.ds(mc * h, h), :]
            else:
                src, dst = srcs[w].at[2 * px + py], lands[w].at[k]
            copies.append(pltpu.make_async_remote_copy(
                src_ref=src, dst_ref=dst, send_sem=send_sems.at[3 * w + k], recv_sem=recv_sems.at[3 * w + k],
                device_id=(px, py, mc), device_id_type=MESH))
    return copies


_SEMS_PER_OPERAND = {"gather": 3, "scatter": 3, "all": 7, "pair": 1}


def _ici_start(kind, srcs, land_shapes, carry, name):
    n = len(srcs)

    def body(*refs):
        ins, lands = refs[:n], refs[n:2 * n]
        send_sems, recv_sems = refs[2 * n + 1], refs[2 * n + 2]
        for cp in _ici_copies(kind, ins, lands, send_sems, recv_sems):
            cp.start()

    hbm = lambda a: pltpu.with_memory_space_constraint(a, pltpu.HBM)
    lands = [lax.empty(s, srcs[0].dtype) for s in land_shapes]
    args = [hbm(a) for a in list(srcs) + lands + [carry]]
    n_sem = _SEMS_PER_OPERAND[kind] * n
    out_shape = ([pltpu.SemaphoreType.DMA((n_sem,)), pltpu.SemaphoreType.DMA((n_sem,))]
                 + [pltpu.HBM(a.shape, a.dtype) for a in args])
    res = pl.pallas_call(
        body, name=name, out_shape=out_shape, in_specs=[_HBM] * len(args), out_specs=[_SEM, _SEM] + [_HBM] * len(args),
        input_output_aliases={i: 2 + i for i in range(len(args))},
        compiler_params=pltpu.CompilerParams(has_side_effects=_EFFECT))(*args)
    return res[0], res[1], list(res[2:2 + n]), list(res[2 + n:2 + 2 * n]), res[2 + 2 * n]


def _ici_wait(kind, send_sems, recv_sems, srcs, lands, after, name):
    n = len(srcs)

    def body(*refs):
        ins, zones = refs[:n], refs[n:2 * n]
        for cp in _ici_copies(kind, ins, zones, refs[2 * n], refs[2 * n + 1]):
            cp.wait_send()
            cp.wait_recv()

    args = list(srcs) + list(lands)
    res = pl.pallas_call(
        body, name=name, out_shape=[pltpu.HBM(a.shape, a.dtype) for a in args],
        in_specs=[_HBM] * len(args) + [_SEM, _SEM, pl.BlockSpec(memory_space=pl.ANY)], out_specs=[_HBM] * len(args),
        input_output_aliases={i: i for i in range(len(args))},
        compiler_params=pltpu.CompilerParams(has_side_effects=_EFFECT))(*args, send_sems, recv_sems, after)
    return list(res[:n]), list(res[n:])


def _gather_finish(shards, lands):
    n = len(shards)

    def body(*refs):
        own, outs = refs[:n], refs[2 * n:3 * n]
        send_sems, recv_sems = refs[3 * n:]
        mx, my, mc = lax.axis_index("x"), lax.axis_index("y"), lax.axis_index("c")
        j_me = 2 * mx + my
        sibling = (mx, my, 1 - mc)
        copies = []

        def push(w, k, src, dst):
            cp = pltpu.make_async_remote_copy(src_ref=src, dst_ref=dst, send_sem=send_sems.at[w, k],
                                              recv_sem=recv_sems.at[w, k], device_id=sibling, device_id_type=MESH)
            cp.start()
            copies.append(cp)

        for w in range(n):
            h = shards[w].shape[0] // 2
            push(w, 3, own[w], outs[w].at[j_me])
            for k, (px, py) in enumerate([(1 - mx, my), (mx, 1 - my), (1 - mx, 1 - my)]):
                got = outs[w].at[2 * px + py, pl.ds(mc * h, h), :]
                push(w, k, got, got)
        for cp in copies:
            cp.wait()

    return pl.pallas_call(
        body, out_shape=[jax.ShapeDtypeStruct(l.shape, l.dtype) for l in lands],
        in_specs=_hbm_specs(2 * n), out_specs=_hbm_specs(n), input_output_aliases={n + w: w for w in range(n)},
        scratch_shapes=[pltpu.SemaphoreType.DMA((n, 4)), pltpu.SemaphoreType.DMA((n, 4))], name="gather_finish",
    )(*shards, *lands)


def _tile_rows(h, c, itemsize, mult):
    best = h
    for t in range(mult, h + 1, mult):
        if h % t == 0 and t * c * itemsize <= (1 << 21):
            best = t
    return best


def _add_pair(g, land, place, name):
    _, h, c = land.shape
    t = _tile_rows(h, c, 2, 16)
    nb = h // t
    return _ew(lambda ids, u, v: (u.astype(F32) + v.astype(F32),), (4, nb),
               [(g, pl.BlockSpec((None, t, c), lambda j, i, s: (j, s[1] * nb + i, 0))),
                (land, pl.BlockSpec((None, t, c), lambda j, i, s: (j, i, 0)))],
               [(land.shape, BF16, pl.BlockSpec((None, t, c), lambda j, i, s: (j, i, 0)), None)], name, scalars=place)[0]


def _add_chips(own, land, place, name):
    _, h, c = land.shape
    t = _tile_rows(h, c, 4, 16)
    nb = h // t

    def fn(ids, a, b):
        return (((a.astype(F32) + b[0].astype(F32)) + b[1].astype(F32)) + b[2].astype(F32),)

    return _ew(fn, (nb,), [(own, pl.BlockSpec((None, t, c), lambda i, s: (s[0], i, 0))),
                           (land, pl.BlockSpec((3, t, c), lambda i, s: (0, i, 0)))],
               [((2 * h, c), F32, pl.BlockSpec((t, c), lambda i, s: (s[1] * nb + i, 0)), None)], name, scalars=place)[0]


W_IN_SEGMENTS = ((0, 256, KV0), (256, 288, KR0 + 64), (288, 672, Q0), (672, 1184, CX0), (1184, 1696, CB0),
                 (1696, 2208, CC0), (2208, 3232, GA0), (3232, 4256, GC0))
W_IN_SHARD = 1064


W_IN_SHARD_PAD = 1088


def _w_in_t_p_from_shards(s):
    pieces = []
    for o0, o1, p0 in sorted(W_IN_SEGMENTS, key=lambda t: t[2]):
        if p0 == KR0 + 64:
            pieces.append(jnp.zeros((64, s.shape[2]), s.dtype))
        for j in range(4):
            lo, hi = max(o0, j * W_IN_SHARD), min(o1, (j + 1) * W_IN_SHARD)
            if lo < hi:
                pieces.append(s[j, lo - j * W_IN_SHARD:hi - j * W_IN_SHARD])
    pieces.append(jnp.zeros((32, s.shape[2]), s.dtype))
    return jnp.concatenate(pieces, axis=0)


def _w_in_t_shards_from_p(g):
    shards = []
    for j in range(4):
        pieces = []
        for o0, o1, p0 in W_IN_SEGMENTS:
            lo, hi = max(o0, j * W_IN_SHARD), min(o1, (j + 1) * W_IN_SHARD)
            if lo < hi:
                pieces.append(g[p0 + lo - o0:p0 + hi - o0])
        pieces.append(jnp.zeros((W_IN_SHARD_PAD - W_IN_SHARD, g.shape[1]), g.dtype))
        shards.append(jnp.concatenate(pieces, axis=0))
    return jnp.stack(shards, axis=0)


def _cols_from_shards(s):
    return jnp.transpose(s, (1, 0, 2)).reshape(s.shape[1], -1)


def _rope_tables(T, TT, inverse):
    rows = T // GRID_W
    row = jnp.repeat(jnp.arange(rows), GRID_W).astype(F32)
    col = jnp.tile(jnp.arange(GRID_W), rows).astype(F32)
    inv = ROPE_THETA ** (-jnp.arange(0, 16, 2, dtype=F32) / 16)
    ang = jnp.concatenate([row[:, None] * inv, col[:, None] * inv], axis=-1)
    cos, sin = jnp.cos(ang), jnp.sin(ang)
    lane = jnp.arange(32)
    src = (lane // 16) * 8 + lane % 8
    lo = ((lane % 16) // 8 == 0).astype(F32)
    sgn = -1.0 if inverse else 1.0
    cos32 = cos[:, src]
    sin_lo32 = -sgn * sin[:, src] * lo
    sin_hi32 = sgn * sin[:, src] * (1.0 - lo)

    def widen(t32, fill):
        t = jnp.concatenate([jnp.full((T, 64), fill, F32), t32, jnp.full((T, 32), fill, F32)], axis=1)
        return jnp.concatenate([t, jnp.full((TT - T, HEAD_PAD), fill, F32)], axis=0)

    return widen(cos32, 1.0), widen(sin_lo32, 0.0), widen(sin_hi32, 0.0)


def _local_step(xx, tgt, mod_lat, mod_ctx, W, late_weights, early_grads):
    TT = xx.shape[0]
    T = tgt.shape[0]
    n_lat, n_all = T // ROW_TILE, TT // ROW_TILE
    sh1, sc1, g1, sh2, sc2, g2 = [mod_lat[:, k * D_MODEL:(k + 1) * D_MODEL] for k in range(6)]
    csh1, csc1 = mod_ctx[:, :D_MODEL], mod_ctx[:, D_MODEL:2 * D_MODEL]
    vec = lambda n: _full((1, n))
    row_out = lambda n, dt, rows=T: ((rows, n), dt, _rows(n), None)
    acc_out = lambda n: ((1, n), F32, _full((1, n)), 0)

    def f_norm1(ids, x, g, a_sh, a_sc, b_sh, b_sc):
        ctx = ids[0] >= n_lat
        sh, sc = jnp.where(ctx, b_sh, a_sh), jnp.where(ctx, b_sc, a_sc)
        return ((x * _rms(x) * g) * (1.0 + sc) + sh,)

    (hh,) = _ew(f_norm1, (n_all,), [(xx, _rows(D_MODEL)), (W["norm1_g"], vec(D_MODEL)), (sh1, vec(D_MODEL)),
                                   (sc1, vec(D_MODEL)), (csh1, vec(D_MODEL)), (csc1, vec(D_MODEL))],
                [row_out(D_MODEL, BF16, TT)], "norm1_fwd")
    tm_all = _pick(TT, (768, 256))
    pp = _mm(hh, W["w_in_t"], "nt", TT, P_COLS, D_MODEL, tm=tm_all, tn=2176, tk=D_MODEL, name="w_in_fwd")

    def f_lowrank(ids, ckv, cq, gkv, gq):
        return ckv * _rms(ckv) * gkv, cq * _rms(cq) * gq

    nkv, nq = _ew(f_lowrank, (n_all,), [(pp, _rows(KV_RANK, KV0 // KV_RANK)), (pp, _rows(Q_RANK, Q0 // Q_RANK)),
                                       (W["kv_norm_g"], vec(KV_RANK)), (W["q_norm_g"], vec(Q_RANK))],
                  [row_out(KV_RANK, BF16, TT), row_out(Q_RANK, BF16, TT)], "lowrank_norm_fwd")
    kv = _mm(nkv, W["w_ukv"], "nn", TT, 1024, KV_RANK, tm=tm_all, tn=256, tk=KV_RANK, name="w_ukv_fwd",
             b_spec=pl.BlockSpec((None, KV_RANK, 256), lambda i, j, k: (j, k, 0)))
    q_raw = _mm(nq, W["w_uq_t"], "nt", TT, 1024, Q_RANK, tm=tm_all, tn=1024, tk=Q_RANK, name="w_uq_fwd")

    tabs = _rope_tables(T, TT, inverse=False)
    tabs_inv = _rope_tables(T, TT, inverse=True)
    o_pad, lse = _attn_fwd(q_raw, kv, pp, tabs, T, TT)
    W = dict(W, **late_weights(o_pad))
    tm_lat = _pick(T, (1024, 512, 256))
    ya = _mm(o_pad, W["w_attn_out"], "nn", T, D_MODEL, 1024, tm=tm_lat, tn=D_MODEL, tk=1024, name="w_attn_out_fwd")

    tc = 256
    colT = lambda blk0: pl.BlockSpec((T, tc), lambda j: (0, blk0 + j))

    def f_conv(ids, xin, cb, cc, w, b):
        return (cb * _conv(cc * xin, w, b),)

    (e,) = _ew(f_conv, (CONV_DIM // tc,),
               [(pp, colT(CX0 // tc)), (pp, colT(CB0 // tc)), (pp, colT(CC0 // tc)),
                (W["conv_w"], pl.BlockSpec((3, tc), lambda j: (0, j))), (W["conv_b"], pl.BlockSpec((1, tc), lambda j: (0, j)))],
               [((T, CONV_DIM), BF16, colT(0), None)], "conv_fwd")
    yc = _mm(e, W["w_conv_out"], "nn", T, D_MODEL, CONV_DIM, tm=tm_lat, tn=256, tk=CONV_DIM, name="w_conv_out_fwd",
             b_spec=pl.BlockSpec((None, CONV_DIM, 256), lambda i, j, k: (j, k, 0)))

    def f_merge(ids, ga, gc, a, c):
        return (_sigmoid(ga) * a + _sigmoid(gc) * c,)

    (mrg,) = _ew(f_merge, (n_lat,), [(pp, _rows(D_MODEL, 0)), (pp, _rows(D_MODEL, 1)), (ya, _rows(D_MODEL)),
                                    (yc, _rows(D_MODEL))], [row_out(D_MODEL, BF16)], "merge_fwd")
    mo = _mm(mrg, W["w_o"], "nn", T, D_MODEL, D_MODEL, tm=tm_lat, tn=D_MODEL, tk=D_MODEL, name="w_o_fwd")

    def f_norm2(ids, x, m, gate, g, sh, sc):
        x1 = x + gate * m
        return x1, (x1 * _rms(x1) * g) * (1.0 + sc) + sh

    x1, h2 = _ew(f_norm2, (n_lat,), [(xx, _rows(D_MODEL)), (mo, _rows(D_MODEL)), (g1, vec(D_MODEL)),
                                    (W["norm2_g"], vec(D_MODEL)), (sh2, vec(D_MODEL)), (sc2, vec(D_MODEL))],
                 [row_out(D_MODEL, F32), row_out(D_MODEL, BF16)], "norm2_fwd")
    up = _mm(h2, W["w_up"], "nn", T, 2 * D_FF, D_MODEL, tm=tm_lat, tn=1408, tk=D_MODEL, name="w_up_fwd",
             b_spec=pl.BlockSpec((None, D_MODEL, 1408), lambda i, j, k: (j, k, 0)))

    n_ff = D_FF // tc
    ffw = lambda off, n=3: pl.BlockSpec((n, tc), lambda j: (0, j + off))

    def f_ffn(ids, ug, uv, wg, wv, bg, bv):
        gate, val = _conv(ug, wg, bg), _conv(uv, wv, bv)
        return (gate * _sigmoid(gate) * val,)

    (act,) = _ew(f_ffn, (n_ff,), [(up, colT(0)), (up, colT(n_ff)), (W["ffn_conv_w"], ffw(0)), (W["ffn_conv_w"], ffw(n_ff)),
                                 (W["ffn_conv_b"], ffw(0, 1)), (W["ffn_conv_b"], ffw(n_ff, 1))],
                 [((T, D_FF), BF16, colT(0), None)], "ffn_act_fwd")
    f = _mm(act, W["w_down"], "nn", T, D_MODEL, D_FF, tm=tm_lat, tn=D_MODEL, tk=D_FF, name="w_down_fwd")

    def f_head(ids, x1_, f_, gate, gf, t):
        x2 = x1_ + gate * f_
        r = _rms(x2)
        xn = x2 * r
        err = xn * gf - t
        loss = 0.5 * jnp.sum(jnp.mean(err * err, axis=-1, keepdims=True))
        dy = err * (1.0 / D_MODEL)
        dx2 = _rms_bwd(dy * gf, xn, r)
        return dx2, dx2 * gate, _colsum(dy * xn), _colsum(dx2 * f_), jnp.full((1, 128), loss, F32)

    dx2, df, dg_f, dg2, loss = _ew(
        f_head, (n_lat,), [(x1, _rows(D_MODEL)), (f, _rows(D_MODEL)), (g2, vec(D_MODEL)), (W["final_g"], vec(D_MODEL)),
                           (tgt, _rows(D_MODEL))],
        [row_out(D_MODEL, F32), row_out(D_MODEL, BF16), acc_out(D_MODEL), acc_out(D_MODEL), acc_out(128)], "loss_head")

    d_w_down = _mm(act, df, "tn", D_FF, D_MODEL, T, tm=1408, tn=D_MODEL, tk=T, name="w_down_dw",
                   out_dtype=BF16).reshape(4, D_FF // 4, D_MODEL)
    da = _mm(df, W["w_down"], "nt", T, D_FF, D_MODEL, tm=tm_lat, tn=1408, tk=D_MODEL, name="w_down_dx")

    tcb = 128
    n_fb = D_FF // tcb
    colb = lambda blk0: pl.BlockSpec((T, tcb), lambda j: (0, blk0 + j))
    ffwb = lambda off, n=3: pl.BlockSpec((n, tcb), lambda j: (0, j + off))
    cvec = ((1, D_FF), F32, pl.BlockSpec((1, tcb), lambda j: (0, j)), None)

    def f_ffn_bwd(ids, ug, uv, d_act, wg, wv, bg, bv):
        gate, val = _conv(ug, wg, bg), _conv(uv, wv, bv)
        s = _sigmoid(gate)
        d_gate = d_act * val * s * (1.0 + gate * (1.0 - s))
        d_val = d_act * gate * s
        wg0, wg1, wg2 = _conv_bwd_w(d_gate, ug)
        wv0, wv1, wv2 = _conv_bwd_w(d_val, uv)
        d_up = [_conv_bwd_x(d_gate, wg), _conv_bwd_x(d_val, wv)]
        return d_up, [_colsum(d_gate), _colsum(d_val), wg0, wg1, wg2, wv0, wv1, wv2]

    d_up3, ffn_stats = _ew(
        f_ffn_bwd, (n_fb,),
        [(up, colb(0)), (up, colb(n_fb)), (da, colb(0)), (W["ffn_conv_w"], ffwb(0)), (W["ffn_conv_w"], ffwb(n_fb)),
         (W["ffn_conv_b"], ffwb(0, 1)), (W["ffn_conv_b"], ffwb(n_fb, 1))],
        [((2, T, D_FF), BF16, pl.BlockSpec((2, T, tcb), lambda j: (0, 0, j)), None),
         ((n_fb, 8, 1, tcb), F32, pl.BlockSpec((None, 8, 1, tcb), lambda j: (j, 0, 0, 0)), None)], "ffn_act_bwd")
    stat = lambda s: ffn_stats[:, s, 0, :].reshape(1, D_FF)
    d_ffn_conv_b = jnp.concatenate([stat(0), stat(1)], axis=1)
    d_ffn_conv_w = jnp.concatenate([jnp.concatenate([stat(2), stat(3), stat(4)], axis=0),
                                    jnp.concatenate([stat(5), stat(6), stat(7)], axis=0)], axis=1)

    tk_t = T
    d_w_up = _mm(h2, d_up3, "tn", D_MODEL, 2 * D_FF, T, tm=D_MODEL, tn=1408, tk=tk_t, name="w_up_dw", out_dtype=BF16,
                 b_spec=pl.BlockSpec((None, tk_t, 1408), lambda i, j, k: (j // 2, k, j % 2)),
                 o_spec=pl.BlockSpec((None, D_MODEL, 1408), lambda i, j, k: (j, i, 0)), out_shape=(4, D_MODEL, 1408))
    dh2 = _mm(d_up3, W["w_up"], "nt", T, D_MODEL, 2 * D_FF, tm=tm_lat, tn=D_MODEL, tk=1408, name="w_up_dx",
              a_spec=pl.BlockSpec((None, tm_lat, 1408), lambda i, j, k: (k // 2, i, k % 2)),
              b_spec=pl.BlockSpec((None, D_MODEL, 1408), lambda i, j, k: (k, j, 0)))

    def f_norm2_bwd(ids, dx2_, dh, x1_, m, g, sc, gate):
        r = _rms(x1_)
        xn = x1_ * r
        dx1 = dx2_ + _rms_bwd(dh * g * (1.0 + sc), xn, r)
        return dx1, dx1 * gate, _colsum(dh), _colsum(dh * xn * g), _colsum(dh * xn * (1.0 + sc)), _colsum(dx1 * m)

    dx1, dmo, dsh2, dsc2, dg_n2, dg1 = _ew(
        f_norm2_bwd, (n_lat,), [(dx2, _rows(D_MODEL)), (dh2, _rows(D_MODEL)), (x1, _rows(D_MODEL)), (mo, _rows(D_MODEL)),
                                (W["norm2_g"], vec(D_MODEL)), (sc2, vec(D_MODEL)), (g1, vec(D_MODEL))],
        [row_out(D_MODEL, F32), row_out(D_MODEL, BF16)] + [acc_out(D_MODEL)] * 4, "norm2_bwd")
    d_w_o = _mm(mrg, dmo, "tn", D_MODEL, D_MODEL, T, tm=D_MODEL, tn=D_MODEL, tk=tk_t, name="w_o_dw",
                out_dtype=BF16).reshape(4, D_MODEL // 4, D_MODEL)
    dmrg = _mm(dmo, W["w_o"], "nt", T, D_MODEL, D_MODEL, tm=tm_lat, tn=D_MODEL, tk=D_MODEL, name="w_o_dx")
    dmrg = early_grads("late", {"w_o": d_w_o, "w_up": d_w_up, "w_down": d_w_down}, dmrg)

    def f_merge_bwd(ids, dm, ga, gc, a, c):
        sa, sc_ = _sigmoid(ga), _sigmoid(gc)
        return dm * sa, dm * sc_, dm * a * sa * (1.0 - sa), dm * c * sc_ * (1.0 - sc_)

    dya, dyc, dp_ga, dp_gc = _ew(
        f_merge_bwd, (n_lat,), [(dmrg, _rows(D_MODEL)), (pp, _rows(D_MODEL, 0)), (pp, _rows(D_MODEL, 1)),
                                (ya, _rows(D_MODEL)), (yc, _rows(D_MODEL))], [row_out(D_MODEL, BF16)] * 4, "merge_bwd")

    d_w_ao_p = _mm(o_pad, dya, "tn", 1024, D_MODEL, T, tm=1024, tn=D_MODEL, tk=tk_t, name="w_attn_out_dw", out_dtype=BF16)
    do_pad = _mm(dya, W["w_attn_out"], "nt", T, 1024, D_MODEL, tm=tm_lat, tn=1024, tk=D_MODEL, name="w_attn_out_dx")
    d_w_co = _mm(e, dyc, "tn", CONV_DIM, D_MODEL, T, tm=CONV_DIM, tn=256, tk=tk_t, name="w_conv_out_dw", out_dtype=BF16,
                 o_spec=pl.BlockSpec((None, CONV_DIM, 256), lambda i, j, k: (j, i, 0)), out_shape=(4, CONV_DIM, 256))
    de = _mm(dyc, W["w_conv_out"], "nt", T, CONV_DIM, D_MODEL, tm=tm_lat, tn=CONV_DIM, tk=256, name="w_conv_out_dx",
             b_spec=pl.BlockSpec((None, CONV_DIM, 256), lambda i, j, k: (k, j, 0)))

    def f_conv_bwd(ids, xin, cb, cc, d_e, w, b):
        z = cc * xin
        cz = _conv(z, w, b)
        dcz = d_e * cb
        w0, w1, w2 = _conv_bwd_w(dcz, z)
        dz = _conv_bwd_x(dcz, w)
        return dz * cc, d_e * cz, dz * xin, _colsum(dcz), w0, w1, w2

    cvec_c = ((1, CONV_DIM), F32, pl.BlockSpec((1, tc), lambda j: (0, j)), None)
    conv_b = _ew(f_conv_bwd, (CONV_DIM // tc,),
                 [(pp, colT(CX0 // tc)), (pp, colT(CB0 // tc)), (pp, colT(CC0 // tc)), (de, colT(0)),
                  (W["conv_w"], pl.BlockSpec((3, tc), lambda j: (0, j))), (W["conv_b"], pl.BlockSpec((1, tc), lambda j: (0, j)))],
                 [((T, CONV_DIM), BF16, colT(0), None)] * 3 + [cvec_c] * 4, "conv_bwd")
    dp_cx, dp_cb, dp_cc, d_conv_b = conv_b[:4]
    d_conv_w = jnp.concatenate(conv_b[4:7], axis=0)

    dq_raw, dkv, dp_kr = _attn_bwd(q_raw, kv, pp, o_pad, do_pad, lse, tabs, tabs_inv, T, TT)

    tk_a = TT
    d_w_uq_t = _mm(nq, dq_raw, "tn", Q_RANK, 1024, TT, tm=Q_RANK, tn=1024, tk=tk_a, name="w_uq_dw", transpose_out=True)
    dnq = _mm(dq_raw, W["w_uq_t"], "nn", TT, Q_RANK, 1024, tm=tm_all, tn=Q_RANK, tk=1024, name="w_uq_dx")
    d_w_ukv = _mm(nkv, dkv, "tn", KV_RANK, 1024, TT, tm=KV_RANK, tn=256, tk=tk_a, name="w_ukv_dw", out_dtype=BF16,
                  o_spec=pl.BlockSpec((None, KV_RANK, 256), lambda i, j, k: (j, i, 0)), out_shape=(4, KV_RANK, 256))
    dnkv = _mm(dkv, W["w_ukv"], "nt", TT, KV_RANK, 1024, tm=tm_all, tn=KV_RANK, tk=256, name="w_ukv_dx",
               b_spec=pl.BlockSpec((None, KV_RANK, 256), lambda i, j, k: (k, j, 0)))
    dnkv = early_grads("mid", {
        "w_attn_out": jnp.transpose(d_w_ao_p.reshape(N_HEADS, HEAD_PAD, 4, 256)[:, 64:], (2, 0, 1, 3)).reshape(
            4, N_HEADS * 64, 256),
        "w_conv_out": d_w_co,
        "w_uq": d_w_uq_t.reshape(4, 2, HEAD_PAD, Q_RANK)[:, :, :QK_DIM].reshape(4, 2 * QK_DIM, Q_RANK).astype(BF16),
        "w_ukv": d_w_ukv}, dnkv)

    def f_lowrank_bwd(ids, ckv, cq, dkv_, dq_, gkv, gq):
        rk, rq = _rms(ckv), _rms(cq)
        nk, nq_ = ckv * rk, cq * rq
        return (_rms_bwd(dkv_ * gkv, nk, rk), _rms_bwd(dq_ * gq, nq_, rq), _colsum(dkv_ * nk), _colsum(dq_ * nq_))

    dp_kv, dp_q, dg_kv, dg_q = _ew(
        f_lowrank_bwd, (n_all,), [(pp, _rows(KV_RANK, KV0 // KV_RANK)), (pp, _rows(Q_RANK, Q0 // Q_RANK)),
                                  (dnkv, _rows(KV_RANK)), (dnq, _rows(Q_RANK)), (W["kv_norm_g"], vec(KV_RANK)),
                                  (W["q_norm_g"], vec(Q_RANK))],
        [row_out(KV_RANK, BF16, TT), row_out(Q_RANK, BF16, TT), acc_out(KV_RANK), acc_out(Q_RANK)], "lowrank_norm_bwd")

    lat_cols = jnp.concatenate([dp_ga, dp_gc, dp_cx, dp_cb, dp_cc], axis=1)
    dpp = jnp.concatenate([jnp.pad(lat_cols, ((0, TT - T), (0, 0))), dp_kv, dp_q, dp_kr.astype(BF16)], axis=1)
    d_w_in_t = _mm(hh, dpp, "tn", D_MODEL, P_COLS, TT, tm=512, tn=2176, tk=TT, name="w_in_dw", out_dtype=BF16,
                   transpose_out=True)
    dhh = _mm(dpp, W["w_in_t"], "nn", TT, D_MODEL, P_COLS, tm=tm_all, tn=512, tk=2176, name="w_in_dx")

    def f_norm1_bwd(ids, x, dh, dres, g, sc):
        r = _rms(x)
        xn = x * r
        return (dres + _rms_bwd(dh * g * (1.0 + sc), xn, r), _colsum(dh), _colsum(dh * xn * g),
                _colsum(dh * xn * (1.0 + sc)))

    grad_x, dsh1, dsc1, dg_n1 = _ew(
        f_norm1_bwd, (n_lat,), [(xx, _rows(D_MODEL)), (dhh, _rows(D_MODEL)), (dx1, _rows(D_MODEL)),
                                (W["norm1_g"], vec(D_MODEL)), (sc1, vec(D_MODEL))],
        [row_out(D_MODEL, F32)] + [acc_out(D_MODEL)] * 3, "norm1_bwd")

    def f_norm1_ctx_bwd(ids, x, dh, g, sc):
        xn = x * _rms(x)
        return _colsum(dh), _colsum(dh * xn * g), _colsum(dh * xn * (1.0 + sc))

    n_ctx = n_all - n_lat
    dcsh1, dcsc1, dg_n1c = _ew(
        f_norm1_ctx_bwd, (n_ctx,), [(xx, _rows(D_MODEL, 0, n_lat)), (dhh, _rows(D_MODEL, 0, n_lat)),
                                    (W["norm1_g"], vec(D_MODEL)), (csc1, vec(D_MODEL))], [acc_out(D_MODEL)] * 3,
        "norm1_ctx_bwd")

    big = {"w_in": _w_in_t_shards_from_p(d_w_in_t).astype(BF16)}
    zero = jnp.zeros((1, 4 * D_MODEL), F32)
    small = {
        "dmod_lat": jnp.concatenate([dsh1, dsc1, dg1, dsh2, dsc2, dg2], axis=1),
        "dmod_ctx": jnp.concatenate([dcsh1, dcsc1, zero], axis=1),
        "norm1_g": dg_n1 + dg_n1c, "norm2_g": dg_n2, "final_g": dg_f, "q_norm_g": dg_q, "kv_norm_g": dg_kv,
        "conv_b": d_conv_b, "conv_w": d_conv_w.reshape(1, -1), "ffn_conv_b": d_ffn_conv_b,
        "ffn_conv_w": d_ffn_conv_w.reshape(1, -1),
    }
    return grad_x, loss, big, small


SMALL = (("dmod_lat", 6144), ("dmod_ctx", 6144), ("norm1_g", 1024), ("norm2_g", 1024), ("final_g", 1024),
         ("q_norm_g", 384), ("kv_norm_g", 256), ("conv_b", 512), ("conv_w", 1536), ("ffn_conv_b", 5632),
         ("ffn_conv_w", 16896), ("loss", 128))
SMALL_ROWS = 320


def _adam_update(w, g, m, v):
    c1, c2 = 1.0 - ADAM_B1 ** ADAM_STEP, 1.0 - ADAM_B2 ** ADAM_STEP
    m2 = ADAM_B1 * m + (1.0 - ADAM_B1) * g
    v2 = ADAM_B2 * v + (1.0 - ADAM_B2) * (g * g)
    return [-ADAM_LR * ((m2 / c1) / (jnp.sqrt(v2 / c2) + ADAM_EPS) + ADAM_WD * w), m2, v2]


def _adamw(w, g, m, v, name):
    R, C = w.shape
    tr = 8 if R % 8 == 0 else R
    for t in range(8, R + 1, 8):
        if R % t == 0 and t * C * 4 <= (1 << 20):
            tr = t
    spec = pl.BlockSpec((tr, C), lambda i: (i, 0))
    return _ew(lambda ids, *vals: _adam_update(*vals), (R // tr,), [(w, spec), (g, spec), (m, spec), (v, spec)],
               [((R, C), F32, spec, None)] * 3, name)


def kernel(x, c, ctx, c_ctx, w_ada, b_ada, norm1_g, w_in, q_norm_g, kv_norm_g, w_uq, w_ukv, conv_w, conv_b, w_attn_out, w_conv_out, w_o, norm2_g, w_up, ffn_conv_w, ffn_conv_b, w_down, final_g, loss_target, m_c_ctx, m_w_ada, m_b_ada, m_norm1_g, m_w_in, m_q_norm_g, m_kv_norm_g, m_w_uq, m_w_ukv, m_conv_w, m_conv_b, m_w_attn_out, m_w_conv_out, m_w_o, m_norm2_g, m_w_up, m_ffn_conv_w, m_ffn_conv_b, m_w_down, m_final_g, v_c_ctx, v_w_ada, v_b_ada, v_norm1_g, v_w_in, v_q_norm_g, v_kv_norm_g, v_w_uq, v_w_ukv, v_conv_w, v_conv_b, v_w_attn_out, v_w_conv_out, v_w_o, v_norm2_g, v_w_up, v_ffn_conv_w, v_ffn_conv_b, v_w_down, v_final_g):
    mx, my, mc = lax.axis_index("x"), lax.axis_index("y"), lax.axis_index("c")
    chip = 2 * mx + my
    dev = 4 * mx + 2 * my + mc
    T, Tc = x.shape[1], ctx.shape[1]
    TT = T + Tc
    w_in_t, m_w_in_t, v_w_in_t = (jnp.transpose(a[0]) for a in (w_in, m_w_in, v_w_in))
    w_uq_t, m_w_uq_t, v_w_uq_t = (jnp.transpose(a[0]) for a in (w_uq, m_w_uq, v_w_uq))
    shards = {"w_in": jnp.pad(w_in_t, ((0, W_IN_SHARD_PAD - W_IN_SHARD), (0, 0))), "w_uq": w_uq_t, "w_ukv": w_ukv[0],
              "w_attn_out": w_attn_out[0], "w_conv_out": w_conv_out[0], "w_o": w_o[0], "w_up": w_up[0],
              "w_down": w_down[0]}

    conv_sh = jnp.concatenate([conv_w[0], ffn_conv_w[0]], axis=1)
    pay1 = jnp.concatenate([jnp.pad(c, ((0, 7), (0, 0))), jnp.pad(conv_sh, ((0, 5), (0, 0)))], axis=1)
    got1 = _allgather8(pay1, "gather_cond", in_vmem=True)[0].reshape(8, 8, 2560)
    c_all = got1[:, 0, :D_MODEL]
    conv_all = got1[0::2, :3, D_MODEL:]
    conv_w_full = _cols_from_shards(conv_all[:, :, :128])
    ffn_conv_w_full = _cols_from_shards(conv_all[:, :, 128:])

    cond = jnp.concatenate([c_all, c_ctx.reshape(1, D_MODEL), jnp.zeros((7, D_MODEL), F32)], axis=0)

    def f_silu(ids, v):
        return (v * _sigmoid(v),)

    (s16,) = _ew(f_silu, (1,), [(cond, _full((16, D_MODEL)))], [((16, D_MODEL), F32, _full((16, D_MODEL)), None)], "silu_cond")
    mod_sh = _mm(s16, w_ada[0], "nn", 16, 1536, D_MODEL, tm=16, tn=768, tk=D_MODEL, name="w_ada_fwd")
    m_send, m_recv, m_src, m_land, ukv_thru = _ici_start("all", [mod_sh], [(8, 16, 1536)], w_ukv[0], "mod_start")
    shards["w_ukv"] = ukv_thru

    names = [n for n, _ in BIG]
    first = [n for n in names if n not in GATHER_LATE]
    gathered, zero = _gather_weights([shards[n].astype(BF16) for n in first])
    full = dict(zip(first, gathered))
    (mod_mine,), (m_land,) = _ici_wait("all", m_send, m_recv, m_src, m_land, gathered[0], "mod_wait")
    got2 = lax.dynamic_update_slice(m_land, mod_mine[None], (dev, 0, 0))
    mod_all = _cols_from_shards(got2[0::2]) + b_ada
    mod_lat = lax.dynamic_slice_in_dim(mod_all, dev, 1, axis=0)
    mod_ctx = mod_all[8:9]
    xx = jnp.concatenate([x[0], ctx[0]], axis=0)
    late_bf = [(shards[n] + zero[0, 0]).astype(BF16) for n in GATHER_LATE]
    g_send, g_recv, late_src, late_land, xx = _ici_start(
        "gather", late_bf, [(4,) + s.shape for s in late_bf], xx, "gather_late_start")

    def late_weights(after):
        src, land = _ici_wait("gather", g_send, g_recv, late_src, late_land, after, "gather_late_wait")
        got = dict(zip(GATHER_LATE, _gather_finish(src, land)))
        wao = _cols_from_shards(got["w_attn_out"]).reshape(N_HEADS, 64, D_MODEL)
        return {"w_attn_out": jnp.pad(wao, ((0, 0), (64, 0), (0, 0))).reshape(N_HEADS * HEAD_PAD, D_MODEL),
                "w_conv_out": got["w_conv_out"], "w_o": got["w_o"].reshape(D_MODEL, D_MODEL), "w_up": got["w_up"],
                "w_down": got["w_down"].reshape(D_FF, D_MODEL)}

    wuq_t = full["w_uq"].reshape(N_HEADS, QK_DIM, Q_RANK)
    W = {
        "w_in_t": _w_in_t_p_from_shards(full["w_in"]),
        "w_uq_t": jnp.pad(wuq_t, ((0, 0), (0, HEAD_PAD - QK_DIM), (0, 0))).reshape(N_HEADS * HEAD_PAD, Q_RANK),
        "w_ukv": full["w_ukv"],
        "norm1_g": norm1_g, "norm2_g": norm2_g, "final_g": final_g.reshape(1, D_MODEL), "q_norm_g": q_norm_g,
        "kv_norm_g": kv_norm_g, "conv_w": conv_w_full, "conv_b": conv_b, "ffn_conv_w": ffn_conv_w_full,
        "ffn_conv_b": ffn_conv_b,
    }

    place = jnp.stack([chip, mc]).astype(jnp.int32)
    early = {}

    def early_grads(tag, g, carry):
        gs = list(g.values())
        from_sib = _rs_pair(gs, "rs_pair_" + tag)
        sums = [_add_pair(gs[w], from_sib[w], place, "rs_pair_add_" + n) for w, n in enumerate(g)]
        send, recv, sums, land, carry = _ici_start(
            "scatter", sums, [(3,) + s.shape[1:] for s in sums], carry, "rs_chips_" + tag + "_start")
        early[tag] = (list(g), send, recv, sums, land)
        return carry

    grad_x, loss_part, gbig, gsmall = _local_step(xx, loss_target[0], mod_lat, mod_ctx, W, late_weights, early_grads)

    gsmall["loss"] = loss_part
    pay3 = jnp.concatenate([gsmall[n].reshape(-1) for n, _ in SMALL])
    pay3 = jnp.pad(pay3, (0, SMALL_ROWS * 128 - pay3.shape[0])).reshape(SMALL_ROWS, 128)
    s_send, s_recv, s_src, s_land, w_in_thru = _ici_start("all", [pay3], [(8, SMALL_ROWS, 128)], gbig["w_in"],
                                                         "small_start")
    gbig = {"w_in": w_in_thru}

    last = list(gbig)
    from_sibling = _rs_pair([gbig[n] for n in last], "rs_pair")
    pair_sums = [_add_pair(gbig[n], from_sibling[w], place, "rs_pair_add_" + n) for w, n in enumerate(last)]
    lands = _rs_chips(pair_sums)

    (pay3,), (s_land,) = _ici_wait("all", s_send, s_recv, s_src, s_land, lands[0], "small_wait")
    got3 = lax.dynamic_update_slice(s_land, pay3[None], (dev, 0, 0)).reshape(8 * SMALL_ROWS, 128)

    def f_sum8(ids, a):
        s = a[0:SMALL_ROWS]
        for d in range(1, 8):
            s = s + a[d * SMALL_ROWS:(d + 1) * SMALL_ROWS]
        return (s,)

    (vsum,) = _ew(f_sum8, (1,), [(got3, _full((8 * SMALL_ROWS, 128)))],
                  [((SMALL_ROWS, 128), F32, _full((SMALL_ROWS, 128)), None)], "sum_small")
    vflat = vsum.reshape(-1)
    gvec, off = {}, 0
    for n, size in SMALL:
        gvec[n] = vflat[off:off + size]
        off += size
    loss = gvec["loss"][0]
    dmod_rows = got3.reshape(8, SMALL_ROWS * 128)[:, :6 * D_MODEL]
    dm16 = jnp.concatenate([dmod_rows, gvec["dmod_ctx"].reshape(1, -1), jnp.zeros((7, 6 * D_MODEL), F32)], axis=0)

    def f_colsum(ids, a):
        return (_colsum(a),)

    (g_b_ada,) = _ew(f_colsum, (1,), [(dm16, _full((16, 6 * D_MODEL)))],
                     [((1, 6 * D_MODEL), F32, _full((1, 6 * D_MODEL)), None)], "b_ada_grad")
    dm_sh = lax.dynamic_slice_in_dim(dm16, chip * 1536, 1536, axis=1)
    g_w_ada = _mm(s16, dm_sh, "tn", D_MODEL, 1536, 16, tm=512, tn=768, tk=16, name="w_ada_dw")
    dcond_part = _mm(dm_sh, w_ada[0], "nt", 16, D_MODEL, 1536, tm=16, tn=512, tk=1536, name="w_ada_dx")
    d_send, d_recv, d_src, d_land, lands[0] = _ici_start("all", [dcond_part[8:16]], [(8, 8, D_MODEL)], lands[0],
                                                         "dcond_start")
    done = last
    for tag, (tag_names, send, recv, sums, land) in early.items():
        sums, land = _ici_wait("scatter", send, recv, sums, land, grad_x, "rs_chips_" + tag + "_wait")
        done, pair_sums, lands = done + tag_names, pair_sums + sums, lands + land
    half_sums = [_add_chips(a, b, place, "rs_chip_add_" + n) for a, b, n in zip(pair_sums, lands, done)]
    gw = dict(zip(done, _rs_pair_back(half_sums)))
    gw["w_ada"] = g_w_ada

    moments = {"w_ada": (w_ada, m_w_ada, v_w_ada), "w_ukv": (w_ukv, m_w_ukv, v_w_ukv),
               "w_attn_out": (w_attn_out, m_w_attn_out, v_w_attn_out),
               "w_conv_out": (w_conv_out, m_w_conv_out, v_w_conv_out), "w_o": (w_o, m_w_o, v_w_o),
               "w_up": (w_up, m_w_up, v_w_up), "w_down": (w_down, m_w_down, v_w_down)}
    grads, deltas, new_m, new_v = {}, {}, {}, {}
    for n, (w_, m_, v_) in moments.items():
        d_, m2, v2 = _adamw(w_[0], gw[n], m_[0], v_[0], "adamw_" + n)
        grads[n], deltas[n], new_m[n], new_v[n] = gw[n][None], d_[None], m2[None], v2[None]
    for n, (w_, m_, v_) in {"w_in": (w_in_t, m_w_in_t, v_w_in_t), "w_uq": (w_uq_t, m_w_uq_t, v_w_uq_t)}.items():
        d_, m2, v2 = _adamw(w_, gw[n], m_, v_, "adamw_" + n)
        back = lambda a: jnp.transpose(a)[None]
        grads[n], deltas[n], new_m[n], new_v[n] = back(gw[n][:w_.shape[0]]), back(d_), back(m2), back(v2)

    (dcond_mine,), (d_land,) = _ici_wait("all", d_send, d_recv, d_src, d_land, deltas["w_up"], "dcond_wait")
    got4 = lax.dynamic_update_slice(d_land, dcond_mine[None], (dev, 0, 0))[0::2, 0]

    def f_c_ctx(ids, parts, cc):
        s = _sigmoid(cc)
        d = parts[0:1] + parts[1:2] + parts[2:3] + parts[3:4]
        return (d * s * (1.0 + cc * (1.0 - s)),)

    (g_c_ctx,) = _ew(f_c_ctx, (1,), [(got4, _full((4, D_MODEL))), (c_ctx.reshape(1, D_MODEL), _full((1, D_MODEL)))],
                     [((1, D_MODEL), F32, _full((1, D_MODEL)), None)], "c_ctx_grad")

    conv_w_g = lax.dynamic_slice_in_dim(gvec["conv_w"].reshape(3, CONV_DIM), chip * 128, 128, axis=1)
    ffn_conv_w_g = lax.dynamic_slice_in_dim(gvec["ffn_conv_w"].reshape(3, 2 * D_FF), chip * 1408, 1408, axis=1)
    vec_params = (("c_ctx", c_ctx, m_c_ctx, v_c_ctx, g_c_ctx), ("b_ada", b_ada, m_b_ada, v_b_ada, g_b_ada),
                  ("norm1_g", norm1_g, m_norm1_g, v_norm1_g, gvec["norm1_g"]),
                  ("q_norm_g", q_norm_g, m_q_norm_g, v_q_norm_g, gvec["q_norm_g"]),
                  ("kv_norm_g", kv_norm_g, m_kv_norm_g, v_kv_norm_g, gvec["kv_norm_g"]),
                  ("conv_w", conv_w, m_conv_w, v_conv_w, conv_w_g), ("conv_b", conv_b, m_conv_b, v_conv_b, gvec["conv_b"]),
                  ("norm2_g", norm2_g, m_norm2_g, v_norm2_g, gvec["norm2_g"]),
                  ("ffn_conv_w", ffn_conv_w, m_ffn_conv_w, v_ffn_conv_w, ffn_conv_w_g),
                  ("ffn_conv_b", ffn_conv_b, m_ffn_conv_b, v_ffn_conv_b, gvec["ffn_conv_b"]),
                  ("final_g", final_g, m_final_g, v_final_g, gvec["final_g"]))
    two_d = lambda a: a.reshape((-1, a.shape[-1]))

    def f_adam_many(ids, *vals):
        out = []
        for k in range(len(vec_params)):
            out += _adam_update(*vals[4 * k:4 * k + 4])
        return out

    ins_v, outs_v = [], []
    for p in vec_params:
        shp = two_d(p[1]).shape
        ins_v += [(two_d(a), _full(shp)) for a in (p[1], p[4], p[2], p[3])]
        outs_v += [(shp, F32, _full(shp), None)] * 3
    res_v = _ew(f_adam_many, (1,), ins_v, outs_v, "adamw_vectors")
    for k, p in enumerate(vec_params):
        n, shape = p[0], p[1].shape
        grads[n] = p[4].reshape(shape)
        deltas[n], new_m[n], new_v[n] = (r.reshape(shape) for r in res_v[3 * k:3 * k + 3])

    order = ("c_ctx", "w_ada", "b_ada", "norm1_g", "w_in", "q_norm_g", "kv_norm_g", "w_uq", "w_ukv", "conv_w", "conv_b",
             "w_attn_out", "w_conv_out", "w_o", "norm2_g", "w_up", "ffn_conv_w", "ffn_conv_b", "w_down", "final_g")
    return (loss, grad_x[None], *[grads[n] for n in order], *[deltas[n] for n in order],
            *[new_m[n] for n in order], *[new_v[n] for n in order])
```

```python
import functools

import jax
import jax.numpy as jnp
from jax import lax
from jax.experimental import pallas as pl
from jax.experimental.pallas import tpu as pltpu

F32, BF16 = jnp.float32, jnp.bfloat16
MESH = pl.DeviceIdType.MESH

D_MODEL = 1024
N_HEADS = 8
HEAD_PAD = 128
QK_DIM = 96
Q_RANK, KV_RANK = 384, 256
CONV_DIM = 512
D_FF = 2816
GRID_W = 64
ROPE_THETA = 10000.0
EPS = 1e-6
GA0, GC0, CX0, CB0, CC0, KV0, Q0, KR0, P_COLS = 0, 1024, 2048, 2560, 3072, 3584, 3840, 4224, 4352
ROW_TILE = 256
VMEM_LIMIT_BYTES = 48 * 1024 * 1024

ADAM_LR, ADAM_B1, ADAM_B2, ADAM_EPS, ADAM_WD, ADAM_STEP = 0.001, 0.9, 0.999, 1e-08, 0.01, 10

BIG = (("w_in", (1088, 1024)), ("w_uq", (192, 384)), ("w_ukv", (256, 256)), ("w_attn_out", (512, 256)),
       ("w_conv_out", (512, 256)), ("w_o", (256, 1024)), ("w_up", (1024, 1408)), ("w_down", (704, 1024)))

GATHER_LATE = ("w_attn_out", "w_conv_out", "w_o", "w_up", "w_down")

NN = (((1,), (0,)), ((), ()))
NT = (((1,), (1,)), ((), ()))
TN = (((0,), (0,)), ((), ()))


def _cp(sem):
    return pltpu.CompilerParams(dimension_semantics=sem, vmem_limit_bytes=VMEM_LIMIT_BYTES)


PIN_BYTES = 1 << 19


def _in_hbm(arrays):
    return [pltpu.with_memory_space_constraint(a, pltpu.HBM) if a.size * a.dtype.itemsize >= PIN_BYTES else a
            for a in arrays]


def _out(shape, dtype):
    n = 1
    for d in shape:
        n *= d
    big = n * jnp.dtype(dtype).itemsize >= PIN_BYTES
    return pltpu.HBM(shape, dtype) if big else jax.ShapeDtypeStruct(shape, dtype)


def _pick(n, prefs):
    for p in prefs:
        if n % p == 0:
            return p
    return n


def _mm(a, b, mode, M, N, K, *, tm, tn, tk, name, out_dtype=F32, a_spec=None, b_spec=None, o_spec=None,
        out_shape=None, transpose_out=False):
    assert M % tm == 0 and N % tn == 0 and K % tk == 0, (name, M, N, K, tm, tn, tk)
    nk = K // tk
    dims = {"nn": NN, "nt": NT, "tn": TN}[mode]
    if a_spec is None:
        a_spec = (pl.BlockSpec((tk, tm), lambda i, j, k: (k, i)) if mode == "tn"
                  else pl.BlockSpec((tm, tk), lambda i, j, k: (i, k)))
    if b_spec is None:
        b_spec = (pl.BlockSpec((tn, tk), lambda i, j, k: (j, k)) if mode == "nt"
                  else pl.BlockSpec((tk, tn), lambda i, j, k: (k, j)))
    if o_spec is None:
        o_spec = (pl.BlockSpec((tn, tm), lambda i, j, k: (j, i)) if transpose_out
                  else pl.BlockSpec((tm, tn), lambda i, j, k: (i, j)))
    if out_shape is None:
        out_shape = (N, M) if transpose_out else (M, N)

    def emit(o_ref, val):
        o_ref[...] = (val.T if transpose_out else val).astype(o_ref.dtype)

    def body(a_ref, b_ref, o_ref, *scratch):
        part = lax.dot_general(a_ref[...].astype(BF16), b_ref[...].astype(BF16), dims, preferred_element_type=F32)
        if nk == 1:
            emit(o_ref, part)
            return
        acc_ref, = scratch
        k = pl.program_id(2)

        @pl.when(k == 0)
        def _():
            acc_ref[...] = part

        @pl.when((k > 0) & (k < nk - 1))
        def _():
            acc_ref[...] += part

        @pl.when(k == nk - 1)
        def _():
            emit(o_ref, acc_ref[...] + part)

    return pl.pallas_call(
        body, grid=(M // tm, N // tn, nk), in_specs=[a_spec, b_spec], out_specs=o_spec,
        out_shape=_out(out_shape, out_dtype),
        scratch_shapes=[pltpu.VMEM((tm, tn), F32)] if nk > 1 else [],
        compiler_params=_cp(("parallel", "parallel", "arbitrary")), name=name)(*_in_hbm([a, b]))


def _ew(fn, grid, ins, outs, name, scalars=None):
    n_in = len(ins)
    n_sc = 0 if scalars is None else 1

    def store(ref, val, acc, ids):
        if isinstance(val, (list, tuple)):
            for h, v in enumerate(val):
                ref[h] = v.astype(ref.dtype)
            return
        if acc is None:
            ref[...] = val.astype(ref.dtype)
            return

        @pl.when(ids[acc] == 0)
        def _():
            ref[...] = val.astype(ref.dtype)

        @pl.when(ids[acc] > 0)
        def _():
            ref[...] += val.astype(ref.dtype)

    def body(*refs):
        refs = refs[n_sc:]
        ids = tuple(pl.program_id(a) for a in range(len(grid)))
        vals = fn(ids, *[r[...] for r in refs[:n_in]])
        for ref, val, (_, _, _, acc) in zip(refs[n_in:], vals, outs):
            store(ref, val, acc, ids)

    acc_axes = {o[3] for o in outs if o[3] is not None}
    sem = tuple("arbitrary" if a in acc_axes else "parallel" for a in range(len(grid)))
    in_specs, out_specs = [s for _, s in ins], [o[2] for o in outs]
    out_shape = [_out(o[0], o[1]) for o in outs]
    args = _in_hbm([a for a, _ in ins])
    if scalars is None:
        return pl.pallas_call(body, grid=grid, in_specs=in_specs, out_specs=out_specs, out_shape=out_shape,
                              compiler_params=_cp(sem), name=name)(*args)
    spec = pltpu.PrefetchScalarGridSpec(num_scalar_prefetch=1, grid=grid, in_specs=in_specs, out_specs=out_specs)
    return pl.pallas_call(body, grid_spec=spec, out_shape=out_shape, compiler_params=_cp(sem), name=name)(scalars, *args)


def _rows(width, cblk=0, roff=0, tr=ROW_TILE):
    return pl.BlockSpec((tr, width), lambda i: (i + roff, cblk))


def _full(shape):
    nd = len(shape)
    return pl.BlockSpec(shape, lambda *_: (0,) * nd)


def _sigmoid(x):
    return 1.0 / (1.0 + jnp.exp(-x))


def _rms(x):
    return lax.rsqrt(jnp.mean(x * x, axis=-1, keepdims=True) + EPS)


def _rms_bwd(dn, xn, r):
    return r * (dn - xn * jnp.mean(dn * xn, axis=-1, keepdims=True))


def _colsum(x):
    return jnp.sum(x, axis=0, keepdims=True)


def _shifts(x):
    n = x.shape[0]
    rows = lax.broadcasted_iota(jnp.int32, x.shape, 0)
    return jnp.where(rows == 0, 0.0, pltpu.roll(x, 1, 0)), jnp.where(rows == n - 1, 0.0, pltpu.roll(x, n - 1, 0))


def _conv(x, w, b, shifted=None):
    prev, nxt = _shifts(x) if shifted is None else shifted
    return b + prev * w[0:1] + x * w[1:2] + nxt * w[2:3]


def _conv_bwd_x(dy, w):
    prev, nxt = _shifts(dy)
    return nxt * w[0:1] + dy * w[1:2] + prev * w[2:3]


def _conv_bwd_w(dy, x, shifted):
    prev, nxt = shifted
    return _colsum(dy * prev), _colsum(dy * x), _colsum(dy * nxt)


def _rope(x, cos, sin_lo, sin_hi):
    return x * cos + pltpu.roll(x, HEAD_PAD - 8, 1) * sin_lo + pltpu.roll(x, 8, 1) * sin_hi


ATTN_SCALE = QK_DIM ** -0.5
LOG2_E = 1.4426950408889634


def _head_keys(kv_ref, kr_ref, cos_ref, slo_ref, shi_ref, kc_ref, vp_ref):
    kv = kv_ref[...]
    lane = lax.broadcasted_iota(jnp.int32, kv.shape, 1)
    kc_ref[...] = jnp.where(lane < 64, kv, _rope(kr_ref[...], cos_ref[...], slo_ref[...], shi_ref[...])).astype(BF16)
    vp_ref[...] = jnp.where(lane >= 64, kv, 0.0).astype(BF16)


def _attn_specs(tq, TT, clamp):
    row = (lambda i: jnp.minimum(i, clamp)) if clamp is not None else (lambda i: i)
    q = pl.BlockSpec((tq, HEAD_PAD), lambda h, i: (i, h))
    lat = pl.BlockSpec((tq, HEAD_PAD), lambda h, i: (row(i), h))
    keys = pl.BlockSpec((TT, HEAD_PAD), lambda h, i: (0, h))
    kr = pl.BlockSpec((TT, HEAD_PAD), lambda h, i: (0, KR0 // HEAD_PAD))
    tab_q = pl.BlockSpec((tq, HEAD_PAD), lambda h, i: (i, 0))
    tab_k = pl.BlockSpec((TT, HEAD_PAD), lambda h, i: (0, 0))
    lse = pl.BlockSpec((None, tq, 1), lambda h, i: (h, row(i), 0))
    return q, lat, keys, kr, tab_q, tab_k, lse


def _attn_fwd(q_raw, kv, pp, tabs, T, TT):
    tq = ROW_TILE
    cos, slo, shi = tabs

    def body(q_ref, kv_ref, kr_ref, cq, lq, hq, ck, lk, hk, o_ref, l_ref, kc, vp):
        @pl.when(pl.program_id(1) == 0)
        def _():
            _head_keys(kv_ref, kr_ref, ck, lk, hk, kc, vp)

        q = _rope(q_ref[...], cq[...], lq[...], hq[...]).astype(BF16)
        s = lax.dot_general(q, kc[...], NT, preferred_element_type=F32)
        m = jnp.max(s, axis=-1, keepdims=True)
        p = jnp.exp2((s - m) * (ATTN_SCALE * LOG2_E))
        l = jnp.sum(p, axis=-1, keepdims=True)
        o = lax.dot_general(p.astype(BF16), vp[...], NN, preferred_element_type=F32)
        o_ref[...] = o / l
        l_ref[...] = m * ATTN_SCALE + jnp.log(l)

    qs, _, keys, kr, tab_q, tab_k, lse = _attn_specs(tq, TT, None)
    return pl.pallas_call(
        body, grid=(N_HEADS, T // tq), in_specs=[qs, keys, kr, tab_q, tab_q, tab_q, tab_k, tab_k, tab_k],
        out_specs=[qs, lse],
        out_shape=[jax.ShapeDtypeStruct((T, N_HEADS * HEAD_PAD), F32), jax.ShapeDtypeStruct((N_HEADS, T, 1), F32)],
        scratch_shapes=[pltpu.VMEM((TT, HEAD_PAD), BF16), pltpu.VMEM((TT, HEAD_PAD), BF16)],
        compiler_params=_cp(("parallel", "arbitrary")), name="attn_fwd",
    )(*_in_hbm([q_raw, kv, pp, cos, slo, shi, cos, slo, shi]))


def _attn_bwd(q_raw, kv, pp, o, do, lse, tabs, tabs_inv, T, TT):
    tq = ROW_TILE
    nq = T // tq
    cos, slo, shi = tabs
    cos_i, slo_i, shi_i = tabs_inv

    def body(q_ref, kv_ref, kr_ref, cq, lq, hq, ck, lk, hk, iq, ilq, ihq, ik, ilk, ihk, o_ref, do_ref, l_ref,
             dq_ref, dkv_ref, dkr_ref, kc, vp, dk, dv):
        h, i = pl.program_id(0), pl.program_id(1)

        @pl.when(i == 0)
        def _():
            _head_keys(kv_ref, kr_ref, ck, lk, hk, kc, vp)
            dk[...] = jnp.zeros_like(dk)
            dv[...] = jnp.zeros_like(dv)

        @pl.when(i < nq)
        def _():
            q = _rope(q_ref[...], cq[...], lq[...], hq[...]).astype(BF16)
            k, v, d_o = kc[...], vp[...], do_ref[...]
            s = lax.dot_general(q, k, NT, preferred_element_type=F32)
            p = jnp.exp2(s * (ATTN_SCALE * LOG2_E) - l_ref[...] * LOG2_E)
            dob = d_o.astype(BF16)
            dp = lax.dot_general(dob, v, NT, preferred_element_type=F32)
            dd = jnp.sum(d_o * o_ref[...], axis=-1, keepdims=True)
            ds = (p * (dp - dd) * ATTN_SCALE).astype(BF16)
            dq = lax.dot_general(ds, k, NN, preferred_element_type=F32)
            dq_ref[...] = _rope(dq, iq[...], ilq[...], ihq[...]).astype(dq_ref.dtype)
            dk[...] += lax.dot_general(ds, q, TN, preferred_element_type=F32)
            dv[...] += lax.dot_general(p.astype(BF16), dob, TN, preferred_element_type=F32)

        @pl.when(i == nq)
        def _():
            dq_ref[...] = jnp.zeros_like(dq_ref)
            dkh = dk[...]
            lane = lax.broadcasted_iota(jnp.int32, dkh.shape, 1)
            dkv_ref[...] = jnp.where(lane < 64, dkh, dv[...]).astype(dkv_ref.dtype)
            rot = _rope(jnp.where((lane >= 64) & (lane < 96), dkh, 0.0), ik[...], ilk[...], ihk[...])

            @pl.when(h == 0)
            def _():
                dkr_ref[...] = rot

            @pl.when(h > 0)
            def _():
                dkr_ref[...] += rot

    qs, lat, keys, kr, tab_q, tab_k, lse_spec = _attn_specs(tq, TT, nq - 1)
    wide = jax.ShapeDtypeStruct((TT, N_HEADS * HEAD_PAD), BF16)
    return pl.pallas_call(
        body, grid=(N_HEADS, TT // tq),
        in_specs=[qs, keys, kr] + [tab_q] * 3 + [tab_k] * 3 + [tab_q] * 3 + [tab_k] * 3 + [lat, lat, lse_spec],
        out_specs=[qs, keys, pl.BlockSpec((TT, HEAD_PAD), lambda h, i: (0, 0))],
        out_shape=[wide, wide, jax.ShapeDtypeStruct((TT, HEAD_PAD), F32)],
        scratch_shapes=[pltpu.VMEM((TT, HEAD_PAD), BF16), pltpu.VMEM((TT, HEAD_PAD), BF16),
                        pltpu.VMEM((TT, HEAD_PAD), F32), pltpu.VMEM((TT, HEAD_PAD), F32)],
        compiler_params=_cp(("arbitrary", "arbitrary")), name="attn_bwd",
    )(*_in_hbm([q_raw, kv, pp, cos, slo, shi, cos, slo, shi, cos_i, slo_i, shi_i, cos_i, slo_i, shi_i, o, do, lse]))


def _allgather8(x, name, in_vmem):
    m_per, n = x.shape

    def body(x_ref, out_ref, token, send_sems, recv_sems, local_sem):
        token[...] = jnp.zeros_like(token)
        mx, my, mc = lax.axis_index("x"), lax.axis_index("y"), lax.axis_index("c")
        me, sibling = (mx, my, mc), (mx, my, 1 - mc)
        chips = [(1 - mx, my), (mx, 1 - my), (1 - mx, 1 - my)]

        def rows(px, py, pc):
            return out_ref.at[pl.ds((4 * px + 2 * py + pc) * m_per, m_per), :]

        def copy(k, block, to, src=None):
            return pltpu.make_async_remote_copy(
                src_ref=rows(*block) if src is None else src, dst_ref=rows(*block),
                send_sem=send_sems.at[k], recv_sem=recv_sems.at[k], device_id=to, device_id_type=MESH)

        mine = pltpu.make_async_copy(x_ref, rows(*me), local_sem)
        mine.start()
        first = [copy(0, me, sibling, src=x_ref)]
        first += [copy(1 + j, me, (*chip, mc), src=x_ref) for j, chip in enumerate(chips)]
        for cp in first:
            cp.start()
        passed = [copy(4 + j, (*chip, mc), sibling) for j, chip in enumerate(chips)]
        for j, chip in enumerate(chips):
            copy(1 + j, (*chip, mc), me).wait_recv()
            passed[j].start()
        copy(0, sibling, me).wait_recv()
        for j, chip in enumerate(chips):
            copy(4 + j, (*chip, 1 - mc), me).wait_recv()
        for cp in first + passed:
            cp.wait_send()
        mine.wait()

    space = pltpu.VMEM if in_vmem else pl.ANY
    return pl.pallas_call(
        body, out_shape=[jax.ShapeDtypeStruct((8 * m_per, n), x.dtype), jax.ShapeDtypeStruct((8, 128), F32)],
        in_specs=[pl.BlockSpec(memory_space=space)],
        out_specs=[pl.BlockSpec(memory_space=space), pl.BlockSpec(memory_space=pltpu.VMEM)],
        scratch_shapes=[pltpu.SemaphoreType.DMA((7,)), pltpu.SemaphoreType.DMA((7,)), pltpu.SemaphoreType.DMA],
        name=name)(x)


def _hbm_specs(n):
    return [pl.BlockSpec(memory_space=pl.ANY)] * n


def _gather_weights(shards):
    n = len(shards)
    halves = [s.shape[0] // 2 for s in shards]

    def body(*refs):
        ins, outs = refs[:n], refs[n:2 * n]
        token, send_sems, recv_sems = refs[2 * n:]
        token[...] = jnp.zeros_like(token)
        mx, my, mc = lax.axis_index("x"), lax.axis_index("y"), lax.axis_index("c")
        j_me = 2 * mx + my
        chips = [(1 - mx, my), (mx, 1 - my), (1 - mx, 1 - my)]

        def half(w, chip_idx, hc):
            return outs[w].at[chip_idx, pl.ds(hc * halves[w], halves[w]), :]

        def copy(w, k, src, dst, to):
            return pltpu.make_async_remote_copy(src_ref=src, dst_ref=dst, send_sem=send_sems.at[w, k],
                                                recv_sem=recv_sems.at[w, k], device_id=to, device_id_type=MESH)

        sends = []
        for w in range(n):
            cp = copy(w, 6, ins[w], outs[w].at[j_me], (mx, my, 1 - mc))
            cp.start()
            sends.append(cp)
        for k, (px, py) in enumerate(chips):
            for w in range(n):
                cp = copy(w, k, ins[w].at[pl.ds(mc * halves[w], halves[w]), :], half(w, j_me, mc), (px, py, mc))
                cp.start()
                sends.append(cp)
        for k, (px, py) in enumerate(chips):
            for w in range(n):
                got = half(w, 2 * px + py, mc)
                copy(w, k, got, got, (px, py, mc)).wait_recv()
                cp = copy(w, 3 + k, got, got, (mx, my, 1 - mc))
                cp.start()
                sends.append(cp)
        for k, (px, py) in enumerate(chips):
            for w in range(n):
                got = half(w, 2 * px + py, 1 - mc)
                copy(w, 3 + k, got, got, (mx, my, 1 - mc)).wait_recv()
        for w in range(n):
            own = outs[w].at[j_me]
            copy(w, 6, own, own, (mx, my, 1 - mc)).wait_recv()
        for cp in sends:
            cp.wait_send()

    res = pl.pallas_call(
        body, out_shape=[jax.ShapeDtypeStruct((4,) + s.shape, s.dtype) for s in shards]
        + [jax.ShapeDtypeStruct((8, 128), F32)],
        in_specs=_hbm_specs(n), out_specs=_hbm_specs(n) + [pl.BlockSpec(memory_space=pltpu.VMEM)],
        scratch_shapes=[pltpu.SemaphoreType.DMA((n, 7)), pltpu.SemaphoreType.DMA((n, 7))],
        name="gather_weights")(*shards)
    return list(res[:n]), res[n]


def _rs_pair(gs, name):
    n = len(gs)
    halves = [g.shape[1] // 2 for g in gs]

    def body(*refs):
        ins, lands = refs[:n], refs[n:2 * n]
        send_sems, recv_sems = refs[2 * n:]
        mx, my, mc = lax.axis_index("x"), lax.axis_index("y"), lax.axis_index("c")
        copies = []
        for w in range(n):
            h = halves[w]
            cp = pltpu.make_async_remote_copy(
                src_ref=ins[w].at[:, pl.ds((1 - mc) * h, h), :], dst_ref=lands[w], send_sem=send_sems.at[w],
                recv_sem=recv_sems.at[w], device_id=(mx, my, 1 - mc), device_id_type=MESH)
            cp.start()
            copies.append(cp)
        for cp in copies:
            cp.wait()

    return pl.pallas_call(
        body, out_shape=[jax.ShapeDtypeStruct((4, h, g.shape[2]), g.dtype) for g, h in zip(gs, halves)],
        in_specs=_hbm_specs(n), out_specs=_hbm_specs(n),
        scratch_shapes=[pltpu.SemaphoreType.DMA((n,)), pltpu.SemaphoreType.DMA((n,))], name=name)(*gs)


def _rs_chips(parts):
    n = len(parts)

    def body(*refs):
        ins, lands = refs[:n], refs[n:2 * n]
        send_sems, recv_sems = refs[2 * n:]
        mx, my, mc = lax.axis_index("x"), lax.axis_index("y"), lax.axis_index("c")
        copies = []
        for k, (px, py) in enumerate([(1 - mx, my), (mx, 1 - my), (1 - mx, 1 - my)]):
            for w in range(n):
                cp = pltpu.make_async_remote_copy(
                    src_ref=ins[w].at[2 * px + py], dst_ref=lands[w].at[k], send_sem=send_sems.at[w, k],
                    recv_sem=recv_sems.at[w, k], device_id=(px, py, mc), device_id_type=MESH)
                cp.start()
                copies.append(cp)
        for cp in copies:
            cp.wait()

    return list(pl.pallas_call(
        body, out_shape=[jax.ShapeDtypeStruct((3,) + p.shape[1:], p.dtype) for p in parts],
        in_specs=_hbm_specs(n), out_specs=_hbm_specs(n),
        scratch_shapes=[pltpu.SemaphoreType.DMA((n, 3)), pltpu.SemaphoreType.DMA((n, 3))], name="rs_chips")(*parts))


def _rs_pair_back(gs):
    n = len(gs)

    def body(*refs):
        outs = refs[n:2 * n]
        send_sems, recv_sems = refs[2 * n:]
        mx, my, mc = lax.axis_index("x"), lax.axis_index("y"), lax.axis_index("c")
        copies = []
        for w in range(n):
            h = gs[w].shape[0] // 2
            mine = outs[w].at[pl.ds(mc * h, h), :]
            cp = pltpu.make_async_remote_copy(src_ref=mine, dst_ref=mine, send_sem=send_sems.at[w],
                                              recv_sem=recv_sems.at[w], device_id=(mx, my, 1 - mc), device_id_type=MESH)
            cp.start()
            copies.append(cp)
        for cp in copies:
            cp.wait()

    return pl.pallas_call(
        body, out_shape=[jax.ShapeDtypeStruct(g.shape, g.dtype) for g in gs],
        in_specs=_hbm_specs(n), out_specs=_hbm_specs(n), input_output_aliases={w: w for w in range(n)},
        scratch_shapes=[pltpu.SemaphoreType.DMA((n,)), pltpu.SemaphoreType.DMA((n,))], name="rs_pair_back")(*gs)


_HBM = pl.BlockSpec(memory_space=pltpu.HBM)
_SEM = pl.BlockSpec(memory_space=pltpu.SEMAPHORE)
_EFFECT = pltpu.SideEffectType.DATAFLOW_SIDE_EFFECTING


def _ici_copies(kind, srcs, lands, send_sems, recv_sems):
    n = len(srcs)
    mx, my, mc = lax.axis_index("x"), lax.axis_index("y"), lax.axis_index("c")
    j_me = 2 * mx + my
    copies = []
    if kind == "all":
        for k in range(7):
            a, b, c = (k + 1) >> 2 & 1, (k + 1) >> 1 & 1, (k + 1) & 1
            peer = (1 - mx if a else mx, 1 - my if b else my, 1 - mc if c else mc)
            for w in range(n):
                copies.append(pltpu.make_async_remote_copy(
                    src_ref=srcs[w], dst_ref=lands[w].at[4 * mx + 2 * my + mc], send_sem=send_sems.at[7 * w + k],
                    recv_sem=recv_sems.at[7 * w + k], device_id=peer, device_id_type=MESH))
        return copies
    if kind == "pair":
        for w in range(n):
            h = srcs[w].shape[1] // 2
            copies.append(pltpu.make_async_remote_copy(
                src_ref=srcs[w].at[:, pl.ds((1 - mc) * h, h), :], dst_ref=lands[w], send_sem=send_sems.at[w],
                recv_sem=recv_sems.at[w], device_id=(mx, my, 1 - mc), device_id_type=MESH))
        return copies
    for k, (px, py) in enumerate([(1 - mx, my), (mx, 1 - my), (1 - mx, 1 - my)]):
        for w in range(n):
            if kind == "gather":
                h = srcs[w].shape[0] // 2
                src, dst = srcs[w].at[pl.ds(mc * h, h), :], lands[w].at[j_me, pl.ds(mc * h, h), :]
            else:
                src, dst = srcs[w].at[2 * px + py], lands[w].at[k]
            copies.append(pltpu.make_async_remote_copy(
                src_ref=src, dst_ref=dst, send_sem=send_sems.at[3 * w + k], recv_sem=recv_sems.at[3 * w + k],
                device_id=(px, py, mc), device_id_type=MESH))
    return copies


_SEMS_PER_OPERAND = {"gather": 3, "scatter": 3, "all": 7, "pair": 1}


def _ici_start(kind, srcs, land_shapes, carry, name):
    n = len(srcs)

    def body(*refs):
        ins, lands = refs[:n], refs[n:2 * n]
        send_sems, recv_sems = refs[2 * n + 1], refs[2 * n + 2]
        for cp in _ici_copies(kind, ins, lands, send_sems, recv_sems):
            cp.start()

    hbm = lambda a: pltpu.with_memory_space_constraint(a, pltpu.HBM)
    lands = [lax.empty(s, srcs[0].dtype) for s in land_shapes]
    args = [hbm(a) for a in list(srcs) + lands + [carry]]
    n_sem = _SEMS_PER_OPERAND[kind] * n
    out_shape = ([pltpu.SemaphoreType.DMA((n_sem,)), pltpu.SemaphoreType.DMA((n_sem,))]
                 + [pltpu.HBM(a.shape, a.dtype) for a in args])
    res = pl.pallas_call(
        body, name=name, out_shape=out_shape, in_specs=[_HBM] * len(args), out_specs=[_SEM, _SEM] + [_HBM] * len(args),
        input_output_aliases={i: 2 + i for i in range(len(args))},
        compiler_params=pltpu.CompilerParams(has_side_effects=_EFFECT))(*args)
    return res[0], res[1], list(res[2:2 + n]), list(res[2 + n:2 + 2 * n]), res[2 + 2 * n]


def _ici_wait(kind, send_sems, recv_sems, srcs, lands, after, name):
    n = len(srcs)

    def body(*refs):
        ins, zones = refs[:n], refs[n:2 * n]
        for cp in _ici_copies(kind, ins, zones, refs[2 * n], refs[2 * n + 1]):
            cp.wait_send()
            cp.wait_recv()

    args = list(srcs) + list(lands)
    res = pl.pallas_call(
        body, name=name, out_shape=[pltpu.HBM(a.shape, a.dtype) for a in args],
        in_specs=[_HBM] * len(args) + [_SEM, _SEM, pl.BlockSpec(memory_space=pl.ANY)], out_specs=[_HBM] * len(args),
        input_output_aliases={i: i for i in range(len(args))},
        compiler_params=pltpu.CompilerParams(has_side_effects=_EFFECT))(*args, send_sems, recv_sems, after)
    return list(res[:n]), list(res[n:])


def _gather_finish(shards, lands):
    n = len(shards)

    def body(*refs):
        own, outs = refs[:n], refs[2 * n:3 * n]
        send_sems, recv_sems = refs[3 * n:]
        mx, my, mc = lax.axis_index("x"), lax.axis_index("y"), lax.axis_index("c")
        j_me = 2 * mx + my
        sibling = (mx, my, 1 - mc)
        copies = []

        def push(w, k, src, dst):
            cp = pltpu.make_async_remote_copy(src_ref=src, dst_ref=dst, send_sem=send_sems.at[w, k],
                                              recv_sem=recv_sems.at[w, k], device_id=sibling, device_id_type=MESH)
            cp.start()
            copies.append(cp)

        for w in range(n):
            h = shards[w].shape[0] // 2
            push(w, 3, own[w], outs[w].at[j_me])
            for k, (px, py) in enumerate([(1 - mx, my), (mx, 1 - my), (1 - mx, 1 - my)]):
                got = outs[w].at[2 * px + py, pl.ds(mc * h, h), :]
                push(w, k, got, got)
        for cp in copies:
            cp.wait()

    return pl.pallas_call(
        body, out_shape=[jax.ShapeDtypeStruct(l.shape, l.dtype) for l in lands],
        in_specs=_hbm_specs(2 * n), out_specs=_hbm_specs(n), input_output_aliases={n + w: w for w in range(n)},
        scratch_shapes=[pltpu.SemaphoreType.DMA((n, 4)), pltpu.SemaphoreType.DMA((n, 4))], name="gather_finish",
    )(*shards, *lands)


def _tile_rows(h, c, itemsize, mult):
    best = h
    for t in range(mult, h + 1, mult):
        if h % t == 0 and t * c * itemsize <= (1 << 21):
            best = t
    return best


def _add_pair(g, land, place, name):
    _, h, c = land.shape
    t = _tile_rows(h, c, 2, 16)
    nb = h // t
    return _ew(lambda ids, u, v: (u.astype(F32) + v.astype(F32),), (4, nb),
               [(g, pl.BlockSpec((None, t, c), lambda j, i, s: (j, s[1] * nb + i, 0))),
                (land, pl.BlockSpec((None, t, c), lambda j, i, s: (j, i, 0)))],
               [(land.shape, BF16, pl.BlockSpec((None, t, c), lambda j, i, s: (j, i, 0)), None)], name, scalars=place)[0]


def _add_chips(own, land, place, name):
    _, h, c = land.shape
    t = _tile_rows(h, c, 4, 16)
    nb = h // t

    def fn(ids, a, b):
        return (((a.astype(F32) + b[0].astype(F32)) + b[1].astype(F32)) + b[2].astype(F32),)

    return _ew(fn, (nb,), [(own, pl.BlockSpec((None, t, c), lambda i, s: (s[0], i, 0))),
                           (land, pl.BlockSpec((3, t, c), lambda i, s: (0, i, 0)))],
               [((2 * h, c), F32, pl.BlockSpec((t, c), lambda i, s: (s[1] * nb + i, 0)), None)], name, scalars=place)[0]


W_IN_SEGMENTS = ((0, 256, KV0), (256, 288, KR0 + 64), (288, 672, Q0), (672, 1184, CX0), (1184, 1696, CB0),
                 (1696, 2208, CC0), (2208, 3232, GA0), (3232, 4256, GC0))
W_IN_SHARD = 1064


W_IN_SHARD_PAD = 1088


def _w_in_t_p_from_shards(s):
    pieces = []
    for o0, o1, p0 in sorted(W_IN_SEGMENTS, key=lambda t: t[2]):
        if p0 == KR0 + 64:
            pieces.append(jnp.zeros((64, s.shape[2]), s.dtype))
        for j in range(4):
            lo, hi = max(o0, j * W_IN_SHARD), min(o1, (j + 1) * W_IN_SHARD)
            if lo < hi:
                pieces.append(s[j, lo - j * W_IN_SHARD:hi - j * W_IN_SHARD])
    pieces.append(jnp.zeros((32, s.shape[2]), s.dtype))
    return jnp.concatenate(pieces, axis=0)


def _w_in_t_shards_from_p(g):
    shards = []
    for j in range(4):
        pieces = []
        for o0, o1, p0 in W_IN_SEGMENTS:
            lo, hi = max(o0, j * W_IN_SHARD), min(o1, (j + 1) * W_IN_SHARD)
            if lo < hi:
                pieces.append(g[p0 + lo - o0:p0 + hi - o0])
        pieces.append(jnp.zeros((W_IN_SHARD_PAD - W_IN_SHARD, g.shape[1]), g.dtype))
        shards.append(jnp.concatenate(pieces, axis=0))
    return jnp.stack(shards, axis=0)


def _cols_from_shards(s):
    return jnp.transpose(s, (1, 0, 2)).reshape(s.shape[1], -1)


def _rope_tables(T, TT, inverse):
    rows = T // GRID_W
    row = jnp.repeat(jnp.arange(rows), GRID_W).astype(F32)
    col = jnp.tile(jnp.arange(GRID_W), rows).astype(F32)
    inv = ROPE_THETA ** (-jnp.arange(0, 16, 2, dtype=F32) / 16)
    ang = jnp.concatenate([row[:, None] * inv, col[:, None] * inv], axis=-1)
    cos, sin = jnp.cos(ang), jnp.sin(ang)
    lane = jnp.arange(32)
    src = (lane // 16) * 8 + lane % 8
    lo = ((lane % 16) // 8 == 0).astype(F32)
    sgn = -1.0 if inverse else 1.0
    cos32 = cos[:, src]
    sin_lo32 = -sgn * sin[:, src] * lo
    sin_hi32 = sgn * sin[:, src] * (1.0 - lo)

    def widen(t32, fill):
        t = jnp.concatenate([jnp.full((T, 64), fill, F32), t32, jnp.full((T, 32), fill, F32)], axis=1)
        return jnp.concatenate([t, jnp.full((TT - T, HEAD_PAD), fill, F32)], axis=0)

    return widen(cos32, 1.0), widen(sin_lo32, 0.0), widen(sin_hi32, 0.0)


def _local_step(xx, tgt, mod_lat, mod_ctx, W, late_weights, early_grads):
    TT = xx.shape[0]
    T = tgt.shape[0]
    n_lat, n_all = T // ROW_TILE, TT // ROW_TILE
    sh1, sc1, g1, sh2, sc2, g2 = [mod_lat[:, k * D_MODEL:(k + 1) * D_MODEL] for k in range(6)]
    csh1, csc1 = mod_ctx[:, :D_MODEL], mod_ctx[:, D_MODEL:2 * D_MODEL]
    vec = lambda n: _full((1, n))
    row_out = lambda n, dt, rows=T: ((rows, n), dt, _rows(n), None)
    acc_out = lambda n: ((1, n), F32, _full((1, n)), 0)

    def f_norm1(ids, x, g, a_sh, a_sc, b_sh, b_sc):
        ctx = ids[0] >= n_lat
        sh, sc = jnp.where(ctx, b_sh, a_sh), jnp.where(ctx, b_sc, a_sc)
        return ((x * _rms(x) * g) * (1.0 + sc) + sh,)

    (hh,) = _ew(f_norm1, (n_all,), [(xx, _rows(D_MODEL)), (W["norm1_g"], vec(D_MODEL)), (sh1, vec(D_MODEL)),
                                   (sc1, vec(D_MODEL)), (csh1, vec(D_MODEL)), (csc1, vec(D_MODEL))],
                [row_out(D_MODEL, BF16, TT)], "norm1_fwd")
    tm_all = _pick(TT, (768, 256))
    pp = _mm(hh, W["w_in_t"], "nt", TT, P_COLS, D_MODEL, tm=tm_all, tn=2176, tk=D_MODEL, name="w_in_fwd")

    def f_lowrank(ids, ckv, cq, gkv, gq):
        return ckv * _rms(ckv) * gkv, cq * _rms(cq) * gq

    nkv, nq = _ew(f_lowrank, (n_all,), [(pp, _rows(KV_RANK, KV0 // KV_RANK)), (pp, _rows(Q_RANK, Q0 // Q_RANK)),
                                       (W["kv_norm_g"], vec(KV_RANK)), (W["q_norm_g"], vec(Q_RANK))],
                  [row_out(KV_RANK, BF16, TT), row_out(Q_RANK, BF16, TT)], "lowrank_norm_fwd")
    kv = _mm(nkv, W["w_ukv"], "nn", TT, 1024, KV_RANK, tm=tm_all, tn=256, tk=KV_RANK, name="w_ukv_fwd",
             b_spec=pl.BlockSpec((None, KV_RANK, 256), lambda i, j, k: (j, k, 0)))
    q_raw = _mm(nq, W["w_uq_t"], "nt", TT, 1024, Q_RANK, tm=tm_all, tn=1024, tk=Q_RANK, name="w_uq_fwd")

    tabs = _rope_tables(T, TT, inverse=False)
    tabs_inv = _rope_tables(T, TT, inverse=True)
    o_pad, lse = _attn_fwd(q_raw, kv, pp, tabs, T, TT)
    W = dict(W, **late_weights(o_pad))
    tm_lat = _pick(T, (1024, 512, 256))
    ya = _mm(o_pad, W["w_attn_out"], "nn", T, D_MODEL, 1024, tm=tm_lat, tn=D_MODEL, tk=1024, name="w_attn_out_fwd")

    tc = 256
    colT = lambda blk0: pl.BlockSpec((T, tc), lambda j: (0, blk0 + j))

    def f_conv(ids, xin, cb, cc, w, b):
        return (cb * _conv(cc * xin, w, b),)

    (e,) = _ew(f_conv, (CONV_DIM // tc,),
               [(pp, colT(CX0 // tc)), (pp, colT(CB0 // tc)), (pp, colT(CC0 // tc)),
                (W["conv_w"], pl.BlockSpec((3, tc), lambda j: (0, j))), (W["conv_b"], pl.BlockSpec((1, tc), lambda j: (0, j)))],
               [((T, CONV_DIM), BF16, colT(0), None)], "conv_fwd")
    yc = _mm(e, W["w_conv_out"], "nn", T, D_MODEL, CONV_DIM, tm=tm_lat, tn=256, tk=CONV_DIM, name="w_conv_out_fwd",
             b_spec=pl.BlockSpec((None, CONV_DIM, 256), lambda i, j, k: (j, k, 0)))

    def f_merge(ids, ga, gc, a, c):
        return (_sigmoid(ga) * a + _sigmoid(gc) * c,)

    (mrg,) = _ew(f_merge, (n_lat,), [(pp, _rows(D_MODEL, 0)), (pp, _rows(D_MODEL, 1)), (ya, _rows(D_MODEL)),
                                    (yc, _rows(D_MODEL))], [row_out(D_MODEL, BF16)], "merge_fwd")
    mo = _mm(mrg, W["w_o"], "nn", T, D_MODEL, D_MODEL, tm=tm_lat, tn=D_MODEL, tk=D_MODEL, name="w_o_fwd")

    def f_norm2(ids, x, m, gate, g, sh, sc):
        x1 = x + gate * m
        return x1, (x1 * _rms(x1) * g) * (1.0 + sc) + sh

    x1, h2 = _ew(f_norm2, (n_lat,), [(xx, _rows(D_MODEL)), (mo, _rows(D_MODEL)), (g1, vec(D_MODEL)),
                                    (W["norm2_g"], vec(D_MODEL)), (sh2, vec(D_MODEL)), (sc2, vec(D_MODEL))],
                 [row_out(D_MODEL, F32), row_out(D_MODEL, BF16)], "norm2_fwd")
    up = _mm(h2, W["w_up"], "nn", T, 2 * D_FF, D_MODEL, tm=tm_lat, tn=1408, tk=D_MODEL, name="w_up_fwd",
             b_spec=pl.BlockSpec((None, D_MODEL, 1408), lambda i, j, k: (j, k, 0)))

    n_ff = D_FF // tc
    ffw = lambda off, n=3: pl.BlockSpec((n, tc), lambda j: (0, j + off))

    def f_ffn(ids, ug, uv, wg, wv, bg, bv):
        gate, val = _conv(ug, wg, bg), _conv(uv, wv, bv)
        return (gate * _sigmoid(gate) * val,)

    (act,) = _ew(f_ffn, (n_ff,), [(up, colT(0)), (up, colT(n_ff)), (W["ffn_conv_w"], ffw(0)), (W["ffn_conv_w"], ffw(n_ff)),
                                 (W["ffn_conv_b"], ffw(0, 1)), (W["ffn_conv_b"], ffw(n_ff, 1))],
                 [((T, D_FF), BF16, colT(0), None)], "ffn_act_fwd")
    f = _mm(act, W["w_down"], "nn", T, D_MODEL, D_FF, tm=tm_lat, tn=D_MODEL, tk=D_FF, name="w_down_fwd")

    def f_head(ids, x1_, f_, gate, gf, t):
        x2 = x1_ + gate * f_
        r = _rms(x2)
        xn = x2 * r
        err = xn * gf - t
        loss = 0.5 * jnp.sum(jnp.mean(err * err, axis=-1, keepdims=True))
        dy = err * (1.0 / D_MODEL)
        dx2 = _rms_bwd(dy * gf, xn, r)
        return dx2, dx2 * gate, _colsum(dy * xn), _colsum(dx2 * f_), jnp.full((1, 128), loss, F32)

    dx2, df, dg_f, dg2, loss = _ew(
        f_head, (n_lat,), [(x1, _rows(D_MODEL)), (f, _rows(D_MODEL)), (g2, vec(D_MODEL)), (W["final_g"], vec(D_MODEL)),
                           (tgt, _rows(D_MODEL))],
        [row_out(D_MODEL, F32), row_out(D_MODEL, BF16), acc_out(D_MODEL), acc_out(D_MODEL), acc_out(128)], "loss_head")

    d_w_down = _mm(act, df, "tn", D_FF, D_MODEL, T, tm=1408, tn=D_MODEL, tk=T, name="w_down_dw",
                   out_dtype=BF16).reshape(4, D_FF // 4, D_MODEL)
    da = _mm(df, W["w_down"], "nt", T, D_FF, D_MODEL, tm=tm_lat, tn=1408, tk=D_MODEL, name="w_down_dx")

    tcb = 128
    n_fb = D_FF // tcb
    colb = lambda blk0: pl.BlockSpec((T, tcb), lambda j: (0, blk0 + j))
    ffwb = lambda off, n=3: pl.BlockSpec((n, tcb), lambda j: (0, j + off))
    cvec = ((1, D_FF), F32, pl.BlockSpec((1, tcb), lambda j: (0, j)), None)

    def f_ffn_bwd(ids, ug, uv, d_act, wg, wv, bg, bv):
        sg, sv = _shifts(ug), _shifts(uv)
        gate, val = _conv(ug, wg, bg, sg), _conv(uv, wv, bv, sv)
        s = _sigmoid(gate)
        d_gate = d_act * val * s * (1.0 + gate * (1.0 - s))
        d_val = d_act * gate * s
        wg0, wg1, wg2 = _conv_bwd_w(d_gate, ug, sg)
        wv0, wv1, wv2 = _conv_bwd_w(d_val, uv, sv)
        d_up = [_conv_bwd_x(d_gate, wg), _conv_bwd_x(d_val, wv)]
        return d_up, [_colsum(d_gate), _colsum(d_val), wg0, wg1, wg2, wv0, wv1, wv2]

    d_up3, ffn_stats = _ew(
        f_ffn_bwd, (n_fb,),
        [(up, colb(0)), (up, colb(n_fb)), (da, colb(0)), (W["ffn_conv_w"], ffwb(0)), (W["ffn_conv_w"], ffwb(n_fb)),
         (W["ffn_conv_b"], ffwb(0, 1)), (W["ffn_conv_b"], ffwb(n_fb, 1))],
        [((2, T, D_FF), BF16, pl.BlockSpec((2, T, tcb), lambda j: (0, 0, j)), None),
         ((n_fb, 8, 1, tcb), F32, pl.BlockSpec((None, 8, 1, tcb), lambda j: (j, 0, 0, 0)), None)], "ffn_act_bwd")
    stat = lambda s: ffn_stats[:, s, 0, :].reshape(1, D_FF)
    d_ffn_conv_b = jnp.concatenate([stat(0), stat(1)], axis=1)
    d_ffn_conv_w = jnp.concatenate([jnp.concatenate([stat(2), stat(3), stat(4)], axis=0),
                                    jnp.concatenate([stat(5), stat(6), stat(7)], axis=0)], axis=1)

    tk_t = T
    d_w_up = _mm(h2, d_up3, "tn", D_MODEL, 2 * D_FF, T, tm=D_MODEL, tn=1408, tk=tk_t, name="w_up_dw", out_dtype=BF16,
                 b_spec=pl.BlockSpec((None, tk_t, 1408), lambda i, j, k: (j // 2, k, j % 2)),
                 o_spec=pl.BlockSpec((None, D_MODEL, 1408), lambda i, j, k: (j, i, 0)), out_shape=(4, D_MODEL, 1408))
    dh2 = _mm(d_up3, W["w_up"], "nt", T, D_MODEL, 2 * D_FF, tm=tm_lat, tn=D_MODEL, tk=1408, name="w_up_dx",
              a_spec=pl.BlockSpec((None, tm_lat, 1408), lambda i, j, k: (k // 2, i, k % 2)),
              b_spec=pl.BlockSpec((None, D_MODEL, 1408), lambda i, j, k: (k, j, 0)))

    def f_norm2_bwd(ids, dx2_, dh, x1_, m, g, sc, gate):
        r = _rms(x1_)
        xn = x1_ * r
        dx1 = dx2_ + _rms_bwd(dh * g * (1.0 + sc), xn, r)
        return dx1, dx1 * gate, _colsum(dh), _colsum(dh * xn * g), _colsum(dh * xn * (1.0 + sc)), _colsum(dx1 * m)

    dx1, dmo, dsh2, dsc2, dg_n2, dg1 = _ew(
        f_norm2_bwd, (n_lat,), [(dx2, _rows(D_MODEL)), (dh2, _rows(D_MODEL)), (x1, _rows(D_MODEL)), (mo, _rows(D_MODEL)),
                                (W["norm2_g"], vec(D_MODEL)), (sc2, vec(D_MODEL)), (g1, vec(D_MODEL))],
        [row_out(D_MODEL, F32), row_out(D_MODEL, BF16)] + [acc_out(D_MODEL)] * 4, "norm2_bwd")
    d_w_o = _mm(mrg, dmo, "tn", D_MODEL, D_MODEL, T, tm=D_MODEL, tn=D_MODEL, tk=tk_t, name="w_o_dw",
                out_dtype=BF16).reshape(4, D_MODEL // 4, D_MODEL)
    dmrg = _mm(dmo, W["w_o"], "nt", T, D_MODEL, D_MODEL, tm=tm_lat, tn=D_MODEL, tk=D_MODEL, name="w_o_dx")
    dmrg = early_grads("late", {"w_o": d_w_o, "w_up": d_w_up, "w_down": d_w_down}, dmrg)

    def f_merge_bwd(ids, dm, ga, gc, a, c):
        sa, sc_ = _sigmoid(ga), _sigmoid(gc)
        return dm * sa, dm * sc_, dm * a * sa * (1.0 - sa), dm * c * sc_ * (1.0 - sc_)

    dya, dyc, dp_ga, dp_gc = _ew(
        f_merge_bwd, (n_lat,), [(dmrg, _rows(D_MODEL)), (pp, _rows(D_MODEL, 0)), (pp, _rows(D_MODEL, 1)),
                                (ya, _rows(D_MODEL)), (yc, _rows(D_MODEL))], [row_out(D_MODEL, BF16)] * 4, "merge_bwd")

    d_w_ao_p = _mm(o_pad, dya, "tn", 1024, D_MODEL, T, tm=1024, tn=D_MODEL, tk=tk_t, name="w_attn_out_dw", out_dtype=BF16)
    do_pad = _mm(dya, W["w_attn_out"], "nt", T, 1024, D_MODEL, tm=tm_lat, tn=1024, tk=D_MODEL, name="w_attn_out_dx")
    d_w_co = _mm(e, dyc, "tn", CONV_DIM, D_MODEL, T, tm=CONV_DIM, tn=256, tk=tk_t, name="w_conv_out_dw", out_dtype=BF16,
                 o_spec=pl.BlockSpec((None, CONV_DIM, 256), lambda i, j, k: (j, i, 0)), out_shape=(4, CONV_DIM, 256))
    de = _mm(dyc, W["w_conv_out"], "nt", T, CONV_DIM, D_MODEL, tm=tm_lat, tn=CONV_DIM, tk=256, name="w_conv_out_dx",
             b_spec=pl.BlockSpec((None, CONV_DIM, 256), lambda i, j, k: (k, j, 0)))

    def f_conv_bwd(ids, xin, cb, cc, d_e, w, b):
        z = cc * xin
        sz = _shifts(z)
        cz = _conv(z, w, b, sz)
        dcz = d_e * cb
        w0, w1, w2 = _conv_bwd_w(dcz, z, sz)
        dz = _conv_bwd_x(dcz, w)
        return dz * cc, d_e * cz, dz * xin, _colsum(dcz), w0, w1, w2

    cvec_c = ((1, CONV_DIM), F32, pl.BlockSpec((1, tc), lambda j: (0, j)), None)
    conv_b = _ew(f_conv_bwd, (CONV_DIM // tc,),
                 [(pp, colT(CX0 // tc)), (pp, colT(CB0 // tc)), (pp, colT(CC0 // tc)), (de, colT(0)),
                  (W["conv_w"], pl.BlockSpec((3, tc), lambda j: (0, j))), (W["conv_b"], pl.BlockSpec((1, tc), lambda j: (0, j)))],
                 [((T, CONV_DIM), BF16, colT(0), None)] * 3 + [cvec_c] * 4, "conv_bwd")
    dp_cx, dp_cb, dp_cc, d_conv_b = conv_b[:4]
    d_conv_w = jnp.concatenate(conv_b[4:7], axis=0)

    dq_raw, dkv, dp_kr = _attn_bwd(q_raw, kv, pp, o_pad, do_pad, lse, tabs, tabs_inv, T, TT)

    tk_a = TT
    d_w_uq_t = _mm(nq, dq_raw, "tn", Q_RANK, 1024, TT, tm=Q_RANK, tn=1024, tk=tk_a, name="w_uq_dw", transpose_out=True)
    dnq = _mm(dq_raw, W["w_uq_t"], "nn", TT, Q_RANK, 1024, tm=tm_all, tn=Q_RANK, tk=1024, name="w_uq_dx")
    d_w_ukv = _mm(nkv, dkv, "tn", KV_RANK, 1024, TT, tm=KV_RANK, tn=256, tk=tk_a, name="w_ukv_dw", out_dtype=BF16,
                  o_spec=pl.BlockSpec((None, KV_RANK, 256), lambda i, j, k: (j, i, 0)), out_shape=(4, KV_RANK, 256))
    dnkv = _mm(dkv, W["w_ukv"], "nt", TT, KV_RANK, 1024, tm=tm_all, tn=KV_RANK, tk=256, name="w_ukv_dx",
               b_spec=pl.BlockSpec((None, KV_RANK, 256), lambda i, j, k: (k, j, 0)))
    dnkv = early_grads("mid", {
        "w_attn_out": jnp.transpose(d_w_ao_p.reshape(N_HEADS, HEAD_PAD, 4, 256)[:, 64:], (2, 0, 1, 3)).reshape(
            4, N_HEADS * 64, 256),
        "w_conv_out": d_w_co,
        "w_uq": d_w_uq_t.reshape(4, 2, HEAD_PAD, Q_RANK)[:, :, :QK_DIM].reshape(4, 2 * QK_DIM, Q_RANK).astype(BF16),
        "w_ukv": d_w_ukv}, dnkv)

    def f_lowrank_bwd(ids, ckv, cq, dkv_, dq_, gkv, gq, ga, gc, cx, cb, cc, kr):
        rk, rq = _rms(ckv), _rms(cq)
        nk, nq_ = ckv * rk, cq * rq
        lat = ids[0] < n_lat
        pieces = [jnp.where(lat, a, jnp.zeros_like(a)) for a in (ga, gc, cx, cb, cc)]
        pieces += [_rms_bwd(dkv_ * gkv, nk, rk).astype(BF16), _rms_bwd(dq_ * gq, nq_, rq).astype(BF16), kr.astype(BF16)]
        return jnp.concatenate(pieces, axis=1), _colsum(dkv_ * nk), _colsum(dq_ * nq_)

    lat_rows = lambda n: pl.BlockSpec((ROW_TILE, n), lambda i: (jnp.minimum(i, n_lat - 1), 0))
    dpp, dg_kv, dg_q = _ew(
        f_lowrank_bwd, (n_all,), [(pp, _rows(KV_RANK, KV0 // KV_RANK)), (pp, _rows(Q_RANK, Q0 // Q_RANK)),
                                  (dnkv, _rows(KV_RANK)), (dnq, _rows(Q_RANK)), (W["kv_norm_g"], vec(KV_RANK)),
                                  (W["q_norm_g"], vec(Q_RANK)), (dp_ga, lat_rows(D_MODEL)), (dp_gc, lat_rows(D_MODEL)),
                                  (dp_cx, lat_rows(CONV_DIM)), (dp_cb, lat_rows(CONV_DIM)), (dp_cc, lat_rows(CONV_DIM)),
                                  (dp_kr, _rows(HEAD_PAD))],
        [row_out(P_COLS, BF16, TT), acc_out(KV_RANK), acc_out(Q_RANK)], "lowrank_norm_bwd")
    d_w_in_t = _mm(hh, dpp, "tn", D_MODEL, P_COLS, TT, tm=512, tn=2176, tk=TT, name="w_in_dw", out_dtype=BF16,
                   transpose_out=True)
    dhh = _mm(dpp, W["w_in_t"], "nn", TT, D_MODEL, P_COLS, tm=tm_all, tn=512, tk=2176, name="w_in_dx")

    def f_norm1_bwd(ids, x, dh, dres, g, sc):
        r = _rms(x)
        xn = x * r
        return (dres + _rms_bwd(dh * g * (1.0 + sc), xn, r), _colsum(dh), _colsum(dh * xn * g),
                _colsum(dh * xn * (1.0 + sc)))

    grad_x, dsh1, dsc1, dg_n1 = _ew(
        f_norm1_bwd, (n_lat,), [(xx, _rows(D_MODEL)), (dhh, _rows(D_MODEL)), (dx1, _rows(D_MODEL)),
                                (W["norm1_g"], vec(D_MODEL)), (sc1, vec(D_MODEL))],
        [row_out(D_MODEL, F32)] + [acc_out(D_MODEL)] * 3, "norm1_bwd")

    def f_norm1_ctx_bwd(ids, x, dh, g, sc):
        xn = x * _rms(x)
        return _colsum(dh), _colsum(dh * xn * g), _colsum(dh * xn * (1.0 + sc))

    n_ctx = n_all - n_lat
    dcsh1, dcsc1, dg_n1c = _ew(
        f_norm1_ctx_bwd, (n_ctx,), [(xx, _rows(D_MODEL, 0, n_lat)), (dhh, _rows(D_MODEL, 0, n_lat)),
                                    (W["norm1_g"], vec(D_MODEL)), (csc1, vec(D_MODEL))], [acc_out(D_MODEL)] * 3,
        "norm1_ctx_bwd")

    big = {"w_in": _w_in_t_shards_from_p(d_w_in_t).astype(BF16)}
    zero = jnp.zeros((1, 4 * D_MODEL), F32)
    small = {
        "dmod_lat": jnp.concatenate([dsh1, dsc1, dg1, dsh2, dsc2, dg2], axis=1),
        "dmod_ctx": jnp.concatenate([dcsh1, dcsc1, zero], axis=1),
        "norm1_g": dg_n1 + dg_n1c, "norm2_g": dg_n2, "final_g": dg_f, "q_norm_g": dg_q, "kv_norm_g": dg_kv,
        "conv_b": d_conv_b, "conv_w": d_conv_w.reshape(1, -1), "ffn_conv_b": d_ffn_conv_b,
        "ffn_conv_w": d_ffn_conv_w.reshape(1, -1),
    }
    return grad_x, loss, big, small


SMALL = (("dmod_lat", 6144), ("dmod_ctx", 6144), ("norm1_g", 1024), ("norm2_g", 1024), ("final_g", 1024),
         ("q_norm_g", 384), ("kv_norm_g", 256), ("conv_b", 512), ("conv_w", 1536), ("ffn_conv_b", 5632),
         ("ffn_conv_w", 16896), ("loss", 128))
SMALL_ROWS = 320


def _adam_update(w, g, m, v):
    c1, c2 = 1.0 - ADAM_B1 ** ADAM_STEP, 1.0 - ADAM_B2 ** ADAM_STEP
    m2 = ADAM_B1 * m + (1.0 - ADAM_B1) * g
    v2 = ADAM_B2 * v + (1.0 - ADAM_B2) * (g * g)
    return [-ADAM_LR * ((m2 / c1) / (jnp.sqrt(v2 / c2) + ADAM_EPS) + ADAM_WD * w), m2, v2]


def _adamw(w, g, m, v, name):
    R, C = w.shape
    tr = 8 if R % 8 == 0 else R
    for t in range(8, R + 1, 8):
        if R % t == 0 and t * C * 4 <= (1 << 20):
            tr = t
    spec = pl.BlockSpec((tr, C), lambda i: (i, 0))
    return _ew(lambda ids, *vals: _adam_update(*vals), (R // tr,), [(w, spec), (g, spec), (m, spec), (v, spec)],
               [((R, C), F32, spec, None)] * 3, name)


def kernel(x, c, ctx, c_ctx, w_ada, b_ada, norm1_g, w_in, q_norm_g, kv_norm_g, w_uq, w_ukv, conv_w, conv_b, w_attn_out, w_conv_out, w_o, norm2_g, w_up, ffn_conv_w, ffn_conv_b, w_down, final_g, loss_target, m_c_ctx, m_w_ada, m_b_ada, m_norm1_g, m_w_in, m_q_norm_g, m_kv_norm_g, m_w_uq, m_w_ukv, m_conv_w, m_conv_b, m_w_attn_out, m_w_conv_out, m_w_o, m_norm2_g, m_w_up, m_ffn_conv_w, m_ffn_conv_b, m_w_down, m_final_g, v_c_ctx, v_w_ada, v_b_ada, v_norm1_g, v_w_in, v_q_norm_g, v_kv_norm_g, v_w_uq, v_w_ukv, v_conv_w, v_conv_b, v_w_attn_out, v_w_conv_out, v_w_o, v_norm2_g, v_w_up, v_ffn_conv_w, v_ffn_conv_b, v_w_down, v_final_g):
    mx, my, mc = lax.axis_index("x"), lax.axis_index("y"), lax.axis_index("c")
    chip = 2 * mx + my
    dev = 4 * mx + 2 * my + mc
    T, Tc = x.shape[1], ctx.shape[1]
    TT = T + Tc
    w_in_t, m_w_in_t, v_w_in_t = (jnp.transpose(a[0]) for a in (w_in, m_w_in, v_w_in))
    w_uq_t, m_w_uq_t, v_w_uq_t = (jnp.transpose(a[0]) for a in (w_uq, m_w_uq, v_w_uq))
    shards = {"w_in": jnp.pad(w_in_t, ((0, W_IN_SHARD_PAD - W_IN_SHARD), (0, 0))), "w_uq": w_uq_t, "w_ukv": w_ukv[0],
              "w_attn_out": w_attn_out[0], "w_conv_out": w_conv_out[0], "w_o": w_o[0], "w_up": w_up[0],
              "w_down": w_down[0]}

    conv_sh = jnp.concatenate([conv_w[0], ffn_conv_w[0]], axis=1)
    pay1 = jnp.concatenate([jnp.pad(c, ((0, 7), (0, 0))), jnp.pad(conv_sh, ((0, 5), (0, 0)))], axis=1)
    got1 = _allgather8(pay1, "gather_cond", in_vmem=True)[0].reshape(8, 8, 2560)
    c_all = got1[:, 0, :D_MODEL]
    conv_all = got1[0::2, :3, D_MODEL:]
    conv_w_full = _cols_from_shards(conv_all[:, :, :128])
    ffn_conv_w_full = _cols_from_shards(conv_all[:, :, 128:])

    cond = jnp.concatenate([c_all, c_ctx.reshape(1, D_MODEL), jnp.zeros((7, D_MODEL), F32)], axis=0)

    def f_silu(ids, v):
        return (v * _sigmoid(v),)

    (s16,) = _ew(f_silu, (1,), [(cond, _full((16, D_MODEL)))], [((16, D_MODEL), F32, _full((16, D_MODEL)), None)], "silu_cond")
    mod_sh = _mm(s16, w_ada[0], "nn", 16, 1536, D_MODEL, tm=16, tn=768, tk=D_MODEL, name="w_ada_fwd")
    m_send, m_recv, m_src, m_land, ukv_thru = _ici_start("all", [mod_sh], [(8, 16, 1536)], w_ukv[0], "mod_start")
    shards["w_ukv"] = ukv_thru

    names = [n for n, _ in BIG]
    first = [n for n in names if n not in GATHER_LATE]
    gathered, zero = _gather_weights([shards[n].astype(BF16) for n in first])
    full = dict(zip(first, gathered))
    (mod_mine,), (m_land,) = _ici_wait("all", m_send, m_recv, m_src, m_land, gathered[0], "mod_wait")
    got2 = lax.dynamic_update_slice(m_land, mod_mine[None], (dev, 0, 0))
    mod_all = _cols_from_shards(got2[0::2]) + b_ada
    mod_lat = lax.dynamic_slice_in_dim(mod_all, dev, 1, axis=0)
    mod_ctx = mod_all[8:9]
    xx = jnp.concatenate([x[0], ctx[0]], axis=0)
    late_bf = [(shards[n] + zero[0, 0]).astype(BF16) for n in GATHER_LATE]
    g_send, g_recv, late_src, late_land, xx = _ici_start(
        "gather", late_bf, [(4,) + s.shape for s in late_bf], xx, "gather_late_start")

    def late_weights(after):
        src, land = _ici_wait("gather", g_send, g_recv, late_src, late_land, after, "gather_late_wait")
        got = dict(zip(GATHER_LATE, _gather_finish(src, land)))
        wao = _cols_from_shards(got["w_attn_out"]).reshape(N_HEADS, 64, D_MODEL)
        return {"w_attn_out": jnp.pad(wao, ((0, 0), (64, 0), (0, 0))).reshape(N_HEADS * HEAD_PAD, D_MODEL),
                "w_conv_out": got["w_conv_out"], "w_o": got["w_o"].reshape(D_MODEL, D_MODEL), "w_up": got["w_up"],
                "w_down": got["w_down"].reshape(D_FF, D_MODEL)}

    wuq_t = full["w_uq"].reshape(N_HEADS, QK_DIM, Q_RANK)
    W = {
        "w_in_t": _w_in_t_p_from_shards(full["w_in"]),
        "w_uq_t": jnp.pad(wuq_t, ((0, 0), (0, HEAD_PAD - QK_DIM), (0, 0))).reshape(N_HEADS * HEAD_PAD, Q_RANK),
        "w_ukv": full["w_ukv"],
        "norm1_g": norm1_g, "norm2_g": norm2_g, "final_g": final_g.reshape(1, D_MODEL), "q_norm_g": q_norm_g,
        "kv_norm_g": kv_norm_g, "conv_w": conv_w_full, "conv_b": conv_b, "ffn_conv_w": ffn_conv_w_full,
        "ffn_conv_b": ffn_conv_b,
    }

    place = jnp.stack([chip, mc]).astype(jnp.int32)
    early = {}

    def early_grads(tag, g, carry):
        gs = list(g.values())
        from_sib = _rs_pair(gs, "rs_pair_" + tag)
        sums = [_add_pair(gs[w], from_sib[w], place, "rs_pair_add_" + n) for w, n in enumerate(g)]
        send, recv, sums, land, carry = _ici_start(
            "scatter", sums, [(3,) + s.shape[1:] for s in sums], carry, "rs_chips_" + tag + "_start")
        early[tag] = (list(g), send, recv, sums, land)
        return carry

    grad_x, loss_part, gbig, gsmall = _local_step(xx, loss_target[0], mod_lat, mod_ctx, W, late_weights, early_grads)

    gsmall["loss"] = loss_part
    pay3 = jnp.concatenate([gsmall[n].reshape(-1) for n, _ in SMALL])
    pay3 = jnp.pad(pay3, (0, SMALL_ROWS * 128 - pay3.shape[0])).reshape(SMALL_ROWS, 128)
    s_send, s_recv, s_src, s_land, w_in_thru = _ici_start("all", [pay3], [(8, SMALL_ROWS, 128)], gbig["w_in"],
                                                         "small_start")
    gbig = {"w_in": w_in_thru}

    last = list(gbig)
    from_sibling = _rs_pair([gbig[n] for n in last], "rs_pair")
    pair_sums = [_add_pair(gbig[n], from_sibling[w], place, "rs_pair_add_" + n) for w, n in enumerate(last)]
    lands = _rs_chips(pair_sums)

    (pay3,), (s_land,) = _ici_wait("all", s_send, s_recv, s_src, s_land, lands[0], "small_wait")
    got3 = lax.dynamic_update_slice(s_land, pay3[None], (dev, 0, 0)).reshape(8 * SMALL_ROWS, 128)

    def f_sum8(ids, a):
        s = a[0:SMALL_ROWS]
        for d in range(1, 8):
            s = s + a[d * SMALL_ROWS:(d + 1) * SMALL_ROWS]
        return (s,)

    (vsum,) = _ew(f_sum8, (1,), [(got3, _full((8 * SMALL_ROWS, 128)))],
                  [((SMALL_ROWS, 128), F32, _full((SMALL_ROWS, 128)), None)], "sum_small")
    vflat = vsum.reshape(-1)
    gvec, off = {}, 0
    for n, size in SMALL:
        gvec[n] = vflat[off:off + size]
        off += size
    loss = gvec["loss"][0]
    dmod_rows = got3.reshape(8, SMALL_ROWS * 128)[:, :6 * D_MODEL]
    dm16 = jnp.concatenate([dmod_rows, gvec["dmod_ctx"].reshape(1, -1), jnp.zeros((7, 6 * D_MODEL), F32)], axis=0)

    def f_colsum(ids, a):
        return (_colsum(a),)

    (g_b_ada,) = _ew(f_colsum, (1,), [(dm16, _full((16, 6 * D_MODEL)))],
                     [((1, 6 * D_MODEL), F32, _full((1, 6 * D_MODEL)), None)], "b_ada_grad")
    dm_sh = lax.dynamic_slice_in_dim(dm16, chip * 1536, 1536, axis=1)
    g_w_ada = _mm(s16, dm_sh, "tn", D_MODEL, 1536, 16, tm=512, tn=768, tk=16, name="w_ada_dw")
    dcond_part = _mm(dm_sh, w_ada[0], "nt", 16, D_MODEL, 1536, tm=16, tn=512, tk=1536, name="w_ada_dx")
    d_send, d_recv, d_src, d_land, lands[0] = _ici_start("all", [dcond_part[8:16]], [(8, 8, D_MODEL)], lands[0],
                                                         "dcond_start")
    done = last
    for tag, (tag_names, send, recv, sums, land) in early.items():
        sums, land = _ici_wait("scatter", send, recv, sums, land, grad_x, "rs_chips_" + tag + "_wait")
        done, pair_sums, lands = done + tag_names, pair_sums + sums, lands + land
    half_sums = [_add_chips(a, b, place, "rs_chip_add_" + n) for a, b, n in zip(pair_sums, lands, done)]
    gw = dict(zip(done, _rs_pair_back(half_sums)))
    gw["w_ada"] = g_w_ada

    moments = {"w_ada": (w_ada, m_w_ada, v_w_ada), "w_ukv": (w_ukv, m_w_ukv, v_w_ukv),
               "w_attn_out": (w_attn_out, m_w_attn_out, v_w_attn_out),
               "w_conv_out": (w_conv_out, m_w_conv_out, v_w_conv_out), "w_o": (w_o, m_w_o, v_w_o),
               "w_up": (w_up, m_w_up, v_w_up), "w_down": (w_down, m_w_down, v_w_down)}
    grads, deltas, new_m, new_v = {}, {}, {}, {}
    for n, (w_, m_, v_) in moments.items():
        d_, m2, v2 = _adamw(w_[0], gw[n], m_[0], v_[0], "adamw_" + n)
        grads[n], deltas[n], new_m[n], new_v[n] = gw[n][None], d_[None], m2[None], v2[None]
    for n, (w_, m_, v_) in {"w_in": (w_in_t, m_w_in_t, v_w_in_t), "w_uq": (w_uq_t, m_w_uq_t, v_w_uq_t)}.items():
        d_, m2, v2 = _adamw(w_, gw[n], m_, v_, "adamw_" + n)
        back = lambda a: jnp.transpose(a)[None]
        grads[n], deltas[n], new_m[n], new_v[n] = back(gw[n][:w_.shape[0]]), back(d_), back(m2), back(v2)

    (dcond_mine,), (d_land,) = _ici_wait("all", d_send, d_recv, d_src, d_land, deltas["w_up"], "dcond_wait")
    got4 = lax.dynamic_update_slice(d_land, dcond_mine[None], (dev, 0, 0))[0::2, 0]

    def f_c_ctx(ids, parts, cc):
        s = _sigmoid(cc)
        d = parts[0:1] + parts[1:2] + parts[2:3] + parts[3:4]
        return (d * s * (1.0 + cc * (1.0 - s)),)

    (g_c_ctx,) = _ew(f_c_ctx, (1,), [(got4, _full((4, D_MODEL))), (c_ctx.reshape(1, D_MODEL), _full((1, D_MODEL)))],
                     [((1, D_MODEL), F32, _full((1, D_MODEL)), None)], "c_ctx_grad")

    conv_w_g = lax.dynamic_slice_in_dim(gvec["conv_w"].reshape(3, CONV_DIM), chip * 128, 128, axis=1)
    ffn_conv_w_g = lax.dynamic_slice_in_dim(gvec["ffn_conv_w"].reshape(3, 2 * D_FF), chip * 1408, 1408, axis=1)
    vec_params = (("c_ctx", c_ctx, m_c_ctx, v_c_ctx, g_c_ctx), ("b_ada", b_ada, m_b_ada, v_b_ada, g_b_ada),
                  ("norm1_g", norm1_g, m_norm1_g, v_norm1_g, gvec["norm1_g"]),
                  ("q_norm_g", q_norm_g, m_q_norm_g, v_q_norm_g, gvec["q_norm_g"]),
                  ("kv_norm_g", kv_norm_g, m_kv_norm_g, v_kv_norm_g, gvec["kv_norm_g"]),
                  ("conv_w", conv_w, m_conv_w, v_conv_w, conv_w_g), ("conv_b", conv_b, m_conv_b, v_conv_b, gvec["conv_b"]),
                  ("norm2_g", norm2_g, m_norm2_g, v_norm2_g, gvec["norm2_g"]),
                  ("ffn_conv_w", ffn_conv_w, m_ffn_conv_w, v_ffn_conv_w, ffn_conv_w_g),
                  ("ffn_conv_b", ffn_conv_b, m_ffn_conv_b, v_ffn_conv_b, gvec["ffn_conv_b"]),
                  ("final_g", final_g, m_final_g, v_final_g, gvec["final_g"]))
    two_d = lambda a: a.reshape((-1, a.shape[-1]))

    def f_adam_many(ids, *vals):
        out = []
        for k in range(len(vec_params)):
            out += _adam_update(*vals[4 * k:4 * k + 4])
        return out

    ins_v, outs_v = [], []
    for p in vec_params:
        shp = two_d(p[1]).shape
        ins_v += [(two_d(a), _full(shp)) for a in (p[1], p[4], p[2], p[3])]
        outs_v += [(shp, F32, _full(shp), None)] * 3
    res_v = _ew(f_adam_many, (1,), ins_v, outs_v, "adamw_vectors")
    for k, p in enumerate(vec_params):
        n, shape = p[0], p[1].shape
        grads[n] = p[4].reshape(shape)
        deltas[n], new_m[n], new_v[n] = (r.reshape(shape) for r in res_v[3 * k:3 * k + 3])

    order = ("c_ctx", "w_ada", "b_ada", "norm1_g", "w_in", "q_norm_g", "kv_norm_g", "w_uq", "w_ukv", "conv_w", "conv_b",
             "w_attn_out", "w_conv_out", "w_o", "norm2_g", "w_up", "ffn_conv_w", "ffn_conv_b", "w_down", "final_g")
    return (loss, grad_x[None], *[grads[n] for n in order], *[deltas[n] for n in order],
            *[new_m[n] for n in order], *[new_v[n] for n in order])
```

```python
import functools

import jax
import jax.numpy as jnp
from jax import lax
from jax.experimental import pallas as pl
from jax.experimental.pallas import tpu as pltpu

F32, BF16 = jnp.float32, jnp.bfloat16
MESH = pl.DeviceIdType.MESH

D_MODEL = 1024
N_HEADS = 8
HEAD_PAD = 128
QK_DIM = 96
Q_RANK, KV_RANK = 384, 256
CONV_DIM = 512
D_FF = 2816
GRID_W = 64
ROPE_THETA = 10000.0
EPS = 1e-6
GA0, GC0, CX0, CB0, CC0, KV0, Q0, KR0, P_COLS = 0, 1024, 2048, 2560, 3072, 3584, 3840, 4224, 4352
ROW_TILE = 256
VMEM_LIMIT_BYTES = 48 * 1024 * 1024

ADAM_LR, ADAM_B1, ADAM_B2, ADAM_EPS, ADAM_WD, ADAM_STEP = 0.001, 0.9, 0.999, 1e-08, 0.01, 10

BIG = (("w_in", (1088, 1024)), ("w_uq", (192, 384)), ("w_ukv", (256, 256)), ("w_attn_out", (512, 256)),
       ("w_conv_out", (512, 256)), ("w_o", (256, 1024)), ("w_up", (1024, 1408)), ("w_down", (704, 1024)))

GATHER_LATE = ("w_attn_out", "w_conv_out", "w_o", "w_up", "w_down")

NN = (((1,), (0,)), ((), ()))
NT = (((1,), (1,)), ((), ()))
TN = (((0,), (0,)), ((), ()))


def _cp(sem):
    return pltpu.CompilerParams(dimension_semantics=sem, vmem_limit_bytes=VMEM_LIMIT_BYTES)


PIN_BYTES = 1 << 19


def _in_hbm(arrays):
    return [pltpu.with_memory_space_constraint(a, pltpu.HBM) if a.size * a.dtype.itemsize >= PIN_BYTES else a
            for a in arrays]


def _out(shape, dtype):
    n = 1
    for d in shape:
        n *= d
    big = n * jnp.dtype(dtype).itemsize >= PIN_BYTES
    return pltpu.HBM(shape, dtype) if big else jax.ShapeDtypeStruct(shape, dtype)


def _pick(n, prefs):
    for p in prefs:
        if n % p == 0:
            return p
    return n


def _mm(a, b, mode, M, N, K, *, tm, tn, tk, name, out_dtype=F32, a_spec=None, b_spec=None, o_spec=None,
        out_shape=None, transpose_out=False):
    assert M % tm == 0 and N % tn == 0 and K % tk == 0, (name, M, N, K, tm, tn, tk)
    nk = K // tk
    dims = {"nn": NN, "nt": NT, "tn": TN}[mode]
    if a_spec is None:
        a_spec = (pl.BlockSpec((tk, tm), lambda i, j, k: (k, i)) if mode == "tn"
                  else pl.BlockSpec((tm, tk), lambda i, j, k: (i, k)))
    if b_spec is None:
        b_spec = (pl.BlockSpec((tn, tk), lambda i, j, k: (j, k)) if mode == "nt"
                  else pl.BlockSpec((tk, tn), lambda i, j, k: (k, j)))
    if o_spec is None:
        o_spec = (pl.BlockSpec((tn, tm), lambda i, j, k: (j, i)) if transpose_out
                  else pl.BlockSpec((tm, tn), lambda i, j, k: (i, j)))
    if out_shape is None:
        out_shape = (N, M) if transpose_out else (M, N)

    def emit(o_ref, val):
        o_ref[...] = (val.T if transpose_out else val).astype(o_ref.dtype)

    def body(a_ref, b_ref, o_ref, *scratch):
        part = lax.dot_general(a_ref[...].astype(BF16), b_ref[...].astype(BF16), dims, preferred_element_type=F32)
        if nk == 1:
            emit(o_ref, part)
            return
        acc_ref, = scratch
        k = pl.program_id(2)

        @pl.when(k == 0)
        def _():
            acc_ref[...] = part

        @pl.when((k > 0) & (k < nk - 1))
        def _():
            acc_ref[...] += part

        @pl.when(k == nk - 1)
        def _():
            emit(o_ref, acc_ref[...] + part)

    return pl.pallas_call(
        body, grid=(M // tm, N // tn, nk), in_specs=[a_spec, b_spec], out_specs=o_spec,
        out_shape=_out(out_shape, out_dtype),
        scratch_shapes=[pltpu.VMEM((tm, tn), F32)] if nk > 1 else [],
        compiler_params=_cp(("parallel", "parallel", "arbitrary")), name=name)(*_in_hbm([a, b]))


def _ew(fn, grid, ins, outs, name, scalars=None):
    n_in = len(ins)
    n_sc = 0 if scalars is None else 1

    def store(ref, val, acc, ids):
        if isinstance(val, (list, tuple)):
            for h, v in enumerate(val):
                ref[h] = v.astype(ref.dtype)
            return
        if acc is None:
            ref[...] = val.astype(ref.dtype)
            return

        @pl.when(ids[acc] == 0)
        def _():
            ref[...] = val.astype(ref.dtype)

        @pl.when(ids[acc] > 0)
        def _():
            ref[...] += val.astype(ref.dtype)

    def body(*refs):
        refs = refs[n_sc:]
        ids = tuple(pl.program_id(a) for a in range(len(grid)))
        vals = fn(ids, *[r[...] for r in refs[:n_in]])
        for ref, val, (_, _, _, acc) in zip(refs[n_in:], vals, outs):
            store(ref, val, acc, ids)

    acc_axes = {o[3] for o in outs if o[3] is not None}
    sem = tuple("arbitrary" if a in acc_axes else "parallel" for a in range(len(grid)))
    in_specs, out_specs = [s for _, s in ins], [o[2] for o in outs]
    out_shape = [_out(o[0], o[1]) for o in outs]
    args = _in_hbm([a for a, _ in ins])
    if scalars is None:
        return pl.pallas_call(body, grid=grid, in_specs=in_specs, out_specs=out_specs, out_shape=out_shape,
                              compiler_params=_cp(sem), name=name)(*args)
    spec = pltpu.PrefetchScalarGridSpec(num_scalar_prefetch=1, grid=grid, in_specs=in_specs, out_specs=out_specs)
    return pl.pallas_call(body, grid_spec=spec, out_shape=out_shape, compiler_params=_cp(sem), name=name)(scalars, *args)


def _rows(width, cblk=0, roff=0, tr=ROW_TILE):
    return pl.BlockSpec((tr, width), lambda i: (i + roff, cblk))


def _full(shape):
    nd = len(shape)
    return pl.BlockSpec(shape, lambda *_: (0,) * nd)


def _sigmoid(x):
    return 1.0 / (1.0 + jnp.exp(-x))


def _rms(x):
    return lax.rsqrt(jnp.mean(x * x, axis=-1, keepdims=True) + EPS)


def _rms_bwd(dn, xn, r):
    return r * (dn - xn * jnp.mean(dn * xn, axis=-1, keepdims=True))


def _colsum(x):
    return jnp.sum(x, axis=0, keepdims=True)


def _shifts(x):
    n = x.shape[0]
    rows = lax.broadcasted_iota(jnp.int32, x.shape, 0)
    return jnp.where(rows == 0, 0.0, pltpu.roll(x, 1, 0)), jnp.where(rows == n - 1, 0.0, pltpu.roll(x, n - 1, 0))


def _conv(x, w, b, shifted=None):
    prev, nxt = _shifts(x) if shifted is None else shifted
    return b + prev * w[0:1] + x * w[1:2] + nxt * w[2:3]


def _conv_bwd_x(dy, w):
    prev, nxt = _shifts(dy)
    return nxt * w[0:1] + dy * w[1:2] + prev * w[2:3]


def _conv_bwd_w(dy, x, shifted):
    prev, nxt = shifted
    return _colsum(dy * prev), _colsum(dy * x), _colsum(dy * nxt)


def _rope(x, cos, sin_lo, sin_hi):
    return x * cos + pltpu.roll(x, HEAD_PAD - 8, 1) * sin_lo + pltpu.roll(x, 8, 1) * sin_hi


ATTN_SCALE = QK_DIM ** -0.5
LOG2_E = 1.4426950408889634


def _head_keys(kv_ref, kr_ref, cos_ref, slo_ref, shi_ref, kc_ref, vp_ref):
    kv = kv_ref[...]
    lane = lax.broadcasted_iota(jnp.int32, kv.shape, 1)
    kc_ref[...] = jnp.where(lane < 64, kv, _rope(kr_ref[...], cos_ref[...], slo_ref[...], shi_ref[...])).astype(BF16)
    vp_ref[...] = jnp.where(lane >= 64, kv, 0.0).astype(BF16)


def _attn_specs(tq, TT, clamp):
    row = (lambda i: jnp.minimum(i, clamp)) if clamp is not None else (lambda i: i)
    q = pl.BlockSpec((tq, HEAD_PAD), lambda h, i: (i, h))
    lat = pl.BlockSpec((tq, HEAD_PAD), lambda h, i: (row(i), h))
    keys = pl.BlockSpec((TT, HEAD_PAD), lambda h, i: (0, h))
    kr = pl.BlockSpec((TT, HEAD_PAD), lambda h, i: (0, KR0 // HEAD_PAD))
    tab_q = pl.BlockSpec((tq, HEAD_PAD), lambda h, i: (i, 0))
    tab_k = pl.BlockSpec((TT, HEAD_PAD), lambda h, i: (0, 0))
    lse = pl.BlockSpec((None, tq, 1), lambda h, i: (h, row(i), 0))
    return q, lat, keys, kr, tab_q, tab_k, lse


def _attn_fwd(q_raw, kv, pp, tabs, T, TT):
    tq = ROW_TILE
    cos, slo, shi = tabs

    def body(q_ref, kv_ref, kr_ref, cq, lq, hq, ck, lk, hk, o_ref, l_ref, kc, vp):
        @pl.when(pl.program_id(1) == 0)
        def _():
            _head_keys(kv_ref, kr_ref, ck, lk, hk, kc, vp)

        q = _rope(q_ref[...], cq[...], lq[...], hq[...]).astype(BF16)
        s = lax.dot_general(q, kc[...], NT, preferred_element_type=F32)
        m = jnp.max(s, axis=-1, keepdims=True)
        p = jnp.exp2((s - m) * (ATTN_SCALE * LOG2_E))
        l = jnp.sum(p, axis=-1, keepdims=True)
        o = lax.dot_general(p.astype(BF16), vp[...], NN, preferred_element_type=F32)
        o_ref[...] = o / l
        l_ref[...] = m * ATTN_SCALE + jnp.log(l)

    qs, _, keys, kr, tab_q, tab_k, lse = _attn_specs(tq, TT, None)
    return pl.pallas_call(
        body, grid=(N_HEADS, T // tq), in_specs=[qs, keys, kr, tab_q, tab_q, tab_q, tab_k, tab_k, tab_k],
        out_specs=[qs, lse],
        out_shape=[jax.ShapeDtypeStruct((T, N_HEADS * HEAD_PAD), F32), jax.ShapeDtypeStruct((N_HEADS, T, 1), F32)],
        scratch_shapes=[pltpu.VMEM((TT, HEAD_PAD), BF16), pltpu.VMEM((TT, HEAD_PAD), BF16)],
        compiler_params=_cp(("parallel", "arbitrary")), name="attn_fwd",
    )(*_in_hbm([q_raw, kv, pp, cos, slo, shi, cos, slo, shi]))


def _attn_bwd(q_raw, kv, pp, o, do, lse, tabs, tabs_inv, T, TT):
    tq = ROW_TILE
    nq = T // tq
    cos, slo, shi = tabs
    cos_i, slo_i, shi_i = tabs_inv

    def body(q_ref, kv_ref, kr_ref, cq, lq, hq, ck, lk, hk, iq, ilq, ihq, ik, ilk, ihk, o_ref, do_ref, l_ref,
             dq_ref, dkv_ref, dkr_ref, kc, vp, dk, dv):
        h, i = pl.program_id(0), pl.program_id(1)

        @pl.when(i == 0)
        def _():
            _head_keys(kv_ref, kr_ref, ck, lk, hk, kc, vp)
            dk[...] = jnp.zeros_like(dk)
            dv[...] = jnp.zeros_like(dv)

        @pl.when(i < nq)
        def _():
            q = _rope(q_ref[...], cq[...], lq[...], hq[...]).astype(BF16)
            k, v, d_o = kc[...], vp[...], do_ref[...]
            s = lax.dot_general(q, k, NT, preferred_element_type=F32)
            p = jnp.exp2(s * (ATTN_SCALE * LOG2_E) - l_ref[...] * LOG2_E)
            dob = d_o.astype(BF16)
            dp = lax.dot_general(dob, v, NT, preferred_element_type=F32)
            dd = jnp.sum(d_o * o_ref[...], axis=-1, keepdims=True)
            ds = (p * (dp - dd) * ATTN_SCALE).astype(BF16)
            dq = lax.dot_general(ds, k, NN, preferred_element_type=F32)
            dq_ref[...] = _rope(dq, iq[...], ilq[...], ihq[...]).astype(dq_ref.dtype)
            dk[...] += lax.dot_general(ds, q, TN, preferred_element_type=F32)
            dv[...] += lax.dot_general(p.astype(BF16), dob, TN, preferred_element_type=F32)

        @pl.when(i == nq)
        def _():
            dq_ref[...] = jnp.zeros_like(dq_ref)
            dkh = dk[...]
            lane = lax.broadcasted_iota(jnp.int32, dkh.shape, 1)
            dkv_ref[...] = jnp.where(lane < 64, dkh, dv[...]).astype(dkv_ref.dtype)
            rot = _rope(jnp.where((lane >= 64) & (lane < 96), dkh, 0.0), ik[...], ilk[...], ihk[...])

            @pl.when(h == 0)
            def _():
                dkr_ref[...] = rot

            @pl.when(h > 0)
            def _():
                dkr_ref[...] += rot

    qs, lat, keys, kr, tab_q, tab_k, lse_spec = _attn_specs(tq, TT, nq - 1)
    wide = jax.ShapeDtypeStruct((TT, N_HEADS * HEAD_PAD), BF16)
    return pl.pallas_call(
        body, grid=(N_HEADS, TT // tq),
        in_specs=[qs, keys, kr] + [tab_q] * 3 + [tab_k] * 3 + [tab_q] * 3 + [tab_k] * 3 + [lat, lat, lse_spec],
        out_specs=[qs, keys, pl.BlockSpec((TT, HEAD_PAD), lambda h, i: (0, 0))],
        out_shape=[wide, wide, jax.ShapeDtypeStruct((TT, HEAD_PAD), F32)],
        scratch_shapes=[pltpu.VMEM((TT, HEAD_PAD), BF16), pltpu.VMEM((TT, HEAD_PAD), BF16),
                        pltpu.VMEM((TT, HEAD_PAD), F32), pltpu.VMEM((TT, HEAD_PAD), F32)],
        compiler_params=_cp(("arbitrary", "arbitrary")), name="attn_bwd",
    )(*_in_hbm([q_raw, kv, pp, cos, slo, shi, cos, slo, shi, cos_i, slo_i, shi_i, cos_i, slo_i, shi_i, o, do, lse]))


def _allgather8(x, name, in_vmem):
    m_per, n = x.shape

    def body(x_ref, out_ref, token, send_sems, recv_sems, local_sem):
        token[...] = jnp.zeros_like(token)
        mx, my, mc = lax.axis_index("x"), lax.axis_index("y"), lax.axis_index("c")
        me, sibling = (mx, my, mc), (mx, my, 1 - mc)
        chips = [(1 - mx, my), (mx, 1 - my), (1 - mx, 1 - my)]

        def rows(px, py, pc):
            return out_ref.at[pl.ds((4 * px + 2 * py + pc) * m_per, m_per), :]

        def copy(k, block, to, src=None):
            return pltpu.make_async_remote_copy(
                src_ref=rows(*block) if src is None else src, dst_ref=rows(*block),
                send_sem=send_sems.at[k], recv_sem=recv_sems.at[k], device_id=to, device_id_type=MESH)

        mine = pltpu.make_async_copy(x_ref, rows(*me), local_sem)
        mine.start()
        first = [copy(0, me, sibling, src=x_ref)]
        first += [copy(1 + j, me, (*chip, mc), src=x_ref) for j, chip in enumerate(chips)]
        for cp in first:
            cp.start()
        passed = [copy(4 + j, (*chip, mc), sibling) for j, chip in enumerate(chips)]
        for j, chip in enumerate(chips):
            copy(1 + j, (*chip, mc), me).wait_recv()
            passed[j].start()
        copy(0, sibling, me).wait_recv()
        for j, chip in enumerate(chips):
            copy(4 + j, (*chip, 1 - mc), me).wait_recv()
        for cp in first + passed:
            cp.wait_send()
        mine.wait()

    space = pltpu.VMEM if in_vmem else pl.ANY
    return pl.pallas_call(
        body, out_shape=[jax.ShapeDtypeStruct((8 * m_per, n), x.dtype), jax.ShapeDtypeStruct((8, 128), F32)],
        in_specs=[pl.BlockSpec(memory_space=space)],
        out_specs=[pl.BlockSpec(memory_space=space), pl.BlockSpec(memory_space=pltpu.VMEM)],
        scratch_shapes=[pltpu.SemaphoreType.DMA((7,)), pltpu.SemaphoreType.DMA((7,)), pltpu.SemaphoreType.DMA],
        name=name)(x)


def _hbm_specs(n):
    return [pl.BlockSpec(memory_space=pl.ANY)] * n


def _gather_weights(shards):
    n = len(shards)
    halves = [s.shape[0] // 2 for s in shards]

    def body(*refs):
        ins, outs = refs[:n], refs[n:2 * n]
        token, send_sems, recv_sems = refs[2 * n:]
        token[...] = jnp.zeros_like(token)
        mx, my, mc = lax.axis_index("x"), lax.axis_index("y"), lax.axis_index("c")
        j_me = 2 * mx + my
        chips = [(1 - mx, my), (mx, 1 - my), (1 - mx, 1 - my)]

        def half(w, chip_idx, hc):
            return outs[w].at[chip_idx, pl.ds(hc * halves[w], halves[w]), :]

        def copy(w, k, src, dst, to):
            return pltpu.make_async_remote_copy(src_ref=src, dst_ref=dst, send_sem=send_sems.at[w, k],
                                                recv_sem=recv_sems.at[w, k], device_id=to, device_id_type=MESH)

        sends = []
        for w in range(n):
            cp = copy(w, 6, ins[w], outs[w].at[j_me], (mx, my, 1 - mc))
            cp.start()
            sends.append(cp)
        for k, (px, py) in enumerate(chips):
            for w in range(n):
                cp = copy(w, k, ins[w].at[pl.ds(mc * halves[w], halves[w]), :], half(w, j_me, mc), (px, py, mc))
                cp.start()
                sends.append(cp)
        for k, (px, py) in enumerate(chips):
            for w in range(n):
                got = half(w, 2 * px + py, mc)
                copy(w, k, got, got, (px, py, mc)).wait_recv()
                cp = copy(w, 3 + k, got, got, (mx, my, 1 - mc))
                cp.start()
                sends.append(cp)
        for k, (px, py) in enumerate(chips):
            for w in range(n):
                got = half(w, 2 * px + py, 1 - mc)
                copy(w, 3 + k, got, got, (mx, my, 1 - mc)).wait_recv()
        for w in range(n):
            own = outs[w].at[j_me]
            copy(w, 6, own, own, (mx, my, 1 - mc)).wait_recv()
        for cp in sends:
            cp.wait_send()

    res = pl.pallas_call(
        body, out_shape=[jax.ShapeDtypeStruct((4,) + s.shape, s.dtype) for s in shards]
        + [jax.ShapeDtypeStruct((8, 128), F32)],
        in_specs=_hbm_specs(n), out_specs=_hbm_specs(n) + [pl.BlockSpec(memory_space=pltpu.VMEM)],
        scratch_shapes=[pltpu.SemaphoreType.DMA((n, 7)), pltpu.SemaphoreType.DMA((n, 7))],
        name="gather_weights")(*shards)
    return list(res[:n]), res[n]


def _rs_pair(gs, name):
    n = len(gs)
    halves = [g.shape[1] // 2 for g in gs]

    def body(*refs):
        ins, lands = refs[:n], refs[n:2 * n]
        send_sems, recv_sems = refs[2 * n:]
        mx, my, mc = lax.axis_index("x"), lax.axis_index("y"), lax.axis_index("c")
        copies = []
        for w in range(n):
            h = halves[w]
            cp = pltpu.make_async_remote_copy(
                src_ref=ins[w].at[:, pl.ds((1 - mc) * h, h), :], dst_ref=lands[w], send_sem=send_sems.at[w],
                recv_sem=recv_sems.at[w], device_id=(mx, my, 1 - mc), device_id_type=MESH)
            cp.start()
            copies.append(cp)
        for cp in copies:
            cp.wait()

    return pl.pallas_call(
        body, out_shape=[jax.ShapeDtypeStruct((4, h, g.shape[2]), g.dtype) for g, h in zip(gs, halves)],
        in_specs=_hbm_specs(n), out_specs=_hbm_specs(n),
        scratch_shapes=[pltpu.SemaphoreType.DMA((n,)), pltpu.SemaphoreType.DMA((n,))], name=name)(*gs)


def _rs_chips(parts):
    n = len(parts)

    def body(*refs):
        ins, lands = refs[:n], refs[n:2 * n]
        send_sems, recv_sems = refs[2 * n:]
        mx, my, mc = lax.axis_index("x"), lax.axis_index("y"), lax.axis_index("c")
        copies = []
        for k, (px, py) in enumerate([(1 - mx, my), (mx, 1 - my), (1 - mx, 1 - my)]):
            for w in range(n):
                cp = pltpu.make_async_remote_copy(
                    src_ref=ins[w].at[2 * px + py], dst_ref=lands[w].at[k], send_sem=send_sems.at[w, k],
                    recv_sem=recv_sems.at[w, k], device_id=(px, py, mc), device_id_type=MESH)
                cp.start()
                copies.append(cp)
        for cp in copies:
            cp.wait()

    return list(pl.pallas_call(
        body, out_shape=[jax.ShapeDtypeStruct((3,) + p.shape[1:], p.dtype) for p in parts],
        in_specs=_hbm_specs(n), out_specs=_hbm_specs(n),
        scratch_shapes=[pltpu.SemaphoreType.DMA((n, 3)), pltpu.SemaphoreType.DMA((n, 3))], name="rs_chips")(*parts))


def _rs_pair_back(gs, name):
    n = len(gs)

    def body(*refs):
        outs = refs[n:2 * n]
        send_sems, recv_sems = refs[2 * n:]
        mx, my, mc = lax.axis_index("x"), lax.axis_index("y"), lax.axis_index("c")
        copies = []
        for w in range(n):
            h = gs[w].shape[0] // 2
            mine = outs[w].at[pl.ds(mc * h, h), :]
            cp = pltpu.make_async_remote_copy(src_ref=mine, dst_ref=mine, send_sem=send_sems.at[w],
                                              recv_sem=recv_sems.at[w], device_id=(mx, my, 1 - mc), device_id_type=MESH)
            cp.start()
            copies.append(cp)
        for cp in copies:
            cp.wait()

    return pl.pallas_call(
        body, out_shape=[jax.ShapeDtypeStruct(g.shape, g.dtype) for g in gs],
        in_specs=_hbm_specs(n), out_specs=_hbm_specs(n), input_output_aliases={w: w for w in range(n)},
        scratch_shapes=[pltpu.SemaphoreType.DMA((n,)), pltpu.SemaphoreType.DMA((n,))], name=name)(*gs)


_HBM = pl.BlockSpec(memory_space=pltpu.HBM)
_SEM = pl.BlockSpec(memory_space=pltpu.SEMAPHORE)
_EFFECT = pltpu.SideEffectType.DATAFLOW_SIDE_EFFECTING


def _ici_copies(kind, srcs, lands, send_sems, recv_sems):
    n = len(srcs)
    mx, my, mc = lax.axis_index("x"), lax.axis_index("y"), lax.axis_index("c")
    j_me = 2 * mx + my
    copies = []
    if kind == "all":
        for k in range(7):
            a, b, c = (k + 1) >> 2 & 1, (k + 1) >> 1 & 1, (k + 1) & 1
            peer = (1 - mx if a else mx, 1 - my if b else my, 1 - mc if c else mc)
            for w in range(n):
                copies.append(pltpu.make_async_remote_copy(
                    src_ref=srcs[w], dst_ref=lands[w].at[4 * mx + 2 * my + mc], send_sem=send_sems.at[7 * w + k],
                    recv_sem=recv_sems.at[7 * w + k], device_id=peer, device_id_type=MESH))
        return copies
    if kind == "pair":
        for w in range(n):
            h = srcs[w].shape[1] // 2
            copies.append(pltpu.make_async_remote_copy(
                src_ref=srcs[w].at[:, pl.ds((1 - mc) * h, h), :], dst_ref=lands[w], send_sem=send_sems.at[w],
                recv_sem=recv_sems.at[w], device_id=(mx, my, 1 - mc), device_id_type=MESH))
        return copies
    for k, (px, py) in enumerate([(1 - mx, my), (mx, 1 - my), (1 - mx, 1 - my)]):
        for w in range(n):
            if kind == "gather":
                h = srcs[w].shape[0] // 2
                src, dst = srcs[w].at[pl.ds(mc * h, h), :], lands[w].at[j_me, pl.ds(mc * h, h), :]
            else:
                src, dst = srcs[w].at[2 * px + py], lands[w].at[k]
            copies.append(pltpu.make_async_remote_copy(
                src_ref=src, dst_ref=dst, send_sem=send_sems.at[3 * w + k], recv_sem=recv_sems.at[3 * w + k],
                device_id=(px, py, mc), device_id_type=MESH))
    return copies


_SEMS_PER_OPERAND = {"gather": 3, "scatter": 3, "all": 7, "pair": 1}


def _ici_start(kind, srcs, land_shapes, carry, name):
    n = len(srcs)

    def body(*refs):
        ins, lands = refs[:n], refs[n:2 * n]
        send_sems, recv_sems = refs[2 * n + 1], refs[2 * n + 2]
        for cp in _ici_copies(kind, ins, lands, send_sems, recv_sems):
            cp.start()

    hbm = lambda a: pltpu.with_memory_space_constraint(a, pltpu.HBM)
    lands = [lax.empty(s, srcs[0].dtype) for s in land_shapes]
    args = [hbm(a) for a in list(srcs) + lands + [carry]]
    n_sem = _SEMS_PER_OPERAND[kind] * n
    out_shape = ([pltpu.SemaphoreType.DMA((n_sem,)), pltpu.SemaphoreType.DMA((n_sem,))]
                 + [pltpu.HBM(a.shape, a.dtype) for a in args])
    res = pl.pallas_call(
        body, name=name, out_shape=out_shape, in_specs=[_HBM] * len(args), out_specs=[_SEM, _SEM] + [_HBM] * len(args),
        input_output_aliases={i: 2 + i for i in range(len(args))},
        compiler_params=pltpu.CompilerParams(has_side_effects=_EFFECT))(*args)
    return res[0], res[1], list(res[2:2 + n]), list(res[2 + n:2 + 2 * n]), res[2 + 2 * n]


def _ici_wait(kind, send_sems, recv_sems, srcs, lands, after, name):
    n = len(srcs)

    def body(*refs):
        ins, zones = refs[:n], refs[n:2 * n]
        for cp in _ici_copies(kind, ins, zones, refs[2 * n], refs[2 * n + 1]):
            cp.wait_send()
            cp.wait_recv()

    args = list(srcs) + list(lands)
    res = pl.pallas_call(
        body, name=name, out_shape=[pltpu.HBM(a.shape, a.dtype) for a in args],
        in_specs=[_HBM] * len(args) + [_SEM, _SEM, pl.BlockSpec(memory_space=pl.ANY)], out_specs=[_HBM] * len(args),
        input_output_aliases={i: i for i in range(len(args))},
        compiler_params=pltpu.CompilerParams(has_side_effects=_EFFECT))(*args, send_sems, recv_sems, after)
    return list(res[:n]), list(res[n:])


def _gather_finish(shards, lands):
    n = len(shards)

    def body(*refs):
        own, outs = refs[:n], refs[2 * n:3 * n]
        send_sems, recv_sems = refs[3 * n:]
        mx, my, mc = lax.axis_index("x"), lax.axis_index("y"), lax.axis_index("c")
        j_me = 2 * mx + my
        sibling = (mx, my, 1 - mc)
        copies = []

        def push(w, k, src, dst):
            cp = pltpu.make_async_remote_copy(src_ref=src, dst_ref=dst, send_sem=send_sems.at[w, k],
                                              recv_sem=recv_sems.at[w, k], device_id=sibling, device_id_type=MESH)
            cp.start()
            copies.append(cp)

        for w in range(n):
            h = shards[w].shape[0] // 2
            push(w, 3, own[w], outs[w].at[j_me])
            for k, (px, py) in enumerate([(1 - mx, my), (mx, 1 - my), (1 - mx, 1 - my)]):
                got = outs[w].at[2 * px + py, pl.ds(mc * h, h), :]
                push(w, k, got, got)
        for cp in copies:
            cp.wait()

    return pl.pallas_call(
        body, out_shape=[jax.ShapeDtypeStruct(l.shape, l.dtype) for l in lands],
        in_specs=_hbm_specs(2 * n), out_specs=_hbm_specs(n), input_output_aliases={n + w: w for w in range(n)},
        scratch_shapes=[pltpu.SemaphoreType.DMA((n, 4)), pltpu.SemaphoreType.DMA((n, 4))], name="gather_finish",
    )(*shards, *lands)


def _tile_rows(h, c, itemsize, mult):
    best = h
    for t in range(mult, h + 1, mult):
        if h % t == 0 and t * c * itemsize <= (1 << 21):
            best = t
    return best


def _add_pair(g, land, place, name):
    _, h, c = land.shape
    t = _tile_rows(h, c, 2, 16)
    nb = h // t
    return _ew(lambda ids, u, v: (u.astype(F32) + v.astype(F32),), (4, nb),
               [(g, pl.BlockSpec((None, t, c), lambda j, i, s: (j, s[1] * nb + i, 0))),
                (land, pl.BlockSpec((None, t, c), lambda j, i, s: (j, i, 0)))],
               [(land.shape, BF16, pl.BlockSpec((None, t, c), lambda j, i, s: (j, i, 0)), None)], name, scalars=place)[0]


def _add_chips(own, land, place, name):
    _, h, c = land.shape
    t = _tile_rows(h, c, 4, 16)
    nb = h // t

    def fn(ids, a, b):
        return (((a.astype(F32) + b[0].astype(F32)) + b[1].astype(F32)) + b[2].astype(F32),)

    return _ew(fn, (nb,), [(own, pl.BlockSpec((None, t, c), lambda i, s: (s[0], i, 0))),
                           (land, pl.BlockSpec((3, t, c), lambda i, s: (0, i, 0)))],
               [((2 * h, c), F32, pl.BlockSpec((t, c), lambda i, s: (s[1] * nb + i, 0)), None)], name, scalars=place)[0]


W_IN_SEGMENTS = ((0, 256, KV0), (256, 288, KR0 + 64), (288, 672, Q0), (672, 1184, CX0), (1184, 1696, CB0),
                 (1696, 2208, CC0), (2208, 3232, GA0), (3232, 4256, GC0))
W_IN_SHARD = 1064


W_IN_SHARD_PAD = 1088


def _w_in_t_p_from_shards(s):
    pieces = []
    for o0, o1, p0 in sorted(W_IN_SEGMENTS, key=lambda t: t[2]):
        if p0 == KR0 + 64:
            pieces.append(jnp.zeros((64, s.shape[2]), s.dtype))
        for j in range(4):
            lo, hi = max(o0, j * W_IN_SHARD), min(o1, (j + 1) * W_IN_SHARD)
            if lo < hi:
                pieces.append(s[j, lo - j * W_IN_SHARD:hi - j * W_IN_SHARD])
    pieces.append(jnp.zeros((32, s.shape[2]), s.dtype))
    return jnp.concatenate(pieces, axis=0)


def _w_in_t_shards_from_p(g):
    shards = []
    for j in range(4):
        pieces = []
        for o0, o1, p0 in W_IN_SEGMENTS:
            lo, hi = max(o0, j * W_IN_SHARD), min(o1, (j + 1) * W_IN_SHARD)
            if lo < hi:
                pieces.append(g[p0 + lo - o0:p0 + hi - o0])
        pieces.append(jnp.zeros((W_IN_SHARD_PAD - W_IN_SHARD, g.shape[1]), g.dtype))
        shards.append(jnp.concatenate(pieces, axis=0))
    return jnp.stack(shards, axis=0)


def _cols_from_shards(s):
    return jnp.transpose(s, (1, 0, 2)).reshape(s.shape[1], -1)


def _rope_tables(T, TT, inverse):
    rows = T // GRID_W
    row = jnp.repeat(jnp.arange(rows), GRID_W).astype(F32)
    col = jnp.tile(jnp.arange(GRID_W), rows).astype(F32)
    inv = ROPE_THETA ** (-jnp.arange(0, 16, 2, dtype=F32) / 16)
    ang = jnp.concatenate([row[:, None] * inv, col[:, None] * inv], axis=-1)
    cos, sin = jnp.cos(ang), jnp.sin(ang)
    lane = jnp.arange(32)
    src = (lane // 16) * 8 + lane % 8
    lo = ((lane % 16) // 8 == 0).astype(F32)
    sgn = -1.0 if inverse else 1.0
    cos32 = cos[:, src]
    sin_lo32 = -sgn * sin[:, src] * lo
    sin_hi32 = sgn * sin[:, src] * (1.0 - lo)

    def widen(t32, fill):
        t = jnp.concatenate([jnp.full((T, 64), fill, F32), t32, jnp.full((T, 32), fill, F32)], axis=1)
        return jnp.concatenate([t, jnp.full((TT - T, HEAD_PAD), fill, F32)], axis=0)

    return widen(cos32, 1.0), widen(sin_lo32, 0.0), widen(sin_hi32, 0.0)


def _local_step(xx, tgt, mod_lat, mod_ctx, W, late_weights, early_grads):
    TT = xx.shape[0]
    T = tgt.shape[0]
    n_lat, n_all = T // ROW_TILE, TT // ROW_TILE
    sh1, sc1, g1, sh2, sc2, g2 = [mod_lat[:, k * D_MODEL:(k + 1) * D_MODEL] for k in range(6)]
    csh1, csc1 = mod_ctx[:, :D_MODEL], mod_ctx[:, D_MODEL:2 * D_MODEL]
    vec = lambda n: _full((1, n))
    row_out = lambda n, dt, rows=T: ((rows, n), dt, _rows(n), None)
    acc_out = lambda n: ((1, n), F32, _full((1, n)), 0)

    def f_norm1(ids, x, g, a_sh, a_sc, b_sh, b_sc):
        ctx = ids[0] >= n_lat
        sh, sc = jnp.where(ctx, b_sh, a_sh), jnp.where(ctx, b_sc, a_sc)
        return ((x * _rms(x) * g) * (1.0 + sc) + sh,)

    (hh,) = _ew(f_norm1, (n_all,), [(xx, _rows(D_MODEL)), (W["norm1_g"], vec(D_MODEL)), (sh1, vec(D_MODEL)),
                                   (sc1, vec(D_MODEL)), (csh1, vec(D_MODEL)), (csc1, vec(D_MODEL))],
                [row_out(D_MODEL, BF16, TT)], "norm1_fwd")
    tm_all = _pick(TT, (768, 256))
    pp = _mm(hh, W["w_in_t"], "nt", TT, P_COLS, D_MODEL, tm=tm_all, tn=2176, tk=D_MODEL, name="w_in_fwd")

    def f_lowrank(ids, ckv, cq, gkv, gq):
        return ckv * _rms(ckv) * gkv, cq * _rms(cq) * gq

    nkv, nq = _ew(f_lowrank, (n_all,), [(pp, _rows(KV_RANK, KV0 // KV_RANK)), (pp, _rows(Q_RANK, Q0 // Q_RANK)),
                                       (W["kv_norm_g"], vec(KV_RANK)), (W["q_norm_g"], vec(Q_RANK))],
                  [row_out(KV_RANK, BF16, TT), row_out(Q_RANK, BF16, TT)], "lowrank_norm_fwd")
    kv = _mm(nkv, W["w_ukv"], "nn", TT, 1024, KV_RANK, tm=tm_all, tn=256, tk=KV_RANK, name="w_ukv_fwd",
             b_spec=pl.BlockSpec((None, KV_RANK, 256), lambda i, j, k: (j, k, 0)))
    q_raw = _mm(nq, W["w_uq_t"], "nt", TT, 1024, Q_RANK, tm=tm_all, tn=1024, tk=Q_RANK, name="w_uq_fwd")

    tabs = _rope_tables(T, TT, inverse=False)
    tabs_inv = _rope_tables(T, TT, inverse=True)
    o_pad, lse = _attn_fwd(q_raw, kv, pp, tabs, T, TT)
    W = dict(W, **late_weights(o_pad))
    tm_lat = _pick(T, (1024, 512, 256))
    ya = _mm(o_pad, W["w_attn_out"], "nn", T, D_MODEL, 1024, tm=tm_lat, tn=D_MODEL, tk=1024, name="w_attn_out_fwd")

    tc = 256
    colT = lambda blk0: pl.BlockSpec((T, tc), lambda j: (0, blk0 + j))

    def f_conv(ids, xin, cb, cc, w, b):
        return (cb * _conv(cc * xin, w, b),)

    (e,) = _ew(f_conv, (CONV_DIM // tc,),
               [(pp, colT(CX0 // tc)), (pp, colT(CB0 // tc)), (pp, colT(CC0 // tc)),
                (W["conv_w"], pl.BlockSpec((3, tc), lambda j: (0, j))), (W["conv_b"], pl.BlockSpec((1, tc), lambda j: (0, j)))],
               [((T, CONV_DIM), BF16, colT(0), None)], "conv_fwd")
    yc = _mm(e, W["w_conv_out"], "nn", T, D_MODEL, CONV_DIM, tm=tm_lat, tn=256, tk=CONV_DIM, name="w_conv_out_fwd",
             b_spec=pl.BlockSpec((None, CONV_DIM, 256), lambda i, j, k: (j, k, 0)))

    def f_merge(ids, ga, gc, a, c):
        return (_sigmoid(ga) * a + _sigmoid(gc) * c,)

    (mrg,) = _ew(f_merge, (n_lat,), [(pp, _rows(D_MODEL, 0)), (pp, _rows(D_MODEL, 1)), (ya, _rows(D_MODEL)),
                                    (yc, _rows(D_MODEL))], [row_out(D_MODEL, BF16)], "merge_fwd")
    mo = _mm(mrg, W["w_o"], "nn", T, D_MODEL, D_MODEL, tm=tm_lat, tn=D_MODEL, tk=D_MODEL, name="w_o_fwd")

    def f_norm2(ids, x, m, gate, g, sh, sc):
        x1 = x + gate * m
        return x1, (x1 * _rms(x1) * g) * (1.0 + sc) + sh

    x1, h2 = _ew(f_norm2, (n_lat,), [(xx, _rows(D_MODEL)), (mo, _rows(D_MODEL)), (g1, vec(D_MODEL)),
                                    (W["norm2_g"], vec(D_MODEL)), (sh2, vec(D_MODEL)), (sc2, vec(D_MODEL))],
                 [row_out(D_MODEL, F32), row_out(D_MODEL, BF16)], "norm2_fwd")
    up = _mm(h2, W["w_up"], "nn", T, 2 * D_FF, D_MODEL, tm=tm_lat, tn=1408, tk=D_MODEL, name="w_up_fwd",
             b_spec=pl.BlockSpec((None, D_MODEL, 1408), lambda i, j, k: (j, k, 0)))

    n_ff = D_FF // tc
    ffw = lambda off, n=3: pl.BlockSpec((n, tc), lambda j: (0, j + off))

    def f_ffn(ids, ug, uv, wg, wv, bg, bv):
        gate, val = _conv(ug, wg, bg), _conv(uv, wv, bv)
        return (gate * _sigmoid(gate) * val,)

    (act,) = _ew(f_ffn, (n_ff,), [(up, colT(0)), (up, colT(n_ff)), (W["ffn_conv_w"], ffw(0)), (W["ffn_conv_w"], ffw(n_ff)),
                                 (W["ffn_conv_b"], ffw(0, 1)), (W["ffn_conv_b"], ffw(n_ff, 1))],
                 [((T, D_FF), BF16, colT(0), None)], "ffn_act_fwd")
    f = _mm(act, W["w_down"], "nn", T, D_MODEL, D_FF, tm=tm_lat, tn=D_MODEL, tk=D_FF, name="w_down_fwd")

    def f_head(ids, x1_, f_, gate, gf, t):
        x2 = x1_ + gate * f_
        r = _rms(x2)
        xn = x2 * r
        err = xn * gf - t
        loss = 0.5 * jnp.sum(jnp.mean(err * err, axis=-1, keepdims=True))
        dy = err * (1.0 / D_MODEL)
        dx2 = _rms_bwd(dy * gf, xn, r)
        return dx2, dx2 * gate, _colsum(dy * xn), _colsum(dx2 * f_), jnp.full((1, 128), loss, F32)

    dx2, df, dg_f, dg2, loss = _ew(
        f_head, (n_lat,), [(x1, _rows(D_MODEL)), (f, _rows(D_MODEL)), (g2, vec(D_MODEL)), (W["final_g"], vec(D_MODEL)),
                           (tgt, _rows(D_MODEL))],
        [row_out(D_MODEL, F32), row_out(D_MODEL, BF16), acc_out(D_MODEL), acc_out(D_MODEL), acc_out(128)], "loss_head")

    d_w_down = _mm(act, df, "tn", D_FF, D_MODEL, T, tm=1408, tn=D_MODEL, tk=T, name="w_down_dw",
                   out_dtype=BF16).reshape(4, D_FF // 4, D_MODEL)
    da = _mm(df, W["w_down"], "nt", T, D_FF, D_MODEL, tm=tm_lat, tn=1408, tk=D_MODEL, name="w_down_dx")

    tcb = 128
    n_fb = D_FF // tcb
    colb = lambda blk0: pl.BlockSpec((T, tcb), lambda j: (0, blk0 + j))
    ffwb = lambda off, n=3: pl.BlockSpec((n, tcb), lambda j: (0, j + off))
    cvec = ((1, D_FF), F32, pl.BlockSpec((1, tcb), lambda j: (0, j)), None)

    def f_ffn_bwd(ids, ug, uv, d_act, wg, wv, bg, bv):
        sg, sv = _shifts(ug), _shifts(uv)
        gate, val = _conv(ug, wg, bg, sg), _conv(uv, wv, bv, sv)
        s = _sigmoid(gate)
        d_gate = d_act * val * s * (1.0 + gate * (1.0 - s))
        d_val = d_act * gate * s
        wg0, wg1, wg2 = _conv_bwd_w(d_gate, ug, sg)
        wv0, wv1, wv2 = _conv_bwd_w(d_val, uv, sv)
        d_up = [_conv_bwd_x(d_gate, wg), _conv_bwd_x(d_val, wv)]
        return d_up, [_colsum(d_gate), _colsum(d_val), wg0, wg1, wg2, wv0, wv1, wv2]

    d_up3, ffn_stats = _ew(
        f_ffn_bwd, (n_fb,),
        [(up, colb(0)), (up, colb(n_fb)), (da, colb(0)), (W["ffn_conv_w"], ffwb(0)), (W["ffn_conv_w"], ffwb(n_fb)),
         (W["ffn_conv_b"], ffwb(0, 1)), (W["ffn_conv_b"], ffwb(n_fb, 1))],
        [((2, T, D_FF), BF16, pl.BlockSpec((2, T, tcb), lambda j: (0, 0, j)), None),
         ((n_fb, 8, 1, tcb), F32, pl.BlockSpec((None, 8, 1, tcb), lambda j: (j, 0, 0, 0)), None)], "ffn_act_bwd")
    stat = lambda s: ffn_stats[:, s, 0, :].reshape(1, D_FF)
    d_ffn_conv_b = jnp.concatenate([stat(0), stat(1)], axis=1)
    d_ffn_conv_w = jnp.concatenate([jnp.concatenate([stat(2), stat(3), stat(4)], axis=0),
                                    jnp.concatenate([stat(5), stat(6), stat(7)], axis=0)], axis=1)

    tk_t = T
    d_w_up = _mm(h2, d_up3, "tn", D_MODEL, 2 * D_FF, T, tm=D_MODEL, tn=1408, tk=tk_t, name="w_up_dw", out_dtype=BF16,
                 b_spec=pl.BlockSpec((None, tk_t, 1408), lambda i, j, k: (j // 2, k, j % 2)),
                 o_spec=pl.BlockSpec((None, D_MODEL, 1408), lambda i, j, k: (j, i, 0)), out_shape=(4, D_MODEL, 1408))
    dh2 = _mm(d_up3, W["w_up"], "nt", T, D_MODEL, 2 * D_FF, tm=tm_lat, tn=D_MODEL, tk=1408, name="w_up_dx",
              a_spec=pl.BlockSpec((None, tm_lat, 1408), lambda i, j, k: (k // 2, i, k % 2)),
              b_spec=pl.BlockSpec((None, D_MODEL, 1408), lambda i, j, k: (k, j, 0)))

    def f_norm2_bwd(ids, dx2_, dh, x1_, m, g, sc, gate):
        r = _rms(x1_)
        xn = x1_ * r
        dx1 = dx2_ + _rms_bwd(dh * g * (1.0 + sc), xn, r)
        return dx1, dx1 * gate, _colsum(dh), _colsum(dh * xn * g), _colsum(dh * xn * (1.0 + sc)), _colsum(dx1 * m)

    dx1, dmo, dsh2, dsc2, dg_n2, dg1 = _ew(
        f_norm2_bwd, (n_lat,), [(dx2, _rows(D_MODEL)), (dh2, _rows(D_MODEL)), (x1, _rows(D_MODEL)), (mo, _rows(D_MODEL)),
                                (W["norm2_g"], vec(D_MODEL)), (sc2, vec(D_MODEL)), (g1, vec(D_MODEL))],
        [row_out(D_MODEL, F32), row_out(D_MODEL, BF16)] + [acc_out(D_MODEL)] * 4, "norm2_bwd")
    d_w_o = _mm(mrg, dmo, "tn", D_MODEL, D_MODEL, T, tm=D_MODEL, tn=D_MODEL, tk=tk_t, name="w_o_dw",
                out_dtype=BF16).reshape(4, D_MODEL // 4, D_MODEL)
    dmrg = _mm(dmo, W["w_o"], "nt", T, D_MODEL, D_MODEL, tm=tm_lat, tn=D_MODEL, tk=D_MODEL, name="w_o_dx")
    dmrg = early_grads("late", {"w_o": d_w_o, "w_up": d_w_up, "w_down": d_w_down}, dmrg)

    def f_merge_bwd(ids, dm, ga, gc, a, c):
        sa, sc_ = _sigmoid(ga), _sigmoid(gc)
        return dm * sa, dm * sc_, dm * a * sa * (1.0 - sa), dm * c * sc_ * (1.0 - sc_)

    dya, dyc, dp_ga, dp_gc = _ew(
        f_merge_bwd, (n_lat,), [(dmrg, _rows(D_MODEL)), (pp, _rows(D_MODEL, 0)), (pp, _rows(D_MODEL, 1)),
                                (ya, _rows(D_MODEL)), (yc, _rows(D_MODEL))], [row_out(D_MODEL, BF16)] * 4, "merge_bwd")

    d_w_ao_p = _mm(o_pad, dya, "tn", 1024, D_MODEL, T, tm=1024, tn=D_MODEL, tk=tk_t, name="w_attn_out_dw", out_dtype=BF16)
    do_pad = _mm(dya, W["w_attn_out"], "nt", T, 1024, D_MODEL, tm=tm_lat, tn=1024, tk=D_MODEL, name="w_attn_out_dx")
    d_w_co = _mm(e, dyc, "tn", CONV_DIM, D_MODEL, T, tm=CONV_DIM, tn=256, tk=tk_t, name="w_conv_out_dw", out_dtype=BF16,
                 o_spec=pl.BlockSpec((None, CONV_DIM, 256), lambda i, j, k: (j, i, 0)), out_shape=(4, CONV_DIM, 256))
    de = _mm(dyc, W["w_conv_out"], "nt", T, CONV_DIM, D_MODEL, tm=tm_lat, tn=CONV_DIM, tk=256, name="w_conv_out_dx",
             b_spec=pl.BlockSpec((None, CONV_DIM, 256), lambda i, j, k: (k, j, 0)))

    def f_conv_bwd(ids, xin, cb, cc, d_e, w, b):
        z = cc * xin
        sz = _shifts(z)
        cz = _conv(z, w, b, sz)
        dcz = d_e * cb
        w0, w1, w2 = _conv_bwd_w(dcz, z, sz)
        dz = _conv_bwd_x(dcz, w)
        return dz * cc, d_e * cz, dz * xin, _colsum(dcz), w0, w1, w2

    cvec_c = ((1, CONV_DIM), F32, pl.BlockSpec((1, tc), lambda j: (0, j)), None)
    conv_b = _ew(f_conv_bwd, (CONV_DIM // tc,),
                 [(pp, colT(CX0 // tc)), (pp, colT(CB0 // tc)), (pp, colT(CC0 // tc)), (de, colT(0)),
                  (W["conv_w"], pl.BlockSpec((3, tc), lambda j: (0, j))), (W["conv_b"], pl.BlockSpec((1, tc), lambda j: (0, j)))],
                 [((T, CONV_DIM), BF16, colT(0), None)] * 3 + [cvec_c] * 4, "conv_bwd")
    dp_cx, dp_cb, dp_cc, d_conv_b = conv_b[:4]
    d_conv_w = jnp.concatenate(conv_b[4:7], axis=0)

    dq_raw, dkv, dp_kr = _attn_bwd(q_raw, kv, pp, o_pad, do_pad, lse, tabs, tabs_inv, T, TT)

    tk_a = TT
    d_w_uq_t = _mm(nq, dq_raw, "tn", Q_RANK, 1024, TT, tm=Q_RANK, tn=1024, tk=tk_a, name="w_uq_dw", transpose_out=True)
    dnq = _mm(dq_raw, W["w_uq_t"], "nn", TT, Q_RANK, 1024, tm=tm_all, tn=Q_RANK, tk=1024, name="w_uq_dx")
    d_w_ukv = _mm(nkv, dkv, "tn", KV_RANK, 1024, TT, tm=KV_RANK, tn=256, tk=tk_a, name="w_ukv_dw", out_dtype=BF16,
                  o_spec=pl.BlockSpec((None, KV_RANK, 256), lambda i, j, k: (j, i, 0)), out_shape=(4, KV_RANK, 256))
    dnkv = _mm(dkv, W["w_ukv"], "nt", TT, KV_RANK, 1024, tm=tm_all, tn=KV_RANK, tk=256, name="w_ukv_dx",
               b_spec=pl.BlockSpec((None, KV_RANK, 256), lambda i, j, k: (k, j, 0)))
    dnkv = early_grads("mid", {
        "w_attn_out": jnp.transpose(d_w_ao_p.reshape(N_HEADS, HEAD_PAD, 4, 256)[:, 64:], (2, 0, 1, 3)).reshape(
            4, N_HEADS * 64, 256),
        "w_conv_out": d_w_co,
        "w_uq": d_w_uq_t.reshape(4, 2, HEAD_PAD, Q_RANK)[:, :, :QK_DIM].reshape(4, 2 * QK_DIM, Q_RANK).astype(BF16),
        "w_ukv": d_w_ukv}, dnkv)

    def f_lowrank_bwd(ids, ckv, cq, dkv_, dq_, gkv, gq, ga, gc, cx, cb, cc, kr):
        rk, rq = _rms(ckv), _rms(cq)
        nk, nq_ = ckv * rk, cq * rq
        lat = ids[0] < n_lat
        pieces = [jnp.where(lat, a, jnp.zeros_like(a)) for a in (ga, gc, cx, cb, cc)]
        pieces += [_rms_bwd(dkv_ * gkv, nk, rk).astype(BF16), _rms_bwd(dq_ * gq, nq_, rq).astype(BF16), kr.astype(BF16)]
        return jnp.concatenate(pieces, axis=1), _colsum(dkv_ * nk), _colsum(dq_ * nq_)

    lat_rows = lambda n: pl.BlockSpec((ROW_TILE, n), lambda i: (jnp.minimum(i, n_lat - 1), 0))
    dpp, dg_kv, dg_q = _ew(
        f_lowrank_bwd, (n_all,), [(pp, _rows(KV_RANK, KV0 // KV_RANK)), (pp, _rows(Q_RANK, Q0 // Q_RANK)),
                                  (dnkv, _rows(KV_RANK)), (dnq, _rows(Q_RANK)), (W["kv_norm_g"], vec(KV_RANK)),
                                  (W["q_norm_g"], vec(Q_RANK)), (dp_ga, lat_rows(D_MODEL)), (dp_gc, lat_rows(D_MODEL)),
                                  (dp_cx, lat_rows(CONV_DIM)), (dp_cb, lat_rows(CONV_DIM)), (dp_cc, lat_rows(CONV_DIM)),
                                  (dp_kr, _rows(HEAD_PAD))],
        [row_out(P_COLS, BF16, TT), acc_out(KV_RANK), acc_out(Q_RANK)], "lowrank_norm_bwd")
    d_w_in_t = _mm(hh, dpp, "tn", D_MODEL, P_COLS, TT, tm=512, tn=2176, tk=TT, name="w_in_dw", out_dtype=BF16,
                   transpose_out=True)
    dhh = _mm(dpp, W["w_in_t"], "nn", TT, D_MODEL, P_COLS, tm=tm_all, tn=512, tk=2176, name="w_in_dx")

    def f_norm1_bwd(ids, x, dh, dres, g, sc):
        r = _rms(x)
        xn = x * r
        return (dres + _rms_bwd(dh * g * (1.0 + sc), xn, r), _colsum(dh), _colsum(dh * xn * g),
                _colsum(dh * xn * (1.0 + sc)))

    grad_x, dsh1, dsc1, dg_n1 = _ew(
        f_norm1_bwd, (n_lat,), [(xx, _rows(D_MODEL)), (dhh, _rows(D_MODEL)), (dx1, _rows(D_MODEL)),
                                (W["norm1_g"], vec(D_MODEL)), (sc1, vec(D_MODEL))],
        [row_out(D_MODEL, F32)] + [acc_out(D_MODEL)] * 3, "norm1_bwd")

    def f_norm1_ctx_bwd(ids, x, dh, g, sc):
        xn = x * _rms(x)
        return _colsum(dh), _colsum(dh * xn * g), _colsum(dh * xn * (1.0 + sc))

    n_ctx = n_all - n_lat
    dcsh1, dcsc1, dg_n1c = _ew(
        f_norm1_ctx_bwd, (n_ctx,), [(xx, _rows(D_MODEL, 0, n_lat)), (dhh, _rows(D_MODEL, 0, n_lat)),
                                    (W["norm1_g"], vec(D_MODEL)), (csc1, vec(D_MODEL))], [acc_out(D_MODEL)] * 3,
        "norm1_ctx_bwd")

    big = {"w_in": _w_in_t_shards_from_p(d_w_in_t).astype(BF16)}
    zero = jnp.zeros((1, 4 * D_MODEL), F32)
    small = {
        "dmod_lat": jnp.concatenate([dsh1, dsc1, dg1, dsh2, dsc2, dg2], axis=1),
        "dmod_ctx": jnp.concatenate([dcsh1, dcsc1, zero], axis=1),
        "norm1_g": dg_n1 + dg_n1c, "norm2_g": dg_n2, "final_g": dg_f, "q_norm_g": dg_q, "kv_norm_g": dg_kv,
        "conv_b": d_conv_b, "conv_w": d_conv_w.reshape(1, -1), "ffn_conv_b": d_ffn_conv_b,
        "ffn_conv_w": d_ffn_conv_w.reshape(1, -1),
    }
    return grad_x, loss, big, small


SMALL = (("dmod_lat", 6144), ("dmod_ctx", 6144), ("norm1_g", 1024), ("norm2_g", 1024), ("final_g", 1024),
         ("q_norm_g", 384), ("kv_norm_g", 256), ("conv_b", 512), ("conv_w", 1536), ("ffn_conv_b", 5632),
         ("ffn_conv_w", 16896), ("loss", 128))
SMALL_ROWS = 320


def _adam_update(w, g, m, v):
    c1, c2 = 1.0 - ADAM_B1 ** ADAM_STEP, 1.0 - ADAM_B2 ** ADAM_STEP
    m2 = ADAM_B1 * m + (1.0 - ADAM_B1) * g
    v2 = ADAM_B2 * v + (1.0 - ADAM_B2) * (g * g)
    return [-ADAM_LR * ((m2 / c1) / (jnp.sqrt(v2 / c2) + ADAM_EPS) + ADAM_WD * w), m2, v2]


def _adamw(w, g, m, v, name):
    R, C = w.shape
    tr = 8 if R % 8 == 0 else R
    for t in range(8, R + 1, 8):
        if R % t == 0 and t * C * 4 <= (1 << 20):
            tr = t
    spec = pl.BlockSpec((tr, C), lambda i: (i, 0))
    return _ew(lambda ids, *vals: _adam_update(*vals), (R // tr,), [(w, spec), (g, spec), (m, spec), (v, spec)],
               [((R, C), F32, spec, None)] * 3, name)


def kernel(x, c, ctx, c_ctx, w_ada, b_ada, norm1_g, w_in, q_norm_g, kv_norm_g, w_uq, w_ukv, conv_w, conv_b, w_attn_out, w_conv_out, w_o, norm2_g, w_up, ffn_conv_w, ffn_conv_b, w_down, final_g, loss_target, m_c_ctx, m_w_ada, m_b_ada, m_norm1_g, m_w_in, m_q_norm_g, m_kv_norm_g, m_w_uq, m_w_ukv, m_conv_w, m_conv_b, m_w_attn_out, m_w_conv_out, m_w_o, m_norm2_g, m_w_up, m_ffn_conv_w, m_ffn_conv_b, m_w_down, m_final_g, v_c_ctx, v_w_ada, v_b_ada, v_norm1_g, v_w_in, v_q_norm_g, v_kv_norm_g, v_w_uq, v_w_ukv, v_conv_w, v_conv_b, v_w_attn_out, v_w_conv_out, v_w_o, v_norm2_g, v_w_up, v_ffn_conv_w, v_ffn_conv_b, v_w_down, v_final_g):
    mx, my, mc = lax.axis_index("x"), lax.axis_index("y"), lax.axis_index("c")
    chip = 2 * mx + my
    dev = 4 * mx + 2 * my + mc
    T, Tc = x.shape[1], ctx.shape[1]
    TT = T + Tc
    w_in_t, m_w_in_t, v_w_in_t = (jnp.transpose(a[0]) for a in (w_in, m_w_in, v_w_in))
    w_uq_t, m_w_uq_t, v_w_uq_t = (jnp.transpose(a[0]) for a in (w_uq, m_w_uq, v_w_uq))
    shards = {"w_in": jnp.pad(w_in_t, ((0, W_IN_SHARD_PAD - W_IN_SHARD), (0, 0))), "w_uq": w_uq_t, "w_ukv": w_ukv[0],
              "w_attn_out": w_attn_out[0], "w_conv_out": w_conv_out[0], "w_o": w_o[0], "w_up": w_up[0],
              "w_down": w_down[0]}

    conv_sh = jnp.concatenate([conv_w[0], ffn_conv_w[0]], axis=1)
    pay1 = jnp.concatenate([jnp.pad(c, ((0, 7), (0, 0))), jnp.pad(conv_sh, ((0, 5), (0, 0)))], axis=1)
    got1 = _allgather8(pay1, "gather_cond", in_vmem=True)[0].reshape(8, 8, 2560)
    c_all = got1[:, 0, :D_MODEL]
    conv_all = got1[0::2, :3, D_MODEL:]
    conv_w_full = _cols_from_shards(conv_all[:, :, :128])
    ffn_conv_w_full = _cols_from_shards(conv_all[:, :, 128:])

    cond = jnp.concatenate([c_all, c_ctx.reshape(1, D_MODEL), jnp.zeros((7, D_MODEL), F32)], axis=0)

    def f_silu(ids, v):
        return (v * _sigmoid(v),)

    (s16,) = _ew(f_silu, (1,), [(cond, _full((16, D_MODEL)))], [((16, D_MODEL), F32, _full((16, D_MODEL)), None)], "silu_cond")
    mod_sh = _mm(s16, w_ada[0], "nn", 16, 1536, D_MODEL, tm=16, tn=768, tk=D_MODEL, name="w_ada_fwd")
    m_send, m_recv, m_src, m_land, ukv_thru = _ici_start("all", [mod_sh], [(8, 16, 1536)], w_ukv[0], "mod_start")
    shards["w_ukv"] = ukv_thru

    names = [n for n, _ in BIG]
    first = [n for n in names if n not in GATHER_LATE]
    gathered, zero = _gather_weights([shards[n].astype(BF16) for n in first])
    full = dict(zip(first, gathered))
    (mod_mine,), (m_land,) = _ici_wait("all", m_send, m_recv, m_src, m_land, gathered[0], "mod_wait")
    got2 = lax.dynamic_update_slice(m_land, mod_mine[None], (dev, 0, 0))
    mod_all = _cols_from_shards(got2[0::2]) + b_ada
    mod_lat = lax.dynamic_slice_in_dim(mod_all, dev, 1, axis=0)
    mod_ctx = mod_all[8:9]
    xx = jnp.concatenate([x[0], ctx[0]], axis=0)
    late_bf = [(shards[n] + zero[0, 0]).astype(BF16) for n in GATHER_LATE]
    g_send, g_recv, late_src, late_land, xx = _ici_start(
        "gather", late_bf, [(4,) + s.shape for s in late_bf], xx, "gather_late_start")

    def late_weights(after):
        src, land = _ici_wait("gather", g_send, g_recv, late_src, late_land, after, "gather_late_wait")
        got = dict(zip(GATHER_LATE, _gather_finish(src, land)))
        wao = _cols_from_shards(got["w_attn_out"]).reshape(N_HEADS, 64, D_MODEL)
        return {"w_attn_out": jnp.pad(wao, ((0, 0), (64, 0), (0, 0))).reshape(N_HEADS * HEAD_PAD, D_MODEL),
                "w_conv_out": got["w_conv_out"], "w_o": got["w_o"].reshape(D_MODEL, D_MODEL), "w_up": got["w_up"],
                "w_down": got["w_down"].reshape(D_FF, D_MODEL)}

    wuq_t = full["w_uq"].reshape(N_HEADS, QK_DIM, Q_RANK)
    W = {
        "w_in_t": _w_in_t_p_from_shards(full["w_in"]),
        "w_uq_t": jnp.pad(wuq_t, ((0, 0), (0, HEAD_PAD - QK_DIM), (0, 0))).reshape(N_HEADS * HEAD_PAD, Q_RANK),
        "w_ukv": full["w_ukv"],
        "norm1_g": norm1_g, "norm2_g": norm2_g, "final_g": final_g.reshape(1, D_MODEL), "q_norm_g": q_norm_g,
        "kv_norm_g": kv_norm_g, "conv_w": conv_w_full, "conv_b": conv_b, "ffn_conv_w": ffn_conv_w_full,
        "ffn_conv_b": ffn_conv_b,
    }

    place = jnp.stack([chip, mc]).astype(jnp.int32)
    early = {}

    def early_grads(tag, g, carry):
        gs = list(g.values())
        from_sib = _rs_pair(gs, "rs_pair_" + tag)
        sums = [_add_pair(gs[w], from_sib[w], place, "rs_pair_add_" + n) for w, n in enumerate(g)]
        send, recv, sums, land, carry = _ici_start(
            "scatter", sums, [(3,) + s.shape[1:] for s in sums], carry, "rs_chips_" + tag + "_start")
        early[tag] = (list(g), send, recv, sums, land)
        return carry

    grad_x, loss_part, gbig, gsmall = _local_step(xx, loss_target[0], mod_lat, mod_ctx, W, late_weights, early_grads)

    gsmall["loss"] = loss_part
    pay3 = jnp.concatenate([gsmall[n].reshape(-1) for n, _ in SMALL])
    pay3 = jnp.pad(pay3, (0, SMALL_ROWS * 128 - pay3.shape[0])).reshape(SMALL_ROWS, 128)
    s_send, s_recv, s_src, s_land, w_in_thru = _ici_start("all", [pay3], [(8, SMALL_ROWS, 128)], gbig["w_in"],
                                                         "small_start")
    gbig = {"w_in": w_in_thru}

    after_small = early_grads("last", gbig, s_src[0])

    (pay3,), (s_land,) = _ici_wait("all", s_send, s_recv, [after_small], s_land, early["last"][3][0], "small_wait")
    got3 = lax.dynamic_update_slice(s_land, pay3[None], (dev, 0, 0)).reshape(8 * SMALL_ROWS, 128)

    def f_sum8(ids, a):
        s = a[0:SMALL_ROWS]
        for d in range(1, 8):
            s = s + a[d * SMALL_ROWS:(d + 1) * SMALL_ROWS]
        return (s,)

    (vsum,) = _ew(f_sum8, (1,), [(got3, _full((8 * SMALL_ROWS, 128)))],
                  [((SMALL_ROWS, 128), F32, _full((SMALL_ROWS, 128)), None)], "sum_small")
    vflat = vsum.reshape(-1)
    gvec, off = {}, 0
    for n, size in SMALL:
        gvec[n] = vflat[off:off + size]
        off += size
    loss = gvec["loss"][0]
    dmod_rows = got3.reshape(8, SMALL_ROWS * 128)[:, :6 * D_MODEL]
    dm16 = jnp.concatenate([dmod_rows, gvec["dmod_ctx"].reshape(1, -1), jnp.zeros((7, 6 * D_MODEL), F32)], axis=0)

    def f_colsum(ids, a):
        return (_colsum(a),)

    (g_b_ada,) = _ew(f_colsum, (1,), [(dm16, _full((16, 6 * D_MODEL)))],
                     [((1, 6 * D_MODEL), F32, _full((1, 6 * D_MODEL)), None)], "b_ada_grad")
    dm_sh = lax.dynamic_slice_in_dim(dm16, chip * 1536, 1536, axis=1)
    g_w_ada = _mm(s16, dm_sh, "tn", D_MODEL, 1536, 16, tm=512, tn=768, tk=16, name="w_ada_dw")
    dcond_part = _mm(dm_sh, w_ada[0], "nt", 16, D_MODEL, 1536, tm=16, tn=512, tk=1536, name="w_ada_dx")
    d_send, d_recv, d_src, d_land, vsum = _ici_start("all", [dcond_part[8:16]], [(8, 8, D_MODEL)], vsum, "dcond_start")

    def finish(tags, after):
        done, own, lands = [], [], []
        for tag in tags:
            tag_names, send, recv, sums, land = early[tag]
            sums, land = _ici_wait("scatter", send, recv, sums, land, after, "rs_chips_" + tag + "_wait")
            done, own, lands = done + tag_names, own + sums, lands + land
        halves = [_add_chips(a, b, place, "rs_chip_add_" + n) for a, b, n in zip(own, lands, done)]
        return dict(zip(done, _rs_pair_back(halves, "rs_pair_back_" + tags[0])))

    grads, deltas, new_m, new_v = {}, {}, {}, {}

    def adam(n, w_, m_, v_, g, transposed):
        d_, m2, v2 = _adamw(w_, g, m_, v_, "adamw_" + n)
        back = (lambda a: jnp.transpose(a)[None]) if transposed else (lambda a: a[None])
        grads[n], deltas[n], new_m[n], new_v[n] = back(g[:w_.shape[0]]), back(d_), back(m2), back(v2)

    gw = finish(["late", "mid"], grad_x)
    adam("w_ada", w_ada[0], m_w_ada[0], v_w_ada[0], g_w_ada, False)
    for n, (w_, m_, v_) in {"w_ukv": (w_ukv, m_w_ukv, v_w_ukv), "w_attn_out": (w_attn_out, m_w_attn_out, v_w_attn_out),
                            "w_conv_out": (w_conv_out, m_w_conv_out, v_w_conv_out), "w_o": (w_o, m_w_o, v_w_o),
                            "w_up": (w_up, m_w_up, v_w_up), "w_down": (w_down, m_w_down, v_w_down)}.items():
        adam(n, w_[0], m_[0], v_[0], gw[n], False)
    adam("w_uq", w_uq_t, m_w_uq_t, v_w_uq_t, gw["w_uq"], True)
    gw_in = finish(["last"], deltas["w_up"])
    adam("w_in", w_in_t, m_w_in_t, v_w_in_t, gw_in["w_in"], True)

    (dcond_mine,), (d_land,) = _ici_wait("all", d_send, d_recv, d_src, d_land, deltas["w_in"], "dcond_wait")
    got4 = lax.dynamic_update_slice(d_land, dcond_mine[None], (dev, 0, 0))[0::2, 0]

    def f_c_ctx(ids, parts, cc):
        s = _sigmoid(cc)
        d = parts[0:1] + parts[1:2] + parts[2:3] + parts[3:4]
        return (d * s * (1.0 + cc * (1.0 - s)),)

    (g_c_ctx,) = _ew(f_c_ctx, (1,), [(got4, _full((4, D_MODEL))), (c_ctx.reshape(1, D_MODEL), _full((1, D_MODEL)))],
                     [((1, D_MODEL), F32, _full((1, D_MODEL)), None)], "c_ctx_grad")

    conv_w_g = lax.dynamic_slice_in_dim(gvec["conv_w"].reshape(3, CONV_DIM), chip * 128, 128, axis=1)
    ffn_conv_w_g = lax.dynamic_slice_in_dim(gvec["ffn_conv_w"].reshape(3, 2 * D_FF), chip * 1408, 1408, axis=1)
    vec_params = (("c_ctx", c_ctx, m_c_ctx, v_c_ctx, g_c_ctx), ("b_ada", b_ada, m_b_ada, v_b_ada, g_b_ada),
                  ("norm1_g", norm1_g, m_norm1_g, v_norm1_g, gvec["norm1_g"]),
                  ("q_norm_g", q_norm_g, m_q_norm_g, v_q_norm_g, gvec["q_norm_g"]),
                  ("kv_norm_g", kv_norm_g, m_kv_norm_g, v_kv_norm_g, gvec["kv_norm_g"]),
                  ("conv_w", conv_w, m_conv_w, v_conv_w, conv_w_g), ("conv_b", conv_b, m_conv_b, v_conv_b, gvec["conv_b"]),
                  ("norm2_g", norm2_g, m_norm2_g, v_norm2_g, gvec["norm2_g"]),
                  ("ffn_conv_w", ffn_conv_w, m_ffn_conv_w, v_ffn_conv_w, ffn_conv_w_g),
                  ("ffn_conv_b", ffn_conv_b, m_ffn_conv_b, v_ffn_conv_b, gvec["ffn_conv_b"]),
                  ("final_g", final_g, m_final_g, v_final_g, gvec["final_g"]))
    two_d = lambda a: a.reshape((-1, a.shape[-1]))

    def f_adam_many(ids, *vals):
        out = []
        for k in range(len(vec_params)):
            out += _adam_update(*vals[4 * k:4 * k + 4])
        return out

    ins_v, outs_v = [], []
    for p in vec_params:
        shp = two_d(p[1]).shape
        ins_v += [(two_d(a), _full(shp)) for a in (p[1], p[4], p[2], p[3])]
        outs_v += [(shp, F32, _full(shp), None)] * 3
    res_v = _ew(f_adam_many, (1,), ins_v, outs_v, "adamw_vectors")
    for k, p in enumerate(vec_params):
        n, shape = p[0], p[1].shape
        grads[n] = p[4].reshape(shape)
        deltas[n], new_m[n], new_v[n] = (r.reshape(shape) for r in res_v[3 * k:3 * k + 3])

    order = ("c_ctx", "w_ada", "b_ada", "norm1_g", "w_in", "q_norm_g", "kv_norm_g", "w_uq", "w_ukv", "conv_w", "conv_b",
             "w_attn_out", "w_conv_out", "w_o", "norm2_g", "w_up", "ffn_conv_w", "ffn_conv_b", "w_down", "final_g")
    return (loss, grad_x[None], *[grads[n] for n in order], *[deltas[n] for n in order],
            *[new_m[n] for n in order], *[new_v[n] for n in order])
```

```python
import functools

import jax
import jax.numpy as jnp
from jax import lax
from jax.experimental import pallas as pl
from jax.experimental.pallas import tpu as pltpu

F32, BF16 = jnp.float32, jnp.bfloat16
MESH = pl.DeviceIdType.MESH

D_MODEL = 1024
N_HEADS = 8
HEAD_PAD = 128
QK_DIM = 96
Q_RANK, KV_RANK = 384, 256
CONV_DIM = 512
D_FF = 2816
GRID_W = 64
ROPE_THETA = 10000.0
EPS = 1e-6
GA0, GC0, CX0, CB0, CC0, KV0, Q0, KR0, P_COLS = 0, 1024, 2048, 2560, 3072, 3584, 3840, 4224, 4352
ROW_TILE = 256
VMEM_LIMIT_BYTES = 48 * 1024 * 1024

ADAM_LR, ADAM_B1, ADAM_B2, ADAM_EPS, ADAM_WD, ADAM_STEP = 0.001, 0.9, 0.999, 1e-08, 0.01, 10

BIG = (("w_in", (1088, 1024)), ("w_uq", (192, 384)), ("w_ukv", (256, 256)), ("w_attn_out", (512, 256)),
       ("w_conv_out", (512, 256)), ("w_o", (256, 1024)), ("w_up", (1024, 1408)), ("w_down", (704, 1024)))

GATHER_LATE = ("w_attn_out", "w_conv_out", "w_o", "w_up", "w_down")

NN = (((1,), (0,)), ((), ()))
NT = (((1,), (1,)), ((), ()))
TN = (((0,), (0,)), ((), ()))


def _cp(sem):
    return pltpu.CompilerParams(dimension_semantics=sem, vmem_limit_bytes=VMEM_LIMIT_BYTES)


PIN_BYTES = 1 << 19


def _in_hbm(arrays):
    return [pltpu.with_memory_space_constraint(a, pltpu.HBM) if a.size * a.dtype.itemsize >= PIN_BYTES else a
            for a in arrays]


def _out(shape, dtype):
    n = 1
    for d in shape:
        n *= d
    big = n * jnp.dtype(dtype).itemsize >= PIN_BYTES
    return pltpu.HBM(shape, dtype) if big else jax.ShapeDtypeStruct(shape, dtype)


def _pick(n, prefs):
    for p in prefs:
        if n % p == 0:
            return p
    return n


def _mm(a, b, mode, M, N, K, *, tm, tn, tk, name, out_dtype=F32, a_spec=None, b_spec=None, o_spec=None,
        out_shape=None, transpose_out=False):
    assert M % tm == 0 and N % tn == 0 and K % tk == 0, (name, M, N, K, tm, tn, tk)
    nk = K // tk
    dims = {"nn": NN, "nt": NT, "tn": TN}[mode]
    if a_spec is None:
        a_spec = (pl.BlockSpec((tk, tm), lambda i, j, k: (k, i)) if mode == "tn"
                  else pl.BlockSpec((tm, tk), lambda i, j, k: (i, k)))
    if b_spec is None:
        b_spec = (pl.BlockSpec((tn, tk), lambda i, j, k: (j, k)) if mode == "nt"
                  else pl.BlockSpec((tk, tn), lambda i, j, k: (k, j)))
    if o_spec is None:
        o_spec = (pl.BlockSpec((tn, tm), lambda i, j, k: (j, i)) if transpose_out
                  else pl.BlockSpec((tm, tn), lambda i, j, k: (i, j)))
    if out_shape is None:
        out_shape = (N, M) if transpose_out else (M, N)

    def emit(o_ref, val):
        o_ref[...] = (val.T if transpose_out else val).astype(o_ref.dtype)

    def body(a_ref, b_ref, o_ref, *scratch):
        part = lax.dot_general(a_ref[...].astype(BF16), b_ref[...].astype(BF16), dims, preferred_element_type=F32)
        if nk == 1:
            emit(o_ref, part)
            return
        acc_ref, = scratch
        k = pl.program_id(2)

        @pl.when(k == 0)
        def _():
            acc_ref[...] = part

        @pl.when((k > 0) & (k < nk - 1))
        def _():
            acc_ref[...] += part

        @pl.when(k == nk - 1)
        def _():
            emit(o_ref, acc_ref[...] + part)

    return pl.pallas_call(
        body, grid=(M // tm, N // tn, nk), in_specs=[a_spec, b_spec], out_specs=o_spec,
        out_shape=_out(out_shape, out_dtype),
        scratch_shapes=[pltpu.VMEM((tm, tn), F32)] if nk > 1 else [],
        compiler_params=_cp(("parallel", "parallel", "arbitrary")), name=name)(*_in_hbm([a, b]))


def _ew(fn, grid, ins, outs, name, scalars=None):
    n_in = len(ins)
    n_sc = 0 if scalars is None else 1

    def store(ref, val, acc, ids):
        if isinstance(val, (list, tuple)):
            for h, v in enumerate(val):
                ref[h] = v.astype(ref.dtype)
            return
        if acc is None:
            ref[...] = val.astype(ref.dtype)
            return

        @pl.when(ids[acc] == 0)
        def _():
            ref[...] = val.astype(ref.dtype)

        @pl.when(ids[acc] > 0)
        def _():
            ref[...] += val.astype(ref.dtype)

    def body(*refs):
        refs = refs[n_sc:]
        ids = tuple(pl.program_id(a) for a in range(len(grid)))
        vals = fn(ids, *[r[...] for r in refs[:n_in]])
        for ref, val, (_, _, _, acc) in zip(refs[n_in:], vals, outs):
            store(ref, val, acc, ids)

    acc_axes = {o[3] for o in outs if o[3] is not None}
    sem = tuple("arbitrary" if a in acc_axes else "parallel" for a in range(len(grid)))
    in_specs, out_specs = [s for _, s in ins], [o[2] for o in outs]
    out_shape = [_out(o[0], o[1]) for o in outs]
    args = _in_hbm([a for a, _ in ins])
    if scalars is None:
        return pl.pallas_call(body, grid=grid, in_specs=in_specs, out_specs=out_specs, out_shape=out_shape,
                              compiler_params=_cp(sem), name=name)(*args)
    spec = pltpu.PrefetchScalarGridSpec(num_scalar_prefetch=1, grid=grid, in_specs=in_specs, out_specs=out_specs)
    return pl.pallas_call(body, grid_spec=spec, out_shape=out_shape, compiler_params=_cp(sem), name=name)(scalars, *args)


def _rows(width, cblk=0, roff=0, tr=ROW_TILE):
    return pl.BlockSpec((tr, width), lambda i: (i + roff, cblk))


def _full(shape):
    nd = len(shape)
    return pl.BlockSpec(shape, lambda *_: (0,) * nd)


def _sigmoid(x):
    return 1.0 / (1.0 + jnp.exp(-x))


def _rms(x):
    return lax.rsqrt(jnp.mean(x * x, axis=-1, keepdims=True) + EPS)


def _rms_bwd(dn, xn, r):
    return r * (dn - xn * jnp.mean(dn * xn, axis=-1, keepdims=True))


def _colsum(x):
    return jnp.sum(x, axis=0, keepdims=True)


def _shifts(x):
    n = x.shape[0]
    rows = lax.broadcasted_iota(jnp.int32, x.shape, 0)
    return jnp.where(rows == 0, 0.0, pltpu.roll(x, 1, 0)), jnp.where(rows == n - 1, 0.0, pltpu.roll(x, n - 1, 0))


def _conv(x, w, b, shifted=None):
    prev, nxt = _shifts(x) if shifted is None else shifted
    return b + prev * w[0:1] + x * w[1:2] + nxt * w[2:3]


def _conv_bwd_x(dy, w):
    prev, nxt = _shifts(dy)
    return nxt * w[0:1] + dy * w[1:2] + prev * w[2:3]


def _conv_bwd_w(dy, x, shifted):
    prev, nxt = shifted
    return _colsum(dy * prev), _colsum(dy * x), _colsum(dy * nxt)


def _rope(x, cos, sin_lo, sin_hi):
    return x * cos + pltpu.roll(x, HEAD_PAD - 8, 1) * sin_lo + pltpu.roll(x, 8, 1) * sin_hi


ATTN_SCALE = QK_DIM ** -0.5
LOG2_E = 1.4426950408889634


def _head_keys(kv_ref, kr_ref, cos_ref, slo_ref, shi_ref, kc_ref, vp_ref):
    kv = kv_ref[...]
    lane = lax.broadcasted_iota(jnp.int32, kv.shape, 1)
    kc_ref[...] = jnp.where(lane < 64, kv, _rope(kr_ref[...], cos_ref[...], slo_ref[...], shi_ref[...])).astype(BF16)
    vp_ref[...] = jnp.where(lane >= 64, kv, 0.0).astype(BF16)


ATTN_Q_TILE = 512


def _attn_specs(tq, TT):
    q = pl.BlockSpec((tq, HEAD_PAD), lambda h, i: (i, h))
    keys = pl.BlockSpec((TT, HEAD_PAD), lambda h, i: (0, h))
    kr = pl.BlockSpec((TT, HEAD_PAD), lambda h, i: (0, KR0 // HEAD_PAD))
    tab_q = pl.BlockSpec((tq, HEAD_PAD), lambda h, i: (i, 0))
    tab_k = pl.BlockSpec((TT, HEAD_PAD), lambda h, i: (0, 0))
    lse = pl.BlockSpec((None, tq, 1), lambda h, i: (h, i, 0))
    return q, keys, kr, tab_q, tab_k, lse


def _attn_fwd(q_raw, kv, pp, tabs, T, TT):
    tq = _pick(T, (ATTN_Q_TILE, ROW_TILE))
    cos, slo, shi = tabs

    def body(q_ref, kv_ref, kr_ref, cq, lq, hq, ck, lk, hk, o_ref, l_ref, kc, vp):
        @pl.when(pl.program_id(1) == 0)
        def _():
            _head_keys(kv_ref, kr_ref, ck, lk, hk, kc, vp)

        q = _rope(q_ref[...], cq[...], lq[...], hq[...]).astype(BF16)
        s = lax.dot_general(q, kc[...], NT, preferred_element_type=F32)
        m = jnp.max(s, axis=-1, keepdims=True)
        p = jnp.exp2((s - m) * (ATTN_SCALE * LOG2_E))
        l = jnp.sum(p, axis=-1, keepdims=True)
        o = lax.dot_general(p.astype(BF16), vp[...], NN, preferred_element_type=F32)
        o_ref[...] = o / l
        l_ref[...] = m * ATTN_SCALE + jnp.log(l)

    qs, keys, kr, tab_q, tab_k, lse = _attn_specs(tq, TT)
    return pl.pallas_call(
        body, grid=(N_HEADS, T // tq), in_specs=[qs, keys, kr, tab_q, tab_q, tab_q, tab_k, tab_k, tab_k],
        out_specs=[qs, lse],
        out_shape=[jax.ShapeDtypeStruct((T, N_HEADS * HEAD_PAD), F32), jax.ShapeDtypeStruct((N_HEADS, T, 1), F32)],
        scratch_shapes=[pltpu.VMEM((TT, HEAD_PAD), BF16), pltpu.VMEM((TT, HEAD_PAD), BF16)],
        compiler_params=_cp(("parallel", "arbitrary")), name="attn_fwd",
    )(*_in_hbm([q_raw, kv, pp, cos, slo, shi, cos, slo, shi]))


def _attn_bwd(q_raw, kv, pp, o, do, lse, tabs, tabs_inv, T, TT):
    tq = _pick(T, (ATTN_Q_TILE, ROW_TILE))
    nq = T // tq
    cos, slo, shi = tabs
    cos_i, slo_i, shi_i = tabs_inv

    def body(q_ref, kv_ref, kr_ref, cq, lq, hq, ck, lk, hk, iq, ilq, ihq, ik, ilk, ihk, o_ref, do_ref, l_ref,
             dq_ref, dkv_ref, dkr_ref, kc, vp, dk, dv):
        h, i = pl.program_id(0), pl.program_id(1)

        @pl.when(i == 0)
        def _():
            _head_keys(kv_ref, kr_ref, ck, lk, hk, kc, vp)
            dk[...] = jnp.zeros_like(dk)
            dv[...] = jnp.zeros_like(dv)

        q = _rope(q_ref[...], cq[...], lq[...], hq[...]).astype(BF16)
        k, v, d_o = kc[...], vp[...], do_ref[...]
        s = lax.dot_general(q, k, NT, preferred_element_type=F32)
        p = jnp.exp2(s * (ATTN_SCALE * LOG2_E) - l_ref[...] * LOG2_E)
        dob = d_o.astype(BF16)
        dp = lax.dot_general(dob, v, NT, preferred_element_type=F32)
        dd = jnp.sum(d_o * o_ref[...], axis=-1, keepdims=True)
        ds = (p * (dp - dd) * ATTN_SCALE).astype(BF16)
        dq = lax.dot_general(ds, k, NN, preferred_element_type=F32)
        dq_ref[...] = _rope(dq, iq[...], ilq[...], ihq[...]).astype(dq_ref.dtype)
        dk[...] += lax.dot_general(ds, q, TN, preferred_element_type=F32)
        dv[...] += lax.dot_general(p.astype(BF16), dob, TN, preferred_element_type=F32)

        @pl.when(i == nq - 1)
        def _():
            dkh = dk[...]
            lane = lax.broadcasted_iota(jnp.int32, dkh.shape, 1)
            dkv_ref[...] = jnp.where(lane < 64, dkh, dv[...]).astype(dkv_ref.dtype)
            rot = _rope(jnp.where((lane >= 64) & (lane < 96), dkh, 0.0), ik[...], ilk[...], ihk[...])

            @pl.when(h == 0)
            def _():
                dkr_ref[...] = rot

            @pl.when(h > 0)
            def _():
                dkr_ref[...] += rot

    qs, keys, kr, tab_q, tab_k, lse_spec = _attn_specs(tq, TT)
    wide = lambda rows: jax.ShapeDtypeStruct((rows, N_HEADS * HEAD_PAD), BF16)
    return pl.pallas_call(
        body, grid=(N_HEADS, nq),
        in_specs=[qs, keys, kr] + [tab_q] * 3 + [tab_k] * 3 + [tab_q] * 3 + [tab_k] * 3 + [qs, qs, lse_spec],
        out_specs=[qs, keys, pl.BlockSpec((TT, HEAD_PAD), lambda h, i: (0, 0))],
        out_shape=[wide(T), wide(TT), jax.ShapeDtypeStruct((TT, HEAD_PAD), F32)],
        scratch_shapes=[pltpu.VMEM((TT, HEAD_PAD), BF16), pltpu.VMEM((TT, HEAD_PAD), BF16),
                        pltpu.VMEM((TT, HEAD_PAD), F32), pltpu.VMEM((TT, HEAD_PAD), F32)],
        compiler_params=_cp(("arbitrary", "arbitrary")), name="attn_bwd",
    )(*_in_hbm([q_raw, kv, pp, cos, slo, shi, cos, slo, shi, cos_i, slo_i, shi_i, cos_i, slo_i, shi_i, o, do, lse]))


def _hbm_specs(n):
    return [pl.BlockSpec(memory_space=pl.ANY)] * n


def _gather_weights(shards):
    n = len(shards)
    halves = [s.shape[0] // 2 for s in shards]

    def body(*refs):
        ins, outs = refs[:n], refs[n:2 * n]
        token, send_sems, recv_sems = refs[2 * n:]
        token[...] = jnp.zeros_like(token)
        mx, my, mc = lax.axis_index("x"), lax.axis_index("y"), lax.axis_index("c")
        j_me = 2 * mx + my
        chips = [(1 - mx, my), (mx, 1 - my), (1 - mx, 1 - my)]

        def half(w, chip_idx, hc):
            return outs[w].at[chip_idx, pl.ds(hc * halves[w], halves[w]), :]

        def copy(w, k, src, dst, to):
            return pltpu.make_async_remote_copy(src_ref=src, dst_ref=dst, send_sem=send_sems.at[w, k],
                                                recv_sem=recv_sems.at[w, k], device_id=to, device_id_type=MESH)

        sends = []
        for w in range(n):
            cp = copy(w, 6, ins[w], outs[w].at[j_me], (mx, my, 1 - mc))
            cp.start()
            sends.append(cp)
        for k, (px, py) in enumerate(chips):
            for w in range(n):
                cp = copy(w, k, ins[w].at[pl.ds(mc * halves[w], halves[w]), :], half(w, j_me, mc), (px, py, mc))
                cp.start()
                sends.append(cp)
        for k, (px, py) in enumerate(chips):
            for w in range(n):
                got = half(w, 2 * px + py, mc)
                copy(w, k, got, got, (px, py, mc)).wait_recv()
                cp = copy(w, 3 + k, got, got, (mx, my, 1 - mc))
                cp.start()
                sends.append(cp)
        for k, (px, py) in enumerate(chips):
            for w in range(n):
                got = half(w, 2 * px + py, 1 - mc)
                copy(w, 3 + k, got, got, (mx, my, 1 - mc)).wait_recv()
        for w in range(n):
            own = outs[w].at[j_me]
            copy(w, 6, own, own, (mx, my, 1 - mc)).wait_recv()
        for cp in sends:
            cp.wait_send()

    res = pl.pallas_call(
        body, out_shape=[jax.ShapeDtypeStruct((4,) + s.shape, s.dtype) for s in shards]
        + [jax.ShapeDtypeStruct((8, 128), F32)],
        in_specs=_hbm_specs(n), out_specs=_hbm_specs(n) + [pl.BlockSpec(memory_space=pltpu.VMEM)],
        scratch_shapes=[pltpu.SemaphoreType.DMA((n, 7)), pltpu.SemaphoreType.DMA((n, 7))],
        name="gather_weights")(*shards)
    return list(res[:n]), res[n]


def _rs_pair(gs, name):
    n = len(gs)
    halves = [g.shape[1] // 2 for g in gs]

    def body(*refs):
        ins, lands = refs[:n], refs[n:2 * n]
        send_sems, recv_sems = refs[2 * n:]
        mx, my, mc = lax.axis_index("x"), lax.axis_index("y"), lax.axis_index("c")
        copies = []
        for w in range(n):
            h = halves[w]
            cp = pltpu.make_async_remote_copy(
                src_ref=ins[w].at[:, pl.ds((1 - mc) * h, h), :], dst_ref=lands[w], send_sem=send_sems.at[w],
                recv_sem=recv_sems.at[w], device_id=(mx, my, 1 - mc), device_id_type=MESH)
            cp.start()
            copies.append(cp)
        for cp in copies:
            cp.wait()

    return pl.pallas_call(
        body, out_shape=[jax.ShapeDtypeStruct((4, h, g.shape[2]), g.dtype) for g, h in zip(gs, halves)],
        in_specs=_hbm_specs(n), out_specs=_hbm_specs(n),
        scratch_shapes=[pltpu.SemaphoreType.DMA((n,)), pltpu.SemaphoreType.DMA((n,))], name=name)(*gs)


def _rs_chips(parts):
    n = len(parts)

    def body(*refs):
        ins, lands = refs[:n], refs[n:2 * n]
        send_sems, recv_sems = refs[2 * n:]
        mx, my, mc = lax.axis_index("x"), lax.axis_index("y"), lax.axis_index("c")
        copies = []
        for k, (px, py) in enumerate([(1 - mx, my), (mx, 1 - my), (1 - mx, 1 - my)]):
            for w in range(n):
                cp = pltpu.make_async_remote_copy(
                    src_ref=ins[w].at[2 * px + py], dst_ref=lands[w].at[k], send_sem=send_sems.at[w, k],
                    recv_sem=recv_sems.at[w, k], device_id=(px, py, mc), device_id_type=MESH)
                cp.start()
                copies.append(cp)
        for cp in copies:
            cp.wait()

    return list(pl.pallas_call(
        body, out_shape=[jax.ShapeDtypeStruct((3,) + p.shape[1:], p.dtype) for p in parts],
        in_specs=_hbm_specs(n), out_specs=_hbm_specs(n),
        scratch_shapes=[pltpu.SemaphoreType.DMA((n, 3)), pltpu.SemaphoreType.DMA((n, 3))], name="rs_chips")(*parts))


def _rs_pair_back(gs, name):
    n = len(gs)

    def body(*refs):
        outs = refs[n:2 * n]
        send_sems, recv_sems = refs[2 * n:]
        mx, my, mc = lax.axis_index("x"), lax.axis_index("y"), lax.axis_index("c")
        copies = []
        for w in range(n):
            h = gs[w].shape[0] // 2
            mine = outs[w].at[pl.ds(mc * h, h), :]
            cp = pltpu.make_async_remote_copy(src_ref=mine, dst_ref=mine, send_sem=send_sems.at[w],
                                              recv_sem=recv_sems.at[w], device_id=(mx, my, 1 - mc), device_id_type=MESH)
            cp.start()
            copies.append(cp)
        for cp in copies:
            cp.wait()

    return pl.pallas_call(
        body, out_shape=[jax.ShapeDtypeStruct(g.shape, g.dtype) for g in gs],
        in_specs=_hbm_specs(n), out_specs=_hbm_specs(n), input_output_aliases={w: w for w in range(n)},
        scratch_shapes=[pltpu.SemaphoreType.DMA((n,)), pltpu.SemaphoreType.DMA((n,))], name=name)(*gs)


_HBM = pl.BlockSpec(memory_space=pltpu.HBM)
_SEM = pl.BlockSpec(memory_space=pltpu.SEMAPHORE)
_EFFECT = pltpu.SideEffectType.DATAFLOW_SIDE_EFFECTING


def _ici_copies(kind, srcs, lands, send_sems, recv_sems):
    n = len(srcs)
    mx, my, mc = lax.axis_index("x"), lax.axis_index("y"), lax.axis_index("c")
    j_me = 2 * mx + my
    copies = []
    if kind == "all":
        for k in range(7):
            a, b, c = (k + 1) >> 2 & 1, (k + 1) >> 1 & 1, (k + 1) & 1
            peer = (1 - mx if a else mx, 1 - my if b else my, 1 - mc if c else mc)
            for w in range(n):
                copies.append(pltpu.make_async_remote_copy(
                    src_ref=srcs[w], dst_ref=lands[w].at[4 * mx + 2 * my + mc], send_sem=send_sems.at[7 * w + k],
                    recv_sem=recv_sems.at[7 * w + k], device_id=peer, device_id_type=MESH))
        return copies
    if kind == "pair":
        for w in range(n):
            h = srcs[w].shape[1] // 2
            copies.append(pltpu.make_async_remote_copy(
                src_ref=srcs[w].at[:, pl.ds((1 - mc) * h, h), :], dst_ref=lands[w], send_sem=send_sems.at[w],
                recv_sem=recv_sems.at[w], device_id=(mx, my, 1 - mc), device_id_type=MESH))
        return copies
    for k, (px, py) in enumerate([(1 - mx, my), (mx, 1 - my), (1 - mx, 1 - my)]):
        for w in range(n):
            if kind == "gather":
                h = srcs[w].shape[0] // 2
                src, dst = srcs[w].at[pl.ds(mc * h, h), :], lands[w].at[j_me, pl.ds(mc * h, h), :]
            else:
                src, dst = srcs[w].at[2 * px + py], lands[w].at[k]
            copies.append(pltpu.make_async_remote_copy(
                src_ref=src, dst_ref=dst, send_sem=send_sems.at[3 * w + k], recv_sem=recv_sems.at[3 * w + k],
                device_id=(px, py, mc), device_id_type=MESH))
    return copies


_SEMS_PER_OPERAND = {"gather": 3, "scatter": 3, "all": 7, "pair": 1}


def _ici_start(kind, srcs, land_shapes, carry, name):
    n = len(srcs)

    def body(*refs):
        ins, lands = refs[:n], refs[n:2 * n]
        send_sems, recv_sems = refs[2 * n + 1], refs[2 * n + 2]
        for cp in _ici_copies(kind, ins, lands, send_sems, recv_sems):
            cp.start()

    hbm = lambda a: pltpu.with_memory_space_constraint(a, pltpu.HBM)
    lands = [lax.empty(s, srcs[0].dtype) for s in land_shapes]
    args = [hbm(a) for a in list(srcs) + lands + [carry]]
    n_sem = _SEMS_PER_OPERAND[kind] * n
    out_shape = ([pltpu.SemaphoreType.DMA((n_sem,)), pltpu.SemaphoreType.DMA((n_sem,))]
                 + [pltpu.HBM(a.shape, a.dtype) for a in args])
    res = pl.pallas_call(
        body, name=name, out_shape=out_shape, in_specs=[_HBM] * len(args), out_specs=[_SEM, _SEM] + [_HBM] * len(args),
        input_output_aliases={i: 2 + i for i in range(len(args))},
        compiler_params=pltpu.CompilerParams(has_side_effects=_EFFECT))(*args)
    return res[0], res[1], list(res[2:2 + n]), list(res[2 + n:2 + 2 * n]), res[2 + 2 * n]


def _ici_wait(kind, send_sems, recv_sems, srcs, lands, after, name):
    n = len(srcs)

    def body(*refs):
        ins, zones = refs[:n], refs[n:2 * n]
        for cp in _ici_copies(kind, ins, zones, refs[2 * n], refs[2 * n + 1]):
            cp.wait_send()
            cp.wait_recv()

    args = list(srcs) + list(lands)
    res = pl.pallas_call(
        body, name=name, out_shape=[pltpu.HBM(a.shape, a.dtype) for a in args],
        in_specs=[_HBM] * len(args) + [_SEM, _SEM, pl.BlockSpec(memory_space=pl.ANY)], out_specs=[_HBM] * len(args),
        input_output_aliases={i: i for i in range(len(args))},
        compiler_params=pltpu.CompilerParams(has_side_effects=_EFFECT))(*args, send_sems, recv_sems, after)
    return list(res[:n]), list(res[n:])


def _gather_finish(shards, lands):
    n = len(shards)

    def body(*refs):
        own, outs = refs[:n], refs[2 * n:3 * n]
        send_sems, recv_sems = refs[3 * n:]
        mx, my, mc = lax.axis_index("x"), lax.axis_index("y"), lax.axis_index("c")
        j_me = 2 * mx + my
        sibling = (mx, my, 1 - mc)
        copies = []

        def push(w, k, src, dst):
            cp = pltpu.make_async_remote_copy(src_ref=src, dst_ref=dst, send_sem=send_sems.at[w, k],
                                              recv_sem=recv_sems.at[w, k], device_id=sibling, device_id_type=MESH)
            cp.start()
            copies.append(cp)

        for w in range(n):
            h = shards[w].shape[0] // 2
            push(w, 3, own[w], outs[w].at[j_me])
            for k, (px, py) in enumerate([(1 - mx, my), (mx, 1 - my), (1 - mx, 1 - my)]):
                got = outs[w].at[2 * px + py, pl.ds(mc * h, h), :]
                push(w, k, got, got)
        for cp in copies:
            cp.wait()

    return pl.pallas_call(
        body, out_shape=[jax.ShapeDtypeStruct(l.shape, l.dtype) for l in lands],
        in_specs=_hbm_specs(2 * n), out_specs=_hbm_specs(n), input_output_aliases={n + w: w for w in range(n)},
        scratch_shapes=[pltpu.SemaphoreType.DMA((n, 4)), pltpu.SemaphoreType.DMA((n, 4))], name="gather_finish",
    )(*shards, *lands)


def _tile_rows(h, c, itemsize, mult):
    best = h
    for t in range(mult, h + 1, mult):
        if h % t == 0 and t * c * itemsize <= (1 << 21):
            best = t
    return best


def _add_pair(g, land, place, name):
    _, h, c = land.shape
    t = _tile_rows(h, c, 2, 16)
    nb = h // t
    return _ew(lambda ids, u, v: (u.astype(F32) + v.astype(F32),), (4, nb),
               [(g, pl.BlockSpec((None, t, c), lambda j, i, s: (j, s[1] * nb + i, 0))),
                (land, pl.BlockSpec((None, t, c), lambda j, i, s: (j, i, 0)))],
               [(land.shape, BF16, pl.BlockSpec((None, t, c), lambda j, i, s: (j, i, 0)), None)], name, scalars=place)[0]


def _add_chips(own, land, place, name):
    _, h, c = land.shape
    t = _tile_rows(h, c, 4, 16)
    nb = h // t

    def fn(ids, a, b):
        return (((a.astype(F32) + b[0].astype(F32)) + b[1].astype(F32)) + b[2].astype(F32),)

    return _ew(fn, (nb,), [(own, pl.BlockSpec((None, t, c), lambda i, s: (s[0], i, 0))),
                           (land, pl.BlockSpec((3, t, c), lambda i, s: (0, i, 0)))],
               [((2 * h, c), F32, pl.BlockSpec((t, c), lambda i, s: (s[1] * nb + i, 0)), None)], name, scalars=place)[0]


W_IN_SEGMENTS = ((0, 256, KV0), (256, 288, KR0 + 64), (288, 672, Q0), (672, 1184, CX0), (1184, 1696, CB0),
                 (1696, 2208, CC0), (2208, 3232, GA0), (3232, 4256, GC0))
W_IN_SHARD = 1064


W_IN_SHARD_PAD = 1088


def _w_in_t_p_from_shards(s):
    pieces = []
    for o0, o1, p0 in sorted(W_IN_SEGMENTS, key=lambda t: t[2]):
        if p0 == KR0 + 64:
            pieces.append(jnp.zeros((64, s.shape[2]), s.dtype))
        for j in range(4):
            lo, hi = max(o0, j * W_IN_SHARD), min(o1, (j + 1) * W_IN_SHARD)
            if lo < hi:
                pieces.append(s[j, lo - j * W_IN_SHARD:hi - j * W_IN_SHARD])
    pieces.append(jnp.zeros((32, s.shape[2]), s.dtype))
    return jnp.concatenate(pieces, axis=0)


def _w_in_t_shards_from_p(g):
    shards = []
    for j in range(4):
        pieces = []
        for o0, o1, p0 in W_IN_SEGMENTS:
            lo, hi = max(o0, j * W_IN_SHARD), min(o1, (j + 1) * W_IN_SHARD)
            if lo < hi:
                pieces.append(g[p0 + lo - o0:p0 + hi - o0])
        pieces.append(jnp.zeros((W_IN_SHARD_PAD - W_IN_SHARD, g.shape[1]), g.dtype))
        shards.append(jnp.concatenate(pieces, axis=0))
    return jnp.stack(shards, axis=0)


def _cols_from_shards(s):
    return jnp.transpose(s, (1, 0, 2)).reshape(s.shape[1], -1)


def _rope_tables(T, TT, inverse):
    rows = T // GRID_W
    row = jnp.repeat(jnp.arange(rows), GRID_W).astype(F32)
    col = jnp.tile(jnp.arange(GRID_W), rows).astype(F32)
    inv = ROPE_THETA ** (-jnp.arange(0, 16, 2, dtype=F32) / 16)
    ang = jnp.concatenate([row[:, None] * inv, col[:, None] * inv], axis=-1)
    cos, sin = jnp.cos(ang), jnp.sin(ang)
    lane = jnp.arange(32)
    src = (lane // 16) * 8 + lane % 8
    lo = ((lane % 16) // 8 == 0).astype(F32)
    sgn = -1.0 if inverse else 1.0
    cos32 = cos[:, src]
    sin_lo32 = -sgn * sin[:, src] * lo
    sin_hi32 = sgn * sin[:, src] * (1.0 - lo)

    def widen(t32, fill):
        t = jnp.concatenate([jnp.full((T, 64), fill, F32), t32, jnp.full((T, 32), fill, F32)], axis=1)
        return jnp.concatenate([t, jnp.full((TT - T, HEAD_PAD), fill, F32)], axis=0)

    return widen(cos32, 1.0), widen(sin_lo32, 0.0), widen(sin_hi32, 0.0)


def _local_step(xx, tgt, mod_lat, mod_ctx, W, late_weights, early_grads):
    TT = xx.shape[0]
    T = tgt.shape[0]
    n_lat, n_all = T // ROW_TILE, TT // ROW_TILE
    sh1, sc1, g1, sh2, sc2, g2 = [mod_lat[:, k * D_MODEL:(k + 1) * D_MODEL] for k in range(6)]
    csh1, csc1 = mod_ctx[:, :D_MODEL], mod_ctx[:, D_MODEL:2 * D_MODEL]
    vec = lambda n: _full((1, n))
    row_out = lambda n, dt, rows=T: ((rows, n), dt, _rows(n), None)
    acc_out = lambda n: ((1, n), F32, _full((1, n)), 0)

    def f_norm1(ids, x, g, a_sh, a_sc, b_sh, b_sc):
        ctx = ids[0] >= n_lat
        sh, sc = jnp.where(ctx, b_sh, a_sh), jnp.where(ctx, b_sc, a_sc)
        return ((x * _rms(x) * g) * (1.0 + sc) + sh,)

    (hh,) = _ew(f_norm1, (n_all,), [(xx, _rows(D_MODEL)), (W["norm1_g"], vec(D_MODEL)), (sh1, vec(D_MODEL)),
                                   (sc1, vec(D_MODEL)), (csh1, vec(D_MODEL)), (csc1, vec(D_MODEL))],
                [row_out(D_MODEL, BF16, TT)], "norm1_fwd")
    tm_all = _pick(TT, (768, 256))
    pp = _mm(hh, W["w_in_t"], "nt", TT, P_COLS, D_MODEL, tm=tm_all, tn=2176, tk=D_MODEL, name="w_in_fwd")

    def f_lowrank(ids, ckv, cq, gkv, gq):
        return ckv * _rms(ckv) * gkv, cq * _rms(cq) * gq

    nkv, nq = _ew(f_lowrank, (n_all,), [(pp, _rows(KV_RANK, KV0 // KV_RANK)), (pp, _rows(Q_RANK, Q0 // Q_RANK)),
                                       (W["kv_norm_g"], vec(KV_RANK)), (W["q_norm_g"], vec(Q_RANK))],
                  [row_out(KV_RANK, BF16, TT), row_out(Q_RANK, BF16, TT)], "lowrank_norm_fwd")
    kv = _mm(nkv, W["w_ukv"], "nn", TT, 1024, KV_RANK, tm=tm_all, tn=256, tk=KV_RANK, name="w_ukv_fwd",
             b_spec=pl.BlockSpec((None, KV_RANK, 256), lambda i, j, k: (j, k, 0)))
    q_raw = _mm(nq, W["w_uq_t"], "nt", TT, 1024, Q_RANK, tm=tm_all, tn=1024, tk=Q_RANK, name="w_uq_fwd")

    tabs = _rope_tables(T, TT, inverse=False)
    tabs_inv = _rope_tables(T, TT, inverse=True)
    o_pad, lse = _attn_fwd(q_raw, kv, pp, tabs, T, TT)
    W = dict(W, **late_weights(o_pad))
    tm_lat = _pick(T, (1024, 512, 256))
    ya = _mm(o_pad, W["w_attn_out"], "nn", T, D_MODEL, 1024, tm=tm_lat, tn=D_MODEL, tk=1024, name="w_attn_out_fwd")

    tc = 256
    colT = lambda blk0: pl.BlockSpec((T, tc), lambda j: (0, blk0 + j))

    def f_conv(ids, xin, cb, cc, w, b):
        return (cb * _conv(cc * xin, w, b),)

    (e,) = _ew(f_conv, (CONV_DIM // tc,),
               [(pp, colT(CX0 // tc)), (pp, colT(CB0 // tc)), (pp, colT(CC0 // tc)),
                (W["conv_w"], pl.BlockSpec((3, tc), lambda j: (0, j))), (W["conv_b"], pl.BlockSpec((1, tc), lambda j: (0, j)))],
               [((T, CONV_DIM), BF16, colT(0), None)], "conv_fwd")
    yc = _mm(e, W["w_conv_out"], "nn", T, D_MODEL, CONV_DIM, tm=tm_lat, tn=256, tk=CONV_DIM, name="w_conv_out_fwd",
             b_spec=pl.BlockSpec((None, CONV_DIM, 256), lambda i, j, k: (j, k, 0)))

    def f_merge(ids, ga, gc, a, c):
        return (_sigmoid(ga) * a + _sigmoid(gc) * c,)

    (mrg,) = _ew(f_merge, (n_lat,), [(pp, _rows(D_MODEL, 0)), (pp, _rows(D_MODEL, 1)), (ya, _rows(D_MODEL)),
                                    (yc, _rows(D_MODEL))], [row_out(D_MODEL, BF16)], "merge_fwd")
    mo = _mm(mrg, W["w_o"], "nn", T, D_MODEL, D_MODEL, tm=tm_lat, tn=D_MODEL, tk=D_MODEL, name="w_o_fwd")

    def f_norm2(ids, x, m, gate, g, sh, sc):
        x1 = x + gate * m
        return x1, (x1 * _rms(x1) * g) * (1.0 + sc) + sh

    x1, h2 = _ew(f_norm2, (n_lat,), [(xx, _rows(D_MODEL)), (mo, _rows(D_MODEL)), (g1, vec(D_MODEL)),
                                    (W["norm2_g"], vec(D_MODEL)), (sh2, vec(D_MODEL)), (sc2, vec(D_MODEL))],
                 [row_out(D_MODEL, F32), row_out(D_MODEL, BF16)], "norm2_fwd")
    up = _mm(h2, W["w_up"], "nn", T, 2 * D_FF, D_MODEL, tm=tm_lat, tn=1408, tk=D_MODEL, name="w_up_fwd",
             b_spec=pl.BlockSpec((None, D_MODEL, 1408), lambda i, j, k: (j, k, 0)))

    n_ff = D_FF // tc
    ffw = lambda off, n=3: pl.BlockSpec((n, tc), lambda j: (0, j + off))

    def f_ffn(ids, ug, uv, wg, wv, bg, bv):
        gate, val = _conv(ug, wg, bg), _conv(uv, wv, bv)
        return (gate * _sigmoid(gate) * val,)

    (act,) = _ew(f_ffn, (n_ff,), [(up, colT(0)), (up, colT(n_ff)), (W["ffn_conv_w"], ffw(0)), (W["ffn_conv_w"], ffw(n_ff)),
                                 (W["ffn_conv_b"], ffw(0, 1)), (W["ffn_conv_b"], ffw(n_ff, 1))],
                 [((T, D_FF), BF16, colT(0), None)], "ffn_act_fwd")
    f = _mm(act, W["w_down"], "nn", T, D_MODEL, D_FF, tm=tm_lat, tn=D_MODEL, tk=D_FF, name="w_down_fwd")

    def f_head(ids, x1_, f_, gate, gf, t):
        x2 = x1_ + gate * f_
        r = _rms(x2)
        xn = x2 * r
        err = xn * gf - t
        loss = 0.5 * jnp.sum(jnp.mean(err * err, axis=-1, keepdims=True))
        dy = err * (1.0 / D_MODEL)
        dx2 = _rms_bwd(dy * gf, xn, r)
        return dx2, dx2 * gate, _colsum(dy * xn), _colsum(dx2 * f_), jnp.full((1, 128), loss, F32)

    dx2, df, dg_f, dg2, loss = _ew(
        f_head, (n_lat,), [(x1, _rows(D_MODEL)), (f, _rows(D_MODEL)), (g2, vec(D_MODEL)), (W["final_g"], vec(D_MODEL)),
                           (tgt, _rows(D_MODEL))],
        [row_out(D_MODEL, F32), row_out(D_MODEL, BF16), acc_out(D_MODEL), acc_out(D_MODEL), acc_out(128)], "loss_head")

    d_w_down = _mm(act, df, "tn", D_FF, D_MODEL, T, tm=1408, tn=D_MODEL, tk=T, name="w_down_dw",
                   out_dtype=BF16).reshape(4, D_FF // 4, D_MODEL)
    da = _mm(df, W["w_down"], "nt", T, D_FF, D_MODEL, tm=tm_lat, tn=1408, tk=D_MODEL, name="w_down_dx")

    tcb = 128
    n_fb = D_FF // tcb
    colb = lambda blk0: pl.BlockSpec((T, tcb), lambda j: (0, blk0 + j))
    ffwb = lambda off, n=3: pl.BlockSpec((n, tcb), lambda j: (0, j + off))
    cvec = ((1, D_FF), F32, pl.BlockSpec((1, tcb), lambda j: (0, j)), None)

    def f_ffn_bwd(ids, ug, uv, d_act, wg, wv, bg, bv):
        sg, sv = _shifts(ug), _shifts(uv)
        gate, val = _conv(ug, wg, bg, sg), _conv(uv, wv, bv, sv)
        s = _sigmoid(gate)
        d_gate = d_act * val * s * (1.0 + gate * (1.0 - s))
        d_val = d_act * gate * s
        wg0, wg1, wg2 = _conv_bwd_w(d_gate, ug, sg)
        wv0, wv1, wv2 = _conv_bwd_w(d_val, uv, sv)
        d_up = [_conv_bwd_x(d_gate, wg), _conv_bwd_x(d_val, wv)]
        return d_up, [_colsum(d_gate), _colsum(d_val), wg0, wg1, wg2, wv0, wv1, wv2]

    d_up3, ffn_stats = _ew(
        f_ffn_bwd, (n_fb,),
        [(up, colb(0)), (up, colb(n_fb)), (da, colb(0)), (W["ffn_conv_w"], ffwb(0)), (W["ffn_conv_w"], ffwb(n_fb)),
         (W["ffn_conv_b"], ffwb(0, 1)), (W["ffn_conv_b"], ffwb(n_fb, 1))],
        [((2, T, D_FF), BF16, pl.BlockSpec((2, T, tcb), lambda j: (0, 0, j)), None),
         ((n_fb, 8, 1, tcb), F32, pl.BlockSpec((None, 8, 1, tcb), lambda j: (j, 0, 0, 0)), None)], "ffn_act_bwd")
    stat = lambda s: ffn_stats[:, s, 0, :].reshape(1, D_FF)
    d_ffn_conv_b = jnp.concatenate([stat(0), stat(1)], axis=1)
    d_ffn_conv_w = jnp.concatenate([jnp.concatenate([stat(2), stat(3), stat(4)], axis=0),
                                    jnp.concatenate([stat(5), stat(6), stat(7)], axis=0)], axis=1)

    tk_t = T
    d_w_up = _mm(h2, d_up3, "tn", D_MODEL, 2 * D_FF, T, tm=D_MODEL, tn=1408, tk=tk_t, name="w_up_dw", out_dtype=BF16,
                 b_spec=pl.BlockSpec((None, tk_t, 1408), lambda i, j, k: (j // 2, k, j % 2)),
                 o_spec=pl.BlockSpec((None, D_MODEL, 1408), lambda i, j, k: (j, i, 0)), out_shape=(4, D_MODEL, 1408))
    dh2 = _mm(d_up3, W["w_up"], "nt", T, D_MODEL, 2 * D_FF, tm=tm_lat, tn=D_MODEL, tk=1408, name="w_up_dx",
              a_spec=pl.BlockSpec((None, tm_lat, 1408), lambda i, j, k: (k // 2, i, k % 2)),
              b_spec=pl.BlockSpec((None, D_MODEL, 1408), lambda i, j, k: (k, j, 0)))

    def f_norm2_bwd(ids, dx2_, dh, x1_, m, g, sc, gate):
        r = _rms(x1_)
        xn = x1_ * r
        dx1 = dx2_ + _rms_bwd(dh * g * (1.0 + sc), xn, r)
        return dx1, dx1 * gate, _colsum(dh), _colsum(dh * xn * g), _colsum(dh * xn * (1.0 + sc)), _colsum(dx1 * m)

    dx1, dmo, dsh2, dsc2, dg_n2, dg1 = _ew(
        f_norm2_bwd, (n_lat,), [(dx2, _rows(D_MODEL)), (dh2, _rows(D_MODEL)), (x1, _rows(D_MODEL)), (mo, _rows(D_MODEL)),
                                (W["norm2_g"], vec(D_MODEL)), (sc2, vec(D_MODEL)), (g1, vec(D_MODEL))],
        [row_out(D_MODEL, F32), row_out(D_MODEL, BF16)] + [acc_out(D_MODEL)] * 4, "norm2_bwd")
    d_w_o = _mm(mrg, dmo, "tn", D_MODEL, D_MODEL, T, tm=D_MODEL, tn=D_MODEL, tk=tk_t, name="w_o_dw",
                out_dtype=BF16).reshape(4, D_MODEL // 4, D_MODEL)
    dmrg = _mm(dmo, W["w_o"], "nt", T, D_MODEL, D_MODEL, tm=tm_lat, tn=D_MODEL, tk=D_MODEL, name="w_o_dx")
    dmrg = early_grads("late", {"w_o": d_w_o, "w_up": d_w_up, "w_down": d_w_down}, dmrg)

    def f_merge_bwd(ids, dm, ga, gc, a, c):
        sa, sc_ = _sigmoid(ga), _sigmoid(gc)
        return dm * sa, dm * sc_, dm * a * sa * (1.0 - sa), dm * c * sc_ * (1.0 - sc_)

    dya, dyc, dp_ga, dp_gc = _ew(
        f_merge_bwd, (n_lat,), [(dmrg, _rows(D_MODEL)), (pp, _rows(D_MODEL, 0)), (pp, _rows(D_MODEL, 1)),
                                (ya, _rows(D_MODEL)), (yc, _rows(D_MODEL))], [row_out(D_MODEL, BF16)] * 4, "merge_bwd")

    d_w_ao_p = _mm(o_pad, dya, "tn", 1024, D_MODEL, T, tm=1024, tn=D_MODEL, tk=tk_t, name="w_attn_out_dw", out_dtype=BF16)
    do_pad = _mm(dya, W["w_attn_out"], "nt", T, 1024, D_MODEL, tm=tm_lat, tn=1024, tk=D_MODEL, name="w_attn_out_dx")
    d_w_co = _mm(e, dyc, "tn", CONV_DIM, D_MODEL, T, tm=CONV_DIM, tn=256, tk=tk_t, name="w_conv_out_dw", out_dtype=BF16,
                 o_spec=pl.BlockSpec((None, CONV_DIM, 256), lambda i, j, k: (j, i, 0)), out_shape=(4, CONV_DIM, 256))
    de = _mm(dyc, W["w_conv_out"], "nt", T, CONV_DIM, D_MODEL, tm=tm_lat, tn=CONV_DIM, tk=256, name="w_conv_out_dx",
             b_spec=pl.BlockSpec((None, CONV_DIM, 256), lambda i, j, k: (k, j, 0)))

    def f_conv_bwd(ids, xin, cb, cc, d_e, w, b):
        z = cc * xin
        sz = _shifts(z)
        cz = _conv(z, w, b, sz)
        dcz = d_e * cb
        w0, w1, w2 = _conv_bwd_w(dcz, z, sz)
        dz = _conv_bwd_x(dcz, w)
        return dz * cc, d_e * cz, dz * xin, _colsum(dcz), w0, w1, w2

    cvec_c = ((1, CONV_DIM), F32, pl.BlockSpec((1, tc), lambda j: (0, j)), None)
    conv_b = _ew(f_conv_bwd, (CONV_DIM // tc,),
                 [(pp, colT(CX0 // tc)), (pp, colT(CB0 // tc)), (pp, colT(CC0 // tc)), (de, colT(0)),
                  (W["conv_w"], pl.BlockSpec((3, tc), lambda j: (0, j))), (W["conv_b"], pl.BlockSpec((1, tc), lambda j: (0, j)))],
                 [((T, CONV_DIM), BF16, colT(0), None)] * 3 + [cvec_c] * 4, "conv_bwd")
    dp_cx, dp_cb, dp_cc, d_conv_b = conv_b[:4]
    d_conv_w = jnp.concatenate(conv_b[4:7], axis=0)

    dq_raw, dkv, dp_kr = _attn_bwd(q_raw, kv, pp, o_pad, do_pad, lse, tabs, tabs_inv, T, TT)

    tk_a = TT
    d_w_uq_t = _mm(nq, dq_raw, "tn", Q_RANK, 1024, T, tm=Q_RANK, tn=1024, tk=T, name="w_uq_dw", transpose_out=True)
    dnq = _mm(dq_raw, W["w_uq_t"], "nn", T, Q_RANK, 1024, tm=tm_lat, tn=Q_RANK, tk=1024, name="w_uq_dx")
    d_w_ukv = _mm(nkv, dkv, "tn", KV_RANK, 1024, TT, tm=KV_RANK, tn=256, tk=tk_a, name="w_ukv_dw", out_dtype=BF16,
                  o_spec=pl.BlockSpec((None, KV_RANK, 256), lambda i, j, k: (j, i, 0)), out_shape=(4, KV_RANK, 256))
    dnkv = _mm(dkv, W["w_ukv"], "nt", TT, KV_RANK, 1024, tm=tm_all, tn=KV_RANK, tk=256, name="w_ukv_dx",
               b_spec=pl.BlockSpec((None, KV_RANK, 256), lambda i, j, k: (k, j, 0)))
    dnkv = early_grads("mid", {
        "w_attn_out": jnp.transpose(d_w_ao_p.reshape(N_HEADS, HEAD_PAD, 4, 256)[:, 64:], (2, 0, 1, 3)).reshape(
            4, N_HEADS * 64, 256),
        "w_conv_out": d_w_co,
        "w_uq": d_w_uq_t.reshape(4, 2, HEAD_PAD, Q_RANK)[:, :, :QK_DIM].reshape(4, 2 * QK_DIM, Q_RANK).astype(BF16),
        "w_ukv": d_w_ukv}, dnkv)

    def f_lowrank_bwd(ids, ckv, cq, dkv_, dq_, gkv, gq, ga, gc, cx, cb, cc, kr):
        rk, rq = _rms(ckv), _rms(cq)
        nk, nq_ = ckv * rk, cq * rq
        lat = ids[0] < n_lat
        dq_ = jnp.where(lat, dq_, 0.0)
        pieces = [jnp.where(lat, a, jnp.zeros_like(a)) for a in (ga, gc, cx, cb, cc)]
        pieces += [_rms_bwd(dkv_ * gkv, nk, rk).astype(BF16), _rms_bwd(dq_ * gq, nq_, rq).astype(BF16), kr.astype(BF16)]
        return jnp.concatenate(pieces, axis=1), _colsum(dkv_ * nk), _colsum(dq_ * nq_)

    lat_rows = lambda n: pl.BlockSpec((ROW_TILE, n), lambda i: (jnp.minimum(i, n_lat - 1), 0))
    dpp, dg_kv, dg_q = _ew(
        f_lowrank_bwd, (n_all,), [(pp, _rows(KV_RANK, KV0 // KV_RANK)), (pp, _rows(Q_RANK, Q0 // Q_RANK)),
                                  (dnkv, _rows(KV_RANK)), (dnq, lat_rows(Q_RANK)), (W["kv_norm_g"], vec(KV_RANK)),
                                  (W["q_norm_g"], vec(Q_RANK)), (dp_ga, lat_rows(D_MODEL)), (dp_gc, lat_rows(D_MODEL)),
                                  (dp_cx, lat_rows(CONV_DIM)), (dp_cb, lat_rows(CONV_DIM)), (dp_cc, lat_rows(CONV_DIM)),
                                  (dp_kr, _rows(HEAD_PAD))],
        [row_out(P_COLS, BF16, TT), acc_out(KV_RANK), acc_out(Q_RANK)], "lowrank_norm_bwd")
    d_w_in_t = _mm(hh, dpp, "tn", D_MODEL, P_COLS, TT, tm=512, tn=2176, tk=TT, name="w_in_dw", out_dtype=BF16,
                   transpose_out=True)
    dhh = _mm(dpp, W["w_in_t"], "nn", TT, D_MODEL, P_COLS, tm=tm_all, tn=512, tk=2176, name="w_in_dx")

    def f_norm1_bwd(ids, x, dh, dres, g, sc):
        r = _rms(x)
        xn = x * r
        return (dres + _rms_bwd(dh * g * (1.0 + sc), xn, r), _colsum(dh), _colsum(dh * xn * g),
                _colsum(dh * xn * (1.0 + sc)))

    grad_x, dsh1, dsc1, dg_n1 = _ew(
        f_norm1_bwd, (n_lat,), [(xx, _rows(D_MODEL)), (dhh, _rows(D_MODEL)), (dx1, _rows(D_MODEL)),
                                (W["norm1_g"], vec(D_MODEL)), (sc1, vec(D_MODEL))],
        [row_out(D_MODEL, F32)] + [acc_out(D_MODEL)] * 3, "norm1_bwd")

    def f_norm1_ctx_bwd(ids, x, dh, g, sc):
        xn = x * _rms(x)
        return _colsum(dh), _colsum(dh * xn * g), _colsum(dh * xn * (1.0 + sc))

    n_ctx = n_all - n_lat
    dcsh1, dcsc1, dg_n1c = _ew(
        f_norm1_ctx_bwd, (n_ctx,), [(xx, _rows(D_MODEL, 0, n_lat)), (dhh, _rows(D_MODEL, 0, n_lat)),
                                    (W["norm1_g"], vec(D_MODEL)), (csc1, vec(D_MODEL))], [acc_out(D_MODEL)] * 3,
        "norm1_ctx_bwd")

    big = {"w_in": _w_in_t_shards_from_p(d_w_in_t).astype(BF16)}
    zero = jnp.zeros((1, 4 * D_MODEL), F32)
    small = {
        "dmod_lat": jnp.concatenate([dsh1, dsc1, dg1, dsh2, dsc2, dg2], axis=1),
        "dmod_ctx": jnp.concatenate([dcsh1, dcsc1, zero], axis=1),
        "norm1_g": dg_n1 + dg_n1c, "norm2_g": dg_n2, "final_g": dg_f, "q_norm_g": dg_q, "kv_norm_g": dg_kv,
        "conv_b": d_conv_b, "conv_w": d_conv_w.reshape(1, -1), "ffn_conv_b": d_ffn_conv_b,
        "ffn_conv_w": d_ffn_conv_w.reshape(1, -1),
    }
    return grad_x, loss, big, small


SMALL = (("dmod_lat", 6144), ("dmod_ctx", 6144), ("norm1_g", 1024), ("norm2_g", 1024), ("final_g", 1024),
         ("q_norm_g", 384), ("kv_norm_g", 256), ("conv_b", 512), ("conv_w", 1536), ("ffn_conv_b", 5632),
         ("ffn_conv_w", 16896), ("loss", 128))
SMALL_ROWS = 320


def _adam_update(w, g, m, v):
    c1, c2 = 1.0 - ADAM_B1 ** ADAM_STEP, 1.0 - ADAM_B2 ** ADAM_STEP
    m2 = ADAM_B1 * m + (1.0 - ADAM_B1) * g
    v2 = ADAM_B2 * v + (1.0 - ADAM_B2) * (g * g)
    return [-ADAM_LR * ((m2 / c1) / (jnp.sqrt(v2 / c2) + ADAM_EPS) + ADAM_WD * w), m2, v2]


def _adamw(w, g, m, v, name):
    R, C = w.shape
    tr = 8 if R % 8 == 0 else R
    for t in range(8, R + 1, 8):
        if R % t == 0 and t * C * 4 <= (1 << 20):
            tr = t
    spec = pl.BlockSpec((tr, C), lambda i: (i, 0))
    return _ew(lambda ids, *vals: _adam_update(*vals), (R // tr,), [(w, spec), (g, spec), (m, spec), (v, spec)],
               [((R, C), F32, spec, None)] * 3, name)


def kernel(x, c, ctx, c_ctx, w_ada, b_ada, norm1_g, w_in, q_norm_g, kv_norm_g, w_uq, w_ukv, conv_w, conv_b, w_attn_out, w_conv_out, w_o, norm2_g, w_up, ffn_conv_w, ffn_conv_b, w_down, final_g, loss_target, m_c_ctx, m_w_ada, m_b_ada, m_norm1_g, m_w_in, m_q_norm_g, m_kv_norm_g, m_w_uq, m_w_ukv, m_conv_w, m_conv_b, m_w_attn_out, m_w_conv_out, m_w_o, m_norm2_g, m_w_up, m_ffn_conv_w, m_ffn_conv_b, m_w_down, m_final_g, v_c_ctx, v_w_ada, v_b_ada, v_norm1_g, v_w_in, v_q_norm_g, v_kv_norm_g, v_w_uq, v_w_ukv, v_conv_w, v_conv_b, v_w_attn_out, v_w_conv_out, v_w_o, v_norm2_g, v_w_up, v_ffn_conv_w, v_ffn_conv_b, v_w_down, v_final_g):
    mx, my, mc = lax.axis_index("x"), lax.axis_index("y"), lax.axis_index("c")
    chip = 2 * mx + my
    dev = 4 * mx + 2 * my + mc
    T, Tc = x.shape[1], ctx.shape[1]
    TT = T + Tc
    w_in_t, m_w_in_t, v_w_in_t = (jnp.transpose(a[0]) for a in (w_in, m_w_in, v_w_in))
    w_uq_t, m_w_uq_t, v_w_uq_t = (jnp.transpose(a[0]) for a in (w_uq, m_w_uq, v_w_uq))
    conv_sh = jnp.concatenate([conv_w[0], ffn_conv_w[0]], axis=1)
    pay1 = jnp.concatenate([jnp.pad(c, ((0, 7), (0, 0))), jnp.pad(conv_sh, ((0, 5), (0, 0)))], axis=1)
    c_send, c_recv, c_src, c_land, w_in_thru = _ici_start("all", [pay1], [(8, 8, 2560)], w_in_t, "cond_start")
    shards = {"w_in": jnp.pad(w_in_thru, ((0, W_IN_SHARD_PAD - W_IN_SHARD), (0, 0))).astype(BF16), "w_uq": w_uq_t,
              "w_ukv": w_ukv[0], "w_attn_out": w_attn_out[0], "w_conv_out": w_conv_out[0], "w_o": w_o[0],
              "w_up": w_up[0], "w_down": w_down[0]}
    (pay1,), (c_land,) = _ici_wait("all", c_send, c_recv, c_src, c_land, shards["w_in"], "cond_wait")
    got1 = lax.dynamic_update_slice(c_land, pay1[None], (dev, 0, 0))
    c_all = got1[:, 0, :D_MODEL]
    conv_all = got1[0::2, :3, D_MODEL:]
    conv_w_full = _cols_from_shards(conv_all[:, :, :128])
    ffn_conv_w_full = _cols_from_shards(conv_all[:, :, 128:])

    cond = jnp.concatenate([c_all, c_ctx.reshape(1, D_MODEL), jnp.zeros((7, D_MODEL), F32)], axis=0)

    def f_silu(ids, v):
        return (v * _sigmoid(v),)

    (s16,) = _ew(f_silu, (1,), [(cond, _full((16, D_MODEL)))], [((16, D_MODEL), F32, _full((16, D_MODEL)), None)], "silu_cond")
    mod_sh = _mm(s16, w_ada[0], "nn", 16, 1536, D_MODEL, tm=16, tn=768, tk=D_MODEL, name="w_ada_fwd")
    m_send, m_recv, m_src, m_land, ukv_thru = _ici_start("all", [mod_sh], [(8, 16, 1536)], w_ukv[0], "mod_start")
    shards["w_ukv"] = ukv_thru

    names = [n for n, _ in BIG]
    first = [n for n in names if n not in GATHER_LATE]
    gathered, zero = _gather_weights([shards[n].astype(BF16) for n in first])
    full = dict(zip(first, gathered))
    (mod_mine,), (m_land,) = _ici_wait("all", m_send, m_recv, m_src, m_land, gathered[0], "mod_wait")
    got2 = lax.dynamic_update_slice(m_land, mod_mine[None], (dev, 0, 0))
    mod_all = _cols_from_shards(got2[0::2]) + b_ada
    mod_lat = lax.dynamic_slice_in_dim(mod_all, dev, 1, axis=0)
    mod_ctx = mod_all[8:9]
    xx = jnp.concatenate([x[0], ctx[0]], axis=0)
    late_bf = [(shards[n] + zero[0, 0]).astype(BF16) for n in GATHER_LATE]
    g_send, g_recv, late_src, late_land, xx = _ici_start(
        "gather", late_bf, [(4,) + s.shape for s in late_bf], xx, "gather_late_start")

    def late_weights(after):
        src, land = _ici_wait("gather", g_send, g_recv, late_src, late_land, after, "gather_late_wait")
        got = dict(zip(GATHER_LATE, _gather_finish(src, land)))
        wao = _cols_from_shards(got["w_attn_out"]).reshape(N_HEADS, 64, D_MODEL)
        return {"w_attn_out": jnp.pad(wao, ((0, 0), (64, 0), (0, 0))).reshape(N_HEADS * HEAD_PAD, D_MODEL),
                "w_conv_out": got["w_conv_out"], "w_o": got["w_o"].reshape(D_MODEL, D_MODEL), "w_up": got["w_up"],
                "w_down": got["w_down"].reshape(D_FF, D_MODEL)}

    wuq_t = full["w_uq"].reshape(N_HEADS, QK_DIM, Q_RANK)
    W = {
        "w_in_t": _w_in_t_p_from_shards(full["w_in"]),
        "w_uq_t": jnp.pad(wuq_t, ((0, 0), (0, HEAD_PAD - QK_DIM), (0, 0))).reshape(N_HEADS * HEAD_PAD, Q_RANK),
        "w_ukv": full["w_ukv"],
        "norm1_g": norm1_g, "norm2_g": norm2_g, "final_g": final_g.reshape(1, D_MODEL), "q_norm_g": q_norm_g,
        "kv_norm_g": kv_norm_g, "conv_w": conv_w_full, "conv_b": conv_b, "ffn_conv_w": ffn_conv_w_full,
        "ffn_conv_b": ffn_conv_b,
    }

    place = jnp.stack([chip, mc]).astype(jnp.int32)
    early = {}

    def early_grads(tag, g, carry):
        gs = list(g.values())
        from_sib = _rs_pair(gs, "rs_pair_" + tag)
        sums = [_add_pair(gs[w], from_sib[w], place, "rs_pair_add_" + n) for w, n in enumerate(g)]
        send, recv, sums, land, carry = _ici_start(
            "scatter", sums, [(3,) + s.shape[1:] for s in sums], carry, "rs_chips_" + tag + "_start")
        early[tag] = (list(g), send, recv, sums, land)
        return carry

    grad_x, loss_part, gbig, gsmall = _local_step(xx, loss_target[0], mod_lat, mod_ctx, W, late_weights, early_grads)

    gsmall["loss"] = loss_part
    pay3 = jnp.concatenate([gsmall[n].reshape(-1) for n, _ in SMALL])
    pay3 = jnp.pad(pay3, (0, SMALL_ROWS * 128 - pay3.shape[0])).reshape(SMALL_ROWS, 128)
    s_send, s_recv, s_src, s_land, w_in_thru = _ici_start("all", [pay3], [(8, SMALL_ROWS, 128)], gbig["w_in"],
                                                         "small_start")
    gbig = {"w_in": w_in_thru}

    after_small = early_grads("last", gbig, s_src[0])

    (pay3,), (s_land,) = _ici_wait("all", s_send, s_recv, [after_small], s_land, early["last"][3][0], "small_wait")
    got3 = lax.dynamic_update_slice(s_land, pay3[None], (dev, 0, 0)).reshape(8 * SMALL_ROWS, 128)

    def f_sum8(ids, a):
        s = a[0:SMALL_ROWS]
        for d in range(1, 8):
            s = s + a[d * SMALL_ROWS:(d + 1) * SMALL_ROWS]
        return (s,)

    (vsum,) = _ew(f_sum8, (1,), [(got3, _full((8 * SMALL_ROWS, 128)))],
                  [((SMALL_ROWS, 128), F32, _full((SMALL_ROWS, 128)), None)], "sum_small")
    vflat = vsum.reshape(-1)
    gvec, off = {}, 0
    for n, size in SMALL:
        gvec[n] = vflat[off:off + size]
        off += size
    loss = gvec["loss"][0]
    dmod_rows = got3.reshape(8, SMALL_ROWS * 128)[:, :6 * D_MODEL]
    dm16 = jnp.concatenate([dmod_rows, gvec["dmod_ctx"].reshape(1, -1), jnp.zeros((7, 6 * D_MODEL), F32)], axis=0)

    def f_colsum(ids, a):
        return (_colsum(a),)

    (g_b_ada,) = _ew(f_colsum, (1,), [(dm16, _full((16, 6 * D_MODEL)))],
                     [((1, 6 * D_MODEL), F32, _full((1, 6 * D_MODEL)), None)], "b_ada_grad")
    dm_sh = lax.dynamic_slice_in_dim(dm16, chip * 1536, 1536, axis=1)
    g_w_ada = _mm(s16, dm_sh, "tn", D_MODEL, 1536, 16, tm=512, tn=768, tk=16, name="w_ada_dw")
    dcond_part = _mm(dm_sh, w_ada[0], "nt", 16, D_MODEL, 1536, tm=16, tn=512, tk=1536, name="w_ada_dx")
    d_send, d_recv, d_src, d_land, vsum = _ici_start("all", [dcond_part[8:16]], [(8, 8, D_MODEL)], vsum, "dcond_start")

    def finish(tags, after):
        done, own, lands = [], [], []
        for tag in tags:
            tag_names, send, recv, sums, land = early[tag]
            sums, land = _ici_wait("scatter", send, recv, sums, land, after, "rs_chips_" + tag + "_wait")
            done, own, lands = done + tag_names, own + sums, lands + land
        halves = [_add_chips(a, b, place, "rs_chip_add_" + n) for a, b, n in zip(own, lands, done)]
        return dict(zip(done, _rs_pair_back(halves, "rs_pair_back_" + tags[0])))

    grads, deltas, new_m, new_v = {}, {}, {}, {}

    def adam(n, w_, m_, v_, g, transposed):
        d_, m2, v2 = _adamw(w_, g, m_, v_, "adamw_" + n)
        back = (lambda a: jnp.transpose(a)[None]) if transposed else (lambda a: a[None])
        grads[n], deltas[n], new_m[n], new_v[n] = back(g[:w_.shape[0]]), back(d_), back(m2), back(v2)

    gw = finish(["late", "mid"], grad_x)
    adam("w_ada", w_ada[0], m_w_ada[0], v_w_ada[0], g_w_ada, False)
    for n, (w_, m_, v_) in {"w_ukv": (w_ukv, m_w_ukv, v_w_ukv), "w_attn_out": (w_attn_out, m_w_attn_out, v_w_attn_out),
                            "w_conv_out": (w_conv_out, m_w_conv_out, v_w_conv_out), "w_o": (w_o, m_w_o, v_w_o),
                            "w_up": (w_up, m_w_up, v_w_up), "w_down": (w_down, m_w_down, v_w_down)}.items():
        adam(n, w_[0], m_[0], v_[0], gw[n], False)
    adam("w_uq", w_uq_t, m_w_uq_t, v_w_uq_t, gw["w_uq"], True)
    gw_in = finish(["last"], deltas["w_up"])
    adam("w_in", w_in_t, m_w_in_t, v_w_in_t, gw_in["w_in"], True)

    (dcond_mine,), (d_land,) = _ici_wait("all", d_send, d_recv, d_src, d_land, deltas["w_in"], "dcond_wait")
    got4 = lax.dynamic_update_slice(d_land, dcond_mine[None], (dev, 0, 0))[0::2, 0]

    def f_c_ctx(ids, parts, cc):
        s = _sigmoid(cc)
        d = parts[0:1] + parts[1:2] + parts[2:3] + parts[3:4]
        return (d * s * (1.0 + cc * (1.0 - s)),)

    (g_c_ctx,) = _ew(f_c_ctx, (1,), [(got4, _full((4, D_MODEL))), (c_ctx.reshape(1, D_MODEL), _full((1, D_MODEL)))],
                     [((1, D_MODEL), F32, _full((1, D_MODEL)), None)], "c_ctx_grad")

    conv_w_g = lax.dynamic_slice_in_dim(gvec["conv_w"].reshape(3, CONV_DIM), chip * 128, 128, axis=1)
    ffn_conv_w_g = lax.dynamic_slice_in_dim(gvec["ffn_conv_w"].reshape(3, 2 * D_FF), chip * 1408, 1408, axis=1)
    vec_params = (("c_ctx", c_ctx, m_c_ctx, v_c_ctx, g_c_ctx), ("b_ada", b_ada, m_b_ada, v_b_ada, g_b_ada),
                  ("norm1_g", norm1_g, m_norm1_g, v_norm1_g, gvec["norm1_g"]),
                  ("q_norm_g", q_norm_g, m_q_norm_g, v_q_norm_g, gvec["q_norm_g"]),
                  ("kv_norm_g", kv_norm_g, m_kv_norm_g, v_kv_norm_g, gvec["kv_norm_g"]),
                  ("conv_w", conv_w, m_conv_w, v_conv_w, conv_w_g), ("conv_b", conv_b, m_conv_b, v_conv_b, gvec["conv_b"]),
                  ("norm2_g", norm2_g, m_norm2_g, v_norm2_g, gvec["norm2_g"]),
                  ("ffn_conv_w", ffn_conv_w, m_ffn_conv_w, v_ffn_conv_w, ffn_conv_w_g),
                  ("ffn_conv_b", ffn_conv_b, m_ffn_conv_b, v_ffn_conv_b, gvec["ffn_conv_b"]),
                  ("final_g", final_g, m_final_g, v_final_g, gvec["final_g"]))
    two_d = lambda a: a.reshape((-1, a.shape[-1]))

    def f_adam_many(ids, *vals):
        out = []
        for k in range(len(vec_params)):
            out += _adam_update(*vals[4 * k:4 * k + 4])
        return out

    ins_v, outs_v = [], []
    for p in vec_params:
        shp = two_d(p[1]).shape
        ins_v += [(two_d(a), _full(shp)) for a in (p[1], p[4], p[2], p[3])]
        outs_v += [(shp, F32, _full(shp), None)] * 3
    res_v = _ew(f_adam_many, (1,), ins_v, outs_v, "adamw_vectors")
    for k, p in enumerate(vec_params):
        n, shape = p[0], p[1].shape
        grads[n] = p[4].reshape(shape)
        deltas[n], new_m[n], new_v[n] = (r.reshape(shape) for r in res_v[3 * k:3 * k + 3])

    order = ("c_ctx", "w_ada", "b_ada", "norm1_g", "w_in", "q_norm_g", "kv_norm_g", "w_uq", "w_ukv", "conv_w", "conv_b",
             "w_attn_out", "w_conv_out", "w_o", "norm2_g", "w_up", "ffn_conv_w", "ffn_conv_b", "w_down", "final_g")
    return (loss, grad_x[None], *[grads[n] for n in order], *[deltas[n] for n in order],
            *[new_m[n] for n in order], *[new_v[n] for n in order])
```

```python
import functools

import jax
import jax.numpy as jnp
from jax import lax
from jax.experimental import pallas as pl
from jax.experimental.pallas import tpu as pltpu

F32, BF16 = jnp.float32, jnp.bfloat16
MESH = pl.DeviceIdType.MESH

D_MODEL = 1024
N_HEADS = 8
HEAD_PAD = 128
QK_DIM = 96
Q_RANK, KV_RANK = 384, 256
CONV_DIM = 512
D_FF = 2816
GRID_W = 64
ROPE_THETA = 10000.0
EPS = 1e-6
GA0, GC0, CX0, CB0, CC0, KV0, Q0, KR0, P_COLS = 0, 1024, 2048, 2560, 3072, 3584, 3840, 4224, 4352
ROW_TILE = 256
VMEM_LIMIT_BYTES = 48 * 1024 * 1024

ADAM_LR, ADAM_B1, ADAM_B2, ADAM_EPS, ADAM_WD, ADAM_STEP = 0.001, 0.9, 0.999, 1e-08, 0.01, 10

BIG = (("w_in", (1088, 1024)), ("w_uq", (192, 384)), ("w_ukv", (256, 256)), ("w_attn_out", (512, 256)),
       ("w_conv_out", (512, 256)), ("w_o", (256, 1024)), ("w_up", (1024, 1408)), ("w_down", (704, 1024)))

GATHER_LATE = ("w_attn_out", "w_conv_out", "w_o", "w_up", "w_down")

NN = (((1,), (0,)), ((), ()))
NT = (((1,), (1,)), ((), ()))
TN = (((0,), (0,)), ((), ()))


def _cp(sem):
    return pltpu.CompilerParams(dimension_semantics=sem, vmem_limit_bytes=VMEM_LIMIT_BYTES)


PIN_BYTES = 1 << 19


def _in_hbm(arrays):
    return [pltpu.with_memory_space_constraint(a, pltpu.HBM) if a.size * a.dtype.itemsize >= PIN_BYTES else a
            for a in arrays]


def _out(shape, dtype):
    n = 1
    for d in shape:
        n *= d
    big = n * jnp.dtype(dtype).itemsize >= PIN_BYTES
    return pltpu.HBM(shape, dtype) if big else jax.ShapeDtypeStruct(shape, dtype)


def _pick(n, prefs):
    for p in prefs:
        if n % p == 0:
            return p
    return n


def _mm(a, b, mode, M, N, K, *, tm, tn, tk, name, out_dtype=F32, a_spec=None, b_spec=None, o_spec=None,
        out_shape=None, transpose_out=False):
    assert M % tm == 0 and N % tn == 0 and K % tk == 0, (name, M, N, K, tm, tn, tk)
    nk = K // tk
    dims = {"nn": NN, "nt": NT, "tn": TN}[mode]
    if a_spec is None:
        a_spec = (pl.BlockSpec((tk, tm), lambda i, j, k: (k, i)) if mode == "tn"
                  else pl.BlockSpec((tm, tk), lambda i, j, k: (i, k)))
    if b_spec is None:
        b_spec = (pl.BlockSpec((tn, tk), lambda i, j, k: (j, k)) if mode == "nt"
                  else pl.BlockSpec((tk, tn), lambda i, j, k: (k, j)))
    if o_spec is None:
        o_spec = (pl.BlockSpec((tn, tm), lambda i, j, k: (j, i)) if transpose_out
                  else pl.BlockSpec((tm, tn), lambda i, j, k: (i, j)))
    if out_shape is None:
        out_shape = (N, M) if transpose_out else (M, N)

    def emit(o_ref, val):
        o_ref[...] = (val.T if transpose_out else val).astype(o_ref.dtype)

    def body(a_ref, b_ref, o_ref, *scratch):
        part = lax.dot_general(a_ref[...].astype(BF16), b_ref[...].astype(BF16), dims, preferred_element_type=F32)
        if nk == 1:
            emit(o_ref, part)
            return
        acc_ref, = scratch
        k = pl.program_id(2)

        @pl.when(k == 0)
        def _():
            acc_ref[...] = part

        @pl.when((k > 0) & (k < nk - 1))
        def _():
            acc_ref[...] += part

        @pl.when(k == nk - 1)
        def _():
            emit(o_ref, acc_ref[...] + part)

    return pl.pallas_call(
        body, grid=(M // tm, N // tn, nk), in_specs=[a_spec, b_spec], out_specs=o_spec,
        out_shape=_out(out_shape, out_dtype),
        scratch_shapes=[pltpu.VMEM((tm, tn), F32)] if nk > 1 else [],
        compiler_params=_cp(("parallel", "parallel", "arbitrary")), name=name)(*_in_hbm([a, b]))


def _ew(fn, grid, ins, outs, name, scalars=None):
    n_in = len(ins)
    n_sc = 0 if scalars is None else 1

    def store(ref, val, acc, ids):
        if isinstance(val, (list, tuple)):
            for h, v in enumerate(val):
                ref[h] = v.astype(ref.dtype)
            return
        if acc is None:
            ref[...] = val.astype(ref.dtype)
            return

        @pl.when(ids[acc] == 0)
        def _():
            ref[...] = val.astype(ref.dtype)

        @pl.when(ids[acc] > 0)
        def _():
            ref[...] += val.astype(ref.dtype)

    def body(*refs):
        refs = refs[n_sc:]
        ids = tuple(pl.program_id(a) for a in range(len(grid)))
        vals = fn(ids, *[r[...] for r in refs[:n_in]])
        for ref, val, (_, _, _, acc) in zip(refs[n_in:], vals, outs):
            store(ref, val, acc, ids)

    acc_axes = {o[3] for o in outs if o[3] is not None}
    sem = tuple("arbitrary" if a in acc_axes else "parallel" for a in range(len(grid)))
    in_specs, out_specs = [s for _, s in ins], [o[2] for o in outs]
    out_shape = [_out(o[0], o[1]) for o in outs]
    args = _in_hbm([a for a, _ in ins])
    if scalars is None:
        return pl.pallas_call(body, grid=grid, in_specs=in_specs, out_specs=out_specs, out_shape=out_shape,
                              compiler_params=_cp(sem), name=name)(*args)
    spec = pltpu.PrefetchScalarGridSpec(num_scalar_prefetch=1, grid=grid, in_specs=in_specs, out_specs=out_specs)
    return pl.pallas_call(body, grid_spec=spec, out_shape=out_shape, compiler_params=_cp(sem), name=name)(scalars, *args)


def _rows(width, cblk=0, roff=0, tr=ROW_TILE):
    return pl.BlockSpec((tr, width), lambda i: (i + roff, cblk))


def _full(shape):
    nd = len(shape)
    return pl.BlockSpec(shape, lambda *_: (0,) * nd)


def _sigmoid(x):
    return 1.0 / (1.0 + jnp.exp(-x))


def _rms(x):
    return lax.rsqrt(jnp.mean(x * x, axis=-1, keepdims=True) + EPS)


def _rms_bwd(dn, xn, r):
    return r * (dn - xn * jnp.mean(dn * xn, axis=-1, keepdims=True))


def _colsum(x):
    return jnp.sum(x, axis=0, keepdims=True)


def _shifts(x):
    n = x.shape[0]
    rows = lax.broadcasted_iota(jnp.int32, x.shape, 0)
    return jnp.where(rows == 0, 0.0, pltpu.roll(x, 1, 0)), jnp.where(rows == n - 1, 0.0, pltpu.roll(x, n - 1, 0))


def _conv(x, w, b, shifted=None):
    prev, nxt = _shifts(x) if shifted is None else shifted
    return b + prev * w[0:1] + x * w[1:2] + nxt * w[2:3]


def _conv_bwd_x(dy, w):
    prev, nxt = _shifts(dy)
    return nxt * w[0:1] + dy * w[1:2] + prev * w[2:3]


def _conv_bwd_w(dy, x, shifted):
    prev, nxt = shifted
    return _colsum(dy * prev), _colsum(dy * x), _colsum(dy * nxt)


def _rope(x, cos, sin_lo, sin_hi):
    return x * cos + pltpu.roll(x, HEAD_PAD - 8, 1) * sin_lo + pltpu.roll(x, 8, 1) * sin_hi


ATTN_SCALE = QK_DIM ** -0.5
LOG2_E = 1.4426950408889634


def _head_keys(kv_ref, kr_ref, cos_ref, slo_ref, shi_ref, kc_ref, vp_ref):
    kv = kv_ref[...]
    lane = lax.broadcasted_iota(jnp.int32, kv.shape, 1)
    kc_ref[...] = jnp.where(lane < 64, kv, _rope(kr_ref[...], cos_ref[...], slo_ref[...], shi_ref[...])).astype(BF16)
    vp_ref[...] = jnp.where(lane >= 64, kv, 0.0).astype(BF16)


ATTN_Q_TILE = 512


def _attn_specs(tq, TT):
    q = pl.BlockSpec((tq, HEAD_PAD), lambda h, i: (i, h))
    keys = pl.BlockSpec((TT, HEAD_PAD), lambda h, i: (0, h))
    kr = pl.BlockSpec((TT, HEAD_PAD), lambda h, i: (0, KR0 // HEAD_PAD))
    tab_q = pl.BlockSpec((tq, HEAD_PAD), lambda h, i: (i, 0))
    tab_k = pl.BlockSpec((TT, HEAD_PAD), lambda h, i: (0, 0))
    lse = pl.BlockSpec((None, tq, 1), lambda h, i: (h, i, 0))
    return q, keys, kr, tab_q, tab_k, lse


def _attn_fwd(q_raw, kv, pp, tabs, T, TT):
    tq = ROW_TILE
    cos, slo, shi = tabs

    def body(q_ref, kv_ref, kr_ref, cq, lq, hq, ck, lk, hk, o_ref, l_ref, kc, vp):
        @pl.when(pl.program_id(1) == 0)
        def _():
            _head_keys(kv_ref, kr_ref, ck, lk, hk, kc, vp)

        q = _rope(q_ref[...], cq[...], lq[...], hq[...]).astype(BF16)
        s = lax.dot_general(q, kc[...], NT, preferred_element_type=F32)
        m = jnp.max(s, axis=-1, keepdims=True)
        p = jnp.exp2((s - m) * (ATTN_SCALE * LOG2_E))
        l = jnp.sum(p, axis=-1, keepdims=True)
        o = lax.dot_general(p.astype(BF16), vp[...], NN, preferred_element_type=F32)
        o_ref[...] = o / l
        l_ref[...] = m * ATTN_SCALE + jnp.log(l)

    qs, keys, kr, tab_q, tab_k, lse = _attn_specs(tq, TT)
    return pl.pallas_call(
        body, grid=(N_HEADS, T // tq), in_specs=[qs, keys, kr, tab_q, tab_q, tab_q, tab_k, tab_k, tab_k],
        out_specs=[qs, lse],
        out_shape=[jax.ShapeDtypeStruct((T, N_HEADS * HEAD_PAD), F32), jax.ShapeDtypeStruct((N_HEADS, T, 1), F32)],
        scratch_shapes=[pltpu.VMEM((TT, HEAD_PAD), BF16), pltpu.VMEM((TT, HEAD_PAD), BF16)],
        compiler_params=_cp(("parallel", "arbitrary")), name="attn_fwd",
    )(*_in_hbm([q_raw, kv, pp, cos, slo, shi, cos, slo, shi]))


def _attn_bwd(q_raw, kv, pp, o, do, lse, tabs, tabs_inv, T, TT):
    tq = _pick(T, (ATTN_Q_TILE, ROW_TILE))
    nq = T // tq
    cos, slo, shi = tabs
    cos_i, slo_i, shi_i = tabs_inv

    def body(q_ref, kv_ref, kr_ref, cq, lq, hq, ck, lk, hk, iq, ilq, ihq, ik, ilk, ihk, o_ref, do_ref, l_ref,
             dq_ref, dkv_ref, dkr_ref, kc, vp, dk, dv):
        h, i = pl.program_id(0), pl.program_id(1)

        @pl.when(i == 0)
        def _():
            _head_keys(kv_ref, kr_ref, ck, lk, hk, kc, vp)
            dk[...] = jnp.zeros_like(dk)
            dv[...] = jnp.zeros_like(dv)

        q = _rope(q_ref[...], cq[...], lq[...], hq[...]).astype(BF16)
        k, v, d_o = kc[...], vp[...], do_ref[...]
        s = lax.dot_general(q, k, NT, preferred_element_type=F32)
        p = jnp.exp2(s * (ATTN_SCALE * LOG2_E) - l_ref[...] * LOG2_E)
        dob = d_o.astype(BF16)
        dp = lax.dot_general(dob, v, NT, preferred_element_type=F32)
        dd = jnp.sum(d_o * o_ref[...], axis=-1, keepdims=True)
        ds = (p * (dp - dd) * ATTN_SCALE).astype(BF16)
        dq = lax.dot_general(ds, k, NN, preferred_element_type=F32)
        dq_ref[...] = _rope(dq, iq[...], ilq[...], ihq[...]).astype(dq_ref.dtype)
        dk[...] += lax.dot_general(ds, q, TN, preferred_element_type=F32)
        dv[...] += lax.dot_general(p.astype(BF16), dob, TN, preferred_element_type=F32)

        @pl.when(i == nq - 1)
        def _():
            dkh = dk[...]
            lane = lax.broadcasted_iota(jnp.int32, dkh.shape, 1)
            dkv_ref[...] = jnp.where(lane < 64, dkh, dv[...]).astype(dkv_ref.dtype)
            rot = _rope(jnp.where((lane >= 64) & (lane < 96), dkh, 0.0), ik[...], ilk[...], ihk[...])

            @pl.when(h == 0)
            def _():
                dkr_ref[...] = rot

            @pl.when(h > 0)
            def _():
                dkr_ref[...] += rot

    qs, keys, kr, tab_q, tab_k, lse_spec = _attn_specs(tq, TT)
    wide = lambda rows: jax.ShapeDtypeStruct((rows, N_HEADS * HEAD_PAD), BF16)
    return pl.pallas_call(
        body, grid=(N_HEADS, nq),
        in_specs=[qs, keys, kr] + [tab_q] * 3 + [tab_k] * 3 + [tab_q] * 3 + [tab_k] * 3 + [qs, qs, lse_spec],
        out_specs=[qs, keys, pl.BlockSpec((TT, HEAD_PAD), lambda h, i: (0, 0))],
        out_shape=[wide(T), wide(TT), jax.ShapeDtypeStruct((TT, HEAD_PAD), F32)],
        scratch_shapes=[pltpu.VMEM((TT, HEAD_PAD), BF16), pltpu.VMEM((TT, HEAD_PAD), BF16),
                        pltpu.VMEM((TT, HEAD_PAD), F32), pltpu.VMEM((TT, HEAD_PAD), F32)],
        compiler_params=_cp(("arbitrary", "arbitrary")), name="attn_bwd",
    )(*_in_hbm([q_raw, kv, pp, cos, slo, shi, cos, slo, shi, cos_i, slo_i, shi_i, cos_i, slo_i, shi_i, o, do, lse]))


def _hbm_specs(n):
    return [pl.BlockSpec(memory_space=pl.ANY)] * n


def _gather_weights(shards):
    n = len(shards)
    halves = [s.shape[0] // 2 for s in shards]

    def body(*refs):
        ins, outs = refs[:n], refs[n:2 * n]
        token, send_sems, recv_sems = refs[2 * n:]
        token[...] = jnp.zeros_like(token)
        mx, my, mc = lax.axis_index("x"), lax.axis_index("y"), lax.axis_index("c")
        j_me = 2 * mx + my
        chips = [(1 - mx, my), (mx, 1 - my), (1 - mx, 1 - my)]

        def half(w, chip_idx, hc):
            return outs[w].at[chip_idx, pl.ds(hc * halves[w], halves[w]), :]

        def copy(w, k, src, dst, to):
            return pltpu.make_async_remote_copy(src_ref=src, dst_ref=dst, send_sem=send_sems.at[w, k],
                                                recv_sem=recv_sems.at[w, k], device_id=to, device_id_type=MESH)

        sends = []
        for w in range(n):
            cp = copy(w, 6, ins[w], outs[w].at[j_me], (mx, my, 1 - mc))
            cp.start()
            sends.append(cp)
        for k, (px, py) in enumerate(chips):
            for w in range(n):
                cp = copy(w, k, ins[w].at[pl.ds(mc * halves[w], halves[w]), :], half(w, j_me, mc), (px, py, mc))
                cp.start()
                sends.append(cp)
        for k, (px, py) in enumerate(chips):
            for w in range(n):
                got = half(w, 2 * px + py, mc)
                copy(w, k, got, got, (px, py, mc)).wait_recv()
                cp = copy(w, 3 + k, got, got, (mx, my, 1 - mc))
                cp.start()
                sends.append(cp)
        for k, (px, py) in enumerate(chips):
            for w in range(n):
                got = half(w, 2 * px + py, 1 - mc)
                copy(w, 3 + k, got, got, (mx, my, 1 - mc)).wait_recv()
        for w in range(n):
            own = outs[w].at[j_me]
            copy(w, 6, own, own, (mx, my, 1 - mc)).wait_recv()
        for cp in sends:
            cp.wait_send()

    res = pl.pallas_call(
        body, out_shape=[jax.ShapeDtypeStruct((4,) + s.shape, s.dtype) for s in shards]
        + [jax.ShapeDtypeStruct((8, 128), F32)],
        in_specs=_hbm_specs(n), out_specs=_hbm_specs(n) + [pl.BlockSpec(memory_space=pltpu.VMEM)],
        scratch_shapes=[pltpu.SemaphoreType.DMA((n, 7)), pltpu.SemaphoreType.DMA((n, 7))],
        name="gather_weights")(*shards)
    return list(res[:n]), res[n]


def _rs_pair(gs, name):
    n = len(gs)
    halves = [g.shape[1] // 2 for g in gs]

    def body(*refs):
        ins, lands = refs[:n], refs[n:2 * n]
        send_sems, recv_sems = refs[2 * n:]
        mx, my, mc = lax.axis_index("x"), lax.axis_index("y"), lax.axis_index("c")
        copies = []
        for w in range(n):
            h = halves[w]
            cp = pltpu.make_async_remote_copy(
                src_ref=ins[w].at[:, pl.ds((1 - mc) * h, h), :], dst_ref=lands[w], send_sem=send_sems.at[w],
                recv_sem=recv_sems.at[w], device_id=(mx, my, 1 - mc), device_id_type=MESH)
            cp.start()
            copies.append(cp)
        for cp in copies:
            cp.wait()

    return pl.pallas_call(
        body, out_shape=[jax.ShapeDtypeStruct((4, h, g.shape[2]), g.dtype) for g, h in zip(gs, halves)],
        in_specs=_hbm_specs(n), out_specs=_hbm_specs(n),
        scratch_shapes=[pltpu.SemaphoreType.DMA((n,)), pltpu.SemaphoreType.DMA((n,))], name=name)(*gs)


def _rs_chips(parts):
    n = len(parts)

    def body(*refs):
        ins, lands = refs[:n], refs[n:2 * n]
        send_sems, recv_sems = refs[2 * n:]
        mx, my, mc = lax.axis_index("x"), lax.axis_index("y"), lax.axis_index("c")
        copies = []
        for k, (px, py) in enumerate([(1 - mx, my), (mx, 1 - my), (1 - mx, 1 - my)]):
            for w in range(n):
                cp = pltpu.make_async_remote_copy(
                    src_ref=ins[w].at[2 * px + py], dst_ref=lands[w].at[k], send_sem=send_sems.at[w, k],
                    recv_sem=recv_sems.at[w, k], device_id=(px, py, mc), device_id_type=MESH)
                cp.start()
                copies.append(cp)
        for cp in copies:
            cp.wait()

    return list(pl.pallas_call(
        body, out_shape=[jax.ShapeDtypeStruct((3,) + p.shape[1:], p.dtype) for p in parts],
        in_specs=_hbm_specs(n), out_specs=_hbm_specs(n),
        scratch_shapes=[pltpu.SemaphoreType.DMA((n, 3)), pltpu.SemaphoreType.DMA((n, 3))], name="rs_chips")(*parts))


def _rs_pair_back(gs, name):
    n = len(gs)

    def body(*refs):
        outs = refs[n:2 * n]
        send_sems, recv_sems = refs[2 * n:]
        mx, my, mc = lax.axis_index("x"), lax.axis_index("y"), lax.axis_index("c")
        copies = []
        for w in range(n):
            h = gs[w].shape[0] // 2
            mine = outs[w].at[pl.ds(mc * h, h), :]
            cp = pltpu.make_async_remote_copy(src_ref=mine, dst_ref=mine, send_sem=send_sems.at[w],
                                              recv_sem=recv_sems.at[w], device_id=(mx, my, 1 - mc), device_id_type=MESH)
            cp.start()
            copies.append(cp)
        for cp in copies:
            cp.wait()

    return pl.pallas_call(
        body, out_shape=[jax.ShapeDtypeStruct(g.shape, g.dtype) for g in gs],
        in_specs=_hbm_specs(n), out_specs=_hbm_specs(n), input_output_aliases={w: w for w in range(n)},
        scratch_shapes=[pltpu.SemaphoreType.DMA((n,)), pltpu.SemaphoreType.DMA((n,))], name=name)(*gs)


_HBM = pl.BlockSpec(memory_space=pltpu.HBM)
_SEM = pl.BlockSpec(memory_space=pltpu.SEMAPHORE)
_EFFECT = pltpu.SideEffectType.DATAFLOW_SIDE_EFFECTING


def _ici_copies(kind, srcs, lands, send_sems, recv_sems):
    n = len(srcs)
    mx, my, mc = lax.axis_index("x"), lax.axis_index("y"), lax.axis_index("c")
    j_me = 2 * mx + my
    copies = []
    if kind == "all":
        for k in range(7):
            a, b, c = (k + 1) >> 2 & 1, (k + 1) >> 1 & 1, (k + 1) & 1
            peer = (1 - mx if a else mx, 1 - my if b else my, 1 - mc if c else mc)
            for w in range(n):
                copies.append(pltpu.make_async_remote_copy(
                    src_ref=srcs[w], dst_ref=lands[w].at[4 * mx + 2 * my + mc], send_sem=send_sems.at[7 * w + k],
                    recv_sem=recv_sems.at[7 * w + k], device_id=peer, device_id_type=MESH))
        return copies
    if kind == "pair":
        for w in range(n):
            h = srcs[w].shape[1] // 2
            copies.append(pltpu.make_async_remote_copy(
                src_ref=srcs[w].at[:, pl.ds((1 - mc) * h, h), :], dst_ref=lands[w], send_sem=send_sems.at[w],
                recv_sem=recv_sems.at[w], device_id=(mx, my, 1 - mc), device_id_type=MESH))
        return copies
    for k, (px, py) in enumerate([(1 - mx, my), (mx, 1 - my), (1 - mx, 1 - my)]):
        for w in range(n):
            if kind == "gather":
                h = srcs[w].shape[0] // 2
                src, dst = srcs[w].at[pl.ds(mc * h, h), :], lands[w].at[j_me, pl.ds(mc * h, h), :]
            else:
                src, dst = srcs[w].at[2 * px + py], lands[w].at[k]
            copies.append(pltpu.make_async_remote_copy(
                src_ref=src, dst_ref=dst, send_sem=send_sems.at[3 * w + k], recv_sem=recv_sems.at[3 * w + k],
                device_id=(px, py, mc), device_id_type=MESH))
    return copies


_SEMS_PER_OPERAND = {"gather": 3, "scatter": 3, "all": 7, "pair": 1}


def _ici_start(kind, srcs, land_shapes, carry, name):
    n = len(srcs)

    def body(*refs):
        ins, lands = refs[:n], refs[n:2 * n]
        send_sems, recv_sems = refs[2 * n + 1], refs[2 * n + 2]
        for cp in _ici_copies(kind, ins, lands, send_sems, recv_sems):
            cp.start()

    hbm = lambda a: pltpu.with_memory_space_constraint(a, pltpu.HBM)
    lands = [lax.empty(s, srcs[0].dtype) for s in land_shapes]
    args = [hbm(a) for a in list(srcs) + lands + [carry]]
    n_sem = _SEMS_PER_OPERAND[kind] * n
    out_shape = ([pltpu.SemaphoreType.DMA((n_sem,)), pltpu.SemaphoreType.DMA((n_sem,))]
                 + [pltpu.HBM(a.shape, a.dtype) for a in args])
    res = pl.pallas_call(
        body, name=name, out_shape=out_shape, in_specs=[_HBM] * len(args), out_specs=[_SEM, _SEM] + [_HBM] * len(args),
        input_output_aliases={i: 2 + i for i in range(len(args))},
        compiler_params=pltpu.CompilerParams(has_side_effects=_EFFECT))(*args)
    return res[0], res[1], list(res[2:2 + n]), list(res[2 + n:2 + 2 * n]), res[2 + 2 * n]


def _ici_wait(kind, send_sems, recv_sems, srcs, lands, after, name):
    n = len(srcs)

    def body(*refs):
        ins, zones = refs[:n], refs[n:2 * n]
        for cp in _ici_copies(kind, ins, zones, refs[2 * n], refs[2 * n + 1]):
            cp.wait_send()
            cp.wait_recv()

    args = list(srcs) + list(lands)
    res = pl.pallas_call(
        body, name=name, out_shape=[pltpu.HBM(a.shape, a.dtype) for a in args],
        in_specs=[_HBM] * len(args) + [_SEM, _SEM, pl.BlockSpec(memory_space=pl.ANY)], out_specs=[_HBM] * len(args),
        input_output_aliases={i: i for i in range(len(args))},
        compiler_params=pltpu.CompilerParams(has_side_effects=_EFFECT))(*args, send_sems, recv_sems, after)
    return list(res[:n]), list(res[n:])


def _gather_finish(shards, lands):
    n = len(shards)

    def body(*refs):
        own, outs = refs[:n], refs[2 * n:3 * n]
        send_sems, recv_sems = refs[3 * n:]
        mx, my, mc = lax.axis_index("x"), lax.axis_index("y"), lax.axis_index("c")
        j_me = 2 * mx + my
        sibling = (mx, my, 1 - mc)
        copies = []

        def push(w, k, src, dst):
            cp = pltpu.make_async_remote_copy(src_ref=src, dst_ref=dst, send_sem=send_sems.at[w, k],
                                              recv_sem=recv_sems.at[w, k], device_id=sibling, device_id_type=MESH)
            cp.start()
            copies.append(cp)

        for w in range(n):
            h = shards[w].shape[0] // 2
            push(w, 3, own[w], outs[w].at[j_me])
            for k, (px, py) in enumerate([(1 - mx, my), (mx, 1 - my), (1 - mx, 1 - my)]):
                got = outs[w].at[2 * px + py, pl.ds(mc * h, h), :]
                push(w, k, got, got)
        for cp in copies:
            cp.wait()

    return pl.pallas_call(
        body, out_shape=[jax.ShapeDtypeStruct(l.shape, l.dtype) for l in lands],
        in_specs=_hbm_specs(2 * n), out_specs=_hbm_specs(n), input_output_aliases={n + w: w for w in range(n)},
        scratch_shapes=[pltpu.SemaphoreType.DMA((n, 4)), pltpu.SemaphoreType.DMA((n, 4))], name="gather_finish",
    )(*shards, *lands)


def _tile_rows(h, c, itemsize, mult):
    best = h
    for t in range(mult, h + 1, mult):
        if h % t == 0 and t * c * itemsize <= (1 << 21):
            best = t
    return best


def _add_pair(g, land, place, name):
    _, h, c = land.shape
    t = _tile_rows(h, c, 2, 16)
    nb = h // t
    return _ew(lambda ids, u, v: (u.astype(F32) + v.astype(F32),), (4, nb),
               [(g, pl.BlockSpec((None, t, c), lambda j, i, s: (j, s[1] * nb + i, 0))),
                (land, pl.BlockSpec((None, t, c), lambda j, i, s: (j, i, 0)))],
               [(land.shape, BF16, pl.BlockSpec((None, t, c), lambda j, i, s: (j, i, 0)), None)], name, scalars=place)[0]


def _add_chips(own, land, place, name):
    _, h, c = land.shape
    t = _tile_rows(h, c, 4, 16)
    nb = h // t

    def fn(ids, a, b):
        return (((a.astype(F32) + b[0].astype(F32)) + b[1].astype(F32)) + b[2].astype(F32),)

    return _ew(fn, (nb,), [(own, pl.BlockSpec((None, t, c), lambda i, s: (s[0], i, 0))),
                           (land, pl.BlockSpec((3, t, c), lambda i, s: (0, i, 0)))],
               [((2 * h, c), F32, pl.BlockSpec((t, c), lambda i, s: (s[1] * nb + i, 0)), None)], name, scalars=place)[0]


W_IN_SEGMENTS = ((0, 256, KV0), (256, 288, KR0 + 64), (288, 672, Q0), (672, 1184, CX0), (1184, 1696, CB0),
                 (1696, 2208, CC0), (2208, 3232, GA0), (3232, 4256, GC0))
W_IN_SHARD = 1064


W_IN_SHARD_PAD = 1088


def _w_in_t_p_from_shards(s):
    pieces = []
    for o0, o1, p0 in sorted(W_IN_SEGMENTS, key=lambda t: t[2]):
        if p0 == KR0 + 64:
            pieces.append(jnp.zeros((64, s.shape[2]), s.dtype))
        for j in range(4):
            lo, hi = max(o0, j * W_IN_SHARD), min(o1, (j + 1) * W_IN_SHARD)
            if lo < hi:
                pieces.append(s[j, lo - j * W_IN_SHARD:hi - j * W_IN_SHARD])
    pieces.append(jnp.zeros((32, s.shape[2]), s.dtype))
    return jnp.concatenate(pieces, axis=0)


def _w_in_t_shards_from_p(g):
    shards = []
    for j in range(4):
        pieces = []
        for o0, o1, p0 in W_IN_SEGMENTS:
            lo, hi = max(o0, j * W_IN_SHARD), min(o1, (j + 1) * W_IN_SHARD)
            if lo < hi:
                pieces.append(g[p0 + lo - o0:p0 + hi - o0])
        pieces.append(jnp.zeros((W_IN_SHARD_PAD - W_IN_SHARD, g.shape[1]), g.dtype))
        shards.append(jnp.concatenate(pieces, axis=0))
    return jnp.stack(shards, axis=0)


def _cols_from_shards(s):
    return jnp.transpose(s, (1, 0, 2)).reshape(s.shape[1], -1)


def _rope_tables(T, TT, inverse):
    rows = T // GRID_W
    row = jnp.repeat(jnp.arange(rows), GRID_W).astype(F32)
    col = jnp.tile(jnp.arange(GRID_W), rows).astype(F32)
    inv = ROPE_THETA ** (-jnp.arange(0, 16, 2, dtype=F32) / 16)
    ang = jnp.concatenate([row[:, None] * inv, col[:, None] * inv], axis=-1)
    cos, sin = jnp.cos(ang), jnp.sin(ang)
    lane = jnp.arange(32)
    src = (lane // 16) * 8 + lane % 8
    lo = ((lane % 16) // 8 == 0).astype(F32)
    sgn = -1.0 if inverse else 1.0
    cos32 = cos[:, src]
    sin_lo32 = -sgn * sin[:, src] * lo
    sin_hi32 = sgn * sin[:, src] * (1.0 - lo)

    def widen(t32, fill):
        t = jnp.concatenate([jnp.full((T, 64), fill, F32), t32, jnp.full((T, 32), fill, F32)], axis=1)
        return jnp.concatenate([t, jnp.full((TT - T, HEAD_PAD), fill, F32)], axis=0)

    return widen(cos32, 1.0), widen(sin_lo32, 0.0), widen(sin_hi32, 0.0)


def _local_step(xx, tgt, mod_lat, mod_ctx, W, late_weights, early_grads, early_continue):
    TT = xx.shape[0]
    T = tgt.shape[0]
    n_lat, n_all = T // ROW_TILE, TT // ROW_TILE
    sh1, sc1, g1, sh2, sc2, g2 = [mod_lat[:, k * D_MODEL:(k + 1) * D_MODEL] for k in range(6)]
    csh1, csc1 = mod_ctx[:, :D_MODEL], mod_ctx[:, D_MODEL:2 * D_MODEL]
    vec = lambda n: _full((1, n))
    row_out = lambda n, dt, rows=T: ((rows, n), dt, _rows(n), None)
    acc_out = lambda n: ((1, n), F32, _full((1, n)), 0)

    def f_norm1(ids, x, g, a_sh, a_sc, b_sh, b_sc):
        ctx = ids[0] >= n_lat
        sh, sc = jnp.where(ctx, b_sh, a_sh), jnp.where(ctx, b_sc, a_sc)
        return ((x * _rms(x) * g) * (1.0 + sc) + sh,)

    (hh,) = _ew(f_norm1, (n_all,), [(xx, _rows(D_MODEL)), (W["norm1_g"], vec(D_MODEL)), (sh1, vec(D_MODEL)),
                                   (sc1, vec(D_MODEL)), (csh1, vec(D_MODEL)), (csc1, vec(D_MODEL))],
                [row_out(D_MODEL, BF16, TT)], "norm1_fwd")
    tm_all = _pick(TT, (768, 256))
    pp = _mm(hh, W["w_in_t"], "nt", TT, P_COLS, D_MODEL, tm=tm_all, tn=2176, tk=D_MODEL, name="w_in_fwd")

    def f_lowrank(ids, ckv, cq, gkv, gq):
        return ckv * _rms(ckv) * gkv, cq * _rms(cq) * gq

    nkv, nq = _ew(f_lowrank, (n_all,), [(pp, _rows(KV_RANK, KV0 // KV_RANK)), (pp, _rows(Q_RANK, Q0 // Q_RANK)),
                                       (W["kv_norm_g"], vec(KV_RANK)), (W["q_norm_g"], vec(Q_RANK))],
                  [row_out(KV_RANK, BF16, TT), row_out(Q_RANK, BF16, TT)], "lowrank_norm_fwd")
    kv = _mm(nkv, W["w_ukv"], "nn", TT, 1024, KV_RANK, tm=tm_all, tn=256, tk=KV_RANK, name="w_ukv_fwd",
             b_spec=pl.BlockSpec((None, KV_RANK, 256), lambda i, j, k: (j, k, 0)))
    q_raw = _mm(nq, W["w_uq_t"], "nt", TT, 1024, Q_RANK, tm=tm_all, tn=1024, tk=Q_RANK, name="w_uq_fwd")

    tabs = _rope_tables(T, TT, inverse=False)
    tabs_inv = _rope_tables(T, TT, inverse=True)
    o_pad, lse = _attn_fwd(q_raw, kv, pp, tabs, T, TT)
    W = dict(W, **late_weights(o_pad))
    tm_lat = _pick(T, (1024, 512, 256))
    ya = _mm(o_pad, W["w_attn_out"], "nn", T, D_MODEL, 1024, tm=tm_lat, tn=D_MODEL, tk=1024, name="w_attn_out_fwd")

    tc = 256
    colT = lambda blk0: pl.BlockSpec((T, tc), lambda j: (0, blk0 + j))

    def f_conv(ids, xin, cb, cc, w, b):
        return (cb * _conv(cc * xin, w, b),)

    (e,) = _ew(f_conv, (CONV_DIM // tc,),
               [(pp, colT(CX0 // tc)), (pp, colT(CB0 // tc)), (pp, colT(CC0 // tc)),
                (W["conv_w"], pl.BlockSpec((3, tc), lambda j: (0, j))), (W["conv_b"], pl.BlockSpec((1, tc), lambda j: (0, j)))],
               [((T, CONV_DIM), BF16, colT(0), None)], "conv_fwd")
    yc = _mm(e, W["w_conv_out"], "nn", T, D_MODEL, CONV_DIM, tm=tm_lat, tn=256, tk=CONV_DIM, name="w_conv_out_fwd",
             b_spec=pl.BlockSpec((None, CONV_DIM, 256), lambda i, j, k: (j, k, 0)))

    def f_merge(ids, ga, gc, a, c):
        return (_sigmoid(ga) * a + _sigmoid(gc) * c,)

    (mrg,) = _ew(f_merge, (n_lat,), [(pp, _rows(D_MODEL, 0)), (pp, _rows(D_MODEL, 1)), (ya, _rows(D_MODEL)),
                                    (yc, _rows(D_MODEL))], [row_out(D_MODEL, BF16)], "merge_fwd")
    mo = _mm(mrg, W["w_o"], "nn", T, D_MODEL, D_MODEL, tm=tm_lat, tn=D_MODEL, tk=D_MODEL, name="w_o_fwd")

    def f_norm2(ids, x, m, gate, g, sh, sc):
        x1 = x + gate * m
        return x1, (x1 * _rms(x1) * g) * (1.0 + sc) + sh

    x1, h2 = _ew(f_norm2, (n_lat,), [(xx, _rows(D_MODEL)), (mo, _rows(D_MODEL)), (g1, vec(D_MODEL)),
                                    (W["norm2_g"], vec(D_MODEL)), (sh2, vec(D_MODEL)), (sc2, vec(D_MODEL))],
                 [row_out(D_MODEL, F32), row_out(D_MODEL, BF16)], "norm2_fwd")
    up = _mm(h2, W["w_up"], "nn", T, 2 * D_FF, D_MODEL, tm=tm_lat, tn=1408, tk=D_MODEL, name="w_up_fwd",
             b_spec=pl.BlockSpec((None, D_MODEL, 1408), lambda i, j, k: (j, k, 0)))

    n_ff = D_FF // tc
    ffw = lambda off, n=3: pl.BlockSpec((n, tc), lambda j: (0, j + off))

    def f_ffn(ids, ug, uv, wg, wv, bg, bv):
        gate, val = _conv(ug, wg, bg), _conv(uv, wv, bv)
        return (gate * _sigmoid(gate) * val,)

    (act,) = _ew(f_ffn, (n_ff,), [(up, colT(0)), (up, colT(n_ff)), (W["ffn_conv_w"], ffw(0)), (W["ffn_conv_w"], ffw(n_ff)),
                                 (W["ffn_conv_b"], ffw(0, 1)), (W["ffn_conv_b"], ffw(n_ff, 1))],
                 [((T, D_FF), BF16, colT(0), None)], "ffn_act_fwd")
    f = _mm(act, W["w_down"], "nn", T, D_MODEL, D_FF, tm=tm_lat, tn=D_MODEL, tk=D_FF, name="w_down_fwd")

    def f_head(ids, x1_, f_, gate, gf, t):
        x2 = x1_ + gate * f_
        r = _rms(x2)
        xn = x2 * r
        err = xn * gf - t
        loss = 0.5 * jnp.sum(jnp.mean(err * err, axis=-1, keepdims=True))
        dy = err * (1.0 / D_MODEL)
        dx2 = _rms_bwd(dy * gf, xn, r)
        return dx2, dx2 * gate, _colsum(dy * xn), _colsum(dx2 * f_), jnp.full((1, 128), loss, F32)

    dx2, df, dg_f, dg2, loss = _ew(
        f_head, (n_lat,), [(x1, _rows(D_MODEL)), (f, _rows(D_MODEL)), (g2, vec(D_MODEL)), (W["final_g"], vec(D_MODEL)),
                           (tgt, _rows(D_MODEL))],
        [row_out(D_MODEL, F32), row_out(D_MODEL, BF16), acc_out(D_MODEL), acc_out(D_MODEL), acc_out(128)], "loss_head")

    d_w_down = _mm(act, df, "tn", D_FF, D_MODEL, T, tm=1408, tn=D_MODEL, tk=T, name="w_down_dw",
                   out_dtype=BF16).reshape(4, D_FF // 4, D_MODEL)
    da = _mm(df, W["w_down"], "nt", T, D_FF, D_MODEL, tm=tm_lat, tn=1408, tk=D_MODEL, name="w_down_dx")

    tcb = 128
    n_fb = D_FF // tcb
    colb = lambda blk0: pl.BlockSpec((T, tcb), lambda j: (0, blk0 + j))
    ffwb = lambda off, n=3: pl.BlockSpec((n, tcb), lambda j: (0, j + off))
    cvec = ((1, D_FF), F32, pl.BlockSpec((1, tcb), lambda j: (0, j)), None)

    def f_ffn_bwd(ids, ug, uv, d_act, wg, wv, bg, bv):
        sg, sv = _shifts(ug), _shifts(uv)
        gate, val = _conv(ug, wg, bg, sg), _conv(uv, wv, bv, sv)
        s = _sigmoid(gate)
        d_gate = d_act * val * s * (1.0 + gate * (1.0 - s))
        d_val = d_act * gate * s
        wg0, wg1, wg2 = _conv_bwd_w(d_gate, ug, sg)
        wv0, wv1, wv2 = _conv_bwd_w(d_val, uv, sv)
        d_up = [_conv_bwd_x(d_gate, wg), _conv_bwd_x(d_val, wv)]
        return d_up, [_colsum(d_gate), _colsum(d_val), wg0, wg1, wg2, wv0, wv1, wv2]

    d_up3, ffn_stats = _ew(
        f_ffn_bwd, (n_fb,),
        [(up, colb(0)), (up, colb(n_fb)), (da, colb(0)), (W["ffn_conv_w"], ffwb(0)), (W["ffn_conv_w"], ffwb(n_fb)),
         (W["ffn_conv_b"], ffwb(0, 1)), (W["ffn_conv_b"], ffwb(n_fb, 1))],
        [((2, T, D_FF), BF16, pl.BlockSpec((2, T, tcb), lambda j: (0, 0, j)), None),
         ((n_fb, 8, 1, tcb), F32, pl.BlockSpec((None, 8, 1, tcb), lambda j: (j, 0, 0, 0)), None)], "ffn_act_bwd")
    stat = lambda s: ffn_stats[:, s, 0, :].reshape(1, D_FF)
    d_ffn_conv_b = jnp.concatenate([stat(0), stat(1)], axis=1)
    d_ffn_conv_w = jnp.concatenate([jnp.concatenate([stat(2), stat(3), stat(4)], axis=0),
                                    jnp.concatenate([stat(5), stat(6), stat(7)], axis=0)], axis=1)

    tk_t = T
    d_w_up = _mm(h2, d_up3, "tn", D_MODEL, 2 * D_FF, T, tm=D_MODEL, tn=1408, tk=tk_t, name="w_up_dw", out_dtype=BF16,
                 b_spec=pl.BlockSpec((None, tk_t, 1408), lambda i, j, k: (j // 2, k, j % 2)),
                 o_spec=pl.BlockSpec((None, D_MODEL, 1408), lambda i, j, k: (j, i, 0)), out_shape=(4, D_MODEL, 1408))
    dh2 = _mm(d_up3, W["w_up"], "nt", T, D_MODEL, 2 * D_FF, tm=tm_lat, tn=D_MODEL, tk=1408, name="w_up_dx",
              a_spec=pl.BlockSpec((None, tm_lat, 1408), lambda i, j, k: (k // 2, i, k % 2)),
              b_spec=pl.BlockSpec((None, D_MODEL, 1408), lambda i, j, k: (k, j, 0)))

    def f_norm2_bwd(ids, dx2_, dh, x1_, m, g, sc, gate):
        r = _rms(x1_)
        xn = x1_ * r
        dx1 = dx2_ + _rms_bwd(dh * g * (1.0 + sc), xn, r)
        return dx1, dx1 * gate, _colsum(dh), _colsum(dh * xn * g), _colsum(dh * xn * (1.0 + sc)), _colsum(dx1 * m)

    dx1, dmo, dsh2, dsc2, dg_n2, dg1 = _ew(
        f_norm2_bwd, (n_lat,), [(dx2, _rows(D_MODEL)), (dh2, _rows(D_MODEL)), (x1, _rows(D_MODEL)), (mo, _rows(D_MODEL)),
                                (W["norm2_g"], vec(D_MODEL)), (sc2, vec(D_MODEL)), (g1, vec(D_MODEL))],
        [row_out(D_MODEL, F32), row_out(D_MODEL, BF16)] + [acc_out(D_MODEL)] * 4, "norm2_bwd")
    d_w_o = _mm(mrg, dmo, "tn", D_MODEL, D_MODEL, T, tm=D_MODEL, tn=D_MODEL, tk=tk_t, name="w_o_dw",
                out_dtype=BF16).reshape(4, D_MODEL // 4, D_MODEL)
    dmrg = _mm(dmo, W["w_o"], "nt", T, D_MODEL, D_MODEL, tm=tm_lat, tn=D_MODEL, tk=D_MODEL, name="w_o_dx")
    dmrg = early_grads("late", {"w_o": d_w_o, "w_up": d_w_up, "w_down": d_w_down}, dmrg, split=True)

    def f_merge_bwd(ids, dm, ga, gc, a, c):
        sa, sc_ = _sigmoid(ga), _sigmoid(gc)
        return dm * sa, dm * sc_, dm * a * sa * (1.0 - sa), dm * c * sc_ * (1.0 - sc_)

    dya, dyc, dp_ga, dp_gc = _ew(
        f_merge_bwd, (n_lat,), [(dmrg, _rows(D_MODEL)), (pp, _rows(D_MODEL, 0)), (pp, _rows(D_MODEL, 1)),
                                (ya, _rows(D_MODEL)), (yc, _rows(D_MODEL))], [row_out(D_MODEL, BF16)] * 4, "merge_bwd")
    dya = early_continue("late", dya)

    d_w_ao_p = _mm(o_pad, dya, "tn", 1024, D_MODEL, T, tm=1024, tn=D_MODEL, tk=tk_t, name="w_attn_out_dw", out_dtype=BF16)
    do_pad = _mm(dya, W["w_attn_out"], "nt", T, 1024, D_MODEL, tm=tm_lat, tn=1024, tk=D_MODEL, name="w_attn_out_dx")
    d_w_co = _mm(e, dyc, "tn", CONV_DIM, D_MODEL, T, tm=CONV_DIM, tn=256, tk=tk_t, name="w_conv_out_dw", out_dtype=BF16,
                 o_spec=pl.BlockSpec((None, CONV_DIM, 256), lambda i, j, k: (j, i, 0)), out_shape=(4, CONV_DIM, 256))
    de = _mm(dyc, W["w_conv_out"], "nt", T, CONV_DIM, D_MODEL, tm=tm_lat, tn=CONV_DIM, tk=256, name="w_conv_out_dx",
             b_spec=pl.BlockSpec((None, CONV_DIM, 256), lambda i, j, k: (k, j, 0)))

    def f_conv_bwd(ids, xin, cb, cc, d_e, w, b):
        z = cc * xin
        sz = _shifts(z)
        cz = _conv(z, w, b, sz)
        dcz = d_e * cb
        w0, w1, w2 = _conv_bwd_w(dcz, z, sz)
        dz = _conv_bwd_x(dcz, w)
        return dz * cc, d_e * cz, dz * xin, _colsum(dcz), w0, w1, w2

    cvec_c = ((1, CONV_DIM), F32, pl.BlockSpec((1, tc), lambda j: (0, j)), None)
    conv_b = _ew(f_conv_bwd, (CONV_DIM // tc,),
                 [(pp, colT(CX0 // tc)), (pp, colT(CB0 // tc)), (pp, colT(CC0 // tc)), (de, colT(0)),
                  (W["conv_w"], pl.BlockSpec((3, tc), lambda j: (0, j))), (W["conv_b"], pl.BlockSpec((1, tc), lambda j: (0, j)))],
                 [((T, CONV_DIM), BF16, colT(0), None)] * 3 + [cvec_c] * 4, "conv_bwd")
    dp_cx, dp_cb, dp_cc, d_conv_b = conv_b[:4]
    d_conv_w = jnp.concatenate(conv_b[4:7], axis=0)

    dq_raw, dkv, dp_kr = _attn_bwd(q_raw, kv, pp, o_pad, do_pad, lse, tabs, tabs_inv, T, TT)

    tk_a = TT
    d_w_uq_t = _mm(nq, dq_raw, "tn", Q_RANK, 1024, T, tm=Q_RANK, tn=1024, tk=T, name="w_uq_dw", transpose_out=True)
    dnq = _mm(dq_raw, W["w_uq_t"], "nn", T, Q_RANK, 1024, tm=tm_lat, tn=Q_RANK, tk=1024, name="w_uq_dx")
    d_w_ukv = _mm(nkv, dkv, "tn", KV_RANK, 1024, TT, tm=KV_RANK, tn=256, tk=tk_a, name="w_ukv_dw", out_dtype=BF16,
                  o_spec=pl.BlockSpec((None, KV_RANK, 256), lambda i, j, k: (j, i, 0)), out_shape=(4, KV_RANK, 256))
    dnkv = _mm(dkv, W["w_ukv"], "nt", TT, KV_RANK, 1024, tm=tm_all, tn=KV_RANK, tk=256, name="w_ukv_dx",
               b_spec=pl.BlockSpec((None, KV_RANK, 256), lambda i, j, k: (k, j, 0)))
    dnkv = early_grads("mid", {
        "w_attn_out": jnp.transpose(d_w_ao_p.reshape(N_HEADS, HEAD_PAD, 4, 256)[:, 64:], (2, 0, 1, 3)).reshape(
            4, N_HEADS * 64, 256),
        "w_conv_out": d_w_co,
        "w_uq": d_w_uq_t.reshape(4, 2, HEAD_PAD, Q_RANK)[:, :, :QK_DIM].reshape(4, 2 * QK_DIM, Q_RANK).astype(BF16),
        "w_ukv": d_w_ukv}, dnkv)

    def f_lowrank_bwd(ids, ckv, cq, dkv_, dq_, gkv, gq, ga, gc, cx, cb, cc, kr):
        rk, rq = _rms(ckv), _rms(cq)
        nk, nq_ = ckv * rk, cq * rq
        lat = ids[0] < n_lat
        dq_ = jnp.where(lat, dq_, 0.0)
        pieces = [jnp.where(lat, a, jnp.zeros_like(a)) for a in (ga, gc, cx, cb, cc)]
        pieces += [_rms_bwd(dkv_ * gkv, nk, rk).astype(BF16), _rms_bwd(dq_ * gq, nq_, rq).astype(BF16), kr.astype(BF16)]
        return jnp.concatenate(pieces, axis=1), _colsum(dkv_ * nk), _colsum(dq_ * nq_)

    lat_rows = lambda n: pl.BlockSpec((ROW_TILE, n), lambda i: (jnp.minimum(i, n_lat - 1), 0))
    dpp, dg_kv, dg_q = _ew(
        f_lowrank_bwd, (n_all,), [(pp, _rows(KV_RANK, KV0 // KV_RANK)), (pp, _rows(Q_RANK, Q0 // Q_RANK)),
                                  (dnkv, _rows(KV_RANK)), (dnq, lat_rows(Q_RANK)), (W["kv_norm_g"], vec(KV_RANK)),
                                  (W["q_norm_g"], vec(Q_RANK)), (dp_ga, lat_rows(D_MODEL)), (dp_gc, lat_rows(D_MODEL)),
                                  (dp_cx, lat_rows(CONV_DIM)), (dp_cb, lat_rows(CONV_DIM)), (dp_cc, lat_rows(CONV_DIM)),
                                  (dp_kr, _rows(HEAD_PAD))],
        [row_out(P_COLS, BF16, TT), acc_out(KV_RANK), acc_out(Q_RANK)], "lowrank_norm_bwd")
    d_w_in_t = _mm(hh, dpp, "tn", D_MODEL, P_COLS, TT, tm=512, tn=2176, tk=TT, name="w_in_dw", out_dtype=BF16,
                   transpose_out=True)
    dhh = _mm(dpp, W["w_in_t"], "nn", TT, D_MODEL, P_COLS, tm=tm_all, tn=512, tk=2176, name="w_in_dx")

    def f_norm1_bwd(ids, x, dh, dres, g, sc):
        r = _rms(x)
        xn = x * r
        return (dres + _rms_bwd(dh * g * (1.0 + sc), xn, r), _colsum(dh), _colsum(dh * xn * g),
                _colsum(dh * xn * (1.0 + sc)))

    grad_x, dsh1, dsc1, dg_n1 = _ew(
        f_norm1_bwd, (n_lat,), [(xx, _rows(D_MODEL)), (dhh, _rows(D_MODEL)), (dx1, _rows(D_MODEL)),
                                (W["norm1_g"], vec(D_MODEL)), (sc1, vec(D_MODEL))],
        [row_out(D_MODEL, F32)] + [acc_out(D_MODEL)] * 3, "norm1_bwd")

    def f_norm1_ctx_bwd(ids, x, dh, g, sc):
        xn = x * _rms(x)
        return _colsum(dh), _colsum(dh * xn * g), _colsum(dh * xn * (1.0 + sc))

    n_ctx = n_all - n_lat
    dcsh1, dcsc1, dg_n1c = _ew(
        f_norm1_ctx_bwd, (n_ctx,), [(xx, _rows(D_MODEL, 0, n_lat)), (dhh, _rows(D_MODEL, 0, n_lat)),
                                    (W["norm1_g"], vec(D_MODEL)), (csc1, vec(D_MODEL))], [acc_out(D_MODEL)] * 3,
        "norm1_ctx_bwd")

    big = {"w_in": _w_in_t_shards_from_p(d_w_in_t).astype(BF16)}
    zero = jnp.zeros((1, 4 * D_MODEL), F32)
    small = {
        "dmod_lat": jnp.concatenate([dsh1, dsc1, dg1, dsh2, dsc2, dg2], axis=1),
        "dmod_ctx": jnp.concatenate([dcsh1, dcsc1, zero], axis=1),
        "norm1_g": dg_n1 + dg_n1c, "norm2_g": dg_n2, "final_g": dg_f, "q_norm_g": dg_q, "kv_norm_g": dg_kv,
        "conv_b": d_conv_b, "conv_w": d_conv_w.reshape(1, -1), "ffn_conv_b": d_ffn_conv_b,
        "ffn_conv_w": d_ffn_conv_w.reshape(1, -1),
    }
    return grad_x, loss, big, small


SMALL = (("dmod_lat", 6144), ("dmod_ctx", 6144), ("norm1_g", 1024), ("norm2_g", 1024), ("final_g", 1024),
         ("q_norm_g", 384), ("kv_norm_g", 256), ("conv_b", 512), ("conv_w", 1536), ("ffn_conv_b", 5632),
         ("ffn_conv_w", 16896), ("loss", 128))
SMALL_ROWS = 320


def _adam_update(w, g, m, v):
    c1, c2 = 1.0 - ADAM_B1 ** ADAM_STEP, 1.0 - ADAM_B2 ** ADAM_STEP
    m2 = ADAM_B1 * m + (1.0 - ADAM_B1) * g
    v2 = ADAM_B2 * v + (1.0 - ADAM_B2) * (g * g)
    return [-ADAM_LR * ((m2 / c1) / (jnp.sqrt(v2 / c2) + ADAM_EPS) + ADAM_WD * w), m2, v2]


def _adamw(w, g, m, v, name):
    R, C = w.shape
    tr = 8 if R % 8 == 0 else R
    for t in range(8, R + 1, 8):
        if R % t == 0 and t * C * 4 <= (1 << 20):
            tr = t
    spec = pl.BlockSpec((tr, C), lambda i: (i, 0))
    return _ew(lambda ids, *vals: _adam_update(*vals), (R // tr,), [(w, spec), (g, spec), (m, spec), (v, spec)],
               [((R, C), F32, spec, None)] * 3, name)


def kernel(x, c, ctx, c_ctx, w_ada, b_ada, norm1_g, w_in, q_norm_g, kv_norm_g, w_uq, w_ukv, conv_w, conv_b, w_attn_out, w_conv_out, w_o, norm2_g, w_up, ffn_conv_w, ffn_conv_b, w_down, final_g, loss_target, m_c_ctx, m_w_ada, m_b_ada, m_norm1_g, m_w_in, m_q_norm_g, m_kv_norm_g, m_w_uq, m_w_ukv, m_conv_w, m_conv_b, m_w_attn_out, m_w_conv_out, m_w_o, m_norm2_g, m_w_up, m_ffn_conv_w, m_ffn_conv_b, m_w_down, m_final_g, v_c_ctx, v_w_ada, v_b_ada, v_norm1_g, v_w_in, v_q_norm_g, v_kv_norm_g, v_w_uq, v_w_ukv, v_conv_w, v_conv_b, v_w_attn_out, v_w_conv_out, v_w_o, v_norm2_g, v_w_up, v_ffn_conv_w, v_ffn_conv_b, v_w_down, v_final_g):
    mx, my, mc = lax.axis_index("x"), lax.axis_index("y"), lax.axis_index("c")
    chip = 2 * mx + my
    dev = 4 * mx + 2 * my + mc
    T, Tc = x.shape[1], ctx.shape[1]
    TT = T + Tc
    w_in_t, m_w_in_t, v_w_in_t = (jnp.transpose(a[0]) for a in (w_in, m_w_in, v_w_in))
    w_uq_t, m_w_uq_t, v_w_uq_t = (jnp.transpose(a[0]) for a in (w_uq, m_w_uq, v_w_uq))
    conv_sh = jnp.concatenate([conv_w[0], ffn_conv_w[0]], axis=1)
    pay1 = jnp.concatenate([jnp.pad(c, ((0, 7), (0, 0))), jnp.pad(conv_sh, ((0, 5), (0, 0)))], axis=1)
    c_send, c_recv, c_src, c_land, zero0 = _ici_start("all", [pay1], [(8, 8, 2560)], jnp.zeros((8, 128), F32),
                                                      "cond_start")
    w_in_bf = (jnp.pad(w_in_t, ((0, W_IN_SHARD_PAD - W_IN_SHARD), (0, 0))) + zero0[0, 0]).astype(BF16)
    shards = {"w_in": w_in_bf, "w_uq": w_uq_t, "w_ukv": w_ukv[0], "w_attn_out": w_attn_out[0],
              "w_conv_out": w_conv_out[0], "w_o": w_o[0], "w_up": w_up[0], "w_down": w_down[0]}
    (pay1,), (c_land,) = _ici_wait("all", c_send, c_recv, c_src, c_land, shards["w_in"], "cond_wait")
    got1 = lax.dynamic_update_slice(c_land, pay1[None], (dev, 0, 0))
    c_all = got1[:, 0, :D_MODEL]
    conv_all = got1[0::2, :3, D_MODEL:]
    conv_w_full = _cols_from_shards(conv_all[:, :, :128])
    ffn_conv_w_full = _cols_from_shards(conv_all[:, :, 128:])

    cond = jnp.concatenate([c_all, c_ctx.reshape(1, D_MODEL), jnp.zeros((7, D_MODEL), F32)], axis=0)

    def f_silu(ids, v):
        return (v * _sigmoid(v),)

    (s16,) = _ew(f_silu, (1,), [(cond, _full((16, D_MODEL)))], [((16, D_MODEL), F32, _full((16, D_MODEL)), None)], "silu_cond")
    mod_sh = _mm(s16, w_ada[0], "nn", 16, 1536, D_MODEL, tm=16, tn=768, tk=D_MODEL, name="w_ada_fwd")
    m_send, m_recv, m_src, m_land, zero1 = _ici_start("all", [mod_sh], [(8, 16, 1536)], jnp.zeros((8, 128), F32),
                                                      "mod_start")
    shards["w_ukv"] = w_ukv[0] + zero1[0, 0]

    names = [n for n, _ in BIG]
    first = [n for n in names if n not in GATHER_LATE]
    gathered, zero = _gather_weights([shards[n].astype(BF16) for n in first])
    full = dict(zip(first, gathered))
    (mod_mine,), (m_land,) = _ici_wait("all", m_send, m_recv, m_src, m_land, gathered[0], "mod_wait")
    got2 = lax.dynamic_update_slice(m_land, mod_mine[None], (dev, 0, 0))
    mod_all = _cols_from_shards(got2[0::2]) + b_ada
    mod_lat = lax.dynamic_slice_in_dim(mod_all, dev, 1, axis=0)
    mod_ctx = mod_all[8:9]
    xx = jnp.concatenate([x[0], ctx[0]], axis=0)
    late_bf = [(shards[n] + zero[0, 0]).astype(BF16) for n in GATHER_LATE]
    g_send, g_recv, late_src, late_land, xx = _ici_start(
        "gather", late_bf, [(4,) + s.shape for s in late_bf], xx, "gather_late_start")

    def late_weights(after):
        src, land = _ici_wait("gather", g_send, g_recv, late_src, late_land, after, "gather_late_wait")
        got = dict(zip(GATHER_LATE, _gather_finish(src, land)))
        wao = _cols_from_shards(got["w_attn_out"]).reshape(N_HEADS, 64, D_MODEL)
        return {"w_attn_out": jnp.pad(wao, ((0, 0), (64, 0), (0, 0))).reshape(N_HEADS * HEAD_PAD, D_MODEL),
                "w_conv_out": got["w_conv_out"], "w_o": got["w_o"].reshape(D_MODEL, D_MODEL), "w_up": got["w_up"],
                "w_down": got["w_down"].reshape(D_FF, D_MODEL)}

    wuq_t = full["w_uq"].reshape(N_HEADS, QK_DIM, Q_RANK)
    W = {
        "w_in_t": _w_in_t_p_from_shards(full["w_in"]),
        "w_uq_t": jnp.pad(wuq_t, ((0, 0), (0, HEAD_PAD - QK_DIM), (0, 0))).reshape(N_HEADS * HEAD_PAD, Q_RANK),
        "w_ukv": full["w_ukv"],
        "norm1_g": norm1_g, "norm2_g": norm2_g, "final_g": final_g.reshape(1, D_MODEL), "q_norm_g": q_norm_g,
        "kv_norm_g": kv_norm_g, "conv_w": conv_w_full, "conv_b": conv_b, "ffn_conv_w": ffn_conv_w_full,
        "ffn_conv_b": ffn_conv_b,
    }

    place = jnp.stack([chip, mc]).astype(jnp.int32)
    early = {}

    pending = {}

    def scatter(tag, group, gs, from_sib, carry):
        sums = [_add_pair(gs[w], from_sib[w], place, "rs_pair_add_" + n) for w, n in enumerate(group)]
        send, recv, sums, land, carry = _ici_start(
            "scatter", sums, [(3,) + s.shape[1:] for s in sums], carry, "rs_chips_" + tag + "_start")
        early[tag] = (group, send, recv, sums, land)
        return carry

    def early_grads(tag, g, carry, split=False):
        gs = list(g.values())
        if not split:
            return scatter(tag, list(g), gs, _rs_pair(gs, "rs_pair_" + tag), carry)
        send, recv, gs, land, carry = _ici_start(
            "pair", gs, [(4, s.shape[1] // 2, s.shape[2]) for s in gs], carry, "rs_pair_" + tag + "_start")
        pending[tag] = (list(g), send, recv, gs, land)
        return carry

    def early_continue(tag, carry):
        group, send, recv, gs, land = pending[tag]
        gs, from_sib = _ici_wait("pair", send, recv, gs, land, carry, "rs_pair_" + tag + "_wait")
        return scatter(tag, group, gs, from_sib, carry)

    grad_x, loss_part, gbig, gsmall = _local_step(xx, loss_target[0], mod_lat, mod_ctx, W, late_weights, early_grads,
                                                  early_continue)

    gsmall["loss"] = loss_part
    pay3 = jnp.concatenate([gsmall[n].reshape(-1) for n, _ in SMALL])
    pay3 = jnp.pad(pay3, (0, SMALL_ROWS * 128 - pay3.shape[0])).reshape(SMALL_ROWS, 128)
    s_send, s_recv, s_src, s_land, w_in_thru = _ici_start("all", [pay3], [(8, SMALL_ROWS, 128)], gbig["w_in"],
                                                         "small_start")
    gbig = {"w_in": w_in_thru}

    after_small = early_grads("last", gbig, s_src[0])

    (pay3,), (s_land,) = _ici_wait("all", s_send, s_recv, [after_small], s_land, early["last"][3][0], "small_wait")
    got3 = lax.dynamic_update_slice(s_land, pay3[None], (dev, 0, 0)).reshape(8 * SMALL_ROWS, 128)

    def f_sum8(ids, a):
        s = a[0:SMALL_ROWS]
        for d in range(1, 8):
            s = s + a[d * SMALL_ROWS:(d + 1) * SMALL_ROWS]
        return (s,)

    (vsum,) = _ew(f_sum8, (1,), [(got3, _full((8 * SMALL_ROWS, 128)))],
                  [((SMALL_ROWS, 128), F32, _full((SMALL_ROWS, 128)), None)], "sum_small")
    vflat = vsum.reshape(-1)
    gvec, off = {}, 0
    for n, size in SMALL:
        gvec[n] = vflat[off:off + size]
        off += size
    loss = gvec["loss"][0]
    dmod_rows = got3.reshape(8, SMALL_ROWS * 128)[:, :6 * D_MODEL]
    dm16 = jnp.concatenate([dmod_rows, gvec["dmod_ctx"].reshape(1, -1), jnp.zeros((7, 6 * D_MODEL), F32)], axis=0)

    def f_colsum(ids, a):
        return (_colsum(a),)

    (g_b_ada,) = _ew(f_colsum, (1,), [(dm16, _full((16, 6 * D_MODEL)))],
                     [((1, 6 * D_MODEL), F32, _full((1, 6 * D_MODEL)), None)], "b_ada_grad")
    dm_sh = lax.dynamic_slice_in_dim(dm16, chip * 1536, 1536, axis=1)
    g_w_ada = _mm(s16, dm_sh, "tn", D_MODEL, 1536, 16, tm=512, tn=768, tk=16, name="w_ada_dw")
    dcond_part = _mm(dm_sh, w_ada[0], "nt", 16, D_MODEL, 1536, tm=16, tn=512, tk=1536, name="w_ada_dx")
    d_send, d_recv, d_src, d_land, vsum = _ici_start("all", [dcond_part[8:16]], [(8, 8, D_MODEL)], vsum, "dcond_start")

    def finish(tags, after):
        done, own, lands = [], [], []
        for tag in tags:
            tag_names, send, recv, sums, land = early[tag]
            sums, land = _ici_wait("scatter", send, recv, sums, land, after, "rs_chips_" + tag + "_wait")
            done, own, lands = done + tag_names, own + sums, lands + land
        halves = [_add_chips(a, b, place, "rs_chip_add_" + n) for a, b, n in zip(own, lands, done)]
        return dict(zip(done, _rs_pair_back(halves, "rs_pair_back_" + tags[0])))

    grads, deltas, new_m, new_v = {}, {}, {}, {}

    def adam(n, w_, m_, v_, g, transposed):
        d_, m2, v2 = _adamw(w_, g, m_, v_, "adamw_" + n)
        back = (lambda a: jnp.transpose(a)[None]) if transposed else (lambda a: a[None])
        grads[n], deltas[n], new_m[n], new_v[n] = back(g[:w_.shape[0]]), back(d_), back(m2), back(v2)

    gw = finish(["late", "mid"], grad_x)
    adam("w_ada", w_ada[0], m_w_ada[0], v_w_ada[0], g_w_ada, False)
    for n, (w_, m_, v_) in {"w_ukv": (w_ukv, m_w_ukv, v_w_ukv), "w_attn_out": (w_attn_out, m_w_attn_out, v_w_attn_out),
                            "w_conv_out": (w_conv_out, m_w_conv_out, v_w_conv_out), "w_o": (w_o, m_w_o, v_w_o),
                            "w_up": (w_up, m_w_up, v_w_up), "w_down": (w_down, m_w_down, v_w_down)}.items():
        adam(n, w_[0], m_[0], v_[0], gw[n], False)
    adam("w_uq", w_uq_t, m_w_uq_t, v_w_uq_t, gw["w_uq"], True)
    gw_in = finish(["last"], deltas["w_up"])
    adam("w_in", w_in_t, m_w_in_t, v_w_in_t, gw_in["w_in"], True)

    (dcond_mine,), (d_land,) = _ici_wait("all", d_send, d_recv, d_src, d_land, deltas["w_in"], "dcond_wait")
    got4 = lax.dynamic_update_slice(d_land, dcond_mine[None], (dev, 0, 0))[0::2, 0]

    def f_c_ctx(ids, parts, cc):
        s = _sigmoid(cc)
        d = parts[0:1] + parts[1:2] + parts[2:3] + parts[3:4]
        return (d * s * (1.0 + cc * (1.0 - s)),)

    (g_c_ctx,) = _ew(f_c_ctx, (1,), [(got4, _full((4, D_MODEL))), (c_ctx.reshape(1, D_MODEL), _full((1, D_MODEL)))],
                     [((1, D_MODEL), F32, _full((1, D_MODEL)), None)], "c_ctx_grad")

    conv_w_g = lax.dynamic_slice_in_dim(gvec["conv_w"].reshape(3, CONV_DIM), chip * 128, 128, axis=1)
    ffn_conv_w_g = lax.dynamic_slice_in_dim(gvec["ffn_conv_w"].reshape(3, 2 * D_FF), chip * 1408, 1408, axis=1)
    vec_params = (("c_ctx", c_ctx, m_c_ctx, v_c_ctx, g_c_ctx), ("b_ada", b_ada, m_b_ada, v_b_ada, g_b_ada),
                  ("norm1_g", norm1_g, m_norm1_g, v_norm1_g, gvec["norm1_g"]),
                  ("q_norm_g", q_norm_g, m_q_norm_g, v_q_norm_g, gvec["q_norm_g"]),
                  ("kv_norm_g", kv_norm_g, m_kv_norm_g, v_kv_norm_g, gvec["kv_norm_g"]),
                  ("conv_w", conv_w, m_conv_w, v_conv_w, conv_w_g), ("conv_b", conv_b, m_conv_b, v_conv_b, gvec["conv_b"]),
                  ("norm2_g", norm2_g, m_norm2_g, v_norm2_g, gvec["norm2_g"]),
                  ("ffn_conv_w", ffn_conv_w, m_ffn_conv_w, v_ffn_conv_w, ffn_conv_w_g),
                  ("ffn_conv_b", ffn_conv_b, m_ffn_conv_b, v_ffn_conv_b, gvec["ffn_conv_b"]),
                  ("final_g", final_g, m_final_g, v_final_g, gvec["final_g"]))
    two_d = lambda a: a.reshape((-1, a.shape[-1]))

    def f_adam_many(ids, *vals):
        out = []
        for k in range(len(vec_params)):
            out += _adam_update(*vals[4 * k:4 * k + 4])
        return out

    ins_v, outs_v = [], []
    for p in vec_params:
        shp = two_d(p[1]).shape
        ins_v += [(two_d(a), _full(shp)) for a in (p[1], p[4], p[2], p[3])]
        outs_v += [(shp, F32, _full(shp), None)] * 3
    res_v = _ew(f_adam_many, (1,), ins_v, outs_v, "adamw_vectors")
    for k, p in enumerate(vec_params):
        n, shape = p[0], p[1].shape
        grads[n] = p[4].reshape(shape)
        deltas[n], new_m[n], new_v[n] = (r.reshape(shape) for r in res_v[3 * k:3 * k + 3])

    order = ("c_ctx", "w_ada", "b_ada", "norm1_g", "w_in", "q_norm_g", "kv_norm_g", "w_uq", "w_ukv", "conv_w", "conv_b",
             "w_attn_out", "w_conv_out", "w_o", "norm2_g", "w_up", "ffn_conv_w", "ffn_conv_b", "w_down", "final_g")
    return (loss, grad_x[None], *[grads[n] for n in order], *[deltas[n] for n in order],
            *[new_m[n] for n in order], *[new_v[n] for n in order])
```

```python
import functools

import jax
import jax.numpy as jnp
from jax import lax
from jax.experimental import pallas as pl
from jax.experimental.pallas import tpu as pltpu

F32, BF16 = jnp.float32, jnp.bfloat16
MESH = pl.DeviceIdType.MESH

D_MODEL = 1024
N_HEADS = 8
HEAD_PAD = 128
QK_DIM = 96
Q_RANK, KV_RANK = 384, 256
CONV_DIM = 512
D_FF = 2816
GRID_W = 64
ROPE_THETA = 10000.0
EPS = 1e-6
GA0, GC0, CX0, CB0, CC0, KV0, Q0, KR0, P_COLS = 0, 1024, 2048, 2560, 3072, 3584, 3840, 4224, 4352
ROW_TILE = 256
VMEM_LIMIT_BYTES = 48 * 1024 * 1024

ADAM_LR, ADAM_B1, ADAM_B2, ADAM_EPS, ADAM_WD, ADAM_STEP = 0.001, 0.9, 0.999, 1e-08, 0.01, 10

BIG = (("w_in", (1088, 1024)), ("w_uq", (192, 384)), ("w_ukv", (256, 256)), ("w_attn_out", (512, 256)),
       ("w_conv_out", (512, 256)), ("w_o", (256, 1024)), ("w_up", (1024, 1408)), ("w_down", (704, 1024)))

GATHER_LATE = ("w_attn_out", "w_conv_out", "w_o", "w_up", "w_down")

NN = (((1,), (0,)), ((), ()))
NT = (((1,), (1,)), ((), ()))
TN = (((0,), (0,)), ((), ()))


def _cp(sem):
    return pltpu.CompilerParams(dimension_semantics=sem, vmem_limit_bytes=VMEM_LIMIT_BYTES)


PIN_BYTES = 1 << 19


def _in_hbm(arrays):
    return [pltpu.with_memory_space_constraint(a, pltpu.HBM) if a.size * a.dtype.itemsize >= PIN_BYTES else a
            for a in arrays]


def _out(shape, dtype):
    n = 1
    for d in shape:
        n *= d
    big = n * jnp.dtype(dtype).itemsize >= PIN_BYTES
    return pltpu.HBM(shape, dtype) if big else jax.ShapeDtypeStruct(shape, dtype)


def _pick(n, prefs):
    for p in prefs:
        if n % p == 0:
            return p
    return n


def _mm(a, b, mode, M, N, K, *, tm, tn, tk, name, out_dtype=F32, a_spec=None, b_spec=None, o_spec=None,
        out_shape=None, transpose_out=False):
    assert M % tm == 0 and N % tn == 0 and K % tk == 0, (name, M, N, K, tm, tn, tk)
    nk = K // tk
    dims = {"nn": NN, "nt": NT, "tn": TN}[mode]
    if a_spec is None:
        a_spec = (pl.BlockSpec((tk, tm), lambda i, j, k: (k, i)) if mode == "tn"
                  else pl.BlockSpec((tm, tk), lambda i, j, k: (i, k)))
    if b_spec is None:
        b_spec = (pl.BlockSpec((tn, tk), lambda i, j, k: (j, k)) if mode == "nt"
                  else pl.BlockSpec((tk, tn), lambda i, j, k: (k, j)))
    if o_spec is None:
        o_spec = (pl.BlockSpec((tn, tm), lambda i, j, k: (j, i)) if transpose_out
                  else pl.BlockSpec((tm, tn), lambda i, j, k: (i, j)))
    if out_shape is None:
        out_shape = (N, M) if transpose_out else (M, N)

    def emit(o_ref, val):
        o_ref[...] = (val.T if transpose_out else val).astype(o_ref.dtype)

    def body(a_ref, b_ref, o_ref, *scratch):
        part = lax.dot_general(a_ref[...].astype(BF16), b_ref[...].astype(BF16), dims, preferred_element_type=F32)
        if nk == 1:
            emit(o_ref, part)
            return
        acc_ref, = scratch
        k = pl.program_id(2)

        @pl.when(k == 0)
        def _():
            acc_ref[...] = part

        @pl.when((k > 0) & (k < nk - 1))
        def _():
            acc_ref[...] += part

        @pl.when(k == nk - 1)
        def _():
            emit(o_ref, acc_ref[...] + part)

    return pl.pallas_call(
        body, grid=(M // tm, N // tn, nk), in_specs=[a_spec, b_spec], out_specs=o_spec,
        out_shape=_out(out_shape, out_dtype),
        scratch_shapes=[pltpu.VMEM((tm, tn), F32)] if nk > 1 else [],
        compiler_params=_cp(("parallel", "parallel", "arbitrary")), name=name)(*_in_hbm([a, b]))


def _ew(fn, grid, ins, outs, name, scalars=None):
    n_in = len(ins)
    n_sc = 0 if scalars is None else 1

    def store(ref, val, acc, ids):
        if isinstance(val, (list, tuple)):
            for h, v in enumerate(val):
                ref[h] = v.astype(ref.dtype)
            return
        if acc is None:
            ref[...] = val.astype(ref.dtype)
            return

        @pl.when(ids[acc] == 0)
        def _():
            ref[...] = val.astype(ref.dtype)

        @pl.when(ids[acc] > 0)
        def _():
            ref[...] += val.astype(ref.dtype)

    def body(*refs):
        refs = refs[n_sc:]
        ids = tuple(pl.program_id(a) for a in range(len(grid)))
        vals = fn(ids, *[r[...] for r in refs[:n_in]])
        for ref, val, (_, _, _, acc) in zip(refs[n_in:], vals, outs):
            store(ref, val, acc, ids)

    acc_axes = {o[3] for o in outs if o[3] is not None}
    sem = tuple("arbitrary" if a in acc_axes else "parallel" for a in range(len(grid)))
    in_specs, out_specs = [s for _, s in ins], [o[2] for o in outs]
    out_shape = [_out(o[0], o[1]) for o in outs]
    args = _in_hbm([a for a, _ in ins])
    if scalars is None:
        return pl.pallas_call(body, grid=grid, in_specs=in_specs, out_specs=out_specs, out_shape=out_shape,
                              compiler_params=_cp(sem), name=name)(*args)
    spec = pltpu.PrefetchScalarGridSpec(num_scalar_prefetch=1, grid=grid, in_specs=in_specs, out_specs=out_specs)
    return pl.pallas_call(body, grid_spec=spec, out_shape=out_shape, compiler_params=_cp(sem), name=name)(scalars, *args)


def _rows(width, cblk=0, roff=0, tr=ROW_TILE):
    return pl.BlockSpec((tr, width), lambda i: (i + roff, cblk))


def _full(shape):
    nd = len(shape)
    return pl.BlockSpec(shape, lambda *_: (0,) * nd)


def _sigmoid(x):
    return 1.0 / (1.0 + jnp.exp(-x))


def _rms(x):
    return lax.rsqrt(jnp.mean(x * x, axis=-1, keepdims=True) + EPS)


def _rms_bwd(dn, xn, r):
    return r * (dn - xn * jnp.mean(dn * xn, axis=-1, keepdims=True))


def _colsum(x):
    return jnp.sum(x, axis=0, keepdims=True)


def _shifts(x):
    n = x.shape[0]
    rows = lax.broadcasted_iota(jnp.int32, x.shape, 0)
    return jnp.where(rows == 0, 0.0, pltpu.roll(x, 1, 0)), jnp.where(rows == n - 1, 0.0, pltpu.roll(x, n - 1, 0))


def _conv(x, w, b, shifted=None):
    prev, nxt = _shifts(x) if shifted is None else shifted
    return b + prev * w[0:1] + x * w[1:2] + nxt * w[2:3]


def _conv_bwd_x(dy, w):
    prev, nxt = _shifts(dy)
    return nxt * w[0:1] + dy * w[1:2] + prev * w[2:3]


def _conv_bwd_w(dy, x, shifted):
    prev, nxt = shifted
    return _colsum(dy * prev), _colsum(dy * x), _colsum(dy * nxt)


def _rope(x, cos, sin_lo, sin_hi):
    return x * cos + pltpu.roll(x, HEAD_PAD - 8, 1) * sin_lo + pltpu.roll(x, 8, 1) * sin_hi


ATTN_SCALE = QK_DIM ** -0.5
LOG2_E = 1.4426950408889634


def _head_keys(kv_ref, kr_ref, cos_ref, slo_ref, shi_ref, kc_ref, vp_ref):
    kv = kv_ref[...]
    lane = lax.broadcasted_iota(jnp.int32, kv.shape, 1)
    kc_ref[...] = jnp.where(lane < 64, kv, _rope(kr_ref[...], cos_ref[...], slo_ref[...], shi_ref[...])).astype(BF16)
    vp_ref[...] = jnp.where(lane >= 64, kv, 0.0).astype(BF16)


ATTN_Q_TILE = 512


def _attn_specs(tq, TT):
    q = pl.BlockSpec((tq, HEAD_PAD), lambda h, i: (i, h))
    keys = pl.BlockSpec((TT, HEAD_PAD), lambda h, i: (0, h))
    kr = pl.BlockSpec((TT, HEAD_PAD), lambda h, i: (0, KR0 // HEAD_PAD))
    tab_q = pl.BlockSpec((tq, HEAD_PAD), lambda h, i: (i, 0))
    tab_k = pl.BlockSpec((TT, HEAD_PAD), lambda h, i: (0, 0))
    lse = pl.BlockSpec((None, tq, 1), lambda h, i: (h, i, 0))
    return q, keys, kr, tab_q, tab_k, lse


def _attn_fwd(q_raw, kv, pp, tabs, T, TT):
    tq = ROW_TILE
    cos, slo, shi = tabs

    def body(q_ref, kv_ref, kr_ref, cq, lq, hq, ck, lk, hk, o_ref, l_ref, kc, vp):
        @pl.when(pl.program_id(1) == 0)
        def _():
            _head_keys(kv_ref, kr_ref, ck, lk, hk, kc, vp)

        q = _rope(q_ref[...], cq[...], lq[...], hq[...]).astype(BF16)
        s = lax.dot_general(q, kc[...], NT, preferred_element_type=F32)
        m = jnp.max(s, axis=-1, keepdims=True)
        p = jnp.exp2((s - m) * (ATTN_SCALE * LOG2_E))
        l = jnp.sum(p, axis=-1, keepdims=True)
        o = lax.dot_general(p.astype(BF16), vp[...], NN, preferred_element_type=F32)
        o_ref[...] = o / l
        l_ref[...] = m * ATTN_SCALE + jnp.log(l)

    qs, keys, kr, tab_q, tab_k, lse = _attn_specs(tq, TT)
    return pl.pallas_call(
        body, grid=(N_HEADS, T // tq), in_specs=[qs, keys, kr, tab_q, tab_q, tab_q, tab_k, tab_k, tab_k],
        out_specs=[qs, lse],
        out_shape=[jax.ShapeDtypeStruct((T, N_HEADS * HEAD_PAD), F32), jax.ShapeDtypeStruct((N_HEADS, T, 1), F32)],
        scratch_shapes=[pltpu.VMEM((TT, HEAD_PAD), BF16), pltpu.VMEM((TT, HEAD_PAD), BF16)],
        compiler_params=_cp(("parallel", "arbitrary")), name="attn_fwd",
    )(*_in_hbm([q_raw, kv, pp, cos, slo, shi, cos, slo, shi]))


def _attn_bwd(q_raw, kv, pp, o, do, lse, tabs, tabs_inv, T, TT):
    tq = _pick(T, (ATTN_Q_TILE, ROW_TILE))
    nq = T // tq
    cos, slo, shi = tabs
    cos_i, slo_i, shi_i = tabs_inv

    def body(q_ref, kv_ref, kr_ref, cq, lq, hq, ck, lk, hk, iq, ilq, ihq, ik, ilk, ihk, o_ref, do_ref, l_ref,
             dq_ref, dkv_ref, dkr_ref, kc, vp, dk, dv):
        h, i = pl.program_id(0), pl.program_id(1)

        @pl.when(i == 0)
        def _():
            _head_keys(kv_ref, kr_ref, ck, lk, hk, kc, vp)
            dk[...] = jnp.zeros_like(dk)
            dv[...] = jnp.zeros_like(dv)

        q = _rope(q_ref[...], cq[...], lq[...], hq[...]).astype(BF16)
        k, v, d_o = kc[...], vp[...], do_ref[...]
        s = lax.dot_general(q, k, NT, preferred_element_type=F32)
        p = jnp.exp2(s * (ATTN_SCALE * LOG2_E) - l_ref[...] * LOG2_E)
        dob = d_o.astype(BF16)
        dp = lax.dot_general(dob, v, NT, preferred_element_type=F32)
        dd = jnp.sum(d_o * o_ref[...], axis=-1, keepdims=True)
        ds = (p * (dp - dd) * ATTN_SCALE).astype(BF16)
        dq = lax.dot_general(ds, k, NN, preferred_element_type=F32)
        dq_ref[...] = _rope(dq, iq[...], ilq[...], ihq[...]).astype(dq_ref.dtype)
        dk[...] += lax.dot_general(q, ds, TN, preferred_element_type=F32)
        dv[...] += lax.dot_general(dob, p.astype(BF16), TN, preferred_element_type=F32)

        @pl.when(i == nq - 1)
        def _():
            dkh = dk[...].T
            lane = lax.broadcasted_iota(jnp.int32, dkh.shape, 1)
            dkv_ref[...] = jnp.where(lane < 64, dkh, dv[...].T).astype(dkv_ref.dtype)
            rot = _rope(jnp.where((lane >= 64) & (lane < 96), dkh, 0.0), ik[...], ilk[...], ihk[...])

            @pl.when(h == 0)
            def _():
                dkr_ref[...] = rot

            @pl.when(h > 0)
            def _():
                dkr_ref[...] += rot

    qs, keys, kr, tab_q, tab_k, lse_spec = _attn_specs(tq, TT)
    wide = lambda rows: jax.ShapeDtypeStruct((rows, N_HEADS * HEAD_PAD), BF16)
    return pl.pallas_call(
        body, grid=(N_HEADS, nq),
        in_specs=[qs, keys, kr] + [tab_q] * 3 + [tab_k] * 3 + [tab_q] * 3 + [tab_k] * 3 + [qs, qs, lse_spec],
        out_specs=[qs, keys, pl.BlockSpec((TT, HEAD_PAD), lambda h, i: (0, 0))],
        out_shape=[wide(T), wide(TT), jax.ShapeDtypeStruct((TT, HEAD_PAD), F32)],
        scratch_shapes=[pltpu.VMEM((TT, HEAD_PAD), BF16), pltpu.VMEM((TT, HEAD_PAD), BF16),
                        pltpu.VMEM((HEAD_PAD, TT), F32), pltpu.VMEM((HEAD_PAD, TT), F32)],
        compiler_params=_cp(("arbitrary", "arbitrary")), name="attn_bwd",
    )(*_in_hbm([q_raw, kv, pp, cos, slo, shi, cos, slo, shi, cos_i, slo_i, shi_i, cos_i, slo_i, shi_i, o, do, lse]))


def _hbm_specs(n):
    return [pl.BlockSpec(memory_space=pl.ANY)] * n


def _gather_weights(shards):
    n = len(shards)
    halves = [s.shape[0] // 2 for s in shards]

    def body(*refs):
        ins, outs = refs[:n], refs[n:2 * n]
        token, send_sems, recv_sems = refs[2 * n:]
        token[...] = jnp.zeros_like(token)
        mx, my, mc = lax.axis_index("x"), lax.axis_index("y"), lax.axis_index("c")
        j_me = 2 * mx + my
        chips = [(1 - mx, my), (mx, 1 - my), (1 - mx, 1 - my)]

        def half(w, chip_idx, hc):
            return outs[w].at[chip_idx, pl.ds(hc * halves[w], halves[w]), :]

        def copy(w, k, src, dst, to):
            return pltpu.make_async_remote_copy(src_ref=src, dst_ref=dst, send_sem=send_sems.at[w, k],
                                                recv_sem=recv_sems.at[w, k], device_id=to, device_id_type=MESH)

        sends = []
        for w in range(n):
            cp = copy(w, 6, ins[w], outs[w].at[j_me], (mx, my, 1 - mc))
            cp.start()
            sends.append(cp)
        for k, (px, py) in enumerate(chips):
            for w in range(n):
                cp = copy(w, k, ins[w].at[pl.ds(mc * halves[w], halves[w]), :], half(w, j_me, mc), (px, py, mc))
                cp.start()
                sends.append(cp)
        for k, (px, py) in enumerate(chips):
            for w in range(n):
                got = half(w, 2 * px + py, mc)
                copy(w, k, got, got, (px, py, mc)).wait_recv()
                cp = copy(w, 3 + k, got, got, (mx, my, 1 - mc))
                cp.start()
                sends.append(cp)
        for k, (px, py) in enumerate(chips):
            for w in range(n):
                got = half(w, 2 * px + py, 1 - mc)
                copy(w, 3 + k, got, got, (mx, my, 1 - mc)).wait_recv()
        for w in range(n):
            own = outs[w].at[j_me]
            copy(w, 6, own, own, (mx, my, 1 - mc)).wait_recv()
        for cp in sends:
            cp.wait_send()

    res = pl.pallas_call(
        body, out_shape=[jax.ShapeDtypeStruct((4,) + s.shape, s.dtype) for s in shards]
        + [jax.ShapeDtypeStruct((8, 128), F32)],
        in_specs=_hbm_specs(n), out_specs=_hbm_specs(n) + [pl.BlockSpec(memory_space=pltpu.VMEM)],
        scratch_shapes=[pltpu.SemaphoreType.DMA((n, 7)), pltpu.SemaphoreType.DMA((n, 7))],
        name="gather_weights")(*shards)
    return list(res[:n]), res[n]


def _rs_pair(gs, name):
    n = len(gs)
    halves = [g.shape[1] // 2 for g in gs]

    def body(*refs):
        ins, lands = refs[:n], refs[n:2 * n]
        send_sems, recv_sems = refs[2 * n:]
        mx, my, mc = lax.axis_index("x"), lax.axis_index("y"), lax.axis_index("c")
        copies = []
        for w in range(n):
            h = halves[w]
            cp = pltpu.make_async_remote_copy(
                src_ref=ins[w].at[:, pl.ds((1 - mc) * h, h), :], dst_ref=lands[w], send_sem=send_sems.at[w],
                recv_sem=recv_sems.at[w], device_id=(mx, my, 1 - mc), device_id_type=MESH)
            cp.start()
            copies.append(cp)
        for cp in copies:
            cp.wait()

    return pl.pallas_call(
        body, out_shape=[jax.ShapeDtypeStruct((4, h, g.shape[2]), g.dtype) for g, h in zip(gs, halves)],
        in_specs=_hbm_specs(n), out_specs=_hbm_specs(n),
        scratch_shapes=[pltpu.SemaphoreType.DMA((n,)), pltpu.SemaphoreType.DMA((n,))], name=name)(*gs)


def _rs_chips(parts):
    n = len(parts)

    def body(*refs):
        ins, lands = refs[:n], refs[n:2 * n]
        send_sems, recv_sems = refs[2 * n:]
        mx, my, mc = lax.axis_index("x"), lax.axis_index("y"), lax.axis_index("c")
        copies = []
        for k, (px, py) in enumerate([(1 - mx, my), (mx, 1 - my), (1 - mx, 1 - my)]):
            for w in range(n):
                cp = pltpu.make_async_remote_copy(
                    src_ref=ins[w].at[2 * px + py], dst_ref=lands[w].at[k], send_sem=send_sems.at[w, k],
                    recv_sem=recv_sems.at[w, k], device_id=(px, py, mc), device_id_type=MESH)
                cp.start()
                copies.append(cp)
        for cp in copies:
            cp.wait()

    return list(pl.pallas_call(
        body, out_shape=[jax.ShapeDtypeStruct((3,) + p.shape[1:], p.dtype) for p in parts],
        in_specs=_hbm_specs(n), out_specs=_hbm_specs(n),
        scratch_shapes=[pltpu.SemaphoreType.DMA((n, 3)), pltpu.SemaphoreType.DMA((n, 3))], name="rs_chips")(*parts))


def _rs_pair_back(gs, name):
    n = len(gs)

    def body(*refs):
        outs = refs[n:2 * n]
        send_sems, recv_sems = refs[2 * n:]
        mx, my, mc = lax.axis_index("x"), lax.axis_index("y"), lax.axis_index("c")
        copies = []
        for w in range(n):
            h = gs[w].shape[0] // 2
            mine = outs[w].at[pl.ds(mc * h, h), :]
            cp = pltpu.make_async_remote_copy(src_ref=mine, dst_ref=mine, send_sem=send_sems.at[w],
                                              recv_sem=recv_sems.at[w], device_id=(mx, my, 1 - mc), device_id_type=MESH)
            cp.start()
            copies.append(cp)
        for cp in copies:
            cp.wait()

    return pl.pallas_call(
        body, out_shape=[jax.ShapeDtypeStruct(g.shape, g.dtype) for g in gs],
        in_specs=_hbm_specs(n), out_specs=_hbm_specs(n), input_output_aliases={w: w for w in range(n)},
        scratch_shapes=[pltpu.SemaphoreType.DMA((n,)), pltpu.SemaphoreType.DMA((n,))], name=name)(*gs)


_HBM = pl.BlockSpec(memory_space=pltpu.HBM)
_SEM = pl.BlockSpec(memory_space=pltpu.SEMAPHORE)
_EFFECT = pltpu.SideEffectType.DATAFLOW_SIDE_EFFECTING


def _ici_copies(kind, srcs, lands, send_sems, recv_sems):
    n = len(srcs)
    mx, my, mc = lax.axis_index("x"), lax.axis_index("y"), lax.axis_index("c")
    j_me = 2 * mx + my
    copies = []
    if kind == "all":
        for k in range(7):
            a, b, c = (k + 1) >> 2 & 1, (k + 1) >> 1 & 1, (k + 1) & 1
            peer = (1 - mx if a else mx, 1 - my if b else my, 1 - mc if c else mc)
            for w in range(n):
                copies.append(pltpu.make_async_remote_copy(
                    src_ref=srcs[w], dst_ref=lands[w].at[4 * mx + 2 * my + mc], send_sem=send_sems.at[7 * w + k],
                    recv_sem=recv_sems.at[7 * w + k], device_id=peer, device_id_type=MESH))
        return copies
    if kind == "pair":
        for w in range(n):
            h = srcs[w].shape[1] // 2
            copies.append(pltpu.make_async_remote_copy(
                src_ref=srcs[w].at[:, pl.ds((1 - mc) * h, h), :], dst_ref=lands[w], send_sem=send_sems.at[w],
                recv_sem=recv_sems.at[w], device_id=(mx, my, 1 - mc), device_id_type=MESH))
        return copies
    for k, (px, py) in enumerate([(1 - mx, my), (mx, 1 - my), (1 - mx, 1 - my)]):
        for w in range(n):
            if kind == "gather":
                h = srcs[w].shape[0] // 2
                src, dst = srcs[w].at[pl.ds(mc * h, h), :], lands[w].at[j_me, pl.ds(mc * h, h), :]
            else:
                src, dst = srcs[w].at[2 * px + py], lands[w].at[k]
            copies.append(pltpu.make_async_remote_copy(
                src_ref=src, dst_ref=dst, send_sem=send_sems.at[3 * w + k], recv_sem=recv_sems.at[3 * w + k],
                device_id=(px, py, mc), device_id_type=MESH))
    return copies


_SEMS_PER_OPERAND = {"gather": 3, "scatter": 3, "all": 7, "pair": 1}


def _ici_start(kind, srcs, land_shapes, carry, name):
    n = len(srcs)

    def body(*refs):
        ins, lands = refs[:n], refs[n:2 * n]
        send_sems, recv_sems = refs[2 * n + 1], refs[2 * n + 2]
        for cp in _ici_copies(kind, ins, lands, send_sems, recv_sems):
            cp.start()

    hbm = lambda a: pltpu.with_memory_space_constraint(a, pltpu.HBM)
    lands = [lax.empty(s, srcs[0].dtype) for s in land_shapes]
    args = [hbm(a) for a in list(srcs) + lands + [carry]]
    n_sem = _SEMS_PER_OPERAND[kind] * n
    out_shape = ([pltpu.SemaphoreType.DMA((n_sem,)), pltpu.SemaphoreType.DMA((n_sem,))]
                 + [pltpu.HBM(a.shape, a.dtype) for a in args])
    res = pl.pallas_call(
        body, name=name, out_shape=out_shape, in_specs=[_HBM] * len(args), out_specs=[_SEM, _SEM] + [_HBM] * len(args),
        input_output_aliases={i: 2 + i for i in range(len(args))},
        compiler_params=pltpu.CompilerParams(has_side_effects=_EFFECT))(*args)
    return res[0], res[1], list(res[2:2 + n]), list(res[2 + n:2 + 2 * n]), res[2 + 2 * n]


def _ici_wait(kind, send_sems, recv_sems, srcs, lands, after, name):
    n = len(srcs)

    def body(*refs):
        ins, zones = refs[:n], refs[n:2 * n]
        for cp in _ici_copies(kind, ins, zones, refs[2 * n], refs[2 * n + 1]):
            cp.wait_send()
            cp.wait_recv()

    args = list(srcs) + list(lands)
    res = pl.pallas_call(
        body, name=name, out_shape=[pltpu.HBM(a.shape, a.dtype) for a in args],
        in_specs=[_HBM] * len(args) + [_SEM, _SEM, pl.BlockSpec(memory_space=pl.ANY)], out_specs=[_HBM] * len(args),
        input_output_aliases={i: i for i in range(len(args))},
        compiler_params=pltpu.CompilerParams(has_side_effects=_EFFECT))(*args, send_sems, recv_sems, after)
    return list(res[:n]), list(res[n:])


def _gather_finish(shards, lands):
    n = len(shards)

    def body(*refs):
        own, outs = refs[:n], refs[2 * n:3 * n]
        send_sems, recv_sems = refs[3 * n:]
        mx, my, mc = lax.axis_index("x"), lax.axis_index("y"), lax.axis_index("c")
        j_me = 2 * mx + my
        sibling = (mx, my, 1 - mc)
        copies = []

        def push(w, k, src, dst):
            cp = pltpu.make_async_remote_copy(src_ref=src, dst_ref=dst, send_sem=send_sems.at[w, k],
                                              recv_sem=recv_sems.at[w, k], device_id=sibling, device_id_type=MESH)
            cp.start()
            copies.append(cp)

        for w in range(n):
            h = shards[w].shape[0] // 2
            push(w, 3, own[w], outs[w].at[j_me])
            for k, (px, py) in enumerate([(1 - mx, my), (mx, 1 - my), (1 - mx, 1 - my)]):
                got = outs[w].at[2 * px + py, pl.ds(mc * h, h), :]
                push(w, k, got, got)
        for cp in copies:
            cp.wait()

    return pl.pallas_call(
        body, out_shape=[jax.ShapeDtypeStruct(l.shape, l.dtype) for l in lands],
        in_specs=_hbm_specs(2 * n), out_specs=_hbm_specs(n), input_output_aliases={n + w: w for w in range(n)},
        scratch_shapes=[pltpu.SemaphoreType.DMA((n, 4)), pltpu.SemaphoreType.DMA((n, 4))], name="gather_finish",
    )(*shards, *lands)


def _tile_rows(h, c, itemsize, mult):
    best = h
    for t in range(mult, h + 1, mult):
        if h % t == 0 and t * c * itemsize <= (1 << 21):
            best = t
    return best


def _add_pair(g, land, place, name):
    _, h, c = land.shape
    t = _tile_rows(h, c, 2, 16)
    nb = h // t
    return _ew(lambda ids, u, v: (u.astype(F32) + v.astype(F32),), (4, nb),
               [(g, pl.BlockSpec((None, t, c), lambda j, i, s: (j, s[1] * nb + i, 0))),
                (land, pl.BlockSpec((None, t, c), lambda j, i, s: (j, i, 0)))],
               [(land.shape, BF16, pl.BlockSpec((None, t, c), lambda j, i, s: (j, i, 0)), None)], name, scalars=place)[0]


def _add_chips(own, land, place, name):
    _, h, c = land.shape
    t = _tile_rows(h, c, 4, 16)
    nb = h // t

    def fn(ids, a, b):
        return (((a.astype(F32) + b[0].astype(F32)) + b[1].astype(F32)) + b[2].astype(F32),)

    return _ew(fn, (nb,), [(own, pl.BlockSpec((None, t, c), lambda i, s: (s[0], i, 0))),
                           (land, pl.BlockSpec((3, t, c), lambda i, s: (0, i, 0)))],
               [((2 * h, c), F32, pl.BlockSpec((t, c), lambda i, s: (s[1] * nb + i, 0)), None)], name, scalars=place)[0]


W_IN_SEGMENTS = ((0, 256, KV0), (256, 288, KR0 + 64), (288, 672, Q0), (672, 1184, CX0), (1184, 1696, CB0),
                 (1696, 2208, CC0), (2208, 3232, GA0), (3232, 4256, GC0))
W_IN_SHARD = 1064


W_IN_SHARD_PAD = 1088


def _w_in_t_p_from_shards(s):
    pieces = []
    for o0, o1, p0 in sorted(W_IN_SEGMENTS, key=lambda t: t[2]):
        if p0 == KR0 + 64:
            pieces.append(jnp.zeros((64, s.shape[2]), s.dtype))
        for j in range(4):
            lo, hi = max(o0, j * W_IN_SHARD), min(o1, (j + 1) * W_IN_SHARD)
            if lo < hi:
                pieces.append(s[j, lo - j * W_IN_SHARD:hi - j * W_IN_SHARD])
    pieces.append(jnp.zeros((32, s.shape[2]), s.dtype))
    return jnp.concatenate(pieces, axis=0)


def _w_in_t_shards_from_p(g):
    shards = []
    for j in range(4):
        pieces = []
        for o0, o1, p0 in W_IN_SEGMENTS:
            lo, hi = max(o0, j * W_IN_SHARD), min(o1, (j + 1) * W_IN_SHARD)
            if lo < hi:
                pieces.append(g[p0 + lo - o0:p0 + hi - o0])
        pieces.append(jnp.zeros((W_IN_SHARD_PAD - W_IN_SHARD, g.shape[1]), g.dtype))
        shards.append(jnp.concatenate(pieces, axis=0))
    return jnp.stack(shards, axis=0)


def _cols_from_shards(s):
    return jnp.transpose(s, (1, 0, 2)).reshape(s.shape[1], -1)


def _rope_tables(T, TT, inverse):
    rows = T // GRID_W
    row = jnp.repeat(jnp.arange(rows), GRID_W).astype(F32)
    col = jnp.tile(jnp.arange(GRID_W), rows).astype(F32)
    inv = ROPE_THETA ** (-jnp.arange(0, 16, 2, dtype=F32) / 16)
    ang = jnp.concatenate([row[:, None] * inv, col[:, None] * inv], axis=-1)
    cos, sin = jnp.cos(ang), jnp.sin(ang)
    lane = jnp.arange(32)
    src = (lane // 16) * 8 + lane % 8
    lo = ((lane % 16) // 8 == 0).astype(F32)
    sgn = -1.0 if inverse else 1.0
    cos32 = cos[:, src]
    sin_lo32 = -sgn * sin[:, src] * lo
    sin_hi32 = sgn * sin[:, src] * (1.0 - lo)

    def widen(t32, fill):
        t = jnp.concatenate([jnp.full((T, 64), fill, F32), t32, jnp.full((T, 32), fill, F32)], axis=1)
        return jnp.concatenate([t, jnp.full((TT - T, HEAD_PAD), fill, F32)], axis=0)

    return widen(cos32, 1.0), widen(sin_lo32, 0.0), widen(sin_hi32, 0.0)


def _local_step(xx, tgt, mod_lat, mod_ctx, W, late_weights, early_grads, early_continue):
    TT = xx.shape[0]
    T = tgt.shape[0]
    n_lat, n_all = T // ROW_TILE, TT // ROW_TILE
    sh1, sc1, g1, sh2, sc2, g2 = [mod_lat[:, k * D_MODEL:(k + 1) * D_MODEL] for k in range(6)]
    csh1, csc1 = mod_ctx[:, :D_MODEL], mod_ctx[:, D_MODEL:2 * D_MODEL]
    vec = lambda n: _full((1, n))
    row_out = lambda n, dt, rows=T: ((rows, n), dt, _rows(n), None)
    acc_out = lambda n: ((1, n), F32, _full((1, n)), 0)

    def f_norm1(ids, x, g, a_sh, a_sc, b_sh, b_sc):
        ctx = ids[0] >= n_lat
        sh, sc = jnp.where(ctx, b_sh, a_sh), jnp.where(ctx, b_sc, a_sc)
        return ((x * _rms(x) * g) * (1.0 + sc) + sh,)

    (hh,) = _ew(f_norm1, (n_all,), [(xx, _rows(D_MODEL)), (W["norm1_g"], vec(D_MODEL)), (sh1, vec(D_MODEL)),
                                   (sc1, vec(D_MODEL)), (csh1, vec(D_MODEL)), (csc1, vec(D_MODEL))],
                [row_out(D_MODEL, BF16, TT)], "norm1_fwd")
    tm_all = _pick(TT, (768, 256))
    pp = _mm(hh, W["w_in_t"], "nt", TT, P_COLS, D_MODEL, tm=tm_all, tn=2176, tk=D_MODEL, name="w_in_fwd")

    def f_lowrank(ids, ckv, cq, gkv, gq):
        return ckv * _rms(ckv) * gkv, cq * _rms(cq) * gq

    nkv, nq = _ew(f_lowrank, (n_all,), [(pp, _rows(KV_RANK, KV0 // KV_RANK)), (pp, _rows(Q_RANK, Q0 // Q_RANK)),
                                       (W["kv_norm_g"], vec(KV_RANK)), (W["q_norm_g"], vec(Q_RANK))],
                  [row_out(KV_RANK, BF16, TT), row_out(Q_RANK, BF16, TT)], "lowrank_norm_fwd")
    kv = _mm(nkv, W["w_ukv"], "nn", TT, 1024, KV_RANK, tm=tm_all, tn=256, tk=KV_RANK, name="w_ukv_fwd",
             b_spec=pl.BlockSpec((None, KV_RANK, 256), lambda i, j, k: (j, k, 0)))
    q_raw = _mm(nq, W["w_uq_t"], "nt", TT, 1024, Q_RANK, tm=tm_all, tn=1024, tk=Q_RANK, name="w_uq_fwd")

    tabs = _rope_tables(T, TT, inverse=False)
    tabs_inv = _rope_tables(T, TT, inverse=True)
    o_pad, lse = _attn_fwd(q_raw, kv, pp, tabs, T, TT)
    W = dict(W, **late_weights(o_pad))
    tm_lat = _pick(T, (1024, 512, 256))
    ya = _mm(o_pad, W["w_attn_out"], "nn", T, D_MODEL, 1024, tm=tm_lat, tn=D_MODEL, tk=1024, name="w_attn_out_fwd")

    tc = 256
    colT = lambda blk0: pl.BlockSpec((T, tc), lambda j: (0, blk0 + j))

    def f_conv(ids, xin, cb, cc, w, b):
        return (cb * _conv(cc * xin, w, b),)

    (e,) = _ew(f_conv, (CONV_DIM // tc,),
               [(pp, colT(CX0 // tc)), (pp, colT(CB0 // tc)), (pp, colT(CC0 // tc)),
                (W["conv_w"], pl.BlockSpec((3, tc), lambda j: (0, j))), (W["conv_b"], pl.BlockSpec((1, tc), lambda j: (0, j)))],
               [((T, CONV_DIM), BF16, colT(0), None)], "conv_fwd")
    yc = _mm(e, W["w_conv_out"], "nn", T, D_MODEL, CONV_DIM, tm=tm_lat, tn=256, tk=CONV_DIM, name="w_conv_out_fwd",
             b_spec=pl.BlockSpec((None, CONV_DIM, 256), lambda i, j, k: (j, k, 0)))

    def f_merge(ids, ga, gc, a, c):
        return (_sigmoid(ga) * a + _sigmoid(gc) * c,)

    (mrg,) = _ew(f_merge, (n_lat,), [(pp, _rows(D_MODEL, 0)), (pp, _rows(D_MODEL, 1)), (ya, _rows(D_MODEL)),
                                    (yc, _rows(D_MODEL))], [row_out(D_MODEL, BF16)], "merge_fwd")
    mo = _mm(mrg, W["w_o"], "nn", T, D_MODEL, D_MODEL, tm=tm_lat, tn=D_MODEL, tk=D_MODEL, name="w_o_fwd")

    def f_norm2(ids, x, m, gate, g, sh, sc):
        x1 = x + gate * m
        return x1, (x1 * _rms(x1) * g) * (1.0 + sc) + sh

    x1, h2 = _ew(f_norm2, (n_lat,), [(xx, _rows(D_MODEL)), (mo, _rows(D_MODEL)), (g1, vec(D_MODEL)),
                                    (W["norm2_g"], vec(D_MODEL)), (sh2, vec(D_MODEL)), (sc2, vec(D_MODEL))],
                 [row_out(D_MODEL, F32), row_out(D_MODEL, BF16)], "norm2_fwd")
    up = _mm(h2, W["w_up"], "nn", T, 2 * D_FF, D_MODEL, tm=tm_lat, tn=1408, tk=D_MODEL, name="w_up_fwd",
             b_spec=pl.BlockSpec((None, D_MODEL, 1408), lambda i, j, k: (j, k, 0)))

    n_ff = D_FF // tc
    ffw = lambda off, n=3: pl.BlockSpec((n, tc), lambda j: (0, j + off))

    def f_ffn(ids, ug, uv, wg, wv, bg, bv):
        gate, val = _conv(ug, wg, bg), _conv(uv, wv, bv)
        return (gate * _sigmoid(gate) * val,)

    (act,) = _ew(f_ffn, (n_ff,), [(up, colT(0)), (up, colT(n_ff)), (W["ffn_conv_w"], ffw(0)), (W["ffn_conv_w"], ffw(n_ff)),
                                 (W["ffn_conv_b"], ffw(0, 1)), (W["ffn_conv_b"], ffw(n_ff, 1))],
                 [((T, D_FF), BF16, colT(0), None)], "ffn_act_fwd")
    f = _mm(act, W["w_down"], "nn", T, D_MODEL, D_FF, tm=tm_lat, tn=D_MODEL, tk=D_FF, name="w_down_fwd")

    def f_head(ids, x1_, f_, gate, gf, t):
        x2 = x1_ + gate * f_
        r = _rms(x2)
        xn = x2 * r
        err = xn * gf - t
        loss = 0.5 * jnp.sum(jnp.mean(err * err, axis=-1, keepdims=True))
        dy = err * (1.0 / D_MODEL)
        dx2 = _rms_bwd(dy * gf, xn, r)
        return dx2, dx2 * gate, _colsum(dy * xn), _colsum(dx2 * f_), jnp.full((1, 128), loss, F32)

    dx2, df, dg_f, dg2, loss = _ew(
        f_head, (n_lat,), [(x1, _rows(D_MODEL)), (f, _rows(D_MODEL)), (g2, vec(D_MODEL)), (W["final_g"], vec(D_MODEL)),
                           (tgt, _rows(D_MODEL))],
        [row_out(D_MODEL, F32), row_out(D_MODEL, BF16), acc_out(D_MODEL), acc_out(D_MODEL), acc_out(128)], "loss_head")

    d_w_down = _mm(act, df, "tn", D_FF, D_MODEL, T, tm=1408, tn=D_MODEL, tk=T, name="w_down_dw",
                   out_dtype=BF16).reshape(4, D_FF // 4, D_MODEL)
    da = _mm(df, W["w_down"], "nt", T, D_FF, D_MODEL, tm=tm_lat, tn=1408, tk=D_MODEL, name="w_down_dx")

    tcb = 128
    n_fb = D_FF // tcb
    colb = lambda blk0: pl.BlockSpec((T, tcb), lambda j: (0, blk0 + j))
    ffwb = lambda off, n=3: pl.BlockSpec((n, tcb), lambda j: (0, j + off))
    cvec = ((1, D_FF), F32, pl.BlockSpec((1, tcb), lambda j: (0, j)), None)

    def f_ffn_bwd(ids, ug, uv, d_act, wg, wv, bg, bv):
        sg, sv = _shifts(ug), _shifts(uv)
        gate, val = _conv(ug, wg, bg, sg), _conv(uv, wv, bv, sv)
        s = _sigmoid(gate)
        d_gate = d_act * val * s * (1.0 + gate * (1.0 - s))
        d_val = d_act * gate * s
        wg0, wg1, wg2 = _conv_bwd_w(d_gate, ug, sg)
        wv0, wv1, wv2 = _conv_bwd_w(d_val, uv, sv)
        d_up = [_conv_bwd_x(d_gate, wg), _conv_bwd_x(d_val, wv)]
        return d_up, [_colsum(d_gate), _colsum(d_val), wg0, wg1, wg2, wv0, wv1, wv2]

    d_up3, ffn_stats = _ew(
        f_ffn_bwd, (n_fb,),
        [(up, colb(0)), (up, colb(n_fb)), (da, colb(0)), (W["ffn_conv_w"], ffwb(0)), (W["ffn_conv_w"], ffwb(n_fb)),
         (W["ffn_conv_b"], ffwb(0, 1)), (W["ffn_conv_b"], ffwb(n_fb, 1))],
        [((2, T, D_FF), BF16, pl.BlockSpec((2, T, tcb), lambda j: (0, 0, j)), None),
         ((n_fb, 8, 1, tcb), F32, pl.BlockSpec((None, 8, 1, tcb), lambda j: (j, 0, 0, 0)), None)], "ffn_act_bwd")
    stat = lambda s: ffn_stats[:, s, 0, :].reshape(1, D_FF)
    d_ffn_conv_b = jnp.concatenate([stat(0), stat(1)], axis=1)
    d_ffn_conv_w = jnp.concatenate([jnp.concatenate([stat(2), stat(3), stat(4)], axis=0),
                                    jnp.concatenate([stat(5), stat(6), stat(7)], axis=0)], axis=1)

    tk_t = T
    d_w_up = _mm(h2, d_up3, "tn", D_MODEL, 2 * D_FF, T, tm=D_MODEL, tn=1408, tk=tk_t, name="w_up_dw", out_dtype=BF16,
                 b_spec=pl.BlockSpec((None, tk_t, 1408), lambda i, j, k: (j // 2, k, j % 2)),
                 o_spec=pl.BlockSpec((None, D_MODEL, 1408), lambda i, j, k: (j, i, 0)), out_shape=(4, D_MODEL, 1408))
    dh2 = _mm(d_up3, W["w_up"], "nt", T, D_MODEL, 2 * D_FF, tm=tm_lat, tn=D_MODEL, tk=1408, name="w_up_dx",
              a_spec=pl.BlockSpec((None, tm_lat, 1408), lambda i, j, k: (k // 2, i, k % 2)),
              b_spec=pl.BlockSpec((None, D_MODEL, 1408), lambda i, j, k: (k, j, 0)))

    def f_norm2_bwd(ids, dx2_, dh, x1_, m, g, sc, gate):
        r = _rms(x1_)
        xn = x1_ * r
        dx1 = dx2_ + _rms_bwd(dh * g * (1.0 + sc), xn, r)
        return dx1, dx1 * gate, _colsum(dh), _colsum(dh * xn * g), _colsum(dh * xn * (1.0 + sc)), _colsum(dx1 * m)

    dx1, dmo, dsh2, dsc2, dg_n2, dg1 = _ew(
        f_norm2_bwd, (n_lat,), [(dx2, _rows(D_MODEL)), (dh2, _rows(D_MODEL)), (x1, _rows(D_MODEL)), (mo, _rows(D_MODEL)),
                                (W["norm2_g"], vec(D_MODEL)), (sc2, vec(D_MODEL)), (g1, vec(D_MODEL))],
        [row_out(D_MODEL, F32), row_out(D_MODEL, BF16)] + [acc_out(D_MODEL)] * 4, "norm2_bwd")
    d_w_o = _mm(mrg, dmo, "tn", D_MODEL, D_MODEL, T, tm=D_MODEL, tn=D_MODEL, tk=tk_t, name="w_o_dw",
                out_dtype=BF16).reshape(4, D_MODEL // 4, D_MODEL)
    dmrg = _mm(dmo, W["w_o"], "nt", T, D_MODEL, D_MODEL, tm=tm_lat, tn=D_MODEL, tk=D_MODEL, name="w_o_dx")
    dmrg = early_grads("late", {"w_o": d_w_o, "w_up": d_w_up, "w_down": d_w_down}, dmrg, split=True)

    def f_merge_bwd(ids, dm, ga, gc, a, c):
        sa, sc_ = _sigmoid(ga), _sigmoid(gc)
        return dm * sa, dm * sc_, dm * a * sa * (1.0 - sa), dm * c * sc_ * (1.0 - sc_)

    dya, dyc, dp_ga, dp_gc = _ew(
        f_merge_bwd, (n_lat,), [(dmrg, _rows(D_MODEL)), (pp, _rows(D_MODEL, 0)), (pp, _rows(D_MODEL, 1)),
                                (ya, _rows(D_MODEL)), (yc, _rows(D_MODEL))], [row_out(D_MODEL, BF16)] * 4, "merge_bwd")
    dya = early_continue("late", dya)

    d_w_ao_p = _mm(o_pad, dya, "tn", 1024, D_MODEL, T, tm=1024, tn=D_MODEL, tk=tk_t, name="w_attn_out_dw", out_dtype=BF16)
    do_pad = _mm(dya, W["w_attn_out"], "nt", T, 1024, D_MODEL, tm=tm_lat, tn=1024, tk=D_MODEL, name="w_attn_out_dx")
    d_w_co = _mm(e, dyc, "tn", CONV_DIM, D_MODEL, T, tm=CONV_DIM, tn=256, tk=tk_t, name="w_conv_out_dw", out_dtype=BF16,
                 o_spec=pl.BlockSpec((None, CONV_DIM, 256), lambda i, j, k: (j, i, 0)), out_shape=(4, CONV_DIM, 256))
    de = _mm(dyc, W["w_conv_out"], "nt", T, CONV_DIM, D_MODEL, tm=tm_lat, tn=CONV_DIM, tk=256, name="w_conv_out_dx",
             b_spec=pl.BlockSpec((None, CONV_DIM, 256), lambda i, j, k: (k, j, 0)))

    def f_conv_bwd(ids, xin, cb, cc, d_e, w, b):
        z = cc * xin
        sz = _shifts(z)
        cz = _conv(z, w, b, sz)
        dcz = d_e * cb
        w0, w1, w2 = _conv_bwd_w(dcz, z, sz)
        dz = _conv_bwd_x(dcz, w)
        return dz * cc, d_e * cz, dz * xin, _colsum(dcz), w0, w1, w2

    cvec_c = ((1, CONV_DIM), F32, pl.BlockSpec((1, tc), lambda j: (0, j)), None)
    conv_b = _ew(f_conv_bwd, (CONV_DIM // tc,),
                 [(pp, colT(CX0 // tc)), (pp, colT(CB0 // tc)), (pp, colT(CC0 // tc)), (de, colT(0)),
                  (W["conv_w"], pl.BlockSpec((3, tc), lambda j: (0, j))), (W["conv_b"], pl.BlockSpec((1, tc), lambda j: (0, j)))],
                 [((T, CONV_DIM), BF16, colT(0), None)] * 3 + [cvec_c] * 4, "conv_bwd")
    dp_cx, dp_cb, dp_cc, d_conv_b = conv_b[:4]
    d_conv_w = jnp.concatenate(conv_b[4:7], axis=0)

    dq_raw, dkv, dp_kr = _attn_bwd(q_raw, kv, pp, o_pad, do_pad, lse, tabs, tabs_inv, T, TT)

    tk_a = TT
    d_w_uq_t = _mm(nq, dq_raw, "tn", Q_RANK, 1024, T, tm=Q_RANK, tn=1024, tk=T, name="w_uq_dw", transpose_out=True)
    dnq = _mm(dq_raw, W["w_uq_t"], "nn", T, Q_RANK, 1024, tm=tm_lat, tn=Q_RANK, tk=1024, name="w_uq_dx")
    d_w_ukv = _mm(nkv, dkv, "tn", KV_RANK, 1024, TT, tm=KV_RANK, tn=256, tk=tk_a, name="w_ukv_dw", out_dtype=BF16,
                  o_spec=pl.BlockSpec((None, KV_RANK, 256), lambda i, j, k: (j, i, 0)), out_shape=(4, KV_RANK, 256))
    dnkv = _mm(dkv, W["w_ukv"], "nt", TT, KV_RANK, 1024, tm=tm_all, tn=KV_RANK, tk=256, name="w_ukv_dx",
               b_spec=pl.BlockSpec((None, KV_RANK, 256), lambda i, j, k: (k, j, 0)))
    dnkv = early_grads("mid", {
        "w_attn_out": jnp.transpose(d_w_ao_p.reshape(N_HEADS, HEAD_PAD, 4, 256)[:, 64:], (2, 0, 1, 3)).reshape(
            4, N_HEADS * 64, 256),
        "w_conv_out": d_w_co,
        "w_uq": d_w_uq_t.reshape(4, 2, HEAD_PAD, Q_RANK)[:, :, :QK_DIM].reshape(4, 2 * QK_DIM, Q_RANK).astype(BF16),
        "w_ukv": d_w_ukv}, dnkv)

    def f_lowrank_bwd(ids, ckv, cq, dkv_, dq_, gkv, gq, ga, gc, cx, cb, cc, kr):
        rk, rq = _rms(ckv), _rms(cq)
        nk, nq_ = ckv * rk, cq * rq
        lat = ids[0] < n_lat
        dq_ = jnp.where(lat, dq_, 0.0)
        pieces = [jnp.where(lat, a, jnp.zeros_like(a)) for a in (ga, gc, cx, cb, cc)]
        pieces += [_rms_bwd(dkv_ * gkv, nk, rk).astype(BF16), _rms_bwd(dq_ * gq, nq_, rq).astype(BF16), kr.astype(BF16)]
        return jnp.concatenate(pieces, axis=1), _colsum(dkv_ * nk), _colsum(dq_ * nq_)

    lat_rows = lambda n: pl.BlockSpec((ROW_TILE, n), lambda i: (jnp.minimum(i, n_lat - 1), 0))
    dpp, dg_kv, dg_q = _ew(
        f_lowrank_bwd, (n_all,), [(pp, _rows(KV_RANK, KV0 // KV_RANK)), (pp, _rows(Q_RANK, Q0 // Q_RANK)),
                                  (dnkv, _rows(KV_RANK)), (dnq, lat_rows(Q_RANK)), (W["kv_norm_g"], vec(KV_RANK)),
                                  (W["q_norm_g"], vec(Q_RANK)), (dp_ga, lat_rows(D_MODEL)), (dp_gc, lat_rows(D_MODEL)),
                                  (dp_cx, lat_rows(CONV_DIM)), (dp_cb, lat_rows(CONV_DIM)), (dp_cc, lat_rows(CONV_DIM)),
                                  (dp_kr, _rows(HEAD_PAD))],
        [row_out(P_COLS, BF16, TT), acc_out(KV_RANK), acc_out(Q_RANK)], "lowrank_norm_bwd")
    d_w_in_t = _mm(hh, dpp, "tn", D_MODEL, P_COLS, TT, tm=512, tn=2176, tk=TT, name="w_in_dw", out_dtype=BF16,
                   transpose_out=True)
    dhh = _mm(dpp, W["w_in_t"], "nn", TT, D_MODEL, P_COLS, tm=tm_all, tn=512, tk=2176, name="w_in_dx")

    def f_norm1_bwd(ids, x, dh, dres, g, sc):
        r = _rms(x)
        xn = x * r
        return (dres + _rms_bwd(dh * g * (1.0 + sc), xn, r), _colsum(dh), _colsum(dh * xn * g),
                _colsum(dh * xn * (1.0 + sc)))

    grad_x, dsh1, dsc1, dg_n1 = _ew(
        f_norm1_bwd, (n_lat,), [(xx, _rows(D_MODEL)), (dhh, _rows(D_MODEL)), (dx1, _rows(D_MODEL)),
                                (W["norm1_g"], vec(D_MODEL)), (sc1, vec(D_MODEL))],
        [row_out(D_MODEL, F32)] + [acc_out(D_MODEL)] * 3, "norm1_bwd")

    def f_norm1_ctx_bwd(ids, x, dh, g, sc):
        xn = x * _rms(x)
        return _colsum(dh), _colsum(dh * xn * g), _colsum(dh * xn * (1.0 + sc))

    n_ctx = n_all - n_lat
    dcsh1, dcsc1, dg_n1c = _ew(
        f_norm1_ctx_bwd, (n_ctx,), [(xx, _rows(D_MODEL, 0, n_lat)), (dhh, _rows(D_MODEL, 0, n_lat)),
                                    (W["norm1_g"], vec(D_MODEL)), (csc1, vec(D_MODEL))], [acc_out(D_MODEL)] * 3,
        "norm1_ctx_bwd")

    big = {"w_in": _w_in_t_shards_from_p(d_w_in_t).astype(BF16)}
    zero = jnp.zeros((1, 4 * D_MODEL), F32)
    small = {
        "dmod_lat": jnp.concatenate([dsh1, dsc1, dg1, dsh2, dsc2, dg2], axis=1),
        "dmod_ctx": jnp.concatenate([dcsh1, dcsc1, zero], axis=1),
        "norm1_g": dg_n1 + dg_n1c, "norm2_g": dg_n2, "final_g": dg_f, "q_norm_g": dg_q, "kv_norm_g": dg_kv,
        "conv_b": d_conv_b, "conv_w": d_conv_w.reshape(1, -1), "ffn_conv_b": d_ffn_conv_b,
        "ffn_conv_w": d_ffn_conv_w.reshape(1, -1),
    }
    return grad_x, loss, big, small


SMALL = (("dmod_lat", 6144), ("dmod_ctx", 6144), ("norm1_g", 1024), ("norm2_g", 1024), ("final_g", 1024),
         ("q_norm_g", 384), ("kv_norm_g", 256), ("conv_b", 512), ("conv_w", 1536), ("ffn_conv_b", 5632),
         ("ffn_conv_w", 16896), ("loss", 128))
SMALL_ROWS = 320


def _adam_update(w, g, m, v):
    c1, c2 = 1.0 - ADAM_B1 ** ADAM_STEP, 1.0 - ADAM_B2 ** ADAM_STEP
    m2 = ADAM_B1 * m + (1.0 - ADAM_B1) * g
    v2 = ADAM_B2 * v + (1.0 - ADAM_B2) * (g * g)
    return [-ADAM_LR * ((m2 / c1) / (jnp.sqrt(v2 / c2) + ADAM_EPS) + ADAM_WD * w), m2, v2]


def _adamw(w, g, m, v, name):
    R, C = w.shape
    tr = 8 if R % 8 == 0 else R
    for t in range(8, R + 1, 8):
        if R % t == 0 and t * C * 4 <= (1 << 20):
            tr = t
    spec = pl.BlockSpec((tr, C), lambda i: (i, 0))
    return _ew(lambda ids, *vals: _adam_update(*vals), (R // tr,), [(w, spec), (g, spec), (m, spec), (v, spec)],
               [((R, C), F32, spec, None)] * 3, name)


def kernel(x, c, ctx, c_ctx, w_ada, b_ada, norm1_g, w_in, q_norm_g, kv_norm_g, w_uq, w_ukv, conv_w, conv_b, w_attn_out, w_conv_out, w_o, norm2_g, w_up, ffn_conv_w, ffn_conv_b, w_down, final_g, loss_target, m_c_ctx, m_w_ada, m_b_ada, m_norm1_g, m_w_in, m_q_norm_g, m_kv_norm_g, m_w_uq, m_w_ukv, m_conv_w, m_conv_b, m_w_attn_out, m_w_conv_out, m_w_o, m_norm2_g, m_w_up, m_ffn_conv_w, m_ffn_conv_b, m_w_down, m_final_g, v_c_ctx, v_w_ada, v_b_ada, v_norm1_g, v_w_in, v_q_norm_g, v_kv_norm_g, v_w_uq, v_w_ukv, v_conv_w, v_conv_b, v_w_attn_out, v_w_conv_out, v_w_o, v_norm2_g, v_w_up, v_ffn_conv_w, v_ffn_conv_b, v_w_down, v_final_g):
    mx, my, mc = lax.axis_index("x"), lax.axis_index("y"), lax.axis_index("c")
    chip = 2 * mx + my
    dev = 4 * mx + 2 * my + mc
    T, Tc = x.shape[1], ctx.shape[1]
    TT = T + Tc
    w_in_t, m_w_in_t, v_w_in_t = (jnp.transpose(a[0]) for a in (w_in, m_w_in, v_w_in))
    w_uq_t, m_w_uq_t, v_w_uq_t = (jnp.transpose(a[0]) for a in (w_uq, m_w_uq, v_w_uq))
    conv_sh = jnp.concatenate([conv_w[0], ffn_conv_w[0]], axis=1)
    pay1 = jnp.concatenate([jnp.pad(c, ((0, 7), (0, 0))), jnp.pad(conv_sh, ((0, 5), (0, 0)))], axis=1)
    c_send, c_recv, c_src, c_land, zero0 = _ici_start("all", [pay1], [(8, 8, 2560)], jnp.zeros((8, 128), F32),
                                                      "cond_start")
    w_in_bf = (jnp.pad(w_in_t, ((0, W_IN_SHARD_PAD - W_IN_SHARD), (0, 0))) + zero0[0, 0]).astype(BF16)
    shards = {"w_in": w_in_bf, "w_uq": w_uq_t, "w_ukv": w_ukv[0], "w_attn_out": w_attn_out[0],
              "w_conv_out": w_conv_out[0], "w_o": w_o[0], "w_up": w_up[0], "w_down": w_down[0]}
    (pay1,), (c_land,) = _ici_wait("all", c_send, c_recv, c_src, c_land, shards["w_in"], "cond_wait")
    got1 = lax.dynamic_update_slice(c_land, pay1[None], (dev, 0, 0))
    c_all = got1[:, 0, :D_MODEL]
    conv_all = got1[0::2, :3, D_MODEL:]
    conv_w_full = _cols_from_shards(conv_all[:, :, :128])
    ffn_conv_w_full = _cols_from_shards(conv_all[:, :, 128:])

    cond = jnp.concatenate([c_all, c_ctx.reshape(1, D_MODEL), jnp.zeros((7, D_MODEL), F32)], axis=0)

    def f_silu(ids, v):
        return (v * _sigmoid(v),)

    (s16,) = _ew(f_silu, (1,), [(cond, _full((16, D_MODEL)))], [((16, D_MODEL), F32, _full((16, D_MODEL)), None)], "silu_cond")
    mod_sh = _mm(s16, w_ada[0], "nn", 16, 1536, D_MODEL, tm=16, tn=768, tk=D_MODEL, name="w_ada_fwd")
    m_send, m_recv, m_src, m_land, zero1 = _ici_start("all", [mod_sh], [(8, 16, 1536)], jnp.zeros((8, 128), F32),
                                                      "mod_start")
    shards["w_ukv"] = w_ukv[0] + zero1[0, 0]

    names = [n for n, _ in BIG]
    first = [n for n in names if n not in GATHER_LATE]
    gathered, zero = _gather_weights([shards[n].astype(BF16) for n in first])
    full = dict(zip(first, gathered))
    (mod_mine,), (m_land,) = _ici_wait("all", m_send, m_recv, m_src, m_land, gathered[0], "mod_wait")
    got2 = lax.dynamic_update_slice(m_land, mod_mine[None], (dev, 0, 0))
    mod_all = _cols_from_shards(got2[0::2]) + b_ada
    mod_lat = lax.dynamic_slice_in_dim(mod_all, dev, 1, axis=0)
    mod_ctx = mod_all[8:9]
    xx = jnp.concatenate([x[0], ctx[0]], axis=0)
    late_bf = [(shards[n] + zero[0, 0]).astype(BF16) for n in GATHER_LATE]
    g_send, g_recv, late_src, late_land, xx = _ici_start(
        "gather", late_bf, [(4,) + s.shape for s in late_bf], xx, "gather_late_start")

    def late_weights(after):
        src, land = _ici_wait("gather", g_send, g_recv, late_src, late_land, after, "gather_late_wait")
        got = dict(zip(GATHER_LATE, _gather_finish(src, land)))
        wao = _cols_from_shards(got["w_attn_out"]).reshape(N_HEADS, 64, D_MODEL)
        return {"w_attn_out": jnp.pad(wao, ((0, 0), (64, 0), (0, 0))).reshape(N_HEADS * HEAD_PAD, D_MODEL),
                "w_conv_out": got["w_conv_out"], "w_o": got["w_o"].reshape(D_MODEL, D_MODEL), "w_up": got["w_up"],
                "w_down": got["w_down"].reshape(D_FF, D_MODEL)}

    wuq_t = full["w_uq"].reshape(N_HEADS, QK_DIM, Q_RANK)
    W = {
        "w_in_t": _w_in_t_p_from_shards(full["w_in"]),
        "w_uq_t": jnp.pad(wuq_t, ((0, 0), (0, HEAD_PAD - QK_DIM), (0, 0))).reshape(N_HEADS * HEAD_PAD, Q_RANK),
        "w_ukv": full["w_ukv"],
        "norm1_g": norm1_g, "norm2_g": norm2_g, "final_g": final_g.reshape(1, D_MODEL), "q_norm_g": q_norm_g,
        "kv_norm_g": kv_norm_g, "conv_w": conv_w_full, "conv_b": conv_b, "ffn_conv_w": ffn_conv_w_full,
        "ffn_conv_b": ffn_conv_b,
    }

    place = jnp.stack([chip, mc]).astype(jnp.int32)
    early = {}

    pending = {}

    def scatter(tag, group, gs, from_sib, carry):
        sums = [_add_pair(gs[w], from_sib[w], place, "rs_pair_add_" + n) for w, n in enumerate(group)]
        send, recv, sums, land, carry = _ici_start(
            "scatter", sums, [(3,) + s.shape[1:] for s in sums], carry, "rs_chips_" + tag + "_start")
        early[tag] = (group, send, recv, sums, land)
        return carry

    def early_grads(tag, g, carry, split=False):
        gs = list(g.values())
        if not split:
            return scatter(tag, list(g), gs, _rs_pair(gs, "rs_pair_" + tag), carry)
        send, recv, gs, land, carry = _ici_start(
            "pair", gs, [(4, s.shape[1] // 2, s.shape[2]) for s in gs], carry, "rs_pair_" + tag + "_start")
        pending[tag] = (list(g), send, recv, gs, land)
        return carry

    def early_continue(tag, carry):
        group, send, recv, gs, land = pending[tag]
        gs, from_sib = _ici_wait("pair", send, recv, gs, land, carry, "rs_pair_" + tag + "_wait")
        return scatter(tag, group, gs, from_sib, carry)

    grad_x, loss_part, gbig, gsmall = _local_step(xx, loss_target[0], mod_lat, mod_ctx, W, late_weights, early_grads,
                                                  early_continue)

    gsmall["loss"] = loss_part
    pay3 = jnp.concatenate([gsmall[n].reshape(-1) for n, _ in SMALL])
    pay3 = jnp.pad(pay3, (0, SMALL_ROWS * 128 - pay3.shape[0])).reshape(SMALL_ROWS, 128)
    s_send, s_recv, s_src, s_land, w_in_thru = _ici_start("all", [pay3], [(8, SMALL_ROWS, 128)], gbig["w_in"],
                                                         "small_start")
    gbig = {"w_in": w_in_thru}

    after_small = early_grads("last", gbig, s_src[0])

    (pay3,), (s_land,) = _ici_wait("all", s_send, s_recv, [after_small], s_land, early["last"][3][0], "small_wait")
    got3 = lax.dynamic_update_slice(s_land, pay3[None], (dev, 0, 0)).reshape(8 * SMALL_ROWS, 128)

    def f_sum8(ids, a):
        s = a[0:SMALL_ROWS]
        for d in range(1, 8):
            s = s + a[d * SMALL_ROWS:(d + 1) * SMALL_ROWS]
        return (s,)

    (vsum,) = _ew(f_sum8, (1,), [(got3, _full((8 * SMALL_ROWS, 128)))],
                  [((SMALL_ROWS, 128), F32, _full((SMALL_ROWS, 128)), None)], "sum_small")
    vflat = vsum.reshape(-1)
    gvec, off = {}, 0
    for n, size in SMALL:
        gvec[n] = vflat[off:off + size]
        off += size
    loss = gvec["loss"][0]
    dmod_rows = got3.reshape(8, SMALL_ROWS * 128)[:, :6 * D_MODEL]
    dm16 = jnp.concatenate([dmod_rows, gvec["dmod_ctx"].reshape(1, -1), jnp.zeros((7, 6 * D_MODEL), F32)], axis=0)

    def f_colsum(ids, a):
        return (_colsum(a),)

    (g_b_ada,) = _ew(f_colsum, (1,), [(dm16, _full((16, 6 * D_MODEL)))],
                     [((1, 6 * D_MODEL), F32, _full((1, 6 * D_MODEL)), None)], "b_ada_grad")
    dm_sh = lax.dynamic_slice_in_dim(dm16, chip * 1536, 1536, axis=1)
    g_w_ada = _mm(s16, dm_sh, "tn", D_MODEL, 1536, 16, tm=512, tn=768, tk=16, name="w_ada_dw")
    dcond_part = _mm(dm_sh, w_ada[0], "nt", 16, D_MODEL, 1536, tm=16, tn=512, tk=1536, name="w_ada_dx")
    d_send, d_recv, d_src, d_land, vsum = _ici_start("all", [dcond_part[8:16]], [(8, 8, D_MODEL)], vsum, "dcond_start")

    def finish(tags, after):
        done, own, lands = [], [], []
        for tag in tags:
            tag_names, send, recv, sums, land = early[tag]
            sums, land = _ici_wait("scatter", send, recv, sums, land, after, "rs_chips_" + tag + "_wait")
            done, own, lands = done + tag_names, own + sums, lands + land
        halves = [_add_chips(a, b, place, "rs_chip_add_" + n) for a, b, n in zip(own, lands, done)]
        return dict(zip(done, _rs_pair_back(halves, "rs_pair_back_" + tags[0])))

    grads, deltas, new_m, new_v = {}, {}, {}, {}

    def adam(n, w_, m_, v_, g, transposed):
        d_, m2, v2 = _adamw(w_, g, m_, v_, "adamw_" + n)
        back = (lambda a: jnp.transpose(a)[None]) if transposed else (lambda a: a[None])
        grads[n], deltas[n], new_m[n], new_v[n] = back(g[:w_.shape[0]]), back(d_), back(m2), back(v2)

    gw = finish(["late", "mid"], grad_x)
    adam("w_ada", w_ada[0], m_w_ada[0], v_w_ada[0], g_w_ada, False)
    for n, (w_, m_, v_) in {"w_ukv": (w_ukv, m_w_ukv, v_w_ukv), "w_attn_out": (w_attn_out, m_w_attn_out, v_w_attn_out),
                            "w_conv_out": (w_conv_out, m_w_conv_out, v_w_conv_out), "w_o": (w_o, m_w_o, v_w_o),
                            "w_up": (w_up, m_w_up, v_w_up), "w_down": (w_down, m_w_down, v_w_down)}.items():
        adam(n, w_[0], m_[0], v_[0], gw[n], False)
    adam("w_uq", w_uq_t, m_w_uq_t, v_w_uq_t, gw["w_uq"], True)
    gw_in = finish(["last"], deltas["w_up"])
    adam("w_in", w_in_t, m_w_in_t, v_w_in_t, gw_in["w_in"], True)

    (dcond_mine,), (d_land,) = _ici_wait("all", d_send, d_recv, d_src, d_land, deltas["w_in"], "dcond_wait")
    got4 = lax.dynamic_update_slice(d_land, dcond_mine[None], (dev, 0, 0))[0::2, 0]

    def f_c_ctx(ids, parts, cc):
        s = _sigmoid(cc)
        d = parts[0:1] + parts[1:2] + parts[2:3] + parts[3:4]
        return (d * s * (1.0 + cc * (1.0 - s)),)

    (g_c_ctx,) = _ew(f_c_ctx, (1,), [(got4, _full((4, D_MODEL))), (c_ctx.reshape(1, D_MODEL), _full((1, D_MODEL)))],
                     [((1, D_MODEL), F32, _full((1, D_MODEL)), None)], "c_ctx_grad")

    conv_w_g = lax.dynamic_slice_in_dim(gvec["conv_w"].reshape(3, CONV_DIM), chip * 128, 128, axis=1)
    ffn_conv_w_g = lax.dynamic_slice_in_dim(gvec["ffn_conv_w"].reshape(3, 2 * D_FF), chip * 1408, 1408, axis=1)
    vec_params = (("c_ctx", c_ctx, m_c_ctx, v_c_ctx, g_c_ctx), ("b_ada", b_ada, m_b_ada, v_b_ada, g_b_ada),
                  ("norm1_g", norm1_g, m_norm1_g, v_norm1_g, gvec["norm1_g"]),
                  ("q_norm_g", q_norm_g, m_q_norm_g, v_q_norm_g, gvec["q_norm_g"]),
                  ("kv_norm_g", kv_norm_g, m_kv_norm_g, v_kv_norm_g, gvec["kv_norm_g"]),
                  ("conv_w", conv_w, m_conv_w, v_conv_w, conv_w_g), ("conv_b", conv_b, m_conv_b, v_conv_b, gvec["conv_b"]),
                  ("norm2_g", norm2_g, m_norm2_g, v_norm2_g, gvec["norm2_g"]),
                  ("ffn_conv_w", ffn_conv_w, m_ffn_conv_w, v_ffn_conv_w, ffn_conv_w_g),
                  ("ffn_conv_b", ffn_conv_b, m_ffn_conv_b, v_ffn_conv_b, gvec["ffn_conv_b"]),
                  ("final_g", final_g, m_final_g, v_final_g, gvec["final_g"]))
    two_d = lambda a: a.reshape((-1, a.shape[-1]))

    def f_adam_many(ids, *vals):
        out = []
        for k in range(len(vec_params)):
            out += _adam_update(*vals[4 * k:4 * k + 4])
        return out

    ins_v, outs_v = [], []
    for p in vec_params:
        shp = two_d(p[1]).shape
        ins_v += [(two_d(a), _full(shp)) for a in (p[1], p[4], p[2], p[3])]
        outs_v += [(shp, F32, _full(shp), None)] * 3
    res_v = _ew(f_adam_many, (1,), ins_v, outs_v, "adamw_vectors")
    for k, p in enumerate(vec_params):
        n, shape = p[0], p[1].shape
        grads[n] = p[4].reshape(shape)
        deltas[n], new_m[n], new_v[n] = (r.reshape(shape) for r in res_v[3 * k:3 * k + 3])

    order = ("c_ctx", "w_ada", "b_ada", "norm1_g", "w_in", "q_norm_g", "kv_norm_g", "w_uq", "w_ukv", "conv_w", "conv_b",
             "w_attn_out", "w_conv_out", "w_o", "norm2_g", "w_up", "ffn_conv_w", "ffn_conv_b", "w_down", "final_g")
    return (loss, grad_x[None], *[grads[n] for n in order], *[deltas[n] for n in order],
            *[new_m[n] for n in order], *[new_v[n] for n in order])
```

```python
import functools

import jax
import jax.numpy as jnp
from jax import lax
from jax.experimental import pallas as pl
from jax.experimental.pallas import tpu as pltpu

F32, BF16 = jnp.float32, jnp.bfloat16
MESH = pl.DeviceIdType.MESH

D_MODEL = 1024
N_HEADS = 8
HEAD_PAD = 128
QK_DIM = 96
Q_RANK, KV_RANK = 384, 256
CONV_DIM = 512
D_FF = 2816
GRID_W = 64
ROPE_THETA = 10000.0
EPS = 1e-6
GA0, GC0, CX0, CB0, CC0, KV0, Q0, KR0, P_COLS = 0, 1024, 2048, 2560, 3072, 3584, 3840, 4224, 4352
ROW_TILE = 256
VMEM_LIMIT_BYTES = 48 * 1024 * 1024

ADAM_LR, ADAM_B1, ADAM_B2, ADAM_EPS, ADAM_WD, ADAM_STEP = 0.001, 0.9, 0.999, 1e-08, 0.01, 10

BIG = (("w_in", (1088, 1024)), ("w_uq", (192, 384)), ("w_ukv", (256, 256)), ("w_attn_out", (512, 256)),
       ("w_conv_out", (512, 256)), ("w_o", (256, 1024)), ("w_up", (1024, 1408)), ("w_down", (704, 1024)))

GATHER_LATE = ("w_attn_out", "w_conv_out", "w_o", "w_up", "w_down")

NN = (((1,), (0,)), ((), ()))
NT = (((1,), (1,)), ((), ()))
TN = (((0,), (0,)), ((), ()))


def _cp(sem):
    return pltpu.CompilerParams(dimension_semantics=sem, vmem_limit_bytes=VMEM_LIMIT_BYTES)


PIN_BYTES = 1 << 19


def _in_hbm(arrays):
    return [pltpu.with_memory_space_constraint(a, pltpu.HBM) if a.size * a.dtype.itemsize >= PIN_BYTES else a
            for a in arrays]


def _out(shape, dtype):
    n = 1
    for d in shape:
        n *= d
    big = n * jnp.dtype(dtype).itemsize >= PIN_BYTES
    return pltpu.HBM(shape, dtype) if big else jax.ShapeDtypeStruct(shape, dtype)


def _pick(n, prefs):
    for p in prefs:
        if n % p == 0:
            return p
    return n


def _mm(a, b, mode, M, N, K, *, tm, tn, tk, name, out_dtype=F32, a_spec=None, b_spec=None, o_spec=None,
        out_shape=None, transpose_out=False):
    assert M % tm == 0 and N % tn == 0 and K % tk == 0, (name, M, N, K, tm, tn, tk)
    nk = K // tk
    dims = {"nn": NN, "nt": NT, "tn": TN}[mode]
    if a_spec is None:
        a_spec = (pl.BlockSpec((tk, tm), lambda i, j, k: (k, i)) if mode == "tn"
                  else pl.BlockSpec((tm, tk), lambda i, j, k: (i, k)))
    if b_spec is None:
        b_spec = (pl.BlockSpec((tn, tk), lambda i, j, k: (j, k)) if mode == "nt"
                  else pl.BlockSpec((tk, tn), lambda i, j, k: (k, j)))
    if o_spec is None:
        o_spec = (pl.BlockSpec((tn, tm), lambda i, j, k: (j, i)) if transpose_out
                  else pl.BlockSpec((tm, tn), lambda i, j, k: (i, j)))
    if out_shape is None:
        out_shape = (N, M) if transpose_out else (M, N)

    def emit(o_ref, val):
        o_ref[...] = (val.T if transpose_out else val).astype(o_ref.dtype)

    def body(a_ref, b_ref, o_ref, *scratch):
        part = lax.dot_general(a_ref[...].astype(BF16), b_ref[...].astype(BF16), dims, preferred_element_type=F32)
        if nk == 1:
            emit(o_ref, part)
            return
        acc_ref, = scratch
        k = pl.program_id(2)

        @pl.when(k == 0)
        def _():
            acc_ref[...] = part

        @pl.when((k > 0) & (k < nk - 1))
        def _():
            acc_ref[...] += part

        @pl.when(k == nk - 1)
        def _():
            emit(o_ref, acc_ref[...] + part)

    return pl.pallas_call(
        body, grid=(M // tm, N // tn, nk), in_specs=[a_spec, b_spec], out_specs=o_spec,
        out_shape=_out(out_shape, out_dtype),
        scratch_shapes=[pltpu.VMEM((tm, tn), F32)] if nk > 1 else [],
        compiler_params=_cp(("parallel", "parallel", "arbitrary")), name=name)(*_in_hbm([a, b]))


def _ew(fn, grid, ins, outs, name, scalars=None):
    n_in = len(ins)
    n_sc = 0 if scalars is None else 1

    def store(ref, val, acc, ids):
        if isinstance(val, (list, tuple)):
            for h, v in enumerate(val):
                ref[h] = v.astype(ref.dtype)
            return
        if acc is None:
            ref[...] = val.astype(ref.dtype)
            return

        @pl.when(ids[acc] == 0)
        def _():
            ref[...] = val.astype(ref.dtype)

        @pl.when(ids[acc] > 0)
        def _():
            ref[...] += val.astype(ref.dtype)

    def body(*refs):
        refs = refs[n_sc:]
        ids = tuple(pl.program_id(a) for a in range(len(grid)))
        vals = fn(ids, *[r[...] for r in refs[:n_in]])
        for ref, val, (_, _, _, acc) in zip(refs[n_in:], vals, outs):
            store(ref, val, acc, ids)

    acc_axes = {o[3] for o in outs if o[3] is not None}
    sem = tuple("arbitrary" if a in acc_axes else "parallel" for a in range(len(grid)))
    in_specs, out_specs = [s for _, s in ins], [o[2] for o in outs]
    out_shape = [_out(o[0], o[1]) for o in outs]
    args = _in_hbm([a for a, _ in ins])
    if scalars is None:
        return pl.pallas_call(body, grid=grid, in_specs=in_specs, out_specs=out_specs, out_shape=out_shape,
                              compiler_params=_cp(sem), name=name)(*args)
    spec = pltpu.PrefetchScalarGridSpec(num_scalar_prefetch=1, grid=grid, in_specs=in_specs, out_specs=out_specs)
    return pl.pallas_call(body, grid_spec=spec, out_shape=out_shape, compiler_params=_cp(sem), name=name)(scalars, *args)


def _rows(width, cblk=0, roff=0, tr=ROW_TILE):
    return pl.BlockSpec((tr, width), lambda i: (i + roff, cblk))


def _full(shape):
    nd = len(shape)
    return pl.BlockSpec(shape, lambda *_: (0,) * nd)


def _sigmoid(x):
    return 1.0 / (1.0 + jnp.exp(-x))


def _rms(x):
    return lax.rsqrt(jnp.mean(x * x, axis=-1, keepdims=True) + EPS)


def _rms_bwd(dn, xn, r):
    return r * (dn - xn * jnp.mean(dn * xn, axis=-1, keepdims=True))


def _colsum(x):
    return jnp.sum(x, axis=0, keepdims=True)


def _shifts(x):
    n = x.shape[0]
    rows = lax.broadcasted_iota(jnp.int32, x.shape, 0)
    return jnp.where(rows == 0, 0.0, pltpu.roll(x, 1, 0)), jnp.where(rows == n - 1, 0.0, pltpu.roll(x, n - 1, 0))


def _conv(x, w, b, shifted=None):
    prev, nxt = _shifts(x) if shifted is None else shifted
    return b + prev * w[0:1] + x * w[1:2] + nxt * w[2:3]


def _conv_bwd_x(dy, w):
    prev, nxt = _shifts(dy)
    return nxt * w[0:1] + dy * w[1:2] + prev * w[2:3]


def _conv_bwd_w(dy, x, shifted):
    prev, nxt = shifted
    return _colsum(dy * prev), _colsum(dy * x), _colsum(dy * nxt)


def _rope(x, cos, sin_lo, sin_hi):
    return x * cos + pltpu.roll(x, HEAD_PAD - 8, 1) * sin_lo + pltpu.roll(x, 8, 1) * sin_hi


ATTN_SCALE = QK_DIM ** -0.5
LOG2_E = 1.4426950408889634


def _head_keys(kv_ref, kr_ref, cos_ref, slo_ref, shi_ref, kc_ref, vp_ref):
    kv = kv_ref[...]
    lane = lax.broadcasted_iota(jnp.int32, kv.shape, 1)
    kc_ref[...] = jnp.where(lane < 64, kv, _rope(kr_ref[...], cos_ref[...], slo_ref[...], shi_ref[...])).astype(BF16)
    vp_ref[...] = jnp.where(lane >= 64, kv, 0.0).astype(BF16)


ATTN_Q_TILE = 512


def _attn_specs(tq, TT):
    q = pl.BlockSpec((tq, HEAD_PAD), lambda h, i: (i, h))
    keys = pl.BlockSpec((TT, HEAD_PAD), lambda h, i: (0, h))
    kr = pl.BlockSpec((TT, HEAD_PAD), lambda h, i: (0, KR0 // HEAD_PAD))
    tab_q = pl.BlockSpec((tq, HEAD_PAD), lambda h, i: (i, 0))
    tab_k = pl.BlockSpec((TT, HEAD_PAD), lambda h, i: (0, 0))
    lse = pl.BlockSpec((None, tq, 1), lambda h, i: (h, i, 0))
    return q, keys, kr, tab_q, tab_k, lse


def _attn_fwd(q_raw, kv, pp, tabs, T, TT):
    tq = ROW_TILE
    cos, slo, shi = tabs

    def body(q_ref, kv_ref, kr_ref, cq, lq, hq, ck, lk, hk, o_ref, l_ref, kc, vp):
        @pl.when(pl.program_id(1) == 0)
        def _():
            _head_keys(kv_ref, kr_ref, ck, lk, hk, kc, vp)

        q = _rope(q_ref[...], cq[...], lq[...], hq[...]).astype(BF16)
        s = lax.dot_general(q, kc[...], NT, preferred_element_type=F32)
        m = jnp.max(s, axis=-1, keepdims=True)
        p = jnp.exp2((s - m) * (ATTN_SCALE * LOG2_E))
        l = jnp.sum(p, axis=-1, keepdims=True)
        o = lax.dot_general(p.astype(BF16), vp[...], NN, preferred_element_type=F32)
        o_ref[...] = o / l
        l_ref[...] = m * ATTN_SCALE + jnp.log(l)

    qs, keys, kr, tab_q, tab_k, lse = _attn_specs(tq, TT)
    return pl.pallas_call(
        body, grid=(N_HEADS, T // tq), in_specs=[qs, keys, kr, tab_q, tab_q, tab_q, tab_k, tab_k, tab_k],
        out_specs=[qs, lse],
        out_shape=[jax.ShapeDtypeStruct((T, N_HEADS * HEAD_PAD), F32), jax.ShapeDtypeStruct((N_HEADS, T, 1), F32)],
        scratch_shapes=[pltpu.VMEM((TT, HEAD_PAD), BF16), pltpu.VMEM((TT, HEAD_PAD), BF16)],
        compiler_params=_cp(("parallel", "arbitrary")), name="attn_fwd",
    )(*_in_hbm([q_raw, kv, pp, cos, slo, shi, cos, slo, shi]))


def _attn_bwd(q_raw, kv, pp, o, do, lse, tabs, tabs_inv, T, TT):
    tq = _pick(T, (ATTN_Q_TILE, ROW_TILE))
    nq = T // tq
    cos, slo, shi = tabs
    cos_i, slo_i, shi_i = tabs_inv

    def body(q_ref, kv_ref, kr_ref, cq, lq, hq, ck, lk, hk, iq, ilq, ihq, ik, ilk, ihk, o_ref, do_ref, l_ref,
             dq_ref, dkv_ref, dkr_ref, kc, vp, dk, dv):
        h, i = pl.program_id(0), pl.program_id(1)

        @pl.when(i == 0)
        def _():
            _head_keys(kv_ref, kr_ref, ck, lk, hk, kc, vp)
            dk[...] = jnp.zeros_like(dk)
            dv[...] = jnp.zeros_like(dv)

        q = _rope(q_ref[...], cq[...], lq[...], hq[...]).astype(BF16)
        k, v, d_o = kc[...], vp[...], do_ref[...]
        s = lax.dot_general(q, k, NT, preferred_element_type=F32)
        p = jnp.exp2(s * (ATTN_SCALE * LOG2_E) - l_ref[...] * LOG2_E)
        dob = d_o.astype(BF16)
        dp = lax.dot_general(dob, v, NT, preferred_element_type=F32)
        dd = jnp.sum(d_o * o_ref[...], axis=-1, keepdims=True)
        ds = (p * (dp - dd) * ATTN_SCALE).astype(BF16)
        dq = lax.dot_general(ds, k, NN, preferred_element_type=F32)
        dq_ref[...] = _rope(dq, iq[...], ilq[...], ihq[...]).astype(dq_ref.dtype)
        dk[...] += lax.dot_general(q, ds, TN, preferred_element_type=F32)
        dv[...] += lax.dot_general(dob, p.astype(BF16), TN, preferred_element_type=F32)

        @pl.when(i == nq - 1)
        def _():
            dkh = dk[...].T
            lane = lax.broadcasted_iota(jnp.int32, dkh.shape, 1)
            dkv_ref[...] = jnp.where(lane < 64, dkh, dv[...].T).astype(dkv_ref.dtype)
            rot = _rope(jnp.where((lane >= 64) & (lane < 96), dkh, 0.0), ik[...], ilk[...], ihk[...])

            @pl.when(h == 0)
            def _():
                dkr_ref[...] = rot

            @pl.when(h > 0)
            def _():
                dkr_ref[...] += rot

    qs, keys, kr, tab_q, tab_k, lse_spec = _attn_specs(tq, TT)
    wide = lambda rows: jax.ShapeDtypeStruct((rows, N_HEADS * HEAD_PAD), BF16)
    return pl.pallas_call(
        body, grid=(N_HEADS, nq),
        in_specs=[qs, keys, kr] + [tab_q] * 3 + [tab_k] * 3 + [tab_q] * 3 + [tab_k] * 3 + [qs, qs, lse_spec],
        out_specs=[qs, keys, pl.BlockSpec((TT, HEAD_PAD), lambda h, i: (0, 0))],
        out_shape=[wide(T), wide(TT), jax.ShapeDtypeStruct((TT, HEAD_PAD), F32)],
        scratch_shapes=[pltpu.VMEM((TT, HEAD_PAD), BF16), pltpu.VMEM((TT, HEAD_PAD), BF16),
                        pltpu.VMEM((HEAD_PAD, TT), F32), pltpu.VMEM((HEAD_PAD, TT), F32)],
        compiler_params=_cp(("arbitrary", "arbitrary")), name="attn_bwd",
    )(*_in_hbm([q_raw, kv, pp, cos, slo, shi, cos, slo, shi, cos_i, slo_i, shi_i, cos_i, slo_i, shi_i, o, do, lse]))


def _hbm_specs(n):
    return [pl.BlockSpec(memory_space=pl.ANY)] * n


def _gather_weights(shards):
    n = len(shards)
    halves = [s.shape[0] // 2 for s in shards]

    def body(*refs):
        ins, outs = refs[:n], refs[n:2 * n]
        token, send_sems, recv_sems = refs[2 * n:]
        token[...] = jnp.zeros_like(token)
        mx, my, mc = lax.axis_index("x"), lax.axis_index("y"), lax.axis_index("c")
        j_me = 2 * mx + my
        chips = [(1 - mx, my), (mx, 1 - my), (1 - mx, 1 - my)]

        def half(w, chip_idx, hc):
            return outs[w].at[chip_idx, pl.ds(hc * halves[w], halves[w]), :]

        def copy(w, k, src, dst, to):
            return pltpu.make_async_remote_copy(src_ref=src, dst_ref=dst, send_sem=send_sems.at[w, k],
                                                recv_sem=recv_sems.at[w, k], device_id=to, device_id_type=MESH)

        sends = []
        for w in range(n):
            cp = copy(w, 6, ins[w], outs[w].at[j_me], (mx, my, 1 - mc))
            cp.start()
            sends.append(cp)
        for k, (px, py) in enumerate(chips):
            for w in range(n):
                cp = copy(w, k, ins[w].at[pl.ds(mc * halves[w], halves[w]), :], half(w, j_me, mc), (px, py, mc))
                cp.start()
                sends.append(cp)
        for k, (px, py) in enumerate(chips):
            for w in range(n):
                got = half(w, 2 * px + py, mc)
                copy(w, k, got, got, (px, py, mc)).wait_recv()
                cp = copy(w, 3 + k, got, got, (mx, my, 1 - mc))
                cp.start()
                sends.append(cp)
        for k, (px, py) in enumerate(chips):
            for w in range(n):
                got = half(w, 2 * px + py, 1 - mc)
                copy(w, 3 + k, got, got, (mx, my, 1 - mc)).wait_recv()
        for w in range(n):
            own = outs[w].at[j_me]
            copy(w, 6, own, own, (mx, my, 1 - mc)).wait_recv()
        for cp in sends:
            cp.wait_send()

    res = pl.pallas_call(
        body, out_shape=[jax.ShapeDtypeStruct((4,) + s.shape, s.dtype) for s in shards]
        + [jax.ShapeDtypeStruct((8, 128), F32)],
        in_specs=_hbm_specs(n), out_specs=_hbm_specs(n) + [pl.BlockSpec(memory_space=pltpu.VMEM)],
        scratch_shapes=[pltpu.SemaphoreType.DMA((n, 7)), pltpu.SemaphoreType.DMA((n, 7))],
        name="gather_weights")(*shards)
    return list(res[:n]), res[n]


def _rs_pair(gs, name):
    n = len(gs)
    halves = [g.shape[1] // 2 for g in gs]

    def body(*refs):
        ins, lands = refs[:n], refs[n:2 * n]
        send_sems, recv_sems = refs[2 * n:]
        mx, my, mc = lax.axis_index("x"), lax.axis_index("y"), lax.axis_index("c")
        copies = []
        for w in range(n):
            h = halves[w]
            cp = pltpu.make_async_remote_copy(
                src_ref=ins[w].at[:, pl.ds((1 - mc) * h, h), :], dst_ref=lands[w], send_sem=send_sems.at[w],
                recv_sem=recv_sems.at[w], device_id=(mx, my, 1 - mc), device_id_type=MESH)
            cp.start()
            copies.append(cp)
        for cp in copies:
            cp.wait()

    return pl.pallas_call(
        body, out_shape=[jax.ShapeDtypeStruct((4, h, g.shape[2]), g.dtype) for g, h in zip(gs, halves)],
        in_specs=_hbm_specs(n), out_specs=_hbm_specs(n),
        scratch_shapes=[pltpu.SemaphoreType.DMA((n,)), pltpu.SemaphoreType.DMA((n,))], name=name)(*gs)


def _rs_chips(parts):
    n = len(parts)

    def body(*refs):
        ins, lands = refs[:n], refs[n:2 * n]
        send_sems, recv_sems = refs[2 * n:]
        mx, my, mc = lax.axis_index("x"), lax.axis_index("y"), lax.axis_index("c")
        copies = []
        for k, (px, py) in enumerate([(1 - mx, my), (mx, 1 - my), (1 - mx, 1 - my)]):
            for w in range(n):
                cp = pltpu.make_async_remote_copy(
                    src_ref=ins[w].at[2 * px + py], dst_ref=lands[w].at[k], send_sem=send_sems.at[w, k],
                    recv_sem=recv_sems.at[w, k], device_id=(px, py, mc), device_id_type=MESH)
                cp.start()
                copies.append(cp)
        for cp in copies:
            cp.wait()

    return list(pl.pallas_call(
        body, out_shape=[jax.ShapeDtypeStruct((3,) + p.shape[1:], p.dtype) for p in parts],
        in_specs=_hbm_specs(n), out_specs=_hbm_specs(n),
        scratch_shapes=[pltpu.SemaphoreType.DMA((n, 3)), pltpu.SemaphoreType.DMA((n, 3))], name="rs_chips")(*parts))


def _rs_pair_back(gs, name):
    n = len(gs)

    def body(*refs):
        outs = refs[n:2 * n]
        send_sems, recv_sems = refs[2 * n:]
        mx, my, mc = lax.axis_index("x"), lax.axis_index("y"), lax.axis_index("c")
        copies = []
        for w in range(n):
            h = gs[w].shape[0] // 2
            mine = outs[w].at[pl.ds(mc * h, h), :]
            cp = pltpu.make_async_remote_copy(src_ref=mine, dst_ref=mine, send_sem=send_sems.at[w],
                                              recv_sem=recv_sems.at[w], device_id=(mx, my, 1 - mc), device_id_type=MESH)
            cp.start()
            copies.append(cp)
        for cp in copies:
            cp.wait()

    return pl.pallas_call(
        body, out_shape=[jax.ShapeDtypeStruct(g.shape, g.dtype) for g in gs],
        in_specs=_hbm_specs(n), out_specs=_hbm_specs(n), input_output_aliases={w: w for w in range(n)},
        scratch_shapes=[pltpu.SemaphoreType.DMA((n,)), pltpu.SemaphoreType.DMA((n,))], name=name)(*gs)


_HBM = pl.BlockSpec(memory_space=pltpu.HBM)
_SEM = pl.BlockSpec(memory_space=pltpu.SEMAPHORE)
_EFFECT = pltpu.SideEffectType.DATAFLOW_SIDE_EFFECTING


def _ici_copies(kind, srcs, lands, send_sems, recv_sems):
    n = len(srcs)
    mx, my, mc = lax.axis_index("x"), lax.axis_index("y"), lax.axis_index("c")
    j_me = 2 * mx + my
    copies = []
    if kind == "all":
        for k in range(7):
            a, b, c = (k + 1) >> 2 & 1, (k + 1) >> 1 & 1, (k + 1) & 1
            peer = (1 - mx if a else mx, 1 - my if b else my, 1 - mc if c else mc)
            for w in range(n):
                copies.append(pltpu.make_async_remote_copy(
                    src_ref=srcs[w], dst_ref=lands[w].at[4 * mx + 2 * my + mc], send_sem=send_sems.at[7 * w + k],
                    recv_sem=recv_sems.at[7 * w + k], device_id=peer, device_id_type=MESH))
        return copies
    if kind == "pair":
        for w in range(n):
            h = srcs[w].shape[1] // 2
            copies.append(pltpu.make_async_remote_copy(
                src_ref=srcs[w].at[:, pl.ds((1 - mc) * h, h), :], dst_ref=lands[w], send_sem=send_sems.at[w],
                recv_sem=recv_sems.at[w], device_id=(mx, my, 1 - mc), device_id_type=MESH))
        return copies
    for k, (px, py) in enumerate([(1 - mx, my), (mx, 1 - my), (1 - mx, 1 - my)]):
        for w in range(n):
            if kind == "gather":
                h = srcs[w].shape[0] // 2
                src, dst = srcs[w].at[pl.ds(mc * h, h), :], lands[w].at[j_me, pl.ds(mc * h, h), :]
            else:
                src, dst = srcs[w].at[2 * px + py], lands[w].at[k]
            copies.append(pltpu.make_async_remote_copy(
                src_ref=src, dst_ref=dst, send_sem=send_sems.at[3 * w + k], recv_sem=recv_sems.at[3 * w + k],
                device_id=(px, py, mc), device_id_type=MESH))
    return copies


_SEMS_PER_OPERAND = {"gather": 3, "scatter": 3, "all": 7, "pair": 1}


def _ici_start(kind, srcs, land_shapes, carry, name):
    n = len(srcs)

    def body(*refs):
        ins, lands = refs[:n], refs[n:2 * n]
        send_sems, recv_sems = refs[2 * n + 1], refs[2 * n + 2]
        for cp in _ici_copies(kind, ins, lands, send_sems, recv_sems):
            cp.start()

    hbm = lambda a: pltpu.with_memory_space_constraint(a, pltpu.HBM)
    lands = [lax.empty(s, srcs[0].dtype) for s in land_shapes]
    args = [hbm(a) for a in list(srcs) + lands + [carry]]
    n_sem = _SEMS_PER_OPERAND[kind] * n
    out_shape = ([pltpu.SemaphoreType.DMA((n_sem,)), pltpu.SemaphoreType.DMA((n_sem,))]
                 + [pltpu.HBM(a.shape, a.dtype) for a in args])
    res = pl.pallas_call(
        body, name=name, out_shape=out_shape, in_specs=[_HBM] * len(args), out_specs=[_SEM, _SEM] + [_HBM] * len(args),
        input_output_aliases={i: 2 + i for i in range(len(args))},
        compiler_params=pltpu.CompilerParams(has_side_effects=_EFFECT))(*args)
    return res[0], res[1], list(res[2:2 + n]), list(res[2 + n:2 + 2 * n]), res[2 + 2 * n]


def _ici_wait(kind, send_sems, recv_sems, srcs, lands, after, name):
    n = len(srcs)

    def body(*refs):
        ins, zones = refs[:n], refs[n:2 * n]
        for cp in _ici_copies(kind, ins, zones, refs[2 * n], refs[2 * n + 1]):
            cp.wait_send()
            cp.wait_recv()

    args = list(srcs) + list(lands)
    res = pl.pallas_call(
        body, name=name, out_shape=[pltpu.HBM(a.shape, a.dtype) for a in args],
        in_specs=[_HBM] * len(args) + [_SEM, _SEM, pl.BlockSpec(memory_space=pl.ANY)], out_specs=[_HBM] * len(args),
        input_output_aliases={i: i for i in range(len(args))},
        compiler_params=pltpu.CompilerParams(has_side_effects=_EFFECT))(*args, send_sems, recv_sems, after)
    return list(res[:n]), list(res[n:])


def _gather_finish(shards, lands):
    n = len(shards)

    def body(*refs):
        own, outs = refs[:n], refs[2 * n:3 * n]
        send_sems, recv_sems = refs[3 * n:]
        mx, my, mc = lax.axis_index("x"), lax.axis_index("y"), lax.axis_index("c")
        j_me = 2 * mx + my
        sibling = (mx, my, 1 - mc)
        copies = []

        def push(w, k, src, dst):
            cp = pltpu.make_async_remote_copy(src_ref=src, dst_ref=dst, send_sem=send_sems.at[w, k],
                                              recv_sem=recv_sems.at[w, k], device_id=sibling, device_id_type=MESH)
            cp.start()
            copies.append(cp)

        for w in range(n):
            h = shards[w].shape[0] // 2
            push(w, 3, own[w], outs[w].at[j_me])
            for k, (px, py) in enumerate([(1 - mx, my), (mx, 1 - my), (1 - mx, 1 - my)]):
                got = outs[w].at[2 * px + py, pl.ds(mc * h, h), :]
                push(w, k, got, got)
        for cp in copies:
            cp.wait()

    return pl.pallas_call(
        body, out_shape=[jax.ShapeDtypeStruct(l.shape, l.dtype) for l in lands],
        in_specs=_hbm_specs(2 * n), out_specs=_hbm_specs(n), input_output_aliases={n + w: w for w in range(n)},
        scratch_shapes=[pltpu.SemaphoreType.DMA((n, 4)), pltpu.SemaphoreType.DMA((n, 4))], name="gather_finish",
    )(*shards, *lands)


def _tile_rows(h, c, itemsize, mult):
    best = h
    for t in range(mult, h + 1, mult):
        if h % t == 0 and t * c * itemsize <= (1 << 21):
            best = t
    return best


def _add_pair(g, land, place, name):
    _, h, c = land.shape
    t = _tile_rows(h, c, 2, 16)
    nb = h // t
    return _ew(lambda ids, u, v: (u.astype(F32) + v.astype(F32),), (4, nb),
               [(g, pl.BlockSpec((None, t, c), lambda j, i, s: (j, s[1] * nb + i, 0))),
                (land, pl.BlockSpec((None, t, c), lambda j, i, s: (j, i, 0)))],
               [(land.shape, BF16, pl.BlockSpec((None, t, c), lambda j, i, s: (j, i, 0)), None)], name, scalars=place)[0]


def _add_pair_many(gs, lands, place, name):
    ins, outs = [], []
    for g, l in zip(gs, lands):
        ins += [(g, pl.BlockSpec(l.shape, lambda i, s: (0, s[1], 0))), (l, pl.BlockSpec(l.shape, lambda i, s: (0, 0, 0)))]
        outs.append((l.shape, BF16, pl.BlockSpec(l.shape, lambda i, s: (0, 0, 0)), None))
    fn = lambda ids, *v: [v[2 * k].astype(F32) + v[2 * k + 1].astype(F32) for k in range(len(gs))]
    return list(_ew(fn, (1,), ins, outs, name, scalars=place))


def _add_chips_many(owns, lands, place, name):
    ins, outs = [], []
    for own, land in zip(owns, lands):
        _, h, c = land.shape
        ins += [(own, pl.BlockSpec((None, h, c), lambda i, s: (s[0], 0, 0))),
                (land, pl.BlockSpec((3, h, c), lambda i, s: (0, 0, 0)))]
        outs.append(((2 * h, c), F32, pl.BlockSpec((h, c), lambda i, s: (s[1], 0)), None))

    def fn(ids, *v):
        return [((v[2 * k].astype(F32) + v[2 * k + 1][0].astype(F32)) + v[2 * k + 1][1].astype(F32))
                + v[2 * k + 1][2].astype(F32) for k in range(len(owns))]

    return list(_ew(fn, (1,), ins, outs, name, scalars=place))


def _add_chips(own, land, place, name):
    _, h, c = land.shape
    t = _tile_rows(h, c, 4, 16)
    nb = h // t

    def fn(ids, a, b):
        return (((a.astype(F32) + b[0].astype(F32)) + b[1].astype(F32)) + b[2].astype(F32),)

    return _ew(fn, (nb,), [(own, pl.BlockSpec((None, t, c), lambda i, s: (s[0], i, 0))),
                           (land, pl.BlockSpec((3, t, c), lambda i, s: (0, i, 0)))],
               [((2 * h, c), F32, pl.BlockSpec((t, c), lambda i, s: (s[1] * nb + i, 0)), None)], name, scalars=place)[0]


W_IN_SEGMENTS = ((0, 256, KV0), (256, 288, KR0 + 64), (288, 672, Q0), (672, 1184, CX0), (1184, 1696, CB0),
                 (1696, 2208, CC0), (2208, 3232, GA0), (3232, 4256, GC0))
W_IN_SHARD = 1064


W_IN_SHARD_PAD = 1088


def _w_in_t_p_from_shards(s):
    pieces = []
    for o0, o1, p0 in sorted(W_IN_SEGMENTS, key=lambda t: t[2]):
        if p0 == KR0 + 64:
            pieces.append(jnp.zeros((64, s.shape[2]), s.dtype))
        for j in range(4):
            lo, hi = max(o0, j * W_IN_SHARD), min(o1, (j + 1) * W_IN_SHARD)
            if lo < hi:
                pieces.append(s[j, lo - j * W_IN_SHARD:hi - j * W_IN_SHARD])
    pieces.append(jnp.zeros((32, s.shape[2]), s.dtype))
    return jnp.concatenate(pieces, axis=0)


def _w_in_t_shards_from_p(g):
    shards = []
    for j in range(4):
        pieces = []
        for o0, o1, p0 in W_IN_SEGMENTS:
            lo, hi = max(o0, j * W_IN_SHARD), min(o1, (j + 1) * W_IN_SHARD)
            if lo < hi:
                pieces.append(g[p0 + lo - o0:p0 + hi - o0])
        pieces.append(jnp.zeros((W_IN_SHARD_PAD - W_IN_SHARD, g.shape[1]), g.dtype))
        shards.append(jnp.concatenate(pieces, axis=0))
    return jnp.stack(shards, axis=0)


def _cols_from_shards(s):
    return jnp.transpose(s, (1, 0, 2)).reshape(s.shape[1], -1)


def _rope_tables(T, TT, inverse):
    rows = T // GRID_W
    row = jnp.repeat(jnp.arange(rows), GRID_W).astype(F32)
    col = jnp.tile(jnp.arange(GRID_W), rows).astype(F32)
    inv = ROPE_THETA ** (-jnp.arange(0, 16, 2, dtype=F32) / 16)
    ang = jnp.concatenate([row[:, None] * inv, col[:, None] * inv], axis=-1)
    cos, sin = jnp.cos(ang), jnp.sin(ang)
    lane = jnp.arange(32)
    src = (lane // 16) * 8 + lane % 8
    lo = ((lane % 16) // 8 == 0).astype(F32)
    sgn = -1.0 if inverse else 1.0
    cos32 = cos[:, src]
    sin_lo32 = -sgn * sin[:, src] * lo
    sin_hi32 = sgn * sin[:, src] * (1.0 - lo)

    def widen(t32, fill):
        t = jnp.concatenate([jnp.full((T, 64), fill, F32), t32, jnp.full((T, 32), fill, F32)], axis=1)
        return jnp.concatenate([t, jnp.full((TT - T, HEAD_PAD), fill, F32)], axis=0)

    return widen(cos32, 1.0), widen(sin_lo32, 0.0), widen(sin_hi32, 0.0)


def _local_step(xx, tgt, mod_lat, mod_ctx, W, late_weights, early_grads, early_continue):
    TT = xx.shape[0]
    T = tgt.shape[0]
    n_lat, n_all = T // ROW_TILE, TT // ROW_TILE
    sh1, sc1, g1, sh2, sc2, g2 = [mod_lat[:, k * D_MODEL:(k + 1) * D_MODEL] for k in range(6)]
    csh1, csc1 = mod_ctx[:, :D_MODEL], mod_ctx[:, D_MODEL:2 * D_MODEL]
    vec = lambda n: _full((1, n))
    row_out = lambda n, dt, rows=T: ((rows, n), dt, _rows(n), None)
    acc_out = lambda n: ((1, n), F32, _full((1, n)), 0)

    def f_norm1(ids, x, g, a_sh, a_sc, b_sh, b_sc):
        ctx = ids[0] >= n_lat
        sh, sc = jnp.where(ctx, b_sh, a_sh), jnp.where(ctx, b_sc, a_sc)
        return ((x * _rms(x) * g) * (1.0 + sc) + sh,)

    (hh,) = _ew(f_norm1, (n_all,), [(xx, _rows(D_MODEL)), (W["norm1_g"], vec(D_MODEL)), (sh1, vec(D_MODEL)),
                                   (sc1, vec(D_MODEL)), (csh1, vec(D_MODEL)), (csc1, vec(D_MODEL))],
                [row_out(D_MODEL, BF16, TT)], "norm1_fwd")
    tm_all = _pick(TT, (768, 256))
    pp = _mm(hh, W["w_in_t"], "nt", TT, P_COLS, D_MODEL, tm=tm_all, tn=2176, tk=D_MODEL, name="w_in_fwd")

    def f_lowrank(ids, ckv, cq, gkv, gq):
        return ckv * _rms(ckv) * gkv, cq * _rms(cq) * gq

    nkv, nq = _ew(f_lowrank, (n_all,), [(pp, _rows(KV_RANK, KV0 // KV_RANK)), (pp, _rows(Q_RANK, Q0 // Q_RANK)),
                                       (W["kv_norm_g"], vec(KV_RANK)), (W["q_norm_g"], vec(Q_RANK))],
                  [row_out(KV_RANK, BF16, TT), row_out(Q_RANK, BF16, TT)], "lowrank_norm_fwd")
    kv = _mm(nkv, W["w_ukv"], "nn", TT, 1024, KV_RANK, tm=tm_all, tn=256, tk=KV_RANK, name="w_ukv_fwd",
             b_spec=pl.BlockSpec((None, KV_RANK, 256), lambda i, j, k: (j, k, 0)))
    q_raw = _mm(nq, W["w_uq_t"], "nt", TT, 1024, Q_RANK, tm=tm_all, tn=1024, tk=Q_RANK, name="w_uq_fwd")

    tabs = _rope_tables(T, TT, inverse=False)
    tabs_inv = _rope_tables(T, TT, inverse=True)
    o_pad, lse = _attn_fwd(q_raw, kv, pp, tabs, T, TT)
    W = dict(W, **late_weights(o_pad))
    tm_lat = _pick(T, (1024, 512, 256))
    ya = _mm(o_pad, W["w_attn_out"], "nn", T, D_MODEL, 1024, tm=tm_lat, tn=D_MODEL, tk=1024, name="w_attn_out_fwd")

    tc = 256
    colT = lambda blk0: pl.BlockSpec((T, tc), lambda j: (0, blk0 + j))

    def f_conv(ids, xin, cb, cc, w, b):
        return (cb * _conv(cc * xin, w, b),)

    (e,) = _ew(f_conv, (CONV_DIM // tc,),
               [(pp, colT(CX0 // tc)), (pp, colT(CB0 // tc)), (pp, colT(CC0 // tc)),
                (W["conv_w"], pl.BlockSpec((3, tc), lambda j: (0, j))), (W["conv_b"], pl.BlockSpec((1, tc), lambda j: (0, j)))],
               [((T, CONV_DIM), BF16, colT(0), None)], "conv_fwd")
    yc = _mm(e, W["w_conv_out"], "nn", T, D_MODEL, CONV_DIM, tm=tm_lat, tn=256, tk=CONV_DIM, name="w_conv_out_fwd",
             b_spec=pl.BlockSpec((None, CONV_DIM, 256), lambda i, j, k: (j, k, 0)))

    def f_merge(ids, ga, gc, a, c):
        return (_sigmoid(ga) * a + _sigmoid(gc) * c,)

    (mrg,) = _ew(f_merge, (n_lat,), [(pp, _rows(D_MODEL, 0)), (pp, _rows(D_MODEL, 1)), (ya, _rows(D_MODEL)),
                                    (yc, _rows(D_MODEL))], [row_out(D_MODEL, BF16)], "merge_fwd")
    mo = _mm(mrg, W["w_o"], "nn", T, D_MODEL, D_MODEL, tm=tm_lat, tn=D_MODEL, tk=D_MODEL, name="w_o_fwd")

    def f_norm2(ids, x, m, gate, g, sh, sc):
        x1 = x + gate * m
        return x1, (x1 * _rms(x1) * g) * (1.0 + sc) + sh

    x1, h2 = _ew(f_norm2, (n_lat,), [(xx, _rows(D_MODEL)), (mo, _rows(D_MODEL)), (g1, vec(D_MODEL)),
                                    (W["norm2_g"], vec(D_MODEL)), (sh2, vec(D_MODEL)), (sc2, vec(D_MODEL))],
                 [row_out(D_MODEL, F32), row_out(D_MODEL, BF16)], "norm2_fwd")
    up = _mm(h2, W["w_up"], "nn", T, 2 * D_FF, D_MODEL, tm=tm_lat, tn=1408, tk=D_MODEL, name="w_up_fwd",
             b_spec=pl.BlockSpec((None, D_MODEL, 1408), lambda i, j, k: (j, k, 0)))

    n_ff = D_FF // tc
    ffw = lambda off, n=3: pl.BlockSpec((n, tc), lambda j: (0, j + off))

    def f_ffn(ids, ug, uv, wg, wv, bg, bv):
        gate, val = _conv(ug, wg, bg), _conv(uv, wv, bv)
        return (gate * _sigmoid(gate) * val,)

    (act,) = _ew(f_ffn, (n_ff,), [(up, colT(0)), (up, colT(n_ff)), (W["ffn_conv_w"], ffw(0)), (W["ffn_conv_w"], ffw(n_ff)),
                                 (W["ffn_conv_b"], ffw(0, 1)), (W["ffn_conv_b"], ffw(n_ff, 1))],
                 [((T, D_FF), BF16, colT(0), None)], "ffn_act_fwd")
    f = _mm(act, W["w_down"], "nn", T, D_MODEL, D_FF, tm=tm_lat, tn=D_MODEL, tk=D_FF, name="w_down_fwd")

    def f_head(ids, x1_, f_, gate, gf, t):
        x2 = x1_ + gate * f_
        r = _rms(x2)
        xn = x2 * r
        err = xn * gf - t
        loss = 0.5 * jnp.sum(jnp.mean(err * err, axis=-1, keepdims=True))
        dy = err * (1.0 / D_MODEL)
        dx2 = _rms_bwd(dy * gf, xn, r)
        return dx2, dx2 * gate, _colsum(dy * xn), _colsum(dx2 * f_), jnp.full((1, 128), loss, F32)

    dx2, df, dg_f, dg2, loss = _ew(
        f_head, (n_lat,), [(x1, _rows(D_MODEL)), (f, _rows(D_MODEL)), (g2, vec(D_MODEL)), (W["final_g"], vec(D_MODEL)),
                           (tgt, _rows(D_MODEL))],
        [row_out(D_MODEL, F32), row_out(D_MODEL, BF16), acc_out(D_MODEL), acc_out(D_MODEL), acc_out(128)], "loss_head")

    d_w_down = _mm(act, df, "tn", D_FF, D_MODEL, T, tm=1408, tn=D_MODEL, tk=T, name="w_down_dw",
                   out_dtype=BF16).reshape(4, D_FF // 4, D_MODEL)
    da = _mm(df, W["w_down"], "nt", T, D_FF, D_MODEL, tm=tm_lat, tn=1408, tk=D_MODEL, name="w_down_dx")

    tcb = 128
    n_fb = D_FF // tcb
    colb = lambda blk0: pl.BlockSpec((T, tcb), lambda j: (0, blk0 + j))
    ffwb = lambda off, n=3: pl.BlockSpec((n, tcb), lambda j: (0, j + off))
    cvec = ((1, D_FF), F32, pl.BlockSpec((1, tcb), lambda j: (0, j)), None)

    def f_ffn_bwd(ids, ug, uv, d_act, wg, wv, bg, bv):
        sg, sv = _shifts(ug), _shifts(uv)
        gate, val = _conv(ug, wg, bg, sg), _conv(uv, wv, bv, sv)
        s = _sigmoid(gate)
        d_gate = d_act * val * s * (1.0 + gate * (1.0 - s))
        d_val = d_act * gate * s
        wg0, wg1, wg2 = _conv_bwd_w(d_gate, ug, sg)
        wv0, wv1, wv2 = _conv_bwd_w(d_val, uv, sv)
        d_up = [_conv_bwd_x(d_gate, wg), _conv_bwd_x(d_val, wv)]
        return d_up, [_colsum(d_gate), _colsum(d_val), wg0, wg1, wg2, wv0, wv1, wv2]

    d_up3, ffn_stats = _ew(
        f_ffn_bwd, (n_fb,),
        [(up, colb(0)), (up, colb(n_fb)), (da, colb(0)), (W["ffn_conv_w"], ffwb(0)), (W["ffn_conv_w"], ffwb(n_fb)),
         (W["ffn_conv_b"], ffwb(0, 1)), (W["ffn_conv_b"], ffwb(n_fb, 1))],
        [((2, T, D_FF), BF16, pl.BlockSpec((2, T, tcb), lambda j: (0, 0, j)), None),
         ((n_fb, 8, 1, tcb), F32, pl.BlockSpec((None, 8, 1, tcb), lambda j: (j, 0, 0, 0)), None)], "ffn_act_bwd")
    stat = lambda s: ffn_stats[:, s, 0, :].reshape(1, D_FF)
    d_ffn_conv_b = jnp.concatenate([stat(0), stat(1)], axis=1)
    d_ffn_conv_w = jnp.concatenate([jnp.concatenate([stat(2), stat(3), stat(4)], axis=0),
                                    jnp.concatenate([stat(5), stat(6), stat(7)], axis=0)], axis=1)

    tk_t = T
    d_w_up = _mm(h2, d_up3, "tn", D_MODEL, 2 * D_FF, T, tm=D_MODEL, tn=1408, tk=tk_t, name="w_up_dw", out_dtype=BF16,
                 b_spec=pl.BlockSpec((None, tk_t, 1408), lambda i, j, k: (j // 2, k, j % 2)),
                 o_spec=pl.BlockSpec((None, D_MODEL, 1408), lambda i, j, k: (j, i, 0)), out_shape=(4, D_MODEL, 1408))
    dh2 = _mm(d_up3, W["w_up"], "nt", T, D_MODEL, 2 * D_FF, tm=tm_lat, tn=D_MODEL, tk=1408, name="w_up_dx",
              a_spec=pl.BlockSpec((None, tm_lat, 1408), lambda i, j, k: (k // 2, i, k % 2)),
              b_spec=pl.BlockSpec((None, D_MODEL, 1408), lambda i, j, k: (k, j, 0)))

    def f_norm2_bwd(ids, dx2_, dh, x1_, m, g, sc, gate):
        r = _rms(x1_)
        xn = x1_ * r
        dx1 = dx2_ + _rms_bwd(dh * g * (1.0 + sc), xn, r)
        return dx1, dx1 * gate, _colsum(dh), _colsum(dh * xn * g), _colsum(dh * xn * (1.0 + sc)), _colsum(dx1 * m)

    dx1, dmo, dsh2, dsc2, dg_n2, dg1 = _ew(
        f_norm2_bwd, (n_lat,), [(dx2, _rows(D_MODEL)), (dh2, _rows(D_MODEL)), (x1, _rows(D_MODEL)), (mo, _rows(D_MODEL)),
                                (W["norm2_g"], vec(D_MODEL)), (sc2, vec(D_MODEL)), (g1, vec(D_MODEL))],
        [row_out(D_MODEL, F32), row_out(D_MODEL, BF16)] + [acc_out(D_MODEL)] * 4, "norm2_bwd")
    d_w_o = _mm(mrg, dmo, "tn", D_MODEL, D_MODEL, T, tm=D_MODEL, tn=D_MODEL, tk=tk_t, name="w_o_dw",
                out_dtype=BF16).reshape(4, D_MODEL // 4, D_MODEL)
    dmrg = _mm(dmo, W["w_o"], "nt", T, D_MODEL, D_MODEL, tm=tm_lat, tn=D_MODEL, tk=D_MODEL, name="w_o_dx")
    dmrg = early_grads("late", {"w_o": d_w_o, "w_up": d_w_up, "w_down": d_w_down}, dmrg, split=True)

    def f_merge_bwd(ids, dm, ga, gc, a, c):
        sa, sc_ = _sigmoid(ga), _sigmoid(gc)
        return dm * sa, dm * sc_, dm * a * sa * (1.0 - sa), dm * c * sc_ * (1.0 - sc_)

    dya, dyc, dp_ga, dp_gc = _ew(
        f_merge_bwd, (n_lat,), [(dmrg, _rows(D_MODEL)), (pp, _rows(D_MODEL, 0)), (pp, _rows(D_MODEL, 1)),
                                (ya, _rows(D_MODEL)), (yc, _rows(D_MODEL))], [row_out(D_MODEL, BF16)] * 4, "merge_bwd")
    dya = early_continue("late", dya)

    d_w_ao_p = _mm(o_pad, dya, "tn", 1024, D_MODEL, T, tm=1024, tn=D_MODEL, tk=tk_t, name="w_attn_out_dw", out_dtype=BF16)
    do_pad = _mm(dya, W["w_attn_out"], "nt", T, 1024, D_MODEL, tm=tm_lat, tn=1024, tk=D_MODEL, name="w_attn_out_dx")
    d_w_co = _mm(e, dyc, "tn", CONV_DIM, D_MODEL, T, tm=CONV_DIM, tn=256, tk=tk_t, name="w_conv_out_dw", out_dtype=BF16,
                 o_spec=pl.BlockSpec((None, CONV_DIM, 256), lambda i, j, k: (j, i, 0)), out_shape=(4, CONV_DIM, 256))
    de = _mm(dyc, W["w_conv_out"], "nt", T, CONV_DIM, D_MODEL, tm=tm_lat, tn=CONV_DIM, tk=256, name="w_conv_out_dx",
             b_spec=pl.BlockSpec((None, CONV_DIM, 256), lambda i, j, k: (k, j, 0)))

    def f_conv_bwd(ids, xin, cb, cc, d_e, w, b):
        z = cc * xin
        sz = _shifts(z)
        cz = _conv(z, w, b, sz)
        dcz = d_e * cb
        w0, w1, w2 = _conv_bwd_w(dcz, z, sz)
        dz = _conv_bwd_x(dcz, w)
        return dz * cc, d_e * cz, dz * xin, _colsum(dcz), w0, w1, w2

    cvec_c = ((1, CONV_DIM), F32, pl.BlockSpec((1, tc), lambda j: (0, j)), None)
    conv_b = _ew(f_conv_bwd, (CONV_DIM // tc,),
                 [(pp, colT(CX0 // tc)), (pp, colT(CB0 // tc)), (pp, colT(CC0 // tc)), (de, colT(0)),
                  (W["conv_w"], pl.BlockSpec((3, tc), lambda j: (0, j))), (W["conv_b"], pl.BlockSpec((1, tc), lambda j: (0, j)))],
                 [((T, CONV_DIM), BF16, colT(0), None)] * 3 + [cvec_c] * 4, "conv_bwd")
    dp_cx, dp_cb, dp_cc, d_conv_b = conv_b[:4]
    d_conv_w = jnp.concatenate(conv_b[4:7], axis=0)

    dq_raw, dkv, dp_kr = _attn_bwd(q_raw, kv, pp, o_pad, do_pad, lse, tabs, tabs_inv, T, TT)

    tk_a = TT
    d_w_uq_t = _mm(nq, dq_raw, "tn", Q_RANK, 1024, T, tm=Q_RANK, tn=1024, tk=T, name="w_uq_dw", transpose_out=True)
    dnq = _mm(dq_raw, W["w_uq_t"], "nn", T, Q_RANK, 1024, tm=tm_lat, tn=Q_RANK, tk=1024, name="w_uq_dx")
    d_w_ukv = _mm(nkv, dkv, "tn", KV_RANK, 1024, TT, tm=KV_RANK, tn=256, tk=tk_a, name="w_ukv_dw", out_dtype=BF16,
                  o_spec=pl.BlockSpec((None, KV_RANK, 256), lambda i, j, k: (j, i, 0)), out_shape=(4, KV_RANK, 256))
    dnkv = _mm(dkv, W["w_ukv"], "nt", TT, KV_RANK, 1024, tm=tm_all, tn=KV_RANK, tk=256, name="w_ukv_dx",
               b_spec=pl.BlockSpec((None, KV_RANK, 256), lambda i, j, k: (k, j, 0)))
    dnkv = early_grads("mid", {
        "w_attn_out": jnp.transpose(d_w_ao_p.reshape(N_HEADS, HEAD_PAD, 4, 256)[:, 64:], (2, 0, 1, 3)).reshape(
            4, N_HEADS * 64, 256),
        "w_conv_out": d_w_co,
        "w_uq": d_w_uq_t.reshape(4, 2, HEAD_PAD, Q_RANK)[:, :, :QK_DIM].reshape(4, 2 * QK_DIM, Q_RANK).astype(BF16),
        "w_ukv": d_w_ukv}, dnkv)

    def f_lowrank_bwd(ids, ckv, cq, dkv_, dq_, gkv, gq, ga, gc, cx, cb, cc, kr):
        rk, rq = _rms(ckv), _rms(cq)
        nk, nq_ = ckv * rk, cq * rq
        lat = ids[0] < n_lat
        dq_ = jnp.where(lat, dq_, 0.0)
        pieces = [jnp.where(lat, a, jnp.zeros_like(a)) for a in (ga, gc, cx, cb, cc)]
        pieces += [_rms_bwd(dkv_ * gkv, nk, rk).astype(BF16), _rms_bwd(dq_ * gq, nq_, rq).astype(BF16), kr.astype(BF16)]
        return jnp.concatenate(pieces, axis=1), _colsum(dkv_ * nk), _colsum(dq_ * nq_)

    lat_rows = lambda n: pl.BlockSpec((ROW_TILE, n), lambda i: (jnp.minimum(i, n_lat - 1), 0))
    dpp, dg_kv, dg_q = _ew(
        f_lowrank_bwd, (n_all,), [(pp, _rows(KV_RANK, KV0 // KV_RANK)), (pp, _rows(Q_RANK, Q0 // Q_RANK)),
                                  (dnkv, _rows(KV_RANK)), (dnq, lat_rows(Q_RANK)), (W["kv_norm_g"], vec(KV_RANK)),
                                  (W["q_norm_g"], vec(Q_RANK)), (dp_ga, lat_rows(D_MODEL)), (dp_gc, lat_rows(D_MODEL)),
                                  (dp_cx, lat_rows(CONV_DIM)), (dp_cb, lat_rows(CONV_DIM)), (dp_cc, lat_rows(CONV_DIM)),
                                  (dp_kr, _rows(HEAD_PAD))],
        [row_out(P_COLS, BF16, TT), acc_out(KV_RANK), acc_out(Q_RANK)], "lowrank_norm_bwd")
    d_w_in_t = _mm(hh, dpp, "tn", D_MODEL, P_COLS, TT, tm=512, tn=2176, tk=TT, name="w_in_dw", out_dtype=BF16,
                   transpose_out=True)
    dhh = _mm(dpp, W["w_in_t"], "nn", TT, D_MODEL, P_COLS, tm=tm_all, tn=512, tk=2176, name="w_in_dx")

    def f_norm1_bwd(ids, x, dh, dres, g, sc):
        r = _rms(x)
        xn = x * r
        return (dres + _rms_bwd(dh * g * (1.0 + sc), xn, r), _colsum(dh), _colsum(dh * xn * g),
                _colsum(dh * xn * (1.0 + sc)))

    grad_x, dsh1, dsc1, dg_n1 = _ew(
        f_norm1_bwd, (n_lat,), [(xx, _rows(D_MODEL)), (dhh, _rows(D_MODEL)), (dx1, _rows(D_MODEL)),
                                (W["norm1_g"], vec(D_MODEL)), (sc1, vec(D_MODEL))],
        [row_out(D_MODEL, F32)] + [acc_out(D_MODEL)] * 3, "norm1_bwd")

    def f_norm1_ctx_bwd(ids, x, dh, g, sc):
        xn = x * _rms(x)
        return _colsum(dh), _colsum(dh * xn * g), _colsum(dh * xn * (1.0 + sc))

    n_ctx = n_all - n_lat
    dcsh1, dcsc1, dg_n1c = _ew(
        f_norm1_ctx_bwd, (n_ctx,), [(xx, _rows(D_MODEL, 0, n_lat)), (dhh, _rows(D_MODEL, 0, n_lat)),
                                    (W["norm1_g"], vec(D_MODEL)), (csc1, vec(D_MODEL))], [acc_out(D_MODEL)] * 3,
        "norm1_ctx_bwd")

    big = {"w_in": _w_in_t_shards_from_p(d_w_in_t).astype(BF16)}
    zero = jnp.zeros((1, 4 * D_MODEL), F32)
    small = {
        "dmod_lat": jnp.concatenate([dsh1, dsc1, dg1, dsh2, dsc2, dg2], axis=1),
        "dmod_ctx": jnp.concatenate([dcsh1, dcsc1, zero], axis=1),
        "norm1_g": dg_n1 + dg_n1c, "norm2_g": dg_n2, "final_g": dg_f, "q_norm_g": dg_q, "kv_norm_g": dg_kv,
        "conv_b": d_conv_b, "conv_w": d_conv_w.reshape(1, -1), "ffn_conv_b": d_ffn_conv_b,
        "ffn_conv_w": d_ffn_conv_w.reshape(1, -1),
    }
    return grad_x, loss, big, small


SMALL = (("dmod_lat", 6144), ("dmod_ctx", 6144), ("norm1_g", 1024), ("norm2_g", 1024), ("final_g", 1024),
         ("q_norm_g", 384), ("kv_norm_g", 256), ("conv_b", 512), ("conv_w", 1536), ("ffn_conv_b", 5632),
         ("ffn_conv_w", 16896), ("loss", 128))
SMALL_ROWS = 320


def _adam_update(w, g, m, v):
    c1, c2 = 1.0 - ADAM_B1 ** ADAM_STEP, 1.0 - ADAM_B2 ** ADAM_STEP
    m2 = ADAM_B1 * m + (1.0 - ADAM_B1) * g
    v2 = ADAM_B2 * v + (1.0 - ADAM_B2) * (g * g)
    return [-ADAM_LR * ((m2 / c1) / (jnp.sqrt(v2 / c2) + ADAM_EPS) + ADAM_WD * w), m2, v2]


def _adamw(w, g, m, v, name):
    R, C = w.shape
    tr = 8 if R % 8 == 0 else R
    for t in range(8, R + 1, 8):
        if R % t == 0 and t * C * 4 <= (1 << 20):
            tr = t
    spec = pl.BlockSpec((tr, C), lambda i: (i, 0))
    return _ew(lambda ids, *vals: _adam_update(*vals), (R // tr,), [(w, spec), (g, spec), (m, spec), (v, spec)],
               [((R, C), F32, spec, None)] * 3, name)


def kernel(x, c, ctx, c_ctx, w_ada, b_ada, norm1_g, w_in, q_norm_g, kv_norm_g, w_uq, w_ukv, conv_w, conv_b, w_attn_out, w_conv_out, w_o, norm2_g, w_up, ffn_conv_w, ffn_conv_b, w_down, final_g, loss_target, m_c_ctx, m_w_ada, m_b_ada, m_norm1_g, m_w_in, m_q_norm_g, m_kv_norm_g, m_w_uq, m_w_ukv, m_conv_w, m_conv_b, m_w_attn_out, m_w_conv_out, m_w_o, m_norm2_g, m_w_up, m_ffn_conv_w, m_ffn_conv_b, m_w_down, m_final_g, v_c_ctx, v_w_ada, v_b_ada, v_norm1_g, v_w_in, v_q_norm_g, v_kv_norm_g, v_w_uq, v_w_ukv, v_conv_w, v_conv_b, v_w_attn_out, v_w_conv_out, v_w_o, v_norm2_g, v_w_up, v_ffn_conv_w, v_ffn_conv_b, v_w_down, v_final_g):
    mx, my, mc = lax.axis_index("x"), lax.axis_index("y"), lax.axis_index("c")
    chip = 2 * mx + my
    dev = 4 * mx + 2 * my + mc
    T, Tc = x.shape[1], ctx.shape[1]
    TT = T + Tc
    w_in_t, m_w_in_t, v_w_in_t = (jnp.transpose(a[0]) for a in (w_in, m_w_in, v_w_in))
    w_uq_t, m_w_uq_t, v_w_uq_t = (jnp.transpose(a[0]) for a in (w_uq, m_w_uq, v_w_uq))
    conv_sh = jnp.concatenate([conv_w[0], ffn_conv_w[0]], axis=1)
    pay1 = jnp.concatenate([jnp.pad(c, ((0, 7), (0, 0))), jnp.pad(conv_sh, ((0, 5), (0, 0)))], axis=1)
    c_send, c_recv, c_src, c_land, zero0 = _ici_start("all", [pay1], [(8, 8, 2560)], jnp.zeros((8, 128), F32),
                                                      "cond_start")
    w_in_bf = (jnp.pad(w_in_t, ((0, W_IN_SHARD_PAD - W_IN_SHARD), (0, 0))) + zero0[0, 0]).astype(BF16)
    shards = {"w_in": w_in_bf, "w_uq": w_uq_t, "w_ukv": w_ukv[0], "w_attn_out": w_attn_out[0],
              "w_conv_out": w_conv_out[0], "w_o": w_o[0], "w_up": w_up[0], "w_down": w_down[0]}
    (pay1,), (c_land,) = _ici_wait("all", c_send, c_recv, c_src, c_land, shards["w_in"], "cond_wait")
    got1 = lax.dynamic_update_slice(c_land, pay1[None], (dev, 0, 0))
    c_all = got1[:, 0, :D_MODEL]
    conv_all = got1[0::2, :3, D_MODEL:]
    conv_w_full = _cols_from_shards(conv_all[:, :, :128])
    ffn_conv_w_full = _cols_from_shards(conv_all[:, :, 128:])

    cond = jnp.concatenate([c_all, c_ctx.reshape(1, D_MODEL), jnp.zeros((7, D_MODEL), F32)], axis=0)

    def f_silu(ids, v):
        return (v * _sigmoid(v),)

    (s16,) = _ew(f_silu, (1,), [(cond, _full((16, D_MODEL)))], [((16, D_MODEL), F32, _full((16, D_MODEL)), None)], "silu_cond")
    mod_sh = _mm(s16, w_ada[0], "nn", 16, 1536, D_MODEL, tm=16, tn=768, tk=D_MODEL, name="w_ada_fwd")
    m_send, m_recv, m_src, m_land, zero1 = _ici_start("all", [mod_sh], [(8, 16, 1536)], jnp.zeros((8, 128), F32),
                                                      "mod_start")
    shards["w_ukv"] = w_ukv[0] + zero1[0, 0]

    names = [n for n, _ in BIG]
    first = [n for n in names if n not in GATHER_LATE]
    gathered, zero = _gather_weights([shards[n].astype(BF16) for n in first])
    full = dict(zip(first, gathered))
    (mod_mine,), (m_land,) = _ici_wait("all", m_send, m_recv, m_src, m_land, gathered[0], "mod_wait")
    got2 = lax.dynamic_update_slice(m_land, mod_mine[None], (dev, 0, 0))
    mod_all = _cols_from_shards(got2[0::2]) + b_ada
    mod_lat = lax.dynamic_slice_in_dim(mod_all, dev, 1, axis=0)
    mod_ctx = mod_all[8:9]
    xx = jnp.concatenate([x[0], ctx[0]], axis=0)
    late_bf = [(shards[n] + zero[0, 0]).astype(BF16) for n in GATHER_LATE]
    g_send, g_recv, late_src, late_land, xx = _ici_start(
        "gather", late_bf, [(4,) + s.shape for s in late_bf], xx, "gather_late_start")

    def late_weights(after):
        src, land = _ici_wait("gather", g_send, g_recv, late_src, late_land, after, "gather_late_wait")
        got = dict(zip(GATHER_LATE, _gather_finish(src, land)))
        wao = _cols_from_shards(got["w_attn_out"]).reshape(N_HEADS, 64, D_MODEL)
        return {"w_attn_out": jnp.pad(wao, ((0, 0), (64, 0), (0, 0))).reshape(N_HEADS * HEAD_PAD, D_MODEL),
                "w_conv_out": got["w_conv_out"], "w_o": got["w_o"].reshape(D_MODEL, D_MODEL), "w_up": got["w_up"],
                "w_down": got["w_down"].reshape(D_FF, D_MODEL)}

    wuq_t = full["w_uq"].reshape(N_HEADS, QK_DIM, Q_RANK)
    W = {
        "w_in_t": _w_in_t_p_from_shards(full["w_in"]),
        "w_uq_t": jnp.pad(wuq_t, ((0, 0), (0, HEAD_PAD - QK_DIM), (0, 0))).reshape(N_HEADS * HEAD_PAD, Q_RANK),
        "w_ukv": full["w_ukv"],
        "norm1_g": norm1_g, "norm2_g": norm2_g, "final_g": final_g.reshape(1, D_MODEL), "q_norm_g": q_norm_g,
        "kv_norm_g": kv_norm_g, "conv_w": conv_w_full, "conv_b": conv_b, "ffn_conv_w": ffn_conv_w_full,
        "ffn_conv_b": ffn_conv_b,
    }

    place = jnp.stack([chip, mc]).astype(jnp.int32)
    early = {}

    pending = {}

    def scatter(tag, group, gs, from_sib, carry):
        if tag == "mid":
            sums = _add_pair_many(gs, from_sib, place, "rs_pair_add_mid")
        else:
            sums = [_add_pair(gs[w], from_sib[w], place, "rs_pair_add_" + n) for w, n in enumerate(group)]
        send, recv, sums, land, carry = _ici_start(
            "scatter", sums, [(3,) + s.shape[1:] for s in sums], carry, "rs_chips_" + tag + "_start")
        early[tag] = (group, send, recv, sums, land)
        return carry

    def early_grads(tag, g, carry, split=False):
        gs = list(g.values())
        if not split:
            return scatter(tag, list(g), gs, _rs_pair(gs, "rs_pair_" + tag), carry)
        send, recv, gs, land, carry = _ici_start(
            "pair", gs, [(4, s.shape[1] // 2, s.shape[2]) for s in gs], carry, "rs_pair_" + tag + "_start")
        pending[tag] = (list(g), send, recv, gs, land)
        return carry

    def early_continue(tag, carry):
        group, send, recv, gs, land = pending[tag]
        gs, from_sib = _ici_wait("pair", send, recv, gs, land, carry, "rs_pair_" + tag + "_wait")
        return scatter(tag, group, gs, from_sib, carry)

    grad_x, loss_part, gbig, gsmall = _local_step(xx, loss_target[0], mod_lat, mod_ctx, W, late_weights, early_grads,
                                                  early_continue)

    gsmall["loss"] = loss_part
    pay3 = jnp.concatenate([gsmall[n].reshape(-1) for n, _ in SMALL])
    pay3 = jnp.pad(pay3, (0, SMALL_ROWS * 128 - pay3.shape[0])).reshape(SMALL_ROWS, 128)
    s_send, s_recv, s_src, s_land, w_in_thru = _ici_start("all", [pay3], [(8, SMALL_ROWS, 128)], gbig["w_in"],
                                                         "small_start")
    gbig = {"w_in": w_in_thru}

    after_small = early_grads("last", gbig, s_src[0])

    (pay3,), (s_land,) = _ici_wait("all", s_send, s_recv, [after_small], s_land, early["last"][3][0], "small_wait")
    got3 = lax.dynamic_update_slice(s_land, pay3[None], (dev, 0, 0)).reshape(8 * SMALL_ROWS, 128)

    def f_sum8(ids, a):
        s = a[0:SMALL_ROWS]
        for d in range(1, 8):
            s = s + a[d * SMALL_ROWS:(d + 1) * SMALL_ROWS]
        return (s,)

    (vsum,) = _ew(f_sum8, (1,), [(got3, _full((8 * SMALL_ROWS, 128)))],
                  [((SMALL_ROWS, 128), F32, _full((SMALL_ROWS, 128)), None)], "sum_small")
    vflat = vsum.reshape(-1)
    gvec, off = {}, 0
    for n, size in SMALL:
        gvec[n] = vflat[off:off + size]
        off += size
    loss = gvec["loss"][0]
    dmod_rows = got3.reshape(8, SMALL_ROWS * 128)[:, :6 * D_MODEL]
    dm16 = jnp.concatenate([dmod_rows, gvec["dmod_ctx"].reshape(1, -1), jnp.zeros((7, 6 * D_MODEL), F32)], axis=0)

    def f_colsum(ids, a):
        return (_colsum(a),)

    (g_b_ada,) = _ew(f_colsum, (1,), [(dm16, _full((16, 6 * D_MODEL)))],
                     [((1, 6 * D_MODEL), F32, _full((1, 6 * D_MODEL)), None)], "b_ada_grad")
    dm_sh = lax.dynamic_slice_in_dim(dm16, chip * 1536, 1536, axis=1)
    g_w_ada = _mm(s16, dm_sh, "tn", D_MODEL, 1536, 16, tm=512, tn=768, tk=16, name="w_ada_dw")
    dcond_part = _mm(dm_sh, w_ada[0], "nt", 16, D_MODEL, 1536, tm=16, tn=512, tk=1536, name="w_ada_dx")
    d_send, d_recv, d_src, d_land, vsum = _ici_start("all", [dcond_part[8:16]], [(8, 8, D_MODEL)], vsum, "dcond_start")

    def finish(tags, after):
        done, halves = [], []
        for tag in tags:
            tag_names, send, recv, sums, land = early[tag]
            sums, land = _ici_wait("scatter", send, recv, sums, land, after, "rs_chips_" + tag + "_wait")
            done += tag_names
            if tag == "mid":
                halves += _add_chips_many(sums, land, place, "rs_chip_add_mid")
            else:
                halves += [_add_chips(a, b, place, "rs_chip_add_" + n) for a, b, n in zip(sums, land, tag_names)]
        return dict(zip(done, _rs_pair_back(halves, "rs_pair_back_" + tags[0])))

    grads, deltas, new_m, new_v = {}, {}, {}, {}

    def adam(n, w_, m_, v_, g, transposed):
        d_, m2, v2 = _adamw(w_, g, m_, v_, "adamw_" + n)
        back = (lambda a: jnp.transpose(a)[None]) if transposed else (lambda a: a[None])
        grads[n], deltas[n], new_m[n], new_v[n] = back(g[:w_.shape[0]]), back(d_), back(m2), back(v2)

    gw = finish(["late", "mid"], grad_x)
    adam("w_ada", w_ada[0], m_w_ada[0], v_w_ada[0], g_w_ada, False)
    for n, (w_, m_, v_) in {"w_o": (w_o, m_w_o, v_w_o), "w_up": (w_up, m_w_up, v_w_up),
                            "w_down": (w_down, m_w_down, v_w_down)}.items():
        adam(n, w_[0], m_[0], v_[0], gw[n], False)
    gw_in = finish(["last"], deltas["w_up"])
    adam("w_in", w_in_t, m_w_in_t, v_w_in_t, gw_in["w_in"], True)

    (dcond_mine,), (d_land,) = _ici_wait("all", d_send, d_recv, d_src, d_land, deltas["w_in"], "dcond_wait")
    got4 = lax.dynamic_update_slice(d_land, dcond_mine[None], (dev, 0, 0))[0::2, 0]

    def f_c_ctx(ids, parts, cc):
        s = _sigmoid(cc)
        d = parts[0:1] + parts[1:2] + parts[2:3] + parts[3:4]
        return (d * s * (1.0 + cc * (1.0 - s)),)

    (g_c_ctx,) = _ew(f_c_ctx, (1,), [(got4, _full((4, D_MODEL))), (c_ctx.reshape(1, D_MODEL), _full((1, D_MODEL)))],
                     [((1, D_MODEL), F32, _full((1, D_MODEL)), None)], "c_ctx_grad")

    conv_w_g = lax.dynamic_slice_in_dim(gvec["conv_w"].reshape(3, CONV_DIM), chip * 128, 128, axis=1)
    ffn_conv_w_g = lax.dynamic_slice_in_dim(gvec["ffn_conv_w"].reshape(3, 2 * D_FF), chip * 1408, 1408, axis=1)
    vec_params = (("c_ctx", c_ctx, m_c_ctx, v_c_ctx, g_c_ctx), ("b_ada", b_ada, m_b_ada, v_b_ada, g_b_ada),
                  ("norm1_g", norm1_g, m_norm1_g, v_norm1_g, gvec["norm1_g"]),
                  ("q_norm_g", q_norm_g, m_q_norm_g, v_q_norm_g, gvec["q_norm_g"]),
                  ("kv_norm_g", kv_norm_g, m_kv_norm_g, v_kv_norm_g, gvec["kv_norm_g"]),
                  ("conv_w", conv_w, m_conv_w, v_conv_w, conv_w_g), ("conv_b", conv_b, m_conv_b, v_conv_b, gvec["conv_b"]),
                  ("norm2_g", norm2_g, m_norm2_g, v_norm2_g, gvec["norm2_g"]),
                  ("ffn_conv_w", ffn_conv_w, m_ffn_conv_w, v_ffn_conv_w, ffn_conv_w_g),
                  ("ffn_conv_b", ffn_conv_b, m_ffn_conv_b, v_ffn_conv_b, gvec["ffn_conv_b"]),
                  ("final_g", final_g, m_final_g, v_final_g, gvec["final_g"]))
    two_d = lambda a: a.reshape((-1, a.shape[-1]))
    many = [p + ((lambda r, s=p[1].shape: r.reshape(s)),) for p in vec_params]
    for n, w_, m_, v_ in (("w_ukv", w_ukv, m_w_ukv, v_w_ukv), ("w_attn_out", w_attn_out, m_w_attn_out, v_w_attn_out),
                          ("w_conv_out", w_conv_out, m_w_conv_out, v_w_conv_out)):
        many.append((n, w_, m_, v_, gw[n], (lambda r, s=w_.shape: r.reshape(s))))
    many.append(("w_uq", w_uq_t, m_w_uq_t, v_w_uq_t, gw["w_uq"], lambda r: jnp.transpose(r)[None]))

    def f_adam_many(ids, *vals):
        out = []
        for k in range(len(many)):
            out += _adam_update(*vals[4 * k:4 * k + 4])
        return out

    ins_v, outs_v = [], []
    for p in many:
        shp = two_d(p[1]).shape
        ins_v += [(two_d(a), _full(shp)) for a in (p[1], p[4], p[2], p[3])]
        outs_v += [(shp, F32, _full(shp), None)] * 3
    res_v = _ew(f_adam_many, (1,), ins_v, outs_v, "adamw_small")
    for k, p in enumerate(many):
        n, post = p[0], p[5]
        grads[n] = post(two_d(p[4]))
        deltas[n], new_m[n], new_v[n] = (post(r) for r in res_v[3 * k:3 * k + 3])

    order = ("c_ctx", "w_ada", "b_ada", "norm1_g", "w_in", "q_norm_g", "kv_norm_g", "w_uq", "w_ukv", "conv_w", "conv_b",
             "w_attn_out", "w_conv_out", "w_o", "norm2_g", "w_up", "ffn_conv_w", "ffn_conv_b", "w_down", "final_g")
    return (loss, grad_x[None], *[grads[n] for n in order], *[deltas[n] for n in order],
            *[new_m[n] for n in order], *[new_v[n] for n in order])
```

```python
import functools

import jax
import jax.numpy as jnp
from jax import lax
from jax.experimental import pallas as pl
from jax.experimental.pallas import tpu as pltpu

F32, BF16 = jnp.float32, jnp.bfloat16
MESH = pl.DeviceIdType.MESH

D_MODEL = 1024
N_HEADS = 8
HEAD_PAD = 128
QK_DIM = 96
Q_RANK, KV_RANK = 384, 256
CONV_DIM = 512
D_FF = 2816
GRID_W = 64
ROPE_THETA = 10000.0
EPS = 1e-6
GA0, GC0, CX0, CB0, CC0, KV0, Q0, KR0, P_COLS = 0, 1024, 2048, 2560, 3072, 3584, 3840, 4224, 4352
ROW_TILE = 256
VMEM_LIMIT_BYTES = 48 * 1024 * 1024

ADAM_LR, ADAM_B1, ADAM_B2, ADAM_EPS, ADAM_WD, ADAM_STEP = 0.001, 0.9, 0.999, 1e-08, 0.01, 10

BIG = (("w_in", (1088, 1024)), ("w_uq", (192, 384)), ("w_ukv", (256, 256)), ("w_attn_out", (512, 256)),
       ("w_conv_out", (512, 256)), ("w_o", (256, 1024)), ("w_up", (1024, 1408)), ("w_down", (704, 1024)))

GATHER_LATE = ("w_attn_out", "w_conv_out", "w_o", "w_up", "w_down")

NN = (((1,), (0,)), ((), ()))
NT = (((1,), (1,)), ((), ()))
TN = (((0,), (0,)), ((), ()))


def _cp(sem):
    return pltpu.CompilerParams(dimension_semantics=sem, vmem_limit_bytes=VMEM_LIMIT_BYTES)


PIN_BYTES = 1 << 19


def _in_hbm(arrays):
    return [pltpu.with_memory_space_constraint(a, pltpu.HBM) if a.size * a.dtype.itemsize >= PIN_BYTES else a
            for a in arrays]


def _out(shape, dtype):
    n = 1
    for d in shape:
        n *= d
    big = n * jnp.dtype(dtype).itemsize >= PIN_BYTES
    return pltpu.HBM(shape, dtype) if big else jax.ShapeDtypeStruct(shape, dtype)


def _pick(n, prefs):
    for p in prefs:
        if n % p == 0:
            return p
    return n


def _mm(a, b, mode, M, N, K, *, tm, tn, tk, name, out_dtype=F32, a_spec=None, b_spec=None, o_spec=None,
        out_shape=None, transpose_out=False):
    assert M % tm == 0 and N % tn == 0 and K % tk == 0, (name, M, N, K, tm, tn, tk)
    nk = K // tk
    dims = {"nn": NN, "nt": NT, "tn": TN}[mode]
    if a_spec is None:
        a_spec = (pl.BlockSpec((tk, tm), lambda i, j, k: (k, i)) if mode == "tn"
                  else pl.BlockSpec((tm, tk), lambda i, j, k: (i, k)))
    if b_spec is None:
        b_spec = (pl.BlockSpec((tn, tk), lambda i, j, k: (j, k)) if mode == "nt"
                  else pl.BlockSpec((tk, tn), lambda i, j, k: (k, j)))
    if o_spec is None:
        o_spec = (pl.BlockSpec((tn, tm), lambda i, j, k: (j, i)) if transpose_out
                  else pl.BlockSpec((tm, tn), lambda i, j, k: (i, j)))
    if out_shape is None:
        out_shape = (N, M) if transpose_out else (M, N)

    def emit(o_ref, val):
        o_ref[...] = (val.T if transpose_out else val).astype(o_ref.dtype)

    def body(a_ref, b_ref, o_ref, *scratch):
        part = lax.dot_general(a_ref[...].astype(BF16), b_ref[...].astype(BF16), dims, preferred_element_type=F32)
        if nk == 1:
            emit(o_ref, part)
            return
        acc_ref, = scratch
        k = pl.program_id(2)

        @pl.when(k == 0)
        def _():
            acc_ref[...] = part

        @pl.when((k > 0) & (k < nk - 1))
        def _():
            acc_ref[...] += part

        @pl.when(k == nk - 1)
        def _():
            emit(o_ref, acc_ref[...] + part)

    return pl.pallas_call(
        body, grid=(M // tm, N // tn, nk), in_specs=[a_spec, b_spec], out_specs=o_spec,
        out_shape=_out(out_shape, out_dtype),
        scratch_shapes=[pltpu.VMEM((tm, tn), F32)] if nk > 1 else [],
        compiler_params=_cp(("parallel", "parallel", "arbitrary")), name=name)(*_in_hbm([a, b]))


def _ew(fn, grid, ins, outs, name, scalars=None):
    n_in = len(ins)
    n_sc = 0 if scalars is None else 1

    def store(ref, val, acc, ids):
        if isinstance(val, (list, tuple)):
            for h, v in enumerate(val):
                ref[h] = v.astype(ref.dtype)
            return
        if acc is None:
            ref[...] = val.astype(ref.dtype)
            return

        @pl.when(ids[acc] == 0)
        def _():
            ref[...] = val.astype(ref.dtype)

        @pl.when(ids[acc] > 0)
        def _():
            ref[...] += val.astype(ref.dtype)

    def body(*refs):
        refs = refs[n_sc:]
        ids = tuple(pl.program_id(a) for a in range(len(grid)))
        vals = fn(ids, *[r[...] for r in refs[:n_in]])
        for ref, val, (_, _, _, acc) in zip(refs[n_in:], vals, outs):
            store(ref, val, acc, ids)

    acc_axes = {o[3] for o in outs if o[3] is not None}
    sem = tuple("arbitrary" if a in acc_axes else "parallel" for a in range(len(grid)))
    in_specs, out_specs = [s for _, s in ins], [o[2] for o in outs]
    out_shape = [_out(o[0], o[1]) for o in outs]
    args = _in_hbm([a for a, _ in ins])
    if scalars is None:
        return pl.pallas_call(body, grid=grid, in_specs=in_specs, out_specs=out_specs, out_shape=out_shape,
                              compiler_params=_cp(sem), name=name)(*args)
    spec = pltpu.PrefetchScalarGridSpec(num_scalar_prefetch=1, grid=grid, in_specs=in_specs, out_specs=out_specs)
    return pl.pallas_call(body, grid_spec=spec, out_shape=out_shape, compiler_params=_cp(sem), name=name)(scalars, *args)


def _rows(width, cblk=0, roff=0, tr=ROW_TILE):
    return pl.BlockSpec((tr, width), lambda i: (i + roff, cblk))


def _full(shape):
    nd = len(shape)
    return pl.BlockSpec(shape, lambda *_: (0,) * nd)


def _sigmoid(x):
    return 1.0 / (1.0 + jnp.exp(-x))


def _rms(x):
    return lax.rsqrt(jnp.mean(x * x, axis=-1, keepdims=True) + EPS)


def _rms_bwd(dn, xn, r):
    return r * (dn - xn * jnp.mean(dn * xn, axis=-1, keepdims=True))


def _colsum(x):
    return jnp.sum(x, axis=0, keepdims=True)


def _shifts(x):
    n = x.shape[0]
    rows = lax.broadcasted_iota(jnp.int32, x.shape, 0)
    return jnp.where(rows == 0, 0.0, pltpu.roll(x, 1, 0)), jnp.where(rows == n - 1, 0.0, pltpu.roll(x, n - 1, 0))


def _conv(x, w, b, shifted=None):
    prev, nxt = _shifts(x) if shifted is None else shifted
    return b + prev * w[0:1] + x * w[1:2] + nxt * w[2:3]


def _conv_bwd_x(dy, w):
    prev, nxt = _shifts(dy)
    return nxt * w[0:1] + dy * w[1:2] + prev * w[2:3]


def _conv_bwd_w(dy, x, shifted):
    prev, nxt = shifted
    return _colsum(dy * prev), _colsum(dy * x), _colsum(dy * nxt)


def _rope(x, cos, sin_lo, sin_hi):
    return x * cos + pltpu.roll(x, HEAD_PAD - 8, 1) * sin_lo + pltpu.roll(x, 8, 1) * sin_hi


ATTN_SCALE = QK_DIM ** -0.5
LOG2_E = 1.4426950408889634


def _head_keys(kv_ref, kr_ref, cos_ref, slo_ref, shi_ref, kc_ref, vp_ref):
    kv = kv_ref[...]
    lane = lax.broadcasted_iota(jnp.int32, kv.shape, 1)
    kc_ref[...] = jnp.where(lane < 64, kv, _rope(kr_ref[...], cos_ref[...], slo_ref[...], shi_ref[...])).astype(BF16)
    vp_ref[...] = jnp.where(lane >= 64, kv, 0.0).astype(BF16)


ATTN_Q_TILE = 512


def _attn_specs(tq, TT):
    q = pl.BlockSpec((tq, HEAD_PAD), lambda h, i: (i, h))
    keys = pl.BlockSpec((TT, HEAD_PAD), lambda h, i: (0, h))
    kr = pl.BlockSpec((TT, HEAD_PAD), lambda h, i: (0, KR0 // HEAD_PAD))
    tab_q = pl.BlockSpec((tq, HEAD_PAD), lambda h, i: (i, 0))
    tab_k = pl.BlockSpec((TT, HEAD_PAD), lambda h, i: (0, 0))
    lse = pl.BlockSpec((None, tq, 1), lambda h, i: (h, i, 0))
    return q, keys, kr, tab_q, tab_k, lse


def _attn_fwd(q_raw, kv, pp, tabs, T, TT):
    tq = ROW_TILE
    cos, slo, shi = tabs

    def body(q_ref, kv_ref, kr_ref, cq, lq, hq, ck, lk, hk, o_ref, l_ref, kc, vp):
        @pl.when(pl.program_id(1) == 0)
        def _():
            _head_keys(kv_ref, kr_ref, ck, lk, hk, kc, vp)

        q = _rope(q_ref[...], cq[...], lq[...], hq[...]).astype(BF16)
        s = lax.dot_general(q, kc[...], NT, preferred_element_type=F32)
        m = jnp.max(s, axis=-1, keepdims=True)
        p = jnp.exp2((s - m) * (ATTN_SCALE * LOG2_E))
        l = jnp.sum(p, axis=-1, keepdims=True)
        o = lax.dot_general(p.astype(BF16), vp[...], NN, preferred_element_type=F32)
        o_ref[...] = o / l
        l_ref[...] = m * ATTN_SCALE + jnp.log(l)

    qs, keys, kr, tab_q, tab_k, lse = _attn_specs(tq, TT)
    return pl.pallas_call(
        body, grid=(N_HEADS, T // tq), in_specs=[qs, keys, kr, tab_q, tab_q, tab_q, tab_k, tab_k, tab_k],
        out_specs=[qs, lse],
        out_shape=[jax.ShapeDtypeStruct((T, N_HEADS * HEAD_PAD), F32), jax.ShapeDtypeStruct((N_HEADS, T, 1), F32)],
        scratch_shapes=[pltpu.VMEM((TT, HEAD_PAD), BF16), pltpu.VMEM((TT, HEAD_PAD), BF16)],
        compiler_params=_cp(("parallel", "arbitrary")), name="attn_fwd",
    )(*_in_hbm([q_raw, kv, pp, cos, slo, shi, cos, slo, shi]))


def _attn_bwd(q_raw, kv, pp, o, do, lse, tabs, tabs_inv, T, TT):
    tq = _pick(T, (ATTN_Q_TILE, ROW_TILE))
    nq = T // tq
    cos, slo, shi = tabs
    cos_i, slo_i, shi_i = tabs_inv

    def body(q_ref, kv_ref, kr_ref, cq, lq, hq, ck, lk, hk, iq, ilq, ihq, ik, ilk, ihk, o_ref, do_ref, l_ref,
             dq_ref, dkv_ref, dkr_ref, kc, vp, dk, dv):
        h, i = pl.program_id(0), pl.program_id(1)

        @pl.when(i == 0)
        def _():
            _head_keys(kv_ref, kr_ref, ck, lk, hk, kc, vp)
            dk[...] = jnp.zeros_like(dk)
            dv[...] = jnp.zeros_like(dv)

        q = _rope(q_ref[...], cq[...], lq[...], hq[...]).astype(BF16)
        k, v, d_o = kc[...], vp[...], do_ref[...]
        s = lax.dot_general(q, k, NT, preferred_element_type=F32)
        p = jnp.exp2(s * (ATTN_SCALE * LOG2_E) - l_ref[...] * LOG2_E)
        dob = d_o.astype(BF16)
        dp = lax.dot_general(dob, v, NT, preferred_element_type=F32)
        dd = jnp.sum(d_o * o_ref[...], axis=-1, keepdims=True)
        ds = (p * (dp - dd) * ATTN_SCALE).astype(BF16)
        dq = lax.dot_general(ds, k, NN, preferred_element_type=F32)
        dq_ref[...] = _rope(dq, iq[...], ilq[...], ihq[...]).astype(dq_ref.dtype)
        dk[...] += lax.dot_general(q, ds, TN, preferred_element_type=F32)
        dv[...] += lax.dot_general(dob, p.astype(BF16), TN, preferred_element_type=F32)

        @pl.when(i == nq - 1)
        def _():
            dkh = dk[...].T
            lane = lax.broadcasted_iota(jnp.int32, dkh.shape, 1)
            dkv_ref[...] = jnp.where(lane < 64, dkh, dv[...].T).astype(dkv_ref.dtype)
            rot = _rope(jnp.where((lane >= 64) & (lane < 96), dkh, 0.0), ik[...], ilk[...], ihk[...])

            @pl.when(h == 0)
            def _():
                dkr_ref[...] = rot

            @pl.when(h > 0)
            def _():
                dkr_ref[...] += rot

    qs, keys, kr, tab_q, tab_k, lse_spec = _attn_specs(tq, TT)
    wide = lambda rows: jax.ShapeDtypeStruct((rows, N_HEADS * HEAD_PAD), BF16)
    return pl.pallas_call(
        body, grid=(N_HEADS, nq),
        in_specs=[qs, keys, kr] + [tab_q] * 3 + [tab_k] * 3 + [tab_q] * 3 + [tab_k] * 3 + [qs, qs, lse_spec],
        out_specs=[qs, keys, pl.BlockSpec((TT, HEAD_PAD), lambda h, i: (0, 0))],
        out_shape=[wide(T), wide(TT), jax.ShapeDtypeStruct((TT, HEAD_PAD), F32)],
        scratch_shapes=[pltpu.VMEM((TT, HEAD_PAD), BF16), pltpu.VMEM((TT, HEAD_PAD), BF16),
                        pltpu.VMEM((HEAD_PAD, TT), F32), pltpu.VMEM((HEAD_PAD, TT), F32)],
        compiler_params=_cp(("arbitrary", "arbitrary")), name="attn_bwd",
    )(*_in_hbm([q_raw, kv, pp, cos, slo, shi, cos, slo, shi, cos_i, slo_i, shi_i, cos_i, slo_i, shi_i, o, do, lse]))


def _hbm_specs(n):
    return [pl.BlockSpec(memory_space=pl.ANY)] * n


def _gather_weights(shards):
    n = len(shards)
    halves = [s.shape[0] // 2 for s in shards]

    def body(*refs):
        ins, outs = refs[:n], refs[n:2 * n]
        token, send_sems, recv_sems = refs[2 * n:]
        token[...] = jnp.zeros_like(token)
        mx, my, mc = lax.axis_index("x"), lax.axis_index("y"), lax.axis_index("c")
        j_me = 2 * mx + my
        chips = [(1 - mx, my), (mx, 1 - my), (1 - mx, 1 - my)]

        def half(w, chip_idx, hc):
            return outs[w].at[chip_idx, pl.ds(hc * halves[w], halves[w]), :]

        def copy(w, k, src, dst, to):
            return pltpu.make_async_remote_copy(src_ref=src, dst_ref=dst, send_sem=send_sems.at[w, k],
                                                recv_sem=recv_sems.at[w, k], device_id=to, device_id_type=MESH)

        sends = []
        for w in range(n):
            cp = copy(w, 6, ins[w], outs[w].at[j_me], (mx, my, 1 - mc))
            cp.start()
            sends.append(cp)
        for k, (px, py) in enumerate(chips):
            for w in range(n):
                cp = copy(w, k, ins[w].at[pl.ds(mc * halves[w], halves[w]), :], half(w, j_me, mc), (px, py, mc))
                cp.start()
                sends.append(cp)
        for k, (px, py) in enumerate(chips):
            for w in range(n):
                got = half(w, 2 * px + py, mc)
                copy(w, k, got, got, (px, py, mc)).wait_recv()
                cp = copy(w, 3 + k, got, got, (mx, my, 1 - mc))
                cp.start()
                sends.append(cp)
        for k, (px, py) in enumerate(chips):
            for w in range(n):
                got = half(w, 2 * px + py, 1 - mc)
                copy(w, 3 + k, got, got, (mx, my, 1 - mc)).wait_recv()
        for w in range(n):
            own = outs[w].at[j_me]
            copy(w, 6, own, own, (mx, my, 1 - mc)).wait_recv()
        for cp in sends:
            cp.wait_send()

    res = pl.pallas_call(
        body, out_shape=[jax.ShapeDtypeStruct((4,) + s.shape, s.dtype) for s in shards]
        + [jax.ShapeDtypeStruct((8, 128), F32)],
        in_specs=_hbm_specs(n), out_specs=_hbm_specs(n) + [pl.BlockSpec(memory_space=pltpu.VMEM)],
        scratch_shapes=[pltpu.SemaphoreType.DMA((n, 7)), pltpu.SemaphoreType.DMA((n, 7))],
        name="gather_weights")(*shards)
    return list(res[:n]), res[n]


def _rs_pair(gs, name):
    n = len(gs)
    halves = [g.shape[1] // 2 for g in gs]

    def body(*refs):
        ins, lands = refs[:n], refs[n:2 * n]
        send_sems, recv_sems = refs[2 * n:]
        mx, my, mc = lax.axis_index("x"), lax.axis_index("y"), lax.axis_index("c")
        copies = []
        for w in range(n):
            h = halves[w]
            cp = pltpu.make_async_remote_copy(
                src_ref=ins[w].at[:, pl.ds((1 - mc) * h, h), :], dst_ref=lands[w], send_sem=send_sems.at[w],
                recv_sem=recv_sems.at[w], device_id=(mx, my, 1 - mc), device_id_type=MESH)
            cp.start()
            copies.append(cp)
        for cp in copies:
            cp.wait()

    return pl.pallas_call(
        body, out_shape=[jax.ShapeDtypeStruct((4, h, g.shape[2]), g.dtype) for g, h in zip(gs, halves)],
        in_specs=_hbm_specs(n), out_specs=_hbm_specs(n),
        scratch_shapes=[pltpu.SemaphoreType.DMA((n,)), pltpu.SemaphoreType.DMA((n,))], name=name)(*gs)


def _rs_chips(parts):
    n = len(parts)

    def body(*refs):
        ins, lands = refs[:n], refs[n:2 * n]
        send_sems, recv_sems = refs[2 * n:]
        mx, my, mc = lax.axis_index("x"), lax.axis_index("y"), lax.axis_index("c")
        copies = []
        for k, (px, py) in enumerate([(1 - mx, my), (mx, 1 - my), (1 - mx, 1 - my)]):
            for w in range(n):
                cp = pltpu.make_async_remote_copy(
                    src_ref=ins[w].at[2 * px + py], dst_ref=lands[w].at[k], send_sem=send_sems.at[w, k],
                    recv_sem=recv_sems.at[w, k], device_id=(px, py, mc), device_id_type=MESH)
                cp.start()
                copies.append(cp)
        for cp in copies:
            cp.wait()

    return list(pl.pallas_call(
        body, out_shape=[jax.ShapeDtypeStruct((3,) + p.shape[1:], p.dtype) for p in parts],
        in_specs=_hbm_specs(n), out_specs=_hbm_specs(n),
        scratch_shapes=[pltpu.SemaphoreType.DMA((n, 3)), pltpu.SemaphoreType.DMA((n, 3))], name="rs_chips")(*parts))


def _rs_pair_back(gs, name):
    n = len(gs)

    def body(*refs):
        outs = refs[n:2 * n]
        send_sems, recv_sems = refs[2 * n:]
        mx, my, mc = lax.axis_index("x"), lax.axis_index("y"), lax.axis_index("c")
        copies = []
        for w in range(n):
            h = gs[w].shape[0] // 2
            mine = outs[w].at[pl.ds(mc * h, h), :]
            cp = pltpu.make_async_remote_copy(src_ref=mine, dst_ref=mine, send_sem=send_sems.at[w],
                                              recv_sem=recv_sems.at[w], device_id=(mx, my, 1 - mc), device_id_type=MESH)
            cp.start()
            copies.append(cp)
        for cp in copies:
            cp.wait()

    return pl.pallas_call(
        body, out_shape=[jax.ShapeDtypeStruct(g.shape, g.dtype) for g in gs],
        in_specs=_hbm_specs(n), out_specs=_hbm_specs(n), input_output_aliases={w: w for w in range(n)},
        scratch_shapes=[pltpu.SemaphoreType.DMA((n,)), pltpu.SemaphoreType.DMA((n,))], name=name)(*gs)


_HBM = pl.BlockSpec(memory_space=pltpu.HBM)
_SEM = pl.BlockSpec(memory_space=pltpu.SEMAPHORE)
_EFFECT = pltpu.SideEffectType.DATAFLOW_SIDE_EFFECTING


def _ici_copies(kind, srcs, lands, send_sems, recv_sems):
    n = len(srcs)
    mx, my, mc = lax.axis_index("x"), lax.axis_index("y"), lax.axis_index("c")
    j_me = 2 * mx + my
    copies = []
    if kind == "all":
        for k in range(7):
            a, b, c = (k + 1) >> 2 & 1, (k + 1) >> 1 & 1, (k + 1) & 1
            peer = (1 - mx if a else mx, 1 - my if b else my, 1 - mc if c else mc)
            for w in range(n):
                copies.append(pltpu.make_async_remote_copy(
                    src_ref=srcs[w], dst_ref=lands[w].at[4 * mx + 2 * my + mc], send_sem=send_sems.at[7 * w + k],
                    recv_sem=recv_sems.at[7 * w + k], device_id=peer, device_id_type=MESH))
        return copies
    if kind == "pair":
        for w in range(n):
            h = srcs[w].shape[1] // 2
            copies.append(pltpu.make_async_remote_copy(
                src_ref=srcs[w].at[:, pl.ds((1 - mc) * h, h), :], dst_ref=lands[w], send_sem=send_sems.at[w],
                recv_sem=recv_sems.at[w], device_id=(mx, my, 1 - mc), device_id_type=MESH))
        return copies
    chips = [(1 - mx, my), (mx, 1 - my), (1 - mx, 1 - my)]
    if kind == "finish":
        for w in range(n):
            h = srcs[w].shape[0] // 2
            pushes = [(lands[w].at[2 * px + py, pl.ds(mc * h, h), :],) * 2 for px, py in chips]
            pushes.append((srcs[w], lands[w].at[j_me]))
            for k, (src, dst) in enumerate(pushes):
                copies.append(pltpu.make_async_remote_copy(
                    src_ref=src, dst_ref=dst, send_sem=send_sems.at[4 * w + k], recv_sem=recv_sems.at[4 * w + k],
                    device_id=(mx, my, 1 - mc), device_id_type=MESH))
        return copies
    for k, (px, py) in enumerate(chips):
        for w in range(n):
            if kind == "gather":
                h = srcs[w].shape[0] // 2
                src, dst = srcs[w].at[pl.ds(mc * h, h), :], lands[w].at[j_me, pl.ds(mc * h, h), :]
            else:
                src, dst = srcs[w].at[2 * px + py], lands[w].at[k]
            copies.append(pltpu.make_async_remote_copy(
                src_ref=src, dst_ref=dst, send_sem=send_sems.at[3 * w + k], recv_sem=recv_sems.at[3 * w + k],
                device_id=(px, py, mc), device_id_type=MESH))
    return copies


_SEMS_PER_OPERAND = {"gather": 3, "scatter": 3, "all": 7, "pair": 1, "finish": 4}


def _ici_start(kind, srcs, land_shapes, carry, name, lands=None):
    n = len(srcs)

    def body(*refs):
        ins, lands = refs[:n], refs[n:2 * n]
        send_sems, recv_sems = refs[2 * n + 1], refs[2 * n + 2]
        for cp in _ici_copies(kind, ins, lands, send_sems, recv_sems):
            cp.start()

    hbm = lambda a: pltpu.with_memory_space_constraint(a, pltpu.HBM)
    if lands is None:
        lands = [lax.empty(s, srcs[0].dtype) for s in land_shapes]
    args = [hbm(a) for a in list(srcs) + list(lands) + [carry]]
    n_sem = _SEMS_PER_OPERAND[kind] * n
    out_shape = ([pltpu.SemaphoreType.DMA((n_sem,)), pltpu.SemaphoreType.DMA((n_sem,))]
                 + [pltpu.HBM(a.shape, a.dtype) for a in args])
    res = pl.pallas_call(
        body, name=name, out_shape=out_shape, in_specs=[_HBM] * len(args), out_specs=[_SEM, _SEM] + [_HBM] * len(args),
        input_output_aliases={i: 2 + i for i in range(len(args))},
        compiler_params=pltpu.CompilerParams(has_side_effects=_EFFECT))(*args)
    return res[0], res[1], list(res[2:2 + n]), list(res[2 + n:2 + 2 * n]), res[2 + 2 * n]


def _ici_wait(kind, send_sems, recv_sems, srcs, lands, after, name):
    n = len(srcs)

    def body(*refs):
        ins, zones = refs[:n], refs[n:2 * n]
        for cp in _ici_copies(kind, ins, zones, refs[2 * n], refs[2 * n + 1]):
            cp.wait_send()
            cp.wait_recv()

    args = list(srcs) + list(lands)
    res = pl.pallas_call(
        body, name=name, out_shape=[pltpu.HBM(a.shape, a.dtype) for a in args],
        in_specs=[_HBM] * len(args) + [_SEM, _SEM, pl.BlockSpec(memory_space=pl.ANY)], out_specs=[_HBM] * len(args),
        input_output_aliases={i: i for i in range(len(args))},
        compiler_params=pltpu.CompilerParams(has_side_effects=_EFFECT))(*args, send_sems, recv_sems, after)
    return list(res[:n]), list(res[n:])


def _tile_rows(h, c, itemsize, mult):
    best = h
    for t in range(mult, h + 1, mult):
        if h % t == 0 and t * c * itemsize <= (1 << 21):
            best = t
    return best


def _add_pair(g, land, place, name):
    _, h, c = land.shape
    t = _tile_rows(h, c, 2, 16)
    nb = h // t
    return _ew(lambda ids, u, v: (u.astype(F32) + v.astype(F32),), (4, nb),
               [(g, pl.BlockSpec((None, t, c), lambda j, i, s: (j, s[1] * nb + i, 0))),
                (land, pl.BlockSpec((None, t, c), lambda j, i, s: (j, i, 0)))],
               [(land.shape, BF16, pl.BlockSpec((None, t, c), lambda j, i, s: (j, i, 0)), None)], name, scalars=place)[0]


def _add_pair_many(gs, lands, place, name):
    ins, outs = [], []
    for g, l in zip(gs, lands):
        ins += [(g, pl.BlockSpec(l.shape, lambda i, s: (0, s[1], 0))), (l, pl.BlockSpec(l.shape, lambda i, s: (0, 0, 0)))]
        outs.append((l.shape, BF16, pl.BlockSpec(l.shape, lambda i, s: (0, 0, 0)), None))
    fn = lambda ids, *v: [v[2 * k].astype(F32) + v[2 * k + 1].astype(F32) for k in range(len(gs))]
    return list(_ew(fn, (1,), ins, outs, name, scalars=place))


def _add_chips_many(owns, lands, place, name):
    ins, outs = [], []
    for own, land in zip(owns, lands):
        _, h, c = land.shape
        ins += [(own, pl.BlockSpec((None, h, c), lambda i, s: (s[0], 0, 0))),
                (land, pl.BlockSpec((3, h, c), lambda i, s: (0, 0, 0)))]
        outs.append(((2 * h, c), F32, pl.BlockSpec((h, c), lambda i, s: (s[1], 0)), None))

    def fn(ids, *v):
        return [((v[2 * k].astype(F32) + v[2 * k + 1][0].astype(F32)) + v[2 * k + 1][1].astype(F32))
                + v[2 * k + 1][2].astype(F32) for k in range(len(owns))]

    return list(_ew(fn, (1,), ins, outs, name, scalars=place))


def _add_chips(own, land, place, name):
    _, h, c = land.shape
    t = _tile_rows(h, c, 4, 16)
    nb = h // t

    def fn(ids, a, b):
        return (((a.astype(F32) + b[0].astype(F32)) + b[1].astype(F32)) + b[2].astype(F32),)

    return _ew(fn, (nb,), [(own, pl.BlockSpec((None, t, c), lambda i, s: (s[0], i, 0))),
                           (land, pl.BlockSpec((3, t, c), lambda i, s: (0, i, 0)))],
               [((2 * h, c), F32, pl.BlockSpec((t, c), lambda i, s: (s[1] * nb + i, 0)), None)], name, scalars=place)[0]


W_IN_SEGMENTS = ((0, 256, KV0), (256, 288, KR0 + 64), (288, 672, Q0), (672, 1184, CX0), (1184, 1696, CB0),
                 (1696, 2208, CC0), (2208, 3232, GA0), (3232, 4256, GC0))
W_IN_SHARD = 1064


W_IN_SHARD_PAD = 1088


def _w_in_t_p_from_shards(s):
    pieces = []
    for o0, o1, p0 in sorted(W_IN_SEGMENTS, key=lambda t: t[2]):
        if p0 == KR0 + 64:
            pieces.append(jnp.zeros((64, s.shape[2]), s.dtype))
        for j in range(4):
            lo, hi = max(o0, j * W_IN_SHARD), min(o1, (j + 1) * W_IN_SHARD)
            if lo < hi:
                pieces.append(s[j, lo - j * W_IN_SHARD:hi - j * W_IN_SHARD])
    pieces.append(jnp.zeros((32, s.shape[2]), s.dtype))
    return jnp.concatenate(pieces, axis=0)


def _w_in_t_shards_from_p(g):
    shards = []
    for j in range(4):
        pieces = []
        for o0, o1, p0 in W_IN_SEGMENTS:
            lo, hi = max(o0, j * W_IN_SHARD), min(o1, (j + 1) * W_IN_SHARD)
            if lo < hi:
                pieces.append(g[p0 + lo - o0:p0 + hi - o0])
        pieces.append(jnp.zeros((W_IN_SHARD_PAD - W_IN_SHARD, g.shape[1]), g.dtype))
        shards.append(jnp.concatenate(pieces, axis=0))
    return jnp.stack(shards, axis=0)


def _cols_from_shards(s):
    return jnp.transpose(s, (1, 0, 2)).reshape(s.shape[1], -1)


def _rope_tables(T, TT, inverse):
    rows = T // GRID_W
    row = jnp.repeat(jnp.arange(rows), GRID_W).astype(F32)
    col = jnp.tile(jnp.arange(GRID_W), rows).astype(F32)
    inv = ROPE_THETA ** (-jnp.arange(0, 16, 2, dtype=F32) / 16)
    ang = jnp.concatenate([row[:, None] * inv, col[:, None] * inv], axis=-1)
    cos, sin = jnp.cos(ang), jnp.sin(ang)
    lane = jnp.arange(32)
    src = (lane // 16) * 8 + lane % 8
    lo = ((lane % 16) // 8 == 0).astype(F32)
    sgn = -1.0 if inverse else 1.0
    cos32 = cos[:, src]
    sin_lo32 = -sgn * sin[:, src] * lo
    sin_hi32 = sgn * sin[:, src] * (1.0 - lo)

    def widen(t32, fill):
        t = jnp.concatenate([jnp.full((T, 64), fill, F32), t32, jnp.full((T, 32), fill, F32)], axis=1)
        return jnp.concatenate([t, jnp.full((TT - T, HEAD_PAD), fill, F32)], axis=0)

    return widen(cos32, 1.0), widen(sin_lo32, 0.0), widen(sin_hi32, 0.0)


def _local_step(xx, tgt, mod_lat, mod_ctx, W, late_weights, early_grads, early_continue):
    TT = xx.shape[0]
    T = tgt.shape[0]
    n_lat, n_all = T // ROW_TILE, TT // ROW_TILE
    sh1, sc1, g1, sh2, sc2, g2 = [mod_lat[:, k * D_MODEL:(k + 1) * D_MODEL] for k in range(6)]
    csh1, csc1 = mod_ctx[:, :D_MODEL], mod_ctx[:, D_MODEL:2 * D_MODEL]
    vec = lambda n: _full((1, n))
    row_out = lambda n, dt, rows=T: ((rows, n), dt, _rows(n), None)
    acc_out = lambda n: ((1, n), F32, _full((1, n)), 0)

    def f_norm1(ids, x, g, a_sh, a_sc, b_sh, b_sc):
        ctx = ids[0] >= n_lat
        sh, sc = jnp.where(ctx, b_sh, a_sh), jnp.where(ctx, b_sc, a_sc)
        return ((x * _rms(x) * g) * (1.0 + sc) + sh,)

    (hh,) = _ew(f_norm1, (n_all,), [(xx, _rows(D_MODEL)), (W["norm1_g"], vec(D_MODEL)), (sh1, vec(D_MODEL)),
                                   (sc1, vec(D_MODEL)), (csh1, vec(D_MODEL)), (csc1, vec(D_MODEL))],
                [row_out(D_MODEL, BF16, TT)], "norm1_fwd")
    tm_all = _pick(TT, (768, 256))
    pp = _mm(hh, W["w_in_t"], "nt", TT, P_COLS, D_MODEL, tm=tm_all, tn=2176, tk=D_MODEL, name="w_in_fwd")

    def f_lowrank(ids, ckv, cq, gkv, gq):
        return ckv * _rms(ckv) * gkv, cq * _rms(cq) * gq

    nkv, nq = _ew(f_lowrank, (n_all,), [(pp, _rows(KV_RANK, KV0 // KV_RANK)), (pp, _rows(Q_RANK, Q0 // Q_RANK)),
                                       (W["kv_norm_g"], vec(KV_RANK)), (W["q_norm_g"], vec(Q_RANK))],
                  [row_out(KV_RANK, BF16, TT), row_out(Q_RANK, BF16, TT)], "lowrank_norm_fwd")
    kv = _mm(nkv, W["w_ukv"], "nn", TT, 1024, KV_RANK, tm=tm_all, tn=256, tk=KV_RANK, name="w_ukv_fwd",
             b_spec=pl.BlockSpec((None, KV_RANK, 256), lambda i, j, k: (j, k, 0)))
    q_raw = _mm(nq, W["w_uq_t"], "nt", TT, 1024, Q_RANK, tm=tm_all, tn=1024, tk=Q_RANK, name="w_uq_fwd")

    tabs = _rope_tables(T, TT, inverse=False)
    tabs_inv = _rope_tables(T, TT, inverse=True)
    _, q_raw = late_weights("before_attn", q_raw)
    o_pad, lse = _attn_fwd(q_raw, kv, pp, tabs, T, TT)
    arrived, o_pad = late_weights("after_attn", o_pad)
    W = dict(W, **arrived)
    tm_lat = _pick(T, (1024, 512, 256))
    ya = _mm(o_pad, W["w_attn_out"], "nn", T, D_MODEL, 1024, tm=tm_lat, tn=D_MODEL, tk=1024, name="w_attn_out_fwd")

    tc = 256
    colT = lambda blk0: pl.BlockSpec((T, tc), lambda j: (0, blk0 + j))

    def f_conv(ids, xin, cb, cc, w, b):
        return (cb * _conv(cc * xin, w, b),)

    (e,) = _ew(f_conv, (CONV_DIM // tc,),
               [(pp, colT(CX0 // tc)), (pp, colT(CB0 // tc)), (pp, colT(CC0 // tc)),
                (W["conv_w"], pl.BlockSpec((3, tc), lambda j: (0, j))), (W["conv_b"], pl.BlockSpec((1, tc), lambda j: (0, j)))],
               [((T, CONV_DIM), BF16, colT(0), None)], "conv_fwd")
    yc = _mm(e, W["w_conv_out"], "nn", T, D_MODEL, CONV_DIM, tm=tm_lat, tn=256, tk=CONV_DIM, name="w_conv_out_fwd",
             b_spec=pl.BlockSpec((None, CONV_DIM, 256), lambda i, j, k: (j, k, 0)))

    def f_merge(ids, ga, gc, a, c):
        return (_sigmoid(ga) * a + _sigmoid(gc) * c,)

    (mrg,) = _ew(f_merge, (n_lat,), [(pp, _rows(D_MODEL, 0)), (pp, _rows(D_MODEL, 1)), (ya, _rows(D_MODEL)),
                                    (yc, _rows(D_MODEL))], [row_out(D_MODEL, BF16)], "merge_fwd")
    mo = _mm(mrg, W["w_o"], "nn", T, D_MODEL, D_MODEL, tm=tm_lat, tn=D_MODEL, tk=D_MODEL, name="w_o_fwd")

    def f_norm2(ids, x, m, gate, g, sh, sc):
        x1 = x + gate * m
        return x1, (x1 * _rms(x1) * g) * (1.0 + sc) + sh

    x1, h2 = _ew(f_norm2, (n_lat,), [(xx, _rows(D_MODEL)), (mo, _rows(D_MODEL)), (g1, vec(D_MODEL)),
                                    (W["norm2_g"], vec(D_MODEL)), (sh2, vec(D_MODEL)), (sc2, vec(D_MODEL))],
                 [row_out(D_MODEL, F32), row_out(D_MODEL, BF16)], "norm2_fwd")
    arrived, h2 = late_weights("before_ffn", h2)
    W = dict(W, **arrived)
    up = _mm(h2, W["w_up"], "nn", T, 2 * D_FF, D_MODEL, tm=tm_lat, tn=1408, tk=D_MODEL, name="w_up_fwd",
             b_spec=pl.BlockSpec((None, D_MODEL, 1408), lambda i, j, k: (j, k, 0)))

    n_ff = D_FF // tc
    ffw = lambda off, n=3: pl.BlockSpec((n, tc), lambda j: (0, j + off))

    def f_ffn(ids, ug, uv, wg, wv, bg, bv):
        gate, val = _conv(ug, wg, bg), _conv(uv, wv, bv)
        return (gate * _sigmoid(gate) * val,)

    (act,) = _ew(f_ffn, (n_ff,), [(up, colT(0)), (up, colT(n_ff)), (W["ffn_conv_w"], ffw(0)), (W["ffn_conv_w"], ffw(n_ff)),
                                 (W["ffn_conv_b"], ffw(0, 1)), (W["ffn_conv_b"], ffw(n_ff, 1))],
                 [((T, D_FF), BF16, colT(0), None)], "ffn_act_fwd")
    f = _mm(act, W["w_down"], "nn", T, D_MODEL, D_FF, tm=tm_lat, tn=D_MODEL, tk=D_FF, name="w_down_fwd")

    def f_head(ids, x1_, f_, gate, gf, t):
        x2 = x1_ + gate * f_
        r = _rms(x2)
        xn = x2 * r
        err = xn * gf - t
        loss = 0.5 * jnp.sum(jnp.mean(err * err, axis=-1, keepdims=True))
        dy = err * (1.0 / D_MODEL)
        dx2 = _rms_bwd(dy * gf, xn, r)
        return dx2, dx2 * gate, _colsum(dy * xn), _colsum(dx2 * f_), jnp.full((1, 128), loss, F32)

    dx2, df, dg_f, dg2, loss = _ew(
        f_head, (n_lat,), [(x1, _rows(D_MODEL)), (f, _rows(D_MODEL)), (g2, vec(D_MODEL)), (W["final_g"], vec(D_MODEL)),
                           (tgt, _rows(D_MODEL))],
        [row_out(D_MODEL, F32), row_out(D_MODEL, BF16), acc_out(D_MODEL), acc_out(D_MODEL), acc_out(128)], "loss_head")

    d_w_down = _mm(act, df, "tn", D_FF, D_MODEL, T, tm=1408, tn=D_MODEL, tk=T, name="w_down_dw",
                   out_dtype=BF16).reshape(4, D_FF // 4, D_MODEL)
    da = _mm(df, W["w_down"], "nt", T, D_FF, D_MODEL, tm=tm_lat, tn=1408, tk=D_MODEL, name="w_down_dx")

    tcb = 128
    n_fb = D_FF // tcb
    colb = lambda blk0: pl.BlockSpec((T, tcb), lambda j: (0, blk0 + j))
    ffwb = lambda off, n=3: pl.BlockSpec((n, tcb), lambda j: (0, j + off))
    cvec = ((1, D_FF), F32, pl.BlockSpec((1, tcb), lambda j: (0, j)), None)

    def f_ffn_bwd(ids, ug, uv, d_act, wg, wv, bg, bv):
        sg, sv = _shifts(ug), _shifts(uv)
        gate, val = _conv(ug, wg, bg, sg), _conv(uv, wv, bv, sv)
        s = _sigmoid(gate)
        d_gate = d_act * val * s * (1.0 + gate * (1.0 - s))
        d_val = d_act * gate * s
        wg0, wg1, wg2 = _conv_bwd_w(d_gate, ug, sg)
        wv0, wv1, wv2 = _conv_bwd_w(d_val, uv, sv)
        d_up = [_conv_bwd_x(d_gate, wg), _conv_bwd_x(d_val, wv)]
        return d_up, [_colsum(d_gate), _colsum(d_val), wg0, wg1, wg2, wv0, wv1, wv2]

    d_up3, ffn_stats = _ew(
        f_ffn_bwd, (n_fb,),
        [(up, colb(0)), (up, colb(n_fb)), (da, colb(0)), (W["ffn_conv_w"], ffwb(0)), (W["ffn_conv_w"], ffwb(n_fb)),
         (W["ffn_conv_b"], ffwb(0, 1)), (W["ffn_conv_b"], ffwb(n_fb, 1))],
        [((2, T, D_FF), BF16, pl.BlockSpec((2, T, tcb), lambda j: (0, 0, j)), None),
         ((n_fb, 8, 1, tcb), F32, pl.BlockSpec((None, 8, 1, tcb), lambda j: (j, 0, 0, 0)), None)], "ffn_act_bwd")
    stat = lambda s: ffn_stats[:, s, 0, :].reshape(1, D_FF)
    d_ffn_conv_b = jnp.concatenate([stat(0), stat(1)], axis=1)
    d_ffn_conv_w = jnp.concatenate([jnp.concatenate([stat(2), stat(3), stat(4)], axis=0),
                                    jnp.concatenate([stat(5), stat(6), stat(7)], axis=0)], axis=1)

    tk_t = T
    d_w_up = _mm(h2, d_up3, "tn", D_MODEL, 2 * D_FF, T, tm=D_MODEL, tn=1408, tk=tk_t, name="w_up_dw", out_dtype=BF16,
                 b_spec=pl.BlockSpec((None, tk_t, 1408), lambda i, j, k: (j // 2, k, j % 2)),
                 o_spec=pl.BlockSpec((None, D_MODEL, 1408), lambda i, j, k: (j, i, 0)), out_shape=(4, D_MODEL, 1408))
    dh2 = _mm(d_up3, W["w_up"], "nt", T, D_MODEL, 2 * D_FF, tm=tm_lat, tn=D_MODEL, tk=1408, name="w_up_dx",
              a_spec=pl.BlockSpec((None, tm_lat, 1408), lambda i, j, k: (k // 2, i, k % 2)),
              b_spec=pl.BlockSpec((None, D_MODEL, 1408), lambda i, j, k: (k, j, 0)))

    def f_norm2_bwd(ids, dx2_, dh, x1_, m, g, sc, gate):
        r = _rms(x1_)
        xn = x1_ * r
        dx1 = dx2_ + _rms_bwd(dh * g * (1.0 + sc), xn, r)
        return dx1, dx1 * gate, _colsum(dh), _colsum(dh * xn * g), _colsum(dh * xn * (1.0 + sc)), _colsum(dx1 * m)

    dx1, dmo, dsh2, dsc2, dg_n2, dg1 = _ew(
        f_norm2_bwd, (n_lat,), [(dx2, _rows(D_MODEL)), (dh2, _rows(D_MODEL)), (x1, _rows(D_MODEL)), (mo, _rows(D_MODEL)),
                                (W["norm2_g"], vec(D_MODEL)), (sc2, vec(D_MODEL)), (g1, vec(D_MODEL))],
        [row_out(D_MODEL, F32), row_out(D_MODEL, BF16)] + [acc_out(D_MODEL)] * 4, "norm2_bwd")
    d_w_o = _mm(mrg, dmo, "tn", D_MODEL, D_MODEL, T, tm=D_MODEL, tn=D_MODEL, tk=tk_t, name="w_o_dw",
                out_dtype=BF16).reshape(4, D_MODEL // 4, D_MODEL)
    dmrg = _mm(dmo, W["w_o"], "nt", T, D_MODEL, D_MODEL, tm=tm_lat, tn=D_MODEL, tk=D_MODEL, name="w_o_dx")
    dmrg = early_grads("late", {"w_o": d_w_o, "w_up": d_w_up, "w_down": d_w_down}, dmrg, split=True)

    def f_merge_bwd(ids, dm, ga, gc, a, c):
        sa, sc_ = _sigmoid(ga), _sigmoid(gc)
        return dm * sa, dm * sc_, dm * a * sa * (1.0 - sa), dm * c * sc_ * (1.0 - sc_)

    dya, dyc, dp_ga, dp_gc = _ew(
        f_merge_bwd, (n_lat,), [(dmrg, _rows(D_MODEL)), (pp, _rows(D_MODEL, 0)), (pp, _rows(D_MODEL, 1)),
                                (ya, _rows(D_MODEL)), (yc, _rows(D_MODEL))], [row_out(D_MODEL, BF16)] * 4, "merge_bwd")
    dya = early_continue("late", dya)

    d_w_ao_p = _mm(o_pad, dya, "tn", 1024, D_MODEL, T, tm=1024, tn=D_MODEL, tk=tk_t, name="w_attn_out_dw", out_dtype=BF16)
    do_pad = _mm(dya, W["w_attn_out"], "nt", T, 1024, D_MODEL, tm=tm_lat, tn=1024, tk=D_MODEL, name="w_attn_out_dx")
    d_w_co = _mm(e, dyc, "tn", CONV_DIM, D_MODEL, T, tm=CONV_DIM, tn=256, tk=tk_t, name="w_conv_out_dw", out_dtype=BF16,
                 o_spec=pl.BlockSpec((None, CONV_DIM, 256), lambda i, j, k: (j, i, 0)), out_shape=(4, CONV_DIM, 256))
    de = _mm(dyc, W["w_conv_out"], "nt", T, CONV_DIM, D_MODEL, tm=tm_lat, tn=CONV_DIM, tk=256, name="w_conv_out_dx",
             b_spec=pl.BlockSpec((None, CONV_DIM, 256), lambda i, j, k: (k, j, 0)))

    def f_conv_bwd(ids, xin, cb, cc, d_e, w, b):
        z = cc * xin
        sz = _shifts(z)
        cz = _conv(z, w, b, sz)
        dcz = d_e * cb
        w0, w1, w2 = _conv_bwd_w(dcz, z, sz)
        dz = _conv_bwd_x(dcz, w)
        return dz * cc, d_e * cz, dz * xin, _colsum(dcz), w0, w1, w2

    cvec_c = ((1, CONV_DIM), F32, pl.BlockSpec((1, tc), lambda j: (0, j)), None)
    conv_b = _ew(f_conv_bwd, (CONV_DIM // tc,),
                 [(pp, colT(CX0 // tc)), (pp, colT(CB0 // tc)), (pp, colT(CC0 // tc)), (de, colT(0)),
                  (W["conv_w"], pl.BlockSpec((3, tc), lambda j: (0, j))), (W["conv_b"], pl.BlockSpec((1, tc), lambda j: (0, j)))],
                 [((T, CONV_DIM), BF16, colT(0), None)] * 3 + [cvec_c] * 4, "conv_bwd")
    dp_cx, dp_cb, dp_cc, d_conv_b = conv_b[:4]
    d_conv_w = jnp.concatenate(conv_b[4:7], axis=0)

    dq_raw, dkv, dp_kr = _attn_bwd(q_raw, kv, pp, o_pad, do_pad, lse, tabs, tabs_inv, T, TT)

    tk_a = TT
    d_w_uq_t = _mm(nq, dq_raw, "tn", Q_RANK, 1024, T, tm=Q_RANK, tn=1024, tk=T, name="w_uq_dw", transpose_out=True)
    dnq = _mm(dq_raw, W["w_uq_t"], "nn", T, Q_RANK, 1024, tm=tm_lat, tn=Q_RANK, tk=1024, name="w_uq_dx")
    d_w_ukv = _mm(nkv, dkv, "tn", KV_RANK, 1024, TT, tm=KV_RANK, tn=256, tk=tk_a, name="w_ukv_dw", out_dtype=BF16,
                  o_spec=pl.BlockSpec((None, KV_RANK, 256), lambda i, j, k: (j, i, 0)), out_shape=(4, KV_RANK, 256))
    dnkv = _mm(dkv, W["w_ukv"], "nt", TT, KV_RANK, 1024, tm=tm_all, tn=KV_RANK, tk=256, name="w_ukv_dx",
               b_spec=pl.BlockSpec((None, KV_RANK, 256), lambda i, j, k: (k, j, 0)))
    dnkv = early_grads("mid", {
        "w_attn_out": jnp.transpose(d_w_ao_p.reshape(N_HEADS, HEAD_PAD, 4, 256)[:, 64:], (2, 0, 1, 3)).reshape(
            4, N_HEADS * 64, 256),
        "w_conv_out": d_w_co,
        "w_uq": d_w_uq_t.reshape(4, 2, HEAD_PAD, Q_RANK)[:, :, :QK_DIM].reshape(4, 2 * QK_DIM, Q_RANK).astype(BF16),
        "w_ukv": d_w_ukv}, dnkv)

    def f_lowrank_bwd(ids, ckv, cq, dkv_, dq_, gkv, gq, ga, gc, cx, cb, cc, kr):
        rk, rq = _rms(ckv), _rms(cq)
        nk, nq_ = ckv * rk, cq * rq
        lat = ids[0] < n_lat
        dq_ = jnp.where(lat, dq_, 0.0)
        pieces = [jnp.where(lat, a, jnp.zeros_like(a)) for a in (ga, gc, cx, cb, cc)]
        pieces += [_rms_bwd(dkv_ * gkv, nk, rk).astype(BF16), _rms_bwd(dq_ * gq, nq_, rq).astype(BF16), kr.astype(BF16)]
        return jnp.concatenate(pieces, axis=1), _colsum(dkv_ * nk), _colsum(dq_ * nq_)

    lat_rows = lambda n: pl.BlockSpec((ROW_TILE, n), lambda i: (jnp.minimum(i, n_lat - 1), 0))
    dpp, dg_kv, dg_q = _ew(
        f_lowrank_bwd, (n_all,), [(pp, _rows(KV_RANK, KV0 // KV_RANK)), (pp, _rows(Q_RANK, Q0 // Q_RANK)),
                                  (dnkv, _rows(KV_RANK)), (dnq, lat_rows(Q_RANK)), (W["kv_norm_g"], vec(KV_RANK)),
                                  (W["q_norm_g"], vec(Q_RANK)), (dp_ga, lat_rows(D_MODEL)), (dp_gc, lat_rows(D_MODEL)),
                                  (dp_cx, lat_rows(CONV_DIM)), (dp_cb, lat_rows(CONV_DIM)), (dp_cc, lat_rows(CONV_DIM)),
                                  (dp_kr, _rows(HEAD_PAD))],
        [row_out(P_COLS, BF16, TT), acc_out(KV_RANK), acc_out(Q_RANK)], "lowrank_norm_bwd")
    d_w_in_t = _mm(hh, dpp, "tn", D_MODEL, P_COLS, TT, tm=512, tn=2176, tk=TT, name="w_in_dw", out_dtype=BF16,
                   transpose_out=True)
    dhh = _mm(dpp, W["w_in_t"], "nn", TT, D_MODEL, P_COLS, tm=tm_all, tn=512, tk=2176, name="w_in_dx")

    def f_norm1_bwd(ids, x, dh, dres, g, sc):
        r = _rms(x)
        xn = x * r
        return (dres + _rms_bwd(dh * g * (1.0 + sc), xn, r), _colsum(dh), _colsum(dh * xn * g),
                _colsum(dh * xn * (1.0 + sc)))

    grad_x, dsh1, dsc1, dg_n1 = _ew(
        f_norm1_bwd, (n_lat,), [(xx, _rows(D_MODEL)), (dhh, _rows(D_MODEL)), (dx1, _rows(D_MODEL)),
                                (W["norm1_g"], vec(D_MODEL)), (sc1, vec(D_MODEL))],
        [row_out(D_MODEL, F32)] + [acc_out(D_MODEL)] * 3, "norm1_bwd")

    def f_norm1_ctx_bwd(ids, x, dh, g, sc):
        xn = x * _rms(x)
        return _colsum(dh), _colsum(dh * xn * g), _colsum(dh * xn * (1.0 + sc))

    n_ctx = n_all - n_lat
    dcsh1, dcsc1, dg_n1c = _ew(
        f_norm1_ctx_bwd, (n_ctx,), [(xx, _rows(D_MODEL, 0, n_lat)), (dhh, _rows(D_MODEL, 0, n_lat)),
                                    (W["norm1_g"], vec(D_MODEL)), (csc1, vec(D_MODEL))], [acc_out(D_MODEL)] * 3,
        "norm1_ctx_bwd")

    big = {"w_in": _w_in_t_shards_from_p(d_w_in_t).astype(BF16)}
    zero = jnp.zeros((1, 4 * D_MODEL), F32)
    small = {
        "dmod_lat": jnp.concatenate([dsh1, dsc1, dg1, dsh2, dsc2, dg2], axis=1),
        "dmod_ctx": jnp.concatenate([dcsh1, dcsc1, zero], axis=1),
        "norm1_g": dg_n1 + dg_n1c, "norm2_g": dg_n2, "final_g": dg_f, "q_norm_g": dg_q, "kv_norm_g": dg_kv,
        "conv_b": d_conv_b, "conv_w": d_conv_w.reshape(1, -1), "ffn_conv_b": d_ffn_conv_b,
        "ffn_conv_w": d_ffn_conv_w.reshape(1, -1),
    }
    return grad_x, loss, big, small


SMALL = (("dmod_lat", 6144), ("dmod_ctx", 6144), ("norm1_g", 1024), ("norm2_g", 1024), ("final_g", 1024),
         ("q_norm_g", 384), ("kv_norm_g", 256), ("conv_b", 512), ("conv_w", 1536), ("ffn_conv_b", 5632),
         ("ffn_conv_w", 16896), ("loss", 128))
SMALL_ROWS = 320


def _adam_update(w, g, m, v):
    c1, c2 = 1.0 - ADAM_B1 ** ADAM_STEP, 1.0 - ADAM_B2 ** ADAM_STEP
    m2 = ADAM_B1 * m + (1.0 - ADAM_B1) * g
    v2 = ADAM_B2 * v + (1.0 - ADAM_B2) * (g * g)
    return [-ADAM_LR * ((m2 / c1) / (jnp.sqrt(v2 / c2) + ADAM_EPS) + ADAM_WD * w), m2, v2]


def _adamw(w, g, m, v, name):
    R, C = w.shape
    tr = 8 if R % 8 == 0 else R
    for t in range(8, R + 1, 8):
        if R % t == 0 and t * C * 4 <= (1 << 20):
            tr = t
    spec = pl.BlockSpec((tr, C), lambda i: (i, 0))
    return _ew(lambda ids, *vals: _adam_update(*vals), (R // tr,), [(w, spec), (g, spec), (m, spec), (v, spec)],
               [((R, C), F32, spec, None)] * 3, name)


def kernel(x, c, ctx, c_ctx, w_ada, b_ada, norm1_g, w_in, q_norm_g, kv_norm_g, w_uq, w_ukv, conv_w, conv_b, w_attn_out, w_conv_out, w_o, norm2_g, w_up, ffn_conv_w, ffn_conv_b, w_down, final_g, loss_target, m_c_ctx, m_w_ada, m_b_ada, m_norm1_g, m_w_in, m_q_norm_g, m_kv_norm_g, m_w_uq, m_w_ukv, m_conv_w, m_conv_b, m_w_attn_out, m_w_conv_out, m_w_o, m_norm2_g, m_w_up, m_ffn_conv_w, m_ffn_conv_b, m_w_down, m_final_g, v_c_ctx, v_w_ada, v_b_ada, v_norm1_g, v_w_in, v_q_norm_g, v_kv_norm_g, v_w_uq, v_w_ukv, v_conv_w, v_conv_b, v_w_attn_out, v_w_conv_out, v_w_o, v_norm2_g, v_w_up, v_ffn_conv_w, v_ffn_conv_b, v_w_down, v_final_g):
    mx, my, mc = lax.axis_index("x"), lax.axis_index("y"), lax.axis_index("c")
    chip = 2 * mx + my
    dev = 4 * mx + 2 * my + mc
    T, Tc = x.shape[1], ctx.shape[1]
    TT = T + Tc
    w_in_t, m_w_in_t, v_w_in_t = (jnp.transpose(a[0]) for a in (w_in, m_w_in, v_w_in))
    w_uq_t, m_w_uq_t, v_w_uq_t = (jnp.transpose(a[0]) for a in (w_uq, m_w_uq, v_w_uq))
    conv_sh = jnp.concatenate([conv_w[0], ffn_conv_w[0]], axis=1)
    pay1 = jnp.concatenate([jnp.pad(c, ((0, 7), (0, 0))), jnp.pad(conv_sh, ((0, 5), (0, 0)))], axis=1)
    c_send, c_recv, c_src, c_land, zero0 = _ici_start("all", [pay1], [(8, 8, 2560)], jnp.zeros((8, 128), F32),
                                                      "cond_start")
    w_in_bf = (jnp.pad(w_in_t, ((0, W_IN_SHARD_PAD - W_IN_SHARD), (0, 0))) + zero0[0, 0]).astype(BF16)
    shards = {"w_in": w_in_bf, "w_uq": w_uq_t, "w_ukv": w_ukv[0], "w_attn_out": w_attn_out[0],
              "w_conv_out": w_conv_out[0], "w_o": w_o[0], "w_up": w_up[0], "w_down": w_down[0]}
    (pay1,), (c_land,) = _ici_wait("all", c_send, c_recv, c_src, c_land, shards["w_in"], "cond_wait")
    got1 = lax.dynamic_update_slice(c_land, pay1[None], (dev, 0, 0))
    c_all = got1[:, 0, :D_MODEL]
    conv_all = got1[0::2, :3, D_MODEL:]
    conv_w_full = _cols_from_shards(conv_all[:, :, :128])
    ffn_conv_w_full = _cols_from_shards(conv_all[:, :, 128:])

    cond = jnp.concatenate([c_all, c_ctx.reshape(1, D_MODEL), jnp.zeros((7, D_MODEL), F32)], axis=0)

    def f_silu(ids, v):
        return (v * _sigmoid(v),)

    (s16,) = _ew(f_silu, (1,), [(cond, _full((16, D_MODEL)))], [((16, D_MODEL), F32, _full((16, D_MODEL)), None)], "silu_cond")
    mod_sh = _mm(s16, w_ada[0], "nn", 16, 1536, D_MODEL, tm=16, tn=768, tk=D_MODEL, name="w_ada_fwd")
    m_send, m_recv, m_src, m_land, zero1 = _ici_start("all", [mod_sh], [(8, 16, 1536)], jnp.zeros((8, 128), F32),
                                                      "mod_start")
    shards["w_ukv"] = w_ukv[0] + zero1[0, 0]

    names = [n for n, _ in BIG]
    first = [n for n in names if n not in GATHER_LATE]
    gathered, zero = _gather_weights([shards[n].astype(BF16) for n in first])
    full = dict(zip(first, gathered))
    (mod_mine,), (m_land,) = _ici_wait("all", m_send, m_recv, m_src, m_land, gathered[0], "mod_wait")
    got2 = lax.dynamic_update_slice(m_land, mod_mine[None], (dev, 0, 0))
    mod_all = _cols_from_shards(got2[0::2]) + b_ada
    mod_lat = lax.dynamic_slice_in_dim(mod_all, dev, 1, axis=0)
    mod_ctx = mod_all[8:9]
    xx = jnp.concatenate([x[0], ctx[0]], axis=0)
    late_groups = {"g1": ("w_attn_out", "w_conv_out", "w_o"), "g2": ("w_up", "w_down")}
    flight = {}
    for tag, group in late_groups.items():
        bf = [(shards[n] + zero[0, 0]).astype(BF16) for n in group]
        flight[tag] = _ici_start("gather", bf, [(4,) + s.shape for s in bf], xx, "gather_" + tag + "_start")
        xx = flight[tag][4]

    def chip_stage_done(tag, x):
        send, recv, src, land, _ = flight[tag]
        src, land = _ici_wait("gather", send, recv, src, land, x, "gather_" + tag + "_wait")
        flight[tag] = _ici_start("finish", src, None, x, "finish_" + tag + "_start", lands=land)
        return flight[tag][4]

    def arrived(tag, x):
        send, recv, src, land, _ = flight[tag]
        return dict(zip(late_groups[tag], _ici_wait("finish", send, recv, src, land, x, "finish_" + tag + "_wait")[1]))

    def late_weights(point, x):
        if point == "before_attn":
            return {}, chip_stage_done("g1", x)
        if point == "after_attn":
            got = arrived("g1", x)
            wao = _cols_from_shards(got["w_attn_out"]).reshape(N_HEADS, 64, D_MODEL)
            ready = {"w_attn_out": jnp.pad(wao, ((0, 0), (64, 0), (0, 0))).reshape(N_HEADS * HEAD_PAD, D_MODEL),
                     "w_conv_out": got["w_conv_out"], "w_o": got["w_o"].reshape(D_MODEL, D_MODEL)}
            return ready, chip_stage_done("g2", x)
        got = arrived("g2", x)
        return {"w_up": got["w_up"], "w_down": got["w_down"].reshape(D_FF, D_MODEL)}, x

    wuq_t = full["w_uq"].reshape(N_HEADS, QK_DIM, Q_RANK)
    W = {
        "w_in_t": _w_in_t_p_from_shards(full["w_in"]),
        "w_uq_t": jnp.pad(wuq_t, ((0, 0), (0, HEAD_PAD - QK_DIM), (0, 0))).reshape(N_HEADS * HEAD_PAD, Q_RANK),
        "w_ukv": full["w_ukv"],
        "norm1_g": norm1_g, "norm2_g": norm2_g, "final_g": final_g.reshape(1, D_MODEL), "q_norm_g": q_norm_g,
        "kv_norm_g": kv_norm_g, "conv_w": conv_w_full, "conv_b": conv_b, "ffn_conv_w": ffn_conv_w_full,
        "ffn_conv_b": ffn_conv_b,
    }

    place = jnp.stack([chip, mc]).astype(jnp.int32)
    early = {}

    pending = {}

    def scatter(tag, group, gs, from_sib, carry):
        if tag == "mid":
            sums = _add_pair_many(gs, from_sib, place, "rs_pair_add_mid")
        else:
            sums = [_add_pair(gs[w], from_sib[w], place, "rs_pair_add_" + n) for w, n in enumerate(group)]
        send, recv, sums, land, carry = _ici_start(
            "scatter", sums, [(3,) + s.shape[1:] for s in sums], carry, "rs_chips_" + tag + "_start")
        early[tag] = (group, send, recv, sums, land)
        return carry

    def early_grads(tag, g, carry, split=False):
        gs = list(g.values())
        if not split:
            return scatter(tag, list(g), gs, _rs_pair(gs, "rs_pair_" + tag), carry)
        send, recv, gs, land, carry = _ici_start(
            "pair", gs, [(4, s.shape[1] // 2, s.shape[2]) for s in gs], carry, "rs_pair_" + tag + "_start")
        pending[tag] = (list(g), send, recv, gs, land)
        return carry

    def early_continue(tag, carry):
        group, send, recv, gs, land = pending[tag]
        gs, from_sib = _ici_wait("pair", send, recv, gs, land, carry, "rs_pair_" + tag + "_wait")
        return scatter(tag, group, gs, from_sib, carry)

    grad_x, loss_part, gbig, gsmall = _local_step(xx, loss_target[0], mod_lat, mod_ctx, W, late_weights, early_grads,
                                                  early_continue)

    gsmall["loss"] = loss_part
    pay3 = jnp.concatenate([gsmall[n].reshape(-1) for n, _ in SMALL])
    pay3 = jnp.pad(pay3, (0, SMALL_ROWS * 128 - pay3.shape[0])).reshape(SMALL_ROWS, 128)
    s_send, s_recv, s_src, s_land, w_in_thru = _ici_start("all", [pay3], [(8, SMALL_ROWS, 128)], gbig["w_in"],
                                                         "small_start")
    gbig = {"w_in": w_in_thru}

    after_small = early_grads("last", gbig, s_src[0])

    (pay3,), (s_land,) = _ici_wait("all", s_send, s_recv, [after_small], s_land, early["last"][3][0], "small_wait")
    got3 = lax.dynamic_update_slice(s_land, pay3[None], (dev, 0, 0)).reshape(8 * SMALL_ROWS, 128)

    def f_sum8(ids, a):
        s = a[0:SMALL_ROWS]
        for d in range(1, 8):
            s = s + a[d * SMALL_ROWS:(d + 1) * SMALL_ROWS]
        return (s,)

    (vsum,) = _ew(f_sum8, (1,), [(got3, _full((8 * SMALL_ROWS, 128)))],
                  [((SMALL_ROWS, 128), F32, _full((SMALL_ROWS, 128)), None)], "sum_small")
    vflat = vsum.reshape(-1)
    gvec, off = {}, 0
    for n, size in SMALL:
        gvec[n] = vflat[off:off + size]
        off += size
    loss = gvec["loss"][0]
    dmod_rows = got3.reshape(8, SMALL_ROWS * 128)[:, :6 * D_MODEL]
    dm16 = jnp.concatenate([dmod_rows, gvec["dmod_ctx"].reshape(1, -1), jnp.zeros((7, 6 * D_MODEL), F32)], axis=0)

    def f_colsum(ids, a):
        return (_colsum(a),)

    (g_b_ada,) = _ew(f_colsum, (1,), [(dm16, _full((16, 6 * D_MODEL)))],
                     [((1, 6 * D_MODEL), F32, _full((1, 6 * D_MODEL)), None)], "b_ada_grad")
    dm_sh = lax.dynamic_slice_in_dim(dm16, chip * 1536, 1536, axis=1)
    g_w_ada = _mm(s16, dm_sh, "tn", D_MODEL, 1536, 16, tm=512, tn=768, tk=16, name="w_ada_dw")
    dcond_part = _mm(dm_sh, w_ada[0], "nt", 16, D_MODEL, 1536, tm=16, tn=512, tk=1536, name="w_ada_dx")
    d_send, d_recv, d_src, d_land, vsum = _ici_start("all", [dcond_part[8:16]], [(8, 8, D_MODEL)], vsum, "dcond_start")

    def finish(tags, after):
        done, halves = [], []
        for tag in tags:
            tag_names, send, recv, sums, land = early[tag]
            sums, land = _ici_wait("scatter", send, recv, sums, land, after, "rs_chips_" + tag + "_wait")
            done += tag_names
            if tag == "mid":
                halves += _add_chips_many(sums, land, place, "rs_chip_add_mid")
            else:
                halves += [_add_chips(a, b, place, "rs_chip_add_" + n) for a, b, n in zip(sums, land, tag_names)]
        return dict(zip(done, _rs_pair_back(halves, "rs_pair_back_" + tags[0])))

    grads, deltas, new_m, new_v = {}, {}, {}, {}

    raw = {}

    def adam(n, w_, m_, v_, g, transposed):
        d_, m2, v2 = _adamw(w_, g, m_, v_, "adamw_" + n)
        raw[n] = d_
        back = (lambda a: jnp.transpose(a)[None]) if transposed else (lambda a: a[None])
        grads[n], deltas[n], new_m[n], new_v[n] = back(g[:w_.shape[0]]), back(d_), back(m2), back(v2)

    gw = finish(["late", "mid"], grad_x)
    adam("w_ada", w_ada[0], m_w_ada[0], v_w_ada[0], g_w_ada, False)
    for n, (w_, m_, v_) in {"w_o": (w_o, m_w_o, v_w_o), "w_up": (w_up, m_w_up, v_w_up),
                            "w_down": (w_down, m_w_down, v_w_down)}.items():
        adam(n, w_[0], m_[0], v_[0], gw[n], False)
    gw_in = finish(["last"], raw["w_up"])
    adam("w_in", w_in_t, m_w_in_t, v_w_in_t, gw_in["w_in"], True)

    (dcond_mine,), (d_land,) = _ici_wait("all", d_send, d_recv, d_src, d_land, raw["w_in"], "dcond_wait")
    got4 = lax.dynamic_update_slice(d_land, dcond_mine[None], (dev, 0, 0))[0::2, 0]

    def f_c_ctx(ids, parts, cc):
        s = _sigmoid(cc)
        d = parts[0:1] + parts[1:2] + parts[2:3] + parts[3:4]
        return (d * s * (1.0 + cc * (1.0 - s)),)

    (g_c_ctx,) = _ew(f_c_ctx, (1,), [(got4, _full((4, D_MODEL))), (c_ctx.reshape(1, D_MODEL), _full((1, D_MODEL)))],
                     [((1, D_MODEL), F32, _full((1, D_MODEL)), None)], "c_ctx_grad")

    conv_w_g = lax.dynamic_slice_in_dim(gvec["conv_w"].reshape(3, CONV_DIM), chip * 128, 128, axis=1)
    ffn_conv_w_g = lax.dynamic_slice_in_dim(gvec["ffn_conv_w"].reshape(3, 2 * D_FF), chip * 1408, 1408, axis=1)
    vec_params = (("c_ctx", c_ctx, m_c_ctx, v_c_ctx, g_c_ctx), ("b_ada", b_ada, m_b_ada, v_b_ada, g_b_ada),
                  ("norm1_g", norm1_g, m_norm1_g, v_norm1_g, gvec["norm1_g"]),
                  ("q_norm_g", q_norm_g, m_q_norm_g, v_q_norm_g, gvec["q_norm_g"]),
                  ("kv_norm_g", kv_norm_g, m_kv_norm_g, v_kv_norm_g, gvec["kv_norm_g"]),
                  ("conv_w", conv_w, m_conv_w, v_conv_w, conv_w_g), ("conv_b", conv_b, m_conv_b, v_conv_b, gvec["conv_b"]),
                  ("norm2_g", norm2_g, m_norm2_g, v_norm2_g, gvec["norm2_g"]),
                  ("ffn_conv_w", ffn_conv_w, m_ffn_conv_w, v_ffn_conv_w, ffn_conv_w_g),
                  ("ffn_conv_b", ffn_conv_b, m_ffn_conv_b, v_ffn_conv_b, gvec["ffn_conv_b"]),
                  ("final_g", final_g, m_final_g, v_final_g, gvec["final_g"]))
    two_d = lambda a: a.reshape((-1, a.shape[-1]))
    many = [p + ((lambda r, s=p[1].shape: r.reshape(s)),) for p in vec_params]
    for n, w_, m_, v_ in (("w_ukv", w_ukv, m_w_ukv, v_w_ukv), ("w_attn_out", w_attn_out, m_w_attn_out, v_w_attn_out),
                          ("w_conv_out", w_conv_out, m_w_conv_out, v_w_conv_out)):
        many.append((n, w_, m_, v_, gw[n], (lambda r, s=w_.shape: r.reshape(s))))
    many.append(("w_uq", w_uq_t, m_w_uq_t, v_w_uq_t, gw["w_uq"], lambda r: jnp.transpose(r)[None]))

    def f_adam_many(ids, *vals):
        out = []
        for k in range(len(many)):
            out += _adam_update(*vals[4 * k:4 * k + 4])
        return out

    ins_v, outs_v = [], []
    for p in many:
        shp = two_d(p[1]).shape
        ins_v += [(two_d(a), _full(shp)) for a in (p[1], p[4], p[2], p[3])]
        outs_v += [(shp, F32, _full(shp), None)] * 3
    res_v = _ew(f_adam_many, (1,), ins_v, outs_v, "adamw_small")
    for k, p in enumerate(many):
        n, post = p[0], p[5]
        grads[n] = post(two_d(p[4]))
        deltas[n], new_m[n], new_v[n] = (post(r) for r in res_v[3 * k:3 * k + 3])

    order = ("c_ctx", "w_ada", "b_ada", "norm1_g", "w_in", "q_norm_g", "kv_norm_g", "w_uq", "w_ukv", "conv_w", "conv_b",
             "w_attn_out", "w_conv_out", "w_o", "norm2_g", "w_up", "ffn_conv_w", "ffn_conv_b", "w_down", "final_g")
    return (loss, grad_x[None], *[grads[n] for n in order], *[deltas[n] for n in order],
            *[new_m[n] for n in order], *[new_v[n] for n in order])
```

```python
import functools

import jax
import jax.numpy as jnp
import numpy as np
from jax import lax
from jax.experimental import pallas as pl
from jax.experimental.pallas import tpu as pltpu

F32, BF16 = jnp.float32, jnp.bfloat16
MESH = pl.DeviceIdType.MESH

D_MODEL = 1024
N_HEADS = 8
HEAD_PAD = 128
QK_DIM = 96
Q_RANK, KV_RANK = 384, 256
CONV_DIM = 512
D_FF = 2816
GRID_W = 64
ROPE_THETA = 10000.0
EPS = 1e-6
GA0, GC0, CX0, CB0, CC0, KV0, Q0, KR0, P_COLS = 0, 1024, 2048, 2560, 3072, 3584, 3840, 4224, 4352
ROW_TILE = 256
VMEM_LIMIT_BYTES = 48 * 1024 * 1024

ADAM_LR, ADAM_B1, ADAM_B2, ADAM_EPS, ADAM_WD, ADAM_STEP = 0.001, 0.9, 0.999, 1e-08, 0.01, 10

BIG = (("w_in", (1088, 1024)), ("w_uq", (192, 384)), ("w_ukv", (256, 256)), ("w_attn_out", (512, 256)),
       ("w_conv_out", (512, 256)), ("w_o", (256, 1024)), ("w_up", (1024, 1408)), ("w_down", (704, 1024)))

GATHER_LATE = ("w_attn_out", "w_conv_out", "w_o", "w_up", "w_down")

NN = (((1,), (0,)), ((), ()))
NT = (((1,), (1,)), ((), ()))
TN = (((0,), (0,)), ((), ()))


def _cp(sem):
    return pltpu.CompilerParams(dimension_semantics=sem, vmem_limit_bytes=VMEM_LIMIT_BYTES)


PIN_BYTES = 1 << 19


def _in_hbm(arrays):
    return [pltpu.with_memory_space_constraint(a, pltpu.HBM) if a.size * a.dtype.itemsize >= PIN_BYTES else a
            for a in arrays]


def _out(shape, dtype):
    n = 1
    for d in shape:
        n *= d
    big = n * jnp.dtype(dtype).itemsize >= PIN_BYTES
    return pltpu.HBM(shape, dtype) if big else jax.ShapeDtypeStruct(shape, dtype)


def _pick(n, prefs):
    for p in prefs:
        if n % p == 0:
            return p
    return n


def _mm(a, b, mode, M, N, K, *, tm, tn, tk, name, out_dtype=F32, a_spec=None, b_spec=None, o_spec=None,
        out_shape=None, transpose_out=False):
    assert M % tm == 0 and N % tn == 0 and K % tk == 0, (name, M, N, K, tm, tn, tk)
    nk = K // tk
    dims = {"nn": NN, "nt": NT, "tn": TN}[mode]
    if a_spec is None:
        a_spec = (pl.BlockSpec((tk, tm), lambda i, j, k: (k, i)) if mode == "tn"
                  else pl.BlockSpec((tm, tk), lambda i, j, k: (i, k)))
    if b_spec is None:
        b_spec = (pl.BlockSpec((tn, tk), lambda i, j, k: (j, k)) if mode == "nt"
                  else pl.BlockSpec((tk, tn), lambda i, j, k: (k, j)))
    if o_spec is None:
        o_spec = (pl.BlockSpec((tn, tm), lambda i, j, k: (j, i)) if transpose_out
                  else pl.BlockSpec((tm, tn), lambda i, j, k: (i, j)))
    if out_shape is None:
        out_shape = (N, M) if transpose_out else (M, N)

    def emit(o_ref, val):
        o_ref[...] = (val.T if transpose_out else val).astype(o_ref.dtype)

    def body(a_ref, b_ref, o_ref, *scratch):
        part = lax.dot_general(a_ref[...].astype(BF16), b_ref[...].astype(BF16), dims, preferred_element_type=F32)
        if nk == 1:
            emit(o_ref, part)
            return
        acc_ref, = scratch
        k = pl.program_id(2)

        @pl.when(k == 0)
        def _():
            acc_ref[...] = part

        @pl.when((k > 0) & (k < nk - 1))
        def _():
            acc_ref[...] += part

        @pl.when(k == nk - 1)
        def _():
            emit(o_ref, acc_ref[...] + part)

    return pl.pallas_call(
        body, grid=(M // tm, N // tn, nk), in_specs=[a_spec, b_spec], out_specs=o_spec,
        out_shape=_out(out_shape, out_dtype),
        scratch_shapes=[pltpu.VMEM((tm, tn), F32)] if nk > 1 else [],
        compiler_params=_cp(("parallel", "parallel", "arbitrary")), name=name)(*_in_hbm([a, b]))


def _ew(fn, grid, ins, outs, name, scalars=None):
    n_in = len(ins)
    n_sc = 0 if scalars is None else 1

    def store(ref, val, acc, ids):
        if isinstance(val, (list, tuple)):
            for h, v in enumerate(val):
                ref[h] = v.astype(ref.dtype)
            return
        if acc is None:
            ref[...] = val.astype(ref.dtype)
            return

        @pl.when(ids[acc] == 0)
        def _():
            ref[...] = val.astype(ref.dtype)

        @pl.when(ids[acc] > 0)
        def _():
            ref[...] += val.astype(ref.dtype)

    def body(*refs):
        refs = refs[n_sc:]
        ids = tuple(pl.program_id(a) for a in range(len(grid)))
        vals = fn(ids, *[r[...] for r in refs[:n_in]])
        for ref, val, (_, _, _, acc) in zip(refs[n_in:], vals, outs):
            store(ref, val, acc, ids)

    acc_axes = {o[3] for o in outs if o[3] is not None}
    sem = tuple("arbitrary" if a in acc_axes else "parallel" for a in range(len(grid)))
    in_specs, out_specs = [s for _, s in ins], [o[2] for o in outs]
    out_shape = [_out(o[0], o[1]) for o in outs]
    args = _in_hbm([a for a, _ in ins])
    if scalars is None:
        return pl.pallas_call(body, grid=grid, in_specs=in_specs, out_specs=out_specs, out_shape=out_shape,
                              compiler_params=_cp(sem), name=name)(*args)
    spec = pltpu.PrefetchScalarGridSpec(num_scalar_prefetch=1, grid=grid, in_specs=in_specs, out_specs=out_specs)
    return pl.pallas_call(body, grid_spec=spec, out_shape=out_shape, compiler_params=_cp(sem), name=name)(scalars, *args)


def _rows(width, cblk=0, roff=0, tr=ROW_TILE):
    return pl.BlockSpec((tr, width), lambda i: (i + roff, cblk))


def _full(shape):
    nd = len(shape)
    return pl.BlockSpec(shape, lambda *_: (0,) * nd)


def _sigmoid(x):
    return 1.0 / (1.0 + jnp.exp2(x * (-1.4426950408889634)))


def _rms(x):
    return lax.rsqrt(jnp.mean(x * x, axis=-1, keepdims=True) + EPS)


def _rms_bwd(dn, xn, r):
    return r * (dn - xn * jnp.mean(dn * xn, axis=-1, keepdims=True))


def _colsum(x):
    return jnp.sum(x, axis=0, keepdims=True)


def _shifts(x):
    n = x.shape[0]
    rows = lax.broadcasted_iota(jnp.int32, x.shape, 0)
    return jnp.where(rows == 0, 0.0, pltpu.roll(x, 1, 0)), jnp.where(rows == n - 1, 0.0, pltpu.roll(x, n - 1, 0))


def _conv(x, w, b, shifted=None):
    prev, nxt = _shifts(x) if shifted is None else shifted
    return b + prev * w[0:1] + x * w[1:2] + nxt * w[2:3]


def _conv_bwd_x(dy, w):
    prev, nxt = _shifts(dy)
    return nxt * w[0:1] + dy * w[1:2] + prev * w[2:3]


def _conv_bwd_w(dy, x, shifted):
    prev, nxt = shifted
    return _colsum(dy * prev), _colsum(dy * x), _colsum(dy * nxt)


def _rope(x, cos, sin_lo, sin_hi):
    return x * cos + pltpu.roll(x, HEAD_PAD - 8, 1) * sin_lo + pltpu.roll(x, 8, 1) * sin_hi


ATTN_SCALE = QK_DIM ** -0.5
LOG2_E = 1.4426950408889634


def _head_keys(kv_ref, kr_ref, cos_ref, slo_ref, shi_ref, kc_ref, vp_ref):
    kv = kv_ref[...]
    lane = lax.broadcasted_iota(jnp.int32, kv.shape, 1)
    kc_ref[...] = jnp.where(lane < 64, kv, _rope(kr_ref[...], cos_ref[...], slo_ref[...], shi_ref[...])).astype(BF16)
    vp_ref[...] = jnp.where(lane >= 64, kv, 0.0).astype(BF16)


ATTN_Q_TILE = 512


def _attn_specs(tq, TT):
    q = pl.BlockSpec((tq, HEAD_PAD), lambda h, i: (i, h))
    keys = pl.BlockSpec((TT, HEAD_PAD), lambda h, i: (0, h))
    kr = pl.BlockSpec((TT, HEAD_PAD), lambda h, i: (0, KR0 // HEAD_PAD))
    tab_q = pl.BlockSpec((tq, HEAD_PAD), lambda h, i: (i, 0))
    tab_k = pl.BlockSpec((TT, HEAD_PAD), lambda h, i: (0, 0))
    lse = pl.BlockSpec((None, tq, 1), lambda h, i: (h, i, 0))
    return q, keys, kr, tab_q, tab_k, lse


def _attn_fwd(q_raw, kv, pp, tabs, T, TT):
    tq = ROW_TILE
    cos, slo, shi = tabs

    def body(q_ref, kv_ref, kr_ref, cq, lq, hq, ck, lk, hk, o_ref, l_ref, kc, vp):
        @pl.when(pl.program_id(1) == 0)
        def _():
            _head_keys(kv_ref, kr_ref, ck, lk, hk, kc, vp)

        q = _rope(q_ref[...], cq[...], lq[...], hq[...]).astype(BF16)
        s = lax.dot_general(q, kc[...], NT, preferred_element_type=F32)
        m = jnp.max(s, axis=-1, keepdims=True)
        p = jnp.exp2((s - m) * (ATTN_SCALE * LOG2_E))
        l = jnp.sum(p, axis=-1, keepdims=True)
        o = lax.dot_general(p.astype(BF16), vp[...], NN, preferred_element_type=F32)
        o_ref[...] = o / l
        l_ref[...] = m * ATTN_SCALE + jnp.log(l)

    qs, keys, kr, tab_q, tab_k, lse = _attn_specs(tq, TT)
    return pl.pallas_call(
        body, grid=(N_HEADS, T // tq), in_specs=[qs, keys, kr, tab_q, tab_q, tab_q, tab_k, tab_k, tab_k],
        out_specs=[qs, lse],
        out_shape=[jax.ShapeDtypeStruct((T, N_HEADS * HEAD_PAD), F32), jax.ShapeDtypeStruct((N_HEADS, T, 1), F32)],
        scratch_shapes=[pltpu.VMEM((TT, HEAD_PAD), BF16), pltpu.VMEM((TT, HEAD_PAD), BF16)],
        compiler_params=_cp(("parallel", "arbitrary")), name="attn_fwd",
    )(*_in_hbm([q_raw, kv, pp, cos, slo, shi, cos, slo, shi]))


def _attn_bwd(q_raw, kv, pp, o, do, lse, tabs, tabs_inv, T, TT):
    tq = _pick(T, (ATTN_Q_TILE, ROW_TILE))
    nq = T // tq
    cos, slo, shi = tabs
    cos_i, slo_i, shi_i = tabs_inv

    def body(q_ref, kv_ref, kr_ref, cq, lq, hq, ck, lk, hk, iq, ilq, ihq, ik, ilk, ihk, o_ref, do_ref, l_ref,
             dq_ref, dkv_ref, dkr_ref, kc, vp, dk, dv):
        h, i = pl.program_id(0), pl.program_id(1)

        @pl.when(i == 0)
        def _():
            _head_keys(kv_ref, kr_ref, ck, lk, hk, kc, vp)
            dk[...] = jnp.zeros_like(dk)
            dv[...] = jnp.zeros_like(dv)

        q = _rope(q_ref[...], cq[...], lq[...], hq[...]).astype(BF16)
        k, v, d_o = kc[...], vp[...], do_ref[...]
        s = lax.dot_general(q, k, NT, preferred_element_type=F32)
        p = jnp.exp2(s * (ATTN_SCALE * LOG2_E) - l_ref[...] * LOG2_E)
        dob = d_o.astype(BF16)
        dp = lax.dot_general(dob, v, NT, preferred_element_type=F32)
        dd = jnp.sum(d_o * o_ref[...], axis=-1, keepdims=True)
        ds = (p * (dp - dd) * ATTN_SCALE).astype(BF16)
        dq = lax.dot_general(ds, k, NN, preferred_element_type=F32)
        dq_ref[...] = _rope(dq, iq[...], ilq[...], ihq[...]).astype(dq_ref.dtype)
        dk[...] += lax.dot_general(q, ds, TN, preferred_element_type=F32)
        dv[...] += lax.dot_general(dob, p.astype(BF16), TN, preferred_element_type=F32)

        @pl.when(i == nq - 1)
        def _():
            dkh = dk[...].T
            lane = lax.broadcasted_iota(jnp.int32, dkh.shape, 1)
            dkv_ref[...] = jnp.where(lane < 64, dkh, dv[...].T).astype(dkv_ref.dtype)
            rot = _rope(jnp.where((lane >= 64) & (lane < 96), dkh, 0.0), ik[...], ilk[...], ihk[...])

            @pl.when(h == 0)
            def _():
                dkr_ref[...] = rot

            @pl.when(h > 0)
            def _():
                dkr_ref[...] += rot

    qs, keys, kr, tab_q, tab_k, lse_spec = _attn_specs(tq, TT)
    wide = lambda rows: jax.ShapeDtypeStruct((rows, N_HEADS * HEAD_PAD), BF16)
    return pl.pallas_call(
        body, grid=(N_HEADS, nq),
        in_specs=[qs, keys, kr] + [tab_q] * 3 + [tab_k] * 3 + [tab_q] * 3 + [tab_k] * 3 + [qs, qs, lse_spec],
        out_specs=[qs, keys, pl.BlockSpec((TT, HEAD_PAD), lambda h, i: (0, 0))],
        out_shape=[wide(T), wide(TT), jax.ShapeDtypeStruct((TT, HEAD_PAD), F32)],
        scratch_shapes=[pltpu.VMEM((TT, HEAD_PAD), BF16), pltpu.VMEM((TT, HEAD_PAD), BF16),
                        pltpu.VMEM((HEAD_PAD, TT), F32), pltpu.VMEM((HEAD_PAD, TT), F32)],
        compiler_params=_cp(("arbitrary", "arbitrary")), name="attn_bwd",
    )(*_in_hbm([q_raw, kv, pp, cos, slo, shi, cos, slo, shi, cos_i, slo_i, shi_i, cos_i, slo_i, shi_i, o, do, lse]))


def _hbm_specs(n):
    return [pl.BlockSpec(memory_space=pl.ANY)] * n


def _gather_weights(shards):
    n = len(shards)
    halves = [s.shape[0] // 2 for s in shards]

    def body(*refs):
        ins, outs = refs[:n], refs[n:2 * n]
        token, send_sems, recv_sems = refs[2 * n:]
        token[...] = jnp.zeros_like(token)
        mx, my, mc = lax.axis_index("x"), lax.axis_index("y"), lax.axis_index("c")
        j_me = 2 * mx + my
        chips = [(1 - mx, my), (mx, 1 - my), (1 - mx, 1 - my)]

        def half(w, chip_idx, hc):
            return outs[w].at[chip_idx, pl.ds(hc * halves[w], halves[w]), :]

        def copy(w, k, src, dst, to):
            return pltpu.make_async_remote_copy(src_ref=src, dst_ref=dst, send_sem=send_sems.at[w, k],
                                                recv_sem=recv_sems.at[w, k], device_id=to, device_id_type=MESH)

        sends = []
        for w in range(n):
            cp = copy(w, 6, ins[w], outs[w].at[j_me], (mx, my, 1 - mc))
            cp.start()
            sends.append(cp)
        for k, (px, py) in enumerate(chips):
            for w in range(n):
                cp = copy(w, k, ins[w].at[pl.ds(mc * halves[w], halves[w]), :], half(w, j_me, mc), (px, py, mc))
                cp.start()
                sends.append(cp)
        for k, (px, py) in enumerate(chips):
            for w in range(n):
                got = half(w, 2 * px + py, mc)
                copy(w, k, got, got, (px, py, mc)).wait_recv()
                cp = copy(w, 3 + k, got, got, (mx, my, 1 - mc))
                cp.start()
                sends.append(cp)
        for k, (px, py) in enumerate(chips):
            for w in range(n):
                got = half(w, 2 * px + py, 1 - mc)
                copy(w, 3 + k, got, got, (mx, my, 1 - mc)).wait_recv()
        for w in range(n):
            own = outs[w].at[j_me]
            copy(w, 6, own, own, (mx, my, 1 - mc)).wait_recv()
        for cp in sends:
            cp.wait_send()

    res = pl.pallas_call(
        body, out_shape=[jax.ShapeDtypeStruct((4,) + s.shape, s.dtype) for s in shards]
        + [jax.ShapeDtypeStruct((8, 128), F32)],
        in_specs=_hbm_specs(n), out_specs=_hbm_specs(n) + [pl.BlockSpec(memory_space=pltpu.VMEM)],
        scratch_shapes=[pltpu.SemaphoreType.DMA((n, 7)), pltpu.SemaphoreType.DMA((n, 7))],
        name="gather_weights")(*shards)
    return list(res[:n]), res[n]


def _rs_pair(gs, name):
    n = len(gs)
    halves = [g.shape[1] // 2 for g in gs]

    def body(*refs):
        ins, lands = refs[:n], refs[n:2 * n]
        send_sems, recv_sems = refs[2 * n:]
        mx, my, mc = lax.axis_index("x"), lax.axis_index("y"), lax.axis_index("c")
        copies = []
        for w in range(n):
            h = halves[w]
            cp = pltpu.make_async_remote_copy(
                src_ref=ins[w].at[:, pl.ds((1 - mc) * h, h), :], dst_ref=lands[w], send_sem=send_sems.at[w],
                recv_sem=recv_sems.at[w], device_id=(mx, my, 1 - mc), device_id_type=MESH)
            cp.start()
            copies.append(cp)
        for cp in copies:
            cp.wait()

    return pl.pallas_call(
        body, out_shape=[jax.ShapeDtypeStruct((4, h, g.shape[2]), g.dtype) for g, h in zip(gs, halves)],
        in_specs=_hbm_specs(n), out_specs=_hbm_specs(n),
        scratch_shapes=[pltpu.SemaphoreType.DMA((n,)), pltpu.SemaphoreType.DMA((n,))], name=name)(*gs)


def _rs_chips(parts):
    n = len(parts)

    def body(*refs):
        ins, lands = refs[:n], refs[n:2 * n]
        send_sems, recv_sems = refs[2 * n:]
        mx, my, mc = lax.axis_index("x"), lax.axis_index("y"), lax.axis_index("c")
        copies = []
        for k, (px, py) in enumerate([(1 - mx, my), (mx, 1 - my), (1 - mx, 1 - my)]):
            for w in range(n):
                cp = pltpu.make_async_remote_copy(
                    src_ref=ins[w].at[2 * px + py], dst_ref=lands[w].at[k], send_sem=send_sems.at[w, k],
                    recv_sem=recv_sems.at[w, k], device_id=(px, py, mc), device_id_type=MESH)
                cp.start()
                copies.append(cp)
        for cp in copies:
            cp.wait()

    return list(pl.pallas_call(
        body, out_shape=[jax.ShapeDtypeStruct((3,) + p.shape[1:], p.dtype) for p in parts],
        in_specs=_hbm_specs(n), out_specs=_hbm_specs(n),
        scratch_shapes=[pltpu.SemaphoreType.DMA((n, 3)), pltpu.SemaphoreType.DMA((n, 3))], name="rs_chips")(*parts))


def _rs_pair_back(gs, name):
    n = len(gs)

    def body(*refs):
        outs = refs[n:2 * n]
        send_sems, recv_sems = refs[2 * n:]
        mx, my, mc = lax.axis_index("x"), lax.axis_index("y"), lax.axis_index("c")
        copies = []
        for w in range(n):
            h = gs[w].shape[0] // 2
            mine = outs[w].at[pl.ds(mc * h, h), :]
            cp = pltpu.make_async_remote_copy(src_ref=mine, dst_ref=mine, send_sem=send_sems.at[w],
                                              recv_sem=recv_sems.at[w], device_id=(mx, my, 1 - mc), device_id_type=MESH)
            cp.start()
            copies.append(cp)
        for cp in copies:
            cp.wait()

    return pl.pallas_call(
        body, out_shape=[jax.ShapeDtypeStruct(g.shape, g.dtype) for g in gs],
        in_specs=_hbm_specs(n), out_specs=_hbm_specs(n), input_output_aliases={w: w for w in range(n)},
        scratch_shapes=[pltpu.SemaphoreType.DMA((n,)), pltpu.SemaphoreType.DMA((n,))], name=name)(*gs)


_HBM = pl.BlockSpec(memory_space=pltpu.HBM)
_SEM = pl.BlockSpec(memory_space=pltpu.SEMAPHORE)
_EFFECT = pltpu.SideEffectType.DATAFLOW_SIDE_EFFECTING


def _ici_copies(kind, srcs, lands, send_sems, recv_sems):
    n = len(srcs)
    mx, my, mc = lax.axis_index("x"), lax.axis_index("y"), lax.axis_index("c")
    j_me = 2 * mx + my
    copies = []
    if kind == "all":
        for k in range(7):
            a, b, c = (k + 1) >> 2 & 1, (k + 1) >> 1 & 1, (k + 1) & 1
            peer = (1 - mx if a else mx, 1 - my if b else my, 1 - mc if c else mc)
            for w in range(n):
                copies.append(pltpu.make_async_remote_copy(
                    src_ref=srcs[w], dst_ref=lands[w].at[4 * mx + 2 * my + mc], send_sem=send_sems.at[7 * w + k],
                    recv_sem=recv_sems.at[7 * w + k], device_id=peer, device_id_type=MESH))
        return copies
    if kind == "pair":
        for w in range(n):
            h = srcs[w].shape[1] // 2
            copies.append(pltpu.make_async_remote_copy(
                src_ref=srcs[w].at[:, pl.ds((1 - mc) * h, h), :], dst_ref=lands[w], send_sem=send_sems.at[w],
                recv_sem=recv_sems.at[w], device_id=(mx, my, 1 - mc), device_id_type=MESH))
        return copies
    chips = [(1 - mx, my), (mx, 1 - my), (1 - mx, 1 - my)]
    if kind == "finish":
        for w in range(n):
            h = srcs[w].shape[0] // 2
            pushes = [(lands[w].at[2 * px + py, pl.ds(mc * h, h), :],) * 2 for px, py in chips]
            pushes.append((srcs[w], lands[w].at[j_me]))
            for k, (src, dst) in enumerate(pushes):
                copies.append(pltpu.make_async_remote_copy(
                    src_ref=src, dst_ref=dst, send_sem=send_sems.at[4 * w + k], recv_sem=recv_sems.at[4 * w + k],
                    device_id=(mx, my, 1 - mc), device_id_type=MESH))
        return copies
    for k, (px, py) in enumerate(chips):
        for w in range(n):
            if kind == "gather":
                h = srcs[w].shape[0] // 2
                src, dst = srcs[w].at[pl.ds(mc * h, h), :], lands[w].at[j_me, pl.ds(mc * h, h), :]
            else:
                src, dst = srcs[w].at[2 * px + py], lands[w].at[k]
            copies.append(pltpu.make_async_remote_copy(
                src_ref=src, dst_ref=dst, send_sem=send_sems.at[3 * w + k], recv_sem=recv_sems.at[3 * w + k],
                device_id=(px, py, mc), device_id_type=MESH))
    return copies


_SEMS_PER_OPERAND = {"gather": 3, "scatter": 3, "all": 7, "pair": 1, "finish": 4}


def _ici_start(kind, srcs, land_shapes, carry, name, lands=None):
    n = len(srcs)

    def body(*refs):
        ins, lands = refs[:n], refs[n:2 * n]
        send_sems, recv_sems = refs[2 * n + 1], refs[2 * n + 2]
        for cp in _ici_copies(kind, ins, lands, send_sems, recv_sems):
            cp.start()

    hbm = lambda a: pltpu.with_memory_space_constraint(a, pltpu.HBM)
    if lands is None:
        lands = [lax.empty(s, srcs[0].dtype) for s in land_shapes]
    args = [hbm(a) for a in list(srcs) + list(lands) + [carry]]
    n_sem = _SEMS_PER_OPERAND[kind] * n
    out_shape = ([pltpu.SemaphoreType.DMA((n_sem,)), pltpu.SemaphoreType.DMA((n_sem,))]
                 + [pltpu.HBM(a.shape, a.dtype) for a in args])
    res = pl.pallas_call(
        body, name=name, out_shape=out_shape, in_specs=[_HBM] * len(args), out_specs=[_SEM, _SEM] + [_HBM] * len(args),
        input_output_aliases={i: 2 + i for i in range(len(args))},
        compiler_params=pltpu.CompilerParams(has_side_effects=_EFFECT))(*args)
    return res[0], res[1], list(res[2:2 + n]), list(res[2 + n:2 + 2 * n]), res[2 + 2 * n]


def _ici_wait(kind, send_sems, recv_sems, srcs, lands, after, name):
    n = len(srcs)

    def body(*refs):
        ins, zones = refs[:n], refs[n:2 * n]
        for cp in _ici_copies(kind, ins, zones, refs[2 * n], refs[2 * n + 1]):
            cp.wait_send()
            cp.wait_recv()

    args = list(srcs) + list(lands)
    res = pl.pallas_call(
        body, name=name, out_shape=[pltpu.HBM(a.shape, a.dtype) for a in args],
        in_specs=[_HBM] * len(args) + [_SEM, _SEM, pl.BlockSpec(memory_space=pl.ANY)], out_specs=[_HBM] * len(args),
        input_output_aliases={i: i for i in range(len(args))},
        compiler_params=pltpu.CompilerParams(has_side_effects=_EFFECT))(*args, send_sems, recv_sems, after)
    return list(res[:n]), list(res[n:])


def _tile_rows(h, c, itemsize, mult):
    best = h
    for t in range(mult, h + 1, mult):
        if h % t == 0 and t * c * itemsize <= (1 << 21):
            best = t
    return best


def _add_pair(g, land, place, name):
    _, h, c = land.shape
    t = _tile_rows(h, c, 2, 16)
    nb = h // t
    return _ew(lambda ids, u, v: (u.astype(F32) + v.astype(F32),), (4, nb),
               [(g, pl.BlockSpec((None, t, c), lambda j, i, s: (j, s[1] * nb + i, 0))),
                (land, pl.BlockSpec((None, t, c), lambda j, i, s: (j, i, 0)))],
               [(land.shape, BF16, pl.BlockSpec((None, t, c), lambda j, i, s: (j, i, 0)), None)], name, scalars=place)[0]


def _add_pair_many(gs, lands, place, name):
    ins, outs = [], []
    for g, l in zip(gs, lands):
        ins += [(g, pl.BlockSpec(l.shape, lambda i, s: (0, s[1], 0))), (l, pl.BlockSpec(l.shape, lambda i, s: (0, 0, 0)))]
        outs.append((l.shape, BF16, pl.BlockSpec(l.shape, lambda i, s: (0, 0, 0)), None))
    fn = lambda ids, *v: [v[2 * k].astype(F32) + v[2 * k + 1].astype(F32) for k in range(len(gs))]
    return list(_ew(fn, (1,), ins, outs, name, scalars=place))


def _add_chips_many(owns, lands, place, name):
    ins, outs = [], []
    for own, land in zip(owns, lands):
        _, h, c = land.shape
        ins += [(own, pl.BlockSpec((None, h, c), lambda i, s: (s[0], 0, 0))),
                (land, pl.BlockSpec((3, h, c), lambda i, s: (0, 0, 0)))]
        outs.append(((2 * h, c), F32, pl.BlockSpec((h, c), lambda i, s: (s[1], 0)), None))

    def fn(ids, *v):
        return [((v[2 * k].astype(F32) + v[2 * k + 1][0].astype(F32)) + v[2 * k + 1][1].astype(F32))
                + v[2 * k + 1][2].astype(F32) for k in range(len(owns))]

    return list(_ew(fn, (1,), ins, outs, name, scalars=place))


def _add_chips(own, land, place, name):
    _, h, c = land.shape
    t = _tile_rows(h, c, 4, 16)
    nb = h // t

    def fn(ids, a, b):
        return (((a.astype(F32) + b[0].astype(F32)) + b[1].astype(F32)) + b[2].astype(F32),)

    return _ew(fn, (nb,), [(own, pl.BlockSpec((None, t, c), lambda i, s: (s[0], i, 0))),
                           (land, pl.BlockSpec((3, t, c), lambda i, s: (0, i, 0)))],
               [((2 * h, c), F32, pl.BlockSpec((t, c), lambda i, s: (s[1] * nb + i, 0)), None)], name, scalars=place)[0]


W_IN_SEGMENTS = ((0, 256, KV0), (256, 288, KR0 + 64), (288, 672, Q0), (672, 1184, CX0), (1184, 1696, CB0),
                 (1696, 2208, CC0), (2208, 3232, GA0), (3232, 4256, GC0))
W_IN_SHARD = 1064


W_IN_SHARD_PAD = 1088


def _w_in_t_p_from_shards(s):
    pieces = []
    for o0, o1, p0 in sorted(W_IN_SEGMENTS, key=lambda t: t[2]):
        if p0 == KR0 + 64:
            pieces.append(jnp.zeros((64, s.shape[2]), s.dtype))
        for j in range(4):
            lo, hi = max(o0, j * W_IN_SHARD), min(o1, (j + 1) * W_IN_SHARD)
            if lo < hi:
                pieces.append(s[j, lo - j * W_IN_SHARD:hi - j * W_IN_SHARD])
    pieces.append(jnp.zeros((32, s.shape[2]), s.dtype))
    return jnp.concatenate(pieces, axis=0)


def _w_in_t_shards_from_p(g):
    shards = []
    for j in range(4):
        pieces = []
        for o0, o1, p0 in W_IN_SEGMENTS:
            lo, hi = max(o0, j * W_IN_SHARD), min(o1, (j + 1) * W_IN_SHARD)
            if lo < hi:
                pieces.append(g[p0 + lo - o0:p0 + hi - o0])
        pieces.append(jnp.zeros((W_IN_SHARD_PAD - W_IN_SHARD, g.shape[1]), g.dtype))
        shards.append(jnp.concatenate(pieces, axis=0))
    return jnp.stack(shards, axis=0)


def _cols_from_shards(s):
    return jnp.transpose(s, (1, 0, 2)).reshape(s.shape[1], -1)


def _rope_tables(T, TT, inverse):
    f32 = np.float32
    rows = T // GRID_W
    row = np.repeat(np.arange(rows), GRID_W).astype(f32)
    col = np.tile(np.arange(GRID_W), rows).astype(f32)
    inv = (f32(ROPE_THETA) ** (-np.arange(0, 16, 2, dtype=f32) / f32(16))).astype(f32)
    ang = np.concatenate([row[:, None] * inv, col[:, None] * inv], axis=-1).astype(f32)
    cos, sin = np.cos(ang).astype(f32), np.sin(ang).astype(f32)
    lane = np.arange(32)
    src = (lane // 16) * 8 + lane % 8
    lo = ((lane % 16) // 8 == 0).astype(f32)
    sgn = f32(-1.0 if inverse else 1.0)
    cos32 = cos[:, src]
    sin_lo32 = -sgn * sin[:, src] * lo
    sin_hi32 = sgn * sin[:, src] * (1 - lo)

    def widen(t32, fill):
        t = np.concatenate([np.full((T, 64), fill, f32), t32, np.full((T, 32), fill, f32)], axis=1)
        return jnp.asarray(np.concatenate([t, np.full((TT - T, HEAD_PAD), fill, f32)], axis=0))

    return widen(cos32, 1.0), widen(sin_lo32, 0.0), widen(sin_hi32, 0.0)


def _local_step(xx, tgt, mod_lat, mod_ctx, W, late_weights, early_grads, early_continue):
    TT = xx.shape[0]
    T = tgt.shape[0]
    n_lat, n_all = T // ROW_TILE, TT // ROW_TILE
    sh1, sc1, g1, sh2, sc2, g2 = [mod_lat[:, k * D_MODEL:(k + 1) * D_MODEL] for k in range(6)]
    csh1, csc1 = mod_ctx[:, :D_MODEL], mod_ctx[:, D_MODEL:2 * D_MODEL]
    vec = lambda n: _full((1, n))
    row_out = lambda n, dt, rows=T: ((rows, n), dt, _rows(n), None)
    acc_out = lambda n: ((1, n), F32, _full((1, n)), 0)

    def f_norm1(ids, x, g, a_sh, a_sc, b_sh, b_sc):
        ctx = ids[0] >= n_lat
        sh, sc = jnp.where(ctx, b_sh, a_sh), jnp.where(ctx, b_sc, a_sc)
        return ((x * _rms(x) * g) * (1.0 + sc) + sh,)

    (hh,) = _ew(f_norm1, (n_all,), [(xx, _rows(D_MODEL)), (W["norm1_g"], vec(D_MODEL)), (sh1, vec(D_MODEL)),
                                   (sc1, vec(D_MODEL)), (csh1, vec(D_MODEL)), (csc1, vec(D_MODEL))],
                [row_out(D_MODEL, BF16, TT)], "norm1_fwd")
    tm_all = _pick(TT, (768, 256))
    pp = _mm(hh, W["w_in_t"], "nt", TT, P_COLS, D_MODEL, tm=tm_all, tn=2176, tk=D_MODEL, name="w_in_fwd")

    def f_lowrank(ids, ckv, cq, gkv, gq):
        return ckv * _rms(ckv) * gkv, cq * _rms(cq) * gq

    nkv, nq = _ew(f_lowrank, (n_all,), [(pp, _rows(KV_RANK, KV0 // KV_RANK)), (pp, _rows(Q_RANK, Q0 // Q_RANK)),
                                       (W["kv_norm_g"], vec(KV_RANK)), (W["q_norm_g"], vec(Q_RANK))],
                  [row_out(KV_RANK, BF16, TT), row_out(Q_RANK, BF16, TT)], "lowrank_norm_fwd")
    kv = _mm(nkv, W["w_ukv"], "nn", TT, 1024, KV_RANK, tm=tm_all, tn=256, tk=KV_RANK, name="w_ukv_fwd",
             b_spec=pl.BlockSpec((None, KV_RANK, 256), lambda i, j, k: (j, k, 0)))
    q_raw = _mm(nq, W["w_uq_t"], "nt", TT, 1024, Q_RANK, tm=tm_all, tn=1024, tk=Q_RANK, name="w_uq_fwd")

    tabs = _rope_tables(T, TT, inverse=False)
    tabs_inv = _rope_tables(T, TT, inverse=True)
    _, q_raw = late_weights("before_attn", q_raw)
    o_pad, lse = _attn_fwd(q_raw, kv, pp, tabs, T, TT)
    arrived, o_pad = late_weights("after_attn", o_pad)
    W = dict(W, **arrived)
    tm_lat = _pick(T, (1024, 512, 256))
    ya = _mm(o_pad, W["w_attn_out"], "nn", T, D_MODEL, 1024, tm=tm_lat, tn=D_MODEL, tk=1024, name="w_attn_out_fwd")

    tc = 256
    colT = lambda blk0: pl.BlockSpec((T, tc), lambda j: (0, blk0 + j))

    def f_conv(ids, xin, cb, cc, w, b):
        return (cb * _conv(cc * xin, w, b),)

    (e,) = _ew(f_conv, (CONV_DIM // tc,),
               [(pp, colT(CX0 // tc)), (pp, colT(CB0 // tc)), (pp, colT(CC0 // tc)),
                (W["conv_w"], pl.BlockSpec((3, tc), lambda j: (0, j))), (W["conv_b"], pl.BlockSpec((1, tc), lambda j: (0, j)))],
               [((T, CONV_DIM), BF16, colT(0), None)], "conv_fwd")
    yc = _mm(e, W["w_conv_out"], "nn", T, D_MODEL, CONV_DIM, tm=tm_lat, tn=256, tk=CONV_DIM, name="w_conv_out_fwd",
             b_spec=pl.BlockSpec((None, CONV_DIM, 256), lambda i, j, k: (j, k, 0)))

    def f_merge(ids, ga, gc, a, c):
        return (_sigmoid(ga) * a + _sigmoid(gc) * c,)

    (mrg,) = _ew(f_merge, (n_lat,), [(pp, _rows(D_MODEL, 0)), (pp, _rows(D_MODEL, 1)), (ya, _rows(D_MODEL)),
                                    (yc, _rows(D_MODEL))], [row_out(D_MODEL, BF16)], "merge_fwd")
    mo = _mm(mrg, W["w_o"], "nn", T, D_MODEL, D_MODEL, tm=tm_lat, tn=D_MODEL, tk=D_MODEL, name="w_o_fwd")

    def f_norm2(ids, x, m, gate, g, sh, sc):
        x1 = x + gate * m
        return x1, (x1 * _rms(x1) * g) * (1.0 + sc) + sh

    x1, h2 = _ew(f_norm2, (n_lat,), [(xx, _rows(D_MODEL)), (mo, _rows(D_MODEL)), (g1, vec(D_MODEL)),
                                    (W["norm2_g"], vec(D_MODEL)), (sh2, vec(D_MODEL)), (sc2, vec(D_MODEL))],
                 [row_out(D_MODEL, F32), row_out(D_MODEL, BF16)], "norm2_fwd")
    arrived, h2 = late_weights("before_ffn", h2)
    W = dict(W, **arrived)
    up = _mm(h2, W["w_up"], "nn", T, 2 * D_FF, D_MODEL, tm=tm_lat, tn=1408, tk=D_MODEL, name="w_up_fwd",
             b_spec=pl.BlockSpec((None, D_MODEL, 1408), lambda i, j, k: (j, k, 0)))

    n_ff = D_FF // tc
    ffw = lambda off, n=3: pl.BlockSpec((n, tc), lambda j: (0, j + off))

    def f_ffn(ids, ug, uv, wg, wv, bg, bv):
        gate, val = _conv(ug, wg, bg), _conv(uv, wv, bv)
        return (gate * _sigmoid(gate) * val,)

    (act,) = _ew(f_ffn, (n_ff,), [(up, colT(0)), (up, colT(n_ff)), (W["ffn_conv_w"], ffw(0)), (W["ffn_conv_w"], ffw(n_ff)),
                                 (W["ffn_conv_b"], ffw(0, 1)), (W["ffn_conv_b"], ffw(n_ff, 1))],
                 [((T, D_FF), BF16, colT(0), None)], "ffn_act_fwd")
    f = _mm(act, W["w_down"], "nn", T, D_MODEL, D_FF, tm=tm_lat, tn=D_MODEL, tk=D_FF, name="w_down_fwd")

    def f_head(ids, x1_, f_, gate, gf, t):
        x2 = x1_ + gate * f_
        r = _rms(x2)
        xn = x2 * r
        err = xn * gf - t
        loss = 0.5 * jnp.sum(jnp.mean(err * err, axis=-1, keepdims=True))
        dy = err * (1.0 / D_MODEL)
        dx2 = _rms_bwd(dy * gf, xn, r)
        return dx2, dx2 * gate, _colsum(dy * xn), _colsum(dx2 * f_), jnp.full((1, 128), loss, F32)

    dx2, df, dg_f, dg2, loss = _ew(
        f_head, (n_lat,), [(x1, _rows(D_MODEL)), (f, _rows(D_MODEL)), (g2, vec(D_MODEL)), (W["final_g"], vec(D_MODEL)),
                           (tgt, _rows(D_MODEL))],
        [row_out(D_MODEL, F32), row_out(D_MODEL, BF16), acc_out(D_MODEL), acc_out(D_MODEL), acc_out(128)], "loss_head")

    d_w_down = _mm(act, df, "tn", D_FF, D_MODEL, T, tm=1408, tn=D_MODEL, tk=T, name="w_down_dw",
                   out_dtype=BF16).reshape(4, D_FF // 4, D_MODEL)
    da = _mm(df, W["w_down"], "nt", T, D_FF, D_MODEL, tm=tm_lat, tn=1408, tk=D_MODEL, name="w_down_dx")

    tcb = 128
    n_fb = D_FF // tcb
    colb = lambda blk0: pl.BlockSpec((T, tcb), lambda j: (0, blk0 + j))
    ffwb = lambda off, n=3: pl.BlockSpec((n, tcb), lambda j: (0, j + off))
    cvec = ((1, D_FF), F32, pl.BlockSpec((1, tcb), lambda j: (0, j)), None)

    def f_ffn_bwd(ids, ug, uv, d_act, wg, wv, bg, bv):
        sg, sv = _shifts(ug), _shifts(uv)
        gate, val = _conv(ug, wg, bg, sg), _conv(uv, wv, bv, sv)
        s = _sigmoid(gate)
        d_gate = d_act * val * s * (1.0 + gate * (1.0 - s))
        d_val = d_act * gate * s
        wg0, wg1, wg2 = _conv_bwd_w(d_gate, ug, sg)
        wv0, wv1, wv2 = _conv_bwd_w(d_val, uv, sv)
        d_up = [_conv_bwd_x(d_gate, wg), _conv_bwd_x(d_val, wv)]
        return d_up, [_colsum(d_gate), _colsum(d_val), wg0, wg1, wg2, wv0, wv1, wv2]

    d_up3, ffn_stats = _ew(
        f_ffn_bwd, (n_fb,),
        [(up, colb(0)), (up, colb(n_fb)), (da, colb(0)), (W["ffn_conv_w"], ffwb(0)), (W["ffn_conv_w"], ffwb(n_fb)),
         (W["ffn_conv_b"], ffwb(0, 1)), (W["ffn_conv_b"], ffwb(n_fb, 1))],
        [((2, T, D_FF), BF16, pl.BlockSpec((2, T, tcb), lambda j: (0, 0, j)), None),
         ((n_fb, 8, 1, tcb), F32, pl.BlockSpec((None, 8, 1, tcb), lambda j: (j, 0, 0, 0)), None)], "ffn_act_bwd")
    stat = lambda s: ffn_stats[:, s, 0, :].reshape(1, D_FF)
    d_ffn_conv_b = jnp.concatenate([stat(0), stat(1)], axis=1)
    d_ffn_conv_w = jnp.concatenate([jnp.concatenate([stat(2), stat(3), stat(4)], axis=0),
                                    jnp.concatenate([stat(5), stat(6), stat(7)], axis=0)], axis=1)

    tk_t = T
    d_w_up = _mm(h2, d_up3, "tn", D_MODEL, 2 * D_FF, T, tm=D_MODEL, tn=1408, tk=tk_t, name="w_up_dw", out_dtype=BF16,
                 b_spec=pl.BlockSpec((None, tk_t, 1408), lambda i, j, k: (j // 2, k, j % 2)),
                 o_spec=pl.BlockSpec((None, D_MODEL, 1408), lambda i, j, k: (j, i, 0)), out_shape=(4, D_MODEL, 1408))
    dh2 = _mm(d_up3, W["w_up"], "nt", T, D_MODEL, 2 * D_FF, tm=tm_lat, tn=D_MODEL, tk=1408, name="w_up_dx",
              a_spec=pl.BlockSpec((None, tm_lat, 1408), lambda i, j, k: (k // 2, i, k % 2)),
              b_spec=pl.BlockSpec((None, D_MODEL, 1408), lambda i, j, k: (k, j, 0)))

    def f_norm2_bwd(ids, dx2_, dh, x1_, m, g, sc, gate):
        r = _rms(x1_)
        xn = x1_ * r
        dx1 = dx2_ + _rms_bwd(dh * g * (1.0 + sc), xn, r)
        return dx1, dx1 * gate, _colsum(dh), _colsum(dh * xn * g), _colsum(dh * xn * (1.0 + sc)), _colsum(dx1 * m)

    dx1, dmo, dsh2, dsc2, dg_n2, dg1 = _ew(
        f_norm2_bwd, (n_lat,), [(dx2, _rows(D_MODEL)), (dh2, _rows(D_MODEL)), (x1, _rows(D_MODEL)), (mo, _rows(D_MODEL)),
                                (W["norm2_g"], vec(D_MODEL)), (sc2, vec(D_MODEL)), (g1, vec(D_MODEL))],
        [row_out(D_MODEL, F32), row_out(D_MODEL, BF16)] + [acc_out(D_MODEL)] * 4, "norm2_bwd")
    d_w_o = _mm(mrg, dmo, "tn", D_MODEL, D_MODEL, T, tm=D_MODEL, tn=D_MODEL, tk=tk_t, name="w_o_dw",
                out_dtype=BF16).reshape(4, D_MODEL // 4, D_MODEL)
    dmrg = _mm(dmo, W["w_o"], "nt", T, D_MODEL, D_MODEL, tm=tm_lat, tn=D_MODEL, tk=D_MODEL, name="w_o_dx")
    dmrg = early_grads("late", {"w_o": d_w_o, "w_up": d_w_up, "w_down": d_w_down}, dmrg, split=True)

    def f_merge_bwd(ids, dm, ga, gc, a, c):
        sa, sc_ = _sigmoid(ga), _sigmoid(gc)
        return dm * sa, dm * sc_, dm * a * sa * (1.0 - sa), dm * c * sc_ * (1.0 - sc_)

    dya, dyc, dp_ga, dp_gc = _ew(
        f_merge_bwd, (n_lat,), [(dmrg, _rows(D_MODEL)), (pp, _rows(D_MODEL, 0)), (pp, _rows(D_MODEL, 1)),
                                (ya, _rows(D_MODEL)), (yc, _rows(D_MODEL))], [row_out(D_MODEL, BF16)] * 4, "merge_bwd")
    dya = early_continue("late", dya)

    d_w_ao_p = _mm(o_pad, dya, "tn", 1024, D_MODEL, T, tm=1024, tn=D_MODEL, tk=tk_t, name="w_attn_out_dw", out_dtype=BF16)
    do_pad = _mm(dya, W["w_attn_out"], "nt", T, 1024, D_MODEL, tm=tm_lat, tn=1024, tk=D_MODEL, name="w_attn_out_dx")
    d_w_co = _mm(e, dyc, "tn", CONV_DIM, D_MODEL, T, tm=CONV_DIM, tn=256, tk=tk_t, name="w_conv_out_dw", out_dtype=BF16,
                 o_spec=pl.BlockSpec((None, CONV_DIM, 256), lambda i, j, k: (j, i, 0)), out_shape=(4, CONV_DIM, 256))
    de = _mm(dyc, W["w_conv_out"], "nt", T, CONV_DIM, D_MODEL, tm=tm_lat, tn=CONV_DIM, tk=256, name="w_conv_out_dx",
             b_spec=pl.BlockSpec((None, CONV_DIM, 256), lambda i, j, k: (k, j, 0)))

    def f_conv_bwd(ids, xin, cb, cc, d_e, w, b):
        z = cc * xin
        sz = _shifts(z)
        cz = _conv(z, w, b, sz)
        dcz = d_e * cb
        w0, w1, w2 = _conv_bwd_w(dcz, z, sz)
        dz = _conv_bwd_x(dcz, w)
        return dz * cc, d_e * cz, dz * xin, _colsum(dcz), w0, w1, w2

    cvec_c = ((1, CONV_DIM), F32, pl.BlockSpec((1, tc), lambda j: (0, j)), None)
    conv_b = _ew(f_conv_bwd, (CONV_DIM // tc,),
                 [(pp, colT(CX0 // tc)), (pp, colT(CB0 // tc)), (pp, colT(CC0 // tc)), (de, colT(0)),
                  (W["conv_w"], pl.BlockSpec((3, tc), lambda j: (0, j))), (W["conv_b"], pl.BlockSpec((1, tc), lambda j: (0, j)))],
                 [((T, CONV_DIM), BF16, colT(0), None)] * 3 + [cvec_c] * 4, "conv_bwd")
    dp_cx, dp_cb, dp_cc, d_conv_b = conv_b[:4]
    d_conv_w = jnp.concatenate(conv_b[4:7], axis=0)

    dq_raw, dkv, dp_kr = _attn_bwd(q_raw, kv, pp, o_pad, do_pad, lse, tabs, tabs_inv, T, TT)

    tk_a = TT
    d_w_uq_t = _mm(nq, dq_raw, "tn", Q_RANK, 1024, T, tm=Q_RANK, tn=1024, tk=T, name="w_uq_dw", transpose_out=True)
    dnq = _mm(dq_raw, W["w_uq_t"], "nn", T, Q_RANK, 1024, tm=tm_lat, tn=Q_RANK, tk=1024, name="w_uq_dx")
    d_w_ukv = _mm(nkv, dkv, "tn", KV_RANK, 1024, TT, tm=KV_RANK, tn=256, tk=tk_a, name="w_ukv_dw", out_dtype=BF16,
                  o_spec=pl.BlockSpec((None, KV_RANK, 256), lambda i, j, k: (j, i, 0)), out_shape=(4, KV_RANK, 256))
    dnkv = _mm(dkv, W["w_ukv"], "nt", TT, KV_RANK, 1024, tm=tm_all, tn=KV_RANK, tk=256, name="w_ukv_dx",
               b_spec=pl.BlockSpec((None, KV_RANK, 256), lambda i, j, k: (k, j, 0)))
    dnkv = early_grads("mid", {
        "w_attn_out": jnp.transpose(d_w_ao_p.reshape(N_HEADS, HEAD_PAD, 4, 256)[:, 64:], (2, 0, 1, 3)).reshape(
            4, N_HEADS * 64, 256),
        "w_conv_out": d_w_co,
        "w_uq": d_w_uq_t.reshape(4, 2, HEAD_PAD, Q_RANK)[:, :, :QK_DIM].reshape(4, 2 * QK_DIM, Q_RANK).astype(BF16),
        "w_ukv": d_w_ukv}, dnkv)

    def f_lowrank_bwd(ids, ckv, cq, dkv_, dq_, gkv, gq, ga, gc, cx, cb, cc, kr):
        rk, rq = _rms(ckv), _rms(cq)
        nk, nq_ = ckv * rk, cq * rq
        lat = ids[0] < n_lat
        dq_ = jnp.where(lat, dq_, 0.0)
        pieces = [jnp.where(lat, a, jnp.zeros_like(a)) for a in (ga, gc, cx, cb, cc)]
        pieces += [_rms_bwd(dkv_ * gkv, nk, rk).astype(BF16), _rms_bwd(dq_ * gq, nq_, rq).astype(BF16), kr.astype(BF16)]
        return jnp.concatenate(pieces, axis=1), _colsum(dkv_ * nk), _colsum(dq_ * nq_)

    lat_rows = lambda n: pl.BlockSpec((ROW_TILE, n), lambda i: (jnp.minimum(i, n_lat - 1), 0))
    dpp, dg_kv, dg_q = _ew(
        f_lowrank_bwd, (n_all,), [(pp, _rows(KV_RANK, KV0 // KV_RANK)), (pp, _rows(Q_RANK, Q0 // Q_RANK)),
                                  (dnkv, _rows(KV_RANK)), (dnq, lat_rows(Q_RANK)), (W["kv_norm_g"], vec(KV_RANK)),
                                  (W["q_norm_g"], vec(Q_RANK)), (dp_ga, lat_rows(D_MODEL)), (dp_gc, lat_rows(D_MODEL)),
                                  (dp_cx, lat_rows(CONV_DIM)), (dp_cb, lat_rows(CONV_DIM)), (dp_cc, lat_rows(CONV_DIM)),
                                  (dp_kr, _rows(HEAD_PAD))],
        [row_out(P_COLS, BF16, TT), acc_out(KV_RANK), acc_out(Q_RANK)], "lowrank_norm_bwd")
    d_w_in_t = _mm(hh, dpp, "tn", D_MODEL, P_COLS, TT, tm=512, tn=2176, tk=TT, name="w_in_dw", out_dtype=BF16,
                   transpose_out=True)
    dhh = _mm(dpp, W["w_in_t"], "nn", TT, D_MODEL, P_COLS, tm=tm_all, tn=512, tk=2176, name="w_in_dx")

    def f_norm1_bwd(ids, x, dh, dres, g, sc):
        r = _rms(x)
        xn = x * r
        return (dres + _rms_bwd(dh * g * (1.0 + sc), xn, r), _colsum(dh), _colsum(dh * xn * g),
                _colsum(dh * xn * (1.0 + sc)))

    grad_x, dsh1, dsc1, dg_n1 = _ew(
        f_norm1_bwd, (n_lat,), [(xx, _rows(D_MODEL)), (dhh, _rows(D_MODEL)), (dx1, _rows(D_MODEL)),
                                (W["norm1_g"], vec(D_MODEL)), (sc1, vec(D_MODEL))],
        [row_out(D_MODEL, F32)] + [acc_out(D_MODEL)] * 3, "norm1_bwd")

    def f_norm1_ctx_bwd(ids, x, dh, g, sc):
        xn = x * _rms(x)
        return _colsum(dh), _colsum(dh * xn * g), _colsum(dh * xn * (1.0 + sc))

    n_ctx = n_all - n_lat
    dcsh1, dcsc1, dg_n1c = _ew(
        f_norm1_ctx_bwd, (n_ctx,), [(xx, _rows(D_MODEL, 0, n_lat)), (dhh, _rows(D_MODEL, 0, n_lat)),
                                    (W["norm1_g"], vec(D_MODEL)), (csc1, vec(D_MODEL))], [acc_out(D_MODEL)] * 3,
        "norm1_ctx_bwd")

    big = {"w_in": _w_in_t_shards_from_p(d_w_in_t).astype(BF16)}
    zero = jnp.zeros((1, 4 * D_MODEL), F32)
    small = {
        "dmod_lat": jnp.concatenate([dsh1, dsc1, dg1, dsh2, dsc2, dg2], axis=1),
        "dmod_ctx": jnp.concatenate([dcsh1, dcsc1, zero], axis=1),
        "norm1_g": dg_n1 + dg_n1c, "norm2_g": dg_n2, "final_g": dg_f, "q_norm_g": dg_q, "kv_norm_g": dg_kv,
        "conv_b": d_conv_b, "conv_w": d_conv_w.reshape(1, -1), "ffn_conv_b": d_ffn_conv_b,
        "ffn_conv_w": d_ffn_conv_w.reshape(1, -1),
    }
    return grad_x, loss, big, small


SMALL = (("dmod_lat", 6144), ("dmod_ctx", 6144), ("norm1_g", 1024), ("norm2_g", 1024), ("final_g", 1024),
         ("q_norm_g", 384), ("kv_norm_g", 256), ("conv_b", 512), ("conv_w", 1536), ("ffn_conv_b", 5632),
         ("ffn_conv_w", 16896), ("loss", 128))
SMALL_ROWS = 320


def _adam_update(w, g, m, v):
    c1, c2 = 1.0 - ADAM_B1 ** ADAM_STEP, 1.0 - ADAM_B2 ** ADAM_STEP
    m2 = ADAM_B1 * m + (1.0 - ADAM_B1) * g
    v2 = ADAM_B2 * v + (1.0 - ADAM_B2) * (g * g)
    return [-ADAM_LR * ((m2 / c1) / (jnp.sqrt(v2 / c2) + ADAM_EPS) + ADAM_WD * w), m2, v2]


def _adamw(w, g, m, v, name):
    R, C = w.shape
    tr = 8 if R % 8 == 0 else R
    for t in range(8, R + 1, 8):
        if R % t == 0 and t * C * 4 <= (1 << 20):
            tr = t
    spec = pl.BlockSpec((tr, C), lambda i: (i, 0))
    return _ew(lambda ids, *vals: _adam_update(*vals), (R // tr,), [(w, spec), (g, spec), (m, spec), (v, spec)],
               [((R, C), F32, spec, None)] * 3, name)


def kernel(x, c, ctx, c_ctx, w_ada, b_ada, norm1_g, w_in, q_norm_g, kv_norm_g, w_uq, w_ukv, conv_w, conv_b, w_attn_out, w_conv_out, w_o, norm2_g, w_up, ffn_conv_w, ffn_conv_b, w_down, final_g, loss_target, m_c_ctx, m_w_ada, m_b_ada, m_norm1_g, m_w_in, m_q_norm_g, m_kv_norm_g, m_w_uq, m_w_ukv, m_conv_w, m_conv_b, m_w_attn_out, m_w_conv_out, m_w_o, m_norm2_g, m_w_up, m_ffn_conv_w, m_ffn_conv_b, m_w_down, m_final_g, v_c_ctx, v_w_ada, v_b_ada, v_norm1_g, v_w_in, v_q_norm_g, v_kv_norm_g, v_w_uq, v_w_ukv, v_conv_w, v_conv_b, v_w_attn_out, v_w_conv_out, v_w_o, v_norm2_g, v_w_up, v_ffn_conv_w, v_ffn_conv_b, v_w_down, v_final_g):
    mx, my, mc = lax.axis_index("x"), lax.axis_index("y"), lax.axis_index("c")
    chip = 2 * mx + my
    dev = 4 * mx + 2 * my + mc
    T, Tc = x.shape[1], ctx.shape[1]
    TT = T + Tc
    w_in_t, m_w_in_t, v_w_in_t = (jnp.transpose(a[0]) for a in (w_in, m_w_in, v_w_in))
    w_uq_t, m_w_uq_t, v_w_uq_t = (jnp.transpose(a[0]) for a in (w_uq, m_w_uq, v_w_uq))
    conv_sh = jnp.concatenate([conv_w[0], ffn_conv_w[0]], axis=1)
    pay1 = jnp.concatenate([jnp.pad(c, ((0, 7), (0, 0))), jnp.pad(conv_sh, ((0, 5), (0, 0)))], axis=1)
    c_send, c_recv, c_src, c_land, zero0 = _ici_start("all", [pay1], [(8, 8, 2560)], jnp.zeros((8, 128), F32),
                                                      "cond_start")
    w_in_bf = (jnp.pad(w_in_t, ((0, W_IN_SHARD_PAD - W_IN_SHARD), (0, 0))) + zero0[0, 0]).astype(BF16)
    shards = {"w_in": w_in_bf, "w_uq": w_uq_t, "w_ukv": w_ukv[0], "w_attn_out": w_attn_out[0],
              "w_conv_out": w_conv_out[0], "w_o": w_o[0], "w_up": w_up[0], "w_down": w_down[0]}
    (pay1,), (c_land,) = _ici_wait("all", c_send, c_recv, c_src, c_land, shards["w_in"], "cond_wait")
    got1 = lax.dynamic_update_slice(c_land, pay1[None], (dev, 0, 0))
    c_all = got1[:, 0, :D_MODEL]
    conv_all = got1[0::2, :3, D_MODEL:]
    conv_w_full = _cols_from_shards(conv_all[:, :, :128])
    ffn_conv_w_full = _cols_from_shards(conv_all[:, :, 128:])

    cond = jnp.concatenate([c_all, c_ctx.reshape(1, D_MODEL), jnp.zeros((7, D_MODEL), F32)], axis=0)

    def f_silu(ids, v):
        return (v * _sigmoid(v),)

    (s16,) = _ew(f_silu, (1,), [(cond, _full((16, D_MODEL)))], [((16, D_MODEL), F32, _full((16, D_MODEL)), None)], "silu_cond")
    mod_sh = _mm(s16, w_ada[0], "nn", 16, 1536, D_MODEL, tm=16, tn=768, tk=D_MODEL, name="w_ada_fwd")
    m_send, m_recv, m_src, m_land, zero1 = _ici_start("all", [mod_sh], [(8, 16, 1536)], jnp.zeros((8, 128), F32),
                                                      "mod_start")
    shards["w_ukv"] = w_ukv[0] + zero1[0, 0]

    names = [n for n, _ in BIG]
    first = [n for n in names if n not in GATHER_LATE]
    gathered, zero = _gather_weights([shards[n].astype(BF16) for n in first])
    full = dict(zip(first, gathered))
    (mod_mine,), (m_land,) = _ici_wait("all", m_send, m_recv, m_src, m_land, gathered[0], "mod_wait")
    got2 = lax.dynamic_update_slice(m_land, mod_mine[None], (dev, 0, 0))
    mod_all = _cols_from_shards(got2[0::2]) + b_ada
    mod_lat = lax.dynamic_slice_in_dim(mod_all, dev, 1, axis=0)
    mod_ctx = mod_all[8:9]
    xx = jnp.concatenate([x[0], ctx[0]], axis=0)
    late_groups = {"g1": ("w_attn_out", "w_conv_out", "w_o"), "g2": ("w_up", "w_down")}
    flight = {}
    for tag, group in late_groups.items():
        bf = [(shards[n] + zero[0, 0]).astype(BF16) for n in group]
        flight[tag] = _ici_start("gather", bf, [(4,) + s.shape for s in bf], xx, "gather_" + tag + "_start")
        xx = flight[tag][4]

    def chip_stage_done(tag, x):
        send, recv, src, land, _ = flight[tag]
        src, land = _ici_wait("gather", send, recv, src, land, x, "gather_" + tag + "_wait")
        flight[tag] = _ici_start("finish", src, None, x, "finish_" + tag + "_start", lands=land)
        return flight[tag][4]

    def arrived(tag, x):
        send, recv, src, land, _ = flight[tag]
        return dict(zip(late_groups[tag], _ici_wait("finish", send, recv, src, land, x, "finish_" + tag + "_wait")[1]))

    def late_weights(point, x):
        if point == "before_attn":
            return {}, chip_stage_done("g1", x)
        if point == "after_attn":
            got = arrived("g1", x)
            wao = _cols_from_shards(got["w_attn_out"]).reshape(N_HEADS, 64, D_MODEL)
            ready = {"w_attn_out": jnp.pad(wao, ((0, 0), (64, 0), (0, 0))).reshape(N_HEADS * HEAD_PAD, D_MODEL),
                     "w_conv_out": got["w_conv_out"], "w_o": got["w_o"].reshape(D_MODEL, D_MODEL)}
            return ready, chip_stage_done("g2", x)
        got = arrived("g2", x)
        return {"w_up": got["w_up"], "w_down": got["w_down"].reshape(D_FF, D_MODEL)}, x

    wuq_t = full["w_uq"].reshape(N_HEADS, QK_DIM, Q_RANK)
    W = {
        "w_in_t": _w_in_t_p_from_shards(full["w_in"]),
        "w_uq_t": jnp.pad(wuq_t, ((0, 0), (0, HEAD_PAD - QK_DIM), (0, 0))).reshape(N_HEADS * HEAD_PAD, Q_RANK),
        "w_ukv": full["w_ukv"],
        "norm1_g": norm1_g, "norm2_g": norm2_g, "final_g": final_g.reshape(1, D_MODEL), "q_norm_g": q_norm_g,
        "kv_norm_g": kv_norm_g, "conv_w": conv_w_full, "conv_b": conv_b, "ffn_conv_w": ffn_conv_w_full,
        "ffn_conv_b": ffn_conv_b,
    }

    place = jnp.stack([chip, mc]).astype(jnp.int32)
    early = {}

    pending = {}

    def scatter(tag, group, gs, from_sib, carry):
        if tag == "mid":
            sums = _add_pair_many(gs, from_sib, place, "rs_pair_add_mid")
        else:
            sums = [_add_pair(gs[w], from_sib[w], place, "rs_pair_add_" + n) for w, n in enumerate(group)]
        send, recv, sums, land, carry = _ici_start(
            "scatter", sums, [(3,) + s.shape[1:] for s in sums], carry, "rs_chips_" + tag + "_start")
        early[tag] = (group, send, recv, sums, land)
        return carry

    def early_grads(tag, g, carry, split=False):
        gs = list(g.values())
        if not split:
            return scatter(tag, list(g), gs, _rs_pair(gs, "rs_pair_" + tag), carry)
        send, recv, gs, land, carry = _ici_start(
            "pair", gs, [(4, s.shape[1] // 2, s.shape[2]) for s in gs], carry, "rs_pair_" + tag + "_start")
        pending[tag] = (list(g), send, recv, gs, land)
        return carry

    def early_continue(tag, carry):
        group, send, recv, gs, land = pending[tag]
        gs, from_sib = _ici_wait("pair", send, recv, gs, land, carry, "rs_pair_" + tag + "_wait")
        return scatter(tag, group, gs, from_sib, carry)

    grad_x, loss_part, gbig, gsmall = _local_step(xx, loss_target[0], mod_lat, mod_ctx, W, late_weights, early_grads,
                                                  early_continue)

    gsmall["loss"] = loss_part
    pay3 = jnp.concatenate([gsmall[n].reshape(-1) for n, _ in SMALL])
    pay3 = jnp.pad(pay3, (0, SMALL_ROWS * 128 - pay3.shape[0])).reshape(SMALL_ROWS, 128)
    s_send, s_recv, s_src, s_land, w_in_thru = _ici_start("all", [pay3], [(8, SMALL_ROWS, 128)], gbig["w_in"],
                                                         "small_start")
    gbig = {"w_in": w_in_thru}

    after_small = early_grads("last", gbig, s_src[0])

    (pay3,), (s_land,) = _ici_wait("all", s_send, s_recv, [after_small], s_land, early["last"][3][0], "small_wait")
    got3 = lax.dynamic_update_slice(s_land, pay3[None], (dev, 0, 0)).reshape(8 * SMALL_ROWS, 128)

    def f_sum8(ids, a):
        s = a[0:SMALL_ROWS]
        for d in range(1, 8):
            s = s + a[d * SMALL_ROWS:(d + 1) * SMALL_ROWS]
        return (s,)

    (vsum,) = _ew(f_sum8, (1,), [(got3, _full((8 * SMALL_ROWS, 128)))],
                  [((SMALL_ROWS, 128), F32, _full((SMALL_ROWS, 128)), None)], "sum_small")
    vflat = vsum.reshape(-1)
    gvec, off = {}, 0
    for n, size in SMALL:
        gvec[n] = vflat[off:off + size]
        off += size
    loss = gvec["loss"][0]
    dmod_rows = got3.reshape(8, SMALL_ROWS * 128)[:, :6 * D_MODEL]
    dm16 = jnp.concatenate([dmod_rows, gvec["dmod_ctx"].reshape(1, -1), jnp.zeros((7, 6 * D_MODEL), F32)], axis=0)

    def f_colsum(ids, a):
        return (_colsum(a),)

    (g_b_ada,) = _ew(f_colsum, (1,), [(dm16, _full((16, 6 * D_MODEL)))],
                     [((1, 6 * D_MODEL), F32, _full((1, 6 * D_MODEL)), None)], "b_ada_grad")
    dm_sh = lax.dynamic_slice_in_dim(dm16, chip * 1536, 1536, axis=1)
    g_w_ada = _mm(s16, dm_sh, "tn", D_MODEL, 1536, 16, tm=512, tn=768, tk=16, name="w_ada_dw")
    dcond_part = _mm(dm_sh, w_ada[0], "nt", 16, D_MODEL, 1536, tm=16, tn=512, tk=1536, name="w_ada_dx")
    d_send, d_recv, d_src, d_land, vsum = _ici_start("all", [dcond_part[8:16]], [(8, 8, D_MODEL)], vsum, "dcond_start")

    def finish(tags, after):
        done, halves = [], []
        for tag in tags:
            tag_names, send, recv, sums, land = early[tag]
            sums, land = _ici_wait("scatter", send, recv, sums, land, after, "rs_chips_" + tag + "_wait")
            done += tag_names
            if tag == "mid":
                halves += _add_chips_many(sums, land, place, "rs_chip_add_mid")
            else:
                halves += [_add_chips(a, b, place, "rs_chip_add_" + n) for a, b, n in zip(sums, land, tag_names)]
        return dict(zip(done, _rs_pair_back(halves, "rs_pair_back_" + tags[0])))

    grads, deltas, new_m, new_v = {}, {}, {}, {}

    raw = {}

    def adam(n, w_, m_, v_, g, transposed):
        d_, m2, v2 = _adamw(w_, g, m_, v_, "adamw_" + n)
        raw[n] = d_
        back = (lambda a: jnp.transpose(a)[None]) if transposed else (lambda a: a[None])
        grads[n], deltas[n], new_m[n], new_v[n] = back(g[:w_.shape[0]]), back(d_), back(m2), back(v2)

    gw = finish(["late", "mid"], grad_x)
    adam("w_ada", w_ada[0], m_w_ada[0], v_w_ada[0], g_w_ada, False)
    for n, (w_, m_, v_) in {"w_o": (w_o, m_w_o, v_w_o), "w_up": (w_up, m_w_up, v_w_up),
                            "w_down": (w_down, m_w_down, v_w_down)}.items():
        adam(n, w_[0], m_[0], v_[0], gw[n], False)
    gw_in = finish(["last"], raw["w_up"])
    adam("w_in", w_in_t, m_w_in_t, v_w_in_t, gw_in["w_in"], True)

    (dcond_mine,), (d_land,) = _ici_wait("all", d_send, d_recv, d_src, d_land, raw["w_in"], "dcond_wait")
    got4 = lax.dynamic_update_slice(d_land, dcond_mine[None], (dev, 0, 0))[0::2, 0]

    def f_c_ctx(ids, parts, cc):
        s = _sigmoid(cc)
        d = parts[0:1] + parts[1:2] + parts[2:3] + parts[3:4]
        return (d * s * (1.0 + cc * (1.0 - s)),)

    (g_c_ctx,) = _ew(f_c_ctx, (1,), [(got4, _full((4, D_MODEL))), (c_ctx.reshape(1, D_MODEL), _full((1, D_MODEL)))],
                     [((1, D_MODEL), F32, _full((1, D_MODEL)), None)], "c_ctx_grad")

    conv_w_g = lax.dynamic_slice_in_dim(gvec["conv_w"].reshape(3, CONV_DIM), chip * 128, 128, axis=1)
    ffn_conv_w_g = lax.dynamic_slice_in_dim(gvec["ffn_conv_w"].reshape(3, 2 * D_FF), chip * 1408, 1408, axis=1)
    vec_params = (("c_ctx", c_ctx, m_c_ctx, v_c_ctx, g_c_ctx), ("b_ada", b_ada, m_b_ada, v_b_ada, g_b_ada),
                  ("norm1_g", norm1_g, m_norm1_g, v_norm1_g, gvec["norm1_g"]),
                  ("q_norm_g", q_norm_g, m_q_norm_g, v_q_norm_g, gvec["q_norm_g"]),
                  ("kv_norm_g", kv_norm_g, m_kv_norm_g, v_kv_norm_g, gvec["kv_norm_g"]),
                  ("conv_w", conv_w, m_conv_w, v_conv_w, conv_w_g), ("conv_b", conv_b, m_conv_b, v_conv_b, gvec["conv_b"]),
                  ("norm2_g", norm2_g, m_norm2_g, v_norm2_g, gvec["norm2_g"]),
                  ("ffn_conv_w", ffn_conv_w, m_ffn_conv_w, v_ffn_conv_w, ffn_conv_w_g),
                  ("ffn_conv_b", ffn_conv_b, m_ffn_conv_b, v_ffn_conv_b, gvec["ffn_conv_b"]),
                  ("final_g", final_g, m_final_g, v_final_g, gvec["final_g"]))
    two_d = lambda a: a.reshape((-1, a.shape[-1]))
    many = [p + ((lambda r, s=p[1].shape: r.reshape(s)),) for p in vec_params]
    for n, w_, m_, v_ in (("w_ukv", w_ukv, m_w_ukv, v_w_ukv), ("w_attn_out", w_attn_out, m_w_attn_out, v_w_attn_out),
                          ("w_conv_out", w_conv_out, m_w_conv_out, v_w_conv_out)):
        many.append((n, w_, m_, v_, gw[n], (lambda r, s=w_.shape: r.reshape(s))))
    many.append(("w_uq", w_uq_t, m_w_uq_t, v_w_uq_t, gw["w_uq"], lambda r: jnp.transpose(r)[None]))

    def f_adam_many(ids, *vals):
        out = []
        for k in range(len(many)):
            out += _adam_update(*vals[4 * k:4 * k + 4])
        return out

    ins_v, outs_v = [], []
    for p in many:
        shp = two_d(p[1]).shape
        ins_v += [(two_d(a), _full(shp)) for a in (p[1], p[4], p[2], p[3])]
        outs_v += [(shp, F32, _full(shp), None)] * 3
    res_v = _ew(f_adam_many, (1,), ins_v, outs_v, "adamw_small")
    for k, p in enumerate(many):
        n, post = p[0], p[5]
        grads[n] = post(two_d(p[4]))
        deltas[n], new_m[n], new_v[n] = (post(r) for r in res_v[3 * k:3 * k + 3])

    order = ("c_ctx", "w_ada", "b_ada", "norm1_g", "w_in", "q_norm_g", "kv_norm_g", "w_uq", "w_ukv", "conv_w", "conv_b",
             "w_attn_out", "w_conv_out", "w_o", "norm2_g", "w_up", "ffn_conv_w", "ffn_conv_b", "w_down", "final_g")
    return (loss, grad_x[None], *[grads[n] for n in order], *[deltas[n] for n in order],
            *[new_m[n] for n in order], *[new_v[n] for n in order])
```

```python
import functools

import jax
import jax.numpy as jnp
import numpy as np
from jax import lax
from jax.experimental import pallas as pl
from jax.experimental.pallas import tpu as pltpu

F32, BF16 = jnp.float32, jnp.bfloat16
MESH = pl.DeviceIdType.MESH

D_MODEL = 1024
N_HEADS = 8
HEAD_PAD = 128
QK_DIM = 96
Q_RANK, KV_RANK = 384, 256
CONV_DIM = 512
D_FF = 2816
GRID_W = 64
ROPE_THETA = 10000.0
EPS = 1e-6
GA0, GC0, CX0, CB0, CC0, KV0, Q0, KR0, P_COLS = 0, 1024, 2048, 2560, 3072, 3584, 3840, 4224, 4352
ROW_TILE = 256
VMEM_LIMIT_BYTES = 48 * 1024 * 1024

ADAM_LR, ADAM_B1, ADAM_B2, ADAM_EPS, ADAM_WD, ADAM_STEP = 0.001, 0.9, 0.999, 1e-08, 0.01, 10

BIG = (("w_in", (1088, 1024)), ("w_uq", (192, 384)), ("w_ukv", (256, 256)), ("w_attn_out", (512, 256)),
       ("w_conv_out", (512, 256)), ("w_o", (256, 1024)), ("w_up", (1024, 1408)), ("w_down", (704, 1024)))

GATHER_LATE = ("w_attn_out", "w_conv_out", "w_o", "w_up", "w_down")

NN = (((1,), (0,)), ((), ()))
NT = (((1,), (1,)), ((), ()))
TN = (((0,), (0,)), ((), ()))


def _cp(sem):
    return pltpu.CompilerParams(dimension_semantics=sem, vmem_limit_bytes=VMEM_LIMIT_BYTES)


PIN_BYTES = 1 << 19


def _in_hbm(arrays):
    return [pltpu.with_memory_space_constraint(a, pltpu.HBM) if a.size * a.dtype.itemsize >= PIN_BYTES else a
            for a in arrays]


def _out(shape, dtype):
    n = 1
    for d in shape:
        n *= d
    big = n * jnp.dtype(dtype).itemsize >= PIN_BYTES
    return pltpu.HBM(shape, dtype) if big else jax.ShapeDtypeStruct(shape, dtype)


def _pick(n, prefs):
    for p in prefs:
        if n % p == 0:
            return p
    return n


def _mm(a, b, mode, M, N, K, *, tm, tn, tk, name, out_dtype=F32, a_spec=None, b_spec=None, o_spec=None,
        out_shape=None, transpose_out=False):
    assert M % tm == 0 and N % tn == 0 and K % tk == 0, (name, M, N, K, tm, tn, tk)
    nk = K // tk
    dims = {"nn": NN, "nt": NT, "tn": TN}[mode]
    if a_spec is None:
        a_spec = (pl.BlockSpec((tk, tm), lambda i, j, k: (k, i)) if mode == "tn"
                  else pl.BlockSpec((tm, tk), lambda i, j, k: (i, k)))
    if b_spec is None:
        b_spec = (pl.BlockSpec((tn, tk), lambda i, j, k: (j, k)) if mode == "nt"
                  else pl.BlockSpec((tk, tn), lambda i, j, k: (k, j)))
    if o_spec is None:
        o_spec = (pl.BlockSpec((tn, tm), lambda i, j, k: (j, i)) if transpose_out
                  else pl.BlockSpec((tm, tn), lambda i, j, k: (i, j)))
    if out_shape is None:
        out_shape = (N, M) if transpose_out else (M, N)

    def emit(o_ref, val):
        o_ref[...] = (val.T if transpose_out else val).astype(o_ref.dtype)

    def body(a_ref, b_ref, o_ref, *scratch):
        part = lax.dot_general(a_ref[...].astype(BF16), b_ref[...].astype(BF16), dims, preferred_element_type=F32)
        if nk == 1:
            emit(o_ref, part)
            return
        acc_ref, = scratch
        k = pl.program_id(2)

        @pl.when(k == 0)
        def _():
            acc_ref[...] = part

        @pl.when((k > 0) & (k < nk - 1))
        def _():
            acc_ref[...] += part

        @pl.when(k == nk - 1)
        def _():
            emit(o_ref, acc_ref[...] + part)

    return pl.pallas_call(
        body, grid=(M // tm, N // tn, nk), in_specs=[a_spec, b_spec], out_specs=o_spec,
        out_shape=_out(out_shape, out_dtype),
        scratch_shapes=[pltpu.VMEM((tm, tn), F32)] if nk > 1 else [],
        compiler_params=_cp(("parallel", "parallel", "arbitrary")), name=name)(*_in_hbm([a, b]))


def _ew(fn, grid, ins, outs, name, scalars=None):
    n_in = len(ins)
    n_sc = 0 if scalars is None else 1

    def store(ref, val, acc, ids):
        if isinstance(val, (list, tuple)):
            for h, v in enumerate(val):
                ref[h] = v.astype(ref.dtype)
            return
        if acc is None:
            ref[...] = val.astype(ref.dtype)
            return

        @pl.when(ids[acc] == 0)
        def _():
            ref[...] = val.astype(ref.dtype)

        @pl.when(ids[acc] > 0)
        def _():
            ref[...] += val.astype(ref.dtype)

    def body(*refs):
        refs = refs[n_sc:]
        ids = tuple(pl.program_id(a) for a in range(len(grid)))
        vals = fn(ids, *[r[...] for r in refs[:n_in]])
        for ref, val, (_, _, _, acc) in zip(refs[n_in:], vals, outs):
            store(ref, val, acc, ids)

    acc_axes = {o[3] for o in outs if o[3] is not None}
    sem = tuple("arbitrary" if a in acc_axes else "parallel" for a in range(len(grid)))
    in_specs, out_specs = [s for _, s in ins], [o[2] for o in outs]
    out_shape = [_out(o[0], o[1]) for o in outs]
    args = _in_hbm([a for a, _ in ins])
    if scalars is None:
        return pl.pallas_call(body, grid=grid, in_specs=in_specs, out_specs=out_specs, out_shape=out_shape,
                              compiler_params=_cp(sem), name=name)(*args)
    spec = pltpu.PrefetchScalarGridSpec(num_scalar_prefetch=1, grid=grid, in_specs=in_specs, out_specs=out_specs)
    return pl.pallas_call(body, grid_spec=spec, out_shape=out_shape, compiler_params=_cp(sem), name=name)(scalars, *args)


def _rows(width, cblk=0, roff=0, tr=ROW_TILE):
    return pl.BlockSpec((tr, width), lambda i: (i + roff, cblk))


def _full(shape):
    nd = len(shape)
    return pl.BlockSpec(shape, lambda *_: (0,) * nd)


def _sigmoid(x):
    return 1.0 / (1.0 + jnp.exp2(x * (-1.4426950408889634)))


def _rms(x):
    return lax.rsqrt(jnp.mean(x * x, axis=-1, keepdims=True) + EPS)


def _rms_bwd(dn, xn, r):
    return r * (dn - xn * jnp.mean(dn * xn, axis=-1, keepdims=True))


def _colsum(x):
    return jnp.sum(x, axis=0, keepdims=True)


def _shifts(x):
    n = x.shape[0]
    rows = lax.broadcasted_iota(jnp.int32, x.shape, 0)
    return jnp.where(rows == 0, 0.0, pltpu.roll(x, 1, 0)), jnp.where(rows == n - 1, 0.0, pltpu.roll(x, n - 1, 0))


def _conv(x, w, b, shifted=None):
    prev, nxt = _shifts(x) if shifted is None else shifted
    return b + prev * w[0:1] + x * w[1:2] + nxt * w[2:3]


def _conv_bwd_x(dy, w):
    prev, nxt = _shifts(dy)
    return nxt * w[0:1] + dy * w[1:2] + prev * w[2:3]


def _conv_bwd_w(dy, x, shifted):
    prev, nxt = shifted
    return _colsum(dy * prev), _colsum(dy * x), _colsum(dy * nxt)


def _rope(x, cos, sin_lo, sin_hi):
    return x * cos + pltpu.roll(x, HEAD_PAD - 8, 1) * sin_lo + pltpu.roll(x, 8, 1) * sin_hi


ATTN_SCALE = QK_DIM ** -0.5
LOG2_E = 1.4426950408889634


def _head_keys(kv_ref, kr_ref, cos_ref, slo_ref, shi_ref, kc_ref, vp_ref):
    kv = kv_ref[...]
    lane = lax.broadcasted_iota(jnp.int32, kv.shape, 1)
    kc_ref[...] = jnp.where(lane < 64, kv, _rope(kr_ref[...], cos_ref[...], slo_ref[...], shi_ref[...])).astype(BF16)
    vp_ref[...] = jnp.where(lane >= 64, kv, 0.0).astype(BF16)


ATTN_Q_TILE = 512


def _attn_specs(tq, TT):
    q = pl.BlockSpec((tq, HEAD_PAD), lambda h, i: (i, h))
    keys = pl.BlockSpec((TT, HEAD_PAD), lambda h, i: (0, h))
    kr = pl.BlockSpec((TT, HEAD_PAD), lambda h, i: (0, KR0 // HEAD_PAD))
    tab_q = pl.BlockSpec((tq, HEAD_PAD), lambda h, i: (i, 0))
    tab_k = pl.BlockSpec((TT, HEAD_PAD), lambda h, i: (0, 0))
    lse = pl.BlockSpec((None, tq, 1), lambda h, i: (h, i, 0))
    return q, keys, kr, tab_q, tab_k, lse


def _attn_fwd(q_raw, kv, pp, tabs, T, TT):
    tq = ROW_TILE
    cos, slo, shi = tabs

    def body(q_ref, kv_ref, kr_ref, cq, lq, hq, ck, lk, hk, o_ref, l_ref, kc, vp):
        @pl.when(pl.program_id(1) == 0)
        def _():
            _head_keys(kv_ref, kr_ref, ck, lk, hk, kc, vp)

        q = _rope(q_ref[...], cq[...], lq[...], hq[...]).astype(BF16)
        s = lax.dot_general(q, kc[...], NT, preferred_element_type=F32)
        m = jnp.max(s, axis=-1, keepdims=True)
        p = jnp.exp2((s - m) * (ATTN_SCALE * LOG2_E))
        l = jnp.sum(p, axis=-1, keepdims=True)
        o = lax.dot_general(p.astype(BF16), vp[...], NN, preferred_element_type=F32)
        o_ref[...] = o / l
        l_ref[...] = m * ATTN_SCALE + jnp.log(l)

    qs, keys, kr, tab_q, tab_k, lse = _attn_specs(tq, TT)
    return pl.pallas_call(
        body, grid=(N_HEADS, T // tq), in_specs=[qs, keys, kr, tab_q, tab_q, tab_q, tab_k, tab_k, tab_k],
        out_specs=[qs, lse],
        out_shape=[jax.ShapeDtypeStruct((T, N_HEADS * HEAD_PAD), F32), jax.ShapeDtypeStruct((N_HEADS, T, 1), F32)],
        scratch_shapes=[pltpu.VMEM((TT, HEAD_PAD), BF16), pltpu.VMEM((TT, HEAD_PAD), BF16)],
        compiler_params=_cp(("parallel", "arbitrary")), name="attn_fwd",
    )(*_in_hbm([q_raw, kv, pp, cos, slo, shi, cos, slo, shi]))


def _attn_bwd(q_raw, kv, pp, o, do, lse, tabs, tabs_inv, T, TT):
    tq = _pick(T, (ATTN_Q_TILE, ROW_TILE))
    nq = T // tq
    cos, slo, shi = tabs
    cos_i, slo_i, shi_i = tabs_inv

    def body(q_ref, kv_ref, kr_ref, cq, lq, hq, ck, lk, hk, iq, ilq, ihq, ik, ilk, ihk, o_ref, do_ref, l_ref,
             dq_ref, dkv_ref, dkr_ref, kc, vp, dk, dv):
        h, i = pl.program_id(0), pl.program_id(1)

        @pl.when(i == 0)
        def _():
            _head_keys(kv_ref, kr_ref, ck, lk, hk, kc, vp)
            dk[...] = jnp.zeros_like(dk)
            dv[...] = jnp.zeros_like(dv)

        q = _rope(q_ref[...], cq[...], lq[...], hq[...]).astype(BF16)
        k, v, d_o = kc[...], vp[...], do_ref[...]
        s = lax.dot_general(q, k, NT, preferred_element_type=F32)
        p = jnp.exp2(s * (ATTN_SCALE * LOG2_E) - l_ref[...] * LOG2_E)
        dob = d_o.astype(BF16)
        dp = lax.dot_general(dob, v, NT, preferred_element_type=F32)
        dd = jnp.sum(d_o * o_ref[...], axis=-1, keepdims=True)
        ds = (p * (dp - dd) * ATTN_SCALE).astype(BF16)
        dq = lax.dot_general(ds, k, NN, preferred_element_type=F32)
        dq_ref[...] = _rope(dq, iq[...], ilq[...], ihq[...]).astype(dq_ref.dtype)
        dk[...] += lax.dot_general(q, ds, TN, preferred_element_type=F32)
        dv[...] += lax.dot_general(dob, p.astype(BF16), TN, preferred_element_type=F32)

        @pl.when(i == nq - 1)
        def _():
            dkh = dk[...].T
            lane = lax.broadcasted_iota(jnp.int32, dkh.shape, 1)
            dkv_ref[...] = jnp.where(lane < 64, dkh, dv[...].T).astype(dkv_ref.dtype)
            rot = _rope(jnp.where((lane >= 64) & (lane < 96), dkh, 0.0), ik[...], ilk[...], ihk[...])

            @pl.when(h == 0)
            def _():
                dkr_ref[...] = rot

            @pl.when(h > 0)
            def _():
                dkr_ref[...] += rot

    qs, keys, kr, tab_q, tab_k, lse_spec = _attn_specs(tq, TT)
    wide = lambda rows: jax.ShapeDtypeStruct((rows, N_HEADS * HEAD_PAD), BF16)
    return pl.pallas_call(
        body, grid=(N_HEADS, nq),
        in_specs=[qs, keys, kr] + [tab_q] * 3 + [tab_k] * 3 + [tab_q] * 3 + [tab_k] * 3 + [qs, qs, lse_spec],
        out_specs=[qs, keys, pl.BlockSpec((TT, HEAD_PAD), lambda h, i: (0, 0))],
        out_shape=[wide(T), wide(TT), jax.ShapeDtypeStruct((TT, HEAD_PAD), F32)],
        scratch_shapes=[pltpu.VMEM((TT, HEAD_PAD), BF16), pltpu.VMEM((TT, HEAD_PAD), BF16),
                        pltpu.VMEM((HEAD_PAD, TT), F32), pltpu.VMEM((HEAD_PAD, TT), F32)],
        compiler_params=_cp(("arbitrary", "arbitrary")), name="attn_bwd",
    )(*_in_hbm([q_raw, kv, pp, cos, slo, shi, cos, slo, shi, cos_i, slo_i, shi_i, cos_i, slo_i, shi_i, o, do, lse]))


def _hbm_specs(n):
    return [pl.BlockSpec(memory_space=pl.ANY)] * n


def _gather_weights(shards):
    n = len(shards)
    halves = [s.shape[0] // 2 for s in shards]

    def body(*refs):
        ins, outs = refs[:n], refs[n:2 * n]
        token, send_sems, recv_sems = refs[2 * n:]
        token[...] = jnp.zeros_like(token)
        mx, my, mc = lax.axis_index("x"), lax.axis_index("y"), lax.axis_index("c")
        j_me = 2 * mx + my
        chips = [(1 - mx, my), (mx, 1 - my), (1 - mx, 1 - my)]

        def half(w, chip_idx, hc):
            return outs[w].at[chip_idx, pl.ds(hc * halves[w], halves[w]), :]

        def copy(w, k, src, dst, to):
            return pltpu.make_async_remote_copy(src_ref=src, dst_ref=dst, send_sem=send_sems.at[w, k],
                                                recv_sem=recv_sems.at[w, k], device_id=to, device_id_type=MESH)

        sends = []
        for w in range(n):
            cp = copy(w, 6, ins[w], outs[w].at[j_me], (mx, my, 1 - mc))
            cp.start()
            sends.append(cp)
        for k, (px, py) in enumerate(chips):
            for w in range(n):
                cp = copy(w, k, ins[w].at[pl.ds(mc * halves[w], halves[w]), :], half(w, j_me, mc), (px, py, mc))
                cp.start()
                sends.append(cp)
        for k, (px, py) in enumerate(chips):
            for w in range(n):
                got = half(w, 2 * px + py, mc)
                copy(w, k, got, got, (px, py, mc)).wait_recv()
                cp = copy(w, 3 + k, got, got, (mx, my, 1 - mc))
                cp.start()
                sends.append(cp)
        for k, (px, py) in enumerate(chips):
            for w in range(n):
                got = half(w, 2 * px + py, 1 - mc)
                copy(w, 3 + k, got, got, (mx, my, 1 - mc)).wait_recv()
        for w in range(n):
            own = outs[w].at[j_me]
            copy(w, 6, own, own, (mx, my, 1 - mc)).wait_recv()
        for cp in sends:
            cp.wait_send()

    res = pl.pallas_call(
        body, out_shape=[jax.ShapeDtypeStruct((4,) + s.shape, s.dtype) for s in shards]
        + [jax.ShapeDtypeStruct((8, 128), F32)],
        in_specs=_hbm_specs(n), out_specs=_hbm_specs(n) + [pl.BlockSpec(memory_space=pltpu.VMEM)],
        scratch_shapes=[pltpu.SemaphoreType.DMA((n, 7)), pltpu.SemaphoreType.DMA((n, 7))],
        name="gather_weights")(*shards)
    return list(res[:n]), res[n]


def _rs_pair(gs, name):
    n = len(gs)
    halves = [g.shape[1] // 2 for g in gs]

    def body(*refs):
        ins, lands = refs[:n], refs[n:2 * n]
        send_sems, recv_sems = refs[2 * n:]
        mx, my, mc = lax.axis_index("x"), lax.axis_index("y"), lax.axis_index("c")
        copies = []
        for w in range(n):
            h = halves[w]
            cp = pltpu.make_async_remote_copy(
                src_ref=ins[w].at[:, pl.ds((1 - mc) * h, h), :], dst_ref=lands[w], send_sem=send_sems.at[w],
                recv_sem=recv_sems.at[w], device_id=(mx, my, 1 - mc), device_id_type=MESH)
            cp.start()
            copies.append(cp)
        for cp in copies:
            cp.wait()

    return pl.pallas_call(
        body, out_shape=[jax.ShapeDtypeStruct((4, h, g.shape[2]), g.dtype) for g, h in zip(gs, halves)],
        in_specs=_hbm_specs(n), out_specs=_hbm_specs(n),
        scratch_shapes=[pltpu.SemaphoreType.DMA((n,)), pltpu.SemaphoreType.DMA((n,))], name=name)(*gs)


def _rs_chips(parts):
    n = len(parts)

    def body(*refs):
        ins, lands = refs[:n], refs[n:2 * n]
        send_sems, recv_sems = refs[2 * n:]
        mx, my, mc = lax.axis_index("x"), lax.axis_index("y"), lax.axis_index("c")
        copies = []
        for k, (px, py) in enumerate([(1 - mx, my), (mx, 1 - my), (1 - mx, 1 - my)]):
            for w in range(n):
                cp = pltpu.make_async_remote_copy(
                    src_ref=ins[w].at[2 * px + py], dst_ref=lands[w].at[k], send_sem=send_sems.at[w, k],
                    recv_sem=recv_sems.at[w, k], device_id=(px, py, mc), device_id_type=MESH)
                cp.start()
                copies.append(cp)
        for cp in copies:
            cp.wait()

    return list(pl.pallas_call(
        body, out_shape=[jax.ShapeDtypeStruct((3,) + p.shape[1:], p.dtype) for p in parts],
        in_specs=_hbm_specs(n), out_specs=_hbm_specs(n),
        scratch_shapes=[pltpu.SemaphoreType.DMA((n, 3)), pltpu.SemaphoreType.DMA((n, 3))], name="rs_chips")(*parts))


_HBM = pl.BlockSpec(memory_space=pltpu.HBM)
_SEM = pl.BlockSpec(memory_space=pltpu.SEMAPHORE)
_EFFECT = pltpu.SideEffectType.DATAFLOW_SIDE_EFFECTING


def _ici_copies(kind, srcs, lands, send_sems, recv_sems):
    n = len(lands)
    mx, my, mc = lax.axis_index("x"), lax.axis_index("y"), lax.axis_index("c")
    j_me = 2 * mx + my
    copies = []
    if kind == "back":
        for w in range(n):
            h = lands[w].shape[0] // 2
            mine = lands[w].at[pl.ds(mc * h, h), :]
            copies.append(pltpu.make_async_remote_copy(
                src_ref=mine, dst_ref=mine, send_sem=send_sems.at[w], recv_sem=recv_sems.at[w],
                device_id=(mx, my, 1 - mc), device_id_type=MESH))
        return copies
    if kind == "all":
        for k in range(7):
            a, b, c = (k + 1) >> 2 & 1, (k + 1) >> 1 & 1, (k + 1) & 1
            peer = (1 - mx if a else mx, 1 - my if b else my, 1 - mc if c else mc)
            for w in range(n):
                copies.append(pltpu.make_async_remote_copy(
                    src_ref=srcs[w], dst_ref=lands[w].at[4 * mx + 2 * my + mc], send_sem=send_sems.at[7 * w + k],
                    recv_sem=recv_sems.at[7 * w + k], device_id=peer, device_id_type=MESH))
        return copies
    if kind == "pair":
        for w in range(n):
            h = srcs[w].shape[1] // 2
            copies.append(pltpu.make_async_remote_copy(
                src_ref=srcs[w].at[:, pl.ds((1 - mc) * h, h), :], dst_ref=lands[w], send_sem=send_sems.at[w],
                recv_sem=recv_sems.at[w], device_id=(mx, my, 1 - mc), device_id_type=MESH))
        return copies
    chips = [(1 - mx, my), (mx, 1 - my), (1 - mx, 1 - my)]
    if kind == "finish":
        for w in range(n):
            h = srcs[w].shape[0] // 2
            pushes = [(lands[w].at[2 * px + py, pl.ds(mc * h, h), :],) * 2 for px, py in chips]
            pushes.append((srcs[w], lands[w].at[j_me]))
            for k, (src, dst) in enumerate(pushes):
                copies.append(pltpu.make_async_remote_copy(
                    src_ref=src, dst_ref=dst, send_sem=send_sems.at[4 * w + k], recv_sem=recv_sems.at[4 * w + k],
                    device_id=(mx, my, 1 - mc), device_id_type=MESH))
        return copies
    for k, (px, py) in enumerate(chips):
        for w in range(n):
            if kind == "gather":
                h = srcs[w].shape[0] // 2
                src, dst = srcs[w].at[pl.ds(mc * h, h), :], lands[w].at[j_me, pl.ds(mc * h, h), :]
            else:
                src, dst = srcs[w].at[2 * px + py], lands[w].at[k]
            copies.append(pltpu.make_async_remote_copy(
                src_ref=src, dst_ref=dst, send_sem=send_sems.at[3 * w + k], recv_sem=recv_sems.at[3 * w + k],
                device_id=(px, py, mc), device_id_type=MESH))
    return copies


_SEMS_PER_OPERAND = {"gather": 3, "scatter": 3, "all": 7, "pair": 1, "finish": 4, "back": 1}


def _ici_start(kind, srcs, land_shapes, carry, name, lands=None):
    hbm = lambda a: pltpu.with_memory_space_constraint(a, pltpu.HBM)
    if lands is None:
        lands = [lax.empty(s, srcs[0].dtype) for s in land_shapes]
    ns, nl = len(srcs), len(lands)

    def body(*refs):
        send_sems, recv_sems = refs[ns + nl + 1], refs[ns + nl + 2]
        for cp in _ici_copies(kind, refs[:ns], refs[ns:ns + nl], send_sems, recv_sems):
            cp.start()

    args = [hbm(a) for a in list(srcs) + list(lands) + [carry]]
    n_sem = _SEMS_PER_OPERAND[kind] * nl
    out_shape = ([pltpu.SemaphoreType.DMA((n_sem,)), pltpu.SemaphoreType.DMA((n_sem,))]
                 + [pltpu.HBM(a.shape, a.dtype) for a in args])
    res = pl.pallas_call(
        body, name=name, out_shape=out_shape, in_specs=[_HBM] * len(args), out_specs=[_SEM, _SEM] + [_HBM] * len(args),
        input_output_aliases={i: 2 + i for i in range(len(args))},
        compiler_params=pltpu.CompilerParams(has_side_effects=_EFFECT))(*args)
    return res[0], res[1], list(res[2:2 + ns]), list(res[2 + ns:2 + ns + nl]), res[2 + ns + nl]


def _ici_wait(kind, send_sems, recv_sems, srcs, lands, after, name):
    ns, nl = len(srcs), len(lands)

    def body(*refs):
        for cp in _ici_copies(kind, refs[:ns], refs[ns:ns + nl], refs[ns + nl], refs[ns + nl + 1]):
            cp.wait_send()
            cp.wait_recv()

    args = list(srcs) + list(lands)
    res = pl.pallas_call(
        body, name=name, out_shape=[pltpu.HBM(a.shape, a.dtype) for a in args],
        in_specs=[_HBM] * len(args) + [_SEM, _SEM, pl.BlockSpec(memory_space=pl.ANY)], out_specs=[_HBM] * len(args),
        input_output_aliases={i: i for i in range(len(args))},
        compiler_params=pltpu.CompilerParams(has_side_effects=_EFFECT))(*args, send_sems, recv_sems, after)
    return list(res[:ns]), list(res[ns:])


def _tile_rows(h, c, itemsize, mult):
    best = h
    for t in range(mult, h + 1, mult):
        if h % t == 0 and t * c * itemsize <= (1 << 21):
            best = t
    return best


def _add_pair(g, land, place, name):
    _, h, c = land.shape
    t = _tile_rows(h, c, 2, 16)
    nb = h // t
    return _ew(lambda ids, u, v: (u.astype(F32) + v.astype(F32),), (4, nb),
               [(g, pl.BlockSpec((None, t, c), lambda j, i, s: (j, s[1] * nb + i, 0))),
                (land, pl.BlockSpec((None, t, c), lambda j, i, s: (j, i, 0)))],
               [(land.shape, BF16, pl.BlockSpec((None, t, c), lambda j, i, s: (j, i, 0)), None)], name, scalars=place)[0]


def _add_pair_many(gs, lands, place, name):
    ins, outs = [], []
    for g, l in zip(gs, lands):
        ins += [(g, pl.BlockSpec(l.shape, lambda i, s: (0, s[1], 0))), (l, pl.BlockSpec(l.shape, lambda i, s: (0, 0, 0)))]
        outs.append((l.shape, BF16, pl.BlockSpec(l.shape, lambda i, s: (0, 0, 0)), None))
    fn = lambda ids, *v: [v[2 * k].astype(F32) + v[2 * k + 1].astype(F32) for k in range(len(gs))]
    return list(_ew(fn, (1,), ins, outs, name, scalars=place))


def _add_chips_many(owns, lands, place, name):
    ins, outs = [], []
    for own, land in zip(owns, lands):
        _, h, c = land.shape
        ins += [(own, pl.BlockSpec((None, h, c), lambda i, s: (s[0], 0, 0))),
                (land, pl.BlockSpec((3, h, c), lambda i, s: (0, 0, 0)))]
        outs.append(((2 * h, c), F32, pl.BlockSpec((h, c), lambda i, s: (s[1], 0)), None))

    def fn(ids, *v):
        return [((v[2 * k].astype(F32) + v[2 * k + 1][0].astype(F32)) + v[2 * k + 1][1].astype(F32))
                + v[2 * k + 1][2].astype(F32) for k in range(len(owns))]

    return list(_ew(fn, (1,), ins, outs, name, scalars=place))


def _add_chips(own, land, place, name):
    _, h, c = land.shape
    t = _tile_rows(h, c, 4, 16)
    nb = h // t

    def fn(ids, a, b):
        return (((a.astype(F32) + b[0].astype(F32)) + b[1].astype(F32)) + b[2].astype(F32),)

    return _ew(fn, (nb,), [(own, pl.BlockSpec((None, t, c), lambda i, s: (s[0], i, 0))),
                           (land, pl.BlockSpec((3, t, c), lambda i, s: (0, i, 0)))],
               [((2 * h, c), F32, pl.BlockSpec((t, c), lambda i, s: (s[1] * nb + i, 0)), None)], name, scalars=place)[0]


W_IN_SEGMENTS = ((0, 256, KV0), (256, 288, KR0 + 64), (288, 672, Q0), (672, 1184, CX0), (1184, 1696, CB0),
                 (1696, 2208, CC0), (2208, 3232, GA0), (3232, 4256, GC0))
W_IN_SHARD = 1064


W_IN_SHARD_PAD = 1088


def _w_in_t_p_from_shards(s):
    pieces = []
    for o0, o1, p0 in sorted(W_IN_SEGMENTS, key=lambda t: t[2]):
        if p0 == KR0 + 64:
            pieces.append(jnp.zeros((64, s.shape[2]), s.dtype))
        for j in range(4):
            lo, hi = max(o0, j * W_IN_SHARD), min(o1, (j + 1) * W_IN_SHARD)
            if lo < hi:
                pieces.append(s[j, lo - j * W_IN_SHARD:hi - j * W_IN_SHARD])
    pieces.append(jnp.zeros((32, s.shape[2]), s.dtype))
    return jnp.concatenate(pieces, axis=0)


def _w_in_t_shards_from_p(g):
    shards = []
    for j in range(4):
        pieces = []
        for o0, o1, p0 in W_IN_SEGMENTS:
            lo, hi = max(o0, j * W_IN_SHARD), min(o1, (j + 1) * W_IN_SHARD)
            if lo < hi:
                pieces.append(g[p0 + lo - o0:p0 + hi - o0])
        pieces.append(jnp.zeros((W_IN_SHARD_PAD - W_IN_SHARD, g.shape[1]), g.dtype))
        shards.append(jnp.concatenate(pieces, axis=0))
    return jnp.stack(shards, axis=0)


def _cols_from_shards(s):
    return jnp.transpose(s, (1, 0, 2)).reshape(s.shape[1], -1)


def _rope_tables(T, TT, inverse):
    f32 = np.float32
    rows = T // GRID_W
    row = np.repeat(np.arange(rows), GRID_W).astype(f32)
    col = np.tile(np.arange(GRID_W), rows).astype(f32)
    inv = (f32(ROPE_THETA) ** (-np.arange(0, 16, 2, dtype=f32) / f32(16))).astype(f32)
    ang = np.concatenate([row[:, None] * inv, col[:, None] * inv], axis=-1).astype(f32)
    cos, sin = np.cos(ang).astype(f32), np.sin(ang).astype(f32)
    lane = np.arange(32)
    src = (lane // 16) * 8 + lane % 8
    lo = ((lane % 16) // 8 == 0).astype(f32)
    sgn = f32(-1.0 if inverse else 1.0)
    cos32 = cos[:, src]
    sin_lo32 = -sgn * sin[:, src] * lo
    sin_hi32 = sgn * sin[:, src] * (1 - lo)

    def widen(t32, fill):
        t = np.concatenate([np.full((T, 64), fill, f32), t32, np.full((T, 32), fill, f32)], axis=1)
        return jnp.asarray(np.concatenate([t, np.full((TT - T, HEAD_PAD), fill, f32)], axis=0))

    return widen(cos32, 1.0), widen(sin_lo32, 0.0), widen(sin_hi32, 0.0)


def _local_step(xx, tgt, mod_lat, mod_ctx, W, late_weights, early_grads, early_continue):
    TT = xx.shape[0]
    T = tgt.shape[0]
    n_lat, n_all = T // ROW_TILE, TT // ROW_TILE
    sh1, sc1, g1, sh2, sc2, g2 = [mod_lat[:, k * D_MODEL:(k + 1) * D_MODEL] for k in range(6)]
    csh1, csc1 = mod_ctx[:, :D_MODEL], mod_ctx[:, D_MODEL:2 * D_MODEL]
    vec = lambda n: _full((1, n))
    row_out = lambda n, dt, rows=T: ((rows, n), dt, _rows(n), None)
    acc_out = lambda n: ((1, n), F32, _full((1, n)), 0)

    def f_norm1(ids, x, g, a_sh, a_sc, b_sh, b_sc):
        ctx = ids[0] >= n_lat
        sh, sc = jnp.where(ctx, b_sh, a_sh), jnp.where(ctx, b_sc, a_sc)
        return ((x * _rms(x) * g) * (1.0 + sc) + sh,)

    (hh,) = _ew(f_norm1, (n_all,), [(xx, _rows(D_MODEL)), (W["norm1_g"], vec(D_MODEL)), (sh1, vec(D_MODEL)),
                                   (sc1, vec(D_MODEL)), (csh1, vec(D_MODEL)), (csc1, vec(D_MODEL))],
                [row_out(D_MODEL, BF16, TT)], "norm1_fwd")
    tm_all = _pick(TT, (768, 256))
    pp = _mm(hh, W["w_in_t"], "nt", TT, P_COLS, D_MODEL, tm=tm_all, tn=2176, tk=D_MODEL, name="w_in_fwd")

    def f_lowrank(ids, ckv, cq, gkv, gq):
        return ckv * _rms(ckv) * gkv, cq * _rms(cq) * gq

    nkv, nq = _ew(f_lowrank, (n_all,), [(pp, _rows(KV_RANK, KV0 // KV_RANK)), (pp, _rows(Q_RANK, Q0 // Q_RANK)),
                                       (W["kv_norm_g"], vec(KV_RANK)), (W["q_norm_g"], vec(Q_RANK))],
                  [row_out(KV_RANK, BF16, TT), row_out(Q_RANK, BF16, TT)], "lowrank_norm_fwd")
    kv = _mm(nkv, W["w_ukv"], "nn", TT, 1024, KV_RANK, tm=tm_all, tn=256, tk=KV_RANK, name="w_ukv_fwd",
             b_spec=pl.BlockSpec((None, KV_RANK, 256), lambda i, j, k: (j, k, 0)))
    q_raw = _mm(nq, W["w_uq_t"], "nt", TT, 1024, Q_RANK, tm=tm_all, tn=1024, tk=Q_RANK, name="w_uq_fwd")

    tabs = _rope_tables(T, TT, inverse=False)
    tabs_inv = _rope_tables(T, TT, inverse=True)
    _, q_raw = late_weights("before_attn", q_raw)
    o_pad, lse = _attn_fwd(q_raw, kv, pp, tabs, T, TT)
    arrived, o_pad = late_weights("after_attn", o_pad)
    W = dict(W, **arrived)
    tm_lat = _pick(T, (1024, 512, 256))
    ya = _mm(o_pad, W["w_attn_out"], "nn", T, D_MODEL, 1024, tm=tm_lat, tn=D_MODEL, tk=1024, name="w_attn_out_fwd")

    tc = 256
    colT = lambda blk0: pl.BlockSpec((T, tc), lambda j: (0, blk0 + j))

    def f_conv(ids, xin, cb, cc, w, b):
        return (cb * _conv(cc * xin, w, b),)

    (e,) = _ew(f_conv, (CONV_DIM // tc,),
               [(pp, colT(CX0 // tc)), (pp, colT(CB0 // tc)), (pp, colT(CC0 // tc)),
                (W["conv_w"], pl.BlockSpec((3, tc), lambda j: (0, j))), (W["conv_b"], pl.BlockSpec((1, tc), lambda j: (0, j)))],
               [((T, CONV_DIM), BF16, colT(0), None)], "conv_fwd")
    yc = _mm(e, W["w_conv_out"], "nn", T, D_MODEL, CONV_DIM, tm=tm_lat, tn=256, tk=CONV_DIM, name="w_conv_out_fwd",
             b_spec=pl.BlockSpec((None, CONV_DIM, 256), lambda i, j, k: (j, k, 0)))

    def f_merge(ids, ga, gc, a, c):
        return (_sigmoid(ga) * a + _sigmoid(gc) * c,)

    (mrg,) = _ew(f_merge, (n_lat,), [(pp, _rows(D_MODEL, 0)), (pp, _rows(D_MODEL, 1)), (ya, _rows(D_MODEL)),
                                    (yc, _rows(D_MODEL))], [row_out(D_MODEL, BF16)], "merge_fwd")
    mo = _mm(mrg, W["w_o"], "nn", T, D_MODEL, D_MODEL, tm=tm_lat, tn=D_MODEL, tk=D_MODEL, name="w_o_fwd")

    def f_norm2(ids, x, m, gate, g, sh, sc):
        x1 = x + gate * m
        return x1, (x1 * _rms(x1) * g) * (1.0 + sc) + sh

    x1, h2 = _ew(f_norm2, (n_lat,), [(xx, _rows(D_MODEL)), (mo, _rows(D_MODEL)), (g1, vec(D_MODEL)),
                                    (W["norm2_g"], vec(D_MODEL)), (sh2, vec(D_MODEL)), (sc2, vec(D_MODEL))],
                 [row_out(D_MODEL, F32), row_out(D_MODEL, BF16)], "norm2_fwd")
    arrived, h2 = late_weights("before_ffn", h2)
    W = dict(W, **arrived)
    up = _mm(h2, W["w_up"], "nn", T, 2 * D_FF, D_MODEL, tm=tm_lat, tn=1408, tk=D_MODEL, name="w_up_fwd",
             b_spec=pl.BlockSpec((None, D_MODEL, 1408), lambda i, j, k: (j, k, 0)))

    n_ff = D_FF // tc
    ffw = lambda off, n=3: pl.BlockSpec((n, tc), lambda j: (0, j + off))

    def f_ffn(ids, ug, uv, wg, wv, bg, bv):
        gate, val = _conv(ug, wg, bg), _conv(uv, wv, bv)
        return (gate * _sigmoid(gate) * val,)

    (act,) = _ew(f_ffn, (n_ff,), [(up, colT(0)), (up, colT(n_ff)), (W["ffn_conv_w"], ffw(0)), (W["ffn_conv_w"], ffw(n_ff)),
                                 (W["ffn_conv_b"], ffw(0, 1)), (W["ffn_conv_b"], ffw(n_ff, 1))],
                 [((T, D_FF), BF16, colT(0), None)], "ffn_act_fwd")
    f = _mm(act, W["w_down"], "nn", T, D_MODEL, D_FF, tm=tm_lat, tn=D_MODEL, tk=D_FF, name="w_down_fwd")

    def f_head(ids, x1_, f_, gate, gf, t):
        x2 = x1_ + gate * f_
        r = _rms(x2)
        xn = x2 * r
        err = xn * gf - t
        loss = 0.5 * jnp.sum(jnp.mean(err * err, axis=-1, keepdims=True))
        dy = err * (1.0 / D_MODEL)
        dx2 = _rms_bwd(dy * gf, xn, r)
        return dx2, dx2 * gate, _colsum(dy * xn), _colsum(dx2 * f_), jnp.full((1, 128), loss, F32)

    dx2, df, dg_f, dg2, loss = _ew(
        f_head, (n_lat,), [(x1, _rows(D_MODEL)), (f, _rows(D_MODEL)), (g2, vec(D_MODEL)), (W["final_g"], vec(D_MODEL)),
                           (tgt, _rows(D_MODEL))],
        [row_out(D_MODEL, F32), row_out(D_MODEL, BF16), acc_out(D_MODEL), acc_out(D_MODEL), acc_out(128)], "loss_head")

    d_w_down = _mm(act, df, "tn", D_FF, D_MODEL, T, tm=1408, tn=D_MODEL, tk=T, name="w_down_dw",
                   out_dtype=BF16).reshape(4, D_FF // 4, D_MODEL)
    da = _mm(df, W["w_down"], "nt", T, D_FF, D_MODEL, tm=tm_lat, tn=1408, tk=D_MODEL, name="w_down_dx")

    tcb = 128
    n_fb = D_FF // tcb
    colb = lambda blk0: pl.BlockSpec((T, tcb), lambda j: (0, blk0 + j))
    ffwb = lambda off, n=3: pl.BlockSpec((n, tcb), lambda j: (0, j + off))
    cvec = ((1, D_FF), F32, pl.BlockSpec((1, tcb), lambda j: (0, j)), None)

    def f_ffn_bwd(ids, ug, uv, d_act, wg, wv, bg, bv):
        sg, sv = _shifts(ug), _shifts(uv)
        gate, val = _conv(ug, wg, bg, sg), _conv(uv, wv, bv, sv)
        s = _sigmoid(gate)
        d_gate = d_act * val * s * (1.0 + gate * (1.0 - s))
        d_val = d_act * gate * s
        wg0, wg1, wg2 = _conv_bwd_w(d_gate, ug, sg)
        wv0, wv1, wv2 = _conv_bwd_w(d_val, uv, sv)
        d_up = [_conv_bwd_x(d_gate, wg), _conv_bwd_x(d_val, wv)]
        return d_up, [_colsum(d_gate), _colsum(d_val), wg0, wg1, wg2, wv0, wv1, wv2]

    d_up3, ffn_stats = _ew(
        f_ffn_bwd, (n_fb,),
        [(up, colb(0)), (up, colb(n_fb)), (da, colb(0)), (W["ffn_conv_w"], ffwb(0)), (W["ffn_conv_w"], ffwb(n_fb)),
         (W["ffn_conv_b"], ffwb(0, 1)), (W["ffn_conv_b"], ffwb(n_fb, 1))],
        [((2, T, D_FF), BF16, pl.BlockSpec((2, T, tcb), lambda j: (0, 0, j)), None),
         ((n_fb, 8, 1, tcb), F32, pl.BlockSpec((None, 8, 1, tcb), lambda j: (j, 0, 0, 0)), None)], "ffn_act_bwd")
    stat = lambda s: ffn_stats[:, s, 0, :].reshape(1, D_FF)
    d_ffn_conv_b = jnp.concatenate([stat(0), stat(1)], axis=1)
    d_ffn_conv_w = jnp.concatenate([jnp.concatenate([stat(2), stat(3), stat(4)], axis=0),
                                    jnp.concatenate([stat(5), stat(6), stat(7)], axis=0)], axis=1)

    tk_t = T
    d_w_up = _mm(h2, d_up3, "tn", D_MODEL, 2 * D_FF, T, tm=D_MODEL, tn=1408, tk=tk_t, name="w_up_dw", out_dtype=BF16,
                 b_spec=pl.BlockSpec((None, tk_t, 1408), lambda i, j, k: (j // 2, k, j % 2)),
                 o_spec=pl.BlockSpec((None, D_MODEL, 1408), lambda i, j, k: (j, i, 0)), out_shape=(4, D_MODEL, 1408))
    dh2 = _mm(d_up3, W["w_up"], "nt", T, D_MODEL, 2 * D_FF, tm=tm_lat, tn=D_MODEL, tk=1408, name="w_up_dx",
              a_spec=pl.BlockSpec((None, tm_lat, 1408), lambda i, j, k: (k // 2, i, k % 2)),
              b_spec=pl.BlockSpec((None, D_MODEL, 1408), lambda i, j, k: (k, j, 0)))

    def f_norm2_bwd(ids, dx2_, dh, x1_, m, g, sc, gate):
        r = _rms(x1_)
        xn = x1_ * r
        dx1 = dx2_ + _rms_bwd(dh * g * (1.0 + sc), xn, r)
        return dx1, dx1 * gate, _colsum(dh), _colsum(dh * xn * g), _colsum(dh * xn * (1.0 + sc)), _colsum(dx1 * m)

    dx1, dmo, dsh2, dsc2, dg_n2, dg1 = _ew(
        f_norm2_bwd, (n_lat,), [(dx2, _rows(D_MODEL)), (dh2, _rows(D_MODEL)), (x1, _rows(D_MODEL)), (mo, _rows(D_MODEL)),
                                (W["norm2_g"], vec(D_MODEL)), (sc2, vec(D_MODEL)), (g1, vec(D_MODEL))],
        [row_out(D_MODEL, F32), row_out(D_MODEL, BF16)] + [acc_out(D_MODEL)] * 4, "norm2_bwd")
    d_w_o = _mm(mrg, dmo, "tn", D_MODEL, D_MODEL, T, tm=D_MODEL, tn=D_MODEL, tk=tk_t, name="w_o_dw",
                out_dtype=BF16).reshape(4, D_MODEL // 4, D_MODEL)
    dmrg = _mm(dmo, W["w_o"], "nt", T, D_MODEL, D_MODEL, tm=tm_lat, tn=D_MODEL, tk=D_MODEL, name="w_o_dx")
    dmrg = early_grads("late", {"w_o": d_w_o, "w_up": d_w_up, "w_down": d_w_down}, dmrg, split=True)

    def f_merge_bwd(ids, dm, ga, gc, a, c):
        sa, sc_ = _sigmoid(ga), _sigmoid(gc)
        return dm * sa, dm * sc_, dm * a * sa * (1.0 - sa), dm * c * sc_ * (1.0 - sc_)

    dya, dyc, dp_ga, dp_gc = _ew(
        f_merge_bwd, (n_lat,), [(dmrg, _rows(D_MODEL)), (pp, _rows(D_MODEL, 0)), (pp, _rows(D_MODEL, 1)),
                                (ya, _rows(D_MODEL)), (yc, _rows(D_MODEL))], [row_out(D_MODEL, BF16)] * 4, "merge_bwd")
    dya = early_continue("late", dya)

    d_w_ao_p = _mm(o_pad, dya, "tn", 1024, D_MODEL, T, tm=1024, tn=D_MODEL, tk=tk_t, name="w_attn_out_dw", out_dtype=BF16)
    do_pad = _mm(dya, W["w_attn_out"], "nt", T, 1024, D_MODEL, tm=tm_lat, tn=1024, tk=D_MODEL, name="w_attn_out_dx")
    d_w_co = _mm(e, dyc, "tn", CONV_DIM, D_MODEL, T, tm=CONV_DIM, tn=256, tk=tk_t, name="w_conv_out_dw", out_dtype=BF16,
                 o_spec=pl.BlockSpec((None, CONV_DIM, 256), lambda i, j, k: (j, i, 0)), out_shape=(4, CONV_DIM, 256))
    de = _mm(dyc, W["w_conv_out"], "nt", T, CONV_DIM, D_MODEL, tm=tm_lat, tn=CONV_DIM, tk=256, name="w_conv_out_dx",
             b_spec=pl.BlockSpec((None, CONV_DIM, 256), lambda i, j, k: (k, j, 0)))

    def f_conv_bwd(ids, xin, cb, cc, d_e, w, b):
        z = cc * xin
        sz = _shifts(z)
        cz = _conv(z, w, b, sz)
        dcz = d_e * cb
        w0, w1, w2 = _conv_bwd_w(dcz, z, sz)
        dz = _conv_bwd_x(dcz, w)
        return dz * cc, d_e * cz, dz * xin, _colsum(dcz), w0, w1, w2

    cvec_c = ((1, CONV_DIM), F32, pl.BlockSpec((1, tc), lambda j: (0, j)), None)
    conv_b = _ew(f_conv_bwd, (CONV_DIM // tc,),
                 [(pp, colT(CX0 // tc)), (pp, colT(CB0 // tc)), (pp, colT(CC0 // tc)), (de, colT(0)),
                  (W["conv_w"], pl.BlockSpec((3, tc), lambda j: (0, j))), (W["conv_b"], pl.BlockSpec((1, tc), lambda j: (0, j)))],
                 [((T, CONV_DIM), BF16, colT(0), None)] * 3 + [cvec_c] * 4, "conv_bwd")
    dp_cx, dp_cb, dp_cc, d_conv_b = conv_b[:4]
    d_conv_w = jnp.concatenate(conv_b[4:7], axis=0)

    dq_raw, dkv, dp_kr = _attn_bwd(q_raw, kv, pp, o_pad, do_pad, lse, tabs, tabs_inv, T, TT)

    tk_a = TT
    d_w_uq_t = _mm(nq, dq_raw, "tn", Q_RANK, 1024, T, tm=Q_RANK, tn=1024, tk=T, name="w_uq_dw", transpose_out=True)
    dnq = _mm(dq_raw, W["w_uq_t"], "nn", T, Q_RANK, 1024, tm=tm_lat, tn=Q_RANK, tk=1024, name="w_uq_dx")
    d_w_ukv = _mm(nkv, dkv, "tn", KV_RANK, 1024, TT, tm=KV_RANK, tn=256, tk=tk_a, name="w_ukv_dw", out_dtype=BF16,
                  o_spec=pl.BlockSpec((None, KV_RANK, 256), lambda i, j, k: (j, i, 0)), out_shape=(4, KV_RANK, 256))
    dnkv = _mm(dkv, W["w_ukv"], "nt", TT, KV_RANK, 1024, tm=tm_all, tn=KV_RANK, tk=256, name="w_ukv_dx",
               b_spec=pl.BlockSpec((None, KV_RANK, 256), lambda i, j, k: (k, j, 0)))
    dnkv = early_grads("mid", {
        "w_attn_out": jnp.transpose(d_w_ao_p.reshape(N_HEADS, HEAD_PAD, 4, 256)[:, 64:], (2, 0, 1, 3)).reshape(
            4, N_HEADS * 64, 256),
        "w_conv_out": d_w_co,
        "w_uq": d_w_uq_t.reshape(4, 2, HEAD_PAD, Q_RANK)[:, :, :QK_DIM].reshape(4, 2 * QK_DIM, Q_RANK).astype(BF16),
        "w_ukv": d_w_ukv}, dnkv)

    def f_lowrank_bwd(ids, ckv, cq, dkv_, dq_, gkv, gq, ga, gc, cx, cb, cc, kr):
        rk, rq = _rms(ckv), _rms(cq)
        nk, nq_ = ckv * rk, cq * rq
        lat = ids[0] < n_lat
        dq_ = jnp.where(lat, dq_, 0.0)
        pieces = [jnp.where(lat, a, jnp.zeros_like(a)) for a in (ga, gc, cx, cb, cc)]
        pieces += [_rms_bwd(dkv_ * gkv, nk, rk).astype(BF16), _rms_bwd(dq_ * gq, nq_, rq).astype(BF16), kr.astype(BF16)]
        return jnp.concatenate(pieces, axis=1), _colsum(dkv_ * nk), _colsum(dq_ * nq_)

    lat_rows = lambda n: pl.BlockSpec((ROW_TILE, n), lambda i: (jnp.minimum(i, n_lat - 1), 0))
    dpp, dg_kv, dg_q = _ew(
        f_lowrank_bwd, (n_all,), [(pp, _rows(KV_RANK, KV0 // KV_RANK)), (pp, _rows(Q_RANK, Q0 // Q_RANK)),
                                  (dnkv, _rows(KV_RANK)), (dnq, lat_rows(Q_RANK)), (W["kv_norm_g"], vec(KV_RANK)),
                                  (W["q_norm_g"], vec(Q_RANK)), (dp_ga, lat_rows(D_MODEL)), (dp_gc, lat_rows(D_MODEL)),
                                  (dp_cx, lat_rows(CONV_DIM)), (dp_cb, lat_rows(CONV_DIM)), (dp_cc, lat_rows(CONV_DIM)),
                                  (dp_kr, _rows(HEAD_PAD))],
        [row_out(P_COLS, BF16, TT), acc_out(KV_RANK), acc_out(Q_RANK)], "lowrank_norm_bwd")
    d_w_in_t = _mm(hh, dpp, "tn", D_MODEL, P_COLS, TT, tm=512, tn=2176, tk=TT, name="w_in_dw", out_dtype=BF16,
                   transpose_out=True)
    dhh = _mm(dpp, W["w_in_t"], "nn", TT, D_MODEL, P_COLS, tm=tm_all, tn=512, tk=2176, name="w_in_dx")

    def f_norm1_bwd(ids, x, dh, dres, g, sc):
        r = _rms(x)
        xn = x * r
        return (dres + _rms_bwd(dh * g * (1.0 + sc), xn, r), _colsum(dh), _colsum(dh * xn * g),
                _colsum(dh * xn * (1.0 + sc)))

    grad_x, dsh1, dsc1, dg_n1 = _ew(
        f_norm1_bwd, (n_lat,), [(xx, _rows(D_MODEL)), (dhh, _rows(D_MODEL)), (dx1, _rows(D_MODEL)),
                                (W["norm1_g"], vec(D_MODEL)), (sc1, vec(D_MODEL))],
        [row_out(D_MODEL, F32)] + [acc_out(D_MODEL)] * 3, "norm1_bwd")

    def f_norm1_ctx_bwd(ids, x, dh, g, sc):
        xn = x * _rms(x)
        return _colsum(dh), _colsum(dh * xn * g), _colsum(dh * xn * (1.0 + sc))

    n_ctx = n_all - n_lat
    dcsh1, dcsc1, dg_n1c = _ew(
        f_norm1_ctx_bwd, (n_ctx,), [(xx, _rows(D_MODEL, 0, n_lat)), (dhh, _rows(D_MODEL, 0, n_lat)),
                                    (W["norm1_g"], vec(D_MODEL)), (csc1, vec(D_MODEL))], [acc_out(D_MODEL)] * 3,
        "norm1_ctx_bwd")

    big = {"w_in": _w_in_t_shards_from_p(d_w_in_t).astype(BF16)}
    zero = jnp.zeros((1, 4 * D_MODEL), F32)
    small = {
        "dmod_lat": jnp.concatenate([dsh1, dsc1, dg1, dsh2, dsc2, dg2], axis=1),
        "dmod_ctx": jnp.concatenate([dcsh1, dcsc1, zero], axis=1),
        "norm1_g": dg_n1 + dg_n1c, "norm2_g": dg_n2, "final_g": dg_f, "q_norm_g": dg_q, "kv_norm_g": dg_kv,
        "conv_b": d_conv_b, "conv_w": d_conv_w.reshape(1, -1), "ffn_conv_b": d_ffn_conv_b,
        "ffn_conv_w": d_ffn_conv_w.reshape(1, -1),
    }
    return grad_x, loss, big, small


SMALL = (("dmod_lat", 6144), ("dmod_ctx", 6144), ("norm1_g", 1024), ("norm2_g", 1024), ("final_g", 1024),
         ("q_norm_g", 384), ("kv_norm_g", 256), ("conv_b", 512), ("conv_w", 1536), ("ffn_conv_b", 5632),
         ("ffn_conv_w", 16896), ("loss", 128))
SMALL_ROWS = 320


def _adam_update(w, g, m, v):
    c1, c2 = 1.0 - ADAM_B1 ** ADAM_STEP, 1.0 - ADAM_B2 ** ADAM_STEP
    m2 = ADAM_B1 * m + (1.0 - ADAM_B1) * g
    v2 = ADAM_B2 * v + (1.0 - ADAM_B2) * (g * g)
    return [-ADAM_LR * ((m2 / c1) / (jnp.sqrt(v2 / c2) + ADAM_EPS) + ADAM_WD * w), m2, v2]


def _adamw(w, g, m, v, name):
    R, C = w.shape
    tr = 8 if R % 8 == 0 else R
    for t in range(8, R + 1, 8):
        if R % t == 0 and t * C * 4 <= (1 << 20):
            tr = t
    spec = pl.BlockSpec((tr, C), lambda i: (i, 0))
    return _ew(lambda ids, *vals: _adam_update(*vals), (R // tr,), [(w, spec), (g, spec), (m, spec), (v, spec)],
               [((R, C), F32, spec, None)] * 3, name)


def kernel(x, c, ctx, c_ctx, w_ada, b_ada, norm1_g, w_in, q_norm_g, kv_norm_g, w_uq, w_ukv, conv_w, conv_b, w_attn_out, w_conv_out, w_o, norm2_g, w_up, ffn_conv_w, ffn_conv_b, w_down, final_g, loss_target, m_c_ctx, m_w_ada, m_b_ada, m_norm1_g, m_w_in, m_q_norm_g, m_kv_norm_g, m_w_uq, m_w_ukv, m_conv_w, m_conv_b, m_w_attn_out, m_w_conv_out, m_w_o, m_norm2_g, m_w_up, m_ffn_conv_w, m_ffn_conv_b, m_w_down, m_final_g, v_c_ctx, v_w_ada, v_b_ada, v_norm1_g, v_w_in, v_q_norm_g, v_kv_norm_g, v_w_uq, v_w_ukv, v_conv_w, v_conv_b, v_w_attn_out, v_w_conv_out, v_w_o, v_norm2_g, v_w_up, v_ffn_conv_w, v_ffn_conv_b, v_w_down, v_final_g):
    mx, my, mc = lax.axis_index("x"), lax.axis_index("y"), lax.axis_index("c")
    chip = 2 * mx + my
    dev = 4 * mx + 2 * my + mc
    T, Tc = x.shape[1], ctx.shape[1]
    TT = T + Tc
    w_in_t, m_w_in_t, v_w_in_t = (jnp.transpose(a[0]) for a in (w_in, m_w_in, v_w_in))
    w_uq_t, m_w_uq_t, v_w_uq_t = (jnp.transpose(a[0]) for a in (w_uq, m_w_uq, v_w_uq))
    conv_sh = jnp.concatenate([conv_w[0], ffn_conv_w[0]], axis=1)
    pay1 = jnp.concatenate([jnp.pad(c, ((0, 7), (0, 0))), jnp.pad(conv_sh, ((0, 5), (0, 0)))], axis=1)
    c_send, c_recv, c_src, c_land, zero0 = _ici_start("all", [pay1], [(8, 8, 2560)], jnp.zeros((8, 128), F32),
                                                      "cond_start")
    w_in_bf = (jnp.pad(w_in_t, ((0, W_IN_SHARD_PAD - W_IN_SHARD), (0, 0))) + zero0[0, 0]).astype(BF16)
    shards = {"w_in": w_in_bf, "w_uq": w_uq_t, "w_ukv": w_ukv[0], "w_attn_out": w_attn_out[0],
              "w_conv_out": w_conv_out[0], "w_o": w_o[0], "w_up": w_up[0], "w_down": w_down[0]}
    (pay1,), (c_land,) = _ici_wait("all", c_send, c_recv, c_src, c_land, shards["w_in"], "cond_wait")
    got1 = lax.dynamic_update_slice(c_land, pay1[None], (dev, 0, 0))
    c_all = got1[:, 0, :D_MODEL]
    conv_all = got1[0::2, :3, D_MODEL:]
    conv_w_full = _cols_from_shards(conv_all[:, :, :128])
    ffn_conv_w_full = _cols_from_shards(conv_all[:, :, 128:])

    cond = jnp.concatenate([c_all, c_ctx.reshape(1, D_MODEL), jnp.zeros((7, D_MODEL), F32)], axis=0)

    def f_silu(ids, v):
        return (v * _sigmoid(v),)

    (s16,) = _ew(f_silu, (1,), [(cond, _full((16, D_MODEL)))], [((16, D_MODEL), F32, _full((16, D_MODEL)), None)], "silu_cond")
    mod_sh = _mm(s16, w_ada[0], "nn", 16, 1536, D_MODEL, tm=16, tn=768, tk=D_MODEL, name="w_ada_fwd")
    m_send, m_recv, m_src, m_land, zero1 = _ici_start("all", [mod_sh], [(8, 16, 1536)], jnp.zeros((8, 128), F32),
                                                      "mod_start")
    shards["w_ukv"] = w_ukv[0] + zero1[0, 0]

    names = [n for n, _ in BIG]
    first = [n for n in names if n not in GATHER_LATE]
    gathered, zero = _gather_weights([shards[n].astype(BF16) for n in first])
    full = dict(zip(first, gathered))
    (mod_mine,), (m_land,) = _ici_wait("all", m_send, m_recv, m_src, m_land, gathered[0], "mod_wait")
    got2 = lax.dynamic_update_slice(m_land, mod_mine[None], (dev, 0, 0))
    mod_all = _cols_from_shards(got2[0::2]) + b_ada
    mod_lat = lax.dynamic_slice_in_dim(mod_all, dev, 1, axis=0)
    mod_ctx = mod_all[8:9]
    xx = jnp.concatenate([x[0], ctx[0]], axis=0)
    late_groups = {"g1": ("w_attn_out", "w_conv_out", "w_o"), "g2": ("w_up", "w_down")}
    flight = {}
    for tag, group in late_groups.items():
        bf = [(shards[n] + zero[0, 0]).astype(BF16) for n in group]
        flight[tag] = _ici_start("gather", bf, [(4,) + s.shape for s in bf], xx, "gather_" + tag + "_start")
        xx = flight[tag][4]

    def chip_stage_done(tag, x):
        send, recv, src, land, _ = flight[tag]
        src, land = _ici_wait("gather", send, recv, src, land, x, "gather_" + tag + "_wait")
        flight[tag] = _ici_start("finish", src, None, x, "finish_" + tag + "_start", lands=land)
        return flight[tag][4]

    def arrived(tag, x):
        send, recv, src, land, _ = flight[tag]
        return dict(zip(late_groups[tag], _ici_wait("finish", send, recv, src, land, x, "finish_" + tag + "_wait")[1]))

    def late_weights(point, x):
        if point == "before_attn":
            return {}, chip_stage_done("g1", x)
        if point == "after_attn":
            got = arrived("g1", x)
            wao = _cols_from_shards(got["w_attn_out"]).reshape(N_HEADS, 64, D_MODEL)
            ready = {"w_attn_out": jnp.pad(wao, ((0, 0), (64, 0), (0, 0))).reshape(N_HEADS * HEAD_PAD, D_MODEL),
                     "w_conv_out": got["w_conv_out"], "w_o": got["w_o"].reshape(D_MODEL, D_MODEL)}
            return ready, chip_stage_done("g2", x)
        got = arrived("g2", x)
        return {"w_up": got["w_up"], "w_down": got["w_down"].reshape(D_FF, D_MODEL)}, x

    wuq_t = full["w_uq"].reshape(N_HEADS, QK_DIM, Q_RANK)
    W = {
        "w_in_t": _w_in_t_p_from_shards(full["w_in"]),
        "w_uq_t": jnp.pad(wuq_t, ((0, 0), (0, HEAD_PAD - QK_DIM), (0, 0))).reshape(N_HEADS * HEAD_PAD, Q_RANK),
        "w_ukv": full["w_ukv"],
        "norm1_g": norm1_g, "norm2_g": norm2_g, "final_g": final_g.reshape(1, D_MODEL), "q_norm_g": q_norm_g,
        "kv_norm_g": kv_norm_g, "conv_w": conv_w_full, "conv_b": conv_b, "ffn_conv_w": ffn_conv_w_full,
        "ffn_conv_b": ffn_conv_b,
    }

    place = jnp.stack([chip, mc]).astype(jnp.int32)
    early = {}

    pending = {}

    def scatter(tag, group, gs, from_sib, carry):
        if tag == "mid":
            sums = _add_pair_many(gs, from_sib, place, "rs_pair_add_mid")
        else:
            sums = [_add_pair(gs[w], from_sib[w], place, "rs_pair_add_" + n) for w, n in enumerate(group)]
        send, recv, sums, land, carry = _ici_start(
            "scatter", sums, [(3,) + s.shape[1:] for s in sums], carry, "rs_chips_" + tag + "_start")
        early[tag] = (group, send, recv, sums, land)
        return carry

    def early_grads(tag, g, carry, split=False):
        gs = list(g.values())
        if not split:
            return scatter(tag, list(g), gs, _rs_pair(gs, "rs_pair_" + tag), carry)
        send, recv, gs, land, carry = _ici_start(
            "pair", gs, [(4, s.shape[1] // 2, s.shape[2]) for s in gs], carry, "rs_pair_" + tag + "_start")
        pending[tag] = (list(g), send, recv, gs, land)
        return carry

    def early_continue(tag, carry):
        group, send, recv, gs, land = pending[tag]
        gs, from_sib = _ici_wait("pair", send, recv, gs, land, carry, "rs_pair_" + tag + "_wait")
        return scatter(tag, group, gs, from_sib, carry)

    grad_x, loss_part, gbig, gsmall = _local_step(xx, loss_target[0], mod_lat, mod_ctx, W, late_weights, early_grads,
                                                  early_continue)

    gsmall["loss"] = loss_part
    pay3 = jnp.concatenate([gsmall[n].reshape(-1) for n, _ in SMALL])
    pay3 = jnp.pad(pay3, (0, SMALL_ROWS * 128 - pay3.shape[0])).reshape(SMALL_ROWS, 128)
    s_send, s_recv, s_src, s_land, w_in_thru = _ici_start("all", [pay3], [(8, SMALL_ROWS, 128)], gbig["w_in"],
                                                         "small_start")
    gbig = {"w_in": w_in_thru}

    after_small = early_grads("last", gbig, s_src[0])

    (pay3,), (s_land,) = _ici_wait("all", s_send, s_recv, [after_small], s_land, early["last"][3][0], "small_wait")
    got3 = lax.dynamic_update_slice(s_land, pay3[None], (dev, 0, 0)).reshape(8 * SMALL_ROWS, 128)

    def f_sum8(ids, a):
        s = a[0:SMALL_ROWS]
        for d in range(1, 8):
            s = s + a[d * SMALL_ROWS:(d + 1) * SMALL_ROWS]
        return (s,)

    (vsum,) = _ew(f_sum8, (1,), [(got3, _full((8 * SMALL_ROWS, 128)))],
                  [((SMALL_ROWS, 128), F32, _full((SMALL_ROWS, 128)), None)], "sum_small")
    vflat = vsum.reshape(-1)
    gvec, off = {}, 0
    for n, size in SMALL:
        gvec[n] = vflat[off:off + size]
        off += size
    loss = gvec["loss"][0]
    dmod_rows = got3.reshape(8, SMALL_ROWS * 128)[:, :6 * D_MODEL]
    dm16 = jnp.concatenate([dmod_rows, gvec["dmod_ctx"].reshape(1, -1), jnp.zeros((7, 6 * D_MODEL), F32)], axis=0)

    def f_colsum(ids, a):
        return (_colsum(a),)

    (g_b_ada,) = _ew(f_colsum, (1,), [(dm16, _full((16, 6 * D_MODEL)))],
                     [((1, 6 * D_MODEL), F32, _full((1, 6 * D_MODEL)), None)], "b_ada_grad")
    dm_sh = lax.dynamic_slice_in_dim(dm16, chip * 1536, 1536, axis=1)
    g_w_ada = _mm(s16, dm_sh, "tn", D_MODEL, 1536, 16, tm=512, tn=768, tk=16, name="w_ada_dw")
    dcond_part = _mm(dm_sh, w_ada[0], "nt", 16, D_MODEL, 1536, tm=16, tn=512, tk=1536, name="w_ada_dx")
    d_send, d_recv, d_src, d_land, vsum = _ici_start("all", [dcond_part[8:16]], [(8, 8, D_MODEL)], vsum, "dcond_start")

    def finish_start(tags, after):
        done, halves = [], []
        for tag in tags:
            tag_names, send, recv, sums, land = early[tag]
            sums, land = _ici_wait("scatter", send, recv, sums, land, after, "rs_chips_" + tag + "_wait")
            done += tag_names
            if tag == "mid":
                halves += _add_chips_many(sums, land, place, "rs_chip_add_mid")
            else:
                halves += [_add_chips(a, b, place, "rs_chip_add_" + n) for a, b, n in zip(sums, land, tag_names)]
        send, recv, _, halves, _ = _ici_start("back", [], None, jnp.zeros((8, 128), F32), "rs_back_" + tags[0] + "_start",
                                              lands=halves)
        return done, send, recv, halves

    def finish_wait(state, after):
        done, send, recv, halves = state
        return dict(zip(done, _ici_wait("back", send, recv, [], halves, after, "rs_back_" + done[0] + "_wait")[1]))

    grads, deltas, new_m, new_v = {}, {}, {}, {}

    raw = {}

    def adam(n, w_, m_, v_, g, transposed):
        d_, m2, v2 = _adamw(w_, g, m_, v_, "adamw_" + n)
        raw[n] = d_
        back = (lambda a: jnp.transpose(a)[None]) if transposed else (lambda a: a[None])
        grads[n], deltas[n], new_m[n], new_v[n] = back(g[:w_.shape[0]]), back(d_), back(m2), back(v2)

    pending_back = finish_start(["late", "mid"], grad_x)
    adam("w_ada", w_ada[0], m_w_ada[0], v_w_ada[0], g_w_ada, False)
    gw = finish_wait(pending_back, raw["w_ada"])
    for n, (w_, m_, v_) in {"w_o": (w_o, m_w_o, v_w_o), "w_up": (w_up, m_w_up, v_w_up),
                            "w_down": (w_down, m_w_down, v_w_down)}.items():
        adam(n, w_[0], m_[0], v_[0], gw[n], False)
    pending_back = finish_start(["last"], raw["w_up"])

    (dcond_mine,), (d_land,) = _ici_wait("all", d_send, d_recv, d_src, d_land, raw["w_down"], "dcond_wait")
    got4 = lax.dynamic_update_slice(d_land, dcond_mine[None], (dev, 0, 0))[0::2, 0]

    def f_c_ctx(ids, parts, cc):
        s = _sigmoid(cc)
        d = parts[0:1] + parts[1:2] + parts[2:3] + parts[3:4]
        return (d * s * (1.0 + cc * (1.0 - s)),)

    (g_c_ctx,) = _ew(f_c_ctx, (1,), [(got4, _full((4, D_MODEL))), (c_ctx.reshape(1, D_MODEL), _full((1, D_MODEL)))],
                     [((1, D_MODEL), F32, _full((1, D_MODEL)), None)], "c_ctx_grad")

    conv_w_g = lax.dynamic_slice_in_dim(gvec["conv_w"].reshape(3, CONV_DIM), chip * 128, 128, axis=1)
    ffn_conv_w_g = lax.dynamic_slice_in_dim(gvec["ffn_conv_w"].reshape(3, 2 * D_FF), chip * 1408, 1408, axis=1)
    vec_params = (("c_ctx", c_ctx, m_c_ctx, v_c_ctx, g_c_ctx), ("b_ada", b_ada, m_b_ada, v_b_ada, g_b_ada),
                  ("norm1_g", norm1_g, m_norm1_g, v_norm1_g, gvec["norm1_g"]),
                  ("q_norm_g", q_norm_g, m_q_norm_g, v_q_norm_g, gvec["q_norm_g"]),
                  ("kv_norm_g", kv_norm_g, m_kv_norm_g, v_kv_norm_g, gvec["kv_norm_g"]),
                  ("conv_w", conv_w, m_conv_w, v_conv_w, conv_w_g), ("conv_b", conv_b, m_conv_b, v_conv_b, gvec["conv_b"]),
                  ("norm2_g", norm2_g, m_norm2_g, v_norm2_g, gvec["norm2_g"]),
                  ("ffn_conv_w", ffn_conv_w, m_ffn_conv_w, v_ffn_conv_w, ffn_conv_w_g),
                  ("ffn_conv_b", ffn_conv_b, m_ffn_conv_b, v_ffn_conv_b, gvec["ffn_conv_b"]),
                  ("final_g", final_g, m_final_g, v_final_g, gvec["final_g"]))
    two_d = lambda a: a.reshape((-1, a.shape[-1]))
    many = [p + ((lambda r, s=p[1].shape: r.reshape(s)),) for p in vec_params]
    for n, w_, m_, v_ in (("w_ukv", w_ukv, m_w_ukv, v_w_ukv), ("w_attn_out", w_attn_out, m_w_attn_out, v_w_attn_out),
                          ("w_conv_out", w_conv_out, m_w_conv_out, v_w_conv_out)):
        many.append((n, w_, m_, v_, gw[n], (lambda r, s=w_.shape: r.reshape(s))))
    many.append(("w_uq", w_uq_t, m_w_uq_t, v_w_uq_t, gw["w_uq"], lambda r: jnp.transpose(r)[None]))

    def f_adam_many(ids, *vals):
        out = []
        for k in range(len(many)):
            out += _adam_update(*vals[4 * k:4 * k + 4])
        return out

    ins_v, outs_v = [], []
    for p in many:
        shp = two_d(p[1]).shape
        ins_v += [(two_d(a), _full(shp)) for a in (p[1], p[4], p[2], p[3])]
        outs_v += [(shp, F32, _full(shp), None)] * 3
    res_v = _ew(f_adam_many, (1,), ins_v, outs_v, "adamw_small")
    for k, p in enumerate(many):
        n, post = p[0], p[5]
        grads[n] = post(two_d(p[4]))
        deltas[n], new_m[n], new_v[n] = (post(r) for r in res_v[3 * k:3 * k + 3])

    gw_in = finish_wait(pending_back, res_v[0])
    adam("w_in", w_in_t, m_w_in_t, v_w_in_t, gw_in["w_in"], True)

    order = ("c_ctx", "w_ada", "b_ada", "norm1_g", "w_in", "q_norm_g", "kv_norm_g", "w_uq", "w_ukv", "conv_w", "conv_b",
             "w_attn_out", "w_conv_out", "w_o", "norm2_g", "w_up", "ffn_conv_w", "ffn_conv_b", "w_down", "final_g")
    return (loss, grad_x[None], *[grads[n] for n in order], *[deltas[n] for n in order],
            *[new_m[n] for n in order], *[new_v[n] for n in order])
```

```python
import functools

import jax
import jax.numpy as jnp
import numpy as np
from jax import lax
from jax.experimental import pallas as pl
from jax.experimental.pallas import tpu as pltpu

F32, BF16 = jnp.float32, jnp.bfloat16
MESH = pl.DeviceIdType.MESH

D_MODEL = 1024
N_HEADS = 8
HEAD_PAD = 128
QK_DIM = 96
Q_RANK, KV_RANK = 384, 256
CONV_DIM = 512
D_FF = 2816
GRID_W = 64
ROPE_THETA = 10000.0
EPS = 1e-6
GA0, GC0, CX0, CB0, CC0, KV0, Q0, KR0, P_COLS = 0, 1024, 2048, 2560, 3072, 3584, 3840, 4224, 4352
ROW_TILE = 256
VMEM_LIMIT_BYTES = 48 * 1024 * 1024

ADAM_LR, ADAM_B1, ADAM_B2, ADAM_EPS, ADAM_WD, ADAM_STEP = 0.001, 0.9, 0.999, 1e-08, 0.01, 10

BIG = (("w_in", (1088, 1024)), ("w_uq", (192, 384)), ("w_ukv", (256, 256)), ("w_attn_out", (512, 256)),
       ("w_conv_out", (512, 256)), ("w_o", (256, 1024)), ("w_up", (1024, 1408)), ("w_down", (704, 1024)))

GATHER_LATE = ("w_attn_out", "w_conv_out", "w_o", "w_up", "w_down")

NN = (((1,), (0,)), ((), ()))
NT = (((1,), (1,)), ((), ()))
TN = (((0,), (0,)), ((), ()))


def _cp(sem):
    return pltpu.CompilerParams(dimension_semantics=sem, vmem_limit_bytes=VMEM_LIMIT_BYTES)


PIN_BYTES = 1 << 19


def _in_hbm(arrays):
    return [pltpu.with_memory_space_constraint(a, pltpu.HBM) if a.size * a.dtype.itemsize >= PIN_BYTES else a
            for a in arrays]


def _out(shape, dtype):
    n = 1
    for d in shape:
        n *= d
    big = n * jnp.dtype(dtype).itemsize >= PIN_BYTES
    return pltpu.HBM(shape, dtype) if big else jax.ShapeDtypeStruct(shape, dtype)


def _pick(n, prefs):
    for p in prefs:
        if n % p == 0:
            return p
    return n


def _mm(a, b, mode, M, N, K, *, tm, tn, tk, name, out_dtype=F32, a_spec=None, b_spec=None, o_spec=None,
        out_shape=None, transpose_out=False):
    assert M % tm == 0 and N % tn == 0 and K % tk == 0, (name, M, N, K, tm, tn, tk)
    nk = K // tk
    dims = {"nn": NN, "nt": NT, "tn": TN}[mode]
    if a_spec is None:
        a_spec = (pl.BlockSpec((tk, tm), lambda i, j, k: (k, i)) if mode == "tn"
                  else pl.BlockSpec((tm, tk), lambda i, j, k: (i, k)))
    if b_spec is None:
        b_spec = (pl.BlockSpec((tn, tk), lambda i, j, k: (j, k)) if mode == "nt"
                  else pl.BlockSpec((tk, tn), lambda i, j, k: (k, j)))
    if o_spec is None:
        o_spec = (pl.BlockSpec((tn, tm), lambda i, j, k: (j, i)) if transpose_out
                  else pl.BlockSpec((tm, tn), lambda i, j, k: (i, j)))
    if out_shape is None:
        out_shape = (N, M) if transpose_out else (M, N)

    def emit(o_ref, val):
        o_ref[...] = (val.T if transpose_out else val).astype(o_ref.dtype)

    def body(a_ref, b_ref, o_ref, *scratch):
        part = lax.dot_general(a_ref[...].astype(BF16), b_ref[...].astype(BF16), dims, preferred_element_type=F32)
        if nk == 1:
            emit(o_ref, part)
            return
        acc_ref, = scratch
        k = pl.program_id(2)

        @pl.when(k == 0)
        def _():
            acc_ref[...] = part

        @pl.when((k > 0) & (k < nk - 1))
        def _():
            acc_ref[...] += part

        @pl.when(k == nk - 1)
        def _():
            emit(o_ref, acc_ref[...] + part)

    return pl.pallas_call(
        body, grid=(M // tm, N // tn, nk), in_specs=[a_spec, b_spec], out_specs=o_spec,
        out_shape=_out(out_shape, out_dtype),
        scratch_shapes=[pltpu.VMEM((tm, tn), F32)] if nk > 1 else [],
        compiler_params=_cp(("parallel", "parallel", "arbitrary")), name=name)(*_in_hbm([a, b]))


def _ew(fn, grid, ins, outs, name, scalars=None):
    n_in = len(ins)
    n_sc = 0 if scalars is None else 1

    def store(ref, val, acc, ids):
        if isinstance(val, (list, tuple)):
            for h, v in enumerate(val):
                ref[h] = v.astype(ref.dtype)
            return
        if acc is None:
            ref[...] = val.astype(ref.dtype)
            return

        @pl.when(ids[acc] == 0)
        def _():
            ref[...] = val.astype(ref.dtype)

        @pl.when(ids[acc] > 0)
        def _():
            ref[...] += val.astype(ref.dtype)

    def body(*refs):
        refs = refs[n_sc:]
        ids = tuple(pl.program_id(a) for a in range(len(grid)))
        vals = fn(ids, *[r[...] for r in refs[:n_in]])
        for ref, val, (_, _, _, acc) in zip(refs[n_in:], vals, outs):
            store(ref, val, acc, ids)

    acc_axes = {o[3] for o in outs if o[3] is not None}
    sem = tuple("arbitrary" if a in acc_axes else "parallel" for a in range(len(grid)))
    in_specs, out_specs = [s for _, s in ins], [o[2] for o in outs]
    out_shape = [_out(o[0], o[1]) for o in outs]
    args = _in_hbm([a for a, _ in ins])
    if scalars is None:
        return pl.pallas_call(body, grid=grid, in_specs=in_specs, out_specs=out_specs, out_shape=out_shape,
                              compiler_params=_cp(sem), name=name)(*args)
    spec = pltpu.PrefetchScalarGridSpec(num_scalar_prefetch=1, grid=grid, in_specs=in_specs, out_specs=out_specs)
    return pl.pallas_call(body, grid_spec=spec, out_shape=out_shape, compiler_params=_cp(sem), name=name)(scalars, *args)


def _rows(width, cblk=0, roff=0, tr=ROW_TILE):
    return pl.BlockSpec((tr, width), lambda i: (i + roff, cblk))


def _full(shape):
    nd = len(shape)
    return pl.BlockSpec(shape, lambda *_: (0,) * nd)


def _sigmoid(x):
    return 1.0 / (1.0 + jnp.exp2(x * (-1.4426950408889634)))


def _rms(x):
    return lax.rsqrt(jnp.mean(x * x, axis=-1, keepdims=True) + EPS)


def _rms_bwd(dn, xn, r):
    return r * (dn - xn * jnp.mean(dn * xn, axis=-1, keepdims=True))


def _colsum(x):
    return jnp.sum(x, axis=0, keepdims=True)


def _shifts(x):
    n = x.shape[0]
    rows = lax.broadcasted_iota(jnp.int32, x.shape, 0)
    return jnp.where(rows == 0, 0.0, pltpu.roll(x, 1, 0)), jnp.where(rows == n - 1, 0.0, pltpu.roll(x, n - 1, 0))


def _conv(x, w, b, shifted=None):
    prev, nxt = _shifts(x) if shifted is None else shifted
    return b + prev * w[0:1] + x * w[1:2] + nxt * w[2:3]


def _conv_bwd_x(dy, w):
    prev, nxt = _shifts(dy)
    return nxt * w[0:1] + dy * w[1:2] + prev * w[2:3]


def _conv_bwd_w(dy, x, shifted):
    prev, nxt = shifted
    return _colsum(dy * prev), _colsum(dy * x), _colsum(dy * nxt)


def _rope(x, cos, sin_lo, sin_hi):
    return x * cos + pltpu.roll(x, HEAD_PAD - 8, 1) * sin_lo + pltpu.roll(x, 8, 1) * sin_hi


ATTN_SCALE = QK_DIM ** -0.5
LOG2_E = 1.4426950408889634


def _rope_t(x, tab, inverse=False):
    o = 3 * HEAD_PAD if inverse else 0
    return _rope(x, tab[:, o:o + HEAD_PAD], tab[:, o + HEAD_PAD:o + 2 * HEAD_PAD], tab[:, o + 2 * HEAD_PAD:o + 3 * HEAD_PAD])


def _head_keys(kv_ref, kr_ref, tab_ref, kc_ref, vp_ref):
    kv = kv_ref[...]
    lane = lax.broadcasted_iota(jnp.int32, kv.shape, 1)
    kc_ref[...] = jnp.where(lane < 64, kv, _rope_t(kr_ref[...], tab_ref[...])).astype(BF16)
    vp_ref[...] = jnp.where(lane >= 64, kv, 0.0).astype(BF16)


ATTN_Q_TILE = 512


def _attn_specs(tq, TT):
    q = pl.BlockSpec((tq, HEAD_PAD), lambda h, i: (i, h))
    keys = pl.BlockSpec((TT, HEAD_PAD), lambda h, i: (0, h))
    kr = pl.BlockSpec((TT, HEAD_PAD), lambda h, i: (0, KR0 // HEAD_PAD))
    tab_q = pl.BlockSpec((tq, 6 * HEAD_PAD), lambda h, i: (i, 0))
    tab_k = pl.BlockSpec((TT, 6 * HEAD_PAD), lambda h, i: (0, 0))
    lse = pl.BlockSpec((None, tq, 1), lambda h, i: (h, i, 0))
    return q, keys, kr, tab_q, tab_k, lse


def _attn_fwd(q_raw, kv, pp, tab, T, TT):
    tq = ROW_TILE

    def body(q_ref, kv_ref, kr_ref, tq_ref, tk_ref, o_ref, l_ref, kc, vp):
        @pl.when(pl.program_id(1) == 0)
        def _():
            _head_keys(kv_ref, kr_ref, tk_ref, kc, vp)

        q = _rope_t(q_ref[...], tq_ref[...]).astype(BF16)
        s = lax.dot_general(q, kc[...], NT, preferred_element_type=F32)
        m = jnp.max(s, axis=-1, keepdims=True)
        p = jnp.exp2((s - m) * (ATTN_SCALE * LOG2_E))
        l = jnp.sum(p, axis=-1, keepdims=True)
        o = lax.dot_general(p.astype(BF16), vp[...], NN, preferred_element_type=F32)
        o_ref[...] = o / l
        l_ref[...] = m * ATTN_SCALE + jnp.log(l)

    qs, keys, kr, tab_q, tab_k, lse = _attn_specs(tq, TT)
    return pl.pallas_call(
        body, grid=(N_HEADS, T // tq), in_specs=[qs, keys, kr, tab_q, tab_k],
        out_specs=[qs, lse],
        out_shape=[jax.ShapeDtypeStruct((T, N_HEADS * HEAD_PAD), F32), jax.ShapeDtypeStruct((N_HEADS, T, 1), F32)],
        scratch_shapes=[pltpu.VMEM((TT, HEAD_PAD), BF16), pltpu.VMEM((TT, HEAD_PAD), BF16)],
        compiler_params=_cp(("parallel", "arbitrary")), name="attn_fwd",
    )(*_in_hbm([q_raw, kv, pp, tab, tab]))


def _attn_bwd(q_raw, kv, pp, o, do, lse, tab, T, TT):
    tq = _pick(T, (ATTN_Q_TILE, ROW_TILE))
    nq = T // tq

    def body(q_ref, kv_ref, kr_ref, tq_ref, tk_ref, o_ref, do_ref, l_ref, dq_ref, dkv_ref, dkr_ref, kc, vp, dk, dv):
        h, i = pl.program_id(0), pl.program_id(1)

        @pl.when(i == 0)
        def _():
            _head_keys(kv_ref, kr_ref, tk_ref, kc, vp)
            dk[...] = jnp.zeros_like(dk)
            dv[...] = jnp.zeros_like(dv)

        q = _rope_t(q_ref[...], tq_ref[...]).astype(BF16)
        k, v, d_o = kc[...], vp[...], do_ref[...]
        s = lax.dot_general(q, k, NT, preferred_element_type=F32)
        p = jnp.exp2(s * (ATTN_SCALE * LOG2_E) - l_ref[...] * LOG2_E)
        dob = d_o.astype(BF16)
        dp = lax.dot_general(dob, v, NT, preferred_element_type=F32)
        dd = jnp.sum(d_o * o_ref[...], axis=-1, keepdims=True)
        ds = (p * (dp - dd) * ATTN_SCALE).astype(BF16)
        dq = lax.dot_general(ds, k, NN, preferred_element_type=F32)
        dq_ref[...] = _rope_t(dq, tq_ref[...], inverse=True).astype(dq_ref.dtype)
        dk[...] += lax.dot_general(q, ds, TN, preferred_element_type=F32)
        dv[...] += lax.dot_general(dob, p.astype(BF16), TN, preferred_element_type=F32)

        @pl.when(i == nq - 1)
        def _():
            dkh = dk[...].T
            lane = lax.broadcasted_iota(jnp.int32, dkh.shape, 1)
            dkv_ref[...] = jnp.where(lane < 64, dkh, dv[...].T).astype(dkv_ref.dtype)
            rot = _rope_t(jnp.where((lane >= 64) & (lane < 96), dkh, 0.0), tk_ref[...], inverse=True)

            @pl.when(h == 0)
            def _():
                dkr_ref[...] = rot

            @pl.when(h > 0)
            def _():
                dkr_ref[...] += rot

    qs, keys, kr, tab_q, tab_k, lse_spec = _attn_specs(tq, TT)
    wide = lambda rows: jax.ShapeDtypeStruct((rows, N_HEADS * HEAD_PAD), BF16)
    return pl.pallas_call(
        body, grid=(N_HEADS, nq),
        in_specs=[qs, keys, kr, tab_q, tab_k, qs, qs, lse_spec],
        out_specs=[qs, keys, pl.BlockSpec((TT, HEAD_PAD), lambda h, i: (0, 0))],
        out_shape=[wide(T), wide(TT), jax.ShapeDtypeStruct((TT, HEAD_PAD), F32)],
        scratch_shapes=[pltpu.VMEM((TT, HEAD_PAD), BF16), pltpu.VMEM((TT, HEAD_PAD), BF16),
                        pltpu.VMEM((HEAD_PAD, TT), F32), pltpu.VMEM((HEAD_PAD, TT), F32)],
        compiler_params=_cp(("arbitrary", "arbitrary")), name="attn_bwd",
    )(*_in_hbm([q_raw, kv, pp, tab, tab, o, do, lse]))


def _hbm_specs(n):
    return [pl.BlockSpec(memory_space=pl.ANY)] * n


def _gather_weights(shards):
    n = len(shards)
    halves = [s.shape[0] // 2 for s in shards]

    def body(*refs):
        ins, outs = refs[:n], refs[n:2 * n]
        token, send_sems, recv_sems = refs[2 * n:]
        token[...] = jnp.zeros_like(token)
        mx, my, mc = lax.axis_index("x"), lax.axis_index("y"), lax.axis_index("c")
        j_me = 2 * mx + my
        chips = [(1 - mx, my), (mx, 1 - my), (1 - mx, 1 - my)]

        def half(w, chip_idx, hc):
            return outs[w].at[chip_idx, pl.ds(hc * halves[w], halves[w]), :]

        def copy(w, k, src, dst, to):
            return pltpu.make_async_remote_copy(src_ref=src, dst_ref=dst, send_sem=send_sems.at[w, k],
                                                recv_sem=recv_sems.at[w, k], device_id=to, device_id_type=MESH)

        sends = []
        for w in range(n):
            cp = copy(w, 6, ins[w], outs[w].at[j_me], (mx, my, 1 - mc))
            cp.start()
            sends.append(cp)
        for k, (px, py) in enumerate(chips):
            for w in range(n):
                cp = copy(w, k, ins[w].at[pl.ds(mc * halves[w], halves[w]), :], half(w, j_me, mc), (px, py, mc))
                cp.start()
                sends.append(cp)
        for k, (px, py) in enumerate(chips):
            for w in range(n):
                got = half(w, 2 * px + py, mc)
                copy(w, k, got, got, (px, py, mc)).wait_recv()
                cp = copy(w, 3 + k, got, got, (mx, my, 1 - mc))
                cp.start()
                sends.append(cp)
        for k, (px, py) in enumerate(chips):
            for w in range(n):
                got = half(w, 2 * px + py, 1 - mc)
                copy(w, 3 + k, got, got, (mx, my, 1 - mc)).wait_recv()
        for w in range(n):
            own = outs[w].at[j_me]
            copy(w, 6, own, own, (mx, my, 1 - mc)).wait_recv()
        for cp in sends:
            cp.wait_send()

    res = pl.pallas_call(
        body, out_shape=[jax.ShapeDtypeStruct((4,) + s.shape, s.dtype) for s in shards]
        + [jax.ShapeDtypeStruct((8, 128), F32)],
        in_specs=_hbm_specs(n), out_specs=_hbm_specs(n) + [pl.BlockSpec(memory_space=pltpu.VMEM)],
        scratch_shapes=[pltpu.SemaphoreType.DMA((n, 7)), pltpu.SemaphoreType.DMA((n, 7))],
        name="gather_weights")(*shards)
    return list(res[:n]), res[n]


def _rs_pair(gs, name):
    n = len(gs)
    halves = [g.shape[1] // 2 for g in gs]

    def body(*refs):
        ins, lands = refs[:n], refs[n:2 * n]
        send_sems, recv_sems = refs[2 * n:]
        mx, my, mc = lax.axis_index("x"), lax.axis_index("y"), lax.axis_index("c")
        copies = []
        for w in range(n):
            h = halves[w]
            cp = pltpu.make_async_remote_copy(
                src_ref=ins[w].at[:, pl.ds((1 - mc) * h, h), :], dst_ref=lands[w], send_sem=send_sems.at[w],
                recv_sem=recv_sems.at[w], device_id=(mx, my, 1 - mc), device_id_type=MESH)
            cp.start()
            copies.append(cp)
        for cp in copies:
            cp.wait()

    return pl.pallas_call(
        body, out_shape=[jax.ShapeDtypeStruct((4, h, g.shape[2]), g.dtype) for g, h in zip(gs, halves)],
        in_specs=_hbm_specs(n), out_specs=_hbm_specs(n),
        scratch_shapes=[pltpu.SemaphoreType.DMA((n,)), pltpu.SemaphoreType.DMA((n,))], name=name)(*gs)


def _rs_chips(parts):
    n = len(parts)

    def body(*refs):
        ins, lands = refs[:n], refs[n:2 * n]
        send_sems, recv_sems = refs[2 * n:]
        mx, my, mc = lax.axis_index("x"), lax.axis_index("y"), lax.axis_index("c")
        copies = []
        for k, (px, py) in enumerate([(1 - mx, my), (mx, 1 - my), (1 - mx, 1 - my)]):
            for w in range(n):
                cp = pltpu.make_async_remote_copy(
                    src_ref=ins[w].at[2 * px + py], dst_ref=lands[w].at[k], send_sem=send_sems.at[w, k],
                    recv_sem=recv_sems.at[w, k], device_id=(px, py, mc), device_id_type=MESH)
                cp.start()
                copies.append(cp)
        for cp in copies:
            cp.wait()

    return list(pl.pallas_call(
        body, out_shape=[jax.ShapeDtypeStruct((3,) + p.shape[1:], p.dtype) for p in parts],
        in_specs=_hbm_specs(n), out_specs=_hbm_specs(n),
        scratch_shapes=[pltpu.SemaphoreType.DMA((n, 3)), pltpu.SemaphoreType.DMA((n, 3))], name="rs_chips")(*parts))


_HBM = pl.BlockSpec(memory_space=pltpu.HBM)
_SEM = pl.BlockSpec(memory_space=pltpu.SEMAPHORE)
_EFFECT = pltpu.SideEffectType.DATAFLOW_SIDE_EFFECTING


def _ici_copies(kind, srcs, lands, send_sems, recv_sems):
    n = len(lands)
    mx, my, mc = lax.axis_index("x"), lax.axis_index("y"), lax.axis_index("c")
    j_me = 2 * mx + my
    copies = []
    if kind == "back":
        for w in range(n):
            h = lands[w].shape[0] // 2
            mine = lands[w].at[pl.ds(mc * h, h), :]
            copies.append(pltpu.make_async_remote_copy(
                src_ref=mine, dst_ref=mine, send_sem=send_sems.at[w], recv_sem=recv_sems.at[w],
                device_id=(mx, my, 1 - mc), device_id_type=MESH))
        return copies
    if kind == "all":
        for k in range(7):
            a, b, c = (k + 1) >> 2 & 1, (k + 1) >> 1 & 1, (k + 1) & 1
            peer = (1 - mx if a else mx, 1 - my if b else my, 1 - mc if c else mc)
            for w in range(n):
                copies.append(pltpu.make_async_remote_copy(
                    src_ref=srcs[w], dst_ref=lands[w].at[4 * mx + 2 * my + mc], send_sem=send_sems.at[7 * w + k],
                    recv_sem=recv_sems.at[7 * w + k], device_id=peer, device_id_type=MESH))
        return copies
    if kind == "pair":
        for w in range(n):
            h = srcs[w].shape[1] // 2
            copies.append(pltpu.make_async_remote_copy(
                src_ref=srcs[w].at[:, pl.ds((1 - mc) * h, h), :], dst_ref=lands[w], send_sem=send_sems.at[w],
                recv_sem=recv_sems.at[w], device_id=(mx, my, 1 - mc), device_id_type=MESH))
        return copies
    chips = [(1 - mx, my), (mx, 1 - my), (1 - mx, 1 - my)]
    if kind == "finish":
        for w in range(n):
            h = srcs[w].shape[0] // 2
            pushes = [(lands[w].at[2 * px + py, pl.ds(mc * h, h), :],) * 2 for px, py in chips]
            pushes.append((srcs[w], lands[w].at[j_me]))
            for k, (src, dst) in enumerate(pushes):
                copies.append(pltpu.make_async_remote_copy(
                    src_ref=src, dst_ref=dst, send_sem=send_sems.at[4 * w + k], recv_sem=recv_sems.at[4 * w + k],
                    device_id=(mx, my, 1 - mc), device_id_type=MESH))
        return copies
    for k, (px, py) in enumerate(chips):
        for w in range(n):
            if kind == "gather":
                h = srcs[w].shape[0] // 2
                src, dst = srcs[w].at[pl.ds(mc * h, h), :], lands[w].at[j_me, pl.ds(mc * h, h), :]
            else:
                src, dst = srcs[w].at[2 * px + py], lands[w].at[k]
            copies.append(pltpu.make_async_remote_copy(
                src_ref=src, dst_ref=dst, send_sem=send_sems.at[3 * w + k], recv_sem=recv_sems.at[3 * w + k],
                device_id=(px, py, mc), device_id_type=MESH))
    return copies


_SEMS_PER_OPERAND = {"gather": 3, "scatter": 3, "all": 7, "pair": 1, "finish": 4, "back": 1}


def _ici_start(kind, srcs, land_shapes, carry, name, lands=None):
    hbm = lambda a: pltpu.with_memory_space_constraint(a, pltpu.HBM)
    if lands is None:
        lands = [lax.empty(s, srcs[0].dtype) for s in land_shapes]
    ns, nl = len(srcs), len(lands)

    def body(*refs):
        send_sems, recv_sems = refs[ns + nl + 1], refs[ns + nl + 2]
        for cp in _ici_copies(kind, refs[:ns], refs[ns:ns + nl], send_sems, recv_sems):
            cp.start()

    args = [hbm(a) for a in list(srcs) + list(lands) + [carry]]
    n_sem = _SEMS_PER_OPERAND[kind] * nl
    out_shape = ([pltpu.SemaphoreType.DMA((n_sem,)), pltpu.SemaphoreType.DMA((n_sem,))]
                 + [pltpu.HBM(a.shape, a.dtype) for a in args])
    res = pl.pallas_call(
        body, name=name, out_shape=out_shape, in_specs=[_HBM] * len(args), out_specs=[_SEM, _SEM] + [_HBM] * len(args),
        input_output_aliases={i: 2 + i for i in range(len(args))},
        compiler_params=pltpu.CompilerParams(has_side_effects=_EFFECT))(*args)
    return res[0], res[1], list(res[2:2 + ns]), list(res[2 + ns:2 + ns + nl]), res[2 + ns + nl]


def _ici_wait(kind, send_sems, recv_sems, srcs, lands, after, name):
    ns, nl = len(srcs), len(lands)

    def body(*refs):
        for cp in _ici_copies(kind, refs[:ns], refs[ns:ns + nl], refs[ns + nl], refs[ns + nl + 1]):
            cp.wait_send()
            cp.wait_recv()

    args = list(srcs) + list(lands)
    res = pl.pallas_call(
        body, name=name, out_shape=[pltpu.HBM(a.shape, a.dtype) for a in args],
        in_specs=[_HBM] * len(args) + [_SEM, _SEM, pl.BlockSpec(memory_space=pl.ANY)], out_specs=[_HBM] * len(args),
        input_output_aliases={i: i for i in range(len(args))},
        compiler_params=pltpu.CompilerParams(has_side_effects=_EFFECT))(*args, send_sems, recv_sems, after)
    return list(res[:ns]), list(res[ns:])


def _tile_rows(h, c, itemsize, mult):
    best = h
    for t in range(mult, h + 1, mult):
        if h % t == 0 and t * c * itemsize <= (1 << 21):
            best = t
    return best


def _add_pair(g, land, place, name):
    _, h, c = land.shape
    t = _tile_rows(h, c, 2, 16)
    nb = h // t
    return _ew(lambda ids, u, v: (u.astype(F32) + v.astype(F32),), (4, nb),
               [(g, pl.BlockSpec((None, t, c), lambda j, i, s: (j, s[1] * nb + i, 0))),
                (land, pl.BlockSpec((None, t, c), lambda j, i, s: (j, i, 0)))],
               [(land.shape, BF16, pl.BlockSpec((None, t, c), lambda j, i, s: (j, i, 0)), None)], name, scalars=place)[0]


def _add_pair_many(gs, lands, place, name):
    ins, outs = [], []
    for g, l in zip(gs, lands):
        ins += [(g, pl.BlockSpec(l.shape, lambda i, s: (0, s[1], 0))), (l, pl.BlockSpec(l.shape, lambda i, s: (0, 0, 0)))]
        outs.append((l.shape, BF16, pl.BlockSpec(l.shape, lambda i, s: (0, 0, 0)), None))
    fn = lambda ids, *v: [v[2 * k].astype(F32) + v[2 * k + 1].astype(F32) for k in range(len(gs))]
    return list(_ew(fn, (1,), ins, outs, name, scalars=place))


def _add_chips_many(owns, lands, place, name):
    ins, outs = [], []
    for own, land in zip(owns, lands):
        _, h, c = land.shape
        ins += [(own, pl.BlockSpec((None, h, c), lambda i, s: (s[0], 0, 0))),
                (land, pl.BlockSpec((3, h, c), lambda i, s: (0, 0, 0)))]
        outs.append(((2 * h, c), F32, pl.BlockSpec((h, c), lambda i, s: (s[1], 0)), None))

    def fn(ids, *v):
        return [((v[2 * k].astype(F32) + v[2 * k + 1][0].astype(F32)) + v[2 * k + 1][1].astype(F32))
                + v[2 * k + 1][2].astype(F32) for k in range(len(owns))]

    return list(_ew(fn, (1,), ins, outs, name, scalars=place))


def _add_chips(own, land, place, name):
    _, h, c = land.shape
    t = _tile_rows(h, c, 4, 16)
    nb = h // t

    def fn(ids, a, b):
        return (((a.astype(F32) + b[0].astype(F32)) + b[1].astype(F32)) + b[2].astype(F32),)

    return _ew(fn, (nb,), [(own, pl.BlockSpec((None, t, c), lambda i, s: (s[0], i, 0))),
                           (land, pl.BlockSpec((3, t, c), lambda i, s: (0, i, 0)))],
               [((2 * h, c), F32, pl.BlockSpec((t, c), lambda i, s: (s[1] * nb + i, 0)), None)], name, scalars=place)[0]


W_IN_SEGMENTS = ((0, 256, KV0), (256, 288, KR0 + 64), (288, 672, Q0), (672, 1184, CX0), (1184, 1696, CB0),
                 (1696, 2208, CC0), (2208, 3232, GA0), (3232, 4256, GC0))
W_IN_SHARD = 1064


W_IN_SHARD_PAD = 1088


def _w_in_t_p_from_shards(s):
    pieces = []
    for o0, o1, p0 in sorted(W_IN_SEGMENTS, key=lambda t: t[2]):
        if p0 == KR0 + 64:
            pieces.append(jnp.zeros((64, s.shape[2]), s.dtype))
        for j in range(4):
            lo, hi = max(o0, j * W_IN_SHARD), min(o1, (j + 1) * W_IN_SHARD)
            if lo < hi:
                pieces.append(s[j, lo - j * W_IN_SHARD:hi - j * W_IN_SHARD])
    pieces.append(jnp.zeros((32, s.shape[2]), s.dtype))
    return jnp.concatenate(pieces, axis=0)


def _w_in_t_shards_from_p(g):
    shards = []
    for j in range(4):
        pieces = []
        for o0, o1, p0 in W_IN_SEGMENTS:
            lo, hi = max(o0, j * W_IN_SHARD), min(o1, (j + 1) * W_IN_SHARD)
            if lo < hi:
                pieces.append(g[p0 + lo - o0:p0 + hi - o0])
        pieces.append(jnp.zeros((W_IN_SHARD_PAD - W_IN_SHARD, g.shape[1]), g.dtype))
        shards.append(jnp.concatenate(pieces, axis=0))
    return jnp.stack(shards, axis=0)


def _cols_from_shards(s):
    return jnp.transpose(s, (1, 0, 2)).reshape(s.shape[1], -1)


def _rope_tables(T, TT, inverse):
    f32 = np.float32
    rows = T // GRID_W
    row = np.repeat(np.arange(rows), GRID_W).astype(f32)
    col = np.tile(np.arange(GRID_W), rows).astype(f32)
    inv = (f32(ROPE_THETA) ** (-np.arange(0, 16, 2, dtype=f32) / f32(16))).astype(f32)
    ang = np.concatenate([row[:, None] * inv, col[:, None] * inv], axis=-1).astype(f32)
    cos, sin = np.cos(ang).astype(f32), np.sin(ang).astype(f32)
    lane = np.arange(32)
    src = (lane // 16) * 8 + lane % 8
    lo = ((lane % 16) // 8 == 0).astype(f32)
    sgn = f32(-1.0 if inverse else 1.0)
    cos32 = cos[:, src]
    sin_lo32 = -sgn * sin[:, src] * lo
    sin_hi32 = sgn * sin[:, src] * (1 - lo)

    def widen(t32, fill):
        t = np.concatenate([np.full((T, 64), fill, f32), t32, np.full((T, 32), fill, f32)], axis=1)
        return np.concatenate([t, np.full((TT - T, HEAD_PAD), fill, f32)], axis=0)

    return [widen(cos32, 1.0), widen(sin_lo32, 0.0), widen(sin_hi32, 0.0)]


def _rope_table(T, TT):
    return jnp.asarray(np.concatenate(_rope_tables(T, TT, False) + _rope_tables(T, TT, True), axis=1))


def _local_step(xx, tgt, mod_lat, mod_ctx, W, late_weights, early_grads, early_continue):
    TT = xx.shape[0]
    T = tgt.shape[0]
    n_lat, n_all = T // ROW_TILE, TT // ROW_TILE
    sh1, sc1, g1, sh2, sc2, g2 = [mod_lat[:, k * D_MODEL:(k + 1) * D_MODEL] for k in range(6)]
    csh1, csc1 = mod_ctx[:, :D_MODEL], mod_ctx[:, D_MODEL:2 * D_MODEL]
    vec = lambda n: _full((1, n))
    row_out = lambda n, dt, rows=T: ((rows, n), dt, _rows(n), None)
    acc_out = lambda n: ((1, n), F32, _full((1, n)), 0)

    def f_norm1(ids, x, g, a_sh, a_sc, b_sh, b_sc):
        ctx = ids[0] >= n_lat
        sh, sc = jnp.where(ctx, b_sh, a_sh), jnp.where(ctx, b_sc, a_sc)
        return ((x * _rms(x) * g) * (1.0 + sc) + sh,)

    (hh,) = _ew(f_norm1, (n_all,), [(xx, _rows(D_MODEL)), (W["norm1_g"], vec(D_MODEL)), (sh1, vec(D_MODEL)),
                                   (sc1, vec(D_MODEL)), (csh1, vec(D_MODEL)), (csc1, vec(D_MODEL))],
                [row_out(D_MODEL, BF16, TT)], "norm1_fwd")
    tm_all = _pick(TT, (768, 256))
    pp = _mm(hh, W["w_in_t"], "nt", TT, P_COLS, D_MODEL, tm=tm_all, tn=2176, tk=D_MODEL, name="w_in_fwd")

    def f_lowrank(ids, ckv, cq, gkv, gq):
        return ckv * _rms(ckv) * gkv, cq * _rms(cq) * gq

    nkv, nq = _ew(f_lowrank, (n_all,), [(pp, _rows(KV_RANK, KV0 // KV_RANK)), (pp, _rows(Q_RANK, Q0 // Q_RANK)),
                                       (W["kv_norm_g"], vec(KV_RANK)), (W["q_norm_g"], vec(Q_RANK))],
                  [row_out(KV_RANK, BF16, TT), row_out(Q_RANK, BF16, TT)], "lowrank_norm_fwd")
    kv = _mm(nkv, W["w_ukv"], "nn", TT, 1024, KV_RANK, tm=tm_all, tn=256, tk=KV_RANK, name="w_ukv_fwd",
             b_spec=pl.BlockSpec((None, KV_RANK, 256), lambda i, j, k: (j, k, 0)))
    q_raw = _mm(nq, W["w_uq_t"], "nt", TT, 1024, Q_RANK, tm=tm_all, tn=1024, tk=Q_RANK, name="w_uq_fwd")

    tab = _rope_table(T, TT)
    _, q_raw = late_weights("before_attn", q_raw)
    o_pad, lse = _attn_fwd(q_raw, kv, pp, tab, T, TT)
    arrived, o_pad = late_weights("after_attn", o_pad)
    W = dict(W, **arrived)
    tm_lat = _pick(T, (1024, 512, 256))
    ya = _mm(o_pad, W["w_attn_out"], "nn", T, D_MODEL, 1024, tm=tm_lat, tn=D_MODEL, tk=1024, name="w_attn_out_fwd")

    tc = 256
    colT = lambda blk0: pl.BlockSpec((T, tc), lambda j: (0, blk0 + j))

    def f_conv(ids, xin, cb, cc, w, b):
        return (cb * _conv(cc * xin, w, b),)

    (e,) = _ew(f_conv, (CONV_DIM // tc,),
               [(pp, colT(CX0 // tc)), (pp, colT(CB0 // tc)), (pp, colT(CC0 // tc)),
                (W["conv_w"], pl.BlockSpec((3, tc), lambda j: (0, j))), (W["conv_b"], pl.BlockSpec((1, tc), lambda j: (0, j)))],
               [((T, CONV_DIM), BF16, colT(0), None)], "conv_fwd")
    yc = _mm(e, W["w_conv_out"], "nn", T, D_MODEL, CONV_DIM, tm=tm_lat, tn=256, tk=CONV_DIM, name="w_conv_out_fwd",
             b_spec=pl.BlockSpec((None, CONV_DIM, 256), lambda i, j, k: (j, k, 0)))

    def f_merge(ids, ga, gc, a, c):
        return (_sigmoid(ga) * a + _sigmoid(gc) * c,)

    (mrg,) = _ew(f_merge, (n_lat,), [(pp, _rows(D_MODEL, 0)), (pp, _rows(D_MODEL, 1)), (ya, _rows(D_MODEL)),
                                    (yc, _rows(D_MODEL))], [row_out(D_MODEL, BF16)], "merge_fwd")
    mo = _mm(mrg, W["w_o"], "nn", T, D_MODEL, D_MODEL, tm=tm_lat, tn=D_MODEL, tk=D_MODEL, name="w_o_fwd")

    def f_norm2(ids, x, m, gate, g, sh, sc):
        x1 = x + gate * m
        return x1, (x1 * _rms(x1) * g) * (1.0 + sc) + sh

    x1, h2 = _ew(f_norm2, (n_lat,), [(xx, _rows(D_MODEL)), (mo, _rows(D_MODEL)), (g1, vec(D_MODEL)),
                                    (W["norm2_g"], vec(D_MODEL)), (sh2, vec(D_MODEL)), (sc2, vec(D_MODEL))],
                 [row_out(D_MODEL, F32), row_out(D_MODEL, BF16)], "norm2_fwd")
    arrived, h2 = late_weights("before_ffn", h2)
    W = dict(W, **arrived)
    up = _mm(h2, W["w_up"], "nn", T, 2 * D_FF, D_MODEL, tm=tm_lat, tn=1408, tk=D_MODEL, name="w_up_fwd",
             b_spec=pl.BlockSpec((None, D_MODEL, 1408), lambda i, j, k: (j, k, 0)))

    n_ff = D_FF // tc
    ffw = lambda off, n=3: pl.BlockSpec((n, tc), lambda j: (0, j + off))

    def f_ffn(ids, ug, uv, wg, wv, bg, bv):
        gate, val = _conv(ug, wg, bg), _conv(uv, wv, bv)
        return (gate * _sigmoid(gate) * val,)

    (act,) = _ew(f_ffn, (n_ff,), [(up, colT(0)), (up, colT(n_ff)), (W["ffn_conv_w"], ffw(0)), (W["ffn_conv_w"], ffw(n_ff)),
                                 (W["ffn_conv_b"], ffw(0, 1)), (W["ffn_conv_b"], ffw(n_ff, 1))],
                 [((T, D_FF), BF16, colT(0), None)], "ffn_act_fwd")
    f = _mm(act, W["w_down"], "nn", T, D_MODEL, D_FF, tm=tm_lat, tn=D_MODEL, tk=D_FF, name="w_down_fwd")

    def f_head(ids, x1_, f_, gate, gf, t):
        x2 = x1_ + gate * f_
        r = _rms(x2)
        xn = x2 * r
        err = xn * gf - t
        loss = 0.5 * jnp.sum(jnp.mean(err * err, axis=-1, keepdims=True))
        dy = err * (1.0 / D_MODEL)
        dx2 = _rms_bwd(dy * gf, xn, r)
        return dx2, dx2 * gate, _colsum(dy * xn), _colsum(dx2 * f_), jnp.full((1, 128), loss, F32)

    dx2, df, dg_f, dg2, loss = _ew(
        f_head, (n_lat,), [(x1, _rows(D_MODEL)), (f, _rows(D_MODEL)), (g2, vec(D_MODEL)), (W["final_g"], vec(D_MODEL)),
                           (tgt, _rows(D_MODEL))],
        [row_out(D_MODEL, F32), row_out(D_MODEL, BF16), acc_out(D_MODEL), acc_out(D_MODEL), acc_out(128)], "loss_head")

    d_w_down = _mm(act, df, "tn", D_FF, D_MODEL, T, tm=1408, tn=D_MODEL, tk=T, name="w_down_dw",
                   out_dtype=BF16).reshape(4, D_FF // 4, D_MODEL)
    da = _mm(df, W["w_down"], "nt", T, D_FF, D_MODEL, tm=tm_lat, tn=1408, tk=D_MODEL, name="w_down_dx")

    tcb = 128
    n_fb = D_FF // tcb
    colb = lambda blk0: pl.BlockSpec((T, tcb), lambda j: (0, blk0 + j))
    ffwb = lambda off, n=3: pl.BlockSpec((n, tcb), lambda j: (0, j + off))
    cvec = ((1, D_FF), F32, pl.BlockSpec((1, tcb), lambda j: (0, j)), None)

    def f_ffn_bwd(ids, ug, uv, d_act, wg, wv, bg, bv):
        sg, sv = _shifts(ug), _shifts(uv)
        gate, val = _conv(ug, wg, bg, sg), _conv(uv, wv, bv, sv)
        s = _sigmoid(gate)
        d_gate = d_act * val * s * (1.0 + gate * (1.0 - s))
        d_val = d_act * gate * s
        wg0, wg1, wg2 = _conv_bwd_w(d_gate, ug, sg)
        wv0, wv1, wv2 = _conv_bwd_w(d_val, uv, sv)
        d_up = [_conv_bwd_x(d_gate, wg), _conv_bwd_x(d_val, wv)]
        return d_up, [_colsum(d_gate), _colsum(d_val), wg0, wg1, wg2, wv0, wv1, wv2]

    d_up3, ffn_stats = _ew(
        f_ffn_bwd, (n_fb,),
        [(up, colb(0)), (up, colb(n_fb)), (da, colb(0)), (W["ffn_conv_w"], ffwb(0)), (W["ffn_conv_w"], ffwb(n_fb)),
         (W["ffn_conv_b"], ffwb(0, 1)), (W["ffn_conv_b"], ffwb(n_fb, 1))],
        [((2, T, D_FF), BF16, pl.BlockSpec((2, T, tcb), lambda j: (0, 0, j)), None),
         ((n_fb, 8, 1, tcb), F32, pl.BlockSpec((None, 8, 1, tcb), lambda j: (j, 0, 0, 0)), None)], "ffn_act_bwd")
    stat = lambda s: ffn_stats[:, s, 0, :].reshape(1, D_FF)
    d_ffn_conv_b = jnp.concatenate([stat(0), stat(1)], axis=1)
    d_ffn_conv_w = jnp.concatenate([jnp.concatenate([stat(2), stat(3), stat(4)], axis=0),
                                    jnp.concatenate([stat(5), stat(6), stat(7)], axis=0)], axis=1)

    tk_t = T
    d_w_up = _mm(h2, d_up3, "tn", D_MODEL, 2 * D_FF, T, tm=D_MODEL, tn=1408, tk=tk_t, name="w_up_dw", out_dtype=BF16,
                 b_spec=pl.BlockSpec((None, tk_t, 1408), lambda i, j, k: (j // 2, k, j % 2)),
                 o_spec=pl.BlockSpec((None, D_MODEL, 1408), lambda i, j, k: (j, i, 0)), out_shape=(4, D_MODEL, 1408))
    dh2 = _mm(d_up3, W["w_up"], "nt", T, D_MODEL, 2 * D_FF, tm=tm_lat, tn=D_MODEL, tk=1408, name="w_up_dx",
              a_spec=pl.BlockSpec((None, tm_lat, 1408), lambda i, j, k: (k // 2, i, k % 2)),
              b_spec=pl.BlockSpec((None, D_MODEL, 1408), lambda i, j, k: (k, j, 0)))

    def f_norm2_bwd(ids, dx2_, dh, x1_, m, g, sc, gate):
        r = _rms(x1_)
        xn = x1_ * r
        dx1 = dx2_ + _rms_bwd(dh * g * (1.0 + sc), xn, r)
        return dx1, dx1 * gate, _colsum(dh), _colsum(dh * xn * g), _colsum(dh * xn * (1.0 + sc)), _colsum(dx1 * m)

    dx1, dmo, dsh2, dsc2, dg_n2, dg1 = _ew(
        f_norm2_bwd, (n_lat,), [(dx2, _rows(D_MODEL)), (dh2, _rows(D_MODEL)), (x1, _rows(D_MODEL)), (mo, _rows(D_MODEL)),
                                (W["norm2_g"], vec(D_MODEL)), (sc2, vec(D_MODEL)), (g1, vec(D_MODEL))],
        [row_out(D_MODEL, F32), row_out(D_MODEL, BF16)] + [acc_out(D_MODEL)] * 4, "norm2_bwd")
    d_w_o = _mm(mrg, dmo, "tn", D_MODEL, D_MODEL, T, tm=D_MODEL, tn=D_MODEL, tk=tk_t, name="w_o_dw",
                out_dtype=BF16).reshape(4, D_MODEL // 4, D_MODEL)
    dmrg = _mm(dmo, W["w_o"], "nt", T, D_MODEL, D_MODEL, tm=tm_lat, tn=D_MODEL, tk=D_MODEL, name="w_o_dx")
    dmrg = early_grads("late", {"w_o": d_w_o, "w_up": d_w_up, "w_down": d_w_down}, dmrg, split=True)

    def f_merge_bwd(ids, dm, ga, gc, a, c):
        sa, sc_ = _sigmoid(ga), _sigmoid(gc)
        return dm * sa, dm * sc_, dm * a * sa * (1.0 - sa), dm * c * sc_ * (1.0 - sc_)

    dya, dyc, dp_ga, dp_gc = _ew(
        f_merge_bwd, (n_lat,), [(dmrg, _rows(D_MODEL)), (pp, _rows(D_MODEL, 0)), (pp, _rows(D_MODEL, 1)),
                                (ya, _rows(D_MODEL)), (yc, _rows(D_MODEL))], [row_out(D_MODEL, BF16)] * 4, "merge_bwd")
    dya = early_continue("late", dya)

    d_w_ao_p = _mm(o_pad, dya, "tn", 1024, D_MODEL, T, tm=1024, tn=D_MODEL, tk=tk_t, name="w_attn_out_dw", out_dtype=BF16)
    do_pad = _mm(dya, W["w_attn_out"], "nt", T, 1024, D_MODEL, tm=tm_lat, tn=1024, tk=D_MODEL, name="w_attn_out_dx")
    d_w_co = _mm(e, dyc, "tn", CONV_DIM, D_MODEL, T, tm=CONV_DIM, tn=256, tk=tk_t, name="w_conv_out_dw", out_dtype=BF16,
                 o_spec=pl.BlockSpec((None, CONV_DIM, 256), lambda i, j, k: (j, i, 0)), out_shape=(4, CONV_DIM, 256))
    de = _mm(dyc, W["w_conv_out"], "nt", T, CONV_DIM, D_MODEL, tm=tm_lat, tn=CONV_DIM, tk=256, name="w_conv_out_dx",
             b_spec=pl.BlockSpec((None, CONV_DIM, 256), lambda i, j, k: (k, j, 0)))

    def f_conv_bwd(ids, xin, cb, cc, d_e, w, b):
        z = cc * xin
        sz = _shifts(z)
        cz = _conv(z, w, b, sz)
        dcz = d_e * cb
        w0, w1, w2 = _conv_bwd_w(dcz, z, sz)
        dz = _conv_bwd_x(dcz, w)
        return dz * cc, d_e * cz, dz * xin, _colsum(dcz), w0, w1, w2

    cvec_c = ((1, CONV_DIM), F32, pl.BlockSpec((1, tc), lambda j: (0, j)), None)
    conv_b = _ew(f_conv_bwd, (CONV_DIM // tc,),
                 [(pp, colT(CX0 // tc)), (pp, colT(CB0 // tc)), (pp, colT(CC0 // tc)), (de, colT(0)),
                  (W["conv_w"], pl.BlockSpec((3, tc), lambda j: (0, j))), (W["conv_b"], pl.BlockSpec((1, tc), lambda j: (0, j)))],
                 [((T, CONV_DIM), BF16, colT(0), None)] * 3 + [cvec_c] * 4, "conv_bwd")
    dp_cx, dp_cb, dp_cc, d_conv_b = conv_b[:4]
    d_conv_w = jnp.concatenate(conv_b[4:7], axis=0)

    dq_raw, dkv, dp_kr = _attn_bwd(q_raw, kv, pp, o_pad, do_pad, lse, tab, T, TT)

    tk_a = TT
    d_w_uq_t = _mm(nq, dq_raw, "tn", Q_RANK, 1024, T, tm=Q_RANK, tn=1024, tk=T, name="w_uq_dw", transpose_out=True)
    dnq = _mm(dq_raw, W["w_uq_t"], "nn", T, Q_RANK, 1024, tm=tm_lat, tn=Q_RANK, tk=1024, name="w_uq_dx")
    d_w_ukv = _mm(nkv, dkv, "tn", KV_RANK, 1024, TT, tm=KV_RANK, tn=256, tk=tk_a, name="w_ukv_dw", out_dtype=BF16,
                  o_spec=pl.BlockSpec((None, KV_RANK, 256), lambda i, j, k: (j, i, 0)), out_shape=(4, KV_RANK, 256))
    dnkv = _mm(dkv, W["w_ukv"], "nt", TT, KV_RANK, 1024, tm=tm_all, tn=KV_RANK, tk=256, name="w_ukv_dx",
               b_spec=pl.BlockSpec((None, KV_RANK, 256), lambda i, j, k: (k, j, 0)))
    dnkv = early_grads("mid", {
        "w_attn_out": jnp.transpose(d_w_ao_p.reshape(N_HEADS, HEAD_PAD, 4, 256)[:, 64:], (2, 0, 1, 3)).reshape(
            4, N_HEADS * 64, 256),
        "w_conv_out": d_w_co,
        "w_uq": d_w_uq_t.reshape(4, 2, HEAD_PAD, Q_RANK)[:, :, :QK_DIM].reshape(4, 2 * QK_DIM, Q_RANK).astype(BF16),
        "w_ukv": d_w_ukv}, dnkv)

    def f_lowrank_bwd(ids, ckv, cq, dkv_, dq_, gkv, gq, ga, gc, cx, cb, cc, kr):
        rk, rq = _rms(ckv), _rms(cq)
        nk, nq_ = ckv * rk, cq * rq
        lat = ids[0] < n_lat
        dq_ = jnp.where(lat, dq_, 0.0)
        pieces = [jnp.where(lat, a, jnp.zeros_like(a)) for a in (ga, gc, cx, cb, cc)]
        pieces += [_rms_bwd(dkv_ * gkv, nk, rk).astype(BF16), _rms_bwd(dq_ * gq, nq_, rq).astype(BF16), kr.astype(BF16)]
        return jnp.concatenate(pieces, axis=1), _colsum(dkv_ * nk), _colsum(dq_ * nq_)

    lat_rows = lambda n: pl.BlockSpec((ROW_TILE, n), lambda i: (jnp.minimum(i, n_lat - 1), 0))
    dpp, dg_kv, dg_q = _ew(
        f_lowrank_bwd, (n_all,), [(pp, _rows(KV_RANK, KV0 // KV_RANK)), (pp, _rows(Q_RANK, Q0 // Q_RANK)),
                                  (dnkv, _rows(KV_RANK)), (dnq, lat_rows(Q_RANK)), (W["kv_norm_g"], vec(KV_RANK)),
                                  (W["q_norm_g"], vec(Q_RANK)), (dp_ga, lat_rows(D_MODEL)), (dp_gc, lat_rows(D_MODEL)),
                                  (dp_cx, lat_rows(CONV_DIM)), (dp_cb, lat_rows(CONV_DIM)), (dp_cc, lat_rows(CONV_DIM)),
                                  (dp_kr, _rows(HEAD_PAD))],
        [row_out(P_COLS, BF16, TT), acc_out(KV_RANK), acc_out(Q_RANK)], "lowrank_norm_bwd")
    d_w_in_t = _mm(hh, dpp, "tn", D_MODEL, P_COLS, TT, tm=512, tn=2176, tk=TT, name="w_in_dw", out_dtype=BF16,
                   transpose_out=True)
    dhh = _mm(dpp, W["w_in_t"], "nn", TT, D_MODEL, P_COLS, tm=tm_all, tn=512, tk=2176, name="w_in_dx")

    def f_norm1_bwd(ids, x, dh, dres, g, sc):
        r = _rms(x)
        xn = x * r
        return (dres + _rms_bwd(dh * g * (1.0 + sc), xn, r), _colsum(dh), _colsum(dh * xn * g),
                _colsum(dh * xn * (1.0 + sc)))

    grad_x, dsh1, dsc1, dg_n1 = _ew(
        f_norm1_bwd, (n_lat,), [(xx, _rows(D_MODEL)), (dhh, _rows(D_MODEL)), (dx1, _rows(D_MODEL)),
                                (W["norm1_g"], vec(D_MODEL)), (sc1, vec(D_MODEL))],
        [row_out(D_MODEL, F32)] + [acc_out(D_MODEL)] * 3, "norm1_bwd")

    def f_norm1_ctx_bwd(ids, x, dh, g, sc):
        xn = x * _rms(x)
        return _colsum(dh), _colsum(dh * xn * g), _colsum(dh * xn * (1.0 + sc))

    n_ctx = n_all - n_lat
    dcsh1, dcsc1, dg_n1c = _ew(
        f_norm1_ctx_bwd, (n_ctx,), [(xx, _rows(D_MODEL, 0, n_lat)), (dhh, _rows(D_MODEL, 0, n_lat)),
                                    (W["norm1_g"], vec(D_MODEL)), (csc1, vec(D_MODEL))], [acc_out(D_MODEL)] * 3,
        "norm1_ctx_bwd")

    big = {"w_in": _w_in_t_shards_from_p(d_w_in_t).astype(BF16)}
    zero = jnp.zeros((1, 4 * D_MODEL), F32)
    small = {
        "dmod_lat": jnp.concatenate([dsh1, dsc1, dg1, dsh2, dsc2, dg2], axis=1),
        "dmod_ctx": jnp.concatenate([dcsh1, dcsc1, zero], axis=1),
        "norm1_g": dg_n1 + dg_n1c, "norm2_g": dg_n2, "final_g": dg_f, "q_norm_g": dg_q, "kv_norm_g": dg_kv,
        "conv_b": d_conv_b, "conv_w": d_conv_w.reshape(1, -1), "ffn_conv_b": d_ffn_conv_b,
        "ffn_conv_w": d_ffn_conv_w.reshape(1, -1),
    }
    return grad_x, loss, big, small


SMALL = (("dmod_lat", 6144), ("dmod_ctx", 6144), ("norm1_g", 1024), ("norm2_g", 1024), ("final_g", 1024),
         ("q_norm_g", 384), ("kv_norm_g", 256), ("conv_b", 512), ("conv_w", 1536), ("ffn_conv_b", 5632),
         ("ffn_conv_w", 16896), ("loss", 128))
SMALL_ROWS = 320


def _adam_update(w, g, m, v):
    c1, c2 = 1.0 - ADAM_B1 ** ADAM_STEP, 1.0 - ADAM_B2 ** ADAM_STEP
    m2 = ADAM_B1 * m + (1.0 - ADAM_B1) * g
    v2 = ADAM_B2 * v + (1.0 - ADAM_B2) * (g * g)
    return [-ADAM_LR * ((m2 / c1) / (jnp.sqrt(v2 / c2) + ADAM_EPS) + ADAM_WD * w), m2, v2]


def _adamw(w, g, m, v, name):
    R, C = w.shape
    tr = 8 if R % 8 == 0 else R
    for t in range(8, R + 1, 8):
        if R % t == 0 and t * C * 4 <= (1 << 20):
            tr = t
    spec = pl.BlockSpec((tr, C), lambda i: (i, 0))
    return _ew(lambda ids, *vals: _adam_update(*vals), (R // tr,), [(w, spec), (g, spec), (m, spec), (v, spec)],
               [((R, C), F32, spec, None)] * 3, name)


def kernel(x, c, ctx, c_ctx, w_ada, b_ada, norm1_g, w_in, q_norm_g, kv_norm_g, w_uq, w_ukv, conv_w, conv_b, w_attn_out, w_conv_out, w_o, norm2_g, w_up, ffn_conv_w, ffn_conv_b, w_down, final_g, loss_target, m_c_ctx, m_w_ada, m_b_ada, m_norm1_g, m_w_in, m_q_norm_g, m_kv_norm_g, m_w_uq, m_w_ukv, m_conv_w, m_conv_b, m_w_attn_out, m_w_conv_out, m_w_o, m_norm2_g, m_w_up, m_ffn_conv_w, m_ffn_conv_b, m_w_down, m_final_g, v_c_ctx, v_w_ada, v_b_ada, v_norm1_g, v_w_in, v_q_norm_g, v_kv_norm_g, v_w_uq, v_w_ukv, v_conv_w, v_conv_b, v_w_attn_out, v_w_conv_out, v_w_o, v_norm2_g, v_w_up, v_ffn_conv_w, v_ffn_conv_b, v_w_down, v_final_g):
    mx, my, mc = lax.axis_index("x"), lax.axis_index("y"), lax.axis_index("c")
    chip = 2 * mx + my
    dev = 4 * mx + 2 * my + mc
    T, Tc = x.shape[1], ctx.shape[1]
    TT = T + Tc
    w_in_t, m_w_in_t, v_w_in_t = (jnp.transpose(a[0]) for a in (w_in, m_w_in, v_w_in))
    w_uq_t, m_w_uq_t, v_w_uq_t = (jnp.transpose(a[0]) for a in (w_uq, m_w_uq, v_w_uq))
    conv_sh = jnp.concatenate([conv_w[0], ffn_conv_w[0]], axis=1)
    pay1 = jnp.concatenate([jnp.pad(c, ((0, 7), (0, 0))), jnp.pad(conv_sh, ((0, 5), (0, 0)))], axis=1)
    c_send, c_recv, c_src, c_land, zero0 = _ici_start("all", [pay1], [(8, 8, 2560)], jnp.zeros((8, 128), F32),
                                                      "cond_start")
    w_in_bf = (jnp.pad(w_in_t, ((0, W_IN_SHARD_PAD - W_IN_SHARD), (0, 0))) + zero0[0, 0]).astype(BF16)
    shards = {"w_in": w_in_bf, "w_uq": w_uq_t, "w_ukv": w_ukv[0], "w_attn_out": w_attn_out[0],
              "w_conv_out": w_conv_out[0], "w_o": w_o[0], "w_up": w_up[0], "w_down": w_down[0]}
    (pay1,), (c_land,) = _ici_wait("all", c_send, c_recv, c_src, c_land, shards["w_in"], "cond_wait")
    got1 = lax.dynamic_update_slice(c_land, pay1[None], (dev, 0, 0))
    c_all = got1[:, 0, :D_MODEL]
    conv_all = got1[0::2, :3, D_MODEL:]
    conv_w_full = _cols_from_shards(conv_all[:, :, :128])
    ffn_conv_w_full = _cols_from_shards(conv_all[:, :, 128:])

    cond = jnp.concatenate([c_all, c_ctx.reshape(1, D_MODEL), jnp.zeros((7, D_MODEL), F32)], axis=0)

    def f_silu(ids, v):
        return (v * _sigmoid(v),)

    (s16,) = _ew(f_silu, (1,), [(cond, _full((16, D_MODEL)))], [((16, D_MODEL), F32, _full((16, D_MODEL)), None)], "silu_cond")
    mod_sh = _mm(s16, w_ada[0], "nn", 16, 1536, D_MODEL, tm=16, tn=768, tk=D_MODEL, name="w_ada_fwd")
    m_send, m_recv, m_src, m_land, zero1 = _ici_start("all", [mod_sh], [(8, 16, 1536)], jnp.zeros((8, 128), F32),
                                                      "mod_start")
    shards["w_ukv"] = w_ukv[0] + zero1[0, 0]

    names = [n for n, _ in BIG]
    first = [n for n in names if n not in GATHER_LATE]
    gathered, zero = _gather_weights([shards[n].astype(BF16) for n in first])
    full = dict(zip(first, gathered))
    (mod_mine,), (m_land,) = _ici_wait("all", m_send, m_recv, m_src, m_land, gathered[0], "mod_wait")
    got2 = lax.dynamic_update_slice(m_land, mod_mine[None], (dev, 0, 0))
    mod_all = _cols_from_shards(got2[0::2]) + b_ada
    mod_lat = lax.dynamic_slice_in_dim(mod_all, dev, 1, axis=0)
    mod_ctx = mod_all[8:9]
    xx = jnp.concatenate([x[0], ctx[0]], axis=0)
    late_groups = {"g1": ("w_attn_out", "w_conv_out", "w_o"), "g2": ("w_up", "w_down")}
    flight = {}
    for tag, group in late_groups.items():
        bf = [(shards[n] + zero[0, 0]).astype(BF16) for n in group]
        flight[tag] = _ici_start("gather", bf, [(4,) + s.shape for s in bf], xx, "gather_" + tag + "_start")
        xx = flight[tag][4]

    def chip_stage_done(tag, x):
        send, recv, src, land, _ = flight[tag]
        src, land = _ici_wait("gather", send, recv, src, land, x, "gather_" + tag + "_wait")
        flight[tag] = _ici_start("finish", src, None, x, "finish_" + tag + "_start", lands=land)
        return flight[tag][4]

    def arrived(tag, x):
        send, recv, src, land, _ = flight[tag]
        return dict(zip(late_groups[tag], _ici_wait("finish", send, recv, src, land, x, "finish_" + tag + "_wait")[1]))

    def late_weights(point, x):
        if point == "before_attn":
            return {}, chip_stage_done("g1", x)
        if point == "after_attn":
            got = arrived("g1", x)
            wao = _cols_from_shards(got["w_attn_out"]).reshape(N_HEADS, 64, D_MODEL)
            ready = {"w_attn_out": jnp.pad(wao, ((0, 0), (64, 0), (0, 0))).reshape(N_HEADS * HEAD_PAD, D_MODEL),
                     "w_conv_out": got["w_conv_out"], "w_o": got["w_o"].reshape(D_MODEL, D_MODEL)}
            return ready, chip_stage_done("g2", x)
        got = arrived("g2", x)
        return {"w_up": got["w_up"], "w_down": got["w_down"].reshape(D_FF, D_MODEL)}, x

    wuq_t = full["w_uq"].reshape(N_HEADS, QK_DIM, Q_RANK)
    W = {
        "w_in_t": _w_in_t_p_from_shards(full["w_in"]),
        "w_uq_t": jnp.pad(wuq_t, ((0, 0), (0, HEAD_PAD - QK_DIM), (0, 0))).reshape(N_HEADS * HEAD_PAD, Q_RANK),
        "w_ukv": full["w_ukv"],
        "norm1_g": norm1_g, "norm2_g": norm2_g, "final_g": final_g.reshape(1, D_MODEL), "q_norm_g": q_norm_g,
        "kv_norm_g": kv_norm_g, "conv_w": conv_w_full, "conv_b": conv_b, "ffn_conv_w": ffn_conv_w_full,
        "ffn_conv_b": ffn_conv_b,
    }

    place = jnp.stack([chip, mc]).astype(jnp.int32)
    early = {}

    pending = {}

    def scatter(tag, group, gs, from_sib, carry):
        if tag == "mid":
            sums = _add_pair_many(gs, from_sib, place, "rs_pair_add_mid")
        else:
            sums = [_add_pair(gs[w], from_sib[w], place, "rs_pair_add_" + n) for w, n in enumerate(group)]
        send, recv, sums, land, carry = _ici_start(
            "scatter", sums, [(3,) + s.shape[1:] for s in sums], carry, "rs_chips_" + tag + "_start")
        early[tag] = (group, send, recv, sums, land)
        return carry

    def early_grads(tag, g, carry, split=False):
        gs = list(g.values())
        if not split:
            return scatter(tag, list(g), gs, _rs_pair(gs, "rs_pair_" + tag), carry)
        send, recv, gs, land, carry = _ici_start(
            "pair", gs, [(4, s.shape[1] // 2, s.shape[2]) for s in gs], carry, "rs_pair_" + tag + "_start")
        pending[tag] = (list(g), send, recv, gs, land)
        return carry

    def early_continue(tag, carry):
        group, send, recv, gs, land = pending[tag]
        gs, from_sib = _ici_wait("pair", send, recv, gs, land, carry, "rs_pair_" + tag + "_wait")
        return scatter(tag, group, gs, from_sib, carry)

    grad_x, loss_part, gbig, gsmall = _local_step(xx, loss_target[0], mod_lat, mod_ctx, W, late_weights, early_grads,
                                                  early_continue)

    gsmall["loss"] = loss_part
    pay3 = jnp.concatenate([gsmall[n].reshape(-1) for n, _ in SMALL])
    pay3 = jnp.pad(pay3, (0, SMALL_ROWS * 128 - pay3.shape[0])).reshape(SMALL_ROWS, 128)
    s_send, s_recv, s_src, s_land, w_in_thru = _ici_start("all", [pay3], [(8, SMALL_ROWS, 128)], gbig["w_in"],
                                                         "small_start")
    gbig = {"w_in": w_in_thru}

    after_small = early_grads("last", gbig, s_src[0])

    (pay3,), (s_land,) = _ici_wait("all", s_send, s_recv, [after_small], s_land, early["last"][3][0], "small_wait")
    got3 = lax.dynamic_update_slice(s_land, pay3[None], (dev, 0, 0)).reshape(8 * SMALL_ROWS, 128)

    def f_sum8(ids, a):
        s = a[0:SMALL_ROWS]
        for d in range(1, 8):
            s = s + a[d * SMALL_ROWS:(d + 1) * SMALL_ROWS]
        return (s,)

    (vsum,) = _ew(f_sum8, (1,), [(got3, _full((8 * SMALL_ROWS, 128)))],
                  [((SMALL_ROWS, 128), F32, _full((SMALL_ROWS, 128)), None)], "sum_small")
    vflat = vsum.reshape(-1)
    gvec, off = {}, 0
    for n, size in SMALL:
        gvec[n] = vflat[off:off + size]
        off += size
    loss = gvec["loss"][0]
    dmod_rows = got3.reshape(8, SMALL_ROWS * 128)[:, :6 * D_MODEL]
    dm16 = jnp.concatenate([dmod_rows, gvec["dmod_ctx"].reshape(1, -1), jnp.zeros((7, 6 * D_MODEL), F32)], axis=0)

    def f_colsum(ids, a):
        return (_colsum(a),)

    (g_b_ada,) = _ew(f_colsum, (1,), [(dm16, _full((16, 6 * D_MODEL)))],
                     [((1, 6 * D_MODEL), F32, _full((1, 6 * D_MODEL)), None)], "b_ada_grad")
    dm_sh = lax.dynamic_slice_in_dim(dm16, chip * 1536, 1536, axis=1)
    g_w_ada = _mm(s16, dm_sh, "tn", D_MODEL, 1536, 16, tm=512, tn=768, tk=16, name="w_ada_dw")
    dcond_part = _mm(dm_sh, w_ada[0], "nt", 16, D_MODEL, 1536, tm=16, tn=512, tk=1536, name="w_ada_dx")
    d_send, d_recv, d_src, d_land, vsum = _ici_start("all", [dcond_part[8:16]], [(8, 8, D_MODEL)], vsum, "dcond_start")

    def finish_start(tags, after):
        done, halves = [], []
        for tag in tags:
            tag_names, send, recv, sums, land = early[tag]
            sums, land = _ici_wait("scatter", send, recv, sums, land, after, "rs_chips_" + tag + "_wait")
            done += tag_names
            if tag == "mid":
                halves += _add_chips_many(sums, land, place, "rs_chip_add_mid")
            else:
                halves += [_add_chips(a, b, place, "rs_chip_add_" + n) for a, b, n in zip(sums, land, tag_names)]
        send, recv, _, halves, _ = _ici_start("back", [], None, jnp.zeros((8, 128), F32), "rs_back_" + tags[0] + "_start",
                                              lands=halves)
        return done, send, recv, halves

    def finish_wait(state, after):
        done, send, recv, halves = state
        return dict(zip(done, _ici_wait("back", send, recv, [], halves, after, "rs_back_" + done[0] + "_wait")[1]))

    grads, deltas, new_m, new_v = {}, {}, {}, {}

    raw = {}

    def adam(n, w_, m_, v_, g, transposed):
        d_, m2, v2 = _adamw(w_, g, m_, v_, "adamw_" + n)
        raw[n] = d_
        back = (lambda a: jnp.transpose(a)[None]) if transposed else (lambda a: a[None])
        grads[n], deltas[n], new_m[n], new_v[n] = back(g[:w_.shape[0]]), back(d_), back(m2), back(v2)

    pending_back = finish_start(["late", "mid"], grad_x)
    adam("w_ada", w_ada[0], m_w_ada[0], v_w_ada[0], g_w_ada, False)
    gw = finish_wait(pending_back, raw["w_ada"])
    for n, (w_, m_, v_) in {"w_o": (w_o, m_w_o, v_w_o), "w_up": (w_up, m_w_up, v_w_up),
                            "w_down": (w_down, m_w_down, v_w_down)}.items():
        adam(n, w_[0], m_[0], v_[0], gw[n], False)
    pending_back = finish_start(["last"], raw["w_up"])

    (dcond_mine,), (d_land,) = _ici_wait("all", d_send, d_recv, d_src, d_land, raw["w_down"], "dcond_wait")
    got4 = lax.dynamic_update_slice(d_land, dcond_mine[None], (dev, 0, 0))[0::2, 0]

    def f_c_ctx(ids, parts, cc):
        s = _sigmoid(cc)
        d = parts[0:1] + parts[1:2] + parts[2:3] + parts[3:4]
        return (d * s * (1.0 + cc * (1.0 - s)),)

    (g_c_ctx,) = _ew(f_c_ctx, (1,), [(got4, _full((4, D_MODEL))), (c_ctx.reshape(1, D_MODEL), _full((1, D_MODEL)))],
                     [((1, D_MODEL), F32, _full((1, D_MODEL)), None)], "c_ctx_grad")

    conv_w_g = lax.dynamic_slice_in_dim(gvec["conv_w"].reshape(3, CONV_DIM), chip * 128, 128, axis=1)
    ffn_conv_w_g = lax.dynamic_slice_in_dim(gvec["ffn_conv_w"].reshape(3, 2 * D_FF), chip * 1408, 1408, axis=1)
    vec_params = (("c_ctx", c_ctx, m_c_ctx, v_c_ctx, g_c_ctx), ("b_ada", b_ada, m_b_ada, v_b_ada, g_b_ada),
                  ("norm1_g", norm1_g, m_norm1_g, v_norm1_g, gvec["norm1_g"]),
                  ("q_norm_g", q_norm_g, m_q_norm_g, v_q_norm_g, gvec["q_norm_g"]),
                  ("kv_norm_g", kv_norm_g, m_kv_norm_g, v_kv_norm_g, gvec["kv_norm_g"]),
                  ("conv_w", conv_w, m_conv_w, v_conv_w, conv_w_g), ("conv_b", conv_b, m_conv_b, v_conv_b, gvec["conv_b"]),
                  ("norm2_g", norm2_g, m_norm2_g, v_norm2_g, gvec["norm2_g"]),
                  ("ffn_conv_w", ffn_conv_w, m_ffn_conv_w, v_ffn_conv_w, ffn_conv_w_g),
                  ("ffn_conv_b", ffn_conv_b, m_ffn_conv_b, v_ffn_conv_b, gvec["ffn_conv_b"]),
                  ("final_g", final_g, m_final_g, v_final_g, gvec["final_g"]))
    two_d = lambda a: a.reshape((-1, a.shape[-1]))
    many = [p + ((lambda r, s=p[1].shape: r.reshape(s)),) for p in vec_params]
    for n, w_, m_, v_ in (("w_ukv", w_ukv, m_w_ukv, v_w_ukv), ("w_attn_out", w_attn_out, m_w_attn_out, v_w_attn_out),
                          ("w_conv_out", w_conv_out, m_w_conv_out, v_w_conv_out)):
        many.append((n, w_, m_, v_, gw[n], (lambda r, s=w_.shape: r.reshape(s))))
    many.append(("w_uq", w_uq_t, m_w_uq_t, v_w_uq_t, gw["w_uq"], lambda r: jnp.transpose(r)[None]))

    def f_adam_many(ids, *vals):
        out = []
        for k in range(len(many)):
            out += _adam_update(*vals[4 * k:4 * k + 4])
        return out

    ins_v, outs_v = [], []
    for p in many:
        shp = two_d(p[1]).shape
        ins_v += [(two_d(a), _full(shp)) for a in (p[1], p[4], p[2], p[3])]
        outs_v += [(shp, F32, _full(shp), None)] * 3
    res_v = _ew(f_adam_many, (1,), ins_v, outs_v, "adamw_small")
    for k, p in enumerate(many):
        n, post = p[0], p[5]
        grads[n] = post(two_d(p[4]))
        deltas[n], new_m[n], new_v[n] = (post(r) for r in res_v[3 * k:3 * k + 3])

    gw_in = finish_wait(pending_back, res_v[0])
    adam("w_in", w_in_t, m_w_in_t, v_w_in_t, gw_in["w_in"], True)

    order = ("c_ctx", "w_ada", "b_ada", "norm1_g", "w_in", "q_norm_g", "kv_norm_g", "w_uq", "w_ukv", "conv_w", "conv_b",
             "w_attn_out", "w_conv_out", "w_o", "norm2_g", "w_up", "ffn_conv_w", "ffn_conv_b", "w_down", "final_g")
    return (loss, grad_x[None], *[grads[n] for n in order], *[deltas[n] for n in order],
            *[new_m[n] for n in order], *[new_v[n] for n in order])
```

```python
import functools

import jax
import jax.numpy as jnp
import numpy as np
from jax import lax
from jax.experimental import pallas as pl
from jax.experimental.pallas import tpu as pltpu

F32, BF16 = jnp.float32, jnp.bfloat16
MESH = pl.DeviceIdType.MESH

D_MODEL = 1024
N_HEADS = 8
HEAD_PAD = 128
QK_DIM = 96
Q_RANK, KV_RANK = 384, 256
CONV_DIM = 512
D_FF = 2816
GRID_W = 64
ROPE_THETA = 10000.0
EPS = 1e-6
GA0, GC0, CX0, CB0, CC0, KV0, Q0, KR0, P_COLS = 0, 1024, 2048, 2560, 3072, 3584, 3840, 4224, 4352
ROW_TILE = 256
VMEM_LIMIT_BYTES = 48 * 1024 * 1024

ADAM_LR, ADAM_B1, ADAM_B2, ADAM_EPS, ADAM_WD, ADAM_STEP = 0.001, 0.9, 0.999, 1e-08, 0.01, 10

BIG = (("w_in", (1088, 1024)), ("w_uq", (192, 384)), ("w_ukv", (256, 256)), ("w_attn_out", (512, 256)),
       ("w_conv_out", (512, 256)), ("w_o", (256, 1024)), ("w_up", (1024, 1408)), ("w_down", (704, 1024)))

GATHER_LATE = ("w_attn_out", "w_conv_out", "w_o", "w_up", "w_down")

NN = (((1,), (0,)), ((), ()))
NT = (((1,), (1,)), ((), ()))
TN = (((0,), (0,)), ((), ()))


def _cp(sem):
    return pltpu.CompilerParams(dimension_semantics=sem, vmem_limit_bytes=VMEM_LIMIT_BYTES)


PIN_BYTES = 1 << 19


def _in_hbm(arrays):
    return [pltpu.with_memory_space_constraint(a, pltpu.HBM) if a.size * a.dtype.itemsize >= PIN_BYTES else a
            for a in arrays]


def _out(shape, dtype):
    n = 1
    for d in shape:
        n *= d
    big = n * jnp.dtype(dtype).itemsize >= PIN_BYTES
    return pltpu.HBM(shape, dtype) if big else jax.ShapeDtypeStruct(shape, dtype)


def _pick(n, prefs):
    for p in prefs:
        if n % p == 0:
            return p
    return n


def _mm(a, b, mode, M, N, K, *, tm, tn, tk, name, out_dtype=F32, a_spec=None, b_spec=None, o_spec=None,
        out_shape=None, transpose_out=False):
    assert M % tm == 0 and N % tn == 0 and K % tk == 0, (name, M, N, K, tm, tn, tk)
    nk = K // tk
    dims = {"nn": NN, "nt": NT, "tn": TN}[mode]
    if a_spec is None:
        a_spec = (pl.BlockSpec((tk, tm), lambda i, j, k: (k, i)) if mode == "tn"
                  else pl.BlockSpec((tm, tk), lambda i, j, k: (i, k)))
    if b_spec is None:
        b_spec = (pl.BlockSpec((tn, tk), lambda i, j, k: (j, k)) if mode == "nt"
                  else pl.BlockSpec((tk, tn), lambda i, j, k: (k, j)))
    if o_spec is None:
        o_spec = (pl.BlockSpec((tn, tm), lambda i, j, k: (j, i)) if transpose_out
                  else pl.BlockSpec((tm, tn), lambda i, j, k: (i, j)))
    if out_shape is None:
        out_shape = (N, M) if transpose_out else (M, N)

    def emit(o_ref, val):
        o_ref[...] = (val.T if transpose_out else val).astype(o_ref.dtype)

    def body(a_ref, b_ref, o_ref, *scratch):
        part = lax.dot_general(a_ref[...].astype(BF16), b_ref[...].astype(BF16), dims, preferred_element_type=F32)
        if nk == 1:
            emit(o_ref, part)
            return
        acc_ref, = scratch
        k = pl.program_id(2)

        @pl.when(k == 0)
        def _():
            acc_ref[...] = part

        @pl.when((k > 0) & (k < nk - 1))
        def _():
            acc_ref[...] += part

        @pl.when(k == nk - 1)
        def _():
            emit(o_ref, acc_ref[...] + part)

    return pl.pallas_call(
        body, grid=(M // tm, N // tn, nk), in_specs=[a_spec, b_spec], out_specs=o_spec,
        out_shape=_out(out_shape, out_dtype),
        scratch_shapes=[pltpu.VMEM((tm, tn), F32)] if nk > 1 else [],
        compiler_params=_cp(("parallel", "parallel", "arbitrary")), name=name)(*_in_hbm([a, b]))


def _ew(fn, grid, ins, outs, name, scalars=None):
    n_in = len(ins)
    n_sc = 0 if scalars is None else 1

    def store(ref, val, acc, ids):
        if isinstance(val, (list, tuple)):
            for h, v in enumerate(val):
                ref[h] = v.astype(ref.dtype)
            return
        if acc is None:
            ref[...] = val.astype(ref.dtype)
            return

        @pl.when(ids[acc] == 0)
        def _():
            ref[...] = val.astype(ref.dtype)

        @pl.when(ids[acc] > 0)
        def _():
            ref[...] += val.astype(ref.dtype)

    def body(*refs):
        refs = refs[n_sc:]
        ids = tuple(pl.program_id(a) for a in range(len(grid)))
        vals = fn(ids, *[r[...] for r in refs[:n_in]])
        for ref, val, (_, _, _, acc) in zip(refs[n_in:], vals, outs):
            store(ref, val, acc, ids)

    acc_axes = {o[3] for o in outs if o[3] is not None}
    sem = tuple("arbitrary" if a in acc_axes else "parallel" for a in range(len(grid)))
    in_specs, out_specs = [s for _, s in ins], [o[2] for o in outs]
    out_shape = [_out(o[0], o[1]) for o in outs]
    args = _in_hbm([a for a, _ in ins])
    if scalars is None:
        return pl.pallas_call(body, grid=grid, in_specs=in_specs, out_specs=out_specs, out_shape=out_shape,
                              compiler_params=_cp(sem), name=name)(*args)
    spec = pltpu.PrefetchScalarGridSpec(num_scalar_prefetch=1, grid=grid, in_specs=in_specs, out_specs=out_specs)
    return pl.pallas_call(body, grid_spec=spec, out_shape=out_shape, compiler_params=_cp(sem), name=name)(scalars, *args)


def _rows(width, cblk=0, roff=0, tr=ROW_TILE):
    return pl.BlockSpec((tr, width), lambda i: (i + roff, cblk))


def _full(shape):
    nd = len(shape)
    return pl.BlockSpec(shape, lambda *_: (0,) * nd)


def _sigmoid(x):
    return 1.0 / (1.0 + jnp.exp2(x * (-1.4426950408889634)))


def _rms(x):
    return lax.rsqrt(jnp.mean(x * x, axis=-1, keepdims=True) + EPS)


def _rms_bwd(dn, xn, r):
    return r * (dn - xn * jnp.mean(dn * xn, axis=-1, keepdims=True))


def _colsum(x):
    return jnp.sum(x, axis=0, keepdims=True)


def _shifts(x):
    n = x.shape[0]
    rows = lax.broadcasted_iota(jnp.int32, x.shape, 0)
    return jnp.where(rows == 0, 0.0, pltpu.roll(x, 1, 0)), jnp.where(rows == n - 1, 0.0, pltpu.roll(x, n - 1, 0))


def _conv(x, w, b, shifted=None):
    prev, nxt = _shifts(x) if shifted is None else shifted
    return b + prev * w[0:1] + x * w[1:2] + nxt * w[2:3]


def _conv_bwd_x(dy, w):
    prev, nxt = _shifts(dy)
    return nxt * w[0:1] + dy * w[1:2] + prev * w[2:3]


def _conv_bwd_w(dy, x, shifted):
    prev, nxt = shifted
    return _colsum(dy * prev), _colsum(dy * x), _colsum(dy * nxt)


def _rope(x, cos, sin_lo, sin_hi):
    return x * cos + pltpu.roll(x, HEAD_PAD - 8, 1) * sin_lo + pltpu.roll(x, 8, 1) * sin_hi


ATTN_SCALE = QK_DIM ** -0.5
LOG2_E = 1.4426950408889634


def _rope_t(x, tab, inverse=False):
    o = 3 * HEAD_PAD if inverse else 0
    return _rope(x, tab[:, o:o + HEAD_PAD], tab[:, o + HEAD_PAD:o + 2 * HEAD_PAD], tab[:, o + 2 * HEAD_PAD:o + 3 * HEAD_PAD])


def _head_keys(kv_ref, kr_ref, tab_ref, kc_ref, vp_ref):
    kv = kv_ref[...]
    lane = lax.broadcasted_iota(jnp.int32, kv.shape, 1)
    kc_ref[...] = jnp.where(lane < 64, kv, _rope_t(kr_ref[...], tab_ref[...])).astype(BF16)
    vp_ref[...] = jnp.where(lane >= 64, kv, 0.0).astype(BF16)


ATTN_Q_TILE = 512


def _attn_specs(tq, TT):
    q = pl.BlockSpec((tq, HEAD_PAD), lambda h, i: (i, h))
    keys = pl.BlockSpec((TT, HEAD_PAD), lambda h, i: (0, h))
    kr = pl.BlockSpec((TT, HEAD_PAD), lambda h, i: (0, KR0 // HEAD_PAD))
    tab_q = pl.BlockSpec((tq, 6 * HEAD_PAD), lambda h, i: (i, 0))
    tab_k = pl.BlockSpec((TT, 6 * HEAD_PAD), lambda h, i: (0, 0))
    lse = pl.BlockSpec((None, tq, 1), lambda h, i: (h, i, 0))
    return q, keys, kr, tab_q, tab_k, lse


def _attn_fwd(q_raw, kv, pp, tab, T, TT):
    tq = ROW_TILE

    def body(q_ref, kv_ref, kr_ref, tq_ref, tk_ref, o_ref, l_ref, kc, vp):
        @pl.when(pl.program_id(1) == 0)
        def _():
            _head_keys(kv_ref, kr_ref, tk_ref, kc, vp)

        q = _rope_t(q_ref[...], tq_ref[...]).astype(BF16)
        s = lax.dot_general(q, kc[...], NT, preferred_element_type=F32)
        m = jnp.max(s, axis=-1, keepdims=True)
        p = jnp.exp2((s - m) * (ATTN_SCALE * LOG2_E))
        l = jnp.sum(p, axis=-1, keepdims=True)
        o = lax.dot_general(p.astype(BF16), vp[...], NN, preferred_element_type=F32)
        o_ref[...] = o / l
        l_ref[...] = m * ATTN_SCALE + jnp.log(l)

    qs, keys, kr, _, _, lse = _attn_specs(tq, TT)
    tab_q = pl.BlockSpec((tq, 3 * HEAD_PAD), lambda h, i: (i, 0))
    tab_k = pl.BlockSpec((TT, 3 * HEAD_PAD), lambda h, i: (0, 0))
    return pl.pallas_call(
        body, grid=(N_HEADS, T // tq), in_specs=[qs, keys, kr, tab_q, tab_k],
        out_specs=[qs, lse],
        out_shape=[jax.ShapeDtypeStruct((T, N_HEADS * HEAD_PAD), F32), jax.ShapeDtypeStruct((N_HEADS, T, 1), F32)],
        scratch_shapes=[pltpu.VMEM((TT, HEAD_PAD), BF16), pltpu.VMEM((TT, HEAD_PAD), BF16)],
        compiler_params=_cp(("parallel", "arbitrary")), name="attn_fwd",
    )(*_in_hbm([q_raw, kv, pp, tab, tab]))


def _attn_bwd(q_raw, kv, pp, o, do, lse, tab, T, TT):
    tq = _pick(T, (ATTN_Q_TILE, ROW_TILE))
    nq = T // tq

    def body(q_ref, kv_ref, kr_ref, tq_ref, tk_ref, o_ref, do_ref, l_ref, dq_ref, dkv_ref, dkr_ref, kc, vp, dk, dv):
        h, i = pl.program_id(0), pl.program_id(1)

        @pl.when(i == 0)
        def _():
            _head_keys(kv_ref, kr_ref, tk_ref, kc, vp)
            dk[...] = jnp.zeros_like(dk)
            dv[...] = jnp.zeros_like(dv)

        q = _rope_t(q_ref[...], tq_ref[...]).astype(BF16)
        k, v, d_o = kc[...], vp[...], do_ref[...]
        s = lax.dot_general(q, k, NT, preferred_element_type=F32)
        p = jnp.exp2(s * (ATTN_SCALE * LOG2_E) - l_ref[...] * LOG2_E)
        dob = d_o.astype(BF16)
        dp = lax.dot_general(dob, v, NT, preferred_element_type=F32)
        dd = jnp.sum(d_o * o_ref[...], axis=-1, keepdims=True)
        ds = (p * (dp - dd) * ATTN_SCALE).astype(BF16)
        dq = lax.dot_general(ds, k, NN, preferred_element_type=F32)
        dq_ref[...] = _rope_t(dq, tq_ref[...], inverse=True).astype(dq_ref.dtype)
        dk[...] += lax.dot_general(q, ds, TN, preferred_element_type=F32)
        dv[...] += lax.dot_general(dob, p.astype(BF16), TN, preferred_element_type=F32)

        @pl.when(i == nq - 1)
        def _():
            dkh = dk[...].T
            lane = lax.broadcasted_iota(jnp.int32, dkh.shape, 1)
            dkv_ref[...] = jnp.where(lane < 64, dkh, dv[...].T).astype(dkv_ref.dtype)
            rot = _rope_t(jnp.where((lane >= 64) & (lane < 96), dkh, 0.0), tk_ref[...], inverse=True)

            @pl.when(h == 0)
            def _():
                dkr_ref[...] = rot

            @pl.when(h > 0)
            def _():
                dkr_ref[...] += rot

    qs, keys, kr, tab_q, tab_k, lse_spec = _attn_specs(tq, TT)
    wide = lambda rows: jax.ShapeDtypeStruct((rows, N_HEADS * HEAD_PAD), BF16)
    return pl.pallas_call(
        body, grid=(N_HEADS, nq),
        in_specs=[qs, keys, kr, tab_q, tab_k, qs, qs, lse_spec],
        out_specs=[qs, keys, pl.BlockSpec((TT, HEAD_PAD), lambda h, i: (0, 0))],
        out_shape=[wide(T), wide(TT), jax.ShapeDtypeStruct((TT, HEAD_PAD), F32)],
        scratch_shapes=[pltpu.VMEM((TT, HEAD_PAD), BF16), pltpu.VMEM((TT, HEAD_PAD), BF16),
                        pltpu.VMEM((HEAD_PAD, TT), F32), pltpu.VMEM((HEAD_PAD, TT), F32)],
        compiler_params=_cp(("arbitrary", "arbitrary")), name="attn_bwd",
    )(*_in_hbm([q_raw, kv, pp, tab, tab, o, do, lse]))


def _hbm_specs(n):
    return [pl.BlockSpec(memory_space=pl.ANY)] * n


def _gather_weights(shards):
    n = len(shards)
    halves = [s.shape[0] // 2 for s in shards]

    def body(*refs):
        ins, outs = refs[:n], refs[n:2 * n]
        token, send_sems, recv_sems = refs[2 * n:]
        token[...] = jnp.zeros_like(token)
        mx, my, mc = lax.axis_index("x"), lax.axis_index("y"), lax.axis_index("c")
        j_me = 2 * mx + my
        chips = [(1 - mx, my), (mx, 1 - my), (1 - mx, 1 - my)]

        def half(w, chip_idx, hc):
            return outs[w].at[chip_idx, pl.ds(hc * halves[w], halves[w]), :]

        def copy(w, k, src, dst, to):
            return pltpu.make_async_remote_copy(src_ref=src, dst_ref=dst, send_sem=send_sems.at[w, k],
                                                recv_sem=recv_sems.at[w, k], device_id=to, device_id_type=MESH)

        sends = []
        for w in range(n):
            cp = copy(w, 6, ins[w], outs[w].at[j_me], (mx, my, 1 - mc))
            cp.start()
            sends.append(cp)
        for k, (px, py) in enumerate(chips):
            for w in range(n):
                cp = copy(w, k, ins[w].at[pl.ds(mc * halves[w], halves[w]), :], half(w, j_me, mc), (px, py, mc))
                cp.start()
                sends.append(cp)
        for k, (px, py) in enumerate(chips):
            for w in range(n):
                got = half(w, 2 * px + py, mc)
                copy(w, k, got, got, (px, py, mc)).wait_recv()
                cp = copy(w, 3 + k, got, got, (mx, my, 1 - mc))
                cp.start()
                sends.append(cp)
        for k, (px, py) in enumerate(chips):
            for w in range(n):
                got = half(w, 2 * px + py, 1 - mc)
                copy(w, 3 + k, got, got, (mx, my, 1 - mc)).wait_recv()
        for w in range(n):
            own = outs[w].at[j_me]
            copy(w, 6, own, own, (mx, my, 1 - mc)).wait_recv()
        for cp in sends:
            cp.wait_send()

    res = pl.pallas_call(
        body, out_shape=[jax.ShapeDtypeStruct((4,) + s.shape, s.dtype) for s in shards]
        + [jax.ShapeDtypeStruct((8, 128), F32)],
        in_specs=_hbm_specs(n), out_specs=_hbm_specs(n) + [pl.BlockSpec(memory_space=pltpu.VMEM)],
        scratch_shapes=[pltpu.SemaphoreType.DMA((n, 7)), pltpu.SemaphoreType.DMA((n, 7))],
        name="gather_weights")(*shards)
    return list(res[:n]), res[n]


def _rs_pair(gs, name):
    n = len(gs)
    halves = [g.shape[1] // 2 for g in gs]

    def body(*refs):
        ins, lands = refs[:n], refs[n:2 * n]
        send_sems, recv_sems = refs[2 * n:]
        mx, my, mc = lax.axis_index("x"), lax.axis_index("y"), lax.axis_index("c")
        copies = []
        for w in range(n):
            h = halves[w]
            cp = pltpu.make_async_remote_copy(
                src_ref=ins[w].at[:, pl.ds((1 - mc) * h, h), :], dst_ref=lands[w], send_sem=send_sems.at[w],
                recv_sem=recv_sems.at[w], device_id=(mx, my, 1 - mc), device_id_type=MESH)
            cp.start()
            copies.append(cp)
        for cp in copies:
            cp.wait()

    return pl.pallas_call(
        body, out_shape=[jax.ShapeDtypeStruct((4, h, g.shape[2]), g.dtype) for g, h in zip(gs, halves)],
        in_specs=_hbm_specs(n), out_specs=_hbm_specs(n),
        scratch_shapes=[pltpu.SemaphoreType.DMA((n,)), pltpu.SemaphoreType.DMA((n,))], name=name)(*gs)


def _rs_chips(parts):
    n = len(parts)

    def body(*refs):
        ins, lands = refs[:n], refs[n:2 * n]
        send_sems, recv_sems = refs[2 * n:]
        mx, my, mc = lax.axis_index("x"), lax.axis_index("y"), lax.axis_index("c")
        copies = []
        for k, (px, py) in enumerate([(1 - mx, my), (mx, 1 - my), (1 - mx, 1 - my)]):
            for w in range(n):
                cp = pltpu.make_async_remote_copy(
                    src_ref=ins[w].at[2 * px + py], dst_ref=lands[w].at[k], send_sem=send_sems.at[w, k],
                    recv_sem=recv_sems.at[w, k], device_id=(px, py, mc), device_id_type=MESH)
                cp.start()
                copies.append(cp)
        for cp in copies:
            cp.wait()

    return list(pl.pallas_call(
        body, out_shape=[jax.ShapeDtypeStruct((3,) + p.shape[1:], p.dtype) for p in parts],
        in_specs=_hbm_specs(n), out_specs=_hbm_specs(n),
        scratch_shapes=[pltpu.SemaphoreType.DMA((n, 3)), pltpu.SemaphoreType.DMA((n, 3))], name="rs_chips")(*parts))


_HBM = pl.BlockSpec(memory_space=pltpu.HBM)
_SEM = pl.BlockSpec(memory_space=pltpu.SEMAPHORE)
_EFFECT = pltpu.SideEffectType.DATAFLOW_SIDE_EFFECTING


def _ici_copies(kind, srcs, lands, send_sems, recv_sems):
    n = len(lands)
    mx, my, mc = lax.axis_index("x"), lax.axis_index("y"), lax.axis_index("c")
    j_me = 2 * mx + my
    copies = []
    if kind == "back":
        for w in range(n):
            h = lands[w].shape[0] // 2
            mine = lands[w].at[pl.ds(mc * h, h), :]
            copies.append(pltpu.make_async_remote_copy(
                src_ref=mine, dst_ref=mine, send_sem=send_sems.at[w], recv_sem=recv_sems.at[w],
                device_id=(mx, my, 1 - mc), device_id_type=MESH))
        return copies
    if kind == "all":
        for k in range(7):
            a, b, c = (k + 1) >> 2 & 1, (k + 1) >> 1 & 1, (k + 1) & 1
            peer = (1 - mx if a else mx, 1 - my if b else my, 1 - mc if c else mc)
            for w in range(n):
                copies.append(pltpu.make_async_remote_copy(
                    src_ref=srcs[w], dst_ref=lands[w].at[4 * mx + 2 * my + mc], send_sem=send_sems.at[7 * w + k],
                    recv_sem=recv_sems.at[7 * w + k], device_id=peer, device_id_type=MESH))
        return copies
    if kind == "pair":
        for w in range(n):
            h = srcs[w].shape[1] // 2
            copies.append(pltpu.make_async_remote_copy(
                src_ref=srcs[w].at[:, pl.ds((1 - mc) * h, h), :], dst_ref=lands[w], send_sem=send_sems.at[w],
                recv_sem=recv_sems.at[w], device_id=(mx, my, 1 - mc), device_id_type=MESH))
        return copies
    chips = [(1 - mx, my), (mx, 1 - my), (1 - mx, 1 - my)]
    if kind == "finish":
        for w in range(n):
            h = srcs[w].shape[0] // 2
            pushes = [(lands[w].at[2 * px + py, pl.ds(mc * h, h), :],) * 2 for px, py in chips]
            pushes.append((srcs[w], lands[w].at[j_me]))
            for k, (src, dst) in enumerate(pushes):
                copies.append(pltpu.make_async_remote_copy(
                    src_ref=src, dst_ref=dst, send_sem=send_sems.at[4 * w + k], recv_sem=recv_sems.at[4 * w + k],
                    device_id=(mx, my, 1 - mc), device_id_type=MESH))
        return copies
    for k, (px, py) in enumerate(chips):
        for w in range(n):
            if kind == "gather":
                h = srcs[w].shape[0] // 2
                src, dst = srcs[w].at[pl.ds(mc * h, h), :], lands[w].at[j_me, pl.ds(mc * h, h), :]
            else:
                src, dst = srcs[w].at[2 * px + py], lands[w].at[k]
            copies.append(pltpu.make_async_remote_copy(
                src_ref=src, dst_ref=dst, send_sem=send_sems.at[3 * w + k], recv_sem=recv_sems.at[3 * w + k],
                device_id=(px, py, mc), device_id_type=MESH))
    return copies


_SEMS_PER_OPERAND = {"gather": 3, "scatter": 3, "all": 7, "pair": 1, "finish": 4, "back": 1}


def _ici_start(kind, srcs, land_shapes, carry, name, lands=None):
    hbm = lambda a: pltpu.with_memory_space_constraint(a, pltpu.HBM)
    if lands is None:
        lands = [lax.empty(s, srcs[0].dtype) for s in land_shapes]
    ns, nl = len(srcs), len(lands)

    def body(*refs):
        send_sems, recv_sems = refs[ns + nl + 1], refs[ns + nl + 2]
        for cp in _ici_copies(kind, refs[:ns], refs[ns:ns + nl], send_sems, recv_sems):
            cp.start()

    args = [hbm(a) for a in list(srcs) + list(lands) + [carry]]
    n_sem = _SEMS_PER_OPERAND[kind] * nl
    out_shape = ([pltpu.SemaphoreType.DMA((n_sem,)), pltpu.SemaphoreType.DMA((n_sem,))]
                 + [pltpu.HBM(a.shape, a.dtype) for a in args])
    res = pl.pallas_call(
        body, name=name, out_shape=out_shape, in_specs=[_HBM] * len(args), out_specs=[_SEM, _SEM] + [_HBM] * len(args),
        input_output_aliases={i: 2 + i for i in range(len(args))},
        compiler_params=pltpu.CompilerParams(has_side_effects=_EFFECT))(*args)
    return res[0], res[1], list(res[2:2 + ns]), list(res[2 + ns:2 + ns + nl]), res[2 + ns + nl]


def _ici_wait(kind, send_sems, recv_sems, srcs, lands, after, name):
    ns, nl = len(srcs), len(lands)

    def body(*refs):
        for cp in _ici_copies(kind, refs[:ns], refs[ns:ns + nl], refs[ns + nl], refs[ns + nl + 1]):
            cp.wait_send()
            cp.wait_recv()

    args = list(srcs) + list(lands)
    res = pl.pallas_call(
        body, name=name, out_shape=[pltpu.HBM(a.shape, a.dtype) for a in args],
        in_specs=[_HBM] * len(args) + [_SEM, _SEM, pl.BlockSpec(memory_space=pl.ANY)], out_specs=[_HBM] * len(args),
        input_output_aliases={i: i for i in range(len(args))},
        compiler_params=pltpu.CompilerParams(has_side_effects=_EFFECT))(*args, send_sems, recv_sems, after)
    return list(res[:ns]), list(res[ns:])


def _tile_rows(h, c, itemsize, mult):
    best = h
    for t in range(mult, h + 1, mult):
        if h % t == 0 and t * c * itemsize <= (1 << 21):
            best = t
    return best


def _add_pair(g, land, place, name):
    _, h, c = land.shape
    t = _tile_rows(h, c, 2, 16)
    nb = h // t
    return _ew(lambda ids, u, v: (u.astype(F32) + v.astype(F32),), (4, nb),
               [(g, pl.BlockSpec((None, t, c), lambda j, i, s: (j, s[1] * nb + i, 0))),
                (land, pl.BlockSpec((None, t, c), lambda j, i, s: (j, i, 0)))],
               [(land.shape, BF16, pl.BlockSpec((None, t, c), lambda j, i, s: (j, i, 0)), None)], name, scalars=place)[0]


def _add_pair_many(gs, lands, place, name):
    ins, outs = [], []
    for g, l in zip(gs, lands):
        ins += [(g, pl.BlockSpec(l.shape, lambda i, s: (0, s[1], 0))), (l, pl.BlockSpec(l.shape, lambda i, s: (0, 0, 0)))]
        outs.append((l.shape, BF16, pl.BlockSpec(l.shape, lambda i, s: (0, 0, 0)), None))
    fn = lambda ids, *v: [v[2 * k].astype(F32) + v[2 * k + 1].astype(F32) for k in range(len(gs))]
    return list(_ew(fn, (1,), ins, outs, name, scalars=place))


def _add_chips_many(owns, lands, place, name):
    ins, outs = [], []
    for own, land in zip(owns, lands):
        _, h, c = land.shape
        ins += [(own, pl.BlockSpec((None, h, c), lambda i, s: (s[0], 0, 0))),
                (land, pl.BlockSpec((3, h, c), lambda i, s: (0, 0, 0)))]
        outs.append(((2 * h, c), F32, pl.BlockSpec((h, c), lambda i, s: (s[1], 0)), None))

    def fn(ids, *v):
        return [((v[2 * k].astype(F32) + v[2 * k + 1][0].astype(F32)) + v[2 * k + 1][1].astype(F32))
                + v[2 * k + 1][2].astype(F32) for k in range(len(owns))]

    return list(_ew(fn, (1,), ins, outs, name, scalars=place))


def _add_chips(own, land, place, name):
    _, h, c = land.shape
    t = _tile_rows(h, c, 4, 16)
    nb = h // t

    def fn(ids, a, b):
        return (((a.astype(F32) + b[0].astype(F32)) + b[1].astype(F32)) + b[2].astype(F32),)

    return _ew(fn, (nb,), [(own, pl.BlockSpec((None, t, c), lambda i, s: (s[0], i, 0))),
                           (land, pl.BlockSpec((3, t, c), lambda i, s: (0, i, 0)))],
               [((2 * h, c), F32, pl.BlockSpec((t, c), lambda i, s: (s[1] * nb + i, 0)), None)], name, scalars=place)[0]


W_IN_SEGMENTS = ((0, 256, KV0), (256, 288, KR0 + 64), (288, 672, Q0), (672, 1184, CX0), (1184, 1696, CB0),
                 (1696, 2208, CC0), (2208, 3232, GA0), (3232, 4256, GC0))
W_IN_SHARD = 1064


W_IN_SHARD_PAD = 1088


def _w_in_t_p_from_shards(s):
    pieces = []
    for o0, o1, p0 in sorted(W_IN_SEGMENTS, key=lambda t: t[2]):
        if p0 == KR0 + 64:
            pieces.append(jnp.zeros((64, s.shape[2]), s.dtype))
        for j in range(4):
            lo, hi = max(o0, j * W_IN_SHARD), min(o1, (j + 1) * W_IN_SHARD)
            if lo < hi:
                pieces.append(s[j, lo - j * W_IN_SHARD:hi - j * W_IN_SHARD])
    pieces.append(jnp.zeros((32, s.shape[2]), s.dtype))
    return jnp.concatenate(pieces, axis=0)


def _w_in_t_shards_from_p(g):
    shards = []
    for j in range(4):
        pieces = []
        for o0, o1, p0 in W_IN_SEGMENTS:
            lo, hi = max(o0, j * W_IN_SHARD), min(o1, (j + 1) * W_IN_SHARD)
            if lo < hi:
                pieces.append(g[p0 + lo - o0:p0 + hi - o0])
        pieces.append(jnp.zeros((W_IN_SHARD_PAD - W_IN_SHARD, g.shape[1]), g.dtype))
        shards.append(jnp.concatenate(pieces, axis=0))
    return jnp.stack(shards, axis=0)


def _cols_from_shards(s):
    return jnp.transpose(s, (1, 0, 2)).reshape(s.shape[1], -1)


def _rope_tables(T, TT, inverse):
    f32 = np.float32
    rows = T // GRID_W
    row = np.repeat(np.arange(rows), GRID_W).astype(f32)
    col = np.tile(np.arange(GRID_W), rows).astype(f32)
    inv = (f32(ROPE_THETA) ** (-np.arange(0, 16, 2, dtype=f32) / f32(16))).astype(f32)
    ang = np.concatenate([row[:, None] * inv, col[:, None] * inv], axis=-1).astype(f32)
    cos, sin = np.cos(ang).astype(f32), np.sin(ang).astype(f32)
    lane = np.arange(32)
    src = (lane // 16) * 8 + lane % 8
    lo = ((lane % 16) // 8 == 0).astype(f32)
    sgn = f32(-1.0 if inverse else 1.0)
    cos32 = cos[:, src]
    sin_lo32 = -sgn * sin[:, src] * lo
    sin_hi32 = sgn * sin[:, src] * (1 - lo)

    def widen(t32, fill):
        t = np.concatenate([np.full((T, 64), fill, f32), t32, np.full((T, 32), fill, f32)], axis=1)
        return np.concatenate([t, np.full((TT - T, HEAD_PAD), fill, f32)], axis=0)

    return [widen(cos32, 1.0), widen(sin_lo32, 0.0), widen(sin_hi32, 0.0)]


def _rope_table(T, TT):
    return jnp.asarray(np.concatenate(_rope_tables(T, TT, False) + _rope_tables(T, TT, True), axis=1))


def _local_step(xx, tgt, mod_lat, mod_ctx, W, late_weights, early_grads, early_continue):
    TT = xx.shape[0]
    T = tgt.shape[0]
    n_lat, n_all = T // ROW_TILE, TT // ROW_TILE
    sh1, sc1, g1, sh2, sc2, g2 = [mod_lat[:, k * D_MODEL:(k + 1) * D_MODEL] for k in range(6)]
    csh1, csc1 = mod_ctx[:, :D_MODEL], mod_ctx[:, D_MODEL:2 * D_MODEL]
    vec = lambda n: _full((1, n))
    row_out = lambda n, dt, rows=T: ((rows, n), dt, _rows(n), None)
    acc_out = lambda n: ((1, n), F32, _full((1, n)), 0)

    def f_norm1(ids, x, g, a_sh, a_sc, b_sh, b_sc):
        ctx = ids[0] >= n_lat
        sh, sc = jnp.where(ctx, b_sh, a_sh), jnp.where(ctx, b_sc, a_sc)
        return ((x * _rms(x) * g) * (1.0 + sc) + sh,)

    (hh,) = _ew(f_norm1, (n_all,), [(xx, _rows(D_MODEL)), (W["norm1_g"], vec(D_MODEL)), (sh1, vec(D_MODEL)),
                                   (sc1, vec(D_MODEL)), (csh1, vec(D_MODEL)), (csc1, vec(D_MODEL))],
                [row_out(D_MODEL, BF16, TT)], "norm1_fwd")
    tm_all = _pick(TT, (768, 256))
    pp = _mm(hh, W["w_in_t"], "nt", TT, P_COLS, D_MODEL, tm=tm_all, tn=2176, tk=D_MODEL, name="w_in_fwd")

    def f_lowrank(ids, ckv, cq, gkv, gq):
        return ckv * _rms(ckv) * gkv, cq * _rms(cq) * gq

    nkv, nq = _ew(f_lowrank, (n_all,), [(pp, _rows(KV_RANK, KV0 // KV_RANK)), (pp, _rows(Q_RANK, Q0 // Q_RANK)),
                                       (W["kv_norm_g"], vec(KV_RANK)), (W["q_norm_g"], vec(Q_RANK))],
                  [row_out(KV_RANK, BF16, TT), row_out(Q_RANK, BF16, TT)], "lowrank_norm_fwd")
    kv = _mm(nkv, W["w_ukv"], "nn", TT, 1024, KV_RANK, tm=tm_all, tn=256, tk=KV_RANK, name="w_ukv_fwd",
             b_spec=pl.BlockSpec((None, KV_RANK, 256), lambda i, j, k: (j, k, 0)))
    q_raw = _mm(nq, W["w_uq_t"], "nt", TT, 1024, Q_RANK, tm=tm_all, tn=1024, tk=Q_RANK, name="w_uq_fwd")

    tab = _rope_table(T, TT)
    _, q_raw = late_weights("before_attn", q_raw)
    o_pad, lse = _attn_fwd(q_raw, kv, pp, tab, T, TT)
    arrived, o_pad = late_weights("after_attn", o_pad)
    W = dict(W, **arrived)
    tm_lat = _pick(T, (1024, 512, 256))
    ya = _mm(o_pad, W["w_attn_out"], "nn", T, D_MODEL, 1024, tm=tm_lat, tn=D_MODEL, tk=1024, name="w_attn_out_fwd",
             out_dtype=BF16)

    tc = 256
    colT = lambda blk0: pl.BlockSpec((T, tc), lambda j: (0, blk0 + j))

    def f_conv(ids, xin, cb, cc, w, b):
        return (cb * _conv(cc * xin, w, b),)

    (e,) = _ew(f_conv, (CONV_DIM // tc,),
               [(pp, colT(CX0 // tc)), (pp, colT(CB0 // tc)), (pp, colT(CC0 // tc)),
                (W["conv_w"], pl.BlockSpec((3, tc), lambda j: (0, j))), (W["conv_b"], pl.BlockSpec((1, tc), lambda j: (0, j)))],
               [((T, CONV_DIM), BF16, colT(0), None)], "conv_fwd")
    yc = _mm(e, W["w_conv_out"], "nn", T, D_MODEL, CONV_DIM, tm=tm_lat, tn=256, tk=CONV_DIM, name="w_conv_out_fwd",
             out_dtype=BF16, b_spec=pl.BlockSpec((None, CONV_DIM, 256), lambda i, j, k: (j, k, 0)))

    def f_merge(ids, ga, gc, a, c):
        return (_sigmoid(ga) * a.astype(F32) + _sigmoid(gc) * c.astype(F32),)

    (mrg,) = _ew(f_merge, (n_lat,), [(pp, _rows(D_MODEL, 0)), (pp, _rows(D_MODEL, 1)), (ya, _rows(D_MODEL)),
                                    (yc, _rows(D_MODEL))], [row_out(D_MODEL, BF16)], "merge_fwd")
    mo = _mm(mrg, W["w_o"], "nn", T, D_MODEL, D_MODEL, tm=tm_lat, tn=D_MODEL, tk=D_MODEL, name="w_o_fwd")

    def f_norm2(ids, x, m, gate, g, sh, sc):
        x1 = x + gate * m
        return x1, (x1 * _rms(x1) * g) * (1.0 + sc) + sh

    x1, h2 = _ew(f_norm2, (n_lat,), [(xx, _rows(D_MODEL)), (mo, _rows(D_MODEL)), (g1, vec(D_MODEL)),
                                    (W["norm2_g"], vec(D_MODEL)), (sh2, vec(D_MODEL)), (sc2, vec(D_MODEL))],
                 [row_out(D_MODEL, F32), row_out(D_MODEL, BF16)], "norm2_fwd")
    arrived, h2 = late_weights("before_ffn", h2)
    W = dict(W, **arrived)
    up = _mm(h2, W["w_up"], "nn", T, 2 * D_FF, D_MODEL, tm=tm_lat, tn=1408, tk=D_MODEL, name="w_up_fwd",
             b_spec=pl.BlockSpec((None, D_MODEL, 1408), lambda i, j, k: (j, k, 0)))

    n_ff = D_FF // tc
    ffw = lambda off, n=3: pl.BlockSpec((n, tc), lambda j: (0, j + off))

    def f_ffn(ids, ug, uv, wg, wv, bg, bv):
        gate, val = _conv(ug, wg, bg), _conv(uv, wv, bv)
        return (gate * _sigmoid(gate) * val,)

    (act,) = _ew(f_ffn, (n_ff,), [(up, colT(0)), (up, colT(n_ff)), (W["ffn_conv_w"], ffw(0)), (W["ffn_conv_w"], ffw(n_ff)),
                                 (W["ffn_conv_b"], ffw(0, 1)), (W["ffn_conv_b"], ffw(n_ff, 1))],
                 [((T, D_FF), BF16, colT(0), None)], "ffn_act_fwd")
    f = _mm(act, W["w_down"], "nn", T, D_MODEL, D_FF, tm=tm_lat, tn=D_MODEL, tk=D_FF, name="w_down_fwd")

    def f_head(ids, x1_, f_, gate, gf, t):
        x2 = x1_ + gate * f_
        r = _rms(x2)
        xn = x2 * r
        err = xn * gf - t
        loss = 0.5 * jnp.sum(jnp.mean(err * err, axis=-1, keepdims=True))
        dy = err * (1.0 / D_MODEL)
        dx2 = _rms_bwd(dy * gf, xn, r)
        return dx2, dx2 * gate, _colsum(dy * xn), _colsum(dx2 * f_), jnp.full((1, 128), loss, F32)

    dx2, df, dg_f, dg2, loss = _ew(
        f_head, (n_lat,), [(x1, _rows(D_MODEL)), (f, _rows(D_MODEL)), (g2, vec(D_MODEL)), (W["final_g"], vec(D_MODEL)),
                           (tgt, _rows(D_MODEL))],
        [row_out(D_MODEL, F32), row_out(D_MODEL, BF16), acc_out(D_MODEL), acc_out(D_MODEL), acc_out(128)], "loss_head")

    d_w_down = _mm(act, df, "tn", D_FF, D_MODEL, T, tm=1408, tn=D_MODEL, tk=T, name="w_down_dw",
                   out_dtype=BF16).reshape(4, D_FF // 4, D_MODEL)
    da = _mm(df, W["w_down"], "nt", T, D_FF, D_MODEL, tm=tm_lat, tn=1408, tk=D_MODEL, name="w_down_dx")

    tcb = 128
    n_fb = D_FF // tcb
    colb = lambda blk0: pl.BlockSpec((T, tcb), lambda j: (0, blk0 + j))
    ffwb = lambda off, n=3: pl.BlockSpec((n, tcb), lambda j: (0, j + off))
    cvec = ((1, D_FF), F32, pl.BlockSpec((1, tcb), lambda j: (0, j)), None)

    def f_ffn_bwd(ids, ug, uv, d_act, wg, wv, bg, bv):
        sg, sv = _shifts(ug), _shifts(uv)
        gate, val = _conv(ug, wg, bg, sg), _conv(uv, wv, bv, sv)
        s = _sigmoid(gate)
        d_gate = d_act * val * s * (1.0 + gate * (1.0 - s))
        d_val = d_act * gate * s
        wg0, wg1, wg2 = _conv_bwd_w(d_gate, ug, sg)
        wv0, wv1, wv2 = _conv_bwd_w(d_val, uv, sv)
        d_up = [_conv_bwd_x(d_gate, wg), _conv_bwd_x(d_val, wv)]
        return d_up, [_colsum(d_gate), _colsum(d_val), wg0, wg1, wg2, wv0, wv1, wv2]

    d_up3, ffn_stats = _ew(
        f_ffn_bwd, (n_fb,),
        [(up, colb(0)), (up, colb(n_fb)), (da, colb(0)), (W["ffn_conv_w"], ffwb(0)), (W["ffn_conv_w"], ffwb(n_fb)),
         (W["ffn_conv_b"], ffwb(0, 1)), (W["ffn_conv_b"], ffwb(n_fb, 1))],
        [((2, T, D_FF), BF16, pl.BlockSpec((2, T, tcb), lambda j: (0, 0, j)), None),
         ((n_fb, 8, 1, tcb), F32, pl.BlockSpec((None, 8, 1, tcb), lambda j: (j, 0, 0, 0)), None)], "ffn_act_bwd")
    stat = lambda s: ffn_stats[:, s, 0, :].reshape(1, D_FF)
    d_ffn_conv_b = jnp.concatenate([stat(0), stat(1)], axis=1)
    d_ffn_conv_w = jnp.concatenate([jnp.concatenate([stat(2), stat(3), stat(4)], axis=0),
                                    jnp.concatenate([stat(5), stat(6), stat(7)], axis=0)], axis=1)

    tk_t = T
    d_w_up = _mm(h2, d_up3, "tn", D_MODEL, 2 * D_FF, T, tm=D_MODEL, tn=1408, tk=tk_t, name="w_up_dw", out_dtype=BF16,
                 b_spec=pl.BlockSpec((None, tk_t, 1408), lambda i, j, k: (j // 2, k, j % 2)),
                 o_spec=pl.BlockSpec((None, D_MODEL, 1408), lambda i, j, k: (j, i, 0)), out_shape=(4, D_MODEL, 1408))
    dh2 = _mm(d_up3, W["w_up"], "nt", T, D_MODEL, 2 * D_FF, tm=tm_lat, tn=D_MODEL, tk=1408, name="w_up_dx",
              a_spec=pl.BlockSpec((None, tm_lat, 1408), lambda i, j, k: (k // 2, i, k % 2)),
              b_spec=pl.BlockSpec((None, D_MODEL, 1408), lambda i, j, k: (k, j, 0)))

    def f_norm2_bwd(ids, dx2_, dh, x1_, m, g, sc, gate):
        r = _rms(x1_)
        xn = x1_ * r
        dx1 = dx2_ + _rms_bwd(dh * g * (1.0 + sc), xn, r)
        return dx1, dx1 * gate, _colsum(dh), _colsum(dh * xn * g), _colsum(dh * xn * (1.0 + sc)), _colsum(dx1 * m)

    dx1, dmo, dsh2, dsc2, dg_n2, dg1 = _ew(
        f_norm2_bwd, (n_lat,), [(dx2, _rows(D_MODEL)), (dh2, _rows(D_MODEL)), (x1, _rows(D_MODEL)), (mo, _rows(D_MODEL)),
                                (W["norm2_g"], vec(D_MODEL)), (sc2, vec(D_MODEL)), (g1, vec(D_MODEL))],
        [row_out(D_MODEL, F32), row_out(D_MODEL, BF16)] + [acc_out(D_MODEL)] * 4, "norm2_bwd")
    d_w_o = _mm(mrg, dmo, "tn", D_MODEL, D_MODEL, T, tm=D_MODEL, tn=D_MODEL, tk=tk_t, name="w_o_dw",
                out_dtype=BF16).reshape(4, D_MODEL // 4, D_MODEL)
    dmrg = _mm(dmo, W["w_o"], "nt", T, D_MODEL, D_MODEL, tm=tm_lat, tn=D_MODEL, tk=D_MODEL, name="w_o_dx",
               out_dtype=BF16)
    dmrg = early_grads("late", {"w_o": d_w_o, "w_up": d_w_up, "w_down": d_w_down}, dmrg, split=True)

    def f_merge_bwd(ids, dm, ga, gc, a, c):
        dm, a, c = dm.astype(F32), a.astype(F32), c.astype(F32)
        sa, sc_ = _sigmoid(ga), _sigmoid(gc)
        return dm * sa, dm * sc_, dm * a * sa * (1.0 - sa), dm * c * sc_ * (1.0 - sc_)

    dya, dyc, dp_ga, dp_gc = _ew(
        f_merge_bwd, (n_lat,), [(dmrg, _rows(D_MODEL)), (pp, _rows(D_MODEL, 0)), (pp, _rows(D_MODEL, 1)),
                                (ya, _rows(D_MODEL)), (yc, _rows(D_MODEL))], [row_out(D_MODEL, BF16)] * 4, "merge_bwd")
    dya = early_continue("late", dya)

    d_w_ao_p = _mm(o_pad, dya, "tn", 1024, D_MODEL, T, tm=1024, tn=D_MODEL, tk=tk_t, name="w_attn_out_dw", out_dtype=BF16)
    do_pad = _mm(dya, W["w_attn_out"], "nt", T, 1024, D_MODEL, tm=tm_lat, tn=1024, tk=D_MODEL, name="w_attn_out_dx")
    d_w_co = _mm(e, dyc, "tn", CONV_DIM, D_MODEL, T, tm=CONV_DIM, tn=256, tk=tk_t, name="w_conv_out_dw", out_dtype=BF16,
                 o_spec=pl.BlockSpec((None, CONV_DIM, 256), lambda i, j, k: (j, i, 0)), out_shape=(4, CONV_DIM, 256))
    de = _mm(dyc, W["w_conv_out"], "nt", T, CONV_DIM, D_MODEL, tm=tm_lat, tn=CONV_DIM, tk=256, name="w_conv_out_dx",
             b_spec=pl.BlockSpec((None, CONV_DIM, 256), lambda i, j, k: (k, j, 0)))

    def f_conv_bwd(ids, xin, cb, cc, d_e, w, b):
        z = cc * xin
        sz = _shifts(z)
        cz = _conv(z, w, b, sz)
        dcz = d_e * cb
        w0, w1, w2 = _conv_bwd_w(dcz, z, sz)
        dz = _conv_bwd_x(dcz, w)
        return dz * cc, d_e * cz, dz * xin, _colsum(dcz), w0, w1, w2

    cvec_c = ((1, CONV_DIM), F32, pl.BlockSpec((1, tc), lambda j: (0, j)), None)
    conv_b = _ew(f_conv_bwd, (CONV_DIM // tc,),
                 [(pp, colT(CX0 // tc)), (pp, colT(CB0 // tc)), (pp, colT(CC0 // tc)), (de, colT(0)),
                  (W["conv_w"], pl.BlockSpec((3, tc), lambda j: (0, j))), (W["conv_b"], pl.BlockSpec((1, tc), lambda j: (0, j)))],
                 [((T, CONV_DIM), BF16, colT(0), None)] * 3 + [cvec_c] * 4, "conv_bwd")
    dp_cx, dp_cb, dp_cc, d_conv_b = conv_b[:4]
    d_conv_w = jnp.concatenate(conv_b[4:7], axis=0)

    dq_raw, dkv, dp_kr = _attn_bwd(q_raw, kv, pp, o_pad, do_pad, lse, tab, T, TT)

    tk_a = TT
    d_w_uq_t = _mm(nq, dq_raw, "tn", Q_RANK, 1024, T, tm=Q_RANK, tn=1024, tk=T, name="w_uq_dw", transpose_out=True)
    dnq = _mm(dq_raw, W["w_uq_t"], "nn", T, Q_RANK, 1024, tm=tm_lat, tn=Q_RANK, tk=1024, name="w_uq_dx")
    d_w_ukv = _mm(nkv, dkv, "tn", KV_RANK, 1024, TT, tm=KV_RANK, tn=256, tk=tk_a, name="w_ukv_dw", out_dtype=BF16,
                  o_spec=pl.BlockSpec((None, KV_RANK, 256), lambda i, j, k: (j, i, 0)), out_shape=(4, KV_RANK, 256))
    dnkv = _mm(dkv, W["w_ukv"], "nt", TT, KV_RANK, 1024, tm=tm_all, tn=KV_RANK, tk=256, name="w_ukv_dx",
               b_spec=pl.BlockSpec((None, KV_RANK, 256), lambda i, j, k: (k, j, 0)))
    dnkv = early_grads("mid", {
        "w_attn_out": jnp.transpose(d_w_ao_p.reshape(N_HEADS, HEAD_PAD, 4, 256)[:, 64:], (2, 0, 1, 3)).reshape(
            4, N_HEADS * 64, 256),
        "w_conv_out": d_w_co,
        "w_uq": d_w_uq_t.reshape(4, 2, HEAD_PAD, Q_RANK)[:, :, :QK_DIM].reshape(4, 2 * QK_DIM, Q_RANK).astype(BF16),
        "w_ukv": d_w_ukv}, dnkv)

    def f_lowrank_bwd(ids, ckv, cq, dkv_, dq_, gkv, gq, ga, gc, cx, cb, cc, kr):
        rk, rq = _rms(ckv), _rms(cq)
        nk, nq_ = ckv * rk, cq * rq
        lat = ids[0] < n_lat
        dq_ = jnp.where(lat, dq_, 0.0)
        pieces = [jnp.where(lat, a, jnp.zeros_like(a)) for a in (ga, gc, cx, cb, cc)]
        pieces += [_rms_bwd(dkv_ * gkv, nk, rk).astype(BF16), _rms_bwd(dq_ * gq, nq_, rq).astype(BF16), kr.astype(BF16)]
        return jnp.concatenate(pieces, axis=1), _colsum(dkv_ * nk), _colsum(dq_ * nq_)

    lat_rows = lambda n: pl.BlockSpec((ROW_TILE, n), lambda i: (jnp.minimum(i, n_lat - 1), 0))
    dpp, dg_kv, dg_q = _ew(
        f_lowrank_bwd, (n_all,), [(pp, _rows(KV_RANK, KV0 // KV_RANK)), (pp, _rows(Q_RANK, Q0 // Q_RANK)),
                                  (dnkv, _rows(KV_RANK)), (dnq, lat_rows(Q_RANK)), (W["kv_norm_g"], vec(KV_RANK)),
                                  (W["q_norm_g"], vec(Q_RANK)), (dp_ga, lat_rows(D_MODEL)), (dp_gc, lat_rows(D_MODEL)),
                                  (dp_cx, lat_rows(CONV_DIM)), (dp_cb, lat_rows(CONV_DIM)), (dp_cc, lat_rows(CONV_DIM)),
                                  (dp_kr, _rows(HEAD_PAD))],
        [row_out(P_COLS, BF16, TT), acc_out(KV_RANK), acc_out(Q_RANK)], "lowrank_norm_bwd")
    d_w_in_t = _mm(hh, dpp, "tn", D_MODEL, P_COLS, TT, tm=512, tn=2176, tk=TT, name="w_in_dw", out_dtype=BF16,
                   transpose_out=True)
    dhh = _mm(dpp, W["w_in_t"], "nn", TT, D_MODEL, P_COLS, tm=tm_all, tn=512, tk=2176, name="w_in_dx")

    def f_norm1_bwd(ids, x, dh, dres, g, sc):
        r = _rms(x)
        xn = x * r
        return (dres + _rms_bwd(dh * g * (1.0 + sc), xn, r), _colsum(dh), _colsum(dh * xn * g),
                _colsum(dh * xn * (1.0 + sc)))

    grad_x, dsh1, dsc1, dg_n1 = _ew(
        f_norm1_bwd, (n_lat,), [(xx, _rows(D_MODEL)), (dhh, _rows(D_MODEL)), (dx1, _rows(D_MODEL)),
                                (W["norm1_g"], vec(D_MODEL)), (sc1, vec(D_MODEL))],
        [row_out(D_MODEL, F32)] + [acc_out(D_MODEL)] * 3, "norm1_bwd")

    def f_norm1_ctx_bwd(ids, x, dh, g, sc):
        xn = x * _rms(x)
        return _colsum(dh), _colsum(dh * xn * g), _colsum(dh * xn * (1.0 + sc))

    n_ctx = n_all - n_lat
    dcsh1, dcsc1, dg_n1c = _ew(
        f_norm1_ctx_bwd, (n_ctx,), [(xx, _rows(D_MODEL, 0, n_lat)), (dhh, _rows(D_MODEL, 0, n_lat)),
                                    (W["norm1_g"], vec(D_MODEL)), (csc1, vec(D_MODEL))], [acc_out(D_MODEL)] * 3,
        "norm1_ctx_bwd")

    big = {"w_in": _w_in_t_shards_from_p(d_w_in_t).astype(BF16)}
    zero = jnp.zeros((1, 4 * D_MODEL), F32)
    small = {
        "dmod_lat": jnp.concatenate([dsh1, dsc1, dg1, dsh2, dsc2, dg2], axis=1),
        "dmod_ctx": jnp.concatenate([dcsh1, dcsc1, zero], axis=1),
        "norm1_g": dg_n1 + dg_n1c, "norm2_g": dg_n2, "final_g": dg_f, "q_norm_g": dg_q, "kv_norm_g": dg_kv,
        "conv_b": d_conv_b, "conv_w": d_conv_w.reshape(1, -1), "ffn_conv_b": d_ffn_conv_b,
        "ffn_conv_w": d_ffn_conv_w.reshape(1, -1),
    }
    return grad_x, loss, big, small


SMALL = (("dmod_lat", 6144), ("dmod_ctx", 6144), ("norm1_g", 1024), ("norm2_g", 1024), ("final_g", 1024),
         ("q_norm_g", 384), ("kv_norm_g", 256), ("conv_b", 512), ("conv_w", 1536), ("ffn_conv_b", 5632),
         ("ffn_conv_w", 16896), ("loss", 128))
SMALL_ROWS = 320


def _adam_update(w, g, m, v):
    c1, c2 = 1.0 - ADAM_B1 ** ADAM_STEP, 1.0 - ADAM_B2 ** ADAM_STEP
    m2 = ADAM_B1 * m + (1.0 - ADAM_B1) * g
    v2 = ADAM_B2 * v + (1.0 - ADAM_B2) * (g * g)
    return [-ADAM_LR * ((m2 / c1) / (jnp.sqrt(v2 / c2) + ADAM_EPS) + ADAM_WD * w), m2, v2]


def _adamw(w, g, m, v, name):
    R, C = w.shape
    tr = 8 if R % 8 == 0 else R
    for t in range(8, R + 1, 8):
        if R % t == 0 and t * C * 4 <= (1 << 20):
            tr = t
    spec = pl.BlockSpec((tr, C), lambda i: (i, 0))
    return _ew(lambda ids, *vals: _adam_update(*vals), (R // tr,), [(w, spec), (g, spec), (m, spec), (v, spec)],
               [((R, C), F32, spec, None)] * 3, name)


def kernel(x, c, ctx, c_ctx, w_ada, b_ada, norm1_g, w_in, q_norm_g, kv_norm_g, w_uq, w_ukv, conv_w, conv_b, w_attn_out, w_conv_out, w_o, norm2_g, w_up, ffn_conv_w, ffn_conv_b, w_down, final_g, loss_target, m_c_ctx, m_w_ada, m_b_ada, m_norm1_g, m_w_in, m_q_norm_g, m_kv_norm_g, m_w_uq, m_w_ukv, m_conv_w, m_conv_b, m_w_attn_out, m_w_conv_out, m_w_o, m_norm2_g, m_w_up, m_ffn_conv_w, m_ffn_conv_b, m_w_down, m_final_g, v_c_ctx, v_w_ada, v_b_ada, v_norm1_g, v_w_in, v_q_norm_g, v_kv_norm_g, v_w_uq, v_w_ukv, v_conv_w, v_conv_b, v_w_attn_out, v_w_conv_out, v_w_o, v_norm2_g, v_w_up, v_ffn_conv_w, v_ffn_conv_b, v_w_down, v_final_g):
    mx, my, mc = lax.axis_index("x"), lax.axis_index("y"), lax.axis_index("c")
    chip = 2 * mx + my
    dev = 4 * mx + 2 * my + mc
    T, Tc = x.shape[1], ctx.shape[1]
    TT = T + Tc
    w_in_t, m_w_in_t, v_w_in_t = (jnp.transpose(a[0]) for a in (w_in, m_w_in, v_w_in))
    w_uq_t, m_w_uq_t, v_w_uq_t = (jnp.transpose(a[0]) for a in (w_uq, m_w_uq, v_w_uq))
    conv_sh = jnp.concatenate([conv_w[0], ffn_conv_w[0]], axis=1)
    pay1 = jnp.concatenate([jnp.pad(c, ((0, 7), (0, 0))), jnp.pad(conv_sh, ((0, 5), (0, 0)))], axis=1)
    c_send, c_recv, c_src, c_land, zero0 = _ici_start("all", [pay1], [(8, 8, 2560)], jnp.zeros((8, 128), F32),
                                                      "cond_start")
    w_in_bf = (jnp.pad(w_in_t, ((0, W_IN_SHARD_PAD - W_IN_SHARD), (0, 0))) + zero0[0, 0]).astype(BF16)
    shards = {"w_in": w_in_bf, "w_uq": w_uq_t, "w_ukv": w_ukv[0], "w_attn_out": w_attn_out[0],
              "w_conv_out": w_conv_out[0], "w_o": w_o[0], "w_up": w_up[0], "w_down": w_down[0]}
    (pay1,), (c_land,) = _ici_wait("all", c_send, c_recv, c_src, c_land, shards["w_in"], "cond_wait")
    got1 = lax.dynamic_update_slice(c_land, pay1[None], (dev, 0, 0))
    c_all = got1[:, 0, :D_MODEL]
    conv_all = got1[0::2, :3, D_MODEL:]
    conv_w_full = _cols_from_shards(conv_all[:, :, :128])
    ffn_conv_w_full = _cols_from_shards(conv_all[:, :, 128:])

    cond = jnp.concatenate([c_all, c_ctx.reshape(1, D_MODEL), jnp.zeros((7, D_MODEL), F32)], axis=0)

    def f_silu(ids, v):
        return (v * _sigmoid(v),)

    (s16,) = _ew(f_silu, (1,), [(cond, _full((16, D_MODEL)))], [((16, D_MODEL), F32, _full((16, D_MODEL)), None)], "silu_cond")
    mod_sh = _mm(s16, w_ada[0], "nn", 16, 1536, D_MODEL, tm=16, tn=768, tk=D_MODEL, name="w_ada_fwd")
    m_send, m_recv, m_src, m_land, zero1 = _ici_start("all", [mod_sh], [(8, 16, 1536)], jnp.zeros((8, 128), F32),
                                                      "mod_start")
    shards["w_ukv"] = w_ukv[0] + zero1[0, 0]

    names = [n for n, _ in BIG]
    first = [n for n in names if n not in GATHER_LATE]
    gathered, zero = _gather_weights([shards[n].astype(BF16) for n in first])
    full = dict(zip(first, gathered))
    (mod_mine,), (m_land,) = _ici_wait("all", m_send, m_recv, m_src, m_land, gathered[0], "mod_wait")
    got2 = lax.dynamic_update_slice(m_land, mod_mine[None], (dev, 0, 0))
    mod_all = _cols_from_shards(got2[0::2]) + b_ada
    mod_lat = lax.dynamic_slice_in_dim(mod_all, dev, 1, axis=0)
    mod_ctx = mod_all[8:9]
    xx = jnp.concatenate([x[0], ctx[0]], axis=0)
    late_groups = {"g1": ("w_attn_out", "w_conv_out", "w_o"), "g2": ("w_up", "w_down")}
    flight = {}
    for tag, group in late_groups.items():
        bf = [(shards[n] + zero[0, 0]).astype(BF16) for n in group]
        flight[tag] = _ici_start("gather", bf, [(4,) + s.shape for s in bf], xx, "gather_" + tag + "_start")
        xx = flight[tag][4]

    def chip_stage_done(tag, x):
        send, recv, src, land, _ = flight[tag]
        src, land = _ici_wait("gather", send, recv, src, land, x, "gather_" + tag + "_wait")
        flight[tag] = _ici_start("finish", src, None, x, "finish_" + tag + "_start", lands=land)
        return flight[tag][4]

    def arrived(tag, x):
        send, recv, src, land, _ = flight[tag]
        return dict(zip(late_groups[tag], _ici_wait("finish", send, recv, src, land, x, "finish_" + tag + "_wait")[1]))

    def late_weights(point, x):
        if point == "before_attn":
            return {}, chip_stage_done("g1", x)
        if point == "after_attn":
            got = arrived("g1", x)
            wao = _cols_from_shards(got["w_attn_out"]).reshape(N_HEADS, 64, D_MODEL)
            ready = {"w_attn_out": jnp.pad(wao, ((0, 0), (64, 0), (0, 0))).reshape(N_HEADS * HEAD_PAD, D_MODEL),
                     "w_conv_out": got["w_conv_out"], "w_o": got["w_o"].reshape(D_MODEL, D_MODEL)}
            return ready, chip_stage_done("g2", x)
        got = arrived("g2", x)
        return {"w_up": got["w_up"], "w_down": got["w_down"].reshape(D_FF, D_MODEL)}, x

    wuq_t = full["w_uq"].reshape(N_HEADS, QK_DIM, Q_RANK)
    W = {
        "w_in_t": _w_in_t_p_from_shards(full["w_in"]),
        "w_uq_t": jnp.pad(wuq_t, ((0, 0), (0, HEAD_PAD - QK_DIM), (0, 0))).reshape(N_HEADS * HEAD_PAD, Q_RANK),
        "w_ukv": full["w_ukv"],
        "norm1_g": norm1_g, "norm2_g": norm2_g, "final_g": final_g.reshape(1, D_MODEL), "q_norm_g": q_norm_g,
        "kv_norm_g": kv_norm_g, "conv_w": conv_w_full, "conv_b": conv_b, "ffn_conv_w": ffn_conv_w_full,
        "ffn_conv_b": ffn_conv_b,
    }

    place = jnp.stack([chip, mc]).astype(jnp.int32)
    early = {}

    pending = {}

    def scatter(tag, group, gs, from_sib, carry):
        if tag == "mid":
            sums = _add_pair_many(gs, from_sib, place, "rs_pair_add_mid")
        else:
            sums = [_add_pair(gs[w], from_sib[w], place, "rs_pair_add_" + n) for w, n in enumerate(group)]
        send, recv, sums, land, carry = _ici_start(
            "scatter", sums, [(3,) + s.shape[1:] for s in sums], carry, "rs_chips_" + tag + "_start")
        early[tag] = (group, send, recv, sums, land)
        return carry

    def early_grads(tag, g, carry, split=False):
        gs = list(g.values())
        if not split:
            return scatter(tag, list(g), gs, _rs_pair(gs, "rs_pair_" + tag), carry)
        send, recv, gs, land, carry = _ici_start(
            "pair", gs, [(4, s.shape[1] // 2, s.shape[2]) for s in gs], carry, "rs_pair_" + tag + "_start")
        pending[tag] = (list(g), send, recv, gs, land)
        return carry

    def early_continue(tag, carry):
        group, send, recv, gs, land = pending[tag]
        gs, from_sib = _ici_wait("pair", send, recv, gs, land, carry, "rs_pair_" + tag + "_wait")
        return scatter(tag, group, gs, from_sib, carry)

    grad_x, loss_part, gbig, gsmall = _local_step(xx, loss_target[0], mod_lat, mod_ctx, W, late_weights, early_grads,
                                                  early_continue)

    gsmall["loss"] = loss_part
    pay3 = jnp.concatenate([gsmall[n].reshape(-1) for n, _ in SMALL])
    pay3 = jnp.pad(pay3, (0, SMALL_ROWS * 128 - pay3.shape[0])).reshape(SMALL_ROWS, 128)
    s_send, s_recv, s_src, s_land, w_in_thru = _ici_start("all", [pay3], [(8, SMALL_ROWS, 128)], gbig["w_in"],
                                                         "small_start")
    gbig = {"w_in": w_in_thru}

    after_small = early_grads("last", gbig, s_src[0])

    (pay3,), (s_land,) = _ici_wait("all", s_send, s_recv, [after_small], s_land, early["last"][3][0], "small_wait")
    got3 = lax.dynamic_update_slice(s_land, pay3[None], (dev, 0, 0)).reshape(8 * SMALL_ROWS, 128)

    def f_sum8(ids, a):
        s = a[0:SMALL_ROWS]
        for d in range(1, 8):
            s = s + a[d * SMALL_ROWS:(d + 1) * SMALL_ROWS]
        return (s,)

    (vsum,) = _ew(f_sum8, (1,), [(got3, _full((8 * SMALL_ROWS, 128)))],
                  [((SMALL_ROWS, 128), F32, _full((SMALL_ROWS, 128)), None)], "sum_small")
    vflat = vsum.reshape(-1)
    gvec, off = {}, 0
    for n, size in SMALL:
        gvec[n] = vflat[off:off + size]
        off += size
    loss = gvec["loss"][0]
    dmod_rows = got3.reshape(8, SMALL_ROWS * 128)[:, :6 * D_MODEL]
    dm16 = jnp.concatenate([dmod_rows, gvec["dmod_ctx"].reshape(1, -1), jnp.zeros((7, 6 * D_MODEL), F32)], axis=0)

    def f_colsum(ids, a):
        return (_colsum(a),)

    (g_b_ada,) = _ew(f_colsum, (1,), [(dm16, _full((16, 6 * D_MODEL)))],
                     [((1, 6 * D_MODEL), F32, _full((1, 6 * D_MODEL)), None)], "b_ada_grad")
    dm_sh = lax.dynamic_slice_in_dim(dm16, chip * 1536, 1536, axis=1)
    g_w_ada = _mm(s16, dm_sh, "tn", D_MODEL, 1536, 16, tm=512, tn=768, tk=16, name="w_ada_dw")
    dcond_part = _mm(dm_sh, w_ada[0], "nt", 16, D_MODEL, 1536, tm=16, tn=512, tk=1536, name="w_ada_dx")
    d_send, d_recv, d_src, d_land, vsum = _ici_start("all", [dcond_part[8:16]], [(8, 8, D_MODEL)], vsum, "dcond_start")

    def finish_start(tags, after):
        done, halves = [], []
        for tag in tags:
            tag_names, send, recv, sums, land = early[tag]
            sums, land = _ici_wait("scatter", send, recv, sums, land, after, "rs_chips_" + tag + "_wait")
            done += tag_names
            if tag == "mid":
                halves += _add_chips_many(sums, land, place, "rs_chip_add_mid")
            else:
                halves += [_add_chips(a, b, place, "rs_chip_add_" + n) for a, b, n in zip(sums, land, tag_names)]
        send, recv, _, halves, _ = _ici_start("back", [], None, jnp.zeros((8, 128), F32), "rs_back_" + tags[0] + "_start",
                                              lands=halves)
        return done, send, recv, halves

    def finish_wait(state, after):
        done, send, recv, halves = state
        return dict(zip(done, _ici_wait("back", send, recv, [], halves, after, "rs_back_" + done[0] + "_wait")[1]))

    grads, deltas, new_m, new_v = {}, {}, {}, {}

    raw = {}

    def adam(n, w_, m_, v_, g, transposed):
        d_, m2, v2 = _adamw(w_, g, m_, v_, "adamw_" + n)
        raw[n] = d_
        back = (lambda a: jnp.transpose(a)[None]) if transposed else (lambda a: a[None])
        grads[n], deltas[n], new_m[n], new_v[n] = back(g[:w_.shape[0]]), back(d_), back(m2), back(v2)

    pending_back = finish_start(["late", "mid"], grad_x)
    adam("w_ada", w_ada[0], m_w_ada[0], v_w_ada[0], g_w_ada, False)
    gw = finish_wait(pending_back, raw["w_ada"])
    for n, (w_, m_, v_) in {"w_o": (w_o, m_w_o, v_w_o), "w_up": (w_up, m_w_up, v_w_up),
                            "w_down": (w_down, m_w_down, v_w_down)}.items():
        adam(n, w_[0], m_[0], v_[0], gw[n], False)
    pending_back = finish_start(["last"], raw["w_up"])

    (dcond_mine,), (d_land,) = _ici_wait("all", d_send, d_recv, d_src, d_land, raw["w_down"], "dcond_wait")
    got4 = lax.dynamic_update_slice(d_land, dcond_mine[None], (dev, 0, 0))[0::2, 0]

    def f_c_ctx(ids, parts, cc):
        s = _sigmoid(cc)
        d = parts[0:1] + parts[1:2] + parts[2:3] + parts[3:4]
        return (d * s * (1.0 + cc * (1.0 - s)),)

    (g_c_ctx,) = _ew(f_c_ctx, (1,), [(got4, _full((4, D_MODEL))), (c_ctx.reshape(1, D_MODEL), _full((1, D_MODEL)))],
                     [((1, D_MODEL), F32, _full((1, D_MODEL)), None)], "c_ctx_grad")

    conv_w_g = lax.dynamic_slice_in_dim(gvec["conv_w"].reshape(3, CONV_DIM), chip * 128, 128, axis=1)
    ffn_conv_w_g = lax.dynamic_slice_in_dim(gvec["ffn_conv_w"].reshape(3, 2 * D_FF), chip * 1408, 1408, axis=1)
    vec_params = (("c_ctx", c_ctx, m_c_ctx, v_c_ctx, g_c_ctx), ("b_ada", b_ada, m_b_ada, v_b_ada, g_b_ada),
                  ("norm1_g", norm1_g, m_norm1_g, v_norm1_g, gvec["norm1_g"]),
                  ("q_norm_g", q_norm_g, m_q_norm_g, v_q_norm_g, gvec["q_norm_g"]),
                  ("kv_norm_g", kv_norm_g, m_kv_norm_g, v_kv_norm_g, gvec["kv_norm_g"]),
                  ("conv_w", conv_w, m_conv_w, v_conv_w, conv_w_g), ("conv_b", conv_b, m_conv_b, v_conv_b, gvec["conv_b"]),
                  ("norm2_g", norm2_g, m_norm2_g, v_norm2_g, gvec["norm2_g"]),
                  ("ffn_conv_w", ffn_conv_w, m_ffn_conv_w, v_ffn_conv_w, ffn_conv_w_g),
                  ("ffn_conv_b", ffn_conv_b, m_ffn_conv_b, v_ffn_conv_b, gvec["ffn_conv_b"]),
                  ("final_g", final_g, m_final_g, v_final_g, gvec["final_g"]))
    two_d = lambda a: a.reshape((-1, a.shape[-1]))
    many = [p + ((lambda r, s=p[1].shape: r.reshape(s)),) for p in vec_params]
    for n, w_, m_, v_ in (("w_ukv", w_ukv, m_w_ukv, v_w_ukv), ("w_attn_out", w_attn_out, m_w_attn_out, v_w_attn_out),
                          ("w_conv_out", w_conv_out, m_w_conv_out, v_w_conv_out)):
        many.append((n, w_, m_, v_, gw[n], (lambda r, s=w_.shape: r.reshape(s))))
    many.append(("w_uq", w_uq_t, m_w_uq_t, v_w_uq_t, gw["w_uq"], lambda r: jnp.transpose(r)[None]))

    def f_adam_many(ids, *vals):
        out = []
        for k in range(len(many)):
            out += _adam_update(*vals[4 * k:4 * k + 4])
        return out

    ins_v, outs_v = [], []
    for p in many:
        shp = two_d(p[1]).shape
        ins_v += [(two_d(a), _full(shp)) for a in (p[1], p[4], p[2], p[3])]
        outs_v += [(shp, F32, _full(shp), None)] * 3
    res_v = _ew(f_adam_many, (1,), ins_v, outs_v, "adamw_small")
    for k, p in enumerate(many):
        n, post = p[0], p[5]
        grads[n] = post(two_d(p[4]))
        deltas[n], new_m[n], new_v[n] = (post(r) for r in res_v[3 * k:3 * k + 3])

    gw_in = finish_wait(pending_back, res_v[0])
    adam("w_in", w_in_t, m_w_in_t, v_w_in_t, gw_in["w_in"], True)

    order = ("c_ctx", "w_ada", "b_ada", "norm1_g", "w_in", "q_norm_g", "kv_norm_g", "w_uq", "w_ukv", "conv_w", "conv_b",
             "w_attn_out", "w_conv_out", "w_o", "norm2_g", "w_up", "ffn_conv_w", "ffn_conv_b", "w_down", "final_g")
    return (loss, grad_x[None], *[grads[n] for n in order], *[deltas[n] for n in order],
            *[new_m[n] for n in order], *[new_v[n] for n in order])
```

```python
import functools

import jax
import jax.numpy as jnp
import numpy as np
from jax import lax
from jax.experimental import pallas as pl
from jax.experimental.pallas import tpu as pltpu

F32, BF16 = jnp.float32, jnp.bfloat16
MESH = pl.DeviceIdType.MESH

D_MODEL = 1024
N_HEADS = 8
HEAD_PAD = 128
QK_DIM = 96
Q_RANK, KV_RANK = 384, 256
CONV_DIM = 512
D_FF = 2816
GRID_W = 64
ROPE_THETA = 10000.0
EPS = 1e-6
GA0, GC0, CX0, CB0, CC0, KV0, Q0, KR0, P_COLS = 0, 1024, 2048, 2560, 3072, 3584, 3840, 4224, 4352
ROW_TILE = 256
VMEM_LIMIT_BYTES = 48 * 1024 * 1024

ADAM_LR, ADAM_B1, ADAM_B2, ADAM_EPS, ADAM_WD, ADAM_STEP = 0.001, 0.9, 0.999, 1e-08, 0.01, 10

BIG = (("w_in", (1088, 1024)), ("w_uq", (192, 384)), ("w_ukv", (256, 256)), ("w_attn_out", (512, 256)),
       ("w_conv_out", (512, 256)), ("w_o", (256, 1024)), ("w_up", (1024, 1408)), ("w_down", (704, 1024)))

GATHER_LATE = ("w_attn_out", "w_conv_out", "w_o", "w_up", "w_down")

NN = (((1,), (0,)), ((), ()))
NT = (((1,), (1,)), ((), ()))
TN = (((0,), (0,)), ((), ()))


def _cp(sem):
    return pltpu.CompilerParams(dimension_semantics=sem, vmem_limit_bytes=VMEM_LIMIT_BYTES)


PIN_BYTES = 1 << 19


def _in_hbm(arrays):
    return [pltpu.with_memory_space_constraint(a, pltpu.HBM) if a.size * a.dtype.itemsize >= PIN_BYTES else a
            for a in arrays]


def _out(shape, dtype):
    n = 1
    for d in shape:
        n *= d
    big = n * jnp.dtype(dtype).itemsize >= PIN_BYTES
    return pltpu.HBM(shape, dtype) if big else jax.ShapeDtypeStruct(shape, dtype)


def _pick(n, prefs):
    for p in prefs:
        if n % p == 0:
            return p
    return n


def _mm(a, b, mode, M, N, K, *, tm, tn, tk, name, out_dtype=F32, a_spec=None, b_spec=None, o_spec=None,
        out_shape=None, transpose_out=False):
    assert M % tm == 0 and N % tn == 0 and K % tk == 0, (name, M, N, K, tm, tn, tk)
    nk = K // tk
    dims = {"nn": NN, "nt": NT, "tn": TN}[mode]
    if a_spec is None:
        a_spec = (pl.BlockSpec((tk, tm), lambda i, j, k: (k, i)) if mode == "tn"
                  else pl.BlockSpec((tm, tk), lambda i, j, k: (i, k)))
    if b_spec is None:
        b_spec = (pl.BlockSpec((tn, tk), lambda i, j, k: (j, k)) if mode == "nt"
                  else pl.BlockSpec((tk, tn), lambda i, j, k: (k, j)))
    if o_spec is None:
        o_spec = (pl.BlockSpec((tn, tm), lambda i, j, k: (j, i)) if transpose_out
                  else pl.BlockSpec((tm, tn), lambda i, j, k: (i, j)))
    if out_shape is None:
        out_shape = (N, M) if transpose_out else (M, N)

    def emit(o_ref, val):
        o_ref[...] = (val.T if transpose_out else val).astype(o_ref.dtype)

    def body(a_ref, b_ref, o_ref, *scratch):
        part = lax.dot_general(a_ref[...].astype(BF16), b_ref[...].astype(BF16), dims, preferred_element_type=F32)
        if nk == 1:
            emit(o_ref, part)
            return
        acc_ref, = scratch
        k = pl.program_id(2)

        @pl.when(k == 0)
        def _():
            acc_ref[...] = part

        @pl.when((k > 0) & (k < nk - 1))
        def _():
            acc_ref[...] += part

        @pl.when(k == nk - 1)
        def _():
            emit(o_ref, acc_ref[...] + part)

    return pl.pallas_call(
        body, grid=(M // tm, N // tn, nk), in_specs=[a_spec, b_spec], out_specs=o_spec,
        out_shape=_out(out_shape, out_dtype),
        scratch_shapes=[pltpu.VMEM((tm, tn), F32)] if nk > 1 else [],
        compiler_params=_cp(("parallel", "parallel", "arbitrary")), name=name)(*_in_hbm([a, b]))


def _ew(fn, grid, ins, outs, name, scalars=None):
    n_in = len(ins)
    n_sc = 0 if scalars is None else 1

    def store(ref, val, acc, ids):
        if isinstance(val, (list, tuple)):
            for h, v in enumerate(val):
                ref[h] = v.astype(ref.dtype)
            return
        if acc is None:
            ref[...] = val.astype(ref.dtype)
            return

        @pl.when(ids[acc] == 0)
        def _():
            ref[...] = val.astype(ref.dtype)

        @pl.when(ids[acc] > 0)
        def _():
            ref[...] += val.astype(ref.dtype)

    def body(*refs):
        refs = refs[n_sc:]
        ids = tuple(pl.program_id(a) for a in range(len(grid)))
        vals = fn(ids, *[r[...] for r in refs[:n_in]])
        for ref, val, (_, _, _, acc) in zip(refs[n_in:], vals, outs):
            store(ref, val, acc, ids)

    acc_axes = {o[3] for o in outs if o[3] is not None}
    sem = tuple("arbitrary" if a in acc_axes else "parallel" for a in range(len(grid)))
    in_specs, out_specs = [s for _, s in ins], [o[2] for o in outs]
    out_shape = [_out(o[0], o[1]) for o in outs]
    args = _in_hbm([a for a, _ in ins])
    if scalars is None:
        return pl.pallas_call(body, grid=grid, in_specs=in_specs, out_specs=out_specs, out_shape=out_shape,
                              compiler_params=_cp(sem), name=name)(*args)
    spec = pltpu.PrefetchScalarGridSpec(num_scalar_prefetch=1, grid=grid, in_specs=in_specs, out_specs=out_specs)
    return pl.pallas_call(body, grid_spec=spec, out_shape=out_shape, compiler_params=_cp(sem), name=name)(scalars, *args)


def _rows(width, cblk=0, roff=0, tr=ROW_TILE):
    return pl.BlockSpec((tr, width), lambda i: (i + roff, cblk))


def _full(shape):
    nd = len(shape)
    return pl.BlockSpec(shape, lambda *_: (0,) * nd)


def _sigmoid(x):
    return 1.0 / (1.0 + jnp.exp2(x * (-1.4426950408889634)))


def _rms(x):
    return lax.rsqrt(jnp.mean(x * x, axis=-1, keepdims=True) + EPS)


def _rms_bwd(dn, xn, r):
    return r * (dn - xn * jnp.mean(dn * xn, axis=-1, keepdims=True))


def _colsum(x):
    return jnp.sum(x, axis=0, keepdims=True)


def _shifts(x):
    n = x.shape[0]
    rows = lax.broadcasted_iota(jnp.int32, x.shape, 0)
    return jnp.where(rows == 0, 0.0, pltpu.roll(x, 1, 0)), jnp.where(rows == n - 1, 0.0, pltpu.roll(x, n - 1, 0))


def _conv(x, w, b, shifted=None):
    prev, nxt = _shifts(x) if shifted is None else shifted
    return b + prev * w[0:1] + x * w[1:2] + nxt * w[2:3]


def _conv_bwd_x(dy, w):
    prev, nxt = _shifts(dy)
    return nxt * w[0:1] + dy * w[1:2] + prev * w[2:3]


def _conv_bwd_w(dy, x, shifted):
    prev, nxt = shifted
    return _colsum(dy * prev), _colsum(dy * x), _colsum(dy * nxt)


def _rope(x, cos, sin_lo, sin_hi):
    return x * cos + pltpu.roll(x, HEAD_PAD - 8, 1) * sin_lo + pltpu.roll(x, 8, 1) * sin_hi


ATTN_SCALE = QK_DIM ** -0.5
LOG2_E = 1.4426950408889634


def _rope_t(x, tab, inverse=False):
    o = 3 * HEAD_PAD if inverse else 0
    return _rope(x, tab[:, o:o + HEAD_PAD], tab[:, o + HEAD_PAD:o + 2 * HEAD_PAD], tab[:, o + 2 * HEAD_PAD:o + 3 * HEAD_PAD])


def _head_keys(kv_ref, kr_ref, tab_ref, kc_ref, vp_ref):
    kv = kv_ref[...]
    lane = lax.broadcasted_iota(jnp.int32, kv.shape, 1)
    kc_ref[...] = jnp.where(lane < 64, kv, _rope_t(kr_ref[...], tab_ref[...])).astype(BF16)
    vp_ref[...] = jnp.where(lane >= 64, kv, 0.0).astype(BF16)


ATTN_Q_TILE = 512


def _attn_specs(tq, TT):
    q = pl.BlockSpec((tq, HEAD_PAD), lambda h, i: (i, h))
    keys = pl.BlockSpec((TT, HEAD_PAD), lambda h, i: (0, h))
    kr = pl.BlockSpec((TT, HEAD_PAD), lambda h, i: (0, KR0 // HEAD_PAD))
    tab_q = pl.BlockSpec((tq, 6 * HEAD_PAD), lambda h, i: (i, 0))
    tab_k = pl.BlockSpec((TT, 6 * HEAD_PAD), lambda h, i: (0, 0))
    return q, keys, kr, tab_q, tab_k


def _attn_fwd(q_raw, kv, pp, tab, T, TT):
    tq = ROW_TILE

    def body(q_ref, kv_ref, kr_ref, tq_ref, tk_ref, o_ref, kc, vp):
        @pl.when(pl.program_id(1) == 0)
        def _():
            _head_keys(kv_ref, kr_ref, tk_ref, kc, vp)

        q = _rope_t(q_ref[...], tq_ref[...]).astype(BF16)
        s = lax.dot_general(q, kc[...], NT, preferred_element_type=F32)
        m = jnp.max(s, axis=-1, keepdims=True)
        p = jnp.exp2((s - m) * (ATTN_SCALE * LOG2_E))
        l = jnp.sum(p, axis=-1, keepdims=True)
        o = lax.dot_general(p.astype(BF16), vp[...], NN, preferred_element_type=F32)
        lane = lax.broadcasted_iota(jnp.int32, o.shape, 1)
        o_ref[...] = jnp.where(lane < 64, m * ATTN_SCALE + jnp.log(l), o / l)

    qs, keys, kr, _, _ = _attn_specs(tq, TT)
    tab_q = pl.BlockSpec((tq, 3 * HEAD_PAD), lambda h, i: (i, 0))
    tab_k = pl.BlockSpec((TT, 3 * HEAD_PAD), lambda h, i: (0, 0))
    return pl.pallas_call(
        body, grid=(N_HEADS, T // tq), in_specs=[qs, keys, kr, tab_q, tab_k], out_specs=qs,
        out_shape=jax.ShapeDtypeStruct((T, N_HEADS * HEAD_PAD), F32),
        scratch_shapes=[pltpu.VMEM((TT, HEAD_PAD), BF16), pltpu.VMEM((TT, HEAD_PAD), BF16)],
        compiler_params=_cp(("parallel", "arbitrary")), name="attn_fwd",
    )(*_in_hbm([q_raw, kv, pp, tab, tab]))


def _attn_bwd(q_raw, kv, pp, o, do, tab, T, TT):
    tq = _pick(T, (ATTN_Q_TILE, ROW_TILE))
    nq = T // tq

    def body(q_ref, kv_ref, kr_ref, tq_ref, tk_ref, o_ref, do_ref, dq_ref, dkv_ref, dkr_ref, kc, vp, dk, dv):
        h, i = pl.program_id(0), pl.program_id(1)

        @pl.when(i == 0)
        def _():
            _head_keys(kv_ref, kr_ref, tk_ref, kc, vp)
            dk[...] = jnp.zeros_like(dk)
            dv[...] = jnp.zeros_like(dv)

        q = _rope_t(q_ref[...], tq_ref[...]).astype(BF16)
        k, v, d_o = kc[...], vp[...], do_ref[...]
        s = lax.dot_general(q, k, NT, preferred_element_type=F32)
        o = o_ref[...]
        p = jnp.exp2(s * (ATTN_SCALE * LOG2_E) - o[:, 0:1] * LOG2_E)
        dob = d_o.astype(BF16)
        dp = lax.dot_general(dob, v, NT, preferred_element_type=F32)
        dd = jnp.sum(d_o * o, axis=-1, keepdims=True)
        ds = (p * (dp - dd) * ATTN_SCALE).astype(BF16)
        dq = lax.dot_general(ds, k, NN, preferred_element_type=F32)
        dq_ref[...] = _rope_t(dq, tq_ref[...], inverse=True).astype(dq_ref.dtype)
        dk[...] += lax.dot_general(q, ds, TN, preferred_element_type=F32)
        dv[...] += lax.dot_general(dob, p.astype(BF16), TN, preferred_element_type=F32)

        @pl.when(i == nq - 1)
        def _():
            dkh = dk[...].T
            lane = lax.broadcasted_iota(jnp.int32, dkh.shape, 1)
            dkv_ref[...] = jnp.where(lane < 64, dkh, dv[...].T).astype(dkv_ref.dtype)
            rot = _rope_t(jnp.where((lane >= 64) & (lane < 96), dkh, 0.0), tk_ref[...], inverse=True)

            @pl.when(h == 0)
            def _():
                dkr_ref[...] = rot

            @pl.when(h > 0)
            def _():
                dkr_ref[...] += rot

    qs, keys, kr, tab_q, tab_k = _attn_specs(tq, TT)
    wide = lambda rows: jax.ShapeDtypeStruct((rows, N_HEADS * HEAD_PAD), BF16)
    return pl.pallas_call(
        body, grid=(N_HEADS, nq),
        in_specs=[qs, keys, kr, tab_q, tab_k, qs, qs],
        out_specs=[qs, keys, pl.BlockSpec((TT, HEAD_PAD), lambda h, i: (0, 0))],
        out_shape=[wide(T), wide(TT), jax.ShapeDtypeStruct((TT, HEAD_PAD), F32)],
        scratch_shapes=[pltpu.VMEM((TT, HEAD_PAD), BF16), pltpu.VMEM((TT, HEAD_PAD), BF16),
                        pltpu.VMEM((HEAD_PAD, TT), F32), pltpu.VMEM((HEAD_PAD, TT), F32)],
        compiler_params=_cp(("arbitrary", "arbitrary")), name="attn_bwd",
    )(*_in_hbm([q_raw, kv, pp, tab, tab, o, do]))


def _hbm_specs(n):
    return [pl.BlockSpec(memory_space=pl.ANY)] * n


def _gather_weights(shards):
    n = len(shards)
    halves = [s.shape[0] // 2 for s in shards]

    def body(*refs):
        ins, outs = refs[:n], refs[n:2 * n]
        token, send_sems, recv_sems = refs[2 * n:]
        token[...] = jnp.zeros_like(token)
        mx, my, mc = lax.axis_index("x"), lax.axis_index("y"), lax.axis_index("c")
        j_me = 2 * mx + my
        chips = [(1 - mx, my), (mx, 1 - my), (1 - mx, 1 - my)]

        def half(w, chip_idx, hc):
            return outs[w].at[chip_idx, pl.ds(hc * halves[w], halves[w]), :]

        def copy(w, k, src, dst, to):
            return pltpu.make_async_remote_copy(src_ref=src, dst_ref=dst, send_sem=send_sems.at[w, k],
                                                recv_sem=recv_sems.at[w, k], device_id=to, device_id_type=MESH)

        sends = []
        for w in range(n):
            cp = copy(w, 6, ins[w], outs[w].at[j_me], (mx, my, 1 - mc))
            cp.start()
            sends.append(cp)
        for k, (px, py) in enumerate(chips):
            for w in range(n):
                cp = copy(w, k, ins[w].at[pl.ds(mc * halves[w], halves[w]), :], half(w, j_me, mc), (px, py, mc))
                cp.start()
                sends.append(cp)
        for k, (px, py) in enumerate(chips):
            for w in range(n):
                got = half(w, 2 * px + py, mc)
                copy(w, k, got, got, (px, py, mc)).wait_recv()
                cp = copy(w, 3 + k, got, got, (mx, my, 1 - mc))
                cp.start()
                sends.append(cp)
        for k, (px, py) in enumerate(chips):
            for w in range(n):
                got = half(w, 2 * px + py, 1 - mc)
                copy(w, 3 + k, got, got, (mx, my, 1 - mc)).wait_recv()
        for w in range(n):
            own = outs[w].at[j_me]
            copy(w, 6, own, own, (mx, my, 1 - mc)).wait_recv()
        for cp in sends:
            cp.wait_send()

    res = pl.pallas_call(
        body, out_shape=[jax.ShapeDtypeStruct((4,) + s.shape, s.dtype) for s in shards]
        + [jax.ShapeDtypeStruct((8, 128), F32)],
        in_specs=_hbm_specs(n), out_specs=_hbm_specs(n) + [pl.BlockSpec(memory_space=pltpu.VMEM)],
        scratch_shapes=[pltpu.SemaphoreType.DMA((n, 7)), pltpu.SemaphoreType.DMA((n, 7))],
        name="gather_weights")(*shards)
    return list(res[:n]), res[n]


def _rs_pair(gs, name):
    n = len(gs)
    halves = [g.shape[1] // 2 for g in gs]

    def body(*refs):
        ins, lands = refs[:n], refs[n:2 * n]
        send_sems, recv_sems = refs[2 * n:]
        mx, my, mc = lax.axis_index("x"), lax.axis_index("y"), lax.axis_index("c")
        copies = []
        for w in range(n):
            h = halves[w]
            cp = pltpu.make_async_remote_copy(
                src_ref=ins[w].at[:, pl.ds((1 - mc) * h, h), :], dst_ref=lands[w], send_sem=send_sems.at[w],
                recv_sem=recv_sems.at[w], device_id=(mx, my, 1 - mc), device_id_type=MESH)
            cp.start()
            copies.append(cp)
        for cp in copies:
            cp.wait()

    return pl.pallas_call(
        body, out_shape=[jax.ShapeDtypeStruct((4, h, g.shape[2]), g.dtype) for g, h in zip(gs, halves)],
        in_specs=_hbm_specs(n), out_specs=_hbm_specs(n),
        scratch_shapes=[pltpu.SemaphoreType.DMA((n,)), pltpu.SemaphoreType.DMA((n,))], name=name)(*gs)


def _rs_chips(parts):
    n = len(parts)

    def body(*refs):
        ins, lands = refs[:n], refs[n:2 * n]
        send_sems, recv_sems = refs[2 * n:]
        mx, my, mc = lax.axis_index("x"), lax.axis_index("y"), lax.axis_index("c")
        copies = []
        for k, (px, py) in enumerate([(1 - mx, my), (mx, 1 - my), (1 - mx, 1 - my)]):
            for w in range(n):
                cp = pltpu.make_async_remote_copy(
                    src_ref=ins[w].at[2 * px + py], dst_ref=lands[w].at[k], send_sem=send_sems.at[w, k],
                    recv_sem=recv_sems.at[w, k], device_id=(px, py, mc), device_id_type=MESH)
                cp.start()
                copies.append(cp)
        for cp in copies:
            cp.wait()

    return list(pl.pallas_call(
        body, out_shape=[jax.ShapeDtypeStruct((3,) + p.shape[1:], p.dtype) for p in parts],
        in_specs=_hbm_specs(n), out_specs=_hbm_specs(n),
        scratch_shapes=[pltpu.SemaphoreType.DMA((n, 3)), pltpu.SemaphoreType.DMA((n, 3))], name="rs_chips")(*parts))


_HBM = pl.BlockSpec(memory_space=pltpu.HBM)
_SEM = pl.BlockSpec(memory_space=pltpu.SEMAPHORE)
_EFFECT = pltpu.SideEffectType.DATAFLOW_SIDE_EFFECTING


def _ici_copies(kind, srcs, lands, send_sems, recv_sems):
    n = len(lands)
    mx, my, mc = lax.axis_index("x"), lax.axis_index("y"), lax.axis_index("c")
    j_me = 2 * mx + my
    copies = []
    if kind == "back":
        for w in range(n):
            h = lands[w].shape[0] // 2
            mine = lands[w].at[pl.ds(mc * h, h), :]
            copies.append(pltpu.make_async_remote_copy(
                src_ref=mine, dst_ref=mine, send_sem=send_sems.at[w], recv_sem=recv_sems.at[w],
                device_id=(mx, my, 1 - mc), device_id_type=MESH))
        return copies
    if kind == "all":
        for k in range(7):
            a, b, c = (k + 1) >> 2 & 1, (k + 1) >> 1 & 1, (k + 1) & 1
            peer = (1 - mx if a else mx, 1 - my if b else my, 1 - mc if c else mc)
            for w in range(n):
                copies.append(pltpu.make_async_remote_copy(
                    src_ref=srcs[w], dst_ref=lands[w].at[4 * mx + 2 * my + mc], send_sem=send_sems.at[7 * w + k],
                    recv_sem=recv_sems.at[7 * w + k], device_id=peer, device_id_type=MESH))
        return copies
    if kind == "pair":
        for w in range(n):
            h = srcs[w].shape[1] // 2
            copies.append(pltpu.make_async_remote_copy(
                src_ref=srcs[w].at[:, pl.ds((1 - mc) * h, h), :], dst_ref=lands[w], send_sem=send_sems.at[w],
                recv_sem=recv_sems.at[w], device_id=(mx, my, 1 - mc), device_id_type=MESH))
        return copies
    chips = [(1 - mx, my), (mx, 1 - my), (1 - mx, 1 - my)]
    if kind == "finish":
        for w in range(n):
            h = srcs[w].shape[0] // 2
            pushes = [(lands[w].at[2 * px + py, pl.ds(mc * h, h), :],) * 2 for px, py in chips]
            pushes.append((srcs[w], lands[w].at[j_me]))
            for k, (src, dst) in enumerate(pushes):
                copies.append(pltpu.make_async_remote_copy(
                    src_ref=src, dst_ref=dst, send_sem=send_sems.at[4 * w + k], recv_sem=recv_sems.at[4 * w + k],
                    device_id=(mx, my, 1 - mc), device_id_type=MESH))
        return copies
    for k, (px, py) in enumerate(chips):
        for w in range(n):
            if kind == "gather":
                h = srcs[w].shape[0] // 2
                src, dst = srcs[w].at[pl.ds(mc * h, h), :], lands[w].at[j_me, pl.ds(mc * h, h), :]
            else:
                src, dst = srcs[w].at[2 * px + py], lands[w].at[k]
            copies.append(pltpu.make_async_remote_copy(
                src_ref=src, dst_ref=dst, send_sem=send_sems.at[3 * w + k], recv_sem=recv_sems.at[3 * w + k],
                device_id=(px, py, mc), device_id_type=MESH))
    return copies


_SEMS_PER_OPERAND = {"gather": 3, "scatter": 3, "all": 7, "pair": 1, "finish": 4, "back": 1}


def _ici_start(kind, srcs, land_shapes, carry, name, lands=None):
    hbm = lambda a: pltpu.with_memory_space_constraint(a, pltpu.HBM)
    if lands is None:
        lands = [lax.empty(s, srcs[0].dtype) for s in land_shapes]
    ns, nl = len(srcs), len(lands)

    def body(*refs):
        send_sems, recv_sems = refs[ns + nl + 1], refs[ns + nl + 2]
        for cp in _ici_copies(kind, refs[:ns], refs[ns:ns + nl], send_sems, recv_sems):
            cp.start()

    args = [hbm(a) for a in list(srcs) + list(lands) + [carry]]
    n_sem = _SEMS_PER_OPERAND[kind] * nl
    out_shape = ([pltpu.SemaphoreType.DMA((n_sem,)), pltpu.SemaphoreType.DMA((n_sem,))]
                 + [pltpu.HBM(a.shape, a.dtype) for a in args])
    res = pl.pallas_call(
        body, name=name, out_shape=out_shape, in_specs=[_HBM] * len(args), out_specs=[_SEM, _SEM] + [_HBM] * len(args),
        input_output_aliases={i: 2 + i for i in range(len(args))},
        compiler_params=pltpu.CompilerParams(has_side_effects=_EFFECT))(*args)
    return res[0], res[1], list(res[2:2 + ns]), list(res[2 + ns:2 + ns + nl]), res[2 + ns + nl]


def _ici_wait(kind, send_sems, recv_sems, srcs, lands, after, name):
    ns, nl = len(srcs), len(lands)

    def body(*refs):
        for cp in _ici_copies(kind, refs[:ns], refs[ns:ns + nl], refs[ns + nl], refs[ns + nl + 1]):
            cp.wait_send()
            cp.wait_recv()

    args = list(srcs) + list(lands)
    res = pl.pallas_call(
        body, name=name, out_shape=[pltpu.HBM(a.shape, a.dtype) for a in args],
        in_specs=[_HBM] * len(args) + [_SEM, _SEM, pl.BlockSpec(memory_space=pl.ANY)], out_specs=[_HBM] * len(args),
        input_output_aliases={i: i for i in range(len(args))},
        compiler_params=pltpu.CompilerParams(has_side_effects=_EFFECT))(*args, send_sems, recv_sems, after)
    return list(res[:ns]), list(res[ns:])


def _tile_rows(h, c, itemsize, mult):
    best = h
    for t in range(mult, h + 1, mult):
        if h % t == 0 and t * c * itemsize <= (1 << 21):
            best = t
    return best


def _add_pair(g, land, place, name):
    _, h, c = land.shape
    t = _tile_rows(h, c, 2, 16)
    nb = h // t
    return _ew(lambda ids, u, v: (u.astype(F32) + v.astype(F32),), (4, nb),
               [(g, pl.BlockSpec((None, t, c), lambda j, i, s: (j, s[1] * nb + i, 0))),
                (land, pl.BlockSpec((None, t, c), lambda j, i, s: (j, i, 0)))],
               [(land.shape, BF16, pl.BlockSpec((None, t, c), lambda j, i, s: (j, i, 0)), None)], name, scalars=place)[0]


def _add_pair_many(gs, lands, place, name):
    ins, outs = [], []
    for g, l in zip(gs, lands):
        ins += [(g, pl.BlockSpec(l.shape, lambda i, s: (0, s[1], 0))), (l, pl.BlockSpec(l.shape, lambda i, s: (0, 0, 0)))]
        outs.append((l.shape, BF16, pl.BlockSpec(l.shape, lambda i, s: (0, 0, 0)), None))
    fn = lambda ids, *v: [v[2 * k].astype(F32) + v[2 * k + 1].astype(F32) for k in range(len(gs))]
    return list(_ew(fn, (1,), ins, outs, name, scalars=place))


def _add_chips_many(owns, lands, place, name):
    ins, outs = [], []
    for own, land in zip(owns, lands):
        _, h, c = land.shape
        ins += [(own, pl.BlockSpec((None, h, c), lambda i, s: (s[0], 0, 0))),
                (land, pl.BlockSpec((3, h, c), lambda i, s: (0, 0, 0)))]
        outs.append(((2 * h, c), F32, pl.BlockSpec((h, c), lambda i, s: (s[1], 0)), None))

    def fn(ids, *v):
        return [((v[2 * k].astype(F32) + v[2 * k + 1][0].astype(F32)) + v[2 * k + 1][1].astype(F32))
                + v[2 * k + 1][2].astype(F32) for k in range(len(owns))]

    return list(_ew(fn, (1,), ins, outs, name, scalars=place))


def _add_chips(own, land, place, name):
    _, h, c = land.shape
    t = _tile_rows(h, c, 4, 16)
    nb = h // t

    def fn(ids, a, b):
        return (((a.astype(F32) + b[0].astype(F32)) + b[1].astype(F32)) + b[2].astype(F32),)

    return _ew(fn, (nb,), [(own, pl.BlockSpec((None, t, c), lambda i, s: (s[0], i, 0))),
                           (land, pl.BlockSpec((3, t, c), lambda i, s: (0, i, 0)))],
               [((2 * h, c), F32, pl.BlockSpec((t, c), lambda i, s: (s[1] * nb + i, 0)), None)], name, scalars=place)[0]


W_IN_SEGMENTS = ((0, 256, KV0), (256, 288, KR0 + 64), (288, 672, Q0), (672, 1184, CX0), (1184, 1696, CB0),
                 (1696, 2208, CC0), (2208, 3232, GA0), (3232, 4256, GC0))
W_IN_SHARD = 1064


W_IN_SHARD_PAD = 1088


def _w_in_t_p_from_shards(s):
    pieces = []
    for o0, o1, p0 in sorted(W_IN_SEGMENTS, key=lambda t: t[2]):
        if p0 == KR0 + 64:
            pieces.append(jnp.zeros((64, s.shape[2]), s.dtype))
        for j in range(4):
            lo, hi = max(o0, j * W_IN_SHARD), min(o1, (j + 1) * W_IN_SHARD)
            if lo < hi:
                pieces.append(s[j, lo - j * W_IN_SHARD:hi - j * W_IN_SHARD])
    pieces.append(jnp.zeros((32, s.shape[2]), s.dtype))
    return jnp.concatenate(pieces, axis=0)


def _w_in_t_shards_from_p(g):
    shards = []
    for j in range(4):
        pieces = []
        for o0, o1, p0 in W_IN_SEGMENTS:
            lo, hi = max(o0, j * W_IN_SHARD), min(o1, (j + 1) * W_IN_SHARD)
            if lo < hi:
                pieces.append(g[p0 + lo - o0:p0 + hi - o0])
        pieces.append(jnp.zeros((W_IN_SHARD_PAD - W_IN_SHARD, g.shape[1]), g.dtype))
        shards.append(jnp.concatenate(pieces, axis=0))
    return jnp.stack(shards, axis=0)


def _cols_from_shards(s):
    return jnp.transpose(s, (1, 0, 2)).reshape(s.shape[1], -1)


def _rope_tables(T, TT, inverse):
    f32 = np.float32
    rows = T // GRID_W
    row = np.repeat(np.arange(rows), GRID_W).astype(f32)
    col = np.tile(np.arange(GRID_W), rows).astype(f32)
    inv = (f32(ROPE_THETA) ** (-np.arange(0, 16, 2, dtype=f32) / f32(16))).astype(f32)
    ang = np.concatenate([row[:, None] * inv, col[:, None] * inv], axis=-1).astype(f32)
    cos, sin = np.cos(ang).astype(f32), np.sin(ang).astype(f32)
    lane = np.arange(32)
    src = (lane // 16) * 8 + lane % 8
    lo = ((lane % 16) // 8 == 0).astype(f32)
    sgn = f32(-1.0 if inverse else 1.0)
    cos32 = cos[:, src]
    sin_lo32 = -sgn * sin[:, src] * lo
    sin_hi32 = sgn * sin[:, src] * (1 - lo)

    def widen(t32, fill):
        t = np.concatenate([np.full((T, 64), fill, f32), t32, np.full((T, 32), fill, f32)], axis=1)
        return np.concatenate([t, np.full((TT - T, HEAD_PAD), fill, f32)], axis=0)

    return [widen(cos32, 1.0), widen(sin_lo32, 0.0), widen(sin_hi32, 0.0)]


def _rope_table(T, TT):
    return jnp.asarray(np.concatenate(_rope_tables(T, TT, False) + _rope_tables(T, TT, True), axis=1))


def _local_step(xx, tgt, mod_lat, mod_ctx, W, late_weights, early_grads, early_continue):
    TT = xx.shape[0]
    T = tgt.shape[0]
    n_lat, n_all = T // ROW_TILE, TT // ROW_TILE
    sh1, sc1, g1, sh2, sc2, g2 = [mod_lat[:, k * D_MODEL:(k + 1) * D_MODEL] for k in range(6)]
    csh1, csc1 = mod_ctx[:, :D_MODEL], mod_ctx[:, D_MODEL:2 * D_MODEL]
    vec = lambda n: _full((1, n))
    row_out = lambda n, dt, rows=T: ((rows, n), dt, _rows(n), None)
    acc_out = lambda n: ((1, n), F32, _full((1, n)), 0)

    def f_norm1(ids, x, g, a_sh, a_sc, b_sh, b_sc):
        ctx = ids[0] >= n_lat
        sh, sc = jnp.where(ctx, b_sh, a_sh), jnp.where(ctx, b_sc, a_sc)
        return ((x * _rms(x) * g) * (1.0 + sc) + sh,)

    (hh,) = _ew(f_norm1, (n_all,), [(xx, _rows(D_MODEL)), (W["norm1_g"], vec(D_MODEL)), (sh1, vec(D_MODEL)),
                                   (sc1, vec(D_MODEL)), (csh1, vec(D_MODEL)), (csc1, vec(D_MODEL))],
                [row_out(D_MODEL, BF16, TT)], "norm1_fwd")
    tm_all = _pick(TT, (768, 256))
    pp = _mm(hh, W["w_in_t"], "nt", TT, P_COLS, D_MODEL, tm=tm_all, tn=2176, tk=D_MODEL, name="w_in_fwd")

    def f_lowrank(ids, ckv, cq, gkv, gq):
        return ckv * _rms(ckv) * gkv, cq * _rms(cq) * gq

    nkv, nq = _ew(f_lowrank, (n_all,), [(pp, _rows(KV_RANK, KV0 // KV_RANK)), (pp, _rows(Q_RANK, Q0 // Q_RANK)),
                                       (W["kv_norm_g"], vec(KV_RANK)), (W["q_norm_g"], vec(Q_RANK))],
                  [row_out(KV_RANK, BF16, TT), row_out(Q_RANK, BF16, TT)], "lowrank_norm_fwd")
    kv = _mm(nkv, W["w_ukv"], "nn", TT, 1024, KV_RANK, tm=tm_all, tn=256, tk=KV_RANK, name="w_ukv_fwd",
             b_spec=pl.BlockSpec((None, KV_RANK, 256), lambda i, j, k: (j, k, 0)))
    q_raw = _mm(nq, W["w_uq_t"], "nt", TT, 1024, Q_RANK, tm=tm_all, tn=1024, tk=Q_RANK, name="w_uq_fwd")

    tab = _rope_table(T, TT)
    _, q_raw = late_weights("before_attn", q_raw)
    o_pad = _attn_fwd(q_raw, kv, pp, tab, T, TT)
    arrived, o_pad = late_weights("after_attn", o_pad)
    W = dict(W, **arrived)
    tm_lat = _pick(T, (1024, 512, 256))
    ya = _mm(o_pad, W["w_attn_out"], "nn", T, D_MODEL, 1024, tm=tm_lat, tn=D_MODEL, tk=1024, name="w_attn_out_fwd",
             out_dtype=BF16)

    tc = 256
    colT = lambda blk0: pl.BlockSpec((T, tc), lambda j: (0, blk0 + j))

    def f_conv(ids, xin, cb, cc, w, b):
        return (cb * _conv(cc * xin, w, b),)

    (e,) = _ew(f_conv, (CONV_DIM // tc,),
               [(pp, colT(CX0 // tc)), (pp, colT(CB0 // tc)), (pp, colT(CC0 // tc)),
                (W["conv_w"], pl.BlockSpec((3, tc), lambda j: (0, j))), (W["conv_b"], pl.BlockSpec((1, tc), lambda j: (0, j)))],
               [((T, CONV_DIM), BF16, colT(0), None)], "conv_fwd")
    yc = _mm(e, W["w_conv_out"], "nn", T, D_MODEL, CONV_DIM, tm=tm_lat, tn=256, tk=CONV_DIM, name="w_conv_out_fwd",
             out_dtype=BF16, b_spec=pl.BlockSpec((None, CONV_DIM, 256), lambda i, j, k: (j, k, 0)))

    def f_merge(ids, ga, gc, a, c):
        return (_sigmoid(ga) * a.astype(F32) + _sigmoid(gc) * c.astype(F32),)

    (mrg,) = _ew(f_merge, (n_lat,), [(pp, _rows(D_MODEL, 0)), (pp, _rows(D_MODEL, 1)), (ya, _rows(D_MODEL)),
                                    (yc, _rows(D_MODEL))], [row_out(D_MODEL, BF16)], "merge_fwd")
    mo = _mm(mrg, W["w_o"], "nn", T, D_MODEL, D_MODEL, tm=tm_lat, tn=D_MODEL, tk=D_MODEL, name="w_o_fwd")

    def f_norm2(ids, x, m, gate, g, sh, sc):
        x1 = x + gate * m
        return x1, (x1 * _rms(x1) * g) * (1.0 + sc) + sh

    x1, h2 = _ew(f_norm2, (n_lat,), [(xx, _rows(D_MODEL)), (mo, _rows(D_MODEL)), (g1, vec(D_MODEL)),
                                    (W["norm2_g"], vec(D_MODEL)), (sh2, vec(D_MODEL)), (sc2, vec(D_MODEL))],
                 [row_out(D_MODEL, F32), row_out(D_MODEL, BF16)], "norm2_fwd")
    arrived, h2 = late_weights("before_ffn", h2)
    W = dict(W, **arrived)
    up = _mm(h2, W["w_up"], "nn", T, 2 * D_FF, D_MODEL, tm=tm_lat, tn=1408, tk=D_MODEL, name="w_up_fwd",
             b_spec=pl.BlockSpec((None, D_MODEL, 1408), lambda i, j, k: (j, k, 0)))

    n_ff = D_FF // tc
    ffw = lambda off, n=3: pl.BlockSpec((n, tc), lambda j: (0, j + off))

    def f_ffn(ids, ug, uv, wg, wv, bg, bv):
        gate, val = _conv(ug, wg, bg), _conv(uv, wv, bv)
        return (gate * _sigmoid(gate) * val,)

    (act,) = _ew(f_ffn, (n_ff,), [(up, colT(0)), (up, colT(n_ff)), (W["ffn_conv_w"], ffw(0)), (W["ffn_conv_w"], ffw(n_ff)),
                                 (W["ffn_conv_b"], ffw(0, 1)), (W["ffn_conv_b"], ffw(n_ff, 1))],
                 [((T, D_FF), BF16, colT(0), None)], "ffn_act_fwd")
    f = _mm(act, W["w_down"], "nn", T, D_MODEL, D_FF, tm=tm_lat, tn=D_MODEL, tk=D_FF, name="w_down_fwd")

    def f_head(ids, x1_, f_, gate, gf, t):
        x2 = x1_ + gate * f_
        r = _rms(x2)
        xn = x2 * r
        err = xn * gf - t
        loss = 0.5 * jnp.sum(jnp.mean(err * err, axis=-1, keepdims=True))
        dy = err * (1.0 / D_MODEL)
        dx2 = _rms_bwd(dy * gf, xn, r)
        return dx2, dx2 * gate, _colsum(dy * xn), _colsum(dx2 * f_), jnp.full((1, 128), loss, F32)

    dx2, df, dg_f, dg2, loss = _ew(
        f_head, (n_lat,), [(x1, _rows(D_MODEL)), (f, _rows(D_MODEL)), (g2, vec(D_MODEL)), (W["final_g"], vec(D_MODEL)),
                           (tgt, _rows(D_MODEL))],
        [row_out(D_MODEL, F32), row_out(D_MODEL, BF16), acc_out(D_MODEL), acc_out(D_MODEL), acc_out(128)], "loss_head")

    d_w_down = _mm(act, df, "tn", D_FF, D_MODEL, T, tm=1408, tn=D_MODEL, tk=T, name="w_down_dw",
                   out_dtype=BF16).reshape(4, D_FF // 4, D_MODEL)
    da = _mm(df, W["w_down"], "nt", T, D_FF, D_MODEL, tm=tm_lat, tn=1408, tk=D_MODEL, name="w_down_dx")

    tcb = 128
    n_fb = D_FF // tcb
    colb = lambda blk0: pl.BlockSpec((T, tcb), lambda j: (0, blk0 + j))
    ffwb = lambda off, n=3: pl.BlockSpec((n, tcb), lambda j: (0, j + off))
    cvec = ((1, D_FF), F32, pl.BlockSpec((1, tcb), lambda j: (0, j)), None)

    def f_ffn_bwd(ids, ug, uv, d_act, wg, wv, bg, bv):
        sg, sv = _shifts(ug), _shifts(uv)
        gate, val = _conv(ug, wg, bg, sg), _conv(uv, wv, bv, sv)
        s = _sigmoid(gate)
        d_gate = d_act * val * s * (1.0 + gate * (1.0 - s))
        d_val = d_act * gate * s
        wg0, wg1, wg2 = _conv_bwd_w(d_gate, ug, sg)
        wv0, wv1, wv2 = _conv_bwd_w(d_val, uv, sv)
        d_up = [_conv_bwd_x(d_gate, wg), _conv_bwd_x(d_val, wv)]
        return d_up, [_colsum(d_gate), _colsum(d_val), wg0, wg1, wg2, wv0, wv1, wv2]

    d_up3, ffn_stats = _ew(
        f_ffn_bwd, (n_fb,),
        [(up, colb(0)), (up, colb(n_fb)), (da, colb(0)), (W["ffn_conv_w"], ffwb(0)), (W["ffn_conv_w"], ffwb(n_fb)),
         (W["ffn_conv_b"], ffwb(0, 1)), (W["ffn_conv_b"], ffwb(n_fb, 1))],
        [((2, T, D_FF), BF16, pl.BlockSpec((2, T, tcb), lambda j: (0, 0, j)), None),
         ((n_fb, 8, 1, tcb), F32, pl.BlockSpec((None, 8, 1, tcb), lambda j: (j, 0, 0, 0)), None)], "ffn_act_bwd")
    stat = lambda s: ffn_stats[:, s, 0, :].reshape(1, D_FF)
    d_ffn_conv_b = jnp.concatenate([stat(0), stat(1)], axis=1)
    d_ffn_conv_w = jnp.concatenate([jnp.concatenate([stat(2), stat(3), stat(4)], axis=0),
                                    jnp.concatenate([stat(5), stat(6), stat(7)], axis=0)], axis=1)

    tk_t = T
    d_w_up = _mm(h2, d_up3, "tn", D_MODEL, 2 * D_FF, T, tm=D_MODEL, tn=1408, tk=tk_t, name="w_up_dw", out_dtype=BF16,
                 b_spec=pl.BlockSpec((None, tk_t, 1408), lambda i, j, k: (j // 2, k, j % 2)),
                 o_spec=pl.BlockSpec((None, D_MODEL, 1408), lambda i, j, k: (j, i, 0)), out_shape=(4, D_MODEL, 1408))
    dh2 = _mm(d_up3, W["w_up"], "nt", T, D_MODEL, 2 * D_FF, tm=tm_lat, tn=D_MODEL, tk=1408, name="w_up_dx",
              a_spec=pl.BlockSpec((None, tm_lat, 1408), lambda i, j, k: (k // 2, i, k % 2)),
              b_spec=pl.BlockSpec((None, D_MODEL, 1408), lambda i, j, k: (k, j, 0)))

    def f_norm2_bwd(ids, dx2_, dh, x1_, m, g, sc, gate):
        r = _rms(x1_)
        xn = x1_ * r
        dx1 = dx2_ + _rms_bwd(dh * g * (1.0 + sc), xn, r)
        return dx1, dx1 * gate, _colsum(dh), _colsum(dh * xn * g), _colsum(dh * xn * (1.0 + sc)), _colsum(dx1 * m)

    dx1, dmo, dsh2, dsc2, dg_n2, dg1 = _ew(
        f_norm2_bwd, (n_lat,), [(dx2, _rows(D_MODEL)), (dh2, _rows(D_MODEL)), (x1, _rows(D_MODEL)), (mo, _rows(D_MODEL)),
                                (W["norm2_g"], vec(D_MODEL)), (sc2, vec(D_MODEL)), (g1, vec(D_MODEL))],
        [row_out(D_MODEL, F32), row_out(D_MODEL, BF16)] + [acc_out(D_MODEL)] * 4, "norm2_bwd")
    d_w_o = _mm(mrg, dmo, "tn", D_MODEL, D_MODEL, T, tm=D_MODEL, tn=D_MODEL, tk=tk_t, name="w_o_dw",
                out_dtype=BF16).reshape(4, D_MODEL // 4, D_MODEL)
    dmrg = _mm(dmo, W["w_o"], "nt", T, D_MODEL, D_MODEL, tm=tm_lat, tn=D_MODEL, tk=D_MODEL, name="w_o_dx",
               out_dtype=BF16)
    dmrg = early_grads("late", {"w_o": d_w_o, "w_up": d_w_up, "w_down": d_w_down}, dmrg, split=True)

    def f_merge_bwd(ids, dm, ga, gc, a, c):
        dm, a, c = dm.astype(F32), a.astype(F32), c.astype(F32)
        sa, sc_ = _sigmoid(ga), _sigmoid(gc)
        return dm * sa, dm * sc_, dm * a * sa * (1.0 - sa), dm * c * sc_ * (1.0 - sc_)

    dya, dyc, dp_ga, dp_gc = _ew(
        f_merge_bwd, (n_lat,), [(dmrg, _rows(D_MODEL)), (pp, _rows(D_MODEL, 0)), (pp, _rows(D_MODEL, 1)),
                                (ya, _rows(D_MODEL)), (yc, _rows(D_MODEL))], [row_out(D_MODEL, BF16)] * 4, "merge_bwd")
    dya = early_continue("late", dya)

    d_w_ao_p = _mm(o_pad, dya, "tn", 1024, D_MODEL, T, tm=1024, tn=D_MODEL, tk=tk_t, name="w_attn_out_dw", out_dtype=BF16)
    do_pad = _mm(dya, W["w_attn_out"], "nt", T, 1024, D_MODEL, tm=tm_lat, tn=1024, tk=D_MODEL, name="w_attn_out_dx")
    d_w_co = _mm(e, dyc, "tn", CONV_DIM, D_MODEL, T, tm=CONV_DIM, tn=256, tk=tk_t, name="w_conv_out_dw", out_dtype=BF16,
                 o_spec=pl.BlockSpec((None, CONV_DIM, 256), lambda i, j, k: (j, i, 0)), out_shape=(4, CONV_DIM, 256))
    de = _mm(dyc, W["w_conv_out"], "nt", T, CONV_DIM, D_MODEL, tm=tm_lat, tn=CONV_DIM, tk=256, name="w_conv_out_dx",
             b_spec=pl.BlockSpec((None, CONV_DIM, 256), lambda i, j, k: (k, j, 0)))

    def f_conv_bwd(ids, xin, cb, cc, d_e, w, b):
        z = cc * xin
        sz = _shifts(z)
        cz = _conv(z, w, b, sz)
        dcz = d_e * cb
        w0, w1, w2 = _conv_bwd_w(dcz, z, sz)
        dz = _conv_bwd_x(dcz, w)
        return dz * cc, d_e * cz, dz * xin, _colsum(dcz), w0, w1, w2

    cvec_c = ((1, CONV_DIM), F32, pl.BlockSpec((1, tc), lambda j: (0, j)), None)
    conv_b = _ew(f_conv_bwd, (CONV_DIM // tc,),
                 [(pp, colT(CX0 // tc)), (pp, colT(CB0 // tc)), (pp, colT(CC0 // tc)), (de, colT(0)),
                  (W["conv_w"], pl.BlockSpec((3, tc), lambda j: (0, j))), (W["conv_b"], pl.BlockSpec((1, tc), lambda j: (0, j)))],
                 [((T, CONV_DIM), BF16, colT(0), None)] * 3 + [cvec_c] * 4, "conv_bwd")
    dp_cx, dp_cb, dp_cc, d_conv_b = conv_b[:4]
    d_conv_w = jnp.concatenate(conv_b[4:7], axis=0)

    dq_raw, dkv, dp_kr = _attn_bwd(q_raw, kv, pp, o_pad, do_pad, tab, T, TT)

    tk_a = TT
    d_w_uq_t = _mm(nq, dq_raw, "tn", Q_RANK, 1024, T, tm=Q_RANK, tn=1024, tk=T, name="w_uq_dw", transpose_out=True)
    dnq = _mm(dq_raw, W["w_uq_t"], "nn", T, Q_RANK, 1024, tm=tm_lat, tn=Q_RANK, tk=1024, name="w_uq_dx")
    d_w_ukv = _mm(nkv, dkv, "tn", KV_RANK, 1024, TT, tm=KV_RANK, tn=256, tk=tk_a, name="w_ukv_dw", out_dtype=BF16,
                  o_spec=pl.BlockSpec((None, KV_RANK, 256), lambda i, j, k: (j, i, 0)), out_shape=(4, KV_RANK, 256))
    dnkv = _mm(dkv, W["w_ukv"], "nt", TT, KV_RANK, 1024, tm=tm_all, tn=KV_RANK, tk=256, name="w_ukv_dx",
               b_spec=pl.BlockSpec((None, KV_RANK, 256), lambda i, j, k: (k, j, 0)))
    dnkv = early_grads("mid", {
        "w_attn_out": jnp.transpose(d_w_ao_p.reshape(N_HEADS, HEAD_PAD, 4, 256)[:, 64:], (2, 0, 1, 3)).reshape(
            4, N_HEADS * 64, 256),
        "w_conv_out": d_w_co,
        "w_uq": d_w_uq_t.reshape(4, 2, HEAD_PAD, Q_RANK)[:, :, :QK_DIM].reshape(4, 2 * QK_DIM, Q_RANK).astype(BF16),
        "w_ukv": d_w_ukv}, dnkv)

    def f_lowrank_bwd(ids, ckv, cq, dkv_, dq_, gkv, gq, ga, gc, cx, cb, cc, kr):
        rk, rq = _rms(ckv), _rms(cq)
        nk, nq_ = ckv * rk, cq * rq
        lat = ids[0] < n_lat
        dq_ = jnp.where(lat, dq_, 0.0)
        pieces = [jnp.where(lat, a, jnp.zeros_like(a)) for a in (ga, gc, cx, cb, cc)]
        pieces += [_rms_bwd(dkv_ * gkv, nk, rk).astype(BF16), _rms_bwd(dq_ * gq, nq_, rq).astype(BF16), kr.astype(BF16)]
        return jnp.concatenate(pieces, axis=1), _colsum(dkv_ * nk), _colsum(dq_ * nq_)

    lat_rows = lambda n: pl.BlockSpec((ROW_TILE, n), lambda i: (jnp.minimum(i, n_lat - 1), 0))
    dpp, dg_kv, dg_q = _ew(
        f_lowrank_bwd, (n_all,), [(pp, _rows(KV_RANK, KV0 // KV_RANK)), (pp, _rows(Q_RANK, Q0 // Q_RANK)),
                                  (dnkv, _rows(KV_RANK)), (dnq, lat_rows(Q_RANK)), (W["kv_norm_g"], vec(KV_RANK)),
                                  (W["q_norm_g"], vec(Q_RANK)), (dp_ga, lat_rows(D_MODEL)), (dp_gc, lat_rows(D_MODEL)),
                                  (dp_cx, lat_rows(CONV_DIM)), (dp_cb, lat_rows(CONV_DIM)), (dp_cc, lat_rows(CONV_DIM)),
                                  (dp_kr, _rows(HEAD_PAD))],
        [row_out(P_COLS, BF16, TT), acc_out(KV_RANK), acc_out(Q_RANK)], "lowrank_norm_bwd")
    d_w_in_t = _mm(hh, dpp, "tn", D_MODEL, P_COLS, TT, tm=512, tn=2176, tk=TT, name="w_in_dw", out_dtype=BF16,
                   transpose_out=True)
    dhh = _mm(dpp, W["w_in_t"], "nn", TT, D_MODEL, P_COLS, tm=tm_all, tn=512, tk=2176, name="w_in_dx")

    def f_norm1_bwd(ids, x, dh, dres, g, sc):
        r = _rms(x)
        xn = x * r
        return (dres + _rms_bwd(dh * g * (1.0 + sc), xn, r), _colsum(dh), _colsum(dh * xn * g),
                _colsum(dh * xn * (1.0 + sc)))

    grad_x, dsh1, dsc1, dg_n1 = _ew(
        f_norm1_bwd, (n_lat,), [(xx, _rows(D_MODEL)), (dhh, _rows(D_MODEL)), (dx1, _rows(D_MODEL)),
                                (W["norm1_g"], vec(D_MODEL)), (sc1, vec(D_MODEL))],
        [row_out(D_MODEL, F32)] + [acc_out(D_MODEL)] * 3, "norm1_bwd")

    def f_norm1_ctx_bwd(ids, x, dh, g, sc):
        xn = x * _rms(x)
        return _colsum(dh), _colsum(dh * xn * g), _colsum(dh * xn * (1.0 + sc))

    n_ctx = n_all - n_lat
    dcsh1, dcsc1, dg_n1c = _ew(
        f_norm1_ctx_bwd, (n_ctx,), [(xx, _rows(D_MODEL, 0, n_lat)), (dhh, _rows(D_MODEL, 0, n_lat)),
                                    (W["norm1_g"], vec(D_MODEL)), (csc1, vec(D_MODEL))], [acc_out(D_MODEL)] * 3,
        "norm1_ctx_bwd")

    big = {"w_in": _w_in_t_shards_from_p(d_w_in_t).astype(BF16)}
    zero = jnp.zeros((1, 4 * D_MODEL), F32)
    small = {
        "dmod_lat": jnp.concatenate([dsh1, dsc1, dg1, dsh2, dsc2, dg2], axis=1),
        "dmod_ctx": jnp.concatenate([dcsh1, dcsc1, zero], axis=1),
        "norm1_g": dg_n1 + dg_n1c, "norm2_g": dg_n2, "final_g": dg_f, "q_norm_g": dg_q, "kv_norm_g": dg_kv,
        "conv_b": d_conv_b, "conv_w": d_conv_w.reshape(1, -1), "ffn_conv_b": d_ffn_conv_b,
        "ffn_conv_w": d_ffn_conv_w.reshape(1, -1),
    }
    return grad_x, loss, big, small


SMALL = (("dmod_lat", 6144), ("dmod_ctx", 6144), ("norm1_g", 1024), ("norm2_g", 1024), ("final_g", 1024),
         ("q_norm_g", 384), ("kv_norm_g", 256), ("conv_b", 512), ("conv_w", 1536), ("ffn_conv_b", 5632),
         ("ffn_conv_w", 16896), ("loss", 128))
SMALL_ROWS = 320


def _adam_update(w, g, m, v):
    c1, c2 = 1.0 - ADAM_B1 ** ADAM_STEP, 1.0 - ADAM_B2 ** ADAM_STEP
    m2 = ADAM_B1 * m + (1.0 - ADAM_B1) * g
    v2 = ADAM_B2 * v + (1.0 - ADAM_B2) * (g * g)
    return [-ADAM_LR * ((m2 / c1) / (jnp.sqrt(v2 / c2) + ADAM_EPS) + ADAM_WD * w), m2, v2]


def _adamw(w, g, m, v, name):
    R, C = w.shape
    tr = 8 if R % 8 == 0 else R
    for t in range(8, R + 1, 8):
        if R % t == 0 and t * C * 4 <= (1 << 20):
            tr = t
    spec = pl.BlockSpec((tr, C), lambda i: (i, 0))
    return _ew(lambda ids, *vals: _adam_update(*vals), (R // tr,), [(w, spec), (g, spec), (m, spec), (v, spec)],
               [((R, C), F32, spec, None)] * 3, name)


def kernel(x, c, ctx, c_ctx, w_ada, b_ada, norm1_g, w_in, q_norm_g, kv_norm_g, w_uq, w_ukv, conv_w, conv_b, w_attn_out, w_conv_out, w_o, norm2_g, w_up, ffn_conv_w, ffn_conv_b, w_down, final_g, loss_target, m_c_ctx, m_w_ada, m_b_ada, m_norm1_g, m_w_in, m_q_norm_g, m_kv_norm_g, m_w_uq, m_w_ukv, m_conv_w, m_conv_b, m_w_attn_out, m_w_conv_out, m_w_o, m_norm2_g, m_w_up, m_ffn_conv_w, m_ffn_conv_b, m_w_down, m_final_g, v_c_ctx, v_w_ada, v_b_ada, v_norm1_g, v_w_in, v_q_norm_g, v_kv_norm_g, v_w_uq, v_w_ukv, v_conv_w, v_conv_b, v_w_attn_out, v_w_conv_out, v_w_o, v_norm2_g, v_w_up, v_ffn_conv_w, v_ffn_conv_b, v_w_down, v_final_g):
    mx, my, mc = lax.axis_index("x"), lax.axis_index("y"), lax.axis_index("c")
    chip = 2 * mx + my
    dev = 4 * mx + 2 * my + mc
    T, Tc = x.shape[1], ctx.shape[1]
    TT = T + Tc
    w_in_t, m_w_in_t, v_w_in_t = (jnp.transpose(a[0]) for a in (w_in, m_w_in, v_w_in))
    w_uq_t, m_w_uq_t, v_w_uq_t = (jnp.transpose(a[0]) for a in (w_uq, m_w_uq, v_w_uq))
    conv_sh = jnp.concatenate([conv_w[0], ffn_conv_w[0]], axis=1)
    pay1 = jnp.concatenate([jnp.pad(c, ((0, 7), (0, 0))), jnp.pad(conv_sh, ((0, 5), (0, 0)))], axis=1)
    c_send, c_recv, c_src, c_land, zero0 = _ici_start("all", [pay1], [(8, 8, 2560)], jnp.zeros((8, 128), F32),
                                                      "cond_start")
    w_in_bf = (jnp.pad(w_in_t, ((0, W_IN_SHARD_PAD - W_IN_SHARD), (0, 0))) + zero0[0, 0]).astype(BF16)
    shards = {"w_in": w_in_bf, "w_uq": w_uq_t, "w_ukv": w_ukv[0], "w_attn_out": w_attn_out[0],
              "w_conv_out": w_conv_out[0], "w_o": w_o[0], "w_up": w_up[0], "w_down": w_down[0]}
    (pay1,), (c_land,) = _ici_wait("all", c_send, c_recv, c_src, c_land, shards["w_in"], "cond_wait")
    got1 = lax.dynamic_update_slice(c_land, pay1[None], (dev, 0, 0))
    c_all = got1[:, 0, :D_MODEL]
    conv_all = got1[0::2, :3, D_MODEL:]
    conv_w_full = _cols_from_shards(conv_all[:, :, :128])
    ffn_conv_w_full = _cols_from_shards(conv_all[:, :, 128:])

    cond = jnp.concatenate([c_all, c_ctx.reshape(1, D_MODEL), jnp.zeros((7, D_MODEL), F32)], axis=0)

    def f_silu(ids, v):
        return (v * _sigmoid(v),)

    (s16,) = _ew(f_silu, (1,), [(cond, _full((16, D_MODEL)))], [((16, D_MODEL), F32, _full((16, D_MODEL)), None)], "silu_cond")
    mod_sh = _mm(s16, w_ada[0], "nn", 16, 1536, D_MODEL, tm=16, tn=768, tk=D_MODEL, name="w_ada_fwd")
    m_send, m_recv, m_src, m_land, zero1 = _ici_start("all", [mod_sh], [(8, 16, 1536)], jnp.zeros((8, 128), F32),
                                                      "mod_start")
    shards["w_ukv"] = w_ukv[0] + zero1[0, 0]

    names = [n for n, _ in BIG]
    first = [n for n in names if n not in GATHER_LATE]
    gathered, zero = _gather_weights([shards[n].astype(BF16) for n in first])
    full = dict(zip(first, gathered))
    (mod_mine,), (m_land,) = _ici_wait("all", m_send, m_recv, m_src, m_land, gathered[0], "mod_wait")
    got2 = lax.dynamic_update_slice(m_land, mod_mine[None], (dev, 0, 0))
    mod_all = _cols_from_shards(got2[0::2]) + b_ada
    mod_lat = lax.dynamic_slice_in_dim(mod_all, dev, 1, axis=0)
    mod_ctx = mod_all[8:9]
    xx = jnp.concatenate([x[0], ctx[0]], axis=0)
    late_groups = {"g1": ("w_attn_out", "w_conv_out", "w_o"), "g2": ("w_up", "w_down")}
    flight = {}
    for tag, group in late_groups.items():
        bf = [(shards[n] + zero[0, 0]).astype(BF16) for n in group]
        flight[tag] = _ici_start("gather", bf, [(4,) + s.shape for s in bf], xx, "gather_" + tag + "_start")
        xx = flight[tag][4]

    def chip_stage_done(tag, x):
        send, recv, src, land, _ = flight[tag]
        src, land = _ici_wait("gather", send, recv, src, land, x, "gather_" + tag + "_wait")
        flight[tag] = _ici_start("finish", src, None, x, "finish_" + tag + "_start", lands=land)
        return flight[tag][4]

    def arrived(tag, x):
        send, recv, src, land, _ = flight[tag]
        return dict(zip(late_groups[tag], _ici_wait("finish", send, recv, src, land, x, "finish_" + tag + "_wait")[1]))

    def late_weights(point, x):
        if point == "before_attn":
            return {}, chip_stage_done("g1", x)
        if point == "after_attn":
            got = arrived("g1", x)
            wao = _cols_from_shards(got["w_attn_out"]).reshape(N_HEADS, 64, D_MODEL)
            ready = {"w_attn_out": jnp.pad(wao, ((0, 0), (64, 0), (0, 0))).reshape(N_HEADS * HEAD_PAD, D_MODEL),
                     "w_conv_out": got["w_conv_out"], "w_o": got["w_o"].reshape(D_MODEL, D_MODEL)}
            return ready, chip_stage_done("g2", x)
        got = arrived("g2", x)
        return {"w_up": got["w_up"], "w_down": got["w_down"].reshape(D_FF, D_MODEL)}, x

    wuq_t = full["w_uq"].reshape(N_HEADS, QK_DIM, Q_RANK)
    W = {
        "w_in_t": _w_in_t_p_from_shards(full["w_in"]),
        "w_uq_t": jnp.pad(wuq_t, ((0, 0), (0, HEAD_PAD - QK_DIM), (0, 0))).reshape(N_HEADS * HEAD_PAD, Q_RANK),
        "w_ukv": full["w_ukv"],
        "norm1_g": norm1_g, "norm2_g": norm2_g, "final_g": final_g.reshape(1, D_MODEL), "q_norm_g": q_norm_g,
        "kv_norm_g": kv_norm_g, "conv_w": conv_w_full, "conv_b": conv_b, "ffn_conv_w": ffn_conv_w_full,
        "ffn_conv_b": ffn_conv_b,
    }

    place = jnp.stack([chip, mc]).astype(jnp.int32)
    early = {}

    pending = {}

    def scatter(tag, group, gs, from_sib, carry):
        if tag == "mid":
            sums = _add_pair_many(gs, from_sib, place, "rs_pair_add_mid")
        else:
            sums = [_add_pair(gs[w], from_sib[w], place, "rs_pair_add_" + n) for w, n in enumerate(group)]
        send, recv, sums, land, carry = _ici_start(
            "scatter", sums, [(3,) + s.shape[1:] for s in sums], carry, "rs_chips_" + tag + "_start")
        early[tag] = (group, send, recv, sums, land)
        return carry

    def early_grads(tag, g, carry, split=False):
        gs = list(g.values())
        if not split:
            return scatter(tag, list(g), gs, _rs_pair(gs, "rs_pair_" + tag), carry)
        send, recv, gs, land, carry = _ici_start(
            "pair", gs, [(4, s.shape[1] // 2, s.shape[2]) for s in gs], carry, "rs_pair_" + tag + "_start")
        pending[tag] = (list(g), send, recv, gs, land)
        return carry

    def early_continue(tag, carry):
        group, send, recv, gs, land = pending[tag]
        gs, from_sib = _ici_wait("pair", send, recv, gs, land, carry, "rs_pair_" + tag + "_wait")
        return scatter(tag, group, gs, from_sib, carry)

    grad_x, loss_part, gbig, gsmall = _local_step(xx, loss_target[0], mod_lat, mod_ctx, W, late_weights, early_grads,
                                                  early_continue)

    gsmall["loss"] = loss_part
    pay3 = jnp.concatenate([gsmall[n].reshape(-1) for n, _ in SMALL])
    pay3 = jnp.pad(pay3, (0, SMALL_ROWS * 128 - pay3.shape[0])).reshape(SMALL_ROWS, 128)
    s_send, s_recv, s_src, s_land, w_in_thru = _ici_start("all", [pay3], [(8, SMALL_ROWS, 128)], gbig["w_in"],
                                                         "small_start")
    gbig = {"w_in": w_in_thru}

    after_small = early_grads("last", gbig, s_src[0])

    (pay3,), (s_land,) = _ici_wait("all", s_send, s_recv, [after_small], s_land, early["last"][3][0], "small_wait")
    got3 = lax.dynamic_update_slice(s_land, pay3[None], (dev, 0, 0)).reshape(8 * SMALL_ROWS, 128)

    def f_sum8(ids, a):
        s = a[0:SMALL_ROWS]
        for d in range(1, 8):
            s = s + a[d * SMALL_ROWS:(d + 1) * SMALL_ROWS]
        return (s,)

    (vsum,) = _ew(f_sum8, (1,), [(got3, _full((8 * SMALL_ROWS, 128)))],
                  [((SMALL_ROWS, 128), F32, _full((SMALL_ROWS, 128)), None)], "sum_small")
    vflat = vsum.reshape(-1)
    gvec, off = {}, 0
    for n, size in SMALL:
        gvec[n] = vflat[off:off + size]
        off += size
    loss = gvec["loss"][0]
    dmod_rows = got3.reshape(8, SMALL_ROWS * 128)[:, :6 * D_MODEL]
    dm16 = jnp.concatenate([dmod_rows, gvec["dmod_ctx"].reshape(1, -1), jnp.zeros((7, 6 * D_MODEL), F32)], axis=0)

    def f_colsum(ids, a):
        return (_colsum(a),)

    (g_b_ada,) = _ew(f_colsum, (1,), [(dm16, _full((16, 6 * D_MODEL)))],
                     [((1, 6 * D_MODEL), F32, _full((1, 6 * D_MODEL)), None)], "b_ada_grad")
    dm_sh = lax.dynamic_slice_in_dim(dm16, chip * 1536, 1536, axis=1)
    g_w_ada = _mm(s16, dm_sh, "tn", D_MODEL, 1536, 16, tm=512, tn=768, tk=16, name="w_ada_dw")
    dcond_part = _mm(dm_sh, w_ada[0], "nt", 16, D_MODEL, 1536, tm=16, tn=512, tk=1536, name="w_ada_dx")
    d_send, d_recv, d_src, d_land, vsum = _ici_start("all", [dcond_part[8:16]], [(8, 8, D_MODEL)], vsum, "dcond_start")

    def finish_start(tags, after):
        done, halves = [], []
        for tag in tags:
            tag_names, send, recv, sums, land = early[tag]
            sums, land = _ici_wait("scatter", send, recv, sums, land, after, "rs_chips_" + tag + "_wait")
            done += tag_names
            if tag == "mid":
                halves += _add_chips_many(sums, land, place, "rs_chip_add_mid")
            else:
                halves += [_add_chips(a, b, place, "rs_chip_add_" + n) for a, b, n in zip(sums, land, tag_names)]
        send, recv, _, halves, _ = _ici_start("back", [], None, jnp.zeros((8, 128), F32), "rs_back_" + tags[0] + "_start",
                                              lands=halves)
        return done, send, recv, halves

    def finish_wait(state, after):
        done, send, recv, halves = state
        return dict(zip(done, _ici_wait("back", send, recv, [], halves, after, "rs_back_" + done[0] + "_wait")[1]))

    grads, deltas, new_m, new_v = {}, {}, {}, {}

    raw = {}

    def adam(n, w_, m_, v_, g, transposed):
        d_, m2, v2 = _adamw(w_, g, m_, v_, "adamw_" + n)
        raw[n] = d_
        back = (lambda a: jnp.transpose(a)[None]) if transposed else (lambda a: a[None])
        grads[n], deltas[n], new_m[n], new_v[n] = back(g[:w_.shape[0]]), back(d_), back(m2), back(v2)

    pending_back = finish_start(["late", "mid"], grad_x)
    adam("w_ada", w_ada[0], m_w_ada[0], v_w_ada[0], g_w_ada, False)
    gw = finish_wait(pending_back, raw["w_ada"])
    for n, (w_, m_, v_) in {"w_o": (w_o, m_w_o, v_w_o), "w_up": (w_up, m_w_up, v_w_up),
                            "w_down": (w_down, m_w_down, v_w_down)}.items():
        adam(n, w_[0], m_[0], v_[0], gw[n], False)
    pending_back = finish_start(["last"], raw["w_up"])

    (dcond_mine,), (d_land,) = _ici_wait("all", d_send, d_recv, d_src, d_land, raw["w_down"], "dcond_wait")
    got4 = lax.dynamic_update_slice(d_land, dcond_mine[None], (dev, 0, 0))[0::2, 0]

    def f_c_ctx(ids, parts, cc):
        s = _sigmoid(cc)
        d = parts[0:1] + parts[1:2] + parts[2:3] + parts[3:4]
        return (d * s * (1.0 + cc * (1.0 - s)),)

    (g_c_ctx,) = _ew(f_c_ctx, (1,), [(got4, _full((4, D_MODEL))), (c_ctx.reshape(1, D_MODEL), _full((1, D_MODEL)))],
                     [((1, D_MODEL), F32, _full((1, D_MODEL)), None)], "c_ctx_grad")

    conv_w_g = lax.dynamic_slice_in_dim(gvec["conv_w"].reshape(3, CONV_DIM), chip * 128, 128, axis=1)
    ffn_conv_w_g = lax.dynamic_slice_in_dim(gvec["ffn_conv_w"].reshape(3, 2 * D_FF), chip * 1408, 1408, axis=1)
    vec_params = (("c_ctx", c_ctx, m_c_ctx, v_c_ctx, g_c_ctx), ("b_ada", b_ada, m_b_ada, v_b_ada, g_b_ada),
                  ("norm1_g", norm1_g, m_norm1_g, v_norm1_g, gvec["norm1_g"]),
                  ("q_norm_g", q_norm_g, m_q_norm_g, v_q_norm_g, gvec["q_norm_g"]),
                  ("kv_norm_g", kv_norm_g, m_kv_norm_g, v_kv_norm_g, gvec["kv_norm_g"]),
                  ("conv_w", conv_w, m_conv_w, v_conv_w, conv_w_g), ("conv_b", conv_b, m_conv_b, v_conv_b, gvec["conv_b"]),
                  ("norm2_g", norm2_g, m_norm2_g, v_norm2_g, gvec["norm2_g"]),
                  ("ffn_conv_w", ffn_conv_w, m_ffn_conv_w, v_ffn_conv_w, ffn_conv_w_g),
                  ("ffn_conv_b", ffn_conv_b, m_ffn_conv_b, v_ffn_conv_b, gvec["ffn_conv_b"]),
                  ("final_g", final_g, m_final_g, v_final_g, gvec["final_g"]))
    two_d = lambda a: a.reshape((-1, a.shape[-1]))
    many = [p + ((lambda r, s=p[1].shape: r.reshape(s)),) for p in vec_params]
    for n, w_, m_, v_ in (("w_ukv", w_ukv, m_w_ukv, v_w_ukv), ("w_attn_out", w_attn_out, m_w_attn_out, v_w_attn_out),
                          ("w_conv_out", w_conv_out, m_w_conv_out, v_w_conv_out)):
        many.append((n, w_, m_, v_, gw[n], (lambda r, s=w_.shape: r.reshape(s))))
    many.append(("w_uq", w_uq_t, m_w_uq_t, v_w_uq_t, gw["w_uq"], lambda r: jnp.transpose(r)[None]))

    def f_adam_many(ids, *vals):
        out = []
        for k in range(len(many)):
            out += _adam_update(*vals[4 * k:4 * k + 4])
        return out

    ins_v, outs_v = [], []
    for p in many:
        shp = two_d(p[1]).shape
        ins_v += [(two_d(a), _full(shp)) for a in (p[1], p[4], p[2], p[3])]
        outs_v += [(shp, F32, _full(shp), None)] * 3
    res_v = _ew(f_adam_many, (1,), ins_v, outs_v, "adamw_small")
    for k, p in enumerate(many):
        n, post = p[0], p[5]
        grads[n] = post(two_d(p[4]))
        deltas[n], new_m[n], new_v[n] = (post(r) for r in res_v[3 * k:3 * k + 3])

    gw_in = finish_wait(pending_back, res_v[0])
    adam("w_in", w_in_t, m_w_in_t, v_w_in_t, gw_in["w_in"], True)

    order = ("c_ctx", "w_ada", "b_ada", "norm1_g", "w_in", "q_norm_g", "kv_norm_g", "w_uq", "w_ukv", "conv_w", "conv_b",
             "w_attn_out", "w_conv_out", "w_o", "norm2_g", "w_up", "ffn_conv_w", "ffn_conv_b", "w_down", "final_g")
    return (loss, grad_x[None], *[grads[n] for n in order], *[deltas[n] for n in order],
            *[new_m[n] for n in order], *[new_v[n] for n in order])
```

```python
import functools

import jax
import jax.numpy as jnp
import numpy as np
from jax import lax
from jax.experimental import pallas as pl
from jax.experimental.pallas import tpu as pltpu

F32, BF16 = jnp.float32, jnp.bfloat16
MESH = pl.DeviceIdType.MESH

D_MODEL = 1024
N_HEADS = 8
HEAD_PAD = 128
QK_DIM = 96
Q_RANK, KV_RANK = 384, 256
CONV_DIM = 512
D_FF = 2816
GRID_W = 64
ROPE_THETA = 10000.0
EPS = 1e-6
GA0, GC0, CX0, CB0, CC0, KV0, Q0, KR0, P_COLS = 0, 1024, 2048, 2560, 3072, 3584, 3840, 4224, 4352
PA_KV0, PA_Q0, PA_KR0, PA_COLS = 0, 384, 768, 896
ROW_TILE = 256
VMEM_LIMIT_BYTES = 48 * 1024 * 1024

ADAM_LR, ADAM_B1, ADAM_B2, ADAM_EPS, ADAM_WD, ADAM_STEP = 0.001, 0.9, 0.999, 1e-08, 0.01, 10

NN = (((1,), (0,)), ((), ()))
NT = (((1,), (1,)), ((), ()))
TN = (((0,), (0,)), ((), ()))


def _cp(sem):
    return pltpu.CompilerParams(dimension_semantics=sem, vmem_limit_bytes=VMEM_LIMIT_BYTES)


PIN_BYTES = 1 << 19


def _in_hbm(arrays):
    return [pltpu.with_memory_space_constraint(a, pltpu.HBM) if a.size * a.dtype.itemsize >= PIN_BYTES else a
            for a in arrays]


def _out(shape, dtype):
    n = 1
    for d in shape:
        n *= d
    big = n * jnp.dtype(dtype).itemsize >= PIN_BYTES
    return pltpu.HBM(shape, dtype) if big else jax.ShapeDtypeStruct(shape, dtype)


def _pick(n, prefs):
    for p in prefs:
        if n % p == 0:
            return p
    return n


def _mm(a, b, mode, M, N, K, *, tm, tn, tk, name, out_dtype=F32, a_spec=None, b_spec=None, o_spec=None,
        out_shape=None, transpose_out=False):
    assert M % tm == 0 and N % tn == 0 and K % tk == 0, (name, M, N, K, tm, tn, tk)
    nk = K // tk
    dims = {"nn": NN, "nt": NT, "tn": TN}[mode]
    if a_spec is None:
        a_spec = (pl.BlockSpec((tk, tm), lambda i, j, k: (k, i)) if mode == "tn"
                  else pl.BlockSpec((tm, tk), lambda i, j, k: (i, k)))
    if b_spec is None:
        b_spec = (pl.BlockSpec((tn, tk), lambda i, j, k: (j, k)) if mode == "nt"
                  else pl.BlockSpec((tk, tn), lambda i, j, k: (k, j)))
    if o_spec is None:
        o_spec = (pl.BlockSpec((tn, tm), lambda i, j, k: (j, i)) if transpose_out
                  else pl.BlockSpec((tm, tn), lambda i, j, k: (i, j)))
    if out_shape is None:
        out_shape = (N, M) if transpose_out else (M, N)

    def emit(o_ref, val):
        o_ref[...] = (val.T if transpose_out else val).astype(o_ref.dtype)

    def body(a_ref, b_ref, o_ref, *scratch):
        part = lax.dot_general(a_ref[...].astype(BF16), b_ref[...].astype(BF16), dims, preferred_element_type=F32)
        if nk == 1:
            emit(o_ref, part)
            return
        acc_ref, = scratch
        k = pl.program_id(2)

        @pl.when(k == 0)
        def _():
            acc_ref[...] = part

        @pl.when((k > 0) & (k < nk - 1))
        def _():
            acc_ref[...] += part

        @pl.when(k == nk - 1)
        def _():
            emit(o_ref, acc_ref[...] + part)

    return pl.pallas_call(
        body, grid=(M // tm, N // tn, nk), in_specs=[a_spec, b_spec], out_specs=o_spec,
        out_shape=_out(out_shape, out_dtype),
        scratch_shapes=[pltpu.VMEM((tm, tn), F32)] if nk > 1 else [],
        compiler_params=_cp(("parallel", "parallel", "arbitrary")), name=name)(*_in_hbm([a, b]))


def _ew(fn, grid, ins, outs, name, scalars=None):
    n_in = len(ins)
    n_sc = 0 if scalars is None else 1

    def store(ref, val, acc, ids):
        if isinstance(val, (list, tuple)):
            for h, v in enumerate(val):
                ref[h] = v.astype(ref.dtype)
            return
        if acc is None:
            ref[...] = val.astype(ref.dtype)
            return

        @pl.when(ids[acc] == 0)
        def _():
            ref[...] = val.astype(ref.dtype)

        @pl.when(ids[acc] > 0)
        def _():
            ref[...] += val.astype(ref.dtype)

    def body(*refs):
        refs = refs[n_sc:]
        ids = tuple(pl.program_id(a) for a in range(len(grid)))
        vals = fn(ids, *[r[...] for r in refs[:n_in]])
        for ref, val, (_, _, _, acc) in zip(refs[n_in:], vals, outs):
            store(ref, val, acc, ids)

    acc_axes = {o[3] for o in outs if o[3] is not None}
    sem = tuple("arbitrary" if a in acc_axes else "parallel" for a in range(len(grid)))
    in_specs, out_specs = [s for _, s in ins], [o[2] for o in outs]
    out_shape = [_out(o[0], o[1]) for o in outs]
    args = _in_hbm([a for a, _ in ins])
    if scalars is None:
        return pl.pallas_call(body, grid=grid, in_specs=in_specs, out_specs=out_specs, out_shape=out_shape,
                              compiler_params=_cp(sem), name=name)(*args)
    spec = pltpu.PrefetchScalarGridSpec(num_scalar_prefetch=1, grid=grid, in_specs=in_specs, out_specs=out_specs)
    return pl.pallas_call(body, grid_spec=spec, out_shape=out_shape, compiler_params=_cp(sem), name=name)(scalars, *args)


def _rows(width, cblk=0, roff=0, tr=ROW_TILE):
    return pl.BlockSpec((tr, width), lambda i: (i + roff, cblk))


def _full(shape):
    nd = len(shape)
    return pl.BlockSpec(shape, lambda *_: (0,) * nd)


def _sigmoid(x):
    return 1.0 / (1.0 + jnp.exp2(x * (-1.4426950408889634)))


def _rms(x):
    return lax.rsqrt(jnp.mean(x * x, axis=-1, keepdims=True) + EPS)


def _rms_bwd(dn, xn, r):
    return r * (dn - xn * jnp.mean(dn * xn, axis=-1, keepdims=True))


def _colsum(x):
    return jnp.sum(x, axis=0, keepdims=True)


def _shifts(x):
    n = x.shape[0]
    rows = lax.broadcasted_iota(jnp.int32, x.shape, 0)
    return jnp.where(rows == 0, 0.0, pltpu.roll(x, 1, 0)), jnp.where(rows == n - 1, 0.0, pltpu.roll(x, n - 1, 0))


def _conv(x, w, b, shifted=None):
    prev, nxt = _shifts(x) if shifted is None else shifted
    return b + prev * w[0:1] + x * w[1:2] + nxt * w[2:3]


def _conv_bwd_x(dy, w):
    prev, nxt = _shifts(dy)
    return nxt * w[0:1] + dy * w[1:2] + prev * w[2:3]


def _conv_bwd_w(dy, x, shifted):
    prev, nxt = shifted
    return _colsum(dy * prev), _colsum(dy * x), _colsum(dy * nxt)


def _rope(x, cos, sin_lo, sin_hi):
    return x * cos + pltpu.roll(x, HEAD_PAD - 8, 1) * sin_lo + pltpu.roll(x, 8, 1) * sin_hi


ATTN_SCALE = QK_DIM ** -0.5
LOG2_E = 1.4426950408889634


def _rope_t(x, tab, inverse=False):
    o = 3 * HEAD_PAD if inverse else 0
    return _rope(x, tab[:, o:o + HEAD_PAD], tab[:, o + HEAD_PAD:o + 2 * HEAD_PAD], tab[:, o + 2 * HEAD_PAD:o + 3 * HEAD_PAD])


def _head_keys(kv_ref, kr_ref, tab_ref, kc_ref, vp_ref):
    kv = kv_ref[...]
    lane = lax.broadcasted_iota(jnp.int32, kv.shape, 1)
    kc_ref[...] = jnp.where(lane < 64, kv, _rope_t(kr_ref[...], tab_ref[...])).astype(BF16)
    vp_ref[...] = jnp.where(lane >= 64, kv, 0.0).astype(BF16)


ATTN_Q_TILE = 512


def _attn_specs(tq, TT):
    q = pl.BlockSpec((tq, HEAD_PAD), lambda h, i: (i, h))
    keys = pl.BlockSpec((TT, HEAD_PAD), lambda h, i: (0, h))
    kr = pl.BlockSpec((TT, HEAD_PAD), lambda h, i: (0, PA_KR0 // HEAD_PAD))
    tab_q = pl.BlockSpec((tq, 6 * HEAD_PAD), lambda h, i: (i, 0))
    tab_k = pl.BlockSpec((TT, 6 * HEAD_PAD), lambda h, i: (0, 0))
    return q, keys, kr, tab_q, tab_k


def _attn_fwd(q_raw, kv, pp, tab, T, TT):
    tq = ROW_TILE

    def body(q_ref, kv_ref, kr_ref, tq_ref, tk_ref, o_ref, kc, vp):
        @pl.when(pl.program_id(1) == 0)
        def _():
            _head_keys(kv_ref, kr_ref, tk_ref, kc, vp)

        q = _rope_t(q_ref[...], tq_ref[...]).astype(BF16)
        s = lax.dot_general(q, kc[...], NT, preferred_element_type=F32)
        m = jnp.max(s, axis=-1, keepdims=True)
        p = jnp.exp2((s - m) * (ATTN_SCALE * LOG2_E))
        l = jnp.sum(p, axis=-1, keepdims=True)
        o = lax.dot_general(p.astype(BF16), vp[...], NN, preferred_element_type=F32)
        lane = lax.broadcasted_iota(jnp.int32, o.shape, 1)
        o_ref[...] = jnp.where(lane < 64, m * ATTN_SCALE + jnp.log(l), o / l)

    qs, keys, kr, _, _ = _attn_specs(tq, TT)
    tab_q = pl.BlockSpec((tq, 3 * HEAD_PAD), lambda h, i: (i, 0))
    tab_k = pl.BlockSpec((TT, 3 * HEAD_PAD), lambda h, i: (0, 0))
    return pl.pallas_call(
        body, grid=(N_HEADS, T // tq), in_specs=[qs, keys, kr, tab_q, tab_k], out_specs=qs,
        out_shape=jax.ShapeDtypeStruct((T, N_HEADS * HEAD_PAD), F32),
        scratch_shapes=[pltpu.VMEM((TT, HEAD_PAD), BF16), pltpu.VMEM((TT, HEAD_PAD), BF16)],
        compiler_params=_cp(("parallel", "arbitrary")), name="attn_fwd",
    )(*_in_hbm([q_raw, kv, pp, tab, tab]))


def _attn_bwd(q_raw, kv, pp, o, do, tab, T, TT):
    tq = _pick(T, (ATTN_Q_TILE, ROW_TILE))
    nq = T // tq

    def body(q_ref, kv_ref, kr_ref, tq_ref, tk_ref, o_ref, do_ref, dq_ref, dkv_ref, dkr_ref, kc, vp, dk, dv):
        h, i = pl.program_id(0), pl.program_id(1)

        @pl.when(i == 0)
        def _():
            _head_keys(kv_ref, kr_ref, tk_ref, kc, vp)
            dk[...] = jnp.zeros_like(dk)
            dv[...] = jnp.zeros_like(dv)

        q = _rope_t(q_ref[...], tq_ref[...]).astype(BF16)
        k, v, d_o = kc[...], vp[...], do_ref[...]
        s = lax.dot_general(q, k, NT, preferred_element_type=F32)
        o = o_ref[...]
        p = jnp.exp2(s * (ATTN_SCALE * LOG2_E) - o[:, 0:1] * LOG2_E)
        dob = d_o.astype(BF16)
        dp = lax.dot_general(dob, v, NT, preferred_element_type=F32)
        dd = jnp.sum(d_o * o, axis=-1, keepdims=True)
        ds = (p * (dp - dd) * ATTN_SCALE).astype(BF16)
        dq = lax.dot_general(ds, k, NN, preferred_element_type=F32)
        dq_ref[...] = _rope_t(dq, tq_ref[...], inverse=True).astype(dq_ref.dtype)
        dk[...] += lax.dot_general(q, ds, TN, preferred_element_type=F32)
        dv[...] += lax.dot_general(dob, p.astype(BF16), TN, preferred_element_type=F32)

        @pl.when(i == nq - 1)
        def _():
            dkh = dk[...].T
            lane = lax.broadcasted_iota(jnp.int32, dkh.shape, 1)
            dkv_ref[...] = jnp.where(lane < 64, dkh, dv[...].T).astype(dkv_ref.dtype)
            rot = _rope_t(jnp.where((lane >= 64) & (lane < 96), dkh, 0.0), tk_ref[...], inverse=True)

            @pl.when(h == 0)
            def _():
                dkr_ref[...] = rot

            @pl.when(h > 0)
            def _():
                dkr_ref[...] += rot

    qs, keys, kr, tab_q, tab_k = _attn_specs(tq, TT)
    wide = lambda rows: jax.ShapeDtypeStruct((rows, N_HEADS * HEAD_PAD), BF16)
    return pl.pallas_call(
        body, grid=(N_HEADS, nq),
        in_specs=[qs, keys, kr, tab_q, tab_k, qs, qs],
        out_specs=[qs, keys, pl.BlockSpec((TT, HEAD_PAD), lambda h, i: (0, 0))],
        out_shape=[wide(T), wide(TT), jax.ShapeDtypeStruct((TT, HEAD_PAD), F32)],
        scratch_shapes=[pltpu.VMEM((TT, HEAD_PAD), BF16), pltpu.VMEM((TT, HEAD_PAD), BF16),
                        pltpu.VMEM((HEAD_PAD, TT), F32), pltpu.VMEM((HEAD_PAD, TT), F32)],
        compiler_params=_cp(("arbitrary", "arbitrary")), name="attn_bwd",
    )(*_in_hbm([q_raw, kv, pp, tab, tab, o, do]))


def _hbm_specs(n):
    return [pl.BlockSpec(memory_space=pl.ANY)] * n


def _gather_weights(shards):
    n = len(shards)
    halves = [s.shape[0] // 2 for s in shards]

    def body(*refs):
        ins, outs = refs[:n], refs[n:2 * n]
        token, send_sems, recv_sems = refs[2 * n:]
        token[...] = jnp.zeros_like(token)
        mx, my, mc = lax.axis_index("x"), lax.axis_index("y"), lax.axis_index("c")
        j_me = 2 * mx + my
        chips = [(1 - mx, my), (mx, 1 - my), (1 - mx, 1 - my)]

        def half(w, chip_idx, hc):
            return outs[w].at[chip_idx, pl.ds(hc * halves[w], halves[w]), :]

        def copy(w, k, src, dst, to):
            return pltpu.make_async_remote_copy(src_ref=src, dst_ref=dst, send_sem=send_sems.at[w, k],
                                                recv_sem=recv_sems.at[w, k], device_id=to, device_id_type=MESH)

        sends = []
        for w in range(n):
            cp = copy(w, 6, ins[w], outs[w].at[j_me], (mx, my, 1 - mc))
            cp.start()
            sends.append(cp)
        for k, (px, py) in enumerate(chips):
            for w in range(n):
                cp = copy(w, k, ins[w].at[pl.ds(mc * halves[w], halves[w]), :], half(w, j_me, mc), (px, py, mc))
                cp.start()
                sends.append(cp)
        for k, (px, py) in enumerate(chips):
            for w in range(n):
                got = half(w, 2 * px + py, mc)
                copy(w, k, got, got, (px, py, mc)).wait_recv()
                cp = copy(w, 3 + k, got, got, (mx, my, 1 - mc))
                cp.start()
                sends.append(cp)
        for k, (px, py) in enumerate(chips):
            for w in range(n):
                got = half(w, 2 * px + py, 1 - mc)
                copy(w, 3 + k, got, got, (mx, my, 1 - mc)).wait_recv()
        for w in range(n):
            own = outs[w].at[j_me]
            copy(w, 6, own, own, (mx, my, 1 - mc)).wait_recv()
        for cp in sends:
            cp.wait_send()

    res = pl.pallas_call(
        body, out_shape=[jax.ShapeDtypeStruct((4,) + s.shape, s.dtype) for s in shards]
        + [jax.ShapeDtypeStruct((8, 128), F32)],
        in_specs=_hbm_specs(n), out_specs=_hbm_specs(n) + [pl.BlockSpec(memory_space=pltpu.VMEM)],
        scratch_shapes=[pltpu.SemaphoreType.DMA((n, 7)), pltpu.SemaphoreType.DMA((n, 7))],
        name="gather_weights")(*shards)
    return list(res[:n]), res[n]


def _rs_pair(gs, name):
    n = len(gs)
    halves = [g.shape[1] // 2 for g in gs]

    def body(*refs):
        ins, lands = refs[:n], refs[n:2 * n]
        send_sems, recv_sems = refs[2 * n:]
        mx, my, mc = lax.axis_index("x"), lax.axis_index("y"), lax.axis_index("c")
        copies = []
        for w in range(n):
            h = halves[w]
            cp = pltpu.make_async_remote_copy(
                src_ref=ins[w].at[:, pl.ds((1 - mc) * h, h), :], dst_ref=lands[w], send_sem=send_sems.at[w],
                recv_sem=recv_sems.at[w], device_id=(mx, my, 1 - mc), device_id_type=MESH)
            cp.start()
            copies.append(cp)
        for cp in copies:
            cp.wait()

    return pl.pallas_call(
        body, out_shape=[jax.ShapeDtypeStruct((4, h, g.shape[2]), g.dtype) for g, h in zip(gs, halves)],
        in_specs=_hbm_specs(n), out_specs=_hbm_specs(n),
        scratch_shapes=[pltpu.SemaphoreType.DMA((n,)), pltpu.SemaphoreType.DMA((n,))], name=name)(*gs)


def _rs_chips(parts):
    n = len(parts)

    def body(*refs):
        ins, lands = refs[:n], refs[n:2 * n]
        send_sems, recv_sems = refs[2 * n:]
        mx, my, mc = lax.axis_index("x"), lax.axis_index("y"), lax.axis_index("c")
        copies = []
        for k, (px, py) in enumerate([(1 - mx, my), (mx, 1 - my), (1 - mx, 1 - my)]):
            for w in range(n):
                cp = pltpu.make_async_remote_copy(
                    src_ref=ins[w].at[2 * px + py], dst_ref=lands[w].at[k], send_sem=send_sems.at[w, k],
                    recv_sem=recv_sems.at[w, k], device_id=(px, py, mc), device_id_type=MESH)
                cp.start()
                copies.append(cp)
        for cp in copies:
            cp.wait()

    return list(pl.pallas_call(
        body, out_shape=[jax.ShapeDtypeStruct((3,) + p.shape[1:], p.dtype) for p in parts],
        in_specs=_hbm_specs(n), out_specs=_hbm_specs(n),
        scratch_shapes=[pltpu.SemaphoreType.DMA((n, 3)), pltpu.SemaphoreType.DMA((n, 3))], name="rs_chips")(*parts))


_HBM = pl.BlockSpec(memory_space=pltpu.HBM)
_SEM = pl.BlockSpec(memory_space=pltpu.SEMAPHORE)
_EFFECT = pltpu.SideEffectType.DATAFLOW_SIDE_EFFECTING


def _ici_copies(kind, srcs, lands, send_sems, recv_sems):
    n = len(lands)
    mx, my, mc = lax.axis_index("x"), lax.axis_index("y"), lax.axis_index("c")
    j_me = 2 * mx + my
    copies = []
    if kind == "back":
        for w in range(n):
            h = lands[w].shape[0] // 2
            mine = lands[w].at[pl.ds(mc * h, h), :]
            copies.append(pltpu.make_async_remote_copy(
                src_ref=mine, dst_ref=mine, send_sem=send_sems.at[w], recv_sem=recv_sems.at[w],
                device_id=(mx, my, 1 - mc), device_id_type=MESH))
        return copies
    if kind == "all":
        for k in range(7):
            a, b, c = (k + 1) >> 2 & 1, (k + 1) >> 1 & 1, (k + 1) & 1
            peer = (1 - mx if a else mx, 1 - my if b else my, 1 - mc if c else mc)
            for w in range(n):
                copies.append(pltpu.make_async_remote_copy(
                    src_ref=srcs[w], dst_ref=lands[w].at[4 * mx + 2 * my + mc], send_sem=send_sems.at[7 * w + k],
                    recv_sem=recv_sems.at[7 * w + k], device_id=peer, device_id_type=MESH))
        return copies
    if kind == "pair":
        for w in range(n):
            h = srcs[w].shape[1] // 2
            copies.append(pltpu.make_async_remote_copy(
                src_ref=srcs[w].at[:, pl.ds((1 - mc) * h, h), :], dst_ref=lands[w], send_sem=send_sems.at[w],
                recv_sem=recv_sems.at[w], device_id=(mx, my, 1 - mc), device_id_type=MESH))
        return copies
    chips = [(1 - mx, my), (mx, 1 - my), (1 - mx, 1 - my)]
    if kind == "finish":
        for w in range(n):
            h = srcs[w].shape[0] // 2
            pushes = [(lands[w].at[2 * px + py, pl.ds(mc * h, h), :],) * 2 for px, py in chips]
            pushes.append((srcs[w], lands[w].at[j_me]))
            for k, (src, dst) in enumerate(pushes):
                copies.append(pltpu.make_async_remote_copy(
                    src_ref=src, dst_ref=dst, send_sem=send_sems.at[4 * w + k], recv_sem=recv_sems.at[4 * w + k],
                    device_id=(mx, my, 1 - mc), device_id_type=MESH))
        return copies
    for k, (px, py) in enumerate(chips):
        for w in range(n):
            if kind == "gather":
                h = srcs[w].shape[0] // 2
                src, dst = srcs[w].at[pl.ds(mc * h, h), :], lands[w].at[j_me, pl.ds(mc * h, h), :]
            else:
                src, dst = srcs[w].at[2 * px + py], lands[w].at[k]
            copies.append(pltpu.make_async_remote_copy(
                src_ref=src, dst_ref=dst, send_sem=send_sems.at[3 * w + k], recv_sem=recv_sems.at[3 * w + k],
                device_id=(px, py, mc), device_id_type=MESH))
    return copies


_SEMS_PER_OPERAND = {"gather": 3, "scatter": 3, "all": 7, "pair": 1, "finish": 4, "back": 1}


def _ici_start(kind, srcs, land_shapes, carry, name, lands=None):
    hbm = lambda a: pltpu.with_memory_space_constraint(a, pltpu.HBM)
    if lands is None:
        lands = [lax.empty(s, srcs[0].dtype) for s in land_shapes]
    ns, nl = len(srcs), len(lands)

    def body(*refs):
        send_sems, recv_sems = refs[ns + nl + 1], refs[ns + nl + 2]
        for cp in _ici_copies(kind, refs[:ns], refs[ns:ns + nl], send_sems, recv_sems):
            cp.start()

    args = [hbm(a) for a in list(srcs) + list(lands) + [carry]]
    n_sem = _SEMS_PER_OPERAND[kind] * nl
    out_shape = ([pltpu.SemaphoreType.DMA((n_sem,)), pltpu.SemaphoreType.DMA((n_sem,))]
                 + [pltpu.HBM(a.shape, a.dtype) for a in args])
    res = pl.pallas_call(
        body, name=name, out_shape=out_shape, in_specs=[_HBM] * len(args), out_specs=[_SEM, _SEM] + [_HBM] * len(args),
        input_output_aliases={i: 2 + i for i in range(len(args))},
        compiler_params=pltpu.CompilerParams(has_side_effects=_EFFECT))(*args)
    return res[0], res[1], list(res[2:2 + ns]), list(res[2 + ns:2 + ns + nl]), res[2 + ns + nl]


def _ici_wait(kind, send_sems, recv_sems, srcs, lands, after, name):
    ns, nl = len(srcs), len(lands)

    def body(*refs):
        for cp in _ici_copies(kind, refs[:ns], refs[ns:ns + nl], refs[ns + nl], refs[ns + nl + 1]):
            cp.wait_send()
            cp.wait_recv()

    args = list(srcs) + list(lands)
    res = pl.pallas_call(
        body, name=name, out_shape=[pltpu.HBM(a.shape, a.dtype) for a in args],
        in_specs=[_HBM] * len(args) + [_SEM, _SEM, pl.BlockSpec(memory_space=pl.ANY)], out_specs=[_HBM] * len(args),
        input_output_aliases={i: i for i in range(len(args))},
        compiler_params=pltpu.CompilerParams(has_side_effects=_EFFECT))(*args, send_sems, recv_sems, after)
    return list(res[:ns]), list(res[ns:])


def _tile_rows(h, c, itemsize, mult):
    best = h
    for t in range(mult, h + 1, mult):
        if h % t == 0 and t * c * itemsize <= (1 << 21):
            best = t
    return best


def _add_pair(g, land, place, name):
    _, h, c = land.shape
    t = _tile_rows(h, c, 2, 16)
    nb = h // t
    return _ew(lambda ids, u, v: (u.astype(F32) + v.astype(F32),), (4, nb),
               [(g, pl.BlockSpec((None, t, c), lambda j, i, s: (j, s[1] * nb + i, 0))),
                (land, pl.BlockSpec((None, t, c), lambda j, i, s: (j, i, 0)))],
               [(land.shape, BF16, pl.BlockSpec((None, t, c), lambda j, i, s: (j, i, 0)), None)], name, scalars=place)[0]


def _add_pair_many(gs, lands, place, name):
    ins, outs = [], []
    for g, l in zip(gs, lands):
        ins += [(g, pl.BlockSpec(l.shape, lambda i, s: (0, s[1], 0))), (l, pl.BlockSpec(l.shape, lambda i, s: (0, 0, 0)))]
        outs.append((l.shape, BF16, pl.BlockSpec(l.shape, lambda i, s: (0, 0, 0)), None))
    fn = lambda ids, *v: [v[2 * k].astype(F32) + v[2 * k + 1].astype(F32) for k in range(len(gs))]
    return list(_ew(fn, (1,), ins, outs, name, scalars=place))


def _add_chips_many(owns, lands, place, name):
    ins, outs = [], []
    for own, land in zip(owns, lands):
        _, h, c = land.shape
        ins += [(own, pl.BlockSpec((None, h, c), lambda i, s: (s[0], 0, 0))),
                (land, pl.BlockSpec((3, h, c), lambda i, s: (0, 0, 0)))]
        outs.append(((2 * h, c), F32, pl.BlockSpec((h, c), lambda i, s: (s[1], 0)), None))

    def fn(ids, *v):
        return [((v[2 * k].astype(F32) + v[2 * k + 1][0].astype(F32)) + v[2 * k + 1][1].astype(F32))
                + v[2 * k + 1][2].astype(F32) for k in range(len(owns))]

    return list(_ew(fn, (1,), ins, outs, name, scalars=place))


def _add_chips(own, land, place, name):
    _, h, c = land.shape
    t = _tile_rows(h, c, 4, 16)
    nb = h // t

    def fn(ids, a, b):
        return (((a.astype(F32) + b[0].astype(F32)) + b[1].astype(F32)) + b[2].astype(F32),)

    return _ew(fn, (nb,), [(own, pl.BlockSpec((None, t, c), lambda i, s: (s[0], i, 0))),
                           (land, pl.BlockSpec((3, t, c), lambda i, s: (0, i, 0)))],
               [((2 * h, c), F32, pl.BlockSpec((t, c), lambda i, s: (s[1] * nb + i, 0)), None)], name, scalars=place)[0]


W_IN_SEGMENTS = ((0, 256, KV0), (256, 288, KR0 + 64), (288, 672, Q0), (672, 1184, CX0), (1184, 1696, CB0),
                 (1696, 2208, CC0), (2208, 3232, GA0), (3232, 4256, GC0))
W_IN_SHARD = 1064


W_IN_SHARD_PAD = 1088
W_IN_EARLY = 672


def _w_in_t_p_from_shards(s):
    pieces = []
    for o0, o1, p0 in sorted(W_IN_SEGMENTS, key=lambda t: t[2]):
        if p0 == KR0 + 64:
            pieces.append(jnp.zeros((64, s.shape[2]), s.dtype))
        for j in range(4):
            lo, hi = max(o0, j * W_IN_SHARD), min(o1, (j + 1) * W_IN_SHARD)
            if lo < hi:
                pieces.append(s[j, lo - j * W_IN_SHARD:hi - j * W_IN_SHARD])
    pieces.append(jnp.zeros((32, s.shape[2]), s.dtype))
    return jnp.concatenate(pieces, axis=0)


def _w_in_t_shards_from_p(g):
    shards = []
    for j in range(4):
        pieces = []
        for o0, o1, p0 in W_IN_SEGMENTS:
            lo, hi = max(o0, j * W_IN_SHARD), min(o1, (j + 1) * W_IN_SHARD)
            if lo < hi:
                pieces.append(g[p0 + lo - o0:p0 + hi - o0])
        pieces.append(jnp.zeros((W_IN_SHARD_PAD - W_IN_SHARD, g.shape[1]), g.dtype))
        shards.append(jnp.concatenate(pieces, axis=0))
    return jnp.stack(shards, axis=0)


def _cols_from_shards(s):
    return jnp.transpose(s, (1, 0, 2)).reshape(s.shape[1], -1)


def _rope_tables(T, TT, inverse):
    f32 = np.float32
    rows = T // GRID_W
    row = np.repeat(np.arange(rows), GRID_W).astype(f32)
    col = np.tile(np.arange(GRID_W), rows).astype(f32)
    inv = (f32(ROPE_THETA) ** (-np.arange(0, 16, 2, dtype=f32) / f32(16))).astype(f32)
    ang = np.concatenate([row[:, None] * inv, col[:, None] * inv], axis=-1).astype(f32)
    cos, sin = np.cos(ang).astype(f32), np.sin(ang).astype(f32)
    lane = np.arange(32)
    src = (lane // 16) * 8 + lane % 8
    lo = ((lane % 16) // 8 == 0).astype(f32)
    sgn = f32(-1.0 if inverse else 1.0)
    cos32 = cos[:, src]
    sin_lo32 = -sgn * sin[:, src] * lo
    sin_hi32 = sgn * sin[:, src] * (1 - lo)

    def widen(t32, fill):
        t = np.concatenate([np.full((T, 64), fill, f32), t32, np.full((T, 32), fill, f32)], axis=1)
        return np.concatenate([t, np.full((TT - T, HEAD_PAD), fill, f32)], axis=0)

    return [widen(cos32, 1.0), widen(sin_lo32, 0.0), widen(sin_hi32, 0.0)]


def _rope_table(T, TT):
    return jnp.asarray(np.concatenate(_rope_tables(T, TT, False) + _rope_tables(T, TT, True), axis=1))


def _local_step(xx, tgt, mod_lat, mod_ctx, W, late_weights, early_grads, early_continue):
    TT = xx.shape[0]
    T = tgt.shape[0]
    n_lat, n_all = T // ROW_TILE, TT // ROW_TILE
    sh1, sc1, g1, sh2, sc2, g2 = [mod_lat[:, k * D_MODEL:(k + 1) * D_MODEL] for k in range(6)]
    csh1, csc1 = mod_ctx[:, :D_MODEL], mod_ctx[:, D_MODEL:2 * D_MODEL]
    vec = lambda n: _full((1, n))
    row_out = lambda n, dt, rows=T: ((rows, n), dt, _rows(n), None)
    acc_out = lambda n: ((1, n), F32, _full((1, n)), 0)

    def f_norm1(ids, x, g, a_sh, a_sc, b_sh, b_sc):
        ctx = ids[0] >= n_lat
        sh, sc = jnp.where(ctx, b_sh, a_sh), jnp.where(ctx, b_sc, a_sc)
        return ((x * _rms(x) * g) * (1.0 + sc) + sh,)

    (hh,) = _ew(f_norm1, (n_all,), [(xx, _rows(D_MODEL)), (W["norm1_g"], vec(D_MODEL)), (sh1, vec(D_MODEL)),
                                   (sc1, vec(D_MODEL)), (csh1, vec(D_MODEL)), (csc1, vec(D_MODEL))],
                [row_out(D_MODEL, BF16, TT)], "norm1_fwd")
    tm_all = _pick(TT, (768, 256))
    pp_a = _mm(hh, W["w_in_a_t"], "nt", TT, PA_COLS, D_MODEL, tm=tm_all, tn=PA_COLS, tk=D_MODEL, name="w_in_fwd_a")

    def f_lowrank(ids, ckv, cq, gkv, gq):
        return ckv * _rms(ckv) * gkv, cq * _rms(cq) * gq

    nkv, nq = _ew(f_lowrank, (n_all,), [(pp_a, _rows(KV_RANK, PA_KV0 // KV_RANK)), (pp_a, _rows(Q_RANK, PA_Q0 // Q_RANK)),
                                       (W["kv_norm_g"], vec(KV_RANK)), (W["q_norm_g"], vec(Q_RANK))],
                  [row_out(KV_RANK, BF16, TT), row_out(Q_RANK, BF16, TT)], "lowrank_norm_fwd")
    kv = _mm(nkv, W["w_ukv"], "nn", TT, 1024, KV_RANK, tm=tm_all, tn=256, tk=KV_RANK, name="w_ukv_fwd",
             b_spec=pl.BlockSpec((None, KV_RANK, 256), lambda i, j, k: (j, k, 0)))
    q_raw = _mm(nq, W["w_uq_t"], "nt", TT, 1024, Q_RANK, tm=tm_all, tn=1024, tk=Q_RANK, name="w_uq_fwd")

    tab = _rope_table(T, TT)
    _, q_raw = late_weights("before_attn", q_raw)
    o_pad = _attn_fwd(q_raw, kv, pp_a, tab, T, TT)
    arrived, o_pad = late_weights("after_attn", o_pad)
    W = dict(W, **arrived)
    tm_lat = _pick(T, (1024, 512, 256))
    pp = _mm(hh, W["w_in_t"], "nt", T, KV0, D_MODEL, tm=tm_lat, tn=KV0 // 2, tk=D_MODEL, name="w_in_fwd_b")
    ya = _mm(o_pad, W["w_attn_out"], "nn", T, D_MODEL, 1024, tm=tm_lat, tn=D_MODEL, tk=1024, name="w_attn_out_fwd",
             out_dtype=BF16)

    tc = 256
    colT = lambda blk0: pl.BlockSpec((T, tc), lambda j: (0, blk0 + j))

    def f_conv(ids, xin, cb, cc, w, b):
        return (cb * _conv(cc * xin, w, b),)

    (e,) = _ew(f_conv, (CONV_DIM // tc,),
               [(pp, colT(CX0 // tc)), (pp, colT(CB0 // tc)), (pp, colT(CC0 // tc)),
                (W["conv_w"], pl.BlockSpec((3, tc), lambda j: (0, j))), (W["conv_b"], pl.BlockSpec((1, tc), lambda j: (0, j)))],
               [((T, CONV_DIM), BF16, colT(0), None)], "conv_fwd")
    yc = _mm(e, W["w_conv_out"], "nn", T, D_MODEL, CONV_DIM, tm=tm_lat, tn=256, tk=CONV_DIM, name="w_conv_out_fwd",
             out_dtype=BF16, b_spec=pl.BlockSpec((None, CONV_DIM, 256), lambda i, j, k: (j, k, 0)))

    def f_merge(ids, ga, gc, a, c):
        return (_sigmoid(ga) * a.astype(F32) + _sigmoid(gc) * c.astype(F32),)

    (mrg,) = _ew(f_merge, (n_lat,), [(pp, _rows(D_MODEL, 0)), (pp, _rows(D_MODEL, 1)), (ya, _rows(D_MODEL)),
                                    (yc, _rows(D_MODEL))], [row_out(D_MODEL, BF16)], "merge_fwd")
    mo = _mm(mrg, W["w_o"], "nn", T, D_MODEL, D_MODEL, tm=tm_lat, tn=D_MODEL, tk=D_MODEL, name="w_o_fwd")

    def f_norm2(ids, x, m, gate, g, sh, sc):
        x1 = x + gate * m
        return x1, (x1 * _rms(x1) * g) * (1.0 + sc) + sh

    x1, h2 = _ew(f_norm2, (n_lat,), [(xx, _rows(D_MODEL)), (mo, _rows(D_MODEL)), (g1, vec(D_MODEL)),
                                    (W["norm2_g"], vec(D_MODEL)), (sh2, vec(D_MODEL)), (sc2, vec(D_MODEL))],
                 [row_out(D_MODEL, F32), row_out(D_MODEL, BF16)], "norm2_fwd")
    arrived, h2 = late_weights("before_ffn", h2)
    W = dict(W, **arrived)
    up = _mm(h2, W["w_up"], "nn", T, 2 * D_FF, D_MODEL, tm=tm_lat, tn=1408, tk=D_MODEL, name="w_up_fwd",
             b_spec=pl.BlockSpec((None, D_MODEL, 1408), lambda i, j, k: (j, k, 0)))

    n_ff = D_FF // tc
    ffw = lambda off, n=3: pl.BlockSpec((n, tc), lambda j: (0, j + off))

    def f_ffn(ids, ug, uv, wg, wv, bg, bv):
        gate, val = _conv(ug, wg, bg), _conv(uv, wv, bv)
        return (gate * _sigmoid(gate) * val,)

    (act,) = _ew(f_ffn, (n_ff,), [(up, colT(0)), (up, colT(n_ff)), (W["ffn_conv_w"], ffw(0)), (W["ffn_conv_w"], ffw(n_ff)),
                                 (W["ffn_conv_b"], ffw(0, 1)), (W["ffn_conv_b"], ffw(n_ff, 1))],
                 [((T, D_FF), BF16, colT(0), None)], "ffn_act_fwd")
    f = _mm(act, W["w_down"], "nn", T, D_MODEL, D_FF, tm=tm_lat, tn=D_MODEL, tk=D_FF, name="w_down_fwd")

    def f_head(ids, x1_, f_, gate, gf, t):
        x2 = x1_ + gate * f_
        r = _rms(x2)
        xn = x2 * r
        err = xn * gf - t
        loss = 0.5 * jnp.sum(jnp.mean(err * err, axis=-1, keepdims=True))
        dy = err * (1.0 / D_MODEL)
        dx2 = _rms_bwd(dy * gf, xn, r)
        return dx2, dx2 * gate, _colsum(dy * xn), _colsum(dx2 * f_), jnp.full((1, 128), loss, F32)

    dx2, df, dg_f, dg2, loss = _ew(
        f_head, (n_lat,), [(x1, _rows(D_MODEL)), (f, _rows(D_MODEL)), (g2, vec(D_MODEL)), (W["final_g"], vec(D_MODEL)),
                           (tgt, _rows(D_MODEL))],
        [row_out(D_MODEL, F32), row_out(D_MODEL, BF16), acc_out(D_MODEL), acc_out(D_MODEL), acc_out(128)], "loss_head")

    d_w_down = _mm(act, df, "tn", D_FF, D_MODEL, T, tm=1408, tn=D_MODEL, tk=T, name="w_down_dw",
                   out_dtype=BF16).reshape(4, D_FF // 4, D_MODEL)
    da = _mm(df, W["w_down"], "nt", T, D_FF, D_MODEL, tm=tm_lat, tn=1408, tk=D_MODEL, name="w_down_dx")

    tcb = 128
    n_fb = D_FF // tcb
    colb = lambda blk0: pl.BlockSpec((T, tcb), lambda j: (0, blk0 + j))
    ffwb = lambda off, n=3: pl.BlockSpec((n, tcb), lambda j: (0, j + off))
    cvec = ((1, D_FF), F32, pl.BlockSpec((1, tcb), lambda j: (0, j)), None)

    def f_ffn_bwd(ids, ug, uv, d_act, wg, wv, bg, bv):
        sg, sv = _shifts(ug), _shifts(uv)
        gate, val = _conv(ug, wg, bg, sg), _conv(uv, wv, bv, sv)
        s = _sigmoid(gate)
        d_gate = d_act * val * s * (1.0 + gate * (1.0 - s))
        d_val = d_act * gate * s
        wg0, wg1, wg2 = _conv_bwd_w(d_gate, ug, sg)
        wv0, wv1, wv2 = _conv_bwd_w(d_val, uv, sv)
        d_up = [_conv_bwd_x(d_gate, wg), _conv_bwd_x(d_val, wv)]
        return d_up, [_colsum(d_gate), _colsum(d_val), wg0, wg1, wg2, wv0, wv1, wv2]

    d_up3, ffn_stats = _ew(
        f_ffn_bwd, (n_fb,),
        [(up, colb(0)), (up, colb(n_fb)), (da, colb(0)), (W["ffn_conv_w"], ffwb(0)), (W["ffn_conv_w"], ffwb(n_fb)),
         (W["ffn_conv_b"], ffwb(0, 1)), (W["ffn_conv_b"], ffwb(n_fb, 1))],
        [((2, T, D_FF), BF16, pl.BlockSpec((2, T, tcb), lambda j: (0, 0, j)), None),
         ((n_fb, 8, 1, tcb), F32, pl.BlockSpec((None, 8, 1, tcb), lambda j: (j, 0, 0, 0)), None)], "ffn_act_bwd")
    stat = lambda s: ffn_stats[:, s, 0, :].reshape(1, D_FF)
    d_ffn_conv_b = jnp.concatenate([stat(0), stat(1)], axis=1)
    d_ffn_conv_w = jnp.concatenate([jnp.concatenate([stat(2), stat(3), stat(4)], axis=0),
                                    jnp.concatenate([stat(5), stat(6), stat(7)], axis=0)], axis=1)

    tk_t = T
    d_w_up = _mm(h2, d_up3, "tn", D_MODEL, 2 * D_FF, T, tm=D_MODEL, tn=1408, tk=tk_t, name="w_up_dw", out_dtype=BF16,
                 b_spec=pl.BlockSpec((None, tk_t, 1408), lambda i, j, k: (j // 2, k, j % 2)),
                 o_spec=pl.BlockSpec((None, D_MODEL, 1408), lambda i, j, k: (j, i, 0)), out_shape=(4, D_MODEL, 1408))
    dh2 = _mm(d_up3, W["w_up"], "nt", T, D_MODEL, 2 * D_FF, tm=tm_lat, tn=D_MODEL, tk=1408, name="w_up_dx",
              a_spec=pl.BlockSpec((None, tm_lat, 1408), lambda i, j, k: (k // 2, i, k % 2)),
              b_spec=pl.BlockSpec((None, D_MODEL, 1408), lambda i, j, k: (k, j, 0)))

    def f_norm2_bwd(ids, dx2_, dh, x1_, m, g, sc, gate):
        r = _rms(x1_)
        xn = x1_ * r
        dx1 = dx2_ + _rms_bwd(dh * g * (1.0 + sc), xn, r)
        return dx1, dx1 * gate, _colsum(dh), _colsum(dh * xn * g), _colsum(dh * xn * (1.0 + sc)), _colsum(dx1 * m)

    dx1, dmo, dsh2, dsc2, dg_n2, dg1 = _ew(
        f_norm2_bwd, (n_lat,), [(dx2, _rows(D_MODEL)), (dh2, _rows(D_MODEL)), (x1, _rows(D_MODEL)), (mo, _rows(D_MODEL)),
                                (W["norm2_g"], vec(D_MODEL)), (sc2, vec(D_MODEL)), (g1, vec(D_MODEL))],
        [row_out(D_MODEL, F32), row_out(D_MODEL, BF16)] + [acc_out(D_MODEL)] * 4, "norm2_bwd")
    d_w_o = _mm(mrg, dmo, "tn", D_MODEL, D_MODEL, T, tm=D_MODEL, tn=D_MODEL, tk=tk_t, name="w_o_dw",
                out_dtype=BF16).reshape(4, D_MODEL // 4, D_MODEL)
    dmrg = _mm(dmo, W["w_o"], "nt", T, D_MODEL, D_MODEL, tm=tm_lat, tn=D_MODEL, tk=D_MODEL, name="w_o_dx",
               out_dtype=BF16)
    dmrg = early_grads("late", {"w_o": d_w_o, "w_up": d_w_up, "w_down": d_w_down}, dmrg, split=True)

    def f_merge_bwd(ids, dm, ga, gc, a, c):
        dm, a, c = dm.astype(F32), a.astype(F32), c.astype(F32)
        sa, sc_ = _sigmoid(ga), _sigmoid(gc)
        return dm * sa, dm * sc_, dm * a * sa * (1.0 - sa), dm * c * sc_ * (1.0 - sc_)

    dya, dyc, dp_ga, dp_gc = _ew(
        f_merge_bwd, (n_lat,), [(dmrg, _rows(D_MODEL)), (pp, _rows(D_MODEL, 0)), (pp, _rows(D_MODEL, 1)),
                                (ya, _rows(D_MODEL)), (yc, _rows(D_MODEL))], [row_out(D_MODEL, BF16)] * 4, "merge_bwd")
    dya = early_continue("late", dya)

    d_w_ao_p = _mm(o_pad, dya, "tn", 1024, D_MODEL, T, tm=1024, tn=D_MODEL, tk=tk_t, name="w_attn_out_dw", out_dtype=BF16)
    do_pad = _mm(dya, W["w_attn_out"], "nt", T, 1024, D_MODEL, tm=tm_lat, tn=1024, tk=D_MODEL, name="w_attn_out_dx")
    d_w_co = _mm(e, dyc, "tn", CONV_DIM, D_MODEL, T, tm=CONV_DIM, tn=256, tk=tk_t, name="w_conv_out_dw", out_dtype=BF16,
                 o_spec=pl.BlockSpec((None, CONV_DIM, 256), lambda i, j, k: (j, i, 0)), out_shape=(4, CONV_DIM, 256))
    de = _mm(dyc, W["w_conv_out"], "nt", T, CONV_DIM, D_MODEL, tm=tm_lat, tn=CONV_DIM, tk=256, name="w_conv_out_dx",
             b_spec=pl.BlockSpec((None, CONV_DIM, 256), lambda i, j, k: (k, j, 0)))

    def f_conv_bwd(ids, xin, cb, cc, d_e, w, b):
        z = cc * xin
        sz = _shifts(z)
        cz = _conv(z, w, b, sz)
        dcz = d_e * cb
        w0, w1, w2 = _conv_bwd_w(dcz, z, sz)
        dz = _conv_bwd_x(dcz, w)
        return dz * cc, d_e * cz, dz * xin, _colsum(dcz), w0, w1, w2

    cvec_c = ((1, CONV_DIM), F32, pl.BlockSpec((1, tc), lambda j: (0, j)), None)
    conv_b = _ew(f_conv_bwd, (CONV_DIM // tc,),
                 [(pp, colT(CX0 // tc)), (pp, colT(CB0 // tc)), (pp, colT(CC0 // tc)), (de, colT(0)),
                  (W["conv_w"], pl.BlockSpec((3, tc), lambda j: (0, j))), (W["conv_b"], pl.BlockSpec((1, tc), lambda j: (0, j)))],
                 [((T, CONV_DIM), BF16, colT(0), None)] * 3 + [cvec_c] * 4, "conv_bwd")
    dp_cx, dp_cb, dp_cc, d_conv_b = conv_b[:4]
    d_conv_w = jnp.concatenate(conv_b[4:7], axis=0)

    dq_raw, dkv, dp_kr = _attn_bwd(q_raw, kv, pp_a, o_pad, do_pad, tab, T, TT)

    tk_a = TT
    d_w_uq_t = _mm(nq, dq_raw, "tn", Q_RANK, 1024, T, tm=Q_RANK, tn=1024, tk=T, name="w_uq_dw", transpose_out=True)
    dnq = _mm(dq_raw, W["w_uq_t"], "nn", T, Q_RANK, 1024, tm=tm_lat, tn=Q_RANK, tk=1024, name="w_uq_dx")
    d_w_ukv = _mm(nkv, dkv, "tn", KV_RANK, 1024, TT, tm=KV_RANK, tn=256, tk=tk_a, name="w_ukv_dw", out_dtype=BF16,
                  o_spec=pl.BlockSpec((None, KV_RANK, 256), lambda i, j, k: (j, i, 0)), out_shape=(4, KV_RANK, 256))
    dnkv = _mm(dkv, W["w_ukv"], "nt", TT, KV_RANK, 1024, tm=tm_all, tn=KV_RANK, tk=256, name="w_ukv_dx",
               b_spec=pl.BlockSpec((None, KV_RANK, 256), lambda i, j, k: (k, j, 0)))
    dnkv = early_grads("mid", {
        "w_attn_out": jnp.transpose(d_w_ao_p.reshape(N_HEADS, HEAD_PAD, 4, 256)[:, 64:], (2, 0, 1, 3)).reshape(
            4, N_HEADS * 64, 256),
        "w_conv_out": d_w_co,
        "w_uq": d_w_uq_t.reshape(4, 2, HEAD_PAD, Q_RANK)[:, :, :QK_DIM].reshape(4, 2 * QK_DIM, Q_RANK).astype(BF16),
        "w_ukv": d_w_ukv}, dnkv)

    def f_lowrank_bwd(ids, ckv, cq, dkv_, dq_, gkv, gq, ga, gc, cx, cb, cc, kr):
        rk, rq = _rms(ckv), _rms(cq)
        nk, nq_ = ckv * rk, cq * rq
        lat = ids[0] < n_lat
        dq_ = jnp.where(lat, dq_, 0.0)
        pieces = [jnp.where(lat, a, jnp.zeros_like(a)) for a in (ga, gc, cx, cb, cc)]
        pieces += [_rms_bwd(dkv_ * gkv, nk, rk).astype(BF16), _rms_bwd(dq_ * gq, nq_, rq).astype(BF16), kr.astype(BF16)]
        return jnp.concatenate(pieces, axis=1), _colsum(dkv_ * nk), _colsum(dq_ * nq_)

    lat_rows = lambda n: pl.BlockSpec((ROW_TILE, n), lambda i: (jnp.minimum(i, n_lat - 1), 0))
    dpp, dg_kv, dg_q = _ew(
        f_lowrank_bwd, (n_all,), [(pp_a, _rows(KV_RANK, PA_KV0 // KV_RANK)), (pp_a, _rows(Q_RANK, PA_Q0 // Q_RANK)),
                                  (dnkv, _rows(KV_RANK)), (dnq, lat_rows(Q_RANK)), (W["kv_norm_g"], vec(KV_RANK)),
                                  (W["q_norm_g"], vec(Q_RANK)), (dp_ga, lat_rows(D_MODEL)), (dp_gc, lat_rows(D_MODEL)),
                                  (dp_cx, lat_rows(CONV_DIM)), (dp_cb, lat_rows(CONV_DIM)), (dp_cc, lat_rows(CONV_DIM)),
                                  (dp_kr, _rows(HEAD_PAD))],
        [row_out(P_COLS, BF16, TT), acc_out(KV_RANK), acc_out(Q_RANK)], "lowrank_norm_bwd")
    d_w_in_t = _mm(hh, dpp, "tn", D_MODEL, P_COLS, TT, tm=512, tn=2176, tk=TT, name="w_in_dw", out_dtype=BF16,
                   transpose_out=True)
    dhh = _mm(dpp, W["w_in_t"], "nn", TT, D_MODEL, P_COLS, tm=tm_all, tn=512, tk=2176, name="w_in_dx")

    def f_norm1_bwd(ids, x, dh, dres, g, sc):
        r = _rms(x)
        xn = x * r
        return (dres + _rms_bwd(dh * g * (1.0 + sc), xn, r), _colsum(dh), _colsum(dh * xn * g),
                _colsum(dh * xn * (1.0 + sc)))

    grad_x, dsh1, dsc1, dg_n1 = _ew(
        f_norm1_bwd, (n_lat,), [(xx, _rows(D_MODEL)), (dhh, _rows(D_MODEL)), (dx1, _rows(D_MODEL)),
                                (W["norm1_g"], vec(D_MODEL)), (sc1, vec(D_MODEL))],
        [row_out(D_MODEL, F32)] + [acc_out(D_MODEL)] * 3, "norm1_bwd")

    def f_norm1_ctx_bwd(ids, x, dh, g, sc):
        xn = x * _rms(x)
        return _colsum(dh), _colsum(dh * xn * g), _colsum(dh * xn * (1.0 + sc))

    n_ctx = n_all - n_lat
    dcsh1, dcsc1, dg_n1c = _ew(
        f_norm1_ctx_bwd, (n_ctx,), [(xx, _rows(D_MODEL, 0, n_lat)), (dhh, _rows(D_MODEL, 0, n_lat)),
                                    (W["norm1_g"], vec(D_MODEL)), (csc1, vec(D_MODEL))], [acc_out(D_MODEL)] * 3,
        "norm1_ctx_bwd")

    big = {"w_in": _w_in_t_shards_from_p(d_w_in_t).astype(BF16)}
    zero = jnp.zeros((1, 4 * D_MODEL), F32)
    small = {
        "dmod_lat": jnp.concatenate([dsh1, dsc1, dg1, dsh2, dsc2, dg2], axis=1),
        "dmod_ctx": jnp.concatenate([dcsh1, dcsc1, zero], axis=1),
        "norm1_g": dg_n1 + dg_n1c, "norm2_g": dg_n2, "final_g": dg_f, "q_norm_g": dg_q, "kv_norm_g": dg_kv,
        "conv_b": d_conv_b, "conv_w": d_conv_w.reshape(1, -1), "ffn_conv_b": d_ffn_conv_b,
        "ffn_conv_w": d_ffn_conv_w.reshape(1, -1),
    }
    return grad_x, loss, big, small


SMALL = (("dmod_lat", 6144), ("dmod_ctx", 6144), ("norm1_g", 1024), ("norm2_g", 1024), ("final_g", 1024),
         ("q_norm_g", 384), ("kv_norm_g", 256), ("conv_b", 512), ("conv_w", 1536), ("ffn_conv_b", 5632),
         ("ffn_conv_w", 16896), ("loss", 128))
SMALL_ROWS = 320


def _adam_update(w, g, m, v):
    c1, c2 = 1.0 - ADAM_B1 ** ADAM_STEP, 1.0 - ADAM_B2 ** ADAM_STEP
    m2 = ADAM_B1 * m + (1.0 - ADAM_B1) * g
    v2 = ADAM_B2 * v + (1.0 - ADAM_B2) * (g * g)
    return [-ADAM_LR * ((m2 / c1) / (jnp.sqrt(v2 / c2) + ADAM_EPS) + ADAM_WD * w), m2, v2]


def _adamw(w, g, m, v, name):
    R, C = w.shape
    tr = 8 if R % 8 == 0 else R
    for t in range(8, R + 1, 8):
        if R % t == 0 and t * C * 4 <= (1 << 20):
            tr = t
    spec = pl.BlockSpec((tr, C), lambda i: (i, 0))
    return _ew(lambda ids, *vals: _adam_update(*vals), (R // tr,), [(w, spec), (g, spec), (m, spec), (v, spec)],
               [((R, C), F32, spec, None)] * 3, name)


def kernel(x, c, ctx, c_ctx, w_ada, b_ada, norm1_g, w_in, q_norm_g, kv_norm_g, w_uq, w_ukv, conv_w, conv_b, w_attn_out, w_conv_out, w_o, norm2_g, w_up, ffn_conv_w, ffn_conv_b, w_down, final_g, loss_target, m_c_ctx, m_w_ada, m_b_ada, m_norm1_g, m_w_in, m_q_norm_g, m_kv_norm_g, m_w_uq, m_w_ukv, m_conv_w, m_conv_b, m_w_attn_out, m_w_conv_out, m_w_o, m_norm2_g, m_w_up, m_ffn_conv_w, m_ffn_conv_b, m_w_down, m_final_g, v_c_ctx, v_w_ada, v_b_ada, v_norm1_g, v_w_in, v_q_norm_g, v_kv_norm_g, v_w_uq, v_w_ukv, v_conv_w, v_conv_b, v_w_attn_out, v_w_conv_out, v_w_o, v_norm2_g, v_w_up, v_ffn_conv_w, v_ffn_conv_b, v_w_down, v_final_g):
    mx, my, mc = lax.axis_index("x"), lax.axis_index("y"), lax.axis_index("c")
    chip = 2 * mx + my
    dev = 4 * mx + 2 * my + mc
    T, Tc = x.shape[1], ctx.shape[1]
    TT = T + Tc
    w_in_t, m_w_in_t, v_w_in_t = (jnp.transpose(a[0]) for a in (w_in, m_w_in, v_w_in))
    w_uq_t, m_w_uq_t, v_w_uq_t = (jnp.transpose(a[0]) for a in (w_uq, m_w_uq, v_w_uq))
    conv_sh = jnp.concatenate([conv_w[0], ffn_conv_w[0]], axis=1)
    pay1 = jnp.concatenate([jnp.pad(c, ((0, 7), (0, 0))), jnp.pad(conv_sh, ((0, 5), (0, 0)))], axis=1)
    c_send, c_recv, c_src, c_land, zero0 = _ici_start("all", [pay1], [(8, 8, 2560)], jnp.zeros((8, 128), F32),
                                                      "cond_start")
    w_in_bf = (jnp.pad(w_in_t, ((0, W_IN_SHARD_PAD - W_IN_SHARD), (0, 0))) + zero0[0, 0]).astype(BF16)
    shards = {"w_in_a": w_in_bf[:W_IN_EARLY], "w_in_b": w_in_bf[W_IN_EARLY:], "w_uq": w_uq_t, "w_ukv": w_ukv[0],
              "w_attn_out": w_attn_out[0], "w_conv_out": w_conv_out[0], "w_o": w_o[0], "w_up": w_up[0],
              "w_down": w_down[0]}
    (pay1,), (c_land,) = _ici_wait("all", c_send, c_recv, c_src, c_land, w_in_bf, "cond_wait")
    got1 = lax.dynamic_update_slice(c_land, pay1[None], (dev, 0, 0))
    c_all = got1[:, 0, :D_MODEL]
    conv_all = got1[0::2, :3, D_MODEL:]
    conv_w_full = _cols_from_shards(conv_all[:, :, :128])
    ffn_conv_w_full = _cols_from_shards(conv_all[:, :, 128:])

    cond = jnp.concatenate([c_all, c_ctx.reshape(1, D_MODEL), jnp.zeros((7, D_MODEL), F32)], axis=0)

    def f_silu(ids, v):
        return (v * _sigmoid(v),)

    (s16,) = _ew(f_silu, (1,), [(cond, _full((16, D_MODEL)))], [((16, D_MODEL), F32, _full((16, D_MODEL)), None)], "silu_cond")
    mod_sh = _mm(s16, w_ada[0], "nn", 16, 1536, D_MODEL, tm=16, tn=768, tk=D_MODEL, name="w_ada_fwd")
    m_send, m_recv, m_src, m_land, zero1 = _ici_start("all", [mod_sh], [(8, 16, 1536)], jnp.zeros((8, 128), F32),
                                                      "mod_start")
    shards["w_ukv"] = w_ukv[0] + zero1[0, 0]

    first = ["w_in_a", "w_uq", "w_ukv"]
    gathered, zero = _gather_weights([shards[n].astype(BF16) for n in first])
    full = dict(zip(first, gathered))
    (mod_mine,), (m_land,) = _ici_wait("all", m_send, m_recv, m_src, m_land, gathered[0], "mod_wait")
    got2 = lax.dynamic_update_slice(m_land, mod_mine[None], (dev, 0, 0))
    mod_all = _cols_from_shards(got2[0::2]) + b_ada
    mod_lat = lax.dynamic_slice_in_dim(mod_all, dev, 1, axis=0)
    mod_ctx = mod_all[8:9]
    xx = jnp.concatenate([x[0], ctx[0]], axis=0)
    late_groups = {"g1": ("w_in_b", "w_attn_out", "w_conv_out", "w_o"), "g2": ("w_up", "w_down")}
    flight = {}
    for tag, group in late_groups.items():
        bf = [(shards[n] + zero[0, 0]).astype(BF16) for n in group]
        flight[tag] = _ici_start("gather", bf, [(4,) + s.shape for s in bf], xx, "gather_" + tag + "_start")
        xx = flight[tag][4]

    def chip_stage_done(tag, x):
        send, recv, src, land, _ = flight[tag]
        src, land = _ici_wait("gather", send, recv, src, land, x, "gather_" + tag + "_wait")
        flight[tag] = _ici_start("finish", src, None, x, "finish_" + tag + "_start", lands=land)
        return flight[tag][4]

    def arrived(tag, x):
        send, recv, src, land, _ = flight[tag]
        return dict(zip(late_groups[tag], _ici_wait("finish", send, recv, src, land, x, "finish_" + tag + "_wait")[1]))

    def late_weights(point, x):
        if point == "before_attn":
            return {}, chip_stage_done("g1", x)
        if point == "after_attn":
            got = arrived("g1", x)
            wao = _cols_from_shards(got["w_attn_out"]).reshape(N_HEADS, 64, D_MODEL)
            w_in_all = jnp.concatenate([full["w_in_a"], got["w_in_b"]], axis=1)
            ready = {"w_in_t": _w_in_t_p_from_shards(w_in_all),
                     "w_attn_out": jnp.pad(wao, ((0, 0), (64, 0), (0, 0))).reshape(N_HEADS * HEAD_PAD, D_MODEL),
                     "w_conv_out": got["w_conv_out"], "w_o": got["w_o"].reshape(D_MODEL, D_MODEL)}
            return ready, chip_stage_done("g2", x)
        got = arrived("g2", x)
        return {"w_up": got["w_up"], "w_down": got["w_down"].reshape(D_FF, D_MODEL)}, x

    wuq_t = full["w_uq"].reshape(N_HEADS, QK_DIM, Q_RANK)
    early_rows = full["w_in_a"][0]
    zrows = lambda n: jnp.zeros((n, D_MODEL), BF16)
    W = {
        "w_in_a_t": jnp.concatenate([early_rows[0:256], zrows(PA_Q0 - 256), early_rows[288:672], zrows(64),
                                     early_rows[256:288], zrows(32)], axis=0),
        "w_uq_t": jnp.pad(wuq_t, ((0, 0), (0, HEAD_PAD - QK_DIM), (0, 0))).reshape(N_HEADS * HEAD_PAD, Q_RANK),
        "w_ukv": full["w_ukv"],
        "norm1_g": norm1_g, "norm2_g": norm2_g, "final_g": final_g.reshape(1, D_MODEL), "q_norm_g": q_norm_g,
        "kv_norm_g": kv_norm_g, "conv_w": conv_w_full, "conv_b": conv_b, "ffn_conv_w": ffn_conv_w_full,
        "ffn_conv_b": ffn_conv_b,
    }

    place = jnp.stack([chip, mc]).astype(jnp.int32)
    early = {}

    pending = {}

    def scatter(tag, group, gs, from_sib, carry):
        if tag == "mid":
            sums = _add_pair_many(gs, from_sib, place, "rs_pair_add_mid")
        else:
            sums = [_add_pair(gs[w], from_sib[w], place, "rs_pair_add_" + n) for w, n in enumerate(group)]
        send, recv, sums, land, carry = _ici_start(
            "scatter", sums, [(3,) + s.shape[1:] for s in sums], carry, "rs_chips_" + tag + "_start")
        early[tag] = (group, send, recv, sums, land)
        return carry

    def early_grads(tag, g, carry, split=False):
        gs = list(g.values())
        if not split:
            return scatter(tag, list(g), gs, _rs_pair(gs, "rs_pair_" + tag), carry)
        send, recv, gs, land, carry = _ici_start(
            "pair", gs, [(4, s.shape[1] // 2, s.shape[2]) for s in gs], carry, "rs_pair_" + tag + "_start")
        pending[tag] = (list(g), send, recv, gs, land)
        return carry

    def early_continue(tag, carry):
        group, send, recv, gs, land = pending[tag]
        gs, from_sib = _ici_wait("pair", send, recv, gs, land, carry, "rs_pair_" + tag + "_wait")
        return scatter(tag, group, gs, from_sib, carry)

    grad_x, loss_part, gbig, gsmall = _local_step(xx, loss_target[0], mod_lat, mod_ctx, W, late_weights, early_grads,
                                                  early_continue)

    gsmall["loss"] = loss_part
    pay3 = jnp.concatenate([gsmall[n].reshape(-1) for n, _ in SMALL])
    pay3 = jnp.pad(pay3, (0, SMALL_ROWS * 128 - pay3.shape[0])).reshape(SMALL_ROWS, 128)
    s_send, s_recv, s_src, s_land, w_in_thru = _ici_start("all", [pay3], [(8, SMALL_ROWS, 128)], gbig["w_in"],
                                                         "small_start")
    gbig = {"w_in": w_in_thru}

    after_small = early_grads("last", gbig, s_src[0])

    (pay3,), (s_land,) = _ici_wait("all", s_send, s_recv, [after_small], s_land, early["last"][3][0], "small_wait")
    got3 = lax.dynamic_update_slice(s_land, pay3[None], (dev, 0, 0)).reshape(8 * SMALL_ROWS, 128)

    def f_sum8(ids, a):
        s = a[0:SMALL_ROWS]
        for d in range(1, 8):
            s = s + a[d * SMALL_ROWS:(d + 1) * SMALL_ROWS]
        return (s,)

    (vsum,) = _ew(f_sum8, (1,), [(got3, _full((8 * SMALL_ROWS, 128)))],
                  [((SMALL_ROWS, 128), F32, _full((SMALL_ROWS, 128)), None)], "sum_small")
    vflat = vsum.reshape(-1)
    gvec, off = {}, 0
    for n, size in SMALL:
        gvec[n] = vflat[off:off + size]
        off += size
    loss = gvec["loss"][0]
    dmod_rows = got3.reshape(8, SMALL_ROWS * 128)[:, :6 * D_MODEL]
    dm16 = jnp.concatenate([dmod_rows, gvec["dmod_ctx"].reshape(1, -1), jnp.zeros((7, 6 * D_MODEL), F32)], axis=0)

    def f_colsum(ids, a):
        return (_colsum(a),)

    (g_b_ada,) = _ew(f_colsum, (1,), [(dm16, _full((16, 6 * D_MODEL)))],
                     [((1, 6 * D_MODEL), F32, _full((1, 6 * D_MODEL)), None)], "b_ada_grad")
    dm_sh = lax.dynamic_slice_in_dim(dm16, chip * 1536, 1536, axis=1)
    g_w_ada = _mm(s16, dm_sh, "tn", D_MODEL, 1536, 16, tm=512, tn=768, tk=16, name="w_ada_dw")
    dcond_part = _mm(dm_sh, w_ada[0], "nt", 16, D_MODEL, 1536, tm=16, tn=512, tk=1536, name="w_ada_dx")
    d_send, d_recv, d_src, d_land, vsum = _ici_start("all", [dcond_part[8:16]], [(8, 8, D_MODEL)], vsum, "dcond_start")

    def finish_start(tags, after):
        done, halves = [], []
        for tag in tags:
            tag_names, send, recv, sums, land = early[tag]
            sums, land = _ici_wait("scatter", send, recv, sums, land, after, "rs_chips_" + tag + "_wait")
            done += tag_names
            if tag == "mid":
                halves += _add_chips_many(sums, land, place, "rs_chip_add_mid")
            else:
                halves += [_add_chips(a, b, place, "rs_chip_add_" + n) for a, b, n in zip(sums, land, tag_names)]
        send, recv, _, halves, _ = _ici_start("back", [], None, jnp.zeros((8, 128), F32), "rs_back_" + tags[0] + "_start",
                                              lands=halves)
        return done, send, recv, halves

    def finish_wait(state, after):
        done, send, recv, halves = state
        return dict(zip(done, _ici_wait("back", send, recv, [], halves, after, "rs_back_" + done[0] + "_wait")[1]))

    grads, deltas, new_m, new_v = {}, {}, {}, {}

    raw = {}

    def adam(n, w_, m_, v_, g, transposed):
        d_, m2, v2 = _adamw(w_, g, m_, v_, "adamw_" + n)
        raw[n] = d_
        back = (lambda a: jnp.transpose(a)[None]) if transposed else (lambda a: a[None])
        grads[n], deltas[n], new_m[n], new_v[n] = back(g[:w_.shape[0]]), back(d_), back(m2), back(v2)

    pending_back = finish_start(["late", "mid"], grad_x)
    adam("w_ada", w_ada[0], m_w_ada[0], v_w_ada[0], g_w_ada, False)
    gw = finish_wait(pending_back, raw["w_ada"])
    for n, (w_, m_, v_) in {"w_o": (w_o, m_w_o, v_w_o), "w_up": (w_up, m_w_up, v_w_up),
                            "w_down": (w_down, m_w_down, v_w_down)}.items():
        adam(n, w_[0], m_[0], v_[0], gw[n], False)
    pending_back = finish_start(["last"], raw["w_up"])

    (dcond_mine,), (d_land,) = _ici_wait("all", d_send, d_recv, d_src, d_land, raw["w_down"], "dcond_wait")
    got4 = lax.dynamic_update_slice(d_land, dcond_mine[None], (dev, 0, 0))[0::2, 0]

    def f_c_ctx(ids, parts, cc):
        s = _sigmoid(cc)
        d = parts[0:1] + parts[1:2] + parts[2:3] + parts[3:4]
        return (d * s * (1.0 + cc * (1.0 - s)),)

    (g_c_ctx,) = _ew(f_c_ctx, (1,), [(got4, _full((4, D_MODEL))), (c_ctx.reshape(1, D_MODEL), _full((1, D_MODEL)))],
                     [((1, D_MODEL), F32, _full((1, D_MODEL)), None)], "c_ctx_grad")

    conv_w_g = lax.dynamic_slice_in_dim(gvec["conv_w"].reshape(3, CONV_DIM), chip * 128, 128, axis=1)
    ffn_conv_w_g = lax.dynamic_slice_in_dim(gvec["ffn_conv_w"].reshape(3, 2 * D_FF), chip * 1408, 1408, axis=1)
    vec_params = (("c_ctx", c_ctx, m_c_ctx, v_c_ctx, g_c_ctx), ("b_ada", b_ada, m_b_ada, v_b_ada, g_b_ada),
                  ("norm1_g", norm1_g, m_norm1_g, v_norm1_g, gvec["norm1_g"]),
                  ("q_norm_g", q_norm_g, m_q_norm_g, v_q_norm_g, gvec["q_norm_g"]),
                  ("kv_norm_g", kv_norm_g, m_kv_norm_g, v_kv_norm_g, gvec["kv_norm_g"]),
                  ("conv_w", conv_w, m_conv_w, v_conv_w, conv_w_g), ("conv_b", conv_b, m_conv_b, v_conv_b, gvec["conv_b"]),
                  ("norm2_g", norm2_g, m_norm2_g, v_norm2_g, gvec["norm2_g"]),
                  ("ffn_conv_w", ffn_conv_w, m_ffn_conv_w, v_ffn_conv_w, ffn_conv_w_g),
                  ("ffn_conv_b", ffn_conv_b, m_ffn_conv_b, v_ffn_conv_b, gvec["ffn_conv_b"]),
                  ("final_g", final_g, m_final_g, v_final_g, gvec["final_g"]))
    two_d = lambda a: a.reshape((-1, a.shape[-1]))
    many = [p + ((lambda r, s=p[1].shape: r.reshape(s)),) for p in vec_params]
    for n, w_, m_, v_ in (("w_ukv", w_ukv, m_w_ukv, v_w_ukv), ("w_attn_out", w_attn_out, m_w_attn_out, v_w_attn_out),
                          ("w_conv_out", w_conv_out, m_w_conv_out, v_w_conv_out)):
        many.append((n, w_, m_, v_, gw[n], (lambda r, s=w_.shape: r.reshape(s))))
    many.append(("w_uq", w_uq_t, m_w_uq_t, v_w_uq_t, gw["w_uq"], lambda r: jnp.transpose(r)[None]))

    def f_adam_many(ids, *vals):
        out = []
        for k in range(len(many)):
            out += _adam_update(*vals[4 * k:4 * k + 4])
        return out

    ins_v, outs_v = [], []
    for p in many:
        shp = two_d(p[1]).shape
        ins_v += [(two_d(a), _full(shp)) for a in (p[1], p[4], p[2], p[3])]
        outs_v += [(shp, F32, _full(shp), None)] * 3
    res_v = _ew(f_adam_many, (1,), ins_v, outs_v, "adamw_small")
    for k, p in enumerate(many):
        n, post = p[0], p[5]
        grads[n] = post(two_d(p[4]))
        deltas[n], new_m[n], new_v[n] = (post(r) for r in res_v[3 * k:3 * k + 3])

    gw_in = finish_wait(pending_back, res_v[0])
    adam("w_in", w_in_t, m_w_in_t, v_w_in_t, gw_in["w_in"], True)

    order = ("c_ctx", "w_ada", "b_ada", "norm1_g", "w_in", "q_norm_g", "kv_norm_g", "w_uq", "w_ukv", "conv_w", "conv_b",
             "w_attn_out", "w_conv_out", "w_o", "norm2_g", "w_up", "ffn_conv_w", "ffn_conv_b", "w_down", "final_g")
    return (loss, grad_x[None], *[grads[n] for n in order], *[deltas[n] for n in order],
            *[new_m[n] for n in order], *[new_v[n] for n in order])
```

```python
import functools

import jax
import jax.numpy as jnp
import numpy as np
from jax import lax
from jax.experimental import pallas as pl
from jax.experimental.pallas import tpu as pltpu

F32, BF16 = jnp.float32, jnp.bfloat16
MESH = pl.DeviceIdType.MESH

D_MODEL = 1024
N_HEADS = 8
HEAD_PAD = 128
QK_DIM = 96
Q_RANK, KV_RANK = 384, 256
CONV_DIM = 512
D_FF = 2816
GRID_W = 64
ROPE_THETA = 10000.0
EPS = 1e-6
GA0, GC0, CX0, CB0, CC0, KV0, Q0, KR0, P_COLS = 0, 1024, 2048, 2560, 3072, 3584, 3840, 4224, 4352
PA_KV0, PA_Q0, PA_KR0, PA_COLS = 0, 384, 768, 896
ROW_TILE = 256
VMEM_LIMIT_BYTES = 48 * 1024 * 1024

ADAM_LR, ADAM_B1, ADAM_B2, ADAM_EPS, ADAM_WD, ADAM_STEP = 0.001, 0.9, 0.999, 1e-08, 0.01, 10

NN = (((1,), (0,)), ((), ()))
NT = (((1,), (1,)), ((), ()))
TN = (((0,), (0,)), ((), ()))


def _cp(sem):
    return pltpu.CompilerParams(dimension_semantics=sem, vmem_limit_bytes=VMEM_LIMIT_BYTES)


PIN_BYTES = 1 << 19


def _in_hbm(arrays):
    return [pltpu.with_memory_space_constraint(a, pltpu.HBM) if a.size * a.dtype.itemsize >= PIN_BYTES else a
            for a in arrays]


def _out(shape, dtype):
    n = 1
    for d in shape:
        n *= d
    big = n * jnp.dtype(dtype).itemsize >= PIN_BYTES
    return pltpu.HBM(shape, dtype) if big else jax.ShapeDtypeStruct(shape, dtype)


def _pick(n, prefs):
    for p in prefs:
        if n % p == 0:
            return p
    return n


def _mm(a, b, mode, M, N, K, *, tm, tn, tk, name, out_dtype=F32, a_spec=None, b_spec=None, o_spec=None,
        out_shape=None, transpose_out=False):
    assert M % tm == 0 and N % tn == 0 and K % tk == 0, (name, M, N, K, tm, tn, tk)
    nk = K // tk
    dims = {"nn": NN, "nt": NT, "tn": TN}[mode]
    if a_spec is None:
        a_spec = (pl.BlockSpec((tk, tm), lambda i, j, k: (k, i)) if mode == "tn"
                  else pl.BlockSpec((tm, tk), lambda i, j, k: (i, k)))
    if b_spec is None:
        b_spec = (pl.BlockSpec((tn, tk), lambda i, j, k: (j, k)) if mode == "nt"
                  else pl.BlockSpec((tk, tn), lambda i, j, k: (k, j)))
    if o_spec is None:
        o_spec = (pl.BlockSpec((tn, tm), lambda i, j, k: (j, i)) if transpose_out
                  else pl.BlockSpec((tm, tn), lambda i, j, k: (i, j)))
    if out_shape is None:
        out_shape = (N, M) if transpose_out else (M, N)

    def emit(o_ref, val):
        o_ref[...] = (val.T if transpose_out else val).astype(o_ref.dtype)

    def body(a_ref, b_ref, o_ref, *scratch):
        part = lax.dot_general(a_ref[...].astype(BF16), b_ref[...].astype(BF16), dims, preferred_element_type=F32)
        if nk == 1:
            emit(o_ref, part)
            return
        acc_ref, = scratch
        k = pl.program_id(2)

        @pl.when(k == 0)
        def _():
            acc_ref[...] = part

        @pl.when((k > 0) & (k < nk - 1))
        def _():
            acc_ref[...] += part

        @pl.when(k == nk - 1)
        def _():
            emit(o_ref, acc_ref[...] + part)

    return pl.pallas_call(
        body, grid=(M // tm, N // tn, nk), in_specs=[a_spec, b_spec], out_specs=o_spec,
        out_shape=_out(out_shape, out_dtype),
        scratch_shapes=[pltpu.VMEM((tm, tn), F32)] if nk > 1 else [],
        compiler_params=_cp(("parallel", "parallel", "arbitrary")), name=name)(*_in_hbm([a, b]))


def _ew(fn, grid, ins, outs, name, scalars=None):
    n_in = len(ins)
    n_sc = 0 if scalars is None else 1

    def store(ref, val, acc, ids):
        if isinstance(val, (list, tuple)):
            for h, v in enumerate(val):
                ref[h] = v.astype(ref.dtype)
            return
        if acc is None:
            ref[...] = val.astype(ref.dtype)
            return

        @pl.when(ids[acc] == 0)
        def _():
            ref[...] = val.astype(ref.dtype)

        @pl.when(ids[acc] > 0)
        def _():
            ref[...] += val.astype(ref.dtype)

    def body(*refs):
        refs = refs[n_sc:]
        ids = tuple(pl.program_id(a) for a in range(len(grid)))
        vals = fn(ids, *[r[...] for r in refs[:n_in]])
        for ref, val, (_, _, _, acc) in zip(refs[n_in:], vals, outs):
            store(ref, val, acc, ids)

    acc_axes = {o[3] for o in outs if o[3] is not None}
    sem = tuple("arbitrary" if a in acc_axes else "parallel" for a in range(len(grid)))
    in_specs, out_specs = [s for _, s in ins], [o[2] for o in outs]
    out_shape = [_out(o[0], o[1]) for o in outs]
    args = _in_hbm([a for a, _ in ins])
    if scalars is None:
        return pl.pallas_call(body, grid=grid, in_specs=in_specs, out_specs=out_specs, out_shape=out_shape,
                              compiler_params=_cp(sem), name=name)(*args)
    spec = pltpu.PrefetchScalarGridSpec(num_scalar_prefetch=1, grid=grid, in_specs=in_specs, out_specs=out_specs)
    return pl.pallas_call(body, grid_spec=spec, out_shape=out_shape, compiler_params=_cp(sem), name=name)(scalars, *args)


def _rows(width, cblk=0, roff=0, tr=ROW_TILE):
    return pl.BlockSpec((tr, width), lambda i: (i + roff, cblk))


def _full(shape):
    nd = len(shape)
    return pl.BlockSpec(shape, lambda *_: (0,) * nd)


def _sigmoid(x):
    return 1.0 / (1.0 + jnp.exp2(x * (-1.4426950408889634)))


def _rms(x):
    return lax.rsqrt(jnp.mean(x * x, axis=-1, keepdims=True) + EPS)


def _rms_bwd(dn, xn, r):
    return r * (dn - xn * jnp.mean(dn * xn, axis=-1, keepdims=True))


def _colsum(x):
    return jnp.sum(x, axis=0, keepdims=True)


def _shifts(x):
    n = x.shape[0]
    rows = lax.broadcasted_iota(jnp.int32, x.shape, 0)
    return jnp.where(rows == 0, 0.0, pltpu.roll(x, 1, 0)), jnp.where(rows == n - 1, 0.0, pltpu.roll(x, n - 1, 0))


def _conv(x, w, b, shifted=None):
    prev, nxt = _shifts(x) if shifted is None else shifted
    return b + prev * w[0:1] + x * w[1:2] + nxt * w[2:3]


def _conv_bwd_x(dy, w):
    prev, nxt = _shifts(dy)
    return nxt * w[0:1] + dy * w[1:2] + prev * w[2:3]


def _conv_bwd_w(dy, x, shifted):
    prev, nxt = shifted
    return _colsum(dy * prev), _colsum(dy * x), _colsum(dy * nxt)


def _rope(x, cos, sin_lo, sin_hi):
    return x * cos + pltpu.roll(x, HEAD_PAD - 8, 1) * sin_lo + pltpu.roll(x, 8, 1) * sin_hi


ATTN_SCALE = QK_DIM ** -0.5
LOG2_E = 1.4426950408889634


def _rope_t(x, tab, inverse=False):
    o = 3 * HEAD_PAD if inverse else 0
    return _rope(x, tab[:, o:o + HEAD_PAD], tab[:, o + HEAD_PAD:o + 2 * HEAD_PAD], tab[:, o + 2 * HEAD_PAD:o + 3 * HEAD_PAD])


def _head_keys(kv_ref, kr_ref, tab_ref, kc_ref, vp_ref):
    kv = kv_ref[...]
    lane = lax.broadcasted_iota(jnp.int32, kv.shape, 1)
    kc_ref[...] = jnp.where(lane < 64, kv, _rope_t(kr_ref[...], tab_ref[...])).astype(BF16)
    vp_ref[...] = jnp.where(lane >= 64, kv, 0.0).astype(BF16)


ATTN_Q_TILE = 512


def _attn_specs(tq, TT):
    q = pl.BlockSpec((tq, HEAD_PAD), lambda h, i: (i, h))
    keys = pl.BlockSpec((TT, HEAD_PAD), lambda h, i: (0, h))
    kr = pl.BlockSpec((TT, HEAD_PAD), lambda h, i: (0, PA_KR0 // HEAD_PAD))
    tab_q = pl.BlockSpec((tq, 6 * HEAD_PAD), lambda h, i: (i, 0))
    tab_k = pl.BlockSpec((TT, 6 * HEAD_PAD), lambda h, i: (0, 0))
    return q, keys, kr, tab_q, tab_k


def _attn_fwd(q_raw, kv, pp, tab, T, TT):
    tq, hp = ROW_TILE, 2
    w = hp * HEAD_PAD

    def body(q_ref, kv_ref, kr_ref, tq_ref, tk_ref, o_ref, kc, vp):
        @pl.when(pl.program_id(1) == 0)
        def _():
            kr_roped = _rope_t(kr_ref[...], tk_ref[...])
            lane = lax.broadcasted_iota(jnp.int32, kr_roped.shape, 1)
            for u in range(hp):
                kv = kv_ref[:, u * HEAD_PAD:(u + 1) * HEAD_PAD]
                kc[u] = jnp.where(lane < 64, kv, kr_roped).astype(BF16)
                vp[u] = jnp.where(lane >= 64, kv, 0.0).astype(BF16)

        tab = tq_ref[...]
        for u in range(hp):
            cols = slice(u * HEAD_PAD, (u + 1) * HEAD_PAD)
            q = _rope_t(q_ref[:, cols], tab).astype(BF16)
            s = lax.dot_general(q, kc[u], NT, preferred_element_type=F32)
            m = jnp.max(s, axis=-1, keepdims=True)
            p = jnp.exp2((s - m) * (ATTN_SCALE * LOG2_E))
            l = jnp.sum(p, axis=-1, keepdims=True)
            o = lax.dot_general(p.astype(BF16), vp[u], NN, preferred_element_type=F32)
            lane = lax.broadcasted_iota(jnp.int32, o.shape, 1)
            o_ref[:, cols] = jnp.where(lane < 64, m * ATTN_SCALE + jnp.log(l), o / l)

    _, _, kr, _, _ = _attn_specs(tq, TT)
    qs = pl.BlockSpec((tq, w), lambda h, i: (i, h))
    keys = pl.BlockSpec((TT, w), lambda h, i: (0, h))
    tab_q = pl.BlockSpec((tq, 3 * HEAD_PAD), lambda h, i: (i, 0))
    tab_k = pl.BlockSpec((TT, 3 * HEAD_PAD), lambda h, i: (0, 0))
    return pl.pallas_call(
        body, grid=(N_HEADS // hp, T // tq), in_specs=[qs, keys, kr, tab_q, tab_k], out_specs=qs,
        out_shape=jax.ShapeDtypeStruct((T, N_HEADS * HEAD_PAD), F32),
        scratch_shapes=[pltpu.VMEM((hp, TT, HEAD_PAD), BF16), pltpu.VMEM((hp, TT, HEAD_PAD), BF16)],
        compiler_params=_cp(("parallel", "arbitrary")), name="attn_fwd",
    )(*_in_hbm([q_raw, kv, pp, tab, tab]))


def _attn_bwd(q_raw, kv, pp, o, do, tab, T, TT):
    tq = _pick(T, (ATTN_Q_TILE, ROW_TILE))
    nq = T // tq

    def body(q_ref, kv_ref, kr_ref, tq_ref, tk_ref, o_ref, do_ref, dq_ref, dkv_ref, dkr_ref, kc, vp, dk, dv):
        h, i = pl.program_id(0), pl.program_id(1)

        @pl.when(i == 0)
        def _():
            _head_keys(kv_ref, kr_ref, tk_ref, kc, vp)
            dk[...] = jnp.zeros_like(dk)
            dv[...] = jnp.zeros_like(dv)

        q = _rope_t(q_ref[...], tq_ref[...]).astype(BF16)
        k, v, d_o = kc[...], vp[...], do_ref[...]
        s = lax.dot_general(q, k, NT, preferred_element_type=F32)
        o = o_ref[...]
        p = jnp.exp2(s * (ATTN_SCALE * LOG2_E) - o[:, 0:1] * LOG2_E)
        dob = d_o.astype(BF16)
        dp = lax.dot_general(dob, v, NT, preferred_element_type=F32)
        dd = jnp.sum(d_o * o, axis=-1, keepdims=True)
        ds = (p * (dp - dd) * ATTN_SCALE).astype(BF16)
        dq = lax.dot_general(ds, k, NN, preferred_element_type=F32)
        dq_ref[...] = _rope_t(dq, tq_ref[...], inverse=True).astype(dq_ref.dtype)
        dk[...] += lax.dot_general(q, ds, TN, preferred_element_type=F32)
        dv[...] += lax.dot_general(dob, p.astype(BF16), TN, preferred_element_type=F32)

        @pl.when(i == nq - 1)
        def _():
            dkh = dk[...].T
            lane = lax.broadcasted_iota(jnp.int32, dkh.shape, 1)
            dkv_ref[...] = jnp.where(lane < 64, dkh, dv[...].T).astype(dkv_ref.dtype)
            rot = _rope_t(jnp.where((lane >= 64) & (lane < 96), dkh, 0.0), tk_ref[...], inverse=True)

            @pl.when(h == 0)
            def _():
                dkr_ref[...] = rot

            @pl.when(h > 0)
            def _():
                dkr_ref[...] += rot

    qs, keys, kr, tab_q, tab_k = _attn_specs(tq, TT)
    wide = lambda rows: jax.ShapeDtypeStruct((rows, N_HEADS * HEAD_PAD), BF16)
    return pl.pallas_call(
        body, grid=(N_HEADS, nq),
        in_specs=[qs, keys, kr, tab_q, tab_k, qs, qs],
        out_specs=[qs, keys, pl.BlockSpec((TT, HEAD_PAD), lambda h, i: (0, 0))],
        out_shape=[wide(T), wide(TT), jax.ShapeDtypeStruct((TT, HEAD_PAD), F32)],
        scratch_shapes=[pltpu.VMEM((TT, HEAD_PAD), BF16), pltpu.VMEM((TT, HEAD_PAD), BF16),
                        pltpu.VMEM((HEAD_PAD, TT), F32), pltpu.VMEM((HEAD_PAD, TT), F32)],
        compiler_params=_cp(("arbitrary", "arbitrary")), name="attn_bwd",
    )(*_in_hbm([q_raw, kv, pp, tab, tab, o, do]))


def _hbm_specs(n):
    return [pl.BlockSpec(memory_space=pl.ANY)] * n


def _gather_weights(shards):
    n = len(shards)
    halves = [s.shape[0] // 2 for s in shards]

    def body(*refs):
        ins, outs = refs[:n], refs[n:2 * n]
        token, send_sems, recv_sems = refs[2 * n:]
        token[...] = jnp.zeros_like(token)
        mx, my, mc = lax.axis_index("x"), lax.axis_index("y"), lax.axis_index("c")
        j_me = 2 * mx + my
        chips = [(1 - mx, my), (mx, 1 - my), (1 - mx, 1 - my)]

        def half(w, chip_idx, hc):
            return outs[w].at[chip_idx, pl.ds(hc * halves[w], halves[w]), :]

        def copy(w, k, src, dst, to):
            return pltpu.make_async_remote_copy(src_ref=src, dst_ref=dst, send_sem=send_sems.at[w, k],
                                                recv_sem=recv_sems.at[w, k], device_id=to, device_id_type=MESH)

        sends = []
        for w in range(n):
            cp = copy(w, 6, ins[w], outs[w].at[j_me], (mx, my, 1 - mc))
            cp.start()
            sends.append(cp)
        for k, (px, py) in enumerate(chips):
            for w in range(n):
                cp = copy(w, k, ins[w].at[pl.ds(mc * halves[w], halves[w]), :], half(w, j_me, mc), (px, py, mc))
                cp.start()
                sends.append(cp)
        for k, (px, py) in enumerate(chips):
            for w in range(n):
                got = half(w, 2 * px + py, mc)
                copy(w, k, got, got, (px, py, mc)).wait_recv()
                cp = copy(w, 3 + k, got, got, (mx, my, 1 - mc))
                cp.start()
                sends.append(cp)
        for k, (px, py) in enumerate(chips):
            for w in range(n):
                got = half(w, 2 * px + py, 1 - mc)
                copy(w, 3 + k, got, got, (mx, my, 1 - mc)).wait_recv()
        for w in range(n):
            own = outs[w].at[j_me]
            copy(w, 6, own, own, (mx, my, 1 - mc)).wait_recv()
        for cp in sends:
            cp.wait_send()

    res = pl.pallas_call(
        body, out_shape=[jax.ShapeDtypeStruct((4,) + s.shape, s.dtype) for s in shards]
        + [jax.ShapeDtypeStruct((8, 128), F32)],
        in_specs=_hbm_specs(n), out_specs=_hbm_specs(n) + [pl.BlockSpec(memory_space=pltpu.VMEM)],
        scratch_shapes=[pltpu.SemaphoreType.DMA((n, 7)), pltpu.SemaphoreType.DMA((n, 7))],
        name="gather_weights")(*shards)
    return list(res[:n]), res[n]


def _rs_pair(gs, name):
    n = len(gs)
    halves = [g.shape[1] // 2 for g in gs]

    def body(*refs):
        ins, lands = refs[:n], refs[n:2 * n]
        send_sems, recv_sems = refs[2 * n:]
        mx, my, mc = lax.axis_index("x"), lax.axis_index("y"), lax.axis_index("c")
        copies = []
        for w in range(n):
            h = halves[w]
            cp = pltpu.make_async_remote_copy(
                src_ref=ins[w].at[:, pl.ds((1 - mc) * h, h), :], dst_ref=lands[w], send_sem=send_sems.at[w],
                recv_sem=recv_sems.at[w], device_id=(mx, my, 1 - mc), device_id_type=MESH)
            cp.start()
            copies.append(cp)
        for cp in copies:
            cp.wait()

    return pl.pallas_call(
        body, out_shape=[jax.ShapeDtypeStruct((4, h, g.shape[2]), g.dtype) for g, h in zip(gs, halves)],
        in_specs=_hbm_specs(n), out_specs=_hbm_specs(n),
        scratch_shapes=[pltpu.SemaphoreType.DMA((n,)), pltpu.SemaphoreType.DMA((n,))], name=name)(*gs)


def _rs_chips(parts):
    n = len(parts)

    def body(*refs):
        ins, lands = refs[:n], refs[n:2 * n]
        send_sems, recv_sems = refs[2 * n:]
        mx, my, mc = lax.axis_index("x"), lax.axis_index("y"), lax.axis_index("c")
        copies = []
        for k, (px, py) in enumerate([(1 - mx, my), (mx, 1 - my), (1 - mx, 1 - my)]):
            for w in range(n):
                cp = pltpu.make_async_remote_copy(
                    src_ref=ins[w].at[2 * px + py], dst_ref=lands[w].at[k], send_sem=send_sems.at[w, k],
                    recv_sem=recv_sems.at[w, k], device_id=(px, py, mc), device_id_type=MESH)
                cp.start()
                copies.append(cp)
        for cp in copies:
            cp.wait()

    return list(pl.pallas_call(
        body, out_shape=[jax.ShapeDtypeStruct((3,) + p.shape[1:], p.dtype) for p in parts],
        in_specs=_hbm_specs(n), out_specs=_hbm_specs(n),
        scratch_shapes=[pltpu.SemaphoreType.DMA((n, 3)), pltpu.SemaphoreType.DMA((n, 3))], name="rs_chips")(*parts))


_HBM = pl.BlockSpec(memory_space=pltpu.HBM)
_SEM = pl.BlockSpec(memory_space=pltpu.SEMAPHORE)
_EFFECT = pltpu.SideEffectType.DATAFLOW_SIDE_EFFECTING


def _ici_copies(kind, srcs, lands, send_sems, recv_sems):
    n = len(lands)
    mx, my, mc = lax.axis_index("x"), lax.axis_index("y"), lax.axis_index("c")
    j_me = 2 * mx + my
    copies = []
    if kind == "back":
        for w in range(n):
            h = lands[w].shape[0] // 2
            mine = lands[w].at[pl.ds(mc * h, h), :]
            copies.append(pltpu.make_async_remote_copy(
                src_ref=mine, dst_ref=mine, send_sem=send_sems.at[w], recv_sem=recv_sems.at[w],
                device_id=(mx, my, 1 - mc), device_id_type=MESH))
        return copies
    if kind == "all":
        for k in range(7):
            a, b, c = (k + 1) >> 2 & 1, (k + 1) >> 1 & 1, (k + 1) & 1
            peer = (1 - mx if a else mx, 1 - my if b else my, 1 - mc if c else mc)
            for w in range(n):
                copies.append(pltpu.make_async_remote_copy(
                    src_ref=srcs[w], dst_ref=lands[w].at[4 * mx + 2 * my + mc], send_sem=send_sems.at[7 * w + k],
                    recv_sem=recv_sems.at[7 * w + k], device_id=peer, device_id_type=MESH))
        return copies
    if kind == "pair":
        for w in range(n):
            h = srcs[w].shape[1] // 2
            copies.append(pltpu.make_async_remote_copy(
                src_ref=srcs[w].at[:, pl.ds((1 - mc) * h, h), :], dst_ref=lands[w], send_sem=send_sems.at[w],
                recv_sem=recv_sems.at[w], device_id=(mx, my, 1 - mc), device_id_type=MESH))
        return copies
    chips = [(1 - mx, my), (mx, 1 - my), (1 - mx, 1 - my)]
    if kind == "finish":
        for w in range(n):
            h = srcs[w].shape[0] // 2
            pushes = [(lands[w].at[2 * px + py, pl.ds(mc * h, h), :],) * 2 for px, py in chips]
            pushes.append((srcs[w], lands[w].at[j_me]))
            for k, (src, dst) in enumerate(pushes):
                copies.append(pltpu.make_async_remote_copy(
                    src_ref=src, dst_ref=dst, send_sem=send_sems.at[4 * w + k], recv_sem=recv_sems.at[4 * w + k],
                    device_id=(mx, my, 1 - mc), device_id_type=MESH))
        return copies
    for k, (px, py) in enumerate(chips):
        for w in range(n):
            if kind == "gather":
                h = srcs[w].shape[0] // 2
                src, dst = srcs[w].at[pl.ds(mc * h, h), :], lands[w].at[j_me, pl.ds(mc * h, h), :]
            else:
                src, dst = srcs[w].at[2 * px + py], lands[w].at[k]
            copies.append(pltpu.make_async_remote_copy(
                src_ref=src, dst_ref=dst, send_sem=send_sems.at[3 * w + k], recv_sem=recv_sems.at[3 * w + k],
                device_id=(px, py, mc), device_id_type=MESH))
    return copies


_SEMS_PER_OPERAND = {"gather": 3, "scatter": 3, "all": 7, "pair": 1, "finish": 4, "back": 1}


def _ici_start(kind, srcs, land_shapes, carry, name, lands=None):
    hbm = lambda a: pltpu.with_memory_space_constraint(a, pltpu.HBM)
    if lands is None:
        lands = [lax.empty(s, srcs[0].dtype) for s in land_shapes]
    ns, nl = len(srcs), len(lands)

    def body(*refs):
        send_sems, recv_sems = refs[ns + nl + 1], refs[ns + nl + 2]
        for cp in _ici_copies(kind, refs[:ns], refs[ns:ns + nl], send_sems, recv_sems):
            cp.start()

    args = [hbm(a) for a in list(srcs) + list(lands) + [carry]]
    n_sem = _SEMS_PER_OPERAND[kind] * nl
    out_shape = ([pltpu.SemaphoreType.DMA((n_sem,)), pltpu.SemaphoreType.DMA((n_sem,))]
                 + [pltpu.HBM(a.shape, a.dtype) for a in args])
    res = pl.pallas_call(
        body, name=name, out_shape=out_shape, in_specs=[_HBM] * len(args), out_specs=[_SEM, _SEM] + [_HBM] * len(args),
        input_output_aliases={i: 2 + i for i in range(len(args))},
        compiler_params=pltpu.CompilerParams(has_side_effects=_EFFECT))(*args)
    return res[0], res[1], list(res[2:2 + ns]), list(res[2 + ns:2 + ns + nl]), res[2 + ns + nl]


def _ici_wait(kind, send_sems, recv_sems, srcs, lands, after, name):
    ns, nl = len(srcs), len(lands)

    def body(*refs):
        for cp in _ici_copies(kind, refs[:ns], refs[ns:ns + nl], refs[ns + nl], refs[ns + nl + 1]):
            cp.wait_send()
            cp.wait_recv()

    args = list(srcs) + list(lands)
    res = pl.pallas_call(
        body, name=name, out_shape=[pltpu.HBM(a.shape, a.dtype) for a in args],
        in_specs=[_HBM] * len(args) + [_SEM, _SEM, pl.BlockSpec(memory_space=pl.ANY)], out_specs=[_HBM] * len(args),
        input_output_aliases={i: i for i in range(len(args))},
        compiler_params=pltpu.CompilerParams(has_side_effects=_EFFECT))(*args, send_sems, recv_sems, after)
    return list(res[:ns]), list(res[ns:])


def _tile_rows(h, c, itemsize, mult):
    best = h
    for t in range(mult, h + 1, mult):
        if h % t == 0 and t * c * itemsize <= (1 << 21):
            best = t
    return best


def _add_pair(g, land, place, name):
    _, h, c = land.shape
    t = _tile_rows(h, c, 2, 16)
    nb = h // t
    return _ew(lambda ids, u, v: (u.astype(F32) + v.astype(F32),), (4, nb),
               [(g, pl.BlockSpec((None, t, c), lambda j, i, s: (j, s[1] * nb + i, 0))),
                (land, pl.BlockSpec((None, t, c), lambda j, i, s: (j, i, 0)))],
               [(land.shape, BF16, pl.BlockSpec((None, t, c), lambda j, i, s: (j, i, 0)), None)], name, scalars=place)[0]


def _add_pair_many(gs, lands, place, name):
    ins, outs = [], []
    for g, l in zip(gs, lands):
        ins += [(g, pl.BlockSpec(l.shape, lambda i, s: (0, s[1], 0))), (l, pl.BlockSpec(l.shape, lambda i, s: (0, 0, 0)))]
        outs.append((l.shape, BF16, pl.BlockSpec(l.shape, lambda i, s: (0, 0, 0)), None))
    fn = lambda ids, *v: [v[2 * k].astype(F32) + v[2 * k + 1].astype(F32) for k in range(len(gs))]
    return list(_ew(fn, (1,), ins, outs, name, scalars=place))


def _add_chips_many(owns, lands, place, name):
    ins, outs = [], []
    for own, land in zip(owns, lands):
        _, h, c = land.shape
        ins += [(own, pl.BlockSpec((None, h, c), lambda i, s: (s[0], 0, 0))),
                (land, pl.BlockSpec((3, h, c), lambda i, s: (0, 0, 0)))]
        outs.append(((2 * h, c), F32, pl.BlockSpec((h, c), lambda i, s: (s[1], 0)), None))

    def fn(ids, *v):
        return [((v[2 * k].astype(F32) + v[2 * k + 1][0].astype(F32)) + v[2 * k + 1][1].astype(F32))
                + v[2 * k + 1][2].astype(F32) for k in range(len(owns))]

    return list(_ew(fn, (1,), ins, outs, name, scalars=place))


def _add_chips(own, land, place, name):
    _, h, c = land.shape
    t = _tile_rows(h, c, 4, 16)
    nb = h // t

    def fn(ids, a, b):
        return (((a.astype(F32) + b[0].astype(F32)) + b[1].astype(F32)) + b[2].astype(F32),)

    return _ew(fn, (nb,), [(own, pl.BlockSpec((None, t, c), lambda i, s: (s[0], i, 0))),
                           (land, pl.BlockSpec((3, t, c), lambda i, s: (0, i, 0)))],
               [((2 * h, c), F32, pl.BlockSpec((t, c), lambda i, s: (s[1] * nb + i, 0)), None)], name, scalars=place)[0]


W_IN_SEGMENTS = ((0, 256, KV0), (256, 288, KR0 + 64), (288, 672, Q0), (672, 1184, CX0), (1184, 1696, CB0),
                 (1696, 2208, CC0), (2208, 3232, GA0), (3232, 4256, GC0))
W_IN_SHARD = 1064


W_IN_SHARD_PAD = 1088
W_IN_EARLY = 672


def _w_in_t_p_from_shards(s):
    pieces = []
    for o0, o1, p0 in sorted(W_IN_SEGMENTS, key=lambda t: t[2]):
        if p0 == KR0 + 64:
            pieces.append(jnp.zeros((64, s.shape[2]), s.dtype))
        for j in range(4):
            lo, hi = max(o0, j * W_IN_SHARD), min(o1, (j + 1) * W_IN_SHARD)
            if lo < hi:
                pieces.append(s[j, lo - j * W_IN_SHARD:hi - j * W_IN_SHARD])
    pieces.append(jnp.zeros((32, s.shape[2]), s.dtype))
    return jnp.concatenate(pieces, axis=0)


def _w_in_t_shards_from_p(g):
    shards = []
    for j in range(4):
        pieces = []
        for o0, o1, p0 in W_IN_SEGMENTS:
            lo, hi = max(o0, j * W_IN_SHARD), min(o1, (j + 1) * W_IN_SHARD)
            if lo < hi:
                pieces.append(g[p0 + lo - o0:p0 + hi - o0])
        pieces.append(jnp.zeros((W_IN_SHARD_PAD - W_IN_SHARD, g.shape[1]), g.dtype))
        shards.append(jnp.concatenate(pieces, axis=0))
    return jnp.stack(shards, axis=0)


def _cols_from_shards(s):
    return jnp.transpose(s, (1, 0, 2)).reshape(s.shape[1], -1)


def _rope_tables(T, TT, inverse):
    f32 = np.float32
    rows = T // GRID_W
    row = np.repeat(np.arange(rows), GRID_W).astype(f32)
    col = np.tile(np.arange(GRID_W), rows).astype(f32)
    inv = (f32(ROPE_THETA) ** (-np.arange(0, 16, 2, dtype=f32) / f32(16))).astype(f32)
    ang = np.concatenate([row[:, None] * inv, col[:, None] * inv], axis=-1).astype(f32)
    cos, sin = np.cos(ang).astype(f32), np.sin(ang).astype(f32)
    lane = np.arange(32)
    src = (lane // 16) * 8 + lane % 8
    lo = ((lane % 16) // 8 == 0).astype(f32)
    sgn = f32(-1.0 if inverse else 1.0)
    cos32 = cos[:, src]
    sin_lo32 = -sgn * sin[:, src] * lo
    sin_hi32 = sgn * sin[:, src] * (1 - lo)

    def widen(t32, fill):
        t = np.concatenate([np.full((T, 64), fill, f32), t32, np.full((T, 32), fill, f32)], axis=1)
        return np.concatenate([t, np.full((TT - T, HEAD_PAD), fill, f32)], axis=0)

    return [widen(cos32, 1.0), widen(sin_lo32, 0.0), widen(sin_hi32, 0.0)]


def _rope_table(T, TT):
    return jnp.asarray(np.concatenate(_rope_tables(T, TT, False) + _rope_tables(T, TT, True), axis=1))


def _local_step(xx, tgt, mod_lat, mod_ctx, W, late_weights, early_grads, early_continue):
    TT = xx.shape[0]
    T = tgt.shape[0]
    n_lat, n_all = T // ROW_TILE, TT // ROW_TILE
    sh1, sc1, g1, sh2, sc2, g2 = [mod_lat[:, k * D_MODEL:(k + 1) * D_MODEL] for k in range(6)]
    csh1, csc1 = mod_ctx[:, :D_MODEL], mod_ctx[:, D_MODEL:2 * D_MODEL]
    vec = lambda n: _full((1, n))
    row_out = lambda n, dt, rows=T: ((rows, n), dt, _rows(n), None)
    acc_out = lambda n: ((1, n), F32, _full((1, n)), 0)

    def f_norm1(ids, x, g, a_sh, a_sc, b_sh, b_sc):
        ctx = ids[0] >= n_lat
        sh, sc = jnp.where(ctx, b_sh, a_sh), jnp.where(ctx, b_sc, a_sc)
        return ((x * _rms(x) * g) * (1.0 + sc) + sh,)

    (hh,) = _ew(f_norm1, (n_all,), [(xx, _rows(D_MODEL)), (W["norm1_g"], vec(D_MODEL)), (sh1, vec(D_MODEL)),
                                   (sc1, vec(D_MODEL)), (csh1, vec(D_MODEL)), (csc1, vec(D_MODEL))],
                [row_out(D_MODEL, BF16, TT)], "norm1_fwd")
    tm_all = _pick(TT, (768, 256))
    pp_a = _mm(hh, W["w_in_a_t"], "nt", TT, PA_COLS, D_MODEL, tm=tm_all, tn=PA_COLS, tk=D_MODEL, name="w_in_fwd_a")

    def f_lowrank(ids, ckv, cq, gkv, gq):
        return ckv * _rms(ckv) * gkv, cq * _rms(cq) * gq

    nkv, nq = _ew(f_lowrank, (n_all,), [(pp_a, _rows(KV_RANK, PA_KV0 // KV_RANK)), (pp_a, _rows(Q_RANK, PA_Q0 // Q_RANK)),
                                       (W["kv_norm_g"], vec(KV_RANK)), (W["q_norm_g"], vec(Q_RANK))],
                  [row_out(KV_RANK, BF16, TT), row_out(Q_RANK, BF16, TT)], "lowrank_norm_fwd")
    kv = _mm(nkv, W["w_ukv"], "nn", TT, 1024, KV_RANK, tm=tm_all, tn=256, tk=KV_RANK, name="w_ukv_fwd",
             b_spec=pl.BlockSpec((None, KV_RANK, 256), lambda i, j, k: (j, k, 0)))
    q_raw = _mm(nq, W["w_uq_t"], "nt", TT, 1024, Q_RANK, tm=tm_all, tn=1024, tk=Q_RANK, name="w_uq_fwd")

    tab = _rope_table(T, TT)
    _, q_raw = late_weights("before_attn", q_raw)
    o_pad = _attn_fwd(q_raw, kv, pp_a, tab, T, TT)
    arrived, o_pad = late_weights("after_attn", o_pad)
    W = dict(W, **arrived)
    tm_lat = _pick(T, (1024, 512, 256))
    pp = _mm(hh, W["w_in_t"], "nt", T, KV0, D_MODEL, tm=tm_lat, tn=KV0 // 2, tk=D_MODEL, name="w_in_fwd_b")
    ya = _mm(o_pad, W["w_attn_out"], "nn", T, D_MODEL, 1024, tm=tm_lat, tn=D_MODEL, tk=1024, name="w_attn_out_fwd",
             out_dtype=BF16)

    tc = 256
    colT = lambda blk0: pl.BlockSpec((T, tc), lambda j: (0, blk0 + j))

    def f_conv(ids, xin, cb, cc, w, b):
        return (cb * _conv(cc * xin, w, b),)

    (e,) = _ew(f_conv, (CONV_DIM // tc,),
               [(pp, colT(CX0 // tc)), (pp, colT(CB0 // tc)), (pp, colT(CC0 // tc)),
                (W["conv_w"], pl.BlockSpec((3, tc), lambda j: (0, j))), (W["conv_b"], pl.BlockSpec((1, tc), lambda j: (0, j)))],
               [((T, CONV_DIM), BF16, colT(0), None)], "conv_fwd")
    yc = _mm(e, W["w_conv_out"], "nn", T, D_MODEL, CONV_DIM, tm=tm_lat, tn=256, tk=CONV_DIM, name="w_conv_out_fwd",
             out_dtype=BF16, b_spec=pl.BlockSpec((None, CONV_DIM, 256), lambda i, j, k: (j, k, 0)))

    def f_merge(ids, ga, gc, a, c):
        return (_sigmoid(ga) * a.astype(F32) + _sigmoid(gc) * c.astype(F32),)

    (mrg,) = _ew(f_merge, (n_lat,), [(pp, _rows(D_MODEL, 0)), (pp, _rows(D_MODEL, 1)), (ya, _rows(D_MODEL)),
                                    (yc, _rows(D_MODEL))], [row_out(D_MODEL, BF16)], "merge_fwd")
    mo = _mm(mrg, W["w_o"], "nn", T, D_MODEL, D_MODEL, tm=tm_lat, tn=D_MODEL, tk=D_MODEL, name="w_o_fwd")

    def f_norm2(ids, x, m, gate, g, sh, sc):
        x1 = x + gate * m
        return x1, (x1 * _rms(x1) * g) * (1.0 + sc) + sh

    x1, h2 = _ew(f_norm2, (n_lat,), [(xx, _rows(D_MODEL)), (mo, _rows(D_MODEL)), (g1, vec(D_MODEL)),
                                    (W["norm2_g"], vec(D_MODEL)), (sh2, vec(D_MODEL)), (sc2, vec(D_MODEL))],
                 [row_out(D_MODEL, F32), row_out(D_MODEL, BF16)], "norm2_fwd")
    arrived, h2 = late_weights("before_ffn", h2)
    W = dict(W, **arrived)
    up = _mm(h2, W["w_up"], "nn", T, 2 * D_FF, D_MODEL, tm=tm_lat, tn=1408, tk=D_MODEL, name="w_up_fwd",
             b_spec=pl.BlockSpec((None, D_MODEL, 1408), lambda i, j, k: (j, k, 0)))

    n_ff = D_FF // tc
    ffw = lambda off, n=3: pl.BlockSpec((n, tc), lambda j: (0, j + off))

    def f_ffn(ids, ug, uv, wg, wv, bg, bv):
        gate, val = _conv(ug, wg, bg), _conv(uv, wv, bv)
        return (gate * _sigmoid(gate) * val,)

    (act,) = _ew(f_ffn, (n_ff,), [(up, colT(0)), (up, colT(n_ff)), (W["ffn_conv_w"], ffw(0)), (W["ffn_conv_w"], ffw(n_ff)),
                                 (W["ffn_conv_b"], ffw(0, 1)), (W["ffn_conv_b"], ffw(n_ff, 1))],
                 [((T, D_FF), BF16, colT(0), None)], "ffn_act_fwd")
    f = _mm(act, W["w_down"], "nn", T, D_MODEL, D_FF, tm=tm_lat, tn=D_MODEL, tk=D_FF, name="w_down_fwd")

    def f_head(ids, x1_, f_, gate, gf, t):
        x2 = x1_ + gate * f_
        r = _rms(x2)
        xn = x2 * r
        err = xn * gf - t
        loss = 0.5 * jnp.sum(jnp.mean(err * err, axis=-1, keepdims=True))
        dy = err * (1.0 / D_MODEL)
        dx2 = _rms_bwd(dy * gf, xn, r)
        return dx2, dx2 * gate, _colsum(dy * xn), _colsum(dx2 * f_), jnp.full((1, 128), loss, F32)

    dx2, df, dg_f, dg2, loss = _ew(
        f_head, (n_lat,), [(x1, _rows(D_MODEL)), (f, _rows(D_MODEL)), (g2, vec(D_MODEL)), (W["final_g"], vec(D_MODEL)),
                           (tgt, _rows(D_MODEL))],
        [row_out(D_MODEL, F32), row_out(D_MODEL, BF16), acc_out(D_MODEL), acc_out(D_MODEL), acc_out(128)], "loss_head")

    d_w_down = _mm(act, df, "tn", D_FF, D_MODEL, T, tm=1408, tn=D_MODEL, tk=T, name="w_down_dw",
                   out_dtype=BF16).reshape(4, D_FF // 4, D_MODEL)
    da = _mm(df, W["w_down"], "nt", T, D_FF, D_MODEL, tm=tm_lat, tn=1408, tk=D_MODEL, name="w_down_dx")

    tcb = 128
    n_fb = D_FF // tcb
    colb = lambda blk0: pl.BlockSpec((T, tcb), lambda j: (0, blk0 + j))
    ffwb = lambda off, n=3: pl.BlockSpec((n, tcb), lambda j: (0, j + off))
    cvec = ((1, D_FF), F32, pl.BlockSpec((1, tcb), lambda j: (0, j)), None)

    def f_ffn_bwd(ids, ug, uv, d_act, wg, wv, bg, bv):
        sg, sv = _shifts(ug), _shifts(uv)
        gate, val = _conv(ug, wg, bg, sg), _conv(uv, wv, bv, sv)
        s = _sigmoid(gate)
        d_gate = d_act * val * s * (1.0 + gate * (1.0 - s))
        d_val = d_act * gate * s
        wg0, wg1, wg2 = _conv_bwd_w(d_gate, ug, sg)
        wv0, wv1, wv2 = _conv_bwd_w(d_val, uv, sv)
        d_up = [_conv_bwd_x(d_gate, wg), _conv_bwd_x(d_val, wv)]
        return d_up, [_colsum(d_gate), _colsum(d_val), wg0, wg1, wg2, wv0, wv1, wv2]

    d_up3, ffn_stats = _ew(
        f_ffn_bwd, (n_fb,),
        [(up, colb(0)), (up, colb(n_fb)), (da, colb(0)), (W["ffn_conv_w"], ffwb(0)), (W["ffn_conv_w"], ffwb(n_fb)),
         (W["ffn_conv_b"], ffwb(0, 1)), (W["ffn_conv_b"], ffwb(n_fb, 1))],
        [((2, T, D_FF), BF16, pl.BlockSpec((2, T, tcb), lambda j: (0, 0, j)), None),
         ((n_fb, 8, 1, tcb), F32, pl.BlockSpec((None, 8, 1, tcb), lambda j: (j, 0, 0, 0)), None)], "ffn_act_bwd")
    stat = lambda s: ffn_stats[:, s, 0, :].reshape(1, D_FF)
    d_ffn_conv_b = jnp.concatenate([stat(0), stat(1)], axis=1)
    d_ffn_conv_w = jnp.concatenate([jnp.concatenate([stat(2), stat(3), stat(4)], axis=0),
                                    jnp.concatenate([stat(5), stat(6), stat(7)], axis=0)], axis=1)

    tk_t = T
    d_w_up = _mm(h2, d_up3, "tn", D_MODEL, 2 * D_FF, T, tm=D_MODEL, tn=1408, tk=tk_t, name="w_up_dw", out_dtype=BF16,
                 b_spec=pl.BlockSpec((None, tk_t, 1408), lambda i, j, k: (j // 2, k, j % 2)),
                 o_spec=pl.BlockSpec((None, D_MODEL, 1408), lambda i, j, k: (j, i, 0)), out_shape=(4, D_MODEL, 1408))
    dh2 = _mm(d_up3, W["w_up"], "nt", T, D_MODEL, 2 * D_FF, tm=tm_lat, tn=D_MODEL, tk=1408, name="w_up_dx",
              a_spec=pl.BlockSpec((None, tm_lat, 1408), lambda i, j, k: (k // 2, i, k % 2)),
              b_spec=pl.BlockSpec((None, D_MODEL, 1408), lambda i, j, k: (k, j, 0)))

    def f_norm2_bwd(ids, dx2_, dh, x1_, m, g, sc, gate):
        r = _rms(x1_)
        xn = x1_ * r
        dx1 = dx2_ + _rms_bwd(dh * g * (1.0 + sc), xn, r)
        return dx1, dx1 * gate, _colsum(dh), _colsum(dh * xn * g), _colsum(dh * xn * (1.0 + sc)), _colsum(dx1 * m)

    dx1, dmo, dsh2, dsc2, dg_n2, dg1 = _ew(
        f_norm2_bwd, (n_lat,), [(dx2, _rows(D_MODEL)), (dh2, _rows(D_MODEL)), (x1, _rows(D_MODEL)), (mo, _rows(D_MODEL)),
                                (W["norm2_g"], vec(D_MODEL)), (sc2, vec(D_MODEL)), (g1, vec(D_MODEL))],
        [row_out(D_MODEL, F32), row_out(D_MODEL, BF16)] + [acc_out(D_MODEL)] * 4, "norm2_bwd")
    d_w_o = _mm(mrg, dmo, "tn", D_MODEL, D_MODEL, T, tm=D_MODEL, tn=D_MODEL, tk=tk_t, name="w_o_dw",
                out_dtype=BF16).reshape(4, D_MODEL // 4, D_MODEL)
    dmrg = _mm(dmo, W["w_o"], "nt", T, D_MODEL, D_MODEL, tm=tm_lat, tn=D_MODEL, tk=D_MODEL, name="w_o_dx",
               out_dtype=BF16)
    dmrg = early_grads("late", {"w_o": d_w_o, "w_up": d_w_up, "w_down": d_w_down}, dmrg, split=True)

    def f_merge_bwd(ids, dm, ga, gc, a, c):
        dm, a, c = dm.astype(F32), a.astype(F32), c.astype(F32)
        sa, sc_ = _sigmoid(ga), _sigmoid(gc)
        return dm * sa, dm * sc_, dm * a * sa * (1.0 - sa), dm * c * sc_ * (1.0 - sc_)

    dya, dyc, dp_ga, dp_gc = _ew(
        f_merge_bwd, (n_lat,), [(dmrg, _rows(D_MODEL)), (pp, _rows(D_MODEL, 0)), (pp, _rows(D_MODEL, 1)),
                                (ya, _rows(D_MODEL)), (yc, _rows(D_MODEL))], [row_out(D_MODEL, BF16)] * 4, "merge_bwd")
    dya = early_continue("late", dya)

    d_w_ao_p = _mm(o_pad, dya, "tn", 1024, D_MODEL, T, tm=1024, tn=D_MODEL, tk=tk_t, name="w_attn_out_dw", out_dtype=BF16)
    do_pad = _mm(dya, W["w_attn_out"], "nt", T, 1024, D_MODEL, tm=tm_lat, tn=1024, tk=D_MODEL, name="w_attn_out_dx")
    d_w_co = _mm(e, dyc, "tn", CONV_DIM, D_MODEL, T, tm=CONV_DIM, tn=256, tk=tk_t, name="w_conv_out_dw", out_dtype=BF16,
                 o_spec=pl.BlockSpec((None, CONV_DIM, 256), lambda i, j, k: (j, i, 0)), out_shape=(4, CONV_DIM, 256))
    de = _mm(dyc, W["w_conv_out"], "nt", T, CONV_DIM, D_MODEL, tm=tm_lat, tn=CONV_DIM, tk=256, name="w_conv_out_dx",
             b_spec=pl.BlockSpec((None, CONV_DIM, 256), lambda i, j, k: (k, j, 0)))

    def f_conv_bwd(ids, xin, cb, cc, d_e, w, b):
        z = cc * xin
        sz = _shifts(z)
        cz = _conv(z, w, b, sz)
        dcz = d_e * cb
        w0, w1, w2 = _conv_bwd_w(dcz, z, sz)
        dz = _conv_bwd_x(dcz, w)
        return dz * cc, d_e * cz, dz * xin, _colsum(dcz), w0, w1, w2

    cvec_c = ((1, CONV_DIM), F32, pl.BlockSpec((1, tc), lambda j: (0, j)), None)
    conv_b = _ew(f_conv_bwd, (CONV_DIM // tc,),
                 [(pp, colT(CX0 // tc)), (pp, colT(CB0 // tc)), (pp, colT(CC0 // tc)), (de, colT(0)),
                  (W["conv_w"], pl.BlockSpec((3, tc), lambda j: (0, j))), (W["conv_b"], pl.BlockSpec((1, tc), lambda j: (0, j)))],
                 [((T, CONV_DIM), BF16, colT(0), None)] * 3 + [cvec_c] * 4, "conv_bwd")
    dp_cx, dp_cb, dp_cc, d_conv_b = conv_b[:4]
    d_conv_w = jnp.concatenate(conv_b[4:7], axis=0)

    dq_raw, dkv, dp_kr = _attn_bwd(q_raw, kv, pp_a, o_pad, do_pad, tab, T, TT)

    tk_a = TT
    d_w_uq_t = _mm(nq, dq_raw, "tn", Q_RANK, 1024, T, tm=Q_RANK, tn=1024, tk=T, name="w_uq_dw", transpose_out=True)
    dnq = _mm(dq_raw, W["w_uq_t"], "nn", T, Q_RANK, 1024, tm=tm_lat, tn=Q_RANK, tk=1024, name="w_uq_dx")
    d_w_ukv = _mm(nkv, dkv, "tn", KV_RANK, 1024, TT, tm=KV_RANK, tn=256, tk=tk_a, name="w_ukv_dw", out_dtype=BF16,
                  o_spec=pl.BlockSpec((None, KV_RANK, 256), lambda i, j, k: (j, i, 0)), out_shape=(4, KV_RANK, 256))
    dnkv = _mm(dkv, W["w_ukv"], "nt", TT, KV_RANK, 1024, tm=tm_all, tn=KV_RANK, tk=256, name="w_ukv_dx",
               b_spec=pl.BlockSpec((None, KV_RANK, 256), lambda i, j, k: (k, j, 0)))
    dnkv = early_grads("mid", {
        "w_attn_out": jnp.transpose(d_w_ao_p.reshape(N_HEADS, HEAD_PAD, 4, 256)[:, 64:], (2, 0, 1, 3)).reshape(
            4, N_HEADS * 64, 256),
        "w_conv_out": d_w_co,
        "w_uq": d_w_uq_t.reshape(4, 2, HEAD_PAD, Q_RANK)[:, :, :QK_DIM].reshape(4, 2 * QK_DIM, Q_RANK).astype(BF16),
        "w_ukv": d_w_ukv}, dnkv)

    def f_lowrank_bwd(ids, ckv, cq, dkv_, dq_, gkv, gq, ga, gc, cx, cb, cc, kr):
        rk, rq = _rms(ckv), _rms(cq)
        nk, nq_ = ckv * rk, cq * rq
        lat = ids[0] < n_lat
        dq_ = jnp.where(lat, dq_, 0.0)
        pieces = [jnp.where(lat, a, jnp.zeros_like(a)) for a in (ga, gc, cx, cb, cc)]
        pieces += [_rms_bwd(dkv_ * gkv, nk, rk).astype(BF16), _rms_bwd(dq_ * gq, nq_, rq).astype(BF16), kr.astype(BF16)]
        return jnp.concatenate(pieces, axis=1), _colsum(dkv_ * nk), _colsum(dq_ * nq_)

    lat_rows = lambda n: pl.BlockSpec((ROW_TILE, n), lambda i: (jnp.minimum(i, n_lat - 1), 0))
    dpp, dg_kv, dg_q = _ew(
        f_lowrank_bwd, (n_all,), [(pp_a, _rows(KV_RANK, PA_KV0 // KV_RANK)), (pp_a, _rows(Q_RANK, PA_Q0 // Q_RANK)),
                                  (dnkv, _rows(KV_RANK)), (dnq, lat_rows(Q_RANK)), (W["kv_norm_g"], vec(KV_RANK)),
                                  (W["q_norm_g"], vec(Q_RANK)), (dp_ga, lat_rows(D_MODEL)), (dp_gc, lat_rows(D_MODEL)),
                                  (dp_cx, lat_rows(CONV_DIM)), (dp_cb, lat_rows(CONV_DIM)), (dp_cc, lat_rows(CONV_DIM)),
                                  (dp_kr, _rows(HEAD_PAD))],
        [row_out(P_COLS, BF16, TT), acc_out(KV_RANK), acc_out(Q_RANK)], "lowrank_norm_bwd")
    d_w_in_t = _mm(hh, dpp, "tn", D_MODEL, P_COLS, TT, tm=512, tn=2176, tk=TT, name="w_in_dw", out_dtype=BF16,
                   transpose_out=True)
    dhh = _mm(dpp, W["w_in_t"], "nn", TT, D_MODEL, P_COLS, tm=tm_all, tn=512, tk=2176, name="w_in_dx")

    def f_norm1_bwd(ids, x, dh, dres, g, sc):
        r = _rms(x)
        xn = x * r
        return (dres + _rms_bwd(dh * g * (1.0 + sc), xn, r), _colsum(dh), _colsum(dh * xn * g),
                _colsum(dh * xn * (1.0 + sc)))

    grad_x, dsh1, dsc1, dg_n1 = _ew(
        f_norm1_bwd, (n_lat,), [(xx, _rows(D_MODEL)), (dhh, _rows(D_MODEL)), (dx1, _rows(D_MODEL)),
                                (W["norm1_g"], vec(D_MODEL)), (sc1, vec(D_MODEL))],
        [row_out(D_MODEL, F32)] + [acc_out(D_MODEL)] * 3, "norm1_bwd")

    def f_norm1_ctx_bwd(ids, x, dh, g, sc):
        xn = x * _rms(x)
        return _colsum(dh), _colsum(dh * xn * g), _colsum(dh * xn * (1.0 + sc))

    n_ctx = n_all - n_lat
    dcsh1, dcsc1, dg_n1c = _ew(
        f_norm1_ctx_bwd, (n_ctx,), [(xx, _rows(D_MODEL, 0, n_lat)), (dhh, _rows(D_MODEL, 0, n_lat)),
                                    (W["norm1_g"], vec(D_MODEL)), (csc1, vec(D_MODEL))], [acc_out(D_MODEL)] * 3,
        "norm1_ctx_bwd")

    big = {"w_in": _w_in_t_shards_from_p(d_w_in_t).astype(BF16)}
    zero = jnp.zeros((1, 4 * D_MODEL), F32)
    small = {
        "dmod_lat": jnp.concatenate([dsh1, dsc1, dg1, dsh2, dsc2, dg2], axis=1),
        "dmod_ctx": jnp.concatenate([dcsh1, dcsc1, zero], axis=1),
        "norm1_g": dg_n1 + dg_n1c, "norm2_g": dg_n2, "final_g": dg_f, "q_norm_g": dg_q, "kv_norm_g": dg_kv,
        "conv_b": d_conv_b, "conv_w": d_conv_w.reshape(1, -1), "ffn_conv_b": d_ffn_conv_b,
        "ffn_conv_w": d_ffn_conv_w.reshape(1, -1),
    }
    return grad_x, loss, big, small


SMALL = (("dmod_lat", 6144), ("dmod_ctx", 6144), ("norm1_g", 1024), ("norm2_g", 1024), ("final_g", 1024),
         ("q_norm_g", 384), ("kv_norm_g", 256), ("conv_b", 512), ("conv_w", 1536), ("ffn_conv_b", 5632),
         ("ffn_conv_w", 16896), ("loss", 128))
SMALL_ROWS = 320


def _adam_update(w, g, m, v):
    c1, c2 = 1.0 - ADAM_B1 ** ADAM_STEP, 1.0 - ADAM_B2 ** ADAM_STEP
    m2 = ADAM_B1 * m + (1.0 - ADAM_B1) * g
    v2 = ADAM_B2 * v + (1.0 - ADAM_B2) * (g * g)
    return [-ADAM_LR * ((m2 / c1) / (jnp.sqrt(v2 / c2) + ADAM_EPS) + ADAM_WD * w), m2, v2]


def _adamw(w, g, m, v, name):
    R, C = w.shape
    tr = 8 if R % 8 == 0 else R
    for t in range(8, R + 1, 8):
        if R % t == 0 and t * C * 4 <= (1 << 20):
            tr = t
    spec = pl.BlockSpec((tr, C), lambda i: (i, 0))
    return _ew(lambda ids, *vals: _adam_update(*vals), (R // tr,), [(w, spec), (g, spec), (m, spec), (v, spec)],
               [((R, C), F32, spec, None)] * 3, name)


def kernel(x, c, ctx, c_ctx, w_ada, b_ada, norm1_g, w_in, q_norm_g, kv_norm_g, w_uq, w_ukv, conv_w, conv_b, w_attn_out, w_conv_out, w_o, norm2_g, w_up, ffn_conv_w, ffn_conv_b, w_down, final_g, loss_target, m_c_ctx, m_w_ada, m_b_ada, m_norm1_g, m_w_in, m_q_norm_g, m_kv_norm_g, m_w_uq, m_w_ukv, m_conv_w, m_conv_b, m_w_attn_out, m_w_conv_out, m_w_o, m_norm2_g, m_w_up, m_ffn_conv_w, m_ffn_conv_b, m_w_down, m_final_g, v_c_ctx, v_w_ada, v_b_ada, v_norm1_g, v_w_in, v_q_norm_g, v_kv_norm_g, v_w_uq, v_w_ukv, v_conv_w, v_conv_b, v_w_attn_out, v_w_conv_out, v_w_o, v_norm2_g, v_w_up, v_ffn_conv_w, v_ffn_conv_b, v_w_down, v_final_g):
    mx, my, mc = lax.axis_index("x"), lax.axis_index("y"), lax.axis_index("c")
    chip = 2 * mx + my
    dev = 4 * mx + 2 * my + mc
    T, Tc = x.shape[1], ctx.shape[1]
    TT = T + Tc
    w_in_t, m_w_in_t, v_w_in_t = (jnp.transpose(a[0]) for a in (w_in, m_w_in, v_w_in))
    w_uq_t, m_w_uq_t, v_w_uq_t = (jnp.transpose(a[0]) for a in (w_uq, m_w_uq, v_w_uq))
    conv_sh = jnp.concatenate([conv_w[0], ffn_conv_w[0]], axis=1)
    pay1 = jnp.concatenate([jnp.pad(c, ((0, 7), (0, 0))), jnp.pad(conv_sh, ((0, 5), (0, 0)))], axis=1)
    c_send, c_recv, c_src, c_land, zero0 = _ici_start("all", [pay1], [(8, 8, 2560)], jnp.zeros((8, 128), F32),
                                                      "cond_start")
    w_in_bf = (jnp.pad(w_in_t, ((0, W_IN_SHARD_PAD - W_IN_SHARD), (0, 0))) + zero0[0, 0]).astype(BF16)
    shards = {"w_in_a": w_in_bf[:W_IN_EARLY], "w_in_b": w_in_bf[W_IN_EARLY:], "w_uq": w_uq_t, "w_ukv": w_ukv[0],
              "w_attn_out": w_attn_out[0], "w_conv_out": w_conv_out[0], "w_o": w_o[0], "w_up": w_up[0],
              "w_down": w_down[0]}
    (pay1,), (c_land,) = _ici_wait("all", c_send, c_recv, c_src, c_land, w_in_bf, "cond_wait")
    got1 = lax.dynamic_update_slice(c_land, pay1[None], (dev, 0, 0))
    c_all = got1[:, 0, :D_MODEL]
    conv_all = got1[0::2, :3, D_MODEL:]
    conv_w_full = _cols_from_shards(conv_all[:, :, :128])
    ffn_conv_w_full = _cols_from_shards(conv_all[:, :, 128:])

    cond = jnp.concatenate([c_all, c_ctx.reshape(1, D_MODEL), jnp.zeros((7, D_MODEL), F32)], axis=0)

    def f_silu(ids, v):
        return (v * _sigmoid(v),)

    (s16,) = _ew(f_silu, (1,), [(cond, _full((16, D_MODEL)))], [((16, D_MODEL), F32, _full((16, D_MODEL)), None)], "silu_cond")
    mod_sh = _mm(s16, w_ada[0], "nn", 16, 1536, D_MODEL, tm=16, tn=768, tk=D_MODEL, name="w_ada_fwd")
    m_send, m_recv, m_src, m_land, zero1 = _ici_start("all", [mod_sh], [(8, 16, 1536)], jnp.zeros((8, 128), F32),
                                                      "mod_start")
    shards["w_ukv"] = w_ukv[0] + zero1[0, 0]

    first = ["w_in_a", "w_uq", "w_ukv"]
    gathered, zero = _gather_weights([shards[n].astype(BF16) for n in first])
    full = dict(zip(first, gathered))
    (mod_mine,), (m_land,) = _ici_wait("all", m_send, m_recv, m_src, m_land, gathered[0], "mod_wait")
    got2 = lax.dynamic_update_slice(m_land, mod_mine[None], (dev, 0, 0))
    mod_all = _cols_from_shards(got2[0::2]) + b_ada
    mod_lat = lax.dynamic_slice_in_dim(mod_all, dev, 1, axis=0)
    mod_ctx = mod_all[8:9]
    xx = jnp.concatenate([x[0], ctx[0]], axis=0)
    late_groups = {"g1": ("w_in_b", "w_attn_out", "w_conv_out", "w_o"), "g2": ("w_up", "w_down")}
    flight = {}
    for tag, group in late_groups.items():
        bf = [(shards[n] + zero[0, 0]).astype(BF16) for n in group]
        flight[tag] = _ici_start("gather", bf, [(4,) + s.shape for s in bf], xx, "gather_" + tag + "_start")
        xx = flight[tag][4]

    def chip_stage_done(tag, x):
        send, recv, src, land, _ = flight[tag]
        src, land = _ici_wait("gather", send, recv, src, land, x, "gather_" + tag + "_wait")
        flight[tag] = _ici_start("finish", src, None, x, "finish_" + tag + "_start", lands=land)
        return flight[tag][4]

    def arrived(tag, x):
        send, recv, src, land, _ = flight[tag]
        return dict(zip(late_groups[tag], _ici_wait("finish", send, recv, src, land, x, "finish_" + tag + "_wait")[1]))

    def late_weights(point, x):
        if point == "before_attn":
            return {}, chip_stage_done("g1", x)
        if point == "after_attn":
            got = arrived("g1", x)
            wao = _cols_from_shards(got["w_attn_out"]).reshape(N_HEADS, 64, D_MODEL)
            w_in_all = jnp.concatenate([full["w_in_a"], got["w_in_b"]], axis=1)
            ready = {"w_in_t": _w_in_t_p_from_shards(w_in_all),
                     "w_attn_out": jnp.pad(wao, ((0, 0), (64, 0), (0, 0))).reshape(N_HEADS * HEAD_PAD, D_MODEL),
                     "w_conv_out": got["w_conv_out"], "w_o": got["w_o"].reshape(D_MODEL, D_MODEL)}
            return ready, chip_stage_done("g2", x)
        got = arrived("g2", x)
        return {"w_up": got["w_up"], "w_down": got["w_down"].reshape(D_FF, D_MODEL)}, x

    wuq_t = full["w_uq"].reshape(N_HEADS, QK_DIM, Q_RANK)
    early_rows = full["w_in_a"][0]
    zrows = lambda n: jnp.zeros((n, D_MODEL), BF16)
    W = {
        "w_in_a_t": jnp.concatenate([early_rows[0:256], zrows(PA_Q0 - 256), early_rows[288:672], zrows(64),
                                     early_rows[256:288], zrows(32)], axis=0),
        "w_uq_t": jnp.pad(wuq_t, ((0, 0), (0, HEAD_PAD - QK_DIM), (0, 0))).reshape(N_HEADS * HEAD_PAD, Q_RANK),
        "w_ukv": full["w_ukv"],
        "norm1_g": norm1_g, "norm2_g": norm2_g, "final_g": final_g.reshape(1, D_MODEL), "q_norm_g": q_norm_g,
        "kv_norm_g": kv_norm_g, "conv_w": conv_w_full, "conv_b": conv_b, "ffn_conv_w": ffn_conv_w_full,
        "ffn_conv_b": ffn_conv_b,
    }

    place = jnp.stack([chip, mc]).astype(jnp.int32)
    early = {}

    pending = {}

    def scatter(tag, group, gs, from_sib, carry):
        if tag == "mid":
            sums = _add_pair_many(gs, from_sib, place, "rs_pair_add_mid")
        else:
            sums = [_add_pair(gs[w], from_sib[w], place, "rs_pair_add_" + n) for w, n in enumerate(group)]
        send, recv, sums, land, carry = _ici_start(
            "scatter", sums, [(3,) + s.shape[1:] for s in sums], carry, "rs_chips_" + tag + "_start")
        early[tag] = (group, send, recv, sums, land)
        return carry

    def early_grads(tag, g, carry, split=False):
        gs = list(g.values())
        if not split:
            return scatter(tag, list(g), gs, _rs_pair(gs, "rs_pair_" + tag), carry)
        send, recv, gs, land, carry = _ici_start(
            "pair", gs, [(4, s.shape[1] // 2, s.shape[2]) for s in gs], carry, "rs_pair_" + tag + "_start")
        pending[tag] = (list(g), send, recv, gs, land)
        return carry

    def early_continue(tag, carry):
        group, send, recv, gs, land = pending[tag]
        gs, from_sib = _ici_wait("pair", send, recv, gs, land, carry, "rs_pair_" + tag + "_wait")
        return scatter(tag, group, gs, from_sib, carry)

    grad_x, loss_part, gbig, gsmall = _local_step(xx, loss_target[0], mod_lat, mod_ctx, W, late_weights, early_grads,
                                                  early_continue)

    gsmall["loss"] = loss_part
    pay3 = jnp.concatenate([gsmall[n].reshape(-1) for n, _ in SMALL])
    pay3 = jnp.pad(pay3, (0, SMALL_ROWS * 128 - pay3.shape[0])).reshape(SMALL_ROWS, 128)
    s_send, s_recv, s_src, s_land, w_in_thru = _ici_start("all", [pay3], [(8, SMALL_ROWS, 128)], gbig["w_in"],
                                                         "small_start")
    gbig = {"w_in": w_in_thru}

    after_small = early_grads("last", gbig, s_src[0])

    (pay3,), (s_land,) = _ici_wait("all", s_send, s_recv, [after_small], s_land, early["last"][3][0], "small_wait")
    got3 = lax.dynamic_update_slice(s_land, pay3[None], (dev, 0, 0)).reshape(8 * SMALL_ROWS, 128)

    def f_sum8(ids, a):
        s = a[0:SMALL_ROWS]
        for d in range(1, 8):
            s = s + a[d * SMALL_ROWS:(d + 1) * SMALL_ROWS]
        return (s,)

    (vsum,) = _ew(f_sum8, (1,), [(got3, _full((8 * SMALL_ROWS, 128)))],
                  [((SMALL_ROWS, 128), F32, _full((SMALL_ROWS, 128)), None)], "sum_small")
    vflat = vsum.reshape(-1)
    gvec, off = {}, 0
    for n, size in SMALL:
        gvec[n] = vflat[off:off + size]
        off += size
    loss = gvec["loss"][0]
    dmod_rows = got3.reshape(8, SMALL_ROWS * 128)[:, :6 * D_MODEL]
    dm16 = jnp.concatenate([dmod_rows, gvec["dmod_ctx"].reshape(1, -1), jnp.zeros((7, 6 * D_MODEL), F32)], axis=0)

    def f_colsum(ids, a):
        return (_colsum(a),)

    (g_b_ada,) = _ew(f_colsum, (1,), [(dm16, _full((16, 6 * D_MODEL)))],
                     [((1, 6 * D_MODEL), F32, _full((1, 6 * D_MODEL)), None)], "b_ada_grad")
    dm_sh = lax.dynamic_slice_in_dim(dm16, chip * 1536, 1536, axis=1)
    g_w_ada = _mm(s16, dm_sh, "tn", D_MODEL, 1536, 16, tm=512, tn=768, tk=16, name="w_ada_dw")
    dcond_part = _mm(dm_sh, w_ada[0], "nt", 16, D_MODEL, 1536, tm=16, tn=512, tk=1536, name="w_ada_dx")
    d_send, d_recv, d_src, d_land, vsum = _ici_start("all", [dcond_part[8:16]], [(8, 8, D_MODEL)], vsum, "dcond_start")

    def finish_start(tags, after):
        done, halves = [], []
        for tag in tags:
            tag_names, send, recv, sums, land = early[tag]
            sums, land = _ici_wait("scatter", send, recv, sums, land, after, "rs_chips_" + tag + "_wait")
            done += tag_names
            if tag == "mid":
                halves += _add_chips_many(sums, land, place, "rs_chip_add_mid")
            else:
                halves += [_add_chips(a, b, place, "rs_chip_add_" + n) for a, b, n in zip(sums, land, tag_names)]
        send, recv, _, halves, _ = _ici_start("back", [], None, jnp.zeros((8, 128), F32), "rs_back_" + tags[0] + "_start",
                                              lands=halves)
        return done, send, recv, halves

    def finish_wait(state, after):
        done, send, recv, halves = state
        return dict(zip(done, _ici_wait("back", send, recv, [], halves, after, "rs_back_" + done[0] + "_wait")[1]))

    grads, deltas, new_m, new_v = {}, {}, {}, {}

    raw = {}

    def adam(n, w_, m_, v_, g, transposed):
        d_, m2, v2 = _adamw(w_, g, m_, v_, "adamw_" + n)
        raw[n] = d_
        back = (lambda a: jnp.transpose(a)[None]) if transposed else (lambda a: a[None])
        grads[n], deltas[n], new_m[n], new_v[n] = back(g[:w_.shape[0]]), back(d_), back(m2), back(v2)

    pending_back = finish_start(["late", "mid"], grad_x)
    adam("w_ada", w_ada[0], m_w_ada[0], v_w_ada[0], g_w_ada, False)
    gw = finish_wait(pending_back, raw["w_ada"])
    for n, (w_, m_, v_) in {"w_o": (w_o, m_w_o, v_w_o), "w_up": (w_up, m_w_up, v_w_up),
                            "w_down": (w_down, m_w_down, v_w_down)}.items():
        adam(n, w_[0], m_[0], v_[0], gw[n], False)
    pending_back = finish_start(["last"], raw["w_up"])

    (dcond_mine,), (d_land,) = _ici_wait("all", d_send, d_recv, d_src, d_land, raw["w_down"], "dcond_wait")
    got4 = lax.dynamic_update_slice(d_land, dcond_mine[None], (dev, 0, 0))[0::2, 0]

    def f_c_ctx(ids, parts, cc):
        s = _sigmoid(cc)
        d = parts[0:1] + parts[1:2] + parts[2:3] + parts[3:4]
        return (d * s * (1.0 + cc * (1.0 - s)),)

    (g_c_ctx,) = _ew(f_c_ctx, (1,), [(got4, _full((4, D_MODEL))), (c_ctx.reshape(1, D_MODEL), _full((1, D_MODEL)))],
                     [((1, D_MODEL), F32, _full((1, D_MODEL)), None)], "c_ctx_grad")

    conv_w_g = lax.dynamic_slice_in_dim(gvec["conv_w"].reshape(3, CONV_DIM), chip * 128, 128, axis=1)
    ffn_conv_w_g = lax.dynamic_slice_in_dim(gvec["ffn_conv_w"].reshape(3, 2 * D_FF), chip * 1408, 1408, axis=1)
    vec_params = (("c_ctx", c_ctx, m_c_ctx, v_c_ctx, g_c_ctx), ("b_ada", b_ada, m_b_ada, v_b_ada, g_b_ada),
                  ("norm1_g", norm1_g, m_norm1_g, v_norm1_g, gvec["norm1_g"]),
                  ("q_norm_g", q_norm_g, m_q_norm_g, v_q_norm_g, gvec["q_norm_g"]),
                  ("kv_norm_g", kv_norm_g, m_kv_norm_g, v_kv_norm_g, gvec["kv_norm_g"]),
                  ("conv_w", conv_w, m_conv_w, v_conv_w, conv_w_g), ("conv_b", conv_b, m_conv_b, v_conv_b, gvec["conv_b"]),
                  ("norm2_g", norm2_g, m_norm2_g, v_norm2_g, gvec["norm2_g"]),
                  ("ffn_conv_w", ffn_conv_w, m_ffn_conv_w, v_ffn_conv_w, ffn_conv_w_g),
                  ("ffn_conv_b", ffn_conv_b, m_ffn_conv_b, v_ffn_conv_b, gvec["ffn_conv_b"]),
                  ("final_g", final_g, m_final_g, v_final_g, gvec["final_g"]))
    two_d = lambda a: a.reshape((-1, a.shape[-1]))
    many = [p + ((lambda r, s=p[1].shape: r.reshape(s)),) for p in vec_params]
    for n, w_, m_, v_ in (("w_ukv", w_ukv, m_w_ukv, v_w_ukv), ("w_attn_out", w_attn_out, m_w_attn_out, v_w_attn_out),
                          ("w_conv_out", w_conv_out, m_w_conv_out, v_w_conv_out)):
        many.append((n, w_, m_, v_, gw[n], (lambda r, s=w_.shape: r.reshape(s))))
    many.append(("w_uq", w_uq_t, m_w_uq_t, v_w_uq_t, gw["w_uq"], lambda r: jnp.transpose(r)[None]))

    def f_adam_many(ids, *vals):
        out = []
        for k in range(len(many)):
            out += _adam_update(*vals[4 * k:4 * k + 4])
        return out

    ins_v, outs_v = [], []
    for p in many:
        shp = two_d(p[1]).shape
        ins_v += [(two_d(a), _full(shp)) for a in (p[1], p[4], p[2], p[3])]
        outs_v += [(shp, F32, _full(shp), None)] * 3
    res_v = _ew(f_adam_many, (1,), ins_v, outs_v, "adamw_small")
    for k, p in enumerate(many):
        n, post = p[0], p[5]
        grads[n] = post(two_d(p[4]))
        deltas[n], new_m[n], new_v[n] = (post(r) for r in res_v[3 * k:3 * k + 3])

    gw_in = finish_wait(pending_back, res_v[0])
    adam("w_in", w_in_t, m_w_in_t, v_w_in_t, gw_in["w_in"], True)

    order = ("c_ctx", "w_ada", "b_ada", "norm1_g", "w_in", "q_norm_g", "kv_norm_g", "w_uq", "w_ukv", "conv_w", "conv_b",
             "w_attn_out", "w_conv_out", "w_o", "norm2_g", "w_up", "ffn_conv_w", "ffn_conv_b", "w_down", "final_g")
    return (loss, grad_x[None], *[grads[n] for n in order], *[deltas[n] for n in order],
            *[new_m[n] for n in order], *[new_v[n] for n in order])
```

```python
import functools

import jax
import jax.numpy as jnp
import numpy as np
from jax import lax
from jax.experimental import pallas as pl
from jax.experimental.pallas import tpu as pltpu

F32, BF16 = jnp.float32, jnp.bfloat16
MESH = pl.DeviceIdType.MESH

D_MODEL = 1024
N_HEADS = 8
HEAD_PAD = 128
QK_DIM = 96
Q_RANK, KV_RANK = 384, 256
CONV_DIM = 512
D_FF = 2816
GRID_W = 64
ROPE_THETA = 10000.0
EPS = 1e-6
GA0, GC0, CX0, CB0, CC0, KV0, Q0, KR0, P_COLS = 0, 1024, 2048, 2560, 3072, 3584, 3840, 4224, 4352
PA_KV0, PA_Q0, PA_KR0, PA_COLS = 0, 384, 768, 896
ROW_TILE = 256
VMEM_LIMIT_BYTES = 48 * 1024 * 1024

ADAM_LR, ADAM_B1, ADAM_B2, ADAM_EPS, ADAM_WD, ADAM_STEP = 0.001, 0.9, 0.999, 1e-08, 0.01, 10

NN = (((1,), (0,)), ((), ()))
NT = (((1,), (1,)), ((), ()))
TN = (((0,), (0,)), ((), ()))


def _cp(sem):
    return pltpu.CompilerParams(dimension_semantics=sem, vmem_limit_bytes=VMEM_LIMIT_BYTES)


PIN_BYTES = 1 << 19


def _in_hbm(arrays):
    return [pltpu.with_memory_space_constraint(a, pltpu.HBM) if a.size * a.dtype.itemsize >= PIN_BYTES else a
            for a in arrays]


def _out(shape, dtype):
    n = 1
    for d in shape:
        n *= d
    big = n * jnp.dtype(dtype).itemsize >= PIN_BYTES
    return pltpu.HBM(shape, dtype) if big else jax.ShapeDtypeStruct(shape, dtype)


def _pick(n, prefs):
    for p in prefs:
        if n % p == 0:
            return p
    return n


def _mm(a, b, mode, M, N, K, *, tm, tn, tk, name, out_dtype=F32, a_spec=None, b_spec=None, o_spec=None,
        out_shape=None, transpose_out=False):
    assert M % tm == 0 and N % tn == 0 and K % tk == 0, (name, M, N, K, tm, tn, tk)
    nk = K // tk
    dims = {"nn": NN, "nt": NT, "tn": TN}[mode]
    if a_spec is None:
        a_spec = (pl.BlockSpec((tk, tm), lambda i, j, k: (k, i)) if mode == "tn"
                  else pl.BlockSpec((tm, tk), lambda i, j, k: (i, k)))
    if b_spec is None:
        b_spec = (pl.BlockSpec((tn, tk), lambda i, j, k: (j, k)) if mode == "nt"
                  else pl.BlockSpec((tk, tn), lambda i, j, k: (k, j)))
    if o_spec is None:
        o_spec = (pl.BlockSpec((tn, tm), lambda i, j, k: (j, i)) if transpose_out
                  else pl.BlockSpec((tm, tn), lambda i, j, k: (i, j)))
    if out_shape is None:
        out_shape = (N, M) if transpose_out else (M, N)

    def emit(o_ref, val):
        o_ref[...] = (val.T if transpose_out else val).astype(o_ref.dtype)

    def body(a_ref, b_ref, o_ref, *scratch):
        part = lax.dot_general(a_ref[...].astype(BF16), b_ref[...].astype(BF16), dims, preferred_element_type=F32)
        if nk == 1:
            emit(o_ref, part)
            return
        acc_ref, = scratch
        k = pl.program_id(2)

        @pl.when(k == 0)
        def _():
            acc_ref[...] = part

        @pl.when((k > 0) & (k < nk - 1))
        def _():
            acc_ref[...] += part

        @pl.when(k == nk - 1)
        def _():
            emit(o_ref, acc_ref[...] + part)

    return pl.pallas_call(
        body, grid=(M // tm, N // tn, nk), in_specs=[a_spec, b_spec], out_specs=o_spec,
        out_shape=_out(out_shape, out_dtype),
        scratch_shapes=[pltpu.VMEM((tm, tn), F32)] if nk > 1 else [],
        compiler_params=_cp(("parallel", "parallel", "arbitrary")), name=name)(*_in_hbm([a, b]))


def _ew(fn, grid, ins, outs, name, scalars=None):
    n_in = len(ins)
    n_sc = 0 if scalars is None else 1

    def store(ref, val, acc, ids):
        if isinstance(val, (list, tuple)):
            for h, v in enumerate(val):
                ref[h] = v.astype(ref.dtype)
            return
        if acc is None:
            ref[...] = val.astype(ref.dtype)
            return

        @pl.when(ids[acc] == 0)
        def _():
            ref[...] = val.astype(ref.dtype)

        @pl.when(ids[acc] > 0)
        def _():
            ref[...] += val.astype(ref.dtype)

    def body(*refs):
        refs = refs[n_sc:]
        ids = tuple(pl.program_id(a) for a in range(len(grid)))
        vals = fn(ids, *[r[...] for r in refs[:n_in]])
        for ref, val, (_, _, _, acc) in zip(refs[n_in:], vals, outs):
            store(ref, val, acc, ids)

    acc_axes = {o[3] for o in outs if o[3] is not None}
    sem = tuple("arbitrary" if a in acc_axes else "parallel" for a in range(len(grid)))
    in_specs, out_specs = [s for _, s in ins], [o[2] for o in outs]
    out_shape = [_out(o[0], o[1]) for o in outs]
    args = _in_hbm([a for a, _ in ins])
    if scalars is None:
        return pl.pallas_call(body, grid=grid, in_specs=in_specs, out_specs=out_specs, out_shape=out_shape,
                              compiler_params=_cp(sem), name=name)(*args)
    spec = pltpu.PrefetchScalarGridSpec(num_scalar_prefetch=1, grid=grid, in_specs=in_specs, out_specs=out_specs)
    return pl.pallas_call(body, grid_spec=spec, out_shape=out_shape, compiler_params=_cp(sem), name=name)(scalars, *args)


def _rows(width, cblk=0, roff=0, tr=ROW_TILE):
    return pl.BlockSpec((tr, width), lambda i: (i + roff, cblk))


def _full(shape):
    nd = len(shape)
    return pl.BlockSpec(shape, lambda *_: (0,) * nd)


def _sigmoid(x):
    return 1.0 / (1.0 + jnp.exp2(x * (-1.4426950408889634)))


def _rms(x):
    return lax.rsqrt(jnp.mean(x * x, axis=-1, keepdims=True) + EPS)


def _rms_bwd(dn, xn, r):
    return r * (dn - xn * jnp.mean(dn * xn, axis=-1, keepdims=True))


def _colsum(x):
    return jnp.sum(x, axis=0, keepdims=True)


def _shifts(x):
    n = x.shape[0]
    rows = lax.broadcasted_iota(jnp.int32, x.shape, 0)
    return jnp.where(rows == 0, 0.0, pltpu.roll(x, 1, 0)), jnp.where(rows == n - 1, 0.0, pltpu.roll(x, n - 1, 0))


def _conv(x, w, b, shifted=None):
    prev, nxt = _shifts(x) if shifted is None else shifted
    return b + prev * w[0:1] + x * w[1:2] + nxt * w[2:3]


def _conv_bwd_x(dy, w):
    prev, nxt = _shifts(dy)
    return nxt * w[0:1] + dy * w[1:2] + prev * w[2:3]


def _conv_bwd_w(dy, x, shifted):
    prev, nxt = shifted
    return _colsum(dy * prev), _colsum(dy * x), _colsum(dy * nxt)


def _rope(x, cos, sin_lo, sin_hi):
    return x * cos + pltpu.roll(x, HEAD_PAD - 8, 1) * sin_lo + pltpu.roll(x, 8, 1) * sin_hi


ATTN_SCALE = QK_DIM ** -0.5
LOG2_E = 1.4426950408889634


def _rope_t(x, tab, inverse=False):
    o = 3 * HEAD_PAD if inverse else 0
    return _rope(x, tab[:, o:o + HEAD_PAD], tab[:, o + HEAD_PAD:o + 2 * HEAD_PAD], tab[:, o + 2 * HEAD_PAD:o + 3 * HEAD_PAD])


def _heads_keys(hp, kv_ref, kr_ref, tab_ref, kc_ref, vp_ref):
    kr_roped = _rope_t(kr_ref[...], tab_ref[...])
    lane = lax.broadcasted_iota(jnp.int32, kr_roped.shape, 1)
    for u in range(hp):
        kv = kv_ref[:, u * HEAD_PAD:(u + 1) * HEAD_PAD]
        kc_ref[u] = jnp.where(lane < 64, kv, kr_roped).astype(BF16)
        vp_ref[u] = jnp.where(lane >= 64, kv, 0.0).astype(BF16)


ATTN_Q_TILE = 512
ATTN_HEADS_PER_STEP = 2


def _attn_specs(tq, TT):
    q = pl.BlockSpec((tq, HEAD_PAD), lambda h, i: (i, h))
    keys = pl.BlockSpec((TT, HEAD_PAD), lambda h, i: (0, h))
    kr = pl.BlockSpec((TT, HEAD_PAD), lambda h, i: (0, PA_KR0 // HEAD_PAD))
    tab_q = pl.BlockSpec((tq, 6 * HEAD_PAD), lambda h, i: (i, 0))
    tab_k = pl.BlockSpec((TT, 6 * HEAD_PAD), lambda h, i: (0, 0))
    return q, keys, kr, tab_q, tab_k


def _attn_fwd(q_raw, kv, pp, tab, T, TT):
    tq, hp = ROW_TILE, 2 * ATTN_HEADS_PER_STEP
    w = hp * HEAD_PAD

    def body(q_ref, kv_ref, kr_ref, tq_ref, tk_ref, o_ref, kc, vp):
        @pl.when(pl.program_id(1) == 0)
        def _():
            _heads_keys(hp, kv_ref, kr_ref, tk_ref, kc, vp)

        tab = tq_ref[...]
        for u in range(hp):
            cols = slice(u * HEAD_PAD, (u + 1) * HEAD_PAD)
            q = _rope_t(q_ref[:, cols], tab).astype(BF16)
            s = lax.dot_general(q, kc[u], NT, preferred_element_type=F32)
            m = jnp.max(s, axis=-1, keepdims=True)
            p = jnp.exp2((s - m) * (ATTN_SCALE * LOG2_E))
            l = jnp.sum(p, axis=-1, keepdims=True)
            o = lax.dot_general(p.astype(BF16), vp[u], NN, preferred_element_type=F32)
            lane = lax.broadcasted_iota(jnp.int32, o.shape, 1)
            o_ref[:, cols] = jnp.where(lane < 64, m * ATTN_SCALE + jnp.log(l), o / l)

    _, _, kr, _, _ = _attn_specs(tq, TT)
    qs = pl.BlockSpec((tq, w), lambda h, i: (i, h))
    keys = pl.BlockSpec((TT, w), lambda h, i: (0, h))
    tab_q = pl.BlockSpec((tq, 3 * HEAD_PAD), lambda h, i: (i, 0))
    tab_k = pl.BlockSpec((TT, 3 * HEAD_PAD), lambda h, i: (0, 0))
    return pl.pallas_call(
        body, grid=(N_HEADS // hp, T // tq), in_specs=[qs, keys, kr, tab_q, tab_k], out_specs=qs,
        out_shape=jax.ShapeDtypeStruct((T, N_HEADS * HEAD_PAD), F32),
        scratch_shapes=[pltpu.VMEM((hp, TT, HEAD_PAD), BF16), pltpu.VMEM((hp, TT, HEAD_PAD), BF16)],
        compiler_params=_cp(("parallel", "arbitrary")), name="attn_fwd",
    )(*_in_hbm([q_raw, kv, pp, tab, tab]))


def _attn_bwd(q_raw, kv, pp, o, do, tab, T, TT):
    tq = _pick(T, (ATTN_Q_TILE, ROW_TILE))
    nq = T // tq
    hp = ATTN_HEADS_PER_STEP
    w = hp * HEAD_PAD

    def body(q_ref, kv_ref, kr_ref, tq_ref, tk_ref, o_ref, do_ref, dq_ref, dkv_ref, dkr_ref, kc, vp, dk, dv):
        g, i = pl.program_id(0), pl.program_id(1)

        @pl.when(i == 0)
        def _():
            _heads_keys(hp, kv_ref, kr_ref, tk_ref, kc, vp)
            dk[...] = jnp.zeros_like(dk)
            dv[...] = jnp.zeros_like(dv)

        tab = tq_ref[...]
        for u in range(hp):
            cols = slice(u * HEAD_PAD, (u + 1) * HEAD_PAD)
            q = _rope_t(q_ref[:, cols], tab).astype(BF16)
            k, v, d_o = kc[u], vp[u], do_ref[:, cols]
            s = lax.dot_general(q, k, NT, preferred_element_type=F32)
            o = o_ref[:, cols]
            p = jnp.exp2(s * (ATTN_SCALE * LOG2_E) - o[:, 0:1] * LOG2_E)
            dob = d_o.astype(BF16)
            dp = lax.dot_general(dob, v, NT, preferred_element_type=F32)
            dd = jnp.sum(d_o * o, axis=-1, keepdims=True)
            ds = (p * (dp - dd) * ATTN_SCALE).astype(BF16)
            dq = lax.dot_general(ds, k, NN, preferred_element_type=F32)
            dq_ref[:, cols] = _rope_t(dq, tab, inverse=True).astype(dq_ref.dtype)
            dk[u] += lax.dot_general(q, ds, TN, preferred_element_type=F32)
            dv[u] += lax.dot_general(dob, p.astype(BF16), TN, preferred_element_type=F32)

        @pl.when(i == nq - 1)
        def _():
            rot = None
            for u in range(hp):
                dkh = dk[u].T
                lane = lax.broadcasted_iota(jnp.int32, dkh.shape, 1)
                dkv_ref[:, u * HEAD_PAD:(u + 1) * HEAD_PAD] = jnp.where(lane < 64, dkh, dv[u].T).astype(dkv_ref.dtype)
                part = jnp.where((lane >= 64) & (lane < 96), dkh, 0.0)
                rot = part if rot is None else rot + part
            rot = _rope_t(rot, tk_ref[...], inverse=True)

            @pl.when(g == 0)
            def _():
                dkr_ref[...] = rot

            @pl.when(g > 0)
            def _():
                dkr_ref[...] += rot

    _, _, kr, tab_q, tab_k = _attn_specs(tq, TT)
    qs = pl.BlockSpec((tq, w), lambda h, i: (i, h))
    keys = pl.BlockSpec((TT, w), lambda h, i: (0, h))
    wide = lambda rows: jax.ShapeDtypeStruct((rows, N_HEADS * HEAD_PAD), BF16)
    return pl.pallas_call(
        body, grid=(N_HEADS // hp, nq),
        in_specs=[qs, keys, kr, tab_q, tab_k, qs, qs],
        out_specs=[qs, keys, pl.BlockSpec((TT, HEAD_PAD), lambda h, i: (0, 0))],
        out_shape=[wide(T), wide(TT), jax.ShapeDtypeStruct((TT, HEAD_PAD), F32)],
        scratch_shapes=[pltpu.VMEM((hp, TT, HEAD_PAD), BF16), pltpu.VMEM((hp, TT, HEAD_PAD), BF16),
                        pltpu.VMEM((hp, HEAD_PAD, TT), F32), pltpu.VMEM((hp, HEAD_PAD, TT), F32)],
        compiler_params=_cp(("arbitrary", "arbitrary")), name="attn_bwd",
    )(*_in_hbm([q_raw, kv, pp, tab, tab, o, do]))


def _hbm_specs(n):
    return [pl.BlockSpec(memory_space=pl.ANY)] * n


def _gather_weights(shards):
    n = len(shards)
    halves = [s.shape[0] // 2 for s in shards]

    def body(*refs):
        ins, outs = refs[:n], refs[n:2 * n]
        token, send_sems, recv_sems = refs[2 * n:]
        token[...] = jnp.zeros_like(token)
        mx, my, mc = lax.axis_index("x"), lax.axis_index("y"), lax.axis_index("c")
        j_me = 2 * mx + my
        chips = [(1 - mx, my), (mx, 1 - my), (1 - mx, 1 - my)]

        def half(w, chip_idx, hc):
            return outs[w].at[chip_idx, pl.ds(hc * halves[w], halves[w]), :]

        def copy(w, k, src, dst, to):
            return pltpu.make_async_remote_copy(src_ref=src, dst_ref=dst, send_sem=send_sems.at[w, k],
                                                recv_sem=recv_sems.at[w, k], device_id=to, device_id_type=MESH)

        sends = []
        for w in range(n):
            cp = copy(w, 6, ins[w], outs[w].at[j_me], (mx, my, 1 - mc))
            cp.start()
            sends.append(cp)
        for k, (px, py) in enumerate(chips):
            for w in range(n):
                cp = copy(w, k, ins[w].at[pl.ds(mc * halves[w], halves[w]), :], half(w, j_me, mc), (px, py, mc))
                cp.start()
                sends.append(cp)
        for k, (px, py) in enumerate(chips):
            for w in range(n):
                got = half(w, 2 * px + py, mc)
                copy(w, k, got, got, (px, py, mc)).wait_recv()
                cp = copy(w, 3 + k, got, got, (mx, my, 1 - mc))
                cp.start()
                sends.append(cp)
        for k, (px, py) in enumerate(chips):
            for w in range(n):
                got = half(w, 2 * px + py, 1 - mc)
                copy(w, 3 + k, got, got, (mx, my, 1 - mc)).wait_recv()
        for w in range(n):
            own = outs[w].at[j_me]
            copy(w, 6, own, own, (mx, my, 1 - mc)).wait_recv()
        for cp in sends:
            cp.wait_send()

    res = pl.pallas_call(
        body, out_shape=[jax.ShapeDtypeStruct((4,) + s.shape, s.dtype) for s in shards]
        + [jax.ShapeDtypeStruct((8, 128), F32)],
        in_specs=_hbm_specs(n), out_specs=_hbm_specs(n) + [pl.BlockSpec(memory_space=pltpu.VMEM)],
        scratch_shapes=[pltpu.SemaphoreType.DMA((n, 7)), pltpu.SemaphoreType.DMA((n, 7))],
        name="gather_weights")(*shards)
    return list(res[:n]), res[n]


def _rs_pair(gs, name):
    n = len(gs)
    halves = [g.shape[1] // 2 for g in gs]

    def body(*refs):
        ins, lands = refs[:n], refs[n:2 * n]
        send_sems, recv_sems = refs[2 * n:]
        mx, my, mc = lax.axis_index("x"), lax.axis_index("y"), lax.axis_index("c")
        copies = []
        for w in range(n):
            h = halves[w]
            cp = pltpu.make_async_remote_copy(
                src_ref=ins[w].at[:, pl.ds((1 - mc) * h, h), :], dst_ref=lands[w], send_sem=send_sems.at[w],
                recv_sem=recv_sems.at[w], device_id=(mx, my, 1 - mc), device_id_type=MESH)
            cp.start()
            copies.append(cp)
        for cp in copies:
            cp.wait()

    return pl.pallas_call(
        body, out_shape=[jax.ShapeDtypeStruct((4, h, g.shape[2]), g.dtype) for g, h in zip(gs, halves)],
        in_specs=_hbm_specs(n), out_specs=_hbm_specs(n),
        scratch_shapes=[pltpu.SemaphoreType.DMA((n,)), pltpu.SemaphoreType.DMA((n,))], name=name)(*gs)


def _rs_chips(parts):
    n = len(parts)

    def body(*refs):
        ins, lands = refs[:n], refs[n:2 * n]
        send_sems, recv_sems = refs[2 * n:]
        mx, my, mc = lax.axis_index("x"), lax.axis_index("y"), lax.axis_index("c")
        copies = []
        for k, (px, py) in enumerate([(1 - mx, my), (mx, 1 - my), (1 - mx, 1 - my)]):
            for w in range(n):
                cp = pltpu.make_async_remote_copy(
                    src_ref=ins[w].at[2 * px + py], dst_ref=lands[w].at[k], send_sem=send_sems.at[w, k],
                    recv_sem=recv_sems.at[w, k], device_id=(px, py, mc), device_id_type=MESH)
                cp.start()
                copies.append(cp)
        for cp in copies:
            cp.wait()

    return list(pl.pallas_call(
        body, out_shape=[jax.ShapeDtypeStruct((3,) + p.shape[1:], p.dtype) for p in parts],
        in_specs=_hbm_specs(n), out_specs=_hbm_specs(n),
        scratch_shapes=[pltpu.SemaphoreType.DMA((n, 3)), pltpu.SemaphoreType.DMA((n, 3))], name="rs_chips")(*parts))


_HBM = pl.BlockSpec(memory_space=pltpu.HBM)
_SEM = pl.BlockSpec(memory_space=pltpu.SEMAPHORE)
_EFFECT = pltpu.SideEffectType.DATAFLOW_SIDE_EFFECTING


def _ici_copies(kind, srcs, lands, send_sems, recv_sems):
    n = len(lands)
    mx, my, mc = lax.axis_index("x"), lax.axis_index("y"), lax.axis_index("c")
    j_me = 2 * mx + my
    copies = []
    if kind == "back":
        for w in range(n):
            h = lands[w].shape[0] // 2
            mine = lands[w].at[pl.ds(mc * h, h), :]
            copies.append(pltpu.make_async_remote_copy(
                src_ref=mine, dst_ref=mine, send_sem=send_sems.at[w], recv_sem=recv_sems.at[w],
                device_id=(mx, my, 1 - mc), device_id_type=MESH))
        return copies
    if kind == "all":
        for k in range(7):
            a, b, c = (k + 1) >> 2 & 1, (k + 1) >> 1 & 1, (k + 1) & 1
            peer = (1 - mx if a else mx, 1 - my if b else my, 1 - mc if c else mc)
            for w in range(n):
                copies.append(pltpu.make_async_remote_copy(
                    src_ref=srcs[w], dst_ref=lands[w].at[4 * mx + 2 * my + mc], send_sem=send_sems.at[7 * w + k],
                    recv_sem=recv_sems.at[7 * w + k], device_id=peer, device_id_type=MESH))
        return copies
    if kind == "pair":
        for w in range(n):
            h = srcs[w].shape[1] // 2
            copies.append(pltpu.make_async_remote_copy(
                src_ref=srcs[w].at[:, pl.ds((1 - mc) * h, h), :], dst_ref=lands[w], send_sem=send_sems.at[w],
                recv_sem=recv_sems.at[w], device_id=(mx, my, 1 - mc), device_id_type=MESH))
        return copies
    chips = [(1 - mx, my), (mx, 1 - my), (1 - mx, 1 - my)]
    if kind == "finish":
        for w in range(n):
            h = srcs[w].shape[0] // 2
            pushes = [(lands[w].at[2 * px + py, pl.ds(mc * h, h), :],) * 2 for px, py in chips]
            pushes.append((srcs[w], lands[w].at[j_me]))
            for k, (src, dst) in enumerate(pushes):
                copies.append(pltpu.make_async_remote_copy(
                    src_ref=src, dst_ref=dst, send_sem=send_sems.at[4 * w + k], recv_sem=recv_sems.at[4 * w + k],
                    device_id=(mx, my, 1 - mc), device_id_type=MESH))
        return copies
    for k, (px, py) in enumerate(chips):
        for w in range(n):
            if kind == "gather":
                h = srcs[w].shape[0] // 2
                src, dst = srcs[w].at[pl.ds(mc * h, h), :], lands[w].at[j_me, pl.ds(mc * h, h), :]
            else:
                src, dst = srcs[w].at[2 * px + py], lands[w].at[k]
            copies.append(pltpu.make_async_remote_copy(
                src_ref=src, dst_ref=dst, send_sem=send_sems.at[3 * w + k], recv_sem=recv_sems.at[3 * w + k],
                device_id=(px, py, mc), device_id_type=MESH))
    return copies


_SEMS_PER_OPERAND = {"gather": 3, "scatter": 3, "all": 7, "pair": 1, "finish": 4, "back": 1}


def _ici_start(kind, srcs, land_shapes, carry, name, lands=None):
    hbm = lambda a: pltpu.with_memory_space_constraint(a, pltpu.HBM)
    if lands is None:
        lands = [lax.empty(s, srcs[0].dtype) for s in land_shapes]
    ns, nl = len(srcs), len(lands)

    def body(*refs):
        send_sems, recv_sems = refs[ns + nl + 1], refs[ns + nl + 2]
        for cp in _ici_copies(kind, refs[:ns], refs[ns:ns + nl], send_sems, recv_sems):
            cp.start()

    args = [hbm(a) for a in list(srcs) + list(lands) + [carry]]
    n_sem = _SEMS_PER_OPERAND[kind] * nl
    out_shape = ([pltpu.SemaphoreType.DMA((n_sem,)), pltpu.SemaphoreType.DMA((n_sem,))]
                 + [pltpu.HBM(a.shape, a.dtype) for a in args])
    res = pl.pallas_call(
        body, name=name, out_shape=out_shape, in_specs=[_HBM] * len(args), out_specs=[_SEM, _SEM] + [_HBM] * len(args),
        input_output_aliases={i: 2 + i for i in range(len(args))},
        compiler_params=pltpu.CompilerParams(has_side_effects=_EFFECT))(*args)
    return res[0], res[1], list(res[2:2 + ns]), list(res[2 + ns:2 + ns + nl]), res[2 + ns + nl]


def _ici_wait(kind, send_sems, recv_sems, srcs, lands, after, name):
    ns, nl = len(srcs), len(lands)

    def body(*refs):
        for cp in _ici_copies(kind, refs[:ns], refs[ns:ns + nl], refs[ns + nl], refs[ns + nl + 1]):
            cp.wait_send()
            cp.wait_recv()

    args = list(srcs) + list(lands)
    res = pl.pallas_call(
        body, name=name, out_shape=[pltpu.HBM(a.shape, a.dtype) for a in args],
        in_specs=[_HBM] * len(args) + [_SEM, _SEM, pl.BlockSpec(memory_space=pl.ANY)], out_specs=[_HBM] * len(args),
        input_output_aliases={i: i for i in range(len(args))},
        compiler_params=pltpu.CompilerParams(has_side_effects=_EFFECT))(*args, send_sems, recv_sems, after)
    return list(res[:ns]), list(res[ns:])


def _tile_rows(h, c, itemsize, mult):
    best = h
    for t in range(mult, h + 1, mult):
        if h % t == 0 and t * c * itemsize <= (1 << 21):
            best = t
    return best


def _add_pair(g, land, place, name):
    _, h, c = land.shape
    t = _tile_rows(h, c, 2, 16)
    nb = h // t
    return _ew(lambda ids, u, v: (u.astype(F32) + v.astype(F32),), (4, nb),
               [(g, pl.BlockSpec((None, t, c), lambda j, i, s: (j, s[1] * nb + i, 0))),
                (land, pl.BlockSpec((None, t, c), lambda j, i, s: (j, i, 0)))],
               [(land.shape, BF16, pl.BlockSpec((None, t, c), lambda j, i, s: (j, i, 0)), None)], name, scalars=place)[0]


def _add_pair_many(gs, lands, place, name):
    ins, outs = [], []
    for g, l in zip(gs, lands):
        ins += [(g, pl.BlockSpec(l.shape, lambda i, s: (0, s[1], 0))), (l, pl.BlockSpec(l.shape, lambda i, s: (0, 0, 0)))]
        outs.append((l.shape, BF16, pl.BlockSpec(l.shape, lambda i, s: (0, 0, 0)), None))
    fn = lambda ids, *v: [v[2 * k].astype(F32) + v[2 * k + 1].astype(F32) for k in range(len(gs))]
    return list(_ew(fn, (1,), ins, outs, name, scalars=place))


def _add_chips_many(owns, lands, place, name):
    ins, outs = [], []
    for own, land in zip(owns, lands):
        _, h, c = land.shape
        ins += [(own, pl.BlockSpec((None, h, c), lambda i, s: (s[0], 0, 0))),
                (land, pl.BlockSpec((3, h, c), lambda i, s: (0, 0, 0)))]
        outs.append(((2 * h, c), F32, pl.BlockSpec((h, c), lambda i, s: (s[1], 0)), None))

    def fn(ids, *v):
        return [((v[2 * k].astype(F32) + v[2 * k + 1][0].astype(F32)) + v[2 * k + 1][1].astype(F32))
                + v[2 * k + 1][2].astype(F32) for k in range(len(owns))]

    return list(_ew(fn, (1,), ins, outs, name, scalars=place))


def _add_chips(own, land, place, name):
    _, h, c = land.shape
    t = _tile_rows(h, c, 4, 16)
    nb = h // t

    def fn(ids, a, b):
        return (((a.astype(F32) + b[0].astype(F32)) + b[1].astype(F32)) + b[2].astype(F32),)

    return _ew(fn, (nb,), [(own, pl.BlockSpec((None, t, c), lambda i, s: (s[0], i, 0))),
                           (land, pl.BlockSpec((3, t, c), lambda i, s: (0, i, 0)))],
               [((2 * h, c), F32, pl.BlockSpec((t, c), lambda i, s: (s[1] * nb + i, 0)), None)], name, scalars=place)[0]


W_IN_SEGMENTS = ((0, 256, KV0), (256, 288, KR0 + 64), (288, 672, Q0), (672, 1184, CX0), (1184, 1696, CB0),
                 (1696, 2208, CC0), (2208, 3232, GA0), (3232, 4256, GC0))
W_IN_SHARD = 1064


W_IN_SHARD_PAD = 1088
W_IN_EARLY = 672


def _w_in_t_p_from_shards(s):
    pieces = []
    for o0, o1, p0 in sorted(W_IN_SEGMENTS, key=lambda t: t[2]):
        if p0 == KR0 + 64:
            pieces.append(jnp.zeros((64, s.shape[2]), s.dtype))
        for j in range(4):
            lo, hi = max(o0, j * W_IN_SHARD), min(o1, (j + 1) * W_IN_SHARD)
            if lo < hi:
                pieces.append(s[j, lo - j * W_IN_SHARD:hi - j * W_IN_SHARD])
    pieces.append(jnp.zeros((32, s.shape[2]), s.dtype))
    return jnp.concatenate(pieces, axis=0)


def _w_in_t_shards_from_p(g):
    shards = []
    for j in range(4):
        pieces = []
        for o0, o1, p0 in W_IN_SEGMENTS:
            lo, hi = max(o0, j * W_IN_SHARD), min(o1, (j + 1) * W_IN_SHARD)
            if lo < hi:
                pieces.append(g[p0 + lo - o0:p0 + hi - o0])
        pieces.append(jnp.zeros((W_IN_SHARD_PAD - W_IN_SHARD, g.shape[1]), g.dtype))
        shards.append(jnp.concatenate(pieces, axis=0))
    return jnp.stack(shards, axis=0)


def _cols_from_shards(s):
    return jnp.transpose(s, (1, 0, 2)).reshape(s.shape[1], -1)


def _rope_tables(T, TT, inverse):
    f32 = np.float32
    rows = T // GRID_W
    row = np.repeat(np.arange(rows), GRID_W).astype(f32)
    col = np.tile(np.arange(GRID_W), rows).astype(f32)
    inv = (f32(ROPE_THETA) ** (-np.arange(0, 16, 2, dtype=f32) / f32(16))).astype(f32)
    ang = np.concatenate([row[:, None] * inv, col[:, None] * inv], axis=-1).astype(f32)
    cos, sin = np.cos(ang).astype(f32), np.sin(ang).astype(f32)
    lane = np.arange(32)
    src = (lane // 16) * 8 + lane % 8
    lo = ((lane % 16) // 8 == 0).astype(f32)
    sgn = f32(-1.0 if inverse else 1.0)
    cos32 = cos[:, src]
    sin_lo32 = -sgn * sin[:, src] * lo
    sin_hi32 = sgn * sin[:, src] * (1 - lo)

    def widen(t32, fill):
        t = np.concatenate([np.full((T, 64), fill, f32), t32, np.full((T, 32), fill, f32)], axis=1)
        return np.concatenate([t, np.full((TT - T, HEAD_PAD), fill, f32)], axis=0)

    return [widen(cos32, 1.0), widen(sin_lo32, 0.0), widen(sin_hi32, 0.0)]


def _rope_table(T, TT):
    return jnp.asarray(np.concatenate(_rope_tables(T, TT, False) + _rope_tables(T, TT, True), axis=1))


def _local_step(xx, tgt, mod_lat, mod_ctx, W, late_weights, early_grads, early_continue):
    TT = xx.shape[0]
    T = tgt.shape[0]
    n_lat, n_all = T // ROW_TILE, TT // ROW_TILE
    sh1, sc1, g1, sh2, sc2, g2 = [mod_lat[:, k * D_MODEL:(k + 1) * D_MODEL] for k in range(6)]
    csh1, csc1 = mod_ctx[:, :D_MODEL], mod_ctx[:, D_MODEL:2 * D_MODEL]
    vec = lambda n: _full((1, n))
    row_out = lambda n, dt, rows=T: ((rows, n), dt, _rows(n), None)
    acc_out = lambda n: ((1, n), F32, _full((1, n)), 0)

    def f_norm1(ids, x, g, a_sh, a_sc, b_sh, b_sc):
        ctx = ids[0] >= n_lat
        sh, sc = jnp.where(ctx, b_sh, a_sh), jnp.where(ctx, b_sc, a_sc)
        return ((x * _rms(x) * g) * (1.0 + sc) + sh,)

    (hh,) = _ew(f_norm1, (n_all,), [(xx, _rows(D_MODEL)), (W["norm1_g"], vec(D_MODEL)), (sh1, vec(D_MODEL)),
                                   (sc1, vec(D_MODEL)), (csh1, vec(D_MODEL)), (csc1, vec(D_MODEL))],
                [row_out(D_MODEL, BF16, TT)], "norm1_fwd")
    tm_all = _pick(TT, (768, 256))
    pp_a = _mm(hh, W["w_in_a_t"], "nt", TT, PA_COLS, D_MODEL, tm=tm_all, tn=PA_COLS, tk=D_MODEL, name="w_in_fwd_a")

    def f_lowrank(ids, ckv, cq, gkv, gq):
        return ckv * _rms(ckv) * gkv, cq * _rms(cq) * gq

    nkv, nq = _ew(f_lowrank, (n_all,), [(pp_a, _rows(KV_RANK, PA_KV0 // KV_RANK)), (pp_a, _rows(Q_RANK, PA_Q0 // Q_RANK)),
                                       (W["kv_norm_g"], vec(KV_RANK)), (W["q_norm_g"], vec(Q_RANK))],
                  [row_out(KV_RANK, BF16, TT), row_out(Q_RANK, BF16, TT)], "lowrank_norm_fwd")
    kv = _mm(nkv, W["w_ukv"], "nn", TT, 1024, KV_RANK, tm=tm_all, tn=256, tk=KV_RANK, name="w_ukv_fwd",
             b_spec=pl.BlockSpec((None, KV_RANK, 256), lambda i, j, k: (j, k, 0)))
    q_raw = _mm(nq, W["w_uq_t"], "nt", TT, 1024, Q_RANK, tm=tm_all, tn=1024, tk=Q_RANK, name="w_uq_fwd")

    tab = _rope_table(T, TT)
    _, q_raw = late_weights("before_attn", q_raw)
    o_pad = _attn_fwd(q_raw, kv, pp_a, tab, T, TT)
    arrived, o_pad = late_weights("after_attn", o_pad)
    W = dict(W, **arrived)
    tm_lat = _pick(T, (1024, 512, 256))
    pp = _mm(hh, W["w_in_t"], "nt", T, KV0, D_MODEL, tm=tm_lat, tn=KV0 // 2, tk=D_MODEL, name="w_in_fwd_b")
    ya = _mm(o_pad, W["w_attn_out"], "nn", T, D_MODEL, 1024, tm=tm_lat, tn=D_MODEL, tk=1024, name="w_attn_out_fwd",
             out_dtype=BF16)

    tc = 256
    colT = lambda blk0: pl.BlockSpec((T, tc), lambda j: (0, blk0 + j))

    def f_conv(ids, xin, cb, cc, w, b):
        return (cb * _conv(cc * xin, w, b),)

    (e,) = _ew(f_conv, (CONV_DIM // tc,),
               [(pp, colT(CX0 // tc)), (pp, colT(CB0 // tc)), (pp, colT(CC0 // tc)),
                (W["conv_w"], pl.BlockSpec((3, tc), lambda j: (0, j))), (W["conv_b"], pl.BlockSpec((1, tc), lambda j: (0, j)))],
               [((T, CONV_DIM), BF16, colT(0), None)], "conv_fwd")
    yc = _mm(e, W["w_conv_out"], "nn", T, D_MODEL, CONV_DIM, tm=tm_lat, tn=256, tk=CONV_DIM, name="w_conv_out_fwd",
             out_dtype=BF16, b_spec=pl.BlockSpec((None, CONV_DIM, 256), lambda i, j, k: (j, k, 0)))

    def f_merge(ids, ga, gc, a, c):
        return (_sigmoid(ga) * a.astype(F32) + _sigmoid(gc) * c.astype(F32),)

    (mrg,) = _ew(f_merge, (n_lat,), [(pp, _rows(D_MODEL, 0)), (pp, _rows(D_MODEL, 1)), (ya, _rows(D_MODEL)),
                                    (yc, _rows(D_MODEL))], [row_out(D_MODEL, BF16)], "merge_fwd")
    mo = _mm(mrg, W["w_o"], "nn", T, D_MODEL, D_MODEL, tm=tm_lat, tn=D_MODEL, tk=D_MODEL, name="w_o_fwd")

    def f_norm2(ids, x, m, gate, g, sh, sc):
        x1 = x + gate * m
        return x1, (x1 * _rms(x1) * g) * (1.0 + sc) + sh

    x1, h2 = _ew(f_norm2, (n_lat,), [(xx, _rows(D_MODEL)), (mo, _rows(D_MODEL)), (g1, vec(D_MODEL)),
                                    (W["norm2_g"], vec(D_MODEL)), (sh2, vec(D_MODEL)), (sc2, vec(D_MODEL))],
                 [row_out(D_MODEL, F32), row_out(D_MODEL, BF16)], "norm2_fwd")
    arrived, h2 = late_weights("before_ffn", h2)
    W = dict(W, **arrived)
    up = _mm(h2, W["w_up"], "nn", T, 2 * D_FF, D_MODEL, tm=tm_lat, tn=1408, tk=D_MODEL, name="w_up_fwd",
             b_spec=pl.BlockSpec((None, D_MODEL, 1408), lambda i, j, k: (j, k, 0)))

    n_ff = D_FF // tc
    ffw = lambda off, n=3: pl.BlockSpec((n, tc), lambda j: (0, j + off))

    def f_ffn(ids, ug, uv, wg, wv, bg, bv):
        gate, val = _conv(ug, wg, bg), _conv(uv, wv, bv)
        return (gate * _sigmoid(gate) * val,)

    (act,) = _ew(f_ffn, (n_ff,), [(up, colT(0)), (up, colT(n_ff)), (W["ffn_conv_w"], ffw(0)), (W["ffn_conv_w"], ffw(n_ff)),
                                 (W["ffn_conv_b"], ffw(0, 1)), (W["ffn_conv_b"], ffw(n_ff, 1))],
                 [((T, D_FF), BF16, colT(0), None)], "ffn_act_fwd")
    f = _mm(act, W["w_down"], "nn", T, D_MODEL, D_FF, tm=tm_lat, tn=D_MODEL, tk=D_FF, name="w_down_fwd")

    def f_head(ids, x1_, f_, gate, gf, t):
        x2 = x1_ + gate * f_
        r = _rms(x2)
        xn = x2 * r
        err = xn * gf - t
        loss = 0.5 * jnp.sum(jnp.mean(err * err, axis=-1, keepdims=True))
        dy = err * (1.0 / D_MODEL)
        dx2 = _rms_bwd(dy * gf, xn, r)
        return dx2, dx2 * gate, _colsum(dy * xn), _colsum(dx2 * f_), jnp.full((1, 128), loss, F32)

    dx2, df, dg_f, dg2, loss = _ew(
        f_head, (n_lat,), [(x1, _rows(D_MODEL)), (f, _rows(D_MODEL)), (g2, vec(D_MODEL)), (W["final_g"], vec(D_MODEL)),
                           (tgt, _rows(D_MODEL))],
        [row_out(D_MODEL, F32), row_out(D_MODEL, BF16), acc_out(D_MODEL), acc_out(D_MODEL), acc_out(128)], "loss_head")

    d_w_down = _mm(act, df, "tn", D_FF, D_MODEL, T, tm=1408, tn=D_MODEL, tk=T, name="w_down_dw",
                   out_dtype=BF16).reshape(4, D_FF // 4, D_MODEL)
    da = _mm(df, W["w_down"], "nt", T, D_FF, D_MODEL, tm=tm_lat, tn=1408, tk=D_MODEL, name="w_down_dx")

    tcb = 128
    n_fb = D_FF // tcb
    colb = lambda blk0: pl.BlockSpec((T, tcb), lambda j: (0, blk0 + j))
    ffwb = lambda off, n=3: pl.BlockSpec((n, tcb), lambda j: (0, j + off))
    cvec = ((1, D_FF), F32, pl.BlockSpec((1, tcb), lambda j: (0, j)), None)

    def f_ffn_bwd(ids, ug, uv, d_act, wg, wv, bg, bv):
        sg, sv = _shifts(ug), _shifts(uv)
        gate, val = _conv(ug, wg, bg, sg), _conv(uv, wv, bv, sv)
        s = _sigmoid(gate)
        d_gate = d_act * val * s * (1.0 + gate * (1.0 - s))
        d_val = d_act * gate * s
        wg0, wg1, wg2 = _conv_bwd_w(d_gate, ug, sg)
        wv0, wv1, wv2 = _conv_bwd_w(d_val, uv, sv)
        d_up = [_conv_bwd_x(d_gate, wg), _conv_bwd_x(d_val, wv)]
        return d_up, [_colsum(d_gate), _colsum(d_val), wg0, wg1, wg2, wv0, wv1, wv2]

    d_up3, ffn_stats = _ew(
        f_ffn_bwd, (n_fb,),
        [(up, colb(0)), (up, colb(n_fb)), (da, colb(0)), (W["ffn_conv_w"], ffwb(0)), (W["ffn_conv_w"], ffwb(n_fb)),
         (W["ffn_conv_b"], ffwb(0, 1)), (W["ffn_conv_b"], ffwb(n_fb, 1))],
        [((2, T, D_FF), BF16, pl.BlockSpec((2, T, tcb), lambda j: (0, 0, j)), None),
         ((n_fb, 8, 1, tcb), F32, pl.BlockSpec((None, 8, 1, tcb), lambda j: (j, 0, 0, 0)), None)], "ffn_act_bwd")
    stat = lambda s: ffn_stats[:, s, 0, :].reshape(1, D_FF)
    d_ffn_conv_b = jnp.concatenate([stat(0), stat(1)], axis=1)
    d_ffn_conv_w = jnp.concatenate([jnp.concatenate([stat(2), stat(3), stat(4)], axis=0),
                                    jnp.concatenate([stat(5), stat(6), stat(7)], axis=0)], axis=1)

    tk_t = T
    d_w_up = _mm(h2, d_up3, "tn", D_MODEL, 2 * D_FF, T, tm=D_MODEL, tn=1408, tk=tk_t, name="w_up_dw", out_dtype=BF16,
                 b_spec=pl.BlockSpec((None, tk_t, 1408), lambda i, j, k: (j // 2, k, j % 2)),
                 o_spec=pl.BlockSpec((None, D_MODEL, 1408), lambda i, j, k: (j, i, 0)), out_shape=(4, D_MODEL, 1408))
    dh2 = _mm(d_up3, W["w_up"], "nt", T, D_MODEL, 2 * D_FF, tm=tm_lat, tn=D_MODEL, tk=1408, name="w_up_dx",
              a_spec=pl.BlockSpec((None, tm_lat, 1408), lambda i, j, k: (k // 2, i, k % 2)),
              b_spec=pl.BlockSpec((None, D_MODEL, 1408), lambda i, j, k: (k, j, 0)))

    def f_norm2_bwd(ids, dx2_, dh, x1_, m, g, sc, gate):
        r = _rms(x1_)
        xn = x1_ * r
        dx1 = dx2_ + _rms_bwd(dh * g * (1.0 + sc), xn, r)
        return dx1, dx1 * gate, _colsum(dh), _colsum(dh * xn * g), _colsum(dh * xn * (1.0 + sc)), _colsum(dx1 * m)

    dx1, dmo, dsh2, dsc2, dg_n2, dg1 = _ew(
        f_norm2_bwd, (n_lat,), [(dx2, _rows(D_MODEL)), (dh2, _rows(D_MODEL)), (x1, _rows(D_MODEL)), (mo, _rows(D_MODEL)),
                                (W["norm2_g"], vec(D_MODEL)), (sc2, vec(D_MODEL)), (g1, vec(D_MODEL))],
        [row_out(D_MODEL, F32), row_out(D_MODEL, BF16)] + [acc_out(D_MODEL)] * 4, "norm2_bwd")
    d_w_o = _mm(mrg, dmo, "tn", D_MODEL, D_MODEL, T, tm=D_MODEL, tn=D_MODEL, tk=tk_t, name="w_o_dw",
                out_dtype=BF16).reshape(4, D_MODEL // 4, D_MODEL)
    dmrg = _mm(dmo, W["w_o"], "nt", T, D_MODEL, D_MODEL, tm=tm_lat, tn=D_MODEL, tk=D_MODEL, name="w_o_dx",
               out_dtype=BF16)
    dmrg = early_grads("late", {"w_o": d_w_o, "w_up": d_w_up, "w_down": d_w_down}, dmrg, split=True)

    def f_merge_bwd(ids, dm, ga, gc, a, c):
        dm, a, c = dm.astype(F32), a.astype(F32), c.astype(F32)
        sa, sc_ = _sigmoid(ga), _sigmoid(gc)
        return dm * sa, dm * sc_, dm * a * sa * (1.0 - sa), dm * c * sc_ * (1.0 - sc_)

    dya, dyc, dp_ga, dp_gc = _ew(
        f_merge_bwd, (n_lat,), [(dmrg, _rows(D_MODEL)), (pp, _rows(D_MODEL, 0)), (pp, _rows(D_MODEL, 1)),
                                (ya, _rows(D_MODEL)), (yc, _rows(D_MODEL))], [row_out(D_MODEL, BF16)] * 4, "merge_bwd")
    dya = early_continue("late", dya)

    d_w_ao_p = _mm(o_pad, dya, "tn", 1024, D_MODEL, T, tm=1024, tn=D_MODEL, tk=tk_t, name="w_attn_out_dw", out_dtype=BF16)
    do_pad = _mm(dya, W["w_attn_out"], "nt", T, 1024, D_MODEL, tm=tm_lat, tn=1024, tk=D_MODEL, name="w_attn_out_dx")
    d_w_co = _mm(e, dyc, "tn", CONV_DIM, D_MODEL, T, tm=CONV_DIM, tn=256, tk=tk_t, name="w_conv_out_dw", out_dtype=BF16,
                 o_spec=pl.BlockSpec((None, CONV_DIM, 256), lambda i, j, k: (j, i, 0)), out_shape=(4, CONV_DIM, 256))
    de = _mm(dyc, W["w_conv_out"], "nt", T, CONV_DIM, D_MODEL, tm=tm_lat, tn=CONV_DIM, tk=256, name="w_conv_out_dx",
             b_spec=pl.BlockSpec((None, CONV_DIM, 256), lambda i, j, k: (k, j, 0)))

    def f_conv_bwd(ids, xin, cb, cc, d_e, w, b):
        z = cc * xin
        sz = _shifts(z)
        cz = _conv(z, w, b, sz)
        dcz = d_e * cb
        w0, w1, w2 = _conv_bwd_w(dcz, z, sz)
        dz = _conv_bwd_x(dcz, w)
        return dz * cc, d_e * cz, dz * xin, _colsum(dcz), w0, w1, w2

    cvec_c = ((1, CONV_DIM), F32, pl.BlockSpec((1, tc), lambda j: (0, j)), None)
    conv_b = _ew(f_conv_bwd, (CONV_DIM // tc,),
                 [(pp, colT(CX0 // tc)), (pp, colT(CB0 // tc)), (pp, colT(CC0 // tc)), (de, colT(0)),
                  (W["conv_w"], pl.BlockSpec((3, tc), lambda j: (0, j))), (W["conv_b"], pl.BlockSpec((1, tc), lambda j: (0, j)))],
                 [((T, CONV_DIM), BF16, colT(0), None)] * 3 + [cvec_c] * 4, "conv_bwd")
    dp_cx, dp_cb, dp_cc, d_conv_b = conv_b[:4]
    d_conv_w = jnp.concatenate(conv_b[4:7], axis=0)

    dq_raw, dkv, dp_kr = _attn_bwd(q_raw, kv, pp_a, o_pad, do_pad, tab, T, TT)

    tk_a = TT
    d_w_uq_t = _mm(nq, dq_raw, "tn", Q_RANK, 1024, T, tm=Q_RANK, tn=1024, tk=T, name="w_uq_dw", transpose_out=True)
    dnq = _mm(dq_raw, W["w_uq_t"], "nn", T, Q_RANK, 1024, tm=tm_lat, tn=Q_RANK, tk=1024, name="w_uq_dx")
    d_w_ukv = _mm(nkv, dkv, "tn", KV_RANK, 1024, TT, tm=KV_RANK, tn=256, tk=tk_a, name="w_ukv_dw", out_dtype=BF16,
                  o_spec=pl.BlockSpec((None, KV_RANK, 256), lambda i, j, k: (j, i, 0)), out_shape=(4, KV_RANK, 256))
    dnkv = _mm(dkv, W["w_ukv"], "nt", TT, KV_RANK, 1024, tm=tm_all, tn=KV_RANK, tk=256, name="w_ukv_dx",
               b_spec=pl.BlockSpec((None, KV_RANK, 256), lambda i, j, k: (k, j, 0)))
    dnkv = early_grads("mid", {
        "w_attn_out": jnp.transpose(d_w_ao_p.reshape(N_HEADS, HEAD_PAD, 4, 256)[:, 64:], (2, 0, 1, 3)).reshape(
            4, N_HEADS * 64, 256),
        "w_conv_out": d_w_co,
        "w_uq": d_w_uq_t.reshape(4, 2, HEAD_PAD, Q_RANK)[:, :, :QK_DIM].reshape(4, 2 * QK_DIM, Q_RANK).astype(BF16),
        "w_ukv": d_w_ukv}, dnkv)

    def f_lowrank_bwd(ids, ckv, cq, dkv_, dq_, gkv, gq, ga, gc, cx, cb, cc, kr):
        rk, rq = _rms(ckv), _rms(cq)
        nk, nq_ = ckv * rk, cq * rq
        lat = ids[0] < n_lat
        dq_ = jnp.where(lat, dq_, 0.0)
        pieces = [jnp.where(lat, a, jnp.zeros_like(a)) for a in (ga, gc, cx, cb, cc)]
        pieces += [_rms_bwd(dkv_ * gkv, nk, rk).astype(BF16), _rms_bwd(dq_ * gq, nq_, rq).astype(BF16), kr.astype(BF16)]
        return jnp.concatenate(pieces, axis=1), _colsum(dkv_ * nk), _colsum(dq_ * nq_)

    lat_rows = lambda n: pl.BlockSpec((ROW_TILE, n), lambda i: (jnp.minimum(i, n_lat - 1), 0))
    dpp, dg_kv, dg_q = _ew(
        f_lowrank_bwd, (n_all,), [(pp_a, _rows(KV_RANK, PA_KV0 // KV_RANK)), (pp_a, _rows(Q_RANK, PA_Q0 // Q_RANK)),
                                  (dnkv, _rows(KV_RANK)), (dnq, lat_rows(Q_RANK)), (W["kv_norm_g"], vec(KV_RANK)),
                                  (W["q_norm_g"], vec(Q_RANK)), (dp_ga, lat_rows(D_MODEL)), (dp_gc, lat_rows(D_MODEL)),
                                  (dp_cx, lat_rows(CONV_DIM)), (dp_cb, lat_rows(CONV_DIM)), (dp_cc, lat_rows(CONV_DIM)),
                                  (dp_kr, _rows(HEAD_PAD))],
        [row_out(P_COLS, BF16, TT), acc_out(KV_RANK), acc_out(Q_RANK)], "lowrank_norm_bwd")
    d_w_in_t = _mm(hh, dpp, "tn", D_MODEL, P_COLS, TT, tm=512, tn=2176, tk=TT, name="w_in_dw", out_dtype=BF16,
                   transpose_out=True)
    dhh = _mm(dpp, W["w_in_t"], "nn", TT, D_MODEL, P_COLS, tm=tm_all, tn=512, tk=2176, name="w_in_dx")

    def f_norm1_bwd(ids, x, dh, dres, g, sc):
        r = _rms(x)
        xn = x * r
        return (dres + _rms_bwd(dh * g * (1.0 + sc), xn, r), _colsum(dh), _colsum(dh * xn * g),
                _colsum(dh * xn * (1.0 + sc)))

    grad_x, dsh1, dsc1, dg_n1 = _ew(
        f_norm1_bwd, (n_lat,), [(xx, _rows(D_MODEL)), (dhh, _rows(D_MODEL)), (dx1, _rows(D_MODEL)),
                                (W["norm1_g"], vec(D_MODEL)), (sc1, vec(D_MODEL))],
        [row_out(D_MODEL, F32)] + [acc_out(D_MODEL)] * 3, "norm1_bwd")

    def f_norm1_ctx_bwd(ids, x, dh, g, sc):
        xn = x * _rms(x)
        return _colsum(dh), _colsum(dh * xn * g), _colsum(dh * xn * (1.0 + sc))

    n_ctx = n_all - n_lat
    dcsh1, dcsc1, dg_n1c = _ew(
        f_norm1_ctx_bwd, (n_ctx,), [(xx, _rows(D_MODEL, 0, n_lat)), (dhh, _rows(D_MODEL, 0, n_lat)),
                                    (W["norm1_g"], vec(D_MODEL)), (csc1, vec(D_MODEL))], [acc_out(D_MODEL)] * 3,
        "norm1_ctx_bwd")

    big = {"w_in": _w_in_t_shards_from_p(d_w_in_t).astype(BF16)}
    zero = jnp.zeros((1, 4 * D_MODEL), F32)
    small = {
        "dmod_lat": jnp.concatenate([dsh1, dsc1, dg1, dsh2, dsc2, dg2], axis=1),
        "dmod_ctx": jnp.concatenate([dcsh1, dcsc1, zero], axis=1),
        "norm1_g": dg_n1 + dg_n1c, "norm2_g": dg_n2, "final_g": dg_f, "q_norm_g": dg_q, "kv_norm_g": dg_kv,
        "conv_b": d_conv_b, "conv_w": d_conv_w.reshape(1, -1), "ffn_conv_b": d_ffn_conv_b,
        "ffn_conv_w": d_ffn_conv_w.reshape(1, -1),
    }
    return grad_x, loss, big, small


SMALL = (("dmod_lat", 6144), ("dmod_ctx", 6144), ("norm1_g", 1024), ("norm2_g", 1024), ("final_g", 1024),
         ("q_norm_g", 384), ("kv_norm_g", 256), ("conv_b", 512), ("conv_w", 1536), ("ffn_conv_b", 5632),
         ("ffn_conv_w", 16896), ("loss", 128))
SMALL_ROWS = 320


def _adam_update(w, g, m, v):
    c1, c2 = 1.0 - ADAM_B1 ** ADAM_STEP, 1.0 - ADAM_B2 ** ADAM_STEP
    m2 = ADAM_B1 * m + (1.0 - ADAM_B1) * g
    v2 = ADAM_B2 * v + (1.0 - ADAM_B2) * (g * g)
    return [-ADAM_LR * ((m2 / c1) / (jnp.sqrt(v2 / c2) + ADAM_EPS) + ADAM_WD * w), m2, v2]


def _adamw(w, g, m, v, name):
    R, C = w.shape
    tr = 8 if R % 8 == 0 else R
    for t in range(8, R + 1, 8):
        if R % t == 0 and t * C * 4 <= (1 << 20):
            tr = t
    spec = pl.BlockSpec((tr, C), lambda i: (i, 0))
    return _ew(lambda ids, *vals: _adam_update(*vals), (R // tr,), [(w, spec), (g, spec), (m, spec), (v, spec)],
               [((R, C), F32, spec, None)] * 3, name)


def kernel(x, c, ctx, c_ctx, w_ada, b_ada, norm1_g, w_in, q_norm_g, kv_norm_g, w_uq, w_ukv, conv_w, conv_b, w_attn_out, w_conv_out, w_o, norm2_g, w_up, ffn_conv_w, ffn_conv_b, w_down, final_g, loss_target, m_c_ctx, m_w_ada, m_b_ada, m_norm1_g, m_w_in, m_q_norm_g, m_kv_norm_g, m_w_uq, m_w_ukv, m_conv_w, m_conv_b, m_w_attn_out, m_w_conv_out, m_w_o, m_norm2_g, m_w_up, m_ffn_conv_w, m_ffn_conv_b, m_w_down, m_final_g, v_c_ctx, v_w_ada, v_b_ada, v_norm1_g, v_w_in, v_q_norm_g, v_kv_norm_g, v_w_uq, v_w_ukv, v_conv_w, v_conv_b, v_w_attn_out, v_w_conv_out, v_w_o, v_norm2_g, v_w_up, v_ffn_conv_w, v_ffn_conv_b, v_w_down, v_final_g):
    mx, my, mc = lax.axis_index("x"), lax.axis_index("y"), lax.axis_index("c")
    chip = 2 * mx + my
    dev = 4 * mx + 2 * my + mc
    T, Tc = x.shape[1], ctx.shape[1]
    TT = T + Tc
    w_in_t, m_w_in_t, v_w_in_t = (jnp.transpose(a[0]) for a in (w_in, m_w_in, v_w_in))
    w_uq_t, m_w_uq_t, v_w_uq_t = (jnp.transpose(a[0]) for a in (w_uq, m_w_uq, v_w_uq))
    conv_sh = jnp.concatenate([conv_w[0], ffn_conv_w[0]], axis=1)
    pay1 = jnp.concatenate([jnp.pad(c, ((0, 7), (0, 0))), jnp.pad(conv_sh, ((0, 5), (0, 0)))], axis=1)
    c_send, c_recv, c_src, c_land, zero0 = _ici_start("all", [pay1], [(8, 8, 2560)], jnp.zeros((8, 128), F32),
                                                      "cond_start")
    w_in_bf = (jnp.pad(w_in_t, ((0, W_IN_SHARD_PAD - W_IN_SHARD), (0, 0))) + zero0[0, 0]).astype(BF16)
    shards = {"w_in_a": w_in_bf[:W_IN_EARLY], "w_in_b": w_in_bf[W_IN_EARLY:], "w_uq": w_uq_t, "w_ukv": w_ukv[0],
              "w_attn_out": w_attn_out[0], "w_conv_out": w_conv_out[0], "w_o": w_o[0], "w_up": w_up[0],
              "w_down": w_down[0]}
    (pay1,), (c_land,) = _ici_wait("all", c_send, c_recv, c_src, c_land, w_in_bf, "cond_wait")
    got1 = lax.dynamic_update_slice(c_land, pay1[None], (dev, 0, 0))
    c_all = got1[:, 0, :D_MODEL]
    conv_all = got1[0::2, :3, D_MODEL:]
    conv_w_full = _cols_from_shards(conv_all[:, :, :128])
    ffn_conv_w_full = _cols_from_shards(conv_all[:, :, 128:])

    cond = jnp.concatenate([c_all, c_ctx.reshape(1, D_MODEL), jnp.zeros((7, D_MODEL), F32)], axis=0)

    def f_silu(ids, v):
        return (v * _sigmoid(v),)

    (s16,) = _ew(f_silu, (1,), [(cond, _full((16, D_MODEL)))], [((16, D_MODEL), F32, _full((16, D_MODEL)), None)], "silu_cond")
    mod_sh = _mm(s16, w_ada[0], "nn", 16, 1536, D_MODEL, tm=16, tn=768, tk=D_MODEL, name="w_ada_fwd")
    m_send, m_recv, m_src, m_land, zero1 = _ici_start("all", [mod_sh], [(8, 16, 1536)], jnp.zeros((8, 128), F32),
                                                      "mod_start")
    shards["w_ukv"] = w_ukv[0] + zero1[0, 0]

    first = ["w_in_a", "w_uq", "w_ukv"]
    gathered, zero = _gather_weights([shards[n].astype(BF16) for n in first])
    full = dict(zip(first, gathered))
    (mod_mine,), (m_land,) = _ici_wait("all", m_send, m_recv, m_src, m_land, gathered[0], "mod_wait")
    got2 = lax.dynamic_update_slice(m_land, mod_mine[None], (dev, 0, 0))
    mod_all = _cols_from_shards(got2[0::2]) + b_ada
    mod_lat = lax.dynamic_slice_in_dim(mod_all, dev, 1, axis=0)
    mod_ctx = mod_all[8:9]
    xx = jnp.concatenate([x[0], ctx[0]], axis=0)
    late_groups = {"g1": ("w_in_b", "w_attn_out", "w_conv_out", "w_o"), "g2": ("w_up", "w_down")}
    flight = {}
    for tag, group in late_groups.items():
        bf = [(shards[n] + zero[0, 0]).astype(BF16) for n in group]
        flight[tag] = _ici_start("gather", bf, [(4,) + s.shape for s in bf], xx, "gather_" + tag + "_start")
        xx = flight[tag][4]

    def chip_stage_done(tag, x):
        send, recv, src, land, _ = flight[tag]
        src, land = _ici_wait("gather", send, recv, src, land, x, "gather_" + tag + "_wait")
        flight[tag] = _ici_start("finish", src, None, x, "finish_" + tag + "_start", lands=land)
        return flight[tag][4]

    def arrived(tag, x):
        send, recv, src, land, _ = flight[tag]
        return dict(zip(late_groups[tag], _ici_wait("finish", send, recv, src, land, x, "finish_" + tag + "_wait")[1]))

    def late_weights(point, x):
        if point == "before_attn":
            return {}, chip_stage_done("g1", x)
        if point == "after_attn":
            got = arrived("g1", x)
            wao = _cols_from_shards(got["w_attn_out"]).reshape(N_HEADS, 64, D_MODEL)
            w_in_all = jnp.concatenate([full["w_in_a"], got["w_in_b"]], axis=1)
            ready = {"w_in_t": _w_in_t_p_from_shards(w_in_all),
                     "w_attn_out": jnp.pad(wao, ((0, 0), (64, 0), (0, 0))).reshape(N_HEADS * HEAD_PAD, D_MODEL),
                     "w_conv_out": got["w_conv_out"], "w_o": got["w_o"].reshape(D_MODEL, D_MODEL)}
            return ready, chip_stage_done("g2", x)
        got = arrived("g2", x)
        return {"w_up": got["w_up"], "w_down": got["w_down"].reshape(D_FF, D_MODEL)}, x

    wuq_t = full["w_uq"].reshape(N_HEADS, QK_DIM, Q_RANK)
    early_rows = full["w_in_a"][0]
    zrows = lambda n: jnp.zeros((n, D_MODEL), BF16)
    W = {
        "w_in_a_t": jnp.concatenate([early_rows[0:256], zrows(PA_Q0 - 256), early_rows[288:672], zrows(64),
                                     early_rows[256:288], zrows(32)], axis=0),
        "w_uq_t": jnp.pad(wuq_t, ((0, 0), (0, HEAD_PAD - QK_DIM), (0, 0))).reshape(N_HEADS * HEAD_PAD, Q_RANK),
        "w_ukv": full["w_ukv"],
        "norm1_g": norm1_g, "norm2_g": norm2_g, "final_g": final_g.reshape(1, D_MODEL), "q_norm_g": q_norm_g,
        "kv_norm_g": kv_norm_g, "conv_w": conv_w_full, "conv_b": conv_b, "ffn_conv_w": ffn_conv_w_full,
        "ffn_conv_b": ffn_conv_b,
    }

    place = jnp.stack([chip, mc]).astype(jnp.int32)
    early = {}

    pending = {}

    def scatter(tag, group, gs, from_sib, carry):
        if tag == "mid":
            sums = _add_pair_many(gs, from_sib, place, "rs_pair_add_mid")
        else:
            sums = [_add_pair(gs[w], from_sib[w], place, "rs_pair_add_" + n) for w, n in enumerate(group)]
        send, recv, sums, land, carry = _ici_start(
            "scatter", sums, [(3,) + s.shape[1:] for s in sums], carry, "rs_chips_" + tag + "_start")
        early[tag] = (group, send, recv, sums, land)
        return carry

    def early_grads(tag, g, carry, split=False):
        gs = list(g.values())
        if not split:
            return scatter(tag, list(g), gs, _rs_pair(gs, "rs_pair_" + tag), carry)
        send, recv, gs, land, carry = _ici_start(
            "pair", gs, [(4, s.shape[1] // 2, s.shape[2]) for s in gs], carry, "rs_pair_" + tag + "_start")
        pending[tag] = (list(g), send, recv, gs, land)
        return carry

    def early_continue(tag, carry):
        group, send, recv, gs, land = pending[tag]
        gs, from_sib = _ici_wait("pair", send, recv, gs, land, carry, "rs_pair_" + tag + "_wait")
        return scatter(tag, group, gs, from_sib, carry)

    grad_x, loss_part, gbig, gsmall = _local_step(xx, loss_target[0], mod_lat, mod_ctx, W, late_weights, early_grads,
                                                  early_continue)

    gsmall["loss"] = loss_part
    pay3 = jnp.concatenate([gsmall[n].reshape(-1) for n, _ in SMALL])
    pay3 = jnp.pad(pay3, (0, SMALL_ROWS * 128 - pay3.shape[0])).reshape(SMALL_ROWS, 128)
    s_send, s_recv, s_src, s_land, w_in_thru = _ici_start("all", [pay3], [(8, SMALL_ROWS, 128)], gbig["w_in"],
                                                         "small_start")
    gbig = {"w_in": w_in_thru}

    after_small = early_grads("last", gbig, s_src[0])

    (pay3,), (s_land,) = _ici_wait("all", s_send, s_recv, [after_small], s_land, early["last"][3][0], "small_wait")
    got3 = lax.dynamic_update_slice(s_land, pay3[None], (dev, 0, 0)).reshape(8 * SMALL_ROWS, 128)

    def f_sum8(ids, a):
        s = a[0:SMALL_ROWS]
        for d in range(1, 8):
            s = s + a[d * SMALL_ROWS:(d + 1) * SMALL_ROWS]
        return (s,)

    (vsum,) = _ew(f_sum8, (1,), [(got3, _full((8 * SMALL_ROWS, 128)))],
                  [((SMALL_ROWS, 128), F32, _full((SMALL_ROWS, 128)), None)], "sum_small")
    vflat = vsum.reshape(-1)
    gvec, off = {}, 0
    for n, size in SMALL:
        gvec[n] = vflat[off:off + size]
        off += size
    loss = gvec["loss"][0]
    dmod_rows = got3.reshape(8, SMALL_ROWS * 128)[:, :6 * D_MODEL]
    dm16 = jnp.concatenate([dmod_rows, gvec["dmod_ctx"].reshape(1, -1), jnp.zeros((7, 6 * D_MODEL), F32)], axis=0)

    def f_colsum(ids, a):
        return (_colsum(a),)

    (g_b_ada,) = _ew(f_colsum, (1,), [(dm16, _full((16, 6 * D_MODEL)))],
                     [((1, 6 * D_MODEL), F32, _full((1, 6 * D_MODEL)), None)], "b_ada_grad")
    dm_sh = lax.dynamic_slice_in_dim(dm16, chip * 1536, 1536, axis=1)
    g_w_ada = _mm(s16, dm_sh, "tn", D_MODEL, 1536, 16, tm=512, tn=768, tk=16, name="w_ada_dw")
    dcond_part = _mm(dm_sh, w_ada[0], "nt", 16, D_MODEL, 1536, tm=16, tn=512, tk=1536, name="w_ada_dx")
    d_send, d_recv, d_src, d_land, vsum = _ici_start("all", [dcond_part[8:16]], [(8, 8, D_MODEL)], vsum, "dcond_start")

    def finish_start(tags, after):
        done, halves = [], []
        for tag in tags:
            tag_names, send, recv, sums, land = early[tag]
            sums, land = _ici_wait("scatter", send, recv, sums, land, after, "rs_chips_" + tag + "_wait")
            done += tag_names
            if tag == "mid":
                halves += _add_chips_many(sums, land, place, "rs_chip_add_mid")
            else:
                halves += [_add_chips(a, b, place, "rs_chip_add_" + n) for a, b, n in zip(sums, land, tag_names)]
        send, recv, _, halves, _ = _ici_start("back", [], None, jnp.zeros((8, 128), F32), "rs_back_" + tags[0] + "_start",
                                              lands=halves)
        return done, send, recv, halves

    def finish_wait(state, after):
        done, send, recv, halves = state
        return dict(zip(done, _ici_wait("back", send, recv, [], halves, after, "rs_back_" + done[0] + "_wait")[1]))

    grads, deltas, new_m, new_v = {}, {}, {}, {}

    raw = {}

    def adam(n, w_, m_, v_, g, transposed):
        d_, m2, v2 = _adamw(w_, g, m_, v_, "adamw_" + n)
        raw[n] = d_
        back = (lambda a: jnp.transpose(a)[None]) if transposed else (lambda a: a[None])
        grads[n], deltas[n], new_m[n], new_v[n] = back(g[:w_.shape[0]]), back(d_), back(m2), back(v2)

    pending_back = finish_start(["late", "mid"], grad_x)
    adam("w_ada", w_ada[0], m_w_ada[0], v_w_ada[0], g_w_ada, False)
    gw = finish_wait(pending_back, raw["w_ada"])
    for n, (w_, m_, v_) in {"w_o": (w_o, m_w_o, v_w_o), "w_up": (w_up, m_w_up, v_w_up),
                            "w_down": (w_down, m_w_down, v_w_down)}.items():
        adam(n, w_[0], m_[0], v_[0], gw[n], False)
    pending_back = finish_start(["last"], raw["w_up"])

    (dcond_mine,), (d_land,) = _ici_wait("all", d_send, d_recv, d_src, d_land, raw["w_down"], "dcond_wait")
    got4 = lax.dynamic_update_slice(d_land, dcond_mine[None], (dev, 0, 0))[0::2, 0]

    def f_c_ctx(ids, parts, cc):
        s = _sigmoid(cc)
        d = parts[0:1] + parts[1:2] + parts[2:3] + parts[3:4]
        return (d * s * (1.0 + cc * (1.0 - s)),)

    (g_c_ctx,) = _ew(f_c_ctx, (1,), [(got4, _full((4, D_MODEL))), (c_ctx.reshape(1, D_MODEL), _full((1, D_MODEL)))],
                     [((1, D_MODEL), F32, _full((1, D_MODEL)), None)], "c_ctx_grad")

    conv_w_g = lax.dynamic_slice_in_dim(gvec["conv_w"].reshape(3, CONV_DIM), chip * 128, 128, axis=1)
    ffn_conv_w_g = lax.dynamic_slice_in_dim(gvec["ffn_conv_w"].reshape(3, 2 * D_FF), chip * 1408, 1408, axis=1)
    vec_params = (("c_ctx", c_ctx, m_c_ctx, v_c_ctx, g_c_ctx), ("b_ada", b_ada, m_b_ada, v_b_ada, g_b_ada),
                  ("norm1_g", norm1_g, m_norm1_g, v_norm1_g, gvec["norm1_g"]),
                  ("q_norm_g", q_norm_g, m_q_norm_g, v_q_norm_g, gvec["q_norm_g"]),
                  ("kv_norm_g", kv_norm_g, m_kv_norm_g, v_kv_norm_g, gvec["kv_norm_g"]),
                  ("conv_w", conv_w, m_conv_w, v_conv_w, conv_w_g), ("conv_b", conv_b, m_conv_b, v_conv_b, gvec["conv_b"]),
                  ("norm2_g", norm2_g, m_norm2_g, v_norm2_g, gvec["norm2_g"]),
                  ("ffn_conv_w", ffn_conv_w, m_ffn_conv_w, v_ffn_conv_w, ffn_conv_w_g),
                  ("ffn_conv_b", ffn_conv_b, m_ffn_conv_b, v_ffn_conv_b, gvec["ffn_conv_b"]),
                  ("final_g", final_g, m_final_g, v_final_g, gvec["final_g"]))
    two_d = lambda a: a.reshape((-1, a.shape[-1]))
    many = [p + ((lambda r, s=p[1].shape: r.reshape(s)),) for p in vec_params]
    for n, w_, m_, v_ in (("w_ukv", w_ukv, m_w_ukv, v_w_ukv), ("w_attn_out", w_attn_out, m_w_attn_out, v_w_attn_out),
                          ("w_conv_out", w_conv_out, m_w_conv_out, v_w_conv_out)):
        many.append((n, w_, m_, v_, gw[n], (lambda r, s=w_.shape: r.reshape(s))))
    many.append(("w_uq", w_uq_t, m_w_uq_t, v_w_uq_t, gw["w_uq"], lambda r: jnp.transpose(r)[None]))

    def f_adam_many(ids, *vals):
        out = []
        for k in range(len(many)):
            out += _adam_update(*vals[4 * k:4 * k + 4])
        return out

    ins_v, outs_v = [], []
    for p in many:
        shp = two_d(p[1]).shape
        ins_v += [(two_d(a), _full(shp)) for a in (p[1], p[4], p[2], p[3])]
        outs_v += [(shp, F32, _full(shp), None)] * 3
    res_v = _ew(f_adam_many, (1,), ins_v, outs_v, "adamw_small")
    for k, p in enumerate(many):
        n, post = p[0], p[5]
        grads[n] = post(two_d(p[4]))
        deltas[n], new_m[n], new_v[n] = (post(r) for r in res_v[3 * k:3 * k + 3])

    gw_in = finish_wait(pending_back, res_v[0])
    adam("w_in", w_in_t, m_w_in_t, v_w_in_t, gw_in["w_in"], True)

    order = ("c_ctx", "w_ada", "b_ada", "norm1_g", "w_in", "q_norm_g", "kv_norm_g", "w_uq", "w_ukv", "conv_w", "conv_b",
             "w_attn_out", "w_conv_out", "w_o", "norm2_g", "w_up", "ffn_conv_w", "ffn_conv_b", "w_down", "final_g")
    return (loss, grad_x[None], *[grads[n] for n in order], *[deltas[n] for n in order],
            *[new_m[n] for n in order], *[new_v[n] for n in order])
```

```python
import functools

import jax
import jax.numpy as jnp
import numpy as np
from jax import lax
from jax.experimental import pallas as pl
from jax.experimental.pallas import tpu as pltpu

F32, BF16 = jnp.float32, jnp.bfloat16
MESH = pl.DeviceIdType.MESH

D_MODEL = 1024
N_HEADS = 8
HEAD_PAD = 128
QK_DIM = 96
Q_RANK, KV_RANK = 384, 256
CONV_DIM = 512
D_FF = 2816
GRID_W = 64
ROPE_THETA = 10000.0
EPS = 1e-6
GA0, GC0, CX0, CB0, CC0, KV0, Q0, KR0, P_COLS = 0, 1024, 2048, 2560, 3072, 3584, 3840, 4224, 4352
PA_KV0, PA_Q0, PA_KR0, PA_COLS = 0, 384, 768, 896
ROW_TILE = 256
VMEM_LIMIT_BYTES = 48 * 1024 * 1024

ADAM_LR, ADAM_B1, ADAM_B2, ADAM_EPS, ADAM_WD, ADAM_STEP = 0.001, 0.9, 0.999, 1e-08, 0.01, 10

NN = (((1,), (0,)), ((), ()))
NT = (((1,), (1,)), ((), ()))
TN = (((0,), (0,)), ((), ()))


def _cp(sem):
    return pltpu.CompilerParams(dimension_semantics=sem, vmem_limit_bytes=VMEM_LIMIT_BYTES)


PIN_BYTES = 1 << 19


def _in_hbm(arrays):
    return [pltpu.with_memory_space_constraint(a, pltpu.HBM) if a.size * a.dtype.itemsize >= PIN_BYTES else a
            for a in arrays]


def _out(shape, dtype):
    n = 1
    for d in shape:
        n *= d
    big = n * jnp.dtype(dtype).itemsize >= PIN_BYTES
    return pltpu.HBM(shape, dtype) if big else jax.ShapeDtypeStruct(shape, dtype)


def _pick(n, prefs):
    for p in prefs:
        if n % p == 0:
            return p
    return n


def _mm(a, b, mode, M, N, K, *, tm, tn, tk, name, out_dtype=F32, a_spec=None, b_spec=None, o_spec=None,
        out_shape=None, transpose_out=False):
    assert M % tm == 0 and N % tn == 0 and K % tk == 0, (name, M, N, K, tm, tn, tk)
    nk = K // tk
    dims = {"nn": NN, "nt": NT, "tn": TN}[mode]
    if a_spec is None:
        a_spec = (pl.BlockSpec((tk, tm), lambda i, j, k: (k, i)) if mode == "tn"
                  else pl.BlockSpec((tm, tk), lambda i, j, k: (i, k)))
    if b_spec is None:
        b_spec = (pl.BlockSpec((tn, tk), lambda i, j, k: (j, k)) if mode == "nt"
                  else pl.BlockSpec((tk, tn), lambda i, j, k: (k, j)))
    if o_spec is None:
        o_spec = (pl.BlockSpec((tn, tm), lambda i, j, k: (j, i)) if transpose_out
                  else pl.BlockSpec((tm, tn), lambda i, j, k: (i, j)))
    if out_shape is None:
        out_shape = (N, M) if transpose_out else (M, N)

    def emit(o_ref, val):
        o_ref[...] = (val.T if transpose_out else val).astype(o_ref.dtype)

    def body(a_ref, b_ref, o_ref, *scratch):
        part = lax.dot_general(a_ref[...].astype(BF16), b_ref[...].astype(BF16), dims, preferred_element_type=F32)
        if nk == 1:
            emit(o_ref, part)
            return
        acc_ref, = scratch
        k = pl.program_id(2)

        @pl.when(k == 0)
        def _():
            acc_ref[...] = part

        @pl.when((k > 0) & (k < nk - 1))
        def _():
            acc_ref[...] += part

        @pl.when(k == nk - 1)
        def _():
            emit(o_ref, acc_ref[...] + part)

    return pl.pallas_call(
        body, grid=(M // tm, N // tn, nk), in_specs=[a_spec, b_spec], out_specs=o_spec,
        out_shape=_out(out_shape, out_dtype),
        scratch_shapes=[pltpu.VMEM((tm, tn), F32)] if nk > 1 else [],
        compiler_params=_cp(("parallel", "parallel", "arbitrary")), name=name)(*_in_hbm([a, b]))


def _ew(fn, grid, ins, outs, name, scalars=None):
    n_in = len(ins)
    n_sc = 0 if scalars is None else 1

    def store(ref, val, acc, ids):
        if isinstance(val, (list, tuple)):
            for h, v in enumerate(val):
                ref[h] = v.astype(ref.dtype)
            return
        if acc is None:
            ref[...] = val.astype(ref.dtype)
            return

        @pl.when(ids[acc] == 0)
        def _():
            ref[...] = val.astype(ref.dtype)

        @pl.when(ids[acc] > 0)
        def _():
            ref[...] += val.astype(ref.dtype)

    def body(*refs):
        refs = refs[n_sc:]
        ids = tuple(pl.program_id(a) for a in range(len(grid)))
        vals = fn(ids, *[r[...] for r in refs[:n_in]])
        for ref, val, (_, _, _, acc) in zip(refs[n_in:], vals, outs):
            store(ref, val, acc, ids)

    acc_axes = {o[3] for o in outs if o[3] is not None}
    sem = tuple("arbitrary" if a in acc_axes else "parallel" for a in range(len(grid)))
    in_specs, out_specs = [s for _, s in ins], [o[2] for o in outs]
    out_shape = [_out(o[0], o[1]) for o in outs]
    args = _in_hbm([a for a, _ in ins])
    if scalars is None:
        return pl.pallas_call(body, grid=grid, in_specs=in_specs, out_specs=out_specs, out_shape=out_shape,
                              compiler_params=_cp(sem), name=name)(*args)
    spec = pltpu.PrefetchScalarGridSpec(num_scalar_prefetch=1, grid=grid, in_specs=in_specs, out_specs=out_specs)
    return pl.pallas_call(body, grid_spec=spec, out_shape=out_shape, compiler_params=_cp(sem), name=name)(scalars, *args)


def _rows(width, cblk=0, roff=0, tr=ROW_TILE):
    return pl.BlockSpec((tr, width), lambda i: (i + roff, cblk))


def _full(shape):
    nd = len(shape)
    return pl.BlockSpec(shape, lambda *_: (0,) * nd)


def _sigmoid(x):
    return 1.0 / (1.0 + jnp.exp2(x * (-1.4426950408889634)))


def _rms(x):
    return lax.rsqrt(jnp.mean(x * x, axis=-1, keepdims=True) + EPS)


def _rms_bwd(dn, xn, r):
    return r * (dn - xn * jnp.mean(dn * xn, axis=-1, keepdims=True))


def _colsum(x):
    return jnp.sum(x, axis=0, keepdims=True)


def _shifts(x):
    n = x.shape[0]
    rows = lax.broadcasted_iota(jnp.int32, x.shape, 0)
    return jnp.where(rows == 0, 0.0, pltpu.roll(x, 1, 0)), jnp.where(rows == n - 1, 0.0, pltpu.roll(x, n - 1, 0))


def _conv(x, w, b, shifted=None):
    prev, nxt = _shifts(x) if shifted is None else shifted
    return b + prev * w[0:1] + x * w[1:2] + nxt * w[2:3]


def _conv_bwd_x(dy, w):
    prev, nxt = _shifts(dy)
    return nxt * w[0:1] + dy * w[1:2] + prev * w[2:3]


def _conv_bwd_w(dy, x, shifted):
    prev, nxt = shifted
    return _colsum(dy * prev), _colsum(dy * x), _colsum(dy * nxt)


def _rope(x, cos, sin_lo, sin_hi):
    return x * cos + pltpu.roll(x, HEAD_PAD - 8, 1) * sin_lo + pltpu.roll(x, 8, 1) * sin_hi


ATTN_SCALE = QK_DIM ** -0.5
LOG2_E = 1.4426950408889634


def _rope_t(x, tab, inverse=False):
    o = 3 * HEAD_PAD if inverse else 0
    return _rope(x, tab[:, o:o + HEAD_PAD], tab[:, o + HEAD_PAD:o + 2 * HEAD_PAD], tab[:, o + 2 * HEAD_PAD:o + 3 * HEAD_PAD])


def _heads_keys(hp, kv_ref, kr_ref, tab_ref, kc_ref, vp_ref):
    kr_roped = _rope_t(kr_ref[...], tab_ref[...])
    lane = lax.broadcasted_iota(jnp.int32, kr_roped.shape, 1)
    for u in range(hp):
        kv = kv_ref[:, u * HEAD_PAD:(u + 1) * HEAD_PAD]
        kc_ref[u] = jnp.where(lane < 64, kv, kr_roped).astype(BF16)
        vp_ref[u] = jnp.where(lane >= 64, kv, 0.0).astype(BF16)


ATTN_Q_TILE = 512
ATTN_HEADS_PER_STEP = 2


def _attn_specs(tq, TT):
    q = pl.BlockSpec((tq, HEAD_PAD), lambda h, i: (i, h))
    keys = pl.BlockSpec((TT, HEAD_PAD), lambda h, i: (0, h))
    kr = pl.BlockSpec((TT, HEAD_PAD), lambda h, i: (0, PA_KR0 // HEAD_PAD))
    tab_q = pl.BlockSpec((tq, 6 * HEAD_PAD), lambda h, i: (i, 0))
    tab_k = pl.BlockSpec((TT, 6 * HEAD_PAD), lambda h, i: (0, 0))
    return q, keys, kr, tab_q, tab_k


def _attn_fwd(q_raw, kv, pp, tab, T, TT):
    tq, hp = ROW_TILE, 2 * ATTN_HEADS_PER_STEP
    w = hp * HEAD_PAD

    def body(q_ref, kv_ref, kr_ref, tq_ref, tk_ref, o_ref, kc, vp):
        @pl.when(pl.program_id(1) == 0)
        def _():
            _heads_keys(hp, kv_ref, kr_ref, tk_ref, kc, vp)

        tab = tq_ref[...]
        for u in range(hp):
            cols = slice(u * HEAD_PAD, (u + 1) * HEAD_PAD)
            q = _rope_t(q_ref[:, cols], tab).astype(BF16)
            s = lax.dot_general(q, kc[u], NT, preferred_element_type=F32)
            m = jnp.max(s, axis=-1, keepdims=True)
            p = jnp.exp2((s - m) * (ATTN_SCALE * LOG2_E))
            l = jnp.sum(p, axis=-1, keepdims=True)
            o = lax.dot_general(p.astype(BF16), vp[u], NN, preferred_element_type=F32)
            lane = lax.broadcasted_iota(jnp.int32, o.shape, 1)
            o_ref[:, cols] = jnp.where(lane < 64, m * ATTN_SCALE + jnp.log(l), o / l)

    _, _, kr, _, _ = _attn_specs(tq, TT)
    qs = pl.BlockSpec((tq, w), lambda h, i: (i, h))
    keys = pl.BlockSpec((TT, w), lambda h, i: (0, h))
    tab_q = pl.BlockSpec((tq, 3 * HEAD_PAD), lambda h, i: (i, 0))
    tab_k = pl.BlockSpec((TT, 3 * HEAD_PAD), lambda h, i: (0, 0))
    return pl.pallas_call(
        body, grid=(N_HEADS // hp, T // tq), in_specs=[qs, keys, kr, tab_q, tab_k], out_specs=qs,
        out_shape=jax.ShapeDtypeStruct((T, N_HEADS * HEAD_PAD), F32),
        scratch_shapes=[pltpu.VMEM((hp, TT, HEAD_PAD), BF16), pltpu.VMEM((hp, TT, HEAD_PAD), BF16)],
        compiler_params=_cp(("parallel", "arbitrary")), name="attn_fwd",
    )(*_in_hbm([q_raw, kv, pp, tab, tab]))


def _attn_bwd(q_raw, kv, pp, o, do, tab, T, TT):
    tq = _pick(T, (ATTN_Q_TILE, ROW_TILE))
    nq = T // tq
    hp = ATTN_HEADS_PER_STEP
    w = hp * HEAD_PAD

    def body(q_ref, kv_ref, kr_ref, tq_ref, tk_ref, o_ref, do_ref, dq_ref, dkv_ref, dkr_ref, kc, vp, dk, dv):
        g, i = pl.program_id(0), pl.program_id(1)

        @pl.when(i == 0)
        def _():
            _heads_keys(hp, kv_ref, kr_ref, tk_ref, kc, vp)
            dk[...] = jnp.zeros_like(dk)
            dv[...] = jnp.zeros_like(dv)

        tab = tq_ref[...]
        for u in range(hp):
            cols = slice(u * HEAD_PAD, (u + 1) * HEAD_PAD)
            q = _rope_t(q_ref[:, cols], tab).astype(BF16)
            k, v, d_o = kc[u], vp[u], do_ref[:, cols]
            s = lax.dot_general(q, k, NT, preferred_element_type=F32)
            o = o_ref[:, cols]
            p = jnp.exp2(s * (ATTN_SCALE * LOG2_E) - o[:, 0:1] * LOG2_E)
            dob = d_o.astype(BF16)
            dp = lax.dot_general(dob, v, NT, preferred_element_type=F32)
            dd = jnp.sum(d_o * o, axis=-1, keepdims=True)
            ds = (p * (dp - dd) * ATTN_SCALE).astype(BF16)
            dq = lax.dot_general(ds, k, NN, preferred_element_type=F32)
            dq_ref[:, cols] = _rope_t(dq, tab, inverse=True).astype(dq_ref.dtype)
            dk[u] += lax.dot_general(q, ds, TN, preferred_element_type=F32)
            dv[u] += lax.dot_general(dob, p.astype(BF16), TN, preferred_element_type=F32)

        @pl.when(i == nq - 1)
        def _():
            rot = None
            for u in range(hp):
                dkh = dk[u].T
                lane = lax.broadcasted_iota(jnp.int32, dkh.shape, 1)
                dkv_ref[:, u * HEAD_PAD:(u + 1) * HEAD_PAD] = jnp.where(lane < 64, dkh, dv[u].T).astype(dkv_ref.dtype)
                part = jnp.where((lane >= 64) & (lane < 96), dkh, 0.0)
                rot = part if rot is None else rot + part
            rot = _rope_t(rot, tk_ref[...], inverse=True)

            @pl.when(g == 0)
            def _():
                dkr_ref[...] = rot

            @pl.when(g > 0)
            def _():
                dkr_ref[...] += rot

    _, _, kr, tab_q, tab_k = _attn_specs(tq, TT)
    qs = pl.BlockSpec((tq, w), lambda h, i: (i, h))
    keys = pl.BlockSpec((TT, w), lambda h, i: (0, h))
    wide = lambda rows: jax.ShapeDtypeStruct((rows, N_HEADS * HEAD_PAD), BF16)
    return pl.pallas_call(
        body, grid=(N_HEADS // hp, nq),
        in_specs=[qs, keys, kr, tab_q, tab_k, qs, qs],
        out_specs=[qs, keys, pl.BlockSpec((TT, HEAD_PAD), lambda h, i: (0, 0))],
        out_shape=[wide(T), wide(TT), jax.ShapeDtypeStruct((TT, HEAD_PAD), F32)],
        scratch_shapes=[pltpu.VMEM((hp, TT, HEAD_PAD), BF16), pltpu.VMEM((hp, TT, HEAD_PAD), BF16),
                        pltpu.VMEM((hp, HEAD_PAD, TT), F32), pltpu.VMEM((hp, HEAD_PAD, TT), F32)],
        compiler_params=_cp(("arbitrary", "arbitrary")), name="attn_bwd",
    )(*_in_hbm([q_raw, kv, pp, tab, tab, o, do]))


def _hbm_specs(n):
    return [pl.BlockSpec(memory_space=pl.ANY)] * n


def _gather_weights(shards):
    n = len(shards)
    halves = [s.shape[0] // 2 for s in shards]

    def body(*refs):
        ins, outs = refs[:n], refs[n:2 * n]
        token, send_sems, recv_sems = refs[2 * n:]
        token[...] = jnp.zeros_like(token)
        mx, my, mc = lax.axis_index("x"), lax.axis_index("y"), lax.axis_index("c")
        j_me = 2 * mx + my
        chips = [(1 - mx, my), (mx, 1 - my), (1 - mx, 1 - my)]

        def half(w, chip_idx, hc):
            return outs[w].at[chip_idx, pl.ds(hc * halves[w], halves[w]), :]

        def copy(w, k, src, dst, to):
            return pltpu.make_async_remote_copy(src_ref=src, dst_ref=dst, send_sem=send_sems.at[w, k],
                                                recv_sem=recv_sems.at[w, k], device_id=to, device_id_type=MESH)

        sends = []
        for w in range(n):
            cp = copy(w, 6, ins[w], outs[w].at[j_me], (mx, my, 1 - mc))
            cp.start()
            sends.append(cp)
        for k, (px, py) in enumerate(chips):
            for w in range(n):
                cp = copy(w, k, ins[w].at[pl.ds(mc * halves[w], halves[w]), :], half(w, j_me, mc), (px, py, mc))
                cp.start()
                sends.append(cp)
        for k, (px, py) in enumerate(chips):
            for w in range(n):
                got = half(w, 2 * px + py, mc)
                copy(w, k, got, got, (px, py, mc)).wait_recv()
                cp = copy(w, 3 + k, got, got, (mx, my, 1 - mc))
                cp.start()
                sends.append(cp)
        for k, (px, py) in enumerate(chips):
            for w in range(n):
                got = half(w, 2 * px + py, 1 - mc)
                copy(w, 3 + k, got, got, (mx, my, 1 - mc)).wait_recv()
        for w in range(n):
            own = outs[w].at[j_me]
            copy(w, 6, own, own, (mx, my, 1 - mc)).wait_recv()
        for cp in sends:
            cp.wait_send()

    res = pl.pallas_call(
        body, out_shape=[jax.ShapeDtypeStruct((4,) + s.shape, s.dtype) for s in shards]
        + [jax.ShapeDtypeStruct((8, 128), F32)],
        in_specs=_hbm_specs(n), out_specs=_hbm_specs(n) + [pl.BlockSpec(memory_space=pltpu.VMEM)],
        scratch_shapes=[pltpu.SemaphoreType.DMA((n, 7)), pltpu.SemaphoreType.DMA((n, 7))],
        name="gather_weights")(*shards)
    return list(res[:n]), res[n]


def _rs_pair(gs, name):
    n = len(gs)
    halves = [g.shape[1] // 2 for g in gs]

    def body(*refs):
        ins, lands = refs[:n], refs[n:2 * n]
        send_sems, recv_sems = refs[2 * n:]
        mx, my, mc = lax.axis_index("x"), lax.axis_index("y"), lax.axis_index("c")
        copies = []
        for w in range(n):
            h = halves[w]
            cp = pltpu.make_async_remote_copy(
                src_ref=ins[w].at[:, pl.ds((1 - mc) * h, h), :], dst_ref=lands[w], send_sem=send_sems.at[w],
                recv_sem=recv_sems.at[w], device_id=(mx, my, 1 - mc), device_id_type=MESH)
            cp.start()
            copies.append(cp)
        for cp in copies:
            cp.wait()

    return pl.pallas_call(
        body, out_shape=[jax.ShapeDtypeStruct((4, h, g.shape[2]), g.dtype) for g, h in zip(gs, halves)],
        in_specs=_hbm_specs(n), out_specs=_hbm_specs(n),
        scratch_shapes=[pltpu.SemaphoreType.DMA((n,)), pltpu.SemaphoreType.DMA((n,))], name=name)(*gs)


def _rs_chips(parts):
    n = len(parts)

    def body(*refs):
        ins, lands = refs[:n], refs[n:2 * n]
        send_sems, recv_sems = refs[2 * n:]
        mx, my, mc = lax.axis_index("x"), lax.axis_index("y"), lax.axis_index("c")
        copies = []
        for k, (px, py) in enumerate([(1 - mx, my), (mx, 1 - my), (1 - mx, 1 - my)]):
            for w in range(n):
                cp = pltpu.make_async_remote_copy(
                    src_ref=ins[w].at[2 * px + py], dst_ref=lands[w].at[k], send_sem=send_sems.at[w, k],
                    recv_sem=recv_sems.at[w, k], device_id=(px, py, mc), device_id_type=MESH)
                cp.start()
                copies.append(cp)
        for cp in copies:
            cp.wait()

    return list(pl.pallas_call(
        body, out_shape=[jax.ShapeDtypeStruct((3,) + p.shape[1:], p.dtype) for p in parts],
        in_specs=_hbm_specs(n), out_specs=_hbm_specs(n),
        scratch_shapes=[pltpu.SemaphoreType.DMA((n, 3)), pltpu.SemaphoreType.DMA((n, 3))], name="rs_chips")(*parts))


_HBM = pl.BlockSpec(memory_space=pltpu.HBM)
_SEM = pl.BlockSpec(memory_space=pltpu.SEMAPHORE)
_EFFECT = pltpu.SideEffectType.DATAFLOW_SIDE_EFFECTING


def _ici_copies(kind, srcs, lands, send_sems, recv_sems):
    n = len(lands)
    mx, my, mc = lax.axis_index("x"), lax.axis_index("y"), lax.axis_index("c")
    j_me = 2 * mx + my
    copies = []
    if kind == "back":
        for w in range(n):
            h = lands[w].shape[0] // 2
            mine = lands[w].at[pl.ds(mc * h, h), :]
            copies.append(pltpu.make_async_remote_copy(
                src_ref=mine, dst_ref=mine, send_sem=send_sems.at[w], recv_sem=recv_sems.at[w],
                device_id=(mx, my, 1 - mc), device_id_type=MESH))
        return copies
    if kind == "all":
        for k in range(7):
            a, b, c = (k + 1) >> 2 & 1, (k + 1) >> 1 & 1, (k + 1) & 1
            peer = (1 - mx if a else mx, 1 - my if b else my, 1 - mc if c else mc)
            for w in range(n):
                copies.append(pltpu.make_async_remote_copy(
                    src_ref=srcs[w], dst_ref=lands[w].at[4 * mx + 2 * my + mc], send_sem=send_sems.at[7 * w + k],
                    recv_sem=recv_sems.at[7 * w + k], device_id=peer, device_id_type=MESH))
        return copies
    if kind == "pair":
        for w in range(n):
            h = srcs[w].shape[1] // 2
            copies.append(pltpu.make_async_remote_copy(
                src_ref=srcs[w].at[:, pl.ds((1 - mc) * h, h), :], dst_ref=lands[w], send_sem=send_sems.at[w],
                recv_sem=recv_sems.at[w], device_id=(mx, my, 1 - mc), device_id_type=MESH))
        return copies
    chips = [(1 - mx, my), (mx, 1 - my), (1 - mx, 1 - my)]
    if kind == "finish":
        for w in range(n):
            h = srcs[w].shape[0] // 2
            pushes = [(lands[w].at[2 * px + py, pl.ds(mc * h, h), :],) * 2 for px, py in chips]
            pushes.append((srcs[w], lands[w].at[j_me]))
            for k, (src, dst) in enumerate(pushes):
                copies.append(pltpu.make_async_remote_copy(
                    src_ref=src, dst_ref=dst, send_sem=send_sems.at[4 * w + k], recv_sem=recv_sems.at[4 * w + k],
                    device_id=(mx, my, 1 - mc), device_id_type=MESH))
        return copies
    for k, (px, py) in enumerate(chips):
        for w in range(n):
            if kind == "gather":
                h = srcs[w].shape[0] // 2
                src, dst = srcs[w].at[pl.ds(mc * h, h), :], lands[w].at[j_me, pl.ds(mc * h, h), :]
            else:
                src, dst = srcs[w].at[2 * px + py], lands[w].at[k]
            copies.append(pltpu.make_async_remote_copy(
                src_ref=src, dst_ref=dst, send_sem=send_sems.at[3 * w + k], recv_sem=recv_sems.at[3 * w + k],
                device_id=(px, py, mc), device_id_type=MESH))
    return copies


_SEMS_PER_OPERAND = {"gather": 3, "scatter": 3, "all": 7, "pair": 1, "finish": 4, "back": 1}


def _ici_start(kind, srcs, land_shapes, carry, name, lands=None):
    hbm = lambda a: pltpu.with_memory_space_constraint(a, pltpu.HBM)
    if lands is None:
        lands = [lax.empty(s, srcs[0].dtype) for s in land_shapes]
    ns, nl = len(srcs), len(lands)

    def body(*refs):
        send_sems, recv_sems = refs[ns + nl + 1], refs[ns + nl + 2]
        for cp in _ici_copies(kind, refs[:ns], refs[ns:ns + nl], send_sems, recv_sems):
            cp.start()

    args = [hbm(a) for a in list(srcs) + list(lands) + [carry]]
    n_sem = _SEMS_PER_OPERAND[kind] * nl
    out_shape = ([pltpu.SemaphoreType.DMA((n_sem,)), pltpu.SemaphoreType.DMA((n_sem,))]
                 + [pltpu.HBM(a.shape, a.dtype) for a in args])
    res = pl.pallas_call(
        body, name=name, out_shape=out_shape, in_specs=[_HBM] * len(args), out_specs=[_SEM, _SEM] + [_HBM] * len(args),
        input_output_aliases={i: 2 + i for i in range(len(args))},
        compiler_params=pltpu.CompilerParams(has_side_effects=_EFFECT))(*args)
    return res[0], res[1], list(res[2:2 + ns]), list(res[2 + ns:2 + ns + nl]), res[2 + ns + nl]


def _ici_wait(kind, send_sems, recv_sems, srcs, lands, after, name):
    ns, nl = len(srcs), len(lands)

    def body(*refs):
        for cp in _ici_copies(kind, refs[:ns], refs[ns:ns + nl], refs[ns + nl], refs[ns + nl + 1]):
            cp.wait_send()
            cp.wait_recv()

    args = list(srcs) + list(lands)
    res = pl.pallas_call(
        body, name=name, out_shape=[pltpu.HBM(a.shape, a.dtype) for a in args],
        in_specs=[_HBM] * len(args) + [_SEM, _SEM, pl.BlockSpec(memory_space=pl.ANY)], out_specs=[_HBM] * len(args),
        input_output_aliases={i: i for i in range(len(args))},
        compiler_params=pltpu.CompilerParams(has_side_effects=_EFFECT))(*args, send_sems, recv_sems, after)
    return list(res[:ns]), list(res[ns:])


def _tile_rows(h, c, itemsize, mult):
    best = h
    for t in range(mult, h + 1, mult):
        if h % t == 0 and t * c * itemsize <= (1 << 21):
            best = t
    return best


def _add_pair(g, land, place, name):
    _, h, c = land.shape
    t = _tile_rows(h, c, 2, 16)
    nb = h // t
    return _ew(lambda ids, u, v: (u.astype(F32) + v.astype(F32),), (4, nb),
               [(g, pl.BlockSpec((None, t, c), lambda j, i, s: (j, s[1] * nb + i, 0))),
                (land, pl.BlockSpec((None, t, c), lambda j, i, s: (j, i, 0)))],
               [(land.shape, BF16, pl.BlockSpec((None, t, c), lambda j, i, s: (j, i, 0)), None)], name, scalars=place)[0]


def _add_pair_many(gs, lands, place, name):
    ins, outs = [], []
    for g, l in zip(gs, lands):
        ins += [(g, pl.BlockSpec(l.shape, lambda i, s: (0, s[1], 0))), (l, pl.BlockSpec(l.shape, lambda i, s: (0, 0, 0)))]
        outs.append((l.shape, BF16, pl.BlockSpec(l.shape, lambda i, s: (0, 0, 0)), None))
    fn = lambda ids, *v: [v[2 * k].astype(F32) + v[2 * k + 1].astype(F32) for k in range(len(gs))]
    return list(_ew(fn, (1,), ins, outs, name, scalars=place))


def _add_chips_many(owns, lands, place, name):
    ins, outs = [], []
    for own, land in zip(owns, lands):
        _, h, c = land.shape
        ins += [(own, pl.BlockSpec((None, h, c), lambda i, s: (s[0], 0, 0))),
                (land, pl.BlockSpec((3, h, c), lambda i, s: (0, 0, 0)))]
        outs.append(((2 * h, c), F32, pl.BlockSpec((h, c), lambda i, s: (s[1], 0)), None))

    def fn(ids, *v):
        return [((v[2 * k].astype(F32) + v[2 * k + 1][0].astype(F32)) + v[2 * k + 1][1].astype(F32))
                + v[2 * k + 1][2].astype(F32) for k in range(len(owns))]

    return list(_ew(fn, (1,), ins, outs, name, scalars=place))


def _add_chips(own, land, place, name):
    _, h, c = land.shape
    t = _tile_rows(h, c, 4, 16)
    nb = h // t

    def fn(ids, a, b):
        return (((a.astype(F32) + b[0].astype(F32)) + b[1].astype(F32)) + b[2].astype(F32),)

    return _ew(fn, (nb,), [(own, pl.BlockSpec((None, t, c), lambda i, s: (s[0], i, 0))),
                           (land, pl.BlockSpec((3, t, c), lambda i, s: (0, i, 0)))],
               [((2 * h, c), F32, pl.BlockSpec((t, c), lambda i, s: (s[1] * nb + i, 0)), None)], name, scalars=place)[0]


W_IN_SEGMENTS = ((0, 256, KV0), (256, 288, KR0 + 64), (288, 672, Q0), (672, 1184, CX0), (1184, 1696, CB0),
                 (1696, 2208, CC0), (2208, 3232, GA0), (3232, 4256, GC0))
W_IN_SHARD = 1064


W_IN_SHARD_PAD = 1088
W_IN_EARLY = 672


def _w_in_t_p_from_shards(s):
    pieces = []
    for o0, o1, p0 in sorted(W_IN_SEGMENTS, key=lambda t: t[2]):
        if p0 == KR0 + 64:
            pieces.append(jnp.zeros((64, s.shape[2]), s.dtype))
        for j in range(4):
            lo, hi = max(o0, j * W_IN_SHARD), min(o1, (j + 1) * W_IN_SHARD)
            if lo < hi:
                pieces.append(s[j, lo - j * W_IN_SHARD:hi - j * W_IN_SHARD])
    pieces.append(jnp.zeros((32, s.shape[2]), s.dtype))
    return jnp.concatenate(pieces, axis=0)


def _w_in_t_shards_from_p(g):
    shards = []
    for j in range(4):
        pieces = []
        for o0, o1, p0 in W_IN_SEGMENTS:
            lo, hi = max(o0, j * W_IN_SHARD), min(o1, (j + 1) * W_IN_SHARD)
            if lo < hi:
                pieces.append(g[p0 + lo - o0:p0 + hi - o0])
        pieces.append(jnp.zeros((W_IN_SHARD_PAD - W_IN_SHARD, g.shape[1]), g.dtype))
        shards.append(jnp.concatenate(pieces, axis=0))
    return jnp.stack(shards, axis=0)


def _cols_from_shards(s):
    return jnp.transpose(s, (1, 0, 2)).reshape(s.shape[1], -1)


def _rope_tables(T, TT, inverse):
    f32 = np.float32
    rows = T // GRID_W
    row = np.repeat(np.arange(rows), GRID_W).astype(f32)
    col = np.tile(np.arange(GRID_W), rows).astype(f32)
    inv = (f32(ROPE_THETA) ** (-np.arange(0, 16, 2, dtype=f32) / f32(16))).astype(f32)
    ang = np.concatenate([row[:, None] * inv, col[:, None] * inv], axis=-1).astype(f32)
    cos, sin = np.cos(ang).astype(f32), np.sin(ang).astype(f32)
    lane = np.arange(32)
    src = (lane // 16) * 8 + lane % 8
    lo = ((lane % 16) // 8 == 0).astype(f32)
    sgn = f32(-1.0 if inverse else 1.0)
    cos32 = cos[:, src]
    sin_lo32 = -sgn * sin[:, src] * lo
    sin_hi32 = sgn * sin[:, src] * (1 - lo)

    def widen(t32, fill):
        t = np.concatenate([np.full((T, 64), fill, f32), t32, np.full((T, 32), fill, f32)], axis=1)
        return np.concatenate([t, np.full((TT - T, HEAD_PAD), fill, f32)], axis=0)

    return [widen(cos32, 1.0), widen(sin_lo32, 0.0), widen(sin_hi32, 0.0)]


def _rope_table(T, TT):
    return jnp.asarray(np.concatenate(_rope_tables(T, TT, False) + _rope_tables(T, TT, True), axis=1))


def _local_step(xx, tgt, mod_lat, mod_ctx, W, late_weights, early_grads, early_continue):
    TT = xx.shape[0]
    T = tgt.shape[0]
    n_lat, n_all = T // ROW_TILE, TT // ROW_TILE
    sh1, sc1, g1, sh2, sc2, g2 = [mod_lat[:, k * D_MODEL:(k + 1) * D_MODEL] for k in range(6)]
    csh1, csc1 = mod_ctx[:, :D_MODEL], mod_ctx[:, D_MODEL:2 * D_MODEL]
    vec = lambda n: _full((1, n))
    row_out = lambda n, dt, rows=T: ((rows, n), dt, _rows(n), None)
    acc_out = lambda n: ((1, n), F32, _full((1, n)), 0)
    lt = _pick(T, (2 * ROW_TILE, ROW_TILE))
    n_lt = T // lt
    lrows = lambda n, cblk=0: _rows(n, cblk, 0, lt)
    lrow_out = lambda n, dt: ((T, n), dt, lrows(n), None)

    def f_norm1(ids, x, g, a_sh, a_sc, b_sh, b_sc):
        ctx = ids[0] >= n_lat
        sh, sc = jnp.where(ctx, b_sh, a_sh), jnp.where(ctx, b_sc, a_sc)
        return ((x * _rms(x) * g) * (1.0 + sc) + sh,)

    (hh,) = _ew(f_norm1, (n_all,), [(xx, _rows(D_MODEL)), (W["norm1_g"], vec(D_MODEL)), (sh1, vec(D_MODEL)),
                                   (sc1, vec(D_MODEL)), (csh1, vec(D_MODEL)), (csc1, vec(D_MODEL))],
                [row_out(D_MODEL, BF16, TT)], "norm1_fwd")
    tm_all = _pick(TT, (768, 256))
    pp_a = _mm(hh, W["w_in_a_t"], "nt", TT, PA_COLS, D_MODEL, tm=tm_all, tn=PA_COLS, tk=D_MODEL, name="w_in_fwd_a")

    def f_lowrank(ids, ckv, cq, gkv, gq):
        return ckv * _rms(ckv) * gkv, cq * _rms(cq) * gq

    nkv, nq = _ew(f_lowrank, (n_all,), [(pp_a, _rows(KV_RANK, PA_KV0 // KV_RANK)), (pp_a, _rows(Q_RANK, PA_Q0 // Q_RANK)),
                                       (W["kv_norm_g"], vec(KV_RANK)), (W["q_norm_g"], vec(Q_RANK))],
                  [row_out(KV_RANK, BF16, TT), row_out(Q_RANK, BF16, TT)], "lowrank_norm_fwd")
    kv = _mm(nkv, W["w_ukv"], "nn", TT, 1024, KV_RANK, tm=tm_all, tn=256, tk=KV_RANK, name="w_ukv_fwd",
             b_spec=pl.BlockSpec((None, KV_RANK, 256), lambda i, j, k: (j, k, 0)))
    q_raw = _mm(nq, W["w_uq_t"], "nt", TT, 1024, Q_RANK, tm=tm_all, tn=1024, tk=Q_RANK, name="w_uq_fwd")

    tab = _rope_table(T, TT)
    _, q_raw = late_weights("before_attn", q_raw)
    o_pad = _attn_fwd(q_raw, kv, pp_a, tab, T, TT)
    arrived, o_pad = late_weights("after_attn", o_pad)
    W = dict(W, **arrived)
    tm_lat = _pick(T, (1024, 512, 256))
    pp = _mm(hh, W["w_in_t"], "nt", T, KV0, D_MODEL, tm=tm_lat, tn=KV0 // 2, tk=D_MODEL, name="w_in_fwd_b")
    ya = _mm(o_pad, W["w_attn_out"], "nn", T, D_MODEL, 1024, tm=tm_lat, tn=D_MODEL, tk=1024, name="w_attn_out_fwd",
             out_dtype=BF16)

    tc = 256
    colT = lambda blk0: pl.BlockSpec((T, tc), lambda j: (0, blk0 + j))

    def f_conv(ids, xin, cb, cc, w, b):
        return (cb * _conv(cc * xin, w, b),)

    (e,) = _ew(f_conv, (CONV_DIM // tc,),
               [(pp, colT(CX0 // tc)), (pp, colT(CB0 // tc)), (pp, colT(CC0 // tc)),
                (W["conv_w"], pl.BlockSpec((3, tc), lambda j: (0, j))), (W["conv_b"], pl.BlockSpec((1, tc), lambda j: (0, j)))],
               [((T, CONV_DIM), BF16, colT(0), None)], "conv_fwd")
    yc = _mm(e, W["w_conv_out"], "nn", T, D_MODEL, CONV_DIM, tm=tm_lat, tn=256, tk=CONV_DIM, name="w_conv_out_fwd",
             out_dtype=BF16, b_spec=pl.BlockSpec((None, CONV_DIM, 256), lambda i, j, k: (j, k, 0)))

    def f_merge(ids, ga, gc, a, c):
        return (_sigmoid(ga) * a.astype(F32) + _sigmoid(gc) * c.astype(F32),)

    (mrg,) = _ew(f_merge, (n_lt,), [(pp, lrows(D_MODEL, 0)), (pp, lrows(D_MODEL, 1)), (ya, lrows(D_MODEL)),
                                   (yc, lrows(D_MODEL))], [lrow_out(D_MODEL, BF16)], "merge_fwd")
    mo = _mm(mrg, W["w_o"], "nn", T, D_MODEL, D_MODEL, tm=tm_lat, tn=D_MODEL, tk=D_MODEL, name="w_o_fwd")

    def f_norm2(ids, x, m, gate, g, sh, sc):
        x1 = x + gate * m
        return x1, (x1 * _rms(x1) * g) * (1.0 + sc) + sh

    x1, h2 = _ew(f_norm2, (n_lt,), [(xx, lrows(D_MODEL)), (mo, lrows(D_MODEL)), (g1, vec(D_MODEL)),
                                   (W["norm2_g"], vec(D_MODEL)), (sh2, vec(D_MODEL)), (sc2, vec(D_MODEL))],
                 [lrow_out(D_MODEL, F32), lrow_out(D_MODEL, BF16)], "norm2_fwd")
    arrived, h2 = late_weights("before_ffn", h2)
    W = dict(W, **arrived)
    up = _mm(h2, W["w_up"], "nn", T, 2 * D_FF, D_MODEL, tm=tm_lat, tn=1408, tk=D_MODEL, name="w_up_fwd",
             b_spec=pl.BlockSpec((None, D_MODEL, 1408), lambda i, j, k: (j, k, 0)))

    n_ff = D_FF // tc
    ffw = lambda off, n=3: pl.BlockSpec((n, tc), lambda j: (0, j + off))

    def f_ffn(ids, ug, uv, wg, wv, bg, bv):
        gate, val = _conv(ug, wg, bg), _conv(uv, wv, bv)
        return (gate * _sigmoid(gate) * val,)

    (act,) = _ew(f_ffn, (n_ff,), [(up, colT(0)), (up, colT(n_ff)), (W["ffn_conv_w"], ffw(0)), (W["ffn_conv_w"], ffw(n_ff)),
                                 (W["ffn_conv_b"], ffw(0, 1)), (W["ffn_conv_b"], ffw(n_ff, 1))],
                 [((T, D_FF), BF16, colT(0), None)], "ffn_act_fwd")
    f = _mm(act, W["w_down"], "nn", T, D_MODEL, D_FF, tm=tm_lat, tn=D_MODEL, tk=D_FF, name="w_down_fwd")

    def f_head(ids, x1_, f_, gate, gf, t):
        x2 = x1_ + gate * f_
        r = _rms(x2)
        xn = x2 * r
        err = xn * gf - t
        loss = 0.5 * jnp.sum(jnp.mean(err * err, axis=-1, keepdims=True))
        dy = err * (1.0 / D_MODEL)
        dx2 = _rms_bwd(dy * gf, xn, r)
        return dx2, dx2 * gate, _colsum(dy * xn), _colsum(dx2 * f_), jnp.full((1, 128), loss, F32)

    dx2, df, dg_f, dg2, loss = _ew(
        f_head, (n_lt,), [(x1, lrows(D_MODEL)), (f, lrows(D_MODEL)), (g2, vec(D_MODEL)), (W["final_g"], vec(D_MODEL)),
                          (tgt, lrows(D_MODEL))],
        [lrow_out(D_MODEL, F32), lrow_out(D_MODEL, BF16), acc_out(D_MODEL), acc_out(D_MODEL), acc_out(128)], "loss_head")

    d_w_down = _mm(act, df, "tn", D_FF, D_MODEL, T, tm=1408, tn=D_MODEL, tk=T, name="w_down_dw",
                   out_dtype=BF16).reshape(4, D_FF // 4, D_MODEL)
    da = _mm(df, W["w_down"], "nt", T, D_FF, D_MODEL, tm=tm_lat, tn=1408, tk=D_MODEL, name="w_down_dx")

    tcb = 128
    n_fb = D_FF // tcb
    colb = lambda blk0: pl.BlockSpec((T, tcb), lambda j: (0, blk0 + j))
    ffwb = lambda off, n=3: pl.BlockSpec((n, tcb), lambda j: (0, j + off))
    cvec = ((1, D_FF), F32, pl.BlockSpec((1, tcb), lambda j: (0, j)), None)

    def f_ffn_bwd(ids, ug, uv, d_act, wg, wv, bg, bv):
        sg, sv = _shifts(ug), _shifts(uv)
        gate, val = _conv(ug, wg, bg, sg), _conv(uv, wv, bv, sv)
        s = _sigmoid(gate)
        d_gate = d_act * val * s * (1.0 + gate * (1.0 - s))
        d_val = d_act * gate * s
        wg0, wg1, wg2 = _conv_bwd_w(d_gate, ug, sg)
        wv0, wv1, wv2 = _conv_bwd_w(d_val, uv, sv)
        d_up = [_conv_bwd_x(d_gate, wg), _conv_bwd_x(d_val, wv)]
        return d_up, [_colsum(d_gate), _colsum(d_val), wg0, wg1, wg2, wv0, wv1, wv2]

    d_up3, ffn_stats = _ew(
        f_ffn_bwd, (n_fb,),
        [(up, colb(0)), (up, colb(n_fb)), (da, colb(0)), (W["ffn_conv_w"], ffwb(0)), (W["ffn_conv_w"], ffwb(n_fb)),
         (W["ffn_conv_b"], ffwb(0, 1)), (W["ffn_conv_b"], ffwb(n_fb, 1))],
        [((2, T, D_FF), BF16, pl.BlockSpec((2, T, tcb), lambda j: (0, 0, j)), None),
         ((n_fb, 8, 1, tcb), F32, pl.BlockSpec((None, 8, 1, tcb), lambda j: (j, 0, 0, 0)), None)], "ffn_act_bwd")
    stat = lambda s: ffn_stats[:, s, 0, :].reshape(1, D_FF)
    d_ffn_conv_b = jnp.concatenate([stat(0), stat(1)], axis=1)
    d_ffn_conv_w = jnp.concatenate([jnp.concatenate([stat(2), stat(3), stat(4)], axis=0),
                                    jnp.concatenate([stat(5), stat(6), stat(7)], axis=0)], axis=1)

    tk_t = T
    d_w_up = _mm(h2, d_up3, "tn", D_MODEL, 2 * D_FF, T, tm=D_MODEL, tn=1408, tk=tk_t, name="w_up_dw", out_dtype=BF16,
                 b_spec=pl.BlockSpec((None, tk_t, 1408), lambda i, j, k: (j // 2, k, j % 2)),
                 o_spec=pl.BlockSpec((None, D_MODEL, 1408), lambda i, j, k: (j, i, 0)), out_shape=(4, D_MODEL, 1408))
    dh2 = _mm(d_up3, W["w_up"], "nt", T, D_MODEL, 2 * D_FF, tm=tm_lat, tn=D_MODEL, tk=1408, name="w_up_dx",
              a_spec=pl.BlockSpec((None, tm_lat, 1408), lambda i, j, k: (k // 2, i, k % 2)),
              b_spec=pl.BlockSpec((None, D_MODEL, 1408), lambda i, j, k: (k, j, 0)))

    def f_norm2_bwd(ids, dx2_, dh, x1_, m, g, sc, gate):
        r = _rms(x1_)
        xn = x1_ * r
        dx1 = dx2_ + _rms_bwd(dh * g * (1.0 + sc), xn, r)
        return dx1, dx1 * gate, _colsum(dh), _colsum(dh * xn * g), _colsum(dh * xn * (1.0 + sc)), _colsum(dx1 * m)

    dx1, dmo, dsh2, dsc2, dg_n2, dg1 = _ew(
        f_norm2_bwd, (n_lt,), [(dx2, lrows(D_MODEL)), (dh2, lrows(D_MODEL)), (x1, lrows(D_MODEL)), (mo, lrows(D_MODEL)),
                               (W["norm2_g"], vec(D_MODEL)), (sc2, vec(D_MODEL)), (g1, vec(D_MODEL))],
        [lrow_out(D_MODEL, F32), lrow_out(D_MODEL, BF16)] + [acc_out(D_MODEL)] * 4, "norm2_bwd")
    d_w_o = _mm(mrg, dmo, "tn", D_MODEL, D_MODEL, T, tm=D_MODEL, tn=D_MODEL, tk=tk_t, name="w_o_dw",
                out_dtype=BF16).reshape(4, D_MODEL // 4, D_MODEL)
    dmrg = _mm(dmo, W["w_o"], "nt", T, D_MODEL, D_MODEL, tm=tm_lat, tn=D_MODEL, tk=D_MODEL, name="w_o_dx",
               out_dtype=BF16)
    dmrg = early_grads("late", {"w_o": d_w_o, "w_up": d_w_up, "w_down": d_w_down}, dmrg, split=True)

    def f_merge_bwd(ids, dm, ga, gc, a, c):
        dm, a, c = dm.astype(F32), a.astype(F32), c.astype(F32)
        sa, sc_ = _sigmoid(ga), _sigmoid(gc)
        return dm * sa, dm * sc_, dm * a * sa * (1.0 - sa), dm * c * sc_ * (1.0 - sc_)

    dya, dyc, dp_ga, dp_gc = _ew(
        f_merge_bwd, (n_lt,), [(dmrg, lrows(D_MODEL)), (pp, lrows(D_MODEL, 0)), (pp, lrows(D_MODEL, 1)),
                               (ya, lrows(D_MODEL)), (yc, lrows(D_MODEL))], [lrow_out(D_MODEL, BF16)] * 4, "merge_bwd")
    dya = early_continue("late", dya)

    d_w_ao_p = _mm(o_pad, dya, "tn", 1024, D_MODEL, T, tm=1024, tn=D_MODEL, tk=tk_t, name="w_attn_out_dw", out_dtype=BF16)
    do_pad = _mm(dya, W["w_attn_out"], "nt", T, 1024, D_MODEL, tm=tm_lat, tn=1024, tk=D_MODEL, name="w_attn_out_dx")
    d_w_co = _mm(e, dyc, "tn", CONV_DIM, D_MODEL, T, tm=CONV_DIM, tn=256, tk=tk_t, name="w_conv_out_dw", out_dtype=BF16,
                 o_spec=pl.BlockSpec((None, CONV_DIM, 256), lambda i, j, k: (j, i, 0)), out_shape=(4, CONV_DIM, 256))
    de = _mm(dyc, W["w_conv_out"], "nt", T, CONV_DIM, D_MODEL, tm=tm_lat, tn=CONV_DIM, tk=256, name="w_conv_out_dx",
             b_spec=pl.BlockSpec((None, CONV_DIM, 256), lambda i, j, k: (k, j, 0)))

    def f_conv_bwd(ids, xin, cb, cc, d_e, w, b):
        z = cc * xin
        sz = _shifts(z)
        cz = _conv(z, w, b, sz)
        dcz = d_e * cb
        w0, w1, w2 = _conv_bwd_w(dcz, z, sz)
        dz = _conv_bwd_x(dcz, w)
        return dz * cc, d_e * cz, dz * xin, _colsum(dcz), w0, w1, w2

    cvec_c = ((1, CONV_DIM), F32, pl.BlockSpec((1, tc), lambda j: (0, j)), None)
    conv_b = _ew(f_conv_bwd, (CONV_DIM // tc,),
                 [(pp, colT(CX0 // tc)), (pp, colT(CB0 // tc)), (pp, colT(CC0 // tc)), (de, colT(0)),
                  (W["conv_w"], pl.BlockSpec((3, tc), lambda j: (0, j))), (W["conv_b"], pl.BlockSpec((1, tc), lambda j: (0, j)))],
                 [((T, CONV_DIM), BF16, colT(0), None)] * 3 + [cvec_c] * 4, "conv_bwd")
    dp_cx, dp_cb, dp_cc, d_conv_b = conv_b[:4]
    d_conv_w = jnp.concatenate(conv_b[4:7], axis=0)

    dq_raw, dkv, dp_kr = _attn_bwd(q_raw, kv, pp_a, o_pad, do_pad, tab, T, TT)

    tk_a = TT
    d_w_uq_t = _mm(nq, dq_raw, "tn", Q_RANK, 1024, T, tm=Q_RANK, tn=1024, tk=T, name="w_uq_dw", transpose_out=True)
    dnq = _mm(dq_raw, W["w_uq_t"], "nn", T, Q_RANK, 1024, tm=tm_lat, tn=Q_RANK, tk=1024, name="w_uq_dx")
    d_w_ukv = _mm(nkv, dkv, "tn", KV_RANK, 1024, TT, tm=KV_RANK, tn=256, tk=tk_a, name="w_ukv_dw", out_dtype=BF16,
                  o_spec=pl.BlockSpec((None, KV_RANK, 256), lambda i, j, k: (j, i, 0)), out_shape=(4, KV_RANK, 256))
    dnkv = _mm(dkv, W["w_ukv"], "nt", TT, KV_RANK, 1024, tm=tm_all, tn=KV_RANK, tk=256, name="w_ukv_dx",
               b_spec=pl.BlockSpec((None, KV_RANK, 256), lambda i, j, k: (k, j, 0)))
    dnkv = early_grads("mid", {
        "w_attn_out": jnp.transpose(d_w_ao_p.reshape(N_HEADS, HEAD_PAD, 4, 256)[:, 64:], (2, 0, 1, 3)).reshape(
            4, N_HEADS * 64, 256),
        "w_conv_out": d_w_co,
        "w_uq": d_w_uq_t.reshape(4, 2, HEAD_PAD, Q_RANK)[:, :, :QK_DIM].reshape(4, 2 * QK_DIM, Q_RANK).astype(BF16),
        "w_ukv": d_w_ukv}, dnkv)

    def f_lowrank_bwd(ids, ckv, cq, dkv_, dq_, gkv, gq, ga, gc, cx, cb, cc, kr):
        rk, rq = _rms(ckv), _rms(cq)
        nk, nq_ = ckv * rk, cq * rq
        lat = ids[0] < n_lat
        dq_ = jnp.where(lat, dq_, 0.0)
        pieces = [jnp.where(lat, a, jnp.zeros_like(a)) for a in (ga, gc, cx, cb, cc)]
        pieces += [_rms_bwd(dkv_ * gkv, nk, rk).astype(BF16), _rms_bwd(dq_ * gq, nq_, rq).astype(BF16), kr.astype(BF16)]
        return jnp.concatenate(pieces, axis=1), _colsum(dkv_ * nk), _colsum(dq_ * nq_)

    lat_rows = lambda n: pl.BlockSpec((ROW_TILE, n), lambda i: (jnp.minimum(i, n_lat - 1), 0))
    dpp, dg_kv, dg_q = _ew(
        f_lowrank_bwd, (n_all,), [(pp_a, _rows(KV_RANK, PA_KV0 // KV_RANK)), (pp_a, _rows(Q_RANK, PA_Q0 // Q_RANK)),
                                  (dnkv, _rows(KV_RANK)), (dnq, lat_rows(Q_RANK)), (W["kv_norm_g"], vec(KV_RANK)),
                                  (W["q_norm_g"], vec(Q_RANK)), (dp_ga, lat_rows(D_MODEL)), (dp_gc, lat_rows(D_MODEL)),
                                  (dp_cx, lat_rows(CONV_DIM)), (dp_cb, lat_rows(CONV_DIM)), (dp_cc, lat_rows(CONV_DIM)),
                                  (dp_kr, _rows(HEAD_PAD))],
        [row_out(P_COLS, BF16, TT), acc_out(KV_RANK), acc_out(Q_RANK)], "lowrank_norm_bwd")
    d_w_in_t = _mm(hh, dpp, "tn", D_MODEL, P_COLS, TT, tm=512, tn=2176, tk=TT, name="w_in_dw", out_dtype=BF16,
                   transpose_out=True)
    dhh = _mm(dpp, W["w_in_t"], "nn", TT, D_MODEL, P_COLS, tm=tm_all, tn=512, tk=2176, name="w_in_dx")

    def f_norm1_bwd(ids, x, dh, dres, g, sc):
        r = _rms(x)
        xn = x * r
        return (dres + _rms_bwd(dh * g * (1.0 + sc), xn, r), _colsum(dh), _colsum(dh * xn * g),
                _colsum(dh * xn * (1.0 + sc)))

    grad_x, dsh1, dsc1, dg_n1 = _ew(
        f_norm1_bwd, (n_lt,), [(xx, lrows(D_MODEL)), (dhh, lrows(D_MODEL)), (dx1, lrows(D_MODEL)),
                               (W["norm1_g"], vec(D_MODEL)), (sc1, vec(D_MODEL))],
        [lrow_out(D_MODEL, F32)] + [acc_out(D_MODEL)] * 3, "norm1_bwd")

    def f_norm1_ctx_bwd(ids, x, dh, g, sc):
        xn = x * _rms(x)
        return _colsum(dh), _colsum(dh * xn * g), _colsum(dh * xn * (1.0 + sc))

    n_ctx = n_all - n_lat
    dcsh1, dcsc1, dg_n1c = _ew(
        f_norm1_ctx_bwd, (n_ctx,), [(xx, _rows(D_MODEL, 0, n_lat)), (dhh, _rows(D_MODEL, 0, n_lat)),
                                    (W["norm1_g"], vec(D_MODEL)), (csc1, vec(D_MODEL))], [acc_out(D_MODEL)] * 3,
        "norm1_ctx_bwd")

    big = {"w_in": _w_in_t_shards_from_p(d_w_in_t).astype(BF16)}
    zero = jnp.zeros((1, 4 * D_MODEL), F32)
    small = {
        "dmod_lat": jnp.concatenate([dsh1, dsc1, dg1, dsh2, dsc2, dg2], axis=1),
        "dmod_ctx": jnp.concatenate([dcsh1, dcsc1, zero], axis=1),
        "norm1_g": dg_n1 + dg_n1c, "norm2_g": dg_n2, "final_g": dg_f, "q_norm_g": dg_q, "kv_norm_g": dg_kv,
        "conv_b": d_conv_b, "conv_w": d_conv_w.reshape(1, -1), "ffn_conv_b": d_ffn_conv_b,
        "ffn_conv_w": d_ffn_conv_w.reshape(1, -1),
    }
    return grad_x, loss, big, small


SMALL = (("dmod_lat", 6144), ("dmod_ctx", 6144), ("norm1_g", 1024), ("norm2_g", 1024), ("final_g", 1024),
         ("q_norm_g", 384), ("kv_norm_g", 256), ("conv_b", 512), ("conv_w", 1536), ("ffn_conv_b", 5632),
         ("ffn_conv_w", 16896), ("loss", 128))
SMALL_ROWS = 320


def _adam_update(w, g, m, v):
    c1, c2 = 1.0 - ADAM_B1 ** ADAM_STEP, 1.0 - ADAM_B2 ** ADAM_STEP
    m2 = ADAM_B1 * m + (1.0 - ADAM_B1) * g
    v2 = ADAM_B2 * v + (1.0 - ADAM_B2) * (g * g)
    return [-ADAM_LR * ((m2 / c1) / (jnp.sqrt(v2 / c2) + ADAM_EPS) + ADAM_WD * w), m2, v2]


def _adamw(w, g, m, v, name):
    R, C = w.shape
    tr = 8 if R % 8 == 0 else R
    for t in range(8, R + 1, 8):
        if R % t == 0 and t * C * 4 <= (1 << 20):
            tr = t
    spec = pl.BlockSpec((tr, C), lambda i: (i, 0))
    return _ew(lambda ids, *vals: _adam_update(*vals), (R // tr,), [(w, spec), (g, spec), (m, spec), (v, spec)],
               [((R, C), F32, spec, None)] * 3, name)


def kernel(x, c, ctx, c_ctx, w_ada, b_ada, norm1_g, w_in, q_norm_g, kv_norm_g, w_uq, w_ukv, conv_w, conv_b, w_attn_out, w_conv_out, w_o, norm2_g, w_up, ffn_conv_w, ffn_conv_b, w_down, final_g, loss_target, m_c_ctx, m_w_ada, m_b_ada, m_norm1_g, m_w_in, m_q_norm_g, m_kv_norm_g, m_w_uq, m_w_ukv, m_conv_w, m_conv_b, m_w_attn_out, m_w_conv_out, m_w_o, m_norm2_g, m_w_up, m_ffn_conv_w, m_ffn_conv_b, m_w_down, m_final_g, v_c_ctx, v_w_ada, v_b_ada, v_norm1_g, v_w_in, v_q_norm_g, v_kv_norm_g, v_w_uq, v_w_ukv, v_conv_w, v_conv_b, v_w_attn_out, v_w_conv_out, v_w_o, v_norm2_g, v_w_up, v_ffn_conv_w, v_ffn_conv_b, v_w_down, v_final_g):
    mx, my, mc = lax.axis_index("x"), lax.axis_index("y"), lax.axis_index("c")
    chip = 2 * mx + my
    dev = 4 * mx + 2 * my + mc
    T, Tc = x.shape[1], ctx.shape[1]
    TT = T + Tc
    w_in_t, m_w_in_t, v_w_in_t = (jnp.transpose(a[0]) for a in (w_in, m_w_in, v_w_in))
    w_uq_t, m_w_uq_t, v_w_uq_t = (jnp.transpose(a[0]) for a in (w_uq, m_w_uq, v_w_uq))
    conv_sh = jnp.concatenate([conv_w[0], ffn_conv_w[0]], axis=1)
    pay1 = jnp.concatenate([jnp.pad(c, ((0, 7), (0, 0))), jnp.pad(conv_sh, ((0, 5), (0, 0)))], axis=1)
    c_send, c_recv, c_src, c_land, zero0 = _ici_start("all", [pay1], [(8, 8, 2560)], jnp.zeros((8, 128), F32),
                                                      "cond_start")
    w_in_bf = (jnp.pad(w_in_t, ((0, W_IN_SHARD_PAD - W_IN_SHARD), (0, 0))) + zero0[0, 0]).astype(BF16)
    shards = {"w_in_a": w_in_bf[:W_IN_EARLY], "w_in_b": w_in_bf[W_IN_EARLY:], "w_uq": w_uq_t, "w_ukv": w_ukv[0],
              "w_attn_out": w_attn_out[0], "w_conv_out": w_conv_out[0], "w_o": w_o[0], "w_up": w_up[0],
              "w_down": w_down[0]}
    (pay1,), (c_land,) = _ici_wait("all", c_send, c_recv, c_src, c_land, w_in_bf, "cond_wait")
    got1 = lax.dynamic_update_slice(c_land, pay1[None], (dev, 0, 0))
    c_all = got1[:, 0, :D_MODEL]
    conv_all = got1[0::2, :3, D_MODEL:]
    conv_w_full = _cols_from_shards(conv_all[:, :, :128])
    ffn_conv_w_full = _cols_from_shards(conv_all[:, :, 128:])

    cond = jnp.concatenate([c_all, c_ctx.reshape(1, D_MODEL), jnp.zeros((7, D_MODEL), F32)], axis=0)

    def f_silu(ids, v):
        return (v * _sigmoid(v),)

    (s16,) = _ew(f_silu, (1,), [(cond, _full((16, D_MODEL)))], [((16, D_MODEL), F32, _full((16, D_MODEL)), None)], "silu_cond")
    mod_sh = _mm(s16, w_ada[0], "nn", 16, 1536, D_MODEL, tm=16, tn=768, tk=D_MODEL, name="w_ada_fwd")
    m_send, m_recv, m_src, m_land, zero1 = _ici_start("all", [mod_sh], [(8, 16, 1536)], jnp.zeros((8, 128), F32),
                                                      "mod_start")
    shards["w_ukv"] = w_ukv[0] + zero1[0, 0]

    first = ["w_in_a", "w_uq", "w_ukv"]
    gathered, zero = _gather_weights([shards[n].astype(BF16) for n in first])
    full = dict(zip(first, gathered))
    (mod_mine,), (m_land,) = _ici_wait("all", m_send, m_recv, m_src, m_land, gathered[0], "mod_wait")
    got2 = lax.dynamic_update_slice(m_land, mod_mine[None], (dev, 0, 0))
    mod_all = _cols_from_shards(got2[0::2]) + b_ada
    mod_lat = lax.dynamic_slice_in_dim(mod_all, dev, 1, axis=0)
    mod_ctx = mod_all[8:9]
    xx = jnp.concatenate([x[0], ctx[0]], axis=0)
    late_groups = {"g1": ("w_in_b", "w_attn_out", "w_conv_out", "w_o"), "g2": ("w_up", "w_down")}
    flight = {}
    for tag, group in late_groups.items():
        bf = [(shards[n] + zero[0, 0]).astype(BF16) for n in group]
        flight[tag] = _ici_start("gather", bf, [(4,) + s.shape for s in bf], xx, "gather_" + tag + "_start")
        xx = flight[tag][4]

    def chip_stage_done(tag, x):
        send, recv, src, land, _ = flight[tag]
        src, land = _ici_wait("gather", send, recv, src, land, x, "gather_" + tag + "_wait")
        flight[tag] = _ici_start("finish", src, None, x, "finish_" + tag + "_start", lands=land)
        return flight[tag][4]

    def arrived(tag, x):
        send, recv, src, land, _ = flight[tag]
        return dict(zip(late_groups[tag], _ici_wait("finish", send, recv, src, land, x, "finish_" + tag + "_wait")[1]))

    def late_weights(point, x):
        if point == "before_attn":
            return {}, chip_stage_done("g1", x)
        if point == "after_attn":
            got = arrived("g1", x)
            wao = _cols_from_shards(got["w_attn_out"]).reshape(N_HEADS, 64, D_MODEL)
            w_in_all = jnp.concatenate([full["w_in_a"], got["w_in_b"]], axis=1)
            ready = {"w_in_t": _w_in_t_p_from_shards(w_in_all),
                     "w_attn_out": jnp.pad(wao, ((0, 0), (64, 0), (0, 0))).reshape(N_HEADS * HEAD_PAD, D_MODEL),
                     "w_conv_out": got["w_conv_out"], "w_o": got["w_o"].reshape(D_MODEL, D_MODEL)}
            return ready, chip_stage_done("g2", x)
        got = arrived("g2", x)
        return {"w_up": got["w_up"], "w_down": got["w_down"].reshape(D_FF, D_MODEL)}, x

    wuq_t = full["w_uq"].reshape(N_HEADS, QK_DIM, Q_RANK)
    early_rows = full["w_in_a"][0]
    zrows = lambda n: jnp.zeros((n, D_MODEL), BF16)
    W = {
        "w_in_a_t": jnp.concatenate([early_rows[0:256], zrows(PA_Q0 - 256), early_rows[288:672], zrows(64),
                                     early_rows[256:288], zrows(32)], axis=0),
        "w_uq_t": jnp.pad(wuq_t, ((0, 0), (0, HEAD_PAD - QK_DIM), (0, 0))).reshape(N_HEADS * HEAD_PAD, Q_RANK),
        "w_ukv": full["w_ukv"],
        "norm1_g": norm1_g, "norm2_g": norm2_g, "final_g": final_g.reshape(1, D_MODEL), "q_norm_g": q_norm_g,
        "kv_norm_g": kv_norm_g, "conv_w": conv_w_full, "conv_b": conv_b, "ffn_conv_w": ffn_conv_w_full,
        "ffn_conv_b": ffn_conv_b,
    }

    place = jnp.stack([chip, mc]).astype(jnp.int32)
    early = {}

    pending = {}

    def scatter(tag, group, gs, from_sib, carry):
        if tag == "mid":
            sums = _add_pair_many(gs, from_sib, place, "rs_pair_add_mid")
        else:
            sums = [_add_pair(gs[w], from_sib[w], place, "rs_pair_add_" + n) for w, n in enumerate(group)]
        send, recv, sums, land, carry = _ici_start(
            "scatter", sums, [(3,) + s.shape[1:] for s in sums], carry, "rs_chips_" + tag + "_start")
        early[tag] = (group, send, recv, sums, land)
        return carry

    def early_grads(tag, g, carry, split=False):
        gs = list(g.values())
        if not split:
            return scatter(tag, list(g), gs, _rs_pair(gs, "rs_pair_" + tag), carry)
        send, recv, gs, land, carry = _ici_start(
            "pair", gs, [(4, s.shape[1] // 2, s.shape[2]) for s in gs], carry, "rs_pair_" + tag + "_start")
        pending[tag] = (list(g), send, recv, gs, land)
        return carry

    def early_continue(tag, carry):
        group, send, recv, gs, land = pending[tag]
        gs, from_sib = _ici_wait("pair", send, recv, gs, land, carry, "rs_pair_" + tag + "_wait")
        return scatter(tag, group, gs, from_sib, carry)

    grad_x, loss_part, gbig, gsmall = _local_step(xx, loss_target[0], mod_lat, mod_ctx, W, late_weights, early_grads,
                                                  early_continue)

    gsmall["loss"] = loss_part
    pay3 = jnp.concatenate([gsmall[n].reshape(-1) for n, _ in SMALL])
    pay3 = jnp.pad(pay3, (0, SMALL_ROWS * 128 - pay3.shape[0])).reshape(SMALL_ROWS, 128)
    s_send, s_recv, s_src, s_land, w_in_thru = _ici_start("all", [pay3], [(8, SMALL_ROWS, 128)], gbig["w_in"],
                                                         "small_start")
    gbig = {"w_in": w_in_thru}

    after_small = early_grads("last", gbig, s_src[0])

    (pay3,), (s_land,) = _ici_wait("all", s_send, s_recv, [after_small], s_land, early["last"][3][0], "small_wait")
    got3 = lax.dynamic_update_slice(s_land, pay3[None], (dev, 0, 0)).reshape(8 * SMALL_ROWS, 128)

    def f_sum8(ids, a):
        s = a[0:SMALL_ROWS]
        for d in range(1, 8):
            s = s + a[d * SMALL_ROWS:(d + 1) * SMALL_ROWS]
        return (s,)

    (vsum,) = _ew(f_sum8, (1,), [(got3, _full((8 * SMALL_ROWS, 128)))],
                  [((SMALL_ROWS, 128), F32, _full((SMALL_ROWS, 128)), None)], "sum_small")
    vflat = vsum.reshape(-1)
    gvec, off = {}, 0
    for n, size in SMALL:
        gvec[n] = vflat[off:off + size]
        off += size
    loss = gvec["loss"][0]
    dmod_rows = got3.reshape(8, SMALL_ROWS * 128)[:, :6 * D_MODEL]
    dm16 = jnp.concatenate([dmod_rows, gvec["dmod_ctx"].reshape(1, -1), jnp.zeros((7, 6 * D_MODEL), F32)], axis=0)

    def f_colsum(ids, a):
        return (_colsum(a),)

    (g_b_ada,) = _ew(f_colsum, (1,), [(dm16, _full((16, 6 * D_MODEL)))],
                     [((1, 6 * D_MODEL), F32, _full((1, 6 * D_MODEL)), None)], "b_ada_grad")
    dm_sh = lax.dynamic_slice_in_dim(dm16, chip * 1536, 1536, axis=1)
    g_w_ada = _mm(s16, dm_sh, "tn", D_MODEL, 1536, 16, tm=512, tn=768, tk=16, name="w_ada_dw")
    dcond_part = _mm(dm_sh, w_ada[0], "nt", 16, D_MODEL, 1536, tm=16, tn=512, tk=1536, name="w_ada_dx")
    d_send, d_recv, d_src, d_land, vsum = _ici_start("all", [dcond_part[8:16]], [(8, 8, D_MODEL)], vsum, "dcond_start")

    def finish_start(tags, after):
        done, halves = [], []
        for tag in tags:
            tag_names, send, recv, sums, land = early[tag]
            sums, land = _ici_wait("scatter", send, recv, sums, land, after, "rs_chips_" + tag + "_wait")
            done += tag_names
            if tag == "mid":
                halves += _add_chips_many(sums, land, place, "rs_chip_add_mid")
            else:
                halves += [_add_chips(a, b, place, "rs_chip_add_" + n) for a, b, n in zip(sums, land, tag_names)]
        send, recv, _, halves, _ = _ici_start("back", [], None, jnp.zeros((8, 128), F32), "rs_back_" + tags[0] + "_start",
                                              lands=halves)
        return done, send, recv, halves

    def finish_wait(state, after):
        done, send, recv, halves = state
        return dict(zip(done, _ici_wait("back", send, recv, [], halves, after, "rs_back_" + done[0] + "_wait")[1]))

    grads, deltas, new_m, new_v = {}, {}, {}, {}

    raw = {}

    def adam(n, w_, m_, v_, g, transposed):
        d_, m2, v2 = _adamw(w_, g, m_, v_, "adamw_" + n)
        raw[n] = d_
        back = (lambda a: jnp.transpose(a)[None]) if transposed else (lambda a: a[None])
        grads[n], deltas[n], new_m[n], new_v[n] = back(g[:w_.shape[0]]), back(d_), back(m2), back(v2)

    pending_back = finish_start(["late", "mid"], grad_x)
    adam("w_ada", w_ada[0], m_w_ada[0], v_w_ada[0], g_w_ada, False)
    gw = finish_wait(pending_back, raw["w_ada"])
    for n, (w_, m_, v_) in {"w_o": (w_o, m_w_o, v_w_o), "w_up": (w_up, m_w_up, v_w_up),
                            "w_down": (w_down, m_w_down, v_w_down)}.items():
        adam(n, w_[0], m_[0], v_[0], gw[n], False)
    pending_back = finish_start(["last"], raw["w_up"])

    (dcond_mine,), (d_land,) = _ici_wait("all", d_send, d_recv, d_src, d_land, raw["w_down"], "dcond_wait")
    got4 = lax.dynamic_update_slice(d_land, dcond_mine[None], (dev, 0, 0))[0::2, 0]

    def f_c_ctx(ids, parts, cc):
        s = _sigmoid(cc)
        d = parts[0:1] + parts[1:2] + parts[2:3] + parts[3:4]
        return (d * s * (1.0 + cc * (1.0 - s)),)

    (g_c_ctx,) = _ew(f_c_ctx, (1,), [(got4, _full((4, D_MODEL))), (c_ctx.reshape(1, D_MODEL), _full((1, D_MODEL)))],
                     [((1, D_MODEL), F32, _full((1, D_MODEL)), None)], "c_ctx_grad")

    conv_w_g = lax.dynamic_slice_in_dim(gvec["conv_w"].reshape(3, CONV_DIM), chip * 128, 128, axis=1)
    ffn_conv_w_g = lax.dynamic_slice_in_dim(gvec["ffn_conv_w"].reshape(3, 2 * D_FF), chip * 1408, 1408, axis=1)
    vec_params = (("c_ctx", c_ctx, m_c_ctx, v_c_ctx, g_c_ctx), ("b_ada", b_ada, m_b_ada, v_b_ada, g_b_ada),
                  ("norm1_g", norm1_g, m_norm1_g, v_norm1_g, gvec["norm1_g"]),
                  ("q_norm_g", q_norm_g, m_q_norm_g, v_q_norm_g, gvec["q_norm_g"]),
                  ("kv_norm_g", kv_norm_g, m_kv_norm_g, v_kv_norm_g, gvec["kv_norm_g"]),
                  ("conv_w", conv_w, m_conv_w, v_conv_w, conv_w_g), ("conv_b", conv_b, m_conv_b, v_conv_b, gvec["conv_b"]),
                  ("norm2_g", norm2_g, m_norm2_g, v_norm2_g, gvec["norm2_g"]),
                  ("ffn_conv_w", ffn_conv_w, m_ffn_conv_w, v_ffn_conv_w, ffn_conv_w_g),
                  ("ffn_conv_b", ffn_conv_b, m_ffn_conv_b, v_ffn_conv_b, gvec["ffn_conv_b"]),
                  ("final_g", final_g, m_final_g, v_final_g, gvec["final_g"]))
    two_d = lambda a: a.reshape((-1, a.shape[-1]))
    many = [p + ((lambda r, s=p[1].shape: r.reshape(s)),) for p in vec_params]
    for n, w_, m_, v_ in (("w_ukv", w_ukv, m_w_ukv, v_w_ukv), ("w_attn_out", w_attn_out, m_w_attn_out, v_w_attn_out),
                          ("w_conv_out", w_conv_out, m_w_conv_out, v_w_conv_out)):
        many.append((n, w_, m_, v_, gw[n], (lambda r, s=w_.shape: r.reshape(s))))
    many.append(("w_uq", w_uq_t, m_w_uq_t, v_w_uq_t, gw["w_uq"], lambda r: jnp.transpose(r)[None]))

    def f_adam_many(ids, *vals):
        out = []
        for k in range(len(many)):
            out += _adam_update(*vals[4 * k:4 * k + 4])
        return out

    ins_v, outs_v = [], []
    for p in many:
        shp = two_d(p[1]).shape
        ins_v += [(two_d(a), _full(shp)) for a in (p[1], p[4], p[2], p[3])]
        outs_v += [(shp, F32, _full(shp), None)] * 3
    res_v = _ew(f_adam_many, (1,), ins_v, outs_v, "adamw_small")
    for k, p in enumerate(many):
        n, post = p[0], p[5]
        grads[n] = post(two_d(p[4]))
        deltas[n], new_m[n], new_v[n] = (post(r) for r in res_v[3 * k:3 * k + 3])

    gw_in = finish_wait(pending_back, res_v[0])
    adam("w_in", w_in_t, m_w_in_t, v_w_in_t, gw_in["w_in"], True)

    order = ("c_ctx", "w_ada", "b_ada", "norm1_g", "w_in", "q_norm_g", "kv_norm_g", "w_uq", "w_ukv", "conv_w", "conv_b",
             "w_attn_out", "w_conv_out", "w_o", "norm2_g", "w_up", "ffn_conv_w", "ffn_conv_b", "w_down", "final_g")
    return (loss, grad_x[None], *[grads[n] for n in order], *[deltas[n] for n in order],
            *[new_m[n] for n in order], *[new_v[n] for n in order])
```

```python
import functools

import jax
import jax.numpy as jnp
import numpy as np
from jax import lax
from jax.experimental import pallas as pl
from jax.experimental.pallas import tpu as pltpu

F32, BF16 = jnp.float32, jnp.bfloat16
MESH = pl.DeviceIdType.MESH

D_MODEL = 1024
N_HEADS = 8
HEAD_PAD = 128
QK_DIM = 96
Q_RANK, KV_RANK = 384, 256
CONV_DIM = 512
D_FF = 2816
GRID_W = 64
ROPE_THETA = 10000.0
EPS = 1e-6
GA0, GC0, CX0, CB0, CC0, KV0, Q0, KR0, P_COLS = 0, 1024, 2048, 2560, 3072, 3584, 3840, 4224, 4352
PA_KV0, PA_Q0, PA_KR0, PA_COLS = 0, 384, 768, 896
ROW_TILE = 256
VMEM_LIMIT_BYTES = 48 * 1024 * 1024

ADAM_LR, ADAM_B1, ADAM_B2, ADAM_EPS, ADAM_WD, ADAM_STEP = 0.001, 0.9, 0.999, 1e-08, 0.01, 10

NN = (((1,), (0,)), ((), ()))
NT = (((1,), (1,)), ((), ()))
TN = (((0,), (0,)), ((), ()))


def _cp(sem):
    return pltpu.CompilerParams(dimension_semantics=sem, vmem_limit_bytes=VMEM_LIMIT_BYTES)


PIN_BYTES = 1 << 19


def _in_hbm(arrays):
    return [pltpu.with_memory_space_constraint(a, pltpu.HBM) if a.size * a.dtype.itemsize >= PIN_BYTES else a
            for a in arrays]


def _out(shape, dtype):
    n = 1
    for d in shape:
        n *= d
    big = n * jnp.dtype(dtype).itemsize >= PIN_BYTES
    return pltpu.HBM(shape, dtype) if big else jax.ShapeDtypeStruct(shape, dtype)


def _pick(n, prefs):
    for p in prefs:
        if n % p == 0:
            return p
    return n


def _mm(a, b, mode, M, N, K, *, tm, tn, tk, name, out_dtype=F32, a_spec=None, b_spec=None, o_spec=None,
        out_shape=None, transpose_out=False):
    assert M % tm == 0 and N % tn == 0 and K % tk == 0, (name, M, N, K, tm, tn, tk)
    nk = K // tk
    dims = {"nn": NN, "nt": NT, "tn": TN}[mode]
    if a_spec is None:
        a_spec = (pl.BlockSpec((tk, tm), lambda i, j, k: (k, i)) if mode == "tn"
                  else pl.BlockSpec((tm, tk), lambda i, j, k: (i, k)))
    if b_spec is None:
        b_spec = (pl.BlockSpec((tn, tk), lambda i, j, k: (j, k)) if mode == "nt"
                  else pl.BlockSpec((tk, tn), lambda i, j, k: (k, j)))
    if o_spec is None:
        o_spec = (pl.BlockSpec((tn, tm), lambda i, j, k: (j, i)) if transpose_out
                  else pl.BlockSpec((tm, tn), lambda i, j, k: (i, j)))
    if out_shape is None:
        out_shape = (N, M) if transpose_out else (M, N)

    def emit(o_ref, val):
        o_ref[...] = (val.T if transpose_out else val).astype(o_ref.dtype)

    def body(a_ref, b_ref, o_ref, *scratch):
        part = lax.dot_general(a_ref[...].astype(BF16), b_ref[...].astype(BF16), dims, preferred_element_type=F32)
        if nk == 1:
            emit(o_ref, part)
            return
        acc_ref, = scratch
        k = pl.program_id(2)

        @pl.when(k == 0)
        def _():
            acc_ref[...] = part

        @pl.when((k > 0) & (k < nk - 1))
        def _():
            acc_ref[...] += part

        @pl.when(k == nk - 1)
        def _():
            emit(o_ref, acc_ref[...] + part)

    return pl.pallas_call(
        body, grid=(M // tm, N // tn, nk), in_specs=[a_spec, b_spec], out_specs=o_spec,
        out_shape=_out(out_shape, out_dtype),
        scratch_shapes=[pltpu.VMEM((tm, tn), F32)] if nk > 1 else [],
        compiler_params=_cp(("parallel", "parallel", "arbitrary")), name=name)(*_in_hbm([a, b]))


def _ew(fn, grid, ins, outs, name, scalars=None):
    n_in = len(ins)
    n_sc = 0 if scalars is None else 1

    def store(ref, val, acc, ids):
        if isinstance(val, (list, tuple)):
            for h, v in enumerate(val):
                ref[h] = v.astype(ref.dtype)
            return
        if acc is None:
            ref[...] = val.astype(ref.dtype)
            return

        @pl.when(ids[acc] == 0)
        def _():
            ref[...] = val.astype(ref.dtype)

        @pl.when(ids[acc] > 0)
        def _():
            ref[...] += val.astype(ref.dtype)

    def body(*refs):
        refs = refs[n_sc:]
        ids = tuple(pl.program_id(a) for a in range(len(grid)))
        vals = fn(ids, *[r[...] for r in refs[:n_in]])
        for ref, val, (_, _, _, acc) in zip(refs[n_in:], vals, outs):
            store(ref, val, acc, ids)

    acc_axes = {o[3] for o in outs if o[3] is not None}
    sem = tuple("arbitrary" if a in acc_axes else "parallel" for a in range(len(grid)))
    in_specs, out_specs = [s for _, s in ins], [o[2] for o in outs]
    out_shape = [_out(o[0], o[1]) for o in outs]
    args = _in_hbm([a for a, _ in ins])
    if scalars is None:
        return pl.pallas_call(body, grid=grid, in_specs=in_specs, out_specs=out_specs, out_shape=out_shape,
                              compiler_params=_cp(sem), name=name)(*args)
    spec = pltpu.PrefetchScalarGridSpec(num_scalar_prefetch=1, grid=grid, in_specs=in_specs, out_specs=out_specs)
    return pl.pallas_call(body, grid_spec=spec, out_shape=out_shape, compiler_params=_cp(sem), name=name)(scalars, *args)


def _rows(width, cblk=0, roff=0, tr=ROW_TILE):
    return pl.BlockSpec((tr, width), lambda i: (i + roff, cblk))


def _full(shape):
    nd = len(shape)
    return pl.BlockSpec(shape, lambda *_: (0,) * nd)


def _sigmoid(x):
    return 1.0 / (1.0 + jnp.exp2(x * (-1.4426950408889634)))


def _rms(x):
    return lax.rsqrt(jnp.mean(x * x, axis=-1, keepdims=True) + EPS)


def _rms_bwd(dn, xn, r):
    return r * (dn - xn * jnp.mean(dn * xn, axis=-1, keepdims=True))


def _colsum(x):
    return jnp.sum(x, axis=0, keepdims=True)


def _shifts(x):
    n = x.shape[0]
    rows = lax.broadcasted_iota(jnp.int32, x.shape, 0)
    return jnp.where(rows == 0, 0.0, pltpu.roll(x, 1, 0)), jnp.where(rows == n - 1, 0.0, pltpu.roll(x, n - 1, 0))


def _conv(x, w, b, shifted=None):
    prev, nxt = _shifts(x) if shifted is None else shifted
    return b + prev * w[0:1] + x * w[1:2] + nxt * w[2:3]


def _conv_bwd_x(dy, w):
    prev, nxt = _shifts(dy)
    return nxt * w[0:1] + dy * w[1:2] + prev * w[2:3]


def _conv_bwd_w(dy, x, shifted):
    prev, nxt = shifted
    return _colsum(dy * prev), _colsum(dy * x), _colsum(dy * nxt)


def _rope(x, cos, sin_lo, sin_hi):
    return x * cos + pltpu.roll(x, HEAD_PAD - 8, 1) * sin_lo + pltpu.roll(x, 8, 1) * sin_hi


ATTN_SCALE = QK_DIM ** -0.5
LOG2_E = 1.4426950408889634


def _rope_t(x, tab, inverse=False):
    o = 3 * HEAD_PAD if inverse else 0
    return _rope(x, tab[:, o:o + HEAD_PAD], tab[:, o + HEAD_PAD:o + 2 * HEAD_PAD], tab[:, o + 2 * HEAD_PAD:o + 3 * HEAD_PAD])


def _heads_keys(hp, kv_ref, kr_ref, tab_ref, kc_ref, vp_ref):
    kr_roped = _rope_t(kr_ref[...], tab_ref[...])
    lane = lax.broadcasted_iota(jnp.int32, kr_roped.shape, 1)
    for u in range(hp):
        kv = kv_ref[:, u * HEAD_PAD:(u + 1) * HEAD_PAD]
        kc_ref[u] = jnp.where(lane < 64, kv, kr_roped).astype(BF16)
        vp_ref[u] = jnp.where(lane >= 64, kv, 0.0).astype(BF16)


ATTN_Q_TILE = 512
ATTN_HEADS_PER_STEP = 2


def _attn_specs(tq, TT):
    q = pl.BlockSpec((tq, HEAD_PAD), lambda h, i: (i, h))
    keys = pl.BlockSpec((TT, HEAD_PAD), lambda h, i: (0, h))
    kr = pl.BlockSpec((TT, HEAD_PAD), lambda h, i: (0, PA_KR0 // HEAD_PAD))
    tab_q = pl.BlockSpec((tq, 6 * HEAD_PAD), lambda h, i: (i, 0))
    tab_k = pl.BlockSpec((TT, 6 * HEAD_PAD), lambda h, i: (0, 0))
    return q, keys, kr, tab_q, tab_k


def _attn_fwd(q_raw, kv, pp, tab, T, TT):
    tq, hp = ROW_TILE, 2 * ATTN_HEADS_PER_STEP
    w = hp * HEAD_PAD

    def body(q_ref, kv_ref, kr_ref, tq_ref, tk_ref, o_ref, kc, vp):
        @pl.when(pl.program_id(1) == 0)
        def _():
            _heads_keys(hp, kv_ref, kr_ref, tk_ref, kc, vp)

        tab = tq_ref[...]
        for u in range(hp):
            cols = slice(u * HEAD_PAD, (u + 1) * HEAD_PAD)
            q = _rope_t(q_ref[:, cols], tab).astype(BF16)
            s = lax.dot_general(q, kc[u], NT, preferred_element_type=F32)
            m = jnp.max(s, axis=-1, keepdims=True)
            p = jnp.exp2((s - m) * (ATTN_SCALE * LOG2_E))
            l = jnp.sum(p, axis=-1, keepdims=True)
            o = lax.dot_general(p.astype(BF16), vp[u], NN, preferred_element_type=F32)
            lane = lax.broadcasted_iota(jnp.int32, o.shape, 1)
            o_ref[:, cols] = jnp.where(lane < 64, m * ATTN_SCALE + jnp.log(l), o / l)

    _, _, kr, _, _ = _attn_specs(tq, TT)
    qs = pl.BlockSpec((tq, w), lambda h, i: (i, h))
    keys = pl.BlockSpec((TT, w), lambda h, i: (0, h))
    tab_q = pl.BlockSpec((tq, 3 * HEAD_PAD), lambda h, i: (i, 0))
    tab_k = pl.BlockSpec((TT, 3 * HEAD_PAD), lambda h, i: (0, 0))
    return pl.pallas_call(
        body, grid=(N_HEADS // hp, T // tq), in_specs=[qs, keys, kr, tab_q, tab_k], out_specs=qs,
        out_shape=jax.ShapeDtypeStruct((T, N_HEADS * HEAD_PAD), F32),
        scratch_shapes=[pltpu.VMEM((hp, TT, HEAD_PAD), BF16), pltpu.VMEM((hp, TT, HEAD_PAD), BF16)],
        compiler_params=_cp(("parallel", "arbitrary")), name="attn_fwd",
    )(*_in_hbm([q_raw, kv, pp, tab, tab]))


def _attn_bwd(q_raw, kv, pp, o, do, tab, T, TT):
    tq = _pick(T, (ATTN_Q_TILE, ROW_TILE))
    nq = T // tq
    hp = ATTN_HEADS_PER_STEP
    w = hp * HEAD_PAD

    def body(q_ref, kv_ref, kr_ref, tq_ref, tk_ref, o_ref, do_ref, dq_ref, dkv_ref, dkr_ref, kc, vp, dk, dv):
        g, i = pl.program_id(0), pl.program_id(1)

        @pl.when(i == 0)
        def _():
            _heads_keys(hp, kv_ref, kr_ref, tk_ref, kc, vp)
            dk[...] = jnp.zeros_like(dk)
            dv[...] = jnp.zeros_like(dv)

        tab = tq_ref[...]
        for u in range(hp):
            cols = slice(u * HEAD_PAD, (u + 1) * HEAD_PAD)
            q = _rope_t(q_ref[:, cols], tab).astype(BF16)
            k, v, d_o = kc[u], vp[u], do_ref[:, cols]
            s = lax.dot_general(q, k, NT, preferred_element_type=F32)
            o = o_ref[:, cols]
            p = jnp.exp2(s * (ATTN_SCALE * LOG2_E) - o[:, 0:1] * LOG2_E)
            dob = d_o.astype(BF16)
            dp = lax.dot_general(dob, v, NT, preferred_element_type=F32)
            dd = jnp.sum(d_o * o, axis=-1, keepdims=True)
            ds = (p * (dp - dd) * ATTN_SCALE).astype(BF16)
            dq = lax.dot_general(ds, k, NN, preferred_element_type=F32)
            dq_ref[:, cols] = _rope_t(dq, tab, inverse=True).astype(dq_ref.dtype)
            dk[u] += lax.dot_general(q, ds, TN, preferred_element_type=F32)
            dv[u] += lax.dot_general(dob, p.astype(BF16), TN, preferred_element_type=F32)

        @pl.when(i == nq - 1)
        def _():
            rot = None
            for u in range(hp):
                dkh = dk[u].T
                lane = lax.broadcasted_iota(jnp.int32, dkh.shape, 1)
                dkv_ref[:, u * HEAD_PAD:(u + 1) * HEAD_PAD] = jnp.where(lane < 64, dkh, dv[u].T).astype(dkv_ref.dtype)
                part = jnp.where((lane >= 64) & (lane < 96), dkh, 0.0)
                rot = part if rot is None else rot + part
            rot = _rope_t(rot, tk_ref[...], inverse=True)

            @pl.when(g == 0)
            def _():
                dkr_ref[...] = rot

            @pl.when(g > 0)
            def _():
                dkr_ref[...] += rot

    _, _, kr, tab_q, tab_k = _attn_specs(tq, TT)
    qs = pl.BlockSpec((tq, w), lambda h, i: (i, h))
    keys = pl.BlockSpec((TT, w), lambda h, i: (0, h))
    wide = lambda rows: jax.ShapeDtypeStruct((rows, N_HEADS * HEAD_PAD), BF16)
    return pl.pallas_call(
        body, grid=(N_HEADS // hp, nq),
        in_specs=[qs, keys, kr, tab_q, tab_k, qs, qs],
        out_specs=[qs, keys, pl.BlockSpec((TT, HEAD_PAD), lambda h, i: (0, 0))],
        out_shape=[wide(T), wide(TT), jax.ShapeDtypeStruct((TT, HEAD_PAD), F32)],
        scratch_shapes=[pltpu.VMEM((hp, TT, HEAD_PAD), BF16), pltpu.VMEM((hp, TT, HEAD_PAD), BF16),
                        pltpu.VMEM((hp, HEAD_PAD, TT), F32), pltpu.VMEM((hp, HEAD_PAD, TT), F32)],
        compiler_params=_cp(("arbitrary", "arbitrary")), name="attn_bwd",
    )(*_in_hbm([q_raw, kv, pp, tab, tab, o, do]))


def _hbm_specs(n):
    return [pl.BlockSpec(memory_space=pl.ANY)] * n


def _gather_weights(shards):
    n = len(shards)
    halves = [s.shape[0] // 2 for s in shards]

    def body(*refs):
        ins, outs = refs[:n], refs[n:2 * n]
        token, send_sems, recv_sems = refs[2 * n:]
        token[...] = jnp.zeros_like(token)
        mx, my, mc = lax.axis_index("x"), lax.axis_index("y"), lax.axis_index("c")
        j_me = 2 * mx + my
        chips = [(1 - mx, my), (mx, 1 - my), (1 - mx, 1 - my)]

        def half(w, chip_idx, hc):
            return outs[w].at[chip_idx, pl.ds(hc * halves[w], halves[w]), :]

        def copy(w, k, src, dst, to):
            return pltpu.make_async_remote_copy(src_ref=src, dst_ref=dst, send_sem=send_sems.at[w, k],
                                                recv_sem=recv_sems.at[w, k], device_id=to, device_id_type=MESH)

        sends = []
        for w in range(n):
            cp = copy(w, 6, ins[w], outs[w].at[j_me], (mx, my, 1 - mc))
            cp.start()
            sends.append(cp)
        for k, (px, py) in enumerate(chips):
            for w in range(n):
                cp = copy(w, k, ins[w].at[pl.ds(mc * halves[w], halves[w]), :], half(w, j_me, mc), (px, py, mc))
                cp.start()
                sends.append(cp)
        for k, (px, py) in enumerate(chips):
            for w in range(n):
                got = half(w, 2 * px + py, mc)
                copy(w, k, got, got, (px, py, mc)).wait_recv()
                cp = copy(w, 3 + k, got, got, (mx, my, 1 - mc))
                cp.start()
                sends.append(cp)
        for k, (px, py) in enumerate(chips):
            for w in range(n):
                got = half(w, 2 * px + py, 1 - mc)
                copy(w, 3 + k, got, got, (mx, my, 1 - mc)).wait_recv()
        for w in range(n):
            own = outs[w].at[j_me]
            copy(w, 6, own, own, (mx, my, 1 - mc)).wait_recv()
        for cp in sends:
            cp.wait_send()

    res = pl.pallas_call(
        body, out_shape=[jax.ShapeDtypeStruct((4,) + s.shape, s.dtype) for s in shards]
        + [jax.ShapeDtypeStruct((8, 128), F32)],
        in_specs=_hbm_specs(n), out_specs=_hbm_specs(n) + [pl.BlockSpec(memory_space=pltpu.VMEM)],
        scratch_shapes=[pltpu.SemaphoreType.DMA((n, 7)), pltpu.SemaphoreType.DMA((n, 7))],
        name="gather_weights")(*shards)
    return list(res[:n]), res[n]


def _rs_pair(gs, name):
    n = len(gs)
    halves = [g.shape[1] // 2 for g in gs]

    def body(*refs):
        ins, lands = refs[:n], refs[n:2 * n]
        send_sems, recv_sems = refs[2 * n:]
        mx, my, mc = lax.axis_index("x"), lax.axis_index("y"), lax.axis_index("c")
        copies = []
        for w in range(n):
            h = halves[w]
            cp = pltpu.make_async_remote_copy(
                src_ref=ins[w].at[:, pl.ds((1 - mc) * h, h), :], dst_ref=lands[w], send_sem=send_sems.at[w],
                recv_sem=recv_sems.at[w], device_id=(mx, my, 1 - mc), device_id_type=MESH)
            cp.start()
            copies.append(cp)
        for cp in copies:
            cp.wait()

    return pl.pallas_call(
        body, out_shape=[jax.ShapeDtypeStruct((4, h, g.shape[2]), g.dtype) for g, h in zip(gs, halves)],
        in_specs=_hbm_specs(n), out_specs=_hbm_specs(n),
        scratch_shapes=[pltpu.SemaphoreType.DMA((n,)), pltpu.SemaphoreType.DMA((n,))], name=name)(*gs)


def _rs_chips(parts):
    n = len(parts)

    def body(*refs):
        ins, lands = refs[:n], refs[n:2 * n]
        send_sems, recv_sems = refs[2 * n:]
        mx, my, mc = lax.axis_index("x"), lax.axis_index("y"), lax.axis_index("c")
        copies = []
        for k, (px, py) in enumerate([(1 - mx, my), (mx, 1 - my), (1 - mx, 1 - my)]):
            for w in range(n):
                cp = pltpu.make_async_remote_copy(
                    src_ref=ins[w].at[2 * px + py], dst_ref=lands[w].at[k], send_sem=send_sems.at[w, k],
                    recv_sem=recv_sems.at[w, k], device_id=(px, py, mc), device_id_type=MESH)
                cp.start()
                copies.append(cp)
        for cp in copies:
            cp.wait()

    return list(pl.pallas_call(
        body, out_shape=[jax.ShapeDtypeStruct((3,) + p.shape[1:], p.dtype) for p in parts],
        in_specs=_hbm_specs(n), out_specs=_hbm_specs(n),
        scratch_shapes=[pltpu.SemaphoreType.DMA((n, 3)), pltpu.SemaphoreType.DMA((n, 3))], name="rs_chips")(*parts))


_HBM = pl.BlockSpec(memory_space=pltpu.HBM)
_SEM = pl.BlockSpec(memory_space=pltpu.SEMAPHORE)
_EFFECT = pltpu.SideEffectType.DATAFLOW_SIDE_EFFECTING


def _ici_copies(kind, srcs, lands, send_sems, recv_sems):
    n = len(lands)
    mx, my, mc = lax.axis_index("x"), lax.axis_index("y"), lax.axis_index("c")
    j_me = 2 * mx + my
    copies = []
    if kind == "back":
        for w in range(n):
            h = lands[w].shape[0] // 2
            mine = lands[w].at[pl.ds(mc * h, h), :]
            copies.append(pltpu.make_async_remote_copy(
                src_ref=mine, dst_ref=mine, send_sem=send_sems.at[w], recv_sem=recv_sems.at[w],
                device_id=(mx, my, 1 - mc), device_id_type=MESH))
        return copies
    if kind == "all":
        for k in range(7):
            a, b, c = (k + 1) >> 2 & 1, (k + 1) >> 1 & 1, (k + 1) & 1
            peer = (1 - mx if a else mx, 1 - my if b else my, 1 - mc if c else mc)
            for w in range(n):
                copies.append(pltpu.make_async_remote_copy(
                    src_ref=srcs[w], dst_ref=lands[w].at[4 * mx + 2 * my + mc], send_sem=send_sems.at[7 * w + k],
                    recv_sem=recv_sems.at[7 * w + k], device_id=peer, device_id_type=MESH))
        return copies
    if kind == "pair":
        for w in range(n):
            h = srcs[w].shape[1] // 2
            copies.append(pltpu.make_async_remote_copy(
                src_ref=srcs[w].at[:, pl.ds((1 - mc) * h, h), :], dst_ref=lands[w], send_sem=send_sems.at[w],
                recv_sem=recv_sems.at[w], device_id=(mx, my, 1 - mc), device_id_type=MESH))
        return copies
    chips = [(1 - mx, my), (mx, 1 - my), (1 - mx, 1 - my)]
    if kind == "finish":
        for w in range(n):
            h = srcs[w].shape[0] // 2
            pushes = [(lands[w].at[2 * px + py, pl.ds(mc * h, h), :],) * 2 for px, py in chips]
            pushes.append((srcs[w], lands[w].at[j_me]))
            for k, (src, dst) in enumerate(pushes):
                copies.append(pltpu.make_async_remote_copy(
                    src_ref=src, dst_ref=dst, send_sem=send_sems.at[4 * w + k], recv_sem=recv_sems.at[4 * w + k],
                    device_id=(mx, my, 1 - mc), device_id_type=MESH))
        return copies
    for k, (px, py) in enumerate(chips):
        for w in range(n):
            if kind == "gather":
                h = srcs[w].shape[0] // 2
                src, dst = srcs[w].at[pl.ds(mc * h, h), :], lands[w].at[j_me, pl.ds(mc * h, h), :]
            else:
                src, dst = srcs[w].at[2 * px + py], lands[w].at[k]
            copies.append(pltpu.make_async_remote_copy(
                src_ref=src, dst_ref=dst, send_sem=send_sems.at[3 * w + k], recv_sem=recv_sems.at[3 * w + k],
                device_id=(px, py, mc), device_id_type=MESH))
    return copies


_SEMS_PER_OPERAND = {"gather": 3, "scatter": 3, "all": 7, "pair": 1, "finish": 4, "back": 1}


def _ici_start(kind, srcs, land_shapes, carry, name, lands=None):
    hbm = lambda a: pltpu.with_memory_space_constraint(a, pltpu.HBM)
    if lands is None:
        lands = [lax.empty(s, srcs[0].dtype) for s in land_shapes]
    ns, nl = len(srcs), len(lands)

    def body(*refs):
        send_sems, recv_sems = refs[ns + nl + 1], refs[ns + nl + 2]
        for cp in _ici_copies(kind, refs[:ns], refs[ns:ns + nl], send_sems, recv_sems):
            cp.start()

    args = [hbm(a) for a in list(srcs) + list(lands) + [carry]]
    n_sem = _SEMS_PER_OPERAND[kind] * nl
    out_shape = ([pltpu.SemaphoreType.DMA((n_sem,)), pltpu.SemaphoreType.DMA((n_sem,))]
                 + [pltpu.HBM(a.shape, a.dtype) for a in args])
    res = pl.pallas_call(
        body, name=name, out_shape=out_shape, in_specs=[_HBM] * len(args), out_specs=[_SEM, _SEM] + [_HBM] * len(args),
        input_output_aliases={i: 2 + i for i in range(len(args))},
        compiler_params=pltpu.CompilerParams(has_side_effects=_EFFECT))(*args)
    return res[0], res[1], list(res[2:2 + ns]), list(res[2 + ns:2 + ns + nl]), res[2 + ns + nl]


def _ici_wait(kind, send_sems, recv_sems, srcs, lands, after, name):
    ns, nl = len(srcs), len(lands)

    def body(*refs):
        for cp in _ici_copies(kind, refs[:ns], refs[ns:ns + nl], refs[ns + nl], refs[ns + nl + 1]):
            cp.wait_send()
            cp.wait_recv()

    args = list(srcs) + list(lands)
    res = pl.pallas_call(
        body, name=name, out_shape=[pltpu.HBM(a.shape, a.dtype) for a in args],
        in_specs=[_HBM] * len(args) + [_SEM, _SEM, pl.BlockSpec(memory_space=pl.ANY)], out_specs=[_HBM] * len(args),
        input_output_aliases={i: i for i in range(len(args))},
        compiler_params=pltpu.CompilerParams(has_side_effects=_EFFECT))(*args, send_sems, recv_sems, after)
    return list(res[:ns]), list(res[ns:])


def _tile_rows(h, c, itemsize, mult):
    best = h
    for t in range(mult, h + 1, mult):
        if h % t == 0 and t * c * itemsize <= (1 << 21):
            best = t
    return best


def _add_pair(g, land, place, name):
    _, h, c = land.shape
    t = _tile_rows(h, c, 2, 16)
    nb = h // t
    return _ew(lambda ids, u, v: (u.astype(F32) + v.astype(F32),), (4, nb),
               [(g, pl.BlockSpec((None, t, c), lambda j, i, s: (j, s[1] * nb + i, 0))),
                (land, pl.BlockSpec((None, t, c), lambda j, i, s: (j, i, 0)))],
               [(land.shape, BF16, pl.BlockSpec((None, t, c), lambda j, i, s: (j, i, 0)), None)], name, scalars=place)[0]


def _add_pair_many(gs, lands, place, name):
    ins, outs = [], []
    for g, l in zip(gs, lands):
        ins += [(g, pl.BlockSpec(l.shape, lambda i, s: (0, s[1], 0))), (l, pl.BlockSpec(l.shape, lambda i, s: (0, 0, 0)))]
        outs.append((l.shape, BF16, pl.BlockSpec(l.shape, lambda i, s: (0, 0, 0)), None))
    fn = lambda ids, *v: [v[2 * k].astype(F32) + v[2 * k + 1].astype(F32) for k in range(len(gs))]
    return list(_ew(fn, (1,), ins, outs, name, scalars=place))


def _add_chips_many(owns, lands, place, name):
    ins, outs = [], []
    for own, land in zip(owns, lands):
        _, h, c = land.shape
        ins += [(own, pl.BlockSpec((None, h, c), lambda i, s: (s[0], 0, 0))),
                (land, pl.BlockSpec((3, h, c), lambda i, s: (0, 0, 0)))]
        outs.append(((2 * h, c), F32, pl.BlockSpec((h, c), lambda i, s: (s[1], 0)), None))

    def fn(ids, *v):
        return [((v[2 * k].astype(F32) + v[2 * k + 1][0].astype(F32)) + v[2 * k + 1][1].astype(F32))
                + v[2 * k + 1][2].astype(F32) for k in range(len(owns))]

    return list(_ew(fn, (1,), ins, outs, name, scalars=place))


def _add_chips(own, land, place, name):
    _, h, c = land.shape
    t = _tile_rows(h, c, 4, 16)
    nb = h // t

    def fn(ids, a, b):
        return (((a.astype(F32) + b[0].astype(F32)) + b[1].astype(F32)) + b[2].astype(F32),)

    return _ew(fn, (nb,), [(own, pl.BlockSpec((None, t, c), lambda i, s: (s[0], i, 0))),
                           (land, pl.BlockSpec((3, t, c), lambda i, s: (0, i, 0)))],
               [((2 * h, c), F32, pl.BlockSpec((t, c), lambda i, s: (s[1] * nb + i, 0)), None)], name, scalars=place)[0]


W_IN_SEGMENTS = ((0, 256, KV0), (256, 288, KR0 + 64), (288, 672, Q0), (672, 1184, CX0), (1184, 1696, CB0),
                 (1696, 2208, CC0), (2208, 3232, GA0), (3232, 4256, GC0))
W_IN_SHARD = 1064


W_IN_SHARD_PAD = 1088
W_IN_EARLY = 672


def _w_in_t_p_from_shards(s):
    pieces = []
    for o0, o1, p0 in sorted(W_IN_SEGMENTS, key=lambda t: t[2]):
        if p0 == KR0 + 64:
            pieces.append(jnp.zeros((64, s.shape[2]), s.dtype))
        for j in range(4):
            lo, hi = max(o0, j * W_IN_SHARD), min(o1, (j + 1) * W_IN_SHARD)
            if lo < hi:
                pieces.append(s[j, lo - j * W_IN_SHARD:hi - j * W_IN_SHARD])
    pieces.append(jnp.zeros((32, s.shape[2]), s.dtype))
    return jnp.concatenate(pieces, axis=0)


def _w_in_t_shards_from_p(g):
    shards = []
    for j in range(4):
        pieces = []
        for o0, o1, p0 in W_IN_SEGMENTS:
            lo, hi = max(o0, j * W_IN_SHARD), min(o1, (j + 1) * W_IN_SHARD)
            if lo < hi:
                pieces.append(g[p0 + lo - o0:p0 + hi - o0])
        pieces.append(jnp.zeros((W_IN_SHARD_PAD - W_IN_SHARD, g.shape[1]), g.dtype))
        shards.append(jnp.concatenate(pieces, axis=0))
    return jnp.stack(shards, axis=0)


def _cols_from_shards(s):
    return jnp.transpose(s, (1, 0, 2)).reshape(s.shape[1], -1)


def _rope_tables(T, TT, inverse):
    f32 = np.float32
    rows = T // GRID_W
    row = np.repeat(np.arange(rows), GRID_W).astype(f32)
    col = np.tile(np.arange(GRID_W), rows).astype(f32)
    inv = (f32(ROPE_THETA) ** (-np.arange(0, 16, 2, dtype=f32) / f32(16))).astype(f32)
    ang = np.concatenate([row[:, None] * inv, col[:, None] * inv], axis=-1).astype(f32)
    cos, sin = np.cos(ang).astype(f32), np.sin(ang).astype(f32)
    lane = np.arange(32)
    src = (lane // 16) * 8 + lane % 8
    lo = ((lane % 16) // 8 == 0).astype(f32)
    sgn = f32(-1.0 if inverse else 1.0)
    cos32 = cos[:, src]
    sin_lo32 = -sgn * sin[:, src] * lo
    sin_hi32 = sgn * sin[:, src] * (1 - lo)

    def widen(t32, fill):
        t = np.concatenate([np.full((T, 64), fill, f32), t32, np.full((T, 32), fill, f32)], axis=1)
        return np.concatenate([t, np.full((TT - T, HEAD_PAD), fill, f32)], axis=0)

    return [widen(cos32, 1.0), widen(sin_lo32, 0.0), widen(sin_hi32, 0.0)]


def _rope_table(T, TT):
    return jnp.asarray(np.concatenate(_rope_tables(T, TT, False) + _rope_tables(T, TT, True), axis=1))


def _local_step(xx, tgt, mod_lat, mod_ctx, W, late_weights, early_grads, early_continue):
    TT = xx.shape[0]
    T = tgt.shape[0]
    n_lat, n_all = T // ROW_TILE, TT // ROW_TILE
    sh1, sc1, g1, sh2, sc2, g2 = [mod_lat[:, k * D_MODEL:(k + 1) * D_MODEL] for k in range(6)]
    csh1, csc1 = mod_ctx[:, :D_MODEL], mod_ctx[:, D_MODEL:2 * D_MODEL]
    vec = lambda n: _full((1, n))
    row_out = lambda n, dt, rows=T: ((rows, n), dt, _rows(n), None)
    acc_out = lambda n: ((1, n), F32, _full((1, n)), 0)
    lt = _pick(T, (2 * ROW_TILE, ROW_TILE))
    n_lt = T // lt
    lrows = lambda n, cblk=0: _rows(n, cblk, 0, lt)
    lrow_out = lambda n, dt: ((T, n), dt, lrows(n), None)

    def f_norm1(ids, x, g, a_sh, a_sc, b_sh, b_sc):
        ctx = ids[0] >= n_lat
        sh, sc = jnp.where(ctx, b_sh, a_sh), jnp.where(ctx, b_sc, a_sc)
        return ((x * _rms(x) * g) * (1.0 + sc) + sh,)

    (hh,) = _ew(f_norm1, (n_all,), [(xx, _rows(D_MODEL)), (W["norm1_g"], vec(D_MODEL)), (sh1, vec(D_MODEL)),
                                   (sc1, vec(D_MODEL)), (csh1, vec(D_MODEL)), (csc1, vec(D_MODEL))],
                [row_out(D_MODEL, BF16, TT)], "norm1_fwd")
    tm_all = _pick(TT, (768, 256))
    pp_a = _mm(hh, W["w_in_a_t"], "nt", TT, PA_COLS, D_MODEL, tm=tm_all, tn=PA_COLS, tk=D_MODEL, name="w_in_fwd_a")

    def f_lowrank(ids, ckv, cq, gkv, gq):
        return ckv * _rms(ckv) * gkv, cq * _rms(cq) * gq

    nkv, nq = _ew(f_lowrank, (n_all,), [(pp_a, _rows(KV_RANK, PA_KV0 // KV_RANK)), (pp_a, _rows(Q_RANK, PA_Q0 // Q_RANK)),
                                       (W["kv_norm_g"], vec(KV_RANK)), (W["q_norm_g"], vec(Q_RANK))],
                  [row_out(KV_RANK, BF16, TT), row_out(Q_RANK, BF16, TT)], "lowrank_norm_fwd")
    kv = _mm(nkv, W["w_ukv"], "nn", TT, 1024, KV_RANK, tm=tm_all, tn=256, tk=KV_RANK, name="w_ukv_fwd",
             b_spec=pl.BlockSpec((None, KV_RANK, 256), lambda i, j, k: (j, k, 0)))
    q_raw = _mm(nq, W["w_uq_t"], "nt", TT, 1024, Q_RANK, tm=tm_all, tn=1024, tk=Q_RANK, name="w_uq_fwd")

    tab = _rope_table(T, TT)
    _, q_raw = late_weights("before_attn", q_raw)
    o_pad = _attn_fwd(q_raw, kv, pp_a, tab, T, TT)
    arrived, o_pad = late_weights("after_attn", o_pad)
    W = dict(W, **arrived)
    tm_lat = _pick(T, (1024, 512, 256))
    pp = _mm(hh, W["w_in_t"], "nt", T, KV0, D_MODEL, tm=tm_lat, tn=KV0 // 2, tk=D_MODEL, name="w_in_fwd_b")
    ya = _mm(o_pad, W["w_attn_out"], "nn", T, D_MODEL, 1024, tm=tm_lat, tn=D_MODEL, tk=1024, name="w_attn_out_fwd",
             out_dtype=BF16)

    tc = 256
    colT = lambda blk0: pl.BlockSpec((T, tc), lambda j: (0, blk0 + j))

    def f_conv(ids, xin, cb, cc, w, b):
        return (cb * _conv(cc * xin, w, b),)

    (e,) = _ew(f_conv, (CONV_DIM // tc,),
               [(pp, colT(CX0 // tc)), (pp, colT(CB0 // tc)), (pp, colT(CC0 // tc)),
                (W["conv_w"], pl.BlockSpec((3, tc), lambda j: (0, j))), (W["conv_b"], pl.BlockSpec((1, tc), lambda j: (0, j)))],
               [((T, CONV_DIM), BF16, colT(0), None)], "conv_fwd")
    yc = _mm(e, W["w_conv_out"], "nn", T, D_MODEL, CONV_DIM, tm=tm_lat, tn=256, tk=CONV_DIM, name="w_conv_out_fwd",
             out_dtype=BF16, b_spec=pl.BlockSpec((None, CONV_DIM, 256), lambda i, j, k: (j, k, 0)))

    def f_merge(ids, ga, gc, a, c):
        return (_sigmoid(ga) * a.astype(F32) + _sigmoid(gc) * c.astype(F32),)

    (mrg,) = _ew(f_merge, (n_lt,), [(pp, lrows(D_MODEL, 0)), (pp, lrows(D_MODEL, 1)), (ya, lrows(D_MODEL)),
                                   (yc, lrows(D_MODEL))], [lrow_out(D_MODEL, BF16)], "merge_fwd")
    mo = _mm(mrg, W["w_o"], "nn", T, D_MODEL, D_MODEL, tm=tm_lat, tn=D_MODEL, tk=D_MODEL, name="w_o_fwd")

    def f_norm2(ids, x, m, gate, g, sh, sc):
        x1 = x + gate * m
        return x1, (x1 * _rms(x1) * g) * (1.0 + sc) + sh

    x1, h2 = _ew(f_norm2, (n_lt,), [(xx, lrows(D_MODEL)), (mo, lrows(D_MODEL)), (g1, vec(D_MODEL)),
                                   (W["norm2_g"], vec(D_MODEL)), (sh2, vec(D_MODEL)), (sc2, vec(D_MODEL))],
                 [lrow_out(D_MODEL, F32), lrow_out(D_MODEL, BF16)], "norm2_fwd")
    arrived, h2 = late_weights("before_ffn", h2)
    W = dict(W, **arrived)
    up = _mm(h2, W["w_up"], "nn", T, 2 * D_FF, D_MODEL, tm=tm_lat, tn=1408, tk=D_MODEL, name="w_up_fwd",
             b_spec=pl.BlockSpec((None, D_MODEL, 1408), lambda i, j, k: (j, k, 0)))

    n_ff = D_FF // tc
    ffw = lambda off, n=3: pl.BlockSpec((n, tc), lambda j: (0, j + off))

    def f_ffn(ids, ug, uv, wg, wv, bg, bv):
        gate, val = _conv(ug, wg, bg), _conv(uv, wv, bv)
        return (gate * _sigmoid(gate) * val,)

    (act,) = _ew(f_ffn, (n_ff,), [(up, colT(0)), (up, colT(n_ff)), (W["ffn_conv_w"], ffw(0)), (W["ffn_conv_w"], ffw(n_ff)),
                                 (W["ffn_conv_b"], ffw(0, 1)), (W["ffn_conv_b"], ffw(n_ff, 1))],
                 [((T, D_FF), BF16, colT(0), None)], "ffn_act_fwd")
    f = _mm(act, W["w_down"], "nn", T, D_MODEL, D_FF, tm=tm_lat, tn=D_MODEL, tk=D_FF, name="w_down_fwd")

    def f_head(ids, x1_, f_, gate, gf, t):
        x2 = x1_ + gate * f_
        r = _rms(x2)
        xn = x2 * r
        err = xn * gf - t
        loss = 0.5 * jnp.sum(jnp.mean(err * err, axis=-1, keepdims=True))
        dy = err * (1.0 / D_MODEL)
        dx2 = _rms_bwd(dy * gf, xn, r)
        return dx2, dx2 * gate, _colsum(dy * xn), _colsum(dx2 * f_), jnp.full((1, 128), loss, F32)

    dx2, df, dg_f, dg2, loss = _ew(
        f_head, (n_lt,), [(x1, lrows(D_MODEL)), (f, lrows(D_MODEL)), (g2, vec(D_MODEL)), (W["final_g"], vec(D_MODEL)),
                          (tgt, lrows(D_MODEL))],
        [lrow_out(D_MODEL, F32), lrow_out(D_MODEL, BF16), acc_out(D_MODEL), acc_out(D_MODEL), acc_out(128)], "loss_head")

    d_w_down = _mm(act, df, "tn", D_FF, D_MODEL, T, tm=1408, tn=D_MODEL, tk=T, name="w_down_dw",
                   out_dtype=BF16).reshape(4, D_FF // 4, D_MODEL)
    da = _mm(df, W["w_down"], "nt", T, D_FF, D_MODEL, tm=tm_lat, tn=1408, tk=D_MODEL, name="w_down_dx")

    tcb = 128
    n_fb = D_FF // tcb
    colb = lambda blk0: pl.BlockSpec((T, tcb), lambda j: (0, blk0 + j))
    ffwb = lambda off, n=3: pl.BlockSpec((n, tcb), lambda j: (0, j + off))
    cvec = ((1, D_FF), F32, pl.BlockSpec((1, tcb), lambda j: (0, j)), None)

    def f_ffn_bwd(ids, ug, uv, d_act, wg, wv, bg, bv):
        sg, sv = _shifts(ug), _shifts(uv)
        gate, val = _conv(ug, wg, bg, sg), _conv(uv, wv, bv, sv)
        s = _sigmoid(gate)
        d_gate = d_act * val * s * (1.0 + gate * (1.0 - s))
        d_val = d_act * gate * s
        wg0, wg1, wg2 = _conv_bwd_w(d_gate, ug, sg)
        wv0, wv1, wv2 = _conv_bwd_w(d_val, uv, sv)
        d_up = [_conv_bwd_x(d_gate, wg), _conv_bwd_x(d_val, wv)]
        return d_up, [_colsum(d_gate), _colsum(d_val), wg0, wg1, wg2, wv0, wv1, wv2]

    d_up3, ffn_stats = _ew(
        f_ffn_bwd, (n_fb,),
        [(up, colb(0)), (up, colb(n_fb)), (da, colb(0)), (W["ffn_conv_w"], ffwb(0)), (W["ffn_conv_w"], ffwb(n_fb)),
         (W["ffn_conv_b"], ffwb(0, 1)), (W["ffn_conv_b"], ffwb(n_fb, 1))],
        [((2, T, D_FF), BF16, pl.BlockSpec((2, T, tcb), lambda j: (0, 0, j)), None),
         ((n_fb, 8, 1, tcb), F32, pl.BlockSpec((None, 8, 1, tcb), lambda j: (j, 0, 0, 0)), None)], "ffn_act_bwd")
    stat = lambda s: ffn_stats[:, s, 0, :].reshape(1, D_FF)
    d_ffn_conv_b = jnp.concatenate([stat(0), stat(1)], axis=1)
    d_ffn_conv_w = jnp.concatenate([jnp.concatenate([stat(2), stat(3), stat(4)], axis=0),
                                    jnp.concatenate([stat(5), stat(6), stat(7)], axis=0)], axis=1)

    tk_t = T
    d_w_up = _mm(h2, d_up3, "tn", D_MODEL, 2 * D_FF, T, tm=D_MODEL, tn=1408, tk=tk_t, name="w_up_dw", out_dtype=BF16,
                 b_spec=pl.BlockSpec((None, tk_t, 1408), lambda i, j, k: (j // 2, k, j % 2)),
                 o_spec=pl.BlockSpec((None, D_MODEL, 1408), lambda i, j, k: (j, i, 0)), out_shape=(4, D_MODEL, 1408))
    dh2 = _mm(d_up3, W["w_up"], "nt", T, D_MODEL, 2 * D_FF, tm=tm_lat, tn=D_MODEL, tk=1408, name="w_up_dx",
              a_spec=pl.BlockSpec((None, tm_lat, 1408), lambda i, j, k: (k // 2, i, k % 2)),
              b_spec=pl.BlockSpec((None, D_MODEL, 1408), lambda i, j, k: (k, j, 0)))

    def f_norm2_bwd(ids, dx2_, dh, x1_, m, g, sc, gate):
        r = _rms(x1_)
        xn = x1_ * r
        dx1 = dx2_ + _rms_bwd(dh * g * (1.0 + sc), xn, r)
        return dx1, dx1 * gate, _colsum(dh), _colsum(dh * xn * g), _colsum(dh * xn * (1.0 + sc)), _colsum(dx1 * m)

    dx1, dmo, dsh2, dsc2, dg_n2, dg1 = _ew(
        f_norm2_bwd, (n_lt,), [(dx2, lrows(D_MODEL)), (dh2, lrows(D_MODEL)), (x1, lrows(D_MODEL)), (mo, lrows(D_MODEL)),
                               (W["norm2_g"], vec(D_MODEL)), (sc2, vec(D_MODEL)), (g1, vec(D_MODEL))],
        [lrow_out(D_MODEL, F32), lrow_out(D_MODEL, BF16)] + [acc_out(D_MODEL)] * 4, "norm2_bwd")
    d_w_o = _mm(mrg, dmo, "tn", D_MODEL, D_MODEL, T, tm=D_MODEL, tn=D_MODEL, tk=tk_t, name="w_o_dw",
                out_dtype=BF16).reshape(4, D_MODEL // 4, D_MODEL)
    dmrg = _mm(dmo, W["w_o"], "nt", T, D_MODEL, D_MODEL, tm=tm_lat, tn=D_MODEL, tk=D_MODEL, name="w_o_dx",
               out_dtype=BF16)
    dmrg = early_grads("late", {"w_o": d_w_o, "w_up": d_w_up, "w_down": d_w_down}, dmrg, split=True)

    def f_merge_bwd(ids, dm, ga, gc, a, c):
        dm, a, c = dm.astype(F32), a.astype(F32), c.astype(F32)
        sa, sc_ = _sigmoid(ga), _sigmoid(gc)
        return dm * sa, dm * sc_, dm * a * sa * (1.0 - sa), dm * c * sc_ * (1.0 - sc_)

    dya, dyc, dp_ga, dp_gc = _ew(
        f_merge_bwd, (n_lt,), [(dmrg, lrows(D_MODEL)), (pp, lrows(D_MODEL, 0)), (pp, lrows(D_MODEL, 1)),
                               (ya, lrows(D_MODEL)), (yc, lrows(D_MODEL))], [lrow_out(D_MODEL, BF16)] * 4, "merge_bwd")
    dya = early_continue("late", dya)

    d_w_ao_p = _mm(o_pad, dya, "tn", 1024, D_MODEL, T, tm=1024, tn=D_MODEL, tk=tk_t, name="w_attn_out_dw", out_dtype=BF16)
    do_pad = _mm(dya, W["w_attn_out"], "nt", T, 1024, D_MODEL, tm=tm_lat, tn=1024, tk=D_MODEL, name="w_attn_out_dx")
    d_w_co = _mm(e, dyc, "tn", CONV_DIM, D_MODEL, T, tm=CONV_DIM, tn=256, tk=tk_t, name="w_conv_out_dw", out_dtype=BF16,
                 o_spec=pl.BlockSpec((None, CONV_DIM, 256), lambda i, j, k: (j, i, 0)), out_shape=(4, CONV_DIM, 256))
    de = _mm(dyc, W["w_conv_out"], "nt", T, CONV_DIM, D_MODEL, tm=tm_lat, tn=CONV_DIM, tk=256, name="w_conv_out_dx",
             b_spec=pl.BlockSpec((None, CONV_DIM, 256), lambda i, j, k: (k, j, 0)))

    def f_conv_bwd(ids, xin, cb, cc, d_e, w, b):
        z = cc * xin
        sz = _shifts(z)
        cz = _conv(z, w, b, sz)
        dcz = d_e * cb
        w0, w1, w2 = _conv_bwd_w(dcz, z, sz)
        dz = _conv_bwd_x(dcz, w)
        return dz * cc, d_e * cz, dz * xin, _colsum(dcz), w0, w1, w2

    cvec_c = ((1, CONV_DIM), F32, pl.BlockSpec((1, tc), lambda j: (0, j)), None)
    conv_b = _ew(f_conv_bwd, (CONV_DIM // tc,),
                 [(pp, colT(CX0 // tc)), (pp, colT(CB0 // tc)), (pp, colT(CC0 // tc)), (de, colT(0)),
                  (W["conv_w"], pl.BlockSpec((3, tc), lambda j: (0, j))), (W["conv_b"], pl.BlockSpec((1, tc), lambda j: (0, j)))],
                 [((T, CONV_DIM), BF16, colT(0), None)] * 3 + [cvec_c] * 4, "conv_bwd")
    dp_cx, dp_cb, dp_cc, d_conv_b = conv_b[:4]
    d_conv_w = jnp.concatenate(conv_b[4:7], axis=0)

    dq_raw, dkv, dp_kr = _attn_bwd(q_raw, kv, pp_a, o_pad, do_pad, tab, T, TT)

    tk_a = TT
    d_w_uq_t = _mm(nq, dq_raw, "tn", Q_RANK, 1024, T, tm=Q_RANK, tn=1024, tk=T, name="w_uq_dw", transpose_out=True)
    dnq = _mm(dq_raw, W["w_uq_t"], "nn", T, Q_RANK, 1024, tm=tm_lat, tn=Q_RANK, tk=1024, name="w_uq_dx")
    d_w_ukv = _mm(nkv, dkv, "tn", KV_RANK, 1024, TT, tm=KV_RANK, tn=256, tk=tk_a, name="w_ukv_dw", out_dtype=BF16,
                  o_spec=pl.BlockSpec((None, KV_RANK, 256), lambda i, j, k: (j, i, 0)), out_shape=(4, KV_RANK, 256))
    dnkv = _mm(dkv, W["w_ukv"], "nt", TT, KV_RANK, 1024, tm=tm_all, tn=KV_RANK, tk=256, name="w_ukv_dx",
               b_spec=pl.BlockSpec((None, KV_RANK, 256), lambda i, j, k: (k, j, 0)))
    dnkv = early_grads("mid", {
        "w_attn_out": jnp.transpose(d_w_ao_p.reshape(N_HEADS, HEAD_PAD, 4, 256)[:, 64:], (2, 0, 1, 3)).reshape(
            4, N_HEADS * 64, 256),
        "w_conv_out": d_w_co,
        "w_uq": d_w_uq_t.reshape(4, 2, HEAD_PAD, Q_RANK)[:, :, :QK_DIM].reshape(4, 2 * QK_DIM, Q_RANK).astype(BF16),
        "w_ukv": d_w_ukv}, dnkv)

    def f_lowrank_bwd(ids, ckv, cq, dkv_, dq_, gkv, gq, ga, gc, cx, cb, cc, kr):
        rk, rq = _rms(ckv), _rms(cq)
        nk, nq_ = ckv * rk, cq * rq
        lat = ids[0] < n_lat
        dq_ = jnp.where(lat, dq_, 0.0)
        pieces = [jnp.where(lat, a, jnp.zeros_like(a)) for a in (ga, gc, cx, cb, cc)]
        pieces += [_rms_bwd(dkv_ * gkv, nk, rk).astype(BF16), _rms_bwd(dq_ * gq, nq_, rq).astype(BF16), kr.astype(BF16)]
        return jnp.concatenate(pieces, axis=1), _colsum(dkv_ * nk), _colsum(dq_ * nq_)

    lat_rows = lambda n: pl.BlockSpec((ROW_TILE, n), lambda i: (jnp.minimum(i, n_lat - 1), 0))
    dpp, dg_kv, dg_q = _ew(
        f_lowrank_bwd, (n_all,), [(pp_a, _rows(KV_RANK, PA_KV0 // KV_RANK)), (pp_a, _rows(Q_RANK, PA_Q0 // Q_RANK)),
                                  (dnkv, _rows(KV_RANK)), (dnq, lat_rows(Q_RANK)), (W["kv_norm_g"], vec(KV_RANK)),
                                  (W["q_norm_g"], vec(Q_RANK)), (dp_ga, lat_rows(D_MODEL)), (dp_gc, lat_rows(D_MODEL)),
                                  (dp_cx, lat_rows(CONV_DIM)), (dp_cb, lat_rows(CONV_DIM)), (dp_cc, lat_rows(CONV_DIM)),
                                  (dp_kr, _rows(HEAD_PAD))],
        [row_out(P_COLS, BF16, TT), acc_out(KV_RANK), acc_out(Q_RANK)], "lowrank_norm_bwd")
    d_w_in_t = _mm(hh, dpp, "tn", D_MODEL, P_COLS, TT, tm=512, tn=2176, tk=TT, name="w_in_dw", out_dtype=BF16,
                   transpose_out=True)
    dhh = _mm(dpp, W["w_in_t"], "nn", TT, D_MODEL, P_COLS, tm=tm_all, tn=512, tk=2176, name="w_in_dx")

    def f_norm1_bwd(ids, x, dh, dres, g, sc):
        r = _rms(x)
        xn = x * r
        return (dres + _rms_bwd(dh * g * (1.0 + sc), xn, r), _colsum(dh), _colsum(dh * xn * g),
                _colsum(dh * xn * (1.0 + sc)))

    grad_x, dsh1, dsc1, dg_n1 = _ew(
        f_norm1_bwd, (n_lt,), [(xx, lrows(D_MODEL)), (dhh, lrows(D_MODEL)), (dx1, lrows(D_MODEL)),
                               (W["norm1_g"], vec(D_MODEL)), (sc1, vec(D_MODEL))],
        [lrow_out(D_MODEL, F32)] + [acc_out(D_MODEL)] * 3, "norm1_bwd")

    def f_norm1_ctx_bwd(ids, x, dh, g, sc):
        xn = x * _rms(x)
        return _colsum(dh), _colsum(dh * xn * g), _colsum(dh * xn * (1.0 + sc))

    n_ctx = n_all - n_lat
    dcsh1, dcsc1, dg_n1c = _ew(
        f_norm1_ctx_bwd, (n_ctx,), [(xx, _rows(D_MODEL, 0, n_lat)), (dhh, _rows(D_MODEL, 0, n_lat)),
                                    (W["norm1_g"], vec(D_MODEL)), (csc1, vec(D_MODEL))], [acc_out(D_MODEL)] * 3,
        "norm1_ctx_bwd")

    big = {"w_in": _w_in_t_shards_from_p(d_w_in_t).astype(BF16)}
    zero = jnp.zeros((1, 4 * D_MODEL), F32)
    small = {
        "dmod_lat": jnp.concatenate([dsh1, dsc1, dg1, dsh2, dsc2, dg2], axis=1),
        "dmod_ctx": jnp.concatenate([dcsh1, dcsc1, zero], axis=1),
        "norm1_g": dg_n1 + dg_n1c, "norm2_g": dg_n2, "final_g": dg_f, "q_norm_g": dg_q, "kv_norm_g": dg_kv,
        "conv_b": d_conv_b, "conv_w": d_conv_w.reshape(1, -1), "ffn_conv_b": d_ffn_conv_b,
        "ffn_conv_w": d_ffn_conv_w.reshape(1, -1),
    }
    return grad_x, loss, big, small


SMALL = (("dmod_lat", 6144), ("dmod_ctx", 6144), ("norm1_g", 1024), ("norm2_g", 1024), ("final_g", 1024),
         ("q_norm_g", 384), ("kv_norm_g", 256), ("conv_b", 512), ("conv_w", 1536), ("ffn_conv_b", 5632),
         ("ffn_conv_w", 16896), ("loss", 128))
SMALL_ROWS = 320


def _adam_update(w, g, m, v):
    c1, c2 = 1.0 - ADAM_B1 ** ADAM_STEP, 1.0 - ADAM_B2 ** ADAM_STEP
    m2 = ADAM_B1 * m + (1.0 - ADAM_B1) * g
    v2 = ADAM_B2 * v + (1.0 - ADAM_B2) * (g * g)
    return [-ADAM_LR * ((m2 / c1) / (jnp.sqrt(v2 / c2) + ADAM_EPS) + ADAM_WD * w), m2, v2]


def _adamw(w, g, m, v, name):
    R, C = w.shape
    tr = 8 if R % 8 == 0 else R
    for t in range(8, R + 1, 8):
        if R % t == 0 and t * C * 4 <= (1 << 21):
            tr = t
    spec = pl.BlockSpec((tr, C), lambda i: (i, 0))
    return _ew(lambda ids, *vals: _adam_update(*vals), (R // tr,), [(w, spec), (g, spec), (m, spec), (v, spec)],
               [((R, C), F32, spec, None)] * 3, name)


def kernel(x, c, ctx, c_ctx, w_ada, b_ada, norm1_g, w_in, q_norm_g, kv_norm_g, w_uq, w_ukv, conv_w, conv_b, w_attn_out, w_conv_out, w_o, norm2_g, w_up, ffn_conv_w, ffn_conv_b, w_down, final_g, loss_target, m_c_ctx, m_w_ada, m_b_ada, m_norm1_g, m_w_in, m_q_norm_g, m_kv_norm_g, m_w_uq, m_w_ukv, m_conv_w, m_conv_b, m_w_attn_out, m_w_conv_out, m_w_o, m_norm2_g, m_w_up, m_ffn_conv_w, m_ffn_conv_b, m_w_down, m_final_g, v_c_ctx, v_w_ada, v_b_ada, v_norm1_g, v_w_in, v_q_norm_g, v_kv_norm_g, v_w_uq, v_w_ukv, v_conv_w, v_conv_b, v_w_attn_out, v_w_conv_out, v_w_o, v_norm2_g, v_w_up, v_ffn_conv_w, v_ffn_conv_b, v_w_down, v_final_g):
    mx, my, mc = lax.axis_index("x"), lax.axis_index("y"), lax.axis_index("c")
    chip = 2 * mx + my
    dev = 4 * mx + 2 * my + mc
    T, Tc = x.shape[1], ctx.shape[1]
    TT = T + Tc
    w_in_t, m_w_in_t, v_w_in_t = (jnp.transpose(a[0]) for a in (w_in, m_w_in, v_w_in))
    w_uq_t, m_w_uq_t, v_w_uq_t = (jnp.transpose(a[0]) for a in (w_uq, m_w_uq, v_w_uq))
    conv_sh = jnp.concatenate([conv_w[0], ffn_conv_w[0]], axis=1)
    pay1 = jnp.concatenate([jnp.pad(c, ((0, 7), (0, 0))), jnp.pad(conv_sh, ((0, 5), (0, 0)))], axis=1)
    c_send, c_recv, c_src, c_land, zero0 = _ici_start("all", [pay1], [(8, 8, 2560)], jnp.zeros((8, 128), F32),
                                                      "cond_start")
    w_in_bf = (jnp.pad(w_in_t, ((0, W_IN_SHARD_PAD - W_IN_SHARD), (0, 0))) + zero0[0, 0]).astype(BF16)
    shards = {"w_in_a": w_in_bf[:W_IN_EARLY], "w_in_b": w_in_bf[W_IN_EARLY:], "w_uq": w_uq_t, "w_ukv": w_ukv[0],
              "w_attn_out": w_attn_out[0], "w_conv_out": w_conv_out[0], "w_o": w_o[0], "w_up": w_up[0],
              "w_down": w_down[0]}
    (pay1,), (c_land,) = _ici_wait("all", c_send, c_recv, c_src, c_land, w_in_bf, "cond_wait")
    got1 = lax.dynamic_update_slice(c_land, pay1[None], (dev, 0, 0))
    c_all = got1[:, 0, :D_MODEL]
    conv_all = got1[0::2, :3, D_MODEL:]
    conv_w_full = _cols_from_shards(conv_all[:, :, :128])
    ffn_conv_w_full = _cols_from_shards(conv_all[:, :, 128:])

    cond = jnp.concatenate([c_all, c_ctx.reshape(1, D_MODEL), jnp.zeros((7, D_MODEL), F32)], axis=0)

    def f_silu(ids, v):
        return (v * _sigmoid(v),)

    (s16,) = _ew(f_silu, (1,), [(cond, _full((16, D_MODEL)))], [((16, D_MODEL), F32, _full((16, D_MODEL)), None)], "silu_cond")
    mod_sh = _mm(s16, w_ada[0], "nn", 16, 1536, D_MODEL, tm=16, tn=768, tk=D_MODEL, name="w_ada_fwd")
    m_send, m_recv, m_src, m_land, zero1 = _ici_start("all", [mod_sh], [(8, 16, 1536)], jnp.zeros((8, 128), F32),
                                                      "mod_start")
    shards["w_ukv"] = w_ukv[0] + zero1[0, 0]

    first = ["w_in_a", "w_uq", "w_ukv"]
    gathered, zero = _gather_weights([shards[n].astype(BF16) for n in first])
    full = dict(zip(first, gathered))
    (mod_mine,), (m_land,) = _ici_wait("all", m_send, m_recv, m_src, m_land, gathered[0], "mod_wait")
    got2 = lax.dynamic_update_slice(m_land, mod_mine[None], (dev, 0, 0))
    mod_all = _cols_from_shards(got2[0::2]) + b_ada
    mod_lat = lax.dynamic_slice_in_dim(mod_all, dev, 1, axis=0)
    mod_ctx = mod_all[8:9]
    xx = jnp.concatenate([x[0], ctx[0]], axis=0)
    late_groups = {"g1": ("w_in_b", "w_attn_out", "w_conv_out", "w_o"), "g2": ("w_up", "w_down")}
    flight = {}
    for tag, group in late_groups.items():
        bf = [(shards[n] + zero[0, 0]).astype(BF16) for n in group]
        flight[tag] = _ici_start("gather", bf, [(4,) + s.shape for s in bf], xx, "gather_" + tag + "_start")
        xx = flight[tag][4]

    def chip_stage_done(tag, x):
        send, recv, src, land, _ = flight[tag]
        src, land = _ici_wait("gather", send, recv, src, land, x, "gather_" + tag + "_wait")
        flight[tag] = _ici_start("finish", src, None, x, "finish_" + tag + "_start", lands=land)
        return flight[tag][4]

    def arrived(tag, x):
        send, recv, src, land, _ = flight[tag]
        return dict(zip(late_groups[tag], _ici_wait("finish", send, recv, src, land, x, "finish_" + tag + "_wait")[1]))

    def late_weights(point, x):
        if point == "before_attn":
            return {}, chip_stage_done("g1", x)
        if point == "after_attn":
            got = arrived("g1", x)
            wao = _cols_from_shards(got["w_attn_out"]).reshape(N_HEADS, 64, D_MODEL)
            w_in_all = jnp.concatenate([full["w_in_a"], got["w_in_b"]], axis=1)
            ready = {"w_in_t": _w_in_t_p_from_shards(w_in_all),
                     "w_attn_out": jnp.pad(wao, ((0, 0), (64, 0), (0, 0))).reshape(N_HEADS * HEAD_PAD, D_MODEL),
                     "w_conv_out": got["w_conv_out"], "w_o": got["w_o"].reshape(D_MODEL, D_MODEL)}
            return ready, chip_stage_done("g2", x)
        got = arrived("g2", x)
        return {"w_up": got["w_up"], "w_down": got["w_down"].reshape(D_FF, D_MODEL)}, x

    wuq_t = full["w_uq"].reshape(N_HEADS, QK_DIM, Q_RANK)
    early_rows = full["w_in_a"][0]
    zrows = lambda n: jnp.zeros((n, D_MODEL), BF16)
    W = {
        "w_in_a_t": jnp.concatenate([early_rows[0:256], zrows(PA_Q0 - 256), early_rows[288:672], zrows(64),
                                     early_rows[256:288], zrows(32)], axis=0),
        "w_uq_t": jnp.pad(wuq_t, ((0, 0), (0, HEAD_PAD - QK_DIM), (0, 0))).reshape(N_HEADS * HEAD_PAD, Q_RANK),
        "w_ukv": full["w_ukv"],
        "norm1_g": norm1_g, "norm2_g": norm2_g, "final_g": final_g.reshape(1, D_MODEL), "q_norm_g": q_norm_g,
        "kv_norm_g": kv_norm_g, "conv_w": conv_w_full, "conv_b": conv_b, "ffn_conv_w": ffn_conv_w_full,
        "ffn_conv_b": ffn_conv_b,
    }

    place = jnp.stack([chip, mc]).astype(jnp.int32)
    early = {}

    pending = {}

    def scatter(tag, group, gs, from_sib, carry):
        if tag == "mid":
            sums = _add_pair_many(gs, from_sib, place, "rs_pair_add_mid")
        else:
            sums = [_add_pair(gs[w], from_sib[w], place, "rs_pair_add_" + n) for w, n in enumerate(group)]
        send, recv, sums, land, carry = _ici_start(
            "scatter", sums, [(3,) + s.shape[1:] for s in sums], carry, "rs_chips_" + tag + "_start")
        early[tag] = (group, send, recv, sums, land)
        return carry

    def early_grads(tag, g, carry, split=False):
        gs = list(g.values())
        if not split:
            return scatter(tag, list(g), gs, _rs_pair(gs, "rs_pair_" + tag), carry)
        send, recv, gs, land, carry = _ici_start(
            "pair", gs, [(4, s.shape[1] // 2, s.shape[2]) for s in gs], carry, "rs_pair_" + tag + "_start")
        pending[tag] = (list(g), send, recv, gs, land)
        return carry

    def early_continue(tag, carry):
        group, send, recv, gs, land = pending[tag]
        gs, from_sib = _ici_wait("pair", send, recv, gs, land, carry, "rs_pair_" + tag + "_wait")
        return scatter(tag, group, gs, from_sib, carry)

    grad_x, loss_part, gbig, gsmall = _local_step(xx, loss_target[0], mod_lat, mod_ctx, W, late_weights, early_grads,
                                                  early_continue)

    gsmall["loss"] = loss_part
    pay3 = jnp.concatenate([gsmall[n].reshape(-1) for n, _ in SMALL])
    pay3 = jnp.pad(pay3, (0, SMALL_ROWS * 128 - pay3.shape[0])).reshape(SMALL_ROWS, 128)
    s_send, s_recv, s_src, s_land, w_in_thru = _ici_start("all", [pay3], [(8, SMALL_ROWS, 128)], gbig["w_in"],
                                                         "small_start")
    gbig = {"w_in": w_in_thru}

    after_small = early_grads("last", gbig, s_src[0])

    (pay3,), (s_land,) = _ici_wait("all", s_send, s_recv, [after_small], s_land, early["last"][3][0], "small_wait")
    got3 = lax.dynamic_update_slice(s_land, pay3[None], (dev, 0, 0)).reshape(8 * SMALL_ROWS, 128)

    def f_sum8(ids, a):
        s = a[0:SMALL_ROWS]
        for d in range(1, 8):
            s = s + a[d * SMALL_ROWS:(d + 1) * SMALL_ROWS]
        return (s,)

    (vsum,) = _ew(f_sum8, (1,), [(got3, _full((8 * SMALL_ROWS, 128)))],
                  [((SMALL_ROWS, 128), F32, _full((SMALL_ROWS, 128)), None)], "sum_small")
    vflat = vsum.reshape(-1)
    gvec, off = {}, 0
    for n, size in SMALL:
        gvec[n] = vflat[off:off + size]
        off += size
    loss = gvec["loss"][0]
    dmod_rows = got3.reshape(8, SMALL_ROWS * 128)[:, :6 * D_MODEL]
    dm16 = jnp.concatenate([dmod_rows, gvec["dmod_ctx"].reshape(1, -1), jnp.zeros((7, 6 * D_MODEL), F32)], axis=0)

    def f_colsum(ids, a):
        return (_colsum(a),)

    (g_b_ada,) = _ew(f_colsum, (1,), [(dm16, _full((16, 6 * D_MODEL)))],
                     [((1, 6 * D_MODEL), F32, _full((1, 6 * D_MODEL)), None)], "b_ada_grad")
    dm_sh = lax.dynamic_slice_in_dim(dm16, chip * 1536, 1536, axis=1)
    g_w_ada = _mm(s16, dm_sh, "tn", D_MODEL, 1536, 16, tm=512, tn=768, tk=16, name="w_ada_dw")
    dcond_part = _mm(dm_sh, w_ada[0], "nt", 16, D_MODEL, 1536, tm=16, tn=512, tk=1536, name="w_ada_dx")
    d_send, d_recv, d_src, d_land, vsum = _ici_start("all", [dcond_part[8:16]], [(8, 8, D_MODEL)], vsum, "dcond_start")

    def finish_start(tags, after):
        done, halves = [], []
        for tag in tags:
            tag_names, send, recv, sums, land = early[tag]
            sums, land = _ici_wait("scatter", send, recv, sums, land, after, "rs_chips_" + tag + "_wait")
            done += tag_names
            if tag == "mid":
                halves += _add_chips_many(sums, land, place, "rs_chip_add_mid")
            else:
                halves += [_add_chips(a, b, place, "rs_chip_add_" + n) for a, b, n in zip(sums, land, tag_names)]
        send, recv, _, halves, _ = _ici_start("back", [], None, jnp.zeros((8, 128), F32), "rs_back_" + tags[0] + "_start",
                                              lands=halves)
        return done, send, recv, halves

    def finish_wait(state, after):
        done, send, recv, halves = state
        return dict(zip(done, _ici_wait("back", send, recv, [], halves, after, "rs_back_" + done[0] + "_wait")[1]))

    grads, deltas, new_m, new_v = {}, {}, {}, {}

    raw = {}

    def adam(n, w_, m_, v_, g, transposed):
        d_, m2, v2 = _adamw(w_, g, m_, v_, "adamw_" + n)
        raw[n] = d_
        back = (lambda a: jnp.transpose(a)[None]) if transposed else (lambda a: a[None])
        grads[n], deltas[n], new_m[n], new_v[n] = back(g[:w_.shape[0]]), back(d_), back(m2), back(v2)

    pending_back = finish_start(["late", "mid"], grad_x)
    adam("w_ada", w_ada[0], m_w_ada[0], v_w_ada[0], g_w_ada, False)
    gw = finish_wait(pending_back, raw["w_ada"])
    for n, (w_, m_, v_) in {"w_o": (w_o, m_w_o, v_w_o), "w_up": (w_up, m_w_up, v_w_up),
                            "w_down": (w_down, m_w_down, v_w_down)}.items():
        adam(n, w_[0], m_[0], v_[0], gw[n], False)
    pending_back = finish_start(["last"], raw["w_up"])

    (dcond_mine,), (d_land,) = _ici_wait("all", d_send, d_recv, d_src, d_land, raw["w_down"], "dcond_wait")
    got4 = lax.dynamic_update_slice(d_land, dcond_mine[None], (dev, 0, 0))[0::2, 0]

    def f_c_ctx(ids, parts, cc):
        s = _sigmoid(cc)
        d = parts[0:1] + parts[1:2] + parts[2:3] + parts[3:4]
        return (d * s * (1.0 + cc * (1.0 - s)),)

    (g_c_ctx,) = _ew(f_c_ctx, (1,), [(got4, _full((4, D_MODEL))), (c_ctx.reshape(1, D_MODEL), _full((1, D_MODEL)))],
                     [((1, D_MODEL), F32, _full((1, D_MODEL)), None)], "c_ctx_grad")

    conv_w_g = lax.dynamic_slice_in_dim(gvec["conv_w"].reshape(3, CONV_DIM), chip * 128, 128, axis=1)
    ffn_conv_w_g = lax.dynamic_slice_in_dim(gvec["ffn_conv_w"].reshape(3, 2 * D_FF), chip * 1408, 1408, axis=1)
    vec_params = (("c_ctx", c_ctx, m_c_ctx, v_c_ctx, g_c_ctx), ("b_ada", b_ada, m_b_ada, v_b_ada, g_b_ada),
                  ("norm1_g", norm1_g, m_norm1_g, v_norm1_g, gvec["norm1_g"]),
                  ("q_norm_g", q_norm_g, m_q_norm_g, v_q_norm_g, gvec["q_norm_g"]),
                  ("kv_norm_g", kv_norm_g, m_kv_norm_g, v_kv_norm_g, gvec["kv_norm_g"]),
                  ("conv_w", conv_w, m_conv_w, v_conv_w, conv_w_g), ("conv_b", conv_b, m_conv_b, v_conv_b, gvec["conv_b"]),
                  ("norm2_g", norm2_g, m_norm2_g, v_norm2_g, gvec["norm2_g"]),
                  ("ffn_conv_w", ffn_conv_w, m_ffn_conv_w, v_ffn_conv_w, ffn_conv_w_g),
                  ("ffn_conv_b", ffn_conv_b, m_ffn_conv_b, v_ffn_conv_b, gvec["ffn_conv_b"]),
                  ("final_g", final_g, m_final_g, v_final_g, gvec["final_g"]))
    two_d = lambda a: a.reshape((-1, a.shape[-1]))
    many = [p + ((lambda r, s=p[1].shape: r.reshape(s)),) for p in vec_params]
    for n, w_, m_, v_ in (("w_ukv", w_ukv, m_w_ukv, v_w_ukv), ("w_attn_out", w_attn_out, m_w_attn_out, v_w_attn_out),
                          ("w_conv_out", w_conv_out, m_w_conv_out, v_w_conv_out)):
        many.append((n, w_, m_, v_, gw[n], (lambda r, s=w_.shape: r.reshape(s))))
    many.append(("w_uq", w_uq_t, m_w_uq_t, v_w_uq_t, gw["w_uq"], lambda r: jnp.transpose(r)[None]))

    def f_adam_many(ids, *vals):
        out = []
        for k in range(len(many)):
            out += _adam_update(*vals[4 * k:4 * k + 4])
        return out

    ins_v, outs_v = [], []
    for p in many:
        shp = two_d(p[1]).shape
        ins_v += [(two_d(a), _full(shp)) for a in (p[1], p[4], p[2], p[3])]
        outs_v += [(shp, F32, _full(shp), None)] * 3
    res_v = _ew(f_adam_many, (1,), ins_v, outs_v, "adamw_small")
    for k, p in enumerate(many):
        n, post = p[0], p[5]
        grads[n] = post(two_d(p[4]))
        deltas[n], new_m[n], new_v[n] = (post(r) for r in res_v[3 * k:3 * k + 3])

    gw_in = finish_wait(pending_back, res_v[0])
    adam("w_in", w_in_t, m_w_in_t, v_w_in_t, gw_in["w_in"], True)

    order = ("c_ctx", "w_ada", "b_ada", "norm1_g", "w_in", "q_norm_g", "kv_norm_g", "w_uq", "w_ukv", "conv_w", "conv_b",
             "w_attn_out", "w_conv_out", "w_o", "norm2_g", "w_up", "ffn_conv_w", "ffn_conv_b", "w_down", "final_g")
    return (loss, grad_x[None], *[grads[n] for n in order], *[deltas[n] for n in order],
            *[new_m[n] for n in order], *[new_v[n] for n in order])
```

```python
import functools

import jax
import jax.numpy as jnp
import numpy as np
from jax import lax
from jax.experimental import pallas as pl
from jax.experimental.pallas import tpu as pltpu

F32, BF16 = jnp.float32, jnp.bfloat16
MESH = pl.DeviceIdType.MESH

D_MODEL = 1024
N_HEADS = 8
HEAD_PAD = 128
QK_DIM = 96
Q_RANK, KV_RANK = 384, 256
CONV_DIM = 512
D_FF = 2816
GRID_W = 64
ROPE_THETA = 10000.0
EPS = 1e-6
GA0, GC0, CX0, CB0, CC0, KV0, Q0, KR0, P_COLS = 0, 1024, 2048, 2560, 3072, 3584, 3840, 4224, 4352
PA_KV0, PA_Q0, PA_KR0, PA_COLS = 0, 384, 768, 896
ROW_TILE = 256
VMEM_LIMIT_BYTES = 48 * 1024 * 1024

ADAM_LR, ADAM_B1, ADAM_B2, ADAM_EPS, ADAM_WD, ADAM_STEP = 0.001, 0.9, 0.999, 1e-08, 0.01, 10

NN = (((1,), (0,)), ((), ()))
NT = (((1,), (1,)), ((), ()))
TN = (((0,), (0,)), ((), ()))


def _cp(sem):
    return pltpu.CompilerParams(dimension_semantics=sem, vmem_limit_bytes=VMEM_LIMIT_BYTES)


PIN_BYTES = 1 << 19


def _in_hbm(arrays):
    return [pltpu.with_memory_space_constraint(a, pltpu.HBM) if a.size * a.dtype.itemsize >= PIN_BYTES else a
            for a in arrays]


def _out(shape, dtype):
    n = 1
    for d in shape:
        n *= d
    big = n * jnp.dtype(dtype).itemsize >= PIN_BYTES
    return pltpu.HBM(shape, dtype) if big else jax.ShapeDtypeStruct(shape, dtype)


def _pick(n, prefs):
    for p in prefs:
        if n % p == 0:
            return p
    return n


def _mm(a, b, mode, M, N, K, *, tm, tn, tk, name, out_dtype=F32, a_spec=None, b_spec=None, o_spec=None,
        out_shape=None, transpose_out=False):
    assert M % tm == 0 and N % tn == 0 and K % tk == 0, (name, M, N, K, tm, tn, tk)
    nk = K // tk
    dims = {"nn": NN, "nt": NT, "tn": TN}[mode]
    if a_spec is None:
        a_spec = (pl.BlockSpec((tk, tm), lambda i, j, k: (k, i)) if mode == "tn"
                  else pl.BlockSpec((tm, tk), lambda i, j, k: (i, k)))
    if b_spec is None:
        b_spec = (pl.BlockSpec((tn, tk), lambda i, j, k: (j, k)) if mode == "nt"
                  else pl.BlockSpec((tk, tn), lambda i, j, k: (k, j)))
    if o_spec is None:
        o_spec = (pl.BlockSpec((tn, tm), lambda i, j, k: (j, i)) if transpose_out
                  else pl.BlockSpec((tm, tn), lambda i, j, k: (i, j)))
    if out_shape is None:
        out_shape = (N, M) if transpose_out else (M, N)

    def emit(o_ref, val):
        o_ref[...] = (val.T if transpose_out else val).astype(o_ref.dtype)

    def body(a_ref, b_ref, o_ref, *scratch):
        part = lax.dot_general(a_ref[...].astype(BF16), b_ref[...].astype(BF16), dims, preferred_element_type=F32)
        if nk == 1:
            emit(o_ref, part)
            return
        acc_ref, = scratch
        k = pl.program_id(2)

        @pl.when(k == 0)
        def _():
            acc_ref[...] = part

        @pl.when((k > 0) & (k < nk - 1))
        def _():
            acc_ref[...] += part

        @pl.when(k == nk - 1)
        def _():
            emit(o_ref, acc_ref[...] + part)

    return pl.pallas_call(
        body, grid=(M // tm, N // tn, nk), in_specs=[a_spec, b_spec], out_specs=o_spec,
        out_shape=_out(out_shape, out_dtype),
        scratch_shapes=[pltpu.VMEM((tm, tn), F32)] if nk > 1 else [],
        compiler_params=_cp(("parallel", "parallel", "arbitrary")), name=name)(*_in_hbm([a, b]))


def _ew(fn, grid, ins, outs, name, scalars=None):
    n_in = len(ins)
    n_sc = 0 if scalars is None else 1

    def store(ref, val, acc, ids):
        if isinstance(val, (list, tuple)):
            for h, v in enumerate(val):
                ref[h] = v.astype(ref.dtype)
            return
        if acc is None:
            ref[...] = val.astype(ref.dtype)
            return

        @pl.when(ids[acc] == 0)
        def _():
            ref[...] = val.astype(ref.dtype)

        @pl.when(ids[acc] > 0)
        def _():
            ref[...] += val.astype(ref.dtype)

    def body(*refs):
        refs = refs[n_sc:]
        ids = tuple(pl.program_id(a) for a in range(len(grid)))
        vals = fn(ids, *[r[...] for r in refs[:n_in]])
        for ref, val, (_, _, _, acc) in zip(refs[n_in:], vals, outs):
            store(ref, val, acc, ids)

    acc_axes = {o[3] for o in outs if o[3] is not None}
    sem = tuple("arbitrary" if a in acc_axes else "parallel" for a in range(len(grid)))
    in_specs, out_specs = [s for _, s in ins], [o[2] for o in outs]
    out_shape = [_out(o[0], o[1]) for o in outs]
    args = _in_hbm([a for a, _ in ins])
    if scalars is None:
        return pl.pallas_call(body, grid=grid, in_specs=in_specs, out_specs=out_specs, out_shape=out_shape,
                              compiler_params=_cp(sem), name=name)(*args)
    spec = pltpu.PrefetchScalarGridSpec(num_scalar_prefetch=1, grid=grid, in_specs=in_specs, out_specs=out_specs)
    return pl.pallas_call(body, grid_spec=spec, out_shape=out_shape, compiler_params=_cp(sem), name=name)(scalars, *args)


def _rows(width, cblk=0, roff=0, tr=ROW_TILE):
    return pl.BlockSpec((tr, width), lambda i: (i + roff, cblk))


def _full(shape):
    nd = len(shape)
    return pl.BlockSpec(shape, lambda *_: (0,) * nd)


def _sigmoid(x):
    return 1.0 / (1.0 + jnp.exp2(x * (-1.4426950408889634)))


def _rms(x):
    return lax.rsqrt(jnp.mean(x * x, axis=-1, keepdims=True) + EPS)


def _rms_bwd(dn, xn, r):
    return r * (dn - xn * jnp.mean(dn * xn, axis=-1, keepdims=True))


def _colsum(x):
    return jnp.sum(x, axis=0, keepdims=True)


def _shifts(x):
    n = x.shape[0]
    rows = lax.broadcasted_iota(jnp.int32, x.shape, 0)
    return jnp.where(rows == 0, 0.0, pltpu.roll(x, 1, 0)), jnp.where(rows == n - 1, 0.0, pltpu.roll(x, n - 1, 0))


def _conv(x, w, b, shifted=None):
    prev, nxt = _shifts(x) if shifted is None else shifted
    return b + prev * w[0:1] + x * w[1:2] + nxt * w[2:3]


def _conv_bwd_x(dy, w):
    prev, nxt = _shifts(dy)
    return nxt * w[0:1] + dy * w[1:2] + prev * w[2:3]


def _conv_bwd_w(dy, x, shifted):
    prev, nxt = shifted
    return _colsum(dy * prev), _colsum(dy * x), _colsum(dy * nxt)


def _rope(x, cos, sin_lo, sin_hi):
    return x * cos + pltpu.roll(x, HEAD_PAD - 8, 1) * sin_lo + pltpu.roll(x, 8, 1) * sin_hi


ATTN_SCALE = QK_DIM ** -0.5
LOG2_E = 1.4426950408889634


def _rope_t(x, tab, inverse=False):
    o = 3 * HEAD_PAD if inverse else 0
    return _rope(x, tab[:, o:o + HEAD_PAD], tab[:, o + HEAD_PAD:o + 2 * HEAD_PAD], tab[:, o + 2 * HEAD_PAD:o + 3 * HEAD_PAD])


def _heads_keys(hp, kv_ref, kr_ref, tab_ref, kc_ref, vp_ref):
    kr_roped = _rope_t(kr_ref[...], tab_ref[...])
    lane = lax.broadcasted_iota(jnp.int32, kr_roped.shape, 1)
    for u in range(hp):
        kv = kv_ref[:, u * HEAD_PAD:(u + 1) * HEAD_PAD]
        kc_ref[u] = jnp.where(lane < 64, kv, kr_roped).astype(BF16)
        vp_ref[u] = jnp.where(lane >= 64, kv, 0.0).astype(BF16)


ATTN_Q_TILE = 512
ATTN_HEADS_PER_STEP = 2


def _attn_specs(tq, TT):
    q = pl.BlockSpec((tq, HEAD_PAD), lambda h, i: (i, h))
    keys = pl.BlockSpec((TT, HEAD_PAD), lambda h, i: (0, h))
    kr = pl.BlockSpec((TT, HEAD_PAD), lambda h, i: (0, PA_KR0 // HEAD_PAD))
    tab_q = pl.BlockSpec((tq, 6 * HEAD_PAD), lambda h, i: (i, 0))
    tab_k = pl.BlockSpec((TT, 6 * HEAD_PAD), lambda h, i: (0, 0))
    return q, keys, kr, tab_q, tab_k


def _attn_fwd(q_raw, kv, pp, tab, T, TT):
    tq, hp = ROW_TILE, 2 * ATTN_HEADS_PER_STEP
    w = hp * HEAD_PAD

    def body(q_ref, kv_ref, kr_ref, tq_ref, tk_ref, o_ref, kc, vp):
        @pl.when(pl.program_id(1) == 0)
        def _():
            _heads_keys(hp, kv_ref, kr_ref, tk_ref, kc, vp)

        tab = tq_ref[...]
        for u in range(hp):
            cols = slice(u * HEAD_PAD, (u + 1) * HEAD_PAD)
            q = _rope_t(q_ref[:, cols], tab).astype(BF16)
            s = lax.dot_general(q, kc[u], NT, preferred_element_type=F32)
            m = jnp.max(s, axis=-1, keepdims=True)
            p = jnp.exp2((s - m) * (ATTN_SCALE * LOG2_E))
            l = jnp.sum(p, axis=-1, keepdims=True)
            o = lax.dot_general(p.astype(BF16), vp[u], NN, preferred_element_type=F32)
            lane = lax.broadcasted_iota(jnp.int32, o.shape, 1)
            o_ref[:, cols] = jnp.where(lane < 64, m * ATTN_SCALE + jnp.log(l), o / l)

    _, _, kr, _, _ = _attn_specs(tq, TT)
    qs = pl.BlockSpec((tq, w), lambda h, i: (i, h))
    keys = pl.BlockSpec((TT, w), lambda h, i: (0, h))
    tab_q = pl.BlockSpec((tq, 3 * HEAD_PAD), lambda h, i: (i, 0))
    tab_k = pl.BlockSpec((TT, 3 * HEAD_PAD), lambda h, i: (0, 0))
    return pl.pallas_call(
        body, grid=(N_HEADS // hp, T // tq), in_specs=[qs, keys, kr, tab_q, tab_k], out_specs=qs,
        out_shape=jax.ShapeDtypeStruct((T, N_HEADS * HEAD_PAD), F32),
        scratch_shapes=[pltpu.VMEM((hp, TT, HEAD_PAD), BF16), pltpu.VMEM((hp, TT, HEAD_PAD), BF16)],
        compiler_params=_cp(("parallel", "arbitrary")), name="attn_fwd",
    )(*_in_hbm([q_raw, kv, pp, tab, tab]))


def _attn_bwd(q_raw, kv, pp, o, do, tab, T, TT):
    tq = _pick(T, (ATTN_Q_TILE, ROW_TILE))
    nq = T // tq
    hp = ATTN_HEADS_PER_STEP
    w = hp * HEAD_PAD

    def body(q_ref, kv_ref, kr_ref, tq_ref, tk_ref, o_ref, do_ref, dq_ref, dkv_ref, dkr_ref, kc, vp, dk, dv):
        g, i = pl.program_id(0), pl.program_id(1)

        @pl.when(i == 0)
        def _():
            _heads_keys(hp, kv_ref, kr_ref, tk_ref, kc, vp)
            dk[...] = jnp.zeros_like(dk)
            dv[...] = jnp.zeros_like(dv)

        tab = tq_ref[...]
        for u in range(hp):
            cols = slice(u * HEAD_PAD, (u + 1) * HEAD_PAD)
            q = _rope_t(q_ref[:, cols], tab).astype(BF16)
            k, v, d_o = kc[u], vp[u], do_ref[:, cols]
            s = lax.dot_general(q, k, NT, preferred_element_type=F32)
            o = o_ref[:, cols]
            p = jnp.exp2(s * (ATTN_SCALE * LOG2_E) - o[:, 0:1] * LOG2_E)
            dob = d_o.astype(BF16)
            dp = lax.dot_general(dob, v, NT, preferred_element_type=F32)
            dd = jnp.sum(d_o * o, axis=-1, keepdims=True)
            ds = (p * (dp - dd) * ATTN_SCALE).astype(BF16)
            dq = lax.dot_general(ds, k, NN, preferred_element_type=F32)
            dq_ref[:, cols] = _rope_t(dq, tab, inverse=True).astype(dq_ref.dtype)
            dk[u] += lax.dot_general(q, ds, TN, preferred_element_type=F32)
            dv[u] += lax.dot_general(dob, p.astype(BF16), TN, preferred_element_type=F32)

        @pl.when(i == nq - 1)
        def _():
            rot = None
            for u in range(hp):
                dkh = dk[u].T
                lane = lax.broadcasted_iota(jnp.int32, dkh.shape, 1)
                dkv_ref[:, u * HEAD_PAD:(u + 1) * HEAD_PAD] = jnp.where(lane < 64, dkh, dv[u].T).astype(dkv_ref.dtype)
                part = jnp.where((lane >= 64) & (lane < 96), dkh, 0.0)
                rot = part if rot is None else rot + part
            rot = _rope_t(rot, tk_ref[...], inverse=True)

            @pl.when(g == 0)
            def _():
                dkr_ref[...] = rot

            @pl.when(g > 0)
            def _():
                dkr_ref[...] += rot

    _, _, kr, tab_q, tab_k = _attn_specs(tq, TT)
    qs = pl.BlockSpec((tq, w), lambda h, i: (i, h))
    keys = pl.BlockSpec((TT, w), lambda h, i: (0, h))
    wide = lambda rows: jax.ShapeDtypeStruct((rows, N_HEADS * HEAD_PAD), BF16)
    return pl.pallas_call(
        body, grid=(N_HEADS // hp, nq),
        in_specs=[qs, keys, kr, tab_q, tab_k, qs, qs],
        out_specs=[qs, keys, pl.BlockSpec((TT, HEAD_PAD), lambda h, i: (0, 0))],
        out_shape=[wide(T), wide(TT), jax.ShapeDtypeStruct((TT, HEAD_PAD), F32)],
        scratch_shapes=[pltpu.VMEM((hp, TT, HEAD_PAD), BF16), pltpu.VMEM((hp, TT, HEAD_PAD), BF16),
                        pltpu.VMEM((hp, HEAD_PAD, TT), F32), pltpu.VMEM((hp, HEAD_PAD, TT), F32)],
        compiler_params=_cp(("arbitrary", "arbitrary")), name="attn_bwd",
    )(*_in_hbm([q_raw, kv, pp, tab, tab, o, do]))


def _hbm_specs(n):
    return [pl.BlockSpec(memory_space=pl.ANY)] * n


def _gather_weights(shards):
    n = len(shards)
    halves = [s.shape[0] // 2 for s in shards]

    def body(*refs):
        ins, outs = refs[:n], refs[n:2 * n]
        token, send_sems, recv_sems = refs[2 * n:]
        token[...] = jnp.zeros_like(token)
        mx, my, mc = lax.axis_index("x"), lax.axis_index("y"), lax.axis_index("c")
        j_me = 2 * mx + my
        chips = [(1 - mx, my), (mx, 1 - my), (1 - mx, 1 - my)]

        def half(w, chip_idx, hc):
            return outs[w].at[chip_idx, pl.ds(hc * halves[w], halves[w]), :]

        def copy(w, k, src, dst, to):
            return pltpu.make_async_remote_copy(src_ref=src, dst_ref=dst, send_sem=send_sems.at[w, k],
                                                recv_sem=recv_sems.at[w, k], device_id=to, device_id_type=MESH)

        sends = []
        for w in range(n):
            cp = copy(w, 6, ins[w], outs[w].at[j_me], (mx, my, 1 - mc))
            cp.start()
            sends.append(cp)
        for k, (px, py) in enumerate(chips):
            for w in range(n):
                cp = copy(w, k, ins[w].at[pl.ds(mc * halves[w], halves[w]), :], half(w, j_me, mc), (px, py, mc))
                cp.start()
                sends.append(cp)
        for k, (px, py) in enumerate(chips):
            for w in range(n):
                got = half(w, 2 * px + py, mc)
                copy(w, k, got, got, (px, py, mc)).wait_recv()
                cp = copy(w, 3 + k, got, got, (mx, my, 1 - mc))
                cp.start()
                sends.append(cp)
        for k, (px, py) in enumerate(chips):
            for w in range(n):
                got = half(w, 2 * px + py, 1 - mc)
                copy(w, 3 + k, got, got, (mx, my, 1 - mc)).wait_recv()
        for w in range(n):
            own = outs[w].at[j_me]
            copy(w, 6, own, own, (mx, my, 1 - mc)).wait_recv()
        for cp in sends:
            cp.wait_send()

    res = pl.pallas_call(
        body, out_shape=[jax.ShapeDtypeStruct((4,) + s.shape, s.dtype) for s in shards]
        + [jax.ShapeDtypeStruct((8, 128), F32)],
        in_specs=_hbm_specs(n), out_specs=_hbm_specs(n) + [pl.BlockSpec(memory_space=pltpu.VMEM)],
        scratch_shapes=[pltpu.SemaphoreType.DMA((n, 7)), pltpu.SemaphoreType.DMA((n, 7))],
        name="gather_weights")(*shards)
    return list(res[:n]), res[n]


def _rs_pair(gs, name):
    n = len(gs)
    halves = [g.shape[1] // 2 for g in gs]

    def body(*refs):
        ins, lands = refs[:n], refs[n:2 * n]
        send_sems, recv_sems = refs[2 * n:]
        mx, my, mc = lax.axis_index("x"), lax.axis_index("y"), lax.axis_index("c")
        copies = []
        for w in range(n):
            h = halves[w]
            cp = pltpu.make_async_remote_copy(
                src_ref=ins[w].at[:, pl.ds((1 - mc) * h, h), :], dst_ref=lands[w], send_sem=send_sems.at[w],
                recv_sem=recv_sems.at[w], device_id=(mx, my, 1 - mc), device_id_type=MESH)
            cp.start()
            copies.append(cp)
        for cp in copies:
            cp.wait()

    return pl.pallas_call(
        body, out_shape=[jax.ShapeDtypeStruct((4, h, g.shape[2]), g.dtype) for g, h in zip(gs, halves)],
        in_specs=_hbm_specs(n), out_specs=_hbm_specs(n),
        scratch_shapes=[pltpu.SemaphoreType.DMA((n,)), pltpu.SemaphoreType.DMA((n,))], name=name)(*gs)


def _rs_chips(parts):
    n = len(parts)

    def body(*refs):
        ins, lands = refs[:n], refs[n:2 * n]
        send_sems, recv_sems = refs[2 * n:]
        mx, my, mc = lax.axis_index("x"), lax.axis_index("y"), lax.axis_index("c")
        copies = []
        for k, (px, py) in enumerate([(1 - mx, my), (mx, 1 - my), (1 - mx, 1 - my)]):
            for w in range(n):
                cp = pltpu.make_async_remote_copy(
                    src_ref=ins[w].at[2 * px + py], dst_ref=lands[w].at[k], send_sem=send_sems.at[w, k],
                    recv_sem=recv_sems.at[w, k], device_id=(px, py, mc), device_id_type=MESH)
                cp.start()
                copies.append(cp)
        for cp in copies:
            cp.wait()

    return list(pl.pallas_call(
        body, out_shape=[jax.ShapeDtypeStruct((3,) + p.shape[1:], p.dtype) for p in parts],
        in_specs=_hbm_specs(n), out_specs=_hbm_specs(n),
        scratch_shapes=[pltpu.SemaphoreType.DMA((n, 3)), pltpu.SemaphoreType.DMA((n, 3))], name="rs_chips")(*parts))


_HBM = pl.BlockSpec(memory_space=pltpu.HBM)
_SEM = pl.BlockSpec(memory_space=pltpu.SEMAPHORE)
_EFFECT = pltpu.SideEffectType.DATAFLOW_SIDE_EFFECTING


def _ici_copies(kind, srcs, lands, send_sems, recv_sems):
    n = len(lands)
    mx, my, mc = lax.axis_index("x"), lax.axis_index("y"), lax.axis_index("c")
    j_me = 2 * mx + my
    copies = []
    if kind == "back":
        for w in range(n):
            h = lands[w].shape[0] // 2
            mine = lands[w].at[pl.ds(mc * h, h), :]
            copies.append(pltpu.make_async_remote_copy(
                src_ref=mine, dst_ref=mine, send_sem=send_sems.at[w], recv_sem=recv_sems.at[w],
                device_id=(mx, my, 1 - mc), device_id_type=MESH))
        return copies
    if kind == "all":
        for k in range(7):
            a, b, c = (k + 1) >> 2 & 1, (k + 1) >> 1 & 1, (k + 1) & 1
            peer = (1 - mx if a else mx, 1 - my if b else my, 1 - mc if c else mc)
            for w in range(n):
                copies.append(pltpu.make_async_remote_copy(
                    src_ref=srcs[w], dst_ref=lands[w].at[4 * mx + 2 * my + mc], send_sem=send_sems.at[7 * w + k],
                    recv_sem=recv_sems.at[7 * w + k], device_id=peer, device_id_type=MESH))
        return copies
    if kind == "pair":
        for w in range(n):
            h = srcs[w].shape[1] // 2
            copies.append(pltpu.make_async_remote_copy(
                src_ref=srcs[w].at[:, pl.ds((1 - mc) * h, h), :], dst_ref=lands[w], send_sem=send_sems.at[w],
                recv_sem=recv_sems.at[w], device_id=(mx, my, 1 - mc), device_id_type=MESH))
        return copies
    chips = [(1 - mx, my), (mx, 1 - my), (1 - mx, 1 - my)]
    if kind == "finish":
        for w in range(n):
            h = srcs[w].shape[0] // 2
            pushes = [(lands[w].at[2 * px + py, pl.ds(mc * h, h), :],) * 2 for px, py in chips]
            pushes.append((srcs[w], lands[w].at[j_me]))
            for k, (src, dst) in enumerate(pushes):
                copies.append(pltpu.make_async_remote_copy(
                    src_ref=src, dst_ref=dst, send_sem=send_sems.at[4 * w + k], recv_sem=recv_sems.at[4 * w + k],
                    device_id=(mx, my, 1 - mc), device_id_type=MESH))
        return copies
    for k, (px, py) in enumerate(chips):
        for w in range(n):
            if kind == "gather":
                h = srcs[w].shape[0] // 2
                src, dst = srcs[w].at[pl.ds(mc * h, h), :], lands[w].at[j_me, pl.ds(mc * h, h), :]
            else:
                src, dst = srcs[w].at[2 * px + py], lands[w].at[k]
            copies.append(pltpu.make_async_remote_copy(
                src_ref=src, dst_ref=dst, send_sem=send_sems.at[3 * w + k], recv_sem=recv_sems.at[3 * w + k],
                device_id=(px, py, mc), device_id_type=MESH))
    return copies


_SEMS_PER_OPERAND = {"gather": 3, "scatter": 3, "all": 7, "pair": 1, "finish": 4, "back": 1}


def _ici_start(kind, srcs, land_shapes, carry, name, lands=None):
    hbm = lambda a: pltpu.with_memory_space_constraint(a, pltpu.HBM)
    if lands is None:
        lands = [lax.empty(s, srcs[0].dtype) for s in land_shapes]
    ns, nl = len(srcs), len(lands)

    def body(*refs):
        send_sems, recv_sems = refs[ns + nl + 1], refs[ns + nl + 2]
        for cp in _ici_copies(kind, refs[:ns], refs[ns:ns + nl], send_sems, recv_sems):
            cp.start()

    args = [hbm(a) for a in list(srcs) + list(lands) + [carry]]
    n_sem = _SEMS_PER_OPERAND[kind] * nl
    out_shape = ([pltpu.SemaphoreType.DMA((n_sem,)), pltpu.SemaphoreType.DMA((n_sem,))]
                 + [pltpu.HBM(a.shape, a.dtype) for a in args])
    res = pl.pallas_call(
        body, name=name, out_shape=out_shape, in_specs=[_HBM] * len(args), out_specs=[_SEM, _SEM] + [_HBM] * len(args),
        input_output_aliases={i: 2 + i for i in range(len(args))},
        compiler_params=pltpu.CompilerParams(has_side_effects=_EFFECT))(*args)
    return res[0], res[1], list(res[2:2 + ns]), list(res[2 + ns:2 + ns + nl]), res[2 + ns + nl]


def _ici_wait(kind, send_sems, recv_sems, srcs, lands, after, name):
    ns, nl = len(srcs), len(lands)

    def body(*refs):
        for cp in _ici_copies(kind, refs[:ns], refs[ns:ns + nl], refs[ns + nl], refs[ns + nl + 1]):
            cp.wait_send()
            cp.wait_recv()

    args = list(srcs) + list(lands)
    res = pl.pallas_call(
        body, name=name, out_shape=[pltpu.HBM(a.shape, a.dtype) for a in args],
        in_specs=[_HBM] * len(args) + [_SEM, _SEM, pl.BlockSpec(memory_space=pl.ANY)], out_specs=[_HBM] * len(args),
        input_output_aliases={i: i for i in range(len(args))},
        compiler_params=pltpu.CompilerParams(has_side_effects=_EFFECT))(*args, send_sems, recv_sems, after)
    return list(res[:ns]), list(res[ns:])


def _tile_rows(h, c, itemsize, mult):
    best = h
    for t in range(mult, h + 1, mult):
        if h % t == 0 and t * c * itemsize <= (1 << 21):
            best = t
    return best


def _add_pair(g, land, place, name):
    _, h, c = land.shape
    t = _tile_rows(h, c, 2, 16)
    nb = h // t
    return _ew(lambda ids, u, v: (u.astype(F32) + v.astype(F32),), (4, nb),
               [(g, pl.BlockSpec((None, t, c), lambda j, i, s: (j, s[1] * nb + i, 0))),
                (land, pl.BlockSpec((None, t, c), lambda j, i, s: (j, i, 0)))],
               [(land.shape, BF16, pl.BlockSpec((None, t, c), lambda j, i, s: (j, i, 0)), None)], name, scalars=place)[0]


def _add_pair_many(gs, lands, place, name):
    ins, outs = [], []
    for g, l in zip(gs, lands):
        ins += [(g, pl.BlockSpec(l.shape, lambda i, s: (0, s[1], 0))), (l, pl.BlockSpec(l.shape, lambda i, s: (0, 0, 0)))]
        outs.append((l.shape, BF16, pl.BlockSpec(l.shape, lambda i, s: (0, 0, 0)), None))
    fn = lambda ids, *v: [v[2 * k].astype(F32) + v[2 * k + 1].astype(F32) for k in range(len(gs))]
    return list(_ew(fn, (1,), ins, outs, name, scalars=place))


def _add_chips_many(owns, lands, place, name):
    ins, outs = [], []
    for own, land in zip(owns, lands):
        _, h, c = land.shape
        ins += [(own, pl.BlockSpec((None, h, c), lambda i, s: (s[0], 0, 0))),
                (land, pl.BlockSpec((3, h, c), lambda i, s: (0, 0, 0)))]
        outs.append(((2 * h, c), F32, pl.BlockSpec((h, c), lambda i, s: (s[1], 0)), None))

    def fn(ids, *v):
        return [((v[2 * k].astype(F32) + v[2 * k + 1][0].astype(F32)) + v[2 * k + 1][1].astype(F32))
                + v[2 * k + 1][2].astype(F32) for k in range(len(owns))]

    return list(_ew(fn, (1,), ins, outs, name, scalars=place))


def _add_chips(own, land, place, name):
    _, h, c = land.shape
    t = _tile_rows(h, c, 4, 16)
    nb = h // t

    def fn(ids, a, b):
        return (((a.astype(F32) + b[0].astype(F32)) + b[1].astype(F32)) + b[2].astype(F32),)

    return _ew(fn, (nb,), [(own, pl.BlockSpec((None, t, c), lambda i, s: (s[0], i, 0))),
                           (land, pl.BlockSpec((3, t, c), lambda i, s: (0, i, 0)))],
               [((2 * h, c), F32, pl.BlockSpec((t, c), lambda i, s: (s[1] * nb + i, 0)), None)], name, scalars=place)[0]


W_IN_SEGMENTS = ((0, 256, KV0), (256, 288, KR0 + 64), (288, 672, Q0), (672, 1184, CX0), (1184, 1696, CB0),
                 (1696, 2208, CC0), (2208, 3232, GA0), (3232, 4256, GC0))
W_IN_SHARD = 1064


W_IN_SHARD_PAD = 1088
W_IN_EARLY = 672


def _w_in_t_p_from_shards(s):
    pieces = []
    for o0, o1, p0 in sorted(W_IN_SEGMENTS, key=lambda t: t[2]):
        if p0 == KR0 + 64:
            pieces.append(jnp.zeros((64, s.shape[2]), s.dtype))
        for j in range(4):
            lo, hi = max(o0, j * W_IN_SHARD), min(o1, (j + 1) * W_IN_SHARD)
            if lo < hi:
                pieces.append(s[j, lo - j * W_IN_SHARD:hi - j * W_IN_SHARD])
    pieces.append(jnp.zeros((32, s.shape[2]), s.dtype))
    return jnp.concatenate(pieces, axis=0)


def _w_in_t_shards_from_p(g):
    shards = []
    for j in range(4):
        pieces = []
        for o0, o1, p0 in W_IN_SEGMENTS:
            lo, hi = max(o0, j * W_IN_SHARD), min(o1, (j + 1) * W_IN_SHARD)
            if lo < hi:
                pieces.append(g[p0 + lo - o0:p0 + hi - o0])
        pieces.append(jnp.zeros((W_IN_SHARD_PAD - W_IN_SHARD, g.shape[1]), g.dtype))
        shards.append(jnp.concatenate(pieces, axis=0))
    return jnp.stack(shards, axis=0)


def _cols_from_shards(s):
    return jnp.transpose(s, (1, 0, 2)).reshape(s.shape[1], -1)


def _rope_tables(T, TT, inverse):
    f32 = np.float32
    rows = T // GRID_W
    row = np.repeat(np.arange(rows), GRID_W).astype(f32)
    col = np.tile(np.arange(GRID_W), rows).astype(f32)
    inv = (f32(ROPE_THETA) ** (-np.arange(0, 16, 2, dtype=f32) / f32(16))).astype(f32)
    ang = np.concatenate([row[:, None] * inv, col[:, None] * inv], axis=-1).astype(f32)
    cos, sin = np.cos(ang).astype(f32), np.sin(ang).astype(f32)
    lane = np.arange(32)
    src = (lane // 16) * 8 + lane % 8
    lo = ((lane % 16) // 8 == 0).astype(f32)
    sgn = f32(-1.0 if inverse else 1.0)
    cos32 = cos[:, src]
    sin_lo32 = -sgn * sin[:, src] * lo
    sin_hi32 = sgn * sin[:, src] * (1 - lo)

    def widen(t32, fill):
        t = np.concatenate([np.full((T, 64), fill, f32), t32, np.full((T, 32), fill, f32)], axis=1)
        return np.concatenate([t, np.full((TT - T, HEAD_PAD), fill, f32)], axis=0)

    return [widen(cos32, 1.0), widen(sin_lo32, 0.0), widen(sin_hi32, 0.0)]


def _rope_table(T, TT):
    return jnp.asarray(np.concatenate(_rope_tables(T, TT, False) + _rope_tables(T, TT, True), axis=1))


def _local_step(xx, tgt, mod_lat, mod_ctx, W, late_weights, early_grads, early_continue):
    TT = xx.shape[0]
    T = tgt.shape[0]
    n_lat, n_all = T // ROW_TILE, TT // ROW_TILE
    sh1, sc1, g1, sh2, sc2, g2 = [mod_lat[:, k * D_MODEL:(k + 1) * D_MODEL] for k in range(6)]
    csh1, csc1 = mod_ctx[:, :D_MODEL], mod_ctx[:, D_MODEL:2 * D_MODEL]
    vec = lambda n: _full((1, n))
    row_out = lambda n, dt, rows=T: ((rows, n), dt, _rows(n), None)
    acc_out = lambda n: ((1, n), F32, _full((1, n)), 0)
    lt = _pick(T, (2 * ROW_TILE, ROW_TILE))
    n_lt = T // lt
    lrows = lambda n, cblk=0: _rows(n, cblk, 0, lt)
    lrow_out = lambda n, dt: ((T, n), dt, lrows(n), None)

    def f_norm1(ids, x, g, a_sh, a_sc, b_sh, b_sc):
        ctx = ids[0] >= n_lat
        sh, sc = jnp.where(ctx, b_sh, a_sh), jnp.where(ctx, b_sc, a_sc)
        return ((x * _rms(x) * g) * (1.0 + sc) + sh,)

    (hh,) = _ew(f_norm1, (n_all,), [(xx, _rows(D_MODEL)), (W["norm1_g"], vec(D_MODEL)), (sh1, vec(D_MODEL)),
                                   (sc1, vec(D_MODEL)), (csh1, vec(D_MODEL)), (csc1, vec(D_MODEL))],
                [row_out(D_MODEL, BF16, TT)], "norm1_fwd")
    tm_all = _pick(TT, (768, 256))
    pp_a = _mm(hh, W["w_in_a_t"], "nt", TT, PA_COLS, D_MODEL, tm=tm_all, tn=PA_COLS, tk=D_MODEL, name="w_in_fwd_a")

    def f_lowrank(ids, ckv, cq, gkv, gq):
        return ckv * _rms(ckv) * gkv, cq * _rms(cq) * gq

    nkv, nq = _ew(f_lowrank, (n_all,), [(pp_a, _rows(KV_RANK, PA_KV0 // KV_RANK)), (pp_a, _rows(Q_RANK, PA_Q0 // Q_RANK)),
                                       (W["kv_norm_g"], vec(KV_RANK)), (W["q_norm_g"], vec(Q_RANK))],
                  [row_out(KV_RANK, BF16, TT), row_out(Q_RANK, BF16, TT)], "lowrank_norm_fwd")
    kv = _mm(nkv, W["w_ukv"], "nn", TT, 1024, KV_RANK, tm=tm_all, tn=256, tk=KV_RANK, name="w_ukv_fwd",
             b_spec=pl.BlockSpec((None, KV_RANK, 256), lambda i, j, k: (j, k, 0)))
    q_raw = _mm(nq, W["w_uq_t"], "nt", TT, 1024, Q_RANK, tm=tm_all, tn=1024, tk=Q_RANK, name="w_uq_fwd")

    tab = _rope_table(T, TT)
    _, q_raw = late_weights("before_attn", q_raw)
    o_pad = _attn_fwd(q_raw, kv, pp_a, tab, T, TT)
    arrived, o_pad = late_weights("after_attn", o_pad)
    W = dict(W, **arrived)
    tm_lat = _pick(T, (1024, 512, 256))
    pp = _mm(hh, W["w_in_t"], "nt", T, KV0, D_MODEL, tm=tm_lat, tn=KV0 // 2, tk=D_MODEL, name="w_in_fwd_b")
    ya = _mm(o_pad, W["w_attn_out"], "nn", T, D_MODEL, 1024, tm=tm_lat, tn=D_MODEL, tk=1024, name="w_attn_out_fwd",
             out_dtype=BF16)

    tc = 256
    colT = lambda blk0: pl.BlockSpec((T, tc), lambda j: (0, blk0 + j))

    def f_conv(ids, xin, cb, cc, w, b):
        return (cb * _conv(cc * xin, w, b),)

    (e,) = _ew(f_conv, (CONV_DIM // tc,),
               [(pp, colT(CX0 // tc)), (pp, colT(CB0 // tc)), (pp, colT(CC0 // tc)),
                (W["conv_w"], pl.BlockSpec((3, tc), lambda j: (0, j))), (W["conv_b"], pl.BlockSpec((1, tc), lambda j: (0, j)))],
               [((T, CONV_DIM), BF16, colT(0), None)], "conv_fwd")
    yc = _mm(e, W["w_conv_out"], "nn", T, D_MODEL, CONV_DIM, tm=tm_lat, tn=256, tk=CONV_DIM, name="w_conv_out_fwd",
             out_dtype=BF16, b_spec=pl.BlockSpec((None, CONV_DIM, 256), lambda i, j, k: (j, k, 0)))

    def f_merge(ids, ga, gc, a, c):
        return (_sigmoid(ga) * a.astype(F32) + _sigmoid(gc) * c.astype(F32),)

    (mrg,) = _ew(f_merge, (n_lt,), [(pp, lrows(D_MODEL, 0)), (pp, lrows(D_MODEL, 1)), (ya, lrows(D_MODEL)),
                                   (yc, lrows(D_MODEL))], [lrow_out(D_MODEL, BF16)], "merge_fwd")
    mo = _mm(mrg, W["w_o"], "nn", T, D_MODEL, D_MODEL, tm=tm_lat, tn=D_MODEL, tk=D_MODEL, name="w_o_fwd")

    def f_norm2(ids, x, m, gate, g, sh, sc):
        x1 = x + gate * m
        return x1, (x1 * _rms(x1) * g) * (1.0 + sc) + sh

    x1, h2 = _ew(f_norm2, (n_lt,), [(xx, lrows(D_MODEL)), (mo, lrows(D_MODEL)), (g1, vec(D_MODEL)),
                                   (W["norm2_g"], vec(D_MODEL)), (sh2, vec(D_MODEL)), (sc2, vec(D_MODEL))],
                 [lrow_out(D_MODEL, F32), lrow_out(D_MODEL, BF16)], "norm2_fwd")
    arrived, h2 = late_weights("before_ffn", h2)
    W = dict(W, **arrived)
    up = _mm(h2, W["w_up"], "nn", T, 2 * D_FF, D_MODEL, tm=tm_lat, tn=1408, tk=D_MODEL, name="w_up_fwd",
             b_spec=pl.BlockSpec((None, D_MODEL, 1408), lambda i, j, k: (j, k, 0)))

    n_ff = D_FF // tc
    ffw = lambda off, n=3: pl.BlockSpec((n, tc), lambda j: (0, j + off))

    def f_ffn(ids, ug, uv, wg, wv, bg, bv):
        gate, val = _conv(ug, wg, bg), _conv(uv, wv, bv)
        return (gate * _sigmoid(gate) * val,)

    (act,) = _ew(f_ffn, (n_ff,), [(up, colT(0)), (up, colT(n_ff)), (W["ffn_conv_w"], ffw(0)), (W["ffn_conv_w"], ffw(n_ff)),
                                 (W["ffn_conv_b"], ffw(0, 1)), (W["ffn_conv_b"], ffw(n_ff, 1))],
                 [((T, D_FF), BF16, colT(0), None)], "ffn_act_fwd")
    f = _mm(act, W["w_down"], "nn", T, D_MODEL, D_FF, tm=tm_lat, tn=D_MODEL, tk=D_FF, name="w_down_fwd")

    def f_head(ids, x1_, f_, gate, gf, t):
        x2 = x1_ + gate * f_
        r = _rms(x2)
        xn = x2 * r
        err = xn * gf - t
        loss = 0.5 * jnp.sum(jnp.mean(err * err, axis=-1, keepdims=True))
        dy = err * (1.0 / D_MODEL)
        dx2 = _rms_bwd(dy * gf, xn, r)
        return dx2, dx2 * gate, _colsum(dy * xn), _colsum(dx2 * f_), jnp.full((1, 128), loss, F32)

    dx2, df, dg_f, dg2, loss = _ew(
        f_head, (n_lt,), [(x1, lrows(D_MODEL)), (f, lrows(D_MODEL)), (g2, vec(D_MODEL)), (W["final_g"], vec(D_MODEL)),
                          (tgt, lrows(D_MODEL))],
        [lrow_out(D_MODEL, F32), lrow_out(D_MODEL, BF16), acc_out(D_MODEL), acc_out(D_MODEL), acc_out(128)], "loss_head")

    d_w_down = _mm(act, df, "tn", D_FF, D_MODEL, T, tm=1408, tn=D_MODEL, tk=T, name="w_down_dw",
                   out_dtype=BF16).reshape(4, D_FF // 4, D_MODEL)
    da = _mm(df, W["w_down"], "nt", T, D_FF, D_MODEL, tm=tm_lat, tn=1408, tk=D_MODEL, name="w_down_dx")

    tcb = 128
    n_fb = D_FF // tcb
    colb = lambda blk0: pl.BlockSpec((T, tcb), lambda j: (0, blk0 + j))
    ffwb = lambda off, n=3: pl.BlockSpec((n, tcb), lambda j: (0, j + off))
    cvec = ((1, D_FF), F32, pl.BlockSpec((1, tcb), lambda j: (0, j)), None)

    def f_ffn_bwd(ids, ug, uv, d_act, wg, wv, bg, bv):
        sg, sv = _shifts(ug), _shifts(uv)
        gate, val = _conv(ug, wg, bg, sg), _conv(uv, wv, bv, sv)
        s = _sigmoid(gate)
        d_gate = d_act * val * s * (1.0 + gate * (1.0 - s))
        d_val = d_act * gate * s
        wg0, wg1, wg2 = _conv_bwd_w(d_gate, ug, sg)
        wv0, wv1, wv2 = _conv_bwd_w(d_val, uv, sv)
        d_up = [_conv_bwd_x(d_gate, wg), _conv_bwd_x(d_val, wv)]
        return d_up, [_colsum(d_gate), _colsum(d_val), wg0, wg1, wg2, wv0, wv1, wv2]

    d_up3, ffn_stats = _ew(
        f_ffn_bwd, (n_fb,),
        [(up, colb(0)), (up, colb(n_fb)), (da, colb(0)), (W["ffn_conv_w"], ffwb(0)), (W["ffn_conv_w"], ffwb(n_fb)),
         (W["ffn_conv_b"], ffwb(0, 1)), (W["ffn_conv_b"], ffwb(n_fb, 1))],
        [((2, T, D_FF), BF16, pl.BlockSpec((2, T, tcb), lambda j: (0, 0, j)), None),
         ((n_fb, 8, 1, tcb), F32, pl.BlockSpec((None, 8, 1, tcb), lambda j: (j, 0, 0, 0)), None)], "ffn_act_bwd")
    stat = lambda s: ffn_stats[:, s, 0, :].reshape(1, D_FF)
    d_ffn_conv_b = jnp.concatenate([stat(0), stat(1)], axis=1)
    d_ffn_conv_w = jnp.concatenate([jnp.concatenate([stat(2), stat(3), stat(4)], axis=0),
                                    jnp.concatenate([stat(5), stat(6), stat(7)], axis=0)], axis=1)

    tk_t = T
    d_w_up = _mm(h2, d_up3, "tn", D_MODEL, 2 * D_FF, T, tm=D_MODEL, tn=1408, tk=tk_t, name="w_up_dw", out_dtype=BF16,
                 b_spec=pl.BlockSpec((None, tk_t, 1408), lambda i, j, k: (j // 2, k, j % 2)),
                 o_spec=pl.BlockSpec((None, D_MODEL, 1408), lambda i, j, k: (j, i, 0)), out_shape=(4, D_MODEL, 1408))
    dh2 = _mm(d_up3, W["w_up"], "nt", T, D_MODEL, 2 * D_FF, tm=tm_lat, tn=D_MODEL, tk=1408, name="w_up_dx",
              a_spec=pl.BlockSpec((None, tm_lat, 1408), lambda i, j, k: (k // 2, i, k % 2)),
              b_spec=pl.BlockSpec((None, D_MODEL, 1408), lambda i, j, k: (k, j, 0)))

    def f_norm2_bwd(ids, dx2_, dh, x1_, m, g, sc, gate):
        r = _rms(x1_)
        xn = x1_ * r
        dx1 = dx2_ + _rms_bwd(dh * g * (1.0 + sc), xn, r)
        return dx1, dx1 * gate, _colsum(dh), _colsum(dh * xn * g), _colsum(dh * xn * (1.0 + sc)), _colsum(dx1 * m)

    dx1, dmo, dsh2, dsc2, dg_n2, dg1 = _ew(
        f_norm2_bwd, (n_lt,), [(dx2, lrows(D_MODEL)), (dh2, lrows(D_MODEL)), (x1, lrows(D_MODEL)), (mo, lrows(D_MODEL)),
                               (W["norm2_g"], vec(D_MODEL)), (sc2, vec(D_MODEL)), (g1, vec(D_MODEL))],
        [lrow_out(D_MODEL, F32), lrow_out(D_MODEL, BF16)] + [acc_out(D_MODEL)] * 4, "norm2_bwd")
    d_w_o = _mm(mrg, dmo, "tn", D_MODEL, D_MODEL, T, tm=D_MODEL, tn=D_MODEL, tk=tk_t, name="w_o_dw",
                out_dtype=BF16).reshape(4, D_MODEL // 4, D_MODEL)
    dmrg = _mm(dmo, W["w_o"], "nt", T, D_MODEL, D_MODEL, tm=tm_lat, tn=D_MODEL, tk=D_MODEL, name="w_o_dx",
               out_dtype=BF16)
    dmrg = early_grads("late", {"w_o": d_w_o, "w_up": d_w_up, "w_down": d_w_down}, dmrg, split=True)

    def f_merge_bwd(ids, dm, ga, gc, a, c):
        dm, a, c = dm.astype(F32), a.astype(F32), c.astype(F32)
        sa, sc_ = _sigmoid(ga), _sigmoid(gc)
        return dm * sa, dm * sc_, dm * a * sa * (1.0 - sa), dm * c * sc_ * (1.0 - sc_)

    dya, dyc, dp_ga, dp_gc = _ew(
        f_merge_bwd, (n_lt,), [(dmrg, lrows(D_MODEL)), (pp, lrows(D_MODEL, 0)), (pp, lrows(D_MODEL, 1)),
                               (ya, lrows(D_MODEL)), (yc, lrows(D_MODEL))], [lrow_out(D_MODEL, BF16)] * 4, "merge_bwd")
    dya = early_continue("late", dya)

    d_w_ao_p = _mm(o_pad, dya, "tn", 1024, D_MODEL, T, tm=1024, tn=D_MODEL, tk=tk_t, name="w_attn_out_dw", out_dtype=BF16)
    do_pad = _mm(dya, W["w_attn_out"], "nt", T, 1024, D_MODEL, tm=tm_lat, tn=1024, tk=D_MODEL, name="w_attn_out_dx")
    d_w_co = _mm(e, dyc, "tn", CONV_DIM, D_MODEL, T, tm=CONV_DIM, tn=256, tk=tk_t, name="w_conv_out_dw", out_dtype=BF16,
                 o_spec=pl.BlockSpec((None, CONV_DIM, 256), lambda i, j, k: (j, i, 0)), out_shape=(4, CONV_DIM, 256))
    de = _mm(dyc, W["w_conv_out"], "nt", T, CONV_DIM, D_MODEL, tm=tm_lat, tn=CONV_DIM, tk=256, name="w_conv_out_dx",
             b_spec=pl.BlockSpec((None, CONV_DIM, 256), lambda i, j, k: (k, j, 0)))

    def f_conv_bwd(ids, xin, cb, cc, d_e, w, b):
        z = cc * xin
        sz = _shifts(z)
        cz = _conv(z, w, b, sz)
        dcz = d_e * cb
        w0, w1, w2 = _conv_bwd_w(dcz, z, sz)
        dz = _conv_bwd_x(dcz, w)
        return dz * cc, d_e * cz, dz * xin, _colsum(dcz), w0, w1, w2

    cvec_c = ((1, CONV_DIM), F32, pl.BlockSpec((1, tc), lambda j: (0, j)), None)
    conv_b = _ew(f_conv_bwd, (CONV_DIM // tc,),
                 [(pp, colT(CX0 // tc)), (pp, colT(CB0 // tc)), (pp, colT(CC0 // tc)), (de, colT(0)),
                  (W["conv_w"], pl.BlockSpec((3, tc), lambda j: (0, j))), (W["conv_b"], pl.BlockSpec((1, tc), lambda j: (0, j)))],
                 [((T, CONV_DIM), BF16, colT(0), None)] * 3 + [cvec_c] * 4, "conv_bwd")
    dp_cx, dp_cb, dp_cc, d_conv_b = conv_b[:4]
    d_conv_w = jnp.concatenate(conv_b[4:7], axis=0)

    dq_raw, dkv, dp_kr = _attn_bwd(q_raw, kv, pp_a, o_pad, do_pad, tab, T, TT)

    tk_a = TT
    d_w_uq_t = _mm(nq, dq_raw, "tn", Q_RANK, 1024, T, tm=Q_RANK, tn=1024, tk=T, name="w_uq_dw", transpose_out=True)
    dnq = _mm(dq_raw, W["w_uq_t"], "nn", T, Q_RANK, 1024, tm=tm_lat, tn=Q_RANK, tk=1024, name="w_uq_dx")
    d_w_ukv = _mm(nkv, dkv, "tn", KV_RANK, 1024, TT, tm=KV_RANK, tn=256, tk=tk_a, name="w_ukv_dw", out_dtype=BF16,
                  o_spec=pl.BlockSpec((None, KV_RANK, 256), lambda i, j, k: (j, i, 0)), out_shape=(4, KV_RANK, 256))
    dnkv = _mm(dkv, W["w_ukv"], "nt", TT, KV_RANK, 1024, tm=tm_all, tn=KV_RANK, tk=256, name="w_ukv_dx",
               b_spec=pl.BlockSpec((None, KV_RANK, 256), lambda i, j, k: (k, j, 0)))
    dnkv = early_grads("mid", {
        "w_attn_out": jnp.transpose(d_w_ao_p.reshape(N_HEADS, HEAD_PAD, 4, 256)[:, 64:], (2, 0, 1, 3)).reshape(
            4, N_HEADS * 64, 256),
        "w_conv_out": d_w_co,
        "w_uq": d_w_uq_t.reshape(4, 2, HEAD_PAD, Q_RANK)[:, :, :QK_DIM].reshape(4, 2 * QK_DIM, Q_RANK).astype(BF16),
        "w_ukv": d_w_ukv}, dnkv)

    def f_lowrank_bwd(ids, ckv, cq, dkv_, dq_, gkv, gq, ga, gc, cx, cb, cc, kr):
        rk, rq = _rms(ckv), _rms(cq)
        nk, nq_ = ckv * rk, cq * rq
        lat = ids[0] < n_lat
        dq_ = jnp.where(lat, dq_, 0.0)
        pieces = [jnp.where(lat, a, jnp.zeros_like(a)) for a in (ga, gc, cx, cb, cc)]
        pieces += [_rms_bwd(dkv_ * gkv, nk, rk).astype(BF16), _rms_bwd(dq_ * gq, nq_, rq).astype(BF16), kr.astype(BF16)]
        return jnp.concatenate(pieces, axis=1), _colsum(dkv_ * nk), _colsum(dq_ * nq_)

    lat_rows = lambda n: pl.BlockSpec((ROW_TILE, n), lambda i: (jnp.minimum(i, n_lat - 1), 0))
    dpp, dg_kv, dg_q = _ew(
        f_lowrank_bwd, (n_all,), [(pp_a, _rows(KV_RANK, PA_KV0 // KV_RANK)), (pp_a, _rows(Q_RANK, PA_Q0 // Q_RANK)),
                                  (dnkv, _rows(KV_RANK)), (dnq, lat_rows(Q_RANK)), (W["kv_norm_g"], vec(KV_RANK)),
                                  (W["q_norm_g"], vec(Q_RANK)), (dp_ga, lat_rows(D_MODEL)), (dp_gc, lat_rows(D_MODEL)),
                                  (dp_cx, lat_rows(CONV_DIM)), (dp_cb, lat_rows(CONV_DIM)), (dp_cc, lat_rows(CONV_DIM)),
                                  (dp_kr, _rows(HEAD_PAD))],
        [row_out(P_COLS, BF16, TT), acc_out(KV_RANK), acc_out(Q_RANK)], "lowrank_norm_bwd")
    d_w_in_t = _mm(hh, dpp, "tn", D_MODEL, P_COLS, TT, tm=512, tn=2176, tk=TT, name="w_in_dw", out_dtype=BF16,
                   transpose_out=True)
    dhh = _mm(dpp, W["w_in_t"], "nn", TT, D_MODEL, P_COLS, tm=tm_all, tn=512, tk=2176, name="w_in_dx")

    def f_norm1_bwd(ids, x, dh, dres, g, sc):
        r = _rms(x)
        xn = x * r
        return (dres + _rms_bwd(dh * g * (1.0 + sc), xn, r), _colsum(dh), _colsum(dh * xn * g),
                _colsum(dh * xn * (1.0 + sc)))

    grad_x, dsh1, dsc1, dg_n1 = _ew(
        f_norm1_bwd, (n_lt,), [(xx, lrows(D_MODEL)), (dhh, lrows(D_MODEL)), (dx1, lrows(D_MODEL)),
                               (W["norm1_g"], vec(D_MODEL)), (sc1, vec(D_MODEL))],
        [lrow_out(D_MODEL, F32)] + [acc_out(D_MODEL)] * 3, "norm1_bwd")

    def f_norm1_ctx_bwd(ids, x, dh, g, sc):
        xn = x * _rms(x)
        return _colsum(dh), _colsum(dh * xn * g), _colsum(dh * xn * (1.0 + sc))

    n_ctx = n_all - n_lat
    dcsh1, dcsc1, dg_n1c = _ew(
        f_norm1_ctx_bwd, (n_ctx,), [(xx, _rows(D_MODEL, 0, n_lat)), (dhh, _rows(D_MODEL, 0, n_lat)),
                                    (W["norm1_g"], vec(D_MODEL)), (csc1, vec(D_MODEL))], [acc_out(D_MODEL)] * 3,
        "norm1_ctx_bwd")

    big = {"w_in": _w_in_t_shards_from_p(d_w_in_t).astype(BF16)}
    zero = jnp.zeros((1, 4 * D_MODEL), F32)
    small = {
        "dmod_lat": jnp.concatenate([dsh1, dsc1, dg1, dsh2, dsc2, dg2], axis=1),
        "dmod_ctx": jnp.concatenate([dcsh1, dcsc1, zero], axis=1),
        "norm1_g": dg_n1 + dg_n1c, "norm2_g": dg_n2, "final_g": dg_f, "q_norm_g": dg_q, "kv_norm_g": dg_kv,
        "conv_b": d_conv_b, "conv_w": d_conv_w.reshape(1, -1), "ffn_conv_b": d_ffn_conv_b,
        "ffn_conv_w": d_ffn_conv_w.reshape(1, -1),
    }
    return grad_x, loss, big, small


SMALL = (("dmod_lat", 6144), ("dmod_ctx", 6144), ("norm1_g", 1024), ("norm2_g", 1024), ("final_g", 1024),
         ("q_norm_g", 384), ("kv_norm_g", 256), ("conv_b", 512), ("conv_w", 1536), ("ffn_conv_b", 5632),
         ("ffn_conv_w", 16896), ("loss", 128))
SMALL_ROWS = 320


def _adam_update(w, g, m, v):
    c1, c2 = 1.0 - ADAM_B1 ** ADAM_STEP, 1.0 - ADAM_B2 ** ADAM_STEP
    m2 = ADAM_B1 * m + (1.0 - ADAM_B1) * g
    v2 = ADAM_B2 * v + (1.0 - ADAM_B2) * (g * g)
    return [-ADAM_LR * ((m2 / c1) / (jnp.sqrt(v2 / c2) + ADAM_EPS) + ADAM_WD * w), m2, v2]


def _adamw(w, g, m, v, name):
    R, C = w.shape
    tr = 8 if R % 8 == 0 else R
    for t in range(8, R + 1, 8):
        if R % t == 0 and t * C * 4 <= (1 << 21):
            tr = t
    spec = pl.BlockSpec((tr, C), lambda i: (i, 0))
    return _ew(lambda ids, *vals: [vals[1]] + _adam_update(*vals), (R // tr,),
               [(w, spec), (g, spec), (m, spec), (v, spec)], [((R, C), F32, spec, None)] * 4, name)


def kernel(x, c, ctx, c_ctx, w_ada, b_ada, norm1_g, w_in, q_norm_g, kv_norm_g, w_uq, w_ukv, conv_w, conv_b, w_attn_out, w_conv_out, w_o, norm2_g, w_up, ffn_conv_w, ffn_conv_b, w_down, final_g, loss_target, m_c_ctx, m_w_ada, m_b_ada, m_norm1_g, m_w_in, m_q_norm_g, m_kv_norm_g, m_w_uq, m_w_ukv, m_conv_w, m_conv_b, m_w_attn_out, m_w_conv_out, m_w_o, m_norm2_g, m_w_up, m_ffn_conv_w, m_ffn_conv_b, m_w_down, m_final_g, v_c_ctx, v_w_ada, v_b_ada, v_norm1_g, v_w_in, v_q_norm_g, v_kv_norm_g, v_w_uq, v_w_ukv, v_conv_w, v_conv_b, v_w_attn_out, v_w_conv_out, v_w_o, v_norm2_g, v_w_up, v_ffn_conv_w, v_ffn_conv_b, v_w_down, v_final_g):
    mx, my, mc = lax.axis_index("x"), lax.axis_index("y"), lax.axis_index("c")
    chip = 2 * mx + my
    dev = 4 * mx + 2 * my + mc
    T, Tc = x.shape[1], ctx.shape[1]
    TT = T + Tc
    w_in_t, m_w_in_t, v_w_in_t = (jnp.transpose(a[0]) for a in (w_in, m_w_in, v_w_in))
    w_uq_t, m_w_uq_t, v_w_uq_t = (jnp.transpose(a[0]) for a in (w_uq, m_w_uq, v_w_uq))
    conv_sh = jnp.concatenate([conv_w[0], ffn_conv_w[0]], axis=1)
    pay1 = jnp.concatenate([jnp.pad(c, ((0, 7), (0, 0))), jnp.pad(conv_sh, ((0, 5), (0, 0)))], axis=1)
    c_send, c_recv, c_src, c_land, zero0 = _ici_start("all", [pay1], [(8, 8, 2560)], jnp.zeros((8, 128), F32),
                                                      "cond_start")
    w_in_bf = (jnp.pad(w_in_t, ((0, W_IN_SHARD_PAD - W_IN_SHARD), (0, 0))) + zero0[0, 0]).astype(BF16)
    shards = {"w_in_a": w_in_bf[:W_IN_EARLY], "w_in_b": w_in_bf[W_IN_EARLY:], "w_uq": w_uq_t, "w_ukv": w_ukv[0],
              "w_attn_out": w_attn_out[0], "w_conv_out": w_conv_out[0], "w_o": w_o[0], "w_up": w_up[0],
              "w_down": w_down[0]}
    (pay1,), (c_land,) = _ici_wait("all", c_send, c_recv, c_src, c_land, w_in_bf, "cond_wait")
    got1 = lax.dynamic_update_slice(c_land, pay1[None], (dev, 0, 0))
    c_all = got1[:, 0, :D_MODEL]
    conv_all = got1[0::2, :3, D_MODEL:]
    conv_w_full = _cols_from_shards(conv_all[:, :, :128])
    ffn_conv_w_full = _cols_from_shards(conv_all[:, :, 128:])

    cond = jnp.concatenate([c_all, c_ctx.reshape(1, D_MODEL), jnp.zeros((7, D_MODEL), F32)], axis=0)

    def f_silu(ids, v):
        return (v * _sigmoid(v),)

    (s16,) = _ew(f_silu, (1,), [(cond, _full((16, D_MODEL)))], [((16, D_MODEL), F32, _full((16, D_MODEL)), None)], "silu_cond")
    mod_sh = _mm(s16, w_ada[0], "nn", 16, 1536, D_MODEL, tm=16, tn=768, tk=D_MODEL, name="w_ada_fwd")
    m_send, m_recv, m_src, m_land, zero1 = _ici_start("all", [mod_sh], [(8, 16, 1536)], jnp.zeros((8, 128), F32),
                                                      "mod_start")
    shards["w_ukv"] = w_ukv[0] + zero1[0, 0]

    first = ["w_in_a", "w_uq", "w_ukv"]
    gathered, zero = _gather_weights([shards[n].astype(BF16) for n in first])
    full = dict(zip(first, gathered))
    (mod_mine,), (m_land,) = _ici_wait("all", m_send, m_recv, m_src, m_land, gathered[0], "mod_wait")
    got2 = lax.dynamic_update_slice(m_land, mod_mine[None], (dev, 0, 0))
    mod_all = _cols_from_shards(got2[0::2]) + b_ada
    mod_lat = lax.dynamic_slice_in_dim(mod_all, dev, 1, axis=0)
    mod_ctx = mod_all[8:9]
    xx = jnp.concatenate([x[0], ctx[0]], axis=0)
    late_groups = {"g1": ("w_in_b", "w_attn_out", "w_conv_out", "w_o"), "g2": ("w_up", "w_down")}
    flight = {}
    for tag, group in late_groups.items():
        bf = [(shards[n] + zero[0, 0]).astype(BF16) for n in group]
        flight[tag] = _ici_start("gather", bf, [(4,) + s.shape for s in bf], xx, "gather_" + tag + "_start")
        xx = flight[tag][4]

    def chip_stage_done(tag, x):
        send, recv, src, land, _ = flight[tag]
        src, land = _ici_wait("gather", send, recv, src, land, x, "gather_" + tag + "_wait")
        flight[tag] = _ici_start("finish", src, None, x, "finish_" + tag + "_start", lands=land)
        return flight[tag][4]

    def arrived(tag, x):
        send, recv, src, land, _ = flight[tag]
        return dict(zip(late_groups[tag], _ici_wait("finish", send, recv, src, land, x, "finish_" + tag + "_wait")[1]))

    def late_weights(point, x):
        if point == "before_attn":
            return {}, chip_stage_done("g1", x)
        if point == "after_attn":
            got = arrived("g1", x)
            wao = _cols_from_shards(got["w_attn_out"]).reshape(N_HEADS, 64, D_MODEL)
            w_in_all = jnp.concatenate([full["w_in_a"], got["w_in_b"]], axis=1)
            ready = {"w_in_t": _w_in_t_p_from_shards(w_in_all),
                     "w_attn_out": jnp.pad(wao, ((0, 0), (64, 0), (0, 0))).reshape(N_HEADS * HEAD_PAD, D_MODEL),
                     "w_conv_out": got["w_conv_out"], "w_o": got["w_o"].reshape(D_MODEL, D_MODEL)}
            return ready, chip_stage_done("g2", x)
        got = arrived("g2", x)
        return {"w_up": got["w_up"], "w_down": got["w_down"].reshape(D_FF, D_MODEL)}, x

    wuq_t = full["w_uq"].reshape(N_HEADS, QK_DIM, Q_RANK)
    early_rows = full["w_in_a"][0]
    zrows = lambda n: jnp.zeros((n, D_MODEL), BF16)
    W = {
        "w_in_a_t": jnp.concatenate([early_rows[0:256], zrows(PA_Q0 - 256), early_rows[288:672], zrows(64),
                                     early_rows[256:288], zrows(32)], axis=0),
        "w_uq_t": jnp.pad(wuq_t, ((0, 0), (0, HEAD_PAD - QK_DIM), (0, 0))).reshape(N_HEADS * HEAD_PAD, Q_RANK),
        "w_ukv": full["w_ukv"],
        "norm1_g": norm1_g, "norm2_g": norm2_g, "final_g": final_g.reshape(1, D_MODEL), "q_norm_g": q_norm_g,
        "kv_norm_g": kv_norm_g, "conv_w": conv_w_full, "conv_b": conv_b, "ffn_conv_w": ffn_conv_w_full,
        "ffn_conv_b": ffn_conv_b,
    }

    place = jnp.stack([chip, mc]).astype(jnp.int32)
    early = {}

    pending = {}

    def scatter(tag, group, gs, from_sib, carry):
        if tag == "mid":
            sums = _add_pair_many(gs, from_sib, place, "rs_pair_add_mid")
        else:
            sums = [_add_pair(gs[w], from_sib[w], place, "rs_pair_add_" + n) for w, n in enumerate(group)]
        send, recv, sums, land, carry = _ici_start(
            "scatter", sums, [(3,) + s.shape[1:] for s in sums], carry, "rs_chips_" + tag + "_start")
        early[tag] = (group, send, recv, sums, land)
        return carry

    def early_grads(tag, g, carry, split=False):
        gs = list(g.values())
        if not split:
            return scatter(tag, list(g), gs, _rs_pair(gs, "rs_pair_" + tag), carry)
        send, recv, gs, land, carry = _ici_start(
            "pair", gs, [(4, s.shape[1] // 2, s.shape[2]) for s in gs], carry, "rs_pair_" + tag + "_start")
        pending[tag] = (list(g), send, recv, gs, land)
        return carry

    def early_continue(tag, carry):
        group, send, recv, gs, land = pending[tag]
        gs, from_sib = _ici_wait("pair", send, recv, gs, land, carry, "rs_pair_" + tag + "_wait")
        return scatter(tag, group, gs, from_sib, carry)

    grad_x, loss_part, gbig, gsmall = _local_step(xx, loss_target[0], mod_lat, mod_ctx, W, late_weights, early_grads,
                                                  early_continue)

    gsmall["loss"] = loss_part
    pay3 = jnp.concatenate([gsmall[n].reshape(-1) for n, _ in SMALL])
    pay3 = jnp.pad(pay3, (0, SMALL_ROWS * 128 - pay3.shape[0])).reshape(SMALL_ROWS, 128)
    s_send, s_recv, s_src, s_land, w_in_thru = _ici_start("all", [pay3], [(8, SMALL_ROWS, 128)], gbig["w_in"],
                                                         "small_start")
    gbig = {"w_in": w_in_thru}

    after_small = early_grads("last", gbig, s_src[0])

    (pay3,), (s_land,) = _ici_wait("all", s_send, s_recv, [after_small], s_land, early["last"][3][0], "small_wait")
    got3 = lax.dynamic_update_slice(s_land, pay3[None], (dev, 0, 0)).reshape(8 * SMALL_ROWS, 128)

    def f_sum8(ids, a):
        s = a[0:SMALL_ROWS]
        for d in range(1, 8):
            s = s + a[d * SMALL_ROWS:(d + 1) * SMALL_ROWS]
        return (s,)

    (vsum,) = _ew(f_sum8, (1,), [(got3, _full((8 * SMALL_ROWS, 128)))],
                  [((SMALL_ROWS, 128), F32, _full((SMALL_ROWS, 128)), None)], "sum_small")
    vflat = vsum.reshape(-1)
    gvec, off = {}, 0
    for n, size in SMALL:
        gvec[n] = vflat[off:off + size]
        off += size
    loss = gvec["loss"][0]
    dmod_rows = got3.reshape(8, SMALL_ROWS * 128)[:, :6 * D_MODEL]
    dm16 = jnp.concatenate([dmod_rows, gvec["dmod_ctx"].reshape(1, -1), jnp.zeros((7, 6 * D_MODEL), F32)], axis=0)

    def f_colsum(ids, a):
        return (_colsum(a),)

    (g_b_ada,) = _ew(f_colsum, (1,), [(dm16, _full((16, 6 * D_MODEL)))],
                     [((1, 6 * D_MODEL), F32, _full((1, 6 * D_MODEL)), None)], "b_ada_grad")
    dm_sh = lax.dynamic_slice_in_dim(dm16, chip * 1536, 1536, axis=1)
    g_w_ada = _mm(s16, dm_sh, "tn", D_MODEL, 1536, 16, tm=512, tn=768, tk=16, name="w_ada_dw")
    dcond_part = _mm(dm_sh, w_ada[0], "nt", 16, D_MODEL, 1536, tm=16, tn=512, tk=1536, name="w_ada_dx")
    d_send, d_recv, d_src, d_land, vsum = _ici_start("all", [dcond_part[8:16]], [(8, 8, D_MODEL)], vsum, "dcond_start")

    def finish_start(tags, after):
        done, halves = [], []
        for tag in tags:
            tag_names, send, recv, sums, land = early[tag]
            sums, land = _ici_wait("scatter", send, recv, sums, land, after, "rs_chips_" + tag + "_wait")
            done += tag_names
            if tag == "mid":
                halves += _add_chips_many(sums, land, place, "rs_chip_add_mid")
            else:
                halves += [_add_chips(a, b, place, "rs_chip_add_" + n) for a, b, n in zip(sums, land, tag_names)]
        send, recv, _, halves, _ = _ici_start("back", [], None, jnp.zeros((8, 128), F32), "rs_back_" + tags[0] + "_start",
                                              lands=halves)
        return done, send, recv, halves

    def finish_wait(state, after):
        done, send, recv, halves = state
        return dict(zip(done, _ici_wait("back", send, recv, [], halves, after, "rs_back_" + done[0] + "_wait")[1]))

    grads, deltas, new_m, new_v = {}, {}, {}, {}

    raw = {}

    def adam(n, w_, m_, v_, g, transposed):
        g_out, d_, m2, v2 = _adamw(w_, g, m_, v_, "adamw_" + n)
        raw[n] = d_
        back = (lambda a: jnp.transpose(a)[None]) if transposed else (lambda a: a[None])
        grads[n], deltas[n], new_m[n], new_v[n] = back(g_out), back(d_), back(m2), back(v2)

    pending_back = finish_start(["late", "mid"], grad_x)
    adam("w_ada", w_ada[0], m_w_ada[0], v_w_ada[0], g_w_ada, False)
    gw = finish_wait(pending_back, raw["w_ada"])
    for n, (w_, m_, v_) in {"w_o": (w_o, m_w_o, v_w_o), "w_up": (w_up, m_w_up, v_w_up),
                            "w_down": (w_down, m_w_down, v_w_down)}.items():
        adam(n, w_[0], m_[0], v_[0], gw[n], False)
    pending_back = finish_start(["last"], raw["w_up"])

    (dcond_mine,), (d_land,) = _ici_wait("all", d_send, d_recv, d_src, d_land, raw["w_down"], "dcond_wait")
    got4 = lax.dynamic_update_slice(d_land, dcond_mine[None], (dev, 0, 0))[0::2, 0]

    def f_c_ctx(ids, parts, cc):
        s = _sigmoid(cc)
        d = parts[0:1] + parts[1:2] + parts[2:3] + parts[3:4]
        return (d * s * (1.0 + cc * (1.0 - s)),)

    (g_c_ctx,) = _ew(f_c_ctx, (1,), [(got4, _full((4, D_MODEL))), (c_ctx.reshape(1, D_MODEL), _full((1, D_MODEL)))],
                     [((1, D_MODEL), F32, _full((1, D_MODEL)), None)], "c_ctx_grad")

    conv_w_g = lax.dynamic_slice_in_dim(gvec["conv_w"].reshape(3, CONV_DIM), chip * 128, 128, axis=1)
    ffn_conv_w_g = lax.dynamic_slice_in_dim(gvec["ffn_conv_w"].reshape(3, 2 * D_FF), chip * 1408, 1408, axis=1)
    vec_params = (("c_ctx", c_ctx, m_c_ctx, v_c_ctx, g_c_ctx), ("b_ada", b_ada, m_b_ada, v_b_ada, g_b_ada),
                  ("norm1_g", norm1_g, m_norm1_g, v_norm1_g, gvec["norm1_g"]),
                  ("q_norm_g", q_norm_g, m_q_norm_g, v_q_norm_g, gvec["q_norm_g"]),
                  ("kv_norm_g", kv_norm_g, m_kv_norm_g, v_kv_norm_g, gvec["kv_norm_g"]),
                  ("conv_w", conv_w, m_conv_w, v_conv_w, conv_w_g), ("conv_b", conv_b, m_conv_b, v_conv_b, gvec["conv_b"]),
                  ("norm2_g", norm2_g, m_norm2_g, v_norm2_g, gvec["norm2_g"]),
                  ("ffn_conv_w", ffn_conv_w, m_ffn_conv_w, v_ffn_conv_w, ffn_conv_w_g),
                  ("ffn_conv_b", ffn_conv_b, m_ffn_conv_b, v_ffn_conv_b, gvec["ffn_conv_b"]),
                  ("final_g", final_g, m_final_g, v_final_g, gvec["final_g"]))
    two_d = lambda a: a.reshape((-1, a.shape[-1]))
    many = [p + ((lambda r, s=p[1].shape: r.reshape(s)),) for p in vec_params]
    for n, w_, m_, v_ in (("w_ukv", w_ukv, m_w_ukv, v_w_ukv), ("w_attn_out", w_attn_out, m_w_attn_out, v_w_attn_out),
                          ("w_conv_out", w_conv_out, m_w_conv_out, v_w_conv_out)):
        many.append((n, w_, m_, v_, gw[n], (lambda r, s=w_.shape: r.reshape(s))))
    many.append(("w_uq", w_uq_t, m_w_uq_t, v_w_uq_t, gw["w_uq"], lambda r: jnp.transpose(r)[None]))

    def f_adam_many(ids, *vals):
        out = []
        for k in range(len(many)):
            out += [vals[4 * k + 1]] + _adam_update(*vals[4 * k:4 * k + 4])
        return out

    ins_v, outs_v = [], []
    for p in many:
        shp = two_d(p[1]).shape
        ins_v += [(two_d(a), _full(shp)) for a in (p[1], p[4], p[2], p[3])]
        outs_v += [(shp, F32, _full(shp), None)] * 4
    res_v = _ew(f_adam_many, (1,), ins_v, outs_v, "adamw_small")
    for k, p in enumerate(many):
        n, post = p[0], p[5]
        grads[n], deltas[n], new_m[n], new_v[n] = (post(r) for r in res_v[4 * k:4 * k + 4])

    gw_in = finish_wait(pending_back, res_v[0])
    adam("w_in", w_in_t, m_w_in_t, v_w_in_t, gw_in["w_in"], True)

    order = ("c_ctx", "w_ada", "b_ada", "norm1_g", "w_in", "q_norm_g", "kv_norm_g", "w_uq", "w_ukv", "conv_w", "conv_b",
             "w_attn_out", "w_conv_out", "w_o", "norm2_g", "w_up", "ffn_conv_w", "ffn_conv_b", "w_down", "final_g")
    return (loss, grad_x[None], *[grads[n] for n in order], *[deltas[n] for n in order],
            *[new_m[n] for n in order], *[new_v[n] for n in order])
```

```python
import functools

import jax
import jax.numpy as jnp
import numpy as np
from jax import lax
from jax.experimental import pallas as pl
from jax.experimental.pallas import tpu as pltpu

F32, BF16 = jnp.float32, jnp.bfloat16
MESH = pl.DeviceIdType.MESH

D_MODEL = 1024
N_HEADS = 8
HEAD_PAD = 128
QK_DIM = 96
Q_RANK, KV_RANK = 384, 256
CONV_DIM = 512
D_FF = 2816
GRID_W = 64
ROPE_THETA = 10000.0
EPS = 1e-6
GA0, GC0, CX0, CB0, CC0, KV0, Q0, KR0, P_COLS = 0, 1024, 2048, 2560, 3072, 3584, 3840, 4224, 4352
PA_KV0, PA_Q0, PA_KR0, PA_COLS = 0, 384, 768, 896
ROW_TILE = 256
VMEM_LIMIT_BYTES = 48 * 1024 * 1024

ADAM_LR, ADAM_B1, ADAM_B2, ADAM_EPS, ADAM_WD, ADAM_STEP = 0.001, 0.9, 0.999, 1e-08, 0.01, 10

NN = (((1,), (0,)), ((), ()))
NT = (((1,), (1,)), ((), ()))
TN = (((0,), (0,)), ((), ()))


def _cp(sem):
    return pltpu.CompilerParams(dimension_semantics=sem, vmem_limit_bytes=VMEM_LIMIT_BYTES)


PIN_BYTES = 1 << 19


def _in_hbm(arrays):
    return [pltpu.with_memory_space_constraint(a, pltpu.HBM) if a.size * a.dtype.itemsize >= PIN_BYTES else a
            for a in arrays]


def _out(shape, dtype):
    n = 1
    for d in shape:
        n *= d
    big = n * jnp.dtype(dtype).itemsize >= PIN_BYTES
    return pltpu.HBM(shape, dtype) if big else jax.ShapeDtypeStruct(shape, dtype)


def _pick(n, prefs):
    for p in prefs:
        if n % p == 0:
            return p
    return n


def _mm(a, b, mode, M, N, K, *, tm, tn, tk, name, out_dtype=F32, a_spec=None, b_spec=None, o_spec=None,
        out_shape=None, transpose_out=False):
    assert M % tm == 0 and N % tn == 0 and K % tk == 0, (name, M, N, K, tm, tn, tk)
    nk = K // tk
    dims = {"nn": NN, "nt": NT, "tn": TN}[mode]
    if a_spec is None:
        a_spec = (pl.BlockSpec((tk, tm), lambda i, j, k: (k, i)) if mode == "tn"
                  else pl.BlockSpec((tm, tk), lambda i, j, k: (i, k)))
    if b_spec is None:
        b_spec = (pl.BlockSpec((tn, tk), lambda i, j, k: (j, k)) if mode == "nt"
                  else pl.BlockSpec((tk, tn), lambda i, j, k: (k, j)))
    if o_spec is None:
        o_spec = (pl.BlockSpec((tn, tm), lambda i, j, k: (j, i)) if transpose_out
                  else pl.BlockSpec((tm, tn), lambda i, j, k: (i, j)))
    if out_shape is None:
        out_shape = (N, M) if transpose_out else (M, N)

    def emit(o_ref, val):
        o_ref[...] = (val.T if transpose_out else val).astype(o_ref.dtype)

    def body(a_ref, b_ref, o_ref, *scratch):
        part = lax.dot_general(a_ref[...].astype(BF16), b_ref[...].astype(BF16), dims, preferred_element_type=F32)
        if nk == 1:
            emit(o_ref, part)
            return
        acc_ref, = scratch
        k = pl.program_id(2)

        @pl.when(k == 0)
        def _():
            acc_ref[...] = part

        @pl.when((k > 0) & (k < nk - 1))
        def _():
            acc_ref[...] += part

        @pl.when(k == nk - 1)
        def _():
            emit(o_ref, acc_ref[...] + part)

    return pl.pallas_call(
        body, grid=(M // tm, N // tn, nk), in_specs=[a_spec, b_spec], out_specs=o_spec,
        out_shape=_out(out_shape, out_dtype),
        scratch_shapes=[pltpu.VMEM((tm, tn), F32)] if nk > 1 else [],
        compiler_params=_cp(("parallel", "parallel", "arbitrary")), name=name)(*_in_hbm([a, b]))


def _ew(fn, grid, ins, outs, name, scalars=None):
    n_in = len(ins)
    n_sc = 0 if scalars is None else 1

    def store(ref, val, acc, ids):
        if isinstance(val, (list, tuple)):
            for h, v in enumerate(val):
                ref[h] = v.astype(ref.dtype)
            return
        if acc is None:
            ref[...] = val.astype(ref.dtype)
            return

        @pl.when(ids[acc] == 0)
        def _():
            ref[...] = val.astype(ref.dtype)

        @pl.when(ids[acc] > 0)
        def _():
            ref[...] += val.astype(ref.dtype)

    def body(*refs):
        refs = refs[n_sc:]
        ids = tuple(pl.program_id(a) for a in range(len(grid)))
        vals = fn(ids, *[r[...] for r in refs[:n_in]])
        for ref, val, (_, _, _, acc) in zip(refs[n_in:], vals, outs):
            store(ref, val, acc, ids)

    acc_axes = {o[3] for o in outs if o[3] is not None}
    sem = tuple("arbitrary" if a in acc_axes else "parallel" for a in range(len(grid)))
    in_specs, out_specs = [s for _, s in ins], [o[2] for o in outs]
    out_shape = [_out(o[0], o[1]) for o in outs]
    args = _in_hbm([a for a, _ in ins])
    if scalars is None:
        return pl.pallas_call(body, grid=grid, in_specs=in_specs, out_specs=out_specs, out_shape=out_shape,
                              compiler_params=_cp(sem), name=name)(*args)
    spec = pltpu.PrefetchScalarGridSpec(num_scalar_prefetch=1, grid=grid, in_specs=in_specs, out_specs=out_specs)
    return pl.pallas_call(body, grid_spec=spec, out_shape=out_shape, compiler_params=_cp(sem), name=name)(scalars, *args)


def _rows(width, cblk=0, roff=0, tr=ROW_TILE):
    return pl.BlockSpec((tr, width), lambda i: (i + roff, cblk))


def _full(shape):
    nd = len(shape)
    return pl.BlockSpec(shape, lambda *_: (0,) * nd)


def _sigmoid(x):
    return 1.0 / (1.0 + jnp.exp2(x * (-1.4426950408889634)))


def _rms(x):
    return lax.rsqrt(jnp.mean(x * x, axis=-1, keepdims=True) + EPS)


def _rms_bwd(dn, xn, r):
    return r * (dn - xn * jnp.mean(dn * xn, axis=-1, keepdims=True))


def _colsum(x):
    return jnp.sum(x, axis=0, keepdims=True)


def _shifts(x):
    n = x.shape[0]
    rows = lax.broadcasted_iota(jnp.int32, x.shape, 0)
    return jnp.where(rows == 0, 0.0, pltpu.roll(x, 1, 0)), jnp.where(rows == n - 1, 0.0, pltpu.roll(x, n - 1, 0))


def _conv(x, w, b, shifted=None):
    prev, nxt = _shifts(x) if shifted is None else shifted
    return b + prev * w[0:1] + x * w[1:2] + nxt * w[2:3]


def _conv_bwd_x(dy, w):
    prev, nxt = _shifts(dy)
    return nxt * w[0:1] + dy * w[1:2] + prev * w[2:3]


def _conv_bwd_w(dy, x, shifted):
    prev, nxt = shifted
    return _colsum(dy * prev), _colsum(dy * x), _colsum(dy * nxt)


def _rope(x, cos, sin_lo, sin_hi):
    return x * cos + pltpu.roll(x, HEAD_PAD - 8, 1) * sin_lo + pltpu.roll(x, 8, 1) * sin_hi


ATTN_SCALE = QK_DIM ** -0.5
LOG2_E = 1.4426950408889634


def _rope_t(x, tab, inverse=False):
    o = 3 * HEAD_PAD if inverse else 0
    return _rope(x, tab[:, o:o + HEAD_PAD], tab[:, o + HEAD_PAD:o + 2 * HEAD_PAD], tab[:, o + 2 * HEAD_PAD:o + 3 * HEAD_PAD])


def _heads_keys(hp, kv_ref, kr_ref, tab_ref, kc_ref, vp_ref):
    kr_roped = _rope_t(kr_ref[...], tab_ref[...])
    lane = lax.broadcasted_iota(jnp.int32, kr_roped.shape, 1)
    for u in range(hp):
        kv = kv_ref[:, u * HEAD_PAD:(u + 1) * HEAD_PAD]
        kc_ref[u] = jnp.where(lane < 64, kv, kr_roped).astype(BF16)
        vp_ref[u] = jnp.where(lane >= 64, kv, 0.0).astype(BF16)


ATTN_Q_TILE = 512
ATTN_HEADS_PER_STEP = 2


def _attn_specs(tq, TT):
    q = pl.BlockSpec((tq, HEAD_PAD), lambda h, i: (i, h))
    keys = pl.BlockSpec((TT, HEAD_PAD), lambda h, i: (0, h))
    kr = pl.BlockSpec((TT, HEAD_PAD), lambda h, i: (0, PA_KR0 // HEAD_PAD))
    tab_q = pl.BlockSpec((tq, 6 * HEAD_PAD), lambda h, i: (i, 0))
    tab_k = pl.BlockSpec((TT, 6 * HEAD_PAD), lambda h, i: (0, 0))
    return q, keys, kr, tab_q, tab_k


def _attn_fwd(q_raw, kv, pp, tab, T, TT):
    tq, hp = ROW_TILE, 2 * ATTN_HEADS_PER_STEP
    w = hp * HEAD_PAD

    def body(q_ref, kv_ref, kr_ref, tq_ref, tk_ref, o_ref, kc, vp):
        @pl.when(pl.program_id(1) == 0)
        def _():
            _heads_keys(hp, kv_ref, kr_ref, tk_ref, kc, vp)

        tab = tq_ref[...]
        for u in range(hp):
            cols = slice(u * HEAD_PAD, (u + 1) * HEAD_PAD)
            q = _rope_t(q_ref[:, cols], tab).astype(BF16)
            s = lax.dot_general(q, kc[u], NT, preferred_element_type=F32)
            m = jnp.max(s, axis=-1, keepdims=True)
            p = jnp.exp2((s - m) * (ATTN_SCALE * LOG2_E))
            l = jnp.sum(p, axis=-1, keepdims=True)
            o = lax.dot_general(p.astype(BF16), vp[u], NN, preferred_element_type=F32)
            lane = lax.broadcasted_iota(jnp.int32, o.shape, 1)
            o_ref[:, cols] = jnp.where(lane < 64, m * ATTN_SCALE + jnp.log(l), o / l)

    _, _, kr, _, _ = _attn_specs(tq, TT)
    qs = pl.BlockSpec((tq, w), lambda h, i: (i, h))
    keys = pl.BlockSpec((TT, w), lambda h, i: (0, h))
    tab_q = pl.BlockSpec((tq, 3 * HEAD_PAD), lambda h, i: (i, 0))
    tab_k = pl.BlockSpec((TT, 3 * HEAD_PAD), lambda h, i: (0, 0))
    return pl.pallas_call(
        body, grid=(N_HEADS // hp, T // tq), in_specs=[qs, keys, kr, tab_q, tab_k], out_specs=qs,
        out_shape=jax.ShapeDtypeStruct((T, N_HEADS * HEAD_PAD), F32),
        scratch_shapes=[pltpu.VMEM((hp, TT, HEAD_PAD), BF16), pltpu.VMEM((hp, TT, HEAD_PAD), BF16)],
        compiler_params=_cp(("parallel", "arbitrary")), name="attn_fwd",
    )(*_in_hbm([q_raw, kv, pp, tab, tab]))


def _attn_bwd(q_raw, kv, pp, o, do, tab, T, TT):
    tq = _pick(T, (ATTN_Q_TILE, ROW_TILE))
    nq = T // tq
    hp = ATTN_HEADS_PER_STEP
    w = hp * HEAD_PAD

    def body(q_ref, kv_ref, kr_ref, tq_ref, tk_ref, o_ref, do_ref, dq_ref, dkv_ref, dkr_ref, kc, vp, dk, dv):
        g, i = pl.program_id(0), pl.program_id(1)

        @pl.when(i == 0)
        def _():
            _heads_keys(hp, kv_ref, kr_ref, tk_ref, kc, vp)
            dk[...] = jnp.zeros_like(dk)
            dv[...] = jnp.zeros_like(dv)

        tab = tq_ref[...]
        for u in range(hp):
            cols = slice(u * HEAD_PAD, (u + 1) * HEAD_PAD)
            q = _rope_t(q_ref[:, cols], tab).astype(BF16)
            k, v, d_o = kc[u], vp[u], do_ref[:, cols]
            s = lax.dot_general(q, k, NT, preferred_element_type=F32)
            o = o_ref[:, cols]
            p = jnp.exp2(s * (ATTN_SCALE * LOG2_E) - o[:, 0:1] * LOG2_E)
            dob = d_o.astype(BF16)
            dp = lax.dot_general(dob, v, NT, preferred_element_type=F32)
            dd = jnp.sum(d_o * o, axis=-1, keepdims=True)
            ds = (p * (dp - dd) * ATTN_SCALE).astype(BF16)
            dq = lax.dot_general(ds, k, NN, preferred_element_type=F32)
            dq_ref[:, cols] = _rope_t(dq, tab, inverse=True).astype(dq_ref.dtype)
            dk[u] += lax.dot_general(q, ds, TN, preferred_element_type=F32)
            dv[u] += lax.dot_general(dob, p.astype(BF16), TN, preferred_element_type=F32)

        @pl.when(i == nq - 1)
        def _():
            rot = None
            for u in range(hp):
                dkh = dk[u].T
                lane = lax.broadcasted_iota(jnp.int32, dkh.shape, 1)
                dkv_ref[:, u * HEAD_PAD:(u + 1) * HEAD_PAD] = jnp.where(lane < 64, dkh, dv[u].T).astype(dkv_ref.dtype)
                part = jnp.where((lane >= 64) & (lane < 96), dkh, 0.0)
                rot = part if rot is None else rot + part
            rot = _rope_t(rot, tk_ref[...], inverse=True)

            @pl.when(g == 0)
            def _():
                dkr_ref[...] = rot

            @pl.when(g > 0)
            def _():
                dkr_ref[...] += rot

    _, _, kr, tab_q, tab_k = _attn_specs(tq, TT)
    qs = pl.BlockSpec((tq, w), lambda h, i: (i, h))
    keys = pl.BlockSpec((TT, w), lambda h, i: (0, h))
    wide = lambda rows: jax.ShapeDtypeStruct((rows, N_HEADS * HEAD_PAD), BF16)
    return pl.pallas_call(
        body, grid=(N_HEADS // hp, nq),
        in_specs=[qs, keys, kr, tab_q, tab_k, qs, qs],
        out_specs=[qs, keys, pl.BlockSpec((TT, HEAD_PAD), lambda h, i: (0, 0))],
        out_shape=[wide(T), wide(TT), jax.ShapeDtypeStruct((TT, HEAD_PAD), F32)],
        scratch_shapes=[pltpu.VMEM((hp, TT, HEAD_PAD), BF16), pltpu.VMEM((hp, TT, HEAD_PAD), BF16),
                        pltpu.VMEM((hp, HEAD_PAD, TT), F32), pltpu.VMEM((hp, HEAD_PAD, TT), F32)],
        compiler_params=_cp(("arbitrary", "arbitrary")), name="attn_bwd",
    )(*_in_hbm([q_raw, kv, pp, tab, tab, o, do]))


def _hbm_specs(n):
    return [pl.BlockSpec(memory_space=pl.ANY)] * n


def _gather_weights(shards, lead):
    n = len(shards)
    halves = [s.shape[0] // 2 for s in shards]
    h0 = lead.shape[0] // 2

    def body(*refs):
        ins, lead_in, outs, lead_out = refs[:n], refs[n], refs[n + 1:2 * n + 1], refs[2 * n + 1]
        token, send_sems, recv_sems, lead_send, lead_recv = refs[2 * n + 2:]
        token[...] = jnp.zeros_like(token)
        mx, my, mc = lax.axis_index("x"), lax.axis_index("y"), lax.axis_index("c")
        j_me = 2 * mx + my
        chips = [(1 - mx, my), (mx, 1 - my), (1 - mx, 1 - my)]
        sibling = (mx, my, 1 - mc)
        on_chip0 = j_me == 0

        def lead_half(ref, hc):
            return ref.at[pl.ds(hc * h0, h0), :]

        def lead_copy(k, src, dst, to):
            return pltpu.make_async_remote_copy(src_ref=src, dst_ref=dst, send_sem=lead_send.at[k],
                                                recv_sem=lead_recv.at[k], device_id=to, device_id_type=MESH)

        for k, (px, py) in enumerate(chips):
            @pl.when(on_chip0)
            def _(k=k, px=px, py=py):
                lead_copy(k, lead_half(lead_in, mc), lead_half(lead_out, mc), (px, py, mc)).start()

        @pl.when(on_chip0)
        def _():
            lead_copy(4, lead_in, lead_out, sibling).start()

        def half(w, chip_idx, hc):
            return outs[w].at[chip_idx, pl.ds(hc * halves[w], halves[w]), :]

        def copy(w, k, src, dst, to):
            return pltpu.make_async_remote_copy(src_ref=src, dst_ref=dst, send_sem=send_sems.at[w, k],
                                                recv_sem=recv_sems.at[w, k], device_id=to, device_id_type=MESH)

        sends = []
        for w in range(n):
            cp = copy(w, 6, ins[w], outs[w].at[j_me], (mx, my, 1 - mc))
            cp.start()
            sends.append(cp)
        for k, (px, py) in enumerate(chips):
            for w in range(n):
                cp = copy(w, k, ins[w].at[pl.ds(mc * halves[w], halves[w]), :], half(w, j_me, mc), (px, py, mc))
                cp.start()
                sends.append(cp)
        for k, (px, py) in enumerate(chips):
            for w in range(n):
                got = half(w, 2 * px + py, mc)
                copy(w, k, got, got, (px, py, mc)).wait_recv()
                cp = copy(w, 3 + k, got, got, (mx, my, 1 - mc))
                cp.start()
                sends.append(cp)
        for k, (px, py) in enumerate(chips):
            for w in range(n):
                got = half(w, 2 * px + py, 1 - mc)
                copy(w, 3 + k, got, got, (mx, my, 1 - mc)).wait_recv()
        for w in range(n):
            own = outs[w].at[j_me]
            copy(w, 6, own, own, (mx, my, 1 - mc)).wait_recv()
        for cp in sends:
            cp.wait_send()

        for k, (px, py) in enumerate(chips):
            @pl.when(2 * px + py == 0)
            def _(k=k, px=px, py=py):
                got = lead_half(lead_out, mc)
                lead_copy(k, got, got, (px, py, mc)).wait_recv()
                lead_copy(3, got, got, sibling).start()

        @pl.when(jnp.logical_not(on_chip0))
        def _():
            other, mine = lead_half(lead_out, 1 - mc), lead_half(lead_out, mc)
            lead_copy(3, other, other, sibling).wait_recv()
            lead_copy(3, mine, mine, sibling).wait_send()

        @pl.when(on_chip0)
        def _():
            lead_copy(4, lead_out, lead_out, sibling).wait_recv()
            lead_copy(4, lead_in, lead_out, sibling).wait_send()
            for k, (px, py) in enumerate(chips):
                lead_copy(k, lead_half(lead_in, mc), lead_half(lead_out, mc), (px, py, mc)).wait_send()

    res = pl.pallas_call(
        body, out_shape=[jax.ShapeDtypeStruct((4,) + s.shape, s.dtype) for s in shards]
        + [jax.ShapeDtypeStruct(lead.shape, lead.dtype), jax.ShapeDtypeStruct((8, 128), F32)],
        in_specs=_hbm_specs(n + 1), out_specs=_hbm_specs(n + 1) + [pl.BlockSpec(memory_space=pltpu.VMEM)],
        scratch_shapes=[pltpu.SemaphoreType.DMA((n, 7)), pltpu.SemaphoreType.DMA((n, 7)),
                        pltpu.SemaphoreType.DMA((5,)), pltpu.SemaphoreType.DMA((5,))],
        name="gather_weights")(*shards, lead)
    return list(res[:n]), res[n], res[n + 1]


def _rs_pair(gs, name):
    n = len(gs)
    halves = [g.shape[1] // 2 for g in gs]

    def body(*refs):
        ins, lands = refs[:n], refs[n:2 * n]
        send_sems, recv_sems = refs[2 * n:]
        mx, my, mc = lax.axis_index("x"), lax.axis_index("y"), lax.axis_index("c")
        copies = []
        for w in range(n):
            h = halves[w]
            cp = pltpu.make_async_remote_copy(
                src_ref=ins[w].at[:, pl.ds((1 - mc) * h, h), :], dst_ref=lands[w], send_sem=send_sems.at[w],
                recv_sem=recv_sems.at[w], device_id=(mx, my, 1 - mc), device_id_type=MESH)
            cp.start()
            copies.append(cp)
        for cp in copies:
            cp.wait()

    return pl.pallas_call(
        body, out_shape=[jax.ShapeDtypeStruct((4, h, g.shape[2]), g.dtype) for g, h in zip(gs, halves)],
        in_specs=_hbm_specs(n), out_specs=_hbm_specs(n),
        scratch_shapes=[pltpu.SemaphoreType.DMA((n,)), pltpu.SemaphoreType.DMA((n,))], name=name)(*gs)


def _rs_chips(parts):
    n = len(parts)

    def body(*refs):
        ins, lands = refs[:n], refs[n:2 * n]
        send_sems, recv_sems = refs[2 * n:]
        mx, my, mc = lax.axis_index("x"), lax.axis_index("y"), lax.axis_index("c")
        copies = []
        for k, (px, py) in enumerate([(1 - mx, my), (mx, 1 - my), (1 - mx, 1 - my)]):
            for w in range(n):
                cp = pltpu.make_async_remote_copy(
                    src_ref=ins[w].at[2 * px + py], dst_ref=lands[w].at[k], send_sem=send_sems.at[w, k],
                    recv_sem=recv_sems.at[w, k], device_id=(px, py, mc), device_id_type=MESH)
                cp.start()
                copies.append(cp)
        for cp in copies:
            cp.wait()

    return list(pl.pallas_call(
        body, out_shape=[jax.ShapeDtypeStruct((3,) + p.shape[1:], p.dtype) for p in parts],
        in_specs=_hbm_specs(n), out_specs=_hbm_specs(n),
        scratch_shapes=[pltpu.SemaphoreType.DMA((n, 3)), pltpu.SemaphoreType.DMA((n, 3))], name="rs_chips")(*parts))


_HBM = pl.BlockSpec(memory_space=pltpu.HBM)
_SEM = pl.BlockSpec(memory_space=pltpu.SEMAPHORE)
_EFFECT = pltpu.SideEffectType.DATAFLOW_SIDE_EFFECTING


def _ici_copies(kind, srcs, lands, send_sems, recv_sems):
    n = len(lands)
    mx, my, mc = lax.axis_index("x"), lax.axis_index("y"), lax.axis_index("c")
    j_me = 2 * mx + my
    copies = []
    if kind == "back":
        for w in range(n):
            h = lands[w].shape[0] // 2
            mine = lands[w].at[pl.ds(mc * h, h), :]
            copies.append(pltpu.make_async_remote_copy(
                src_ref=mine, dst_ref=mine, send_sem=send_sems.at[w], recv_sem=recv_sems.at[w],
                device_id=(mx, my, 1 - mc), device_id_type=MESH))
        return copies
    if kind == "all":
        for k in range(7):
            a, b, c = (k + 1) >> 2 & 1, (k + 1) >> 1 & 1, (k + 1) & 1
            peer = (1 - mx if a else mx, 1 - my if b else my, 1 - mc if c else mc)
            for w in range(n):
                copies.append(pltpu.make_async_remote_copy(
                    src_ref=srcs[w], dst_ref=lands[w].at[4 * mx + 2 * my + mc], send_sem=send_sems.at[7 * w + k],
                    recv_sem=recv_sems.at[7 * w + k], device_id=peer, device_id_type=MESH))
        return copies
    if kind == "pair":
        for w in range(n):
            h = srcs[w].shape[1] // 2
            copies.append(pltpu.make_async_remote_copy(
                src_ref=srcs[w].at[:, pl.ds((1 - mc) * h, h), :], dst_ref=lands[w], send_sem=send_sems.at[w],
                recv_sem=recv_sems.at[w], device_id=(mx, my, 1 - mc), device_id_type=MESH))
        return copies
    chips = [(1 - mx, my), (mx, 1 - my), (1 - mx, 1 - my)]
    if kind == "finish":
        for w in range(n):
            h = srcs[w].shape[0] // 2
            pushes = [(lands[w].at[2 * px + py, pl.ds(mc * h, h), :],) * 2 for px, py in chips]
            pushes.append((srcs[w], lands[w].at[j_me]))
            for k, (src, dst) in enumerate(pushes):
                copies.append(pltpu.make_async_remote_copy(
                    src_ref=src, dst_ref=dst, send_sem=send_sems.at[4 * w + k], recv_sem=recv_sems.at[4 * w + k],
                    device_id=(mx, my, 1 - mc), device_id_type=MESH))
        return copies
    for k, (px, py) in enumerate(chips):
        for w in range(n):
            if kind == "gather":
                h = srcs[w].shape[0] // 2
                src, dst = srcs[w].at[pl.ds(mc * h, h), :], lands[w].at[j_me, pl.ds(mc * h, h), :]
            else:
                src, dst = srcs[w].at[2 * px + py], lands[w].at[k]
            copies.append(pltpu.make_async_remote_copy(
                src_ref=src, dst_ref=dst, send_sem=send_sems.at[3 * w + k], recv_sem=recv_sems.at[3 * w + k],
                device_id=(px, py, mc), device_id_type=MESH))
    return copies


_SEMS_PER_OPERAND = {"gather": 3, "scatter": 3, "all": 7, "pair": 1, "finish": 4, "back": 1}


def _ici_start(kind, srcs, land_shapes, carry, name, lands=None):
    hbm = lambda a: pltpu.with_memory_space_constraint(a, pltpu.HBM)
    if lands is None:
        lands = [lax.empty(s, srcs[0].dtype) for s in land_shapes]
    ns, nl = len(srcs), len(lands)

    def body(*refs):
        send_sems, recv_sems = refs[ns + nl + 1], refs[ns + nl + 2]
        for cp in _ici_copies(kind, refs[:ns], refs[ns:ns + nl], send_sems, recv_sems):
            cp.start()

    args = [hbm(a) for a in list(srcs) + list(lands) + [carry]]
    n_sem = _SEMS_PER_OPERAND[kind] * nl
    out_shape = ([pltpu.SemaphoreType.DMA((n_sem,)), pltpu.SemaphoreType.DMA((n_sem,))]
                 + [pltpu.HBM(a.shape, a.dtype) for a in args])
    res = pl.pallas_call(
        body, name=name, out_shape=out_shape, in_specs=[_HBM] * len(args), out_specs=[_SEM, _SEM] + [_HBM] * len(args),
        input_output_aliases={i: 2 + i for i in range(len(args))},
        compiler_params=pltpu.CompilerParams(has_side_effects=_EFFECT))(*args)
    return res[0], res[1], list(res[2:2 + ns]), list(res[2 + ns:2 + ns + nl]), res[2 + ns + nl]


def _ici_wait(kind, send_sems, recv_sems, srcs, lands, after, name):
    ns, nl = len(srcs), len(lands)

    def body(*refs):
        for cp in _ici_copies(kind, refs[:ns], refs[ns:ns + nl], refs[ns + nl], refs[ns + nl + 1]):
            cp.wait_send()
            cp.wait_recv()

    args = list(srcs) + list(lands)
    res = pl.pallas_call(
        body, name=name, out_shape=[pltpu.HBM(a.shape, a.dtype) for a in args],
        in_specs=[_HBM] * len(args) + [_SEM, _SEM, pl.BlockSpec(memory_space=pl.ANY)], out_specs=[_HBM] * len(args),
        input_output_aliases={i: i for i in range(len(args))},
        compiler_params=pltpu.CompilerParams(has_side_effects=_EFFECT))(*args, send_sems, recv_sems, after)
    return list(res[:ns]), list(res[ns:])


def _tile_rows(h, c, itemsize, mult):
    best = h
    for t in range(mult, h + 1, mult):
        if h % t == 0 and t * c * itemsize <= (1 << 21):
            best = t
    return best


def _add_pair(g, land, place, name):
    _, h, c = land.shape
    t = _tile_rows(h, c, 2, 16)
    nb = h // t
    return _ew(lambda ids, u, v: (u.astype(F32) + v.astype(F32),), (4, nb),
               [(g, pl.BlockSpec((None, t, c), lambda j, i, s: (j, s[1] * nb + i, 0))),
                (land, pl.BlockSpec((None, t, c), lambda j, i, s: (j, i, 0)))],
               [(land.shape, BF16, pl.BlockSpec((None, t, c), lambda j, i, s: (j, i, 0)), None)], name, scalars=place)[0]


def _add_pair_many(gs, lands, place, name):
    ins, outs = [], []
    for g, l in zip(gs, lands):
        ins += [(g, pl.BlockSpec(l.shape, lambda i, s: (0, s[1], 0))), (l, pl.BlockSpec(l.shape, lambda i, s: (0, 0, 0)))]
        outs.append((l.shape, BF16, pl.BlockSpec(l.shape, lambda i, s: (0, 0, 0)), None))
    fn = lambda ids, *v: [v[2 * k].astype(F32) + v[2 * k + 1].astype(F32) for k in range(len(gs))]
    return list(_ew(fn, (1,), ins, outs, name, scalars=place))


def _add_chips_many(owns, lands, place, name):
    ins, outs = [], []
    for own, land in zip(owns, lands):
        _, h, c = land.shape
        ins += [(own, pl.BlockSpec((None, h, c), lambda i, s: (s[0], 0, 0))),
                (land, pl.BlockSpec((3, h, c), lambda i, s: (0, 0, 0)))]
        outs.append(((2 * h, c), F32, pl.BlockSpec((h, c), lambda i, s: (s[1], 0)), None))

    def fn(ids, *v):
        return [((v[2 * k].astype(F32) + v[2 * k + 1][0].astype(F32)) + v[2 * k + 1][1].astype(F32))
                + v[2 * k + 1][2].astype(F32) for k in range(len(owns))]

    return list(_ew(fn, (1,), ins, outs, name, scalars=place))


def _add_chips(own, land, place, name):
    _, h, c = land.shape
    t = _tile_rows(h, c, 4, 16)
    nb = h // t

    def fn(ids, a, b):
        return (((a.astype(F32) + b[0].astype(F32)) + b[1].astype(F32)) + b[2].astype(F32),)

    return _ew(fn, (nb,), [(own, pl.BlockSpec((None, t, c), lambda i, s: (s[0], i, 0))),
                           (land, pl.BlockSpec((3, t, c), lambda i, s: (0, i, 0)))],
               [((2 * h, c), F32, pl.BlockSpec((t, c), lambda i, s: (s[1] * nb + i, 0)), None)], name, scalars=place)[0]


W_IN_SEGMENTS = ((0, 256, KV0), (256, 288, KR0 + 64), (288, 672, Q0), (672, 1184, CX0), (1184, 1696, CB0),
                 (1696, 2208, CC0), (2208, 3232, GA0), (3232, 4256, GC0))
W_IN_SHARD = 1064


W_IN_SHARD_PAD = 1088
W_IN_EARLY = 672


def _w_in_t_p_from_shards(s):
    pieces = []
    for o0, o1, p0 in sorted(W_IN_SEGMENTS, key=lambda t: t[2]):
        if p0 == KR0 + 64:
            pieces.append(jnp.zeros((64, s.shape[2]), s.dtype))
        for j in range(4):
            lo, hi = max(o0, j * W_IN_SHARD), min(o1, (j + 1) * W_IN_SHARD)
            if lo < hi:
                pieces.append(s[j, lo - j * W_IN_SHARD:hi - j * W_IN_SHARD])
    pieces.append(jnp.zeros((32, s.shape[2]), s.dtype))
    return jnp.concatenate(pieces, axis=0)


def _w_in_t_shards_from_p(g):
    shards = []
    for j in range(4):
        pieces = []
        for o0, o1, p0 in W_IN_SEGMENTS:
            lo, hi = max(o0, j * W_IN_SHARD), min(o1, (j + 1) * W_IN_SHARD)
            if lo < hi:
                pieces.append(g[p0 + lo - o0:p0 + hi - o0])
        pieces.append(jnp.zeros((W_IN_SHARD_PAD - W_IN_SHARD, g.shape[1]), g.dtype))
        shards.append(jnp.concatenate(pieces, axis=0))
    return jnp.stack(shards, axis=0)


def _cols_from_shards(s):
    return jnp.transpose(s, (1, 0, 2)).reshape(s.shape[1], -1)


def _rope_tables(T, TT, inverse):
    f32 = np.float32
    rows = T // GRID_W
    row = np.repeat(np.arange(rows), GRID_W).astype(f32)
    col = np.tile(np.arange(GRID_W), rows).astype(f32)
    inv = (f32(ROPE_THETA) ** (-np.arange(0, 16, 2, dtype=f32) / f32(16))).astype(f32)
    ang = np.concatenate([row[:, None] * inv, col[:, None] * inv], axis=-1).astype(f32)
    cos, sin = np.cos(ang).astype(f32), np.sin(ang).astype(f32)
    lane = np.arange(32)
    src = (lane // 16) * 8 + lane % 8
    lo = ((lane % 16) // 8 == 0).astype(f32)
    sgn = f32(-1.0 if inverse else 1.0)
    cos32 = cos[:, src]
    sin_lo32 = -sgn * sin[:, src] * lo
    sin_hi32 = sgn * sin[:, src] * (1 - lo)

    def widen(t32, fill):
        t = np.concatenate([np.full((T, 64), fill, f32), t32, np.full((T, 32), fill, f32)], axis=1)
        return np.concatenate([t, np.full((TT - T, HEAD_PAD), fill, f32)], axis=0)

    return [widen(cos32, 1.0), widen(sin_lo32, 0.0), widen(sin_hi32, 0.0)]


def _rope_table(T, TT):
    return jnp.asarray(np.concatenate(_rope_tables(T, TT, False) + _rope_tables(T, TT, True), axis=1))


def _local_step(xx, tgt, mod_lat, mod_ctx, W, late_weights, early_grads, early_continue):
    TT = xx.shape[0]
    T = tgt.shape[0]
    n_lat, n_all = T // ROW_TILE, TT // ROW_TILE
    sh1, sc1, g1, sh2, sc2, g2 = [mod_lat[:, k * D_MODEL:(k + 1) * D_MODEL] for k in range(6)]
    csh1, csc1 = mod_ctx[:, :D_MODEL], mod_ctx[:, D_MODEL:2 * D_MODEL]
    vec = lambda n: _full((1, n))
    row_out = lambda n, dt, rows=T: ((rows, n), dt, _rows(n), None)
    acc_out = lambda n: ((1, n), F32, _full((1, n)), 0)
    lt = _pick(T, (2 * ROW_TILE, ROW_TILE))
    n_lt = T // lt
    lrows = lambda n, cblk=0: _rows(n, cblk, 0, lt)
    lrow_out = lambda n, dt: ((T, n), dt, lrows(n), None)

    def f_norm1(ids, x, g, a_sh, a_sc, b_sh, b_sc):
        ctx = ids[0] >= n_lat
        sh, sc = jnp.where(ctx, b_sh, a_sh), jnp.where(ctx, b_sc, a_sc)
        return ((x * _rms(x) * g) * (1.0 + sc) + sh,)

    (hh,) = _ew(f_norm1, (n_all,), [(xx, _rows(D_MODEL)), (W["norm1_g"], vec(D_MODEL)), (sh1, vec(D_MODEL)),
                                   (sc1, vec(D_MODEL)), (csh1, vec(D_MODEL)), (csc1, vec(D_MODEL))],
                [row_out(D_MODEL, BF16, TT)], "norm1_fwd")
    tm_all = _pick(TT, (768, 256))
    pp_a = _mm(hh, W["w_in_a_t"], "nt", TT, PA_COLS, D_MODEL, tm=tm_all, tn=PA_COLS, tk=D_MODEL, name="w_in_fwd_a")

    def f_lowrank(ids, ckv, cq, gkv, gq):
        return ckv * _rms(ckv) * gkv, cq * _rms(cq) * gq

    nkv, nq = _ew(f_lowrank, (n_all,), [(pp_a, _rows(KV_RANK, PA_KV0 // KV_RANK)), (pp_a, _rows(Q_RANK, PA_Q0 // Q_RANK)),
                                       (W["kv_norm_g"], vec(KV_RANK)), (W["q_norm_g"], vec(Q_RANK))],
                  [row_out(KV_RANK, BF16, TT), row_out(Q_RANK, BF16, TT)], "lowrank_norm_fwd")
    kv = _mm(nkv, W["w_ukv"], "nn", TT, 1024, KV_RANK, tm=tm_all, tn=256, tk=KV_RANK, name="w_ukv_fwd",
             b_spec=pl.BlockSpec((None, KV_RANK, 256), lambda i, j, k: (j, k, 0)))
    q_raw = _mm(nq, W["w_uq_t"], "nt", TT, 1024, Q_RANK, tm=tm_all, tn=1024, tk=Q_RANK, name="w_uq_fwd")

    tab = _rope_table(T, TT)
    _, q_raw = late_weights("before_attn", q_raw)
    o_pad = _attn_fwd(q_raw, kv, pp_a, tab, T, TT)
    arrived, o_pad = late_weights("after_attn", o_pad)
    W = dict(W, **arrived)
    tm_lat = _pick(T, (1024, 512, 256))
    pp = _mm(hh, W["w_in_t"], "nt", T, KV0, D_MODEL, tm=tm_lat, tn=KV0 // 2, tk=D_MODEL, name="w_in_fwd_b")
    ya = _mm(o_pad, W["w_attn_out"], "nn", T, D_MODEL, 1024, tm=tm_lat, tn=D_MODEL, tk=1024, name="w_attn_out_fwd",
             out_dtype=BF16)

    tc = 256
    colT = lambda blk0: pl.BlockSpec((T, tc), lambda j: (0, blk0 + j))

    def f_conv(ids, xin, cb, cc, w, b):
        return (cb * _conv(cc * xin, w, b),)

    (e,) = _ew(f_conv, (CONV_DIM // tc,),
               [(pp, colT(CX0 // tc)), (pp, colT(CB0 // tc)), (pp, colT(CC0 // tc)),
                (W["conv_w"], pl.BlockSpec((3, tc), lambda j: (0, j))), (W["conv_b"], pl.BlockSpec((1, tc), lambda j: (0, j)))],
               [((T, CONV_DIM), BF16, colT(0), None)], "conv_fwd")
    yc = _mm(e, W["w_conv_out"], "nn", T, D_MODEL, CONV_DIM, tm=tm_lat, tn=256, tk=CONV_DIM, name="w_conv_out_fwd",
             out_dtype=BF16, b_spec=pl.BlockSpec((None, CONV_DIM, 256), lambda i, j, k: (j, k, 0)))

    def f_merge(ids, ga, gc, a, c):
        return (_sigmoid(ga) * a.astype(F32) + _sigmoid(gc) * c.astype(F32),)

    (mrg,) = _ew(f_merge, (n_lt,), [(pp, lrows(D_MODEL, 0)), (pp, lrows(D_MODEL, 1)), (ya, lrows(D_MODEL)),
                                   (yc, lrows(D_MODEL))], [lrow_out(D_MODEL, BF16)], "merge_fwd")
    mo = _mm(mrg, W["w_o"], "nn", T, D_MODEL, D_MODEL, tm=tm_lat, tn=D_MODEL, tk=D_MODEL, name="w_o_fwd")

    def f_norm2(ids, x, m, gate, g, sh, sc):
        x1 = x + gate * m
        return x1, (x1 * _rms(x1) * g) * (1.0 + sc) + sh

    x1, h2 = _ew(f_norm2, (n_lt,), [(xx, lrows(D_MODEL)), (mo, lrows(D_MODEL)), (g1, vec(D_MODEL)),
                                   (W["norm2_g"], vec(D_MODEL)), (sh2, vec(D_MODEL)), (sc2, vec(D_MODEL))],
                 [lrow_out(D_MODEL, F32), lrow_out(D_MODEL, BF16)], "norm2_fwd")
    arrived, h2 = late_weights("before_ffn", h2)
    W = dict(W, **arrived)
    up = _mm(h2, W["w_up"], "nn", T, 2 * D_FF, D_MODEL, tm=tm_lat, tn=1408, tk=D_MODEL, name="w_up_fwd",
             b_spec=pl.BlockSpec((None, D_MODEL, 1408), lambda i, j, k: (j, k, 0)))

    n_ff = D_FF // tc
    ffw = lambda off, n=3: pl.BlockSpec((n, tc), lambda j: (0, j + off))

    def f_ffn(ids, ug, uv, wg, wv, bg, bv):
        gate, val = _conv(ug, wg, bg), _conv(uv, wv, bv)
        return (gate * _sigmoid(gate) * val,)

    (act,) = _ew(f_ffn, (n_ff,), [(up, colT(0)), (up, colT(n_ff)), (W["ffn_conv_w"], ffw(0)), (W["ffn_conv_w"], ffw(n_ff)),
                                 (W["ffn_conv_b"], ffw(0, 1)), (W["ffn_conv_b"], ffw(n_ff, 1))],
                 [((T, D_FF), BF16, colT(0), None)], "ffn_act_fwd")
    f = _mm(act, W["w_down"], "nn", T, D_MODEL, D_FF, tm=tm_lat, tn=D_MODEL, tk=D_FF, name="w_down_fwd")

    def f_head(ids, x1_, f_, gate, gf, t):
        x2 = x1_ + gate * f_
        r = _rms(x2)
        xn = x2 * r
        err = xn * gf - t
        loss = 0.5 * jnp.sum(jnp.mean(err * err, axis=-1, keepdims=True))
        dy = err * (1.0 / D_MODEL)
        dx2 = _rms_bwd(dy * gf, xn, r)
        return dx2, dx2 * gate, _colsum(dy * xn), _colsum(dx2 * f_), jnp.full((1, 128), loss, F32)

    dx2, df, dg_f, dg2, loss = _ew(
        f_head, (n_lt,), [(x1, lrows(D_MODEL)), (f, lrows(D_MODEL)), (g2, vec(D_MODEL)), (W["final_g"], vec(D_MODEL)),
                          (tgt, lrows(D_MODEL))],
        [lrow_out(D_MODEL, F32), lrow_out(D_MODEL, BF16), acc_out(D_MODEL), acc_out(D_MODEL), acc_out(128)], "loss_head")

    d_w_down = _mm(act, df, "tn", D_FF, D_MODEL, T, tm=1408, tn=D_MODEL, tk=T, name="w_down_dw",
                   out_dtype=BF16).reshape(4, D_FF // 4, D_MODEL)
    da = _mm(df, W["w_down"], "nt", T, D_FF, D_MODEL, tm=tm_lat, tn=1408, tk=D_MODEL, name="w_down_dx")

    tcb = 128
    n_fb = D_FF // tcb
    colb = lambda blk0: pl.BlockSpec((T, tcb), lambda j: (0, blk0 + j))
    ffwb = lambda off, n=3: pl.BlockSpec((n, tcb), lambda j: (0, j + off))
    cvec = ((1, D_FF), F32, pl.BlockSpec((1, tcb), lambda j: (0, j)), None)

    def f_ffn_bwd(ids, ug, uv, d_act, wg, wv, bg, bv):
        sg, sv = _shifts(ug), _shifts(uv)
        gate, val = _conv(ug, wg, bg, sg), _conv(uv, wv, bv, sv)
        s = _sigmoid(gate)
        d_gate = d_act * val * s * (1.0 + gate * (1.0 - s))
        d_val = d_act * gate * s
        wg0, wg1, wg2 = _conv_bwd_w(d_gate, ug, sg)
        wv0, wv1, wv2 = _conv_bwd_w(d_val, uv, sv)
        d_up = [_conv_bwd_x(d_gate, wg), _conv_bwd_x(d_val, wv)]
        return d_up, [_colsum(d_gate), _colsum(d_val), wg0, wg1, wg2, wv0, wv1, wv2]

    d_up3, ffn_stats = _ew(
        f_ffn_bwd, (n_fb,),
        [(up, colb(0)), (up, colb(n_fb)), (da, colb(0)), (W["ffn_conv_w"], ffwb(0)), (W["ffn_conv_w"], ffwb(n_fb)),
         (W["ffn_conv_b"], ffwb(0, 1)), (W["ffn_conv_b"], ffwb(n_fb, 1))],
        [((2, T, D_FF), BF16, pl.BlockSpec((2, T, tcb), lambda j: (0, 0, j)), None),
         ((n_fb, 8, 1, tcb), F32, pl.BlockSpec((None, 8, 1, tcb), lambda j: (j, 0, 0, 0)), None)], "ffn_act_bwd")
    stat = lambda s: ffn_stats[:, s, 0, :].reshape(1, D_FF)
    d_ffn_conv_b = jnp.concatenate([stat(0), stat(1)], axis=1)
    d_ffn_conv_w = jnp.concatenate([jnp.concatenate([stat(2), stat(3), stat(4)], axis=0),
                                    jnp.concatenate([stat(5), stat(6), stat(7)], axis=0)], axis=1)

    tk_t = T
    d_w_up = _mm(h2, d_up3, "tn", D_MODEL, 2 * D_FF, T, tm=D_MODEL, tn=1408, tk=tk_t, name="w_up_dw", out_dtype=BF16,
                 b_spec=pl.BlockSpec((None, tk_t, 1408), lambda i, j, k: (j // 2, k, j % 2)),
                 o_spec=pl.BlockSpec((None, D_MODEL, 1408), lambda i, j, k: (j, i, 0)), out_shape=(4, D_MODEL, 1408))
    dh2 = _mm(d_up3, W["w_up"], "nt", T, D_MODEL, 2 * D_FF, tm=tm_lat, tn=D_MODEL, tk=1408, name="w_up_dx",
              a_spec=pl.BlockSpec((None, tm_lat, 1408), lambda i, j, k: (k // 2, i, k % 2)),
              b_spec=pl.BlockSpec((None, D_MODEL, 1408), lambda i, j, k: (k, j, 0)))

    def f_norm2_bwd(ids, dx2_, dh, x1_, m, g, sc, gate):
        r = _rms(x1_)
        xn = x1_ * r
        dx1 = dx2_ + _rms_bwd(dh * g * (1.0 + sc), xn, r)
        return dx1, dx1 * gate, _colsum(dh), _colsum(dh * xn * g), _colsum(dh * xn * (1.0 + sc)), _colsum(dx1 * m)

    dx1, dmo, dsh2, dsc2, dg_n2, dg1 = _ew(
        f_norm2_bwd, (n_lt,), [(dx2, lrows(D_MODEL)), (dh2, lrows(D_MODEL)), (x1, lrows(D_MODEL)), (mo, lrows(D_MODEL)),
                               (W["norm2_g"], vec(D_MODEL)), (sc2, vec(D_MODEL)), (g1, vec(D_MODEL))],
        [lrow_out(D_MODEL, F32), lrow_out(D_MODEL, BF16)] + [acc_out(D_MODEL)] * 4, "norm2_bwd")
    d_w_o = _mm(mrg, dmo, "tn", D_MODEL, D_MODEL, T, tm=D_MODEL, tn=D_MODEL, tk=tk_t, name="w_o_dw",
                out_dtype=BF16).reshape(4, D_MODEL // 4, D_MODEL)
    dmrg = _mm(dmo, W["w_o"], "nt", T, D_MODEL, D_MODEL, tm=tm_lat, tn=D_MODEL, tk=D_MODEL, name="w_o_dx",
               out_dtype=BF16)
    dmrg = early_grads("late", {"w_o": d_w_o, "w_up": d_w_up, "w_down": d_w_down}, dmrg, split=True)

    def f_merge_bwd(ids, dm, ga, gc, a, c):
        dm, a, c = dm.astype(F32), a.astype(F32), c.astype(F32)
        sa, sc_ = _sigmoid(ga), _sigmoid(gc)
        return dm * sa, dm * sc_, dm * a * sa * (1.0 - sa), dm * c * sc_ * (1.0 - sc_)

    dya, dyc, dp_ga, dp_gc = _ew(
        f_merge_bwd, (n_lt,), [(dmrg, lrows(D_MODEL)), (pp, lrows(D_MODEL, 0)), (pp, lrows(D_MODEL, 1)),
                               (ya, lrows(D_MODEL)), (yc, lrows(D_MODEL))], [lrow_out(D_MODEL, BF16)] * 4, "merge_bwd")
    dya = early_continue("late", dya)

    d_w_ao_p = _mm(o_pad, dya, "tn", 1024, D_MODEL, T, tm=1024, tn=D_MODEL, tk=tk_t, name="w_attn_out_dw", out_dtype=BF16)
    do_pad = _mm(dya, W["w_attn_out"], "nt", T, 1024, D_MODEL, tm=tm_lat, tn=1024, tk=D_MODEL, name="w_attn_out_dx")
    d_w_co = _mm(e, dyc, "tn", CONV_DIM, D_MODEL, T, tm=CONV_DIM, tn=256, tk=tk_t, name="w_conv_out_dw", out_dtype=BF16,
                 o_spec=pl.BlockSpec((None, CONV_DIM, 256), lambda i, j, k: (j, i, 0)), out_shape=(4, CONV_DIM, 256))
    de = _mm(dyc, W["w_conv_out"], "nt", T, CONV_DIM, D_MODEL, tm=tm_lat, tn=CONV_DIM, tk=256, name="w_conv_out_dx",
             b_spec=pl.BlockSpec((None, CONV_DIM, 256), lambda i, j, k: (k, j, 0)))

    def f_conv_bwd(ids, xin, cb, cc, d_e, w, b):
        z = cc * xin
        sz = _shifts(z)
        cz = _conv(z, w, b, sz)
        dcz = d_e * cb
        w0, w1, w2 = _conv_bwd_w(dcz, z, sz)
        dz = _conv_bwd_x(dcz, w)
        return dz * cc, d_e * cz, dz * xin, _colsum(dcz), w0, w1, w2

    cvec_c = ((1, CONV_DIM), F32, pl.BlockSpec((1, tc), lambda j: (0, j)), None)
    conv_b = _ew(f_conv_bwd, (CONV_DIM // tc,),
                 [(pp, colT(CX0 // tc)), (pp, colT(CB0 // tc)), (pp, colT(CC0 // tc)), (de, colT(0)),
                  (W["conv_w"], pl.BlockSpec((3, tc), lambda j: (0, j))), (W["conv_b"], pl.BlockSpec((1, tc), lambda j: (0, j)))],
                 [((T, CONV_DIM), BF16, colT(0), None)] * 3 + [cvec_c] * 4, "conv_bwd")
    dp_cx, dp_cb, dp_cc, d_conv_b = conv_b[:4]
    d_conv_w = jnp.concatenate(conv_b[4:7], axis=0)

    dq_raw, dkv, dp_kr = _attn_bwd(q_raw, kv, pp_a, o_pad, do_pad, tab, T, TT)

    tk_a = TT
    d_w_uq_t = _mm(nq, dq_raw, "tn", Q_RANK, 1024, T, tm=Q_RANK, tn=1024, tk=T, name="w_uq_dw", transpose_out=True)
    dnq = _mm(dq_raw, W["w_uq_t"], "nn", T, Q_RANK, 1024, tm=tm_lat, tn=Q_RANK, tk=1024, name="w_uq_dx")
    d_w_ukv = _mm(nkv, dkv, "tn", KV_RANK, 1024, TT, tm=KV_RANK, tn=256, tk=tk_a, name="w_ukv_dw", out_dtype=BF16,
                  o_spec=pl.BlockSpec((None, KV_RANK, 256), lambda i, j, k: (j, i, 0)), out_shape=(4, KV_RANK, 256))
    dnkv = _mm(dkv, W["w_ukv"], "nt", TT, KV_RANK, 1024, tm=tm_all, tn=KV_RANK, tk=256, name="w_ukv_dx",
               b_spec=pl.BlockSpec((None, KV_RANK, 256), lambda i, j, k: (k, j, 0)))
    dnkv = early_grads("mid", {
        "w_attn_out": jnp.transpose(d_w_ao_p.reshape(N_HEADS, HEAD_PAD, 4, 256)[:, 64:], (2, 0, 1, 3)).reshape(
            4, N_HEADS * 64, 256),
        "w_conv_out": d_w_co,
        "w_uq": d_w_uq_t.reshape(4, 2, HEAD_PAD, Q_RANK)[:, :, :QK_DIM].reshape(4, 2 * QK_DIM, Q_RANK).astype(BF16),
        "w_ukv": d_w_ukv}, dnkv)

    def f_lowrank_bwd(ids, ckv, cq, dkv_, dq_, gkv, gq, ga, gc, cx, cb, cc, kr):
        rk, rq = _rms(ckv), _rms(cq)
        nk, nq_ = ckv * rk, cq * rq
        lat = ids[0] < n_lat
        dq_ = jnp.where(lat, dq_, 0.0)
        pieces = [jnp.where(lat, a, jnp.zeros_like(a)) for a in (ga, gc, cx, cb, cc)]
        pieces += [_rms_bwd(dkv_ * gkv, nk, rk).astype(BF16), _rms_bwd(dq_ * gq, nq_, rq).astype(BF16), kr.astype(BF16)]
        return jnp.concatenate(pieces, axis=1), _colsum(dkv_ * nk), _colsum(dq_ * nq_)

    lat_rows = lambda n: pl.BlockSpec((ROW_TILE, n), lambda i: (jnp.minimum(i, n_lat - 1), 0))
    dpp, dg_kv, dg_q = _ew(
        f_lowrank_bwd, (n_all,), [(pp_a, _rows(KV_RANK, PA_KV0 // KV_RANK)), (pp_a, _rows(Q_RANK, PA_Q0 // Q_RANK)),
                                  (dnkv, _rows(KV_RANK)), (dnq, lat_rows(Q_RANK)), (W["kv_norm_g"], vec(KV_RANK)),
                                  (W["q_norm_g"], vec(Q_RANK)), (dp_ga, lat_rows(D_MODEL)), (dp_gc, lat_rows(D_MODEL)),
                                  (dp_cx, lat_rows(CONV_DIM)), (dp_cb, lat_rows(CONV_DIM)), (dp_cc, lat_rows(CONV_DIM)),
                                  (dp_kr, _rows(HEAD_PAD))],
        [row_out(P_COLS, BF16, TT), acc_out(KV_RANK), acc_out(Q_RANK)], "lowrank_norm_bwd")
    d_w_in_t = _mm(hh, dpp, "tn", D_MODEL, P_COLS, TT, tm=512, tn=2176, tk=TT, name="w_in_dw", out_dtype=BF16,
                   transpose_out=True)
    dhh = _mm(dpp, W["w_in_t"], "nn", TT, D_MODEL, P_COLS, tm=tm_all, tn=512, tk=2176, name="w_in_dx")

    def f_norm1_bwd(ids, x, dh, dres, g, sc):
        r = _rms(x)
        xn = x * r
        return (dres + _rms_bwd(dh * g * (1.0 + sc), xn, r), _colsum(dh), _colsum(dh * xn * g),
                _colsum(dh * xn * (1.0 + sc)))

    grad_x, dsh1, dsc1, dg_n1 = _ew(
        f_norm1_bwd, (n_lt,), [(xx, lrows(D_MODEL)), (dhh, lrows(D_MODEL)), (dx1, lrows(D_MODEL)),
                               (W["norm1_g"], vec(D_MODEL)), (sc1, vec(D_MODEL))],
        [lrow_out(D_MODEL, F32)] + [acc_out(D_MODEL)] * 3, "norm1_bwd")

    def f_norm1_ctx_bwd(ids, x, dh, g, sc):
        xn = x * _rms(x)
        return _colsum(dh), _colsum(dh * xn * g), _colsum(dh * xn * (1.0 + sc))

    n_ctx = n_all - n_lat
    dcsh1, dcsc1, dg_n1c = _ew(
        f_norm1_ctx_bwd, (n_ctx,), [(xx, _rows(D_MODEL, 0, n_lat)), (dhh, _rows(D_MODEL, 0, n_lat)),
                                    (W["norm1_g"], vec(D_MODEL)), (csc1, vec(D_MODEL))], [acc_out(D_MODEL)] * 3,
        "norm1_ctx_bwd")

    big = {"w_in": _w_in_t_shards_from_p(d_w_in_t).astype(BF16)}
    zero = jnp.zeros((1, 4 * D_MODEL), F32)
    small = {
        "dmod_lat": jnp.concatenate([dsh1, dsc1, dg1, dsh2, dsc2, dg2], axis=1),
        "dmod_ctx": jnp.concatenate([dcsh1, dcsc1, zero], axis=1),
        "norm1_g": dg_n1 + dg_n1c, "norm2_g": dg_n2, "final_g": dg_f, "q_norm_g": dg_q, "kv_norm_g": dg_kv,
        "conv_b": d_conv_b, "conv_w": d_conv_w.reshape(1, -1), "ffn_conv_b": d_ffn_conv_b,
        "ffn_conv_w": d_ffn_conv_w.reshape(1, -1),
    }
    return grad_x, loss, big, small


SMALL = (("dmod_lat", 6144), ("dmod_ctx", 6144), ("norm1_g", 1024), ("norm2_g", 1024), ("final_g", 1024),
         ("q_norm_g", 384), ("kv_norm_g", 256), ("conv_b", 512), ("conv_w", 1536), ("ffn_conv_b", 5632),
         ("ffn_conv_w", 16896), ("loss", 128))
SMALL_ROWS = 320


def _adam_update(w, g, m, v):
    c1, c2 = 1.0 - ADAM_B1 ** ADAM_STEP, 1.0 - ADAM_B2 ** ADAM_STEP
    m2 = ADAM_B1 * m + (1.0 - ADAM_B1) * g
    v2 = ADAM_B2 * v + (1.0 - ADAM_B2) * (g * g)
    return [-ADAM_LR * ((m2 / c1) / (jnp.sqrt(v2 / c2) + ADAM_EPS) + ADAM_WD * w), m2, v2]


def _adamw(w, g, m, v, name):
    R, C = w.shape
    tr = 8 if R % 8 == 0 else R
    for t in range(8, R + 1, 8):
        if R % t == 0 and t * C * 4 <= (1 << 21):
            tr = t
    spec = pl.BlockSpec((tr, C), lambda i: (i, 0))
    return _ew(lambda ids, *vals: [vals[1]] + _adam_update(*vals), (R // tr,),
               [(w, spec), (g, spec), (m, spec), (v, spec)], [((R, C), F32, spec, None)] * 4, name)


def kernel(x, c, ctx, c_ctx, w_ada, b_ada, norm1_g, w_in, q_norm_g, kv_norm_g, w_uq, w_ukv, conv_w, conv_b, w_attn_out, w_conv_out, w_o, norm2_g, w_up, ffn_conv_w, ffn_conv_b, w_down, final_g, loss_target, m_c_ctx, m_w_ada, m_b_ada, m_norm1_g, m_w_in, m_q_norm_g, m_kv_norm_g, m_w_uq, m_w_ukv, m_conv_w, m_conv_b, m_w_attn_out, m_w_conv_out, m_w_o, m_norm2_g, m_w_up, m_ffn_conv_w, m_ffn_conv_b, m_w_down, m_final_g, v_c_ctx, v_w_ada, v_b_ada, v_norm1_g, v_w_in, v_q_norm_g, v_kv_norm_g, v_w_uq, v_w_ukv, v_conv_w, v_conv_b, v_w_attn_out, v_w_conv_out, v_w_o, v_norm2_g, v_w_up, v_ffn_conv_w, v_ffn_conv_b, v_w_down, v_final_g):
    mx, my, mc = lax.axis_index("x"), lax.axis_index("y"), lax.axis_index("c")
    chip = 2 * mx + my
    dev = 4 * mx + 2 * my + mc
    T, Tc = x.shape[1], ctx.shape[1]
    TT = T + Tc
    w_in_t, m_w_in_t, v_w_in_t = (jnp.transpose(a[0]) for a in (w_in, m_w_in, v_w_in))
    w_uq_t, m_w_uq_t, v_w_uq_t = (jnp.transpose(a[0]) for a in (w_uq, m_w_uq, v_w_uq))
    conv_sh = jnp.concatenate([conv_w[0], ffn_conv_w[0]], axis=1)
    pay1 = jnp.concatenate([jnp.pad(c, ((0, 7), (0, 0))), jnp.pad(conv_sh, ((0, 5), (0, 0)))], axis=1)
    c_send, c_recv, c_src, c_land, zero0 = _ici_start("all", [pay1], [(8, 8, 2560)], jnp.zeros((8, 128), F32),
                                                      "cond_start")
    w_in_bf = (jnp.pad(w_in_t, ((0, W_IN_SHARD_PAD - W_IN_SHARD), (0, 0))) + zero0[0, 0]).astype(BF16)
    shards = {"w_in_a": w_in_bf[:W_IN_EARLY], "w_in": w_in_bf, "w_uq": w_uq_t, "w_ukv": w_ukv[0],
              "w_attn_out": w_attn_out[0], "w_conv_out": w_conv_out[0], "w_o": w_o[0], "w_up": w_up[0],
              "w_down": w_down[0]}
    (pay1,), (c_land,) = _ici_wait("all", c_send, c_recv, c_src, c_land, w_in_bf, "cond_wait")
    got1 = lax.dynamic_update_slice(c_land, pay1[None], (dev, 0, 0))
    c_all = got1[:, 0, :D_MODEL]
    conv_all = got1[0::2, :3, D_MODEL:]
    conv_w_full = _cols_from_shards(conv_all[:, :, :128])
    ffn_conv_w_full = _cols_from_shards(conv_all[:, :, 128:])

    cond = jnp.concatenate([c_all, c_ctx.reshape(1, D_MODEL), jnp.zeros((7, D_MODEL), F32)], axis=0)

    def f_silu(ids, v):
        return (v * _sigmoid(v),)

    (s16,) = _ew(f_silu, (1,), [(cond, _full((16, D_MODEL)))], [((16, D_MODEL), F32, _full((16, D_MODEL)), None)], "silu_cond")
    mod_sh = _mm(s16, w_ada[0], "nn", 16, 1536, D_MODEL, tm=16, tn=768, tk=D_MODEL, name="w_ada_fwd")
    m_send, m_recv, m_src, m_land, zero1 = _ici_start("all", [mod_sh], [(8, 16, 1536)], jnp.zeros((8, 128), F32),
                                                      "mod_start")
    shards["w_ukv"] = w_ukv[0] + zero1[0, 0]

    first = ["w_uq", "w_ukv"]
    gathered, early_rows, zero = _gather_weights([shards[n].astype(BF16) for n in first], shards["w_in_a"])
    full = dict(zip(first, gathered))
    (mod_mine,), (m_land,) = _ici_wait("all", m_send, m_recv, m_src, m_land, gathered[0], "mod_wait")
    got2 = lax.dynamic_update_slice(m_land, mod_mine[None], (dev, 0, 0))
    mod_all = _cols_from_shards(got2[0::2]) + b_ada
    mod_lat = lax.dynamic_slice_in_dim(mod_all, dev, 1, axis=0)
    mod_ctx = mod_all[8:9]
    xx = jnp.concatenate([x[0], ctx[0]], axis=0)
    late_groups = {"g1": ("w_in", "w_attn_out", "w_conv_out", "w_o"), "g2": ("w_up", "w_down")}
    flight = {}
    for tag, group in late_groups.items():
        bf = [(shards[n] + zero[0, 0]).astype(BF16) for n in group]
        flight[tag] = _ici_start("gather", bf, [(4,) + s.shape for s in bf], xx, "gather_" + tag + "_start")
        xx = flight[tag][4]

    def chip_stage_done(tag, x):
        send, recv, src, land, _ = flight[tag]
        src, land = _ici_wait("gather", send, recv, src, land, x, "gather_" + tag + "_wait")
        flight[tag] = _ici_start("finish", src, None, x, "finish_" + tag + "_start", lands=land)
        return flight[tag][4]

    def arrived(tag, x):
        send, recv, src, land, _ = flight[tag]
        return dict(zip(late_groups[tag], _ici_wait("finish", send, recv, src, land, x, "finish_" + tag + "_wait")[1]))

    def late_weights(point, x):
        if point == "before_attn":
            return {}, chip_stage_done("g1", x)
        if point == "after_attn":
            got = arrived("g1", x)
            wao = _cols_from_shards(got["w_attn_out"]).reshape(N_HEADS, 64, D_MODEL)
            ready = {"w_in_t": _w_in_t_p_from_shards(got["w_in"]),
                     "w_attn_out": jnp.pad(wao, ((0, 0), (64, 0), (0, 0))).reshape(N_HEADS * HEAD_PAD, D_MODEL),
                     "w_conv_out": got["w_conv_out"], "w_o": got["w_o"].reshape(D_MODEL, D_MODEL)}
            return ready, chip_stage_done("g2", x)
        got = arrived("g2", x)
        return {"w_up": got["w_up"], "w_down": got["w_down"].reshape(D_FF, D_MODEL)}, x

    wuq_t = full["w_uq"].reshape(N_HEADS, QK_DIM, Q_RANK)
    zrows = lambda n: jnp.zeros((n, D_MODEL), BF16)
    W = {
        "w_in_a_t": jnp.concatenate([early_rows[0:256], zrows(PA_Q0 - 256), early_rows[288:672], zrows(64),
                                     early_rows[256:288], zrows(32)], axis=0),
        "w_uq_t": jnp.pad(wuq_t, ((0, 0), (0, HEAD_PAD - QK_DIM), (0, 0))).reshape(N_HEADS * HEAD_PAD, Q_RANK),
        "w_ukv": full["w_ukv"],
        "norm1_g": norm1_g, "norm2_g": norm2_g, "final_g": final_g.reshape(1, D_MODEL), "q_norm_g": q_norm_g,
        "kv_norm_g": kv_norm_g, "conv_w": conv_w_full, "conv_b": conv_b, "ffn_conv_w": ffn_conv_w_full,
        "ffn_conv_b": ffn_conv_b,
    }

    place = jnp.stack([chip, mc]).astype(jnp.int32)
    early = {}

    pending = {}

    def scatter(tag, group, gs, from_sib, carry):
        if tag == "mid":
            sums = _add_pair_many(gs, from_sib, place, "rs_pair_add_mid")
        else:
            sums = [_add_pair(gs[w], from_sib[w], place, "rs_pair_add_" + n) for w, n in enumerate(group)]
        send, recv, sums, land, carry = _ici_start(
            "scatter", sums, [(3,) + s.shape[1:] for s in sums], carry, "rs_chips_" + tag + "_start")
        early[tag] = (group, send, recv, sums, land)
        return carry

    def early_grads(tag, g, carry, split=False):
        gs = list(g.values())
        if not split:
            return scatter(tag, list(g), gs, _rs_pair(gs, "rs_pair_" + tag), carry)
        send, recv, gs, land, carry = _ici_start(
            "pair", gs, [(4, s.shape[1] // 2, s.shape[2]) for s in gs], carry, "rs_pair_" + tag + "_start")
        pending[tag] = (list(g), send, recv, gs, land)
        return carry

    def early_continue(tag, carry):
        group, send, recv, gs, land = pending[tag]
        gs, from_sib = _ici_wait("pair", send, recv, gs, land, carry, "rs_pair_" + tag + "_wait")
        return scatter(tag, group, gs, from_sib, carry)

    grad_x, loss_part, gbig, gsmall = _local_step(xx, loss_target[0], mod_lat, mod_ctx, W, late_weights, early_grads,
                                                  early_continue)

    gsmall["loss"] = loss_part
    pay3 = jnp.concatenate([gsmall[n].reshape(-1) for n, _ in SMALL])
    pay3 = jnp.pad(pay3, (0, SMALL_ROWS * 128 - pay3.shape[0])).reshape(SMALL_ROWS, 128)
    s_send, s_recv, s_src, s_land, w_in_thru = _ici_start("all", [pay3], [(8, SMALL_ROWS, 128)], gbig["w_in"],
                                                         "small_start")
    gbig = {"w_in": w_in_thru}

    after_small = early_grads("last", gbig, s_src[0])

    (pay3,), (s_land,) = _ici_wait("all", s_send, s_recv, [after_small], s_land, early["last"][3][0], "small_wait")
    got3 = lax.dynamic_update_slice(s_land, pay3[None], (dev, 0, 0)).reshape(8 * SMALL_ROWS, 128)

    def f_sum8(ids, a):
        s = a[0:SMALL_ROWS]
        for d in range(1, 8):
            s = s + a[d * SMALL_ROWS:(d + 1) * SMALL_ROWS]
        return (s,)

    (vsum,) = _ew(f_sum8, (1,), [(got3, _full((8 * SMALL_ROWS, 128)))],
                  [((SMALL_ROWS, 128), F32, _full((SMALL_ROWS, 128)), None)], "sum_small")
    vflat = vsum.reshape(-1)
    gvec, off = {}, 0
    for n, size in SMALL:
        gvec[n] = vflat[off:off + size]
        off += size
    loss = gvec["loss"][0]
    dmod_rows = got3.reshape(8, SMALL_ROWS * 128)[:, :6 * D_MODEL]
    dm16 = jnp.concatenate([dmod_rows, gvec["dmod_ctx"].reshape(1, -1), jnp.zeros((7, 6 * D_MODEL), F32)], axis=0)

    def f_colsum(ids, a):
        return (_colsum(a),)

    (g_b_ada,) = _ew(f_colsum, (1,), [(dm16, _full((16, 6 * D_MODEL)))],
                     [((1, 6 * D_MODEL), F32, _full((1, 6 * D_MODEL)), None)], "b_ada_grad")
    dm_sh = lax.dynamic_slice_in_dim(dm16, chip * 1536, 1536, axis=1)
    g_w_ada = _mm(s16, dm_sh, "tn", D_MODEL, 1536, 16, tm=512, tn=768, tk=16, name="w_ada_dw")
    dcond_part = _mm(dm_sh, w_ada[0], "nt", 16, D_MODEL, 1536, tm=16, tn=512, tk=1536, name="w_ada_dx")
    d_send, d_recv, d_src, d_land, vsum = _ici_start("all", [dcond_part[8:16]], [(8, 8, D_MODEL)], vsum, "dcond_start")

    def finish_start(tags, after):
        done, halves = [], []
        for tag in tags:
            tag_names, send, recv, sums, land = early[tag]
            sums, land = _ici_wait("scatter", send, recv, sums, land, after, "rs_chips_" + tag + "_wait")
            done += tag_names
            if tag == "mid":
                halves += _add_chips_many(sums, land, place, "rs_chip_add_mid")
            else:
                halves += [_add_chips(a, b, place, "rs_chip_add_" + n) for a, b, n in zip(sums, land, tag_names)]
        send, recv, _, halves, _ = _ici_start("back", [], None, jnp.zeros((8, 128), F32), "rs_back_" + tags[0] + "_start",
                                              lands=halves)
        return done, send, recv, halves

    def finish_wait(state, after):
        done, send, recv, halves = state
        return dict(zip(done, _ici_wait("back", send, recv, [], halves, after, "rs_back_" + done[0] + "_wait")[1]))

    grads, deltas, new_m, new_v = {}, {}, {}, {}

    raw = {}

    def adam(n, w_, m_, v_, g, transposed):
        g_out, d_, m2, v2 = _adamw(w_, g, m_, v_, "adamw_" + n)
        raw[n] = d_
        back = (lambda a: jnp.transpose(a)[None]) if transposed else (lambda a: a[None])
        grads[n], deltas[n], new_m[n], new_v[n] = back(g_out), back(d_), back(m2), back(v2)

    pending_back = finish_start(["late", "mid"], grad_x)
    adam("w_ada", w_ada[0], m_w_ada[0], v_w_ada[0], g_w_ada, False)
    gw = finish_wait(pending_back, raw["w_ada"])
    for n, (w_, m_, v_) in {"w_o": (w_o, m_w_o, v_w_o), "w_up": (w_up, m_w_up, v_w_up),
                            "w_down": (w_down, m_w_down, v_w_down)}.items():
        adam(n, w_[0], m_[0], v_[0], gw[n], False)
    pending_back = finish_start(["last"], raw["w_up"])

    (dcond_mine,), (d_land,) = _ici_wait("all", d_send, d_recv, d_src, d_land, raw["w_down"], "dcond_wait")
    got4 = lax.dynamic_update_slice(d_land, dcond_mine[None], (dev, 0, 0))[0::2, 0]

    def f_c_ctx(ids, parts, cc):
        s = _sigmoid(cc)
        d = parts[0:1] + parts[1:2] + parts[2:3] + parts[3:4]
        return (d * s * (1.0 + cc * (1.0 - s)),)

    (g_c_ctx,) = _ew(f_c_ctx, (1,), [(got4, _full((4, D_MODEL))), (c_ctx.reshape(1, D_MODEL), _full((1, D_MODEL)))],
                     [((1, D_MODEL), F32, _full((1, D_MODEL)), None)], "c_ctx_grad")

    conv_w_g = lax.dynamic_slice_in_dim(gvec["conv_w"].reshape(3, CONV_DIM), chip * 128, 128, axis=1)
    ffn_conv_w_g = lax.dynamic_slice_in_dim(gvec["ffn_conv_w"].reshape(3, 2 * D_FF), chip * 1408, 1408, axis=1)
    vec_params = (("c_ctx", c_ctx, m_c_ctx, v_c_ctx, g_c_ctx), ("b_ada", b_ada, m_b_ada, v_b_ada, g_b_ada),
                  ("norm1_g", norm1_g, m_norm1_g, v_norm1_g, gvec["norm1_g"]),
                  ("q_norm_g", q_norm_g, m_q_norm_g, v_q_norm_g, gvec["q_norm_g"]),
                  ("kv_norm_g", kv_norm_g, m_kv_norm_g, v_kv_norm_g, gvec["kv_norm_g"]),
                  ("conv_w", conv_w, m_conv_w, v_conv_w, conv_w_g), ("conv_b", conv_b, m_conv_b, v_conv_b, gvec["conv_b"]),
                  ("norm2_g", norm2_g, m_norm2_g, v_norm2_g, gvec["norm2_g"]),
                  ("ffn_conv_w", ffn_conv_w, m_ffn_conv_w, v_ffn_conv_w, ffn_conv_w_g),
                  ("ffn_conv_b", ffn_conv_b, m_ffn_conv_b, v_ffn_conv_b, gvec["ffn_conv_b"]),
                  ("final_g", final_g, m_final_g, v_final_g, gvec["final_g"]))
    two_d = lambda a: a.reshape((-1, a.shape[-1]))
    many = [p + ((lambda r, s=p[1].shape: r.reshape(s)),) for p in vec_params]
    for n, w_, m_, v_ in (("w_ukv", w_ukv, m_w_ukv, v_w_ukv), ("w_attn_out", w_attn_out, m_w_attn_out, v_w_attn_out),
                          ("w_conv_out", w_conv_out, m_w_conv_out, v_w_conv_out)):
        many.append((n, w_, m_, v_, gw[n], (lambda r, s=w_.shape: r.reshape(s))))
    many.append(("w_uq", w_uq_t, m_w_uq_t, v_w_uq_t, gw["w_uq"], lambda r: jnp.transpose(r)[None]))

    def f_adam_many(ids, *vals):
        out = []
        for k in range(len(many)):
            out += [vals[4 * k + 1]] + _adam_update(*vals[4 * k:4 * k + 4])
        return out

    ins_v, outs_v = [], []
    for p in many:
        shp = two_d(p[1]).shape
        ins_v += [(two_d(a), _full(shp)) for a in (p[1], p[4], p[2], p[3])]
        outs_v += [(shp, F32, _full(shp), None)] * 4
    res_v = _ew(f_adam_many, (1,), ins_v, outs_v, "adamw_small")
    for k, p in enumerate(many):
        n, post = p[0], p[5]
        grads[n], deltas[n], new_m[n], new_v[n] = (post(r) for r in res_v[4 * k:4 * k + 4])

    gw_in = finish_wait(pending_back, res_v[0])
    adam("w_in", w_in_t, m_w_in_t, v_w_in_t, gw_in["w_in"], True)

    order = ("c_ctx", "w_ada", "b_ada", "norm1_g", "w_in", "q_norm_g", "kv_norm_g", "w_uq", "w_ukv", "conv_w", "conv_b",
             "w_attn_out", "w_conv_out", "w_o", "norm2_g", "w_up", "ffn_conv_w", "ffn_conv_b", "w_down", "final_g")
    return (loss, grad_x[None], *[grads[n] for n in order], *[deltas[n] for n in order],
            *[new_m[n] for n in order], *[new_v[n] for n in order])
```

```python
import functools

import jax
import jax.numpy as jnp
import numpy as np
from jax import lax
from jax.experimental import pallas as pl
from jax.experimental.pallas import tpu as pltpu

F32, BF16 = jnp.float32, jnp.bfloat16
MESH = pl.DeviceIdType.MESH

D_MODEL = 1024
N_HEADS = 8
HEAD_PAD = 128
QK_DIM = 96
Q_RANK, KV_RANK = 384, 256
CONV_DIM = 512
D_FF = 2816
GRID_W = 64
ROPE_THETA = 10000.0
EPS = 1e-6
GA0, GC0, CX0, CB0, CC0, KV0, Q0, KR0, P_COLS = 0, 1024, 2048, 2560, 3072, 3584, 3840, 4224, 4352
PA_KV0, PA_Q0, PA_KR0, PA_COLS = 0, 384, 768, 896
ROW_TILE = 256
VMEM_LIMIT_BYTES = 48 * 1024 * 1024

ADAM_LR, ADAM_B1, ADAM_B2, ADAM_EPS, ADAM_WD, ADAM_STEP = 0.001, 0.9, 0.999, 1e-08, 0.01, 10

NN = (((1,), (0,)), ((), ()))
NT = (((1,), (1,)), ((), ()))
TN = (((0,), (0,)), ((), ()))


def _cp(sem):
    return pltpu.CompilerParams(dimension_semantics=sem, vmem_limit_bytes=VMEM_LIMIT_BYTES)


PIN_BYTES = 1 << 19


def _in_hbm(arrays):
    return [pltpu.with_memory_space_constraint(a, pltpu.HBM) if a.size * a.dtype.itemsize >= PIN_BYTES else a
            for a in arrays]


def _out(shape, dtype):
    n = 1
    for d in shape:
        n *= d
    big = n * jnp.dtype(dtype).itemsize >= PIN_BYTES
    return pltpu.HBM(shape, dtype) if big else jax.ShapeDtypeStruct(shape, dtype)


def _pick(n, prefs):
    for p in prefs:
        if n % p == 0:
            return p
    return n


def _mm(a, b, mode, M, N, K, *, tm, tn, tk, name, out_dtype=F32, a_spec=None, b_spec=None, o_spec=None,
        out_shape=None, transpose_out=False):
    assert M % tm == 0 and N % tn == 0 and K % tk == 0, (name, M, N, K, tm, tn, tk)
    nk = K // tk
    dims = {"nn": NN, "nt": NT, "tn": TN}[mode]
    if a_spec is None:
        a_spec = (pl.BlockSpec((tk, tm), lambda i, j, k: (k, i)) if mode == "tn"
                  else pl.BlockSpec((tm, tk), lambda i, j, k: (i, k)))
    if b_spec is None:
        b_spec = (pl.BlockSpec((tn, tk), lambda i, j, k: (j, k)) if mode == "nt"
                  else pl.BlockSpec((tk, tn), lambda i, j, k: (k, j)))
    if o_spec is None:
        o_spec = (pl.BlockSpec((tn, tm), lambda i, j, k: (j, i)) if transpose_out
                  else pl.BlockSpec((tm, tn), lambda i, j, k: (i, j)))
    if out_shape is None:
        out_shape = (N, M) if transpose_out else (M, N)

    def emit(o_ref, val):
        o_ref[...] = (val.T if transpose_out else val).astype(o_ref.dtype)

    def body(a_ref, b_ref, o_ref, *scratch):
        part = lax.dot_general(a_ref[...].astype(BF16), b_ref[...].astype(BF16), dims, preferred_element_type=F32)
        if nk == 1:
            emit(o_ref, part)
            return
        acc_ref, = scratch
        k = pl.program_id(2)

        @pl.when(k == 0)
        def _():
            acc_ref[...] = part

        @pl.when((k > 0) & (k < nk - 1))
        def _():
            acc_ref[...] += part

        @pl.when(k == nk - 1)
        def _():
            emit(o_ref, acc_ref[...] + part)

    return pl.pallas_call(
        body, grid=(M // tm, N // tn, nk), in_specs=[a_spec, b_spec], out_specs=o_spec,
        out_shape=_out(out_shape, out_dtype),
        scratch_shapes=[pltpu.VMEM((tm, tn), F32)] if nk > 1 else [],
        compiler_params=_cp(("parallel", "parallel", "arbitrary")), name=name)(*_in_hbm([a, b]))


def _ew(fn, grid, ins, outs, name, scalars=None):
    n_in = len(ins)
    n_sc = 0 if scalars is None else 1

    def store(ref, val, acc, ids):
        if isinstance(val, (list, tuple)):
            for h, v in enumerate(val):
                ref[h] = v.astype(ref.dtype)
            return
        if acc is None:
            ref[...] = val.astype(ref.dtype)
            return

        @pl.when(ids[acc] == 0)
        def _():
            ref[...] = val.astype(ref.dtype)

        @pl.when(ids[acc] > 0)
        def _():
            ref[...] += val.astype(ref.dtype)

    def body(*refs):
        refs = refs[n_sc:]
        ids = tuple(pl.program_id(a) for a in range(len(grid)))
        vals = fn(ids, *[r[...] for r in refs[:n_in]])
        for ref, val, (_, _, _, acc) in zip(refs[n_in:], vals, outs):
            store(ref, val, acc, ids)

    acc_axes = {o[3] for o in outs if o[3] is not None}
    sem = tuple("arbitrary" if a in acc_axes else "parallel" for a in range(len(grid)))
    in_specs, out_specs = [s for _, s in ins], [o[2] for o in outs]
    out_shape = [_out(o[0], o[1]) for o in outs]
    args = _in_hbm([a for a, _ in ins])
    if scalars is None:
        return pl.pallas_call(body, grid=grid, in_specs=in_specs, out_specs=out_specs, out_shape=out_shape,
                              compiler_params=_cp(sem), name=name)(*args)
    spec = pltpu.PrefetchScalarGridSpec(num_scalar_prefetch=1, grid=grid, in_specs=in_specs, out_specs=out_specs)
    return pl.pallas_call(body, grid_spec=spec, out_shape=out_shape, compiler_params=_cp(sem), name=name)(scalars, *args)


def _rows(width, cblk=0, roff=0, tr=ROW_TILE):
    return pl.BlockSpec((tr, width), lambda i: (i + roff, cblk))


def _full(shape):
    nd = len(shape)
    return pl.BlockSpec(shape, lambda *_: (0,) * nd)


def _sigmoid(x):
    return 1.0 / (1.0 + jnp.exp2(x * (-1.4426950408889634)))


def _rms(x):
    return lax.rsqrt(jnp.mean(x * x, axis=-1, keepdims=True) + EPS)


def _rms_bwd(dn, xn, r):
    return r * (dn - xn * jnp.mean(dn * xn, axis=-1, keepdims=True))


def _colsum(x):
    return jnp.sum(x, axis=0, keepdims=True)


def _shifts(x):
    n = x.shape[0]
    rows = lax.broadcasted_iota(jnp.int32, x.shape, 0)
    return jnp.where(rows == 0, 0.0, pltpu.roll(x, 1, 0)), jnp.where(rows == n - 1, 0.0, pltpu.roll(x, n - 1, 0))


def _conv(x, w, b, shifted=None):
    prev, nxt = _shifts(x) if shifted is None else shifted
    return b + prev * w[0:1] + x * w[1:2] + nxt * w[2:3]


def _conv_bwd_x(dy, w):
    prev, nxt = _shifts(dy)
    return nxt * w[0:1] + dy * w[1:2] + prev * w[2:3]


def _conv_bwd_w(dy, x, shifted):
    prev, nxt = shifted
    return _colsum(dy * prev), _colsum(dy * x), _colsum(dy * nxt)


def _rope(x, cos, sin_lo, sin_hi):
    return x * cos + pltpu.roll(x, HEAD_PAD - 8, 1) * sin_lo + pltpu.roll(x, 8, 1) * sin_hi


ATTN_SCALE = QK_DIM ** -0.5
LOG2_E = 1.4426950408889634


def _rope_t(x, tab, inverse=False):
    o = 3 * HEAD_PAD if inverse else 0
    return _rope(x, tab[:, o:o + HEAD_PAD], tab[:, o + HEAD_PAD:o + 2 * HEAD_PAD], tab[:, o + 2 * HEAD_PAD:o + 3 * HEAD_PAD])


def _heads_keys(hp, kv_ref, kr_ref, tab_ref, kc_ref, vp_ref):
    kr_roped = _rope_t(kr_ref[...], tab_ref[...])
    lane = lax.broadcasted_iota(jnp.int32, kr_roped.shape, 1)
    for u in range(hp):
        kv = kv_ref[:, u * HEAD_PAD:(u + 1) * HEAD_PAD]
        kc_ref[u] = jnp.where(lane < 64, kv, kr_roped).astype(BF16)
        vp_ref[u] = jnp.where(lane >= 64, kv, 0.0).astype(BF16)


ATTN_Q_TILE = 512
ATTN_HEADS_PER_STEP = 2


def _attn_specs(tq, TT):
    q = pl.BlockSpec((tq, HEAD_PAD), lambda h, i: (i, h))
    keys = pl.BlockSpec((TT, HEAD_PAD), lambda h, i: (0, h))
    kr = pl.BlockSpec((TT, HEAD_PAD), lambda h, i: (0, PA_KR0 // HEAD_PAD))
    tab_q = pl.BlockSpec((tq, 6 * HEAD_PAD), lambda h, i: (i, 0))
    tab_k = pl.BlockSpec((TT, 6 * HEAD_PAD), lambda h, i: (0, 0))
    return q, keys, kr, tab_q, tab_k


def _attn_fwd(q_raw, kv, pp, tab, T, TT):
    tq, hp = ROW_TILE, 2 * ATTN_HEADS_PER_STEP
    w = hp * HEAD_PAD

    def body(q_ref, kv_ref, kr_ref, tq_ref, tk_ref, o_ref, kc, vp):
        @pl.when(pl.program_id(1) == 0)
        def _():
            _heads_keys(hp, kv_ref, kr_ref, tk_ref, kc, vp)

        tab = tq_ref[...]
        for u in range(hp):
            cols = slice(u * HEAD_PAD, (u + 1) * HEAD_PAD)
            q = _rope_t(q_ref[:, cols], tab).astype(BF16)
            s = lax.dot_general(q, kc[u], NT, preferred_element_type=F32)
            m = jnp.max(s, axis=-1, keepdims=True)
            p = jnp.exp2((s - m) * (ATTN_SCALE * LOG2_E))
            l = jnp.sum(p, axis=-1, keepdims=True)
            o = lax.dot_general(p.astype(BF16), vp[u], NN, preferred_element_type=F32)
            lane = lax.broadcasted_iota(jnp.int32, o.shape, 1)
            o_ref[:, cols] = jnp.where(lane < 64, m * ATTN_SCALE + jnp.log(l), o / l)

    _, _, kr, _, _ = _attn_specs(tq, TT)
    qs = pl.BlockSpec((tq, w), lambda h, i: (i, h))
    keys = pl.BlockSpec((TT, w), lambda h, i: (0, h))
    tab_q = pl.BlockSpec((tq, 3 * HEAD_PAD), lambda h, i: (i, 0))
    tab_k = pl.BlockSpec((TT, 3 * HEAD_PAD), lambda h, i: (0, 0))
    return pl.pallas_call(
        body, grid=(N_HEADS // hp, T // tq), in_specs=[qs, keys, kr, tab_q, tab_k], out_specs=qs,
        out_shape=jax.ShapeDtypeStruct((T, N_HEADS * HEAD_PAD), F32),
        scratch_shapes=[pltpu.VMEM((hp, TT, HEAD_PAD), BF16), pltpu.VMEM((hp, TT, HEAD_PAD), BF16)],
        compiler_params=_cp(("parallel", "arbitrary")), name="attn_fwd",
    )(*_in_hbm([q_raw, kv, pp, tab, tab]))


def _attn_bwd(q_raw, kv, pp, o, do, tab, T, TT):
    tq = _pick(T, (ATTN_Q_TILE, ROW_TILE))
    nq = T // tq
    hp = ATTN_HEADS_PER_STEP
    w = hp * HEAD_PAD

    def body(q_ref, kv_ref, kr_ref, tq_ref, tk_ref, o_ref, do_ref, dq_ref, dkv_ref, dkr_ref, kc, vp, dk, dv):
        g, i = pl.program_id(0), pl.program_id(1)

        @pl.when(i == 0)
        def _():
            _heads_keys(hp, kv_ref, kr_ref, tk_ref, kc, vp)
            dk[...] = jnp.zeros_like(dk)
            dv[...] = jnp.zeros_like(dv)

        tab = tq_ref[...]
        for u in range(hp):
            cols = slice(u * HEAD_PAD, (u + 1) * HEAD_PAD)
            q = _rope_t(q_ref[:, cols], tab).astype(BF16)
            k, v, d_o = kc[u], vp[u], do_ref[:, cols]
            s = lax.dot_general(q, k, NT, preferred_element_type=F32)
            o = o_ref[:, cols]
            p = jnp.exp2(s * (ATTN_SCALE * LOG2_E) - o[:, 0:1] * LOG2_E)
            dob = d_o.astype(BF16)
            dp = lax.dot_general(dob, v, NT, preferred_element_type=F32)
            dd = jnp.sum(d_o * o, axis=-1, keepdims=True)
            ds = (p * (dp - dd) * ATTN_SCALE).astype(BF16)
            dq = lax.dot_general(ds, k, NN, preferred_element_type=F32)
            dq_ref[:, cols] = _rope_t(dq, tab, inverse=True).astype(dq_ref.dtype)
            dk[u] += lax.dot_general(q, ds, TN, preferred_element_type=F32)
            dv[u] += lax.dot_general(dob, p.astype(BF16), TN, preferred_element_type=F32)

        @pl.when(i == nq - 1)
        def _():
            rot = None
            for u in range(hp):
                dkh = dk[u].T
                lane = lax.broadcasted_iota(jnp.int32, dkh.shape, 1)
                dkv_ref[:, u * HEAD_PAD:(u + 1) * HEAD_PAD] = jnp.where(lane < 64, dkh, dv[u].T).astype(dkv_ref.dtype)
                part = jnp.where((lane >= 64) & (lane < 96), dkh, 0.0)
                rot = part if rot is None else rot + part
            rot = _rope_t(rot, tk_ref[...], inverse=True)

            @pl.when(g == 0)
            def _():
                dkr_ref[...] = rot

            @pl.when(g > 0)
            def _():
                dkr_ref[...] += rot

    _, _, kr, tab_q, tab_k = _attn_specs(tq, TT)
    qs = pl.BlockSpec((tq, w), lambda h, i: (i, h))
    keys = pl.BlockSpec((TT, w), lambda h, i: (0, h))
    wide = lambda rows: jax.ShapeDtypeStruct((rows, N_HEADS * HEAD_PAD), BF16)
    return pl.pallas_call(
        body, grid=(N_HEADS // hp, nq),
        in_specs=[qs, keys, kr, tab_q, tab_k, qs, qs],
        out_specs=[qs, keys, pl.BlockSpec((TT, HEAD_PAD), lambda h, i: (0, 0))],
        out_shape=[wide(T), wide(TT), jax.ShapeDtypeStruct((TT, HEAD_PAD), F32)],
        scratch_shapes=[pltpu.VMEM((hp, TT, HEAD_PAD), BF16), pltpu.VMEM((hp, TT, HEAD_PAD), BF16),
                        pltpu.VMEM((hp, HEAD_PAD, TT), F32), pltpu.VMEM((hp, HEAD_PAD, TT), F32)],
        compiler_params=_cp(("arbitrary", "arbitrary")), name="attn_bwd",
    )(*_in_hbm([q_raw, kv, pp, tab, tab, o, do]))


def _hbm_specs(n):
    return [pl.BlockSpec(memory_space=pl.ANY)] * n


def _gather_weights(shards):
    n = len(shards)
    halves = [s.shape[0] // 2 for s in shards]

    def body(*refs):
        ins, outs = refs[:n], refs[n:2 * n]
        token, send_sems, recv_sems = refs[2 * n:]
        token[...] = jnp.zeros_like(token)
        mx, my, mc = lax.axis_index("x"), lax.axis_index("y"), lax.axis_index("c")
        j_me = 2 * mx + my
        chips = [(1 - mx, my), (mx, 1 - my), (1 - mx, 1 - my)]

        def half(w, chip_idx, hc):
            return outs[w].at[chip_idx, pl.ds(hc * halves[w], halves[w]), :]

        def copy(w, k, src, dst, to):
            return pltpu.make_async_remote_copy(src_ref=src, dst_ref=dst, send_sem=send_sems.at[w, k],
                                                recv_sem=recv_sems.at[w, k], device_id=to, device_id_type=MESH)

        sends = []
        for w in range(n):
            cp = copy(w, 6, ins[w], outs[w].at[j_me], (mx, my, 1 - mc))
            cp.start()
            sends.append(cp)
        for k, (px, py) in enumerate(chips):
            for w in range(n):
                cp = copy(w, k, ins[w].at[pl.ds(mc * halves[w], halves[w]), :], half(w, j_me, mc), (px, py, mc))
                cp.start()
                sends.append(cp)
        for k, (px, py) in enumerate(chips):
            for w in range(n):
                got = half(w, 2 * px + py, mc)
                copy(w, k, got, got, (px, py, mc)).wait_recv()
                cp = copy(w, 3 + k, got, got, (mx, my, 1 - mc))
                cp.start()
                sends.append(cp)
        for k, (px, py) in enumerate(chips):
            for w in range(n):
                got = half(w, 2 * px + py, 1 - mc)
                copy(w, 3 + k, got, got, (mx, my, 1 - mc)).wait_recv()
        for w in range(n):
            own = outs[w].at[j_me]
            copy(w, 6, own, own, (mx, my, 1 - mc)).wait_recv()
        for cp in sends:
            cp.wait_send()

    res = pl.pallas_call(
        body, out_shape=[jax.ShapeDtypeStruct((4,) + s.shape, s.dtype) for s in shards]
        + [jax.ShapeDtypeStruct((8, 128), F32)],
        in_specs=_hbm_specs(n), out_specs=_hbm_specs(n) + [pl.BlockSpec(memory_space=pltpu.VMEM)],
        scratch_shapes=[pltpu.SemaphoreType.DMA((n, 7)), pltpu.SemaphoreType.DMA((n, 7))],
        name="gather_weights")(*shards)
    return list(res[:n]), res[n]


def _rs_pair(gs, name):
    n = len(gs)
    halves = [g.shape[1] // 2 for g in gs]

    def body(*refs):
        ins, lands = refs[:n], refs[n:2 * n]
        send_sems, recv_sems = refs[2 * n:]
        mx, my, mc = lax.axis_index("x"), lax.axis_index("y"), lax.axis_index("c")
        copies = []
        for w in range(n):
            h = halves[w]
            cp = pltpu.make_async_remote_copy(
                src_ref=ins[w].at[:, pl.ds((1 - mc) * h, h), :], dst_ref=lands[w], send_sem=send_sems.at[w],
                recv_sem=recv_sems.at[w], device_id=(mx, my, 1 - mc), device_id_type=MESH)
            cp.start()
            copies.append(cp)
        for cp in copies:
            cp.wait()

    return pl.pallas_call(
        body, out_shape=[jax.ShapeDtypeStruct((4, h, g.shape[2]), g.dtype) for g, h in zip(gs, halves)],
        in_specs=_hbm_specs(n), out_specs=_hbm_specs(n),
        scratch_shapes=[pltpu.SemaphoreType.DMA((n,)), pltpu.SemaphoreType.DMA((n,))], name=name)(*gs)


def _rs_chips(parts):
    n = len(parts)

    def body(*refs):
        ins, lands = refs[:n], refs[n:2 * n]
        send_sems, recv_sems = refs[2 * n:]
        mx, my, mc = lax.axis_index("x"), lax.axis_index("y"), lax.axis_index("c")
        copies = []
        for k, (px, py) in enumerate([(1 - mx, my), (mx, 1 - my), (1 - mx, 1 - my)]):
            for w in range(n):
                cp = pltpu.make_async_remote_copy(
                    src_ref=ins[w].at[2 * px + py], dst_ref=lands[w].at[k], send_sem=send_sems.at[w, k],
                    recv_sem=recv_sems.at[w, k], device_id=(px, py, mc), device_id_type=MESH)
                cp.start()
                copies.append(cp)
        for cp in copies:
            cp.wait()

    return list(pl.pallas_call(
        body, out_shape=[jax.ShapeDtypeStruct((3,) + p.shape[1:], p.dtype) for p in parts],
        in_specs=_hbm_specs(n), out_specs=_hbm_specs(n),
        scratch_shapes=[pltpu.SemaphoreType.DMA((n, 3)), pltpu.SemaphoreType.DMA((n, 3))], name="rs_chips")(*parts))


_HBM = pl.BlockSpec(memory_space=pltpu.HBM)
_SEM = pl.BlockSpec(memory_space=pltpu.SEMAPHORE)
_EFFECT = pltpu.SideEffectType.DATAFLOW_SIDE_EFFECTING


def _ici_copies(kind, srcs, lands, send_sems, recv_sems):
    n = len(lands)
    mx, my, mc = lax.axis_index("x"), lax.axis_index("y"), lax.axis_index("c")
    j_me = 2 * mx + my
    copies = []
    if kind == "back":
        for w in range(n):
            h = lands[w].shape[0] // 2
            mine = lands[w].at[pl.ds(mc * h, h), :]
            copies.append(pltpu.make_async_remote_copy(
                src_ref=mine, dst_ref=mine, send_sem=send_sems.at[w], recv_sem=recv_sems.at[w],
                device_id=(mx, my, 1 - mc), device_id_type=MESH))
        return copies
    if kind == "all":
        for k in range(7):
            a, b, c = (k + 1) >> 2 & 1, (k + 1) >> 1 & 1, (k + 1) & 1
            peer = (1 - mx if a else mx, 1 - my if b else my, 1 - mc if c else mc)
            for w in range(n):
                copies.append(pltpu.make_async_remote_copy(
                    src_ref=srcs[w], dst_ref=lands[w].at[4 * mx + 2 * my + mc], send_sem=send_sems.at[7 * w + k],
                    recv_sem=recv_sems.at[7 * w + k], device_id=peer, device_id_type=MESH))
        return copies
    if kind == "pair":
        for w in range(n):
            h = srcs[w].shape[1] // 2
            copies.append(pltpu.make_async_remote_copy(
                src_ref=srcs[w].at[:, pl.ds((1 - mc) * h, h), :], dst_ref=lands[w], send_sem=send_sems.at[w],
                recv_sem=recv_sems.at[w], device_id=(mx, my, 1 - mc), device_id_type=MESH))
        return copies
    chips = [(1 - mx, my), (mx, 1 - my), (1 - mx, 1 - my)]
    if kind == "finish":
        for w in range(n):
            h = srcs[w].shape[0] // 2
            pushes = [(lands[w].at[2 * px + py, pl.ds(mc * h, h), :],) * 2 for px, py in chips]
            pushes.append((srcs[w], lands[w].at[j_me]))
            for k, (src, dst) in enumerate(pushes):
                copies.append(pltpu.make_async_remote_copy(
                    src_ref=src, dst_ref=dst, send_sem=send_sems.at[4 * w + k], recv_sem=recv_sems.at[4 * w + k],
                    device_id=(mx, my, 1 - mc), device_id_type=MESH))
        return copies
    for k, (px, py) in enumerate(chips):
        for w in range(n):
            if kind == "gather":
                h = srcs[w].shape[0] // 2
                src, dst = srcs[w].at[pl.ds(mc * h, h), :], lands[w].at[j_me, pl.ds(mc * h, h), :]
            else:
                src, dst = srcs[w].at[2 * px + py], lands[w].at[k]
            copies.append(pltpu.make_async_remote_copy(
                src_ref=src, dst_ref=dst, send_sem=send_sems.at[3 * w + k], recv_sem=recv_sems.at[3 * w + k],
                device_id=(px, py, mc), device_id_type=MESH))
    return copies


_SEMS_PER_OPERAND = {"gather": 3, "scatter": 3, "all": 7, "pair": 1, "finish": 4, "back": 1}


def _ici_start(kind, srcs, land_shapes, carry, name, lands=None):
    hbm = lambda a: pltpu.with_memory_space_constraint(a, pltpu.HBM)
    if lands is None:
        lands = [lax.empty(s, srcs[0].dtype) for s in land_shapes]
    ns, nl = len(srcs), len(lands)

    def body(*refs):
        send_sems, recv_sems = refs[ns + nl + 1], refs[ns + nl + 2]
        for cp in _ici_copies(kind, refs[:ns], refs[ns:ns + nl], send_sems, recv_sems):
            cp.start()

    args = [hbm(a) for a in list(srcs) + list(lands) + [carry]]
    n_sem = _SEMS_PER_OPERAND[kind] * nl
    out_shape = ([pltpu.SemaphoreType.DMA((n_sem,)), pltpu.SemaphoreType.DMA((n_sem,))]
                 + [pltpu.HBM(a.shape, a.dtype) for a in args])
    res = pl.pallas_call(
        body, name=name, out_shape=out_shape, in_specs=[_HBM] * len(args), out_specs=[_SEM, _SEM] + [_HBM] * len(args),
        input_output_aliases={i: 2 + i for i in range(len(args))},
        compiler_params=pltpu.CompilerParams(has_side_effects=_EFFECT))(*args)
    return res[0], res[1], list(res[2:2 + ns]), list(res[2 + ns:2 + ns + nl]), res[2 + ns + nl]


def _ici_wait(kind, send_sems, recv_sems, srcs, lands, after, name):
    ns, nl = len(srcs), len(lands)

    def body(*refs):
        for cp in _ici_copies(kind, refs[:ns], refs[ns:ns + nl], refs[ns + nl], refs[ns + nl + 1]):
            cp.wait_send()
            cp.wait_recv()

    args = list(srcs) + list(lands)
    res = pl.pallas_call(
        body, name=name, out_shape=[pltpu.HBM(a.shape, a.dtype) for a in args],
        in_specs=[_HBM] * len(args) + [_SEM, _SEM, pl.BlockSpec(memory_space=pl.ANY)], out_specs=[_HBM] * len(args),
        input_output_aliases={i: i for i in range(len(args))},
        compiler_params=pltpu.CompilerParams(has_side_effects=_EFFECT))(*args, send_sems, recv_sems, after)
    return list(res[:ns]), list(res[ns:])


def _tile_rows(h, c, itemsize, mult):
    best = h
    for t in range(mult, h + 1, mult):
        if h % t == 0 and t * c * itemsize <= (1 << 21):
            best = t
    return best


def _add_pair(g, land, place, name):
    _, h, c = land.shape
    t = _tile_rows(h, c, 2, 16)
    nb = h // t
    return _ew(lambda ids, u, v: (u.astype(F32) + v.astype(F32),), (4, nb),
               [(g, pl.BlockSpec((None, t, c), lambda j, i, s: (j, s[1] * nb + i, 0))),
                (land, pl.BlockSpec((None, t, c), lambda j, i, s: (j, i, 0)))],
               [(land.shape, BF16, pl.BlockSpec((None, t, c), lambda j, i, s: (j, i, 0)), None)], name, scalars=place)[0]


def _add_pair_many(gs, lands, place, name):
    ins, outs = [], []
    for g, l in zip(gs, lands):
        ins += [(g, pl.BlockSpec(l.shape, lambda i, s: (0, s[1], 0))), (l, pl.BlockSpec(l.shape, lambda i, s: (0, 0, 0)))]
        outs.append((l.shape, BF16, pl.BlockSpec(l.shape, lambda i, s: (0, 0, 0)), None))
    fn = lambda ids, *v: [v[2 * k].astype(F32) + v[2 * k + 1].astype(F32) for k in range(len(gs))]
    return list(_ew(fn, (1,), ins, outs, name, scalars=place))


def _add_chips_many(owns, lands, place, name):
    ins, outs = [], []
    for own, land in zip(owns, lands):
        _, h, c = land.shape
        ins += [(own, pl.BlockSpec((None, h, c), lambda i, s: (s[0], 0, 0))),
                (land, pl.BlockSpec((3, h, c), lambda i, s: (0, 0, 0)))]
        outs.append(((2 * h, c), F32, pl.BlockSpec((h, c), lambda i, s: (s[1], 0)), None))

    def fn(ids, *v):
        return [((v[2 * k].astype(F32) + v[2 * k + 1][0].astype(F32)) + v[2 * k + 1][1].astype(F32))
                + v[2 * k + 1][2].astype(F32) for k in range(len(owns))]

    return list(_ew(fn, (1,), ins, outs, name, scalars=place))


def _add_chips(own, land, place, name):
    _, h, c = land.shape
    t = _tile_rows(h, c, 4, 16)
    nb = h // t

    def fn(ids, a, b):
        return (((a.astype(F32) + b[0].astype(F32)) + b[1].astype(F32)) + b[2].astype(F32),)

    return _ew(fn, (nb,), [(own, pl.BlockSpec((None, t, c), lambda i, s: (s[0], i, 0))),
                           (land, pl.BlockSpec((3, t, c), lambda i, s: (0, i, 0)))],
               [((2 * h, c), F32, pl.BlockSpec((t, c), lambda i, s: (s[1] * nb + i, 0)), None)], name, scalars=place)[0]


W_IN_SEGMENTS = ((0, 256, KV0), (256, 288, KR0 + 64), (288, 672, Q0), (672, 1184, CX0), (1184, 1696, CB0),
                 (1696, 2208, CC0), (2208, 3232, GA0), (3232, 4256, GC0))
W_IN_SHARD = 1064


W_IN_SHARD_PAD = 1088
W_IN_EARLY = 672


def _w_in_t_p_from_shards(s):
    pieces = []
    for o0, o1, p0 in sorted(W_IN_SEGMENTS, key=lambda t: t[2]):
        if p0 == KR0 + 64:
            pieces.append(jnp.zeros((64, s.shape[2]), s.dtype))
        for j in range(4):
            lo, hi = max(o0, j * W_IN_SHARD), min(o1, (j + 1) * W_IN_SHARD)
            if lo < hi:
                pieces.append(s[j, lo - j * W_IN_SHARD:hi - j * W_IN_SHARD])
    pieces.append(jnp.zeros((32, s.shape[2]), s.dtype))
    return jnp.concatenate(pieces, axis=0)


def _w_in_t_shards_from_p(g):
    shards = []
    for j in range(4):
        pieces = []
        for o0, o1, p0 in W_IN_SEGMENTS:
            lo, hi = max(o0, j * W_IN_SHARD), min(o1, (j + 1) * W_IN_SHARD)
            if lo < hi:
                pieces.append(g[p0 + lo - o0:p0 + hi - o0])
        pieces.append(jnp.zeros((W_IN_SHARD_PAD - W_IN_SHARD, g.shape[1]), g.dtype))
        shards.append(jnp.concatenate(pieces, axis=0))
    return jnp.stack(shards, axis=0)


def _cols_from_shards(s):
    return jnp.transpose(s, (1, 0, 2)).reshape(s.shape[1], -1)


def _rope_tables(T, TT, inverse):
    f32 = np.float32
    rows = T // GRID_W
    row = np.repeat(np.arange(rows), GRID_W).astype(f32)
    col = np.tile(np.arange(GRID_W), rows).astype(f32)
    inv = (f32(ROPE_THETA) ** (-np.arange(0, 16, 2, dtype=f32) / f32(16))).astype(f32)
    ang = np.concatenate([row[:, None] * inv, col[:, None] * inv], axis=-1).astype(f32)
    cos, sin = np.cos(ang).astype(f32), np.sin(ang).astype(f32)
    lane = np.arange(32)
    src = (lane // 16) * 8 + lane % 8
    lo = ((lane % 16) // 8 == 0).astype(f32)
    sgn = f32(-1.0 if inverse else 1.0)
    cos32 = cos[:, src]
    sin_lo32 = -sgn * sin[:, src] * lo
    sin_hi32 = sgn * sin[:, src] * (1 - lo)

    def widen(t32, fill):
        t = np.concatenate([np.full((T, 64), fill, f32), t32, np.full((T, 32), fill, f32)], axis=1)
        return np.concatenate([t, np.full((TT - T, HEAD_PAD), fill, f32)], axis=0)

    return [widen(cos32, 1.0), widen(sin_lo32, 0.0), widen(sin_hi32, 0.0)]


def _rope_table(T, TT):
    return jnp.asarray(np.concatenate(_rope_tables(T, TT, False) + _rope_tables(T, TT, True), axis=1))


def _local_step(xx, tgt, mod_lat, mod_ctx, W, late_weights, early_grads, early_continue):
    TT = xx.shape[0]
    T = tgt.shape[0]
    n_lat, n_all = T // ROW_TILE, TT // ROW_TILE
    sh1, sc1, g1, sh2, sc2, g2 = [mod_lat[:, k * D_MODEL:(k + 1) * D_MODEL] for k in range(6)]
    csh1, csc1 = mod_ctx[:, :D_MODEL], mod_ctx[:, D_MODEL:2 * D_MODEL]
    vec = lambda n: _full((1, n))
    row_out = lambda n, dt, rows=T: ((rows, n), dt, _rows(n), None)
    acc_out = lambda n: ((1, n), F32, _full((1, n)), 0)
    lt = _pick(T, (2 * ROW_TILE, ROW_TILE))
    n_lt = T // lt
    lrows = lambda n, cblk=0: _rows(n, cblk, 0, lt)
    lrow_out = lambda n, dt: ((T, n), dt, lrows(n), None)

    def f_norm1(ids, x, g, a_sh, a_sc, b_sh, b_sc):
        ctx = ids[0] >= n_lat
        sh, sc = jnp.where(ctx, b_sh, a_sh), jnp.where(ctx, b_sc, a_sc)
        return ((x * _rms(x) * g) * (1.0 + sc) + sh,)

    (hh,) = _ew(f_norm1, (n_all,), [(xx, _rows(D_MODEL)), (W["norm1_g"], vec(D_MODEL)), (sh1, vec(D_MODEL)),
                                   (sc1, vec(D_MODEL)), (csh1, vec(D_MODEL)), (csc1, vec(D_MODEL))],
                [row_out(D_MODEL, BF16, TT)], "norm1_fwd")
    tm_all = _pick(TT, (768, 256))
    pp_a = _mm(hh, W["w_in_a_t"], "nt", TT, PA_COLS, D_MODEL, tm=tm_all, tn=PA_COLS, tk=D_MODEL, name="w_in_fwd_a")

    def f_lowrank(ids, ckv, cq, gkv, gq):
        return ckv * _rms(ckv) * gkv, cq * _rms(cq) * gq

    nkv, nq = _ew(f_lowrank, (n_all,), [(pp_a, _rows(KV_RANK, PA_KV0 // KV_RANK)), (pp_a, _rows(Q_RANK, PA_Q0 // Q_RANK)),
                                       (W["kv_norm_g"], vec(KV_RANK)), (W["q_norm_g"], vec(Q_RANK))],
                  [row_out(KV_RANK, BF16, TT), row_out(Q_RANK, BF16, TT)], "lowrank_norm_fwd")
    kv = _mm(nkv, W["w_ukv"], "nn", TT, 1024, KV_RANK, tm=tm_all, tn=256, tk=KV_RANK, name="w_ukv_fwd",
             b_spec=pl.BlockSpec((None, KV_RANK, 256), lambda i, j, k: (j, k, 0)))
    q_raw = _mm(nq, W["w_uq_t"], "nt", TT, 1024, Q_RANK, tm=tm_all, tn=1024, tk=Q_RANK, name="w_uq_fwd")

    tab = _rope_table(T, TT)
    _, q_raw = late_weights("before_attn", q_raw)
    o_pad = _attn_fwd(q_raw, kv, pp_a, tab, T, TT)
    arrived, o_pad = late_weights("after_attn", o_pad)
    W = dict(W, **arrived)
    tm_lat = _pick(T, (1024, 512, 256))
    pp = _mm(hh, W["w_in_t"], "nt", T, KV0, D_MODEL, tm=tm_lat, tn=KV0 // 2, tk=D_MODEL, name="w_in_fwd_b")
    ya = _mm(o_pad, W["w_attn_out"], "nn", T, D_MODEL, 1024, tm=tm_lat, tn=D_MODEL, tk=1024, name="w_attn_out_fwd",
             out_dtype=BF16)

    tc = 256
    colT = lambda blk0: pl.BlockSpec((T, tc), lambda j: (0, blk0 + j))

    def f_conv(ids, xin, cb, cc, w, b):
        return (cb * _conv(cc * xin, w, b),)

    (e,) = _ew(f_conv, (CONV_DIM // tc,),
               [(pp, colT(CX0 // tc)), (pp, colT(CB0 // tc)), (pp, colT(CC0 // tc)),
                (W["conv_w"], pl.BlockSpec((3, tc), lambda j: (0, j))), (W["conv_b"], pl.BlockSpec((1, tc), lambda j: (0, j)))],
               [((T, CONV_DIM), BF16, colT(0), None)], "conv_fwd")
    yc = _mm(e, W["w_conv_out"], "nn", T, D_MODEL, CONV_DIM, tm=tm_lat, tn=256, tk=CONV_DIM, name="w_conv_out_fwd",
             out_dtype=BF16, b_spec=pl.BlockSpec((None, CONV_DIM, 256), lambda i, j, k: (j, k, 0)))

    def f_merge(ids, ga, gc, a, c):
        return (_sigmoid(ga) * a.astype(F32) + _sigmoid(gc) * c.astype(F32),)

    (mrg,) = _ew(f_merge, (n_lt,), [(pp, lrows(D_MODEL, 0)), (pp, lrows(D_MODEL, 1)), (ya, lrows(D_MODEL)),
                                   (yc, lrows(D_MODEL))], [lrow_out(D_MODEL, BF16)], "merge_fwd")
    mo = _mm(mrg, W["w_o"], "nn", T, D_MODEL, D_MODEL, tm=tm_lat, tn=D_MODEL, tk=D_MODEL, name="w_o_fwd")

    def f_norm2(ids, x, m, gate, g, sh, sc):
        x1 = x + gate * m
        return x1, (x1 * _rms(x1) * g) * (1.0 + sc) + sh

    x1, h2 = _ew(f_norm2, (n_lt,), [(xx, lrows(D_MODEL)), (mo, lrows(D_MODEL)), (g1, vec(D_MODEL)),
                                   (W["norm2_g"], vec(D_MODEL)), (sh2, vec(D_MODEL)), (sc2, vec(D_MODEL))],
                 [lrow_out(D_MODEL, F32), lrow_out(D_MODEL, BF16)], "norm2_fwd")
    arrived, h2 = late_weights("before_ffn", h2)
    W = dict(W, **arrived)
    up = _mm(h2, W["w_up"], "nn", T, 2 * D_FF, D_MODEL, tm=tm_lat, tn=1408, tk=D_MODEL, name="w_up_fwd",
             b_spec=pl.BlockSpec((None, D_MODEL, 1408), lambda i, j, k: (j, k, 0)))

    n_ff = D_FF // tc
    ffw = lambda off, n=3: pl.BlockSpec((n, tc), lambda j: (0, j + off))

    def f_ffn(ids, ug, uv, wg, wv, bg, bv):
        gate, val = _conv(ug, wg, bg), _conv(uv, wv, bv)
        return (gate * _sigmoid(gate) * val,)

    (act,) = _ew(f_ffn, (n_ff,), [(up, colT(0)), (up, colT(n_ff)), (W["ffn_conv_w"], ffw(0)), (W["ffn_conv_w"], ffw(n_ff)),
                                 (W["ffn_conv_b"], ffw(0, 1)), (W["ffn_conv_b"], ffw(n_ff, 1))],
                 [((T, D_FF), BF16, colT(0), None)], "ffn_act_fwd")
    f = _mm(act, W["w_down"], "nn", T, D_MODEL, D_FF, tm=tm_lat, tn=D_MODEL, tk=D_FF, name="w_down_fwd")

    def f_head(ids, x1_, f_, gate, gf, t):
        x2 = x1_ + gate * f_
        r = _rms(x2)
        xn = x2 * r
        err = xn * gf - t
        loss = 0.5 * jnp.sum(jnp.mean(err * err, axis=-1, keepdims=True))
        dy = err * (1.0 / D_MODEL)
        dx2 = _rms_bwd(dy * gf, xn, r)
        return dx2, dx2 * gate, _colsum(dy * xn), _colsum(dx2 * f_), jnp.full((1, 128), loss, F32)

    dx2, df, dg_f, dg2, loss = _ew(
        f_head, (n_lt,), [(x1, lrows(D_MODEL)), (f, lrows(D_MODEL)), (g2, vec(D_MODEL)), (W["final_g"], vec(D_MODEL)),
                          (tgt, lrows(D_MODEL))],
        [lrow_out(D_MODEL, F32), lrow_out(D_MODEL, BF16), acc_out(D_MODEL), acc_out(D_MODEL), acc_out(128)], "loss_head")

    d_w_down = _mm(act, df, "tn", D_FF, D_MODEL, T, tm=1408, tn=D_MODEL, tk=T, name="w_down_dw",
                   out_dtype=BF16).reshape(4, D_FF // 4, D_MODEL)
    da = _mm(df, W["w_down"], "nt", T, D_FF, D_MODEL, tm=tm_lat, tn=1408, tk=D_MODEL, name="w_down_dx")

    tcb = 128
    n_fb = D_FF // tcb
    colb = lambda blk0: pl.BlockSpec((T, tcb), lambda j: (0, blk0 + j))
    ffwb = lambda off, n=3: pl.BlockSpec((n, tcb), lambda j: (0, j + off))
    cvec = ((1, D_FF), F32, pl.BlockSpec((1, tcb), lambda j: (0, j)), None)

    def f_ffn_bwd(ids, ug, uv, d_act, wg, wv, bg, bv):
        sg, sv = _shifts(ug), _shifts(uv)
        gate, val = _conv(ug, wg, bg, sg), _conv(uv, wv, bv, sv)
        s = _sigmoid(gate)
        d_gate = d_act * val * s * (1.0 + gate * (1.0 - s))
        d_val = d_act * gate * s
        wg0, wg1, wg2 = _conv_bwd_w(d_gate, ug, sg)
        wv0, wv1, wv2 = _conv_bwd_w(d_val, uv, sv)
        d_up = [_conv_bwd_x(d_gate, wg), _conv_bwd_x(d_val, wv)]
        return d_up, [_colsum(d_gate), _colsum(d_val), wg0, wg1, wg2, wv0, wv1, wv2]

    d_up3, ffn_stats = _ew(
        f_ffn_bwd, (n_fb,),
        [(up, colb(0)), (up, colb(n_fb)), (da, colb(0)), (W["ffn_conv_w"], ffwb(0)), (W["ffn_conv_w"], ffwb(n_fb)),
         (W["ffn_conv_b"], ffwb(0, 1)), (W["ffn_conv_b"], ffwb(n_fb, 1))],
        [((2, T, D_FF), BF16, pl.BlockSpec((2, T, tcb), lambda j: (0, 0, j)), None),
         ((n_fb, 8, 1, tcb), F32, pl.BlockSpec((None, 8, 1, tcb), lambda j: (j, 0, 0, 0)), None)], "ffn_act_bwd")
    stat = lambda s: ffn_stats[:, s, 0, :].reshape(1, D_FF)
    d_ffn_conv_b = jnp.concatenate([stat(0), stat(1)], axis=1)
    d_ffn_conv_w = jnp.concatenate([jnp.concatenate([stat(2), stat(3), stat(4)], axis=0),
                                    jnp.concatenate([stat(5), stat(6), stat(7)], axis=0)], axis=1)

    tk_t = T
    d_w_up = _mm(h2, d_up3, "tn", D_MODEL, 2 * D_FF, T, tm=D_MODEL, tn=1408, tk=tk_t, name="w_up_dw", out_dtype=BF16,
                 b_spec=pl.BlockSpec((None, tk_t, 1408), lambda i, j, k: (j // 2, k, j % 2)),
                 o_spec=pl.BlockSpec((None, D_MODEL, 1408), lambda i, j, k: (j, i, 0)), out_shape=(4, D_MODEL, 1408))
    dh2 = _mm(d_up3, W["w_up"], "nt", T, D_MODEL, 2 * D_FF, tm=tm_lat, tn=D_MODEL, tk=1408, name="w_up_dx",
              a_spec=pl.BlockSpec((None, tm_lat, 1408), lambda i, j, k: (k // 2, i, k % 2)),
              b_spec=pl.BlockSpec((None, D_MODEL, 1408), lambda i, j, k: (k, j, 0)))

    def f_norm2_bwd(ids, dx2_, dh, x1_, m, g, sc, gate):
        r = _rms(x1_)
        xn = x1_ * r
        dx1 = dx2_ + _rms_bwd(dh * g * (1.0 + sc), xn, r)
        return dx1, dx1 * gate, _colsum(dh), _colsum(dh * xn * g), _colsum(dh * xn * (1.0 + sc)), _colsum(dx1 * m)

    dx1, dmo, dsh2, dsc2, dg_n2, dg1 = _ew(
        f_norm2_bwd, (n_lt,), [(dx2, lrows(D_MODEL)), (dh2, lrows(D_MODEL)), (x1, lrows(D_MODEL)), (mo, lrows(D_MODEL)),
                               (W["norm2_g"], vec(D_MODEL)), (sc2, vec(D_MODEL)), (g1, vec(D_MODEL))],
        [lrow_out(D_MODEL, F32), lrow_out(D_MODEL, BF16)] + [acc_out(D_MODEL)] * 4, "norm2_bwd")
    d_w_o = _mm(mrg, dmo, "tn", D_MODEL, D_MODEL, T, tm=D_MODEL, tn=D_MODEL, tk=tk_t, name="w_o_dw",
                out_dtype=BF16).reshape(4, D_MODEL // 4, D_MODEL)
    dmrg = _mm(dmo, W["w_o"], "nt", T, D_MODEL, D_MODEL, tm=tm_lat, tn=D_MODEL, tk=D_MODEL, name="w_o_dx",
               out_dtype=BF16)
    dmrg = early_grads("late", {"w_o": d_w_o, "w_up": d_w_up, "w_down": d_w_down}, dmrg, split=True)

    def f_merge_bwd(ids, dm, ga, gc, a, c):
        dm, a, c = dm.astype(F32), a.astype(F32), c.astype(F32)
        sa, sc_ = _sigmoid(ga), _sigmoid(gc)
        return dm * sa, dm * sc_, dm * a * sa * (1.0 - sa), dm * c * sc_ * (1.0 - sc_)

    dya, dyc, dp_ga, dp_gc = _ew(
        f_merge_bwd, (n_lt,), [(dmrg, lrows(D_MODEL)), (pp, lrows(D_MODEL, 0)), (pp, lrows(D_MODEL, 1)),
                               (ya, lrows(D_MODEL)), (yc, lrows(D_MODEL))], [lrow_out(D_MODEL, BF16)] * 4, "merge_bwd")
    dya = early_continue("late", dya)

    d_w_ao_p = _mm(o_pad, dya, "tn", 1024, D_MODEL, T, tm=1024, tn=D_MODEL, tk=tk_t, name="w_attn_out_dw", out_dtype=BF16)
    do_pad = _mm(dya, W["w_attn_out"], "nt", T, 1024, D_MODEL, tm=tm_lat, tn=1024, tk=D_MODEL, name="w_attn_out_dx")
    d_w_co = _mm(e, dyc, "tn", CONV_DIM, D_MODEL, T, tm=CONV_DIM, tn=256, tk=tk_t, name="w_conv_out_dw", out_dtype=BF16,
                 o_spec=pl.BlockSpec((None, CONV_DIM, 256), lambda i, j, k: (j, i, 0)), out_shape=(4, CONV_DIM, 256))
    de = _mm(dyc, W["w_conv_out"], "nt", T, CONV_DIM, D_MODEL, tm=tm_lat, tn=CONV_DIM, tk=256, name="w_conv_out_dx",
             b_spec=pl.BlockSpec((None, CONV_DIM, 256), lambda i, j, k: (k, j, 0)))

    def f_conv_bwd(ids, xin, cb, cc, d_e, w, b):
        z = cc * xin
        sz = _shifts(z)
        cz = _conv(z, w, b, sz)
        dcz = d_e * cb
        w0, w1, w2 = _conv_bwd_w(dcz, z, sz)
        dz = _conv_bwd_x(dcz, w)
        return dz * cc, d_e * cz, dz * xin, _colsum(dcz), w0, w1, w2

    cvec_c = ((1, CONV_DIM), F32, pl.BlockSpec((1, tc), lambda j: (0, j)), None)
    conv_b = _ew(f_conv_bwd, (CONV_DIM // tc,),
                 [(pp, colT(CX0 // tc)), (pp, colT(CB0 // tc)), (pp, colT(CC0 // tc)), (de, colT(0)),
                  (W["conv_w"], pl.BlockSpec((3, tc), lambda j: (0, j))), (W["conv_b"], pl.BlockSpec((1, tc), lambda j: (0, j)))],
                 [((T, CONV_DIM), BF16, colT(0), None)] * 3 + [cvec_c] * 4, "conv_bwd")
    dp_cx, dp_cb, dp_cc, d_conv_b = conv_b[:4]
    d_conv_w = jnp.concatenate(conv_b[4:7], axis=0)

    dq_raw, dkv, dp_kr = _attn_bwd(q_raw, kv, pp_a, o_pad, do_pad, tab, T, TT)

    tk_a = TT
    d_w_uq_t = _mm(nq, dq_raw, "tn", Q_RANK, 1024, T, tm=Q_RANK, tn=1024, tk=T, name="w_uq_dw", transpose_out=True)
    dnq = _mm(dq_raw, W["w_uq_t"], "nn", T, Q_RANK, 1024, tm=tm_lat, tn=Q_RANK, tk=1024, name="w_uq_dx")
    d_w_ukv = _mm(nkv, dkv, "tn", KV_RANK, 1024, TT, tm=KV_RANK, tn=256, tk=tk_a, name="w_ukv_dw", out_dtype=BF16,
                  o_spec=pl.BlockSpec((None, KV_RANK, 256), lambda i, j, k: (j, i, 0)), out_shape=(4, KV_RANK, 256))
    dnkv = _mm(dkv, W["w_ukv"], "nt", TT, KV_RANK, 1024, tm=tm_all, tn=KV_RANK, tk=256, name="w_ukv_dx",
               b_spec=pl.BlockSpec((None, KV_RANK, 256), lambda i, j, k: (k, j, 0)))
    dnkv = early_grads("mid", {
        "w_attn_out": jnp.transpose(d_w_ao_p.reshape(N_HEADS, HEAD_PAD, 4, 256)[:, 64:], (2, 0, 1, 3)).reshape(
            4, N_HEADS * 64, 256),
        "w_conv_out": d_w_co,
        "w_uq": d_w_uq_t.reshape(4, 2, HEAD_PAD, Q_RANK)[:, :, :QK_DIM].reshape(4, 2 * QK_DIM, Q_RANK).astype(BF16),
        "w_ukv": d_w_ukv}, dnkv)

    def f_lowrank_bwd(ids, ckv, cq, dkv_, dq_, gkv, gq, ga, gc, cx, cb, cc, kr):
        rk, rq = _rms(ckv), _rms(cq)
        nk, nq_ = ckv * rk, cq * rq
        lat = ids[0] < n_lat
        dq_ = jnp.where(lat, dq_, 0.0)
        pieces = [jnp.where(lat, a, jnp.zeros_like(a)) for a in (ga, gc, cx, cb, cc)]
        pieces += [_rms_bwd(dkv_ * gkv, nk, rk).astype(BF16), _rms_bwd(dq_ * gq, nq_, rq).astype(BF16), kr.astype(BF16)]
        return jnp.concatenate(pieces, axis=1), _colsum(dkv_ * nk), _colsum(dq_ * nq_)

    lat_rows = lambda n: pl.BlockSpec((ROW_TILE, n), lambda i: (jnp.minimum(i, n_lat - 1), 0))
    dpp, dg_kv, dg_q = _ew(
        f_lowrank_bwd, (n_all,), [(pp_a, _rows(KV_RANK, PA_KV0 // KV_RANK)), (pp_a, _rows(Q_RANK, PA_Q0 // Q_RANK)),
                                  (dnkv, _rows(KV_RANK)), (dnq, lat_rows(Q_RANK)), (W["kv_norm_g"], vec(KV_RANK)),
                                  (W["q_norm_g"], vec(Q_RANK)), (dp_ga, lat_rows(D_MODEL)), (dp_gc, lat_rows(D_MODEL)),
                                  (dp_cx, lat_rows(CONV_DIM)), (dp_cb, lat_rows(CONV_DIM)), (dp_cc, lat_rows(CONV_DIM)),
                                  (dp_kr, _rows(HEAD_PAD))],
        [row_out(P_COLS, BF16, TT), acc_out(KV_RANK), acc_out(Q_RANK)], "lowrank_norm_bwd")
    d_w_in_t = _mm(hh, dpp, "tn", D_MODEL, P_COLS, TT, tm=512, tn=2176, tk=TT, name="w_in_dw", out_dtype=BF16,
                   transpose_out=True)
    big = {"w_in": _w_in_t_shards_from_p(d_w_in_t).astype(BF16)}
    dpp = early_grads("last", big, dpp, split=True)
    dhh = _mm(dpp, W["w_in_t"], "nn", TT, D_MODEL, P_COLS, tm=tm_all, tn=512, tk=2176, name="w_in_dx")

    def f_norm1_bwd(ids, x, dh, dres, g, sc):
        r = _rms(x)
        xn = x * r
        return (dres + _rms_bwd(dh * g * (1.0 + sc), xn, r), _colsum(dh), _colsum(dh * xn * g),
                _colsum(dh * xn * (1.0 + sc)))

    grad_x, dsh1, dsc1, dg_n1 = _ew(
        f_norm1_bwd, (n_lt,), [(xx, lrows(D_MODEL)), (dhh, lrows(D_MODEL)), (dx1, lrows(D_MODEL)),
                               (W["norm1_g"], vec(D_MODEL)), (sc1, vec(D_MODEL))],
        [lrow_out(D_MODEL, F32)] + [acc_out(D_MODEL)] * 3, "norm1_bwd")

    def f_norm1_ctx_bwd(ids, x, dh, g, sc):
        xn = x * _rms(x)
        return _colsum(dh), _colsum(dh * xn * g), _colsum(dh * xn * (1.0 + sc))

    n_ctx = n_all - n_lat
    dcsh1, dcsc1, dg_n1c = _ew(
        f_norm1_ctx_bwd, (n_ctx,), [(xx, _rows(D_MODEL, 0, n_lat)), (dhh, _rows(D_MODEL, 0, n_lat)),
                                    (W["norm1_g"], vec(D_MODEL)), (csc1, vec(D_MODEL))], [acc_out(D_MODEL)] * 3,
        "norm1_ctx_bwd")

    zero = jnp.zeros((1, 4 * D_MODEL), F32)
    small = {
        "dmod_lat": jnp.concatenate([dsh1, dsc1, dg1, dsh2, dsc2, dg2], axis=1),
        "dmod_ctx": jnp.concatenate([dcsh1, dcsc1, zero], axis=1),
        "norm1_g": dg_n1 + dg_n1c, "norm2_g": dg_n2, "final_g": dg_f, "q_norm_g": dg_q, "kv_norm_g": dg_kv,
        "conv_b": d_conv_b, "conv_w": d_conv_w.reshape(1, -1), "ffn_conv_b": d_ffn_conv_b,
        "ffn_conv_w": d_ffn_conv_w.reshape(1, -1),
    }
    return grad_x, loss, big, small


SMALL = (("dmod_lat", 6144), ("dmod_ctx", 6144), ("norm1_g", 1024), ("norm2_g", 1024), ("final_g", 1024),
         ("q_norm_g", 384), ("kv_norm_g", 256), ("conv_b", 512), ("conv_w", 1536), ("ffn_conv_b", 5632),
         ("ffn_conv_w", 16896), ("loss", 128))
SMALL_ROWS = 320


def _adam_update(w, g, m, v):
    c1, c2 = 1.0 - ADAM_B1 ** ADAM_STEP, 1.0 - ADAM_B2 ** ADAM_STEP
    m2 = ADAM_B1 * m + (1.0 - ADAM_B1) * g
    v2 = ADAM_B2 * v + (1.0 - ADAM_B2) * (g * g)
    return [-ADAM_LR * ((m2 / c1) / (jnp.sqrt(v2 / c2) + ADAM_EPS) + ADAM_WD * w), m2, v2]


def _adamw(w, g, m, v, name):
    R, C = w.shape
    tr = 8 if R % 8 == 0 else R
    for t in range(8, R + 1, 8):
        if R % t == 0 and t * C * 4 <= (1 << 21):
            tr = t
    spec = pl.BlockSpec((tr, C), lambda i: (i, 0))
    return _ew(lambda ids, *vals: [vals[1]] + _adam_update(*vals), (R // tr,),
               [(w, spec), (g, spec), (m, spec), (v, spec)], [((R, C), F32, spec, None)] * 4, name)


def kernel(x, c, ctx, c_ctx, w_ada, b_ada, norm1_g, w_in, q_norm_g, kv_norm_g, w_uq, w_ukv, conv_w, conv_b, w_attn_out, w_conv_out, w_o, norm2_g, w_up, ffn_conv_w, ffn_conv_b, w_down, final_g, loss_target, m_c_ctx, m_w_ada, m_b_ada, m_norm1_g, m_w_in, m_q_norm_g, m_kv_norm_g, m_w_uq, m_w_ukv, m_conv_w, m_conv_b, m_w_attn_out, m_w_conv_out, m_w_o, m_norm2_g, m_w_up, m_ffn_conv_w, m_ffn_conv_b, m_w_down, m_final_g, v_c_ctx, v_w_ada, v_b_ada, v_norm1_g, v_w_in, v_q_norm_g, v_kv_norm_g, v_w_uq, v_w_ukv, v_conv_w, v_conv_b, v_w_attn_out, v_w_conv_out, v_w_o, v_norm2_g, v_w_up, v_ffn_conv_w, v_ffn_conv_b, v_w_down, v_final_g):
    mx, my, mc = lax.axis_index("x"), lax.axis_index("y"), lax.axis_index("c")
    chip = 2 * mx + my
    dev = 4 * mx + 2 * my + mc
    T, Tc = x.shape[1], ctx.shape[1]
    TT = T + Tc
    w_in_t, m_w_in_t, v_w_in_t = (jnp.transpose(a[0]) for a in (w_in, m_w_in, v_w_in))
    w_uq_t, m_w_uq_t, v_w_uq_t = (jnp.transpose(a[0]) for a in (w_uq, m_w_uq, v_w_uq))
    conv_sh = jnp.concatenate([conv_w[0], ffn_conv_w[0]], axis=1)
    pay1 = jnp.concatenate([jnp.pad(c, ((0, 7), (0, 0))), jnp.pad(conv_sh, ((0, 5), (0, 0)))], axis=1)
    c_send, c_recv, c_src, c_land, zero0 = _ici_start("all", [pay1], [(8, 8, 2560)], jnp.zeros((8, 128), F32),
                                                      "cond_start")
    w_in_bf = (jnp.pad(w_in_t, ((0, W_IN_SHARD_PAD - W_IN_SHARD), (0, 0))) + zero0[0, 0]).astype(BF16)
    shards = {"w_in_a": w_in_bf[:W_IN_EARLY], "w_in_b": w_in_bf[W_IN_EARLY:], "w_uq": w_uq_t, "w_ukv": w_ukv[0],
              "w_attn_out": w_attn_out[0], "w_conv_out": w_conv_out[0], "w_o": w_o[0], "w_up": w_up[0],
              "w_down": w_down[0]}
    (pay1,), (c_land,) = _ici_wait("all", c_send, c_recv, c_src, c_land, w_in_bf, "cond_wait")
    got1 = lax.dynamic_update_slice(c_land, pay1[None], (dev, 0, 0))
    c_all = got1[:, 0, :D_MODEL]
    conv_all = got1[0::2, :3, D_MODEL:]
    conv_w_full = _cols_from_shards(conv_all[:, :, :128])
    ffn_conv_w_full = _cols_from_shards(conv_all[:, :, 128:])

    cond = jnp.concatenate([c_all, c_ctx.reshape(1, D_MODEL), jnp.zeros((7, D_MODEL), F32)], axis=0)

    def f_silu(ids, v):
        return (v * _sigmoid(v),)

    (s16,) = _ew(f_silu, (1,), [(cond, _full((16, D_MODEL)))], [((16, D_MODEL), F32, _full((16, D_MODEL)), None)], "silu_cond")
    mod_sh = _mm(s16, w_ada[0], "nn", 16, 1536, D_MODEL, tm=16, tn=768, tk=D_MODEL, name="w_ada_fwd")
    m_send, m_recv, m_src, m_land, zero1 = _ici_start("all", [mod_sh], [(8, 16, 1536)], jnp.zeros((8, 128), F32),
                                                      "mod_start")
    shards["w_ukv"] = w_ukv[0] + zero1[0, 0]

    first = ["w_in_a", "w_uq", "w_ukv"]
    gathered, zero = _gather_weights([shards[n].astype(BF16) for n in first])
    full = dict(zip(first, gathered))
    (mod_mine,), (m_land,) = _ici_wait("all", m_send, m_recv, m_src, m_land, gathered[0], "mod_wait")
    got2 = lax.dynamic_update_slice(m_land, mod_mine[None], (dev, 0, 0))
    mod_all = _cols_from_shards(got2[0::2]) + b_ada
    mod_lat = lax.dynamic_slice_in_dim(mod_all, dev, 1, axis=0)
    mod_ctx = mod_all[8:9]
    xx = jnp.concatenate([x[0], ctx[0]], axis=0)
    late_groups = {"g1": ("w_in_b", "w_attn_out", "w_conv_out", "w_o"), "g2": ("w_up", "w_down")}
    flight = {}
    for tag, group in late_groups.items():
        bf = [(shards[n] + zero[0, 0]).astype(BF16) for n in group]
        flight[tag] = _ici_start("gather", bf, [(4,) + s.shape for s in bf], xx, "gather_" + tag + "_start")
        xx = flight[tag][4]

    def chip_stage_done(tag, x):
        send, recv, src, land, _ = flight[tag]
        src, land = _ici_wait("gather", send, recv, src, land, x, "gather_" + tag + "_wait")
        flight[tag] = _ici_start("finish", src, None, x, "finish_" + tag + "_start", lands=land)
        return flight[tag][4]

    def arrived(tag, x):
        send, recv, src, land, _ = flight[tag]
        return dict(zip(late_groups[tag], _ici_wait("finish", send, recv, src, land, x, "finish_" + tag + "_wait")[1]))

    def late_weights(point, x):
        if point == "before_attn":
            return {}, chip_stage_done("g1", x)
        if point == "after_attn":
            got = arrived("g1", x)
            wao = _cols_from_shards(got["w_attn_out"]).reshape(N_HEADS, 64, D_MODEL)
            w_in_all = jnp.concatenate([full["w_in_a"], got["w_in_b"]], axis=1)
            ready = {"w_in_t": _w_in_t_p_from_shards(w_in_all),
                     "w_attn_out": jnp.pad(wao, ((0, 0), (64, 0), (0, 0))).reshape(N_HEADS * HEAD_PAD, D_MODEL),
                     "w_conv_out": got["w_conv_out"], "w_o": got["w_o"].reshape(D_MODEL, D_MODEL)}
            return ready, chip_stage_done("g2", x)
        got = arrived("g2", x)
        return {"w_up": got["w_up"], "w_down": got["w_down"].reshape(D_FF, D_MODEL)}, x

    wuq_t = full["w_uq"].reshape(N_HEADS, QK_DIM, Q_RANK)
    early_rows = full["w_in_a"][0]
    zrows = lambda n: jnp.zeros((n, D_MODEL), BF16)
    W = {
        "w_in_a_t": jnp.concatenate([early_rows[0:256], zrows(PA_Q0 - 256), early_rows[288:672], zrows(64),
                                     early_rows[256:288], zrows(32)], axis=0),
        "w_uq_t": jnp.pad(wuq_t, ((0, 0), (0, HEAD_PAD - QK_DIM), (0, 0))).reshape(N_HEADS * HEAD_PAD, Q_RANK),
        "w_ukv": full["w_ukv"],
        "norm1_g": norm1_g, "norm2_g": norm2_g, "final_g": final_g.reshape(1, D_MODEL), "q_norm_g": q_norm_g,
        "kv_norm_g": kv_norm_g, "conv_w": conv_w_full, "conv_b": conv_b, "ffn_conv_w": ffn_conv_w_full,
        "ffn_conv_b": ffn_conv_b,
    }

    place = jnp.stack([chip, mc]).astype(jnp.int32)
    early = {}

    pending = {}

    def scatter(tag, group, gs, from_sib, carry):
        if tag == "mid":
            sums = _add_pair_many(gs, from_sib, place, "rs_pair_add_mid")
        else:
            sums = [_add_pair(gs[w], from_sib[w], place, "rs_pair_add_" + n) for w, n in enumerate(group)]
        send, recv, sums, land, carry = _ici_start(
            "scatter", sums, [(3,) + s.shape[1:] for s in sums], carry, "rs_chips_" + tag + "_start")
        early[tag] = (group, send, recv, sums, land)
        return carry

    def early_grads(tag, g, carry, split=False):
        gs = list(g.values())
        if not split:
            return scatter(tag, list(g), gs, _rs_pair(gs, "rs_pair_" + tag), carry)
        send, recv, gs, land, carry = _ici_start(
            "pair", gs, [(4, s.shape[1] // 2, s.shape[2]) for s in gs], carry, "rs_pair_" + tag + "_start")
        pending[tag] = (list(g), send, recv, gs, land)
        return carry

    def early_continue(tag, carry):
        group, send, recv, gs, land = pending[tag]
        gs, from_sib = _ici_wait("pair", send, recv, gs, land, carry, "rs_pair_" + tag + "_wait")
        return scatter(tag, group, gs, from_sib, carry)

    grad_x, loss_part, gbig, gsmall = _local_step(xx, loss_target[0], mod_lat, mod_ctx, W, late_weights, early_grads,
                                                  early_continue)

    gsmall["loss"] = loss_part
    pay3 = jnp.concatenate([gsmall[n].reshape(-1) for n, _ in SMALL])
    pay3 = jnp.pad(pay3, (0, SMALL_ROWS * 128 - pay3.shape[0])).reshape(SMALL_ROWS, 128)
    s_send, s_recv, s_src, s_land, _ = _ici_start("all", [pay3], [(8, SMALL_ROWS, 128)], jnp.zeros((8, 128), F32),
                                                  "small_start")

    after_small = early_continue("last", s_src[0])

    (pay3,), (s_land,) = _ici_wait("all", s_send, s_recv, [after_small], s_land, early["last"][3][0], "small_wait")
    got3 = lax.dynamic_update_slice(s_land, pay3[None], (dev, 0, 0)).reshape(8 * SMALL_ROWS, 128)

    def f_sum8(ids, a):
        s = a[0:SMALL_ROWS]
        for d in range(1, 8):
            s = s + a[d * SMALL_ROWS:(d + 1) * SMALL_ROWS]
        return (s,)

    (vsum,) = _ew(f_sum8, (1,), [(got3, _full((8 * SMALL_ROWS, 128)))],
                  [((SMALL_ROWS, 128), F32, _full((SMALL_ROWS, 128)), None)], "sum_small")
    vflat = vsum.reshape(-1)
    gvec, off = {}, 0
    for n, size in SMALL:
        gvec[n] = vflat[off:off + size]
        off += size
    loss = gvec["loss"][0]
    dmod_rows = got3.reshape(8, SMALL_ROWS * 128)[:, :6 * D_MODEL]
    dm16 = jnp.concatenate([dmod_rows, gvec["dmod_ctx"].reshape(1, -1), jnp.zeros((7, 6 * D_MODEL), F32)], axis=0)

    def f_colsum(ids, a):
        return (_colsum(a),)

    (g_b_ada,) = _ew(f_colsum, (1,), [(dm16, _full((16, 6 * D_MODEL)))],
                     [((1, 6 * D_MODEL), F32, _full((1, 6 * D_MODEL)), None)], "b_ada_grad")
    dm_sh = lax.dynamic_slice_in_dim(dm16, chip * 1536, 1536, axis=1)
    g_w_ada = _mm(s16, dm_sh, "tn", D_MODEL, 1536, 16, tm=512, tn=768, tk=16, name="w_ada_dw")
    dcond_part = _mm(dm_sh, w_ada[0], "nt", 16, D_MODEL, 1536, tm=16, tn=512, tk=1536, name="w_ada_dx")
    d_send, d_recv, d_src, d_land, vsum = _ici_start("all", [dcond_part[8:16]], [(8, 8, D_MODEL)], vsum, "dcond_start")

    def finish_start(tags, after):
        done, halves = [], []
        for tag in tags:
            tag_names, send, recv, sums, land = early[tag]
            sums, land = _ici_wait("scatter", send, recv, sums, land, after, "rs_chips_" + tag + "_wait")
            done += tag_names
            if tag == "mid":
                halves += _add_chips_many(sums, land, place, "rs_chip_add_mid")
            else:
                halves += [_add_chips(a, b, place, "rs_chip_add_" + n) for a, b, n in zip(sums, land, tag_names)]
        send, recv, _, halves, _ = _ici_start("back", [], None, jnp.zeros((8, 128), F32), "rs_back_" + tags[0] + "_start",
                                              lands=halves)
        return done, send, recv, halves

    def finish_wait(state, after):
        done, send, recv, halves = state
        return dict(zip(done, _ici_wait("back", send, recv, [], halves, after, "rs_back_" + done[0] + "_wait")[1]))

    grads, deltas, new_m, new_v = {}, {}, {}, {}

    raw = {}

    def adam(n, w_, m_, v_, g, transposed):
        g_out, d_, m2, v2 = _adamw(w_, g, m_, v_, "adamw_" + n)
        raw[n] = d_
        back = (lambda a: jnp.transpose(a)[None]) if transposed else (lambda a: a[None])
        grads[n], deltas[n], new_m[n], new_v[n] = back(g_out), back(d_), back(m2), back(v2)

    pending_back = finish_start(["late", "mid"], grad_x)
    adam("w_ada", w_ada[0], m_w_ada[0], v_w_ada[0], g_w_ada, False)
    gw = finish_wait(pending_back, raw["w_ada"])
    for n, (w_, m_, v_) in {"w_o": (w_o, m_w_o, v_w_o), "w_up": (w_up, m_w_up, v_w_up),
                            "w_down": (w_down, m_w_down, v_w_down)}.items():
        adam(n, w_[0], m_[0], v_[0], gw[n], False)
    pending_back = finish_start(["last"], raw["w_up"])

    (dcond_mine,), (d_land,) = _ici_wait("all", d_send, d_recv, d_src, d_land, raw["w_down"], "dcond_wait")
    got4 = lax.dynamic_update_slice(d_land, dcond_mine[None], (dev, 0, 0))[0::2, 0]

    def f_c_ctx(ids, parts, cc):
        s = _sigmoid(cc)
        d = parts[0:1] + parts[1:2] + parts[2:3] + parts[3:4]
        return (d * s * (1.0 + cc * (1.0 - s)),)

    (g_c_ctx,) = _ew(f_c_ctx, (1,), [(got4, _full((4, D_MODEL))), (c_ctx.reshape(1, D_MODEL), _full((1, D_MODEL)))],
                     [((1, D_MODEL), F32, _full((1, D_MODEL)), None)], "c_ctx_grad")

    conv_w_g = lax.dynamic_slice_in_dim(gvec["conv_w"].reshape(3, CONV_DIM), chip * 128, 128, axis=1)
    ffn_conv_w_g = lax.dynamic_slice_in_dim(gvec["ffn_conv_w"].reshape(3, 2 * D_FF), chip * 1408, 1408, axis=1)
    vec_params = (("c_ctx", c_ctx, m_c_ctx, v_c_ctx, g_c_ctx), ("b_ada", b_ada, m_b_ada, v_b_ada, g_b_ada),
                  ("norm1_g", norm1_g, m_norm1_g, v_norm1_g, gvec["norm1_g"]),
                  ("q_norm_g", q_norm_g, m_q_norm_g, v_q_norm_g, gvec["q_norm_g"]),
                  ("kv_norm_g", kv_norm_g, m_kv_norm_g, v_kv_norm_g, gvec["kv_norm_g"]),
                  ("conv_w", conv_w, m_conv_w, v_conv_w, conv_w_g), ("conv_b", conv_b, m_conv_b, v_conv_b, gvec["conv_b"]),
                  ("norm2_g", norm2_g, m_norm2_g, v_norm2_g, gvec["norm2_g"]),
                  ("ffn_conv_w", ffn_conv_w, m_ffn_conv_w, v_ffn_conv_w, ffn_conv_w_g),
                  ("ffn_conv_b", ffn_conv_b, m_ffn_conv_b, v_ffn_conv_b, gvec["ffn_conv_b"]),
                  ("final_g", final_g, m_final_g, v_final_g, gvec["final_g"]))
    two_d = lambda a: a.reshape((-1, a.shape[-1]))
    many = [p + ((lambda r, s=p[1].shape: r.reshape(s)),) for p in vec_params]
    for n, w_, m_, v_ in (("w_ukv", w_ukv, m_w_ukv, v_w_ukv), ("w_attn_out", w_attn_out, m_w_attn_out, v_w_attn_out),
                          ("w_conv_out", w_conv_out, m_w_conv_out, v_w_conv_out)):
        many.append((n, w_, m_, v_, gw[n], (lambda r, s=w_.shape: r.reshape(s))))
    many.append(("w_uq", w_uq_t, m_w_uq_t, v_w_uq_t, gw["w_uq"], lambda r: jnp.transpose(r)[None]))

    def f_adam_many(ids, *vals):
        out = []
        for k in range(len(many)):
            out += [vals[4 * k + 1]] + _adam_update(*vals[4 * k:4 * k + 4])
        return out

    ins_v, outs_v = [], []
    for p in many:
        shp = two_d(p[1]).shape
        ins_v += [(two_d(a), _full(shp)) for a in (p[1], p[4], p[2], p[3])]
        outs_v += [(shp, F32, _full(shp), None)] * 4
    res_v = _ew(f_adam_many, (1,), ins_v, outs_v, "adamw_small")
    for k, p in enumerate(many):
        n, post = p[0], p[5]
        grads[n], deltas[n], new_m[n], new_v[n] = (post(r) for r in res_v[4 * k:4 * k + 4])

    gw_in = finish_wait(pending_back, res_v[0])
    adam("w_in", w_in_t, m_w_in_t, v_w_in_t, gw_in["w_in"], True)

    order = ("c_ctx", "w_ada", "b_ada", "norm1_g", "w_in", "q_norm_g", "kv_norm_g", "w_uq", "w_ukv", "conv_w", "conv_b",
             "w_attn_out", "w_conv_out", "w_o", "norm2_g", "w_up", "ffn_conv_w", "ffn_conv_b", "w_down", "final_g")
    return (loss, grad_x[None], *[grads[n] for n in order], *[deltas[n] for n in order],
            *[new_m[n] for n in order], *[new_v[n] for n in order])
```

```python
import functools

import jax
import jax.numpy as jnp
import numpy as np
from jax import lax
from jax.experimental import pallas as pl
from jax.experimental.pallas import tpu as pltpu

F32, BF16 = jnp.float32, jnp.bfloat16
MESH = pl.DeviceIdType.MESH

D_MODEL = 1024
N_HEADS = 8
HEAD_PAD = 128
QK_DIM = 96
Q_RANK, KV_RANK = 384, 256
CONV_DIM = 512
D_FF = 2816
GRID_W = 64
ROPE_THETA = 10000.0
EPS = 1e-6
GA0, GC0, CX0, CB0, CC0, KV0, Q0, KR0, P_COLS = 0, 1024, 2048, 2560, 3072, 3584, 3840, 4224, 4352
PA_KV0, PA_Q0, PA_KR0, PA_COLS = 0, 384, 768, 896
ROW_TILE = 256
VMEM_LIMIT_BYTES = 48 * 1024 * 1024

ADAM_LR, ADAM_B1, ADAM_B2, ADAM_EPS, ADAM_WD, ADAM_STEP = 0.001, 0.9, 0.999, 1e-08, 0.01, 10

NN = (((1,), (0,)), ((), ()))
NT = (((1,), (1,)), ((), ()))
TN = (((0,), (0,)), ((), ()))


def _cp(sem):
    return pltpu.CompilerParams(dimension_semantics=sem, vmem_limit_bytes=VMEM_LIMIT_BYTES)


PIN_BYTES = 1 << 19


def _in_hbm(arrays):
    return [pltpu.with_memory_space_constraint(a, pltpu.HBM) if a.size * a.dtype.itemsize >= PIN_BYTES else a
            for a in arrays]


def _out(shape, dtype):
    n = 1
    for d in shape:
        n *= d
    big = n * jnp.dtype(dtype).itemsize >= PIN_BYTES
    return pltpu.HBM(shape, dtype) if big else jax.ShapeDtypeStruct(shape, dtype)


def _pick(n, prefs):
    for p in prefs:
        if n % p == 0:
            return p
    return n


def _mm(a, b, mode, M, N, K, *, tm, tn, tk, name, out_dtype=F32, a_spec=None, b_spec=None, o_spec=None,
        out_shape=None, transpose_out=False):
    assert M % tm == 0 and N % tn == 0 and K % tk == 0, (name, M, N, K, tm, tn, tk)
    nk = K // tk
    dims = {"nn": NN, "nt": NT, "tn": TN}[mode]
    if a_spec is None:
        a_spec = (pl.BlockSpec((tk, tm), lambda i, j, k: (k, i)) if mode == "tn"
                  else pl.BlockSpec((tm, tk), lambda i, j, k: (i, k)))
    if b_spec is None:
        b_spec = (pl.BlockSpec((tn, tk), lambda i, j, k: (j, k)) if mode == "nt"
                  else pl.BlockSpec((tk, tn), lambda i, j, k: (k, j)))
    if o_spec is None:
        o_spec = (pl.BlockSpec((tn, tm), lambda i, j, k: (j, i)) if transpose_out
                  else pl.BlockSpec((tm, tn), lambda i, j, k: (i, j)))
    if out_shape is None:
        out_shape = (N, M) if transpose_out else (M, N)

    def emit(o_ref, val):
        o_ref[...] = (val.T if transpose_out else val).astype(o_ref.dtype)

    def body(a_ref, b_ref, o_ref, *scratch):
        part = lax.dot_general(a_ref[...].astype(BF16), b_ref[...].astype(BF16), dims, preferred_element_type=F32)
        if nk == 1:
            emit(o_ref, part)
            return
        acc_ref, = scratch
        k = pl.program_id(2)

        @pl.when(k == 0)
        def _():
            acc_ref[...] = part

        @pl.when((k > 0) & (k < nk - 1))
        def _():
            acc_ref[...] += part

        @pl.when(k == nk - 1)
        def _():
            emit(o_ref, acc_ref[...] + part)

    return pl.pallas_call(
        body, grid=(M // tm, N // tn, nk), in_specs=[a_spec, b_spec], out_specs=o_spec,
        out_shape=_out(out_shape, out_dtype),
        scratch_shapes=[pltpu.VMEM((tm, tn), F32)] if nk > 1 else [],
        compiler_params=_cp(("parallel", "parallel", "arbitrary")), name=name)(*_in_hbm([a, b]))


def _ew(fn, grid, ins, outs, name, scalars=None):
    n_in = len(ins)
    n_sc = 0 if scalars is None else 1

    def store(ref, val, acc, ids):
        if isinstance(val, (list, tuple)):
            for h, v in enumerate(val):
                ref[h] = v.astype(ref.dtype)
            return
        if acc is None:
            ref[...] = val.astype(ref.dtype)
            return

        @pl.when(ids[acc] == 0)
        def _():
            ref[...] = val.astype(ref.dtype)

        @pl.when(ids[acc] > 0)
        def _():
            ref[...] += val.astype(ref.dtype)

    def body(*refs):
        refs = refs[n_sc:]
        ids = tuple(pl.program_id(a) for a in range(len(grid)))
        vals = fn(ids, *[r[...] for r in refs[:n_in]])
        for ref, val, (_, _, _, acc) in zip(refs[n_in:], vals, outs):
            store(ref, val, acc, ids)

    acc_axes = {o[3] for o in outs if o[3] is not None}
    sem = tuple("arbitrary" if a in acc_axes else "parallel" for a in range(len(grid)))
    in_specs, out_specs = [s for _, s in ins], [o[2] for o in outs]
    out_shape = [_out(o[0], o[1]) for o in outs]
    args = _in_hbm([a for a, _ in ins])
    if scalars is None:
        return pl.pallas_call(body, grid=grid, in_specs=in_specs, out_specs=out_specs, out_shape=out_shape,
                              compiler_params=_cp(sem), name=name)(*args)
    spec = pltpu.PrefetchScalarGridSpec(num_scalar_prefetch=1, grid=grid, in_specs=in_specs, out_specs=out_specs)
    return pl.pallas_call(body, grid_spec=spec, out_shape=out_shape, compiler_params=_cp(sem), name=name)(scalars, *args)


def _rows(width, cblk=0, roff=0, tr=ROW_TILE):
    return pl.BlockSpec((tr, width), lambda i: (i + roff, cblk))


def _full(shape):
    nd = len(shape)
    return pl.BlockSpec(shape, lambda *_: (0,) * nd)


def _sigmoid(x):
    return 1.0 / (1.0 + jnp.exp2(x * (-1.4426950408889634)))


def _rms(x):
    return lax.rsqrt(jnp.mean(x * x, axis=-1, keepdims=True) + EPS)


def _rms_bwd(dn, xn, r):
    return r * (dn - xn * jnp.mean(dn * xn, axis=-1, keepdims=True))


def _colsum(x):
    return jnp.sum(x, axis=0, keepdims=True)


def _shifts(x):
    n = x.shape[0]
    rows = lax.broadcasted_iota(jnp.int32, x.shape, 0)
    return jnp.where(rows == 0, 0.0, pltpu.roll(x, 1, 0)), jnp.where(rows == n - 1, 0.0, pltpu.roll(x, n - 1, 0))


def _conv(x, w, b, shifted=None):
    prev, nxt = _shifts(x) if shifted is None else shifted
    return b + prev * w[0:1] + x * w[1:2] + nxt * w[2:3]


def _conv_bwd_x(dy, w):
    prev, nxt = _shifts(dy)
    return nxt * w[0:1] + dy * w[1:2] + prev * w[2:3]


def _conv_bwd_w(dy, x, shifted):
    prev, nxt = shifted
    return _colsum(dy * prev), _colsum(dy * x), _colsum(dy * nxt)


def _rope(x, cos, sin_lo, sin_hi):
    return x * cos + pltpu.roll(x, HEAD_PAD - 8, 1) * sin_lo + pltpu.roll(x, 8, 1) * sin_hi


ATTN_SCALE = QK_DIM ** -0.5
LOG2_E = 1.4426950408889634


def _rope_t(x, tab, inverse=False):
    o = 3 * HEAD_PAD if inverse else 0
    return _rope(x, tab[:, o:o + HEAD_PAD], tab[:, o + HEAD_PAD:o + 2 * HEAD_PAD], tab[:, o + 2 * HEAD_PAD:o + 3 * HEAD_PAD])


def _heads_keys(hp, kv_ref, kr_ref, tab_ref, kc_ref, vp_ref):
    kr_roped = _rope_t(kr_ref[...], tab_ref[...])
    lane = lax.broadcasted_iota(jnp.int32, kr_roped.shape, 1)
    for u in range(hp):
        kv = kv_ref[:, u * HEAD_PAD:(u + 1) * HEAD_PAD]
        kc_ref[u] = jnp.where(lane < 64, kv, kr_roped).astype(BF16)
        vp_ref[u] = jnp.where(lane >= 64, kv, 0.0).astype(BF16)


ATTN_Q_TILE = 512
ATTN_HEADS_PER_STEP = 2


def _attn_specs(tq, TT):
    q = pl.BlockSpec((tq, HEAD_PAD), lambda h, i: (i, h))
    keys = pl.BlockSpec((TT, HEAD_PAD), lambda h, i: (0, h))
    kr = pl.BlockSpec((TT, HEAD_PAD), lambda h, i: (0, PA_KR0 // HEAD_PAD))
    tab_q = pl.BlockSpec((tq, 6 * HEAD_PAD), lambda h, i: (i, 0))
    tab_k = pl.BlockSpec((TT, 6 * HEAD_PAD), lambda h, i: (0, 0))
    return q, keys, kr, tab_q, tab_k


def _attn_fwd(q_raw, kv, pp, tab, T, TT):
    tq, hp = ROW_TILE, 2 * ATTN_HEADS_PER_STEP
    w = hp * HEAD_PAD

    def body(q_ref, kv_ref, kr_ref, tq_ref, tk_ref, o_ref, kc, vp):
        @pl.when(pl.program_id(1) == 0)
        def _():
            _heads_keys(hp, kv_ref, kr_ref, tk_ref, kc, vp)

        tab = tq_ref[...]
        for u in range(hp):
            cols = slice(u * HEAD_PAD, (u + 1) * HEAD_PAD)
            q = _rope_t(q_ref[:, cols], tab).astype(BF16)
            s = lax.dot_general(q, kc[u], NT, preferred_element_type=F32)
            m = jnp.max(s, axis=-1, keepdims=True)
            p = jnp.exp2((s - m) * (ATTN_SCALE * LOG2_E))
            l = jnp.sum(p, axis=-1, keepdims=True)
            o = lax.dot_general(p.astype(BF16), vp[u], NN, preferred_element_type=F32)
            lane = lax.broadcasted_iota(jnp.int32, o.shape, 1)
            o_ref[:, cols] = jnp.where(lane < 64, m * ATTN_SCALE + jnp.log(l), o / l)

    _, _, kr, _, _ = _attn_specs(tq, TT)
    qs = pl.BlockSpec((tq, w), lambda h, i: (i, h))
    keys = pl.BlockSpec((TT, w), lambda h, i: (0, h))
    tab_q = pl.BlockSpec((tq, 3 * HEAD_PAD), lambda h, i: (i, 0))
    tab_k = pl.BlockSpec((TT, 3 * HEAD_PAD), lambda h, i: (0, 0))
    return pl.pallas_call(
        body, grid=(N_HEADS // hp, T // tq), in_specs=[qs, keys, kr, tab_q, tab_k], out_specs=qs,
        out_shape=jax.ShapeDtypeStruct((T, N_HEADS * HEAD_PAD), F32),
        scratch_shapes=[pltpu.VMEM((hp, TT, HEAD_PAD), BF16), pltpu.VMEM((hp, TT, HEAD_PAD), BF16)],
        compiler_params=_cp(("parallel", "arbitrary")), name="attn_fwd",
    )(*_in_hbm([q_raw, kv, pp, tab, tab]))


def _attn_bwd(q_raw, kv, pp, o, do, tab, T, TT):
    tq = _pick(T, (ATTN_Q_TILE, ROW_TILE))
    nq = T // tq
    hp = ATTN_HEADS_PER_STEP
    w = hp * HEAD_PAD

    def body(q_ref, kv_ref, kr_ref, tq_ref, tk_ref, o_ref, do_ref, dq_ref, dkv_ref, dkr_ref, kc, vp, dk, dv):
        g, i = pl.program_id(0), pl.program_id(1)

        @pl.when(i == 0)
        def _():
            _heads_keys(hp, kv_ref, kr_ref, tk_ref, kc, vp)
            dk[...] = jnp.zeros_like(dk)
            dv[...] = jnp.zeros_like(dv)

        tab = tq_ref[...]
        for u in range(hp):
            cols = slice(u * HEAD_PAD, (u + 1) * HEAD_PAD)
            q = _rope_t(q_ref[:, cols], tab).astype(BF16)
            k, v, d_o = kc[u], vp[u], do_ref[:, cols]
            s = lax.dot_general(q, k, NT, preferred_element_type=F32)
            o = o_ref[:, cols]
            p = jnp.exp2(s * (ATTN_SCALE * LOG2_E) - o[:, 0:1] * LOG2_E)
            dob = d_o.astype(BF16)
            dp = lax.dot_general(dob, v, NT, preferred_element_type=F32)
            dd = jnp.sum(d_o * o, axis=-1, keepdims=True)
            ds = (p * (dp - dd) * ATTN_SCALE).astype(BF16)
            dq = lax.dot_general(ds, k, NN, preferred_element_type=F32)
            dq_ref[:, cols] = _rope_t(dq, tab, inverse=True).astype(dq_ref.dtype)
            dk[u] += lax.dot_general(q, ds, TN, preferred_element_type=F32)
            dv[u] += lax.dot_general(dob, p.astype(BF16), TN, preferred_element_type=F32)

        @pl.when(i == nq - 1)
        def _():
            rot = None
            for u in range(hp):
                dkh = dk[u].T
                lane = lax.broadcasted_iota(jnp.int32, dkh.shape, 1)
                dkv_ref[:, u * HEAD_PAD:(u + 1) * HEAD_PAD] = jnp.where(lane < 64, dkh, dv[u].T).astype(dkv_ref.dtype)
                part = jnp.where((lane >= 64) & (lane < 96), dkh, 0.0)
                rot = part if rot is None else rot + part
            rot = _rope_t(rot, tk_ref[...], inverse=True)

            @pl.when(g == 0)
            def _():
                dkr_ref[...] = rot

            @pl.when(g > 0)
            def _():
                dkr_ref[...] += rot

    _, _, kr, tab_q, tab_k = _attn_specs(tq, TT)
    qs = pl.BlockSpec((tq, w), lambda h, i: (i, h))
    keys = pl.BlockSpec((TT, w), lambda h, i: (0, h))
    wide = lambda rows: jax.ShapeDtypeStruct((rows, N_HEADS * HEAD_PAD), BF16)
    return pl.pallas_call(
        body, grid=(N_HEADS // hp, nq),
        in_specs=[qs, keys, kr, tab_q, tab_k, qs, qs],
        out_specs=[qs, keys, pl.BlockSpec((TT, HEAD_PAD), lambda h, i: (0, 0))],
        out_shape=[wide(T), wide(TT), jax.ShapeDtypeStruct((TT, HEAD_PAD), F32)],
        scratch_shapes=[pltpu.VMEM((hp, TT, HEAD_PAD), BF16), pltpu.VMEM((hp, TT, HEAD_PAD), BF16),
                        pltpu.VMEM((hp, HEAD_PAD, TT), F32), pltpu.VMEM((hp, HEAD_PAD, TT), F32)],
        compiler_params=_cp(("arbitrary", "arbitrary")), name="attn_bwd",
    )(*_in_hbm([q_raw, kv, pp, tab, tab, o, do]))


def _hbm_specs(n):
    return [pl.BlockSpec(memory_space=pl.ANY)] * n


def _gather_weights(shards):
    n = len(shards)
    halves = [s.shape[0] // 2 for s in shards]

    def body(*refs):
        ins, outs = refs[:n], refs[n:2 * n]
        token, send_sems, recv_sems = refs[2 * n:]
        token[...] = jnp.zeros_like(token)
        mx, my, mc = lax.axis_index("x"), lax.axis_index("y"), lax.axis_index("c")
        j_me = 2 * mx + my
        chips = [(1 - mx, my), (mx, 1 - my), (1 - mx, 1 - my)]

        def half(w, chip_idx, hc):
            return outs[w].at[chip_idx, pl.ds(hc * halves[w], halves[w]), :]

        def copy(w, k, src, dst, to):
            return pltpu.make_async_remote_copy(src_ref=src, dst_ref=dst, send_sem=send_sems.at[w, k],
                                                recv_sem=recv_sems.at[w, k], device_id=to, device_id_type=MESH)

        sends = []
        for w in range(n):
            cp = copy(w, 6, ins[w], outs[w].at[j_me], (mx, my, 1 - mc))
            cp.start()
            sends.append(cp)
        for k, (px, py) in enumerate(chips):
            for w in range(n):
                cp = copy(w, k, ins[w].at[pl.ds(mc * halves[w], halves[w]), :], half(w, j_me, mc), (px, py, mc))
                cp.start()
                sends.append(cp)
        for k, (px, py) in enumerate(chips):
            for w in range(n):
                got = half(w, 2 * px + py, mc)
                copy(w, k, got, got, (px, py, mc)).wait_recv()
                cp = copy(w, 3 + k, got, got, (mx, my, 1 - mc))
                cp.start()
                sends.append(cp)
        for k, (px, py) in enumerate(chips):
            for w in range(n):
                got = half(w, 2 * px + py, 1 - mc)
                copy(w, 3 + k, got, got, (mx, my, 1 - mc)).wait_recv()
        for w in range(n):
            own = outs[w].at[j_me]
            copy(w, 6, own, own, (mx, my, 1 - mc)).wait_recv()
        for cp in sends:
            cp.wait_send()

    res = pl.pallas_call(
        body, out_shape=[jax.ShapeDtypeStruct((4,) + s.shape, s.dtype) for s in shards]
        + [jax.ShapeDtypeStruct((8, 128), F32)],
        in_specs=_hbm_specs(n), out_specs=_hbm_specs(n) + [pl.BlockSpec(memory_space=pltpu.VMEM)],
        scratch_shapes=[pltpu.SemaphoreType.DMA((n, 7)), pltpu.SemaphoreType.DMA((n, 7))],
        name="gather_weights")(*shards)
    return list(res[:n]), res[n]


def _rs_pair(gs, name):
    n = len(gs)
    halves = [g.shape[1] // 2 for g in gs]

    def body(*refs):
        ins, lands = refs[:n], refs[n:2 * n]
        send_sems, recv_sems = refs[2 * n:]
        mx, my, mc = lax.axis_index("x"), lax.axis_index("y"), lax.axis_index("c")
        copies = []
        for w in range(n):
            h = halves[w]
            cp = pltpu.make_async_remote_copy(
                src_ref=ins[w].at[:, pl.ds((1 - mc) * h, h), :], dst_ref=lands[w], send_sem=send_sems.at[w],
                recv_sem=recv_sems.at[w], device_id=(mx, my, 1 - mc), device_id_type=MESH)
            cp.start()
            copies.append(cp)
        for cp in copies:
            cp.wait()

    return pl.pallas_call(
        body, out_shape=[jax.ShapeDtypeStruct((4, h, g.shape[2]), g.dtype) for g, h in zip(gs, halves)],
        in_specs=_hbm_specs(n), out_specs=_hbm_specs(n),
        scratch_shapes=[pltpu.SemaphoreType.DMA((n,)), pltpu.SemaphoreType.DMA((n,))], name=name)(*gs)


def _rs_chips(parts):
    n = len(parts)

    def body(*refs):
        ins, lands = refs[:n], refs[n:2 * n]
        send_sems, recv_sems = refs[2 * n:]
        mx, my, mc = lax.axis_index("x"), lax.axis_index("y"), lax.axis_index("c")
        copies = []
        for k, (px, py) in enumerate([(1 - mx, my), (mx, 1 - my), (1 - mx, 1 - my)]):
            for w in range(n):
                cp = pltpu.make_async_remote_copy(
                    src_ref=ins[w].at[2 * px + py], dst_ref=lands[w].at[k], send_sem=send_sems.at[w, k],
                    recv_sem=recv_sems.at[w, k], device_id=(px, py, mc), device_id_type=MESH)
                cp.start()
                copies.append(cp)
        for cp in copies:
            cp.wait()

    return list(pl.pallas_call(
        body, out_shape=[jax.ShapeDtypeStruct((3,) + p.shape[1:], p.dtype) for p in parts],
        in_specs=_hbm_specs(n), out_specs=_hbm_specs(n),
        scratch_shapes=[pltpu.SemaphoreType.DMA((n, 3)), pltpu.SemaphoreType.DMA((n, 3))], name="rs_chips")(*parts))


_HBM = pl.BlockSpec(memory_space=pltpu.HBM)
_SEM = pl.BlockSpec(memory_space=pltpu.SEMAPHORE)
_EFFECT = pltpu.SideEffectType.DATAFLOW_SIDE_EFFECTING


def _ici_copies(kind, srcs, lands, send_sems, recv_sems):
    n = len(lands)
    mx, my, mc = lax.axis_index("x"), lax.axis_index("y"), lax.axis_index("c")
    j_me = 2 * mx + my
    copies = []
    if kind == "back":
        for w in range(n):
            h = lands[w].shape[0] // 2
            mine = lands[w].at[pl.ds(mc * h, h), :]
            copies.append(pltpu.make_async_remote_copy(
                src_ref=mine, dst_ref=mine, send_sem=send_sems.at[w], recv_sem=recv_sems.at[w],
                device_id=(mx, my, 1 - mc), device_id_type=MESH))
        return copies
    if kind == "all":
        for k in range(7):
            a, b, c = (k + 1) >> 2 & 1, (k + 1) >> 1 & 1, (k + 1) & 1
            peer = (1 - mx if a else mx, 1 - my if b else my, 1 - mc if c else mc)
            for w in range(n):
                copies.append(pltpu.make_async_remote_copy(
                    src_ref=srcs[w], dst_ref=lands[w].at[4 * mx + 2 * my + mc], send_sem=send_sems.at[7 * w + k],
                    recv_sem=recv_sems.at[7 * w + k], device_id=peer, device_id_type=MESH))
        return copies
    if kind == "pair":
        for w in range(n):
            h = srcs[w].shape[1] // 2
            copies.append(pltpu.make_async_remote_copy(
                src_ref=srcs[w].at[:, pl.ds((1 - mc) * h, h), :], dst_ref=lands[w], send_sem=send_sems.at[w],
                recv_sem=recv_sems.at[w], device_id=(mx, my, 1 - mc), device_id_type=MESH))
        return copies
    chips = [(1 - mx, my), (mx, 1 - my), (1 - mx, 1 - my)]
    if kind == "finish":
        for w in range(n):
            h = srcs[w].shape[0] // 2
            pushes = [(lands[w].at[2 * px + py, pl.ds(mc * h, h), :],) * 2 for px, py in chips]
            pushes.append((srcs[w], lands[w].at[j_me]))
            for k, (src, dst) in enumerate(pushes):
                copies.append(pltpu.make_async_remote_copy(
                    src_ref=src, dst_ref=dst, send_sem=send_sems.at[4 * w + k], recv_sem=recv_sems.at[4 * w + k],
                    device_id=(mx, my, 1 - mc), device_id_type=MESH))
        return copies
    for k, (px, py) in enumerate(chips):
        for w in range(n):
            if kind == "gather":
                h = srcs[w].shape[0] // 2
                src, dst = srcs[w].at[pl.ds(mc * h, h), :], lands[w].at[j_me, pl.ds(mc * h, h), :]
            else:
                src, dst = srcs[w].at[2 * px + py], lands[w].at[k]
            copies.append(pltpu.make_async_remote_copy(
                src_ref=src, dst_ref=dst, send_sem=send_sems.at[3 * w + k], recv_sem=recv_sems.at[3 * w + k],
                device_id=(px, py, mc), device_id_type=MESH))
    return copies


_SEMS_PER_OPERAND = {"gather": 3, "scatter": 3, "all": 7, "pair": 1, "finish": 4, "back": 1}


def _ici_start(kind, srcs, land_shapes, carry, name, lands=None):
    hbm = lambda a: pltpu.with_memory_space_constraint(a, pltpu.HBM)
    if lands is None:
        lands = [lax.empty(s, srcs[0].dtype) for s in land_shapes]
    ns, nl = len(srcs), len(lands)

    def body(*refs):
        send_sems, recv_sems = refs[ns + nl + 1], refs[ns + nl + 2]
        for cp in _ici_copies(kind, refs[:ns], refs[ns:ns + nl], send_sems, recv_sems):
            cp.start()

    args = [hbm(a) for a in list(srcs) + list(lands) + [carry]]
    n_sem = _SEMS_PER_OPERAND[kind] * nl
    out_shape = ([pltpu.SemaphoreType.DMA((n_sem,)), pltpu.SemaphoreType.DMA((n_sem,))]
                 + [pltpu.HBM(a.shape, a.dtype) for a in args])
    res = pl.pallas_call(
        body, name=name, out_shape=out_shape, in_specs=[_HBM] * len(args), out_specs=[_SEM, _SEM] + [_HBM] * len(args),
        input_output_aliases={i: 2 + i for i in range(len(args))},
        compiler_params=pltpu.CompilerParams(has_side_effects=_EFFECT))(*args)
    return res[0], res[1], list(res[2:2 + ns]), list(res[2 + ns:2 + ns + nl]), res[2 + ns + nl]


def _ici_wait(kind, send_sems, recv_sems, srcs, lands, after, name):
    ns, nl = len(srcs), len(lands)

    def body(*refs):
        for cp in _ici_copies(kind, refs[:ns], refs[ns:ns + nl], refs[ns + nl], refs[ns + nl + 1]):
            cp.wait_send()
            cp.wait_recv()

    args = list(srcs) + list(lands)
    res = pl.pallas_call(
        body, name=name, out_shape=[pltpu.HBM(a.shape, a.dtype) for a in args],
        in_specs=[_HBM] * len(args) + [_SEM, _SEM, pl.BlockSpec(memory_space=pl.ANY)], out_specs=[_HBM] * len(args),
        input_output_aliases={i: i for i in range(len(args))},
        compiler_params=pltpu.CompilerParams(has_side_effects=_EFFECT))(*args, send_sems, recv_sems, after)
    return list(res[:ns]), list(res[ns:])


def _tile_rows(h, c, itemsize, mult):
    best = h
    for t in range(mult, h + 1, mult):
        if h % t == 0 and t * c * itemsize <= (1 << 21):
            best = t
    return best


def _add_pair(g, land, place, name):
    _, h, c = land.shape
    t = _tile_rows(h, c, 2, 16)
    nb = h // t
    return _ew(lambda ids, u, v: (u.astype(F32) + v.astype(F32),), (4, nb),
               [(g, pl.BlockSpec((None, t, c), lambda j, i, s: (j, s[1] * nb + i, 0))),
                (land, pl.BlockSpec((None, t, c), lambda j, i, s: (j, i, 0)))],
               [(land.shape, BF16, pl.BlockSpec((None, t, c), lambda j, i, s: (j, i, 0)), None)], name, scalars=place)[0]


def _add_pair_many(gs, lands, place, name):
    ins, outs = [], []
    for g, l in zip(gs, lands):
        ins += [(g, pl.BlockSpec(l.shape, lambda i, s: (0, s[1], 0))), (l, pl.BlockSpec(l.shape, lambda i, s: (0, 0, 0)))]
        outs.append((l.shape, BF16, pl.BlockSpec(l.shape, lambda i, s: (0, 0, 0)), None))
    fn = lambda ids, *v: [v[2 * k].astype(F32) + v[2 * k + 1].astype(F32) for k in range(len(gs))]
    return list(_ew(fn, (1,), ins, outs, name, scalars=place))


def _add_chips_many(owns, lands, place, name):
    ins, outs = [], []
    for own, land in zip(owns, lands):
        _, h, c = land.shape
        ins += [(own, pl.BlockSpec((None, h, c), lambda i, s: (s[0], 0, 0))),
                (land, pl.BlockSpec((3, h, c), lambda i, s: (0, 0, 0)))]
        outs.append(((2 * h, c), F32, pl.BlockSpec((h, c), lambda i, s: (s[1], 0)), None))

    def fn(ids, *v):
        return [((v[2 * k].astype(F32) + v[2 * k + 1][0].astype(F32)) + v[2 * k + 1][1].astype(F32))
                + v[2 * k + 1][2].astype(F32) for k in range(len(owns))]

    return list(_ew(fn, (1,), ins, outs, name, scalars=place))


def _add_chips(own, land, place, name):
    _, h, c = land.shape
    t = _tile_rows(h, c, 4, 16)
    nb = h // t

    def fn(ids, a, b):
        return (((a.astype(F32) + b[0].astype(F32)) + b[1].astype(F32)) + b[2].astype(F32),)

    return _ew(fn, (nb,), [(own, pl.BlockSpec((None, t, c), lambda i, s: (s[0], i, 0))),
                           (land, pl.BlockSpec((3, t, c), lambda i, s: (0, i, 0)))],
               [((2 * h, c), F32, pl.BlockSpec((t, c), lambda i, s: (s[1] * nb + i, 0)), None)], name, scalars=place)[0]


W_IN_SEGMENTS = ((0, 256, KV0), (256, 288, KR0 + 64), (288, 672, Q0), (672, 1184, CX0), (1184, 1696, CB0),
                 (1696, 2208, CC0), (2208, 3232, GA0), (3232, 4256, GC0))
W_IN_SHARD = 1064


W_IN_SHARD_PAD = 1088
W_IN_EARLY = 672


def _w_in_t_p_from_shards(s):
    pieces = []
    for o0, o1, p0 in sorted(W_IN_SEGMENTS, key=lambda t: t[2]):
        if p0 == KR0 + 64:
            pieces.append(jnp.zeros((64, s.shape[2]), s.dtype))
        for j in range(4):
            lo, hi = max(o0, j * W_IN_SHARD), min(o1, (j + 1) * W_IN_SHARD)
            if lo < hi:
                pieces.append(s[j, lo - j * W_IN_SHARD:hi - j * W_IN_SHARD])
    pieces.append(jnp.zeros((32, s.shape[2]), s.dtype))
    return jnp.concatenate(pieces, axis=0)


def _w_in_t_shards_from_p(g):
    shards = []
    for j in range(4):
        pieces = []
        for o0, o1, p0 in W_IN_SEGMENTS:
            lo, hi = max(o0, j * W_IN_SHARD), min(o1, (j + 1) * W_IN_SHARD)
            if lo < hi:
                pieces.append(g[p0 + lo - o0:p0 + hi - o0])
        pieces.append(jnp.zeros((W_IN_SHARD_PAD - W_IN_SHARD, g.shape[1]), g.dtype))
        shards.append(jnp.concatenate(pieces, axis=0))
    return jnp.stack(shards, axis=0)


def _cols_from_shards(s):
    return jnp.transpose(s, (1, 0, 2)).reshape(s.shape[1], -1)


def _rope_tables(T, TT, inverse):
    f32 = np.float32
    rows = T // GRID_W
    row = np.repeat(np.arange(rows), GRID_W).astype(f32)
    col = np.tile(np.arange(GRID_W), rows).astype(f32)
    inv = (f32(ROPE_THETA) ** (-np.arange(0, 16, 2, dtype=f32) / f32(16))).astype(f32)
    ang = np.concatenate([row[:, None] * inv, col[:, None] * inv], axis=-1).astype(f32)
    cos, sin = np.cos(ang).astype(f32), np.sin(ang).astype(f32)
    lane = np.arange(32)
    src = (lane // 16) * 8 + lane % 8
    lo = ((lane % 16) // 8 == 0).astype(f32)
    sgn = f32(-1.0 if inverse else 1.0)
    cos32 = cos[:, src]
    sin_lo32 = -sgn * sin[:, src] * lo
    sin_hi32 = sgn * sin[:, src] * (1 - lo)

    def widen(t32, fill):
        t = np.concatenate([np.full((T, 64), fill, f32), t32, np.full((T, 32), fill, f32)], axis=1)
        return np.concatenate([t, np.full((TT - T, HEAD_PAD), fill, f32)], axis=0)

    return [widen(cos32, 1.0), widen(sin_lo32, 0.0), widen(sin_hi32, 0.0)]


def _rope_table(T, TT):
    return jnp.asarray(np.concatenate(_rope_tables(T, TT, False) + _rope_tables(T, TT, True), axis=1))


def _local_step(xx, tgt, mod_lat, mod_ctx, W, late_weights, early_grads, early_continue):
    TT = xx.shape[0]
    T = tgt.shape[0]
    n_lat, n_all = T // ROW_TILE, TT // ROW_TILE
    sh1, sc1, g1, sh2, sc2, g2 = [mod_lat[:, k * D_MODEL:(k + 1) * D_MODEL] for k in range(6)]
    csh1, csc1 = mod_ctx[:, :D_MODEL], mod_ctx[:, D_MODEL:2 * D_MODEL]
    vec = lambda n: _full((1, n))
    row_out = lambda n, dt, rows=T: ((rows, n), dt, _rows(n), None)
    acc_out = lambda n: ((1, n), F32, _full((1, n)), 0)
    lt = _pick(T, (2 * ROW_TILE, ROW_TILE))
    n_lt = T // lt
    lrows = lambda n, cblk=0: _rows(n, cblk, 0, lt)
    lrow_out = lambda n, dt: ((T, n), dt, lrows(n), None)

    def f_norm1(ids, x, g, a_sh, a_sc, b_sh, b_sc):
        ctx = ids[0] >= n_lat
        sh, sc = jnp.where(ctx, b_sh, a_sh), jnp.where(ctx, b_sc, a_sc)
        return ((x * _rms(x) * g) * (1.0 + sc) + sh,)

    (hh,) = _ew(f_norm1, (n_all,), [(xx, _rows(D_MODEL)), (W["norm1_g"], vec(D_MODEL)), (sh1, vec(D_MODEL)),
                                   (sc1, vec(D_MODEL)), (csh1, vec(D_MODEL)), (csc1, vec(D_MODEL))],
                [row_out(D_MODEL, BF16, TT)], "norm1_fwd")
    tm_all = _pick(TT, (768, 256))
    pp_a = _mm(hh, W["w_in_a_t"], "nt", TT, PA_COLS, D_MODEL, tm=tm_all, tn=PA_COLS, tk=D_MODEL, name="w_in_fwd_a")

    def f_lowrank(ids, ckv, cq, gkv, gq):
        return ckv * _rms(ckv) * gkv, cq * _rms(cq) * gq

    nkv, nq = _ew(f_lowrank, (n_all,), [(pp_a, _rows(KV_RANK, PA_KV0 // KV_RANK)), (pp_a, _rows(Q_RANK, PA_Q0 // Q_RANK)),
                                       (W["kv_norm_g"], vec(KV_RANK)), (W["q_norm_g"], vec(Q_RANK))],
                  [row_out(KV_RANK, BF16, TT), row_out(Q_RANK, BF16, TT)], "lowrank_norm_fwd")
    kv = _mm(nkv, W["w_ukv"], "nn", TT, 1024, KV_RANK, tm=tm_all, tn=256, tk=KV_RANK, name="w_ukv_fwd",
             b_spec=pl.BlockSpec((None, KV_RANK, 256), lambda i, j, k: (j, k, 0)))
    q_raw = _mm(nq, W["w_uq_t"], "nt", TT, 1024, Q_RANK, tm=tm_all, tn=1024, tk=Q_RANK, name="w_uq_fwd")

    tab = _rope_table(T, TT)
    _, q_raw = late_weights("before_attn", q_raw)
    o_pad = _attn_fwd(q_raw, kv, pp_a, tab, T, TT)
    arrived, o_pad = late_weights("after_attn", o_pad)
    W = dict(W, **arrived)
    tm_lat = _pick(T, (1024, 512, 256))
    pp = _mm(hh, W["w_in_t"], "nt", T, KV0, D_MODEL, tm=tm_lat, tn=KV0 // 2, tk=D_MODEL, name="w_in_fwd_b")
    ya = _mm(o_pad, W["w_attn_out"], "nn", T, D_MODEL, 1024, tm=tm_lat, tn=D_MODEL, tk=1024, name="w_attn_out_fwd",
             out_dtype=BF16)

    tc = 256
    colT = lambda blk0: pl.BlockSpec((T, tc), lambda j: (0, blk0 + j))

    def f_conv(ids, xin, cb, cc, w, b):
        return (cb * _conv(cc * xin, w, b),)

    (e,) = _ew(f_conv, (CONV_DIM // tc,),
               [(pp, colT(CX0 // tc)), (pp, colT(CB0 // tc)), (pp, colT(CC0 // tc)),
                (W["conv_w"], pl.BlockSpec((3, tc), lambda j: (0, j))), (W["conv_b"], pl.BlockSpec((1, tc), lambda j: (0, j)))],
               [((T, CONV_DIM), BF16, colT(0), None)], "conv_fwd")
    yc = _mm(e, W["w_conv_out"], "nn", T, D_MODEL, CONV_DIM, tm=tm_lat, tn=256, tk=CONV_DIM, name="w_conv_out_fwd",
             out_dtype=BF16, b_spec=pl.BlockSpec((None, CONV_DIM, 256), lambda i, j, k: (j, k, 0)))

    def f_merge(ids, ga, gc, a, c):
        return (_sigmoid(ga) * a.astype(F32) + _sigmoid(gc) * c.astype(F32),)

    (mrg,) = _ew(f_merge, (n_lt,), [(pp, lrows(D_MODEL, 0)), (pp, lrows(D_MODEL, 1)), (ya, lrows(D_MODEL)),
                                   (yc, lrows(D_MODEL))], [lrow_out(D_MODEL, BF16)], "merge_fwd")
    mo = _mm(mrg, W["w_o"], "nn", T, D_MODEL, D_MODEL, tm=tm_lat, tn=D_MODEL, tk=D_MODEL, name="w_o_fwd")

    def f_norm2(ids, x, m, gate, g, sh, sc):
        x1 = x + gate * m
        return x1, (x1 * _rms(x1) * g) * (1.0 + sc) + sh

    x1, h2 = _ew(f_norm2, (n_lt,), [(xx, lrows(D_MODEL)), (mo, lrows(D_MODEL)), (g1, vec(D_MODEL)),
                                   (W["norm2_g"], vec(D_MODEL)), (sh2, vec(D_MODEL)), (sc2, vec(D_MODEL))],
                 [lrow_out(D_MODEL, F32), lrow_out(D_MODEL, BF16)], "norm2_fwd")
    arrived, h2 = late_weights("before_ffn", h2)
    W = dict(W, **arrived)
    up = _mm(h2, W["w_up"], "nn", T, 2 * D_FF, D_MODEL, tm=tm_lat, tn=1408, tk=D_MODEL, name="w_up_fwd",
             b_spec=pl.BlockSpec((None, D_MODEL, 1408), lambda i, j, k: (j, k, 0)))

    n_ff = D_FF // tc
    ffw = lambda off, n=3: pl.BlockSpec((n, tc), lambda j: (0, j + off))

    def f_ffn(ids, ug, uv, wg, wv, bg, bv):
        gate, val = _conv(ug, wg, bg), _conv(uv, wv, bv)
        return (gate * _sigmoid(gate) * val,)

    (act,) = _ew(f_ffn, (n_ff,), [(up, colT(0)), (up, colT(n_ff)), (W["ffn_conv_w"], ffw(0)), (W["ffn_conv_w"], ffw(n_ff)),
                                 (W["ffn_conv_b"], ffw(0, 1)), (W["ffn_conv_b"], ffw(n_ff, 1))],
                 [((T, D_FF), BF16, colT(0), None)], "ffn_act_fwd")
    f = _mm(act, W["w_down"], "nn", T, D_MODEL, D_FF, tm=tm_lat, tn=D_MODEL, tk=D_FF, name="w_down_fwd")

    def f_head(ids, x1_, f_, gate, gf, t):
        x2 = x1_ + gate * f_
        r = _rms(x2)
        xn = x2 * r
        err = xn * gf - t
        loss = 0.5 * jnp.sum(jnp.mean(err * err, axis=-1, keepdims=True))
        dy = err * (1.0 / D_MODEL)
        dx2 = _rms_bwd(dy * gf, xn, r)
        return dx2, dx2 * gate, _colsum(dy * xn), _colsum(dx2 * f_), jnp.full((1, 128), loss, F32)

    dx2, df, dg_f, dg2, loss = _ew(
        f_head, (n_lt,), [(x1, lrows(D_MODEL)), (f, lrows(D_MODEL)), (g2, vec(D_MODEL)), (W["final_g"], vec(D_MODEL)),
                          (tgt, lrows(D_MODEL))],
        [lrow_out(D_MODEL, F32), lrow_out(D_MODEL, BF16), acc_out(D_MODEL), acc_out(D_MODEL), acc_out(128)], "loss_head")

    d_w_down = _mm(act, df, "tn", D_FF, D_MODEL, T, tm=1408, tn=D_MODEL, tk=T, name="w_down_dw",
                   out_dtype=BF16).reshape(4, D_FF // 4, D_MODEL)
    da = _mm(df, W["w_down"], "nt", T, D_FF, D_MODEL, tm=tm_lat, tn=1408, tk=D_MODEL, name="w_down_dx")

    tcb = 128
    n_fb = D_FF // tcb
    colb = lambda blk0: pl.BlockSpec((T, tcb), lambda j: (0, blk0 + j))
    ffwb = lambda off, n=3: pl.BlockSpec((n, tcb), lambda j: (0, j + off))
    cvec = ((1, D_FF), F32, pl.BlockSpec((1, tcb), lambda j: (0, j)), None)

    def f_ffn_bwd(ids, ug, uv, d_act, wg, wv, bg, bv):
        sg, sv = _shifts(ug), _shifts(uv)
        gate, val = _conv(ug, wg, bg, sg), _conv(uv, wv, bv, sv)
        s = _sigmoid(gate)
        d_gate = d_act * val * s * (1.0 + gate * (1.0 - s))
        d_val = d_act * gate * s
        wg0, wg1, wg2 = _conv_bwd_w(d_gate, ug, sg)
        wv0, wv1, wv2 = _conv_bwd_w(d_val, uv, sv)
        d_up = [_conv_bwd_x(d_gate, wg), _conv_bwd_x(d_val, wv)]
        return d_up, [_colsum(d_gate), _colsum(d_val), wg0, wg1, wg2, wv0, wv1, wv2]

    d_up3, ffn_stats = _ew(
        f_ffn_bwd, (n_fb,),
        [(up, colb(0)), (up, colb(n_fb)), (da, colb(0)), (W["ffn_conv_w"], ffwb(0)), (W["ffn_conv_w"], ffwb(n_fb)),
         (W["ffn_conv_b"], ffwb(0, 1)), (W["ffn_conv_b"], ffwb(n_fb, 1))],
        [((2, T, D_FF), BF16, pl.BlockSpec((2, T, tcb), lambda j: (0, 0, j)), None),
         ((n_fb, 8, 1, tcb), F32, pl.BlockSpec((None, 8, 1, tcb), lambda j: (j, 0, 0, 0)), None)], "ffn_act_bwd")
    stat = lambda s: ffn_stats[:, s, 0, :].reshape(1, D_FF)
    d_ffn_conv_b = jnp.concatenate([stat(0), stat(1)], axis=1)
    d_ffn_conv_w = jnp.concatenate([jnp.concatenate([stat(2), stat(3), stat(4)], axis=0),
                                    jnp.concatenate([stat(5), stat(6), stat(7)], axis=0)], axis=1)

    tk_t = T
    d_w_up = _mm(h2, d_up3, "tn", D_MODEL, 2 * D_FF, T, tm=D_MODEL, tn=1408, tk=tk_t, name="w_up_dw", out_dtype=BF16,
                 b_spec=pl.BlockSpec((None, tk_t, 1408), lambda i, j, k: (j // 2, k, j % 2)),
                 o_spec=pl.BlockSpec((None, D_MODEL, 1408), lambda i, j, k: (j, i, 0)), out_shape=(4, D_MODEL, 1408))
    dh2 = _mm(d_up3, W["w_up"], "nt", T, D_MODEL, 2 * D_FF, tm=tm_lat, tn=D_MODEL, tk=1408, name="w_up_dx",
              a_spec=pl.BlockSpec((None, tm_lat, 1408), lambda i, j, k: (k // 2, i, k % 2)),
              b_spec=pl.BlockSpec((None, D_MODEL, 1408), lambda i, j, k: (k, j, 0)))

    def f_norm2_bwd(ids, dx2_, dh, x1_, m, g, sc, gate):
        r = _rms(x1_)
        xn = x1_ * r
        dx1 = dx2_ + _rms_bwd(dh * g * (1.0 + sc), xn, r)
        return dx1, dx1 * gate, _colsum(dh), _colsum(dh * xn * g), _colsum(dh * xn * (1.0 + sc)), _colsum(dx1 * m)

    dx1, dmo, dsh2, dsc2, dg_n2, dg1 = _ew(
        f_norm2_bwd, (n_lt,), [(dx2, lrows(D_MODEL)), (dh2, lrows(D_MODEL)), (x1, lrows(D_MODEL)), (mo, lrows(D_MODEL)),
                               (W["norm2_g"], vec(D_MODEL)), (sc2, vec(D_MODEL)), (g1, vec(D_MODEL))],
        [lrow_out(D_MODEL, F32), lrow_out(D_MODEL, BF16)] + [acc_out(D_MODEL)] * 4, "norm2_bwd")
    d_w_o = _mm(mrg, dmo, "tn", D_MODEL, D_MODEL, T, tm=D_MODEL, tn=D_MODEL, tk=tk_t, name="w_o_dw",
                out_dtype=BF16).reshape(4, D_MODEL // 4, D_MODEL)
    dmrg = _mm(dmo, W["w_o"], "nt", T, D_MODEL, D_MODEL, tm=tm_lat, tn=D_MODEL, tk=D_MODEL, name="w_o_dx",
               out_dtype=BF16)
    dmrg = early_grads("late", {"w_o": d_w_o, "w_up": d_w_up, "w_down": d_w_down}, dmrg, split=True)

    def f_merge_bwd(ids, dm, ga, gc, a, c):
        dm, a, c = dm.astype(F32), a.astype(F32), c.astype(F32)
        sa, sc_ = _sigmoid(ga), _sigmoid(gc)
        return dm * sa, dm * sc_, dm * a * sa * (1.0 - sa), dm * c * sc_ * (1.0 - sc_)

    dya, dyc, dp_ga, dp_gc = _ew(
        f_merge_bwd, (n_lt,), [(dmrg, lrows(D_MODEL)), (pp, lrows(D_MODEL, 0)), (pp, lrows(D_MODEL, 1)),
                               (ya, lrows(D_MODEL)), (yc, lrows(D_MODEL))], [lrow_out(D_MODEL, BF16)] * 4, "merge_bwd")
    dya = early_continue("late", dya)

    d_w_ao_p = _mm(o_pad, dya, "tn", 1024, D_MODEL, T, tm=1024, tn=D_MODEL, tk=tk_t, name="w_attn_out_dw", out_dtype=BF16)
    do_pad = _mm(dya, W["w_attn_out"], "nt", T, 1024, D_MODEL, tm=tm_lat, tn=1024, tk=D_MODEL, name="w_attn_out_dx")
    d_w_co = _mm(e, dyc, "tn", CONV_DIM, D_MODEL, T, tm=CONV_DIM, tn=256, tk=tk_t, name="w_conv_out_dw", out_dtype=BF16,
                 o_spec=pl.BlockSpec((None, CONV_DIM, 256), lambda i, j, k: (j, i, 0)), out_shape=(4, CONV_DIM, 256))
    de = _mm(dyc, W["w_conv_out"], "nt", T, CONV_DIM, D_MODEL, tm=tm_lat, tn=CONV_DIM, tk=256, name="w_conv_out_dx",
             b_spec=pl.BlockSpec((None, CONV_DIM, 256), lambda i, j, k: (k, j, 0)))

    def f_conv_bwd(ids, xin, cb, cc, d_e, w, b):
        z = cc * xin
        sz = _shifts(z)
        cz = _conv(z, w, b, sz)
        dcz = d_e * cb
        w0, w1, w2 = _conv_bwd_w(dcz, z, sz)
        dz = _conv_bwd_x(dcz, w)
        return dz * cc, d_e * cz, dz * xin, _colsum(dcz), w0, w1, w2

    cvec_c = ((1, CONV_DIM), F32, pl.BlockSpec((1, tc), lambda j: (0, j)), None)
    conv_b = _ew(f_conv_bwd, (CONV_DIM // tc,),
                 [(pp, colT(CX0 // tc)), (pp, colT(CB0 // tc)), (pp, colT(CC0 // tc)), (de, colT(0)),
                  (W["conv_w"], pl.BlockSpec((3, tc), lambda j: (0, j))), (W["conv_b"], pl.BlockSpec((1, tc), lambda j: (0, j)))],
                 [((T, CONV_DIM), BF16, colT(0), None)] * 3 + [cvec_c] * 4, "conv_bwd")
    dp_cx, dp_cb, dp_cc, d_conv_b = conv_b[:4]
    d_conv_w = jnp.concatenate(conv_b[4:7], axis=0)

    dq_raw, dkv, dp_kr = _attn_bwd(q_raw, kv, pp_a, o_pad, do_pad, tab, T, TT)

    tk_a = TT
    d_w_uq_t = _mm(nq, dq_raw, "tn", Q_RANK, 1024, T, tm=Q_RANK, tn=1024, tk=T, name="w_uq_dw", transpose_out=True)
    dnq = _mm(dq_raw, W["w_uq_t"], "nn", T, Q_RANK, 1024, tm=tm_lat, tn=Q_RANK, tk=1024, name="w_uq_dx")
    d_w_ukv = _mm(nkv, dkv, "tn", KV_RANK, 1024, TT, tm=KV_RANK, tn=256, tk=tk_a, name="w_ukv_dw", out_dtype=BF16,
                  o_spec=pl.BlockSpec((None, KV_RANK, 256), lambda i, j, k: (j, i, 0)), out_shape=(4, KV_RANK, 256))
    dnkv = _mm(dkv, W["w_ukv"], "nt", TT, KV_RANK, 1024, tm=tm_all, tn=KV_RANK, tk=256, name="w_ukv_dx",
               b_spec=pl.BlockSpec((None, KV_RANK, 256), lambda i, j, k: (k, j, 0)))
    dnkv = early_grads("mid", {
        "w_attn_out": jnp.transpose(d_w_ao_p.reshape(N_HEADS, HEAD_PAD, 4, 256)[:, 64:], (2, 0, 1, 3)).reshape(
            4, N_HEADS * 64, 256),
        "w_conv_out": d_w_co,
        "w_uq": d_w_uq_t.reshape(4, 2, HEAD_PAD, Q_RANK)[:, :, :QK_DIM].reshape(4, 2 * QK_DIM, Q_RANK).astype(BF16),
        "w_ukv": d_w_ukv}, dnkv)

    def f_lowrank_bwd(ids, ckv, cq, dkv_, dq_, gkv, gq, ga, gc, cx, cb, cc, kr):
        rk, rq = _rms(ckv), _rms(cq)
        nk, nq_ = ckv * rk, cq * rq
        lat = ids[0] < n_lat
        dq_ = jnp.where(lat, dq_, 0.0)
        pieces = [jnp.where(lat, a, jnp.zeros_like(a)) for a in (ga, gc, cx, cb, cc)]
        pieces += [_rms_bwd(dkv_ * gkv, nk, rk).astype(BF16), _rms_bwd(dq_ * gq, nq_, rq).astype(BF16), kr.astype(BF16)]
        return jnp.concatenate(pieces, axis=1), _colsum(dkv_ * nk), _colsum(dq_ * nq_)

    lat_rows = lambda n: pl.BlockSpec((ROW_TILE, n), lambda i: (jnp.minimum(i, n_lat - 1), 0))
    dpp, dg_kv, dg_q = _ew(
        f_lowrank_bwd, (n_all,), [(pp_a, _rows(KV_RANK, PA_KV0 // KV_RANK)), (pp_a, _rows(Q_RANK, PA_Q0 // Q_RANK)),
                                  (dnkv, _rows(KV_RANK)), (dnq, lat_rows(Q_RANK)), (W["kv_norm_g"], vec(KV_RANK)),
                                  (W["q_norm_g"], vec(Q_RANK)), (dp_ga, lat_rows(D_MODEL)), (dp_gc, lat_rows(D_MODEL)),
                                  (dp_cx, lat_rows(CONV_DIM)), (dp_cb, lat_rows(CONV_DIM)), (dp_cc, lat_rows(CONV_DIM)),
                                  (dp_kr, _rows(HEAD_PAD))],
        [row_out(P_COLS, BF16, TT), acc_out(KV_RANK), acc_out(Q_RANK)], "lowrank_norm_bwd")
    d_w_in_t = _mm(hh, dpp, "tn", D_MODEL, P_COLS, TT, tm=512, tn=2176, tk=TT, name="w_in_dw", out_dtype=BF16,
                   transpose_out=True)
    dhh = _mm(dpp, W["w_in_t"], "nn", TT, D_MODEL, P_COLS, tm=tm_all, tn=512, tk=2176, name="w_in_dx")

    def f_norm1_bwd(ids, x, dh, dres, g, sc):
        r = _rms(x)
        xn = x * r
        return (dres + _rms_bwd(dh * g * (1.0 + sc), xn, r), _colsum(dh), _colsum(dh * xn * g),
                _colsum(dh * xn * (1.0 + sc)))

    grad_x, dsh1, dsc1, dg_n1 = _ew(
        f_norm1_bwd, (n_lt,), [(xx, lrows(D_MODEL)), (dhh, lrows(D_MODEL)), (dx1, lrows(D_MODEL)),
                               (W["norm1_g"], vec(D_MODEL)), (sc1, vec(D_MODEL))],
        [lrow_out(D_MODEL, F32)] + [acc_out(D_MODEL)] * 3, "norm1_bwd")

    def f_norm1_ctx_bwd(ids, x, dh, g, sc):
        xn = x * _rms(x)
        return _colsum(dh), _colsum(dh * xn * g), _colsum(dh * xn * (1.0 + sc))

    n_ctx = n_all - n_lat
    dcsh1, dcsc1, dg_n1c = _ew(
        f_norm1_ctx_bwd, (n_ctx,), [(xx, _rows(D_MODEL, 0, n_lat)), (dhh, _rows(D_MODEL, 0, n_lat)),
                                    (W["norm1_g"], vec(D_MODEL)), (csc1, vec(D_MODEL))], [acc_out(D_MODEL)] * 3,
        "norm1_ctx_bwd")

    big = {"w_in": _w_in_t_shards_from_p(d_w_in_t).astype(BF16)}
    zero = jnp.zeros((1, 4 * D_MODEL), F32)
    small = {
        "dmod_lat": jnp.concatenate([dsh1, dsc1, dg1, dsh2, dsc2, dg2], axis=1),
        "dmod_ctx": jnp.concatenate([dcsh1, dcsc1, zero], axis=1),
        "norm1_g": dg_n1 + dg_n1c, "norm2_g": dg_n2, "final_g": dg_f, "q_norm_g": dg_q, "kv_norm_g": dg_kv,
        "conv_b": d_conv_b, "conv_w": d_conv_w.reshape(1, -1), "ffn_conv_b": d_ffn_conv_b,
        "ffn_conv_w": d_ffn_conv_w.reshape(1, -1),
    }
    return grad_x, loss, big, small


SMALL = (("dmod_lat", 6144), ("dmod_ctx", 6144), ("norm1_g", 1024), ("norm2_g", 1024), ("final_g", 1024),
         ("q_norm_g", 384), ("kv_norm_g", 256), ("conv_b", 512), ("conv_w", 1536), ("ffn_conv_b", 5632),
         ("ffn_conv_w", 16896), ("loss", 128))
SMALL_ROWS = 320


def _adam_update(w, g, m, v):
    c1, c2 = 1.0 - ADAM_B1 ** ADAM_STEP, 1.0 - ADAM_B2 ** ADAM_STEP
    m2 = ADAM_B1 * m + (1.0 - ADAM_B1) * g
    v2 = ADAM_B2 * v + (1.0 - ADAM_B2) * (g * g)
    return [-ADAM_LR * ((m2 / c1) / (jnp.sqrt(v2 / c2) + ADAM_EPS) + ADAM_WD * w), m2, v2]


def _adamw(w, g, m, v, name):
    R, C = w.shape
    tr = 8 if R % 8 == 0 else R
    for t in range(8, R + 1, 8):
        if R % t == 0 and t * C * 4 <= (1 << 21):
            tr = t
    spec = pl.BlockSpec((tr, C), lambda i: (i, 0))
    return _ew(lambda ids, *vals: [vals[1]] + _adam_update(*vals), (R // tr,),
               [(w, spec), (g, spec), (m, spec), (v, spec)], [((R, C), F32, spec, None)] * 4, name)


def kernel(x, c, ctx, c_ctx, w_ada, b_ada, norm1_g, w_in, q_norm_g, kv_norm_g, w_uq, w_ukv, conv_w, conv_b, w_attn_out, w_conv_out, w_o, norm2_g, w_up, ffn_conv_w, ffn_conv_b, w_down, final_g, loss_target, m_c_ctx, m_w_ada, m_b_ada, m_norm1_g, m_w_in, m_q_norm_g, m_kv_norm_g, m_w_uq, m_w_ukv, m_conv_w, m_conv_b, m_w_attn_out, m_w_conv_out, m_w_o, m_norm2_g, m_w_up, m_ffn_conv_w, m_ffn_conv_b, m_w_down, m_final_g, v_c_ctx, v_w_ada, v_b_ada, v_norm1_g, v_w_in, v_q_norm_g, v_kv_norm_g, v_w_uq, v_w_ukv, v_conv_w, v_conv_b, v_w_attn_out, v_w_conv_out, v_w_o, v_norm2_g, v_w_up, v_ffn_conv_w, v_ffn_conv_b, v_w_down, v_final_g):
    mx, my, mc = lax.axis_index("x"), lax.axis_index("y"), lax.axis_index("c")
    chip = 2 * mx + my
    dev = 4 * mx + 2 * my + mc
    T, Tc = x.shape[1], ctx.shape[1]
    TT = T + Tc
    w_in_t, m_w_in_t, v_w_in_t = (jnp.transpose(a[0]) for a in (w_in, m_w_in, v_w_in))
    w_uq_t, m_w_uq_t, v_w_uq_t = (jnp.transpose(a[0]) for a in (w_uq, m_w_uq, v_w_uq))
    conv_sh = jnp.concatenate([conv_w[0], ffn_conv_w[0]], axis=1)
    pay1 = jnp.concatenate([jnp.pad(c, ((0, 7), (0, 0))), jnp.pad(conv_sh, ((0, 5), (0, 0)))], axis=1)
    c_send, c_recv, c_src, c_land, zero0 = _ici_start("all", [pay1], [(8, 8, 2560)], jnp.zeros((8, 128), F32),
                                                      "cond_start")
    w_in_bf = (jnp.pad(w_in_t, ((0, W_IN_SHARD_PAD - W_IN_SHARD), (0, 0))) + zero0[0, 0]).astype(BF16)
    shards = {"w_in_a": w_in_bf[:W_IN_EARLY], "w_in_b": w_in_bf[W_IN_EARLY:], "w_uq": w_uq_t, "w_ukv": w_ukv[0],
              "w_attn_out": w_attn_out[0], "w_conv_out": w_conv_out[0], "w_o": w_o[0], "w_up": w_up[0],
              "w_down": w_down[0]}
    first = ["w_in_a", "w_uq", "w_ukv"]
    bf0 = [(shards[n] + zero0[0, 0].astype(shards[n].dtype)).astype(BF16) for n in first]
    flight0 = _ici_start("gather", bf0, [(4,) + s.shape for s in bf0], jnp.zeros((8, 128), F32), "gather_g0_start")
    (pay1,), (c_land,) = _ici_wait("all", c_send, c_recv, c_src, c_land, flight0[4], "cond_wait")
    got1 = lax.dynamic_update_slice(c_land, pay1[None], (dev, 0, 0))
    c_all = got1[:, 0, :D_MODEL]
    conv_all = got1[0::2, :3, D_MODEL:]
    conv_w_full = _cols_from_shards(conv_all[:, :, :128])
    ffn_conv_w_full = _cols_from_shards(conv_all[:, :, 128:])

    cond = jnp.concatenate([c_all, c_ctx.reshape(1, D_MODEL), jnp.zeros((7, D_MODEL), F32)], axis=0)

    def f_silu(ids, v):
        return (v * _sigmoid(v),)

    (s16,) = _ew(f_silu, (1,), [(cond, _full((16, D_MODEL)))], [((16, D_MODEL), F32, _full((16, D_MODEL)), None)], "silu_cond")
    mod_sh = _mm(s16, w_ada[0], "nn", 16, 1536, D_MODEL, tm=16, tn=768, tk=D_MODEL, name="w_ada_fwd")
    m_send, m_recv, m_src, m_land, zero1 = _ici_start("all", [mod_sh], [(8, 16, 1536)], jnp.zeros((8, 128), F32),
                                                      "mod_start")
    g_send, g_recv, g_src, g_land, _ = flight0
    g_src, g_land = _ici_wait("gather", g_send, g_recv, g_src, g_land, zero1, "gather_g0_wait")
    f_send, f_recv, f_src, f_land, zero = _ici_start("finish", g_src, None, zero1, "finish_g0_start", lands=g_land)
    gathered = _ici_wait("finish", f_send, f_recv, f_src, f_land, zero, "finish_g0_wait")[1]
    full = dict(zip(first, gathered))
    (mod_mine,), (m_land,) = _ici_wait("all", m_send, m_recv, m_src, m_land, gathered[0], "mod_wait")
    got2 = lax.dynamic_update_slice(m_land, mod_mine[None], (dev, 0, 0))
    mod_all = _cols_from_shards(got2[0::2]) + b_ada
    mod_lat = lax.dynamic_slice_in_dim(mod_all, dev, 1, axis=0)
    mod_ctx = mod_all[8:9]
    xx = jnp.concatenate([x[0], ctx[0]], axis=0)
    late_groups = {"g1": ("w_in_b", "w_attn_out", "w_conv_out", "w_o"), "g2": ("w_up", "w_down")}
    flight = {}
    for tag, group in late_groups.items():
        bf = [(shards[n] + zero[0, 0]).astype(BF16) for n in group]
        flight[tag] = _ici_start("gather", bf, [(4,) + s.shape for s in bf], xx, "gather_" + tag + "_start")
        xx = flight[tag][4]

    def chip_stage_done(tag, x):
        send, recv, src, land, _ = flight[tag]
        src, land = _ici_wait("gather", send, recv, src, land, x, "gather_" + tag + "_wait")
        flight[tag] = _ici_start("finish", src, None, x, "finish_" + tag + "_start", lands=land)
        return flight[tag][4]

    def arrived(tag, x):
        send, recv, src, land, _ = flight[tag]
        return dict(zip(late_groups[tag], _ici_wait("finish", send, recv, src, land, x, "finish_" + tag + "_wait")[1]))

    def late_weights(point, x):
        if point == "before_attn":
            return {}, chip_stage_done("g1", x)
        if point == "after_attn":
            got = arrived("g1", x)
            wao = _cols_from_shards(got["w_attn_out"]).reshape(N_HEADS, 64, D_MODEL)
            w_in_all = jnp.concatenate([full["w_in_a"], got["w_in_b"]], axis=1)
            ready = {"w_in_t": _w_in_t_p_from_shards(w_in_all),
                     "w_attn_out": jnp.pad(wao, ((0, 0), (64, 0), (0, 0))).reshape(N_HEADS * HEAD_PAD, D_MODEL),
                     "w_conv_out": got["w_conv_out"], "w_o": got["w_o"].reshape(D_MODEL, D_MODEL)}
            return ready, chip_stage_done("g2", x)
        got = arrived("g2", x)
        return {"w_up": got["w_up"], "w_down": got["w_down"].reshape(D_FF, D_MODEL)}, x

    wuq_t = full["w_uq"].reshape(N_HEADS, QK_DIM, Q_RANK)
    early_rows = full["w_in_a"][0]
    zrows = lambda n: jnp.zeros((n, D_MODEL), BF16)
    W = {
        "w_in_a_t": jnp.concatenate([early_rows[0:256], zrows(PA_Q0 - 256), early_rows[288:672], zrows(64),
                                     early_rows[256:288], zrows(32)], axis=0),
        "w_uq_t": jnp.pad(wuq_t, ((0, 0), (0, HEAD_PAD - QK_DIM), (0, 0))).reshape(N_HEADS * HEAD_PAD, Q_RANK),
        "w_ukv": full["w_ukv"],
        "norm1_g": norm1_g, "norm2_g": norm2_g, "final_g": final_g.reshape(1, D_MODEL), "q_norm_g": q_norm_g,
        "kv_norm_g": kv_norm_g, "conv_w": conv_w_full, "conv_b": conv_b, "ffn_conv_w": ffn_conv_w_full,
        "ffn_conv_b": ffn_conv_b,
    }

    place = jnp.stack([chip, mc]).astype(jnp.int32)
    early = {}

    pending = {}

    def scatter(tag, group, gs, from_sib, carry):
        if tag == "mid":
            sums = _add_pair_many(gs, from_sib, place, "rs_pair_add_mid")
        else:
            sums = [_add_pair(gs[w], from_sib[w], place, "rs_pair_add_" + n) for w, n in enumerate(group)]
        send, recv, sums, land, carry = _ici_start(
            "scatter", sums, [(3,) + s.shape[1:] for s in sums], carry, "rs_chips_" + tag + "_start")
        early[tag] = (group, send, recv, sums, land)
        return carry

    def early_grads(tag, g, carry, split=False):
        gs = list(g.values())
        if not split:
            return scatter(tag, list(g), gs, _rs_pair(gs, "rs_pair_" + tag), carry)
        send, recv, gs, land, carry = _ici_start(
            "pair", gs, [(4, s.shape[1] // 2, s.shape[2]) for s in gs], carry, "rs_pair_" + tag + "_start")
        pending[tag] = (list(g), send, recv, gs, land)
        return carry

    def early_continue(tag, carry):
        group, send, recv, gs, land = pending[tag]
        gs, from_sib = _ici_wait("pair", send, recv, gs, land, carry, "rs_pair_" + tag + "_wait")
        return scatter(tag, group, gs, from_sib, carry)

    grad_x, loss_part, gbig, gsmall = _local_step(xx, loss_target[0], mod_lat, mod_ctx, W, late_weights, early_grads,
                                                  early_continue)

    gsmall["loss"] = loss_part
    pay3 = jnp.concatenate([gsmall[n].reshape(-1) for n, _ in SMALL])
    pay3 = jnp.pad(pay3, (0, SMALL_ROWS * 128 - pay3.shape[0])).reshape(SMALL_ROWS, 128)
    s_send, s_recv, s_src, s_land, w_in_thru = _ici_start("all", [pay3], [(8, SMALL_ROWS, 128)], gbig["w_in"],
                                                         "small_start")
    gbig = {"w_in": w_in_thru}

    after_small = early_grads("last", gbig, s_src[0])

    (pay3,), (s_land,) = _ici_wait("all", s_send, s_recv, [after_small], s_land, early["last"][3][0], "small_wait")
    got3 = lax.dynamic_update_slice(s_land, pay3[None], (dev, 0, 0)).reshape(8 * SMALL_ROWS, 128)

    def f_sum8(ids, a):
        s = a[0:SMALL_ROWS]
        for d in range(1, 8):
            s = s + a[d * SMALL_ROWS:(d + 1) * SMALL_ROWS]
        return (s,)

    (vsum,) = _ew(f_sum8, (1,), [(got3, _full((8 * SMALL_ROWS, 128)))],
                  [((SMALL_ROWS, 128), F32, _full((SMALL_ROWS, 128)), None)], "sum_small")
    vflat = vsum.reshape(-1)
    gvec, off = {}, 0
    for n, size in SMALL:
        gvec[n] = vflat[off:off + size]
        off += size
    loss = gvec["loss"][0]
    dmod_rows = got3.reshape(8, SMALL_ROWS * 128)[:, :6 * D_MODEL]
    dm16 = jnp.concatenate([dmod_rows, gvec["dmod_ctx"].reshape(1, -1), jnp.zeros((7, 6 * D_MODEL), F32)], axis=0)

    def f_colsum(ids, a):
        return (_colsum(a),)

    (g_b_ada,) = _ew(f_colsum, (1,), [(dm16, _full((16, 6 * D_MODEL)))],
                     [((1, 6 * D_MODEL), F32, _full((1, 6 * D_MODEL)), None)], "b_ada_grad")
    dm_sh = lax.dynamic_slice_in_dim(dm16, chip * 1536, 1536, axis=1)
    g_w_ada = _mm(s16, dm_sh, "tn", D_MODEL, 1536, 16, tm=512, tn=768, tk=16, name="w_ada_dw")
    dcond_part = _mm(dm_sh, w_ada[0], "nt", 16, D_MODEL, 1536, tm=16, tn=512, tk=1536, name="w_ada_dx")
    d_send, d_recv, d_src, d_land, vsum = _ici_start("all", [dcond_part[8:16]], [(8, 8, D_MODEL)], vsum, "dcond_start")

    def finish_start(tags, after):
        done, halves = [], []
        for tag in tags:
            tag_names, send, recv, sums, land = early[tag]
            sums, land = _ici_wait("scatter", send, recv, sums, land, after, "rs_chips_" + tag + "_wait")
            done += tag_names
            if tag == "mid":
                halves += _add_chips_many(sums, land, place, "rs_chip_add_mid")
            else:
                halves += [_add_chips(a, b, place, "rs_chip_add_" + n) for a, b, n in zip(sums, land, tag_names)]
        send, recv, _, halves, _ = _ici_start("back", [], None, jnp.zeros((8, 128), F32), "rs_back_" + tags[0] + "_start",
                                              lands=halves)
        return done, send, recv, halves

    def finish_wait(state, after):
        done, send, recv, halves = state
        return dict(zip(done, _ici_wait("back", send, recv, [], halves, after, "rs_back_" + done[0] + "_wait")[1]))

    grads, deltas, new_m, new_v = {}, {}, {}, {}

    raw = {}

    def adam(n, w_, m_, v_, g, transposed):
        g_out, d_, m2, v2 = _adamw(w_, g, m_, v_, "adamw_" + n)
        raw[n] = d_
        back = (lambda a: jnp.transpose(a)[None]) if transposed else (lambda a: a[None])
        grads[n], deltas[n], new_m[n], new_v[n] = back(g_out), back(d_), back(m2), back(v2)

    pending_back = finish_start(["late", "mid"], grad_x)
    adam("w_ada", w_ada[0], m_w_ada[0], v_w_ada[0], g_w_ada, False)
    gw = finish_wait(pending_back, raw["w_ada"])
    for n, (w_, m_, v_) in {"w_o": (w_o, m_w_o, v_w_o), "w_up": (w_up, m_w_up, v_w_up),
                            "w_down": (w_down, m_w_down, v_w_down)}.items():
        adam(n, w_[0], m_[0], v_[0], gw[n], False)
    pending_back = finish_start(["last"], raw["w_up"])

    (dcond_mine,), (d_land,) = _ici_wait("all", d_send, d_recv, d_src, d_land, raw["w_down"], "dcond_wait")
    got4 = lax.dynamic_update_slice(d_land, dcond_mine[None], (dev, 0, 0))[0::2, 0]

    def f_c_ctx(ids, parts, cc):
        s = _sigmoid(cc)
        d = parts[0:1] + parts[1:2] + parts[2:3] + parts[3:4]
        return (d * s * (1.0 + cc * (1.0 - s)),)

    (g_c_ctx,) = _ew(f_c_ctx, (1,), [(got4, _full((4, D_MODEL))), (c_ctx.reshape(1, D_MODEL), _full((1, D_MODEL)))],
                     [((1, D_MODEL), F32, _full((1, D_MODEL)), None)], "c_ctx_grad")

    conv_w_g = lax.dynamic_slice_in_dim(gvec["conv_w"].reshape(3, CONV_DIM), chip * 128, 128, axis=1)
    ffn_conv_w_g = lax.dynamic_slice_in_dim(gvec["ffn_conv_w"].reshape(3, 2 * D_FF), chip * 1408, 1408, axis=1)
    vec_params = (("c_ctx", c_ctx, m_c_ctx, v_c_ctx, g_c_ctx), ("b_ada", b_ada, m_b_ada, v_b_ada, g_b_ada),
                  ("norm1_g", norm1_g, m_norm1_g, v_norm1_g, gvec["norm1_g"]),
                  ("q_norm_g", q_norm_g, m_q_norm_g, v_q_norm_g, gvec["q_norm_g"]),
                  ("kv_norm_g", kv_norm_g, m_kv_norm_g, v_kv_norm_g, gvec["kv_norm_g"]),
                  ("conv_w", conv_w, m_conv_w, v_conv_w, conv_w_g), ("conv_b", conv_b, m_conv_b, v_conv_b, gvec["conv_b"]),
                  ("norm2_g", norm2_g, m_norm2_g, v_norm2_g, gvec["norm2_g"]),
                  ("ffn_conv_w", ffn_conv_w, m_ffn_conv_w, v_ffn_conv_w, ffn_conv_w_g),
                  ("ffn_conv_b", ffn_conv_b, m_ffn_conv_b, v_ffn_conv_b, gvec["ffn_conv_b"]),
                  ("final_g", final_g, m_final_g, v_final_g, gvec["final_g"]))
    two_d = lambda a: a.reshape((-1, a.shape[-1]))
    many = [p + ((lambda r, s=p[1].shape: r.reshape(s)),) for p in vec_params]
    for n, w_, m_, v_ in (("w_ukv", w_ukv, m_w_ukv, v_w_ukv), ("w_attn_out", w_attn_out, m_w_attn_out, v_w_attn_out),
                          ("w_conv_out", w_conv_out, m_w_conv_out, v_w_conv_out)):
        many.append((n, w_, m_, v_, gw[n], (lambda r, s=w_.shape: r.reshape(s))))
    many.append(("w_uq", w_uq_t, m_w_uq_t, v_w_uq_t, gw["w_uq"], lambda r: jnp.transpose(r)[None]))

    def f_adam_many(ids, *vals):
        out = []
        for k in range(len(many)):
            out += [vals[4 * k + 1]] + _adam_update(*vals[4 * k:4 * k + 4])
        return out

    ins_v, outs_v = [], []
    for p in many:
        shp = two_d(p[1]).shape
        ins_v += [(two_d(a), _full(shp)) for a in (p[1], p[4], p[2], p[3])]
        outs_v += [(shp, F32, _full(shp), None)] * 4
    res_v = _ew(f_adam_many, (1,), ins_v, outs_v, "adamw_small")
    for k, p in enumerate(many):
        n, post = p[0], p[5]
        grads[n], deltas[n], new_m[n], new_v[n] = (post(r) for r in res_v[4 * k:4 * k + 4])

    gw_in = finish_wait(pending_back, res_v[0])
    adam("w_in", w_in_t, m_w_in_t, v_w_in_t, gw_in["w_in"], True)

    order = ("c_ctx", "w_ada", "b_ada", "norm1_g", "w_in", "q_norm_g", "kv_norm_g", "w_uq", "w_ukv", "conv_w", "conv_b",
             "w_attn_out", "w_conv_out", "w_o", "norm2_g", "w_up", "ffn_conv_w", "ffn_conv_b", "w_down", "final_g")
    return (loss, grad_x[None], *[grads[n] for n in order], *[deltas[n] for n in order],
            *[new_m[n] for n in order], *[new_v[n] for n in order])
```

```python
import functools

import jax
import jax.numpy as jnp
import numpy as np
from jax import lax
from jax.experimental import pallas as pl
from jax.experimental.pallas import tpu as pltpu

F32, BF16 = jnp.float32, jnp.bfloat16
MESH = pl.DeviceIdType.MESH

D_MODEL = 1024
N_HEADS = 8
HEAD_PAD = 128
QK_DIM = 96
Q_RANK, KV_RANK = 384, 256
CONV_DIM = 512
D_FF = 2816
GRID_W = 64
ROPE_THETA = 10000.0
EPS = 1e-6
GA0, GC0, CX0, CB0, CC0, KV0, Q0, KR0, P_COLS = 0, 1024, 2048, 2560, 3072, 3584, 3840, 4224, 4352
PA_KV0, PA_Q0, PA_KR0, PA_COLS = 0, 384, 768, 896
ROW_TILE = 256
VMEM_LIMIT_BYTES = 48 * 1024 * 1024

ADAM_LR, ADAM_B1, ADAM_B2, ADAM_EPS, ADAM_WD, ADAM_STEP = 0.001, 0.9, 0.999, 1e-08, 0.01, 10

NN = (((1,), (0,)), ((), ()))
NT = (((1,), (1,)), ((), ()))
TN = (((0,), (0,)), ((), ()))


def _cp(sem):
    return pltpu.CompilerParams(dimension_semantics=sem, vmem_limit_bytes=VMEM_LIMIT_BYTES)


PIN_BYTES = 1 << 19


def _in_hbm(arrays):
    return [pltpu.with_memory_space_constraint(a, pltpu.HBM) if a.size * a.dtype.itemsize >= PIN_BYTES else a
            for a in arrays]


def _out(shape, dtype):
    n = 1
    for d in shape:
        n *= d
    big = n * jnp.dtype(dtype).itemsize >= PIN_BYTES
    return pltpu.HBM(shape, dtype) if big else jax.ShapeDtypeStruct(shape, dtype)


def _pick(n, prefs):
    for p in prefs:
        if n % p == 0:
            return p
    return n


def _mm(a, b, mode, M, N, K, *, tm, tn, tk, name, out_dtype=F32, a_spec=None, b_spec=None, o_spec=None,
        out_shape=None, transpose_out=False):
    assert M % tm == 0 and N % tn == 0 and K % tk == 0, (name, M, N, K, tm, tn, tk)
    nk = K // tk
    dims = {"nn": NN, "nt": NT, "tn": TN}[mode]
    if a_spec is None:
        a_spec = (pl.BlockSpec((tk, tm), lambda i, j, k: (k, i)) if mode == "tn"
                  else pl.BlockSpec((tm, tk), lambda i, j, k: (i, k)))
    if b_spec is None:
        b_spec = (pl.BlockSpec((tn, tk), lambda i, j, k: (j, k)) if mode == "nt"
                  else pl.BlockSpec((tk, tn), lambda i, j, k: (k, j)))
    if o_spec is None:
        o_spec = (pl.BlockSpec((tn, tm), lambda i, j, k: (j, i)) if transpose_out
                  else pl.BlockSpec((tm, tn), lambda i, j, k: (i, j)))
    if out_shape is None:
        out_shape = (N, M) if transpose_out else (M, N)

    def emit(o_ref, val):
        o_ref[...] = (val.T if transpose_out else val).astype(o_ref.dtype)

    def body(a_ref, b_ref, o_ref, *scratch):
        part = lax.dot_general(a_ref[...].astype(BF16), b_ref[...].astype(BF16), dims, preferred_element_type=F32)
        if nk == 1:
            emit(o_ref, part)
            return
        acc_ref, = scratch
        k = pl.program_id(2)

        @pl.when(k == 0)
        def _():
            acc_ref[...] = part

        @pl.when((k > 0) & (k < nk - 1))
        def _():
            acc_ref[...] += part

        @pl.when(k == nk - 1)
        def _():
            emit(o_ref, acc_ref[...] + part)

    return pl.pallas_call(
        body, grid=(M // tm, N // tn, nk), in_specs=[a_spec, b_spec], out_specs=o_spec,
        out_shape=_out(out_shape, out_dtype),
        scratch_shapes=[pltpu.VMEM((tm, tn), F32)] if nk > 1 else [],
        compiler_params=_cp(("parallel", "parallel", "arbitrary")), name=name)(*_in_hbm([a, b]))


def _ew(fn, grid, ins, outs, name, scalars=None):
    n_in = len(ins)
    n_sc = 0 if scalars is None else 1

    def store(ref, val, acc, ids):
        if isinstance(val, (list, tuple)):
            for h, v in enumerate(val):
                ref[h] = v.astype(ref.dtype)
            return
        if acc is None:
            ref[...] = val.astype(ref.dtype)
            return

        @pl.when(ids[acc] == 0)
        def _():
            ref[...] = val.astype(ref.dtype)

        @pl.when(ids[acc] > 0)
        def _():
            ref[...] += val.astype(ref.dtype)

    def body(*refs):
        refs = refs[n_sc:]
        ids = tuple(pl.program_id(a) for a in range(len(grid)))
        vals = fn(ids, *[r[...] for r in refs[:n_in]])
        for ref, val, (_, _, _, acc) in zip(refs[n_in:], vals, outs):
            store(ref, val, acc, ids)

    acc_axes = {o[3] for o in outs if o[3] is not None}
    sem = tuple("arbitrary" if a in acc_axes else "parallel" for a in range(len(grid)))
    in_specs, out_specs = [s for _, s in ins], [o[2] for o in outs]
    out_shape = [_out(o[0], o[1]) for o in outs]
    args = _in_hbm([a for a, _ in ins])
    if scalars is None:
        return pl.pallas_call(body, grid=grid, in_specs=in_specs, out_specs=out_specs, out_shape=out_shape,
                              compiler_params=_cp(sem), name=name)(*args)
    spec = pltpu.PrefetchScalarGridSpec(num_scalar_prefetch=1, grid=grid, in_specs=in_specs, out_specs=out_specs)
    return pl.pallas_call(body, grid_spec=spec, out_shape=out_shape, compiler_params=_cp(sem), name=name)(scalars, *args)


def _rows(width, cblk=0, roff=0, tr=ROW_TILE):
    return pl.BlockSpec((tr, width), lambda i: (i + roff, cblk))


def _full(shape):
    nd = len(shape)
    return pl.BlockSpec(shape, lambda *_: (0,) * nd)


def _sigmoid(x):
    return 1.0 / (1.0 + jnp.exp2(x * (-1.4426950408889634)))


def _rms(x):
    return lax.rsqrt(jnp.mean(x * x, axis=-1, keepdims=True) + EPS)


def _rms_bwd(dn, xn, r):
    return r * (dn - xn * jnp.mean(dn * xn, axis=-1, keepdims=True))


def _colsum(x):
    return jnp.sum(x, axis=0, keepdims=True)


def _shifts(x):
    n = x.shape[0]
    rows = lax.broadcasted_iota(jnp.int32, x.shape, 0)
    return jnp.where(rows == 0, 0.0, pltpu.roll(x, 1, 0)), jnp.where(rows == n - 1, 0.0, pltpu.roll(x, n - 1, 0))


def _conv(x, w, b, shifted=None):
    prev, nxt = _shifts(x) if shifted is None else shifted
    return b + prev * w[0:1] + x * w[1:2] + nxt * w[2:3]


def _conv_bwd_x(dy, w):
    prev, nxt = _shifts(dy)
    return nxt * w[0:1] + dy * w[1:2] + prev * w[2:3]


def _conv_bwd_w(dy, x, shifted):
    prev, nxt = shifted
    return _colsum(dy * prev), _colsum(dy * x), _colsum(dy * nxt)


def _rope(x, cos, sin_lo, sin_hi):
    return x * cos + pltpu.roll(x, HEAD_PAD - 8, 1) * sin_lo + pltpu.roll(x, 8, 1) * sin_hi


ATTN_SCALE = QK_DIM ** -0.5
LOG2_E = 1.4426950408889634


def _rope_t(x, tab, inverse=False):
    o = 3 * HEAD_PAD if inverse else 0
    return _rope(x, tab[:, o:o + HEAD_PAD], tab[:, o + HEAD_PAD:o + 2 * HEAD_PAD], tab[:, o + 2 * HEAD_PAD:o + 3 * HEAD_PAD])


def _heads_keys(hp, kv_ref, kr_ref, tab_ref, kc_ref, vp_ref):
    kr_roped = _rope_t(kr_ref[...], tab_ref[...])
    lane = lax.broadcasted_iota(jnp.int32, kr_roped.shape, 1)
    for u in range(hp):
        kv = kv_ref[:, u * HEAD_PAD:(u + 1) * HEAD_PAD]
        kc_ref[u] = jnp.where(lane < 64, kv, kr_roped).astype(BF16)
        vp_ref[u] = jnp.where(lane >= 64, kv, 0.0).astype(BF16)


ATTN_Q_TILE = 512
ATTN_HEADS_PER_STEP = 2


def _attn_specs(tq, TT):
    q = pl.BlockSpec((tq, HEAD_PAD), lambda h, i: (i, h))
    keys = pl.BlockSpec((TT, HEAD_PAD), lambda h, i: (0, h))
    kr = pl.BlockSpec((TT, HEAD_PAD), lambda h, i: (0, PA_KR0 // HEAD_PAD))
    tab_q = pl.BlockSpec((tq, 6 * HEAD_PAD), lambda h, i: (i, 0))
    tab_k = pl.BlockSpec((TT, 6 * HEAD_PAD), lambda h, i: (0, 0))
    return q, keys, kr, tab_q, tab_k


def _attn_fwd(q_raw, kv, pp, tab, T, TT):
    tq, hp = ROW_TILE, 2 * ATTN_HEADS_PER_STEP
    w = hp * HEAD_PAD

    def body(q_ref, kv_ref, kr_ref, tq_ref, tk_ref, o_ref, kc, vp):
        @pl.when(pl.program_id(1) == 0)
        def _():
            _heads_keys(hp, kv_ref, kr_ref, tk_ref, kc, vp)

        tab = tq_ref[...]
        for u in range(hp):
            cols = slice(u * HEAD_PAD, (u + 1) * HEAD_PAD)
            q = _rope_t(q_ref[:, cols], tab).astype(BF16)
            s = lax.dot_general(q, kc[u], NT, preferred_element_type=F32)
            m = jnp.max(s, axis=-1, keepdims=True)
            p = jnp.exp2((s - m) * (ATTN_SCALE * LOG2_E))
            l = jnp.sum(p, axis=-1, keepdims=True)
            o = lax.dot_general(p.astype(BF16), vp[u], NN, preferred_element_type=F32)
            lane = lax.broadcasted_iota(jnp.int32, o.shape, 1)
            o_ref[:, cols] = jnp.where(lane < 64, m * ATTN_SCALE + jnp.log(l), o / l)

    _, _, kr, _, _ = _attn_specs(tq, TT)
    qs = pl.BlockSpec((tq, w), lambda h, i: (i, h))
    keys = pl.BlockSpec((TT, w), lambda h, i: (0, h))
    tab_q = pl.BlockSpec((tq, 3 * HEAD_PAD), lambda h, i: (i, 0))
    tab_k = pl.BlockSpec((TT, 3 * HEAD_PAD), lambda h, i: (0, 0))
    return pl.pallas_call(
        body, grid=(N_HEADS // hp, T // tq), in_specs=[qs, keys, kr, tab_q, tab_k], out_specs=qs,
        out_shape=jax.ShapeDtypeStruct((T, N_HEADS * HEAD_PAD), F32),
        scratch_shapes=[pltpu.VMEM((hp, TT, HEAD_PAD), BF16), pltpu.VMEM((hp, TT, HEAD_PAD), BF16)],
        compiler_params=_cp(("parallel", "arbitrary")), name="attn_fwd",
    )(*_in_hbm([q_raw, kv, pp, tab, tab]))


def _attn_bwd(q_raw, kv, pp, o, do, tab, T, TT):
    tq = _pick(T, (ATTN_Q_TILE, ROW_TILE))
    nq = T // tq
    hp = ATTN_HEADS_PER_STEP
    w = hp * HEAD_PAD

    def body(q_ref, kv_ref, kr_ref, tq_ref, tk_ref, o_ref, do_ref, dq_ref, dkv_ref, dkr_ref, kc, vp, dk, dv):
        g, i = pl.program_id(0), pl.program_id(1)

        @pl.when(i == 0)
        def _():
            _heads_keys(hp, kv_ref, kr_ref, tk_ref, kc, vp)
            dk[...] = jnp.zeros_like(dk)
            dv[...] = jnp.zeros_like(dv)

        tab = tq_ref[...]
        for u in range(hp):
            cols = slice(u * HEAD_PAD, (u + 1) * HEAD_PAD)
            q = _rope_t(q_ref[:, cols], tab).astype(BF16)
            k, v, d_o = kc[u], vp[u], do_ref[:, cols]
            s = lax.dot_general(q, k, NT, preferred_element_type=F32)
            o = o_ref[:, cols]
            p = jnp.exp2(s * (ATTN_SCALE * LOG2_E) - o[:, 0:1] * LOG2_E)
            dob = d_o.astype(BF16)
            dp = lax.dot_general(dob, v, NT, preferred_element_type=F32)
            dd = jnp.sum(d_o * o, axis=-1, keepdims=True)
            ds = (p * (dp - dd) * ATTN_SCALE).astype(BF16)
            dq = lax.dot_general(ds, k, NN, preferred_element_type=F32)
            dq_ref[:, cols] = _rope_t(dq, tab, inverse=True).astype(dq_ref.dtype)
            dk[u] += lax.dot_general(q, ds, TN, preferred_element_type=F32)
            dv[u] += lax.dot_general(dob, p.astype(BF16), TN, preferred_element_type=F32)

        @pl.when(i == nq - 1)
        def _():
            rot = None
            for u in range(hp):
                dkh = dk[u].T
                lane = lax.broadcasted_iota(jnp.int32, dkh.shape, 1)
                dkv_ref[:, u * HEAD_PAD:(u + 1) * HEAD_PAD] = jnp.where(lane < 64, dkh, dv[u].T).astype(dkv_ref.dtype)
                part = jnp.where((lane >= 64) & (lane < 96), dkh, 0.0)
                rot = part if rot is None else rot + part
            rot = _rope_t(rot, tk_ref[...], inverse=True)

            @pl.when(g == 0)
            def _():
                dkr_ref[...] = rot

            @pl.when(g > 0)
            def _():
                dkr_ref[...] += rot

    _, _, kr, tab_q, tab_k = _attn_specs(tq, TT)
    qs = pl.BlockSpec((tq, w), lambda h, i: (i, h))
    keys = pl.BlockSpec((TT, w), lambda h, i: (0, h))
    wide = lambda rows: jax.ShapeDtypeStruct((rows, N_HEADS * HEAD_PAD), BF16)
    return pl.pallas_call(
        body, grid=(N_HEADS // hp, nq),
        in_specs=[qs, keys, kr, tab_q, tab_k, qs, qs],
        out_specs=[qs, keys, pl.BlockSpec((TT, HEAD_PAD), lambda h, i: (0, 0))],
        out_shape=[wide(T), wide(TT), jax.ShapeDtypeStruct((TT, HEAD_PAD), F32)],
        scratch_shapes=[pltpu.VMEM((hp, TT, HEAD_PAD), BF16), pltpu.VMEM((hp, TT, HEAD_PAD), BF16),
                        pltpu.VMEM((hp, HEAD_PAD, TT), F32), pltpu.VMEM((hp, HEAD_PAD, TT), F32)],
        compiler_params=_cp(("arbitrary", "arbitrary")), name="attn_bwd",
    )(*_in_hbm([q_raw, kv, pp, tab, tab, o, do]))


def _hbm_specs(n):
    return [pl.BlockSpec(memory_space=pl.ANY)] * n


def _gather_weights(shards):
    n = len(shards)
    halves = [s.shape[0] // 2 for s in shards]

    def body(*refs):
        ins, outs = refs[:n], refs[n:2 * n]
        token, send_sems, recv_sems = refs[2 * n:]
        token[...] = jnp.zeros_like(token)
        mx, my, mc = lax.axis_index("x"), lax.axis_index("y"), lax.axis_index("c")
        j_me = 2 * mx + my
        chips = [(1 - mx, my), (mx, 1 - my), (1 - mx, 1 - my)]

        def half(w, chip_idx, hc):
            return outs[w].at[chip_idx, pl.ds(hc * halves[w], halves[w]), :]

        def copy(w, k, src, dst, to):
            return pltpu.make_async_remote_copy(src_ref=src, dst_ref=dst, send_sem=send_sems.at[w, k],
                                                recv_sem=recv_sems.at[w, k], device_id=to, device_id_type=MESH)

        sends = []
        for w in range(n):
            cp = copy(w, 6, ins[w], outs[w].at[j_me], (mx, my, 1 - mc))
            cp.start()
            sends.append(cp)
        for k, (px, py) in enumerate(chips):
            for w in range(n):
                cp = copy(w, k, ins[w].at[pl.ds(mc * halves[w], halves[w]), :], half(w, j_me, mc), (px, py, mc))
                cp.start()
                sends.append(cp)
        for k, (px, py) in enumerate(chips):
            for w in range(n):
                got = half(w, 2 * px + py, mc)
                copy(w, k, got, got, (px, py, mc)).wait_recv()
                cp = copy(w, 3 + k, got, got, (mx, my, 1 - mc))
                cp.start()
                sends.append(cp)
        for k, (px, py) in enumerate(chips):
            for w in range(n):
                got = half(w, 2 * px + py, 1 - mc)
                copy(w, 3 + k, got, got, (mx, my, 1 - mc)).wait_recv()
        for w in range(n):
            own = outs[w].at[j_me]
            copy(w, 6, own, own, (mx, my, 1 - mc)).wait_recv()
        for cp in sends:
            cp.wait_send()

    res = pl.pallas_call(
        body, out_shape=[jax.ShapeDtypeStruct((4,) + s.shape, s.dtype) for s in shards]
        + [jax.ShapeDtypeStruct((8, 128), F32)],
        in_specs=_hbm_specs(n), out_specs=_hbm_specs(n) + [pl.BlockSpec(memory_space=pltpu.VMEM)],
        scratch_shapes=[pltpu.SemaphoreType.DMA((n, 7)), pltpu.SemaphoreType.DMA((n, 7))],
        name="gather_weights")(*shards)
    return list(res[:n]), res[n]


def _rs_pair(gs, name):
    n = len(gs)
    halves = [g.shape[1] // 2 for g in gs]

    def body(*refs):
        ins, lands = refs[:n], refs[n:2 * n]
        send_sems, recv_sems = refs[2 * n:]
        mx, my, mc = lax.axis_index("x"), lax.axis_index("y"), lax.axis_index("c")
        copies = []
        for w in range(n):
            h = halves[w]
            cp = pltpu.make_async_remote_copy(
                src_ref=ins[w].at[:, pl.ds((1 - mc) * h, h), :], dst_ref=lands[w], send_sem=send_sems.at[w],
                recv_sem=recv_sems.at[w], device_id=(mx, my, 1 - mc), device_id_type=MESH)
            cp.start()
            copies.append(cp)
        for cp in copies:
            cp.wait()

    return pl.pallas_call(
        body, out_shape=[jax.ShapeDtypeStruct((4, h, g.shape[2]), g.dtype) for g, h in zip(gs, halves)],
        in_specs=_hbm_specs(n), out_specs=_hbm_specs(n),
        scratch_shapes=[pltpu.SemaphoreType.DMA((n,)), pltpu.SemaphoreType.DMA((n,))], name=name)(*gs)


def _rs_chips(parts):
    n = len(parts)

    def body(*refs):
        ins, lands = refs[:n], refs[n:2 * n]
        send_sems, recv_sems = refs[2 * n:]
        mx, my, mc = lax.axis_index("x"), lax.axis_index("y"), lax.axis_index("c")
        copies = []
        for k, (px, py) in enumerate([(1 - mx, my), (mx, 1 - my), (1 - mx, 1 - my)]):
            for w in range(n):
                cp = pltpu.make_async_remote_copy(
                    src_ref=ins[w].at[2 * px + py], dst_ref=lands[w].at[k], send_sem=send_sems.at[w, k],
                    recv_sem=recv_sems.at[w, k], device_id=(px, py, mc), device_id_type=MESH)
                cp.start()
                copies.append(cp)
        for cp in copies:
            cp.wait()

    return list(pl.pallas_call(
        body, out_shape=[jax.ShapeDtypeStruct((3,) + p.shape[1:], p.dtype) for p in parts],
        in_specs=_hbm_specs(n), out_specs=_hbm_specs(n),
        scratch_shapes=[pltpu.SemaphoreType.DMA((n, 3)), pltpu.SemaphoreType.DMA((n, 3))], name="rs_chips")(*parts))


_HBM = pl.BlockSpec(memory_space=pltpu.HBM)
_SEM = pl.BlockSpec(memory_space=pltpu.SEMAPHORE)
_EFFECT = pltpu.SideEffectType.DATAFLOW_SIDE_EFFECTING


def _ici_copies(kind, srcs, lands, send_sems, recv_sems):
    n = len(lands)
    mx, my, mc = lax.axis_index("x"), lax.axis_index("y"), lax.axis_index("c")
    j_me = 2 * mx + my
    copies = []
    if kind == "back":
        for w in range(n):
            h = lands[w].shape[0] // 2
            mine = lands[w].at[pl.ds(mc * h, h), :]
            copies.append(pltpu.make_async_remote_copy(
                src_ref=mine, dst_ref=mine, send_sem=send_sems.at[w], recv_sem=recv_sems.at[w],
                device_id=(mx, my, 1 - mc), device_id_type=MESH))
        return copies
    if kind == "all":
        for k in range(7):
            a, b, c = (k + 1) >> 2 & 1, (k + 1) >> 1 & 1, (k + 1) & 1
            peer = (1 - mx if a else mx, 1 - my if b else my, 1 - mc if c else mc)
            for w in range(n):
                copies.append(pltpu.make_async_remote_copy(
                    src_ref=srcs[w], dst_ref=lands[w].at[4 * mx + 2 * my + mc], send_sem=send_sems.at[7 * w + k],
                    recv_sem=recv_sems.at[7 * w + k], device_id=peer, device_id_type=MESH))
        return copies
    if kind == "pair":
        for w in range(n):
            h = srcs[w].shape[1] // 2
            copies.append(pltpu.make_async_remote_copy(
                src_ref=srcs[w].at[:, pl.ds((1 - mc) * h, h), :], dst_ref=lands[w], send_sem=send_sems.at[w],
                recv_sem=recv_sems.at[w], device_id=(mx, my, 1 - mc), device_id_type=MESH))
        return copies
    chips = [(1 - mx, my), (mx, 1 - my), (1 - mx, 1 - my)]
    if kind == "finish":
        for w in range(n):
            h = srcs[w].shape[0] // 2
            pushes = [(lands[w].at[2 * px + py, pl.ds(mc * h, h), :],) * 2 for px, py in chips]
            pushes.append((srcs[w], lands[w].at[j_me]))
            for k, (src, dst) in enumerate(pushes):
                copies.append(pltpu.make_async_remote_copy(
                    src_ref=src, dst_ref=dst, send_sem=send_sems.at[4 * w + k], recv_sem=recv_sems.at[4 * w + k],
                    device_id=(mx, my, 1 - mc), device_id_type=MESH))
        return copies
    for k, (px, py) in enumerate(chips):
        for w in range(n):
            if kind == "gather":
                h = srcs[w].shape[0] // 2
                src, dst = srcs[w].at[pl.ds(mc * h, h), :], lands[w].at[j_me, pl.ds(mc * h, h), :]
            else:
                src, dst = srcs[w].at[2 * px + py], lands[w].at[k]
            copies.append(pltpu.make_async_remote_copy(
                src_ref=src, dst_ref=dst, send_sem=send_sems.at[3 * w + k], recv_sem=recv_sems.at[3 * w + k],
                device_id=(px, py, mc), device_id_type=MESH))
    return copies


_SEMS_PER_OPERAND = {"gather": 3, "scatter": 3, "all": 7, "pair": 1, "finish": 4, "back": 1}


def _ici_start(kind, srcs, land_shapes, carry, name, lands=None):
    hbm = lambda a: pltpu.with_memory_space_constraint(a, pltpu.HBM)
    if lands is None:
        lands = [lax.empty(s, srcs[0].dtype) for s in land_shapes]
    ns, nl = len(srcs), len(lands)

    def body(*refs):
        send_sems, recv_sems = refs[ns + nl + 1], refs[ns + nl + 2]
        for cp in _ici_copies(kind, refs[:ns], refs[ns:ns + nl], send_sems, recv_sems):
            cp.start()

    args = [hbm(a) for a in list(srcs) + list(lands) + [carry]]
    n_sem = _SEMS_PER_OPERAND[kind] * nl
    out_shape = ([pltpu.SemaphoreType.DMA((n_sem,)), pltpu.SemaphoreType.DMA((n_sem,))]
                 + [pltpu.HBM(a.shape, a.dtype) for a in args])
    res = pl.pallas_call(
        body, name=name, out_shape=out_shape, in_specs=[_HBM] * len(args), out_specs=[_SEM, _SEM] + [_HBM] * len(args),
        input_output_aliases={i: 2 + i for i in range(len(args))},
        compiler_params=pltpu.CompilerParams(has_side_effects=_EFFECT))(*args)
    return res[0], res[1], list(res[2:2 + ns]), list(res[2 + ns:2 + ns + nl]), res[2 + ns + nl]


def _ici_wait(kind, send_sems, recv_sems, srcs, lands, after, name):
    ns, nl = len(srcs), len(lands)

    def body(*refs):
        for cp in _ici_copies(kind, refs[:ns], refs[ns:ns + nl], refs[ns + nl], refs[ns + nl + 1]):
            cp.wait_send()
            cp.wait_recv()

    args = list(srcs) + list(lands)
    res = pl.pallas_call(
        body, name=name, out_shape=[pltpu.HBM(a.shape, a.dtype) for a in args],
        in_specs=[_HBM] * len(args) + [_SEM, _SEM, pl.BlockSpec(memory_space=pl.ANY)], out_specs=[_HBM] * len(args),
        input_output_aliases={i: i for i in range(len(args))},
        compiler_params=pltpu.CompilerParams(has_side_effects=_EFFECT))(*args, send_sems, recv_sems, after)
    return list(res[:ns]), list(res[ns:])


def _tile_rows(h, c, itemsize, mult):
    best = h
    for t in range(mult, h + 1, mult):
        if h % t == 0 and t * c * itemsize <= (1 << 21):
            best = t
    return best


def _add_pair(g, land, place, name):
    _, h, c = land.shape
    t = _tile_rows(h, c, 2, 16)
    nb = h // t
    return _ew(lambda ids, u, v: (u.astype(F32) + v.astype(F32),), (4, nb),
               [(g, pl.BlockSpec((None, t, c), lambda j, i, s: (j, s[1] * nb + i, 0))),
                (land, pl.BlockSpec((None, t, c), lambda j, i, s: (j, i, 0)))],
               [(land.shape, BF16, pl.BlockSpec((None, t, c), lambda j, i, s: (j, i, 0)), None)], name, scalars=place)[0]


def _add_pair_many(gs, lands, place, name):
    ins, outs = [], []
    for g, l in zip(gs, lands):
        ins += [(g, pl.BlockSpec(l.shape, lambda i, s: (0, s[1], 0))), (l, pl.BlockSpec(l.shape, lambda i, s: (0, 0, 0)))]
        outs.append((l.shape, BF16, pl.BlockSpec(l.shape, lambda i, s: (0, 0, 0)), None))
    fn = lambda ids, *v: [v[2 * k].astype(F32) + v[2 * k + 1].astype(F32) for k in range(len(gs))]
    return list(_ew(fn, (1,), ins, outs, name, scalars=place))


def _add_chips_many(owns, lands, place, name):
    ins, outs = [], []
    for own, land in zip(owns, lands):
        _, h, c = land.shape
        ins += [(own, pl.BlockSpec((None, h, c), lambda i, s: (s[0], 0, 0))),
                (land, pl.BlockSpec((3, h, c), lambda i, s: (0, 0, 0)))]
        outs.append(((2 * h, c), F32, pl.BlockSpec((h, c), lambda i, s: (s[1], 0)), None))

    def fn(ids, *v):
        return [((v[2 * k].astype(F32) + v[2 * k + 1][0].astype(F32)) + v[2 * k + 1][1].astype(F32))
                + v[2 * k + 1][2].astype(F32) for k in range(len(owns))]

    return list(_ew(fn, (1,), ins, outs, name, scalars=place))


def _add_chips(own, land, place, name):
    _, h, c = land.shape
    t = _tile_rows(h, c, 4, 16)
    nb = h // t

    def fn(ids, a, b):
        return (((a.astype(F32) + b[0].astype(F32)) + b[1].astype(F32)) + b[2].astype(F32),)

    return _ew(fn, (nb,), [(own, pl.BlockSpec((None, t, c), lambda i, s: (s[0], i, 0))),
                           (land, pl.BlockSpec((3, t, c), lambda i, s: (0, i, 0)))],
               [((2 * h, c), F32, pl.BlockSpec((t, c), lambda i, s: (s[1] * nb + i, 0)), None)], name, scalars=place)[0]


W_IN_SEGMENTS = ((0, 256, KV0), (256, 288, KR0 + 64), (288, 672, Q0), (672, 1184, CX0), (1184, 1696, CB0),
                 (1696, 2208, CC0), (2208, 3232, GA0), (3232, 4256, GC0))
W_IN_SHARD = 1064


W_IN_SHARD_PAD = 1088
W_IN_EARLY = 672


def _w_in_t_p_from_shards(s):
    pieces = []
    for o0, o1, p0 in sorted(W_IN_SEGMENTS, key=lambda t: t[2]):
        if p0 == KR0 + 64:
            pieces.append(jnp.zeros((64, s.shape[2]), s.dtype))
        for j in range(4):
            lo, hi = max(o0, j * W_IN_SHARD), min(o1, (j + 1) * W_IN_SHARD)
            if lo < hi:
                pieces.append(s[j, lo - j * W_IN_SHARD:hi - j * W_IN_SHARD])
    pieces.append(jnp.zeros((32, s.shape[2]), s.dtype))
    return jnp.concatenate(pieces, axis=0)


def _w_in_t_shards_from_p(g):
    shards = []
    for j in range(4):
        pieces = []
        for o0, o1, p0 in W_IN_SEGMENTS:
            lo, hi = max(o0, j * W_IN_SHARD), min(o1, (j + 1) * W_IN_SHARD)
            if lo < hi:
                pieces.append(g[p0 + lo - o0:p0 + hi - o0])
        pieces.append(jnp.zeros((W_IN_SHARD_PAD - W_IN_SHARD, g.shape[1]), g.dtype))
        shards.append(jnp.concatenate(pieces, axis=0))
    return jnp.stack(shards, axis=0)


def _cols_from_shards(s):
    return jnp.transpose(s, (1, 0, 2)).reshape(s.shape[1], -1)


def _rope_tables(T, TT, inverse):
    f32 = np.float32
    rows = T // GRID_W
    row = np.repeat(np.arange(rows), GRID_W).astype(f32)
    col = np.tile(np.arange(GRID_W), rows).astype(f32)
    inv = (f32(ROPE_THETA) ** (-np.arange(0, 16, 2, dtype=f32) / f32(16))).astype(f32)
    ang = np.concatenate([row[:, None] * inv, col[:, None] * inv], axis=-1).astype(f32)
    cos, sin = np.cos(ang).astype(f32), np.sin(ang).astype(f32)
    lane = np.arange(32)
    src = (lane // 16) * 8 + lane % 8
    lo = ((lane % 16) // 8 == 0).astype(f32)
    sgn = f32(-1.0 if inverse else 1.0)
    cos32 = cos[:, src]
    sin_lo32 = -sgn * sin[:, src] * lo
    sin_hi32 = sgn * sin[:, src] * (1 - lo)

    def widen(t32, fill):
        t = np.concatenate([np.full((T, 64), fill, f32), t32, np.full((T, 32), fill, f32)], axis=1)
        return np.concatenate([t, np.full((TT - T, HEAD_PAD), fill, f32)], axis=0)

    return [widen(cos32, 1.0), widen(sin_lo32, 0.0), widen(sin_hi32, 0.0)]


def _rope_table(T, TT):
    return jnp.asarray(np.concatenate(_rope_tables(T, TT, False) + _rope_tables(T, TT, True), axis=1))


def _local_step(xx, tgt, mod_lat, mod_ctx, W, late_weights, early_grads, early_continue):
    TT = xx.shape[0]
    T = tgt.shape[0]
    n_lat, n_all = T // ROW_TILE, TT // ROW_TILE
    sh1, sc1, g1, sh2, sc2, g2 = [mod_lat[:, k * D_MODEL:(k + 1) * D_MODEL] for k in range(6)]
    csh1, csc1 = mod_ctx[:, :D_MODEL], mod_ctx[:, D_MODEL:2 * D_MODEL]
    vec = lambda n: _full((1, n))
    row_out = lambda n, dt, rows=T: ((rows, n), dt, _rows(n), None)
    acc_out = lambda n: ((1, n), F32, _full((1, n)), 0)
    lt = _pick(T, (2 * ROW_TILE, ROW_TILE))
    n_lt = T // lt
    lrows = lambda n, cblk=0: _rows(n, cblk, 0, lt)
    lrow_out = lambda n, dt: ((T, n), dt, lrows(n), None)

    def f_norm1(ids, x, g, a_sh, a_sc, b_sh, b_sc):
        ctx = ids[0] >= n_lat
        sh, sc = jnp.where(ctx, b_sh, a_sh), jnp.where(ctx, b_sc, a_sc)
        return ((x * _rms(x) * g) * (1.0 + sc) + sh,)

    (hh,) = _ew(f_norm1, (n_all,), [(xx, _rows(D_MODEL)), (W["norm1_g"], vec(D_MODEL)), (sh1, vec(D_MODEL)),
                                   (sc1, vec(D_MODEL)), (csh1, vec(D_MODEL)), (csc1, vec(D_MODEL))],
                [row_out(D_MODEL, BF16, TT)], "norm1_fwd")
    tm_all = _pick(TT, (768, 256))
    pp_a = _mm(hh, W["w_in_a_t"], "nt", TT, PA_COLS, D_MODEL, tm=tm_all, tn=PA_COLS, tk=D_MODEL, name="w_in_fwd_a")

    def f_lowrank(ids, ckv, cq, gkv, gq):
        return ckv * _rms(ckv) * gkv, cq * _rms(cq) * gq

    nkv, nq = _ew(f_lowrank, (n_all,), [(pp_a, _rows(KV_RANK, PA_KV0 // KV_RANK)), (pp_a, _rows(Q_RANK, PA_Q0 // Q_RANK)),
                                       (W["kv_norm_g"], vec(KV_RANK)), (W["q_norm_g"], vec(Q_RANK))],
                  [row_out(KV_RANK, BF16, TT), row_out(Q_RANK, BF16, TT)], "lowrank_norm_fwd")
    kv = _mm(nkv, W["w_ukv"], "nn", TT, 1024, KV_RANK, tm=tm_all, tn=256, tk=KV_RANK, name="w_ukv_fwd",
             b_spec=pl.BlockSpec((None, KV_RANK, 256), lambda i, j, k: (j, k, 0)))
    q_raw = _mm(nq, W["w_uq_t"], "nt", TT, 1024, Q_RANK, tm=tm_all, tn=1024, tk=Q_RANK, name="w_uq_fwd")

    tab = _rope_table(T, TT)
    _, q_raw = late_weights("before_attn", q_raw)
    o_pad = _attn_fwd(q_raw, kv, pp_a, tab, T, TT)
    arrived, o_pad = late_weights("after_attn", o_pad)
    W = dict(W, **arrived)
    tm_lat = _pick(T, (1024, 512, 256))
    pp = _mm(hh, W["w_in_t"], "nt", T, KV0, D_MODEL, tm=tm_lat, tn=KV0 // 2, tk=D_MODEL, name="w_in_fwd_b")
    ya = _mm(o_pad, W["w_attn_out"], "nn", T, D_MODEL, 1024, tm=tm_lat, tn=D_MODEL, tk=1024, name="w_attn_out_fwd",
             out_dtype=BF16)

    tc = 256
    colT = lambda blk0: pl.BlockSpec((T, tc), lambda j: (0, blk0 + j))

    def f_conv(ids, xin, cb, cc, w, b):
        return (cb * _conv(cc * xin, w, b),)

    (e,) = _ew(f_conv, (CONV_DIM // tc,),
               [(pp, colT(CX0 // tc)), (pp, colT(CB0 // tc)), (pp, colT(CC0 // tc)),
                (W["conv_w"], pl.BlockSpec((3, tc), lambda j: (0, j))), (W["conv_b"], pl.BlockSpec((1, tc), lambda j: (0, j)))],
               [((T, CONV_DIM), BF16, colT(0), None)], "conv_fwd")
    yc = _mm(e, W["w_conv_out"], "nn", T, D_MODEL, CONV_DIM, tm=tm_lat, tn=256, tk=CONV_DIM, name="w_conv_out_fwd",
             out_dtype=BF16, b_spec=pl.BlockSpec((None, CONV_DIM, 256), lambda i, j, k: (j, k, 0)))

    def f_merge(ids, ga, gc, a, c):
        return (_sigmoid(ga) * a.astype(F32) + _sigmoid(gc) * c.astype(F32),)

    (mrg,) = _ew(f_merge, (n_lt,), [(pp, lrows(D_MODEL, 0)), (pp, lrows(D_MODEL, 1)), (ya, lrows(D_MODEL)),
                                   (yc, lrows(D_MODEL))], [lrow_out(D_MODEL, BF16)], "merge_fwd")
    mo = _mm(mrg, W["w_o"], "nn", T, D_MODEL, D_MODEL, tm=tm_lat, tn=D_MODEL, tk=D_MODEL, name="w_o_fwd")

    def f_norm2(ids, x, m, gate, g, sh, sc):
        x1 = x + gate * m
        return x1, (x1 * _rms(x1) * g) * (1.0 + sc) + sh

    x1, h2 = _ew(f_norm2, (n_lt,), [(xx, lrows(D_MODEL)), (mo, lrows(D_MODEL)), (g1, vec(D_MODEL)),
                                   (W["norm2_g"], vec(D_MODEL)), (sh2, vec(D_MODEL)), (sc2, vec(D_MODEL))],
                 [lrow_out(D_MODEL, F32), lrow_out(D_MODEL, BF16)], "norm2_fwd")
    arrived, h2 = late_weights("before_ffn", h2)
    W = dict(W, **arrived)
    up = _mm(h2, W["w_up"], "nn", T, 2 * D_FF, D_MODEL, tm=tm_lat, tn=1408, tk=D_MODEL, name="w_up_fwd",
             b_spec=pl.BlockSpec((None, D_MODEL, 1408), lambda i, j, k: (j, k, 0)))

    n_ff = D_FF // tc
    ffw = lambda off, n=3: pl.BlockSpec((n, tc), lambda j: (0, j + off))

    def f_ffn(ids, ug, uv, wg, wv, bg, bv):
        gate, val = _conv(ug, wg, bg), _conv(uv, wv, bv)
        return (gate * _sigmoid(gate) * val,)

    (act,) = _ew(f_ffn, (n_ff,), [(up, colT(0)), (up, colT(n_ff)), (W["ffn_conv_w"], ffw(0)), (W["ffn_conv_w"], ffw(n_ff)),
                                 (W["ffn_conv_b"], ffw(0, 1)), (W["ffn_conv_b"], ffw(n_ff, 1))],
                 [((T, D_FF), BF16, colT(0), None)], "ffn_act_fwd")
    f = _mm(act, W["w_down"], "nn", T, D_MODEL, D_FF, tm=tm_lat, tn=D_MODEL, tk=D_FF, name="w_down_fwd")

    def f_head(ids, x1_, f_, gate, gf, t):
        x2 = x1_ + gate * f_
        r = _rms(x2)
        xn = x2 * r
        err = xn * gf - t
        loss = 0.5 * jnp.sum(jnp.mean(err * err, axis=-1, keepdims=True))
        dy = err * (1.0 / D_MODEL)
        dx2 = _rms_bwd(dy * gf, xn, r)
        return dx2, dx2 * gate, _colsum(dy * xn), _colsum(dx2 * f_), jnp.full((1, 128), loss, F32)

    dx2, df, dg_f, dg2, loss = _ew(
        f_head, (n_lt,), [(x1, lrows(D_MODEL)), (f, lrows(D_MODEL)), (g2, vec(D_MODEL)), (W["final_g"], vec(D_MODEL)),
                          (tgt, lrows(D_MODEL))],
        [lrow_out(D_MODEL, F32), lrow_out(D_MODEL, BF16), acc_out(D_MODEL), acc_out(D_MODEL), acc_out(128)], "loss_head")

    d_w_down = _mm(act, df, "tn", D_FF, D_MODEL, T, tm=1408, tn=D_MODEL, tk=T, name="w_down_dw",
                   out_dtype=BF16).reshape(4, D_FF // 4, D_MODEL)
    da = _mm(df, W["w_down"], "nt", T, D_FF, D_MODEL, tm=tm_lat, tn=1408, tk=D_MODEL, name="w_down_dx")

    tcb = 128
    n_fb = D_FF // tcb
    colb = lambda blk0: pl.BlockSpec((T, tcb), lambda j: (0, blk0 + j))
    ffwb = lambda off, n=3: pl.BlockSpec((n, tcb), lambda j: (0, j + off))
    cvec = ((1, D_FF), F32, pl.BlockSpec((1, tcb), lambda j: (0, j)), None)

    def f_ffn_bwd(ids, ug, uv, d_act, wg, wv, bg, bv):
        sg, sv = _shifts(ug), _shifts(uv)
        gate, val = _conv(ug, wg, bg, sg), _conv(uv, wv, bv, sv)
        s = _sigmoid(gate)
        d_gate = d_act * val * s * (1.0 + gate * (1.0 - s))
        d_val = d_act * gate * s
        wg0, wg1, wg2 = _conv_bwd_w(d_gate, ug, sg)
        wv0, wv1, wv2 = _conv_bwd_w(d_val, uv, sv)
        d_up = [_conv_bwd_x(d_gate, wg), _conv_bwd_x(d_val, wv)]
        return d_up, [_colsum(d_gate), _colsum(d_val), wg0, wg1, wg2, wv0, wv1, wv2]

    d_up3, ffn_stats = _ew(
        f_ffn_bwd, (n_fb,),
        [(up, colb(0)), (up, colb(n_fb)), (da, colb(0)), (W["ffn_conv_w"], ffwb(0)), (W["ffn_conv_w"], ffwb(n_fb)),
         (W["ffn_conv_b"], ffwb(0, 1)), (W["ffn_conv_b"], ffwb(n_fb, 1))],
        [((2, T, D_FF), BF16, pl.BlockSpec((2, T, tcb), lambda j: (0, 0, j)), None),
         ((n_fb, 8, 1, tcb), F32, pl.BlockSpec((None, 8, 1, tcb), lambda j: (j, 0, 0, 0)), None)], "ffn_act_bwd")
    stat = lambda s: ffn_stats[:, s, 0, :].reshape(1, D_FF)
    d_ffn_conv_b = jnp.concatenate([stat(0), stat(1)], axis=1)
    d_ffn_conv_w = jnp.concatenate([jnp.concatenate([stat(2), stat(3), stat(4)], axis=0),
                                    jnp.concatenate([stat(5), stat(6), stat(7)], axis=0)], axis=1)

    tk_t = T
    d_w_up = _mm(h2, d_up3, "tn", D_MODEL, 2 * D_FF, T, tm=D_MODEL, tn=1408, tk=tk_t, name="w_up_dw", out_dtype=BF16,
                 b_spec=pl.BlockSpec((None, tk_t, 1408), lambda i, j, k: (j // 2, k, j % 2)),
                 o_spec=pl.BlockSpec((None, D_MODEL, 1408), lambda i, j, k: (j, i, 0)), out_shape=(4, D_MODEL, 1408))
    dh2 = _mm(d_up3, W["w_up"], "nt", T, D_MODEL, 2 * D_FF, tm=tm_lat, tn=D_MODEL, tk=1408, name="w_up_dx",
              a_spec=pl.BlockSpec((None, tm_lat, 1408), lambda i, j, k: (k // 2, i, k % 2)),
              b_spec=pl.BlockSpec((None, D_MODEL, 1408), lambda i, j, k: (k, j, 0)))

    def f_norm2_bwd(ids, dx2_, dh, x1_, m, g, sc, gate):
        r = _rms(x1_)
        xn = x1_ * r
        dx1 = dx2_ + _rms_bwd(dh * g * (1.0 + sc), xn, r)
        return dx1, dx1 * gate, _colsum(dh), _colsum(dh * xn * g), _colsum(dh * xn * (1.0 + sc)), _colsum(dx1 * m)

    dx1, dmo, dsh2, dsc2, dg_n2, dg1 = _ew(
        f_norm2_bwd, (n_lt,), [(dx2, lrows(D_MODEL)), (dh2, lrows(D_MODEL)), (x1, lrows(D_MODEL)), (mo, lrows(D_MODEL)),
                               (W["norm2_g"], vec(D_MODEL)), (sc2, vec(D_MODEL)), (g1, vec(D_MODEL))],
        [lrow_out(D_MODEL, F32), lrow_out(D_MODEL, BF16)] + [acc_out(D_MODEL)] * 4, "norm2_bwd")
    d_w_o = _mm(mrg, dmo, "tn", D_MODEL, D_MODEL, T, tm=D_MODEL, tn=D_MODEL, tk=tk_t, name="w_o_dw",
                out_dtype=BF16).reshape(4, D_MODEL // 4, D_MODEL)
    dmrg = _mm(dmo, W["w_o"], "nt", T, D_MODEL, D_MODEL, tm=tm_lat, tn=D_MODEL, tk=D_MODEL, name="w_o_dx",
               out_dtype=BF16)
    dmrg = early_grads("late", {"w_o": d_w_o, "w_up": d_w_up, "w_down": d_w_down}, dmrg, split=True)

    def f_merge_bwd(ids, dm, ga, gc, a, c):
        dm, a, c = dm.astype(F32), a.astype(F32), c.astype(F32)
        sa, sc_ = _sigmoid(ga), _sigmoid(gc)
        return dm * sa, dm * sc_, dm * a * sa * (1.0 - sa), dm * c * sc_ * (1.0 - sc_)

    dya, dyc, dp_ga, dp_gc = _ew(
        f_merge_bwd, (n_lt,), [(dmrg, lrows(D_MODEL)), (pp, lrows(D_MODEL, 0)), (pp, lrows(D_MODEL, 1)),
                               (ya, lrows(D_MODEL)), (yc, lrows(D_MODEL))], [lrow_out(D_MODEL, BF16)] * 4, "merge_bwd")
    dya = early_continue("late", dya)

    d_w_ao_p = _mm(o_pad, dya, "tn", 1024, D_MODEL, T, tm=1024, tn=D_MODEL, tk=tk_t, name="w_attn_out_dw", out_dtype=BF16)
    do_pad = _mm(dya, W["w_attn_out"], "nt", T, 1024, D_MODEL, tm=tm_lat, tn=1024, tk=D_MODEL, name="w_attn_out_dx")
    d_w_co = _mm(e, dyc, "tn", CONV_DIM, D_MODEL, T, tm=CONV_DIM, tn=256, tk=tk_t, name="w_conv_out_dw", out_dtype=BF16,
                 o_spec=pl.BlockSpec((None, CONV_DIM, 256), lambda i, j, k: (j, i, 0)), out_shape=(4, CONV_DIM, 256))
    de = _mm(dyc, W["w_conv_out"], "nt", T, CONV_DIM, D_MODEL, tm=tm_lat, tn=CONV_DIM, tk=256, name="w_conv_out_dx",
             b_spec=pl.BlockSpec((None, CONV_DIM, 256), lambda i, j, k: (k, j, 0)))

    def f_conv_bwd(ids, xin, cb, cc, d_e, w, b):
        z = cc * xin
        sz = _shifts(z)
        cz = _conv(z, w, b, sz)
        dcz = d_e * cb
        w0, w1, w2 = _conv_bwd_w(dcz, z, sz)
        dz = _conv_bwd_x(dcz, w)
        return dz * cc, d_e * cz, dz * xin, _colsum(dcz), w0, w1, w2

    cvec_c = ((1, CONV_DIM), F32, pl.BlockSpec((1, tc), lambda j: (0, j)), None)
    conv_b = _ew(f_conv_bwd, (CONV_DIM // tc,),
                 [(pp, colT(CX0 // tc)), (pp, colT(CB0 // tc)), (pp, colT(CC0 // tc)), (de, colT(0)),
                  (W["conv_w"], pl.BlockSpec((3, tc), lambda j: (0, j))), (W["conv_b"], pl.BlockSpec((1, tc), lambda j: (0, j)))],
                 [((T, CONV_DIM), BF16, colT(0), None)] * 3 + [cvec_c] * 4, "conv_bwd")
    dp_cx, dp_cb, dp_cc, d_conv_b = conv_b[:4]
    d_conv_w = jnp.concatenate(conv_b[4:7], axis=0)

    dq_raw, dkv, dp_kr = _attn_bwd(q_raw, kv, pp_a, o_pad, do_pad, tab, T, TT)

    tk_a = TT
    d_w_uq_t = _mm(nq, dq_raw, "tn", Q_RANK, 1024, T, tm=Q_RANK, tn=1024, tk=T, name="w_uq_dw", transpose_out=True)
    dnq = _mm(dq_raw, W["w_uq_t"], "nn", T, Q_RANK, 1024, tm=tm_lat, tn=Q_RANK, tk=1024, name="w_uq_dx")
    d_w_ukv = _mm(nkv, dkv, "tn", KV_RANK, 1024, TT, tm=KV_RANK, tn=256, tk=tk_a, name="w_ukv_dw", out_dtype=BF16,
                  o_spec=pl.BlockSpec((None, KV_RANK, 256), lambda i, j, k: (j, i, 0)), out_shape=(4, KV_RANK, 256))
    dnkv = _mm(dkv, W["w_ukv"], "nt", TT, KV_RANK, 1024, tm=tm_all, tn=KV_RANK, tk=256, name="w_ukv_dx",
               b_spec=pl.BlockSpec((None, KV_RANK, 256), lambda i, j, k: (k, j, 0)))
    dnkv = early_grads("mid", {
        "w_attn_out": jnp.transpose(d_w_ao_p.reshape(N_HEADS, HEAD_PAD, 4, 256)[:, 64:], (2, 0, 1, 3)).reshape(
            4, N_HEADS * 64, 256),
        "w_conv_out": d_w_co,
        "w_uq": d_w_uq_t.reshape(4, 2, HEAD_PAD, Q_RANK)[:, :, :QK_DIM].reshape(4, 2 * QK_DIM, Q_RANK).astype(BF16),
        "w_ukv": d_w_ukv}, dnkv)

    def f_lowrank_bwd(ids, ckv, cq, dkv_, dq_, gkv, gq, ga, gc, cx, cb, cc, kr):
        rk, rq = _rms(ckv), _rms(cq)
        nk, nq_ = ckv * rk, cq * rq
        lat = ids[0] < n_lat
        dq_ = jnp.where(lat, dq_, 0.0)
        pieces = [jnp.where(lat, a, jnp.zeros_like(a)) for a in (ga, gc, cx, cb, cc)]
        pieces += [_rms_bwd(dkv_ * gkv, nk, rk).astype(BF16), _rms_bwd(dq_ * gq, nq_, rq).astype(BF16), kr.astype(BF16)]
        return jnp.concatenate(pieces, axis=1), _colsum(dkv_ * nk), _colsum(dq_ * nq_)

    lat_rows = lambda n: pl.BlockSpec((ROW_TILE, n), lambda i: (jnp.minimum(i, n_lat - 1), 0))
    dpp, dg_kv, dg_q = _ew(
        f_lowrank_bwd, (n_all,), [(pp_a, _rows(KV_RANK, PA_KV0 // KV_RANK)), (pp_a, _rows(Q_RANK, PA_Q0 // Q_RANK)),
                                  (dnkv, _rows(KV_RANK)), (dnq, lat_rows(Q_RANK)), (W["kv_norm_g"], vec(KV_RANK)),
                                  (W["q_norm_g"], vec(Q_RANK)), (dp_ga, lat_rows(D_MODEL)), (dp_gc, lat_rows(D_MODEL)),
                                  (dp_cx, lat_rows(CONV_DIM)), (dp_cb, lat_rows(CONV_DIM)), (dp_cc, lat_rows(CONV_DIM)),
                                  (dp_kr, _rows(HEAD_PAD))],
        [row_out(P_COLS, BF16, TT), acc_out(KV_RANK), acc_out(Q_RANK)], "lowrank_norm_bwd")
    d_w_in_t = _mm(hh, dpp, "tn", D_MODEL, P_COLS, TT, tm=512, tn=2176, tk=TT, name="w_in_dw", out_dtype=BF16,
                   transpose_out=True)
    dhh = _mm(dpp, W["w_in_t"], "nn", TT, D_MODEL, P_COLS, tm=tm_all, tn=512, tk=2176, name="w_in_dx")

    def f_norm1_bwd(ids, x, dh, dres, g, sc):
        r = _rms(x)
        xn = x * r
        return (dres + _rms_bwd(dh * g * (1.0 + sc), xn, r), _colsum(dh), _colsum(dh * xn * g),
                _colsum(dh * xn * (1.0 + sc)))

    grad_x, dsh1, dsc1, dg_n1 = _ew(
        f_norm1_bwd, (n_lt,), [(xx, lrows(D_MODEL)), (dhh, lrows(D_MODEL)), (dx1, lrows(D_MODEL)),
                               (W["norm1_g"], vec(D_MODEL)), (sc1, vec(D_MODEL))],
        [lrow_out(D_MODEL, F32)] + [acc_out(D_MODEL)] * 3, "norm1_bwd")

    def f_norm1_ctx_bwd(ids, x, dh, g, sc):
        xn = x * _rms(x)
        return _colsum(dh), _colsum(dh * xn * g), _colsum(dh * xn * (1.0 + sc))

    n_ctx = n_all - n_lat
    dcsh1, dcsc1, dg_n1c = _ew(
        f_norm1_ctx_bwd, (n_ctx,), [(xx, _rows(D_MODEL, 0, n_lat)), (dhh, _rows(D_MODEL, 0, n_lat)),
                                    (W["norm1_g"], vec(D_MODEL)), (csc1, vec(D_MODEL))], [acc_out(D_MODEL)] * 3,
        "norm1_ctx_bwd")

    big = {"w_in": _w_in_t_shards_from_p(d_w_in_t).astype(BF16)}
    zero = jnp.zeros((1, 4 * D_MODEL), F32)
    small = {
        "dmod_lat": jnp.concatenate([dsh1, dsc1, dg1, dsh2, dsc2, dg2], axis=1),
        "dmod_ctx": jnp.concatenate([dcsh1, dcsc1, zero], axis=1),
        "norm1_g": dg_n1 + dg_n1c, "norm2_g": dg_n2, "final_g": dg_f, "q_norm_g": dg_q, "kv_norm_g": dg_kv,
        "conv_b": d_conv_b, "conv_w": d_conv_w.reshape(1, -1), "ffn_conv_b": d_ffn_conv_b,
        "ffn_conv_w": d_ffn_conv_w.reshape(1, -1),
    }
    return grad_x, loss, big, small


SMALL = (("dmod_lat", 6144), ("dmod_ctx", 6144), ("norm1_g", 1024), ("norm2_g", 1024), ("final_g", 1024),
         ("q_norm_g", 384), ("kv_norm_g", 256), ("conv_b", 512), ("conv_w", 1536), ("ffn_conv_b", 5632),
         ("ffn_conv_w", 16896), ("loss", 128))
SMALL_ROWS = 320


def _adam_update(w, g, m, v):
    c1, c2 = 1.0 - ADAM_B1 ** ADAM_STEP, 1.0 - ADAM_B2 ** ADAM_STEP
    m2 = ADAM_B1 * m + (1.0 - ADAM_B1) * g
    v2 = ADAM_B2 * v + (1.0 - ADAM_B2) * (g * g)
    return [-ADAM_LR * ((m2 / c1) / (jnp.sqrt(v2 / c2) + ADAM_EPS) + ADAM_WD * w), m2, v2]


def _adamw(w, g, m, v, name):
    R, C = w.shape
    tr = 8 if R % 8 == 0 else R
    for t in range(8, R + 1, 8):
        if R % t == 0 and t * C * 4 <= (1 << 21):
            tr = t
    spec = pl.BlockSpec((tr, C), lambda i: (i, 0))
    return _ew(lambda ids, *vals: [vals[1]] + _adam_update(*vals), (R // tr,),
               [(w, spec), (g, spec), (m, spec), (v, spec)], [((R, C), F32, spec, None)] * 4, name)


def kernel(x, c, ctx, c_ctx, w_ada, b_ada, norm1_g, w_in, q_norm_g, kv_norm_g, w_uq, w_ukv, conv_w, conv_b, w_attn_out, w_conv_out, w_o, norm2_g, w_up, ffn_conv_w, ffn_conv_b, w_down, final_g, loss_target, m_c_ctx, m_w_ada, m_b_ada, m_norm1_g, m_w_in, m_q_norm_g, m_kv_norm_g, m_w_uq, m_w_ukv, m_conv_w, m_conv_b, m_w_attn_out, m_w_conv_out, m_w_o, m_norm2_g, m_w_up, m_ffn_conv_w, m_ffn_conv_b, m_w_down, m_final_g, v_c_ctx, v_w_ada, v_b_ada, v_norm1_g, v_w_in, v_q_norm_g, v_kv_norm_g, v_w_uq, v_w_ukv, v_conv_w, v_conv_b, v_w_attn_out, v_w_conv_out, v_w_o, v_norm2_g, v_w_up, v_ffn_conv_w, v_ffn_conv_b, v_w_down, v_final_g):
    mx, my, mc = lax.axis_index("x"), lax.axis_index("y"), lax.axis_index("c")
    chip = 2 * mx + my
    dev = 4 * mx + 2 * my + mc
    T, Tc = x.shape[1], ctx.shape[1]
    TT = T + Tc
    w_in_t, m_w_in_t, v_w_in_t = (jnp.transpose(a[0]) for a in (w_in, m_w_in, v_w_in))
    w_uq_t, m_w_uq_t, v_w_uq_t = (jnp.transpose(a[0]) for a in (w_uq, m_w_uq, v_w_uq))
    conv_sh = jnp.concatenate([conv_w[0], ffn_conv_w[0]], axis=1)
    pay1 = jnp.concatenate([jnp.pad(c, ((0, 7), (0, 0))), jnp.pad(conv_sh, ((0, 5), (0, 0)))], axis=1)
    c_send, c_recv, c_src, c_land, zero0 = _ici_start("all", [pay1], [(8, 8, 2560)], jnp.zeros((8, 128), F32),
                                                      "cond_start")
    w_in_bf = (jnp.pad(w_in_t, ((0, W_IN_SHARD_PAD - W_IN_SHARD), (0, 0))) + zero0[0, 0]).astype(BF16)
    shards = {"w_in_a": w_in_bf[:W_IN_EARLY], "w_in_b": w_in_bf[W_IN_EARLY:], "w_uq": w_uq_t, "w_ukv": w_ukv[0],
              "w_attn_out": w_attn_out[0], "w_conv_out": w_conv_out[0], "w_o": w_o[0], "w_up": w_up[0],
              "w_down": w_down[0]}
    first = ["w_in_a", "w_uq", "w_ukv"]
    bf0 = [(shards[n] + zero0[0, 0].astype(shards[n].dtype)).astype(BF16) for n in first]
    flight0 = _ici_start("gather", bf0, [(4,) + s.shape for s in bf0], jnp.zeros((8, 128), F32), "gather_g0_start")
    (pay1,), (c_land,) = _ici_wait("all", c_send, c_recv, c_src, c_land, flight0[4], "cond_wait")
    got1 = lax.dynamic_update_slice(c_land, pay1[None], (dev, 0, 0))
    c_all = got1[:, 0, :D_MODEL]
    conv_all = got1[0::2, :3, D_MODEL:]
    conv_w_full = _cols_from_shards(conv_all[:, :, :128])
    ffn_conv_w_full = _cols_from_shards(conv_all[:, :, 128:])

    cond = jnp.concatenate([c_all, c_ctx.reshape(1, D_MODEL), jnp.zeros((7, D_MODEL), F32)], axis=0)

    def f_silu(ids, v):
        return (v * _sigmoid(v),)

    (s16,) = _ew(f_silu, (1,), [(cond, _full((16, D_MODEL)))], [((16, D_MODEL), F32, _full((16, D_MODEL)), None)], "silu_cond")
    mod_sh = _mm(s16, w_ada[0], "nn", 16, 1536, D_MODEL, tm=16, tn=768, tk=D_MODEL, name="w_ada_fwd")
    m_send, m_recv, m_src, m_land, zero1 = _ici_start("all", [mod_sh], [(8, 16, 1536)], jnp.zeros((8, 128), F32),
                                                      "mod_start")
    late_groups = {"g1": ("w_in_b", "w_attn_out", "w_conv_out", "w_o"), "g2": ("w_up", "w_down")}
    late_bf = {n: (shards[n] + zero1[0, 0].astype(shards[n].dtype)).astype(BF16) for g in late_groups.values() for n in g}
    casts_done = jnp.zeros((8, 128), F32) + sum(b[0, 0].astype(F32) for b in late_bf.values())
    g_send, g_recv, g_src, g_land, _ = flight0
    g_src, g_land = _ici_wait("gather", g_send, g_recv, g_src, g_land, casts_done, "gather_g0_wait")
    f_send, f_recv, f_src, f_land, zero = _ici_start("finish", g_src, None, zero1, "finish_g0_start", lands=g_land)
    gathered = _ici_wait("finish", f_send, f_recv, f_src, f_land, zero, "finish_g0_wait")[1]
    full = dict(zip(first, gathered))
    (mod_mine,), (m_land,) = _ici_wait("all", m_send, m_recv, m_src, m_land, gathered[0], "mod_wait")
    got2 = lax.dynamic_update_slice(m_land, mod_mine[None], (dev, 0, 0))
    mod_all = _cols_from_shards(got2[0::2]) + b_ada
    mod_lat = lax.dynamic_slice_in_dim(mod_all, dev, 1, axis=0)
    mod_ctx = mod_all[8:9]
    xx = jnp.concatenate([x[0], ctx[0]], axis=0) + zero[0, 0]
    flight = {}
    for tag, group in late_groups.items():
        bf = [late_bf[n] for n in group]
        flight[tag] = _ici_start("gather", bf, [(4,) + s.shape for s in bf], xx, "gather_" + tag + "_start")
        xx = flight[tag][4]

    def chip_stage_done(tag, x):
        send, recv, src, land, _ = flight[tag]
        src, land = _ici_wait("gather", send, recv, src, land, x, "gather_" + tag + "_wait")
        flight[tag] = _ici_start("finish", src, None, x, "finish_" + tag + "_start", lands=land)
        return flight[tag][4]

    def arrived(tag, x):
        send, recv, src, land, _ = flight[tag]
        return dict(zip(late_groups[tag], _ici_wait("finish", send, recv, src, land, x, "finish_" + tag + "_wait")[1]))

    def late_weights(point, x):
        if point == "before_attn":
            return {}, chip_stage_done("g1", x)
        if point == "after_attn":
            got = arrived("g1", x)
            wao = _cols_from_shards(got["w_attn_out"]).reshape(N_HEADS, 64, D_MODEL)
            w_in_all = jnp.concatenate([full["w_in_a"], got["w_in_b"]], axis=1)
            ready = {"w_in_t": _w_in_t_p_from_shards(w_in_all),
                     "w_attn_out": jnp.pad(wao, ((0, 0), (64, 0), (0, 0))).reshape(N_HEADS * HEAD_PAD, D_MODEL),
                     "w_conv_out": got["w_conv_out"], "w_o": got["w_o"].reshape(D_MODEL, D_MODEL)}
            return ready, chip_stage_done("g2", x)
        got = arrived("g2", x)
        return {"w_up": got["w_up"], "w_down": got["w_down"].reshape(D_FF, D_MODEL)}, x

    wuq_t = full["w_uq"].reshape(N_HEADS, QK_DIM, Q_RANK)
    early_rows = full["w_in_a"][0]
    zrows = lambda n: jnp.zeros((n, D_MODEL), BF16)
    W = {
        "w_in_a_t": jnp.concatenate([early_rows[0:256], zrows(PA_Q0 - 256), early_rows[288:672], zrows(64),
                                     early_rows[256:288], zrows(32)], axis=0),
        "w_uq_t": jnp.pad(wuq_t, ((0, 0), (0, HEAD_PAD - QK_DIM), (0, 0))).reshape(N_HEADS * HEAD_PAD, Q_RANK),
        "w_ukv": full["w_ukv"],
        "norm1_g": norm1_g, "norm2_g": norm2_g, "final_g": final_g.reshape(1, D_MODEL), "q_norm_g": q_norm_g,
        "kv_norm_g": kv_norm_g, "conv_w": conv_w_full, "conv_b": conv_b, "ffn_conv_w": ffn_conv_w_full,
        "ffn_conv_b": ffn_conv_b,
    }

    place = jnp.stack([chip, mc]).astype(jnp.int32)
    early = {}

    pending = {}

    def scatter(tag, group, gs, from_sib, carry):
        if tag == "mid":
            sums = _add_pair_many(gs, from_sib, place, "rs_pair_add_mid")
        else:
            sums = [_add_pair(gs[w], from_sib[w], place, "rs_pair_add_" + n) for w, n in enumerate(group)]
        send, recv, sums, land, carry = _ici_start(
            "scatter", sums, [(3,) + s.shape[1:] for s in sums], carry, "rs_chips_" + tag + "_start")
        early[tag] = (group, send, recv, sums, land)
        return carry

    def early_grads(tag, g, carry, split=False):
        gs = list(g.values())
        if not split:
            return scatter(tag, list(g), gs, _rs_pair(gs, "rs_pair_" + tag), carry)
        send, recv, gs, land, carry = _ici_start(
            "pair", gs, [(4, s.shape[1] // 2, s.shape[2]) for s in gs], carry, "rs_pair_" + tag + "_start")
        pending[tag] = (list(g), send, recv, gs, land)
        return carry

    def early_continue(tag, carry):
        group, send, recv, gs, land = pending[tag]
        gs, from_sib = _ici_wait("pair", send, recv, gs, land, carry, "rs_pair_" + tag + "_wait")
        return scatter(tag, group, gs, from_sib, carry)

    grad_x, loss_part, gbig, gsmall = _local_step(xx, loss_target[0], mod_lat, mod_ctx, W, late_weights, early_grads,
                                                  early_continue)

    gsmall["loss"] = loss_part
    pay3 = jnp.concatenate([gsmall[n].reshape(-1) for n, _ in SMALL])
    pay3 = jnp.pad(pay3, (0, SMALL_ROWS * 128 - pay3.shape[0])).reshape(SMALL_ROWS, 128)
    s_send, s_recv, s_src, s_land, w_in_thru = _ici_start("all", [pay3], [(8, SMALL_ROWS, 128)], gbig["w_in"],
                                                         "small_start")
    gbig = {"w_in": w_in_thru}

    after_small = early_grads("last", gbig, s_src[0])

    (pay3,), (s_land,) = _ici_wait("all", s_send, s_recv, [after_small], s_land, early["last"][3][0], "small_wait")
    got3 = lax.dynamic_update_slice(s_land, pay3[None], (dev, 0, 0)).reshape(8 * SMALL_ROWS, 128)

    def f_sum8(ids, a):
        s = a[0:SMALL_ROWS]
        for d in range(1, 8):
            s = s + a[d * SMALL_ROWS:(d + 1) * SMALL_ROWS]
        return (s,)

    (vsum,) = _ew(f_sum8, (1,), [(got3, _full((8 * SMALL_ROWS, 128)))],
                  [((SMALL_ROWS, 128), F32, _full((SMALL_ROWS, 128)), None)], "sum_small")
    vflat = vsum.reshape(-1)
    gvec, off = {}, 0
    for n, size in SMALL:
        gvec[n] = vflat[off:off + size]
        off += size
    loss = gvec["loss"][0]
    dmod_rows = got3.reshape(8, SMALL_ROWS * 128)[:, :6 * D_MODEL]
    dm16 = jnp.concatenate([dmod_rows, gvec["dmod_ctx"].reshape(1, -1), jnp.zeros((7, 6 * D_MODEL), F32)], axis=0)

    def f_colsum(ids, a):
        return (_colsum(a),)

    (g_b_ada,) = _ew(f_colsum, (1,), [(dm16, _full((16, 6 * D_MODEL)))],
                     [((1, 6 * D_MODEL), F32, _full((1, 6 * D_MODEL)), None)], "b_ada_grad")
    dm_sh = lax.dynamic_slice_in_dim(dm16, chip * 1536, 1536, axis=1)
    g_w_ada = _mm(s16, dm_sh, "tn", D_MODEL, 1536, 16, tm=512, tn=768, tk=16, name="w_ada_dw")
    dcond_part = _mm(dm_sh, w_ada[0], "nt", 16, D_MODEL, 1536, tm=16, tn=512, tk=1536, name="w_ada_dx")
    d_send, d_recv, d_src, d_land, vsum = _ici_start("all", [dcond_part[8:16]], [(8, 8, D_MODEL)], vsum, "dcond_start")

    def finish_start(tags, after):
        done, halves = [], []
        for tag in tags:
            tag_names, send, recv, sums, land = early[tag]
            sums, land = _ici_wait("scatter", send, recv, sums, land, after, "rs_chips_" + tag + "_wait")
            done += tag_names
            if tag == "mid":
                halves += _add_chips_many(sums, land, place, "rs_chip_add_mid")
            else:
                halves += [_add_chips(a, b, place, "rs_chip_add_" + n) for a, b, n in zip(sums, land, tag_names)]
        send, recv, _, halves, _ = _ici_start("back", [], None, jnp.zeros((8, 128), F32), "rs_back_" + tags[0] + "_start",
                                              lands=halves)
        return done, send, recv, halves

    def finish_wait(state, after):
        done, send, recv, halves = state
        return dict(zip(done, _ici_wait("back", send, recv, [], halves, after, "rs_back_" + done[0] + "_wait")[1]))

    grads, deltas, new_m, new_v = {}, {}, {}, {}

    raw = {}

    def adam(n, w_, m_, v_, g, transposed):
        g_out, d_, m2, v2 = _adamw(w_, g, m_, v_, "adamw_" + n)
        raw[n] = d_
        back = (lambda a: jnp.transpose(a)[None]) if transposed else (lambda a: a[None])
        grads[n], deltas[n], new_m[n], new_v[n] = back(g_out), back(d_), back(m2), back(v2)

    pending_back = finish_start(["late", "mid"], grad_x)
    adam("w_ada", w_ada[0], m_w_ada[0], v_w_ada[0], g_w_ada, False)
    gw = finish_wait(pending_back, raw["w_ada"])
    for n, (w_, m_, v_) in {"w_o": (w_o, m_w_o, v_w_o), "w_up": (w_up, m_w_up, v_w_up),
                            "w_down": (w_down, m_w_down, v_w_down)}.items():
        adam(n, w_[0], m_[0], v_[0], gw[n], False)
    pending_back = finish_start(["last"], raw["w_up"])

    (dcond_mine,), (d_land,) = _ici_wait("all", d_send, d_recv, d_src, d_land, raw["w_down"], "dcond_wait")
    got4 = lax.dynamic_update_slice(d_land, dcond_mine[None], (dev, 0, 0))[0::2, 0]

    def f_c_ctx(ids, parts, cc):
        s = _sigmoid(cc)
        d = parts[0:1] + parts[1:2] + parts[2:3] + parts[3:4]
        return (d * s * (1.0 + cc * (1.0 - s)),)

    (g_c_ctx,) = _ew(f_c_ctx, (1,), [(got4, _full((4, D_MODEL))), (c_ctx.reshape(1, D_MODEL), _full((1, D_MODEL)))],
                     [((1, D_MODEL), F32, _full((1, D_MODEL)), None)], "c_ctx_grad")

    conv_w_g = lax.dynamic_slice_in_dim(gvec["conv_w"].reshape(3, CONV_DIM), chip * 128, 128, axis=1)
    ffn_conv_w_g = lax.dynamic_slice_in_dim(gvec["ffn_conv_w"].reshape(3, 2 * D_FF), chip * 1408, 1408, axis=1)
    vec_params = (("c_ctx", c_ctx, m_c_ctx, v_c_ctx, g_c_ctx), ("b_ada", b_ada, m_b_ada, v_b_ada, g_b_ada),
                  ("norm1_g", norm1_g, m_norm1_g, v_norm1_g, gvec["norm1_g"]),
                  ("q_norm_g", q_norm_g, m_q_norm_g, v_q_norm_g, gvec["q_norm_g"]),
                  ("kv_norm_g", kv_norm_g, m_kv_norm_g, v_kv_norm_g, gvec["kv_norm_g"]),
                  ("conv_w", conv_w, m_conv_w, v_conv_w, conv_w_g), ("conv_b", conv_b, m_conv_b, v_conv_b, gvec["conv_b"]),
                  ("norm2_g", norm2_g, m_norm2_g, v_norm2_g, gvec["norm2_g"]),
                  ("ffn_conv_w", ffn_conv_w, m_ffn_conv_w, v_ffn_conv_w, ffn_conv_w_g),
                  ("ffn_conv_b", ffn_conv_b, m_ffn_conv_b, v_ffn_conv_b, gvec["ffn_conv_b"]),
                  ("final_g", final_g, m_final_g, v_final_g, gvec["final_g"]))
    two_d = lambda a: a.reshape((-1, a.shape[-1]))
    many = [p + ((lambda r, s=p[1].shape: r.reshape(s)),) for p in vec_params]
    for n, w_, m_, v_ in (("w_ukv", w_ukv, m_w_ukv, v_w_ukv), ("w_attn_out", w_attn_out, m_w_attn_out, v_w_attn_out),
                          ("w_conv_out", w_conv_out, m_w_conv_out, v_w_conv_out)):
        many.append((n, w_, m_, v_, gw[n], (lambda r, s=w_.shape: r.reshape(s))))
    many.append(("w_uq", w_uq_t, m_w_uq_t, v_w_uq_t, gw["w_uq"], lambda r: jnp.transpose(r)[None]))

    def f_adam_many(ids, *vals):
        out = []
        for k in range(len(many)):
            out += [vals[4 * k + 1]] + _adam_update(*vals[4 * k:4 * k + 4])
        return out

    ins_v, outs_v = [], []
    for p in many:
        shp = two_d(p[1]).shape
        ins_v += [(two_d(a), _full(shp)) for a in (p[1], p[4], p[2], p[3])]
        outs_v += [(shp, F32, _full(shp), None)] * 4
    res_v = _ew(f_adam_many, (1,), ins_v, outs_v, "adamw_small")
    for k, p in enumerate(many):
        n, post = p[0], p[5]
        grads[n], deltas[n], new_m[n], new_v[n] = (post(r) for r in res_v[4 * k:4 * k + 4])

    gw_in = finish_wait(pending_back, res_v[0])
    adam("w_in", w_in_t, m_w_in_t, v_w_in_t, gw_in["w_in"], True)

    order = ("c_ctx", "w_ada", "b_ada", "norm1_g", "w_in", "q_norm_g", "kv_norm_g", "w_uq", "w_ukv", "conv_w", "conv_b",
             "w_attn_out", "w_conv_out", "w_o", "norm2_g", "w_up", "ffn_conv_w", "ffn_conv_b", "w_down", "final_g")
    return (loss, grad_x[None], *[grads[n] for n in order], *[deltas[n] for n in order],
            *[new_m[n] for n in order], *[new_v[n] for n in order])
```

```python
import functools

import jax
import jax.numpy as jnp
import numpy as np
from jax import lax
from jax.experimental import pallas as pl
from jax.experimental.pallas import tpu as pltpu

F32, BF16 = jnp.float32, jnp.bfloat16
MESH = pl.DeviceIdType.MESH

D_MODEL = 1024
N_HEADS = 8
HEAD_PAD = 128
QK_DIM = 96
Q_RANK, KV_RANK = 384, 256
CONV_DIM = 512
D_FF = 2816
GRID_W = 64
ROPE_THETA = 10000.0
EPS = 1e-6
GA0, GC0, CX0, CB0, CC0, KV0, Q0, KR0, P_COLS = 0, 1024, 2048, 2560, 3072, 3584, 3840, 4224, 4352
PA_KV0, PA_Q0, PA_KR0, PA_COLS = 0, 384, 768, 896
ROW_TILE = 256
VMEM_LIMIT_BYTES = 48 * 1024 * 1024

ADAM_LR, ADAM_B1, ADAM_B2, ADAM_EPS, ADAM_WD, ADAM_STEP = 0.001, 0.9, 0.999, 1e-08, 0.01, 10

NN = (((1,), (0,)), ((), ()))
NT = (((1,), (1,)), ((), ()))
TN = (((0,), (0,)), ((), ()))


def _cp(sem):
    return pltpu.CompilerParams(dimension_semantics=sem, vmem_limit_bytes=VMEM_LIMIT_BYTES)


PIN_BYTES = 1 << 19


def _in_hbm(arrays):
    return [pltpu.with_memory_space_constraint(a, pltpu.HBM) if a.size * a.dtype.itemsize >= PIN_BYTES else a
            for a in arrays]


def _out(shape, dtype):
    n = 1
    for d in shape:
        n *= d
    big = n * jnp.dtype(dtype).itemsize >= PIN_BYTES
    return pltpu.HBM(shape, dtype) if big else jax.ShapeDtypeStruct(shape, dtype)


def _pick(n, prefs):
    for p in prefs:
        if n % p == 0:
            return p
    return n


def _mm(a, b, mode, M, N, K, *, tm, tn, tk, name, out_dtype=F32, a_spec=None, b_spec=None, o_spec=None,
        out_shape=None, transpose_out=False):
    assert M % tm == 0 and N % tn == 0 and K % tk == 0, (name, M, N, K, tm, tn, tk)
    nk = K // tk
    dims = {"nn": NN, "nt": NT, "tn": TN}[mode]
    if a_spec is None:
        a_spec = (pl.BlockSpec((tk, tm), lambda i, j, k: (k, i)) if mode == "tn"
                  else pl.BlockSpec((tm, tk), lambda i, j, k: (i, k)))
    if b_spec is None:
        b_spec = (pl.BlockSpec((tn, tk), lambda i, j, k: (j, k)) if mode == "nt"
                  else pl.BlockSpec((tk, tn), lambda i, j, k: (k, j)))
    if o_spec is None:
        o_spec = (pl.BlockSpec((tn, tm), lambda i, j, k: (j, i)) if transpose_out
                  else pl.BlockSpec((tm, tn), lambda i, j, k: (i, j)))
    if out_shape is None:
        out_shape = (N, M) if transpose_out else (M, N)

    def emit(o_ref, val):
        o_ref[...] = (val.T if transpose_out else val).astype(o_ref.dtype)

    def body(a_ref, b_ref, o_ref, *scratch):
        part = lax.dot_general(a_ref[...].astype(BF16), b_ref[...].astype(BF16), dims, preferred_element_type=F32)
        if nk == 1:
            emit(o_ref, part)
            return
        acc_ref, = scratch
        k = pl.program_id(2)

        @pl.when(k == 0)
        def _():
            acc_ref[...] = part

        @pl.when((k > 0) & (k < nk - 1))
        def _():
            acc_ref[...] += part

        @pl.when(k == nk - 1)
        def _():
            emit(o_ref, acc_ref[...] + part)

    return pl.pallas_call(
        body, grid=(M // tm, N // tn, nk), in_specs=[a_spec, b_spec], out_specs=o_spec,
        out_shape=_out(out_shape, out_dtype),
        scratch_shapes=[pltpu.VMEM((tm, tn), F32)] if nk > 1 else [],
        compiler_params=_cp(("parallel", "parallel", "arbitrary")), name=name)(*_in_hbm([a, b]))


def _ew(fn, grid, ins, outs, name, scalars=None):
    n_in = len(ins)
    n_sc = 0 if scalars is None else 1

    def store(ref, val, acc, ids):
        if isinstance(val, (list, tuple)):
            for h, v in enumerate(val):
                ref[h] = v.astype(ref.dtype)
            return
        if acc is None:
            ref[...] = val.astype(ref.dtype)
            return

        @pl.when(ids[acc] == 0)
        def _():
            ref[...] = val.astype(ref.dtype)

        @pl.when(ids[acc] > 0)
        def _():
            ref[...] += val.astype(ref.dtype)

    def body(*refs):
        refs = refs[n_sc:]
        ids = tuple(pl.program_id(a) for a in range(len(grid)))
        vals = fn(ids, *[r[...] for r in refs[:n_in]])
        for ref, val, (_, _, _, acc) in zip(refs[n_in:], vals, outs):
            store(ref, val, acc, ids)

    acc_axes = {o[3] for o in outs if o[3] is not None}
    sem = tuple("arbitrary" if a in acc_axes else "parallel" for a in range(len(grid)))
    in_specs, out_specs = [s for _, s in ins], [o[2] for o in outs]
    out_shape = [_out(o[0], o[1]) for o in outs]
    args = _in_hbm([a for a, _ in ins])
    if scalars is None:
        return pl.pallas_call(body, grid=grid, in_specs=in_specs, out_specs=out_specs, out_shape=out_shape,
                              compiler_params=_cp(sem), name=name)(*args)
    spec = pltpu.PrefetchScalarGridSpec(num_scalar_prefetch=1, grid=grid, in_specs=in_specs, out_specs=out_specs)
    return pl.pallas_call(body, grid_spec=spec, out_shape=out_shape, compiler_params=_cp(sem), name=name)(scalars, *args)


def _rows(width, cblk=0, roff=0, tr=ROW_TILE):
    return pl.BlockSpec((tr, width), lambda i: (i + roff, cblk))


def _full(shape):
    nd = len(shape)
    return pl.BlockSpec(shape, lambda *_: (0,) * nd)


def _sigmoid(x):
    return 1.0 / (1.0 + jnp.exp2(x * (-1.4426950408889634)))


def _rms(x):
    return lax.rsqrt(jnp.mean(x * x, axis=-1, keepdims=True) + EPS)


def _rms_bwd(dn, xn, r):
    return r * (dn - xn * jnp.mean(dn * xn, axis=-1, keepdims=True))


def _colsum(x):
    return jnp.sum(x, axis=0, keepdims=True)


def _shifts(x):
    n = x.shape[0]
    rows = lax.broadcasted_iota(jnp.int32, x.shape, 0)
    return jnp.where(rows == 0, 0.0, pltpu.roll(x, 1, 0)), jnp.where(rows == n - 1, 0.0, pltpu.roll(x, n - 1, 0))


def _conv(x, w, b, shifted=None):
    prev, nxt = _shifts(x) if shifted is None else shifted
    return b + prev * w[0:1] + x * w[1:2] + nxt * w[2:3]


def _conv_bwd_x(dy, w):
    prev, nxt = _shifts(dy)
    return nxt * w[0:1] + dy * w[1:2] + prev * w[2:3]


def _conv_bwd_w(dy, x, shifted):
    prev, nxt = shifted
    return _colsum(dy * prev), _colsum(dy * x), _colsum(dy * nxt)


def _rope(x, cos, sin_lo, sin_hi):
    return x * cos + pltpu.roll(x, HEAD_PAD - 8, 1) * sin_lo + pltpu.roll(x, 8, 1) * sin_hi


ATTN_SCALE = QK_DIM ** -0.5
LOG2_E = 1.4426950408889634


def _rope_t(x, tab, inverse=False):
    o = 3 * HEAD_PAD if inverse else 0
    return _rope(x, tab[:, o:o + HEAD_PAD], tab[:, o + HEAD_PAD:o + 2 * HEAD_PAD], tab[:, o + 2 * HEAD_PAD:o + 3 * HEAD_PAD])


def _heads_keys(hp, kv_ref, kr_ref, tab_ref, kc_ref, vp_ref):
    kr_roped = _rope_t(kr_ref[...], tab_ref[...])
    lane = lax.broadcasted_iota(jnp.int32, kr_roped.shape, 1)
    for u in range(hp):
        kv = kv_ref[:, u * HEAD_PAD:(u + 1) * HEAD_PAD]
        kc_ref[u] = jnp.where(lane < 64, kv, kr_roped).astype(BF16)
        vp_ref[u] = jnp.where(lane >= 64, kv, 0.0).astype(BF16)


ATTN_Q_TILE = 512
ATTN_HEADS_PER_STEP = 2


def _attn_specs(tq, TT):
    q = pl.BlockSpec((tq, HEAD_PAD), lambda h, i: (i, h))
    keys = pl.BlockSpec((TT, HEAD_PAD), lambda h, i: (0, h))
    kr = pl.BlockSpec((TT, HEAD_PAD), lambda h, i: (0, PA_KR0 // HEAD_PAD))
    tab_q = pl.BlockSpec((tq, 6 * HEAD_PAD), lambda h, i: (i, 0))
    tab_k = pl.BlockSpec((TT, 6 * HEAD_PAD), lambda h, i: (0, 0))
    return q, keys, kr, tab_q, tab_k


def _attn_fwd(q_raw, kv, pp, tab, T, TT):
    tq, hp = ROW_TILE, 2 * ATTN_HEADS_PER_STEP
    w = hp * HEAD_PAD

    def body(q_ref, kv_ref, kr_ref, tq_ref, tk_ref, o_ref, kc, vp):
        @pl.when(pl.program_id(1) == 0)
        def _():
            _heads_keys(hp, kv_ref, kr_ref, tk_ref, kc, vp)

        tab = tq_ref[...]
        for u in range(hp):
            cols = slice(u * HEAD_PAD, (u + 1) * HEAD_PAD)
            q = _rope_t(q_ref[:, cols], tab).astype(BF16)
            s = lax.dot_general(q, kc[u], NT, preferred_element_type=F32)
            m = jnp.max(s, axis=-1, keepdims=True)
            p = jnp.exp2((s - m) * (ATTN_SCALE * LOG2_E))
            l = jnp.sum(p, axis=-1, keepdims=True)
            o = lax.dot_general(p.astype(BF16), vp[u], NN, preferred_element_type=F32)
            lane = lax.broadcasted_iota(jnp.int32, o.shape, 1)
            o_ref[:, cols] = jnp.where(lane < 64, m * ATTN_SCALE + jnp.log(l), o / l)

    _, _, kr, _, _ = _attn_specs(tq, TT)
    qs = pl.BlockSpec((tq, w), lambda h, i: (i, h))
    keys = pl.BlockSpec((TT, w), lambda h, i: (0, h))
    tab_q = pl.BlockSpec((tq, 3 * HEAD_PAD), lambda h, i: (i, 0))
    tab_k = pl.BlockSpec((TT, 3 * HEAD_PAD), lambda h, i: (0, 0))
    return pl.pallas_call(
        body, grid=(N_HEADS // hp, T // tq), in_specs=[qs, keys, kr, tab_q, tab_k], out_specs=qs,
        out_shape=jax.ShapeDtypeStruct((T, N_HEADS * HEAD_PAD), F32),
        scratch_shapes=[pltpu.VMEM((hp, TT, HEAD_PAD), BF16), pltpu.VMEM((hp, TT, HEAD_PAD), BF16)],
        compiler_params=_cp(("parallel", "arbitrary")), name="attn_fwd",
    )(*_in_hbm([q_raw, kv, pp, tab, tab]))


def _attn_bwd(q_raw, kv, pp, o, do, tab, T, TT):
    tq = _pick(T, (ATTN_Q_TILE, ROW_TILE))
    nq = T // tq
    hp = ATTN_HEADS_PER_STEP
    w = hp * HEAD_PAD

    def body(q_ref, kv_ref, kr_ref, tq_ref, tk_ref, o_ref, do_ref, dq_ref, dkv_ref, dkr_ref, kc, vp, dk, dv):
        g, i = pl.program_id(0), pl.program_id(1)

        @pl.when(i == 0)
        def _():
            _heads_keys(hp, kv_ref, kr_ref, tk_ref, kc, vp)
            dk[...] = jnp.zeros_like(dk)
            dv[...] = jnp.zeros_like(dv)

        tab = tq_ref[...]
        for u in range(hp):
            cols = slice(u * HEAD_PAD, (u + 1) * HEAD_PAD)
            q = _rope_t(q_ref[:, cols], tab).astype(BF16)
            k, v, d_o = kc[u], vp[u], do_ref[:, cols]
            s = lax.dot_general(q, k, NT, preferred_element_type=F32)
            o = o_ref[:, cols]
            p = jnp.exp2(s * (ATTN_SCALE * LOG2_E) - o[:, 0:1] * LOG2_E)
            dob = d_o.astype(BF16)
            dp = lax.dot_general(dob, v, NT, preferred_element_type=F32)
            dd = jnp.sum(d_o * o, axis=-1, keepdims=True)
            ds = (p * (dp - dd) * ATTN_SCALE).astype(BF16)
            dq = lax.dot_general(ds, k, NN, preferred_element_type=F32)
            dq_ref[:, cols] = _rope_t(dq, tab, inverse=True).astype(dq_ref.dtype)
            dk[u] += lax.dot_general(q, ds, TN, preferred_element_type=F32)
            dv[u] += lax.dot_general(dob, p.astype(BF16), TN, preferred_element_type=F32)

        @pl.when(i == nq - 1)
        def _():
            rot = None
            for u in range(hp):
                dkh = dk[u].T
                lane = lax.broadcasted_iota(jnp.int32, dkh.shape, 1)
                dkv_ref[:, u * HEAD_PAD:(u + 1) * HEAD_PAD] = jnp.where(lane < 64, dkh, dv[u].T).astype(dkv_ref.dtype)
                part = jnp.where((lane >= 64) & (lane < 96), dkh, 0.0)
                rot = part if rot is None else rot + part
            rot = _rope_t(rot, tk_ref[...], inverse=True)

            @pl.when(g == 0)
            def _():
                dkr_ref[...] = rot

            @pl.when(g > 0)
            def _():
                dkr_ref[...] += rot

    _, _, kr, tab_q, tab_k = _attn_specs(tq, TT)
    qs = pl.BlockSpec((tq, w), lambda h, i: (i, h))
    keys = pl.BlockSpec((TT, w), lambda h, i: (0, h))
    wide = lambda rows: jax.ShapeDtypeStruct((rows, N_HEADS * HEAD_PAD), BF16)
    return pl.pallas_call(
        body, grid=(N_HEADS // hp, nq),
        in_specs=[qs, keys, kr, tab_q, tab_k, qs, qs],
        out_specs=[qs, keys, pl.BlockSpec((TT, HEAD_PAD), lambda h, i: (0, 0))],
        out_shape=[wide(T), wide(TT), jax.ShapeDtypeStruct((TT, HEAD_PAD), F32)],
        scratch_shapes=[pltpu.VMEM((hp, TT, HEAD_PAD), BF16), pltpu.VMEM((hp, TT, HEAD_PAD), BF16),
                        pltpu.VMEM((hp, HEAD_PAD, TT), F32), pltpu.VMEM((hp, HEAD_PAD, TT), F32)],
        compiler_params=_cp(("arbitrary", "arbitrary")), name="attn_bwd",
    )(*_in_hbm([q_raw, kv, pp, tab, tab, o, do]))


def _hbm_specs(n):
    return [pl.BlockSpec(memory_space=pl.ANY)] * n


def _gather_weights(shards):
    n = len(shards)
    halves = [s.shape[0] // 2 for s in shards]

    def body(*refs):
        ins, outs = refs[:n], refs[n:2 * n]
        token, send_sems, recv_sems = refs[2 * n:]
        token[...] = jnp.zeros_like(token)
        mx, my, mc = lax.axis_index("x"), lax.axis_index("y"), lax.axis_index("c")
        j_me = 2 * mx + my
        chips = [(1 - mx, my), (mx, 1 - my), (1 - mx, 1 - my)]

        def half(w, chip_idx, hc):
            return outs[w].at[chip_idx, pl.ds(hc * halves[w], halves[w]), :]

        def copy(w, k, src, dst, to):
            return pltpu.make_async_remote_copy(src_ref=src, dst_ref=dst, send_sem=send_sems.at[w, k],
                                                recv_sem=recv_sems.at[w, k], device_id=to, device_id_type=MESH)

        sends = []
        for w in range(n):
            cp = copy(w, 6, ins[w], outs[w].at[j_me], (mx, my, 1 - mc))
            cp.start()
            sends.append(cp)
        for k, (px, py) in enumerate(chips):
            for w in range(n):
                cp = copy(w, k, ins[w].at[pl.ds(mc * halves[w], halves[w]), :], half(w, j_me, mc), (px, py, mc))
                cp.start()
                sends.append(cp)
        for k, (px, py) in enumerate(chips):
            for w in range(n):
                got = half(w, 2 * px + py, mc)
                copy(w, k, got, got, (px, py, mc)).wait_recv()
                cp = copy(w, 3 + k, got, got, (mx, my, 1 - mc))
                cp.start()
                sends.append(cp)
        for k, (px, py) in enumerate(chips):
            for w in range(n):
                got = half(w, 2 * px + py, 1 - mc)
                copy(w, 3 + k, got, got, (mx, my, 1 - mc)).wait_recv()
        for w in range(n):
            own = outs[w].at[j_me]
            copy(w, 6, own, own, (mx, my, 1 - mc)).wait_recv()
        for cp in sends:
            cp.wait_send()

    res = pl.pallas_call(
        body, out_shape=[jax.ShapeDtypeStruct((4,) + s.shape, s.dtype) for s in shards]
        + [jax.ShapeDtypeStruct((8, 128), F32)],
        in_specs=_hbm_specs(n), out_specs=_hbm_specs(n) + [pl.BlockSpec(memory_space=pltpu.VMEM)],
        scratch_shapes=[pltpu.SemaphoreType.DMA((n, 7)), pltpu.SemaphoreType.DMA((n, 7))],
        name="gather_weights")(*shards)
    return list(res[:n]), res[n]


def _rs_pair(gs, name):
    n = len(gs)
    halves = [g.shape[1] // 2 for g in gs]

    def body(*refs):
        ins, lands = refs[:n], refs[n:2 * n]
        send_sems, recv_sems = refs[2 * n:]
        mx, my, mc = lax.axis_index("x"), lax.axis_index("y"), lax.axis_index("c")
        copies = []
        for w in range(n):
            h = halves[w]
            cp = pltpu.make_async_remote_copy(
                src_ref=ins[w].at[:, pl.ds((1 - mc) * h, h), :], dst_ref=lands[w], send_sem=send_sems.at[w],
                recv_sem=recv_sems.at[w], device_id=(mx, my, 1 - mc), device_id_type=MESH)
            cp.start()
            copies.append(cp)
        for cp in copies:
            cp.wait()

    return pl.pallas_call(
        body, out_shape=[jax.ShapeDtypeStruct((4, h, g.shape[2]), g.dtype) for g, h in zip(gs, halves)],
        in_specs=_hbm_specs(n), out_specs=_hbm_specs(n),
        scratch_shapes=[pltpu.SemaphoreType.DMA((n,)), pltpu.SemaphoreType.DMA((n,))], name=name)(*gs)


def _rs_chips(parts):
    n = len(parts)

    def body(*refs):
        ins, lands = refs[:n], refs[n:2 * n]
        send_sems, recv_sems = refs[2 * n:]
        mx, my, mc = lax.axis_index("x"), lax.axis_index("y"), lax.axis_index("c")
        copies = []
        for k, (px, py) in enumerate([(1 - mx, my), (mx, 1 - my), (1 - mx, 1 - my)]):
            for w in range(n):
                cp = pltpu.make_async_remote_copy(
                    src_ref=ins[w].at[2 * px + py], dst_ref=lands[w].at[k], send_sem=send_sems.at[w, k],
                    recv_sem=recv_sems.at[w, k], device_id=(px, py, mc), device_id_type=MESH)
                cp.start()
                copies.append(cp)
        for cp in copies:
            cp.wait()

    return list(pl.pallas_call(
        body, out_shape=[jax.ShapeDtypeStruct((3,) + p.shape[1:], p.dtype) for p in parts],
        in_specs=_hbm_specs(n), out_specs=_hbm_specs(n),
        scratch_shapes=[pltpu.SemaphoreType.DMA((n, 3)), pltpu.SemaphoreType.DMA((n, 3))], name="rs_chips")(*parts))


_HBM = pl.BlockSpec(memory_space=pltpu.HBM)
_SEM = pl.BlockSpec(memory_space=pltpu.SEMAPHORE)
_EFFECT = pltpu.SideEffectType.DATAFLOW_SIDE_EFFECTING


def _ici_copies(kind, srcs, lands, send_sems, recv_sems):
    n = len(lands)
    mx, my, mc = lax.axis_index("x"), lax.axis_index("y"), lax.axis_index("c")
    j_me = 2 * mx + my
    copies = []
    if kind == "back":
        for w in range(n):
            h = lands[w].shape[0] // 2
            mine = lands[w].at[pl.ds(mc * h, h), :]
            copies.append(pltpu.make_async_remote_copy(
                src_ref=mine, dst_ref=mine, send_sem=send_sems.at[w], recv_sem=recv_sems.at[w],
                device_id=(mx, my, 1 - mc), device_id_type=MESH))
        return copies
    if kind == "all":
        for k in range(7):
            a, b, c = (k + 1) >> 2 & 1, (k + 1) >> 1 & 1, (k + 1) & 1
            peer = (1 - mx if a else mx, 1 - my if b else my, 1 - mc if c else mc)
            for w in range(n):
                copies.append(pltpu.make_async_remote_copy(
                    src_ref=srcs[w], dst_ref=lands[w].at[4 * mx + 2 * my + mc], send_sem=send_sems.at[7 * w + k],
                    recv_sem=recv_sems.at[7 * w + k], device_id=peer, device_id_type=MESH))
        return copies
    if kind == "pair":
        for w in range(n):
            h = srcs[w].shape[1] // 2
            copies.append(pltpu.make_async_remote_copy(
                src_ref=srcs[w].at[:, pl.ds((1 - mc) * h, h), :], dst_ref=lands[w], send_sem=send_sems.at[w],
                recv_sem=recv_sems.at[w], device_id=(mx, my, 1 - mc), device_id_type=MESH))
        return copies
    chips = [(1 - mx, my), (mx, 1 - my), (1 - mx, 1 - my)]
    if kind == "finish":
        for w in range(n):
            h = srcs[w].shape[0] // 2
            pushes = [(lands[w].at[2 * px + py, pl.ds(mc * h, h), :],) * 2 for px, py in chips]
            pushes.append((srcs[w], lands[w].at[j_me]))
            for k, (src, dst) in enumerate(pushes):
                copies.append(pltpu.make_async_remote_copy(
                    src_ref=src, dst_ref=dst, send_sem=send_sems.at[4 * w + k], recv_sem=recv_sems.at[4 * w + k],
                    device_id=(mx, my, 1 - mc), device_id_type=MESH))
        return copies
    for k, (px, py) in enumerate(chips):
        for w in range(n):
            if kind == "gather":
                h = srcs[w].shape[0] // 2
                src, dst = srcs[w].at[pl.ds(mc * h, h), :], lands[w].at[j_me, pl.ds(mc * h, h), :]
            else:
                src, dst = srcs[w].at[2 * px + py], lands[w].at[k]
            copies.append(pltpu.make_async_remote_copy(
                src_ref=src, dst_ref=dst, send_sem=send_sems.at[3 * w + k], recv_sem=recv_sems.at[3 * w + k],
                device_id=(px, py, mc), device_id_type=MESH))
    return copies


_SEMS_PER_OPERAND = {"gather": 3, "scatter": 3, "all": 7, "pair": 1, "finish": 4, "back": 1}


def _ici_start(kind, srcs, land_shapes, carry, name, lands=None):
    hbm = lambda a: pltpu.with_memory_space_constraint(a, pltpu.HBM)
    if lands is None:
        lands = [lax.empty(s, srcs[0].dtype) for s in land_shapes]
    ns, nl = len(srcs), len(lands)

    def body(*refs):
        send_sems, recv_sems = refs[ns + nl + 1], refs[ns + nl + 2]
        for cp in _ici_copies(kind, refs[:ns], refs[ns:ns + nl], send_sems, recv_sems):
            cp.start()

    args = [hbm(a) for a in list(srcs) + list(lands) + [carry]]
    n_sem = _SEMS_PER_OPERAND[kind] * nl
    out_shape = ([pltpu.SemaphoreType.DMA((n_sem,)), pltpu.SemaphoreType.DMA((n_sem,))]
                 + [pltpu.HBM(a.shape, a.dtype) for a in args])
    res = pl.pallas_call(
        body, name=name, out_shape=out_shape, in_specs=[_HBM] * len(args), out_specs=[_SEM, _SEM] + [_HBM] * len(args),
        input_output_aliases={i: 2 + i for i in range(len(args))},
        compiler_params=pltpu.CompilerParams(has_side_effects=_EFFECT))(*args)
    return res[0], res[1], list(res[2:2 + ns]), list(res[2 + ns:2 + ns + nl]), res[2 + ns + nl]


def _ici_wait(kind, send_sems, recv_sems, srcs, lands, after, name):
    ns, nl = len(srcs), len(lands)

    def body(*refs):
        for cp in _ici_copies(kind, refs[:ns], refs[ns:ns + nl], refs[ns + nl], refs[ns + nl + 1]):
            cp.wait_send()
            cp.wait_recv()

    args = list(srcs) + list(lands)
    res = pl.pallas_call(
        body, name=name, out_shape=[pltpu.HBM(a.shape, a.dtype) for a in args],
        in_specs=[_HBM] * len(args) + [_SEM, _SEM, pl.BlockSpec(memory_space=pl.ANY)], out_specs=[_HBM] * len(args),
        input_output_aliases={i: i for i in range(len(args))},
        compiler_params=pltpu.CompilerParams(has_side_effects=_EFFECT))(*args, send_sems, recv_sems, after)
    return list(res[:ns]), list(res[ns:])


def _tile_rows(h, c, itemsize, mult):
    best = h
    for t in range(mult, h + 1, mult):
        if h % t == 0 and t * c * itemsize <= (1 << 21):
            best = t
    return best


def _add_pair(g, land, place, name):
    _, h, c = land.shape
    t = _tile_rows(h, c, 2, 16)
    nb = h // t
    return _ew(lambda ids, u, v: (u.astype(F32) + v.astype(F32),), (4, nb),
               [(g, pl.BlockSpec((None, t, c), lambda j, i, s: (j, s[1] * nb + i, 0))),
                (land, pl.BlockSpec((None, t, c), lambda j, i, s: (j, i, 0)))],
               [(land.shape, BF16, pl.BlockSpec((None, t, c), lambda j, i, s: (j, i, 0)), None)], name, scalars=place)[0]


def _add_pair_many(gs, lands, place, name):
    ins, outs = [], []
    for g, l in zip(gs, lands):
        ins += [(g, pl.BlockSpec(l.shape, lambda i, s: (0, s[1], 0))), (l, pl.BlockSpec(l.shape, lambda i, s: (0, 0, 0)))]
        outs.append((l.shape, BF16, pl.BlockSpec(l.shape, lambda i, s: (0, 0, 0)), None))
    fn = lambda ids, *v: [v[2 * k].astype(F32) + v[2 * k + 1].astype(F32) for k in range(len(gs))]
    return list(_ew(fn, (1,), ins, outs, name, scalars=place))


def _add_chips_many(owns, lands, place, name):
    ins, outs = [], []
    for own, land in zip(owns, lands):
        _, h, c = land.shape
        ins += [(own, pl.BlockSpec((None, h, c), lambda i, s: (s[0], 0, 0))),
                (land, pl.BlockSpec((3, h, c), lambda i, s: (0, 0, 0)))]
        outs.append(((2 * h, c), F32, pl.BlockSpec((h, c), lambda i, s: (s[1], 0)), None))

    def fn(ids, *v):
        return [((v[2 * k].astype(F32) + v[2 * k + 1][0].astype(F32)) + v[2 * k + 1][1].astype(F32))
                + v[2 * k + 1][2].astype(F32) for k in range(len(owns))]

    return list(_ew(fn, (1,), ins, outs, name, scalars=place))


def _add_chips(own, land, place, name):
    _, h, c = land.shape
    t = _tile_rows(h, c, 4, 16)
    nb = h // t

    def fn(ids, a, b):
        return (((a.astype(F32) + b[0].astype(F32)) + b[1].astype(F32)) + b[2].astype(F32),)

    return _ew(fn, (nb,), [(own, pl.BlockSpec((None, t, c), lambda i, s: (s[0], i, 0))),
                           (land, pl.BlockSpec((3, t, c), lambda i, s: (0, i, 0)))],
               [((2 * h, c), F32, pl.BlockSpec((t, c), lambda i, s: (s[1] * nb + i, 0)), None)], name, scalars=place)[0]


W_IN_SEGMENTS = ((0, 256, KV0), (256, 288, KR0 + 64), (288, 672, Q0), (672, 1184, CX0), (1184, 1696, CB0),
                 (1696, 2208, CC0), (2208, 3232, GA0), (3232, 4256, GC0))
W_IN_SHARD = 1064


W_IN_SHARD_PAD = 1088
W_IN_EARLY = 672


def _w_in_t_p_from_shards(s):
    pieces = []
    for o0, o1, p0 in sorted(W_IN_SEGMENTS, key=lambda t: t[2]):
        if p0 == KR0 + 64:
            pieces.append(jnp.zeros((64, s.shape[2]), s.dtype))
        for j in range(4):
            lo, hi = max(o0, j * W_IN_SHARD), min(o1, (j + 1) * W_IN_SHARD)
            if lo < hi:
                pieces.append(s[j, lo - j * W_IN_SHARD:hi - j * W_IN_SHARD])
    pieces.append(jnp.zeros((32, s.shape[2]), s.dtype))
    return jnp.concatenate(pieces, axis=0)


def _w_in_t_shards_from_p(g):
    shards = []
    for j in range(4):
        pieces = []
        for o0, o1, p0 in W_IN_SEGMENTS:
            lo, hi = max(o0, j * W_IN_SHARD), min(o1, (j + 1) * W_IN_SHARD)
            if lo < hi:
                pieces.append(g[p0 + lo - o0:p0 + hi - o0])
        pieces.append(jnp.zeros((W_IN_SHARD_PAD - W_IN_SHARD, g.shape[1]), g.dtype))
        shards.append(jnp.concatenate(pieces, axis=0))
    return jnp.stack(shards, axis=0)


def _cols_from_shards(s):
    return jnp.transpose(s, (1, 0, 2)).reshape(s.shape[1], -1)


def _rope_tables(T, TT, inverse):
    f32 = np.float32
    rows = T // GRID_W
    row = np.repeat(np.arange(rows), GRID_W).astype(f32)
    col = np.tile(np.arange(GRID_W), rows).astype(f32)
    inv = (f32(ROPE_THETA) ** (-np.arange(0, 16, 2, dtype=f32) / f32(16))).astype(f32)
    ang = np.concatenate([row[:, None] * inv, col[:, None] * inv], axis=-1).astype(f32)
    cos, sin = np.cos(ang).astype(f32), np.sin(ang).astype(f32)
    lane = np.arange(32)
    src = (lane // 16) * 8 + lane % 8
    lo = ((lane % 16) // 8 == 0).astype(f32)
    sgn = f32(-1.0 if inverse else 1.0)
    cos32 = cos[:, src]
    sin_lo32 = -sgn * sin[:, src] * lo
    sin_hi32 = sgn * sin[:, src] * (1 - lo)

    def widen(t32, fill):
        t = np.concatenate([np.full((T, 64), fill, f32), t32, np.full((T, 32), fill, f32)], axis=1)
        return np.concatenate([t, np.full((TT - T, HEAD_PAD), fill, f32)], axis=0)

    return [widen(cos32, 1.0), widen(sin_lo32, 0.0), widen(sin_hi32, 0.0)]


def _rope_table(T, TT):
    return jnp.asarray(np.concatenate(_rope_tables(T, TT, False) + _rope_tables(T, TT, True), axis=1))


def _local_step(xx, tgt, mod_lat, mod_ctx, W, late_weights, early_grads, early_continue):
    TT = xx.shape[0]
    T = tgt.shape[0]
    n_lat, n_all = T // ROW_TILE, TT // ROW_TILE
    sh1, sc1, g1, sh2, sc2, g2 = [mod_lat[:, k * D_MODEL:(k + 1) * D_MODEL] for k in range(6)]
    csh1, csc1 = mod_ctx[:, :D_MODEL], mod_ctx[:, D_MODEL:2 * D_MODEL]
    vec = lambda n: _full((1, n))
    row_out = lambda n, dt, rows=T: ((rows, n), dt, _rows(n), None)
    acc_out = lambda n: ((1, n), F32, _full((1, n)), 0)
    lt = _pick(T, (2 * ROW_TILE, ROW_TILE))
    n_lt = T // lt
    lrows = lambda n, cblk=0: _rows(n, cblk, 0, lt)
    lrow_out = lambda n, dt: ((T, n), dt, lrows(n), None)

    def f_norm1(ids, x, g, a_sh, a_sc, b_sh, b_sc):
        ctx = ids[0] >= n_lat
        sh, sc = jnp.where(ctx, b_sh, a_sh), jnp.where(ctx, b_sc, a_sc)
        return ((x * _rms(x) * g) * (1.0 + sc) + sh,)

    (hh,) = _ew(f_norm1, (n_all,), [(xx, _rows(D_MODEL)), (W["norm1_g"], vec(D_MODEL)), (sh1, vec(D_MODEL)),
                                   (sc1, vec(D_MODEL)), (csh1, vec(D_MODEL)), (csc1, vec(D_MODEL))],
                [row_out(D_MODEL, BF16, TT)], "norm1_fwd")
    tm_all = _pick(TT, (768, 256))
    pp_a = _mm(hh, W["w_in_a_t"], "nt", TT, PA_COLS, D_MODEL, tm=tm_all, tn=PA_COLS, tk=D_MODEL, name="w_in_fwd_a")

    def f_lowrank(ids, ckv, cq, gkv, gq):
        return ckv * _rms(ckv) * gkv, cq * _rms(cq) * gq

    nkv, nq = _ew(f_lowrank, (n_all,), [(pp_a, _rows(KV_RANK, PA_KV0 // KV_RANK)), (pp_a, _rows(Q_RANK, PA_Q0 // Q_RANK)),
                                       (W["kv_norm_g"], vec(KV_RANK)), (W["q_norm_g"], vec(Q_RANK))],
                  [row_out(KV_RANK, BF16, TT), row_out(Q_RANK, BF16, TT)], "lowrank_norm_fwd")
    kv = _mm(nkv, W["w_ukv"], "nn", TT, 1024, KV_RANK, tm=tm_all, tn=256, tk=KV_RANK, name="w_ukv_fwd",
             b_spec=pl.BlockSpec((None, KV_RANK, 256), lambda i, j, k: (j, k, 0)))
    q_raw = _mm(nq, W["w_uq_t"], "nt", TT, 1024, Q_RANK, tm=tm_all, tn=1024, tk=Q_RANK, name="w_uq_fwd")

    tab = _rope_table(T, TT)
    _, q_raw = late_weights("before_attn", q_raw)
    o_pad = _attn_fwd(q_raw, kv, pp_a, tab, T, TT)
    arrived, o_pad = late_weights("after_attn", o_pad)
    W = dict(W, **arrived)
    tm_lat = _pick(T, (1024, 512, 256))
    pp = _mm(hh, W["w_in_t"], "nt", T, KV0, D_MODEL, tm=tm_lat, tn=KV0 // 2, tk=D_MODEL, name="w_in_fwd_b")
    ya = _mm(o_pad, W["w_attn_out"], "nn", T, D_MODEL, 1024, tm=tm_lat, tn=D_MODEL, tk=1024, name="w_attn_out_fwd",
             out_dtype=BF16)

    tc = 256
    colT = lambda blk0: pl.BlockSpec((T, tc), lambda j: (0, blk0 + j))

    def f_conv(ids, xin, cb, cc, w, b):
        return (cb * _conv(cc * xin, w, b),)

    (e,) = _ew(f_conv, (CONV_DIM // tc,),
               [(pp, colT(CX0 // tc)), (pp, colT(CB0 // tc)), (pp, colT(CC0 // tc)),
                (W["conv_w"], pl.BlockSpec((3, tc), lambda j: (0, j))), (W["conv_b"], pl.BlockSpec((1, tc), lambda j: (0, j)))],
               [((T, CONV_DIM), BF16, colT(0), None)], "conv_fwd")
    yc = _mm(e, W["w_conv_out"], "nn", T, D_MODEL, CONV_DIM, tm=tm_lat, tn=256, tk=CONV_DIM, name="w_conv_out_fwd",
             out_dtype=BF16, b_spec=pl.BlockSpec((None, CONV_DIM, 256), lambda i, j, k: (j, k, 0)))

    def f_merge(ids, ga, gc, a, c):
        return (_sigmoid(ga) * a.astype(F32) + _sigmoid(gc) * c.astype(F32),)

    (mrg,) = _ew(f_merge, (n_lt,), [(pp, lrows(D_MODEL, 0)), (pp, lrows(D_MODEL, 1)), (ya, lrows(D_MODEL)),
                                   (yc, lrows(D_MODEL))], [lrow_out(D_MODEL, BF16)], "merge_fwd")
    mo = _mm(mrg, W["w_o"], "nn", T, D_MODEL, D_MODEL, tm=tm_lat, tn=D_MODEL, tk=D_MODEL, name="w_o_fwd")

    def f_norm2(ids, x, m, gate, g, sh, sc):
        x1 = x + gate * m
        return x1, (x1 * _rms(x1) * g) * (1.0 + sc) + sh

    x1, h2 = _ew(f_norm2, (n_lt,), [(xx, lrows(D_MODEL)), (mo, lrows(D_MODEL)), (g1, vec(D_MODEL)),
                                   (W["norm2_g"], vec(D_MODEL)), (sh2, vec(D_MODEL)), (sc2, vec(D_MODEL))],
                 [lrow_out(D_MODEL, F32), lrow_out(D_MODEL, BF16)], "norm2_fwd")
    arrived, h2 = late_weights("before_ffn", h2)
    W = dict(W, **arrived)
    up = _mm(h2, W["w_up"], "nn", T, 2 * D_FF, D_MODEL, tm=tm_lat, tn=1408, tk=D_MODEL, name="w_up_fwd",
             b_spec=pl.BlockSpec((None, D_MODEL, 1408), lambda i, j, k: (j, k, 0)))

    n_ff = D_FF // tc
    ffw = lambda off, n=3: pl.BlockSpec((n, tc), lambda j: (0, j + off))

    def f_ffn(ids, ug, uv, wg, wv, bg, bv):
        gate, val = _conv(ug, wg, bg), _conv(uv, wv, bv)
        return (gate * _sigmoid(gate) * val,)

    (act,) = _ew(f_ffn, (n_ff,), [(up, colT(0)), (up, colT(n_ff)), (W["ffn_conv_w"], ffw(0)), (W["ffn_conv_w"], ffw(n_ff)),
                                 (W["ffn_conv_b"], ffw(0, 1)), (W["ffn_conv_b"], ffw(n_ff, 1))],
                 [((T, D_FF), BF16, colT(0), None)], "ffn_act_fwd")
    f = _mm(act, W["w_down"], "nn", T, D_MODEL, D_FF, tm=tm_lat, tn=D_MODEL, tk=D_FF, name="w_down_fwd")

    def f_head(ids, x1_, f_, gate, gf, t):
        x2 = x1_ + gate * f_
        r = _rms(x2)
        xn = x2 * r
        err = xn * gf - t
        loss = 0.5 * jnp.sum(jnp.mean(err * err, axis=-1, keepdims=True))
        dy = err * (1.0 / D_MODEL)
        dx2 = _rms_bwd(dy * gf, xn, r)
        return dx2, dx2 * gate, _colsum(dy * xn), _colsum(dx2 * f_), jnp.full((1, 128), loss, F32)

    dx2, df, dg_f, dg2, loss = _ew(
        f_head, (n_lt,), [(x1, lrows(D_MODEL)), (f, lrows(D_MODEL)), (g2, vec(D_MODEL)), (W["final_g"], vec(D_MODEL)),
                          (tgt, lrows(D_MODEL))],
        [lrow_out(D_MODEL, F32), lrow_out(D_MODEL, BF16), acc_out(D_MODEL), acc_out(D_MODEL), acc_out(128)], "loss_head")

    d_w_down = _mm(act, df, "tn", D_FF, D_MODEL, T, tm=1408, tn=D_MODEL, tk=T, name="w_down_dw",
                   out_dtype=BF16).reshape(4, D_FF // 4, D_MODEL)
    da = _mm(df, W["w_down"], "nt", T, D_FF, D_MODEL, tm=tm_lat, tn=1408, tk=D_MODEL, name="w_down_dx")

    tcb = 128
    n_fb = D_FF // tcb
    colb = lambda blk0: pl.BlockSpec((T, tcb), lambda j: (0, blk0 + j))
    ffwb = lambda off, n=3: pl.BlockSpec((n, tcb), lambda j: (0, j + off))
    cvec = ((1, D_FF), F32, pl.BlockSpec((1, tcb), lambda j: (0, j)), None)

    def f_ffn_bwd(ids, ug, uv, d_act, wg, wv, bg, bv):
        sg, sv = _shifts(ug), _shifts(uv)
        gate, val = _conv(ug, wg, bg, sg), _conv(uv, wv, bv, sv)
        s = _sigmoid(gate)
        d_gate = d_act * val * s * (1.0 + gate * (1.0 - s))
        d_val = d_act * gate * s
        wg0, wg1, wg2 = _conv_bwd_w(d_gate, ug, sg)
        wv0, wv1, wv2 = _conv_bwd_w(d_val, uv, sv)
        d_up = [_conv_bwd_x(d_gate, wg), _conv_bwd_x(d_val, wv)]
        return d_up, [_colsum(d_gate), _colsum(d_val), wg0, wg1, wg2, wv0, wv1, wv2]

    d_up3, ffn_stats = _ew(
        f_ffn_bwd, (n_fb,),
        [(up, colb(0)), (up, colb(n_fb)), (da, colb(0)), (W["ffn_conv_w"], ffwb(0)), (W["ffn_conv_w"], ffwb(n_fb)),
         (W["ffn_conv_b"], ffwb(0, 1)), (W["ffn_conv_b"], ffwb(n_fb, 1))],
        [((2, T, D_FF), BF16, pl.BlockSpec((2, T, tcb), lambda j: (0, 0, j)), None),
         ((n_fb, 8, 1, tcb), F32, pl.BlockSpec((None, 8, 1, tcb), lambda j: (j, 0, 0, 0)), None)], "ffn_act_bwd")
    stat = lambda s: ffn_stats[:, s, 0, :].reshape(1, D_FF)
    d_ffn_conv_b = jnp.concatenate([stat(0), stat(1)], axis=1)
    d_ffn_conv_w = jnp.concatenate([jnp.concatenate([stat(2), stat(3), stat(4)], axis=0),
                                    jnp.concatenate([stat(5), stat(6), stat(7)], axis=0)], axis=1)

    tk_t = T
    d_w_up = _mm(h2, d_up3, "tn", D_MODEL, 2 * D_FF, T, tm=D_MODEL, tn=1408, tk=tk_t, name="w_up_dw", out_dtype=BF16,
                 b_spec=pl.BlockSpec((None, tk_t, 1408), lambda i, j, k: (j // 2, k, j % 2)),
                 o_spec=pl.BlockSpec((None, D_MODEL, 1408), lambda i, j, k: (j, i, 0)), out_shape=(4, D_MODEL, 1408))
    dh2 = _mm(d_up3, W["w_up"], "nt", T, D_MODEL, 2 * D_FF, tm=tm_lat, tn=D_MODEL, tk=1408, name="w_up_dx",
              a_spec=pl.BlockSpec((None, tm_lat, 1408), lambda i, j, k: (k // 2, i, k % 2)),
              b_spec=pl.BlockSpec((None, D_MODEL, 1408), lambda i, j, k: (k, j, 0)))

    def f_norm2_bwd(ids, dx2_, dh, x1_, m, g, sc, gate):
        r = _rms(x1_)
        xn = x1_ * r
        dx1 = dx2_ + _rms_bwd(dh * g * (1.0 + sc), xn, r)
        return dx1, dx1 * gate, _colsum(dh), _colsum(dh * xn * g), _colsum(dh * xn * (1.0 + sc)), _colsum(dx1 * m)

    dx1, dmo, dsh2, dsc2, dg_n2, dg1 = _ew(
        f_norm2_bwd, (n_lt,), [(dx2, lrows(D_MODEL)), (dh2, lrows(D_MODEL)), (x1, lrows(D_MODEL)), (mo, lrows(D_MODEL)),
                               (W["norm2_g"], vec(D_MODEL)), (sc2, vec(D_MODEL)), (g1, vec(D_MODEL))],
        [lrow_out(D_MODEL, F32), lrow_out(D_MODEL, BF16)] + [acc_out(D_MODEL)] * 4, "norm2_bwd")
    d_w_o = _mm(mrg, dmo, "tn", D_MODEL, D_MODEL, T, tm=D_MODEL, tn=D_MODEL, tk=tk_t, name="w_o_dw",
                out_dtype=BF16).reshape(4, D_MODEL // 4, D_MODEL)
    dmrg = _mm(dmo, W["w_o"], "nt", T, D_MODEL, D_MODEL, tm=tm_lat, tn=D_MODEL, tk=D_MODEL, name="w_o_dx",
               out_dtype=BF16)
    dmrg = early_grads("late", {"w_o": d_w_o, "w_up": d_w_up, "w_down": d_w_down}, dmrg, split=True)

    def f_merge_bwd(ids, dm, ga, gc, a, c):
        dm, a, c = dm.astype(F32), a.astype(F32), c.astype(F32)
        sa, sc_ = _sigmoid(ga), _sigmoid(gc)
        return dm * sa, dm * sc_, dm * a * sa * (1.0 - sa), dm * c * sc_ * (1.0 - sc_)

    dya, dyc, dp_ga, dp_gc = _ew(
        f_merge_bwd, (n_lt,), [(dmrg, lrows(D_MODEL)), (pp, lrows(D_MODEL, 0)), (pp, lrows(D_MODEL, 1)),
                               (ya, lrows(D_MODEL)), (yc, lrows(D_MODEL))], [lrow_out(D_MODEL, BF16)] * 4, "merge_bwd")
    dya = early_continue("late", dya)

    d_w_ao_p = _mm(o_pad, dya, "tn", 1024, D_MODEL, T, tm=1024, tn=D_MODEL, tk=tk_t, name="w_attn_out_dw", out_dtype=BF16)
    do_pad = _mm(dya, W["w_attn_out"], "nt", T, 1024, D_MODEL, tm=tm_lat, tn=1024, tk=D_MODEL, name="w_attn_out_dx")
    d_w_co = _mm(e, dyc, "tn", CONV_DIM, D_MODEL, T, tm=CONV_DIM, tn=256, tk=tk_t, name="w_conv_out_dw", out_dtype=BF16,
                 o_spec=pl.BlockSpec((None, CONV_DIM, 256), lambda i, j, k: (j, i, 0)), out_shape=(4, CONV_DIM, 256))
    de = _mm(dyc, W["w_conv_out"], "nt", T, CONV_DIM, D_MODEL, tm=tm_lat, tn=CONV_DIM, tk=256, name="w_conv_out_dx",
             b_spec=pl.BlockSpec((None, CONV_DIM, 256), lambda i, j, k: (k, j, 0)))

    def f_conv_bwd(ids, xin, cb, cc, d_e, w, b):
        z = cc * xin
        sz = _shifts(z)
        cz = _conv(z, w, b, sz)
        dcz = d_e * cb
        w0, w1, w2 = _conv_bwd_w(dcz, z, sz)
        dz = _conv_bwd_x(dcz, w)
        return dz * cc, d_e * cz, dz * xin, _colsum(dcz), w0, w1, w2

    cvec_c = ((1, CONV_DIM), F32, pl.BlockSpec((1, tc), lambda j: (0, j)), None)
    conv_b = _ew(f_conv_bwd, (CONV_DIM // tc,),
                 [(pp, colT(CX0 // tc)), (pp, colT(CB0 // tc)), (pp, colT(CC0 // tc)), (de, colT(0)),
                  (W["conv_w"], pl.BlockSpec((3, tc), lambda j: (0, j))), (W["conv_b"], pl.BlockSpec((1, tc), lambda j: (0, j)))],
                 [((T, CONV_DIM), BF16, colT(0), None)] * 3 + [cvec_c] * 4, "conv_bwd")
    dp_cx, dp_cb, dp_cc, d_conv_b = conv_b[:4]
    d_conv_w = jnp.concatenate(conv_b[4:7], axis=0)

    dq_raw, dkv, dp_kr = _attn_bwd(q_raw, kv, pp_a, o_pad, do_pad, tab, T, TT)

    tk_a = TT
    d_w_uq_t = _mm(nq, dq_raw, "tn", Q_RANK, 1024, T, tm=Q_RANK, tn=1024, tk=T, name="w_uq_dw", transpose_out=True)
    dnq = _mm(dq_raw, W["w_uq_t"], "nn", T, Q_RANK, 1024, tm=tm_lat, tn=Q_RANK, tk=1024, name="w_uq_dx")
    d_w_ukv = _mm(nkv, dkv, "tn", KV_RANK, 1024, TT, tm=KV_RANK, tn=256, tk=tk_a, name="w_ukv_dw", out_dtype=BF16,
                  o_spec=pl.BlockSpec((None, KV_RANK, 256), lambda i, j, k: (j, i, 0)), out_shape=(4, KV_RANK, 256))
    dnkv = _mm(dkv, W["w_ukv"], "nt", TT, KV_RANK, 1024, tm=tm_all, tn=KV_RANK, tk=256, name="w_ukv_dx",
               b_spec=pl.BlockSpec((None, KV_RANK, 256), lambda i, j, k: (k, j, 0)))
    dnkv = early_grads("mid", {
        "w_attn_out": jnp.transpose(d_w_ao_p.reshape(N_HEADS, HEAD_PAD, 4, 256)[:, 64:], (2, 0, 1, 3)).reshape(
            4, N_HEADS * 64, 256),
        "w_conv_out": d_w_co,
        "w_uq": d_w_uq_t.reshape(4, 2, HEAD_PAD, Q_RANK)[:, :, :QK_DIM].reshape(4, 2 * QK_DIM, Q_RANK).astype(BF16),
        "w_ukv": d_w_ukv}, dnkv)

    def f_lowrank_bwd(ids, ckv, cq, dkv_, dq_, gkv, gq, ga, gc, cx, cb, cc, kr):
        rk, rq = _rms(ckv), _rms(cq)
        nk, nq_ = ckv * rk, cq * rq
        lat = ids[0] < n_lat
        dq_ = jnp.where(lat, dq_, 0.0)
        pieces = [jnp.where(lat, a, jnp.zeros_like(a)) for a in (ga, gc, cx, cb, cc)]
        pieces += [_rms_bwd(dkv_ * gkv, nk, rk).astype(BF16), _rms_bwd(dq_ * gq, nq_, rq).astype(BF16), kr.astype(BF16)]
        return jnp.concatenate(pieces, axis=1), _colsum(dkv_ * nk), _colsum(dq_ * nq_)

    lat_rows = lambda n: pl.BlockSpec((ROW_TILE, n), lambda i: (jnp.minimum(i, n_lat - 1), 0))
    dpp, dg_kv, dg_q = _ew(
        f_lowrank_bwd, (n_all,), [(pp_a, _rows(KV_RANK, PA_KV0 // KV_RANK)), (pp_a, _rows(Q_RANK, PA_Q0 // Q_RANK)),
                                  (dnkv, _rows(KV_RANK)), (dnq, lat_rows(Q_RANK)), (W["kv_norm_g"], vec(KV_RANK)),
                                  (W["q_norm_g"], vec(Q_RANK)), (dp_ga, lat_rows(D_MODEL)), (dp_gc, lat_rows(D_MODEL)),
                                  (dp_cx, lat_rows(CONV_DIM)), (dp_cb, lat_rows(CONV_DIM)), (dp_cc, lat_rows(CONV_DIM)),
                                  (dp_kr, _rows(HEAD_PAD))],
        [row_out(P_COLS, BF16, TT), acc_out(KV_RANK), acc_out(Q_RANK)], "lowrank_norm_bwd")
    d_w_in_t = _mm(hh, dpp, "tn", D_MODEL, P_COLS, TT, tm=512, tn=2176, tk=TT, name="w_in_dw", out_dtype=BF16,
                   transpose_out=True)
    dhh = _mm(dpp, W["w_in_t"], "nn", TT, D_MODEL, P_COLS, tm=tm_all, tn=512, tk=2176, name="w_in_dx")

    def f_norm1_bwd(ids, x, dh, dres, g, sc):
        r = _rms(x)
        xn = x * r
        return (dres + _rms_bwd(dh * g * (1.0 + sc), xn, r), _colsum(dh), _colsum(dh * xn * g),
                _colsum(dh * xn * (1.0 + sc)))

    grad_x, dsh1, dsc1, dg_n1 = _ew(
        f_norm1_bwd, (n_lt,), [(xx, lrows(D_MODEL)), (dhh, lrows(D_MODEL)), (dx1, lrows(D_MODEL)),
                               (W["norm1_g"], vec(D_MODEL)), (sc1, vec(D_MODEL))],
        [lrow_out(D_MODEL, F32)] + [acc_out(D_MODEL)] * 3, "norm1_bwd")

    def f_norm1_ctx_bwd(ids, x, dh, g, sc):
        xn = x * _rms(x)
        return _colsum(dh), _colsum(dh * xn * g), _colsum(dh * xn * (1.0 + sc))

    n_ctx = n_all - n_lat
    dcsh1, dcsc1, dg_n1c = _ew(
        f_norm1_ctx_bwd, (n_ctx,), [(xx, _rows(D_MODEL, 0, n_lat)), (dhh, _rows(D_MODEL, 0, n_lat)),
                                    (W["norm1_g"], vec(D_MODEL)), (csc1, vec(D_MODEL))], [acc_out(D_MODEL)] * 3,
        "norm1_ctx_bwd")

    big = {"w_in": _w_in_t_shards_from_p(d_w_in_t).astype(BF16)}
    zero = jnp.zeros((1, 4 * D_MODEL), F32)
    small = {
        "dmod_lat": jnp.concatenate([dsh1, dsc1, dg1, dsh2, dsc2, dg2], axis=1),
        "dmod_ctx": jnp.concatenate([dcsh1, dcsc1, zero], axis=1),
        "norm1_g": dg_n1 + dg_n1c, "norm2_g": dg_n2, "final_g": dg_f, "q_norm_g": dg_q, "kv_norm_g": dg_kv,
        "conv_b": d_conv_b, "conv_w": d_conv_w.reshape(1, -1), "ffn_conv_b": d_ffn_conv_b,
        "ffn_conv_w": d_ffn_conv_w.reshape(1, -1),
    }
    return grad_x, loss, big, small


SMALL = (("dmod_lat", 6144), ("dmod_ctx", 6144), ("norm1_g", 1024), ("norm2_g", 1024), ("final_g", 1024),
         ("q_norm_g", 384), ("kv_norm_g", 256), ("conv_b", 512), ("conv_w", 1536), ("ffn_conv_b", 5632),
         ("ffn_conv_w", 16896), ("loss", 128))
SMALL_ROWS = 320


def _adam_update(w, g, m, v):
    c1, c2 = 1.0 - ADAM_B1 ** ADAM_STEP, 1.0 - ADAM_B2 ** ADAM_STEP
    m2 = ADAM_B1 * m + (1.0 - ADAM_B1) * g
    v2 = ADAM_B2 * v + (1.0 - ADAM_B2) * (g * g)
    return [-ADAM_LR * ((m2 / c1) / (jnp.sqrt(v2 / c2) + ADAM_EPS) + ADAM_WD * w), m2, v2]


def _adamw(w, g, m, v, name):
    R, C = w.shape
    tr = 8 if R % 8 == 0 else R
    for t in range(8, R + 1, 8):
        if R % t == 0 and t * C * 4 <= (1 << 21):
            tr = t
    spec = pl.BlockSpec((tr, C), lambda i: (i, 0))
    return _ew(lambda ids, *vals: [vals[1]] + _adam_update(*vals), (R // tr,),
               [(w, spec), (g, spec), (m, spec), (v, spec)], [((R, C), F32, spec, None)] * 4, name)


def kernel(x, c, ctx, c_ctx, w_ada, b_ada, norm1_g, w_in, q_norm_g, kv_norm_g, w_uq, w_ukv, conv_w, conv_b, w_attn_out, w_conv_out, w_o, norm2_g, w_up, ffn_conv_w, ffn_conv_b, w_down, final_g, loss_target, m_c_ctx, m_w_ada, m_b_ada, m_norm1_g, m_w_in, m_q_norm_g, m_kv_norm_g, m_w_uq, m_w_ukv, m_conv_w, m_conv_b, m_w_attn_out, m_w_conv_out, m_w_o, m_norm2_g, m_w_up, m_ffn_conv_w, m_ffn_conv_b, m_w_down, m_final_g, v_c_ctx, v_w_ada, v_b_ada, v_norm1_g, v_w_in, v_q_norm_g, v_kv_norm_g, v_w_uq, v_w_ukv, v_conv_w, v_conv_b, v_w_attn_out, v_w_conv_out, v_w_o, v_norm2_g, v_w_up, v_ffn_conv_w, v_ffn_conv_b, v_w_down, v_final_g):
    mx, my, mc = lax.axis_index("x"), lax.axis_index("y"), lax.axis_index("c")
    chip = 2 * mx + my
    dev = 4 * mx + 2 * my + mc
    T, Tc = x.shape[1], ctx.shape[1]
    TT = T + Tc
    w_in_t, m_w_in_t, v_w_in_t = (jnp.transpose(a[0]) for a in (w_in, m_w_in, v_w_in))
    w_uq_t, m_w_uq_t, v_w_uq_t = (jnp.transpose(a[0]) for a in (w_uq, m_w_uq, v_w_uq))
    conv_sh = jnp.concatenate([conv_w[0], ffn_conv_w[0]], axis=1)
    pay1 = jnp.concatenate([jnp.pad(c, ((0, 7), (0, 0))), jnp.pad(conv_sh, ((0, 5), (0, 0)))], axis=1)
    c_send, c_recv, c_src, c_land, zero0 = _ici_start("all", [pay1], [(8, 8, 2560)], jnp.zeros((8, 128), F32),
                                                      "cond_start")
    w_in_bf = (jnp.pad(w_in_t, ((0, W_IN_SHARD_PAD - W_IN_SHARD), (0, 0))) + zero0[0, 0]).astype(BF16)
    shards = {"w_in_a": w_in_bf[:W_IN_EARLY], "w_in_b": w_in_bf[W_IN_EARLY:], "w_uq": w_uq_t, "w_ukv": w_ukv[0],
              "w_attn_out": w_attn_out[0], "w_conv_out": w_conv_out[0], "w_o": w_o[0], "w_up": w_up[0],
              "w_down": w_down[0]}
    first = ["w_in_a", "w_uq", "w_ukv"]
    bf0 = [(shards[n] + zero0[0, 0].astype(shards[n].dtype)).astype(BF16) for n in first]
    flight0 = _ici_start("gather", bf0, [(4,) + s.shape for s in bf0], jnp.zeros((8, 128), F32), "gather_g0_start")
    (pay1,), (c_land,) = _ici_wait("all", c_send, c_recv, c_src, c_land, flight0[4], "cond_wait")
    got1 = lax.dynamic_update_slice(c_land, pay1[None], (dev, 0, 0))
    c_all = got1[:, 0, :D_MODEL]
    conv_all = got1[0::2, :3, D_MODEL:]
    conv_w_full = _cols_from_shards(conv_all[:, :, :128])
    ffn_conv_w_full = _cols_from_shards(conv_all[:, :, 128:])

    cond = jnp.concatenate([c_all, c_ctx.reshape(1, D_MODEL), jnp.zeros((7, D_MODEL), F32)], axis=0)

    def f_silu(ids, v):
        return (v * _sigmoid(v),)

    (s16,) = _ew(f_silu, (1,), [(cond, _full((16, D_MODEL)))], [((16, D_MODEL), F32, _full((16, D_MODEL)), None)], "silu_cond")
    mod_sh = _mm(s16, w_ada[0], "nn", 16, 1536, D_MODEL, tm=16, tn=768, tk=D_MODEL, name="w_ada_fwd")
    m_send, m_recv, m_src, m_land, zero1 = _ici_start("all", [mod_sh], [(8, 16, 1536)], jnp.zeros((8, 128), F32),
                                                      "mod_start")
    late_groups = {"g1": ("w_in_b", "w_attn_out", "w_conv_out", "w_o"), "g2": ("w_up", "w_down")}
    late_bf = {n: (shards[n] + zero1[0, 0].astype(shards[n].dtype)).astype(BF16) for g in late_groups.values() for n in g}
    xx = jnp.concatenate([x[0], ctx[0]], axis=0)
    casts_done = jnp.zeros((8, 128), F32) + sum(b[0, 0].astype(F32) for b in [xx, *late_bf.values()])
    g_send, g_recv, g_src, g_land, _ = flight0
    g_src, g_land = _ici_wait("gather", g_send, g_recv, g_src, g_land, casts_done, "gather_g0_wait")
    f_send, f_recv, f_src, f_land, zero = _ici_start("finish", g_src, None, zero1, "finish_g0_start", lands=g_land)
    gathered = _ici_wait("finish", f_send, f_recv, f_src, f_land, zero, "finish_g0_wait")[1]
    full = dict(zip(first, gathered))
    (mod_mine,), (m_land,) = _ici_wait("all", m_send, m_recv, m_src, m_land, gathered[0], "mod_wait")
    got2 = lax.dynamic_update_slice(m_land, mod_mine[None], (dev, 0, 0))
    mod_all = _cols_from_shards(got2[0::2]) + b_ada
    mod_lat = lax.dynamic_slice_in_dim(mod_all, dev, 1, axis=0)
    mod_ctx = mod_all[8:9]
    flight = {}
    for tag, group in late_groups.items():
        bf = [late_bf[n] for n in group]
        flight[tag] = _ici_start("gather", bf, [(4,) + s.shape for s in bf], zero, "gather_" + tag + "_start")
        zero = flight[tag][4]
    mod_lat = mod_lat + zero[0, 0]

    def chip_stage_done(tag, x):
        send, recv, src, land, _ = flight[tag]
        src, land = _ici_wait("gather", send, recv, src, land, x, "gather_" + tag + "_wait")
        flight[tag] = _ici_start("finish", src, None, x, "finish_" + tag + "_start", lands=land)
        return flight[tag][4]

    def arrived(tag, x):
        send, recv, src, land, _ = flight[tag]
        return dict(zip(late_groups[tag], _ici_wait("finish", send, recv, src, land, x, "finish_" + tag + "_wait")[1]))

    def late_weights(point, x):
        if point == "before_attn":
            return {}, chip_stage_done("g1", x)
        if point == "after_attn":
            got = arrived("g1", x)
            wao = _cols_from_shards(got["w_attn_out"]).reshape(N_HEADS, 64, D_MODEL)
            w_in_all = jnp.concatenate([full["w_in_a"], got["w_in_b"]], axis=1)
            ready = {"w_in_t": _w_in_t_p_from_shards(w_in_all),
                     "w_attn_out": jnp.pad(wao, ((0, 0), (64, 0), (0, 0))).reshape(N_HEADS * HEAD_PAD, D_MODEL),
                     "w_conv_out": got["w_conv_out"], "w_o": got["w_o"].reshape(D_MODEL, D_MODEL)}
            return ready, chip_stage_done("g2", x)
        got = arrived("g2", x)
        return {"w_up": got["w_up"], "w_down": got["w_down"].reshape(D_FF, D_MODEL)}, x

    wuq_t = full["w_uq"].reshape(N_HEADS, QK_DIM, Q_RANK)
    early_rows = full["w_in_a"][0]
    zrows = lambda n: jnp.zeros((n, D_MODEL), BF16)
    W = {
        "w_in_a_t": jnp.concatenate([early_rows[0:256], zrows(PA_Q0 - 256), early_rows[288:672], zrows(64),
                                     early_rows[256:288], zrows(32)], axis=0),
        "w_uq_t": jnp.pad(wuq_t, ((0, 0), (0, HEAD_PAD - QK_DIM), (0, 0))).reshape(N_HEADS * HEAD_PAD, Q_RANK),
        "w_ukv": full["w_ukv"],
        "norm1_g": norm1_g, "norm2_g": norm2_g, "final_g": final_g.reshape(1, D_MODEL), "q_norm_g": q_norm_g,
        "kv_norm_g": kv_norm_g, "conv_w": conv_w_full, "conv_b": conv_b, "ffn_conv_w": ffn_conv_w_full,
        "ffn_conv_b": ffn_conv_b,
    }

    place = jnp.stack([chip, mc]).astype(jnp.int32)
    early = {}

    pending = {}

    def scatter(tag, group, gs, from_sib, carry):
        if tag == "mid":
            sums = _add_pair_many(gs, from_sib, place, "rs_pair_add_mid")
        else:
            sums = [_add_pair(gs[w], from_sib[w], place, "rs_pair_add_" + n) for w, n in enumerate(group)]
        send, recv, sums, land, carry = _ici_start(
            "scatter", sums, [(3,) + s.shape[1:] for s in sums], carry, "rs_chips_" + tag + "_start")
        early[tag] = (group, send, recv, sums, land)
        return carry

    def early_grads(tag, g, carry, split=False):
        gs = list(g.values())
        if not split:
            return scatter(tag, list(g), gs, _rs_pair(gs, "rs_pair_" + tag), carry)
        send, recv, gs, land, carry = _ici_start(
            "pair", gs, [(4, s.shape[1] // 2, s.shape[2]) for s in gs], carry, "rs_pair_" + tag + "_start")
        pending[tag] = (list(g), send, recv, gs, land)
        return carry

    def early_continue(tag, carry):
        group, send, recv, gs, land = pending[tag]
        gs, from_sib = _ici_wait("pair", send, recv, gs, land, carry, "rs_pair_" + tag + "_wait")
        return scatter(tag, group, gs, from_sib, carry)

    grad_x, loss_part, gbig, gsmall = _local_step(xx, loss_target[0], mod_lat, mod_ctx, W, late_weights, early_grads,
                                                  early_continue)

    gsmall["loss"] = loss_part
    pay3 = jnp.concatenate([gsmall[n].reshape(-1) for n, _ in SMALL])
    pay3 = jnp.pad(pay3, (0, SMALL_ROWS * 128 - pay3.shape[0])).reshape(SMALL_ROWS, 128)
    s_send, s_recv, s_src, s_land, w_in_thru = _ici_start("all", [pay3], [(8, SMALL_ROWS, 128)], gbig["w_in"],
                                                         "small_start")
    gbig = {"w_in": w_in_thru}

    after_small = early_grads("last", gbig, s_src[0])

    (pay3,), (s_land,) = _ici_wait("all", s_send, s_recv, [after_small], s_land, early["last"][3][0], "small_wait")
    got3 = lax.dynamic_update_slice(s_land, pay3[None], (dev, 0, 0)).reshape(8 * SMALL_ROWS, 128)

    def f_sum8(ids, a):
        s = a[0:SMALL_ROWS]
        for d in range(1, 8):
            s = s + a[d * SMALL_ROWS:(d + 1) * SMALL_ROWS]
        return (s,)

    (vsum,) = _ew(f_sum8, (1,), [(got3, _full((8 * SMALL_ROWS, 128)))],
                  [((SMALL_ROWS, 128), F32, _full((SMALL_ROWS, 128)), None)], "sum_small")
    vflat = vsum.reshape(-1)
    gvec, off = {}, 0
    for n, size in SMALL:
        gvec[n] = vflat[off:off + size]
        off += size
    loss = gvec["loss"][0]
    dmod_rows = got3.reshape(8, SMALL_ROWS * 128)[:, :6 * D_MODEL]
    dm16 = jnp.concatenate([dmod_rows, gvec["dmod_ctx"].reshape(1, -1), jnp.zeros((7, 6 * D_MODEL), F32)], axis=0)

    def f_colsum(ids, a):
        return (_colsum(a),)

    (g_b_ada,) = _ew(f_colsum, (1,), [(dm16, _full((16, 6 * D_MODEL)))],
                     [((1, 6 * D_MODEL), F32, _full((1, 6 * D_MODEL)), None)], "b_ada_grad")
    dm_sh = lax.dynamic_slice_in_dim(dm16, chip * 1536, 1536, axis=1)
    g_w_ada = _mm(s16, dm_sh, "tn", D_MODEL, 1536, 16, tm=512, tn=768, tk=16, name="w_ada_dw")
    dcond_part = _mm(dm_sh, w_ada[0], "nt", 16, D_MODEL, 1536, tm=16, tn=512, tk=1536, name="w_ada_dx")
    d_send, d_recv, d_src, d_land, vsum = _ici_start("all", [dcond_part[8:16]], [(8, 8, D_MODEL)], vsum, "dcond_start")

    def finish_start(tags, after):
        done, halves = [], []
        for tag in tags:
            tag_names, send, recv, sums, land = early[tag]
            sums, land = _ici_wait("scatter", send, recv, sums, land, after, "rs_chips_" + tag + "_wait")
            done += tag_names
            if tag == "mid":
                halves += _add_chips_many(sums, land, place, "rs_chip_add_mid")
            else:
                halves += [_add_chips(a, b, place, "rs_chip_add_" + n) for a, b, n in zip(sums, land, tag_names)]
        send, recv, _, halves, _ = _ici_start("back", [], None, jnp.zeros((8, 128), F32), "rs_back_" + tags[0] + "_start",
                                              lands=halves)
        return done, send, recv, halves

    def finish_wait(state, after):
        done, send, recv, halves = state
        return dict(zip(done, _ici_wait("back", send, recv, [], halves, after, "rs_back_" + done[0] + "_wait")[1]))

    grads, deltas, new_m, new_v = {}, {}, {}, {}

    raw = {}

    def adam(n, w_, m_, v_, g, transposed):
        g_out, d_, m2, v2 = _adamw(w_, g, m_, v_, "adamw_" + n)
        raw[n] = d_
        back = (lambda a: jnp.transpose(a)[None]) if transposed else (lambda a: a[None])
        grads[n], deltas[n], new_m[n], new_v[n] = back(g_out), back(d_), back(m2), back(v2)

    pending_back = finish_start(["late", "mid"], grad_x)
    adam("w_ada", w_ada[0], m_w_ada[0], v_w_ada[0], g_w_ada, False)
    gw = finish_wait(pending_back, raw["w_ada"])
    for n, (w_, m_, v_) in {"w_o": (w_o, m_w_o, v_w_o), "w_up": (w_up, m_w_up, v_w_up),
                            "w_down": (w_down, m_w_down, v_w_down)}.items():
        adam(n, w_[0], m_[0], v_[0], gw[n], False)
    pending_back = finish_start(["last"], raw["w_up"])

    (dcond_mine,), (d_land,) = _ici_wait("all", d_send, d_recv, d_src, d_land, raw["w_down"], "dcond_wait")
    got4 = lax.dynamic_update_slice(d_land, dcond_mine[None], (dev, 0, 0))[0::2, 0]

    def f_c_ctx(ids, parts, cc):
        s = _sigmoid(cc)
        d = parts[0:1] + parts[1:2] + parts[2:3] + parts[3:4]
        return (d * s * (1.0 + cc * (1.0 - s)),)

    (g_c_ctx,) = _ew(f_c_ctx, (1,), [(got4, _full((4, D_MODEL))), (c_ctx.reshape(1, D_MODEL), _full((1, D_MODEL)))],
                     [((1, D_MODEL), F32, _full((1, D_MODEL)), None)], "c_ctx_grad")

    conv_w_g = lax.dynamic_slice_in_dim(gvec["conv_w"].reshape(3, CONV_DIM), chip * 128, 128, axis=1)
    ffn_conv_w_g = lax.dynamic_slice_in_dim(gvec["ffn_conv_w"].reshape(3, 2 * D_FF), chip * 1408, 1408, axis=1)
    vec_params = (("c_ctx", c_ctx, m_c_ctx, v_c_ctx, g_c_ctx), ("b_ada", b_ada, m_b_ada, v_b_ada, g_b_ada),
                  ("norm1_g", norm1_g, m_norm1_g, v_norm1_g, gvec["norm1_g"]),
                  ("q_norm_g", q_norm_g, m_q_norm_g, v_q_norm_g, gvec["q_norm_g"]),
                  ("kv_norm_g", kv_norm_g, m_kv_norm_g, v_kv_norm_g, gvec["kv_norm_g"]),
                  ("conv_w", conv_w, m_conv_w, v_conv_w, conv_w_g), ("conv_b", conv_b, m_conv_b, v_conv_b, gvec["conv_b"]),
                  ("norm2_g", norm2_g, m_norm2_g, v_norm2_g, gvec["norm2_g"]),
                  ("ffn_conv_w", ffn_conv_w, m_ffn_conv_w, v_ffn_conv_w, ffn_conv_w_g),
                  ("ffn_conv_b", ffn_conv_b, m_ffn_conv_b, v_ffn_conv_b, gvec["ffn_conv_b"]),
                  ("final_g", final_g, m_final_g, v_final_g, gvec["final_g"]))
    two_d = lambda a: a.reshape((-1, a.shape[-1]))
    many = [p + ((lambda r, s=p[1].shape: r.reshape(s)),) for p in vec_params]
    for n, w_, m_, v_ in (("w_ukv", w_ukv, m_w_ukv, v_w_ukv), ("w_attn_out", w_attn_out, m_w_attn_out, v_w_attn_out),
                          ("w_conv_out", w_conv_out, m_w_conv_out, v_w_conv_out)):
        many.append((n, w_, m_, v_, gw[n], (lambda r, s=w_.shape: r.reshape(s))))
    many.append(("w_uq", w_uq_t, m_w_uq_t, v_w_uq_t, gw["w_uq"], lambda r: jnp.transpose(r)[None]))

    def f_adam_many(ids, *vals):
        out = []
        for k in range(len(many)):
            out += [vals[4 * k + 1]] + _adam_update(*vals[4 * k:4 * k + 4])
        return out

    ins_v, outs_v = [], []
    for p in many:
        shp = two_d(p[1]).shape
        ins_v += [(two_d(a), _full(shp)) for a in (p[1], p[4], p[2], p[3])]
        outs_v += [(shp, F32, _full(shp), None)] * 4
    res_v = _ew(f_adam_many, (1,), ins_v, outs_v, "adamw_small")
    for k, p in enumerate(many):
        n, post = p[0], p[5]
        grads[n], deltas[n], new_m[n], new_v[n] = (post(r) for r in res_v[4 * k:4 * k + 4])

    gw_in = finish_wait(pending_back, res_v[0])
    adam("w_in", w_in_t, m_w_in_t, v_w_in_t, gw_in["w_in"], True)

    order = ("c_ctx", "w_ada", "b_ada", "norm1_g", "w_in", "q_norm_g", "kv_norm_g", "w_uq", "w_ukv", "conv_w", "conv_b",
             "w_attn_out", "w_conv_out", "w_o", "norm2_g", "w_up", "ffn_conv_w", "ffn_conv_b", "w_down", "final_g")
    return (loss, grad_x[None], *[grads[n] for n in order], *[deltas[n] for n in order],
            *[new_m[n] for n in order], *[new_v[n] for n in order])
```

```python
import functools

import jax
import jax.numpy as jnp
import numpy as np
from jax import lax
from jax.experimental import pallas as pl
from jax.experimental.pallas import tpu as pltpu

F32, BF16 = jnp.float32, jnp.bfloat16
MESH = pl.DeviceIdType.MESH

D_MODEL = 1024
N_HEADS = 8
HEAD_PAD = 128
QK_DIM = 96
Q_RANK, KV_RANK = 384, 256
CONV_DIM = 512
D_FF = 2816
GRID_W = 64
ROPE_THETA = 10000.0
EPS = 1e-6
GA0, GC0, CX0, CB0, CC0, KV0, Q0, KR0, P_COLS = 0, 1024, 2048, 2560, 3072, 3584, 3840, 4224, 4352
PA_KV0, PA_Q0, PA_KR0, PA_COLS = 0, 384, 768, 896
ROW_TILE = 256
VMEM_LIMIT_BYTES = 48 * 1024 * 1024

ADAM_LR, ADAM_B1, ADAM_B2, ADAM_EPS, ADAM_WD, ADAM_STEP = 0.001, 0.9, 0.999, 1e-08, 0.01, 10

NN = (((1,), (0,)), ((), ()))
NT = (((1,), (1,)), ((), ()))
TN = (((0,), (0,)), ((), ()))


def _cp(sem):
    return pltpu.CompilerParams(dimension_semantics=sem, vmem_limit_bytes=VMEM_LIMIT_BYTES)


PIN_BYTES = 1 << 19


def _in_hbm(arrays):
    return [pltpu.with_memory_space_constraint(a, pltpu.HBM) if a.size * a.dtype.itemsize >= PIN_BYTES else a
            for a in arrays]


def _out(shape, dtype):
    n = 1
    for d in shape:
        n *= d
    big = n * jnp.dtype(dtype).itemsize >= PIN_BYTES
    return pltpu.HBM(shape, dtype) if big else jax.ShapeDtypeStruct(shape, dtype)


def _pick(n, prefs):
    for p in prefs:
        if n % p == 0:
            return p
    return n


def _mm(a, b, mode, M, N, K, *, tm, tn, tk, name, out_dtype=F32, a_spec=None, b_spec=None, o_spec=None,
        out_shape=None, transpose_out=False):
    assert M % tm == 0 and N % tn == 0 and K % tk == 0, (name, M, N, K, tm, tn, tk)
    nk = K // tk
    dims = {"nn": NN, "nt": NT, "tn": TN}[mode]
    if a_spec is None:
        a_spec = (pl.BlockSpec((tk, tm), lambda i, j, k: (k, i)) if mode == "tn"
                  else pl.BlockSpec((tm, tk), lambda i, j, k: (i, k)))
    if b_spec is None:
        b_spec = (pl.BlockSpec((tn, tk), lambda i, j, k: (j, k)) if mode == "nt"
                  else pl.BlockSpec((tk, tn), lambda i, j, k: (k, j)))
    if o_spec is None:
        o_spec = (pl.BlockSpec((tn, tm), lambda i, j, k: (j, i)) if transpose_out
                  else pl.BlockSpec((tm, tn), lambda i, j, k: (i, j)))
    if out_shape is None:
        out_shape = (N, M) if transpose_out else (M, N)

    def emit(o_ref, val):
        o_ref[...] = (val.T if transpose_out else val).astype(o_ref.dtype)

    def body(a_ref, b_ref, o_ref, *scratch):
        part = lax.dot_general(a_ref[...].astype(BF16), b_ref[...].astype(BF16), dims, preferred_element_type=F32)
        if nk == 1:
            emit(o_ref, part)
            return
        acc_ref, = scratch
        k = pl.program_id(2)

        @pl.when(k == 0)
        def _():
            acc_ref[...] = part

        @pl.when((k > 0) & (k < nk - 1))
        def _():
            acc_ref[...] += part

        @pl.when(k == nk - 1)
        def _():
            emit(o_ref, acc_ref[...] + part)

    return pl.pallas_call(
        body, grid=(M // tm, N // tn, nk), in_specs=[a_spec, b_spec], out_specs=o_spec,
        out_shape=_out(out_shape, out_dtype),
        scratch_shapes=[pltpu.VMEM((tm, tn), F32)] if nk > 1 else [],
        compiler_params=_cp(("parallel", "parallel", "arbitrary")), name=name)(*_in_hbm([a, b]))


def _ew(fn, grid, ins, outs, name, scalars=None):
    n_in = len(ins)
    n_sc = 0 if scalars is None else 1

    def store(ref, val, acc, ids):
        if isinstance(val, (list, tuple)):
            for h, v in enumerate(val):
                ref[h] = v.astype(ref.dtype)
            return
        if acc is None:
            ref[...] = val.astype(ref.dtype)
            return

        @pl.when(ids[acc] == 0)
        def _():
            ref[...] = val.astype(ref.dtype)

        @pl.when(ids[acc] > 0)
        def _():
            ref[...] += val.astype(ref.dtype)

    def body(*refs):
        refs = refs[n_sc:]
        ids = tuple(pl.program_id(a) for a in range(len(grid)))
        vals = fn(ids, *[r[...] for r in refs[:n_in]])
        for ref, val, (_, _, _, acc) in zip(refs[n_in:], vals, outs):
            store(ref, val, acc, ids)

    acc_axes = {o[3] for o in outs if o[3] is not None}
    sem = tuple("arbitrary" if a in acc_axes else "parallel" for a in range(len(grid)))
    in_specs, out_specs = [s for _, s in ins], [o[2] for o in outs]
    out_shape = [_out(o[0], o[1]) for o in outs]
    args = _in_hbm([a for a, _ in ins])
    if scalars is None:
        return pl.pallas_call(body, grid=grid, in_specs=in_specs, out_specs=out_specs, out_shape=out_shape,
                              compiler_params=_cp(sem), name=name)(*args)
    spec = pltpu.PrefetchScalarGridSpec(num_scalar_prefetch=1, grid=grid, in_specs=in_specs, out_specs=out_specs)
    return pl.pallas_call(body, grid_spec=spec, out_shape=out_shape, compiler_params=_cp(sem), name=name)(scalars, *args)


def _rows(width, cblk=0, roff=0, tr=ROW_TILE):
    return pl.BlockSpec((tr, width), lambda i: (i + roff, cblk))


def _full(shape):
    nd = len(shape)
    return pl.BlockSpec(shape, lambda *_: (0,) * nd)


def _sigmoid(x):
    return 1.0 / (1.0 + jnp.exp2(x * (-1.4426950408889634)))


def _rms(x):
    return lax.rsqrt(jnp.mean(x * x, axis=-1, keepdims=True) + EPS)


def _rms_bwd(dn, xn, r):
    return r * (dn - xn * jnp.mean(dn * xn, axis=-1, keepdims=True))


def _colsum(x):
    return jnp.sum(x, axis=0, keepdims=True)


def _shifts(x):
    n = x.shape[0]
    rows = lax.broadcasted_iota(jnp.int32, x.shape, 0)
    return jnp.where(rows == 0, 0.0, pltpu.roll(x, 1, 0)), jnp.where(rows == n - 1, 0.0, pltpu.roll(x, n - 1, 0))


def _conv(x, w, b, shifted=None):
    prev, nxt = _shifts(x) if shifted is None else shifted
    return b + prev * w[0:1] + x * w[1:2] + nxt * w[2:3]


def _conv_bwd_x(dy, w):
    prev, nxt = _shifts(dy)
    return nxt * w[0:1] + dy * w[1:2] + prev * w[2:3]


def _conv_bwd_w(dy, x, shifted):
    prev, nxt = shifted
    return _colsum(dy * prev), _colsum(dy * x), _colsum(dy * nxt)


def _rope(x, cos, sin_lo, sin_hi):
    return x * cos + pltpu.roll(x, HEAD_PAD - 8, 1) * sin_lo + pltpu.roll(x, 8, 1) * sin_hi


ATTN_SCALE = QK_DIM ** -0.5
LOG2_E = 1.4426950408889634


def _rope_t(x, tab, inverse=False):
    o = 3 * HEAD_PAD if inverse else 0
    return _rope(x, tab[:, o:o + HEAD_PAD], tab[:, o + HEAD_PAD:o + 2 * HEAD_PAD], tab[:, o + 2 * HEAD_PAD:o + 3 * HEAD_PAD])


def _heads_keys(hp, kv_ref, kr_ref, tab_ref, kc_ref, vp_ref):
    kr_roped = _rope_t(kr_ref[...], tab_ref[...])
    lane = lax.broadcasted_iota(jnp.int32, kr_roped.shape, 1)
    for u in range(hp):
        kv = kv_ref[:, u * HEAD_PAD:(u + 1) * HEAD_PAD]
        kc_ref[u] = jnp.where(lane < 64, kv, kr_roped).astype(BF16)
        vp_ref[u] = jnp.where(lane >= 64, kv, 0.0).astype(BF16)


ATTN_Q_TILE = 512
ATTN_HEADS_PER_STEP = 2


def _attn_specs(tq, TT):
    q = pl.BlockSpec((tq, HEAD_PAD), lambda h, i: (i, h))
    keys = pl.BlockSpec((TT, HEAD_PAD), lambda h, i: (0, h))
    kr = pl.BlockSpec((TT, HEAD_PAD), lambda h, i: (0, PA_KR0 // HEAD_PAD))
    tab_q = pl.BlockSpec((tq, 6 * HEAD_PAD), lambda h, i: (i, 0))
    tab_k = pl.BlockSpec((TT, 6 * HEAD_PAD), lambda h, i: (0, 0))
    return q, keys, kr, tab_q, tab_k


def _attn_fwd(q_raw, kv, pp, tab, T, TT):
    tq, hp = ROW_TILE, 2 * ATTN_HEADS_PER_STEP
    w = hp * HEAD_PAD

    def body(q_ref, kv_ref, kr_ref, tq_ref, tk_ref, o_ref, kc, vp):
        @pl.when(pl.program_id(1) == 0)
        def _():
            _heads_keys(hp, kv_ref, kr_ref, tk_ref, kc, vp)

        tab = tq_ref[...]
        for u in range(hp):
            cols = slice(u * HEAD_PAD, (u + 1) * HEAD_PAD)
            q = _rope_t(q_ref[:, cols], tab).astype(BF16)
            s = lax.dot_general(q, kc[u], NT, preferred_element_type=F32)
            m = jnp.max(s, axis=-1, keepdims=True)
            p = jnp.exp2((s - m) * (ATTN_SCALE * LOG2_E))
            l = jnp.sum(p, axis=-1, keepdims=True)
            o = lax.dot_general(p.astype(BF16), vp[u], NN, preferred_element_type=F32)
            lane = lax.broadcasted_iota(jnp.int32, o.shape, 1)
            o_ref[:, cols] = jnp.where(lane < 64, m * ATTN_SCALE + jnp.log(l), o / l)

    _, _, kr, _, _ = _attn_specs(tq, TT)
    qs = pl.BlockSpec((tq, w), lambda h, i: (i, h))
    keys = pl.BlockSpec((TT, w), lambda h, i: (0, h))
    tab_q = pl.BlockSpec((tq, 3 * HEAD_PAD), lambda h, i: (i, 0))
    tab_k = pl.BlockSpec((TT, 3 * HEAD_PAD), lambda h, i: (0, 0))
    return pl.pallas_call(
        body, grid=(N_HEADS // hp, T // tq), in_specs=[qs, keys, kr, tab_q, tab_k], out_specs=qs,
        out_shape=jax.ShapeDtypeStruct((T, N_HEADS * HEAD_PAD), F32),
        scratch_shapes=[pltpu.VMEM((hp, TT, HEAD_PAD), BF16), pltpu.VMEM((hp, TT, HEAD_PAD), BF16)],
        compiler_params=_cp(("parallel", "arbitrary")), name="attn_fwd",
    )(*_in_hbm([q_raw, kv, pp, tab, tab]))


def _attn_bwd(q_raw, kv, pp, o, do, tab, T, TT):
    tq = _pick(T, (ATTN_Q_TILE, ROW_TILE))
    nq = T // tq
    hp = ATTN_HEADS_PER_STEP
    w = hp * HEAD_PAD

    def body(q_ref, kv_ref, kr_ref, tq_ref, tk_ref, o_ref, do_ref, dq_ref, dkv_ref, dkr_ref, kc, vp, dk, dv):
        g, i = pl.program_id(0), pl.program_id(1)

        @pl.when(i == 0)
        def _():
            _heads_keys(hp, kv_ref, kr_ref, tk_ref, kc, vp)
            dk[...] = jnp.zeros_like(dk)
            dv[...] = jnp.zeros_like(dv)

        tab = tq_ref[...]
        for u in range(hp):
            cols = slice(u * HEAD_PAD, (u + 1) * HEAD_PAD)
            q = _rope_t(q_ref[:, cols], tab).astype(BF16)
            k, v, d_o = kc[u], vp[u], do_ref[:, cols]
            s = lax.dot_general(q, k, NT, preferred_element_type=F32)
            o = o_ref[:, cols]
            p = jnp.exp2(s * (ATTN_SCALE * LOG2_E) - o[:, 0:1] * LOG2_E)
            dob = d_o.astype(BF16)
            dp = lax.dot_general(dob, v, NT, preferred_element_type=F32)
            dd = jnp.sum(d_o * o, axis=-1, keepdims=True)
            ds = (p * (dp - dd) * ATTN_SCALE).astype(BF16)
            dq = lax.dot_general(ds, k, NN, preferred_element_type=F32)
            dq_ref[:, cols] = _rope_t(dq, tab, inverse=True).astype(dq_ref.dtype)
            dk[u] += lax.dot_general(q, ds, TN, preferred_element_type=F32)
            dv[u] += lax.dot_general(dob, p.astype(BF16), TN, preferred_element_type=F32)

        @pl.when(i == nq - 1)
        def _():
            rot = None
            for u in range(hp):
                dkh = dk[u].T
                lane = lax.broadcasted_iota(jnp.int32, dkh.shape, 1)
                dkv_ref[:, u * HEAD_PAD:(u + 1) * HEAD_PAD] = jnp.where(lane < 64, dkh, dv[u].T).astype(dkv_ref.dtype)
                part = jnp.where((lane >= 64) & (lane < 96), dkh, 0.0)
                rot = part if rot is None else rot + part
            rot = _rope_t(rot, tk_ref[...], inverse=True)

            @pl.when(g == 0)
            def _():
                dkr_ref[...] = rot

            @pl.when(g > 0)
            def _():
                dkr_ref[...] += rot

    _, _, kr, tab_q, tab_k = _attn_specs(tq, TT)
    qs = pl.BlockSpec((tq, w), lambda h, i: (i, h))
    keys = pl.BlockSpec((TT, w), lambda h, i: (0, h))
    wide = lambda rows: jax.ShapeDtypeStruct((rows, N_HEADS * HEAD_PAD), BF16)
    return pl.pallas_call(
        body, grid=(N_HEADS // hp, nq),
        in_specs=[qs, keys, kr, tab_q, tab_k, qs, qs],
        out_specs=[qs, keys, pl.BlockSpec((TT, HEAD_PAD), lambda h, i: (0, 0))],
        out_shape=[wide(T), wide(TT), jax.ShapeDtypeStruct((TT, HEAD_PAD), F32)],
        scratch_shapes=[pltpu.VMEM((hp, TT, HEAD_PAD), BF16), pltpu.VMEM((hp, TT, HEAD_PAD), BF16),
                        pltpu.VMEM((hp, HEAD_PAD, TT), F32), pltpu.VMEM((hp, HEAD_PAD, TT), F32)],
        compiler_params=_cp(("arbitrary", "arbitrary")), name="attn_bwd",
    )(*_in_hbm([q_raw, kv, pp, tab, tab, o, do]))


def _hbm_specs(n):
    return [pl.BlockSpec(memory_space=pl.ANY)] * n


def _gather_weights(shards):
    n = len(shards)
    halves = [s.shape[0] // 2 for s in shards]

    def body(*refs):
        ins, outs = refs[:n], refs[n:2 * n]
        token, send_sems, recv_sems = refs[2 * n:]
        token[...] = jnp.zeros_like(token)
        mx, my, mc = lax.axis_index("x"), lax.axis_index("y"), lax.axis_index("c")
        j_me = 2 * mx + my
        chips = [(1 - mx, my), (mx, 1 - my), (1 - mx, 1 - my)]

        def half(w, chip_idx, hc):
            return outs[w].at[chip_idx, pl.ds(hc * halves[w], halves[w]), :]

        def copy(w, k, src, dst, to):
            return pltpu.make_async_remote_copy(src_ref=src, dst_ref=dst, send_sem=send_sems.at[w, k],
                                                recv_sem=recv_sems.at[w, k], device_id=to, device_id_type=MESH)

        sends = []
        for w in range(n):
            cp = copy(w, 6, ins[w], outs[w].at[j_me], (mx, my, 1 - mc))
            cp.start()
            sends.append(cp)
        for k, (px, py) in enumerate(chips):
            for w in range(n):
                cp = copy(w, k, ins[w].at[pl.ds(mc * halves[w], halves[w]), :], half(w, j_me, mc), (px, py, mc))
                cp.start()
                sends.append(cp)
        for k, (px, py) in enumerate(chips):
            for w in range(n):
                got = half(w, 2 * px + py, mc)
                copy(w, k, got, got, (px, py, mc)).wait_recv()
                cp = copy(w, 3 + k, got, got, (mx, my, 1 - mc))
                cp.start()
                sends.append(cp)
        for k, (px, py) in enumerate(chips):
            for w in range(n):
                got = half(w, 2 * px + py, 1 - mc)
                copy(w, 3 + k, got, got, (mx, my, 1 - mc)).wait_recv()
        for w in range(n):
            own = outs[w].at[j_me]
            copy(w, 6, own, own, (mx, my, 1 - mc)).wait_recv()
        for cp in sends:
            cp.wait_send()

    res = pl.pallas_call(
        body, out_shape=[jax.ShapeDtypeStruct((4,) + s.shape, s.dtype) for s in shards]
        + [jax.ShapeDtypeStruct((8, 128), F32)],
        in_specs=_hbm_specs(n), out_specs=_hbm_specs(n) + [pl.BlockSpec(memory_space=pltpu.VMEM)],
        scratch_shapes=[pltpu.SemaphoreType.DMA((n, 7)), pltpu.SemaphoreType.DMA((n, 7))],
        name="gather_weights")(*shards)
    return list(res[:n]), res[n]


def _rs_pair(gs, name):
    n = len(gs)
    halves = [g.shape[1] // 2 for g in gs]

    def body(*refs):
        ins, lands = refs[:n], refs[n:2 * n]
        send_sems, recv_sems = refs[2 * n:]
        mx, my, mc = lax.axis_index("x"), lax.axis_index("y"), lax.axis_index("c")
        copies = []
        for w in range(n):
            h = halves[w]
            cp = pltpu.make_async_remote_copy(
                src_ref=ins[w].at[:, pl.ds((1 - mc) * h, h), :], dst_ref=lands[w], send_sem=send_sems.at[w],
                recv_sem=recv_sems.at[w], device_id=(mx, my, 1 - mc), device_id_type=MESH)
            cp.start()
            copies.append(cp)
        for cp in copies:
            cp.wait()

    return pl.pallas_call(
        body, out_shape=[jax.ShapeDtypeStruct((4, h, g.shape[2]), g.dtype) for g, h in zip(gs, halves)],
        in_specs=_hbm_specs(n), out_specs=_hbm_specs(n),
        scratch_shapes=[pltpu.SemaphoreType.DMA((n,)), pltpu.SemaphoreType.DMA((n,))], name=name)(*gs)


def _rs_chips(parts):
    n = len(parts)

    def body(*refs):
        ins, lands = refs[:n], refs[n:2 * n]
        send_sems, recv_sems = refs[2 * n:]
        mx, my, mc = lax.axis_index("x"), lax.axis_index("y"), lax.axis_index("c")
        copies = []
        for k, (px, py) in enumerate([(1 - mx, my), (mx, 1 - my), (1 - mx, 1 - my)]):
            for w in range(n):
                cp = pltpu.make_async_remote_copy(
                    src_ref=ins[w].at[2 * px + py], dst_ref=lands[w].at[k], send_sem=send_sems.at[w, k],
                    recv_sem=recv_sems.at[w, k], device_id=(px, py, mc), device_id_type=MESH)
                cp.start()
                copies.append(cp)
        for cp in copies:
            cp.wait()

    return list(pl.pallas_call(
        body, out_shape=[jax.ShapeDtypeStruct((3,) + p.shape[1:], p.dtype) for p in parts],
        in_specs=_hbm_specs(n), out_specs=_hbm_specs(n),
        scratch_shapes=[pltpu.SemaphoreType.DMA((n, 3)), pltpu.SemaphoreType.DMA((n, 3))], name="rs_chips")(*parts))


_HBM = pl.BlockSpec(memory_space=pltpu.HBM)
_SEM = pl.BlockSpec(memory_space=pltpu.SEMAPHORE)
_EFFECT = pltpu.SideEffectType.DATAFLOW_SIDE_EFFECTING


def _ici_copies(kind, srcs, lands, send_sems, recv_sems):
    n = len(lands)
    mx, my, mc = lax.axis_index("x"), lax.axis_index("y"), lax.axis_index("c")
    j_me = 2 * mx + my
    copies = []
    if kind == "back":
        for w in range(n):
            h = lands[w].shape[0] // 2
            mine = lands[w].at[pl.ds(mc * h, h), :]
            copies.append(pltpu.make_async_remote_copy(
                src_ref=mine, dst_ref=mine, send_sem=send_sems.at[w], recv_sem=recv_sems.at[w],
                device_id=(mx, my, 1 - mc), device_id_type=MESH))
        return copies
    if kind == "all":
        for k in range(7):
            a, b, c = (k + 1) >> 2 & 1, (k + 1) >> 1 & 1, (k + 1) & 1
            peer = (1 - mx if a else mx, 1 - my if b else my, 1 - mc if c else mc)
            for w in range(n):
                copies.append(pltpu.make_async_remote_copy(
                    src_ref=srcs[w], dst_ref=lands[w].at[4 * mx + 2 * my + mc], send_sem=send_sems.at[7 * w + k],
                    recv_sem=recv_sems.at[7 * w + k], device_id=peer, device_id_type=MESH))
        return copies
    if kind == "pair":
        for w in range(n):
            h = srcs[w].shape[1] // 2
            copies.append(pltpu.make_async_remote_copy(
                src_ref=srcs[w].at[:, pl.ds((1 - mc) * h, h), :], dst_ref=lands[w], send_sem=send_sems.at[w],
                recv_sem=recv_sems.at[w], device_id=(mx, my, 1 - mc), device_id_type=MESH))
        return copies
    chips = [(1 - mx, my), (mx, 1 - my), (1 - mx, 1 - my)]
    if kind == "finish":
        for w in range(n):
            h = srcs[w].shape[0] // 2
            pushes = [(lands[w].at[2 * px + py, pl.ds(mc * h, h), :],) * 2 for px, py in chips]
            pushes.append((srcs[w], lands[w].at[j_me]))
            for k, (src, dst) in enumerate(pushes):
                copies.append(pltpu.make_async_remote_copy(
                    src_ref=src, dst_ref=dst, send_sem=send_sems.at[4 * w + k], recv_sem=recv_sems.at[4 * w + k],
                    device_id=(mx, my, 1 - mc), device_id_type=MESH))
        return copies
    for k, (px, py) in enumerate(chips):
        for w in range(n):
            if kind == "gather":
                h = srcs[w].shape[0] // 2
                src, dst = srcs[w].at[pl.ds(mc * h, h), :], lands[w].at[j_me, pl.ds(mc * h, h), :]
            else:
                src, dst = srcs[w].at[2 * px + py], lands[w].at[k]
            copies.append(pltpu.make_async_remote_copy(
                src_ref=src, dst_ref=dst, send_sem=send_sems.at[3 * w + k], recv_sem=recv_sems.at[3 * w + k],
                device_id=(px, py, mc), device_id_type=MESH))
    return copies


_SEMS_PER_OPERAND = {"gather": 3, "scatter": 3, "all": 7, "pair": 1, "finish": 4, "back": 1}


def _ici_start(kind, srcs, land_shapes, carry, name, lands=None):
    hbm = lambda a: pltpu.with_memory_space_constraint(a, pltpu.HBM)
    if lands is None:
        lands = [lax.empty(s, srcs[0].dtype) for s in land_shapes]
    ns, nl = len(srcs), len(lands)

    def body(*refs):
        send_sems, recv_sems = refs[ns + nl + 1], refs[ns + nl + 2]
        for cp in _ici_copies(kind, refs[:ns], refs[ns:ns + nl], send_sems, recv_sems):
            cp.start()

    args = [hbm(a) for a in list(srcs) + list(lands) + [carry]]
    n_sem = _SEMS_PER_OPERAND[kind] * nl
    out_shape = ([pltpu.SemaphoreType.DMA((n_sem,)), pltpu.SemaphoreType.DMA((n_sem,))]
                 + [pltpu.HBM(a.shape, a.dtype) for a in args])
    res = pl.pallas_call(
        body, name=name, out_shape=out_shape, in_specs=[_HBM] * len(args), out_specs=[_SEM, _SEM] + [_HBM] * len(args),
        input_output_aliases={i: 2 + i for i in range(len(args))},
        compiler_params=pltpu.CompilerParams(has_side_effects=_EFFECT))(*args)
    return res[0], res[1], list(res[2:2 + ns]), list(res[2 + ns:2 + ns + nl]), res[2 + ns + nl]


def _ici_wait(kind, send_sems, recv_sems, srcs, lands, after, name):
    ns, nl = len(srcs), len(lands)

    def body(*refs):
        for cp in _ici_copies(kind, refs[:ns], refs[ns:ns + nl], refs[ns + nl], refs[ns + nl + 1]):
            cp.wait_send()
            cp.wait_recv()

    args = list(srcs) + list(lands)
    res = pl.pallas_call(
        body, name=name, out_shape=[pltpu.HBM(a.shape, a.dtype) for a in args],
        in_specs=[_HBM] * len(args) + [_SEM, _SEM, pl.BlockSpec(memory_space=pl.ANY)], out_specs=[_HBM] * len(args),
        input_output_aliases={i: i for i in range(len(args))},
        compiler_params=pltpu.CompilerParams(has_side_effects=_EFFECT))(*args, send_sems, recv_sems, after)
    return list(res[:ns]), list(res[ns:])


def _tile_rows(h, c, itemsize, mult):
    best = h
    for t in range(mult, h + 1, mult):
        if h % t == 0 and t * c * itemsize <= (1 << 21):
            best = t
    return best


def _add_pair(g, land, place, name):
    _, h, c = land.shape
    t = _tile_rows(h, c, 2, 16)
    nb = h // t
    return _ew(lambda ids, u, v: (u.astype(F32) + v.astype(F32),), (4, nb),
               [(g, pl.BlockSpec((None, t, c), lambda j, i, s: (j, s[1] * nb + i, 0))),
                (land, pl.BlockSpec((None, t, c), lambda j, i, s: (j, i, 0)))],
               [(land.shape, BF16, pl.BlockSpec((None, t, c), lambda j, i, s: (j, i, 0)), None)], name, scalars=place)[0]


def _add_pair_many(gs, lands, place, name):
    ins, outs = [], []
    for g, l in zip(gs, lands):
        ins += [(g, pl.BlockSpec(l.shape, lambda i, s: (0, s[1], 0))), (l, pl.BlockSpec(l.shape, lambda i, s: (0, 0, 0)))]
        outs.append((l.shape, BF16, pl.BlockSpec(l.shape, lambda i, s: (0, 0, 0)), None))
    fn = lambda ids, *v: [v[2 * k].astype(F32) + v[2 * k + 1].astype(F32) for k in range(len(gs))]
    return list(_ew(fn, (1,), ins, outs, name, scalars=place))


def _add_chips_many(owns, lands, place, name):
    ins, outs = [], []
    for own, land in zip(owns, lands):
        _, h, c = land.shape
        ins += [(own, pl.BlockSpec((None, h, c), lambda i, s: (s[0], 0, 0))),
                (land, pl.BlockSpec((3, h, c), lambda i, s: (0, 0, 0)))]
        outs.append(((2 * h, c), F32, pl.BlockSpec((h, c), lambda i, s: (s[1], 0)), None))

    def fn(ids, *v):
        return [((v[2 * k].astype(F32) + v[2 * k + 1][0].astype(F32)) + v[2 * k + 1][1].astype(F32))
                + v[2 * k + 1][2].astype(F32) for k in range(len(owns))]

    return list(_ew(fn, (1,), ins, outs, name, scalars=place))


def _add_chips(own, land, place, name):
    _, h, c = land.shape
    t = _tile_rows(h, c, 4, 16)
    nb = h // t

    def fn(ids, a, b):
        return (((a.astype(F32) + b[0].astype(F32)) + b[1].astype(F32)) + b[2].astype(F32),)

    return _ew(fn, (nb,), [(own, pl.BlockSpec((None, t, c), lambda i, s: (s[0], i, 0))),
                           (land, pl.BlockSpec((3, t, c), lambda i, s: (0, i, 0)))],
               [((2 * h, c), F32, pl.BlockSpec((t, c), lambda i, s: (s[1] * nb + i, 0)), None)], name, scalars=place)[0]


W_IN_SEGMENTS = ((0, 256, KV0), (256, 288, KR0 + 64), (288, 672, Q0), (672, 1184, CX0), (1184, 1696, CB0),
                 (1696, 2208, CC0), (2208, 3232, GA0), (3232, 4256, GC0))
W_IN_SHARD = 1064


W_IN_SHARD_PAD = 1088
W_IN_EARLY = 672


def _w_in_t_p_from_shards(s):
    pieces = []
    for o0, o1, p0 in sorted(W_IN_SEGMENTS, key=lambda t: t[2]):
        if p0 == KR0 + 64:
            pieces.append(jnp.zeros((64, s.shape[2]), s.dtype))
        for j in range(4):
            lo, hi = max(o0, j * W_IN_SHARD), min(o1, (j + 1) * W_IN_SHARD)
            if lo < hi:
                pieces.append(s[j, lo - j * W_IN_SHARD:hi - j * W_IN_SHARD])
    pieces.append(jnp.zeros((32, s.shape[2]), s.dtype))
    return jnp.concatenate(pieces, axis=0)


def _w_in_t_shards_from_p(g):
    shards = []
    for j in range(4):
        pieces = []
        for o0, o1, p0 in W_IN_SEGMENTS:
            lo, hi = max(o0, j * W_IN_SHARD), min(o1, (j + 1) * W_IN_SHARD)
            if lo < hi:
                pieces.append(g[p0 + lo - o0:p0 + hi - o0])
        pieces.append(jnp.zeros((W_IN_SHARD_PAD - W_IN_SHARD, g.shape[1]), g.dtype))
        shards.append(jnp.concatenate(pieces, axis=0))
    return jnp.stack(shards, axis=0)


def _cols_from_shards(s):
    return jnp.transpose(s, (1, 0, 2)).reshape(s.shape[1], -1)


def _rope_tables(T, TT, inverse):
    f32 = np.float32
    rows = T // GRID_W
    row = np.repeat(np.arange(rows), GRID_W).astype(f32)
    col = np.tile(np.arange(GRID_W), rows).astype(f32)
    inv = (f32(ROPE_THETA) ** (-np.arange(0, 16, 2, dtype=f32) / f32(16))).astype(f32)
    ang = np.concatenate([row[:, None] * inv, col[:, None] * inv], axis=-1).astype(f32)
    cos, sin = np.cos(ang).astype(f32), np.sin(ang).astype(f32)
    lane = np.arange(32)
    src = (lane // 16) * 8 + lane % 8
    lo = ((lane % 16) // 8 == 0).astype(f32)
    sgn = f32(-1.0 if inverse else 1.0)
    cos32 = cos[:, src]
    sin_lo32 = -sgn * sin[:, src] * lo
    sin_hi32 = sgn * sin[:, src] * (1 - lo)

    def widen(t32, fill):
        t = np.concatenate([np.full((T, 64), fill, f32), t32, np.full((T, 32), fill, f32)], axis=1)
        return np.concatenate([t, np.full((TT - T, HEAD_PAD), fill, f32)], axis=0)

    return [widen(cos32, 1.0), widen(sin_lo32, 0.0), widen(sin_hi32, 0.0)]


def _rope_table(T, TT):
    return jnp.asarray(np.concatenate(_rope_tables(T, TT, False) + _rope_tables(T, TT, True), axis=1))


def _local_step(xx, tgt, mod_lat, mod_ctx, W, late_weights, early_grads, early_continue):
    TT = xx.shape[0]
    T = tgt.shape[0]
    n_lat, n_all = T // ROW_TILE, TT // ROW_TILE
    sh1, sc1, g1, sh2, sc2, g2 = [mod_lat[:, k * D_MODEL:(k + 1) * D_MODEL] for k in range(6)]
    csh1, csc1 = mod_ctx[:, :D_MODEL], mod_ctx[:, D_MODEL:2 * D_MODEL]
    vec = lambda n: _full((1, n))
    row_out = lambda n, dt, rows=T: ((rows, n), dt, _rows(n), None)
    acc_out = lambda n: ((1, n), F32, _full((1, n)), 0)
    lt = _pick(T, (2 * ROW_TILE, ROW_TILE))
    n_lt = T // lt
    lrows = lambda n, cblk=0: _rows(n, cblk, 0, lt)
    lrow_out = lambda n, dt: ((T, n), dt, lrows(n), None)

    def f_norm1(ids, x, g, a_sh, a_sc, b_sh, b_sc):
        ctx = ids[0] >= n_lat
        sh, sc = jnp.where(ctx, b_sh, a_sh), jnp.where(ctx, b_sc, a_sc)
        return ((x * _rms(x) * g) * (1.0 + sc) + sh,)

    (hh,) = _ew(f_norm1, (n_all,), [(xx, _rows(D_MODEL)), (W["norm1_g"], vec(D_MODEL)), (sh1, vec(D_MODEL)),
                                   (sc1, vec(D_MODEL)), (csh1, vec(D_MODEL)), (csc1, vec(D_MODEL))],
                [row_out(D_MODEL, BF16, TT)], "norm1_fwd")
    tm_all = _pick(TT, (768, 256))
    pp_a = _mm(hh, W["w_in_a_t"], "nt", TT, PA_COLS, D_MODEL, tm=tm_all, tn=PA_COLS, tk=D_MODEL, name="w_in_fwd_a")

    def f_lowrank(ids, ckv, cq, gkv, gq):
        return ckv * _rms(ckv) * gkv, cq * _rms(cq) * gq

    nkv, nq = _ew(f_lowrank, (n_all,), [(pp_a, _rows(KV_RANK, PA_KV0 // KV_RANK)), (pp_a, _rows(Q_RANK, PA_Q0 // Q_RANK)),
                                       (W["kv_norm_g"], vec(KV_RANK)), (W["q_norm_g"], vec(Q_RANK))],
                  [row_out(KV_RANK, BF16, TT), row_out(Q_RANK, BF16, TT)], "lowrank_norm_fwd")
    kv = _mm(nkv, W["w_ukv"], "nn", TT, 1024, KV_RANK, tm=tm_all, tn=256, tk=KV_RANK, name="w_ukv_fwd",
             b_spec=pl.BlockSpec((None, KV_RANK, 256), lambda i, j, k: (j, k, 0)))
    q_raw = _mm(nq, W["w_uq_t"], "nt", TT, 1024, Q_RANK, tm=tm_all, tn=1024, tk=Q_RANK, name="w_uq_fwd")

    tab = _rope_table(T, TT)
    _, q_raw = late_weights("before_attn", q_raw)
    o_pad = _attn_fwd(q_raw, kv, pp_a, tab, T, TT)
    arrived, o_pad = late_weights("after_attn", o_pad)
    W = dict(W, **arrived)
    tm_lat = _pick(T, (1024, 512, 256))
    pp = _mm(hh, W["w_in_t"], "nt", T, KV0, D_MODEL, tm=tm_lat, tn=KV0 // 2, tk=D_MODEL, name="w_in_fwd_b")
    ya = _mm(o_pad, W["w_attn_out"], "nn", T, D_MODEL, 1024, tm=tm_lat, tn=D_MODEL, tk=1024, name="w_attn_out_fwd",
             out_dtype=BF16)

    tc = 256
    colT = lambda blk0: pl.BlockSpec((T, tc), lambda j: (0, blk0 + j))

    def f_conv(ids, xin, cb, cc, w, b):
        return (cb * _conv(cc * xin, w, b),)

    (e,) = _ew(f_conv, (CONV_DIM // tc,),
               [(pp, colT(CX0 // tc)), (pp, colT(CB0 // tc)), (pp, colT(CC0 // tc)),
                (W["conv_w"], pl.BlockSpec((3, tc), lambda j: (0, j))), (W["conv_b"], pl.BlockSpec((1, tc), lambda j: (0, j)))],
               [((T, CONV_DIM), BF16, colT(0), None)], "conv_fwd")
    yc = _mm(e, W["w_conv_out"], "nn", T, D_MODEL, CONV_DIM, tm=tm_lat, tn=256, tk=CONV_DIM, name="w_conv_out_fwd",
             out_dtype=BF16, b_spec=pl.BlockSpec((None, CONV_DIM, 256), lambda i, j, k: (j, k, 0)))

    def f_merge(ids, ga, gc, a, c):
        return (_sigmoid(ga) * a.astype(F32) + _sigmoid(gc) * c.astype(F32),)

    (mrg,) = _ew(f_merge, (n_lt,), [(pp, lrows(D_MODEL, 0)), (pp, lrows(D_MODEL, 1)), (ya, lrows(D_MODEL)),
                                   (yc, lrows(D_MODEL))], [lrow_out(D_MODEL, BF16)], "merge_fwd")
    mo = _mm(mrg, W["w_o"], "nn", T, D_MODEL, D_MODEL, tm=tm_lat, tn=D_MODEL, tk=D_MODEL, name="w_o_fwd")

    def f_norm2(ids, x, m, gate, g, sh, sc):
        x1 = x + gate * m
        return x1, (x1 * _rms(x1) * g) * (1.0 + sc) + sh

    x1, h2 = _ew(f_norm2, (n_lt,), [(xx, lrows(D_MODEL)), (mo, lrows(D_MODEL)), (g1, vec(D_MODEL)),
                                   (W["norm2_g"], vec(D_MODEL)), (sh2, vec(D_MODEL)), (sc2, vec(D_MODEL))],
                 [lrow_out(D_MODEL, F32), lrow_out(D_MODEL, BF16)], "norm2_fwd")
    arrived, h2 = late_weights("before_ffn", h2)
    W = dict(W, **arrived)
    up = _mm(h2, W["w_up"], "nn", T, 2 * D_FF, D_MODEL, tm=tm_lat, tn=1408, tk=D_MODEL, name="w_up_fwd",
             b_spec=pl.BlockSpec((None, D_MODEL, 1408), lambda i, j, k: (j, k, 0)))

    n_ff = D_FF // tc
    ffw = lambda off, n=3: pl.BlockSpec((n, tc), lambda j: (0, j + off))

    def f_ffn(ids, ug, uv, wg, wv, bg, bv):
        gate, val = _conv(ug, wg, bg), _conv(uv, wv, bv)
        return (gate * _sigmoid(gate) * val,)

    (act,) = _ew(f_ffn, (n_ff,), [(up, colT(0)), (up, colT(n_ff)), (W["ffn_conv_w"], ffw(0)), (W["ffn_conv_w"], ffw(n_ff)),
                                 (W["ffn_conv_b"], ffw(0, 1)), (W["ffn_conv_b"], ffw(n_ff, 1))],
                 [((T, D_FF), BF16, colT(0), None)], "ffn_act_fwd")
    f = _mm(act, W["w_down"], "nn", T, D_MODEL, D_FF, tm=tm_lat, tn=D_MODEL, tk=D_FF, name="w_down_fwd")

    def f_head(ids, x1_, f_, gate, gf, t):
        x2 = x1_ + gate * f_
        r = _rms(x2)
        xn = x2 * r
        err = xn * gf - t
        loss = 0.5 * jnp.sum(jnp.mean(err * err, axis=-1, keepdims=True))
        dy = err * (1.0 / D_MODEL)
        dx2 = _rms_bwd(dy * gf, xn, r)
        return dx2, dx2 * gate, _colsum(dy * xn), _colsum(dx2 * f_), jnp.full((1, 128), loss, F32)

    dx2, df, dg_f, dg2, loss = _ew(
        f_head, (n_lt,), [(x1, lrows(D_MODEL)), (f, lrows(D_MODEL)), (g2, vec(D_MODEL)), (W["final_g"], vec(D_MODEL)),
                          (tgt, lrows(D_MODEL))],
        [lrow_out(D_MODEL, F32), lrow_out(D_MODEL, BF16), acc_out(D_MODEL), acc_out(D_MODEL), acc_out(128)], "loss_head")

    d_w_down = _mm(act, df, "tn", D_FF, D_MODEL, T, tm=1408, tn=D_MODEL, tk=T, name="w_down_dw",
                   out_dtype=BF16).reshape(4, D_FF // 4, D_MODEL)
    da = _mm(df, W["w_down"], "nt", T, D_FF, D_MODEL, tm=tm_lat, tn=1408, tk=D_MODEL, name="w_down_dx")

    tcb = 128
    n_fb = D_FF // tcb
    colb = lambda blk0: pl.BlockSpec((T, tcb), lambda j: (0, blk0 + j))
    ffwb = lambda off, n=3: pl.BlockSpec((n, tcb), lambda j: (0, j + off))
    cvec = ((1, D_FF), F32, pl.BlockSpec((1, tcb), lambda j: (0, j)), None)

    def f_ffn_bwd(ids, ug, uv, d_act, wg, wv, bg, bv):
        sg, sv = _shifts(ug), _shifts(uv)
        gate, val = _conv(ug, wg, bg, sg), _conv(uv, wv, bv, sv)
        s = _sigmoid(gate)
        d_gate = d_act * val * s * (1.0 + gate * (1.0 - s))
        d_val = d_act * gate * s
        wg0, wg1, wg2 = _conv_bwd_w(d_gate, ug, sg)
        wv0, wv1, wv2 = _conv_bwd_w(d_val, uv, sv)
        d_up = [_conv_bwd_x(d_gate, wg), _conv_bwd_x(d_val, wv)]
        return d_up, [_colsum(d_gate), _colsum(d_val), wg0, wg1, wg2, wv0, wv1, wv2]

    d_up3, ffn_stats = _ew(
        f_ffn_bwd, (n_fb,),
        [(up, colb(0)), (up, colb(n_fb)), (da, colb(0)), (W["ffn_conv_w"], ffwb(0)), (W["ffn_conv_w"], ffwb(n_fb)),
         (W["ffn_conv_b"], ffwb(0, 1)), (W["ffn_conv_b"], ffwb(n_fb, 1))],
        [((2, T, D_FF), BF16, pl.BlockSpec((2, T, tcb), lambda j: (0, 0, j)), None),
         ((n_fb, 8, 1, tcb), F32, pl.BlockSpec((None, 8, 1, tcb), lambda j: (j, 0, 0, 0)), None)], "ffn_act_bwd")
    stat = lambda s: ffn_stats[:, s, 0, :].reshape(1, D_FF)
    d_ffn_conv_b = jnp.concatenate([stat(0), stat(1)], axis=1)
    d_ffn_conv_w = jnp.concatenate([jnp.concatenate([stat(2), stat(3), stat(4)], axis=0),
                                    jnp.concatenate([stat(5), stat(6), stat(7)], axis=0)], axis=1)

    tk_t = T
    d_w_up = _mm(h2, d_up3, "tn", D_MODEL, 2 * D_FF, T, tm=D_MODEL, tn=1408, tk=tk_t, name="w_up_dw", out_dtype=BF16,
                 b_spec=pl.BlockSpec((None, tk_t, 1408), lambda i, j, k: (j // 2, k, j % 2)),
                 o_spec=pl.BlockSpec((None, D_MODEL, 1408), lambda i, j, k: (j, i, 0)), out_shape=(4, D_MODEL, 1408))
    dh2 = _mm(d_up3, W["w_up"], "nt", T, D_MODEL, 2 * D_FF, tm=tm_lat, tn=D_MODEL, tk=1408, name="w_up_dx",
              a_spec=pl.BlockSpec((None, tm_lat, 1408), lambda i, j, k: (k // 2, i, k % 2)),
              b_spec=pl.BlockSpec((None, D_MODEL, 1408), lambda i, j, k: (k, j, 0)))

    def f_norm2_bwd(ids, dx2_, dh, x1_, m, g, sc, gate):
        r = _rms(x1_)
        xn = x1_ * r
        dx1 = dx2_ + _rms_bwd(dh * g * (1.0 + sc), xn, r)
        return dx1, dx1 * gate, _colsum(dh), _colsum(dh * xn * g), _colsum(dh * xn * (1.0 + sc)), _colsum(dx1 * m)

    dx1, dmo, dsh2, dsc2, dg_n2, dg1 = _ew(
        f_norm2_bwd, (n_lt,), [(dx2, lrows(D_MODEL)), (dh2, lrows(D_MODEL)), (x1, lrows(D_MODEL)), (mo, lrows(D_MODEL)),
                               (W["norm2_g"], vec(D_MODEL)), (sc2, vec(D_MODEL)), (g1, vec(D_MODEL))],
        [lrow_out(D_MODEL, F32), lrow_out(D_MODEL, BF16)] + [acc_out(D_MODEL)] * 4, "norm2_bwd")
    d_w_o = _mm(mrg, dmo, "tn", D_MODEL, D_MODEL, T, tm=D_MODEL, tn=D_MODEL, tk=tk_t, name="w_o_dw",
                out_dtype=BF16).reshape(4, D_MODEL // 4, D_MODEL)
    dmrg = _mm(dmo, W["w_o"], "nt", T, D_MODEL, D_MODEL, tm=tm_lat, tn=D_MODEL, tk=D_MODEL, name="w_o_dx",
               out_dtype=BF16)
    dmrg = early_grads("late", {"w_o": d_w_o, "w_up": d_w_up, "w_down": d_w_down}, dmrg, split=True)

    def f_merge_bwd(ids, dm, ga, gc, a, c):
        dm, a, c = dm.astype(F32), a.astype(F32), c.astype(F32)
        sa, sc_ = _sigmoid(ga), _sigmoid(gc)
        return dm * sa, dm * sc_, dm * a * sa * (1.0 - sa), dm * c * sc_ * (1.0 - sc_)

    dya, dyc, dp_ga, dp_gc = _ew(
        f_merge_bwd, (n_lt,), [(dmrg, lrows(D_MODEL)), (pp, lrows(D_MODEL, 0)), (pp, lrows(D_MODEL, 1)),
                               (ya, lrows(D_MODEL)), (yc, lrows(D_MODEL))], [lrow_out(D_MODEL, BF16)] * 4, "merge_bwd")
    dya = early_continue("late", dya)

    d_w_ao_p = _mm(o_pad, dya, "tn", 1024, D_MODEL, T, tm=1024, tn=D_MODEL, tk=tk_t, name="w_attn_out_dw", out_dtype=BF16)
    do_pad = _mm(dya, W["w_attn_out"], "nt", T, 1024, D_MODEL, tm=tm_lat, tn=1024, tk=D_MODEL, name="w_attn_out_dx")
    d_w_co = _mm(e, dyc, "tn", CONV_DIM, D_MODEL, T, tm=CONV_DIM, tn=256, tk=tk_t, name="w_conv_out_dw", out_dtype=BF16,
                 o_spec=pl.BlockSpec((None, CONV_DIM, 256), lambda i, j, k: (j, i, 0)), out_shape=(4, CONV_DIM, 256))
    de = _mm(dyc, W["w_conv_out"], "nt", T, CONV_DIM, D_MODEL, tm=tm_lat, tn=CONV_DIM, tk=256, name="w_conv_out_dx",
             b_spec=pl.BlockSpec((None, CONV_DIM, 256), lambda i, j, k: (k, j, 0)))

    def f_conv_bwd(ids, xin, cb, cc, d_e, w, b):
        z = cc * xin
        sz = _shifts(z)
        cz = _conv(z, w, b, sz)
        dcz = d_e * cb
        w0, w1, w2 = _conv_bwd_w(dcz, z, sz)
        dz = _conv_bwd_x(dcz, w)
        return dz * cc, d_e * cz, dz * xin, _colsum(dcz), w0, w1, w2

    cvec_c = ((1, CONV_DIM), F32, pl.BlockSpec((1, tc), lambda j: (0, j)), None)
    conv_b = _ew(f_conv_bwd, (CONV_DIM // tc,),
                 [(pp, colT(CX0 // tc)), (pp, colT(CB0 // tc)), (pp, colT(CC0 // tc)), (de, colT(0)),
                  (W["conv_w"], pl.BlockSpec((3, tc), lambda j: (0, j))), (W["conv_b"], pl.BlockSpec((1, tc), lambda j: (0, j)))],
                 [((T, CONV_DIM), BF16, colT(0), None)] * 3 + [cvec_c] * 4, "conv_bwd")
    dp_cx, dp_cb, dp_cc, d_conv_b = conv_b[:4]
    d_conv_w = jnp.concatenate(conv_b[4:7], axis=0)

    dq_raw, dkv, dp_kr = _attn_bwd(q_raw, kv, pp_a, o_pad, do_pad, tab, T, TT)

    tk_a = TT
    d_w_uq_t = _mm(nq, dq_raw, "tn", Q_RANK, 1024, T, tm=Q_RANK, tn=1024, tk=T, name="w_uq_dw", transpose_out=True)
    dnq = _mm(dq_raw, W["w_uq_t"], "nn", T, Q_RANK, 1024, tm=tm_lat, tn=Q_RANK, tk=1024, name="w_uq_dx")
    d_w_ukv = _mm(nkv, dkv, "tn", KV_RANK, 1024, TT, tm=KV_RANK, tn=256, tk=tk_a, name="w_ukv_dw", out_dtype=BF16,
                  o_spec=pl.BlockSpec((None, KV_RANK, 256), lambda i, j, k: (j, i, 0)), out_shape=(4, KV_RANK, 256))
    dnkv = _mm(dkv, W["w_ukv"], "nt", TT, KV_RANK, 1024, tm=tm_all, tn=KV_RANK, tk=256, name="w_ukv_dx",
               b_spec=pl.BlockSpec((None, KV_RANK, 256), lambda i, j, k: (k, j, 0)))
    dnkv = early_grads("mid", {
        "w_attn_out": jnp.transpose(d_w_ao_p.reshape(N_HEADS, HEAD_PAD, 4, 256)[:, 64:], (2, 0, 1, 3)).reshape(
            4, N_HEADS * 64, 256),
        "w_conv_out": d_w_co,
        "w_uq": d_w_uq_t.reshape(4, 2, HEAD_PAD, Q_RANK)[:, :, :QK_DIM].reshape(4, 2 * QK_DIM, Q_RANK).astype(BF16),
        "w_ukv": d_w_ukv}, dnkv)

    def f_lowrank_bwd(ids, ckv, cq, dkv_, dq_, gkv, gq, ga, gc, cx, cb, cc, kr):
        rk, rq = _rms(ckv), _rms(cq)
        nk, nq_ = ckv * rk, cq * rq
        lat = ids[0] < n_lat
        dq_ = jnp.where(lat, dq_, 0.0)
        pieces = [jnp.where(lat, a, jnp.zeros_like(a)) for a in (ga, gc, cx, cb, cc)]
        pieces += [_rms_bwd(dkv_ * gkv, nk, rk).astype(BF16), _rms_bwd(dq_ * gq, nq_, rq).astype(BF16), kr.astype(BF16)]
        return jnp.concatenate(pieces, axis=1), _colsum(dkv_ * nk), _colsum(dq_ * nq_)

    lat_rows = lambda n: pl.BlockSpec((ROW_TILE, n), lambda i: (jnp.minimum(i, n_lat - 1), 0))
    dpp, dg_kv, dg_q = _ew(
        f_lowrank_bwd, (n_all,), [(pp_a, _rows(KV_RANK, PA_KV0 // KV_RANK)), (pp_a, _rows(Q_RANK, PA_Q0 // Q_RANK)),
                                  (dnkv, _rows(KV_RANK)), (dnq, lat_rows(Q_RANK)), (W["kv_norm_g"], vec(KV_RANK)),
                                  (W["q_norm_g"], vec(Q_RANK)), (dp_ga, lat_rows(D_MODEL)), (dp_gc, lat_rows(D_MODEL)),
                                  (dp_cx, lat_rows(CONV_DIM)), (dp_cb, lat_rows(CONV_DIM)), (dp_cc, lat_rows(CONV_DIM)),
                                  (dp_kr, _rows(HEAD_PAD))],
        [row_out(P_COLS, BF16, TT), acc_out(KV_RANK), acc_out(Q_RANK)], "lowrank_norm_bwd")
    d_w_in_t = _mm(hh, dpp, "tn", D_MODEL, P_COLS, TT, tm=512, tn=2176, tk=TT, name="w_in_dw", out_dtype=BF16,
                   transpose_out=True)
    dhh = _mm(dpp, W["w_in_t"], "nn", TT, D_MODEL, P_COLS, tm=tm_all, tn=512, tk=2176, name="w_in_dx")

    def f_norm1_bwd(ids, x, dh, dres, g, sc):
        r = _rms(x)
        xn = x * r
        return (dres + _rms_bwd(dh * g * (1.0 + sc), xn, r), _colsum(dh), _colsum(dh * xn * g),
                _colsum(dh * xn * (1.0 + sc)))

    grad_x, dsh1, dsc1, dg_n1 = _ew(
        f_norm1_bwd, (n_lt,), [(xx, lrows(D_MODEL)), (dhh, lrows(D_MODEL)), (dx1, lrows(D_MODEL)),
                               (W["norm1_g"], vec(D_MODEL)), (sc1, vec(D_MODEL))],
        [lrow_out(D_MODEL, F32)] + [acc_out(D_MODEL)] * 3, "norm1_bwd")

    def f_norm1_ctx_bwd(ids, x, dh, g, sc):
        xn = x * _rms(x)
        return _colsum(dh), _colsum(dh * xn * g), _colsum(dh * xn * (1.0 + sc))

    n_ctx = n_all - n_lat
    dcsh1, dcsc1, dg_n1c = _ew(
        f_norm1_ctx_bwd, (n_ctx,), [(xx, _rows(D_MODEL, 0, n_lat)), (dhh, _rows(D_MODEL, 0, n_lat)),
                                    (W["norm1_g"], vec(D_MODEL)), (csc1, vec(D_MODEL))], [acc_out(D_MODEL)] * 3,
        "norm1_ctx_bwd")

    big = {"w_in": _w_in_t_shards_from_p(d_w_in_t).astype(BF16)}
    zero = jnp.zeros((1, 4 * D_MODEL), F32)
    small = {
        "dmod_lat": jnp.concatenate([dsh1, dsc1, dg1, dsh2, dsc2, dg2], axis=1),
        "dmod_ctx": jnp.concatenate([dcsh1, dcsc1, zero], axis=1),
        "norm1_g": dg_n1 + dg_n1c, "norm2_g": dg_n2, "final_g": dg_f, "q_norm_g": dg_q, "kv_norm_g": dg_kv,
        "conv_b": d_conv_b, "conv_w": d_conv_w.reshape(1, -1), "ffn_conv_b": d_ffn_conv_b,
        "ffn_conv_w": d_ffn_conv_w.reshape(1, -1),
    }
    return grad_x, loss, big, small


SMALL = (("dmod_lat", 6144), ("dmod_ctx", 6144), ("norm1_g", 1024), ("norm2_g", 1024), ("final_g", 1024),
         ("q_norm_g", 384), ("kv_norm_g", 256), ("conv_b", 512), ("conv_w", 1536), ("ffn_conv_b", 5632),
         ("ffn_conv_w", 16896), ("loss", 128))
SMALL_ROWS = 320


def _adam_update(w, g, m, v):
    c1, c2 = 1.0 - ADAM_B1 ** ADAM_STEP, 1.0 - ADAM_B2 ** ADAM_STEP
    m2 = ADAM_B1 * m + (1.0 - ADAM_B1) * g
    v2 = ADAM_B2 * v + (1.0 - ADAM_B2) * (g * g)
    return [-ADAM_LR * ((m2 / c1) / (jnp.sqrt(v2 / c2) + ADAM_EPS) + ADAM_WD * w), m2, v2]


def _adamw(w, g, m, v, name):
    R, C = w.shape
    tr = 8 if R % 8 == 0 else R
    for t in range(8, R + 1, 8):
        if R % t == 0 and t * C * 4 <= (1 << 21):
            tr = t
    spec = pl.BlockSpec((tr, C), lambda i: (i, 0))
    return _ew(lambda ids, *vals: [vals[1]] + _adam_update(*vals), (R // tr,),
               [(w, spec), (g, spec), (m, spec), (v, spec)], [((R, C), F32, spec, None)] * 4, name)


def kernel(x, c, ctx, c_ctx, w_ada, b_ada, norm1_g, w_in, q_norm_g, kv_norm_g, w_uq, w_ukv, conv_w, conv_b, w_attn_out, w_conv_out, w_o, norm2_g, w_up, ffn_conv_w, ffn_conv_b, w_down, final_g, loss_target, m_c_ctx, m_w_ada, m_b_ada, m_norm1_g, m_w_in, m_q_norm_g, m_kv_norm_g, m_w_uq, m_w_ukv, m_conv_w, m_conv_b, m_w_attn_out, m_w_conv_out, m_w_o, m_norm2_g, m_w_up, m_ffn_conv_w, m_ffn_conv_b, m_w_down, m_final_g, v_c_ctx, v_w_ada, v_b_ada, v_norm1_g, v_w_in, v_q_norm_g, v_kv_norm_g, v_w_uq, v_w_ukv, v_conv_w, v_conv_b, v_w_attn_out, v_w_conv_out, v_w_o, v_norm2_g, v_w_up, v_ffn_conv_w, v_ffn_conv_b, v_w_down, v_final_g):
    mx, my, mc = lax.axis_index("x"), lax.axis_index("y"), lax.axis_index("c")
    chip = 2 * mx + my
    dev = 4 * mx + 2 * my + mc
    T, Tc = x.shape[1], ctx.shape[1]
    TT = T + Tc
    w_in_t, m_w_in_t, v_w_in_t = (jnp.transpose(a[0]) for a in (w_in, m_w_in, v_w_in))
    w_uq_t, m_w_uq_t, v_w_uq_t = (jnp.transpose(a[0]) for a in (w_uq, m_w_uq, v_w_uq))
    conv_sh = jnp.concatenate([conv_w[0], ffn_conv_w[0]], axis=1)
    pay1 = jnp.concatenate([jnp.pad(c, ((0, 7), (0, 0))), jnp.pad(conv_sh, ((0, 5), (0, 0)))], axis=1)
    c_send, c_recv, c_src, c_land, zero0 = _ici_start("all", [pay1], [(8, 8, 2560)], jnp.zeros((8, 128), F32),
                                                      "cond_start")
    w_in_late = jnp.pad(w_in_t[W_IN_EARLY:], ((0, W_IN_SHARD_PAD - W_IN_SHARD), (0, 0)))
    shards = {"w_in_a": w_in_t[:W_IN_EARLY], "w_in_b": w_in_late, "w_uq": w_uq_t, "w_ukv": w_ukv[0],
              "w_attn_out": w_attn_out[0], "w_conv_out": w_conv_out[0], "w_o": w_o[0], "w_up": w_up[0],
              "w_down": w_down[0]}
    first = ["w_in_a", "w_uq", "w_ukv"]
    bf0 = [(shards[n] + zero0[0, 0].astype(shards[n].dtype)).astype(BF16) for n in first]
    flight0 = _ici_start("gather", bf0, [(4,) + s.shape for s in bf0], jnp.zeros((8, 128), F32), "gather_g0_start")
    (pay1,), (c_land,) = _ici_wait("all", c_send, c_recv, c_src, c_land, flight0[4], "cond_wait")
    got1 = lax.dynamic_update_slice(c_land, pay1[None], (dev, 0, 0))
    c_all = got1[:, 0, :D_MODEL]
    conv_all = got1[0::2, :3, D_MODEL:]
    conv_w_full = _cols_from_shards(conv_all[:, :, :128])
    ffn_conv_w_full = _cols_from_shards(conv_all[:, :, 128:])

    cond = jnp.concatenate([c_all, c_ctx.reshape(1, D_MODEL), jnp.zeros((7, D_MODEL), F32)], axis=0)

    def f_silu(ids, v):
        return (v * _sigmoid(v),)

    (s16,) = _ew(f_silu, (1,), [(cond, _full((16, D_MODEL)))], [((16, D_MODEL), F32, _full((16, D_MODEL)), None)], "silu_cond")
    mod_sh = _mm(s16, w_ada[0], "nn", 16, 1536, D_MODEL, tm=16, tn=768, tk=D_MODEL, name="w_ada_fwd")
    m_send, m_recv, m_src, m_land, zero1 = _ici_start("all", [mod_sh], [(8, 16, 1536)], jnp.zeros((8, 128), F32),
                                                      "mod_start")
    late_groups = {"g1": ("w_in_b", "w_attn_out", "w_conv_out", "w_o"), "g2": ("w_up", "w_down")}
    late_bf = {n: (shards[n] + zero1[0, 0].astype(shards[n].dtype)).astype(BF16) for g in late_groups.values() for n in g}
    casts_done = jnp.zeros((8, 128), F32) + sum(b[0, 0].astype(F32) for b in late_bf.values())
    g_send, g_recv, g_src, g_land, _ = flight0
    g_src, g_land = _ici_wait("gather", g_send, g_recv, g_src, g_land, casts_done, "gather_g0_wait")
    f_send, f_recv, f_src, f_land, zero = _ici_start("finish", g_src, None, zero1, "finish_g0_start", lands=g_land)
    gathered = _ici_wait("finish", f_send, f_recv, f_src, f_land, zero, "finish_g0_wait")[1]
    full = dict(zip(first, gathered))
    (mod_mine,), (m_land,) = _ici_wait("all", m_send, m_recv, m_src, m_land, gathered[0], "mod_wait")
    got2 = lax.dynamic_update_slice(m_land, mod_mine[None], (dev, 0, 0))
    mod_all = _cols_from_shards(got2[0::2]) + b_ada
    mod_lat = lax.dynamic_slice_in_dim(mod_all, dev, 1, axis=0)
    mod_ctx = mod_all[8:9]
    xx = jnp.concatenate([x[0], ctx[0]], axis=0) + zero[0, 0]
    flight = {}
    for tag, group in late_groups.items():
        bf = [late_bf[n] for n in group]
        flight[tag] = _ici_start("gather", bf, [(4,) + s.shape for s in bf], xx, "gather_" + tag + "_start")
        xx = flight[tag][4]

    def chip_stage_done(tag, x):
        send, recv, src, land, _ = flight[tag]
        src, land = _ici_wait("gather", send, recv, src, land, x, "gather_" + tag + "_wait")
        flight[tag] = _ici_start("finish", src, None, x, "finish_" + tag + "_start", lands=land)
        return flight[tag][4]

    def arrived(tag, x):
        send, recv, src, land, _ = flight[tag]
        return dict(zip(late_groups[tag], _ici_wait("finish", send, recv, src, land, x, "finish_" + tag + "_wait")[1]))

    def late_weights(point, x):
        if point == "before_attn":
            return {}, chip_stage_done("g1", x)
        if point == "after_attn":
            got = arrived("g1", x)
            wao = _cols_from_shards(got["w_attn_out"]).reshape(N_HEADS, 64, D_MODEL)
            w_in_all = jnp.concatenate([full["w_in_a"], got["w_in_b"]], axis=1)
            ready = {"w_in_t": _w_in_t_p_from_shards(w_in_all),
                     "w_attn_out": jnp.pad(wao, ((0, 0), (64, 0), (0, 0))).reshape(N_HEADS * HEAD_PAD, D_MODEL),
                     "w_conv_out": got["w_conv_out"], "w_o": got["w_o"].reshape(D_MODEL, D_MODEL)}
            return ready, chip_stage_done("g2", x)
        got = arrived("g2", x)
        return {"w_up": got["w_up"], "w_down": got["w_down"].reshape(D_FF, D_MODEL)}, x

    wuq_t = full["w_uq"].reshape(N_HEADS, QK_DIM, Q_RANK)
    early_rows = full["w_in_a"][0]
    zrows = lambda n: jnp.zeros((n, D_MODEL), BF16)
    W = {
        "w_in_a_t": jnp.concatenate([early_rows[0:256], zrows(PA_Q0 - 256), early_rows[288:672], zrows(64),
                                     early_rows[256:288], zrows(32)], axis=0),
        "w_uq_t": jnp.pad(wuq_t, ((0, 0), (0, HEAD_PAD - QK_DIM), (0, 0))).reshape(N_HEADS * HEAD_PAD, Q_RANK),
        "w_ukv": full["w_ukv"],
        "norm1_g": norm1_g, "norm2_g": norm2_g, "final_g": final_g.reshape(1, D_MODEL), "q_norm_g": q_norm_g,
        "kv_norm_g": kv_norm_g, "conv_w": conv_w_full, "conv_b": conv_b, "ffn_conv_w": ffn_conv_w_full,
        "ffn_conv_b": ffn_conv_b,
    }

    place = jnp.stack([chip, mc]).astype(jnp.int32)
    early = {}

    pending = {}

    def scatter(tag, group, gs, from_sib, carry):
        if tag == "mid":
            sums = _add_pair_many(gs, from_sib, place, "rs_pair_add_mid")
        else:
            sums = [_add_pair(gs[w], from_sib[w], place, "rs_pair_add_" + n) for w, n in enumerate(group)]
        send, recv, sums, land, carry = _ici_start(
            "scatter", sums, [(3,) + s.shape[1:] for s in sums], carry, "rs_chips_" + tag + "_start")
        early[tag] = (group, send, recv, sums, land)
        return carry

    def early_grads(tag, g, carry, split=False):
        gs = list(g.values())
        if not split:
            return scatter(tag, list(g), gs, _rs_pair(gs, "rs_pair_" + tag), carry)
        send, recv, gs, land, carry = _ici_start(
            "pair", gs, [(4, s.shape[1] // 2, s.shape[2]) for s in gs], carry, "rs_pair_" + tag + "_start")
        pending[tag] = (list(g), send, recv, gs, land)
        return carry

    def early_continue(tag, carry):
        group, send, recv, gs, land = pending[tag]
        gs, from_sib = _ici_wait("pair", send, recv, gs, land, carry, "rs_pair_" + tag + "_wait")
        return scatter(tag, group, gs, from_sib, carry)

    grad_x, loss_part, gbig, gsmall = _local_step(xx, loss_target[0], mod_lat, mod_ctx, W, late_weights, early_grads,
                                                  early_continue)

    gsmall["loss"] = loss_part
    pay3 = jnp.concatenate([gsmall[n].reshape(-1) for n, _ in SMALL])
    pay3 = jnp.pad(pay3, (0, SMALL_ROWS * 128 - pay3.shape[0])).reshape(SMALL_ROWS, 128)
    s_send, s_recv, s_src, s_land, w_in_thru = _ici_start("all", [pay3], [(8, SMALL_ROWS, 128)], gbig["w_in"],
                                                         "small_start")
    gbig = {"w_in": w_in_thru}

    after_small = early_grads("last", gbig, s_src[0])

    (pay3,), (s_land,) = _ici_wait("all", s_send, s_recv, [after_small], s_land, early["last"][3][0], "small_wait")
    got3 = lax.dynamic_update_slice(s_land, pay3[None], (dev, 0, 0)).reshape(8 * SMALL_ROWS, 128)

    def f_sum8(ids, a):
        s = a[0:SMALL_ROWS]
        for d in range(1, 8):
            s = s + a[d * SMALL_ROWS:(d + 1) * SMALL_ROWS]
        return (s,)

    (vsum,) = _ew(f_sum8, (1,), [(got3, _full((8 * SMALL_ROWS, 128)))],
                  [((SMALL_ROWS, 128), F32, _full((SMALL_ROWS, 128)), None)], "sum_small")
    vflat = vsum.reshape(-1)
    gvec, off = {}, 0
    for n, size in SMALL:
        gvec[n] = vflat[off:off + size]
        off += size
    loss = gvec["loss"][0]
    dmod_rows = got3.reshape(8, SMALL_ROWS * 128)[:, :6 * D_MODEL]
    dm16 = jnp.concatenate([dmod_rows, gvec["dmod_ctx"].reshape(1, -1), jnp.zeros((7, 6 * D_MODEL), F32)], axis=0)

    def f_colsum(ids, a):
        return (_colsum(a),)

    (g_b_ada,) = _ew(f_colsum, (1,), [(dm16, _full((16, 6 * D_MODEL)))],
                     [((1, 6 * D_MODEL), F32, _full((1, 6 * D_MODEL)), None)], "b_ada_grad")
    dm_sh = lax.dynamic_slice_in_dim(dm16, chip * 1536, 1536, axis=1)
    g_w_ada = _mm(s16, dm_sh, "tn", D_MODEL, 1536, 16, tm=512, tn=768, tk=16, name="w_ada_dw")
    dcond_part = _mm(dm_sh, w_ada[0], "nt", 16, D_MODEL, 1536, tm=16, tn=512, tk=1536, name="w_ada_dx")
    d_send, d_recv, d_src, d_land, vsum = _ici_start("all", [dcond_part[8:16]], [(8, 8, D_MODEL)], vsum, "dcond_start")

    def finish_start(tags, after):
        done, halves = [], []
        for tag in tags:
            tag_names, send, recv, sums, land = early[tag]
            sums, land = _ici_wait("scatter", send, recv, sums, land, after, "rs_chips_" + tag + "_wait")
            done += tag_names
            if tag == "mid":
                halves += _add_chips_many(sums, land, place, "rs_chip_add_mid")
            else:
                halves += [_add_chips(a, b, place, "rs_chip_add_" + n) for a, b, n in zip(sums, land, tag_names)]
        send, recv, _, halves, _ = _ici_start("back", [], None, jnp.zeros((8, 128), F32), "rs_back_" + tags[0] + "_start",
                                              lands=halves)
        return done, send, recv, halves

    def finish_wait(state, after):
        done, send, recv, halves = state
        return dict(zip(done, _ici_wait("back", send, recv, [], halves, after, "rs_back_" + done[0] + "_wait")[1]))

    grads, deltas, new_m, new_v = {}, {}, {}, {}

    raw = {}

    def adam(n, w_, m_, v_, g, transposed):
        g_out, d_, m2, v2 = _adamw(w_, g, m_, v_, "adamw_" + n)
        raw[n] = d_
        back = (lambda a: jnp.transpose(a)[None]) if transposed else (lambda a: a[None])
        grads[n], deltas[n], new_m[n], new_v[n] = back(g_out), back(d_), back(m2), back(v2)

    pending_back = finish_start(["late", "mid"], grad_x)
    adam("w_ada", w_ada[0], m_w_ada[0], v_w_ada[0], g_w_ada, False)
    gw = finish_wait(pending_back, raw["w_ada"])
    for n, (w_, m_, v_) in {"w_o": (w_o, m_w_o, v_w_o), "w_up": (w_up, m_w_up, v_w_up),
                            "w_down": (w_down, m_w_down, v_w_down)}.items():
        adam(n, w_[0], m_[0], v_[0], gw[n], False)
    pending_back = finish_start(["last"], raw["w_up"])

    (dcond_mine,), (d_land,) = _ici_wait("all", d_send, d_recv, d_src, d_land, raw["w_down"], "dcond_wait")
    got4 = lax.dynamic_update_slice(d_land, dcond_mine[None], (dev, 0, 0))[0::2, 0]

    def f_c_ctx(ids, parts, cc):
        s = _sigmoid(cc)
        d = parts[0:1] + parts[1:2] + parts[2:3] + parts[3:4]
        return (d * s * (1.0 + cc * (1.0 - s)),)

    (g_c_ctx,) = _ew(f_c_ctx, (1,), [(got4, _full((4, D_MODEL))), (c_ctx.reshape(1, D_MODEL), _full((1, D_MODEL)))],
                     [((1, D_MODEL), F32, _full((1, D_MODEL)), None)], "c_ctx_grad")

    conv_w_g = lax.dynamic_slice_in_dim(gvec["conv_w"].reshape(3, CONV_DIM), chip * 128, 128, axis=1)
    ffn_conv_w_g = lax.dynamic_slice_in_dim(gvec["ffn_conv_w"].reshape(3, 2 * D_FF), chip * 1408, 1408, axis=1)
    vec_params = (("c_ctx", c_ctx, m_c_ctx, v_c_ctx, g_c_ctx), ("b_ada", b_ada, m_b_ada, v_b_ada, g_b_ada),
                  ("norm1_g", norm1_g, m_norm1_g, v_norm1_g, gvec["norm1_g"]),
                  ("q_norm_g", q_norm_g, m_q_norm_g, v_q_norm_g, gvec["q_norm_g"]),
                  ("kv_norm_g", kv_norm_g, m_kv_norm_g, v_kv_norm_g, gvec["kv_norm_g"]),
                  ("conv_w", conv_w, m_conv_w, v_conv_w, conv_w_g), ("conv_b", conv_b, m_conv_b, v_conv_b, gvec["conv_b"]),
                  ("norm2_g", norm2_g, m_norm2_g, v_norm2_g, gvec["norm2_g"]),
                  ("ffn_conv_w", ffn_conv_w, m_ffn_conv_w, v_ffn_conv_w, ffn_conv_w_g),
                  ("ffn_conv_b", ffn_conv_b, m_ffn_conv_b, v_ffn_conv_b, gvec["ffn_conv_b"]),
                  ("final_g", final_g, m_final_g, v_final_g, gvec["final_g"]))
    two_d = lambda a: a.reshape((-1, a.shape[-1]))
    many = [p + ((lambda r, s=p[1].shape: r.reshape(s)),) for p in vec_params]
    for n, w_, m_, v_ in (("w_ukv", w_ukv, m_w_ukv, v_w_ukv), ("w_attn_out", w_attn_out, m_w_attn_out, v_w_attn_out),
                          ("w_conv_out", w_conv_out, m_w_conv_out, v_w_conv_out)):
        many.append((n, w_, m_, v_, gw[n], (lambda r, s=w_.shape: r.reshape(s))))
    many.append(("w_uq", w_uq_t, m_w_uq_t, v_w_uq_t, gw["w_uq"], lambda r: jnp.transpose(r)[None]))

    def f_adam_many(ids, *vals):
        out = []
        for k in range(len(many)):
            out += [vals[4 * k + 1]] + _adam_update(*vals[4 * k:4 * k + 4])
        return out

    ins_v, outs_v = [], []
    for p in many:
        shp = two_d(p[1]).shape
        ins_v += [(two_d(a), _full(shp)) for a in (p[1], p[4], p[2], p[3])]
        outs_v += [(shp, F32, _full(shp), None)] * 4
    res_v = _ew(f_adam_many, (1,), ins_v, outs_v, "adamw_small")
    for k, p in enumerate(many):
        n, post = p[0], p[5]
        grads[n], deltas[n], new_m[n], new_v[n] = (post(r) for r in res_v[4 * k:4 * k + 4])

    gw_in = finish_wait(pending_back, res_v[0])
    adam("w_in", w_in_t, m_w_in_t, v_w_in_t, gw_in["w_in"], True)

    order = ("c_ctx", "w_ada", "b_ada", "norm1_g", "w_in", "q_norm_g", "kv_norm_g", "w_uq", "w_ukv", "conv_w", "conv_b",
             "w_attn_out", "w_conv_out", "w_o", "norm2_g", "w_up", "ffn_conv_w", "ffn_conv_b", "w_down", "final_g")
    return (loss, grad_x[None], *[grads[n] for n in order], *[deltas[n] for n in order],
            *[new_m[n] for n in order], *[new_v[n] for n in order])
```
